```python
import jax, jax.numpy as jnp
from jax import lax
import numpy as np

D_MODEL = 1024
BATCH = 8
SEQ = 16384
DEPTH = 2

CHUNK = 64
N_MIXERS = 4
HEAD_DIM = 64
D_MIX = D_MODEL
D_BRANCH = D_MIX // N_MIXERS
N_HEADS = D_BRANCH // HEAD_DIM
LOOKBACK_CHUNKS = 8
BAND = (LOOKBACK_CHUNKS + 1) * CHUNK
MAX_REL = 128
SG_CHUNK = 128
Q_BLOCK = 128
EPS = 1e-6
IN_SIZES = ([D_BRANCH] * 4
            + [D_BRANCH] * 3
            + [D_BRANCH] * 4
            + [N_HEADS]
            + [D_BRANCH] * 4)
N_IN = sum(IN_SIZES)

kernel_name = "hybrid_chunk_stream_heads"


def rms_norm(x, g):
    xf = x.astype(jnp.float32)
    y = xf * lax.rsqrt(jnp.mean(xf * xf, axis=-1, keepdims=True) + EPS)
    return (y * g.astype(jnp.float32)).astype(x.dtype)


def layer_norm(x, g):
    xf = x.astype(jnp.float32)
    mu = jnp.mean(xf, axis=-1, keepdims=True)
    xc = xf - mu
    y = xc * lax.rsqrt(jnp.mean(xc * xc, axis=-1, keepdims=True) + EPS)
    return (y * g.astype(jnp.float32)).astype(x.dtype)


def heads(t):
    b, s, _ = t.shape
    return t.reshape(b, s, N_HEADS, HEAD_DIM)


def chunk_band(t):
    b, s, h, dh = t.shape
    nc = s // CHUNK
    tp = jnp.pad(t, ((0, 0), (LOOKBACK_CHUNKS * CHUNK, 0), (0, 0), (0, 0)))
    tc = tp.reshape(b, nc + LOOKBACK_CHUNKS, CHUNK, h, dh)
    return jnp.concatenate([tc[:, m:m + nc] for m in range(LOOKBACK_CHUNKS + 1)], axis=2)


def chunk_relbias_attention(q, k, v, rel_bias):
    b, s, h, dh = q.shape
    nc = s // CHUNK
    qc = q.reshape(b, nc, CHUNK, h, dh)
    kb = chunk_band(k)
    vb = chunk_band(v)
    i = np.arange(CHUNK)[:, None]
    j = np.arange(BAND)[None, :]
    rel = np.clip(i - j + LOOKBACK_CHUNKS * CHUNK, -MAX_REL, MAX_REL) + MAX_REL
    bias = rel_bias[:, rel].astype(jnp.float32)
    key_chunk = jnp.arange(nc)[:, None] - LOOKBACK_CHUNKS + jnp.arange(BAND)[None, :] // CHUNK
    valid = key_chunk >= 0
    sc = jnp.einsum('bnihd,bnjhd->bnhij', qc, kb).astype(jnp.float32) * (dh ** -0.5)
    sc = sc + bias[None, None]
    sc = jnp.where(valid[None, :, None, None, :], sc, -jnp.inf)
    p = jax.nn.softmax(sc, axis=-1).astype(v.dtype)
    out = jnp.einsum('bnhij,bnjhd->bnihd', p, vb)
    return out.reshape(b, s, h * dh)


def spatial_gating(u, v, v_gain, w_s, b_s):
    b, s, _ = u.shape
    c = D_BRANCH // N_HEADS
    vn = layer_norm(v, v_gain).reshape(b, s // SG_CHUNK, SG_CHUNK, N_HEADS, c)
    w = w_s * jnp.tril(jnp.ones((SG_CHUNK, SG_CHUNK), w_s.dtype))[None]
    mixed = jnp.einsum('gts,bnsgc->bntgc', w, vn) + jnp.transpose(b_s)[None, None, :, :, None]
    return u * mixed.reshape(b, s, D_BRANCH)


def query_blocks(t):
    b, s = t.shape[:2]
    return jnp.moveaxis(t.reshape((b, s // Q_BLOCK, Q_BLOCK) + t.shape[2:]), 1, 0)


def forgetting_attention(q, k, v, f_logit):
    b, s, h, dh = q.shape
    nb = s // Q_BLOCK
    c = jnp.cumsum(jax.nn.log_sigmoid(f_logit.astype(jnp.float32)), axis=1)
    c_k = jnp.transpose(c, (0, 2, 1))[:, :, None, :]
    k_pos = jnp.arange(s)
    scale = dh ** -0.5

    def block(args):
        q_i, c_i, s0 = args
        sc = jnp.einsum('bqhd,bkhd->bhqk', q_i, k).astype(jnp.float32) * scale
        sc = sc + jnp.transpose(c_i, (0, 2, 1))[..., None] - c_k
        q_pos = s0 + jnp.arange(Q_BLOCK)
        mask = k_pos[None, :] <= q_pos[:, None]
        sc = jnp.where(mask, sc, -jnp.inf)
        p = jax.nn.softmax(sc, axis=-1).astype(v.dtype)
        return jnp.einsum('bhqk,bkhd->bqhd', p, v)

    out = lax.map(block, (query_blocks(q), query_blocks(c), jnp.arange(nb) * Q_BLOCK))
    return jnp.moveaxis(out, 0, 1).reshape(b, s, h * dh)


def stick_breaking_attention(q, k, v):
    b, s, h, dh = q.shape
    nb = s // Q_BLOCK
    k_pos = jnp.arange(s)
    scale = dh ** -0.5

    def block(args):
        q_i, s0 = args
        z = jnp.einsum('bqhd,bkhd->bhqk', q_i, k).astype(jnp.float32) * scale
        q_pos = s0 + jnp.arange(Q_BLOCK)
        mask = k_pos[None, :] < q_pos[:, None]
        log_1m = jnp.where(mask, jax.nn.log_sigmoid(-z), 0.0)
        between = lax.cumsum(log_1m, axis=3, reverse=True) - log_1m
        a = jnp.where(mask, jnp.exp(jax.nn.log_sigmoid(z) + between), 0.0)
        return jnp.einsum('bhqk,bkhd->bqhd', a.astype(v.dtype), v)

    out = lax.map(block, (query_blocks(q), jnp.arange(nb) * Q_BLOCK))
    return jnp.moveaxis(out, 0, 1).reshape(b, s, h * dh)


def hybrid_layer(x, norm_g, w_in, b_f, rel_bias, w_s, b_s, v_gain, branch_gain, w_out):
    h = rms_norm(x, norm_g)
    p = jnp.einsum('bsd,dn->bsn', h, w_in)
    (qa, ka, va, ga,
     ub, vbr, gb,
     qc, kc, vc, gc, fc,
     qd, kd, vd, gd) = jnp.split(p, [int(o) for o in np.cumsum(IN_SIZES)[:-1]], axis=-1)
    y_a = chunk_relbias_attention(heads(qa), heads(ka), heads(va), rel_bias)
    y_b = spatial_gating(ub, vbr, v_gain, w_s, b_s)
    y_c = forgetting_attention(heads(qc), heads(kc), heads(vc), fc + b_f)
    y_d = stick_breaking_attention(heads(qd), heads(kd), heads(vd))
    merged = jnp.concatenate([
        rms_norm(y_a, branch_gain[0]) * jax.nn.silu(ga),
        rms_norm(y_b, branch_gain[1]) * jax.nn.silu(gb),
        rms_norm(y_c, branch_gain[2]) * jax.nn.silu(gc),
        rms_norm(y_d, branch_gain[3]) * jax.nn.silu(gd),
    ], axis=-1)
    return x + jnp.einsum('bsm,md->bsd', merged, w_out)


def _fwd_setup_inputs(seed: int = 0) -> dict:
    key = jax.random.key(seed)
    ks = jax.random.split(key, 12)
    f32 = jnp.float32
    x = jax.random.normal(ks[0], (BATCH, SEQ, D_MODEL), f32)
    norm_g = 1.0 + 0.02 * jax.random.normal(ks[1], (DEPTH, D_MODEL), f32)
    w_in = jax.random.normal(ks[2], (DEPTH, D_MODEL, N_IN), f32) * D_MODEL ** -0.5
    b_f = 4.0 + 0.5 * jax.random.normal(ks[3], (DEPTH, N_HEADS), f32)
    rel_bias = 0.5 * jax.random.normal(ks[4], (DEPTH, N_HEADS, 2 * MAX_REL + 1), f32)
    w_s = jax.random.normal(ks[5], (DEPTH, N_HEADS, SG_CHUNK, SG_CHUNK), f32) * SG_CHUNK ** -0.5
    b_s = 1.0 + 0.1 * jax.random.normal(ks[6], (DEPTH, N_HEADS, SG_CHUNK), f32)
    v_gain = 1.0 + 0.02 * jax.random.normal(ks[7], (DEPTH, D_BRANCH), f32)
    branch_gain = 1.0 + 0.02 * jax.random.normal(ks[8], (DEPTH, N_MIXERS, D_BRANCH), f32)
    w_out = jax.random.normal(ks[9], (DEPTH, D_MIX, D_MODEL), f32) * (0.5 * D_MIX ** -0.5)
    final_g = 1.0 + 0.02 * jax.random.normal(ks[10], (D_MODEL,), f32)
    return {"x": x, "norm_g": norm_g, "w_in": w_in, "b_f": b_f, "rel_bias": rel_bias,
            "w_s": w_s, "b_s": b_s, "v_gain": v_gain, "branch_gain": branch_gain,
            "w_out": w_out, "final_g": final_g}


def _fwd_reference(x, norm_g, w_in, b_f, rel_bias, w_s, b_s, v_gain, branch_gain, w_out, final_g):
    for l in range(DEPTH):
        x = hybrid_layer(x, norm_g[l], w_in[l], b_f[l], rel_bias[l], w_s[l], b_s[l],
                         v_gain[l], branch_gain[l], w_out[l])
    return rms_norm(x, final_g)


import jax as _jax
import jax.numpy as _jnp

TWIN_FORMAT = 'train_step'
FWD_PARAMS = ['x', 'norm_g', 'w_in', 'b_f', 'rel_bias', 'w_s', 'b_s', 'v_gain', 'branch_gain', 'w_out', 'final_g']
TWIN_WEIGHTS = ['norm_g', 'w_in', 'b_f', 'rel_bias', 'w_s', 'b_s', 'v_gain', 'branch_gain', 'w_out', 'final_g']
TWIN_DIFF_INPUT = 'x'
TWIN_INPUTS = ['x', 'norm_g', 'w_in', 'b_f', 'rel_bias', 'w_s', 'b_s', 'v_gain', 'branch_gain', 'w_out', 'final_g', 'loss_target', 'm_norm_g', 'm_w_in', 'm_b_f', 'm_rel_bias', 'm_w_s', 'm_b_s', 'm_v_gain', 'm_branch_gain', 'm_w_out', 'm_final_g', 'v_norm_g', 'v_w_in', 'v_b_f', 'v_rel_bias', 'v_w_s', 'v_b_s', 'v_v_gain', 'v_branch_gain', 'v_w_out', 'v_final_g']
TWIN_OUTPUTS = ['loss', 'grad_x', 'grad_norm_g', 'grad_w_in', 'grad_b_f', 'grad_rel_bias', 'grad_w_s', 'grad_b_s', 'grad_v_gain', 'grad_branch_gain', 'grad_w_out', 'grad_final_g', 'delta_norm_g', 'delta_w_in', 'delta_b_f', 'delta_rel_bias', 'delta_w_s', 'delta_b_s', 'delta_v_gain', 'delta_branch_gain', 'delta_w_out', 'delta_final_g', 'new_m_norm_g', 'new_m_w_in', 'new_m_b_f', 'new_m_rel_bias', 'new_m_w_s', 'new_m_b_s', 'new_m_v_gain', 'new_m_branch_gain', 'new_m_w_out', 'new_m_final_g', 'new_v_norm_g', 'new_v_w_in', 'new_v_b_f', 'new_v_rel_bias', 'new_v_w_s', 'new_v_b_s', 'new_v_v_gain', 'new_v_branch_gain', 'new_v_w_out', 'new_v_final_g']
TWIN_LEAF_KINDS = {'loss': 'loss', 'grad_x': 'grad_x', 'grad_norm_g': 'grad_w', 'grad_w_in': 'grad_w', 'grad_b_f': 'grad_w', 'grad_rel_bias': 'grad_w', 'grad_w_s': 'grad_w', 'grad_b_s': 'grad_w', 'grad_v_gain': 'grad_w', 'grad_branch_gain': 'grad_w', 'grad_w_out': 'grad_w', 'grad_final_g': 'grad_w', 'delta_norm_g': 'delta_w', 'delta_w_in': 'delta_w', 'delta_b_f': 'delta_w', 'delta_rel_bias': 'delta_w', 'delta_w_s': 'delta_w', 'delta_b_s': 'delta_w', 'delta_v_gain': 'delta_w', 'delta_branch_gain': 'delta_w', 'delta_w_out': 'delta_w', 'delta_final_g': 'delta_w', 'new_m_norm_g': 'new_m', 'new_m_w_in': 'new_m', 'new_m_b_f': 'new_m', 'new_m_rel_bias': 'new_m', 'new_m_w_s': 'new_m', 'new_m_b_s': 'new_m', 'new_m_v_gain': 'new_m', 'new_m_branch_gain': 'new_m', 'new_m_w_out': 'new_m', 'new_m_final_g': 'new_m', 'new_v_norm_g': 'new_v', 'new_v_w_in': 'new_v', 'new_v_b_f': 'new_v', 'new_v_rel_bias': 'new_v', 'new_v_w_s': 'new_v', 'new_v_b_s': 'new_v', 'new_v_v_gain': 'new_v', 'new_v_branch_gain': 'new_v', 'new_v_w_out': 'new_v', 'new_v_final_g': 'new_v'}


def _forward(args):
    return _fwd_reference(*[args[k] for k in FWD_PARAMS])


def _output_shape():
    def fwd():
        inp = _fwd_setup_inputs(0)
        return _fwd_reference(*[inp[k] for k in FWD_PARAMS])
    out = _jax.eval_shape(fwd)
    return out.shape, out.dtype

N_MICROBATCH = 1
ADAM_LR = 0.001
ADAM_B1 = 0.9
ADAM_B2 = 0.999
ADAM_EPS = 1e-08
ADAM_WD = 0.01
ADAM_STEP = 10
PER_EXAMPLE_BATCH_AXIS = {'x': 0, 'loss_target': 0}
SHARED_INPUTS = []
_WEIGHT_DTYPES = {'norm_g': _jnp.float32, 'w_in': _jnp.float32, 'b_f': _jnp.float32, 'rel_bias': _jnp.float32, 'w_s': _jnp.float32, 'b_s': _jnp.float32, 'v_gain': _jnp.float32, 'branch_gain': _jnp.float32, 'w_out': _jnp.float32, 'final_g': _jnp.float32}
MOMENT_SCALE = {'norm_g': 1.973535e-01, 'w_in': 9.526462e-02, 'b_f': 5.130882e-01, 'rel_bias': 4.464738e-02, 'w_s': 4.065083e-02, 'b_s': 6.123414e-02, 'v_gain': 5.713459e-02, 'branch_gain': 1.071868e-01, 'w_out': 2.087170e-01, 'final_g': 1.279974e+02}


def _to_microbatches(a, axis):
    t = _jnp.moveaxis(a, axis, 0)
    t = t.reshape((N_MICROBATCH, t.shape[0] // N_MICROBATCH) + t.shape[1:])
    return _jnp.moveaxis(t, 1, axis + 1)


def setup_inputs(seed: int = 0) -> dict:
    inp = _fwd_setup_inputs(seed)
    key = _jax.random.fold_in(_jax.random.key(seed), 7919)
    shape, _ = _output_shape()
    out = dict(inp)
    out["loss_target"] = _jax.random.normal(_jax.random.fold_in(key, 0), shape, _jnp.float32)
    for i, name in enumerate(TWIN_WEIGHTS):
        w = inp[name].astype(_jnp.float32)
        if MOMENT_SCALE is None:
            s = _jnp.sqrt(_jnp.mean(_jnp.square(w)) + 1e-30)
        else:
            s = MOMENT_SCALE[name]
        km, kv = _jax.random.split(_jax.random.fold_in(key, i + 1))
        out[name] = w
        out["m_" + name] = s * _jax.random.normal(km, w.shape, _jnp.float32)
        out["v_" + name] = (s * s) * _jax.random.uniform(kv, w.shape, _jnp.float32, 0.5, 1.5)
    if N_MICROBATCH > 1:
        for name, axis in PER_EXAMPLE_BATCH_AXIS.items():
            out[name] = _to_microbatches(out[name], axis)
    return {'x': out['x'], 'norm_g': out['norm_g'], 'w_in': out['w_in'], 'b_f': out['b_f'], 'rel_bias': out['rel_bias'], 'w_s': out['w_s'], 'b_s': out['b_s'], 'v_gain': out['v_gain'], 'branch_gain': out['branch_gain'], 'w_out': out['w_out'], 'final_g': out['final_g'], 'loss_target': out['loss_target'], 'm_norm_g': out['m_norm_g'], 'm_w_in': out['m_w_in'], 'm_b_f': out['m_b_f'], 'm_rel_bias': out['m_rel_bias'], 'm_w_s': out['m_w_s'], 'm_b_s': out['m_b_s'], 'm_v_gain': out['m_v_gain'], 'm_branch_gain': out['m_branch_gain'], 'm_w_out': out['m_w_out'], 'm_final_g': out['m_final_g'], 'v_norm_g': out['v_norm_g'], 'v_w_in': out['v_w_in'], 'v_b_f': out['v_b_f'], 'v_rel_bias': out['v_rel_bias'], 'v_w_s': out['v_w_s'], 'v_b_s': out['v_b_s'], 'v_v_gain': out['v_v_gain'], 'v_branch_gain': out['v_branch_gain'], 'v_w_out': out['v_w_out'], 'v_final_g': out['v_final_g']}


def _loss(weights, diff, rest, loss_target):
    with _jax.named_scope("forward"):
        args = {**rest, TWIN_DIFF_INPUT: diff, **{k: w.astype(_WEIGHT_DTYPES[k]) for k, w in weights.items()}}
        y = _forward(args)
    with _jax.named_scope("loss_head"):
        err = _jnp.square(y.astype(_jnp.float32) - loss_target)
        return 0.5 * _jnp.sum(_jnp.mean(err, axis=-1)) if err.ndim else 0.5 * err


def _adamw(w, g, m, v):
    m = ADAM_B1 * m + (1.0 - ADAM_B1) * g
    v = ADAM_B2 * v + (1.0 - ADAM_B2) * _jnp.square(g)
    m_hat = m / (1.0 - ADAM_B1 ** ADAM_STEP)
    v_hat = v / (1.0 - ADAM_B2 ** ADAM_STEP)
    delta = -ADAM_LR * (m_hat / (_jnp.sqrt(v_hat) + ADAM_EPS) + ADAM_WD * w)
    return delta, m, v


def reference(x, norm_g, w_in, b_f, rel_bias, w_s, b_s, v_gain, branch_gain, w_out, final_g, loss_target, m_norm_g, m_w_in, m_b_f, m_rel_bias, m_w_s, m_b_s, m_v_gain, m_branch_gain, m_w_out, m_final_g, v_norm_g, v_w_in, v_b_f, v_rel_bias, v_w_s, v_b_s, v_v_gain, v_branch_gain, v_w_out, v_final_g):
    given = dict(x=x, norm_g=norm_g, w_in=w_in, b_f=b_f, rel_bias=rel_bias, w_s=w_s, b_s=b_s, v_gain=v_gain, branch_gain=branch_gain, w_out=w_out, final_g=final_g, loss_target=loss_target, m_norm_g=m_norm_g, m_w_in=m_w_in, m_b_f=m_b_f, m_rel_bias=m_rel_bias, m_w_s=m_w_s, m_b_s=m_b_s, m_v_gain=m_v_gain, m_branch_gain=m_branch_gain, m_w_out=m_w_out, m_final_g=m_final_g, v_norm_g=v_norm_g, v_w_in=v_w_in, v_b_f=v_b_f, v_rel_bias=v_rel_bias, v_w_s=v_w_s, v_b_s=v_b_s, v_v_gain=v_v_gain, v_branch_gain=v_branch_gain, v_w_out=v_w_out, v_final_g=v_final_g)
    weights = {n: given[n] for n in TWIN_WEIGHTS}
    shared = {n: given[n] for n in SHARED_INPUTS}
    per_example = {n: given[n] for n in ['x']}
    grad_fn = _jax.value_and_grad(_loss, argnums=(0, 1))

    def one_microbatch(ex, loss_target):
        ex = dict(ex)
        diff = ex.pop(TWIN_DIFF_INPUT)
        return grad_fn(weights, diff, {**shared, **ex}, loss_target)

    if N_MICROBATCH == 1:
        loss, (grad_w, grad_x) = one_microbatch(per_example, given["loss_target"])
    else:
        def body(carry, xs):
            loss_sum, grad_sum = carry
            l_k, (gw_k, gx_k) = one_microbatch(xs[0], xs[1])
            with _jax.named_scope("update"):
                return (loss_sum + l_k, _jax.tree.map(_jnp.add, grad_sum, gw_k)), gx_k

        init = (_jnp.zeros((), _jnp.float32), _jax.tree.map(_jnp.zeros_like, weights))
        (loss, grad_w), grad_x = _jax.lax.scan(body, init, (per_example, given["loss_target"]))
    with _jax.named_scope("update"):
        delta_w, new_m, new_v = {}, {}, {}
        for n in TWIN_WEIGHTS:
            delta_w[n], new_m[n], new_v[n] = _adamw(weights[n], grad_w[n], given["m_" + n], given["v_" + n])
    return (loss, grad_x, *[grad_w[n] for n in TWIN_WEIGHTS], *[delta_w[n] for n in TWIN_WEIGHTS],
            *[new_m[n] for n in TWIN_WEIGHTS], *[new_v[n] for n in TWIN_WEIGHTS])
```

```python
import functools

import jax
import jax.numpy as jnp
import numpy as np
from jax import lax
from jax.experimental import pallas as pl
from jax.experimental.pallas import tpu as pltpu

F32 = jnp.float32
BF16 = jnp.bfloat16
MESH = pl.DeviceIdType.MESH

D_MODEL = 1024
D_BRANCH = 256
N_HEADS = 4
HEAD_DIM = 64
CHUNK = 64
LOOKBACK = 8
MAX_REL = 128
SG_CHUNK = 128
EPS = 1e-6
N_IN = 3844
N_PACK = 3968
F_COL = 3840
N_SHARD = 961
NEG = -1e30

A_TQ = 128
A_NKB = 5
A_BAND = A_TQ * A_NKB
ATT_T = 256
ROW_T = 512
VMEM_LIMIT = 56 * 1024 * 1024

ADAM_LR = 0.001
ADAM_B1 = 0.9
ADAM_B2 = 0.999
ADAM_EPS = 1e-08
ADAM_WD = 0.01
ADAM_STEP = 10

SEC_A_Q, SEC_A_K, SEC_A_V, SEC_A_G = 0, 256, 512, 768
SEC_B_U, SEC_B_V, SEC_B_G = 1024, 1280, 1536
SEC_C_Q, SEC_C_K, SEC_C_V, SEC_C_G = 1792, 2048, 2304, 2560
SEC_D_Q, SEC_D_K, SEC_D_V, SEC_D_G = 2816, 3072, 3328, 3584
QKV_SECS = (SEC_A_Q, SEC_A_K, SEC_A_V, SEC_C_Q, SEC_C_K, SEC_C_V, SEC_D_Q, SEC_D_K, SEC_D_V)
GATE_SECS = (SEC_A_G, SEC_B_G, SEC_C_G, SEC_D_G)


def _dot(a, b):
    return jnp.dot(a, b, preferred_element_type=F32)


def _dot_nt(a, b):
    return lax.dot_general(a, b, (((1,), (1,)), ((), ())), preferred_element_type=F32)


def _dot_tn(a, b):
    return lax.dot_general(a, b, (((0,), (0,)), ((), ())), preferred_element_type=F32)


def _split2(x):
    hi = x.astype(BF16)
    lo = (x - hi.astype(F32)).astype(BF16)
    return hi, lo


def _split3(x):
    hi = x.astype(BF16)
    r = x - hi.astype(F32)
    mid = r.astype(BF16)
    lo = (r - mid.astype(F32)).astype(BF16)
    return hi, mid, lo


def _sigmoid(x):
    return 1.0 / (1.0 + jnp.exp(-x))


def _params(sem=None, vmem=VMEM_LIMIT):
    return pltpu.CompilerParams(dimension_semantics=sem, vmem_limit_bytes=vmem)


def _heads_to_lanes(ref):
    return jnp.concatenate([ref[h] for h in range(N_HEADS)], axis=1)


def inproj_fwd(x, g, wp):
    s = x.shape[0]
    tm = min(ROW_T, s)

    def body(x_ref, g_ref, w_ref, h_ref, qkv_ref, gates_ref, uv_ref, f_ref):
        xv = x_ref[...]
        r = lax.rsqrt(jnp.mean(xv * xv, axis=-1, keepdims=True) + EPS)
        h = (xv * r * g_ref[...]).astype(BF16)
        h_ref[...] = h
        for n, off in enumerate(QKV_SECS):
            p = _dot(h, w_ref[:, off:off + D_BRANCH])
            for hh in range(N_HEADS):
                qkv_ref[n, hh] = p[:, hh * HEAD_DIM:(hh + 1) * HEAD_DIM].astype(BF16)
        for n, off in enumerate(GATE_SECS):
            gates_ref[:, n * D_BRANCH:(n + 1) * D_BRANCH] = _dot(h, w_ref[:, off:off + D_BRANCH])
        uv_ref[...] = _dot(h, w_ref[:, SEC_B_U:SEC_B_U + 2 * D_BRANCH])
        f_ref[...] = _dot(h, w_ref[:, F_COL:F_COL + 128])

    return pl.pallas_call(
        body, name="inproj_fwd", grid=(s // tm,),
        in_specs=[pl.BlockSpec((tm, D_MODEL), lambda i: (i, 0)),
                  pl.BlockSpec((1, D_MODEL), lambda i: (0, 0)),
                  pl.BlockSpec((D_MODEL, N_PACK), lambda i: (0, 0))],
        out_specs=[pl.BlockSpec((tm, D_MODEL), lambda i: (i, 0)),
                   pl.BlockSpec((9, N_HEADS, tm, HEAD_DIM), lambda i: (0, 0, i, 0)),
                   pl.BlockSpec((tm, D_MODEL), lambda i: (i, 0)),
                   pl.BlockSpec((tm, 2 * D_BRANCH), lambda i: (i, 0)),
                   pl.BlockSpec((tm, 128), lambda i: (i, 0))],
        out_shape=[jax.ShapeDtypeStruct((s, D_MODEL), BF16),
                   jax.ShapeDtypeStruct((9, N_HEADS, s, HEAD_DIM), BF16),
                   jax.ShapeDtypeStruct((s, D_MODEL), F32),
                   jax.ShapeDtypeStruct((s, 2 * D_BRANCH), F32),
                   jax.ShapeDtypeStruct((s, 128), F32)],
        compiler_params=_params(("arbitrary",)),
    )(x, g, wp)


def inproj_bwd(dqkv, dgates, duv, dfp, wp, x, g, dres):
    s = x.shape[0]
    tm = min(ROW_T, s)

    def body(*refs):
        dq_refs = refs[:9]
        dgates_ref, duv_ref, dfp_ref, w_ref, x_ref, g_ref, dres_ref, dp_ref, dx_ref, dg_ref = refs[9:]
        i = pl.program_id(0)
        a_q, a_k, a_v, c_q, c_k, c_v, d_q, d_k, d_v = [_heads_to_lanes(r).astype(BF16) for r in dq_refs]
        dgt = dgates_ref[...]
        duv_b = duv_ref[...].astype(BF16)
        dp = jnp.concatenate(
            [a_q, a_k, a_v, dgt[:, 0:256], duv_b, dgt[:, 256:512], c_q, c_k, c_v, dgt[:, 512:768],
             d_q, d_k, d_v, dgt[:, 768:1024], dfp_ref[...].astype(BF16)], axis=1)
        dp_ref[...] = dp
        dh = _dot_nt(dp, w_ref[...])
        xv = x_ref[...]
        r = lax.rsqrt(jnp.mean(xv * xv, axis=-1, keepdims=True) + EPS)
        xn = xv * r
        u = dh * g_ref[...]
        dx_ref[...] = dres_ref[...] + r * (u - xn * jnp.mean(xn * u, axis=-1, keepdims=True))

        @pl.when(i == 0)
        def _():
            dg_ref[...] = jnp.zeros_like(dg_ref)

        dg_ref[...] += jnp.sum(dh * xn, axis=0, keepdims=True)

    head_spec = pl.BlockSpec((N_HEADS, tm, HEAD_DIM), lambda i: (0, i, 0))
    return pl.pallas_call(
        body, name="inproj_bwd", grid=(s // tm,),
        in_specs=[head_spec] * 9 + [
            pl.BlockSpec((tm, D_MODEL), lambda i: (i, 0)),
            pl.BlockSpec((tm, 2 * D_BRANCH), lambda i: (i, 0)),
            pl.BlockSpec((tm, 128), lambda i: (i, 0)),
            pl.BlockSpec((D_MODEL, N_PACK), lambda i: (0, 0)),
            pl.BlockSpec((tm, D_MODEL), lambda i: (i, 0)),
            pl.BlockSpec((1, D_MODEL), lambda i: (0, 0)),
            pl.BlockSpec((tm, D_MODEL), lambda i: (i, 0))],
        out_specs=[pl.BlockSpec((tm, N_PACK), lambda i: (i, 0)),
                   pl.BlockSpec((tm, D_MODEL), lambda i: (i, 0)),
                   pl.BlockSpec((1, D_MODEL), lambda i: (0, 0))],
        out_shape=[jax.ShapeDtypeStruct((s, N_PACK), BF16),
                   jax.ShapeDtypeStruct((s, D_MODEL), F32),
                   jax.ShapeDtypeStruct((1, D_MODEL), F32)],
        compiler_params=_params(("arbitrary",)),
    )(*dqkv, dgates, duv, dfp, wp, x, g, dres)


def weight_grad(a, b, name):
    s, m = a.shape
    n = b.shape[1]
    tm = min(ROW_T, s)
    tmm = 256
    nsteps = s // tm

    def body(a_ref, b_ref, o_ref):
        k = pl.program_id(1)

        @pl.when(k == 0)
        def _():
            o_ref[...] = jnp.zeros_like(o_ref)

        o_ref[...] += _dot_tn(a_ref[...], b_ref[...])

    return pl.pallas_call(
        body, name=name, grid=(m // tmm, nsteps),
        in_specs=[pl.BlockSpec((tm, tmm), lambda j, k: (k, j)),
                  pl.BlockSpec((tm, n), lambda j, k: (k, 0))],
        out_specs=pl.BlockSpec((tmm, n), lambda j, k: (j, 0)),
        out_shape=jax.ShapeDtypeStruct((m, n), F32),
        compiler_params=_params(("arbitrary", "arbitrary")),
    )(a, b)


def _a_specs(s):
    nq = s // A_TQ
    q_spec = pl.BlockSpec((None, None, A_TQ, HEAD_DIM), lambda h, i: (0, h, jnp.minimum(i, nq - 1), 0))
    kv_specs = [pl.BlockSpec((None, A_TQ, HEAD_DIM),
                             lambda h, i, m=m: (h, jnp.minimum(i + m, nq + A_NKB - 2), 0)) for m in range(A_NKB)]
    t_spec = pl.BlockSpec((None, A_TQ, A_BAND), lambda h, i: (h, 0, 0))
    return nq, q_spec, kv_specs, t_spec


def _a_scores(q_ref, k_refs, t_ref, i):
    qs = q_ref[...] * 0.125
    k = jnp.concatenate([r[...] for r in k_refs], axis=0)
    sc = _dot_nt(qs, k) + t_ref[...]
    col = lax.broadcasted_iota(jnp.int32, (A_TQ, A_BAND), 1)
    sc = jnp.where(col >= (A_BAND - A_TQ) - i * A_TQ, sc, NEG)
    return qs, k, sc


def mix_a_fwd(qkv, kpad, vpad, tbias):
    s = qkv.shape[2]
    nq, q_spec, kv_specs, t_spec = _a_specs(s)

    def body(*refs):
        q_ref = refs[0]
        k_refs = refs[1:1 + A_NKB]
        v_refs = refs[1 + A_NKB:1 + 2 * A_NKB]
        t_ref, o_ref, lse_ref = refs[1 + 2 * A_NKB:]
        i = pl.program_id(1)
        _, _, sc = _a_scores(q_ref, k_refs, t_ref, i)
        m = jnp.max(sc, axis=-1, keepdims=True)
        p = jnp.exp(sc - m)
        l = jnp.sum(p, axis=-1, keepdims=True)
        v = jnp.concatenate([r[...] for r in v_refs], axis=0)
        o_ref[...] = _dot(p.astype(BF16), v) / l
        lse_ref[...] = m + jnp.log(l)

    return pl.pallas_call(
        body, name="mix_a_fwd", grid=(N_HEADS, nq),
        in_specs=[q_spec] + kv_specs + kv_specs + [t_spec],
        out_specs=[pl.BlockSpec((None, A_TQ, HEAD_DIM), lambda h, i: (h, i, 0)),
                   pl.BlockSpec((None, A_TQ, 1), lambda h, i: (h, i, 0))],
        out_shape=[jax.ShapeDtypeStruct((N_HEADS, s, HEAD_DIM), F32),
                   jax.ShapeDtypeStruct((N_HEADS, s, 1), F32)],
        compiler_params=_params(("arbitrary", "arbitrary")),
    )(qkv, *([kpad] * A_NKB), *([vpad] * A_NKB), tbias)


def mix_a_bwd(qkv, kpad, vpad, tbias, do, o, lse):
    s = qkv.shape[2]
    nq, q_spec, kv_specs, t_spec = _a_specs(s)
    row_spec = lambda w: pl.BlockSpec((None, A_TQ, w), lambda h, i: (h, jnp.minimum(i, nq - 1), 0))
    keep = A_BAND - A_TQ

    def body(*refs):
        q_ref = refs[0]
        k_refs = refs[1:1 + A_NKB]
        v_refs = refs[1 + A_NKB:1 + 2 * A_NKB]
        t_ref, do_ref, o_ref, lse_ref, dq_ref, dk_ref, dv_ref, dt_ref, dk_win, dv_win = refs[1 + 2 * A_NKB:]
        i = pl.program_id(1)

        @pl.when(i == 0)
        def _():
            dk_win[...] = jnp.zeros_like(dk_win)
            dv_win[...] = jnp.zeros_like(dv_win)
            dt_ref[...] = jnp.zeros_like(dt_ref)

        @pl.when(i < nq)
        def _():
            qs, k, sc = _a_scores(q_ref, k_refs, t_ref, i)
            v = jnp.concatenate([r[...] for r in v_refs], axis=0)
            dob = do_ref[...]
            p = jnp.exp(sc - lse_ref[...])
            delta = jnp.sum(o_ref[...] * dob.astype(F32), axis=-1, keepdims=True)
            ds = p * (_dot_nt(dob, v) - delta)
            dsb = ds.astype(BF16)
            dq_ref[...] = _dot(dsb, k) * 0.125
            dk_win[...] += _dot_tn(dsb, qs)
            dv_win[...] += _dot_tn(p.astype(BF16), dob)
            dt_ref[...] += ds

        dk_ref[...] = dk_win[0:A_TQ, :]
        dv_ref[...] = dv_win[0:A_TQ, :]
        dk_rest = dk_win[A_TQ:A_BAND, :]
        dv_rest = dv_win[A_TQ:A_BAND, :]
        dk_win[0:keep, :] = dk_rest
        dv_win[0:keep, :] = dv_rest
        dk_win[keep:A_BAND, :] = jnp.zeros((A_TQ, HEAD_DIM), F32)
        dv_win[keep:A_BAND, :] = jnp.zeros((A_TQ, HEAD_DIM), F32)

    nsteps = nq + A_NKB - 1
    return pl.pallas_call(
        body, name="mix_a_bwd", grid=(N_HEADS, nsteps),
        in_specs=[q_spec] + kv_specs + kv_specs + [t_spec, row_spec(HEAD_DIM), row_spec(HEAD_DIM), row_spec(1)],
        out_specs=[row_spec(HEAD_DIM),
                   pl.BlockSpec((None, A_TQ, HEAD_DIM), lambda h, i: (h, i, 0)),
                   pl.BlockSpec((None, A_TQ, HEAD_DIM), lambda h, i: (h, i, 0)),
                   t_spec],
        out_shape=[jax.ShapeDtypeStruct((N_HEADS, s, HEAD_DIM), F32),
                   jax.ShapeDtypeStruct((N_HEADS, s + keep, HEAD_DIM), F32),
                   jax.ShapeDtypeStruct((N_HEADS, s + keep, HEAD_DIM), F32),
                   jax.ShapeDtypeStruct((N_HEADS, A_TQ, A_BAND), F32)],
        scratch_shapes=[pltpu.VMEM((A_BAND, HEAD_DIM), F32), pltpu.VMEM((A_BAND, HEAD_DIM), F32)],
        compiler_params=_params(("arbitrary", "arbitrary")),
    )(qkv, *([kpad] * A_NKB), *([vpad] * A_NKB), tbias, do, o, lse)


def relbias_tile(rel_bias, relmat):
    nrel = 2 * MAX_REL + 1

    def body(rb_ref, rel_ref, o_ref):
        rel = rel_ref[...]
        o_ref[...] = jnp.full(o_ref.shape, NEG, F32)

        def step(r, carry):
            hit = rel == r
            for h in range(N_HEADS):
                o_ref[h] = jnp.where(hit, rb_ref[h, r], o_ref[h])
            return carry

        lax.fori_loop(0, nrel, step, 0)

    return pl.pallas_call(
        body, name="relbias_tile",
        in_specs=[pl.BlockSpec(memory_space=pltpu.SMEM), pl.BlockSpec(memory_space=pltpu.VMEM)],
        out_specs=pl.BlockSpec(memory_space=pltpu.VMEM),
        out_shape=jax.ShapeDtypeStruct((N_HEADS, A_TQ, A_BAND), F32),
        compiler_params=_params(),
    )(rel_bias, relmat)


def relbias_grad(dt, relmat):
    nrel = 2 * MAX_REL + 1

    def body(dt_ref, rel_ref, o_ref):
        rel = rel_ref[...]
        lane = lax.broadcasted_iota(jnp.int32, (8, 384), 1)
        row = lax.broadcasted_iota(jnp.int32, (8, 384), 0)

        def step(r, acc):
            hit = rel == r
            for h in range(N_HEADS):
                val = jnp.sum(jnp.where(hit, dt_ref[h], 0.0))
                acc = jnp.where((lane == r) & (row == h), val, acc)
            return acc

        o_ref[...] = lax.fori_loop(0, nrel, step, jnp.zeros((8, 384), F32))

    return pl.pallas_call(
        body, name="relbias_grad",
        out_shape=jax.ShapeDtypeStruct((8, 384), F32),
        compiler_params=_params(),
    )(dt, relmat)


def _b_norm(v, gain):
    mu = jnp.mean(v, axis=-1, keepdims=True)
    xc = v - mu
    rstd = lax.rsqrt(jnp.mean(xc * xc, axis=-1, keepdims=True) + EPS)
    xhat = xc * rstd
    return xhat, rstd, xhat * gain


def _tril_mask():
    t = lax.broadcasted_iota(jnp.int32, (SG_CHUNK, SG_CHUNK), 0)
    u = lax.broadcasted_iota(jnp.int32, (SG_CHUNK, SG_CHUNK), 1)
    return u <= t


def mix_b_fwd(uv, gain, w_s, b_col):
    s = uv.shape[0]
    tm = min(ROW_T, s)

    def body(uv_ref, gain_ref, w_ref, b_ref, y_ref):
        tril = _tril_mask()
        ws = [jnp.where(tril, w_ref[g], 0.0).astype(BF16) for g in range(N_HEADS)]
        for c in range(tm // SG_CHUNK):
            rows = slice(c * SG_CHUNK, (c + 1) * SG_CHUNK)
            u = uv_ref[rows, 0:D_BRANCH]
            _, _, vn = _b_norm(uv_ref[rows, D_BRANCH:2 * D_BRANCH], gain_ref[...])
            vnb = vn.astype(BF16)
            outs = []
            for g in range(N_HEADS):
                cols = slice(g * HEAD_DIM, (g + 1) * HEAD_DIM)
                mixed = _dot(ws[g], vnb[:, cols]) + b_ref[g]
                outs.append(u[:, cols] * mixed)
            y_ref[rows, :] = jnp.concatenate(outs, axis=1)

    return pl.pallas_call(
        body, name="mix_b_fwd", grid=(s // tm,),
        in_specs=[pl.BlockSpec((tm, 2 * D_BRANCH), lambda i: (i, 0)),
                  pl.BlockSpec((1, D_BRANCH), lambda i: (0, 0)),
                  pl.BlockSpec((N_HEADS, SG_CHUNK, SG_CHUNK), lambda i: (0, 0, 0)),
                  pl.BlockSpec((N_HEADS, SG_CHUNK, 1), lambda i: (0, 0, 0))],
        out_specs=pl.BlockSpec((tm, D_BRANCH), lambda i: (i, 0)),
        out_shape=jax.ShapeDtypeStruct((s, D_BRANCH), F32),
        compiler_params=_params(("arbitrary",)),
    )(uv, gain, w_s, b_col)


def mix_b_bwd(uv, gain, w_s, b_col, dy):
    s = uv.shape[0]
    tm = min(ROW_T, s)

    def body(uv_ref, gain_ref, w_ref, b_ref, dy_ref, duv_ref, dw_ref, db_ref, dgain_ref):
        i = pl.program_id(0)

        @pl.when(i == 0)
        def _():
            dw_ref[...] = jnp.zeros_like(dw_ref)
            db_ref[...] = jnp.zeros_like(db_ref)
            dgain_ref[...] = jnp.zeros_like(dgain_ref)

        tril = _tril_mask()
        ws = [jnp.where(tril, w_ref[g], 0.0).astype(BF16) for g in range(N_HEADS)]
        gain_v = gain_ref[...]
        for c in range(tm // SG_CHUNK):
            rows = slice(c * SG_CHUNK, (c + 1) * SG_CHUNK)
            u = uv_ref[rows, 0:D_BRANCH]
            xhat, rstd, vn = _b_norm(uv_ref[rows, D_BRANCH:2 * D_BRANCH], gain_v)
            vnb = vn.astype(BF16)
            dyv = dy_ref[rows, :]
            dus, dvns = [], []
            for g in range(N_HEADS):
                cols = slice(g * HEAD_DIM, (g + 1) * HEAD_DIM)
                mixed = _dot(ws[g], vnb[:, cols]) + b_ref[g]
                dus.append(dyv[:, cols] * mixed)
                dmixed = dyv[:, cols] * u[:, cols]
                dmb = dmixed.astype(BF16)
                db_ref[g] += jnp.sum(dmixed, axis=-1, keepdims=True)
                dw_ref[g] += jnp.where(tril, _dot_nt(dmb, vnb[:, cols]), 0.0)
                dvns.append(_dot_tn(ws[g], dmb))
            dvn = jnp.concatenate(dvns, axis=1)
            dgain_ref[...] += jnp.sum(dvn * xhat, axis=0, keepdims=True)
            dxh = dvn * gain_v
            dv = rstd * (dxh - jnp.mean(dxh, axis=-1, keepdims=True)
                         - xhat * jnp.mean(dxh * xhat, axis=-1, keepdims=True))
            duv_ref[rows, :] = jnp.concatenate(dus + [dv], axis=1)

    return pl.pallas_call(
        body, name="mix_b_bwd", grid=(s // tm,),
        in_specs=[pl.BlockSpec((tm, 2 * D_BRANCH), lambda i: (i, 0)),
                  pl.BlockSpec((1, D_BRANCH), lambda i: (0, 0)),
                  pl.BlockSpec((N_HEADS, SG_CHUNK, SG_CHUNK), lambda i: (0, 0, 0)),
                  pl.BlockSpec((N_HEADS, SG_CHUNK, 1), lambda i: (0, 0, 0)),
                  pl.BlockSpec((tm, D_BRANCH), lambda i: (i, 0))],
        out_specs=[pl.BlockSpec((tm, 2 * D_BRANCH), lambda i: (i, 0)),
                   pl.BlockSpec((N_HEADS, SG_CHUNK, SG_CHUNK), lambda i: (0, 0, 0)),
                   pl.BlockSpec((N_HEADS, SG_CHUNK, 1), lambda i: (0, 0, 0)),
                   pl.BlockSpec((1, D_BRANCH), lambda i: (0, 0))],
        out_shape=[jax.ShapeDtypeStruct((s, 2 * D_BRANCH), F32),
                   jax.ShapeDtypeStruct((N_HEADS, SG_CHUNK, SG_CHUNK), F32),
                   jax.ShapeDtypeStruct((N_HEADS, SG_CHUNK, 1), F32),
                   jax.ShapeDtypeStruct((1, D_BRANCH), F32)],
        compiler_params=_params(("arbitrary",)),
    )(uv, gain, w_s, b_col, dy)


def _scan_mats(nrow):
    a = lax.broadcasted_iota(jnp.int32, (128, 128), 0)
    b = lax.broadcasted_iota(jnp.int32, (128, 128), 1)
    r = lax.broadcasted_iota(jnp.int32, (nrow, nrow), 0)
    c = lax.broadcasted_iota(jnp.int32, (nrow, nrow), 1)
    nb = nrow // N_HEADS
    same = (r // nb) == (c // nb)
    return a, b, r, c, same


def _exact_dot(x, m):
    hi, mid, lo = _split3(x)
    return _dot(hi, m) + _dot(mid, m) + _dot(lo, m)


def _exact_dot_left(m, x):
    hi, mid, lo = _split3(x)
    return _dot(m, hi) + _dot(m, mid) + _dot(m, lo)


def fox_gate_fwd(ft, bcol):
    nrow = ft.shape[0]

    def body(f_ref, b_ref, c_ref):
        z = f_ref[...] + b_ref[...]
        ls = jnp.minimum(z, 0.0) - jnp.log(1.0 + jnp.exp(-jnp.abs(z)))
        a, b, r, c, same = _scan_mats(nrow)
        within = _exact_dot(ls, (a <= b).astype(BF16))
        tot = jnp.broadcast_to(within[:, 127:128], within.shape)
        before = _exact_dot_left((same & (c < r)).astype(BF16), tot)
        c_ref[...] = within + before

    return pl.pallas_call(
        body, name="fox_gate_fwd",
        out_shape=jax.ShapeDtypeStruct((nrow, 128), F32),
        compiler_params=_params(),
    )(ft, bcol)


def fox_gate_bwd(ft, bcol, dc):
    nrow = ft.shape[0]

    def body(f_ref, b_ref, dc_ref, df_ref, db_ref):
        a, b, r, c, same = _scan_mats(nrow)
        dcv = dc_ref[...]
        within = _exact_dot(dcv, (a >= b).astype(BF16))
        tot = jnp.broadcast_to(within[:, 0:1], within.shape)
        after = _exact_dot_left((same & (c > r)).astype(BF16), tot)
        dls = within + after
        z = f_ref[...] + b_ref[...]
        dz = dls * _sigmoid(-z)
        df_ref[...] = dz
        rs = jnp.broadcast_to(jnp.sum(dz, axis=-1, keepdims=True), dz.shape)
        hr = lax.broadcasted_iota(jnp.int32, (8, nrow), 0)
        hc = lax.broadcasted_iota(jnp.int32, (8, nrow), 1)
        db_ref[...] = _exact_dot_left((hr == hc // (nrow // N_HEADS)).astype(BF16), rs)

    return pl.pallas_call(
        body, name="fox_gate_bwd",
        out_shape=[jax.ShapeDtypeStruct((nrow, 128), F32), jax.ShapeDtypeStruct((8, 128), F32)],
        compiler_params=_params(),
    )(ft, bcol, dc)


def _att_specs(s, qi, ki, vi):
    t = ATT_T
    q_spec = pl.BlockSpec((None, None, t, HEAD_DIM), lambda h, i: (qi, h, i, 0))
    k_spec = pl.BlockSpec((None, None, s, HEAD_DIM), lambda h, i: (ki, h, 0, 0))
    v_spec = pl.BlockSpec((None, None, s, HEAD_DIM), lambda h, i: (vi, h, 0, 0))
    row_spec = lambda w: pl.BlockSpec((None, t, w), lambda h, i: (h, i, 0))
    return q_spec, k_spec, v_spec, row_spec


def _causal(strict):
    row = lax.broadcasted_iota(jnp.int32, (ATT_T, ATT_T), 0)
    col = lax.broadcasted_iota(jnp.int32, (ATT_T, ATT_T), 1)
    return (col < row) if strict else (col <= row)


def fox_fwd(qkv, c_col, c_row):
    s = qkv.shape[2]
    t = ATT_T
    nq = s // t
    q_spec, k_spec, v_spec, row_spec = _att_specs(s, 3, 4, 5)

    def body(q_ref, k_ref, v_ref, cc_ref, cr_ref, o_ref, lse_ref):
        i = pl.program_id(1)
        qs = q_ref[...] * 0.125
        cc = cc_ref[...]

        def tile(kb, carry, masked):
            m, l, acc = carry
            k0 = pl.multiple_of(kb * t, t)
            sc = _dot_nt(qs, k_ref[pl.ds(k0, t), :]) + (cc - cr_ref[kb])
            if masked:
                sc = jnp.where(_causal(False), sc, NEG)
            m_new = jnp.maximum(m, jnp.max(sc, axis=-1, keepdims=True))
            alpha = jnp.exp(m - m_new)
            p = jnp.exp(sc - m_new)
            l = alpha * l + jnp.sum(p, axis=-1, keepdims=True)
            p_hi, p_lo = _split2(p)
            v = v_ref[pl.ds(k0, t), :]
            acc = alpha * acc + (_dot(p_hi, v) + _dot(p_lo, v))
            return m_new, l, acc

        init = (jnp.full((t, 1), NEG, F32), jnp.zeros((t, 1), F32), jnp.zeros((t, HEAD_DIM), F32))
        carry = lax.fori_loop(0, i, lambda kb, c: tile(kb, c, False), init)
        m, l, acc = tile(i, carry, True)
        o_ref[...] = acc / l
        lse_ref[...] = m + jnp.log(l)

    return pl.pallas_call(
        body, name="fox_fwd", grid=(N_HEADS, nq),
        in_specs=[q_spec, k_spec, v_spec, row_spec(1),
                  pl.BlockSpec((None, nq, 1, t), lambda h, i: (h, 0, 0, 0))],
        out_specs=[row_spec(HEAD_DIM), row_spec(1)],
        out_shape=[jax.ShapeDtypeStruct((N_HEADS, s, HEAD_DIM), F32),
                   jax.ShapeDtypeStruct((N_HEADS, s, 1), F32)],
        compiler_params=_params(("arbitrary", "arbitrary")),
    )(qkv, qkv, qkv, c_col, c_row)


def fox_bwd(qkv, c_col, c_row, do, o, lse):
    s = qkv.shape[2]
    t = ATT_T
    nq = s // t
    q_spec, k_spec, v_spec, row_spec = _att_specs(s, 3, 4, 5)
    any_spec = pl.BlockSpec(memory_space=pl.ANY)

    def body(q_ref, k_ref, v_ref, cc_ref, cr_ref, do_ref, o_ref, lse_ref,
             dq_ref, dk_hbm, dv_hbm, dc_ref, dk_acc, dv_acc):
        h = pl.program_id(0)
        i = pl.program_id(1)

        @pl.when(i == 0)
        def _():
            dk_acc[...] = jnp.zeros_like(dk_acc)
            dv_acc[...] = jnp.zeros_like(dv_acc)
            dc_ref[...] = jnp.zeros_like(dc_ref)

        qs = q_ref[...] * 0.125
        dob = do_ref[...]
        delta = jnp.sum(o_ref[...] * dob.astype(F32), axis=-1, keepdims=True)
        lse = lse_ref[...]
        cc = cc_ref[...]

        def tile(kb, dq, masked):
            k0 = pl.multiple_of(kb * t, t)
            k = k_ref[pl.ds(k0, t), :]
            sc = _dot_nt(qs, k) + (cc - cr_ref[kb])
            if masked:
                sc = jnp.where(_causal(False), sc, NEG)
            p = jnp.exp(sc - lse)
            ds = p * (_dot_nt(dob, v_ref[pl.ds(k0, t), :]) - delta)
            dsb = ds.astype(BF16)
            dk_acc[pl.ds(k0, t), :] += _dot_tn(dsb, qs)
            dv_acc[pl.ds(k0, t), :] += _dot_tn(p.astype(BF16), dob)
            dc_ref[kb] += -jnp.sum(ds, axis=0, keepdims=True)
            return dq + _dot(dsb, k)

        dq = lax.fori_loop(0, i, lambda kb, c: tile(kb, c, False), jnp.zeros((t, HEAD_DIM), F32))
        dq = tile(i, dq, True)
        dq_ref[...] = dq * 0.125

        @pl.when(i == nq - 1)
        def _():
            pltpu.sync_copy(dk_acc, dk_hbm.at[h])
            pltpu.sync_copy(dv_acc, dv_hbm.at[h])

    return pl.pallas_call(
        body, name="fox_bwd", grid=(N_HEADS, nq),
        in_specs=[q_spec, k_spec, v_spec, row_spec(1),
                  pl.BlockSpec((None, nq, 1, t), lambda h, i: (h, 0, 0, 0)),
                  row_spec(HEAD_DIM), row_spec(HEAD_DIM), row_spec(1)],
        out_specs=[row_spec(HEAD_DIM), any_spec, any_spec,
                   pl.BlockSpec((None, nq, 1, t), lambda h, i: (h, 0, 0, 0))],
        out_shape=[jax.ShapeDtypeStruct((N_HEADS, s, HEAD_DIM), F32),
                   jax.ShapeDtypeStruct((N_HEADS, s, HEAD_DIM), F32),
                   jax.ShapeDtypeStruct((N_HEADS, s, HEAD_DIM), F32),
                   jax.ShapeDtypeStruct((N_HEADS, nq, 1, t), F32)],
        scratch_shapes=[pltpu.VMEM((s, HEAD_DIM), F32), pltpu.VMEM((s, HEAD_DIM), F32)],
        compiler_params=_params(("arbitrary", "arbitrary")),
    )(qkv, qkv, qkv, c_col, c_row, do, o, lse)


def _sb_tile(qs, k, run, masked):
    z = _dot_nt(qs, k)
    sp = jnp.log(1.0 + jnp.exp(-jnp.abs(z)))
    ls = jnp.minimum(z, 0.0) - sp
    lm = -jnp.maximum(z, 0.0) - sp
    if masked:
        valid = _causal(True)
        lm = jnp.where(valid, lm, 0.0)
    row = lax.broadcasted_iota(jnp.int32, (ATT_T, ATT_T), 0)
    col = lax.broadcasted_iota(jnp.int32, (ATT_T, ATT_T), 1)
    later = (row > col).astype(BF16)
    hi, lo = _split2(lm)
    between = run + _dot(hi, later) + _dot(lo, later)
    a = jnp.exp(ls + between)
    if masked:
        a = jnp.where(valid, a, 0.0)
    return ls, lm, a


def sb_fwd(qkv):
    s = qkv.shape[2]
    t = ATT_T
    nq = s // t
    q_spec, k_spec, v_spec, row_spec = _att_specs(s, 6, 7, 8)

    def body(q_ref, k_ref, v_ref, o_ref):
        i = pl.program_id(1)
        qs = q_ref[...] * 0.125

        def tile(kb, carry, masked):
            run, acc = carry
            k0 = pl.multiple_of(kb * t, t)
            _, lm, a = _sb_tile(qs, k_ref[pl.ds(k0, t), :], run, masked)
            acc = acc + _dot(a.astype(BF16), v_ref[pl.ds(k0, t), :])
            return run + jnp.sum(lm, axis=-1, keepdims=True), acc

        carry = tile(i, (jnp.zeros((t, 1), F32), jnp.zeros((t, HEAD_DIM), F32)), True)
        _, acc = lax.fori_loop(0, i, lambda n, c: tile(i - 1 - n, c, False), carry)
        o_ref[...] = acc

    return pl.pallas_call(
        body, name="sb_fwd", grid=(N_HEADS, nq),
        in_specs=[q_spec, k_spec, v_spec],
        out_specs=row_spec(HEAD_DIM),
        out_shape=jax.ShapeDtypeStruct((N_HEADS, s, HEAD_DIM), F32),
        compiler_params=_params(("arbitrary", "arbitrary")),
    )(qkv, qkv, qkv)


def sb_bwd(qkv, do, o):
    s = qkv.shape[2]
    t = ATT_T
    nq = s // t
    q_spec, k_spec, v_spec, row_spec = _att_specs(s, 6, 7, 8)
    any_spec = pl.BlockSpec(memory_space=pl.ANY)

    def body(q_ref, k_ref, v_ref, do_ref, o_ref, dq_ref, dk_hbm, dv_hbm, dk_acc, dv_acc):
        h = pl.program_id(0)
        i = pl.program_id(1)

        @pl.when(i == 0)
        def _():
            dk_acc[...] = jnp.zeros_like(dk_acc)
            dv_acc[...] = jnp.zeros_like(dv_acc)

        qs = q_ref[...] * 0.125
        dob = do_ref[...]
        tot = jnp.sum(o_ref[...] * dob.astype(F32), axis=-1, keepdims=True)

        def tile(kb, carry, masked):
            run, run_g, dq = carry
            k0 = pl.multiple_of(kb * t, t)
            k = k_ref[pl.ds(k0, t), :]
            ls, lm, a = _sb_tile(qs, k, run, masked)
            ab = a.astype(BF16)
            g = ab.astype(F32) * _dot_nt(dob, v_ref[pl.ds(k0, t), :])
            row = lax.broadcasted_iota(jnp.int32, (t, t), 0)
            col = lax.broadcasted_iota(jnp.int32, (t, t), 1)
            from_here = (row >= col).astype(BF16)
            hi, lo = _split2(g)
            g_right = run_g + _dot(hi, from_here) + _dot(lo, from_here)
            g_left = tot - g_right
            dz = g - jnp.exp(ls) * (g + g_left)
            if masked:
                dz = jnp.where(_causal(True), dz, 0.0)
            dzb = dz.astype(BF16)
            dk_acc[pl.ds(k0, t), :] += _dot_tn(dzb, qs)
            dv_acc[pl.ds(k0, t), :] += _dot_tn(ab, dob)
            return (run + jnp.sum(lm, axis=-1, keepdims=True),
                    run_g + jnp.sum(g, axis=-1, keepdims=True),
                    dq + _dot(dzb, k))

        zero = jnp.zeros((t, 1), F32)
        carry = tile(i, (zero, zero, jnp.zeros((t, HEAD_DIM), F32)), True)
        _, _, dq = lax.fori_loop(0, i, lambda n, c: tile(i - 1 - n, c, False), carry)
        dq_ref[...] = dq * 0.125

        @pl.when(i == nq - 1)
        def _():
            pltpu.sync_copy(dk_acc, dk_hbm.at[h])
            pltpu.sync_copy(dv_acc, dv_hbm.at[h])

    return pl.pallas_call(
        body, name="sb_bwd", grid=(N_HEADS, nq),
        in_specs=[q_spec, k_spec, v_spec, row_spec(HEAD_DIM), row_spec(HEAD_DIM)],
        out_specs=[row_spec(HEAD_DIM), any_spec, any_spec],
        out_shape=[jax.ShapeDtypeStruct((N_HEADS, s, HEAD_DIM), F32)] * 3,
        scratch_shapes=[pltpu.VMEM((s, HEAD_DIM), F32), pltpu.VMEM((s, HEAD_DIM), F32)],
        compiler_params=_params(("arbitrary", "arbitrary")),
    )(qkv, qkv, qkv, do, o)


def _branch_inputs(refs, br):
    ya_ref, yb_ref, yc_ref, yd_ref = refs
    if br == 1:
        return yb_ref[...]
    return _heads_to_lanes((ya_ref, None, yc_ref, yd_ref)[br])


def outproj_fwd(x, ya, yb, yc, yd, gates, bg, wout):
    s = x.shape[0]
    tm = min(ROW_T, s)

    def body(x_ref, ya_ref, yb_ref, yc_ref, yd_ref, gates_ref, bg_ref, w_ref, out_ref):
        pieces = []
        for br in range(4):
            cols = slice(br * D_BRANCH, (br + 1) * D_BRANCH)
            y = _branch_inputs((ya_ref, yb_ref, yc_ref, yd_ref), br)
            r = lax.rsqrt(jnp.mean(y * y, axis=-1, keepdims=True) + EPS)
            gt = gates_ref[:, cols]
            pieces.append((y * r * bg_ref[:, cols]) * (gt * _sigmoid(gt)))
        merged = jnp.concatenate(pieces, axis=1).astype(BF16)
        out_ref[...] = x_ref[...] + _dot(merged, w_ref[...])

    head_spec = pl.BlockSpec((N_HEADS, tm, HEAD_DIM), lambda i: (0, i, 0))
    return pl.pallas_call(
        body, name="outproj_fwd", grid=(s // tm,),
        in_specs=[pl.BlockSpec((tm, D_MODEL), lambda i: (i, 0)),
                  head_spec, pl.BlockSpec((tm, D_BRANCH), lambda i: (i, 0)), head_spec, head_spec,
                  pl.BlockSpec((tm, D_MODEL), lambda i: (i, 0)),
                  pl.BlockSpec((1, D_MODEL), lambda i: (0, 0)),
                  pl.BlockSpec((D_MODEL, D_MODEL), lambda i: (0, 0))],
        out_specs=pl.BlockSpec((tm, D_MODEL), lambda i: (i, 0)),
        out_shape=jax.ShapeDtypeStruct((s, D_MODEL), F32),
        compiler_params=_params(("arbitrary",)),
    )(x, ya, yb, yc, yd, gates, bg, wout)


def outproj_bwd(dout, ya, yb, yc, yd, gates, bg, wout):
    s = dout.shape[0]
    tm = min(ROW_T, s)

    def body(dout_ref, ya_ref, yb_ref, yc_ref, yd_ref, gates_ref, bg_ref, w_ref,
             dya_ref, dyb_ref, dyc_ref, dyd_ref, dgates_ref, dbg_ref, dw_ref):
        i = pl.program_id(0)

        @pl.when(i == 0)
        def _():
            dbg_ref[...] = jnp.zeros_like(dbg_ref)
            dw_ref[...] = jnp.zeros_like(dw_ref)

        doutb = dout_ref[...].astype(BF16)
        dmerged = _dot_nt(doutb, w_ref[...])
        pieces = []
        for br in range(4):
            cols = slice(br * D_BRANCH, (br + 1) * D_BRANCH)
            y = _branch_inputs((ya_ref, yb_ref, yc_ref, yd_ref), br)
            r = lax.rsqrt(jnp.mean(y * y, axis=-1, keepdims=True) + EPS)
            yn = y * r
            bgv = bg_ref[:, cols]
            gt = gates_ref[:, cols]
            sig = _sigmoid(gt)
            act = gt * sig
            n = yn * bgv
            pieces.append(n * act)
            dm = dmerged[:, cols]
            dn = dm * act
            dgates_ref[:, cols] = (dm * n * (sig * (1.0 + gt * (1.0 - sig)))).astype(BF16)
            dbg_ref[:, cols] += jnp.sum(dn * yn, axis=0, keepdims=True)
            u = dn * bgv
            dy = r * (u - yn * jnp.mean(yn * u, axis=-1, keepdims=True))
            if br == 1:
                dyb_ref[...] = dy
            else:
                dref = (dya_ref, None, dyc_ref, dyd_ref)[br]
                for hh in range(N_HEADS):
                    dref[hh] = dy[:, hh * HEAD_DIM:(hh + 1) * HEAD_DIM].astype(BF16)
        merged = jnp.concatenate(pieces, axis=1).astype(BF16)
        dw_ref[...] += _dot_tn(merged, doutb)

    head_spec = pl.BlockSpec((N_HEADS, tm, HEAD_DIM), lambda i: (0, i, 0))
    head_shape = jax.ShapeDtypeStruct((N_HEADS, s, HEAD_DIM), BF16)
    return pl.pallas_call(
        body, name="outproj_bwd", grid=(s // tm,),
        in_specs=[pl.BlockSpec((tm, D_MODEL), lambda i: (i, 0)),
                  head_spec, pl.BlockSpec((tm, D_BRANCH), lambda i: (i, 0)), head_spec, head_spec,
                  pl.BlockSpec((tm, D_MODEL), lambda i: (i, 0)),
                  pl.BlockSpec((1, D_MODEL), lambda i: (0, 0)),
                  pl.BlockSpec((D_MODEL, D_MODEL), lambda i: (0, 0))],
        out_specs=[head_spec, pl.BlockSpec((tm, D_BRANCH), lambda i: (i, 0)), head_spec, head_spec,
                   pl.BlockSpec((tm, D_MODEL), lambda i: (i, 0)),
                   pl.BlockSpec((1, D_MODEL), lambda i: (0, 0)),
                   pl.BlockSpec((D_MODEL, D_MODEL), lambda i: (0, 0))],
        out_shape=[head_shape, jax.ShapeDtypeStruct((s, D_BRANCH), F32), head_shape, head_shape,
                   jax.ShapeDtypeStruct((s, D_MODEL), BF16),
                   jax.ShapeDtypeStruct((1, D_MODEL), F32),
                   jax.ShapeDtypeStruct((D_MODEL, D_MODEL), F32)],
        compiler_params=_params(("arbitrary",)),
    )(dout, ya, yb, yc, yd, gates, bg, wout)


def final_loss(x, tgt, g):
    s = x.shape[0]
    tm = min(ROW_T, s)

    def body(x_ref, t_ref, g_ref, loss_ref, dx_ref, dg_ref):
        i = pl.program_id(0)

        @pl.when(i == 0)
        def _():
            loss_ref[...] = jnp.zeros_like(loss_ref)
            dg_ref[...] = jnp.zeros_like(dg_ref)

        xv = x_ref[...]
        gv = g_ref[...]
        r = lax.rsqrt(jnp.mean(xv * xv, axis=-1, keepdims=True) + EPS)
        xn = xv * r
        err = xn * gv - t_ref[...]
        loss_ref[...] += jnp.sum(err * err) * (0.5 / D_MODEL)
        dy = err * (1.0 / D_MODEL)
        u = dy * gv
        dx_ref[...] = r * (u - xn * jnp.mean(xn * u, axis=-1, keepdims=True))
        dg_ref[...] += jnp.sum(dy * xn, axis=0, keepdims=True)

    return pl.pallas_call(
        body, name="final_loss", grid=(s // tm,),
        in_specs=[pl.BlockSpec((tm, D_MODEL), lambda i: (i, 0)),
                  pl.BlockSpec((tm, D_MODEL), lambda i: (i, 0)),
                  pl.BlockSpec((1, D_MODEL), lambda i: (0, 0))],
        out_specs=[pl.BlockSpec((1, 128), lambda i: (0, 0)),
                   pl.BlockSpec((tm, D_MODEL), lambda i: (i, 0)),
                   pl.BlockSpec((1, D_MODEL), lambda i: (0, 0))],
        out_shape=[jax.ShapeDtypeStruct((1, 128), F32),
                   jax.ShapeDtypeStruct((s, D_MODEL), F32),
                   jax.ShapeDtypeStruct((1, D_MODEL), F32)],
        compiler_params=_params(("arbitrary",)),
    )(x, tgt, g)


def _rel_index():
    i = np.arange(A_TQ)[:, None]
    j = np.arange(A_BAND)[None, :]
    rel = np.clip(i - j + (A_BAND - A_TQ), -MAX_REL, MAX_REL) + MAX_REL
    dchunk = i // CHUNK + LOOKBACK - j // CHUNK
    valid = (dchunk >= 0) & (dchunk <= LOOKBACK)
    return jnp.asarray(np.where(valid, rel, -1).astype(np.int32))


def _layer_consts(p):
    tbias = relbias_tile(p["rel_bias"], _rel_index())
    return dict(
        norm_g=p["norm_g"].reshape(1, D_MODEL),
        v_gain=p["v_gain"].reshape(1, D_BRANCH),
        b_col=p["b_s"].reshape(N_HEADS, SG_CHUNK, 1),
        bg=p["branch_gain"].reshape(1, D_MODEL),
        tbias=tbias,
    )


def _gate_layout(fp, b_f, s):
    nb = s // 128
    ft = fp[:, :N_HEADS].T.reshape(N_HEADS * nb, 128)
    bcol = jnp.repeat(b_f, nb).reshape(N_HEADS * nb, 1)
    return ft, bcol


def layer_fwd(x, p):
    s = x.shape[0]
    c = _layer_consts(p)
    h, qkv, gates, uv, fp = inproj_fwd(x, c["norm_g"], p["wp"])
    keep = A_BAND - A_TQ
    kpad = jnp.pad(qkv[1], ((0, 0), (keep, 0), (0, 0)))
    vpad = jnp.pad(qkv[2], ((0, 0), (keep, 0), (0, 0)))
    ya, lse_a = mix_a_fwd(qkv, kpad, vpad, c["tbias"])
    yb = mix_b_fwd(uv, c["v_gain"], p["w_s"], c["b_col"])
    ft, bcol = _gate_layout(fp, p["b_f"], s)
    cum = fox_gate_fwd(ft, bcol).reshape(N_HEADS, s)
    c_col = cum.reshape(N_HEADS, s, 1)
    c_row = cum.reshape(N_HEADS, s // ATT_T, 1, ATT_T)
    yc, lse_c = fox_fwd(qkv, c_col, c_row)
    yd = sb_fwd(qkv)
    out = outproj_fwd(x, ya, yb, yc, yd, gates, c["bg"], p["wout"])
    saved = dict(consts=c, x=x, h=h, qkv=qkv, gates=gates, uv=uv, kpad=kpad, vpad=vpad, ft=ft, bcol=bcol,
                 c_col=c_col, c_row=c_row, ya=ya, lse_a=lse_a, yb=yb, yc=yc, lse_c=lse_c, yd=yd)
    return out, saved


def layer_bwd(dout, p, sv):
    s = dout.shape[0]
    c = sv["consts"]
    dya, dyb, dyc, dyd, dgates, dbg, dwout = outproj_bwd(
        dout, sv["ya"], sv["yb"], sv["yc"], sv["yd"], sv["gates"], c["bg"], p["wout"])
    keep = A_BAND - A_TQ
    dqa, dkpad, dvpad, dt = mix_a_bwd(sv["qkv"], sv["kpad"], sv["vpad"], c["tbias"], dya, sv["ya"], sv["lse_a"])
    dka, dva = dkpad[:, keep:], dvpad[:, keep:]
    drel = relbias_grad(dt, _rel_index())[:N_HEADS, :2 * MAX_REL + 1]
    duv, dws, dbs, dvgain = mix_b_bwd(sv["uv"], c["v_gain"], p["w_s"], c["b_col"], dyb)
    dqc, dkc, dvc, dc = fox_bwd(sv["qkv"], sv["c_col"], sv["c_row"], dyc, sv["yc"], sv["lse_c"])
    dft, dbf = fox_gate_bwd(sv["ft"], sv["bcol"], dc.reshape(N_HEADS * (s // 128), 128))
    dfp = jnp.pad(dft.reshape(N_HEADS, s).T, ((0, 0), (0, 128 - N_HEADS)))
    dqd, dkd, dvd = sb_bwd(sv["qkv"], dyd, sv["yd"])
    dp, dx, dnorm = inproj_bwd((dqa, dka, dva, dqc, dkc, dvc, dqd, dkd, dvd), dgates, duv, dfp,
                               p["wp"], sv["x"], c["norm_g"], dout)
    dwp = weight_grad(sv["h"], dp, "inproj_wgrad")
    grads = dict(norm_g=dnorm.reshape(D_MODEL), wp=dwp, b_f=dbf[:N_HEADS, 0], rel_bias=drel,
                 w_s=dws, b_s=dbs.reshape(N_HEADS, SG_CHUNK), v_gain=dvgain.reshape(D_BRANCH),
                 branch_gain=dbg.reshape(4, D_BRANCH), wout=dwout)
    return dx, grads


def local_step(x, tgt, layers, final_g):
    saved = []
    cur = x
    for p in layers:
        cur, sv = layer_fwd(cur, p)
        saved.append(sv)
    loss, dcur, dfinal = final_loss(cur, tgt, final_g.reshape(1, D_MODEL))
    grads = [None] * len(layers)
    for l in reversed(range(len(layers))):
        dcur, grads[l] = layer_bwd(dcur, layers[l], saved[l])
    return loss[0, 0], dcur, grads, dfinal.reshape(D_MODEL)


def gather_weights(wb, wf):
    def body(wb_ref, wf_ref, ob_ref, of_ref, send_sems, recv_sems, loc_sems):
        x, y, c = lax.axis_index("x"), lax.axis_index("y"), lax.axis_index("c")
        me = 2 * x + y
        chips = [(1 - x, y), (x, 1 - y), (1 - x, 1 - y)]
        pairs = [(wb_ref, ob_ref), (wf_ref, of_ref)]
        local = [pltpu.make_async_copy(src, dst.at[me], loc_sems.at[n]) for n, (src, dst) in enumerate(pairs)]
        for cp in local:
            cp.start()

        def copy(j, n, slot):
            src, dst = pairs[n]
            return pltpu.make_async_remote_copy(
                src_ref=src, dst_ref=dst.at[slot], send_sem=send_sems.at[2 * j + n], recv_sem=recv_sems.at[2 * j + n],
                device_id=(chips[j][0], chips[j][1], c), device_id_type=MESH)

        sends = [copy(j, n, me) for j in range(3) for n in range(2)]
        for cp in sends:
            cp.start()
        for j in range(3):
            for n in range(2):
                copy(j, n, 2 * chips[j][0] + chips[j][1]).wait_recv()
        for cp in sends:
            cp.wait_send()
        for cp in local:
            cp.wait()

    any_spec = pl.BlockSpec(memory_space=pl.ANY)
    return pl.pallas_call(
        body, name="gather_weights",
        in_specs=[any_spec, any_spec], out_specs=[any_spec, any_spec],
        out_shape=[jax.ShapeDtypeStruct((4,) + wb.shape, wb.dtype), jax.ShapeDtypeStruct((4,) + wf.shape, wf.dtype)],
        scratch_shapes=[pltpu.SemaphoreType.DMA((6,)), pltpu.SemaphoreType.DMA((6,)), pltpu.SemaphoreType.DMA((2,))],
    )(wb, wf)


def exchange_grads(send):
    def body(s_ref, r_ref, send_sems, recv_sems, loc_sem):
        x, y, c = lax.axis_index("x"), lax.axis_index("y"), lax.axis_index("c")
        me_chip = 2 * x + y
        me = 4 * x + 2 * y + c
        peers = [(x, y, 1 - c)]
        for px, py in [(1 - x, y), (x, 1 - y), (1 - x, 1 - y)]:
            peers += [(px, py, c), (px, py, 1 - c)]
        local = pltpu.make_async_copy(s_ref.at[me_chip], r_ref.at[me], loc_sem)
        local.start()

        def copy(n, chip, slot):
            return pltpu.make_async_remote_copy(
                src_ref=s_ref.at[chip], dst_ref=r_ref.at[slot], send_sem=send_sems.at[n], recv_sem=recv_sems.at[n],
                device_id=peers[n], device_id_type=MESH)

        sends = [copy(n, 2 * px + py, me) for n, (px, py, _) in enumerate(peers)]
        for cp in sends:
            cp.start()
        for n, (px, py, pc) in enumerate(peers):
            copy(n, me_chip, 4 * px + 2 * py + pc).wait_recv()
        for cp in sends:
            cp.wait_send()
        local.wait()

    any_spec = pl.BlockSpec(memory_space=pl.ANY)
    return pl.pallas_call(
        body, name="exchange_grads",
        in_specs=[any_spec], out_specs=any_spec,
        out_shape=jax.ShapeDtypeStruct((8,) + send.shape[1:], send.dtype),
        scratch_shapes=[pltpu.SemaphoreType.DMA((7,)), pltpu.SemaphoreType.DMA((7,)), pltpu.SemaphoreType.DMA],
    )(send)


def adamw_reduce(parts, w, m, v):
    rows = w.shape[0]
    tr = 512
    c1 = 1.0 - ADAM_B1 ** ADAM_STEP
    c2 = 1.0 - ADAM_B2 ** ADAM_STEP

    def body(p_ref, w_ref, m_ref, v_ref, g_ref, d_ref, nm_ref, nv_ref):
        g = p_ref[0]
        for n in range(1, 8):
            g = g + p_ref[n]
        g_ref[...] = g
        nm = ADAM_B1 * m_ref[...] + (1.0 - ADAM_B1) * g
        nv = ADAM_B2 * v_ref[...] + (1.0 - ADAM_B2) * (g * g)
        nm_ref[...] = nm
        nv_ref[...] = nv
        d_ref[...] = -ADAM_LR * ((nm / c1) / (jnp.sqrt(nv / c2) + ADAM_EPS) + ADAM_WD * w_ref[...])

    spec = pl.BlockSpec((tr, 128), lambda i: (i, 0))
    shape = jax.ShapeDtypeStruct((rows, 128), F32)
    return pl.pallas_call(
        body, name="adamw_reduce", grid=(rows // tr,),
        in_specs=[pl.BlockSpec((8, tr, 128), lambda i: (0, i, 0)), spec, spec, spec],
        out_specs=[spec] * 4, out_shape=[shape] * 4,
        compiler_params=_params(("arbitrary",)),
    )(parts, w, m, v)


SHARDED = ("w_in", "w_out", "branch_gain")
SMALL = ("norm_g", "b_f", "rel_bias", "w_s", "b_s", "v_gain", "final_g")
WEIGHTS = ("norm_g", "w_in", "b_f", "rel_bias", "w_s", "b_s", "v_gain", "branch_gain", "w_out", "final_g")
PACK_ORDER = SHARDED + SMALL
PACK_ROW_TILE = 512


def _rows_of(shape):
    return -(-int(np.prod(shape)) // 128)


def _pack(leaves):
    parts = []
    for a in leaves:
        flat = a.reshape(-1).astype(F32)
        parts.append(jnp.pad(flat, (0, _rows_of(a.shape) * 128 - flat.shape[0])))
    flat = jnp.concatenate(parts)
    rows = flat.shape[0] // 128
    total = -(-rows // PACK_ROW_TILE) * PACK_ROW_TILE
    return jnp.pad(flat, (0, (total - rows) * 128)).reshape(total, 128)


def _unpack(slab, shapes):
    out, row = [], 0
    for shp in shapes:
        n = int(np.prod(shp))
        r = _rows_of(shp)
        out.append(slab[row:row + r].reshape(-1)[:n].reshape(shp))
        row += r
    return out


def _pack_w_in(w):
    return jnp.concatenate([w[:, :2816], w[:, 2820:], w[:, 2816:2820],
                            jnp.zeros((w.shape[0], N_PACK - N_IN), w.dtype)], axis=1)


def _unpack_w_in(wp):
    return jnp.concatenate([wp[:, :2816], wp[:, F_COL:F_COL + N_HEADS], wp[:, 2816:F_COL]], axis=1)


def kernel(x, norm_g, w_in, b_f, rel_bias, w_s, b_s, v_gain, branch_gain, w_out, final_g, loss_target, m_norm_g, m_w_in, m_b_f, m_rel_bias, m_w_s, m_b_s, m_v_gain, m_branch_gain, m_w_out, m_final_g, v_norm_g, v_w_in, v_b_f, v_rel_bias, v_w_s, v_b_s, v_v_gain, v_branch_gain, v_w_out, v_final_g):
    depth = norm_g.shape[0]
    weights = dict(norm_g=norm_g, w_in=w_in, b_f=b_f, rel_bias=rel_bias, w_s=w_s, b_s=b_s, v_gain=v_gain,
                   branch_gain=branch_gain, w_out=w_out, final_g=final_g)
    mom1 = dict(norm_g=m_norm_g, w_in=m_w_in, b_f=m_b_f, rel_bias=m_rel_bias, w_s=m_w_s, b_s=m_b_s,
                v_gain=m_v_gain, branch_gain=m_branch_gain, w_out=m_w_out, final_g=m_final_g)
    mom2 = dict(norm_g=v_norm_g, w_in=v_w_in, b_f=v_b_f, rel_bias=v_rel_bias, w_s=v_w_s, b_s=v_b_s,
                v_gain=v_v_gain, branch_gain=v_branch_gain, w_out=v_w_out, final_g=v_final_g)

    n_in_rows = _rows_of(w_in.shape)
    n_out_rows = _rows_of(w_out.shape)
    wb = jnp.concatenate([w_in.astype(BF16).reshape(n_in_rows, 128), w_out.astype(BF16).reshape(n_out_rows, 128)])
    wf = jnp.pad(branch_gain.reshape(-1), (0, 8 * 128 - branch_gain.size)).reshape(8, 128)
    gb, gf = gather_weights(wb, wf)
    w_in_full = gb[:, :n_in_rows].reshape((4,) + w_in.shape)
    w_in_full = jnp.moveaxis(w_in_full, 0, 2).reshape(depth, D_MODEL, N_IN)
    w_out_full = gb[:, n_in_rows:].reshape((4,) + w_out.shape)
    w_out_full = jnp.moveaxis(w_out_full, 0, 1).reshape(depth, D_MODEL, D_MODEL)
    bg_full = gf.reshape(4, -1)[:, :branch_gain.size].reshape((4,) + branch_gain.shape)
    bg_full = jnp.moveaxis(bg_full, 0, 2).reshape(depth, 4, D_BRANCH)

    layers = [dict(norm_g=norm_g[l], wp=_pack_w_in(w_in_full[l]), b_f=b_f[l], rel_bias=rel_bias[l], w_s=w_s[l],
                   b_s=b_s[l], v_gain=v_gain[l], branch_gain=bg_full[l], wout=w_out_full[l]) for l in range(depth)]

    loss_part, grad_x, lgrads, dfinal = local_step(x[0], loss_target[0], layers, final_g)
    loss = lax.psum(loss_part, ("x", "y", "c"))

    stack = lambda k: jnp.stack([g[k] for g in lgrads])
    d_w_in = jnp.stack([_unpack_w_in(g["wp"]) for g in lgrads])
    d_w_out = stack("wout")
    d_bg = stack("branch_gain")
    small = dict(norm_g=stack("norm_g"), b_f=stack("b_f"), rel_bias=stack("rel_bias"), w_s=stack("w_s"),
                 b_s=stack("b_s"), v_gain=stack("v_gain"), final_g=dfinal)
    slabs = []
    for sidx in range(4):
        leaves = [d_w_in[:, :, sidx * N_SHARD:(sidx + 1) * N_SHARD],
                  d_w_out[:, sidx * D_BRANCH:(sidx + 1) * D_BRANCH, :],
                  d_bg[:, :, sidx * HEAD_DIM:(sidx + 1) * HEAD_DIM]] + [small[k] for k in SMALL]
        slabs.append(_pack(leaves))
    parts = exchange_grads(jnp.stack(slabs))

    pack_local = lambda d: _pack([d[k] for k in PACK_ORDER])
    g_slab, d_slab, m_slab, v_slab = adamw_reduce(parts, pack_local(weights), pack_local(mom1), pack_local(mom2))
    shapes = [weights[k].shape for k in PACK_ORDER]
    outs = {}
    for tag, slab in (("grad", g_slab), ("delta", d_slab), ("new_m", m_slab), ("new_v", v_slab)):
        for k, a in zip(PACK_ORDER, _unpack(slab, shapes)):
            outs[tag, k] = a
    result = [loss, grad_x[None]]
    for tag in ("grad", "delta", "new_m", "new_v"):
        result += [outs[tag, k] for k in WEIGHTS]
    return tuple(result)
```

```python
import functools

import jax
import jax.numpy as jnp
import numpy as np
from jax import lax
from jax.experimental import pallas as pl
from jax.experimental.pallas import tpu as pltpu

F32 = jnp.float32
BF16 = jnp.bfloat16
MESH = pl.DeviceIdType.MESH

D_MODEL = 1024
D_BRANCH = 256
N_HEADS = 4
HEAD_DIM = 64
CHUNK = 64
LOOKBACK = 8
MAX_REL = 128
SG_CHUNK = 128
EPS = 1e-6
N_IN = 3844
N_PACK = 3968
F_COL = 3840
N_SHARD = 961
NEG = -1e30

A_TQ = 128
A_NKB = 5
A_BAND = A_TQ * A_NKB
ATT_T = 256
FOX_WIDE = 4
SB_DEAD = -110.0
ROW_T = 512
VMEM_LIMIT = 56 * 1024 * 1024

ADAM_LR = 0.001
ADAM_B1 = 0.9
ADAM_B2 = 0.999
ADAM_EPS = 1e-08
ADAM_WD = 0.01
ADAM_STEP = 10

SEC_A_Q, SEC_A_K, SEC_A_V, SEC_A_G = 0, 256, 512, 768
SEC_B_U, SEC_B_V, SEC_B_G = 1024, 1280, 1536
SEC_C_Q, SEC_C_K, SEC_C_V, SEC_C_G = 1792, 2048, 2304, 2560
SEC_D_Q, SEC_D_K, SEC_D_V, SEC_D_G = 2816, 3072, 3328, 3584
QKV_SECS = (SEC_A_Q, SEC_A_K, SEC_A_V, SEC_C_Q, SEC_C_K, SEC_C_V, SEC_D_Q, SEC_D_K, SEC_D_V)
GATE_SECS = (SEC_A_G, SEC_B_G, SEC_C_G, SEC_D_G)


def _dot(a, b):
    return jnp.dot(a, b, preferred_element_type=F32)


def _dot_nt(a, b):
    return lax.dot_general(a, b, (((1,), (1,)), ((), ())), preferred_element_type=F32)


def _dot_tn(a, b):
    return lax.dot_general(a, b, (((0,), (0,)), ((), ())), preferred_element_type=F32)


def _split2(x):
    hi = x.astype(BF16)
    lo = (x - hi.astype(F32)).astype(BF16)
    return hi, lo


def _split3(x):
    hi = x.astype(BF16)
    r = x - hi.astype(F32)
    mid = r.astype(BF16)
    lo = (r - mid.astype(F32)).astype(BF16)
    return hi, mid, lo


def _sigmoid(x):
    return 1.0 / (1.0 + jnp.exp(-x))


def _params(sem=None, vmem=VMEM_LIMIT):
    return pltpu.CompilerParams(dimension_semantics=sem, vmem_limit_bytes=vmem)


def _heads_to_lanes(ref):
    return jnp.concatenate([ref[h] for h in range(N_HEADS)], axis=1)


def inproj_fwd(x, g, wp):
    s = x.shape[0]
    tm = min(ROW_T, s)

    def body(x_ref, g_ref, w_ref, h_ref, qkv_ref, gates_ref, uv_ref, f_ref):
        xv = x_ref[...]
        r = lax.rsqrt(jnp.mean(xv * xv, axis=-1, keepdims=True) + EPS)
        h = (xv * r * g_ref[...]).astype(BF16)
        h_ref[...] = h
        for n, off in enumerate(QKV_SECS):
            p = _dot(h, w_ref[:, off:off + D_BRANCH])
            for hh in range(N_HEADS):
                qkv_ref[n, hh] = p[:, hh * HEAD_DIM:(hh + 1) * HEAD_DIM].astype(BF16)
        for n, off in enumerate(GATE_SECS):
            gates_ref[:, n * D_BRANCH:(n + 1) * D_BRANCH] = _dot(h, w_ref[:, off:off + D_BRANCH])
        uv_ref[...] = _dot(h, w_ref[:, SEC_B_U:SEC_B_U + 2 * D_BRANCH])
        f_ref[...] = _dot(h, w_ref[:, F_COL:F_COL + 128])

    return pl.pallas_call(
        body, name="inproj_fwd", grid=(s // tm,),
        in_specs=[pl.BlockSpec((tm, D_MODEL), lambda i: (i, 0)),
                  pl.BlockSpec((1, D_MODEL), lambda i: (0, 0)),
                  pl.BlockSpec((D_MODEL, N_PACK), lambda i: (0, 0))],
        out_specs=[pl.BlockSpec((tm, D_MODEL), lambda i: (i, 0)),
                   pl.BlockSpec((9, N_HEADS, tm, HEAD_DIM), lambda i: (0, 0, i, 0)),
                   pl.BlockSpec((tm, D_MODEL), lambda i: (i, 0)),
                   pl.BlockSpec((tm, 2 * D_BRANCH), lambda i: (i, 0)),
                   pl.BlockSpec((tm, 128), lambda i: (i, 0))],
        out_shape=[jax.ShapeDtypeStruct((s, D_MODEL), BF16),
                   jax.ShapeDtypeStruct((9, N_HEADS, s, HEAD_DIM), BF16),
                   jax.ShapeDtypeStruct((s, D_MODEL), F32),
                   jax.ShapeDtypeStruct((s, 2 * D_BRANCH), F32),
                   jax.ShapeDtypeStruct((s, 128), F32)],
        compiler_params=_params(("arbitrary",)),
    )(x, g, wp)


def inproj_bwd(dqkv, dgates, duv, dfp, wp, x, g, dres):
    s = x.shape[0]
    tm = min(ROW_T, s)

    def body(*refs):
        dq_refs = refs[:9]
        dgates_ref, duv_ref, dfp_ref, w_ref, x_ref, g_ref, dres_ref, dp_ref, dx_ref, dg_ref = refs[9:]
        i = pl.program_id(0)
        a_q, a_k, a_v, c_q, c_k, c_v, d_q, d_k, d_v = [_heads_to_lanes(r).astype(BF16) for r in dq_refs]
        dgt = dgates_ref[...]
        duv_b = duv_ref[...].astype(BF16)
        dp = jnp.concatenate(
            [a_q, a_k, a_v, dgt[:, 0:256], duv_b, dgt[:, 256:512], c_q, c_k, c_v, dgt[:, 512:768],
             d_q, d_k, d_v, dgt[:, 768:1024], dfp_ref[...].astype(BF16)], axis=1)
        dp_ref[...] = dp
        dh = _dot_nt(dp, w_ref[...])
        xv = x_ref[...]
        r = lax.rsqrt(jnp.mean(xv * xv, axis=-1, keepdims=True) + EPS)
        xn = xv * r
        u = dh * g_ref[...]
        dx_ref[...] = dres_ref[...] + r * (u - xn * jnp.mean(xn * u, axis=-1, keepdims=True))

        @pl.when(i == 0)
        def _():
            dg_ref[...] = jnp.zeros_like(dg_ref)

        dg_ref[...] += jnp.sum(dh * xn, axis=0, keepdims=True)

    head_spec = pl.BlockSpec((N_HEADS, tm, HEAD_DIM), lambda i: (0, i, 0))
    return pl.pallas_call(
        body, name="inproj_bwd", grid=(s // tm,),
        in_specs=[head_spec] * 9 + [
            pl.BlockSpec((tm, D_MODEL), lambda i: (i, 0)),
            pl.BlockSpec((tm, 2 * D_BRANCH), lambda i: (i, 0)),
            pl.BlockSpec((tm, 128), lambda i: (i, 0)),
            pl.BlockSpec((D_MODEL, N_PACK), lambda i: (0, 0)),
            pl.BlockSpec((tm, D_MODEL), lambda i: (i, 0)),
            pl.BlockSpec((1, D_MODEL), lambda i: (0, 0)),
            pl.BlockSpec((tm, D_MODEL), lambda i: (i, 0))],
        out_specs=[pl.BlockSpec((tm, N_PACK), lambda i: (i, 0)),
                   pl.BlockSpec((tm, D_MODEL), lambda i: (i, 0)),
                   pl.BlockSpec((1, D_MODEL), lambda i: (0, 0))],
        out_shape=[jax.ShapeDtypeStruct((s, N_PACK), BF16),
                   jax.ShapeDtypeStruct((s, D_MODEL), F32),
                   jax.ShapeDtypeStruct((1, D_MODEL), F32)],
        compiler_params=_params(("arbitrary",)),
    )(*dqkv, dgates, duv, dfp, wp, x, g, dres)


def weight_grad(a, b, name):
    s, m = a.shape
    n = b.shape[1]
    tm = min(ROW_T, s)
    tmm = 256
    nsteps = s // tm

    def body(a_ref, b_ref, o_ref):
        k = pl.program_id(1)

        @pl.when(k == 0)
        def _():
            o_ref[...] = jnp.zeros_like(o_ref)

        o_ref[...] += _dot_tn(a_ref[...], b_ref[...])

    return pl.pallas_call(
        body, name=name, grid=(m // tmm, nsteps),
        in_specs=[pl.BlockSpec((tm, tmm), lambda j, k: (k, j)),
                  pl.BlockSpec((tm, n), lambda j, k: (k, 0))],
        out_specs=pl.BlockSpec((tmm, n), lambda j, k: (j, 0)),
        out_shape=jax.ShapeDtypeStruct((m, n), F32),
        compiler_params=_params(("arbitrary", "arbitrary")),
    )(a, b)


def _a_specs(s):
    nq = s // A_TQ
    q_spec = pl.BlockSpec((None, None, A_TQ, HEAD_DIM), lambda h, i: (0, h, jnp.minimum(i, nq - 1), 0))
    kv_specs = [pl.BlockSpec((None, A_TQ, HEAD_DIM),
                             lambda h, i, m=m: (h, jnp.minimum(i + m, nq + A_NKB - 2), 0)) for m in range(A_NKB)]
    t_spec = pl.BlockSpec((None, A_TQ, A_BAND), lambda h, i: (h, 0, 0))
    return nq, q_spec, kv_specs, t_spec


def _a_scores(q_ref, k_refs, t_ref, i):
    qs = q_ref[...] * 0.125
    k = jnp.concatenate([r[...] for r in k_refs], axis=0)
    sc = _dot_nt(qs, k) + t_ref[...]
    col = lax.broadcasted_iota(jnp.int32, (A_TQ, A_BAND), 1)
    sc = jnp.where(col >= (A_BAND - A_TQ) - i * A_TQ, sc, NEG)
    return qs, k, sc


def mix_a_fwd(qkv, kpad, vpad, tbias):
    s = qkv.shape[2]
    nq, q_spec, kv_specs, t_spec = _a_specs(s)

    def body(*refs):
        q_ref = refs[0]
        k_refs = refs[1:1 + A_NKB]
        v_refs = refs[1 + A_NKB:1 + 2 * A_NKB]
        t_ref, o_ref, lse_ref = refs[1 + 2 * A_NKB:]
        i = pl.program_id(1)
        _, _, sc = _a_scores(q_ref, k_refs, t_ref, i)
        m = jnp.max(sc, axis=-1, keepdims=True)
        p = jnp.exp(sc - m)
        l = jnp.sum(p, axis=-1, keepdims=True)
        v = jnp.concatenate([r[...] for r in v_refs], axis=0)
        o_ref[...] = _dot(p.astype(BF16), v) / l
        lse_ref[...] = m + jnp.log(l)

    return pl.pallas_call(
        body, name="mix_a_fwd", grid=(N_HEADS, nq),
        in_specs=[q_spec] + kv_specs + kv_specs + [t_spec],
        out_specs=[pl.BlockSpec((None, A_TQ, HEAD_DIM), lambda h, i: (h, i, 0)),
                   pl.BlockSpec((None, A_TQ, 1), lambda h, i: (h, i, 0))],
        out_shape=[jax.ShapeDtypeStruct((N_HEADS, s, HEAD_DIM), F32),
                   jax.ShapeDtypeStruct((N_HEADS, s, 1), F32)],
        compiler_params=_params(("arbitrary", "arbitrary")),
    )(qkv, *([kpad] * A_NKB), *([vpad] * A_NKB), tbias)


def mix_a_bwd(qkv, kpad, vpad, tbias, do, o, lse):
    s = qkv.shape[2]
    nq, q_spec, kv_specs, t_spec = _a_specs(s)
    row_spec = lambda w: pl.BlockSpec((None, A_TQ, w), lambda h, i: (h, jnp.minimum(i, nq - 1), 0))
    keep = A_BAND - A_TQ

    def body(*refs):
        q_ref = refs[0]
        k_refs = refs[1:1 + A_NKB]
        v_refs = refs[1 + A_NKB:1 + 2 * A_NKB]
        t_ref, do_ref, o_ref, lse_ref, dq_ref, dk_ref, dv_ref, dt_ref, dk_win, dv_win = refs[1 + 2 * A_NKB:]
        i = pl.program_id(1)

        @pl.when(i == 0)
        def _():
            dk_win[...] = jnp.zeros_like(dk_win)
            dv_win[...] = jnp.zeros_like(dv_win)
            dt_ref[...] = jnp.zeros_like(dt_ref)

        @pl.when(i < nq)
        def _():
            qs, k, sc = _a_scores(q_ref, k_refs, t_ref, i)
            v = jnp.concatenate([r[...] for r in v_refs], axis=0)
            dob = do_ref[...]
            p = jnp.exp(sc - lse_ref[...])
            delta = jnp.sum(o_ref[...] * dob.astype(F32), axis=-1, keepdims=True)
            ds = p * (_dot_nt(dob, v) - delta)
            dsb = ds.astype(BF16)
            dq_ref[...] = _dot(dsb, k) * 0.125
            dk_win[...] += _dot_tn(dsb, qs)
            dv_win[...] += _dot_tn(p.astype(BF16), dob)
            dt_ref[...] += ds

        dk_ref[...] = dk_win[0:A_TQ, :]
        dv_ref[...] = dv_win[0:A_TQ, :]
        dk_rest = dk_win[A_TQ:A_BAND, :]
        dv_rest = dv_win[A_TQ:A_BAND, :]
        dk_win[0:keep, :] = dk_rest
        dv_win[0:keep, :] = dv_rest
        dk_win[keep:A_BAND, :] = jnp.zeros((A_TQ, HEAD_DIM), F32)
        dv_win[keep:A_BAND, :] = jnp.zeros((A_TQ, HEAD_DIM), F32)

    nsteps = nq + A_NKB - 1
    return pl.pallas_call(
        body, name="mix_a_bwd", grid=(N_HEADS, nsteps),
        in_specs=[q_spec] + kv_specs + kv_specs + [t_spec, row_spec(HEAD_DIM), row_spec(HEAD_DIM), row_spec(1)],
        out_specs=[row_spec(HEAD_DIM),
                   pl.BlockSpec((None, A_TQ, HEAD_DIM), lambda h, i: (h, i, 0)),
                   pl.BlockSpec((None, A_TQ, HEAD_DIM), lambda h, i: (h, i, 0)),
                   t_spec],
        out_shape=[jax.ShapeDtypeStruct((N_HEADS, s, HEAD_DIM), F32),
                   jax.ShapeDtypeStruct((N_HEADS, s + keep, HEAD_DIM), F32),
                   jax.ShapeDtypeStruct((N_HEADS, s + keep, HEAD_DIM), F32),
                   jax.ShapeDtypeStruct((N_HEADS, A_TQ, A_BAND), F32)],
        scratch_shapes=[pltpu.VMEM((A_BAND, HEAD_DIM), F32), pltpu.VMEM((A_BAND, HEAD_DIM), F32)],
        compiler_params=_params(("arbitrary", "arbitrary")),
    )(qkv, *([kpad] * A_NKB), *([vpad] * A_NKB), tbias, do, o, lse)


def relbias_tile(rel_bias, relmat):
    nrel = 2 * MAX_REL + 1

    def body(rb_ref, rel_ref, o_ref):
        rel = rel_ref[...]
        o_ref[...] = jnp.full(o_ref.shape, NEG, F32)

        def step(r, carry):
            hit = rel == r
            for h in range(N_HEADS):
                o_ref[h] = jnp.where(hit, rb_ref[h, r], o_ref[h])
            return carry

        lax.fori_loop(0, nrel, step, 0)

    return pl.pallas_call(
        body, name="relbias_tile",
        in_specs=[pl.BlockSpec(memory_space=pltpu.SMEM), pl.BlockSpec(memory_space=pltpu.VMEM)],
        out_specs=pl.BlockSpec(memory_space=pltpu.VMEM),
        out_shape=jax.ShapeDtypeStruct((N_HEADS, A_TQ, A_BAND), F32),
        compiler_params=_params(),
    )(rel_bias, relmat)


def relbias_grad(dt, relmat):
    nrel = 2 * MAX_REL + 1

    def body(dt_ref, rel_ref, o_ref):
        rel = rel_ref[...]
        lane = lax.broadcasted_iota(jnp.int32, (8, 384), 1)
        row = lax.broadcasted_iota(jnp.int32, (8, 384), 0)

        def step(r, acc):
            hit = rel == r
            for h in range(N_HEADS):
                val = jnp.sum(jnp.where(hit, dt_ref[h], 0.0))
                acc = jnp.where((lane == r) & (row == h), val, acc)
            return acc

        o_ref[...] = lax.fori_loop(0, nrel, step, jnp.zeros((8, 384), F32))

    return pl.pallas_call(
        body, name="relbias_grad",
        out_shape=jax.ShapeDtypeStruct((8, 384), F32),
        compiler_params=_params(),
    )(dt, relmat)


def _b_norm(v, gain):
    mu = jnp.mean(v, axis=-1, keepdims=True)
    xc = v - mu
    rstd = lax.rsqrt(jnp.mean(xc * xc, axis=-1, keepdims=True) + EPS)
    xhat = xc * rstd
    return xhat, rstd, xhat * gain


def _tril_mask():
    t = lax.broadcasted_iota(jnp.int32, (SG_CHUNK, SG_CHUNK), 0)
    u = lax.broadcasted_iota(jnp.int32, (SG_CHUNK, SG_CHUNK), 1)
    return u <= t


def mix_b_fwd(uv, gain, w_s, b_col):
    s = uv.shape[0]
    tm = min(ROW_T, s)

    def body(uv_ref, gain_ref, w_ref, b_ref, y_ref):
        tril = _tril_mask()
        ws = [jnp.where(tril, w_ref[g], 0.0).astype(BF16) for g in range(N_HEADS)]
        for c in range(tm // SG_CHUNK):
            rows = slice(c * SG_CHUNK, (c + 1) * SG_CHUNK)
            u = uv_ref[rows, 0:D_BRANCH]
            _, _, vn = _b_norm(uv_ref[rows, D_BRANCH:2 * D_BRANCH], gain_ref[...])
            vnb = vn.astype(BF16)
            outs = []
            for g in range(N_HEADS):
                cols = slice(g * HEAD_DIM, (g + 1) * HEAD_DIM)
                mixed = _dot(ws[g], vnb[:, cols]) + b_ref[g]
                outs.append(u[:, cols] * mixed)
            y_ref[rows, :] = jnp.concatenate(outs, axis=1)

    return pl.pallas_call(
        body, name="mix_b_fwd", grid=(s // tm,),
        in_specs=[pl.BlockSpec((tm, 2 * D_BRANCH), lambda i: (i, 0)),
                  pl.BlockSpec((1, D_BRANCH), lambda i: (0, 0)),
                  pl.BlockSpec((N_HEADS, SG_CHUNK, SG_CHUNK), lambda i: (0, 0, 0)),
                  pl.BlockSpec((N_HEADS, SG_CHUNK, 1), lambda i: (0, 0, 0))],
        out_specs=pl.BlockSpec((tm, D_BRANCH), lambda i: (i, 0)),
        out_shape=jax.ShapeDtypeStruct((s, D_BRANCH), F32),
        compiler_params=_params(("arbitrary",)),
    )(uv, gain, w_s, b_col)


def mix_b_bwd(uv, gain, w_s, b_col, dy):
    s = uv.shape[0]
    tm = min(ROW_T, s)

    def body(uv_ref, gain_ref, w_ref, b_ref, dy_ref, duv_ref, dw_ref, db_ref, dgain_ref):
        i = pl.program_id(0)

        @pl.when(i == 0)
        def _():
            dw_ref[...] = jnp.zeros_like(dw_ref)
            db_ref[...] = jnp.zeros_like(db_ref)
            dgain_ref[...] = jnp.zeros_like(dgain_ref)

        tril = _tril_mask()
        ws = [jnp.where(tril, w_ref[g], 0.0).astype(BF16) for g in range(N_HEADS)]
        gain_v = gain_ref[...]
        for c in range(tm // SG_CHUNK):
            rows = slice(c * SG_CHUNK, (c + 1) * SG_CHUNK)
            u = uv_ref[rows, 0:D_BRANCH]
            xhat, rstd, vn = _b_norm(uv_ref[rows, D_BRANCH:2 * D_BRANCH], gain_v)
            vnb = vn.astype(BF16)
            dyv = dy_ref[rows, :]
            dus, dvns = [], []
            for g in range(N_HEADS):
                cols = slice(g * HEAD_DIM, (g + 1) * HEAD_DIM)
                mixed = _dot(ws[g], vnb[:, cols]) + b_ref[g]
                dus.append(dyv[:, cols] * mixed)
                dmixed = dyv[:, cols] * u[:, cols]
                dmb = dmixed.astype(BF16)
                db_ref[g] += jnp.sum(dmixed, axis=-1, keepdims=True)
                dw_ref[g] += jnp.where(tril, _dot_nt(dmb, vnb[:, cols]), 0.0)
                dvns.append(_dot_tn(ws[g], dmb))
            dvn = jnp.concatenate(dvns, axis=1)
            dgain_ref[...] += jnp.sum(dvn * xhat, axis=0, keepdims=True)
            dxh = dvn * gain_v
            dv = rstd * (dxh - jnp.mean(dxh, axis=-1, keepdims=True)
                         - xhat * jnp.mean(dxh * xhat, axis=-1, keepdims=True))
            duv_ref[rows, :] = jnp.concatenate(dus + [dv], axis=1)

    return pl.pallas_call(
        body, name="mix_b_bwd", grid=(s // tm,),
        in_specs=[pl.BlockSpec((tm, 2 * D_BRANCH), lambda i: (i, 0)),
                  pl.BlockSpec((1, D_BRANCH), lambda i: (0, 0)),
                  pl.BlockSpec((N_HEADS, SG_CHUNK, SG_CHUNK), lambda i: (0, 0, 0)),
                  pl.BlockSpec((N_HEADS, SG_CHUNK, 1), lambda i: (0, 0, 0)),
                  pl.BlockSpec((tm, D_BRANCH), lambda i: (i, 0))],
        out_specs=[pl.BlockSpec((tm, 2 * D_BRANCH), lambda i: (i, 0)),
                   pl.BlockSpec((N_HEADS, SG_CHUNK, SG_CHUNK), lambda i: (0, 0, 0)),
                   pl.BlockSpec((N_HEADS, SG_CHUNK, 1), lambda i: (0, 0, 0)),
                   pl.BlockSpec((1, D_BRANCH), lambda i: (0, 0))],
        out_shape=[jax.ShapeDtypeStruct((s, 2 * D_BRANCH), F32),
                   jax.ShapeDtypeStruct((N_HEADS, SG_CHUNK, SG_CHUNK), F32),
                   jax.ShapeDtypeStruct((N_HEADS, SG_CHUNK, 1), F32),
                   jax.ShapeDtypeStruct((1, D_BRANCH), F32)],
        compiler_params=_params(("arbitrary",)),
    )(uv, gain, w_s, b_col, dy)


def _scan_mats(nrow):
    a = lax.broadcasted_iota(jnp.int32, (128, 128), 0)
    b = lax.broadcasted_iota(jnp.int32, (128, 128), 1)
    r = lax.broadcasted_iota(jnp.int32, (nrow, nrow), 0)
    c = lax.broadcasted_iota(jnp.int32, (nrow, nrow), 1)
    nb = nrow // N_HEADS
    same = (r // nb) == (c // nb)
    return a, b, r, c, same


def _exact_dot(x, m):
    hi, mid, lo = _split3(x)
    return _dot(hi, m) + _dot(mid, m) + _dot(lo, m)


def _exact_dot_left(m, x):
    hi, mid, lo = _split3(x)
    return _dot(m, hi) + _dot(m, mid) + _dot(m, lo)


def fox_gate_fwd(ft, bcol):
    nrow = ft.shape[0]

    def body(f_ref, b_ref, c_ref):
        z = f_ref[...] + b_ref[...]
        ls = jnp.minimum(z, 0.0) - jnp.log(1.0 + jnp.exp(-jnp.abs(z)))
        a, b, r, c, same = _scan_mats(nrow)
        within = _exact_dot(ls, (a <= b).astype(BF16))
        tot = jnp.broadcast_to(within[:, 127:128], within.shape)
        before = _exact_dot_left((same & (c < r)).astype(BF16), tot)
        c_ref[...] = within + before

    return pl.pallas_call(
        body, name="fox_gate_fwd",
        out_shape=jax.ShapeDtypeStruct((nrow, 128), F32),
        compiler_params=_params(),
    )(ft, bcol)


def fox_gate_bwd(ft, bcol, dc):
    nrow = ft.shape[0]

    def body(f_ref, b_ref, dc_ref, df_ref, db_ref):
        a, b, r, c, same = _scan_mats(nrow)
        dcv = dc_ref[...]
        within = _exact_dot(dcv, (a >= b).astype(BF16))
        tot = jnp.broadcast_to(within[:, 0:1], within.shape)
        after = _exact_dot_left((same & (c > r)).astype(BF16), tot)
        dls = within + after
        z = f_ref[...] + b_ref[...]
        dz = dls * _sigmoid(-z)
        df_ref[...] = dz
        rs = jnp.broadcast_to(jnp.sum(dz, axis=-1, keepdims=True), dz.shape)
        hr = lax.broadcasted_iota(jnp.int32, (8, nrow), 0)
        hc = lax.broadcasted_iota(jnp.int32, (8, nrow), 1)
        db_ref[...] = _exact_dot_left((hr == hc // (nrow // N_HEADS)).astype(BF16), rs)

    return pl.pallas_call(
        body, name="fox_gate_bwd",
        out_shape=[jax.ShapeDtypeStruct((nrow, 128), F32), jax.ShapeDtypeStruct((8, 128), F32)],
        compiler_params=_params(),
    )(ft, bcol, dc)


def _att_specs(s, qi, ki, vi):
    t = ATT_T
    q_spec = pl.BlockSpec((None, None, t, HEAD_DIM), lambda h, i: (qi, h, i, 0))
    k_spec = pl.BlockSpec((None, None, s, HEAD_DIM), lambda h, i: (ki, h, 0, 0))
    v_spec = pl.BlockSpec((None, None, s, HEAD_DIM), lambda h, i: (vi, h, 0, 0))
    row_spec = lambda w: pl.BlockSpec((None, t, w), lambda h, i: (h, i, 0))
    return q_spec, k_spec, v_spec, row_spec


def _causal(strict):
    row = lax.broadcasted_iota(jnp.int32, (ATT_T, ATT_T), 0)
    col = lax.broadcasted_iota(jnp.int32, (ATT_T, ATT_T), 1)
    return (col < row) if strict else (col <= row)


def _gate_row(cr_ref, kb, g):
    if g == 1:
        return cr_ref[kb]
    return jnp.concatenate([cr_ref[kb + n] for n in range(g)], axis=1)


def _fox_walk(i, carry, tile):
    g = FOX_WIDE
    nwide = i // g
    carry = lax.fori_loop(0, nwide, lambda n, c: tile(n * g, g, c, False), carry)
    carry = lax.fori_loop(nwide * g, i, lambda n, c: tile(n, 1, c, False), carry)
    return tile(i, 1, carry, True)


def fox_fwd(qkv, c_col, c_row):
    s = qkv.shape[2]
    t = ATT_T
    nq = s // t
    q_spec, k_spec, v_spec, row_spec = _att_specs(s, 3, 4, 5)

    def body(q_ref, k_ref, v_ref, cc_ref, cr_ref, o_ref, lse_ref):
        i = pl.program_id(1)
        qs = q_ref[...] * 0.125
        cc = cc_ref[...]

        def tile(kb, g, carry, masked):
            m, l, acc = carry
            k0 = pl.multiple_of(kb * t, t)
            sc = _dot_nt(qs, k_ref[pl.ds(k0, g * t), :]) + (cc - _gate_row(cr_ref, kb, g))
            if masked:
                sc = jnp.where(_causal(False), sc, NEG)
            m_new = jnp.maximum(m, jnp.max(sc, axis=-1, keepdims=True))
            alpha = jnp.exp(m - m_new)
            p = jnp.exp(sc - m_new)
            l = alpha * l + jnp.sum(p, axis=-1, keepdims=True)
            p_hi, p_lo = _split2(p)
            v = v_ref[pl.ds(k0, g * t), :]
            acc = alpha * acc + (_dot(p_hi, v) + _dot(p_lo, v))
            return m_new, l, acc

        init = (jnp.full((t, 1), NEG, F32), jnp.zeros((t, 1), F32), jnp.zeros((t, HEAD_DIM), F32))
        m, l, acc = _fox_walk(i, init, tile)
        o_ref[...] = acc / l
        lse_ref[...] = m + jnp.log(l)

    return pl.pallas_call(
        body, name="fox_fwd", grid=(N_HEADS, nq),
        in_specs=[q_spec, k_spec, v_spec, row_spec(1),
                  pl.BlockSpec((None, nq, 1, t), lambda h, i: (h, 0, 0, 0))],
        out_specs=[row_spec(HEAD_DIM), row_spec(1)],
        out_shape=[jax.ShapeDtypeStruct((N_HEADS, s, HEAD_DIM), F32),
                   jax.ShapeDtypeStruct((N_HEADS, s, 1), F32)],
        compiler_params=_params(("arbitrary", "arbitrary")),
    )(qkv, qkv, qkv, c_col, c_row)


def fox_bwd(qkv, c_col, c_row, do, o, lse):
    s = qkv.shape[2]
    t = ATT_T
    nq = s // t
    q_spec, k_spec, v_spec, row_spec = _att_specs(s, 3, 4, 5)
    any_spec = pl.BlockSpec(memory_space=pl.ANY)

    def body(q_ref, k_ref, v_ref, cc_ref, cr_ref, do_ref, o_ref, lse_ref,
             dq_ref, dk_hbm, dv_hbm, dc_ref, dk_acc, dv_acc):
        h = pl.program_id(0)
        i = pl.program_id(1)

        @pl.when(i == 0)
        def _():
            dk_acc[...] = jnp.zeros_like(dk_acc)
            dv_acc[...] = jnp.zeros_like(dv_acc)
            dc_ref[...] = jnp.zeros_like(dc_ref)

        qs = q_ref[...] * 0.125
        dob = do_ref[...]
        delta = jnp.sum(o_ref[...] * dob.astype(F32), axis=-1, keepdims=True)
        lse = lse_ref[...]
        cc = cc_ref[...]

        def tile(kb, g, dq, masked):
            k0 = pl.multiple_of(kb * t, t)
            k = k_ref[pl.ds(k0, g * t), :]
            sc = _dot_nt(qs, k) + (cc - _gate_row(cr_ref, kb, g))
            if masked:
                sc = jnp.where(_causal(False), sc, NEG)
            p = jnp.exp(sc - lse)
            ds = p * (_dot_nt(dob, v_ref[pl.ds(k0, g * t), :]) - delta)
            dsb = ds.astype(BF16)
            dk_acc[pl.ds(k0, g * t), :] += _dot_tn(dsb, qs)
            dv_acc[pl.ds(k0, g * t), :] += _dot_tn(p.astype(BF16), dob)
            dcs = -jnp.sum(ds, axis=0, keepdims=True)
            for n in range(g):
                dc_ref[kb + n] += dcs[:, n * t:(n + 1) * t]
            return dq + _dot(dsb, k)

        dq = _fox_walk(i, jnp.zeros((t, HEAD_DIM), F32), tile)
        dq_ref[...] = dq * 0.125

        @pl.when(i == nq - 1)
        def _():
            pltpu.sync_copy(dk_acc, dk_hbm.at[h])
            pltpu.sync_copy(dv_acc, dv_hbm.at[h])

    return pl.pallas_call(
        body, name="fox_bwd", grid=(N_HEADS, nq),
        in_specs=[q_spec, k_spec, v_spec, row_spec(1),
                  pl.BlockSpec((None, nq, 1, t), lambda h, i: (h, 0, 0, 0)),
                  row_spec(HEAD_DIM), row_spec(HEAD_DIM), row_spec(1)],
        out_specs=[row_spec(HEAD_DIM), any_spec, any_spec,
                   pl.BlockSpec((None, nq, 1, t), lambda h, i: (h, 0, 0, 0))],
        out_shape=[jax.ShapeDtypeStruct((N_HEADS, s, HEAD_DIM), F32),
                   jax.ShapeDtypeStruct((N_HEADS, s, HEAD_DIM), F32),
                   jax.ShapeDtypeStruct((N_HEADS, s, HEAD_DIM), F32),
                   jax.ShapeDtypeStruct((N_HEADS, nq, 1, t), F32)],
        scratch_shapes=[pltpu.VMEM((s, HEAD_DIM), F32), pltpu.VMEM((s, HEAD_DIM), F32)],
        compiler_params=_params(("arbitrary", "arbitrary")),
    )(qkv, qkv, qkv, c_col, c_row, do, o, lse)


def _sb_tile(qs, k, run, masked):
    z = _dot_nt(qs, k)
    sp = jnp.log(1.0 + jnp.exp(-jnp.abs(z)))
    ls = jnp.minimum(z, 0.0) - sp
    lm = -jnp.maximum(z, 0.0) - sp
    if masked:
        valid = _causal(True)
        lm = jnp.where(valid, lm, 0.0)
    row = lax.broadcasted_iota(jnp.int32, (ATT_T, ATT_T), 0)
    col = lax.broadcasted_iota(jnp.int32, (ATT_T, ATT_T), 1)
    later = (row > col).astype(BF16)
    hi, lo = _split2(lm)
    between = run + _dot(hi, later) + _dot(lo, later)
    a = jnp.exp(ls + between)
    if masked:
        a = jnp.where(valid, a, 0.0)
    return ls, lm, a


def _sb_walk(i, carry, tile):
    def alive_of(c):
        return (jnp.max(c[0]) > SB_DEAD).astype(jnp.int32)

    def cond(state):
        n, alive = state[0], state[1]
        return jnp.logical_and(n < i, alive > 0)

    def step(state):
        n = state[0]
        c = tile(i - 1 - n, state[2:], False)
        return (n + 1, alive_of(c)) + tuple(c)

    out = lax.while_loop(cond, step, (jnp.int32(0), alive_of(carry)) + tuple(carry))
    return out[2:]


def sb_fwd(qkv):
    s = qkv.shape[2]
    t = ATT_T
    nq = s // t
    q_spec, k_spec, v_spec, row_spec = _att_specs(s, 6, 7, 8)

    def body(q_ref, k_ref, v_ref, o_ref):
        i = pl.program_id(1)
        qs = q_ref[...] * 0.125

        def tile(kb, carry, masked):
            run, acc = carry
            k0 = pl.multiple_of(kb * t, t)
            _, lm, a = _sb_tile(qs, k_ref[pl.ds(k0, t), :], run, masked)
            acc = acc + _dot(a.astype(BF16), v_ref[pl.ds(k0, t), :])
            return run + jnp.sum(lm, axis=-1, keepdims=True), acc

        carry = tile(i, (jnp.zeros((t, 1), F32), jnp.zeros((t, HEAD_DIM), F32)), True)
        _, acc = _sb_walk(i, carry, tile)
        o_ref[...] = acc

    return pl.pallas_call(
        body, name="sb_fwd", grid=(N_HEADS, nq),
        in_specs=[q_spec, k_spec, v_spec],
        out_specs=row_spec(HEAD_DIM),
        out_shape=jax.ShapeDtypeStruct((N_HEADS, s, HEAD_DIM), F32),
        compiler_params=_params(("arbitrary", "arbitrary")),
    )(qkv, qkv, qkv)


def sb_bwd(qkv, do, o):
    s = qkv.shape[2]
    t = ATT_T
    nq = s // t
    q_spec, k_spec, v_spec, row_spec = _att_specs(s, 6, 7, 8)
    any_spec = pl.BlockSpec(memory_space=pl.ANY)

    def body(q_ref, k_ref, v_ref, do_ref, o_ref, dq_ref, dk_hbm, dv_hbm, dk_acc, dv_acc):
        h = pl.program_id(0)
        i = pl.program_id(1)

        @pl.when(i == 0)
        def _():
            dk_acc[...] = jnp.zeros_like(dk_acc)
            dv_acc[...] = jnp.zeros_like(dv_acc)

        qs = q_ref[...] * 0.125
        dob = do_ref[...]
        tot = jnp.sum(o_ref[...] * dob.astype(F32), axis=-1, keepdims=True)

        def tile(kb, carry, masked):
            run, run_g, dq = carry
            k0 = pl.multiple_of(kb * t, t)
            k = k_ref[pl.ds(k0, t), :]
            ls, lm, a = _sb_tile(qs, k, run, masked)
            ab = a.astype(BF16)
            g = ab.astype(F32) * _dot_nt(dob, v_ref[pl.ds(k0, t), :])
            row = lax.broadcasted_iota(jnp.int32, (t, t), 0)
            col = lax.broadcasted_iota(jnp.int32, (t, t), 1)
            from_here = (row >= col).astype(BF16)
            hi, lo = _split2(g)
            g_right = run_g + _dot(hi, from_here) + _dot(lo, from_here)
            g_left = tot - g_right
            dz = g - jnp.exp(ls) * (g + g_left)
            if masked:
                dz = jnp.where(_causal(True), dz, 0.0)
            dzb = dz.astype(BF16)
            dk_acc[pl.ds(k0, t), :] += _dot_tn(dzb, qs)
            dv_acc[pl.ds(k0, t), :] += _dot_tn(ab, dob)
            return (run + jnp.sum(lm, axis=-1, keepdims=True),
                    run_g + jnp.sum(g, axis=-1, keepdims=True),
                    dq + _dot(dzb, k))

        zero = jnp.zeros((t, 1), F32)
        carry = tile(i, (zero, zero, jnp.zeros((t, HEAD_DIM), F32)), True)
        _, _, dq = _sb_walk(i, carry, tile)
        dq_ref[...] = dq * 0.125

        @pl.when(i == nq - 1)
        def _():
            pltpu.sync_copy(dk_acc, dk_hbm.at[h])
            pltpu.sync_copy(dv_acc, dv_hbm.at[h])

    return pl.pallas_call(
        body, name="sb_bwd", grid=(N_HEADS, nq),
        in_specs=[q_spec, k_spec, v_spec, row_spec(HEAD_DIM), row_spec(HEAD_DIM)],
        out_specs=[row_spec(HEAD_DIM), any_spec, any_spec],
        out_shape=[jax.ShapeDtypeStruct((N_HEADS, s, HEAD_DIM), F32)] * 3,
        scratch_shapes=[pltpu.VMEM((s, HEAD_DIM), F32), pltpu.VMEM((s, HEAD_DIM), F32)],
        compiler_params=_params(("arbitrary", "arbitrary")),
    )(qkv, qkv, qkv, do, o)


def _branch_inputs(refs, br):
    ya_ref, yb_ref, yc_ref, yd_ref = refs
    if br == 1:
        return yb_ref[...]
    return _heads_to_lanes((ya_ref, None, yc_ref, yd_ref)[br])


def outproj_fwd(x, ya, yb, yc, yd, gates, bg, wout):
    s = x.shape[0]
    tm = min(ROW_T, s)

    def body(x_ref, ya_ref, yb_ref, yc_ref, yd_ref, gates_ref, bg_ref, w_ref, out_ref):
        pieces = []
        for br in range(4):
            cols = slice(br * D_BRANCH, (br + 1) * D_BRANCH)
            y = _branch_inputs((ya_ref, yb_ref, yc_ref, yd_ref), br)
            r = lax.rsqrt(jnp.mean(y * y, axis=-1, keepdims=True) + EPS)
            gt = gates_ref[:, cols]
            pieces.append((y * r * bg_ref[:, cols]) * (gt * _sigmoid(gt)))
        merged = jnp.concatenate(pieces, axis=1).astype(BF16)
        out_ref[...] = x_ref[...] + _dot(merged, w_ref[...])

    head_spec = pl.BlockSpec((N_HEADS, tm, HEAD_DIM), lambda i: (0, i, 0))
    return pl.pallas_call(
        body, name="outproj_fwd", grid=(s // tm,),
        in_specs=[pl.BlockSpec((tm, D_MODEL), lambda i: (i, 0)),
                  head_spec, pl.BlockSpec((tm, D_BRANCH), lambda i: (i, 0)), head_spec, head_spec,
                  pl.BlockSpec((tm, D_MODEL), lambda i: (i, 0)),
                  pl.BlockSpec((1, D_MODEL), lambda i: (0, 0)),
                  pl.BlockSpec((D_MODEL, D_MODEL), lambda i: (0, 0))],
        out_specs=pl.BlockSpec((tm, D_MODEL), lambda i: (i, 0)),
        out_shape=jax.ShapeDtypeStruct((s, D_MODEL), F32),
        compiler_params=_params(("arbitrary",)),
    )(x, ya, yb, yc, yd, gates, bg, wout)


def outproj_bwd(dout, ya, yb, yc, yd, gates, bg, wout):
    s = dout.shape[0]
    tm = min(ROW_T, s)

    def body(dout_ref, ya_ref, yb_ref, yc_ref, yd_ref, gates_ref, bg_ref, w_ref,
             dya_ref, dyb_ref, dyc_ref, dyd_ref, dgates_ref, dbg_ref, dw_ref):
        i = pl.program_id(0)

        @pl.when(i == 0)
        def _():
            dbg_ref[...] = jnp.zeros_like(dbg_ref)
            dw_ref[...] = jnp.zeros_like(dw_ref)

        doutb = dout_ref[...].astype(BF16)
        dmerged = _dot_nt(doutb, w_ref[...])
        pieces = []
        for br in range(4):
            cols = slice(br * D_BRANCH, (br + 1) * D_BRANCH)
            y = _branch_inputs((ya_ref, yb_ref, yc_ref, yd_ref), br)
            r = lax.rsqrt(jnp.mean(y * y, axis=-1, keepdims=True) + EPS)
            yn = y * r
            bgv = bg_ref[:, cols]
            gt = gates_ref[:, cols]
            sig = _sigmoid(gt)
            act = gt * sig
            n = yn * bgv
            pieces.append(n * act)
            dm = dmerged[:, cols]
            dn = dm * act
            dgates_ref[:, cols] = (dm * n * (sig * (1.0 + gt * (1.0 - sig)))).astype(BF16)
            dbg_ref[:, cols] += jnp.sum(dn * yn, axis=0, keepdims=True)
            u = dn * bgv
            dy = r * (u - yn * jnp.mean(yn * u, axis=-1, keepdims=True))
            if br == 1:
                dyb_ref[...] = dy
            else:
                dref = (dya_ref, None, dyc_ref, dyd_ref)[br]
                for hh in range(N_HEADS):
                    dref[hh] = dy[:, hh * HEAD_DIM:(hh + 1) * HEAD_DIM].astype(BF16)
        merged = jnp.concatenate(pieces, axis=1).astype(BF16)
        dw_ref[...] += _dot_tn(merged, doutb)

    head_spec = pl.BlockSpec((N_HEADS, tm, HEAD_DIM), lambda i: (0, i, 0))
    head_shape = jax.ShapeDtypeStruct((N_HEADS, s, HEAD_DIM), BF16)
    return pl.pallas_call(
        body, name="outproj_bwd", grid=(s // tm,),
        in_specs=[pl.BlockSpec((tm, D_MODEL), lambda i: (i, 0)),
                  head_spec, pl.BlockSpec((tm, D_BRANCH), lambda i: (i, 0)), head_spec, head_spec,
                  pl.BlockSpec((tm, D_MODEL), lambda i: (i, 0)),
                  pl.BlockSpec((1, D_MODEL), lambda i: (0, 0)),
                  pl.BlockSpec((D_MODEL, D_MODEL), lambda i: (0, 0))],
        out_specs=[head_spec, pl.BlockSpec((tm, D_BRANCH), lambda i: (i, 0)), head_spec, head_spec,
                   pl.BlockSpec((tm, D_MODEL), lambda i: (i, 0)),
                   pl.BlockSpec((1, D_MODEL), lambda i: (0, 0)),
                   pl.BlockSpec((D_MODEL, D_MODEL), lambda i: (0, 0))],
        out_shape=[head_shape, jax.ShapeDtypeStruct((s, D_BRANCH), F32), head_shape, head_shape,
                   jax.ShapeDtypeStruct((s, D_MODEL), BF16),
                   jax.ShapeDtypeStruct((1, D_MODEL), F32),
                   jax.ShapeDtypeStruct((D_MODEL, D_MODEL), F32)],
        compiler_params=_params(("arbitrary",)),
    )(dout, ya, yb, yc, yd, gates, bg, wout)


def final_loss(x, tgt, g):
    s = x.shape[0]
    tm = min(ROW_T, s)

    def body(x_ref, t_ref, g_ref, loss_ref, dx_ref, dg_ref):
        i = pl.program_id(0)

        @pl.when(i == 0)
        def _():
            loss_ref[...] = jnp.zeros_like(loss_ref)
            dg_ref[...] = jnp.zeros_like(dg_ref)

        xv = x_ref[...]
        gv = g_ref[...]
        r = lax.rsqrt(jnp.mean(xv * xv, axis=-1, keepdims=True) + EPS)
        xn = xv * r
        err = xn * gv - t_ref[...]
        loss_ref[...] += jnp.sum(err * err) * (0.5 / D_MODEL)
        dy = err * (1.0 / D_MODEL)
        u = dy * gv
        dx_ref[...] = r * (u - xn * jnp.mean(xn * u, axis=-1, keepdims=True))
        dg_ref[...] += jnp.sum(dy * xn, axis=0, keepdims=True)

    return pl.pallas_call(
        body, name="final_loss", grid=(s // tm,),
        in_specs=[pl.BlockSpec((tm, D_MODEL), lambda i: (i, 0)),
                  pl.BlockSpec((tm, D_MODEL), lambda i: (i, 0)),
                  pl.BlockSpec((1, D_MODEL), lambda i: (0, 0))],
        out_specs=[pl.BlockSpec((1, 128), lambda i: (0, 0)),
                   pl.BlockSpec((tm, D_MODEL), lambda i: (i, 0)),
                   pl.BlockSpec((1, D_MODEL), lambda i: (0, 0))],
        out_shape=[jax.ShapeDtypeStruct((1, 128), F32),
                   jax.ShapeDtypeStruct((s, D_MODEL), F32),
                   jax.ShapeDtypeStruct((1, D_MODEL), F32)],
        compiler_params=_params(("arbitrary",)),
    )(x, tgt, g)


def _rel_index():
    i = np.arange(A_TQ)[:, None]
    j = np.arange(A_BAND)[None, :]
    rel = np.clip(i - j + (A_BAND - A_TQ), -MAX_REL, MAX_REL) + MAX_REL
    dchunk = i // CHUNK + LOOKBACK - j // CHUNK
    valid = (dchunk >= 0) & (dchunk <= LOOKBACK)
    return jnp.asarray(np.where(valid, rel, -1).astype(np.int32))


def _layer_consts(p):
    tbias = relbias_tile(p["rel_bias"], _rel_index())
    return dict(
        norm_g=p["norm_g"].reshape(1, D_MODEL),
        v_gain=p["v_gain"].reshape(1, D_BRANCH),
        b_col=p["b_s"].reshape(N_HEADS, SG_CHUNK, 1),
        bg=p["branch_gain"].reshape(1, D_MODEL),
        tbias=tbias,
    )


def _gate_layout(fp, b_f, s):
    nb = s // 128
    ft = fp[:, :N_HEADS].T.reshape(N_HEADS * nb, 128)
    bcol = jnp.repeat(b_f, nb).reshape(N_HEADS * nb, 1)
    return ft, bcol


def layer_fwd(x, p):
    s = x.shape[0]
    c = _layer_consts(p)
    h, qkv, gates, uv, fp = inproj_fwd(x, c["norm_g"], p["wp"])
    keep = A_BAND - A_TQ
    kpad = jnp.pad(qkv[1], ((0, 0), (keep, 0), (0, 0)))
    vpad = jnp.pad(qkv[2], ((0, 0), (keep, 0), (0, 0)))
    ya, lse_a = mix_a_fwd(qkv, kpad, vpad, c["tbias"])
    yb = mix_b_fwd(uv, c["v_gain"], p["w_s"], c["b_col"])
    ft, bcol = _gate_layout(fp, p["b_f"], s)
    cum = fox_gate_fwd(ft, bcol).reshape(N_HEADS, s)
    c_col = cum.reshape(N_HEADS, s, 1)
    c_row = cum.reshape(N_HEADS, s // ATT_T, 1, ATT_T)
    yc, lse_c = fox_fwd(qkv, c_col, c_row)
    yd = sb_fwd(qkv)
    out = outproj_fwd(x, ya, yb, yc, yd, gates, c["bg"], p["wout"])
    saved = dict(consts=c, x=x, h=h, qkv=qkv, gates=gates, uv=uv, kpad=kpad, vpad=vpad, ft=ft, bcol=bcol,
                 c_col=c_col, c_row=c_row, ya=ya, lse_a=lse_a, yb=yb, yc=yc, lse_c=lse_c, yd=yd)
    return out, saved


def layer_bwd(dout, p, sv):
    s = dout.shape[0]
    c = sv["consts"]
    dya, dyb, dyc, dyd, dgates, dbg, dwout = outproj_bwd(
        dout, sv["ya"], sv["yb"], sv["yc"], sv["yd"], sv["gates"], c["bg"], p["wout"])
    keep = A_BAND - A_TQ
    dqa, dkpad, dvpad, dt = mix_a_bwd(sv["qkv"], sv["kpad"], sv["vpad"], c["tbias"], dya, sv["ya"], sv["lse_a"])
    dka, dva = dkpad[:, keep:], dvpad[:, keep:]
    drel = relbias_grad(dt, _rel_index())[:N_HEADS, :2 * MAX_REL + 1]
    duv, dws, dbs, dvgain = mix_b_bwd(sv["uv"], c["v_gain"], p["w_s"], c["b_col"], dyb)
    dqc, dkc, dvc, dc = fox_bwd(sv["qkv"], sv["c_col"], sv["c_row"], dyc, sv["yc"], sv["lse_c"])
    dft, dbf = fox_gate_bwd(sv["ft"], sv["bcol"], dc.reshape(N_HEADS * (s // 128), 128))
    dfp = jnp.pad(dft.reshape(N_HEADS, s).T, ((0, 0), (0, 128 - N_HEADS)))
    dqd, dkd, dvd = sb_bwd(sv["qkv"], dyd, sv["yd"])
    dp, dx, dnorm = inproj_bwd((dqa, dka, dva, dqc, dkc, dvc, dqd, dkd, dvd), dgates, duv, dfp,
                               p["wp"], sv["x"], c["norm_g"], dout)
    dwp = weight_grad(sv["h"], dp, "inproj_wgrad")
    grads = dict(norm_g=dnorm.reshape(D_MODEL), wp=dwp, b_f=dbf[:N_HEADS, 0], rel_bias=drel,
                 w_s=dws, b_s=dbs.reshape(N_HEADS, SG_CHUNK), v_gain=dvgain.reshape(D_BRANCH),
                 branch_gain=dbg.reshape(4, D_BRANCH), wout=dwout)
    return dx, grads


def local_step(x, tgt, layers, final_g):
    saved = []
    cur = x
    for p in layers:
        cur, sv = layer_fwd(cur, p)
        saved.append(sv)
    loss, dcur, dfinal = final_loss(cur, tgt, final_g.reshape(1, D_MODEL))
    grads = [None] * len(layers)
    for l in reversed(range(len(layers))):
        dcur, grads[l] = layer_bwd(dcur, layers[l], saved[l])
    return loss[0, 0], dcur, grads, dfinal.reshape(D_MODEL)


def gather_weights(wb, wf):
    def body(wb_ref, wf_ref, ob_ref, of_ref, send_sems, recv_sems, loc_sems):
        x, y, c = lax.axis_index("x"), lax.axis_index("y"), lax.axis_index("c")
        me = 2 * x + y
        chips = [(1 - x, y), (x, 1 - y), (1 - x, 1 - y)]
        pairs = [(wb_ref, ob_ref), (wf_ref, of_ref)]
        local = [pltpu.make_async_copy(src, dst.at[me], loc_sems.at[n]) for n, (src, dst) in enumerate(pairs)]
        for cp in local:
            cp.start()

        def copy(j, n, slot):
            src, dst = pairs[n]
            return pltpu.make_async_remote_copy(
                src_ref=src, dst_ref=dst.at[slot], send_sem=send_sems.at[2 * j + n], recv_sem=recv_sems.at[2 * j + n],
                device_id=(chips[j][0], chips[j][1], c), device_id_type=MESH)

        sends = [copy(j, n, me) for j in range(3) for n in range(2)]
        for cp in sends:
            cp.start()
        for j in range(3):
            for n in range(2):
                copy(j, n, 2 * chips[j][0] + chips[j][1]).wait_recv()
        for cp in sends:
            cp.wait_send()
        for cp in local:
            cp.wait()

    any_spec = pl.BlockSpec(memory_space=pl.ANY)
    return pl.pallas_call(
        body, name="gather_weights",
        in_specs=[any_spec, any_spec], out_specs=[any_spec, any_spec],
        out_shape=[jax.ShapeDtypeStruct((4,) + wb.shape, wb.dtype), jax.ShapeDtypeStruct((4,) + wf.shape, wf.dtype)],
        scratch_shapes=[pltpu.SemaphoreType.DMA((6,)), pltpu.SemaphoreType.DMA((6,)), pltpu.SemaphoreType.DMA((2,))],
    )(wb, wf)


def exchange_grads(send):
    def body(s_ref, r_ref, send_sems, recv_sems, loc_sem):
        x, y, c = lax.axis_index("x"), lax.axis_index("y"), lax.axis_index("c")
        me_chip = 2 * x + y
        me = 4 * x + 2 * y + c
        peers = [(x, y, 1 - c)]
        for px, py in [(1 - x, y), (x, 1 - y), (1 - x, 1 - y)]:
            peers += [(px, py, c), (px, py, 1 - c)]
        local = pltpu.make_async_copy(s_ref.at[me_chip], r_ref.at[me], loc_sem)
        local.start()

        def copy(n, chip, slot):
            return pltpu.make_async_remote_copy(
                src_ref=s_ref.at[chip], dst_ref=r_ref.at[slot], send_sem=send_sems.at[n], recv_sem=recv_sems.at[n],
                device_id=peers[n], device_id_type=MESH)

        sends = [copy(n, 2 * px + py, me) for n, (px, py, _) in enumerate(peers)]
        for cp in sends:
            cp.start()
        for n, (px, py, pc) in enumerate(peers):
            copy(n, me_chip, 4 * px + 2 * py + pc).wait_recv()
        for cp in sends:
            cp.wait_send()
        local.wait()

    any_spec = pl.BlockSpec(memory_space=pl.ANY)
    return pl.pallas_call(
        body, name="exchange_grads",
        in_specs=[any_spec], out_specs=any_spec,
        out_shape=jax.ShapeDtypeStruct((8,) + send.shape[1:], send.dtype),
        scratch_shapes=[pltpu.SemaphoreType.DMA((7,)), pltpu.SemaphoreType.DMA((7,)), pltpu.SemaphoreType.DMA],
    )(send)


def adamw_reduce(parts, w, m, v):
    rows = w.shape[0]
    tr = 512
    c1 = 1.0 - ADAM_B1 ** ADAM_STEP
    c2 = 1.0 - ADAM_B2 ** ADAM_STEP

    def body(p_ref, w_ref, m_ref, v_ref, g_ref, d_ref, nm_ref, nv_ref):
        g = p_ref[0]
        for n in range(1, 8):
            g = g + p_ref[n]
        g_ref[...] = g
        nm = ADAM_B1 * m_ref[...] + (1.0 - ADAM_B1) * g
        nv = ADAM_B2 * v_ref[...] + (1.0 - ADAM_B2) * (g * g)
        nm_ref[...] = nm
        nv_ref[...] = nv
        d_ref[...] = -ADAM_LR * ((nm / c1) / (jnp.sqrt(nv / c2) + ADAM_EPS) + ADAM_WD * w_ref[...])

    spec = pl.BlockSpec((tr, 128), lambda i: (i, 0))
    shape = jax.ShapeDtypeStruct((rows, 128), F32)
    return pl.pallas_call(
        body, name="adamw_reduce", grid=(rows // tr,),
        in_specs=[pl.BlockSpec((8, tr, 128), lambda i: (0, i, 0)), spec, spec, spec],
        out_specs=[spec] * 4, out_shape=[shape] * 4,
        compiler_params=_params(("arbitrary",)),
    )(parts, w, m, v)


SHARDED = ("w_in", "w_out", "branch_gain")
SMALL = ("norm_g", "b_f", "rel_bias", "w_s", "b_s", "v_gain", "final_g")
WEIGHTS = ("norm_g", "w_in", "b_f", "rel_bias", "w_s", "b_s", "v_gain", "branch_gain", "w_out", "final_g")
PACK_ORDER = SHARDED + SMALL
PACK_ROW_TILE = 512


def _rows_of(shape):
    return -(-int(np.prod(shape)) // 128)


def _pack(leaves):
    parts = []
    for a in leaves:
        flat = a.reshape(-1).astype(F32)
        parts.append(jnp.pad(flat, (0, _rows_of(a.shape) * 128 - flat.shape[0])))
    flat = jnp.concatenate(parts)
    rows = flat.shape[0] // 128
    total = -(-rows // PACK_ROW_TILE) * PACK_ROW_TILE
    return jnp.pad(flat, (0, (total - rows) * 128)).reshape(total, 128)


def _unpack(slab, shapes):
    out, row = [], 0
    for shp in shapes:
        n = int(np.prod(shp))
        r = _rows_of(shp)
        out.append(slab[row:row + r].reshape(-1)[:n].reshape(shp))
        row += r
    return out


def _pack_w_in(w):
    return jnp.concatenate([w[:, :2816], w[:, 2820:], w[:, 2816:2820],
                            jnp.zeros((w.shape[0], N_PACK - N_IN), w.dtype)], axis=1)


def _unpack_w_in(wp):
    return jnp.concatenate([wp[:, :2816], wp[:, F_COL:F_COL + N_HEADS], wp[:, 2816:F_COL]], axis=1)


def kernel(x, norm_g, w_in, b_f, rel_bias, w_s, b_s, v_gain, branch_gain, w_out, final_g, loss_target, m_norm_g, m_w_in, m_b_f, m_rel_bias, m_w_s, m_b_s, m_v_gain, m_branch_gain, m_w_out, m_final_g, v_norm_g, v_w_in, v_b_f, v_rel_bias, v_w_s, v_b_s, v_v_gain, v_branch_gain, v_w_out, v_final_g):
    depth = norm_g.shape[0]
    weights = dict(norm_g=norm_g, w_in=w_in, b_f=b_f, rel_bias=rel_bias, w_s=w_s, b_s=b_s, v_gain=v_gain,
                   branch_gain=branch_gain, w_out=w_out, final_g=final_g)
    mom1 = dict(norm_g=m_norm_g, w_in=m_w_in, b_f=m_b_f, rel_bias=m_rel_bias, w_s=m_w_s, b_s=m_b_s,
                v_gain=m_v_gain, branch_gain=m_branch_gain, w_out=m_w_out, final_g=m_final_g)
    mom2 = dict(norm_g=v_norm_g, w_in=v_w_in, b_f=v_b_f, rel_bias=v_rel_bias, w_s=v_w_s, b_s=v_b_s,
                v_gain=v_v_gain, branch_gain=v_branch_gain, w_out=v_w_out, final_g=v_final_g)

    n_in_rows = _rows_of(w_in.shape)
    n_out_rows = _rows_of(w_out.shape)
    wb = jnp.concatenate([w_in.astype(BF16).reshape(n_in_rows, 128), w_out.astype(BF16).reshape(n_out_rows, 128)])
    wf = jnp.pad(branch_gain.reshape(-1), (0, 8 * 128 - branch_gain.size)).reshape(8, 128)
    gb, gf = gather_weights(wb, wf)
    w_in_full = gb[:, :n_in_rows].reshape((4,) + w_in.shape)
    w_in_full = jnp.moveaxis(w_in_full, 0, 2).reshape(depth, D_MODEL, N_IN)
    w_out_full = gb[:, n_in_rows:].reshape((4,) + w_out.shape)
    w_out_full = jnp.moveaxis(w_out_full, 0, 1).reshape(depth, D_MODEL, D_MODEL)
    bg_full = gf.reshape(4, -1)[:, :branch_gain.size].reshape((4,) + branch_gain.shape)
    bg_full = jnp.moveaxis(bg_full, 0, 2).reshape(depth, 4, D_BRANCH)

    layers = [dict(norm_g=norm_g[l], wp=_pack_w_in(w_in_full[l]), b_f=b_f[l], rel_bias=rel_bias[l], w_s=w_s[l],
                   b_s=b_s[l], v_gain=v_gain[l], branch_gain=bg_full[l], wout=w_out_full[l]) for l in range(depth)]

    loss_part, grad_x, lgrads, dfinal = local_step(x[0], loss_target[0], layers, final_g)
    loss = lax.psum(loss_part, ("x", "y", "c"))

    stack = lambda k: jnp.stack([g[k] for g in lgrads])
    d_w_in = jnp.stack([_unpack_w_in(g["wp"]) for g in lgrads])
    d_w_out = stack("wout")
    d_bg = stack("branch_gain")
    small = dict(norm_g=stack("norm_g"), b_f=stack("b_f"), rel_bias=stack("rel_bias"), w_s=stack("w_s"),
                 b_s=stack("b_s"), v_gain=stack("v_gain"), final_g=dfinal)
    slabs = []
    for sidx in range(4):
        leaves = [d_w_in[:, :, sidx * N_SHARD:(sidx + 1) * N_SHARD],
                  d_w_out[:, sidx * D_BRANCH:(sidx + 1) * D_BRANCH, :],
                  d_bg[:, :, sidx * HEAD_DIM:(sidx + 1) * HEAD_DIM]] + [small[k] for k in SMALL]
        slabs.append(_pack(leaves))
    parts = exchange_grads(jnp.stack(slabs))

    pack_local = lambda d: _pack([d[k] for k in PACK_ORDER])
    g_slab, d_slab, m_slab, v_slab = adamw_reduce(parts, pack_local(weights), pack_local(mom1), pack_local(mom2))
    shapes = [weights[k].shape for k in PACK_ORDER]
    outs = {}
    for tag, slab in (("grad", g_slab), ("delta", d_slab), ("new_m", m_slab), ("new_v", v_slab)):
        for k, a in zip(PACK_ORDER, _unpack(slab, shapes)):
            outs[tag, k] = a
    result = [loss, grad_x[None]]
    for tag in ("grad", "delta", "new_m", "new_v"):
        result += [outs[tag, k] for k in WEIGHTS]
    return tuple(result)
```

```python
import functools

import jax
import jax.numpy as jnp
import numpy as np
from jax import lax
from jax.experimental import pallas as pl
from jax.experimental.pallas import tpu as pltpu

F32 = jnp.float32
BF16 = jnp.bfloat16
MESH = pl.DeviceIdType.MESH

D_MODEL = 1024
D_BRANCH = 256
N_HEADS = 4
HEAD_DIM = 64
CHUNK = 64
LOOKBACK = 8
MAX_REL = 128
SG_CHUNK = 128
EPS = 1e-6
N_IN = 3844
N_PACK = 3968
F_COL = 3840
N_SHARD = 961
NEG = -1e30

A_TQ = 128
A_NKB = 5
A_BAND = A_TQ * A_NKB
ATT_T = 256
FOX_WIDE = 4
FOX_DEAD = -110.0
SB_DEAD = -110.0
ROW_T = 512
VMEM_LIMIT = 56 * 1024 * 1024

ADAM_LR = 0.001
ADAM_B1 = 0.9
ADAM_B2 = 0.999
ADAM_EPS = 1e-08
ADAM_WD = 0.01
ADAM_STEP = 10

SEC_A_Q, SEC_A_K, SEC_A_V, SEC_A_G = 0, 256, 512, 768
SEC_B_U, SEC_B_V, SEC_B_G = 1024, 1280, 1536
SEC_C_Q, SEC_C_K, SEC_C_V, SEC_C_G = 1792, 2048, 2304, 2560
SEC_D_Q, SEC_D_K, SEC_D_V, SEC_D_G = 2816, 3072, 3328, 3584
QKV_SECS = (SEC_A_Q, SEC_A_K, SEC_A_V, SEC_C_Q, SEC_C_K, SEC_C_V, SEC_D_Q, SEC_D_K, SEC_D_V)
GATE_SECS = (SEC_A_G, SEC_B_G, SEC_C_G, SEC_D_G)


def _dot(a, b):
    return jnp.dot(a, b, preferred_element_type=F32)


def _dot_nt(a, b):
    return lax.dot_general(a, b, (((1,), (1,)), ((), ())), preferred_element_type=F32)


def _dot_tn(a, b):
    return lax.dot_general(a, b, (((0,), (0,)), ((), ())), preferred_element_type=F32)


def _split2(x):
    hi = x.astype(BF16)
    lo = (x - hi.astype(F32)).astype(BF16)
    return hi, lo


def _split3(x):
    hi = x.astype(BF16)
    r = x - hi.astype(F32)
    mid = r.astype(BF16)
    lo = (r - mid.astype(F32)).astype(BF16)
    return hi, mid, lo


def _sigmoid(x):
    return 1.0 / (1.0 + jnp.exp(-x))


def _params(sem=None, vmem=VMEM_LIMIT):
    return pltpu.CompilerParams(dimension_semantics=sem, vmem_limit_bytes=vmem)


def _heads_to_lanes(ref):
    return jnp.concatenate([ref[h] for h in range(N_HEADS)], axis=1)


def inproj_fwd(x, g, wp):
    s = x.shape[0]
    tm = min(ROW_T, s)

    def body(x_ref, g_ref, w_ref, h_ref, qkv_ref, gates_ref, uv_ref, f_ref):
        xv = x_ref[...]
        r = lax.rsqrt(jnp.mean(xv * xv, axis=-1, keepdims=True) + EPS)
        h = (xv * r * g_ref[...]).astype(BF16)
        h_ref[...] = h
        for n, off in enumerate(QKV_SECS):
            p = _dot(h, w_ref[:, off:off + D_BRANCH])
            for hh in range(N_HEADS):
                qkv_ref[n, hh] = p[:, hh * HEAD_DIM:(hh + 1) * HEAD_DIM].astype(BF16)
        for n, off in enumerate(GATE_SECS):
            gates_ref[:, n * D_BRANCH:(n + 1) * D_BRANCH] = _dot(h, w_ref[:, off:off + D_BRANCH])
        uv_ref[...] = _dot(h, w_ref[:, SEC_B_U:SEC_B_U + 2 * D_BRANCH])
        f_ref[...] = _dot(h, w_ref[:, F_COL:F_COL + 128])

    return pl.pallas_call(
        body, name="inproj_fwd", grid=(s // tm,),
        in_specs=[pl.BlockSpec((tm, D_MODEL), lambda i: (i, 0)),
                  pl.BlockSpec((1, D_MODEL), lambda i: (0, 0)),
                  pl.BlockSpec((D_MODEL, N_PACK), lambda i: (0, 0))],
        out_specs=[pl.BlockSpec((tm, D_MODEL), lambda i: (i, 0)),
                   pl.BlockSpec((9, N_HEADS, tm, HEAD_DIM), lambda i: (0, 0, i, 0)),
                   pl.BlockSpec((tm, D_MODEL), lambda i: (i, 0)),
                   pl.BlockSpec((tm, 2 * D_BRANCH), lambda i: (i, 0)),
                   pl.BlockSpec((tm, 128), lambda i: (i, 0))],
        out_shape=[jax.ShapeDtypeStruct((s, D_MODEL), BF16),
                   jax.ShapeDtypeStruct((9, N_HEADS, s, HEAD_DIM), BF16),
                   jax.ShapeDtypeStruct((s, D_MODEL), F32),
                   jax.ShapeDtypeStruct((s, 2 * D_BRANCH), F32),
                   jax.ShapeDtypeStruct((s, 128), F32)],
        compiler_params=_params(("arbitrary",)),
    )(x, g, wp)


def inproj_bwd(dqkv, dgates, duv, dfp, wp, x, g, dres):
    s = x.shape[0]
    tm = min(ROW_T, s)

    def body(*refs):
        dq_refs = refs[:9]
        dgates_ref, duv_ref, dfp_ref, w_ref, x_ref, g_ref, dres_ref, dp_ref, dx_ref, dg_ref = refs[9:]
        i = pl.program_id(0)
        a_q, a_k, a_v, c_q, c_k, c_v, d_q, d_k, d_v = [_heads_to_lanes(r).astype(BF16) for r in dq_refs]
        dgt = dgates_ref[...]
        duv_b = duv_ref[...].astype(BF16)
        dp = jnp.concatenate(
            [a_q, a_k, a_v, dgt[:, 0:256], duv_b, dgt[:, 256:512], c_q, c_k, c_v, dgt[:, 512:768],
             d_q, d_k, d_v, dgt[:, 768:1024], dfp_ref[...].astype(BF16)], axis=1)
        dp_ref[...] = dp
        dh = _dot_nt(dp, w_ref[...])
        xv = x_ref[...]
        r = lax.rsqrt(jnp.mean(xv * xv, axis=-1, keepdims=True) + EPS)
        xn = xv * r
        u = dh * g_ref[...]
        dx_ref[...] = dres_ref[...] + r * (u - xn * jnp.mean(xn * u, axis=-1, keepdims=True))

        @pl.when(i == 0)
        def _():
            dg_ref[...] = jnp.zeros_like(dg_ref)

        dg_ref[...] += jnp.sum(dh * xn, axis=0, keepdims=True)

    head_spec = pl.BlockSpec((N_HEADS, tm, HEAD_DIM), lambda i: (0, i, 0))
    return pl.pallas_call(
        body, name="inproj_bwd", grid=(s // tm,),
        in_specs=[head_spec] * 9 + [
            pl.BlockSpec((tm, D_MODEL), lambda i: (i, 0)),
            pl.BlockSpec((tm, 2 * D_BRANCH), lambda i: (i, 0)),
            pl.BlockSpec((tm, 128), lambda i: (i, 0)),
            pl.BlockSpec((D_MODEL, N_PACK), lambda i: (0, 0)),
            pl.BlockSpec((tm, D_MODEL), lambda i: (i, 0)),
            pl.BlockSpec((1, D_MODEL), lambda i: (0, 0)),
            pl.BlockSpec((tm, D_MODEL), lambda i: (i, 0))],
        out_specs=[pl.BlockSpec((tm, N_PACK), lambda i: (i, 0)),
                   pl.BlockSpec((tm, D_MODEL), lambda i: (i, 0)),
                   pl.BlockSpec((1, D_MODEL), lambda i: (0, 0))],
        out_shape=[jax.ShapeDtypeStruct((s, N_PACK), BF16),
                   jax.ShapeDtypeStruct((s, D_MODEL), F32),
                   jax.ShapeDtypeStruct((1, D_MODEL), F32)],
        compiler_params=_params(("arbitrary",)),
    )(*dqkv, dgates, duv, dfp, wp, x, g, dres)


def weight_grad(a, b, name):
    s, m = a.shape
    n = b.shape[1]
    tm = min(ROW_T, s)
    tmm = 256
    nsteps = s // tm

    def body(a_ref, b_ref, o_ref):
        k = pl.program_id(1)

        @pl.when(k == 0)
        def _():
            o_ref[...] = jnp.zeros_like(o_ref)

        o_ref[...] += _dot_tn(a_ref[...], b_ref[...])

    return pl.pallas_call(
        body, name=name, grid=(m // tmm, nsteps),
        in_specs=[pl.BlockSpec((tm, tmm), lambda j, k: (k, j)),
                  pl.BlockSpec((tm, n), lambda j, k: (k, 0))],
        out_specs=pl.BlockSpec((tmm, n), lambda j, k: (j, 0)),
        out_shape=jax.ShapeDtypeStruct((m, n), F32),
        compiler_params=_params(("arbitrary", "arbitrary")),
    )(a, b)


def _a_specs(s):
    nq = s // A_TQ
    q_spec = pl.BlockSpec((None, None, A_TQ, HEAD_DIM), lambda h, i: (0, h, jnp.minimum(i, nq - 1), 0))
    kv_specs = [pl.BlockSpec((None, A_TQ, HEAD_DIM),
                             lambda h, i, m=m: (h, jnp.minimum(i + m, nq + A_NKB - 2), 0)) for m in range(A_NKB)]
    t_spec = pl.BlockSpec((None, A_TQ, A_BAND), lambda h, i: (h, 0, 0))
    return nq, q_spec, kv_specs, t_spec


def _a_scores(q_ref, k_refs, t_ref, i):
    qs = q_ref[...] * 0.125
    k = jnp.concatenate([r[...] for r in k_refs], axis=0)
    sc = _dot_nt(qs, k) + t_ref[...]
    col = lax.broadcasted_iota(jnp.int32, (A_TQ, A_BAND), 1)
    sc = jnp.where(col >= (A_BAND - A_TQ) - i * A_TQ, sc, NEG)
    return qs, k, sc


def mix_a_fwd(qkv, kpad, vpad, tbias):
    s = qkv.shape[2]
    nq, q_spec, kv_specs, t_spec = _a_specs(s)

    def body(*refs):
        q_ref = refs[0]
        k_refs = refs[1:1 + A_NKB]
        v_refs = refs[1 + A_NKB:1 + 2 * A_NKB]
        t_ref, o_ref, lse_ref = refs[1 + 2 * A_NKB:]
        i = pl.program_id(1)
        _, _, sc = _a_scores(q_ref, k_refs, t_ref, i)
        m = jnp.max(sc, axis=-1, keepdims=True)
        p = jnp.exp(sc - m)
        l = jnp.sum(p, axis=-1, keepdims=True)
        v = jnp.concatenate([r[...] for r in v_refs], axis=0)
        o_ref[...] = _dot(p.astype(BF16), v) / l
        lse_ref[...] = m + jnp.log(l)

    return pl.pallas_call(
        body, name="mix_a_fwd", grid=(N_HEADS, nq),
        in_specs=[q_spec] + kv_specs + kv_specs + [t_spec],
        out_specs=[pl.BlockSpec((None, A_TQ, HEAD_DIM), lambda h, i: (h, i, 0)),
                   pl.BlockSpec((None, A_TQ, 1), lambda h, i: (h, i, 0))],
        out_shape=[jax.ShapeDtypeStruct((N_HEADS, s, HEAD_DIM), F32),
                   jax.ShapeDtypeStruct((N_HEADS, s, 1), F32)],
        compiler_params=_params(("arbitrary", "arbitrary")),
    )(qkv, *([kpad] * A_NKB), *([vpad] * A_NKB), tbias)


def mix_a_bwd(qkv, kpad, vpad, tbias, do, o, lse):
    s = qkv.shape[2]
    nq, q_spec, kv_specs, t_spec = _a_specs(s)
    row_spec = lambda w: pl.BlockSpec((None, A_TQ, w), lambda h, i: (h, jnp.minimum(i, nq - 1), 0))
    keep = A_BAND - A_TQ

    def body(*refs):
        q_ref = refs[0]
        k_refs = refs[1:1 + A_NKB]
        v_refs = refs[1 + A_NKB:1 + 2 * A_NKB]
        t_ref, do_ref, o_ref, lse_ref, dq_ref, dk_ref, dv_ref, dt_ref, dk_win, dv_win = refs[1 + 2 * A_NKB:]
        i = pl.program_id(1)

        @pl.when(i == 0)
        def _():
            dk_win[...] = jnp.zeros_like(dk_win)
            dv_win[...] = jnp.zeros_like(dv_win)
            dt_ref[...] = jnp.zeros_like(dt_ref)

        @pl.when(i < nq)
        def _():
            qs, k, sc = _a_scores(q_ref, k_refs, t_ref, i)
            v = jnp.concatenate([r[...] for r in v_refs], axis=0)
            dob = do_ref[...]
            p = jnp.exp(sc - lse_ref[...])
            delta = jnp.sum(o_ref[...] * dob.astype(F32), axis=-1, keepdims=True)
            ds = p * (_dot_nt(dob, v) - delta)
            dsb = ds.astype(BF16)
            dq_ref[...] = _dot(dsb, k) * 0.125
            dk_win[...] += _dot_tn(dsb, qs)
            dv_win[...] += _dot_tn(p.astype(BF16), dob)
            dt_ref[...] += ds

        dk_ref[...] = dk_win[0:A_TQ, :]
        dv_ref[...] = dv_win[0:A_TQ, :]
        dk_rest = dk_win[A_TQ:A_BAND, :]
        dv_rest = dv_win[A_TQ:A_BAND, :]
        dk_win[0:keep, :] = dk_rest
        dv_win[0:keep, :] = dv_rest
        dk_win[keep:A_BAND, :] = jnp.zeros((A_TQ, HEAD_DIM), F32)
        dv_win[keep:A_BAND, :] = jnp.zeros((A_TQ, HEAD_DIM), F32)

    nsteps = nq + A_NKB - 1
    return pl.pallas_call(
        body, name="mix_a_bwd", grid=(N_HEADS, nsteps),
        in_specs=[q_spec] + kv_specs + kv_specs + [t_spec, row_spec(HEAD_DIM), row_spec(HEAD_DIM), row_spec(1)],
        out_specs=[row_spec(HEAD_DIM),
                   pl.BlockSpec((None, A_TQ, HEAD_DIM), lambda h, i: (h, i, 0)),
                   pl.BlockSpec((None, A_TQ, HEAD_DIM), lambda h, i: (h, i, 0)),
                   t_spec],
        out_shape=[jax.ShapeDtypeStruct((N_HEADS, s, HEAD_DIM), F32),
                   jax.ShapeDtypeStruct((N_HEADS, s + keep, HEAD_DIM), F32),
                   jax.ShapeDtypeStruct((N_HEADS, s + keep, HEAD_DIM), F32),
                   jax.ShapeDtypeStruct((N_HEADS, A_TQ, A_BAND), F32)],
        scratch_shapes=[pltpu.VMEM((A_BAND, HEAD_DIM), F32), pltpu.VMEM((A_BAND, HEAD_DIM), F32)],
        compiler_params=_params(("arbitrary", "arbitrary")),
    )(qkv, *([kpad] * A_NKB), *([vpad] * A_NKB), tbias, do, o, lse)


def relbias_tile(rel_bias, relmat):
    nrel = 2 * MAX_REL + 1

    def body(rb_ref, rel_ref, o_ref):
        rel = rel_ref[...]
        o_ref[...] = jnp.full(o_ref.shape, NEG, F32)

        def step(r, carry):
            hit = rel == r
            for h in range(N_HEADS):
                o_ref[h] = jnp.where(hit, rb_ref[h, r], o_ref[h])
            return carry

        lax.fori_loop(0, nrel, step, 0)

    return pl.pallas_call(
        body, name="relbias_tile",
        in_specs=[pl.BlockSpec(memory_space=pltpu.SMEM), pl.BlockSpec(memory_space=pltpu.VMEM)],
        out_specs=pl.BlockSpec(memory_space=pltpu.VMEM),
        out_shape=jax.ShapeDtypeStruct((N_HEADS, A_TQ, A_BAND), F32),
        compiler_params=_params(),
    )(rel_bias, relmat)


def relbias_grad(dt, relmat):
    nrel = 2 * MAX_REL + 1

    def body(dt_ref, rel_ref, o_ref):
        rel = rel_ref[...]
        lane = lax.broadcasted_iota(jnp.int32, (8, 384), 1)
        row = lax.broadcasted_iota(jnp.int32, (8, 384), 0)

        def step(r, acc):
            hit = rel == r
            for h in range(N_HEADS):
                val = jnp.sum(jnp.where(hit, dt_ref[h], 0.0))
                acc = jnp.where((lane == r) & (row == h), val, acc)
            return acc

        o_ref[...] = lax.fori_loop(0, nrel, step, jnp.zeros((8, 384), F32))

    return pl.pallas_call(
        body, name="relbias_grad",
        out_shape=jax.ShapeDtypeStruct((8, 384), F32),
        compiler_params=_params(),
    )(dt, relmat)


def _b_norm(v, gain):
    mu = jnp.mean(v, axis=-1, keepdims=True)
    xc = v - mu
    rstd = lax.rsqrt(jnp.mean(xc * xc, axis=-1, keepdims=True) + EPS)
    xhat = xc * rstd
    return xhat, rstd, xhat * gain


def _tril_mask():
    t = lax.broadcasted_iota(jnp.int32, (SG_CHUNK, SG_CHUNK), 0)
    u = lax.broadcasted_iota(jnp.int32, (SG_CHUNK, SG_CHUNK), 1)
    return u <= t


def mix_b_fwd(uv, gain, w_s, b_col):
    s = uv.shape[0]
    tm = min(ROW_T, s)

    def body(uv_ref, gain_ref, w_ref, b_ref, y_ref):
        tril = _tril_mask()
        ws = [jnp.where(tril, w_ref[g], 0.0).astype(BF16) for g in range(N_HEADS)]
        for c in range(tm // SG_CHUNK):
            rows = slice(c * SG_CHUNK, (c + 1) * SG_CHUNK)
            u = uv_ref[rows, 0:D_BRANCH]
            _, _, vn = _b_norm(uv_ref[rows, D_BRANCH:2 * D_BRANCH], gain_ref[...])
            vnb = vn.astype(BF16)
            outs = []
            for g in range(N_HEADS):
                cols = slice(g * HEAD_DIM, (g + 1) * HEAD_DIM)
                mixed = _dot(ws[g], vnb[:, cols]) + b_ref[g]
                outs.append(u[:, cols] * mixed)
            y_ref[rows, :] = jnp.concatenate(outs, axis=1)

    return pl.pallas_call(
        body, name="mix_b_fwd", grid=(s // tm,),
        in_specs=[pl.BlockSpec((tm, 2 * D_BRANCH), lambda i: (i, 0)),
                  pl.BlockSpec((1, D_BRANCH), lambda i: (0, 0)),
                  pl.BlockSpec((N_HEADS, SG_CHUNK, SG_CHUNK), lambda i: (0, 0, 0)),
                  pl.BlockSpec((N_HEADS, SG_CHUNK, 1), lambda i: (0, 0, 0))],
        out_specs=pl.BlockSpec((tm, D_BRANCH), lambda i: (i, 0)),
        out_shape=jax.ShapeDtypeStruct((s, D_BRANCH), F32),
        compiler_params=_params(("arbitrary",)),
    )(uv, gain, w_s, b_col)


def mix_b_bwd(uv, gain, w_s, b_col, dy):
    s = uv.shape[0]
    tm = min(ROW_T, s)

    def body(uv_ref, gain_ref, w_ref, b_ref, dy_ref, duv_ref, dw_ref, db_ref, dgain_ref):
        i = pl.program_id(0)

        @pl.when(i == 0)
        def _():
            dw_ref[...] = jnp.zeros_like(dw_ref)
            db_ref[...] = jnp.zeros_like(db_ref)
            dgain_ref[...] = jnp.zeros_like(dgain_ref)

        tril = _tril_mask()
        ws = [jnp.where(tril, w_ref[g], 0.0).astype(BF16) for g in range(N_HEADS)]
        gain_v = gain_ref[...]
        for c in range(tm // SG_CHUNK):
            rows = slice(c * SG_CHUNK, (c + 1) * SG_CHUNK)
            u = uv_ref[rows, 0:D_BRANCH]
            xhat, rstd, vn = _b_norm(uv_ref[rows, D_BRANCH:2 * D_BRANCH], gain_v)
            vnb = vn.astype(BF16)
            dyv = dy_ref[rows, :]
            dus, dvns = [], []
            for g in range(N_HEADS):
                cols = slice(g * HEAD_DIM, (g + 1) * HEAD_DIM)
                mixed = _dot(ws[g], vnb[:, cols]) + b_ref[g]
                dus.append(dyv[:, cols] * mixed)
                dmixed = dyv[:, cols] * u[:, cols]
                dmb = dmixed.astype(BF16)
                db_ref[g] += jnp.sum(dmixed, axis=-1, keepdims=True)
                dw_ref[g] += jnp.where(tril, _dot_nt(dmb, vnb[:, cols]), 0.0)
                dvns.append(_dot_tn(ws[g], dmb))
            dvn = jnp.concatenate(dvns, axis=1)
            dgain_ref[...] += jnp.sum(dvn * xhat, axis=0, keepdims=True)
            dxh = dvn * gain_v
            dv = rstd * (dxh - jnp.mean(dxh, axis=-1, keepdims=True)
                         - xhat * jnp.mean(dxh * xhat, axis=-1, keepdims=True))
            duv_ref[rows, :] = jnp.concatenate(dus + [dv], axis=1)

    return pl.pallas_call(
        body, name="mix_b_bwd", grid=(s // tm,),
        in_specs=[pl.BlockSpec((tm, 2 * D_BRANCH), lambda i: (i, 0)),
                  pl.BlockSpec((1, D_BRANCH), lambda i: (0, 0)),
                  pl.BlockSpec((N_HEADS, SG_CHUNK, SG_CHUNK), lambda i: (0, 0, 0)),
                  pl.BlockSpec((N_HEADS, SG_CHUNK, 1), lambda i: (0, 0, 0)),
                  pl.BlockSpec((tm, D_BRANCH), lambda i: (i, 0))],
        out_specs=[pl.BlockSpec((tm, 2 * D_BRANCH), lambda i: (i, 0)),
                   pl.BlockSpec((N_HEADS, SG_CHUNK, SG_CHUNK), lambda i: (0, 0, 0)),
                   pl.BlockSpec((N_HEADS, SG_CHUNK, 1), lambda i: (0, 0, 0)),
                   pl.BlockSpec((1, D_BRANCH), lambda i: (0, 0))],
        out_shape=[jax.ShapeDtypeStruct((s, 2 * D_BRANCH), F32),
                   jax.ShapeDtypeStruct((N_HEADS, SG_CHUNK, SG_CHUNK), F32),
                   jax.ShapeDtypeStruct((N_HEADS, SG_CHUNK, 1), F32),
                   jax.ShapeDtypeStruct((1, D_BRANCH), F32)],
        compiler_params=_params(("arbitrary",)),
    )(uv, gain, w_s, b_col, dy)


def _scan_mats(nrow):
    a = lax.broadcasted_iota(jnp.int32, (128, 128), 0)
    b = lax.broadcasted_iota(jnp.int32, (128, 128), 1)
    r = lax.broadcasted_iota(jnp.int32, (nrow, nrow), 0)
    c = lax.broadcasted_iota(jnp.int32, (nrow, nrow), 1)
    nb = nrow // N_HEADS
    same = (r // nb) == (c // nb)
    return a, b, r, c, same


def _exact_dot(x, m):
    hi, mid, lo = _split3(x)
    return _dot(hi, m) + _dot(mid, m) + _dot(lo, m)


def _exact_dot_left(m, x):
    hi, mid, lo = _split3(x)
    return _dot(m, hi) + _dot(m, mid) + _dot(m, lo)


def fox_gate_fwd(ft, bcol):
    nrow = ft.shape[0]

    def body(f_ref, b_ref, c_ref):
        z = f_ref[...] + b_ref[...]
        ls = jnp.minimum(z, 0.0) - jnp.log(1.0 + jnp.exp(-jnp.abs(z)))
        a, b, r, c, same = _scan_mats(nrow)
        within = _exact_dot(ls, (a <= b).astype(BF16))
        tot = jnp.broadcast_to(within[:, 127:128], within.shape)
        before = _exact_dot_left((same & (c < r)).astype(BF16), tot)
        c_ref[...] = within + before

    return pl.pallas_call(
        body, name="fox_gate_fwd",
        out_shape=jax.ShapeDtypeStruct((nrow, 128), F32),
        compiler_params=_params(),
    )(ft, bcol)


def fox_gate_bwd(ft, bcol, dc):
    nrow = ft.shape[0]

    def body(f_ref, b_ref, dc_ref, df_ref, db_ref):
        a, b, r, c, same = _scan_mats(nrow)
        dcv = dc_ref[...]
        within = _exact_dot(dcv, (a >= b).astype(BF16))
        tot = jnp.broadcast_to(within[:, 0:1], within.shape)
        after = _exact_dot_left((same & (c > r)).astype(BF16), tot)
        dls = within + after
        z = f_ref[...] + b_ref[...]
        dz = dls * _sigmoid(-z)
        df_ref[...] = dz
        rs = jnp.broadcast_to(jnp.sum(dz, axis=-1, keepdims=True), dz.shape)
        hr = lax.broadcasted_iota(jnp.int32, (8, nrow), 0)
        hc = lax.broadcasted_iota(jnp.int32, (8, nrow), 1)
        db_ref[...] = _exact_dot_left((hr == hc // (nrow // N_HEADS)).astype(BF16), rs)

    return pl.pallas_call(
        body, name="fox_gate_bwd",
        out_shape=[jax.ShapeDtypeStruct((nrow, 128), F32), jax.ShapeDtypeStruct((8, 128), F32)],
        compiler_params=_params(),
    )(ft, bcol, dc)


def _att_specs(s, qi, ki, vi):
    t = ATT_T
    q_spec = pl.BlockSpec((None, None, t, HEAD_DIM), lambda h, i: (qi, h, i, 0))
    k_spec = pl.BlockSpec((None, None, s, HEAD_DIM), lambda h, i: (ki, h, 0, 0))
    v_spec = pl.BlockSpec((None, None, s, HEAD_DIM), lambda h, i: (vi, h, 0, 0))
    row_spec = lambda w: pl.BlockSpec((None, t, w), lambda h, i: (h, i, 0))
    return q_spec, k_spec, v_spec, row_spec


def _causal(strict):
    row = lax.broadcasted_iota(jnp.int32, (ATT_T, ATT_T), 0)
    col = lax.broadcasted_iota(jnp.int32, (ATT_T, ATT_T), 1)
    return (col < row) if strict else (col <= row)


def _gate_row(cr_ref, kb, g):
    if g == 1:
        return cr_ref[kb]
    return jnp.concatenate([cr_ref[kb + n] for n in range(g)], axis=1)


def _fox_walk(i, carry, tile, alive):
    g = FOX_WIDE
    nwide = i // g
    carry = tile(i, 1, carry, True)
    carry = lax.fori_loop(0, i - nwide * g, lambda n, c: tile(i - 1 - n, 1, c, False), carry)

    def cond(state):
        return jnp.logical_and(state[0] >= 0, state[1] > 0)

    def step(state):
        n = state[0]
        c = tile(n * g, g, state[2:], False)
        return (n - 1, alive(n * g, c)) + tuple(c)

    out = lax.while_loop(cond, step, (nwide - 1, alive(nwide * g, carry)) + tuple(carry))
    return out[2:]


def _fox_reach(qs, k_ref, kmax_ref, cc, i):
    s = k_ref.shape[0]
    rows = 4 * ATT_T

    @pl.when(i == 0)
    def _():
        def chunk(n, mx):
            kc = k_ref[pl.ds(pl.multiple_of(n * rows, rows), rows), :].astype(F32)
            return jnp.maximum(mx, jnp.max(jnp.sum(kc * kc, axis=-1, keepdims=True)))

        kmax_ref[0] = jnp.sqrt(lax.fori_loop(0, s // rows, chunk, jnp.float32(0.0)))

    qf = qs.astype(F32)
    return jnp.sqrt(jnp.sum(qf * qf, axis=-1, keepdims=True)) * kmax_ref[0] + cc


def fox_fwd(qkv, c_col, c_row):
    s = qkv.shape[2]
    t = ATT_T
    nq = s // t
    q_spec, k_spec, v_spec, row_spec = _att_specs(s, 3, 4, 5)

    def body(q_ref, k_ref, v_ref, cc_ref, cr_ref, o_ref, lse_ref, kmax_ref):
        i = pl.program_id(1)
        qs = q_ref[...] * 0.125
        cc = cc_ref[...]
        reach = _fox_reach(qs, k_ref, kmax_ref, cc, i)

        def alive(kb, carry):
            return (jnp.max(reach - cr_ref[kb][:, 0:1] - carry[0]) > FOX_DEAD).astype(jnp.int32)

        def tile(kb, g, carry, masked):
            m, l, acc = carry
            k0 = pl.multiple_of(kb * t, t)
            sc = _dot_nt(qs, k_ref[pl.ds(k0, g * t), :]) + (cc - _gate_row(cr_ref, kb, g))
            if masked:
                sc = jnp.where(_causal(False), sc, NEG)
            m_new = jnp.maximum(m, jnp.max(sc, axis=-1, keepdims=True))
            alpha = jnp.exp(m - m_new)
            p = jnp.exp(sc - m_new)
            l = alpha * l + jnp.sum(p, axis=-1, keepdims=True)
            p_hi, p_lo = _split2(p)
            v = v_ref[pl.ds(k0, g * t), :]
            acc = alpha * acc + (_dot(p_hi, v) + _dot(p_lo, v))
            return m_new, l, acc

        init = (jnp.full((t, 1), NEG, F32), jnp.zeros((t, 1), F32), jnp.zeros((t, HEAD_DIM), F32))
        m, l, acc = _fox_walk(i, init, tile, alive)
        o_ref[...] = acc / l
        lse_ref[...] = m + jnp.log(l)

    return pl.pallas_call(
        body, name="fox_fwd", grid=(N_HEADS, nq),
        in_specs=[q_spec, k_spec, v_spec, row_spec(1),
                  pl.BlockSpec((None, nq, 1, t), lambda h, i: (h, 0, 0, 0))],
        out_specs=[row_spec(HEAD_DIM), row_spec(1)],
        out_shape=[jax.ShapeDtypeStruct((N_HEADS, s, HEAD_DIM), F32),
                   jax.ShapeDtypeStruct((N_HEADS, s, 1), F32)],
        scratch_shapes=[pltpu.SMEM((1,), F32)],
        compiler_params=_params(("arbitrary", "arbitrary")),
    )(qkv, qkv, qkv, c_col, c_row)


def fox_bwd(qkv, c_col, c_row, do, o, lse):
    s = qkv.shape[2]
    t = ATT_T
    nq = s // t
    q_spec, k_spec, v_spec, row_spec = _att_specs(s, 3, 4, 5)
    any_spec = pl.BlockSpec(memory_space=pl.ANY)

    def body(q_ref, k_ref, v_ref, cc_ref, cr_ref, do_ref, o_ref, lse_ref,
             dq_ref, dk_hbm, dv_hbm, dc_ref, dk_acc, dv_acc, kmax_ref):
        h = pl.program_id(0)
        i = pl.program_id(1)

        @pl.when(i == 0)
        def _():
            dk_acc[...] = jnp.zeros_like(dk_acc)
            dv_acc[...] = jnp.zeros_like(dv_acc)
            dc_ref[...] = jnp.zeros_like(dc_ref)

        qs = q_ref[...] * 0.125
        dob = do_ref[...]
        delta = jnp.sum(o_ref[...] * dob.astype(F32), axis=-1, keepdims=True)
        lse = lse_ref[...]
        cc = cc_ref[...]
        margin = _fox_reach(qs, k_ref, kmax_ref, cc, i) - lse

        def alive(kb, carry):
            return (jnp.max(margin - cr_ref[kb][:, 0:1]) > FOX_DEAD).astype(jnp.int32)

        def tile(kb, g, carry, masked):
            dq, = carry
            k0 = pl.multiple_of(kb * t, t)
            k = k_ref[pl.ds(k0, g * t), :]
            sc = _dot_nt(qs, k) + (cc - _gate_row(cr_ref, kb, g))
            if masked:
                sc = jnp.where(_causal(False), sc, NEG)
            p = jnp.exp(sc - lse)
            ds = p * (_dot_nt(dob, v_ref[pl.ds(k0, g * t), :]) - delta)
            dsb = ds.astype(BF16)
            dk_acc[pl.ds(k0, g * t), :] += _dot_tn(dsb, qs)
            dv_acc[pl.ds(k0, g * t), :] += _dot_tn(p.astype(BF16), dob)
            dcs = -jnp.sum(ds, axis=0, keepdims=True)
            for n in range(g):
                dc_ref[kb + n] += dcs[:, n * t:(n + 1) * t]
            return (dq + _dot(dsb, k),)

        dq, = _fox_walk(i, (jnp.zeros((t, HEAD_DIM), F32),), tile, alive)
        dq_ref[...] = dq * 0.125

        @pl.when(i == nq - 1)
        def _():
            pltpu.sync_copy(dk_acc, dk_hbm.at[h])
            pltpu.sync_copy(dv_acc, dv_hbm.at[h])

    return pl.pallas_call(
        body, name="fox_bwd", grid=(N_HEADS, nq),
        in_specs=[q_spec, k_spec, v_spec, row_spec(1),
                  pl.BlockSpec((None, nq, 1, t), lambda h, i: (h, 0, 0, 0)),
                  row_spec(HEAD_DIM), row_spec(HEAD_DIM), row_spec(1)],
        out_specs=[row_spec(HEAD_DIM), any_spec, any_spec,
                   pl.BlockSpec((None, nq, 1, t), lambda h, i: (h, 0, 0, 0))],
        out_shape=[jax.ShapeDtypeStruct((N_HEADS, s, HEAD_DIM), F32),
                   jax.ShapeDtypeStruct((N_HEADS, s, HEAD_DIM), F32),
                   jax.ShapeDtypeStruct((N_HEADS, s, HEAD_DIM), F32),
                   jax.ShapeDtypeStruct((N_HEADS, nq, 1, t), F32)],
        scratch_shapes=[pltpu.VMEM((s, HEAD_DIM), F32), pltpu.VMEM((s, HEAD_DIM), F32), pltpu.SMEM((1,), F32)],
        compiler_params=_params(("arbitrary", "arbitrary")),
    )(qkv, qkv, qkv, c_col, c_row, do, o, lse)


def _sb_tile(qs, k, run, masked):
    z = _dot_nt(qs, k)
    sp = jnp.log(1.0 + jnp.exp(-jnp.abs(z)))
    ls = jnp.minimum(z, 0.0) - sp
    lm = -jnp.maximum(z, 0.0) - sp
    if masked:
        valid = _causal(True)
        lm = jnp.where(valid, lm, 0.0)
    row = lax.broadcasted_iota(jnp.int32, (ATT_T, ATT_T), 0)
    col = lax.broadcasted_iota(jnp.int32, (ATT_T, ATT_T), 1)
    later = (row > col).astype(BF16)
    hi, lo = _split2(lm)
    between = run + _dot(hi, later) + _dot(lo, later)
    a = jnp.exp(ls + between)
    if masked:
        a = jnp.where(valid, a, 0.0)
    return ls, lm, a


def _sb_walk(i, carry, tile):
    def alive_of(c):
        return (jnp.max(c[0]) > SB_DEAD).astype(jnp.int32)

    def cond(state):
        n, alive = state[0], state[1]
        return jnp.logical_and(n < i, alive > 0)

    def step(state):
        n = state[0]
        c = tile(i - 1 - n, state[2:], False)
        return (n + 1, alive_of(c)) + tuple(c)

    out = lax.while_loop(cond, step, (jnp.int32(0), alive_of(carry)) + tuple(carry))
    return out[2:]


def sb_fwd(qkv):
    s = qkv.shape[2]
    t = ATT_T
    nq = s // t
    q_spec, k_spec, v_spec, row_spec = _att_specs(s, 6, 7, 8)

    def body(q_ref, k_ref, v_ref, o_ref):
        i = pl.program_id(1)
        qs = q_ref[...] * 0.125

        def tile(kb, carry, masked):
            run, acc = carry
            k0 = pl.multiple_of(kb * t, t)
            _, lm, a = _sb_tile(qs, k_ref[pl.ds(k0, t), :], run, masked)
            acc = acc + _dot(a.astype(BF16), v_ref[pl.ds(k0, t), :])
            return run + jnp.sum(lm, axis=-1, keepdims=True), acc

        carry = tile(i, (jnp.zeros((t, 1), F32), jnp.zeros((t, HEAD_DIM), F32)), True)
        _, acc = _sb_walk(i, carry, tile)
        o_ref[...] = acc

    return pl.pallas_call(
        body, name="sb_fwd", grid=(N_HEADS, nq),
        in_specs=[q_spec, k_spec, v_spec],
        out_specs=row_spec(HEAD_DIM),
        out_shape=jax.ShapeDtypeStruct((N_HEADS, s, HEAD_DIM), F32),
        compiler_params=_params(("arbitrary", "arbitrary")),
    )(qkv, qkv, qkv)


def sb_bwd(qkv, do, o):
    s = qkv.shape[2]
    t = ATT_T
    nq = s // t
    q_spec, k_spec, v_spec, row_spec = _att_specs(s, 6, 7, 8)
    any_spec = pl.BlockSpec(memory_space=pl.ANY)

    def body(q_ref, k_ref, v_ref, do_ref, o_ref, dq_ref, dk_hbm, dv_hbm, dk_acc, dv_acc):
        h = pl.program_id(0)
        i = pl.program_id(1)

        @pl.when(i == 0)
        def _():
            dk_acc[...] = jnp.zeros_like(dk_acc)
            dv_acc[...] = jnp.zeros_like(dv_acc)

        qs = q_ref[...] * 0.125
        dob = do_ref[...]
        tot = jnp.sum(o_ref[...] * dob.astype(F32), axis=-1, keepdims=True)

        def tile(kb, carry, masked):
            run, run_g, dq = carry
            k0 = pl.multiple_of(kb * t, t)
            k = k_ref[pl.ds(k0, t), :]
            ls, lm, a = _sb_tile(qs, k, run, masked)
            ab = a.astype(BF16)
            g = ab.astype(F32) * _dot_nt(dob, v_ref[pl.ds(k0, t), :])
            row = lax.broadcasted_iota(jnp.int32, (t, t), 0)
            col = lax.broadcasted_iota(jnp.int32, (t, t), 1)
            from_here = (row >= col).astype(BF16)
            hi, lo = _split2(g)
            g_right = run_g + _dot(hi, from_here) + _dot(lo, from_here)
            g_left = tot - g_right
            dz = g - jnp.exp(ls) * (g + g_left)
            if masked:
                dz = jnp.where(_causal(True), dz, 0.0)
            dzb = dz.astype(BF16)
            dk_acc[pl.ds(k0, t), :] += _dot_tn(dzb, qs)
            dv_acc[pl.ds(k0, t), :] += _dot_tn(ab, dob)
            return (run + jnp.sum(lm, axis=-1, keepdims=True),
                    run_g + jnp.sum(g, axis=-1, keepdims=True),
                    dq + _dot(dzb, k))

        zero = jnp.zeros((t, 1), F32)
        carry = tile(i, (zero, zero, jnp.zeros((t, HEAD_DIM), F32)), True)
        _, _, dq = _sb_walk(i, carry, tile)
        dq_ref[...] = dq * 0.125

        @pl.when(i == nq - 1)
        def _():
            pltpu.sync_copy(dk_acc, dk_hbm.at[h])
            pltpu.sync_copy(dv_acc, dv_hbm.at[h])

    return pl.pallas_call(
        body, name="sb_bwd", grid=(N_HEADS, nq),
        in_specs=[q_spec, k_spec, v_spec, row_spec(HEAD_DIM), row_spec(HEAD_DIM)],
        out_specs=[row_spec(HEAD_DIM), any_spec, any_spec],
        out_shape=[jax.ShapeDtypeStruct((N_HEADS, s, HEAD_DIM), F32)] * 3,
        scratch_shapes=[pltpu.VMEM((s, HEAD_DIM), F32), pltpu.VMEM((s, HEAD_DIM), F32)],
        compiler_params=_params(("arbitrary", "arbitrary")),
    )(qkv, qkv, qkv, do, o)


def _branch_inputs(refs, br):
    ya_ref, yb_ref, yc_ref, yd_ref = refs
    if br == 1:
        return yb_ref[...]
    return _heads_to_lanes((ya_ref, None, yc_ref, yd_ref)[br])


def outproj_fwd(x, ya, yb, yc, yd, gates, bg, wout):
    s = x.shape[0]
    tm = min(ROW_T, s)

    def body(x_ref, ya_ref, yb_ref, yc_ref, yd_ref, gates_ref, bg_ref, w_ref, out_ref):
        pieces = []
        for br in range(4):
            cols = slice(br * D_BRANCH, (br + 1) * D_BRANCH)
            y = _branch_inputs((ya_ref, yb_ref, yc_ref, yd_ref), br)
            r = lax.rsqrt(jnp.mean(y * y, axis=-1, keepdims=True) + EPS)
            gt = gates_ref[:, cols]
            pieces.append((y * r * bg_ref[:, cols]) * (gt * _sigmoid(gt)))
        merged = jnp.concatenate(pieces, axis=1).astype(BF16)
        out_ref[...] = x_ref[...] + _dot(merged, w_ref[...])

    head_spec = pl.BlockSpec((N_HEADS, tm, HEAD_DIM), lambda i: (0, i, 0))
    return pl.pallas_call(
        body, name="outproj_fwd", grid=(s // tm,),
        in_specs=[pl.BlockSpec((tm, D_MODEL), lambda i: (i, 0)),
                  head_spec, pl.BlockSpec((tm, D_BRANCH), lambda i: (i, 0)), head_spec, head_spec,
                  pl.BlockSpec((tm, D_MODEL), lambda i: (i, 0)),
                  pl.BlockSpec((1, D_MODEL), lambda i: (0, 0)),
                  pl.BlockSpec((D_MODEL, D_MODEL), lambda i: (0, 0))],
        out_specs=pl.BlockSpec((tm, D_MODEL), lambda i: (i, 0)),
        out_shape=jax.ShapeDtypeStruct((s, D_MODEL), F32),
        compiler_params=_params(("arbitrary",)),
    )(x, ya, yb, yc, yd, gates, bg, wout)


def outproj_bwd(dout, ya, yb, yc, yd, gates, bg, wout):
    s = dout.shape[0]
    tm = min(ROW_T, s)

    def body(dout_ref, ya_ref, yb_ref, yc_ref, yd_ref, gates_ref, bg_ref, w_ref,
             dya_ref, dyb_ref, dyc_ref, dyd_ref, dgates_ref, dbg_ref, dw_ref):
        i = pl.program_id(0)

        @pl.when(i == 0)
        def _():
            dbg_ref[...] = jnp.zeros_like(dbg_ref)
            dw_ref[...] = jnp.zeros_like(dw_ref)

        doutb = dout_ref[...].astype(BF16)
        dmerged = _dot_nt(doutb, w_ref[...])
        pieces = []
        for br in range(4):
            cols = slice(br * D_BRANCH, (br + 1) * D_BRANCH)
            y = _branch_inputs((ya_ref, yb_ref, yc_ref, yd_ref), br)
            r = lax.rsqrt(jnp.mean(y * y, axis=-1, keepdims=True) + EPS)
            yn = y * r
            bgv = bg_ref[:, cols]
            gt = gates_ref[:, cols]
            sig = _sigmoid(gt)
            act = gt * sig
            n = yn * bgv
            pieces.append(n * act)
            dm = dmerged[:, cols]
            dn = dm * act
            dgates_ref[:, cols] = (dm * n * (sig * (1.0 + gt * (1.0 - sig)))).astype(BF16)
            dbg_ref[:, cols] += jnp.sum(dn * yn, axis=0, keepdims=True)
            u = dn * bgv
            dy = r * (u - yn * jnp.mean(yn * u, axis=-1, keepdims=True))
            if br == 1:
                dyb_ref[...] = dy
            else:
                dref = (dya_ref, None, dyc_ref, dyd_ref)[br]
                for hh in range(N_HEADS):
                    dref[hh] = dy[:, hh * HEAD_DIM:(hh + 1) * HEAD_DIM].astype(BF16)
        merged = jnp.concatenate(pieces, axis=1).astype(BF16)
        dw_ref[...] += _dot_tn(merged, doutb)

    head_spec = pl.BlockSpec((N_HEADS, tm, HEAD_DIM), lambda i: (0, i, 0))
    head_shape = jax.ShapeDtypeStruct((N_HEADS, s, HEAD_DIM), BF16)
    return pl.pallas_call(
        body, name="outproj_bwd", grid=(s // tm,),
        in_specs=[pl.BlockSpec((tm, D_MODEL), lambda i: (i, 0)),
                  head_spec, pl.BlockSpec((tm, D_BRANCH), lambda i: (i, 0)), head_spec, head_spec,
                  pl.BlockSpec((tm, D_MODEL), lambda i: (i, 0)),
                  pl.BlockSpec((1, D_MODEL), lambda i: (0, 0)),
                  pl.BlockSpec((D_MODEL, D_MODEL), lambda i: (0, 0))],
        out_specs=[head_spec, pl.BlockSpec((tm, D_BRANCH), lambda i: (i, 0)), head_spec, head_spec,
                   pl.BlockSpec((tm, D_MODEL), lambda i: (i, 0)),
                   pl.BlockSpec((1, D_MODEL), lambda i: (0, 0)),
                   pl.BlockSpec((D_MODEL, D_MODEL), lambda i: (0, 0))],
        out_shape=[head_shape, jax.ShapeDtypeStruct((s, D_BRANCH), F32), head_shape, head_shape,
                   jax.ShapeDtypeStruct((s, D_MODEL), BF16),
                   jax.ShapeDtypeStruct((1, D_MODEL), F32),
                   jax.ShapeDtypeStruct((D_MODEL, D_MODEL), F32)],
        compiler_params=_params(("arbitrary",)),
    )(dout, ya, yb, yc, yd, gates, bg, wout)


def final_loss(x, tgt, g):
    s = x.shape[0]
    tm = min(ROW_T, s)

    def body(x_ref, t_ref, g_ref, loss_ref, dx_ref, dg_ref):
        i = pl.program_id(0)

        @pl.when(i == 0)
        def _():
            loss_ref[...] = jnp.zeros_like(loss_ref)
            dg_ref[...] = jnp.zeros_like(dg_ref)

        xv = x_ref[...]
        gv = g_ref[...]
        r = lax.rsqrt(jnp.mean(xv * xv, axis=-1, keepdims=True) + EPS)
        xn = xv * r
        err = xn * gv - t_ref[...]
        loss_ref[...] += jnp.sum(err * err) * (0.5 / D_MODEL)
        dy = err * (1.0 / D_MODEL)
        u = dy * gv
        dx_ref[...] = r * (u - xn * jnp.mean(xn * u, axis=-1, keepdims=True))
        dg_ref[...] += jnp.sum(dy * xn, axis=0, keepdims=True)

    return pl.pallas_call(
        body, name="final_loss", grid=(s // tm,),
        in_specs=[pl.BlockSpec((tm, D_MODEL), lambda i: (i, 0)),
                  pl.BlockSpec((tm, D_MODEL), lambda i: (i, 0)),
                  pl.BlockSpec((1, D_MODEL), lambda i: (0, 0))],
        out_specs=[pl.BlockSpec((1, 128), lambda i: (0, 0)),
                   pl.BlockSpec((tm, D_MODEL), lambda i: (i, 0)),
                   pl.BlockSpec((1, D_MODEL), lambda i: (0, 0))],
        out_shape=[jax.ShapeDtypeStruct((1, 128), F32),
                   jax.ShapeDtypeStruct((s, D_MODEL), F32),
                   jax.ShapeDtypeStruct((1, D_MODEL), F32)],
        compiler_params=_params(("arbitrary",)),
    )(x, tgt, g)


def _rel_index():
    i = np.arange(A_TQ)[:, None]
    j = np.arange(A_BAND)[None, :]
    rel = np.clip(i - j + (A_BAND - A_TQ), -MAX_REL, MAX_REL) + MAX_REL
    dchunk = i // CHUNK + LOOKBACK - j // CHUNK
    valid = (dchunk >= 0) & (dchunk <= LOOKBACK)
    return jnp.asarray(np.where(valid, rel, -1).astype(np.int32))


def _layer_consts(p):
    tbias = relbias_tile(p["rel_bias"], _rel_index())
    return dict(
        norm_g=p["norm_g"].reshape(1, D_MODEL),
        v_gain=p["v_gain"].reshape(1, D_BRANCH),
        b_col=p["b_s"].reshape(N_HEADS, SG_CHUNK, 1),
        bg=p["branch_gain"].reshape(1, D_MODEL),
        tbias=tbias,
    )


def _gate_layout(fp, b_f, s):
    nb = s // 128
    ft = fp[:, :N_HEADS].T.reshape(N_HEADS * nb, 128)
    bcol = jnp.repeat(b_f, nb).reshape(N_HEADS * nb, 1)
    return ft, bcol


def layer_fwd(x, p):
    s = x.shape[0]
    c = _layer_consts(p)
    h, qkv, gates, uv, fp = inproj_fwd(x, c["norm_g"], p["wp"])
    keep = A_BAND - A_TQ
    kpad = jnp.pad(qkv[1], ((0, 0), (keep, 0), (0, 0)))
    vpad = jnp.pad(qkv[2], ((0, 0), (keep, 0), (0, 0)))
    ya, lse_a = mix_a_fwd(qkv, kpad, vpad, c["tbias"])
    yb = mix_b_fwd(uv, c["v_gain"], p["w_s"], c["b_col"])
    ft, bcol = _gate_layout(fp, p["b_f"], s)
    cum = fox_gate_fwd(ft, bcol).reshape(N_HEADS, s)
    c_col = cum.reshape(N_HEADS, s, 1)
    c_row = cum.reshape(N_HEADS, s // ATT_T, 1, ATT_T)
    yc, lse_c = fox_fwd(qkv, c_col, c_row)
    yd = sb_fwd(qkv)
    out = outproj_fwd(x, ya, yb, yc, yd, gates, c["bg"], p["wout"])
    saved = dict(consts=c, x=x, h=h, qkv=qkv, gates=gates, uv=uv, kpad=kpad, vpad=vpad, ft=ft, bcol=bcol,
                 c_col=c_col, c_row=c_row, ya=ya, lse_a=lse_a, yb=yb, yc=yc, lse_c=lse_c, yd=yd)
    return out, saved


def layer_bwd(dout, p, sv):
    s = dout.shape[0]
    c = sv["consts"]
    dya, dyb, dyc, dyd, dgates, dbg, dwout = outproj_bwd(
        dout, sv["ya"], sv["yb"], sv["yc"], sv["yd"], sv["gates"], c["bg"], p["wout"])
    keep = A_BAND - A_TQ
    dqa, dkpad, dvpad, dt = mix_a_bwd(sv["qkv"], sv["kpad"], sv["vpad"], c["tbias"], dya, sv["ya"], sv["lse_a"])
    dka, dva = dkpad[:, keep:], dvpad[:, keep:]
    drel = relbias_grad(dt, _rel_index())[:N_HEADS, :2 * MAX_REL + 1]
    duv, dws, dbs, dvgain = mix_b_bwd(sv["uv"], c["v_gain"], p["w_s"], c["b_col"], dyb)
    dqc, dkc, dvc, dc = fox_bwd(sv["qkv"], sv["c_col"], sv["c_row"], dyc, sv["yc"], sv["lse_c"])
    dft, dbf = fox_gate_bwd(sv["ft"], sv["bcol"], dc.reshape(N_HEADS * (s // 128), 128))
    dfp = jnp.pad(dft.reshape(N_HEADS, s).T, ((0, 0), (0, 128 - N_HEADS)))
    dqd, dkd, dvd = sb_bwd(sv["qkv"], dyd, sv["yd"])
    dp, dx, dnorm = inproj_bwd((dqa, dka, dva, dqc, dkc, dvc, dqd, dkd, dvd), dgates, duv, dfp,
                               p["wp"], sv["x"], c["norm_g"], dout)
    dwp = weight_grad(sv["h"], dp, "inproj_wgrad")
    grads = dict(norm_g=dnorm.reshape(D_MODEL), wp=dwp, b_f=dbf[:N_HEADS, 0], rel_bias=drel,
                 w_s=dws, b_s=dbs.reshape(N_HEADS, SG_CHUNK), v_gain=dvgain.reshape(D_BRANCH),
                 branch_gain=dbg.reshape(4, D_BRANCH), wout=dwout)
    return dx, grads


def local_step(x, tgt, layers, final_g):
    saved = []
    cur = x
    for p in layers:
        cur, sv = layer_fwd(cur, p)
        saved.append(sv)
    loss, dcur, dfinal = final_loss(cur, tgt, final_g.reshape(1, D_MODEL))
    grads = [None] * len(layers)
    for l in reversed(range(len(layers))):
        dcur, grads[l] = layer_bwd(dcur, layers[l], saved[l])
    return loss[0, 0], dcur, grads, dfinal.reshape(D_MODEL)


def gather_weights(wb, wf):
    def body(wb_ref, wf_ref, ob_ref, of_ref, send_sems, recv_sems, loc_sems):
        x, y, c = lax.axis_index("x"), lax.axis_index("y"), lax.axis_index("c")
        me = 2 * x + y
        chips = [(1 - x, y), (x, 1 - y), (1 - x, 1 - y)]
        pairs = [(wb_ref, ob_ref), (wf_ref, of_ref)]
        local = [pltpu.make_async_copy(src, dst.at[me], loc_sems.at[n]) for n, (src, dst) in enumerate(pairs)]
        for cp in local:
            cp.start()

        def copy(j, n, slot):
            src, dst = pairs[n]
            return pltpu.make_async_remote_copy(
                src_ref=src, dst_ref=dst.at[slot], send_sem=send_sems.at[2 * j + n], recv_sem=recv_sems.at[2 * j + n],
                device_id=(chips[j][0], chips[j][1], c), device_id_type=MESH)

        sends = [copy(j, n, me) for j in range(3) for n in range(2)]
        for cp in sends:
            cp.start()
        for j in range(3):
            for n in range(2):
                copy(j, n, 2 * chips[j][0] + chips[j][1]).wait_recv()
        for cp in sends:
            cp.wait_send()
        for cp in local:
            cp.wait()

    any_spec = pl.BlockSpec(memory_space=pl.ANY)
    return pl.pallas_call(
        body, name="gather_weights",
        in_specs=[any_spec, any_spec], out_specs=[any_spec, any_spec],
        out_shape=[jax.ShapeDtypeStruct((4,) + wb.shape, wb.dtype), jax.ShapeDtypeStruct((4,) + wf.shape, wf.dtype)],
        scratch_shapes=[pltpu.SemaphoreType.DMA((6,)), pltpu.SemaphoreType.DMA((6,)), pltpu.SemaphoreType.DMA((2,))],
    )(wb, wf)


def exchange_grads(send):
    def body(s_ref, r_ref, send_sems, recv_sems, loc_sem):
        x, y, c = lax.axis_index("x"), lax.axis_index("y"), lax.axis_index("c")
        me_chip = 2 * x + y
        me = 4 * x + 2 * y + c
        peers = [(x, y, 1 - c)]
        for px, py in [(1 - x, y), (x, 1 - y), (1 - x, 1 - y)]:
            peers += [(px, py, c), (px, py, 1 - c)]
        local = pltpu.make_async_copy(s_ref.at[me_chip], r_ref.at[me], loc_sem)
        local.start()

        def copy(n, chip, slot):
            return pltpu.make_async_remote_copy(
                src_ref=s_ref.at[chip], dst_ref=r_ref.at[slot], send_sem=send_sems.at[n], recv_sem=recv_sems.at[n],
                device_id=peers[n], device_id_type=MESH)

        sends = [copy(n, 2 * px + py, me) for n, (px, py, _) in enumerate(peers)]
        for cp in sends:
            cp.start()
        for n, (px, py, pc) in enumerate(peers):
            copy(n, me_chip, 4 * px + 2 * py + pc).wait_recv()
        for cp in sends:
            cp.wait_send()
        local.wait()

    any_spec = pl.BlockSpec(memory_space=pl.ANY)
    return pl.pallas_call(
        body, name="exchange_grads",
        in_specs=[any_spec], out_specs=any_spec,
        out_shape=jax.ShapeDtypeStruct((8,) + send.shape[1:], send.dtype),
        scratch_shapes=[pltpu.SemaphoreType.DMA((7,)), pltpu.SemaphoreType.DMA((7,)), pltpu.SemaphoreType.DMA],
    )(send)


def adamw_reduce(parts, w, m, v):
    rows = w.shape[0]
    tr = 512
    c1 = 1.0 - ADAM_B1 ** ADAM_STEP
    c2 = 1.0 - ADAM_B2 ** ADAM_STEP

    def body(p_ref, w_ref, m_ref, v_ref, g_ref, d_ref, nm_ref, nv_ref):
        g = p_ref[0]
        for n in range(1, 8):
            g = g + p_ref[n]
        g_ref[...] = g
        nm = ADAM_B1 * m_ref[...] + (1.0 - ADAM_B1) * g
        nv = ADAM_B2 * v_ref[...] + (1.0 - ADAM_B2) * (g * g)
        nm_ref[...] = nm
        nv_ref[...] = nv
        d_ref[...] = -ADAM_LR * ((nm / c1) / (jnp.sqrt(nv / c2) + ADAM_EPS) + ADAM_WD * w_ref[...])

    spec = pl.BlockSpec((tr, 128), lambda i: (i, 0))
    shape = jax.ShapeDtypeStruct((rows, 128), F32)
    return pl.pallas_call(
        body, name="adamw_reduce", grid=(rows // tr,),
        in_specs=[pl.BlockSpec((8, tr, 128), lambda i: (0, i, 0)), spec, spec, spec],
        out_specs=[spec] * 4, out_shape=[shape] * 4,
        compiler_params=_params(("arbitrary",)),
    )(parts, w, m, v)


SHARDED = ("w_in", "w_out", "branch_gain")
SMALL = ("norm_g", "b_f", "rel_bias", "w_s", "b_s", "v_gain", "final_g")
WEIGHTS = ("norm_g", "w_in", "b_f", "rel_bias", "w_s", "b_s", "v_gain", "branch_gain", "w_out", "final_g")
PACK_ORDER = SHARDED + SMALL
PACK_ROW_TILE = 512


def _rows_of(shape):
    return -(-int(np.prod(shape)) // 128)


def _pack(leaves):
    parts = []
    for a in leaves:
        flat = a.reshape(-1).astype(F32)
        parts.append(jnp.pad(flat, (0, _rows_of(a.shape) * 128 - flat.shape[0])))
    flat = jnp.concatenate(parts)
    rows = flat.shape[0] // 128
    total = -(-rows // PACK_ROW_TILE) * PACK_ROW_TILE
    return jnp.pad(flat, (0, (total - rows) * 128)).reshape(total, 128)


def _unpack(slab, shapes):
    out, row = [], 0
    for shp in shapes:
        n = int(np.prod(shp))
        r = _rows_of(shp)
        out.append(slab[row:row + r].reshape(-1)[:n].reshape(shp))
        row += r
    return out


def _pack_w_in(w):
    return jnp.concatenate([w[:, :2816], w[:, 2820:], w[:, 2816:2820],
                            jnp.zeros((w.shape[0], N_PACK - N_IN), w.dtype)], axis=1)


def _unpack_w_in(wp):
    return jnp.concatenate([wp[:, :2816], wp[:, F_COL:F_COL + N_HEADS], wp[:, 2816:F_COL]], axis=1)


def kernel(x, norm_g, w_in, b_f, rel_bias, w_s, b_s, v_gain, branch_gain, w_out, final_g, loss_target, m_norm_g, m_w_in, m_b_f, m_rel_bias, m_w_s, m_b_s, m_v_gain, m_branch_gain, m_w_out, m_final_g, v_norm_g, v_w_in, v_b_f, v_rel_bias, v_w_s, v_b_s, v_v_gain, v_branch_gain, v_w_out, v_final_g):
    depth = norm_g.shape[0]
    weights = dict(norm_g=norm_g, w_in=w_in, b_f=b_f, rel_bias=rel_bias, w_s=w_s, b_s=b_s, v_gain=v_gain,
                   branch_gain=branch_gain, w_out=w_out, final_g=final_g)
    mom1 = dict(norm_g=m_norm_g, w_in=m_w_in, b_f=m_b_f, rel_bias=m_rel_bias, w_s=m_w_s, b_s=m_b_s,
                v_gain=m_v_gain, branch_gain=m_branch_gain, w_out=m_w_out, final_g=m_final_g)
    mom2 = dict(norm_g=v_norm_g, w_in=v_w_in, b_f=v_b_f, rel_bias=v_rel_bias, w_s=v_w_s, b_s=v_b_s,
                v_gain=v_v_gain, branch_gain=v_branch_gain, w_out=v_w_out, final_g=v_final_g)

    n_in_rows = _rows_of(w_in.shape)
    n_out_rows = _rows_of(w_out.shape)
    wb = jnp.concatenate([w_in.astype(BF16).reshape(n_in_rows, 128), w_out.astype(BF16).reshape(n_out_rows, 128)])
    wf = jnp.pad(branch_gain.reshape(-1), (0, 8 * 128 - branch_gain.size)).reshape(8, 128)
    gb, gf = gather_weights(wb, wf)
    w_in_full = gb[:, :n_in_rows].reshape((4,) + w_in.shape)
    w_in_full = jnp.moveaxis(w_in_full, 0, 2).reshape(depth, D_MODEL, N_IN)
    w_out_full = gb[:, n_in_rows:].reshape((4,) + w_out.shape)
    w_out_full = jnp.moveaxis(w_out_full, 0, 1).reshape(depth, D_MODEL, D_MODEL)
    bg_full = gf.reshape(4, -1)[:, :branch_gain.size].reshape((4,) + branch_gain.shape)
    bg_full = jnp.moveaxis(bg_full, 0, 2).reshape(depth, 4, D_BRANCH)

    layers = [dict(norm_g=norm_g[l], wp=_pack_w_in(w_in_full[l]), b_f=b_f[l], rel_bias=rel_bias[l], w_s=w_s[l],
                   b_s=b_s[l], v_gain=v_gain[l], branch_gain=bg_full[l], wout=w_out_full[l]) for l in range(depth)]

    loss_part, grad_x, lgrads, dfinal = local_step(x[0], loss_target[0], layers, final_g)
    loss = lax.psum(loss_part, ("x", "y", "c"))

    stack = lambda k: jnp.stack([g[k] for g in lgrads])
    d_w_in = jnp.stack([_unpack_w_in(g["wp"]) for g in lgrads])
    d_w_out = stack("wout")
    d_bg = stack("branch_gain")
    small = dict(norm_g=stack("norm_g"), b_f=stack("b_f"), rel_bias=stack("rel_bias"), w_s=stack("w_s"),
                 b_s=stack("b_s"), v_gain=stack("v_gain"), final_g=dfinal)
    slabs = []
    for sidx in range(4):
        leaves = [d_w_in[:, :, sidx * N_SHARD:(sidx + 1) * N_SHARD],
                  d_w_out[:, sidx * D_BRANCH:(sidx + 1) * D_BRANCH, :],
                  d_bg[:, :, sidx * HEAD_DIM:(sidx + 1) * HEAD_DIM]] + [small[k] for k in SMALL]
        slabs.append(_pack(leaves))
    parts = exchange_grads(jnp.stack(slabs))

    pack_local = lambda d: _pack([d[k] for k in PACK_ORDER])
    g_slab, d_slab, m_slab, v_slab = adamw_reduce(parts, pack_local(weights), pack_local(mom1), pack_local(mom2))
    shapes = [weights[k].shape for k in PACK_ORDER]
    outs = {}
    for tag, slab in (("grad", g_slab), ("delta", d_slab), ("new_m", m_slab), ("new_v", v_slab)):
        for k, a in zip(PACK_ORDER, _unpack(slab, shapes)):
            outs[tag, k] = a
    result = [loss, grad_x[None]]
    for tag in ("grad", "delta", "new_m", "new_v"):
        result += [outs[tag, k] for k in WEIGHTS]
    return tuple(result)
```

```python
import functools

import jax
import jax.numpy as jnp
import numpy as np
from jax import lax
from jax.experimental import pallas as pl
from jax.experimental.pallas import tpu as pltpu

F32 = jnp.float32
BF16 = jnp.bfloat16
MESH = pl.DeviceIdType.MESH

D_MODEL = 1024
D_BRANCH = 256
N_HEADS = 4
HEAD_DIM = 64
CHUNK = 64
LOOKBACK = 8
MAX_REL = 128
SG_CHUNK = 128
EPS = 1e-6
N_IN = 3844
N_PACK = 3968
F_COL = 3840
N_SHARD = 961
NEG = -1e30

A_TQ = 128
A_BAND = A_TQ + LOOKBACK * CHUNK
A_QB = 512
ATT_T = 256
FOX_MID = 4
FOX_WIDE = 8
FOX_DEAD = -110.0
SB_DEAD = -110.0
ROW_T = 512
VMEM_LIMIT = 56 * 1024 * 1024

ADAM_LR = 0.001
ADAM_B1 = 0.9
ADAM_B2 = 0.999
ADAM_EPS = 1e-08
ADAM_WD = 0.01
ADAM_STEP = 10

SEC_A_Q, SEC_A_K, SEC_A_V, SEC_A_G = 0, 256, 512, 768
SEC_B_U, SEC_B_V, SEC_B_G = 1024, 1280, 1536
SEC_C_Q, SEC_C_K, SEC_C_V, SEC_C_G = 1792, 2048, 2304, 2560
SEC_D_Q, SEC_D_K, SEC_D_V, SEC_D_G = 2816, 3072, 3328, 3584
QKV_SECS = (SEC_A_Q, SEC_A_K, SEC_A_V, SEC_C_Q, SEC_C_K, SEC_C_V, SEC_D_Q, SEC_D_K, SEC_D_V)
GATE_SECS = (SEC_A_G, SEC_B_G, SEC_C_G, SEC_D_G)


def _dot(a, b):
    return jnp.dot(a, b, preferred_element_type=F32)


def _dot_nt(a, b):
    return lax.dot_general(a, b, (((1,), (1,)), ((), ())), preferred_element_type=F32)


def _dot_tn(a, b):
    return lax.dot_general(a, b, (((0,), (0,)), ((), ())), preferred_element_type=F32)


def _split2(x):
    hi = x.astype(BF16)
    lo = (x - hi.astype(F32)).astype(BF16)
    return hi, lo


def _split3(x):
    hi = x.astype(BF16)
    r = x - hi.astype(F32)
    mid = r.astype(BF16)
    lo = (r - mid.astype(F32)).astype(BF16)
    return hi, mid, lo


def _sigmoid(x):
    return 1.0 / (1.0 + jnp.exp(-x))


def _params(sem=None, vmem=VMEM_LIMIT):
    return pltpu.CompilerParams(dimension_semantics=sem, vmem_limit_bytes=vmem)


def _heads_to_lanes(ref):
    return jnp.concatenate([ref[h] for h in range(N_HEADS)], axis=1)


def inproj_fwd(x, g, wp):
    s = x.shape[0]
    tm = min(ROW_T, s)

    def body(x_ref, g_ref, w_ref, h_ref, qkv_ref, gates_ref, uv_ref, f_ref):
        xv = x_ref[...]
        r = lax.rsqrt(jnp.mean(xv * xv, axis=-1, keepdims=True) + EPS)
        h = (xv * r * g_ref[...]).astype(BF16)
        h_ref[...] = h
        for n, off in enumerate(QKV_SECS):
            p = _dot(h, w_ref[:, off:off + D_BRANCH])
            for hh in range(N_HEADS):
                qkv_ref[n, hh] = p[:, hh * HEAD_DIM:(hh + 1) * HEAD_DIM].astype(BF16)
        for n, off in enumerate(GATE_SECS):
            gates_ref[:, n * D_BRANCH:(n + 1) * D_BRANCH] = _dot(h, w_ref[:, off:off + D_BRANCH])
        uv_ref[...] = _dot(h, w_ref[:, SEC_B_U:SEC_B_U + 2 * D_BRANCH])
        f_ref[...] = _dot(h, w_ref[:, F_COL:F_COL + 128])

    return pl.pallas_call(
        body, name="inproj_fwd", grid=(s // tm,),
        in_specs=[pl.BlockSpec((tm, D_MODEL), lambda i: (i, 0)),
                  pl.BlockSpec((1, D_MODEL), lambda i: (0, 0)),
                  pl.BlockSpec((D_MODEL, N_PACK), lambda i: (0, 0))],
        out_specs=[pl.BlockSpec((tm, D_MODEL), lambda i: (i, 0)),
                   pl.BlockSpec((9, N_HEADS, tm, HEAD_DIM), lambda i: (0, 0, i, 0)),
                   pl.BlockSpec((tm, D_MODEL), lambda i: (i, 0)),
                   pl.BlockSpec((tm, 2 * D_BRANCH), lambda i: (i, 0)),
                   pl.BlockSpec((tm, 128), lambda i: (i, 0))],
        out_shape=[jax.ShapeDtypeStruct((s, D_MODEL), BF16),
                   jax.ShapeDtypeStruct((9, N_HEADS, s, HEAD_DIM), BF16),
                   jax.ShapeDtypeStruct((s, D_MODEL), F32),
                   jax.ShapeDtypeStruct((s, 2 * D_BRANCH), F32),
                   jax.ShapeDtypeStruct((s, 128), F32)],
        compiler_params=_params(("arbitrary",)),
    )(x, g, wp)


def inproj_bwd(dqkv, dgates, duv, dfp, wp, x, g, dres):
    s = x.shape[0]
    tm = min(ROW_T, s)

    def body(*refs):
        dq_refs = refs[:9]
        dgates_ref, duv_ref, dfp_ref, w_ref, x_ref, g_ref, dres_ref, dp_ref, dx_ref, dg_ref = refs[9:]
        i = pl.program_id(0)
        a_q, a_k, a_v, c_q, c_k, c_v, d_q, d_k, d_v = [_heads_to_lanes(r).astype(BF16) for r in dq_refs]
        dgt = dgates_ref[...]
        duv_b = duv_ref[...].astype(BF16)
        dp = jnp.concatenate(
            [a_q, a_k, a_v, dgt[:, 0:256], duv_b, dgt[:, 256:512], c_q, c_k, c_v, dgt[:, 512:768],
             d_q, d_k, d_v, dgt[:, 768:1024], dfp_ref[...].astype(BF16)], axis=1)
        dp_ref[...] = dp
        dh = _dot_nt(dp, w_ref[...])
        xv = x_ref[...]
        r = lax.rsqrt(jnp.mean(xv * xv, axis=-1, keepdims=True) + EPS)
        xn = xv * r
        u = dh * g_ref[...]
        dx_ref[...] = dres_ref[...] + r * (u - xn * jnp.mean(xn * u, axis=-1, keepdims=True))

        @pl.when(i == 0)
        def _():
            dg_ref[...] = jnp.zeros_like(dg_ref)

        dg_ref[...] += jnp.sum(dh * xn, axis=0, keepdims=True)

    head_spec = pl.BlockSpec((N_HEADS, tm, HEAD_DIM), lambda i: (0, i, 0))
    return pl.pallas_call(
        body, name="inproj_bwd", grid=(s // tm,),
        in_specs=[head_spec] * 9 + [
            pl.BlockSpec((tm, D_MODEL), lambda i: (i, 0)),
            pl.BlockSpec((tm, 2 * D_BRANCH), lambda i: (i, 0)),
            pl.BlockSpec((tm, 128), lambda i: (i, 0)),
            pl.BlockSpec((D_MODEL, N_PACK), lambda i: (0, 0)),
            pl.BlockSpec((tm, D_MODEL), lambda i: (i, 0)),
            pl.BlockSpec((1, D_MODEL), lambda i: (0, 0)),
            pl.BlockSpec((tm, D_MODEL), lambda i: (i, 0))],
        out_specs=[pl.BlockSpec((tm, N_PACK), lambda i: (i, 0)),
                   pl.BlockSpec((tm, D_MODEL), lambda i: (i, 0)),
                   pl.BlockSpec((1, D_MODEL), lambda i: (0, 0))],
        out_shape=[jax.ShapeDtypeStruct((s, N_PACK), BF16),
                   jax.ShapeDtypeStruct((s, D_MODEL), F32),
                   jax.ShapeDtypeStruct((1, D_MODEL), F32)],
        compiler_params=_params(("arbitrary",)),
    )(*dqkv, dgates, duv, dfp, wp, x, g, dres)


def weight_grad(a, b, name):
    s, m = a.shape
    n = b.shape[1]
    tm = min(ROW_T, s)
    tmm = 256
    nsteps = s // tm

    def body(a_ref, b_ref, o_ref):
        k = pl.program_id(1)

        @pl.when(k == 0)
        def _():
            o_ref[...] = jnp.zeros_like(o_ref)

        o_ref[...] += _dot_tn(a_ref[...], b_ref[...])

    return pl.pallas_call(
        body, name=name, grid=(m // tmm, nsteps),
        in_specs=[pl.BlockSpec((tm, tmm), lambda j, k: (k, j)),
                  pl.BlockSpec((tm, n), lambda j, k: (k, 0))],
        out_specs=pl.BlockSpec((tmm, n), lambda j, k: (j, 0)),
        out_shape=jax.ShapeDtypeStruct((m, n), F32),
        compiler_params=_params(("arbitrary", "arbitrary")),
    )(a, b)


def _a_specs(s):
    nq = s // A_QB
    q_spec = pl.BlockSpec((None, None, A_QB, HEAD_DIM), lambda h, i: (0, h, jnp.minimum(i, nq - 1), 0))
    kv_specs = [pl.BlockSpec((None, A_QB, HEAD_DIM), lambda h, i, m=m: (h, jnp.minimum(i + m, nq), 0)) for m in range(2)]
    t_spec = pl.BlockSpec((None, A_TQ, A_BAND), lambda h, i: (h, 0, 0))
    return nq, q_spec, kv_specs, t_spec


def _a_scores(q_ref, k, t_ref, i, j):
    rows = slice(j * A_TQ, (j + 1) * A_TQ)
    qs = q_ref[rows, :] * 0.125
    kj = k[j * A_TQ:j * A_TQ + A_BAND, :]
    sc = _dot_nt(qs, kj) + t_ref[...]
    col = lax.broadcasted_iota(jnp.int32, (A_TQ, A_BAND), 1)
    sc = jnp.where(col >= (A_BAND - A_TQ) - i * A_QB - j * A_TQ, sc, NEG)
    return rows, qs, kj, sc


def mix_a_fwd(qkv, kpad, vpad, tbias):
    s = qkv.shape[2]
    nq, q_spec, kv_specs, t_spec = _a_specs(s)

    def body(q_ref, k0_ref, k1_ref, v0_ref, v1_ref, t_ref, o_ref, lse_ref):
        i = pl.program_id(1)
        k = jnp.concatenate([k0_ref[...], k1_ref[...]], axis=0)
        v = jnp.concatenate([v0_ref[...], v1_ref[...]], axis=0)
        for j in range(A_QB // A_TQ):
            rows, _, _, sc = _a_scores(q_ref, k, t_ref, i, j)
            m = jnp.max(sc, axis=-1, keepdims=True)
            p = jnp.exp(sc - m)
            l = jnp.sum(p, axis=-1, keepdims=True)
            o_ref[rows, :] = _dot(p.astype(BF16), v[j * A_TQ:j * A_TQ + A_BAND, :]) / l
            lse_ref[rows, :] = m + jnp.log(l)

    return pl.pallas_call(
        body, name="mix_a_fwd", grid=(N_HEADS, nq),
        in_specs=[q_spec] + kv_specs + kv_specs + [t_spec],
        out_specs=[pl.BlockSpec((None, A_QB, HEAD_DIM), lambda h, i: (h, i, 0)),
                   pl.BlockSpec((None, A_QB, 1), lambda h, i: (h, i, 0))],
        out_shape=[jax.ShapeDtypeStruct((N_HEADS, s, HEAD_DIM), F32),
                   jax.ShapeDtypeStruct((N_HEADS, s, 1), F32)],
        compiler_params=_params(("arbitrary", "arbitrary")),
    )(qkv, kpad, kpad, vpad, vpad, tbias)


def mix_a_bwd(qkv, kpad, vpad, tbias, do, o, lse):
    s = qkv.shape[2]
    nq, q_spec, kv_specs, t_spec = _a_specs(s)
    row_spec = lambda w: pl.BlockSpec((None, A_QB, w), lambda h, i: (h, jnp.minimum(i, nq - 1), 0))
    keep = A_BAND - A_TQ
    win = 2 * A_QB

    def body(q_ref, k0_ref, k1_ref, v0_ref, v1_ref, t_ref, do_ref, o_ref, lse_ref,
             dq_ref, dk_ref, dv_ref, dt_ref, dk_win, dv_win):
        i = pl.program_id(1)

        @pl.when(i == 0)
        def _():
            dk_win[...] = jnp.zeros_like(dk_win)
            dv_win[...] = jnp.zeros_like(dv_win)
            dt_ref[...] = jnp.zeros_like(dt_ref)

        @pl.when(i < nq)
        def _():
            k = jnp.concatenate([k0_ref[...], k1_ref[...]], axis=0)
            v = jnp.concatenate([v0_ref[...], v1_ref[...]], axis=0)
            dt = jnp.zeros((A_TQ, A_BAND), F32)
            for j in range(A_QB // A_TQ):
                rows, qs, kj, sc = _a_scores(q_ref, k, t_ref, i, j)
                keys = slice(j * A_TQ, j * A_TQ + A_BAND)
                dob = do_ref[rows, :]
                p = jnp.exp(sc - lse_ref[rows, :])
                delta = jnp.sum(o_ref[rows, :] * dob.astype(F32), axis=-1, keepdims=True)
                ds = p * (_dot_nt(dob, v[keys, :]) - delta)
                dsb = ds.astype(BF16)
                dq_ref[rows, :] = _dot(dsb, kj) * 0.125
                dk_win[keys, :] += _dot_tn(dsb, qs)
                dv_win[keys, :] += _dot_tn(p.astype(BF16), dob)
                dt = dt + ds
            dt_ref[...] += dt

        dk_ref[...] = dk_win[0:A_QB, :]
        dv_ref[...] = dv_win[0:A_QB, :]
        dk_rest = dk_win[A_QB:win, :]
        dv_rest = dv_win[A_QB:win, :]
        dk_win[0:A_QB, :] = dk_rest
        dv_win[0:A_QB, :] = dv_rest
        dk_win[A_QB:win, :] = jnp.zeros((A_QB, HEAD_DIM), F32)
        dv_win[A_QB:win, :] = jnp.zeros((A_QB, HEAD_DIM), F32)

    return pl.pallas_call(
        body, name="mix_a_bwd", grid=(N_HEADS, nq + 1),
        in_specs=[q_spec] + kv_specs + kv_specs + [t_spec, row_spec(HEAD_DIM), row_spec(HEAD_DIM), row_spec(1)],
        out_specs=[row_spec(HEAD_DIM),
                   pl.BlockSpec((None, A_QB, HEAD_DIM), lambda h, i: (h, i, 0)),
                   pl.BlockSpec((None, A_QB, HEAD_DIM), lambda h, i: (h, i, 0)),
                   t_spec],
        out_shape=[jax.ShapeDtypeStruct((N_HEADS, s, HEAD_DIM), F32),
                   jax.ShapeDtypeStruct((N_HEADS, s + keep, HEAD_DIM), F32),
                   jax.ShapeDtypeStruct((N_HEADS, s + keep, HEAD_DIM), F32),
                   jax.ShapeDtypeStruct((N_HEADS, A_TQ, A_BAND), F32)],
        scratch_shapes=[pltpu.VMEM((win, HEAD_DIM), F32), pltpu.VMEM((win, HEAD_DIM), F32)],
        compiler_params=_params(("arbitrary", "arbitrary")),
    )(qkv, kpad, kpad, vpad, vpad, tbias, do, o, lse)


def relbias_tile(rel_bias, relmat):
    nrel = 2 * MAX_REL + 1

    def body(rb_ref, rel_ref, o_ref):
        rel = rel_ref[...]
        o_ref[...] = jnp.full(o_ref.shape, NEG, F32)

        def step(r, carry):
            hit = rel == r
            for h in range(N_HEADS):
                o_ref[h] = jnp.where(hit, rb_ref[h, r], o_ref[h])
            return carry

        lax.fori_loop(0, nrel, step, 0)

    return pl.pallas_call(
        body, name="relbias_tile",
        in_specs=[pl.BlockSpec(memory_space=pltpu.SMEM), pl.BlockSpec(memory_space=pltpu.VMEM)],
        out_specs=pl.BlockSpec(memory_space=pltpu.VMEM),
        out_shape=jax.ShapeDtypeStruct((N_HEADS, A_TQ, A_BAND), F32),
        compiler_params=_params(),
    )(rel_bias, relmat)


def relbias_grad(dt, relmat):
    nrel = 2 * MAX_REL + 1

    def body(dt_ref, rel_ref, o_ref):
        rel = rel_ref[...]
        lane = lax.broadcasted_iota(jnp.int32, (8, 384), 1)
        row = lax.broadcasted_iota(jnp.int32, (8, 384), 0)

        def step(r, acc):
            hit = rel == r
            for h in range(N_HEADS):
                val = jnp.sum(jnp.where(hit, dt_ref[h], 0.0))
                acc = jnp.where((lane == r) & (row == h), val, acc)
            return acc

        o_ref[...] = lax.fori_loop(0, nrel, step, jnp.zeros((8, 384), F32))

    return pl.pallas_call(
        body, name="relbias_grad",
        out_shape=jax.ShapeDtypeStruct((8, 384), F32),
        compiler_params=_params(),
    )(dt, relmat)


def _b_norm(v, gain):
    mu = jnp.mean(v, axis=-1, keepdims=True)
    xc = v - mu
    rstd = lax.rsqrt(jnp.mean(xc * xc, axis=-1, keepdims=True) + EPS)
    xhat = xc * rstd
    return xhat, rstd, xhat * gain


def _tril_mask():
    t = lax.broadcasted_iota(jnp.int32, (SG_CHUNK, SG_CHUNK), 0)
    u = lax.broadcasted_iota(jnp.int32, (SG_CHUNK, SG_CHUNK), 1)
    return u <= t


def mix_b_fwd(uv, gain, w_s, b_col):
    s = uv.shape[0]
    tm = min(ROW_T, s)

    def body(uv_ref, gain_ref, w_ref, b_ref, y_ref):
        tril = _tril_mask()
        ws = [jnp.where(tril, w_ref[g], 0.0).astype(BF16) for g in range(N_HEADS)]
        for c in range(tm // SG_CHUNK):
            rows = slice(c * SG_CHUNK, (c + 1) * SG_CHUNK)
            u = uv_ref[rows, 0:D_BRANCH]
            _, _, vn = _b_norm(uv_ref[rows, D_BRANCH:2 * D_BRANCH], gain_ref[...])
            vnb = vn.astype(BF16)
            outs = []
            for g in range(N_HEADS):
                cols = slice(g * HEAD_DIM, (g + 1) * HEAD_DIM)
                mixed = _dot(ws[g], vnb[:, cols]) + b_ref[g]
                outs.append(u[:, cols] * mixed)
            y_ref[rows, :] = jnp.concatenate(outs, axis=1)

    return pl.pallas_call(
        body, name="mix_b_fwd", grid=(s // tm,),
        in_specs=[pl.BlockSpec((tm, 2 * D_BRANCH), lambda i: (i, 0)),
                  pl.BlockSpec((1, D_BRANCH), lambda i: (0, 0)),
                  pl.BlockSpec((N_HEADS, SG_CHUNK, SG_CHUNK), lambda i: (0, 0, 0)),
                  pl.BlockSpec((N_HEADS, SG_CHUNK, 1), lambda i: (0, 0, 0))],
        out_specs=pl.BlockSpec((tm, D_BRANCH), lambda i: (i, 0)),
        out_shape=jax.ShapeDtypeStruct((s, D_BRANCH), F32),
        compiler_params=_params(("arbitrary",)),
    )(uv, gain, w_s, b_col)


def mix_b_bwd(uv, gain, w_s, b_col, dy):
    s = uv.shape[0]
    tm = min(ROW_T, s)

    def body(uv_ref, gain_ref, w_ref, b_ref, dy_ref, duv_ref, dw_ref, db_ref, dgain_ref):
        i = pl.program_id(0)

        @pl.when(i == 0)
        def _():
            dw_ref[...] = jnp.zeros_like(dw_ref)
            db_ref[...] = jnp.zeros_like(db_ref)
            dgain_ref[...] = jnp.zeros_like(dgain_ref)

        tril = _tril_mask()
        ws = [jnp.where(tril, w_ref[g], 0.0).astype(BF16) for g in range(N_HEADS)]
        gain_v = gain_ref[...]
        for c in range(tm // SG_CHUNK):
            rows = slice(c * SG_CHUNK, (c + 1) * SG_CHUNK)
            u = uv_ref[rows, 0:D_BRANCH]
            xhat, rstd, vn = _b_norm(uv_ref[rows, D_BRANCH:2 * D_BRANCH], gain_v)
            vnb = vn.astype(BF16)
            dyv = dy_ref[rows, :]
            dus, dvns = [], []
            for g in range(N_HEADS):
                cols = slice(g * HEAD_DIM, (g + 1) * HEAD_DIM)
                mixed = _dot(ws[g], vnb[:, cols]) + b_ref[g]
                dus.append(dyv[:, cols] * mixed)
                dmixed = dyv[:, cols] * u[:, cols]
                dmb = dmixed.astype(BF16)
                db_ref[g] += jnp.sum(dmixed, axis=-1, keepdims=True)
                dw_ref[g] += jnp.where(tril, _dot_nt(dmb, vnb[:, cols]), 0.0)
                dvns.append(_dot_tn(ws[g], dmb))
            dvn = jnp.concatenate(dvns, axis=1)
            dgain_ref[...] += jnp.sum(dvn * xhat, axis=0, keepdims=True)
            dxh = dvn * gain_v
            dv = rstd * (dxh - jnp.mean(dxh, axis=-1, keepdims=True)
                         - xhat * jnp.mean(dxh * xhat, axis=-1, keepdims=True))
            duv_ref[rows, :] = jnp.concatenate(dus + [dv], axis=1)

    return pl.pallas_call(
        body, name="mix_b_bwd", grid=(s // tm,),
        in_specs=[pl.BlockSpec((tm, 2 * D_BRANCH), lambda i: (i, 0)),
                  pl.BlockSpec((1, D_BRANCH), lambda i: (0, 0)),
                  pl.BlockSpec((N_HEADS, SG_CHUNK, SG_CHUNK), lambda i: (0, 0, 0)),
                  pl.BlockSpec((N_HEADS, SG_CHUNK, 1), lambda i: (0, 0, 0)),
                  pl.BlockSpec((tm, D_BRANCH), lambda i: (i, 0))],
        out_specs=[pl.BlockSpec((tm, 2 * D_BRANCH), lambda i: (i, 0)),
                   pl.BlockSpec((N_HEADS, SG_CHUNK, SG_CHUNK), lambda i: (0, 0, 0)),
                   pl.BlockSpec((N_HEADS, SG_CHUNK, 1), lambda i: (0, 0, 0)),
                   pl.BlockSpec((1, D_BRANCH), lambda i: (0, 0))],
        out_shape=[jax.ShapeDtypeStruct((s, 2 * D_BRANCH), F32),
                   jax.ShapeDtypeStruct((N_HEADS, SG_CHUNK, SG_CHUNK), F32),
                   jax.ShapeDtypeStruct((N_HEADS, SG_CHUNK, 1), F32),
                   jax.ShapeDtypeStruct((1, D_BRANCH), F32)],
        compiler_params=_params(("arbitrary",)),
    )(uv, gain, w_s, b_col, dy)


def _scan_mats(nrow):
    a = lax.broadcasted_iota(jnp.int32, (128, 128), 0)
    b = lax.broadcasted_iota(jnp.int32, (128, 128), 1)
    r = lax.broadcasted_iota(jnp.int32, (nrow, nrow), 0)
    c = lax.broadcasted_iota(jnp.int32, (nrow, nrow), 1)
    nb = nrow // N_HEADS
    same = (r // nb) == (c // nb)
    return a, b, r, c, same


def _exact_dot(x, m):
    hi, mid, lo = _split3(x)
    return _dot(hi, m) + _dot(mid, m) + _dot(lo, m)


def _exact_dot_left(m, x):
    hi, mid, lo = _split3(x)
    return _dot(m, hi) + _dot(m, mid) + _dot(m, lo)


def fox_gate_fwd(ft, bcol):
    nrow = ft.shape[0]

    def body(f_ref, b_ref, c_ref):
        z = f_ref[...] + b_ref[...]
        ls = jnp.minimum(z, 0.0) - jnp.log(1.0 + jnp.exp(-jnp.abs(z)))
        a, b, r, c, same = _scan_mats(nrow)
        within = _exact_dot(ls, (a <= b).astype(BF16))
        tot = jnp.broadcast_to(within[:, 127:128], within.shape)
        before = _exact_dot_left((same & (c < r)).astype(BF16), tot)
        c_ref[...] = within + before

    return pl.pallas_call(
        body, name="fox_gate_fwd",
        out_shape=jax.ShapeDtypeStruct((nrow, 128), F32),
        compiler_params=_params(),
    )(ft, bcol)


def fox_gate_bwd(ft, bcol, dc):
    nrow = ft.shape[0]

    def body(f_ref, b_ref, dc_ref, df_ref, db_ref):
        a, b, r, c, same = _scan_mats(nrow)
        dcv = dc_ref[...]
        within = _exact_dot(dcv, (a >= b).astype(BF16))
        tot = jnp.broadcast_to(within[:, 0:1], within.shape)
        after = _exact_dot_left((same & (c > r)).astype(BF16), tot)
        dls = within + after
        z = f_ref[...] + b_ref[...]
        dz = dls * _sigmoid(-z)
        df_ref[...] = dz
        rs = jnp.broadcast_to(jnp.sum(dz, axis=-1, keepdims=True), dz.shape)
        hr = lax.broadcasted_iota(jnp.int32, (8, nrow), 0)
        hc = lax.broadcasted_iota(jnp.int32, (8, nrow), 1)
        db_ref[...] = _exact_dot_left((hr == hc // (nrow // N_HEADS)).astype(BF16), rs)

    return pl.pallas_call(
        body, name="fox_gate_bwd",
        out_shape=[jax.ShapeDtypeStruct((nrow, 128), F32), jax.ShapeDtypeStruct((8, 128), F32)],
        compiler_params=_params(),
    )(ft, bcol, dc)


def _att_specs(s, qi, ki, vi):
    t = ATT_T
    q_spec = pl.BlockSpec((None, None, t, HEAD_DIM), lambda h, i: (qi, h, i, 0))
    k_spec = pl.BlockSpec((None, None, s, HEAD_DIM), lambda h, i: (ki, h, 0, 0))
    v_spec = pl.BlockSpec((None, None, s, HEAD_DIM), lambda h, i: (vi, h, 0, 0))
    row_spec = lambda w: pl.BlockSpec((None, t, w), lambda h, i: (h, i, 0))
    return q_spec, k_spec, v_spec, row_spec


def _causal(strict):
    row = lax.broadcasted_iota(jnp.int32, (ATT_T, ATT_T), 0)
    col = lax.broadcasted_iota(jnp.int32, (ATT_T, ATT_T), 1)
    return (col < row) if strict else (col <= row)


def _gate_row(cr_ref, kb, g):
    if g == 1:
        return cr_ref[kb]
    return jnp.concatenate([cr_ref[kb + n] for n in range(g)], axis=1)


def _fox_walk(i, carry, tile, alive):
    g = FOX_WIDE
    nmid = i // FOX_MID
    nwide = i // g
    carry = tile(i, 1, carry, True)
    carry = lax.fori_loop(0, i - nmid * FOX_MID, lambda n, c: tile(i - 1 - n, 1, c, False), carry)
    carry = lax.fori_loop(0, nmid - nwide * (g // FOX_MID), lambda n, c: tile(nwide * g, FOX_MID, c, False), carry)

    def cond(state):
        return jnp.logical_and(state[0] >= 0, state[1] > 0)

    def step(state):
        n = state[0]
        c = tile(n * g, g, state[2:], False)
        return (n - 1, alive(n * g, c)) + tuple(c)

    out = lax.while_loop(cond, step, (nwide - 1, alive(nwide * g, carry)) + tuple(carry))
    return out[2:]


def _fox_reach(qs, k_ref, kmax_ref, cc, i):
    s = k_ref.shape[0]
    rows = 4 * ATT_T

    @pl.when(i == 0)
    def _():
        def chunk(n, mx):
            kc = k_ref[pl.ds(pl.multiple_of(n * rows, rows), rows), :].astype(F32)
            return jnp.maximum(mx, jnp.max(jnp.sum(kc * kc, axis=-1, keepdims=True)))

        kmax_ref[0] = jnp.sqrt(lax.fori_loop(0, s // rows, chunk, jnp.float32(0.0)))

    qf = qs.astype(F32)
    return jnp.sqrt(jnp.sum(qf * qf, axis=-1, keepdims=True)) * kmax_ref[0] + cc


def fox_fwd(qkv, c_col, c_row):
    s = qkv.shape[2]
    t = ATT_T
    nq = s // t
    q_spec, k_spec, v_spec, row_spec = _att_specs(s, 3, 4, 5)

    def body(q_ref, k_ref, v_ref, cc_ref, cr_ref, o_ref, lse_ref, kmax_ref):
        i = pl.program_id(1)
        qs = q_ref[...] * 0.125
        cc = cc_ref[...]
        reach = _fox_reach(qs, k_ref, kmax_ref, cc, i)

        def alive(kb, carry):
            return (jnp.max(reach - cr_ref[kb][:, 0:1] - carry[0]) > FOX_DEAD).astype(jnp.int32)

        def tile(kb, g, carry, masked):
            m, l, acc = carry
            k0 = pl.multiple_of(kb * t, t)
            sc = _dot_nt(qs, k_ref[pl.ds(k0, g * t), :]) + (cc - _gate_row(cr_ref, kb, g))
            if masked:
                sc = jnp.where(_causal(False), sc, NEG)
            m_new = jnp.maximum(m, jnp.max(sc, axis=-1, keepdims=True))
            alpha = jnp.exp(m - m_new)
            p = jnp.exp(sc - m_new)
            l = alpha * l + jnp.sum(p, axis=-1, keepdims=True)
            p_hi, p_lo = _split2(p)
            v = v_ref[pl.ds(k0, g * t), :]
            acc = alpha * acc + (_dot(p_hi, v) + _dot(p_lo, v))
            return m_new, l, acc

        init = (jnp.full((t, 1), NEG, F32), jnp.zeros((t, 1), F32), jnp.zeros((t, HEAD_DIM), F32))
        m, l, acc = _fox_walk(i, init, tile, alive)
        o_ref[...] = acc / l
        lse_ref[...] = m + jnp.log(l)

    return pl.pallas_call(
        body, name="fox_fwd", grid=(N_HEADS, nq),
        in_specs=[q_spec, k_spec, v_spec, row_spec(1),
                  pl.BlockSpec((None, nq, 1, t), lambda h, i: (h, 0, 0, 0))],
        out_specs=[row_spec(HEAD_DIM), row_spec(1)],
        out_shape=[jax.ShapeDtypeStruct((N_HEADS, s, HEAD_DIM), F32),
                   jax.ShapeDtypeStruct((N_HEADS, s, 1), F32)],
        scratch_shapes=[pltpu.SMEM((1,), F32)],
        compiler_params=_params(("arbitrary", "arbitrary")),
    )(qkv, qkv, qkv, c_col, c_row)


def fox_bwd(qkv, c_col, c_row, do, o, lse):
    s = qkv.shape[2]
    t = ATT_T
    nq = s // t
    q_spec, k_spec, v_spec, row_spec = _att_specs(s, 3, 4, 5)
    any_spec = pl.BlockSpec(memory_space=pl.ANY)

    def body(q_ref, k_ref, v_ref, cc_ref, cr_ref, do_ref, o_ref, lse_ref,
             dq_ref, dk_hbm, dv_hbm, dc_ref, dk_acc, dv_acc, kmax_ref):
        h = pl.program_id(0)
        i = pl.program_id(1)

        @pl.when(i == 0)
        def _():
            dk_acc[...] = jnp.zeros_like(dk_acc)
            dv_acc[...] = jnp.zeros_like(dv_acc)
            dc_ref[...] = jnp.zeros_like(dc_ref)

        qs = q_ref[...] * 0.125
        dob = do_ref[...]
        delta = jnp.sum(o_ref[...] * dob.astype(F32), axis=-1, keepdims=True)
        lse = lse_ref[...]
        cc = cc_ref[...]
        margin = _fox_reach(qs, k_ref, kmax_ref, cc, i) - lse

        def alive(kb, carry):
            return (jnp.max(margin - cr_ref[kb][:, 0:1]) > FOX_DEAD).astype(jnp.int32)

        def tile(kb, g, carry, masked):
            dq, = carry
            k0 = pl.multiple_of(kb * t, t)
            k = k_ref[pl.ds(k0, g * t), :]
            sc = _dot_nt(qs, k) + (cc - _gate_row(cr_ref, kb, g))
            if masked:
                sc = jnp.where(_causal(False), sc, NEG)
            p = jnp.exp(sc - lse)
            ds = p * (_dot_nt(dob, v_ref[pl.ds(k0, g * t), :]) - delta)
            dsb = ds.astype(BF16)
            dk_acc[pl.ds(k0, g * t), :] += _dot_tn(dsb, qs)
            dv_acc[pl.ds(k0, g * t), :] += _dot_tn(p.astype(BF16), dob)
            dcs = -jnp.sum(ds, axis=0, keepdims=True)
            for n in range(g):
                dc_ref[kb + n] += dcs[:, n * t:(n + 1) * t]
            return (dq + _dot(dsb, k),)

        dq, = _fox_walk(i, (jnp.zeros((t, HEAD_DIM), F32),), tile, alive)
        dq_ref[...] = dq * 0.125

        @pl.when(i == nq - 1)
        def _():
            pltpu.sync_copy(dk_acc, dk_hbm.at[h])
            pltpu.sync_copy(dv_acc, dv_hbm.at[h])

    return pl.pallas_call(
        body, name="fox_bwd", grid=(N_HEADS, nq),
        in_specs=[q_spec, k_spec, v_spec, row_spec(1),
                  pl.BlockSpec((None, nq, 1, t), lambda h, i: (h, 0, 0, 0)),
                  row_spec(HEAD_DIM), row_spec(HEAD_DIM), row_spec(1)],
        out_specs=[row_spec(HEAD_DIM), any_spec, any_spec,
                   pl.BlockSpec((None, nq, 1, t), lambda h, i: (h, 0, 0, 0))],
        out_shape=[jax.ShapeDtypeStruct((N_HEADS, s, HEAD_DIM), F32),
                   jax.ShapeDtypeStruct((N_HEADS, s, HEAD_DIM), F32),
                   jax.ShapeDtypeStruct((N_HEADS, s, HEAD_DIM), F32),
                   jax.ShapeDtypeStruct((N_HEADS, nq, 1, t), F32)],
        scratch_shapes=[pltpu.VMEM((s, HEAD_DIM), F32), pltpu.VMEM((s, HEAD_DIM), F32), pltpu.SMEM((1,), F32)],
        compiler_params=_params(("arbitrary", "arbitrary")),
    )(qkv, qkv, qkv, c_col, c_row, do, o, lse)


def _sb_tile(qs, k, run, masked):
    z = _dot_nt(qs, k)
    sp = jnp.log(1.0 + jnp.exp(-jnp.abs(z)))
    ls = jnp.minimum(z, 0.0) - sp
    lm = -jnp.maximum(z, 0.0) - sp
    if masked:
        valid = _causal(True)
        lm = jnp.where(valid, lm, 0.0)
    row = lax.broadcasted_iota(jnp.int32, (ATT_T, ATT_T), 0)
    col = lax.broadcasted_iota(jnp.int32, (ATT_T, ATT_T), 1)
    later = (row > col).astype(BF16)
    hi, lo = _split2(lm)
    between = run + _dot(hi, later) + _dot(lo, later)
    a = jnp.exp(ls + between)
    if masked:
        a = jnp.where(valid, a, 0.0)
    return ls, lm, a


def _sb_walk(i, carry, tile):
    def alive_of(c):
        return (jnp.max(c[0]) > SB_DEAD).astype(jnp.int32)

    def cond(state):
        n, alive = state[0], state[1]
        return jnp.logical_and(n < i, alive > 0)

    def step(state):
        n = state[0]
        c = tile(i - 1 - n, state[2:], False)
        return (n + 1, alive_of(c)) + tuple(c)

    out = lax.while_loop(cond, step, (jnp.int32(0), alive_of(carry)) + tuple(carry))
    return out[2:]


def sb_fwd(qkv):
    s = qkv.shape[2]
    t = ATT_T
    nq = s // t
    q_spec, k_spec, v_spec, row_spec = _att_specs(s, 6, 7, 8)

    def body(q_ref, k_ref, v_ref, o_ref):
        i = pl.program_id(1)
        qs = q_ref[...] * 0.125

        def tile(kb, carry, masked):
            run, acc = carry
            k0 = pl.multiple_of(kb * t, t)
            _, lm, a = _sb_tile(qs, k_ref[pl.ds(k0, t), :], run, masked)
            acc = acc + _dot(a.astype(BF16), v_ref[pl.ds(k0, t), :])
            return run + jnp.sum(lm, axis=-1, keepdims=True), acc

        carry = tile(i, (jnp.zeros((t, 1), F32), jnp.zeros((t, HEAD_DIM), F32)), True)
        _, acc = _sb_walk(i, carry, tile)
        o_ref[...] = acc

    return pl.pallas_call(
        body, name="sb_fwd", grid=(N_HEADS, nq),
        in_specs=[q_spec, k_spec, v_spec],
        out_specs=row_spec(HEAD_DIM),
        out_shape=jax.ShapeDtypeStruct((N_HEADS, s, HEAD_DIM), F32),
        compiler_params=_params(("arbitrary", "arbitrary")),
    )(qkv, qkv, qkv)


def sb_bwd(qkv, do, o):
    s = qkv.shape[2]
    t = ATT_T
    nq = s // t
    q_spec, k_spec, v_spec, row_spec = _att_specs(s, 6, 7, 8)
    any_spec = pl.BlockSpec(memory_space=pl.ANY)

    def body(q_ref, k_ref, v_ref, do_ref, o_ref, dq_ref, dk_hbm, dv_hbm, dk_acc, dv_acc):
        h = pl.program_id(0)
        i = pl.program_id(1)

        @pl.when(i == 0)
        def _():
            dk_acc[...] = jnp.zeros_like(dk_acc)
            dv_acc[...] = jnp.zeros_like(dv_acc)

        qs = q_ref[...] * 0.125
        dob = do_ref[...]
        tot = jnp.sum(o_ref[...] * dob.astype(F32), axis=-1, keepdims=True)

        def tile(kb, carry, masked):
            run, run_g, dq = carry
            k0 = pl.multiple_of(kb * t, t)
            k = k_ref[pl.ds(k0, t), :]
            ls, lm, a = _sb_tile(qs, k, run, masked)
            ab = a.astype(BF16)
            g = ab.astype(F32) * _dot_nt(dob, v_ref[pl.ds(k0, t), :])
            row = lax.broadcasted_iota(jnp.int32, (t, t), 0)
            col = lax.broadcasted_iota(jnp.int32, (t, t), 1)
            from_here = (row >= col).astype(BF16)
            hi, lo = _split2(g)
            g_right = run_g + _dot(hi, from_here) + _dot(lo, from_here)
            g_left = tot - g_right
            dz = g - jnp.exp(ls) * (g + g_left)
            if masked:
                dz = jnp.where(_causal(True), dz, 0.0)
            dzb = dz.astype(BF16)
            dk_acc[pl.ds(k0, t), :] += _dot_tn(dzb, qs)
            dv_acc[pl.ds(k0, t), :] += _dot_tn(ab, dob)
            return (run + jnp.sum(lm, axis=-1, keepdims=True),
                    run_g + jnp.sum(g, axis=-1, keepdims=True),
                    dq + _dot(dzb, k))

        zero = jnp.zeros((t, 1), F32)
        carry = tile(i, (zero, zero, jnp.zeros((t, HEAD_DIM), F32)), True)
        _, _, dq = _sb_walk(i, carry, tile)
        dq_ref[...] = dq * 0.125

        @pl.when(i == nq - 1)
        def _():
            pltpu.sync_copy(dk_acc, dk_hbm.at[h])
            pltpu.sync_copy(dv_acc, dv_hbm.at[h])

    return pl.pallas_call(
        body, name="sb_bwd", grid=(N_HEADS, nq),
        in_specs=[q_spec, k_spec, v_spec, row_spec(HEAD_DIM), row_spec(HEAD_DIM)],
        out_specs=[row_spec(HEAD_DIM), any_spec, any_spec],
        out_shape=[jax.ShapeDtypeStruct((N_HEADS, s, HEAD_DIM), F32)] * 3,
        scratch_shapes=[pltpu.VMEM((s, HEAD_DIM), F32), pltpu.VMEM((s, HEAD_DIM), F32)],
        compiler_params=_params(("arbitrary", "arbitrary")),
    )(qkv, qkv, qkv, do, o)


def _branch_inputs(refs, br):
    ya_ref, yb_ref, yc_ref, yd_ref = refs
    if br == 1:
        return yb_ref[...]
    return _heads_to_lanes((ya_ref, None, yc_ref, yd_ref)[br])


def outproj_fwd(x, ya, yb, yc, yd, gates, bg, wout):
    s = x.shape[0]
    tm = min(ROW_T, s)

    def body(x_ref, ya_ref, yb_ref, yc_ref, yd_ref, gates_ref, bg_ref, w_ref, out_ref):
        pieces = []
        for br in range(4):
            cols = slice(br * D_BRANCH, (br + 1) * D_BRANCH)
            y = _branch_inputs((ya_ref, yb_ref, yc_ref, yd_ref), br)
            r = lax.rsqrt(jnp.mean(y * y, axis=-1, keepdims=True) + EPS)
            gt = gates_ref[:, cols]
            pieces.append((y * r * bg_ref[:, cols]) * (gt * _sigmoid(gt)))
        merged = jnp.concatenate(pieces, axis=1).astype(BF16)
        out_ref[...] = x_ref[...] + _dot(merged, w_ref[...])

    head_spec = pl.BlockSpec((N_HEADS, tm, HEAD_DIM), lambda i: (0, i, 0))
    return pl.pallas_call(
        body, name="outproj_fwd", grid=(s // tm,),
        in_specs=[pl.BlockSpec((tm, D_MODEL), lambda i: (i, 0)),
                  head_spec, pl.BlockSpec((tm, D_BRANCH), lambda i: (i, 0)), head_spec, head_spec,
                  pl.BlockSpec((tm, D_MODEL), lambda i: (i, 0)),
                  pl.BlockSpec((1, D_MODEL), lambda i: (0, 0)),
                  pl.BlockSpec((D_MODEL, D_MODEL), lambda i: (0, 0))],
        out_specs=pl.BlockSpec((tm, D_MODEL), lambda i: (i, 0)),
        out_shape=jax.ShapeDtypeStruct((s, D_MODEL), F32),
        compiler_params=_params(("arbitrary",)),
    )(x, ya, yb, yc, yd, gates, bg, wout)


def outproj_bwd(dout, ya, yb, yc, yd, gates, bg, wout):
    s = dout.shape[0]
    tm = min(ROW_T, s)

    def body(dout_ref, ya_ref, yb_ref, yc_ref, yd_ref, gates_ref, bg_ref, w_ref,
             dya_ref, dyb_ref, dyc_ref, dyd_ref, dgates_ref, dbg_ref, dw_ref):
        i = pl.program_id(0)

        @pl.when(i == 0)
        def _():
            dbg_ref[...] = jnp.zeros_like(dbg_ref)
            dw_ref[...] = jnp.zeros_like(dw_ref)

        doutb = dout_ref[...].astype(BF16)
        dmerged = _dot_nt(doutb, w_ref[...])
        pieces = []
        for br in range(4):
            cols = slice(br * D_BRANCH, (br + 1) * D_BRANCH)
            y = _branch_inputs((ya_ref, yb_ref, yc_ref, yd_ref), br)
            r = lax.rsqrt(jnp.mean(y * y, axis=-1, keepdims=True) + EPS)
            yn = y * r
            bgv = bg_ref[:, cols]
            gt = gates_ref[:, cols]
            sig = _sigmoid(gt)
            act = gt * sig
            n = yn * bgv
            pieces.append(n * act)
            dm = dmerged[:, cols]
            dn = dm * act
            dgates_ref[:, cols] = (dm * n * (sig * (1.0 + gt * (1.0 - sig)))).astype(BF16)
            dbg_ref[:, cols] += jnp.sum(dn * yn, axis=0, keepdims=True)
            u = dn * bgv
            dy = r * (u - yn * jnp.mean(yn * u, axis=-1, keepdims=True))
            if br == 1:
                dyb_ref[...] = dy
            else:
                dref = (dya_ref, None, dyc_ref, dyd_ref)[br]
                for hh in range(N_HEADS):
                    dref[hh] = dy[:, hh * HEAD_DIM:(hh + 1) * HEAD_DIM].astype(BF16)
        merged = jnp.concatenate(pieces, axis=1).astype(BF16)
        dw_ref[...] += _dot_tn(merged, doutb)

    head_spec = pl.BlockSpec((N_HEADS, tm, HEAD_DIM), lambda i: (0, i, 0))
    head_shape = jax.ShapeDtypeStruct((N_HEADS, s, HEAD_DIM), BF16)
    return pl.pallas_call(
        body, name="outproj_bwd", grid=(s // tm,),
        in_specs=[pl.BlockSpec((tm, D_MODEL), lambda i: (i, 0)),
                  head_spec, pl.BlockSpec((tm, D_BRANCH), lambda i: (i, 0)), head_spec, head_spec,
                  pl.BlockSpec((tm, D_MODEL), lambda i: (i, 0)),
                  pl.BlockSpec((1, D_MODEL), lambda i: (0, 0)),
                  pl.BlockSpec((D_MODEL, D_MODEL), lambda i: (0, 0))],
        out_specs=[head_spec, pl.BlockSpec((tm, D_BRANCH), lambda i: (i, 0)), head_spec, head_spec,
                   pl.BlockSpec((tm, D_MODEL), lambda i: (i, 0)),
                   pl.BlockSpec((1, D_MODEL), lambda i: (0, 0)),
                   pl.BlockSpec((D_MODEL, D_MODEL), lambda i: (0, 0))],
        out_shape=[head_shape, jax.ShapeDtypeStruct((s, D_BRANCH), F32), head_shape, head_shape,
                   jax.ShapeDtypeStruct((s, D_MODEL), BF16),
                   jax.ShapeDtypeStruct((1, D_MODEL), F32),
                   jax.ShapeDtypeStruct((D_MODEL, D_MODEL), F32)],
        compiler_params=_params(("arbitrary",)),
    )(dout, ya, yb, yc, yd, gates, bg, wout)


def final_loss(x, tgt, g):
    s = x.shape[0]
    tm = min(ROW_T, s)

    def body(x_ref, t_ref, g_ref, loss_ref, dx_ref, dg_ref):
        i = pl.program_id(0)

        @pl.when(i == 0)
        def _():
            loss_ref[...] = jnp.zeros_like(loss_ref)
            dg_ref[...] = jnp.zeros_like(dg_ref)

        xv = x_ref[...]
        gv = g_ref[...]
        r = lax.rsqrt(jnp.mean(xv * xv, axis=-1, keepdims=True) + EPS)
        xn = xv * r
        err = xn * gv - t_ref[...]
        loss_ref[...] += jnp.sum(err * err) * (0.5 / D_MODEL)
        dy = err * (1.0 / D_MODEL)
        u = dy * gv
        dx_ref[...] = r * (u - xn * jnp.mean(xn * u, axis=-1, keepdims=True))
        dg_ref[...] += jnp.sum(dy * xn, axis=0, keepdims=True)

    return pl.pallas_call(
        body, name="final_loss", grid=(s // tm,),
        in_specs=[pl.BlockSpec((tm, D_MODEL), lambda i: (i, 0)),
                  pl.BlockSpec((tm, D_MODEL), lambda i: (i, 0)),
                  pl.BlockSpec((1, D_MODEL), lambda i: (0, 0))],
        out_specs=[pl.BlockSpec((1, 128), lambda i: (0, 0)),
                   pl.BlockSpec((tm, D_MODEL), lambda i: (i, 0)),
                   pl.BlockSpec((1, D_MODEL), lambda i: (0, 0))],
        out_shape=[jax.ShapeDtypeStruct((1, 128), F32),
                   jax.ShapeDtypeStruct((s, D_MODEL), F32),
                   jax.ShapeDtypeStruct((1, D_MODEL), F32)],
        compiler_params=_params(("arbitrary",)),
    )(x, tgt, g)


def _rel_index():
    i = np.arange(A_TQ)[:, None]
    j = np.arange(A_BAND)[None, :]
    rel = np.clip(i - j + (A_BAND - A_TQ), -MAX_REL, MAX_REL) + MAX_REL
    dchunk = i // CHUNK + LOOKBACK - j // CHUNK
    valid = (dchunk >= 0) & (dchunk <= LOOKBACK)
    return jnp.asarray(np.where(valid, rel, -1).astype(np.int32))


def _layer_consts(p):
    tbias = relbias_tile(p["rel_bias"], _rel_index())
    return dict(
        norm_g=p["norm_g"].reshape(1, D_MODEL),
        v_gain=p["v_gain"].reshape(1, D_BRANCH),
        b_col=p["b_s"].reshape(N_HEADS, SG_CHUNK, 1),
        bg=p["branch_gain"].reshape(1, D_MODEL),
        tbias=tbias,
    )


def _gate_layout(fp, b_f, s):
    nb = s // 128
    ft = fp[:, :N_HEADS].T.reshape(N_HEADS * nb, 128)
    bcol = jnp.repeat(b_f, nb).reshape(N_HEADS * nb, 1)
    return ft, bcol


def layer_fwd(x, p):
    s = x.shape[0]
    c = _layer_consts(p)
    h, qkv, gates, uv, fp = inproj_fwd(x, c["norm_g"], p["wp"])
    keep = A_BAND - A_TQ
    kpad = jnp.pad(qkv[1], ((0, 0), (keep, 0), (0, 0)))
    vpad = jnp.pad(qkv[2], ((0, 0), (keep, 0), (0, 0)))
    ya, lse_a = mix_a_fwd(qkv, kpad, vpad, c["tbias"])
    yb = mix_b_fwd(uv, c["v_gain"], p["w_s"], c["b_col"])
    ft, bcol = _gate_layout(fp, p["b_f"], s)
    cum = fox_gate_fwd(ft, bcol).reshape(N_HEADS, s)
    c_col = cum.reshape(N_HEADS, s, 1)
    c_row = cum.reshape(N_HEADS, s // ATT_T, 1, ATT_T)
    yc, lse_c = fox_fwd(qkv, c_col, c_row)
    yd = sb_fwd(qkv)
    out = outproj_fwd(x, ya, yb, yc, yd, gates, c["bg"], p["wout"])
    saved = dict(consts=c, x=x, h=h, qkv=qkv, gates=gates, uv=uv, kpad=kpad, vpad=vpad, ft=ft, bcol=bcol,
                 c_col=c_col, c_row=c_row, ya=ya, lse_a=lse_a, yb=yb, yc=yc, lse_c=lse_c, yd=yd)
    return out, saved


def layer_bwd(dout, p, sv):
    s = dout.shape[0]
    c = sv["consts"]
    dya, dyb, dyc, dyd, dgates, dbg, dwout = outproj_bwd(
        dout, sv["ya"], sv["yb"], sv["yc"], sv["yd"], sv["gates"], c["bg"], p["wout"])
    keep = A_BAND - A_TQ
    dqa, dkpad, dvpad, dt = mix_a_bwd(sv["qkv"], sv["kpad"], sv["vpad"], c["tbias"], dya, sv["ya"], sv["lse_a"])
    dka, dva = dkpad[:, keep:], dvpad[:, keep:]
    drel = relbias_grad(dt, _rel_index())[:N_HEADS, :2 * MAX_REL + 1]
    duv, dws, dbs, dvgain = mix_b_bwd(sv["uv"], c["v_gain"], p["w_s"], c["b_col"], dyb)
    dqc, dkc, dvc, dc = fox_bwd(sv["qkv"], sv["c_col"], sv["c_row"], dyc, sv["yc"], sv["lse_c"])
    dft, dbf = fox_gate_bwd(sv["ft"], sv["bcol"], dc.reshape(N_HEADS * (s // 128), 128))
    dfp = jnp.pad(dft.reshape(N_HEADS, s).T, ((0, 0), (0, 128 - N_HEADS)))
    dqd, dkd, dvd = sb_bwd(sv["qkv"], dyd, sv["yd"])
    dp, dx, dnorm = inproj_bwd((dqa, dka, dva, dqc, dkc, dvc, dqd, dkd, dvd), dgates, duv, dfp,
                               p["wp"], sv["x"], c["norm_g"], dout)
    dwp = weight_grad(sv["h"], dp, "inproj_wgrad")
    grads = dict(norm_g=dnorm.reshape(D_MODEL), wp=dwp, b_f=dbf[:N_HEADS, 0], rel_bias=drel,
                 w_s=dws, b_s=dbs.reshape(N_HEADS, SG_CHUNK), v_gain=dvgain.reshape(D_BRANCH),
                 branch_gain=dbg.reshape(4, D_BRANCH), wout=dwout)
    return dx, grads


def local_step(x, tgt, layers, final_g):
    saved = []
    cur = x
    for p in layers:
        cur, sv = layer_fwd(cur, p)
        saved.append(sv)
    loss, dcur, dfinal = final_loss(cur, tgt, final_g.reshape(1, D_MODEL))
    grads = [None] * len(layers)
    for l in reversed(range(len(layers))):
        dcur, grads[l] = layer_bwd(dcur, layers[l], saved[l])
    return loss[0, 0], dcur, grads, dfinal.reshape(D_MODEL)


def gather_weights(wb, wf):
    def body(wb_ref, wf_ref, ob_ref, of_ref, send_sems, recv_sems, loc_sems):
        x, y, c = lax.axis_index("x"), lax.axis_index("y"), lax.axis_index("c")
        me = 2 * x + y
        chips = [(1 - x, y), (x, 1 - y), (1 - x, 1 - y)]
        pairs = [(wb_ref, ob_ref), (wf_ref, of_ref)]
        local = [pltpu.make_async_copy(src, dst.at[me], loc_sems.at[n]) for n, (src, dst) in enumerate(pairs)]
        for cp in local:
            cp.start()

        def copy(j, n, slot):
            src, dst = pairs[n]
            return pltpu.make_async_remote_copy(
                src_ref=src, dst_ref=dst.at[slot], send_sem=send_sems.at[2 * j + n], recv_sem=recv_sems.at[2 * j + n],
                device_id=(chips[j][0], chips[j][1], c), device_id_type=MESH)

        sends = [copy(j, n, me) for j in range(3) for n in range(2)]
        for cp in sends:
            cp.start()
        for j in range(3):
            for n in range(2):
                copy(j, n, 2 * chips[j][0] + chips[j][1]).wait_recv()
        for cp in sends:
            cp.wait_send()
        for cp in local:
            cp.wait()

    any_spec = pl.BlockSpec(memory_space=pl.ANY)
    return pl.pallas_call(
        body, name="gather_weights",
        in_specs=[any_spec, any_spec], out_specs=[any_spec, any_spec],
        out_shape=[jax.ShapeDtypeStruct((4,) + wb.shape, wb.dtype), jax.ShapeDtypeStruct((4,) + wf.shape, wf.dtype)],
        scratch_shapes=[pltpu.SemaphoreType.DMA((6,)), pltpu.SemaphoreType.DMA((6,)), pltpu.SemaphoreType.DMA((2,))],
    )(wb, wf)


def exchange_grads(send):
    def body(s_ref, r_ref, send_sems, recv_sems, loc_sem):
        x, y, c = lax.axis_index("x"), lax.axis_index("y"), lax.axis_index("c")
        me_chip = 2 * x + y
        me = 4 * x + 2 * y + c
        peers = [(x, y, 1 - c)]
        for px, py in [(1 - x, y), (x, 1 - y), (1 - x, 1 - y)]:
            peers += [(px, py, c), (px, py, 1 - c)]
        local = pltpu.make_async_copy(s_ref.at[me_chip], r_ref.at[me], loc_sem)
        local.start()

        def copy(n, chip, slot):
            return pltpu.make_async_remote_copy(
                src_ref=s_ref.at[chip], dst_ref=r_ref.at[slot], send_sem=send_sems.at[n], recv_sem=recv_sems.at[n],
                device_id=peers[n], device_id_type=MESH)

        sends = [copy(n, 2 * px + py, me) for n, (px, py, _) in enumerate(peers)]
        for cp in sends:
            cp.start()
        for n, (px, py, pc) in enumerate(peers):
            copy(n, me_chip, 4 * px + 2 * py + pc).wait_recv()
        for cp in sends:
            cp.wait_send()
        local.wait()

    any_spec = pl.BlockSpec(memory_space=pl.ANY)
    return pl.pallas_call(
        body, name="exchange_grads",
        in_specs=[any_spec], out_specs=any_spec,
        out_shape=jax.ShapeDtypeStruct((8,) + send.shape[1:], send.dtype),
        scratch_shapes=[pltpu.SemaphoreType.DMA((7,)), pltpu.SemaphoreType.DMA((7,)), pltpu.SemaphoreType.DMA],
    )(send)


def adamw_reduce(parts, w, m, v):
    rows = w.shape[0]
    tr = 512
    c1 = 1.0 - ADAM_B1 ** ADAM_STEP
    c2 = 1.0 - ADAM_B2 ** ADAM_STEP

    def body(p_ref, w_ref, m_ref, v_ref, g_ref, d_ref, nm_ref, nv_ref):
        g = p_ref[0]
        for n in range(1, 8):
            g = g + p_ref[n]
        g_ref[...] = g
        nm = ADAM_B1 * m_ref[...] + (1.0 - ADAM_B1) * g
        nv = ADAM_B2 * v_ref[...] + (1.0 - ADAM_B2) * (g * g)
        nm_ref[...] = nm
        nv_ref[...] = nv
        d_ref[...] = -ADAM_LR * ((nm / c1) / (jnp.sqrt(nv / c2) + ADAM_EPS) + ADAM_WD * w_ref[...])

    spec = pl.BlockSpec((tr, 128), lambda i: (i, 0))
    shape = jax.ShapeDtypeStruct((rows, 128), F32)
    return pl.pallas_call(
        body, name="adamw_reduce", grid=(rows // tr,),
        in_specs=[pl.BlockSpec((8, tr, 128), lambda i: (0, i, 0)), spec, spec, spec],
        out_specs=[spec] * 4, out_shape=[shape] * 4,
        compiler_params=_params(("arbitrary",)),
    )(parts, w, m, v)


SHARDED = ("w_in", "w_out", "branch_gain")
SMALL = ("norm_g", "b_f", "rel_bias", "w_s", "b_s", "v_gain", "final_g")
WEIGHTS = ("norm_g", "w_in", "b_f", "rel_bias", "w_s", "b_s", "v_gain", "branch_gain", "w_out", "final_g")
PACK_ORDER = SHARDED + SMALL
PACK_ROW_TILE = 512


def _rows_of(shape):
    return -(-int(np.prod(shape)) // 128)


def _pack(leaves):
    parts = []
    for a in leaves:
        flat = a.reshape(-1).astype(F32)
        parts.append(jnp.pad(flat, (0, _rows_of(a.shape) * 128 - flat.shape[0])))
    flat = jnp.concatenate(parts)
    rows = flat.shape[0] // 128
    total = -(-rows // PACK_ROW_TILE) * PACK_ROW_TILE
    return jnp.pad(flat, (0, (total - rows) * 128)).reshape(total, 128)


def _unpack(slab, shapes):
    out, row = [], 0
    for shp in shapes:
        n = int(np.prod(shp))
        r = _rows_of(shp)
        out.append(slab[row:row + r].reshape(-1)[:n].reshape(shp))
        row += r
    return out


def _pack_w_in(w):
    return jnp.concatenate([w[:, :2816], w[:, 2820:], w[:, 2816:2820],
                            jnp.zeros((w.shape[0], N_PACK - N_IN), w.dtype)], axis=1)


def _unpack_w_in(wp):
    return jnp.concatenate([wp[:, :2816], wp[:, F_COL:F_COL + N_HEADS], wp[:, 2816:F_COL]], axis=1)


def kernel(x, norm_g, w_in, b_f, rel_bias, w_s, b_s, v_gain, branch_gain, w_out, final_g, loss_target, m_norm_g, m_w_in, m_b_f, m_rel_bias, m_w_s, m_b_s, m_v_gain, m_branch_gain, m_w_out, m_final_g, v_norm_g, v_w_in, v_b_f, v_rel_bias, v_w_s, v_b_s, v_v_gain, v_branch_gain, v_w_out, v_final_g):
    depth = norm_g.shape[0]
    weights = dict(norm_g=norm_g, w_in=w_in, b_f=b_f, rel_bias=rel_bias, w_s=w_s, b_s=b_s, v_gain=v_gain,
                   branch_gain=branch_gain, w_out=w_out, final_g=final_g)
    mom1 = dict(norm_g=m_norm_g, w_in=m_w_in, b_f=m_b_f, rel_bias=m_rel_bias, w_s=m_w_s, b_s=m_b_s,
                v_gain=m_v_gain, branch_gain=m_branch_gain, w_out=m_w_out, final_g=m_final_g)
    mom2 = dict(norm_g=v_norm_g, w_in=v_w_in, b_f=v_b_f, rel_bias=v_rel_bias, w_s=v_w_s, b_s=v_b_s,
                v_gain=v_v_gain, branch_gain=v_branch_gain, w_out=v_w_out, final_g=v_final_g)

    n_in_rows = _rows_of(w_in.shape)
    n_out_rows = _rows_of(w_out.shape)
    wb = jnp.concatenate([w_in.astype(BF16).reshape(n_in_rows, 128), w_out.astype(BF16).reshape(n_out_rows, 128)])
    wf = jnp.pad(branch_gain.reshape(-1), (0, 8 * 128 - branch_gain.size)).reshape(8, 128)
    gb, gf = gather_weights(wb, wf)
    w_in_full = gb[:, :n_in_rows].reshape((4,) + w_in.shape)
    w_in_full = jnp.moveaxis(w_in_full, 0, 2).reshape(depth, D_MODEL, N_IN)
    w_out_full = gb[:, n_in_rows:].reshape((4,) + w_out.shape)
    w_out_full = jnp.moveaxis(w_out_full, 0, 1).reshape(depth, D_MODEL, D_MODEL)
    bg_full = gf.reshape(4, -1)[:, :branch_gain.size].reshape((4,) + branch_gain.shape)
    bg_full = jnp.moveaxis(bg_full, 0, 2).reshape(depth, 4, D_BRANCH)

    layers = [dict(norm_g=norm_g[l], wp=_pack_w_in(w_in_full[l]), b_f=b_f[l], rel_bias=rel_bias[l], w_s=w_s[l],
                   b_s=b_s[l], v_gain=v_gain[l], branch_gain=bg_full[l], wout=w_out_full[l]) for l in range(depth)]

    loss_part, grad_x, lgrads, dfinal = local_step(x[0], loss_target[0], layers, final_g)
    loss = lax.psum(loss_part, ("x", "y", "c"))

    stack = lambda k: jnp.stack([g[k] for g in lgrads])
    d_w_in = jnp.stack([_unpack_w_in(g["wp"]) for g in lgrads])
    d_w_out = stack("wout")
    d_bg = stack("branch_gain")
    small = dict(norm_g=stack("norm_g"), b_f=stack("b_f"), rel_bias=stack("rel_bias"), w_s=stack("w_s"),
                 b_s=stack("b_s"), v_gain=stack("v_gain"), final_g=dfinal)
    slabs = []
    for sidx in range(4):
        leaves = [d_w_in[:, :, sidx * N_SHARD:(sidx + 1) * N_SHARD],
                  d_w_out[:, sidx * D_BRANCH:(sidx + 1) * D_BRANCH, :],
                  d_bg[:, :, sidx * HEAD_DIM:(sidx + 1) * HEAD_DIM]] + [small[k] for k in SMALL]
        slabs.append(_pack(leaves))
    parts = exchange_grads(jnp.stack(slabs))

    pack_local = lambda d: _pack([d[k] for k in PACK_ORDER])
    g_slab, d_slab, m_slab, v_slab = adamw_reduce(parts, pack_local(weights), pack_local(mom1), pack_local(mom2))
    shapes = [weights[k].shape for k in PACK_ORDER]
    outs = {}
    for tag, slab in (("grad", g_slab), ("delta", d_slab), ("new_m", m_slab), ("new_v", v_slab)):
        for k, a in zip(PACK_ORDER, _unpack(slab, shapes)):
            outs[tag, k] = a
    result = [loss, grad_x[None]]
    for tag in ("grad", "delta", "new_m", "new_v"):
        result += [outs[tag, k] for k in WEIGHTS]
    return tuple(result)
```

```python
import functools

import jax
import jax.numpy as jnp
import numpy as np
from jax import lax
from jax.experimental import pallas as pl
from jax.experimental.pallas import tpu as pltpu

F32 = jnp.float32
BF16 = jnp.bfloat16
MESH = pl.DeviceIdType.MESH

D_MODEL = 1024
D_BRANCH = 256
N_HEADS = 4
HEAD_DIM = 64
CHUNK = 64
LOOKBACK = 8
MAX_REL = 128
SG_CHUNK = 128
EPS = 1e-6
N_IN = 3844
N_PACK = 3968
F_COL = 3840
N_SHARD = 961
NEG = -1e30

A_TQ = 128
A_BAND = A_TQ + LOOKBACK * CHUNK
A_QB = 512
ATT_T = 256
FOX_MID = 4
FOX_WIDE = 8
FOX_DEAD = -110.0
SB_DEAD = -110.0
ROW_T = 512
VMEM_LIMIT = 56 * 1024 * 1024

ADAM_LR = 0.001
ADAM_B1 = 0.9
ADAM_B2 = 0.999
ADAM_EPS = 1e-08
ADAM_WD = 0.01
ADAM_STEP = 10

SEC_A_Q, SEC_A_K, SEC_A_V, SEC_A_G = 0, 256, 512, 768
SEC_B_U, SEC_B_V, SEC_B_G = 1024, 1280, 1536
SEC_C_Q, SEC_C_K, SEC_C_V, SEC_C_G = 1792, 2048, 2304, 2560
SEC_D_Q, SEC_D_K, SEC_D_V, SEC_D_G = 2816, 3072, 3328, 3584
QKV_SECS = (SEC_A_Q, SEC_A_K, SEC_A_V, SEC_C_Q, SEC_C_K, SEC_C_V, SEC_D_Q, SEC_D_K, SEC_D_V)
GATE_SECS = (SEC_A_G, SEC_B_G, SEC_C_G, SEC_D_G)


def _dot(a, b):
    return jnp.dot(a, b, preferred_element_type=F32)


def _dot_nt(a, b):
    return lax.dot_general(a, b, (((1,), (1,)), ((), ())), preferred_element_type=F32)


def _dot_tn(a, b):
    return lax.dot_general(a, b, (((0,), (0,)), ((), ())), preferred_element_type=F32)


def _split2(x):
    hi = x.astype(BF16)
    lo = (x - hi.astype(F32)).astype(BF16)
    return hi, lo


def _split3(x):
    hi = x.astype(BF16)
    r = x - hi.astype(F32)
    mid = r.astype(BF16)
    lo = (r - mid.astype(F32)).astype(BF16)
    return hi, mid, lo


def _sigmoid(x):
    return 1.0 / (1.0 + jnp.exp(-x))


def _params(sem=None, vmem=VMEM_LIMIT):
    return pltpu.CompilerParams(dimension_semantics=sem, vmem_limit_bytes=vmem)


def _heads_to_lanes(ref):
    return jnp.concatenate([ref[h] for h in range(N_HEADS)], axis=1)


def inproj_fwd(x, g, wp):
    s = x.shape[0]
    tm = min(ROW_T, s)

    def body(x_ref, g_ref, w_ref, h_ref, qkv_ref, gates_ref, uv_ref, f_ref):
        xv = x_ref[...]
        r = lax.rsqrt(jnp.mean(xv * xv, axis=-1, keepdims=True) + EPS)
        h = (xv * r * g_ref[...]).astype(BF16)
        h_ref[...] = h
        for n, off in enumerate(QKV_SECS):
            p = _dot(h, w_ref[:, off:off + D_BRANCH])
            for hh in range(N_HEADS):
                qkv_ref[n, hh] = p[:, hh * HEAD_DIM:(hh + 1) * HEAD_DIM].astype(BF16)
        for n, off in enumerate(GATE_SECS):
            gates_ref[:, n * D_BRANCH:(n + 1) * D_BRANCH] = _dot(h, w_ref[:, off:off + D_BRANCH])
        uv_ref[...] = _dot(h, w_ref[:, SEC_B_U:SEC_B_U + 2 * D_BRANCH])
        f_ref[...] = _dot(h, w_ref[:, F_COL:F_COL + 128])

    return pl.pallas_call(
        body, name="inproj_fwd", grid=(s // tm,),
        in_specs=[pl.BlockSpec((tm, D_MODEL), lambda i: (i, 0)),
                  pl.BlockSpec((1, D_MODEL), lambda i: (0, 0)),
                  pl.BlockSpec((D_MODEL, N_PACK), lambda i: (0, 0))],
        out_specs=[pl.BlockSpec((tm, D_MODEL), lambda i: (i, 0)),
                   pl.BlockSpec((9, N_HEADS, tm, HEAD_DIM), lambda i: (0, 0, i, 0)),
                   pl.BlockSpec((tm, D_MODEL), lambda i: (i, 0)),
                   pl.BlockSpec((tm, 2 * D_BRANCH), lambda i: (i, 0)),
                   pl.BlockSpec((tm, 128), lambda i: (i, 0))],
        out_shape=[jax.ShapeDtypeStruct((s, D_MODEL), BF16),
                   jax.ShapeDtypeStruct((9, N_HEADS, s, HEAD_DIM), BF16),
                   jax.ShapeDtypeStruct((s, D_MODEL), F32),
                   jax.ShapeDtypeStruct((s, 2 * D_BRANCH), F32),
                   jax.ShapeDtypeStruct((s, 128), F32)],
        compiler_params=_params(("arbitrary",)),
    )(x, g, wp)


def inproj_bwd(dqkv, dgates, duv, dfp, wp, x, g, dres):
    s = x.shape[0]
    tm = min(ROW_T, s)

    def body(*refs):
        dq_refs = refs[:9]
        dgates_ref, duv_ref, dfp_ref, w_ref, x_ref, g_ref, dres_ref, dp_ref, dx_ref, dg_ref = refs[9:]
        i = pl.program_id(0)
        a_q, a_k, a_v, c_q, c_k, c_v, d_q, d_k, d_v = [_heads_to_lanes(r).astype(BF16) for r in dq_refs]
        dgt = dgates_ref[...]
        duv_b = duv_ref[...].astype(BF16)
        dp = jnp.concatenate(
            [a_q, a_k, a_v, dgt[:, 0:256], duv_b, dgt[:, 256:512], c_q, c_k, c_v, dgt[:, 512:768],
             d_q, d_k, d_v, dgt[:, 768:1024], dfp_ref[...].astype(BF16)], axis=1)
        dp_ref[...] = dp
        dh = _dot_nt(dp, w_ref[...])
        xv = x_ref[...]
        r = lax.rsqrt(jnp.mean(xv * xv, axis=-1, keepdims=True) + EPS)
        xn = xv * r
        u = dh * g_ref[...]
        dx_ref[...] = dres_ref[...] + r * (u - xn * jnp.mean(xn * u, axis=-1, keepdims=True))

        @pl.when(i == 0)
        def _():
            dg_ref[...] = jnp.zeros_like(dg_ref)

        dg_ref[...] += jnp.sum(dh * xn, axis=0, keepdims=True)

    head_spec = pl.BlockSpec((N_HEADS, tm, HEAD_DIM), lambda i: (0, i, 0))
    return pl.pallas_call(
        body, name="inproj_bwd", grid=(s // tm,),
        in_specs=[head_spec] * 9 + [
            pl.BlockSpec((tm, D_MODEL), lambda i: (i, 0)),
            pl.BlockSpec((tm, 2 * D_BRANCH), lambda i: (i, 0)),
            pl.BlockSpec((tm, 128), lambda i: (i, 0)),
            pl.BlockSpec((D_MODEL, N_PACK), lambda i: (0, 0)),
            pl.BlockSpec((tm, D_MODEL), lambda i: (i, 0)),
            pl.BlockSpec((1, D_MODEL), lambda i: (0, 0)),
            pl.BlockSpec((tm, D_MODEL), lambda i: (i, 0))],
        out_specs=[pl.BlockSpec((tm, N_PACK), lambda i: (i, 0)),
                   pl.BlockSpec((tm, D_MODEL), lambda i: (i, 0)),
                   pl.BlockSpec((1, D_MODEL), lambda i: (0, 0))],
        out_shape=[jax.ShapeDtypeStruct((s, N_PACK), BF16),
                   jax.ShapeDtypeStruct((s, D_MODEL), F32),
                   jax.ShapeDtypeStruct((1, D_MODEL), F32)],
        compiler_params=_params(("arbitrary",)),
    )(*dqkv, dgates, duv, dfp, wp, x, g, dres)


def weight_grad(a, b, name):
    s, m = a.shape
    n = b.shape[1]
    tm = min(ROW_T, s)
    tmm = 256
    nsteps = s // tm

    def body(a_ref, b_ref, o_ref):
        k = pl.program_id(1)

        @pl.when(k == 0)
        def _():
            o_ref[...] = jnp.zeros_like(o_ref)

        o_ref[...] += _dot_tn(a_ref[...], b_ref[...])

    return pl.pallas_call(
        body, name=name, grid=(m // tmm, nsteps),
        in_specs=[pl.BlockSpec((tm, tmm), lambda j, k: (k, j)),
                  pl.BlockSpec((tm, n), lambda j, k: (k, 0))],
        out_specs=pl.BlockSpec((tmm, n), lambda j, k: (j, 0)),
        out_shape=jax.ShapeDtypeStruct((m, n), F32),
        compiler_params=_params(("arbitrary", "arbitrary")),
    )(a, b)


def _a_specs(s):
    nq = s // A_QB
    q_spec = pl.BlockSpec((None, None, A_QB, HEAD_DIM), lambda h, i: (0, h, jnp.minimum(i, nq - 1), 0))
    kv_specs = [pl.BlockSpec((None, A_QB, HEAD_DIM), lambda h, i, m=m: (h, jnp.minimum(i + m, nq), 0)) for m in range(2)]
    t_spec = pl.BlockSpec((None, A_TQ, A_BAND), lambda h, i: (h, 0, 0))
    return nq, q_spec, kv_specs, t_spec


def _a_scores(q_ref, k, t_ref, i, j):
    rows = slice(j * A_TQ, (j + 1) * A_TQ)
    qs = q_ref[rows, :] * 0.125
    kj = k[j * A_TQ:j * A_TQ + A_BAND, :]
    sc = _dot_nt(qs, kj) + t_ref[...]
    col = lax.broadcasted_iota(jnp.int32, (A_TQ, A_BAND), 1)
    sc = jnp.where(col >= (A_BAND - A_TQ) - i * A_QB - j * A_TQ, sc, NEG)
    return rows, qs, kj, sc


def mix_a_fwd(qkv, kpad, vpad, tbias):
    s = qkv.shape[2]
    nq, q_spec, kv_specs, t_spec = _a_specs(s)

    def body(q_ref, k0_ref, k1_ref, v0_ref, v1_ref, t_ref, o_ref, lse_ref):
        i = pl.program_id(1)
        k = jnp.concatenate([k0_ref[...], k1_ref[...]], axis=0)
        v = jnp.concatenate([v0_ref[...], v1_ref[...]], axis=0)
        for j in range(A_QB // A_TQ):
            rows, _, _, sc = _a_scores(q_ref, k, t_ref, i, j)
            m = jnp.max(sc, axis=-1, keepdims=True)
            p = jnp.exp(sc - m)
            l = jnp.sum(p, axis=-1, keepdims=True)
            o_ref[rows, :] = _dot(p.astype(BF16), v[j * A_TQ:j * A_TQ + A_BAND, :]) / l
            lse_ref[rows, :] = m + jnp.log(l)

    return pl.pallas_call(
        body, name="mix_a_fwd", grid=(N_HEADS, nq),
        in_specs=[q_spec] + kv_specs + kv_specs + [t_spec],
        out_specs=[pl.BlockSpec((None, A_QB, HEAD_DIM), lambda h, i: (h, i, 0)),
                   pl.BlockSpec((None, A_QB, 1), lambda h, i: (h, i, 0))],
        out_shape=[jax.ShapeDtypeStruct((N_HEADS, s, HEAD_DIM), F32),
                   jax.ShapeDtypeStruct((N_HEADS, s, 1), F32)],
        compiler_params=_params(("arbitrary", "arbitrary")),
    )(qkv, kpad, kpad, vpad, vpad, tbias)


def mix_a_bwd(qkv, kpad, vpad, tbias, do, o, lse):
    s = qkv.shape[2]
    nq, q_spec, kv_specs, t_spec = _a_specs(s)
    row_spec = lambda w: pl.BlockSpec((None, A_QB, w), lambda h, i: (h, jnp.minimum(i, nq - 1), 0))
    keep = A_BAND - A_TQ
    win = 2 * A_QB

    def body(q_ref, k0_ref, k1_ref, v0_ref, v1_ref, t_ref, do_ref, o_ref, lse_ref,
             dq_ref, dk_ref, dv_ref, dt_ref, dk_win, dv_win):
        i = pl.program_id(1)

        @pl.when(i == 0)
        def _():
            dk_win[...] = jnp.zeros_like(dk_win)
            dv_win[...] = jnp.zeros_like(dv_win)
            dt_ref[...] = jnp.zeros_like(dt_ref)

        @pl.when(i < nq)
        def _():
            k = jnp.concatenate([k0_ref[...], k1_ref[...]], axis=0)
            v = jnp.concatenate([v0_ref[...], v1_ref[...]], axis=0)
            dt = jnp.zeros((A_TQ, A_BAND), F32)
            for j in range(A_QB // A_TQ):
                rows, qs, kj, sc = _a_scores(q_ref, k, t_ref, i, j)
                keys = slice(j * A_TQ, j * A_TQ + A_BAND)
                dob = do_ref[rows, :]
                p = jnp.exp(sc - lse_ref[rows, :])
                delta = jnp.sum(o_ref[rows, :] * dob.astype(F32), axis=-1, keepdims=True)
                ds = p * (_dot_nt(dob, v[keys, :]) - delta)
                dsb = ds.astype(BF16)
                dq_ref[rows, :] = _dot(dsb, kj) * 0.125
                dk_win[keys, :] += _dot_tn(dsb, qs)
                dv_win[keys, :] += _dot_tn(p.astype(BF16), dob)
                dt = dt + ds
            dt_ref[...] += dt

        dk_ref[...] = dk_win[0:A_QB, :]
        dv_ref[...] = dv_win[0:A_QB, :]
        dk_rest = dk_win[A_QB:win, :]
        dv_rest = dv_win[A_QB:win, :]
        dk_win[0:A_QB, :] = dk_rest
        dv_win[0:A_QB, :] = dv_rest
        dk_win[A_QB:win, :] = jnp.zeros((A_QB, HEAD_DIM), F32)
        dv_win[A_QB:win, :] = jnp.zeros((A_QB, HEAD_DIM), F32)

    return pl.pallas_call(
        body, name="mix_a_bwd", grid=(N_HEADS, nq + 1),
        in_specs=[q_spec] + kv_specs + kv_specs + [t_spec, row_spec(HEAD_DIM), row_spec(HEAD_DIM), row_spec(1)],
        out_specs=[row_spec(HEAD_DIM),
                   pl.BlockSpec((None, A_QB, HEAD_DIM), lambda h, i: (h, i, 0)),
                   pl.BlockSpec((None, A_QB, HEAD_DIM), lambda h, i: (h, i, 0)),
                   t_spec],
        out_shape=[jax.ShapeDtypeStruct((N_HEADS, s, HEAD_DIM), F32),
                   jax.ShapeDtypeStruct((N_HEADS, s + keep, HEAD_DIM), F32),
                   jax.ShapeDtypeStruct((N_HEADS, s + keep, HEAD_DIM), F32),
                   jax.ShapeDtypeStruct((N_HEADS, A_TQ, A_BAND), F32)],
        scratch_shapes=[pltpu.VMEM((win, HEAD_DIM), F32), pltpu.VMEM((win, HEAD_DIM), F32)],
        compiler_params=_params(("arbitrary", "arbitrary")),
    )(qkv, kpad, kpad, vpad, vpad, tbias, do, o, lse)


def relbias_tile(rel_bias, relmat):
    nrel = 2 * MAX_REL + 1

    def body(rb_ref, rel_ref, o_ref):
        rel = rel_ref[...]
        o_ref[...] = jnp.full(o_ref.shape, NEG, F32)

        def step(r, carry):
            hit = rel == r
            for h in range(N_HEADS):
                o_ref[h] = jnp.where(hit, rb_ref[h, r], o_ref[h])
            return carry

        lax.fori_loop(0, nrel, step, 0)

    return pl.pallas_call(
        body, name="relbias_tile",
        in_specs=[pl.BlockSpec(memory_space=pltpu.SMEM), pl.BlockSpec(memory_space=pltpu.VMEM)],
        out_specs=pl.BlockSpec(memory_space=pltpu.VMEM),
        out_shape=jax.ShapeDtypeStruct((N_HEADS, A_TQ, A_BAND), F32),
        compiler_params=_params(),
    )(rel_bias, relmat)


def relbias_grad(dt, relmat):
    nrel = 2 * MAX_REL + 1

    def body(dt_ref, rel_ref, o_ref):
        rel = rel_ref[...]
        lane = lax.broadcasted_iota(jnp.int32, (8, 384), 1)
        row = lax.broadcasted_iota(jnp.int32, (8, 384), 0)

        def step(r, acc):
            hit = rel == r
            for h in range(N_HEADS):
                val = jnp.sum(jnp.where(hit, dt_ref[h], 0.0))
                acc = jnp.where((lane == r) & (row == h), val, acc)
            return acc

        o_ref[...] = lax.fori_loop(0, nrel, step, jnp.zeros((8, 384), F32))

    return pl.pallas_call(
        body, name="relbias_grad",
        out_shape=jax.ShapeDtypeStruct((8, 384), F32),
        compiler_params=_params(),
    )(dt, relmat)


def _b_norm(v, gain):
    mu = jnp.mean(v, axis=-1, keepdims=True)
    xc = v - mu
    rstd = lax.rsqrt(jnp.mean(xc * xc, axis=-1, keepdims=True) + EPS)
    xhat = xc * rstd
    return xhat, rstd, xhat * gain


def _tril_mask():
    t = lax.broadcasted_iota(jnp.int32, (SG_CHUNK, SG_CHUNK), 0)
    u = lax.broadcasted_iota(jnp.int32, (SG_CHUNK, SG_CHUNK), 1)
    return u <= t


def mix_b_fwd(uv, gain, w_s, b_col):
    s = uv.shape[0]
    tm = min(ROW_T, s)

    def body(uv_ref, gain_ref, w_ref, b_ref, y_ref):
        tril = _tril_mask()
        ws = [jnp.where(tril, w_ref[g], 0.0).astype(BF16) for g in range(N_HEADS)]
        for c in range(tm // SG_CHUNK):
            rows = slice(c * SG_CHUNK, (c + 1) * SG_CHUNK)
            u = uv_ref[rows, 0:D_BRANCH]
            _, _, vn = _b_norm(uv_ref[rows, D_BRANCH:2 * D_BRANCH], gain_ref[...])
            vnb = vn.astype(BF16)
            outs = []
            for g in range(N_HEADS):
                cols = slice(g * HEAD_DIM, (g + 1) * HEAD_DIM)
                mixed = _dot(ws[g], vnb[:, cols]) + b_ref[g]
                outs.append(u[:, cols] * mixed)
            y_ref[rows, :] = jnp.concatenate(outs, axis=1)

    return pl.pallas_call(
        body, name="mix_b_fwd", grid=(s // tm,),
        in_specs=[pl.BlockSpec((tm, 2 * D_BRANCH), lambda i: (i, 0)),
                  pl.BlockSpec((1, D_BRANCH), lambda i: (0, 0)),
                  pl.BlockSpec((N_HEADS, SG_CHUNK, SG_CHUNK), lambda i: (0, 0, 0)),
                  pl.BlockSpec((N_HEADS, SG_CHUNK, 1), lambda i: (0, 0, 0))],
        out_specs=pl.BlockSpec((tm, D_BRANCH), lambda i: (i, 0)),
        out_shape=jax.ShapeDtypeStruct((s, D_BRANCH), F32),
        compiler_params=_params(("arbitrary",)),
    )(uv, gain, w_s, b_col)


def mix_b_bwd(uv, gain, w_s, b_col, dy):
    s = uv.shape[0]
    tm = min(ROW_T, s)

    def body(uv_ref, gain_ref, w_ref, b_ref, dy_ref, duv_ref, dw_ref, db_ref, dgain_ref):
        i = pl.program_id(0)

        @pl.when(i == 0)
        def _():
            dw_ref[...] = jnp.zeros_like(dw_ref)
            db_ref[...] = jnp.zeros_like(db_ref)
            dgain_ref[...] = jnp.zeros_like(dgain_ref)

        tril = _tril_mask()
        ws = [jnp.where(tril, w_ref[g], 0.0).astype(BF16) for g in range(N_HEADS)]
        gain_v = gain_ref[...]
        for c in range(tm // SG_CHUNK):
            rows = slice(c * SG_CHUNK, (c + 1) * SG_CHUNK)
            u = uv_ref[rows, 0:D_BRANCH]
            xhat, rstd, vn = _b_norm(uv_ref[rows, D_BRANCH:2 * D_BRANCH], gain_v)
            vnb = vn.astype(BF16)
            dyv = dy_ref[rows, :]
            dus, dvns = [], []
            for g in range(N_HEADS):
                cols = slice(g * HEAD_DIM, (g + 1) * HEAD_DIM)
                mixed = _dot(ws[g], vnb[:, cols]) + b_ref[g]
                dus.append(dyv[:, cols] * mixed)
                dmixed = dyv[:, cols] * u[:, cols]
                dmb = dmixed.astype(BF16)
                db_ref[g] += jnp.sum(dmixed, axis=-1, keepdims=True)
                dw_ref[g] += jnp.where(tril, _dot_nt(dmb, vnb[:, cols]), 0.0)
                dvns.append(_dot_tn(ws[g], dmb))
            dvn = jnp.concatenate(dvns, axis=1)
            dgain_ref[...] += jnp.sum(dvn * xhat, axis=0, keepdims=True)
            dxh = dvn * gain_v
            dv = rstd * (dxh - jnp.mean(dxh, axis=-1, keepdims=True)
                         - xhat * jnp.mean(dxh * xhat, axis=-1, keepdims=True))
            duv_ref[rows, :] = jnp.concatenate(dus + [dv], axis=1)

    return pl.pallas_call(
        body, name="mix_b_bwd", grid=(s // tm,),
        in_specs=[pl.BlockSpec((tm, 2 * D_BRANCH), lambda i: (i, 0)),
                  pl.BlockSpec((1, D_BRANCH), lambda i: (0, 0)),
                  pl.BlockSpec((N_HEADS, SG_CHUNK, SG_CHUNK), lambda i: (0, 0, 0)),
                  pl.BlockSpec((N_HEADS, SG_CHUNK, 1), lambda i: (0, 0, 0)),
                  pl.BlockSpec((tm, D_BRANCH), lambda i: (i, 0))],
        out_specs=[pl.BlockSpec((tm, 2 * D_BRANCH), lambda i: (i, 0)),
                   pl.BlockSpec((N_HEADS, SG_CHUNK, SG_CHUNK), lambda i: (0, 0, 0)),
                   pl.BlockSpec((N_HEADS, SG_CHUNK, 1), lambda i: (0, 0, 0)),
                   pl.BlockSpec((1, D_BRANCH), lambda i: (0, 0))],
        out_shape=[jax.ShapeDtypeStruct((s, 2 * D_BRANCH), F32),
                   jax.ShapeDtypeStruct((N_HEADS, SG_CHUNK, SG_CHUNK), F32),
                   jax.ShapeDtypeStruct((N_HEADS, SG_CHUNK, 1), F32),
                   jax.ShapeDtypeStruct((1, D_BRANCH), F32)],
        compiler_params=_params(("arbitrary",)),
    )(uv, gain, w_s, b_col, dy)


def _scan_mats(nrow):
    a = lax.broadcasted_iota(jnp.int32, (128, 128), 0)
    b = lax.broadcasted_iota(jnp.int32, (128, 128), 1)
    r = lax.broadcasted_iota(jnp.int32, (nrow, nrow), 0)
    c = lax.broadcasted_iota(jnp.int32, (nrow, nrow), 1)
    nb = nrow // N_HEADS
    same = (r // nb) == (c // nb)
    return a, b, r, c, same


def _exact_dot(x, m):
    hi, mid, lo = _split3(x)
    return _dot(hi, m) + _dot(mid, m) + _dot(lo, m)


def _exact_dot_left(m, x):
    hi, mid, lo = _split3(x)
    return _dot(m, hi) + _dot(m, mid) + _dot(m, lo)


def fox_gate_fwd(ft, bcol):
    nrow = ft.shape[0]

    def body(f_ref, b_ref, c_ref):
        z = f_ref[...] + b_ref[...]
        ls = jnp.minimum(z, 0.0) - jnp.log(1.0 + jnp.exp(-jnp.abs(z)))
        a, b, r, c, same = _scan_mats(nrow)
        within = _exact_dot(ls, (a <= b).astype(BF16))
        tot = jnp.broadcast_to(within[:, 127:128], within.shape)
        before = _exact_dot_left((same & (c < r)).astype(BF16), tot)
        c_ref[...] = within + before

    return pl.pallas_call(
        body, name="fox_gate_fwd",
        out_shape=jax.ShapeDtypeStruct((nrow, 128), F32),
        compiler_params=_params(),
    )(ft, bcol)


def fox_gate_bwd(ft, bcol, dc):
    nrow = ft.shape[0]

    def body(f_ref, b_ref, dc_ref, df_ref, db_ref):
        a, b, r, c, same = _scan_mats(nrow)
        dcv = dc_ref[...]
        within = _exact_dot(dcv, (a >= b).astype(BF16))
        tot = jnp.broadcast_to(within[:, 0:1], within.shape)
        after = _exact_dot_left((same & (c > r)).astype(BF16), tot)
        dls = within + after
        z = f_ref[...] + b_ref[...]
        dz = dls * _sigmoid(-z)
        df_ref[...] = dz
        rs = jnp.broadcast_to(jnp.sum(dz, axis=-1, keepdims=True), dz.shape)
        hr = lax.broadcasted_iota(jnp.int32, (8, nrow), 0)
        hc = lax.broadcasted_iota(jnp.int32, (8, nrow), 1)
        db_ref[...] = _exact_dot_left((hr == hc // (nrow // N_HEADS)).astype(BF16), rs)

    return pl.pallas_call(
        body, name="fox_gate_bwd",
        out_shape=[jax.ShapeDtypeStruct((nrow, 128), F32), jax.ShapeDtypeStruct((8, 128), F32)],
        compiler_params=_params(),
    )(ft, bcol, dc)


def _att_specs(s, qi, ki, vi):
    t = ATT_T
    q_spec = pl.BlockSpec((None, None, t, HEAD_DIM), lambda h, i: (qi, h, i, 0))
    k_spec = pl.BlockSpec((None, None, s, HEAD_DIM), lambda h, i: (ki, h, 0, 0))
    v_spec = pl.BlockSpec((None, None, s, HEAD_DIM), lambda h, i: (vi, h, 0, 0))
    row_spec = lambda w: pl.BlockSpec((None, t, w), lambda h, i: (h, i, 0))
    return q_spec, k_spec, v_spec, row_spec


def _causal(strict):
    row = lax.broadcasted_iota(jnp.int32, (ATT_T, ATT_T), 0)
    col = lax.broadcasted_iota(jnp.int32, (ATT_T, ATT_T), 1)
    return (col < row) if strict else (col <= row)


def _gate_row(cr_ref, kb, g):
    if g == 1:
        return cr_ref[kb]
    return jnp.concatenate([cr_ref[kb + n] for n in range(g)], axis=1)


def _fox_walk(i, carry, tile, alive):
    g = FOX_WIDE
    nmid = i // FOX_MID
    nwide = i // g
    carry = tile(i, 1, carry, True)
    carry = lax.fori_loop(0, i - nmid * FOX_MID, lambda n, c: tile(i - 1 - n, 1, c, False), carry)
    carry = lax.fori_loop(0, nmid - nwide * (g // FOX_MID), lambda n, c: tile(nwide * g, FOX_MID, c, False), carry)

    def cond(state):
        return jnp.logical_and(state[0] >= 0, state[1] > 0)

    def step(state):
        n = state[0]
        c = tile(n * g, g, state[2:], False)
        return (n - 1, alive(n * g, c)) + tuple(c)

    out = lax.while_loop(cond, step, (nwide - 1, alive(nwide * g, carry)) + tuple(carry))
    return out[2:]


def _fox_reach(qs, k_ref, kmax_ref, cc, i):
    s = k_ref.shape[0]
    rows = 4 * ATT_T

    @pl.when(i == 0)
    def _():
        def chunk(n, mx):
            kc = k_ref[pl.ds(pl.multiple_of(n * rows, rows), rows), :].astype(F32)
            return jnp.maximum(mx, jnp.max(jnp.sum(kc * kc, axis=-1, keepdims=True)))

        kmax_ref[0] = jnp.sqrt(lax.fori_loop(0, s // rows, chunk, jnp.float32(0.0)))

    qf = qs.astype(F32)
    return jnp.sqrt(jnp.sum(qf * qf, axis=-1, keepdims=True)) * kmax_ref[0] + cc


def fox_fwd(qkv, c_col, c_row):
    s = qkv.shape[2]
    t = ATT_T
    nq = s // t
    q_spec, k_spec, v_spec, row_spec = _att_specs(s, 3, 4, 5)

    def body(q_ref, k_ref, v_ref, cc_ref, cr_ref, o_ref, lse_ref, kmax_ref):
        i = pl.program_id(1)
        qs = q_ref[...] * 0.125
        cc = cc_ref[...]
        reach = _fox_reach(qs, k_ref, kmax_ref, cc, i)

        def alive(kb, carry):
            return (jnp.max(reach - cr_ref[kb][:, 0:1] - carry[0]) > FOX_DEAD).astype(jnp.int32)

        def tile(kb, g, carry, masked):
            m, l, acc = carry
            k0 = pl.multiple_of(kb * t, t)
            sc = _dot_nt(qs, k_ref[pl.ds(k0, g * t), :]) + (cc - _gate_row(cr_ref, kb, g))
            if masked:
                sc = jnp.where(_causal(False), sc, NEG)
            m_new = jnp.maximum(m, jnp.max(sc, axis=-1, keepdims=True))
            alpha = jnp.exp(m - m_new)
            p = jnp.exp(sc - m_new)
            l = alpha * l + jnp.sum(p, axis=-1, keepdims=True)
            p_hi, p_lo = _split2(p)
            v = v_ref[pl.ds(k0, g * t), :]
            acc = alpha * acc + (_dot(p_hi, v) + _dot(p_lo, v))
            return m_new, l, acc

        init = (jnp.full((t, 1), NEG, F32), jnp.zeros((t, 1), F32), jnp.zeros((t, HEAD_DIM), F32))
        m, l, acc = _fox_walk(i, init, tile, alive)
        o_ref[...] = acc / l
        lse_ref[...] = m + jnp.log(l)

    return pl.pallas_call(
        body, name="fox_fwd", grid=(N_HEADS, nq),
        in_specs=[q_spec, k_spec, v_spec, row_spec(1),
                  pl.BlockSpec((None, nq, 1, t), lambda h, i: (h, 0, 0, 0))],
        out_specs=[row_spec(HEAD_DIM), row_spec(1)],
        out_shape=[jax.ShapeDtypeStruct((N_HEADS, s, HEAD_DIM), F32),
                   jax.ShapeDtypeStruct((N_HEADS, s, 1), F32)],
        scratch_shapes=[pltpu.SMEM((1,), F32)],
        compiler_params=_params(("arbitrary", "arbitrary")),
    )(qkv, qkv, qkv, c_col, c_row)


def fox_bwd(qkv, c_col, c_row, do, o, lse):
    s = qkv.shape[2]
    t = ATT_T
    nq = s // t
    q_spec, k_spec, v_spec, row_spec = _att_specs(s, 3, 4, 5)
    any_spec = pl.BlockSpec(memory_space=pl.ANY)

    def body(q_ref, k_ref, v_ref, cc_ref, cr_ref, do_ref, o_ref, lse_ref,
             dq_ref, dk_hbm, dv_hbm, dc_ref, dk_acc, dv_acc, kmax_ref):
        h = pl.program_id(0)
        i = pl.program_id(1)

        @pl.when(i == 0)
        def _():
            dk_acc[...] = jnp.zeros_like(dk_acc)
            dv_acc[...] = jnp.zeros_like(dv_acc)
            dc_ref[...] = jnp.zeros_like(dc_ref)

        qs = q_ref[...] * 0.125
        dob = do_ref[...]
        delta = jnp.sum(o_ref[...] * dob.astype(F32), axis=-1, keepdims=True)
        lse = lse_ref[...]
        cc = cc_ref[...]
        margin = _fox_reach(qs, k_ref, kmax_ref, cc, i) - lse

        def alive(kb, carry):
            return (jnp.max(margin - cr_ref[kb][:, 0:1]) > FOX_DEAD).astype(jnp.int32)

        def tile(kb, g, carry, masked):
            dq, = carry
            k0 = pl.multiple_of(kb * t, t)
            k = k_ref[pl.ds(k0, g * t), :]
            sc = _dot_nt(qs, k) + (cc - _gate_row(cr_ref, kb, g))
            if masked:
                sc = jnp.where(_causal(False), sc, NEG)
            p = jnp.exp(sc - lse)
            ds = p * (_dot_nt(dob, v_ref[pl.ds(k0, g * t), :]) - delta)
            dsb = ds.astype(BF16)
            dk_acc[pl.ds(k0, g * t), :] += _dot_tn(dsb, qs)
            dv_acc[pl.ds(k0, g * t), :] += _dot_tn(p.astype(BF16), dob)
            dcs = -jnp.sum(ds, axis=0, keepdims=True)
            for n in range(g):
                dc_ref[kb + n] += dcs[:, n * t:(n + 1) * t]
            return (dq + _dot(dsb, k),)

        dq, = _fox_walk(i, (jnp.zeros((t, HEAD_DIM), F32),), tile, alive)
        dq_ref[...] = dq * 0.125

        @pl.when(i == nq - 1)
        def _():
            pltpu.sync_copy(dk_acc, dk_hbm.at[h])
            pltpu.sync_copy(dv_acc, dv_hbm.at[h])

    return pl.pallas_call(
        body, name="fox_bwd", grid=(N_HEADS, nq),
        in_specs=[q_spec, k_spec, v_spec, row_spec(1),
                  pl.BlockSpec((None, nq, 1, t), lambda h, i: (h, 0, 0, 0)),
                  row_spec(HEAD_DIM), row_spec(HEAD_DIM), row_spec(1)],
        out_specs=[row_spec(HEAD_DIM), any_spec, any_spec,
                   pl.BlockSpec((None, nq, 1, t), lambda h, i: (h, 0, 0, 0))],
        out_shape=[jax.ShapeDtypeStruct((N_HEADS, s, HEAD_DIM), F32),
                   jax.ShapeDtypeStruct((N_HEADS, s, HEAD_DIM), F32),
                   jax.ShapeDtypeStruct((N_HEADS, s, HEAD_DIM), F32),
                   jax.ShapeDtypeStruct((N_HEADS, nq, 1, t), F32)],
        scratch_shapes=[pltpu.VMEM((s, HEAD_DIM), F32), pltpu.VMEM((s, HEAD_DIM), F32), pltpu.SMEM((1,), F32)],
        compiler_params=_params(("arbitrary", "arbitrary")),
    )(qkv, qkv, qkv, c_col, c_row, do, o, lse)


def _sb_tile(qs, k, run, masked):
    z = _dot_nt(qs, k)
    sp = jnp.log(1.0 + jnp.exp(-jnp.abs(z)))
    ls = jnp.minimum(z, 0.0) - sp
    lm = -jnp.maximum(z, 0.0) - sp
    if masked:
        valid = _causal(True)
        lm = jnp.where(valid, lm, 0.0)
    row = lax.broadcasted_iota(jnp.int32, (ATT_T, ATT_T), 0)
    col = lax.broadcasted_iota(jnp.int32, (ATT_T, ATT_T), 1)
    later = (row > col).astype(BF16)
    hi, lo = _split2(lm)
    between = run + _dot(hi, later) + _dot(lo, later)
    a = jnp.exp(ls + between)
    if masked:
        a = jnp.where(valid, a, 0.0)
    return ls, lm, a


def _sb_walk(i, carry, tile):
    def alive_of(c):
        return (jnp.max(c[0]) > SB_DEAD).astype(jnp.int32)

    def cond(state):
        n, alive = state[0], state[1]
        return jnp.logical_and(n < i, alive > 0)

    def step(state):
        n = state[0]
        c = tile(i - 1 - n, state[2:], False)
        return (n + 1, alive_of(c)) + tuple(c)

    out = lax.while_loop(cond, step, (jnp.int32(0), alive_of(carry)) + tuple(carry))
    return out[2:]


def sb_fwd(qkv):
    s = qkv.shape[2]
    t = ATT_T
    nq = s // t
    q_spec, k_spec, v_spec, row_spec = _att_specs(s, 6, 7, 8)

    def body(q_ref, k_ref, v_ref, o_ref):
        i = pl.program_id(1)
        qs = q_ref[...] * 0.125

        def tile(kb, carry, masked):
            run, acc = carry
            k0 = pl.multiple_of(kb * t, t)
            _, lm, a = _sb_tile(qs, k_ref[pl.ds(k0, t), :], run, masked)
            acc = acc + _dot(a.astype(BF16), v_ref[pl.ds(k0, t), :])
            return run + jnp.sum(lm, axis=-1, keepdims=True), acc

        carry = tile(i, (jnp.zeros((t, 1), F32), jnp.zeros((t, HEAD_DIM), F32)), True)
        _, acc = _sb_walk(i, carry, tile)
        o_ref[...] = acc

    return pl.pallas_call(
        body, name="sb_fwd", grid=(N_HEADS, nq),
        in_specs=[q_spec, k_spec, v_spec],
        out_specs=row_spec(HEAD_DIM),
        out_shape=jax.ShapeDtypeStruct((N_HEADS, s, HEAD_DIM), F32),
        compiler_params=_params(("arbitrary", "arbitrary")),
    )(qkv, qkv, qkv)


def sb_bwd(qkv, do, o):
    s = qkv.shape[2]
    t = ATT_T
    nq = s // t
    q_spec, k_spec, v_spec, row_spec = _att_specs(s, 6, 7, 8)
    any_spec = pl.BlockSpec(memory_space=pl.ANY)

    def body(q_ref, k_ref, v_ref, do_ref, o_ref, dq_ref, dk_hbm, dv_hbm, dk_acc, dv_acc):
        h = pl.program_id(0)
        i = pl.program_id(1)

        @pl.when(i == 0)
        def _():
            dk_acc[...] = jnp.zeros_like(dk_acc)
            dv_acc[...] = jnp.zeros_like(dv_acc)

        qs = q_ref[...] * 0.125
        dob = do_ref[...]
        tot = jnp.sum(o_ref[...] * dob.astype(F32), axis=-1, keepdims=True)

        def tile(kb, carry, masked):
            run, run_g, dq = carry
            k0 = pl.multiple_of(kb * t, t)
            k = k_ref[pl.ds(k0, t), :]
            ls, lm, a = _sb_tile(qs, k, run, masked)
            ab = a.astype(BF16)
            g = ab.astype(F32) * _dot_nt(dob, v_ref[pl.ds(k0, t), :])
            row = lax.broadcasted_iota(jnp.int32, (t, t), 0)
            col = lax.broadcasted_iota(jnp.int32, (t, t), 1)
            from_here = (row >= col).astype(BF16)
            hi, lo = _split2(g)
            g_right = run_g + _dot(hi, from_here) + _dot(lo, from_here)
            g_left = tot - g_right
            dz = g - jnp.exp(ls) * (g + g_left)
            if masked:
                dz = jnp.where(_causal(True), dz, 0.0)
            dzb = dz.astype(BF16)
            dk_acc[pl.ds(k0, t), :] += _dot_tn(dzb, qs)
            dv_acc[pl.ds(k0, t), :] += _dot_tn(ab, dob)
            return (run + jnp.sum(lm, axis=-1, keepdims=True),
                    run_g + jnp.sum(g, axis=-1, keepdims=True),
                    dq + _dot(dzb, k))

        zero = jnp.zeros((t, 1), F32)
        carry = tile(i, (zero, zero, jnp.zeros((t, HEAD_DIM), F32)), True)
        _, _, dq = _sb_walk(i, carry, tile)
        dq_ref[...] = dq * 0.125

        @pl.when(i == nq - 1)
        def _():
            pltpu.sync_copy(dk_acc, dk_hbm.at[h])
            pltpu.sync_copy(dv_acc, dv_hbm.at[h])

    return pl.pallas_call(
        body, name="sb_bwd", grid=(N_HEADS, nq),
        in_specs=[q_spec, k_spec, v_spec, row_spec(HEAD_DIM), row_spec(HEAD_DIM)],
        out_specs=[row_spec(HEAD_DIM), any_spec, any_spec],
        out_shape=[jax.ShapeDtypeStruct((N_HEADS, s, HEAD_DIM), F32)] * 3,
        scratch_shapes=[pltpu.VMEM((s, HEAD_DIM), F32), pltpu.VMEM((s, HEAD_DIM), F32)],
        compiler_params=_params(("arbitrary", "arbitrary")),
    )(qkv, qkv, qkv, do, o)


def _branch_inputs(refs, br):
    ya_ref, yb_ref, yc_ref, yd_ref = refs
    if br == 1:
        return yb_ref[...]
    return _heads_to_lanes((ya_ref, None, yc_ref, yd_ref)[br])


def outproj_fwd(x, ya, yb, yc, yd, gates, bg, wout):
    s = x.shape[0]
    tm = min(ROW_T, s)

    def body(x_ref, ya_ref, yb_ref, yc_ref, yd_ref, gates_ref, bg_ref, w_ref, out_ref):
        pieces = []
        for br in range(4):
            cols = slice(br * D_BRANCH, (br + 1) * D_BRANCH)
            y = _branch_inputs((ya_ref, yb_ref, yc_ref, yd_ref), br)
            r = lax.rsqrt(jnp.mean(y * y, axis=-1, keepdims=True) + EPS)
            gt = gates_ref[:, cols]
            pieces.append((y * r * bg_ref[:, cols]) * (gt * _sigmoid(gt)))
        merged = jnp.concatenate(pieces, axis=1).astype(BF16)
        out_ref[...] = x_ref[...] + _dot(merged, w_ref[...])

    head_spec = pl.BlockSpec((N_HEADS, tm, HEAD_DIM), lambda i: (0, i, 0))
    return pl.pallas_call(
        body, name="outproj_fwd", grid=(s // tm,),
        in_specs=[pl.BlockSpec((tm, D_MODEL), lambda i: (i, 0)),
                  head_spec, pl.BlockSpec((tm, D_BRANCH), lambda i: (i, 0)), head_spec, head_spec,
                  pl.BlockSpec((tm, D_MODEL), lambda i: (i, 0)),
                  pl.BlockSpec((1, D_MODEL), lambda i: (0, 0)),
                  pl.BlockSpec((D_MODEL, D_MODEL), lambda i: (0, 0))],
        out_specs=pl.BlockSpec((tm, D_MODEL), lambda i: (i, 0)),
        out_shape=jax.ShapeDtypeStruct((s, D_MODEL), F32),
        compiler_params=_params(("arbitrary",)),
    )(x, ya, yb, yc, yd, gates, bg, wout)


def outproj_bwd(dout, ya, yb, yc, yd, gates, bg, wout):
    s = dout.shape[0]
    tm = min(ROW_T, s)

    def body(dout_ref, ya_ref, yb_ref, yc_ref, yd_ref, gates_ref, bg_ref, w_ref,
             dya_ref, dyb_ref, dyc_ref, dyd_ref, dgates_ref, dbg_ref, dw_ref):
        i = pl.program_id(0)

        @pl.when(i == 0)
        def _():
            dbg_ref[...] = jnp.zeros_like(dbg_ref)
            dw_ref[...] = jnp.zeros_like(dw_ref)

        doutb = dout_ref[...].astype(BF16)
        dmerged = _dot_nt(doutb, w_ref[...])
        pieces = []
        for br in range(4):
            cols = slice(br * D_BRANCH, (br + 1) * D_BRANCH)
            y = _branch_inputs((ya_ref, yb_ref, yc_ref, yd_ref), br)
            r = lax.rsqrt(jnp.mean(y * y, axis=-1, keepdims=True) + EPS)
            yn = y * r
            bgv = bg_ref[:, cols]
            gt = gates_ref[:, cols]
            sig = _sigmoid(gt)
            act = gt * sig
            n = yn * bgv
            pieces.append(n * act)
            dm = dmerged[:, cols]
            dn = dm * act
            dgates_ref[:, cols] = (dm * n * (sig * (1.0 + gt * (1.0 - sig)))).astype(BF16)
            dbg_ref[:, cols] += jnp.sum(dn * yn, axis=0, keepdims=True)
            u = dn * bgv
            dy = r * (u - yn * jnp.mean(yn * u, axis=-1, keepdims=True))
            if br == 1:
                dyb_ref[...] = dy
            else:
                dref = (dya_ref, None, dyc_ref, dyd_ref)[br]
                for hh in range(N_HEADS):
                    dref[hh] = dy[:, hh * HEAD_DIM:(hh + 1) * HEAD_DIM].astype(BF16)
        merged = jnp.concatenate(pieces, axis=1).astype(BF16)
        dw_ref[...] += _dot_tn(merged, doutb)

    head_spec = pl.BlockSpec((N_HEADS, tm, HEAD_DIM), lambda i: (0, i, 0))
    head_shape = jax.ShapeDtypeStruct((N_HEADS, s, HEAD_DIM), BF16)
    return pl.pallas_call(
        body, name="outproj_bwd", grid=(s // tm,),
        in_specs=[pl.BlockSpec((tm, D_MODEL), lambda i: (i, 0)),
                  head_spec, pl.BlockSpec((tm, D_BRANCH), lambda i: (i, 0)), head_spec, head_spec,
                  pl.BlockSpec((tm, D_MODEL), lambda i: (i, 0)),
                  pl.BlockSpec((1, D_MODEL), lambda i: (0, 0)),
                  pl.BlockSpec((D_MODEL, D_MODEL), lambda i: (0, 0))],
        out_specs=[head_spec, pl.BlockSpec((tm, D_BRANCH), lambda i: (i, 0)), head_spec, head_spec,
                   pl.BlockSpec((tm, D_MODEL), lambda i: (i, 0)),
                   pl.BlockSpec((1, D_MODEL), lambda i: (0, 0)),
                   pl.BlockSpec((D_MODEL, D_MODEL), lambda i: (0, 0))],
        out_shape=[head_shape, jax.ShapeDtypeStruct((s, D_BRANCH), F32), head_shape, head_shape,
                   jax.ShapeDtypeStruct((s, D_MODEL), BF16),
                   jax.ShapeDtypeStruct((1, D_MODEL), F32),
                   jax.ShapeDtypeStruct((D_MODEL, D_MODEL), F32)],
        compiler_params=_params(("arbitrary",)),
    )(dout, ya, yb, yc, yd, gates, bg, wout)


def final_loss(x, tgt, g):
    s = x.shape[0]
    tm = min(ROW_T, s)

    def body(x_ref, t_ref, g_ref, loss_ref, dx_ref, dg_ref):
        i = pl.program_id(0)

        @pl.when(i == 0)
        def _():
            loss_ref[...] = jnp.zeros_like(loss_ref)
            dg_ref[...] = jnp.zeros_like(dg_ref)

        xv = x_ref[...]
        gv = g_ref[...]
        r = lax.rsqrt(jnp.mean(xv * xv, axis=-1, keepdims=True) + EPS)
        xn = xv * r
        err = xn * gv - t_ref[...]
        loss_ref[...] += jnp.sum(err * err) * (0.5 / D_MODEL)
        dy = err * (1.0 / D_MODEL)
        u = dy * gv
        dx_ref[...] = r * (u - xn * jnp.mean(xn * u, axis=-1, keepdims=True))
        dg_ref[...] += jnp.sum(dy * xn, axis=0, keepdims=True)

    return pl.pallas_call(
        body, name="final_loss", grid=(s // tm,),
        in_specs=[pl.BlockSpec((tm, D_MODEL), lambda i: (i, 0)),
                  pl.BlockSpec((tm, D_MODEL), lambda i: (i, 0)),
                  pl.BlockSpec((1, D_MODEL), lambda i: (0, 0))],
        out_specs=[pl.BlockSpec((1, 128), lambda i: (0, 0)),
                   pl.BlockSpec((tm, D_MODEL), lambda i: (i, 0)),
                   pl.BlockSpec((1, D_MODEL), lambda i: (0, 0))],
        out_shape=[jax.ShapeDtypeStruct((1, 128), F32),
                   jax.ShapeDtypeStruct((s, D_MODEL), F32),
                   jax.ShapeDtypeStruct((1, D_MODEL), F32)],
        compiler_params=_params(("arbitrary",)),
    )(x, tgt, g)


def _rel_index():
    i = np.arange(A_TQ)[:, None]
    j = np.arange(A_BAND)[None, :]
    rel = np.clip(i - j + (A_BAND - A_TQ), -MAX_REL, MAX_REL) + MAX_REL
    dchunk = i // CHUNK + LOOKBACK - j // CHUNK
    valid = (dchunk >= 0) & (dchunk <= LOOKBACK)
    return jnp.asarray(np.where(valid, rel, -1).astype(np.int32))


def _layer_consts(p):
    tbias = relbias_tile(p["rel_bias"], _rel_index())
    return dict(
        norm_g=p["norm_g"].reshape(1, D_MODEL),
        v_gain=p["v_gain"].reshape(1, D_BRANCH),
        b_col=p["b_s"].reshape(N_HEADS, SG_CHUNK, 1),
        bg=p["branch_gain"].reshape(1, D_MODEL),
        tbias=tbias,
    )


def _gate_layout(fp, b_f, s):
    nb = s // 128
    ft = fp[:, :N_HEADS].T.reshape(N_HEADS * nb, 128)
    bcol = jnp.repeat(b_f, nb).reshape(N_HEADS * nb, 1)
    return ft, bcol


def layer_fwd(x, p):
    s = x.shape[0]
    c = _layer_consts(p)
    h, qkv, gates, uv, fp = inproj_fwd(x, c["norm_g"], p["wp"])
    keep = A_BAND - A_TQ
    kpad = jnp.pad(qkv[1], ((0, 0), (keep, 0), (0, 0)))
    vpad = jnp.pad(qkv[2], ((0, 0), (keep, 0), (0, 0)))
    ya, lse_a = mix_a_fwd(qkv, kpad, vpad, c["tbias"])
    yb = mix_b_fwd(uv, c["v_gain"], p["w_s"], c["b_col"])
    ft, bcol = _gate_layout(fp, p["b_f"], s)
    cum = fox_gate_fwd(ft, bcol).reshape(N_HEADS, s)
    c_col = cum.reshape(N_HEADS, s, 1)
    c_row = cum.reshape(N_HEADS, s // ATT_T, 1, ATT_T)
    yc, lse_c = fox_fwd(qkv, c_col, c_row)
    yd = sb_fwd(qkv)
    out = outproj_fwd(x, ya, yb, yc, yd, gates, c["bg"], p["wout"])
    saved = dict(consts=c, x=x, h=h, qkv=qkv, gates=gates, uv=uv, kpad=kpad, vpad=vpad, ft=ft, bcol=bcol,
                 c_col=c_col, c_row=c_row, ya=ya, lse_a=lse_a, yb=yb, yc=yc, lse_c=lse_c, yd=yd)
    return out, saved


def layer_bwd(dout, p, sv):
    s = dout.shape[0]
    c = sv["consts"]
    dya, dyb, dyc, dyd, dgates, dbg, dwout = outproj_bwd(
        dout, sv["ya"], sv["yb"], sv["yc"], sv["yd"], sv["gates"], c["bg"], p["wout"])
    keep = A_BAND - A_TQ
    dqa, dkpad, dvpad, dt = mix_a_bwd(sv["qkv"], sv["kpad"], sv["vpad"], c["tbias"], dya, sv["ya"], sv["lse_a"])
    dka, dva = dkpad[:, keep:], dvpad[:, keep:]
    drel = relbias_grad(dt, _rel_index())[:N_HEADS, :2 * MAX_REL + 1]
    duv, dws, dbs, dvgain = mix_b_bwd(sv["uv"], c["v_gain"], p["w_s"], c["b_col"], dyb)
    dqc, dkc, dvc, dc = fox_bwd(sv["qkv"], sv["c_col"], sv["c_row"], dyc, sv["yc"], sv["lse_c"])
    dft, dbf = fox_gate_bwd(sv["ft"], sv["bcol"], dc.reshape(N_HEADS * (s // 128), 128))
    dfp = jnp.pad(dft.reshape(N_HEADS, s).T, ((0, 0), (0, 128 - N_HEADS)))
    dqd, dkd, dvd = sb_bwd(sv["qkv"], dyd, sv["yd"])
    dp, dx, dnorm = inproj_bwd((dqa, dka, dva, dqc, dkc, dvc, dqd, dkd, dvd), dgates, duv, dfp,
                               p["wp"], sv["x"], c["norm_g"], dout)
    dwp = weight_grad(sv["h"], dp, "inproj_wgrad")
    grads = dict(norm_g=dnorm.reshape(D_MODEL), wp=dwp, b_f=dbf[:N_HEADS, 0], rel_bias=drel,
                 w_s=dws, b_s=dbs.reshape(N_HEADS, SG_CHUNK), v_gain=dvgain.reshape(D_BRANCH),
                 branch_gain=dbg.reshape(4, D_BRANCH), wout=dwout)
    return dx, grads


def local_step(x, tgt, layers, final_g):
    saved = []
    cur = x
    for p in layers:
        cur, sv = layer_fwd(cur, p)
        saved.append(sv)
    loss, dcur, dfinal = final_loss(cur, tgt, final_g.reshape(1, D_MODEL))
    grads = [None] * len(layers)
    for l in reversed(range(len(layers))):
        dcur, grads[l] = layer_bwd(dcur, layers[l], saved[l])
    return loss[0, 0], dcur, grads, dfinal.reshape(D_MODEL)


def gather_weights(wb, wf):
    def body(wb_ref, wf_ref, ob_ref, of_ref, send_sems, recv_sems, loc_sems):
        x, y, c = lax.axis_index("x"), lax.axis_index("y"), lax.axis_index("c")
        me = 2 * x + y
        chips = [(1 - x, y), (x, 1 - y), (1 - x, 1 - y)]
        pairs = [(wb_ref, ob_ref), (wf_ref, of_ref)]
        local = [pltpu.make_async_copy(src, dst.at[me], loc_sems.at[n]) for n, (src, dst) in enumerate(pairs)]
        for cp in local:
            cp.start()

        def copy(j, n, slot):
            src, dst = pairs[n]
            return pltpu.make_async_remote_copy(
                src_ref=src, dst_ref=dst.at[slot], send_sem=send_sems.at[2 * j + n], recv_sem=recv_sems.at[2 * j + n],
                device_id=(chips[j][0], chips[j][1], c), device_id_type=MESH)

        sends = [copy(j, n, me) for j in range(3) for n in range(2)]
        for cp in sends:
            cp.start()
        for j in range(3):
            for n in range(2):
                copy(j, n, 2 * chips[j][0] + chips[j][1]).wait_recv()
        for cp in sends:
            cp.wait_send()
        for cp in local:
            cp.wait()

    any_spec = pl.BlockSpec(memory_space=pl.ANY)
    return pl.pallas_call(
        body, name="gather_weights",
        in_specs=[any_spec, any_spec], out_specs=[any_spec, any_spec],
        out_shape=[jax.ShapeDtypeStruct((4,) + wb.shape, wb.dtype), jax.ShapeDtypeStruct((4,) + wf.shape, wf.dtype)],
        scratch_shapes=[pltpu.SemaphoreType.DMA((6,)), pltpu.SemaphoreType.DMA((6,)), pltpu.SemaphoreType.DMA((2,))],
    )(wb, wf)


def pair_swap(send):
    half = send.shape[1] // 2

    def body(s_ref, o_ref, send_sem, recv_sem, loc_sem):
        x, y, c = lax.axis_index("x"), lax.axis_index("y"), lax.axis_index("c")
        local = pltpu.make_async_copy(s_ref.at[:, pl.ds(c * half, half), :], o_ref.at[0], loc_sem)
        local.start()
        swap = pltpu.make_async_remote_copy(
            src_ref=s_ref.at[:, pl.ds((1 - c) * half, half), :], dst_ref=o_ref.at[1], send_sem=send_sem,
            recv_sem=recv_sem, device_id=(x, y, 1 - c), device_id_type=MESH)
        swap.start()
        swap.wait()
        local.wait()

    any_spec = pl.BlockSpec(memory_space=pl.ANY)
    return pl.pallas_call(
        body, name="pair_swap",
        in_specs=[any_spec], out_specs=any_spec, out_shape=jax.ShapeDtypeStruct((2, 4, half, 128), send.dtype),
        scratch_shapes=[pltpu.SemaphoreType.DMA, pltpu.SemaphoreType.DMA, pltpu.SemaphoreType.DMA],
    )(send)


def chip_exchange(pair):
    def body(p_ref, r_ref, send_sems, recv_sems, loc_sem):
        x, y, c = lax.axis_index("x"), lax.axis_index("y"), lax.axis_index("c")
        me = 2 * x + y
        chips = [(1 - x, y), (x, 1 - y), (1 - x, 1 - y)]
        local = pltpu.make_async_copy(p_ref.at[me], r_ref.at[me], loc_sem)
        local.start()

        def copy(j, slab, slot):
            return pltpu.make_async_remote_copy(
                src_ref=p_ref.at[slab], dst_ref=r_ref.at[slot], send_sem=send_sems.at[j], recv_sem=recv_sems.at[j],
                device_id=(chips[j][0], chips[j][1], c), device_id_type=MESH)

        sends = [copy(j, 2 * px + py, me) for j, (px, py) in enumerate(chips)]
        for cp in sends:
            cp.start()
        for j, (px, py) in enumerate(chips):
            copy(j, me, 2 * px + py).wait_recv()
        for cp in sends:
            cp.wait_send()
        local.wait()

    any_spec = pl.BlockSpec(memory_space=pl.ANY)
    return pl.pallas_call(
        body, name="chip_exchange",
        in_specs=[any_spec], out_specs=any_spec, out_shape=jax.ShapeDtypeStruct(pair.shape, pair.dtype),
        scratch_shapes=[pltpu.SemaphoreType.DMA((3,)), pltpu.SemaphoreType.DMA((3,)), pltpu.SemaphoreType.DMA],
    )(pair)


def half_swap(mine):
    def body(m_ref, f_ref, send_sem, recv_sem, loc_sem):
        x, y, c = lax.axis_index("x"), lax.axis_index("y"), lax.axis_index("c")
        local = pltpu.make_async_copy(m_ref, f_ref.at[c], loc_sem)
        local.start()
        pltpu.make_async_remote_copy(src_ref=m_ref, dst_ref=f_ref.at[c], send_sem=send_sem, recv_sem=recv_sem,
                                     device_id=(x, y, 1 - c), device_id_type=MESH).start()
        wait = pltpu.make_async_remote_copy(src_ref=m_ref, dst_ref=f_ref.at[1 - c], send_sem=send_sem,
                                            recv_sem=recv_sem, device_id=(x, y, 1 - c), device_id_type=MESH)
        wait.wait_send()
        wait.wait_recv()
        local.wait()

    any_spec = pl.BlockSpec(memory_space=pl.ANY)
    return pl.pallas_call(
        body, name="half_swap",
        in_specs=[any_spec], out_specs=any_spec, out_shape=jax.ShapeDtypeStruct((2,) + mine.shape, mine.dtype),
        scratch_shapes=[pltpu.SemaphoreType.DMA, pltpu.SemaphoreType.DMA, pltpu.SemaphoreType.DMA],
    )(mine)


def sum_slabs(stacked, name):
    n, rows, _ = stacked.shape
    tr = 256

    def body(*refs):
        acc = refs[0][...]
        for r in refs[1:-1]:
            acc = acc + r[...]
        refs[-1][...] = acc

    return pl.pallas_call(
        body, name=name, grid=(rows // tr,),
        in_specs=[pl.BlockSpec((None, tr, 128), lambda i, k=k: (k, i, 0)) for k in range(n)],
        out_specs=pl.BlockSpec((tr, 128), lambda i: (i, 0)), out_shape=jax.ShapeDtypeStruct((rows, 128), F32),
        compiler_params=_params(("arbitrary",)),
    )(*([stacked] * n))


def reduce_grads(send):
    half = send.shape[1] // 2
    both = pair_swap(send)
    pair = sum_slabs(both.reshape(2, 4 * half, 128), "pair_sum").reshape(4, half, 128)
    red = sum_slabs(chip_exchange(pair), "chip_sum")
    return half_swap(red).reshape(2 * half, 128)


def adamw_update(g, w, m, v):
    rows = w.shape[0]
    tr = 512
    c1 = 1.0 - ADAM_B1 ** ADAM_STEP
    c2 = 1.0 - ADAM_B2 ** ADAM_STEP

    def body(g_ref, w_ref, m_ref, v_ref, d_ref, nm_ref, nv_ref):
        g = g_ref[...]
        nm = ADAM_B1 * m_ref[...] + (1.0 - ADAM_B1) * g
        nv = ADAM_B2 * v_ref[...] + (1.0 - ADAM_B2) * (g * g)
        nm_ref[...] = nm
        nv_ref[...] = nv
        d_ref[...] = -ADAM_LR * ((nm / c1) / (jnp.sqrt(nv / c2) + ADAM_EPS) + ADAM_WD * w_ref[...])

    spec = pl.BlockSpec((tr, 128), lambda i: (i, 0))
    shape = jax.ShapeDtypeStruct((rows, 128), F32)
    return pl.pallas_call(
        body, name="adamw_update", grid=(rows // tr,),
        in_specs=[spec] * 4, out_specs=[spec] * 3, out_shape=[shape] * 3,
        compiler_params=_params(("arbitrary",)),
    )(g, w, m, v)


SHARDED = ("w_in", "w_out", "branch_gain")
SMALL = ("norm_g", "b_f", "rel_bias", "w_s", "b_s", "v_gain", "final_g")
WEIGHTS = ("norm_g", "w_in", "b_f", "rel_bias", "w_s", "b_s", "v_gain", "branch_gain", "w_out", "final_g")
PACK_ORDER = SHARDED + SMALL
PACK_ROW_TILE = 512


def _rows_of(shape):
    return -(-int(np.prod(shape)) // 128)


def _pack(leaves):
    parts = []
    for a in leaves:
        flat = a.reshape(-1).astype(F32)
        parts.append(jnp.pad(flat, (0, _rows_of(a.shape) * 128 - flat.shape[0])))
    flat = jnp.concatenate(parts)
    rows = flat.shape[0] // 128
    total = -(-rows // PACK_ROW_TILE) * PACK_ROW_TILE
    return jnp.pad(flat, (0, (total - rows) * 128)).reshape(total, 128)


def _unpack(slab, shapes):
    out, row = [], 0
    for shp in shapes:
        n = int(np.prod(shp))
        r = _rows_of(shp)
        out.append(slab[row:row + r].reshape(-1)[:n].reshape(shp))
        row += r
    return out


def _pack_w_in(w):
    return jnp.concatenate([w[:, :2816], w[:, 2820:], w[:, 2816:2820],
                            jnp.zeros((w.shape[0], N_PACK - N_IN), w.dtype)], axis=1)


def _unpack_w_in(wp):
    return jnp.concatenate([wp[:, :2816], wp[:, F_COL:F_COL + N_HEADS], wp[:, 2816:F_COL]], axis=1)


def kernel(x, norm_g, w_in, b_f, rel_bias, w_s, b_s, v_gain, branch_gain, w_out, final_g, loss_target, m_norm_g, m_w_in, m_b_f, m_rel_bias, m_w_s, m_b_s, m_v_gain, m_branch_gain, m_w_out, m_final_g, v_norm_g, v_w_in, v_b_f, v_rel_bias, v_w_s, v_b_s, v_v_gain, v_branch_gain, v_w_out, v_final_g):
    depth = norm_g.shape[0]
    weights = dict(norm_g=norm_g, w_in=w_in, b_f=b_f, rel_bias=rel_bias, w_s=w_s, b_s=b_s, v_gain=v_gain,
                   branch_gain=branch_gain, w_out=w_out, final_g=final_g)
    mom1 = dict(norm_g=m_norm_g, w_in=m_w_in, b_f=m_b_f, rel_bias=m_rel_bias, w_s=m_w_s, b_s=m_b_s,
                v_gain=m_v_gain, branch_gain=m_branch_gain, w_out=m_w_out, final_g=m_final_g)
    mom2 = dict(norm_g=v_norm_g, w_in=v_w_in, b_f=v_b_f, rel_bias=v_rel_bias, w_s=v_w_s, b_s=v_b_s,
                v_gain=v_v_gain, branch_gain=v_branch_gain, w_out=v_w_out, final_g=v_final_g)

    n_in_rows = _rows_of(w_in.shape)
    n_out_rows = _rows_of(w_out.shape)
    wb = jnp.concatenate([w_in.astype(BF16).reshape(n_in_rows, 128), w_out.astype(BF16).reshape(n_out_rows, 128)])
    wf = jnp.pad(branch_gain.reshape(-1), (0, 8 * 128 - branch_gain.size)).reshape(8, 128)
    gb, gf = gather_weights(wb, wf)
    w_in_full = gb[:, :n_in_rows].reshape((4,) + w_in.shape)
    w_in_full = jnp.moveaxis(w_in_full, 0, 2).reshape(depth, D_MODEL, N_IN)
    w_out_full = gb[:, n_in_rows:].reshape((4,) + w_out.shape)
    w_out_full = jnp.moveaxis(w_out_full, 0, 1).reshape(depth, D_MODEL, D_MODEL)
    bg_full = gf.reshape(4, -1)[:, :branch_gain.size].reshape((4,) + branch_gain.shape)
    bg_full = jnp.moveaxis(bg_full, 0, 2).reshape(depth, 4, D_BRANCH)

    layers = [dict(norm_g=norm_g[l], wp=_pack_w_in(w_in_full[l]), b_f=b_f[l], rel_bias=rel_bias[l], w_s=w_s[l],
                   b_s=b_s[l], v_gain=v_gain[l], branch_gain=bg_full[l], wout=w_out_full[l]) for l in range(depth)]

    loss_part, grad_x, lgrads, dfinal = local_step(x[0], loss_target[0], layers, final_g)
    loss = lax.psum(loss_part, ("x", "y", "c"))

    stack = lambda k: jnp.stack([g[k] for g in lgrads])
    d_w_in = jnp.stack([_unpack_w_in(g["wp"]) for g in lgrads])
    d_w_out = stack("wout")
    d_bg = stack("branch_gain")
    small = dict(norm_g=stack("norm_g"), b_f=stack("b_f"), rel_bias=stack("rel_bias"), w_s=stack("w_s"),
                 b_s=stack("b_s"), v_gain=stack("v_gain"), final_g=dfinal)
    slabs = []
    for sidx in range(4):
        leaves = [d_w_in[:, :, sidx * N_SHARD:(sidx + 1) * N_SHARD],
                  d_w_out[:, sidx * D_BRANCH:(sidx + 1) * D_BRANCH, :],
                  d_bg[:, :, sidx * HEAD_DIM:(sidx + 1) * HEAD_DIM]] + [small[k] for k in SMALL]
        slabs.append(_pack(leaves))
    g_slab = reduce_grads(jnp.stack(slabs))

    pack_local = lambda d: _pack([d[k] for k in PACK_ORDER])
    d_slab, m_slab, v_slab = adamw_update(g_slab, pack_local(weights), pack_local(mom1), pack_local(mom2))
    shapes = [weights[k].shape for k in PACK_ORDER]
    outs = {}
    for tag, slab in (("grad", g_slab), ("delta", d_slab), ("new_m", m_slab), ("new_v", v_slab)):
        for k, a in zip(PACK_ORDER, _unpack(slab, shapes)):
            outs[tag, k] = a
    result = [loss, grad_x[None]]
    for tag in ("grad", "delta", "new_m", "new_v"):
        result += [outs[tag, k] for k in WEIGHTS]
    return tuple(result)
```

```python
import functools

import jax
import jax.numpy as jnp
import numpy as np
from jax import lax
from jax.experimental import pallas as pl
from jax.experimental.pallas import tpu as pltpu

F32 = jnp.float32
BF16 = jnp.bfloat16
MESH = pl.DeviceIdType.MESH

D_MODEL = 1024
D_BRANCH = 256
N_HEADS = 4
HEAD_DIM = 64
CHUNK = 64
LOOKBACK = 8
MAX_REL = 128
SG_CHUNK = 128
EPS = 1e-6
N_IN = 3844
N_PACK = 3968
F_COL = 3840
N_SHARD = 961
NEG = -1e30

A_TQ = 128
A_BAND = A_TQ + LOOKBACK * CHUNK
A_QB = 512
ATT_T = 256
FOX_MID = 4
FOX_WIDE = 8
FOX_DEAD = -110.0
SB_DEAD = -110.0
ROW_T = 512
VMEM_LIMIT = 56 * 1024 * 1024

ADAM_LR = 0.001
ADAM_B1 = 0.9
ADAM_B2 = 0.999
ADAM_EPS = 1e-08
ADAM_WD = 0.01
ADAM_STEP = 10

SEC_A_Q, SEC_A_K, SEC_A_V, SEC_A_G = 0, 256, 512, 768
SEC_B_U, SEC_B_V, SEC_B_G = 1024, 1280, 1536
SEC_C_Q, SEC_C_K, SEC_C_V, SEC_C_G = 1792, 2048, 2304, 2560
SEC_D_Q, SEC_D_K, SEC_D_V, SEC_D_G = 2816, 3072, 3328, 3584
QKV_SECS = (SEC_A_Q, SEC_A_K, SEC_A_V, SEC_C_Q, SEC_C_K, SEC_C_V, SEC_D_Q, SEC_D_K, SEC_D_V)
GATE_SECS = (SEC_A_G, SEC_B_G, SEC_C_G, SEC_D_G)


def _dot(a, b):
    return jnp.dot(a, b, preferred_element_type=F32)


def _dot_nt(a, b):
    return lax.dot_general(a, b, (((1,), (1,)), ((), ())), preferred_element_type=F32)


def _dot_tn(a, b):
    return lax.dot_general(a, b, (((0,), (0,)), ((), ())), preferred_element_type=F32)


def _split2(x):
    hi = x.astype(BF16)
    lo = (x - hi.astype(F32)).astype(BF16)
    return hi, lo


def _split3(x):
    hi = x.astype(BF16)
    r = x - hi.astype(F32)
    mid = r.astype(BF16)
    lo = (r - mid.astype(F32)).astype(BF16)
    return hi, mid, lo


def _sigmoid(x):
    return 1.0 / (1.0 + jnp.exp(-x))


def _params(sem=None, vmem=VMEM_LIMIT):
    return pltpu.CompilerParams(dimension_semantics=sem, vmem_limit_bytes=vmem)


def _heads_to_lanes(ref):
    return jnp.concatenate([ref[h] for h in range(N_HEADS)], axis=1)


def inproj_fwd(x, g, wp):
    s = x.shape[0]
    tm = min(ROW_T, s)

    def body(x_ref, g_ref, w_ref, h_ref, qkv_ref, gates_ref, uv_ref, f_ref):
        xv = x_ref[...]
        r = lax.rsqrt(jnp.mean(xv * xv, axis=-1, keepdims=True) + EPS)
        h = (xv * r * g_ref[...]).astype(BF16)
        h_ref[...] = h
        for n, off in enumerate(QKV_SECS):
            p = _dot(h, w_ref[:, off:off + D_BRANCH])
            for hh in range(N_HEADS):
                qkv_ref[n, hh] = p[:, hh * HEAD_DIM:(hh + 1) * HEAD_DIM].astype(BF16)
        for n, off in enumerate(GATE_SECS):
            gates_ref[:, n * D_BRANCH:(n + 1) * D_BRANCH] = _dot(h, w_ref[:, off:off + D_BRANCH])
        uv_ref[...] = _dot(h, w_ref[:, SEC_B_U:SEC_B_U + 2 * D_BRANCH])
        f_ref[...] = _dot(h, w_ref[:, F_COL:F_COL + 128])

    return pl.pallas_call(
        body, name="inproj_fwd", grid=(s // tm,),
        in_specs=[pl.BlockSpec((tm, D_MODEL), lambda i: (i, 0)),
                  pl.BlockSpec((1, D_MODEL), lambda i: (0, 0)),
                  pl.BlockSpec((D_MODEL, N_PACK), lambda i: (0, 0))],
        out_specs=[pl.BlockSpec((tm, D_MODEL), lambda i: (i, 0)),
                   pl.BlockSpec((9, N_HEADS, tm, HEAD_DIM), lambda i: (0, 0, i, 0)),
                   pl.BlockSpec((tm, D_MODEL), lambda i: (i, 0)),
                   pl.BlockSpec((tm, 2 * D_BRANCH), lambda i: (i, 0)),
                   pl.BlockSpec((tm, 128), lambda i: (i, 0))],
        out_shape=[jax.ShapeDtypeStruct((s, D_MODEL), BF16),
                   jax.ShapeDtypeStruct((9, N_HEADS, s, HEAD_DIM), BF16),
                   jax.ShapeDtypeStruct((s, D_MODEL), F32),
                   jax.ShapeDtypeStruct((s, 2 * D_BRANCH), F32),
                   jax.ShapeDtypeStruct((s, 128), F32)],
        compiler_params=_params(("arbitrary",)),
    )(x, g, wp)


def inproj_bwd(dqkv, dgates, duv, dfp, wp, x, g, dres):
    s = x.shape[0]
    tm = min(ROW_T, s)

    def body(*refs):
        dq_refs = refs[:9]
        dgates_ref, duv_ref, dfp_ref, w_ref, x_ref, g_ref, dres_ref, dp_ref, dx_ref, dg_ref = refs[9:]
        i = pl.program_id(0)
        a_q, a_k, a_v, c_q, c_k, c_v, d_q, d_k, d_v = [_heads_to_lanes(r).astype(BF16) for r in dq_refs]
        dgt = dgates_ref[...]
        duv_b = duv_ref[...].astype(BF16)
        dp = jnp.concatenate(
            [a_q, a_k, a_v, dgt[:, 0:256], duv_b, dgt[:, 256:512], c_q, c_k, c_v, dgt[:, 512:768],
             d_q, d_k, d_v, dgt[:, 768:1024], dfp_ref[...].astype(BF16)], axis=1)
        dp_ref[...] = dp
        dh = _dot_nt(dp, w_ref[...])
        xv = x_ref[...]
        r = lax.rsqrt(jnp.mean(xv * xv, axis=-1, keepdims=True) + EPS)
        xn = xv * r
        u = dh * g_ref[...]
        dx_ref[...] = dres_ref[...] + r * (u - xn * jnp.mean(xn * u, axis=-1, keepdims=True))

        @pl.when(i == 0)
        def _():
            dg_ref[...] = jnp.zeros_like(dg_ref)

        dg_ref[...] += jnp.sum(dh * xn, axis=0, keepdims=True)

    head_spec = pl.BlockSpec((N_HEADS, tm, HEAD_DIM), lambda i: (0, i, 0))
    return pl.pallas_call(
        body, name="inproj_bwd", grid=(s // tm,),
        in_specs=[head_spec] * 9 + [
            pl.BlockSpec((tm, D_MODEL), lambda i: (i, 0)),
            pl.BlockSpec((tm, 2 * D_BRANCH), lambda i: (i, 0)),
            pl.BlockSpec((tm, 128), lambda i: (i, 0)),
            pl.BlockSpec((D_MODEL, N_PACK), lambda i: (0, 0)),
            pl.BlockSpec((tm, D_MODEL), lambda i: (i, 0)),
            pl.BlockSpec((1, D_MODEL), lambda i: (0, 0)),
            pl.BlockSpec((tm, D_MODEL), lambda i: (i, 0))],
        out_specs=[pl.BlockSpec((tm, N_PACK), lambda i: (i, 0)),
                   pl.BlockSpec((tm, D_MODEL), lambda i: (i, 0)),
                   pl.BlockSpec((1, D_MODEL), lambda i: (0, 0))],
        out_shape=[jax.ShapeDtypeStruct((s, N_PACK), BF16),
                   jax.ShapeDtypeStruct((s, D_MODEL), F32),
                   jax.ShapeDtypeStruct((1, D_MODEL), F32)],
        compiler_params=_params(("arbitrary",)),
    )(*dqkv, dgates, duv, dfp, wp, x, g, dres)


def weight_grad(a, b, name):
    s, m = a.shape
    n = b.shape[1]
    tm = min(ROW_T, s)
    tmm = 256
    nsteps = s // tm

    def body(a_ref, b_ref, o_ref):
        k = pl.program_id(1)

        @pl.when(k == 0)
        def _():
            o_ref[...] = jnp.zeros_like(o_ref)

        o_ref[...] += _dot_tn(a_ref[...], b_ref[...])

    return pl.pallas_call(
        body, name=name, grid=(m // tmm, nsteps),
        in_specs=[pl.BlockSpec((tm, tmm), lambda j, k: (k, j)),
                  pl.BlockSpec((tm, n), lambda j, k: (k, 0))],
        out_specs=pl.BlockSpec((tmm, n), lambda j, k: (j, 0)),
        out_shape=jax.ShapeDtypeStruct((m, n), F32),
        compiler_params=_params(("arbitrary", "arbitrary")),
    )(a, b)


def _a_specs(s):
    nq = s // A_QB
    q_spec = pl.BlockSpec((None, None, A_QB, HEAD_DIM), lambda h, i: (0, h, jnp.minimum(i, nq - 1), 0))
    kv_specs = [pl.BlockSpec((None, A_QB, HEAD_DIM), lambda h, i, m=m: (h, jnp.minimum(i + m, nq), 0)) for m in range(2)]
    t_spec = pl.BlockSpec((None, A_TQ, A_BAND), lambda h, i: (h, 0, 0))
    return nq, q_spec, kv_specs, t_spec


def _a_scores(q_ref, k, t_ref, i, j):
    rows = slice(j * A_TQ, (j + 1) * A_TQ)
    qs = q_ref[rows, :] * 0.125
    kj = k[j * A_TQ:j * A_TQ + A_BAND, :]
    sc = _dot_nt(qs, kj) + t_ref[...]
    col = lax.broadcasted_iota(jnp.int32, (A_TQ, A_BAND), 1)
    sc = jnp.where(col >= (A_BAND - A_TQ) - i * A_QB - j * A_TQ, sc, NEG)
    return rows, qs, kj, sc


def mix_a_fwd(qkv, kpad, vpad, tbias):
    s = qkv.shape[2]
    nq, q_spec, kv_specs, t_spec = _a_specs(s)

    def body(q_ref, k0_ref, k1_ref, v0_ref, v1_ref, t_ref, o_ref, lse_ref):
        i = pl.program_id(1)
        k = jnp.concatenate([k0_ref[...], k1_ref[...]], axis=0)
        v = jnp.concatenate([v0_ref[...], v1_ref[...]], axis=0)
        for j in range(A_QB // A_TQ):
            rows, _, _, sc = _a_scores(q_ref, k, t_ref, i, j)
            m = jnp.max(sc, axis=-1, keepdims=True)
            p = jnp.exp(sc - m)
            l = jnp.sum(p, axis=-1, keepdims=True)
            o_ref[rows, :] = _dot(p.astype(BF16), v[j * A_TQ:j * A_TQ + A_BAND, :]) / l
            lse_ref[rows, :] = m + jnp.log(l)

    return pl.pallas_call(
        body, name="mix_a_fwd", grid=(N_HEADS, nq),
        in_specs=[q_spec] + kv_specs + kv_specs + [t_spec],
        out_specs=[pl.BlockSpec((None, A_QB, HEAD_DIM), lambda h, i: (h, i, 0)),
                   pl.BlockSpec((None, A_QB, 1), lambda h, i: (h, i, 0))],
        out_shape=[jax.ShapeDtypeStruct((N_HEADS, s, HEAD_DIM), F32),
                   jax.ShapeDtypeStruct((N_HEADS, s, 1), F32)],
        compiler_params=_params(("arbitrary", "arbitrary")),
    )(qkv, kpad, kpad, vpad, vpad, tbias)


def mix_a_bwd(qkv, kpad, vpad, tbias, do, o, lse):
    s = qkv.shape[2]
    nq, q_spec, kv_specs, t_spec = _a_specs(s)
    row_spec = lambda w: pl.BlockSpec((None, A_QB, w), lambda h, i: (h, jnp.minimum(i, nq - 1), 0))
    keep = A_BAND - A_TQ
    win = 2 * A_QB

    def body(q_ref, k0_ref, k1_ref, v0_ref, v1_ref, t_ref, do_ref, o_ref, lse_ref,
             dq_ref, dk_ref, dv_ref, dt_ref, dk_win, dv_win):
        i = pl.program_id(1)

        @pl.when(i == 0)
        def _():
            dk_win[...] = jnp.zeros_like(dk_win)
            dv_win[...] = jnp.zeros_like(dv_win)
            dt_ref[...] = jnp.zeros_like(dt_ref)

        @pl.when(i < nq)
        def _():
            k = jnp.concatenate([k0_ref[...], k1_ref[...]], axis=0)
            v = jnp.concatenate([v0_ref[...], v1_ref[...]], axis=0)
            dt = jnp.zeros((A_TQ, A_BAND), F32)
            for j in range(A_QB // A_TQ):
                rows, qs, kj, sc = _a_scores(q_ref, k, t_ref, i, j)
                keys = slice(j * A_TQ, j * A_TQ + A_BAND)
                dob = do_ref[rows, :]
                p = jnp.exp(sc - lse_ref[rows, :])
                delta = jnp.sum(o_ref[rows, :] * dob.astype(F32), axis=-1, keepdims=True)
                ds = p * (_dot_nt(dob, v[keys, :]) - delta)
                dsb = ds.astype(BF16)
                dq_ref[rows, :] = _dot(dsb, kj) * 0.125
                dk_win[keys, :] += _dot_tn(dsb, qs)
                dv_win[keys, :] += _dot_tn(p.astype(BF16), dob)
                dt = dt + ds
            dt_ref[...] += dt

        dk_ref[...] = dk_win[0:A_QB, :]
        dv_ref[...] = dv_win[0:A_QB, :]
        dk_rest = dk_win[A_QB:win, :]
        dv_rest = dv_win[A_QB:win, :]
        dk_win[0:A_QB, :] = dk_rest
        dv_win[0:A_QB, :] = dv_rest
        dk_win[A_QB:win, :] = jnp.zeros((A_QB, HEAD_DIM), F32)
        dv_win[A_QB:win, :] = jnp.zeros((A_QB, HEAD_DIM), F32)

    return pl.pallas_call(
        body, name="mix_a_bwd", grid=(N_HEADS, nq + 1),
        in_specs=[q_spec] + kv_specs + kv_specs + [t_spec, row_spec(HEAD_DIM), row_spec(HEAD_DIM), row_spec(1)],
        out_specs=[row_spec(HEAD_DIM),
                   pl.BlockSpec((None, A_QB, HEAD_DIM), lambda h, i: (h, i, 0)),
                   pl.BlockSpec((None, A_QB, HEAD_DIM), lambda h, i: (h, i, 0)),
                   t_spec],
        out_shape=[jax.ShapeDtypeStruct((N_HEADS, s, HEAD_DIM), F32),
                   jax.ShapeDtypeStruct((N_HEADS, s + keep, HEAD_DIM), F32),
                   jax.ShapeDtypeStruct((N_HEADS, s + keep, HEAD_DIM), F32),
                   jax.ShapeDtypeStruct((N_HEADS, A_TQ, A_BAND), F32)],
        scratch_shapes=[pltpu.VMEM((win, HEAD_DIM), F32), pltpu.VMEM((win, HEAD_DIM), F32)],
        compiler_params=_params(("arbitrary", "arbitrary")),
    )(qkv, kpad, kpad, vpad, vpad, tbias, do, o, lse)


def relbias_tile(rel_bias, relmat):
    nrel = 2 * MAX_REL + 1

    def body(rb_ref, rel_ref, o_ref):
        rel = rel_ref[...]
        o_ref[...] = jnp.full(o_ref.shape, NEG, F32)

        def step(r, carry):
            hit = rel == r
            for h in range(N_HEADS):
                o_ref[h] = jnp.where(hit, rb_ref[h, r], o_ref[h])
            return carry

        lax.fori_loop(0, nrel, step, 0)

    return pl.pallas_call(
        body, name="relbias_tile",
        in_specs=[pl.BlockSpec(memory_space=pltpu.SMEM), pl.BlockSpec(memory_space=pltpu.VMEM)],
        out_specs=pl.BlockSpec(memory_space=pltpu.VMEM),
        out_shape=jax.ShapeDtypeStruct((N_HEADS, A_TQ, A_BAND), F32),
        compiler_params=_params(),
    )(rel_bias, relmat)


def relbias_grad(dt, relmat):
    nrel = 2 * MAX_REL + 1

    def body(dt_ref, rel_ref, o_ref):
        rel = rel_ref[...]
        lane = lax.broadcasted_iota(jnp.int32, (8, 384), 1)
        row = lax.broadcasted_iota(jnp.int32, (8, 384), 0)

        def step(r, acc):
            hit = rel == r
            for h in range(N_HEADS):
                val = jnp.sum(jnp.where(hit, dt_ref[h], 0.0))
                acc = jnp.where((lane == r) & (row == h), val, acc)
            return acc

        o_ref[...] = lax.fori_loop(0, nrel, step, jnp.zeros((8, 384), F32))

    return pl.pallas_call(
        body, name="relbias_grad",
        out_shape=jax.ShapeDtypeStruct((8, 384), F32),
        compiler_params=_params(),
    )(dt, relmat)


def _b_norm(v, gain):
    mu = jnp.mean(v, axis=-1, keepdims=True)
    xc = v - mu
    rstd = lax.rsqrt(jnp.mean(xc * xc, axis=-1, keepdims=True) + EPS)
    xhat = xc * rstd
    return xhat, rstd, xhat * gain


def _tril_mask():
    t = lax.broadcasted_iota(jnp.int32, (SG_CHUNK, SG_CHUNK), 0)
    u = lax.broadcasted_iota(jnp.int32, (SG_CHUNK, SG_CHUNK), 1)
    return u <= t


def mix_b_fwd(uv, gain, w_s, b_col):
    s = uv.shape[0]
    tm = min(ROW_T, s)

    def body(uv_ref, gain_ref, w_ref, b_ref, y_ref):
        tril = _tril_mask()
        ws = [jnp.where(tril, w_ref[g], 0.0).astype(BF16) for g in range(N_HEADS)]
        for c in range(tm // SG_CHUNK):
            rows = slice(c * SG_CHUNK, (c + 1) * SG_CHUNK)
            u = uv_ref[rows, 0:D_BRANCH]
            _, _, vn = _b_norm(uv_ref[rows, D_BRANCH:2 * D_BRANCH], gain_ref[...])
            vnb = vn.astype(BF16)
            outs = []
            for g in range(N_HEADS):
                cols = slice(g * HEAD_DIM, (g + 1) * HEAD_DIM)
                mixed = _dot(ws[g], vnb[:, cols]) + b_ref[g]
                outs.append(u[:, cols] * mixed)
            y_ref[rows, :] = jnp.concatenate(outs, axis=1)

    return pl.pallas_call(
        body, name="mix_b_fwd", grid=(s // tm,),
        in_specs=[pl.BlockSpec((tm, 2 * D_BRANCH), lambda i: (i, 0)),
                  pl.BlockSpec((1, D_BRANCH), lambda i: (0, 0)),
                  pl.BlockSpec((N_HEADS, SG_CHUNK, SG_CHUNK), lambda i: (0, 0, 0)),
                  pl.BlockSpec((N_HEADS, SG_CHUNK, 1), lambda i: (0, 0, 0))],
        out_specs=pl.BlockSpec((tm, D_BRANCH), lambda i: (i, 0)),
        out_shape=jax.ShapeDtypeStruct((s, D_BRANCH), F32),
        compiler_params=_params(("arbitrary",)),
    )(uv, gain, w_s, b_col)


def mix_b_bwd(uv, gain, w_s, b_col, dy):
    s = uv.shape[0]
    tm = min(ROW_T, s)

    def body(uv_ref, gain_ref, w_ref, b_ref, dy_ref, duv_ref, dw_ref, db_ref, dgain_ref):
        i = pl.program_id(0)

        @pl.when(i == 0)
        def _():
            dw_ref[...] = jnp.zeros_like(dw_ref)
            db_ref[...] = jnp.zeros_like(db_ref)
            dgain_ref[...] = jnp.zeros_like(dgain_ref)

        tril = _tril_mask()
        ws = [jnp.where(tril, w_ref[g], 0.0).astype(BF16) for g in range(N_HEADS)]
        gain_v = gain_ref[...]
        for c in range(tm // SG_CHUNK):
            rows = slice(c * SG_CHUNK, (c + 1) * SG_CHUNK)
            u = uv_ref[rows, 0:D_BRANCH]
            xhat, rstd, vn = _b_norm(uv_ref[rows, D_BRANCH:2 * D_BRANCH], gain_v)
            vnb = vn.astype(BF16)
            dyv = dy_ref[rows, :]
            dus, dvns = [], []
            for g in range(N_HEADS):
                cols = slice(g * HEAD_DIM, (g + 1) * HEAD_DIM)
                mixed = _dot(ws[g], vnb[:, cols]) + b_ref[g]
                dus.append(dyv[:, cols] * mixed)
                dmixed = dyv[:, cols] * u[:, cols]
                dmb = dmixed.astype(BF16)
                db_ref[g] += jnp.sum(dmixed, axis=-1, keepdims=True)
                dw_ref[g] += jnp.where(tril, _dot_nt(dmb, vnb[:, cols]), 0.0)
                dvns.append(_dot_tn(ws[g], dmb))
            dvn = jnp.concatenate(dvns, axis=1)
            dgain_ref[...] += jnp.sum(dvn * xhat, axis=0, keepdims=True)
            dxh = dvn * gain_v
            dv = rstd * (dxh - jnp.mean(dxh, axis=-1, keepdims=True)
                         - xhat * jnp.mean(dxh * xhat, axis=-1, keepdims=True))
            duv_ref[rows, :] = jnp.concatenate(dus + [dv], axis=1)

    return pl.pallas_call(
        body, name="mix_b_bwd", grid=(s // tm,),
        in_specs=[pl.BlockSpec((tm, 2 * D_BRANCH), lambda i: (i, 0)),
                  pl.BlockSpec((1, D_BRANCH), lambda i: (0, 0)),
                  pl.BlockSpec((N_HEADS, SG_CHUNK, SG_CHUNK), lambda i: (0, 0, 0)),
                  pl.BlockSpec((N_HEADS, SG_CHUNK, 1), lambda i: (0, 0, 0)),
                  pl.BlockSpec((tm, D_BRANCH), lambda i: (i, 0))],
        out_specs=[pl.BlockSpec((tm, 2 * D_BRANCH), lambda i: (i, 0)),
                   pl.BlockSpec((N_HEADS, SG_CHUNK, SG_CHUNK), lambda i: (0, 0, 0)),
                   pl.BlockSpec((N_HEADS, SG_CHUNK, 1), lambda i: (0, 0, 0)),
                   pl.BlockSpec((1, D_BRANCH), lambda i: (0, 0))],
        out_shape=[jax.ShapeDtypeStruct((s, 2 * D_BRANCH), F32),
                   jax.ShapeDtypeStruct((N_HEADS, SG_CHUNK, SG_CHUNK), F32),
                   jax.ShapeDtypeStruct((N_HEADS, SG_CHUNK, 1), F32),
                   jax.ShapeDtypeStruct((1, D_BRANCH), F32)],
        compiler_params=_params(("arbitrary",)),
    )(uv, gain, w_s, b_col, dy)


def _scan_mats(nrow):
    a = lax.broadcasted_iota(jnp.int32, (128, 128), 0)
    b = lax.broadcasted_iota(jnp.int32, (128, 128), 1)
    r = lax.broadcasted_iota(jnp.int32, (nrow, nrow), 0)
    c = lax.broadcasted_iota(jnp.int32, (nrow, nrow), 1)
    nb = nrow // N_HEADS
    same = (r // nb) == (c // nb)
    return a, b, r, c, same


def _exact_dot(x, m):
    hi, mid, lo = _split3(x)
    return _dot(hi, m) + _dot(mid, m) + _dot(lo, m)


def _exact_dot_left(m, x):
    hi, mid, lo = _split3(x)
    return _dot(m, hi) + _dot(m, mid) + _dot(m, lo)


def fox_gate_fwd(ft, bcol):
    nrow = ft.shape[0]

    def body(f_ref, b_ref, c_ref):
        z = f_ref[...] + b_ref[...]
        ls = jnp.minimum(z, 0.0) - jnp.log(1.0 + jnp.exp(-jnp.abs(z)))
        a, b, r, c, same = _scan_mats(nrow)
        within = _exact_dot(ls, (a <= b).astype(BF16))
        tot = jnp.broadcast_to(within[:, 127:128], within.shape)
        before = _exact_dot_left((same & (c < r)).astype(BF16), tot)
        c_ref[...] = within + before

    return pl.pallas_call(
        body, name="fox_gate_fwd",
        out_shape=jax.ShapeDtypeStruct((nrow, 128), F32),
        compiler_params=_params(),
    )(ft, bcol)


def fox_gate_bwd(ft, bcol, dc):
    nrow = ft.shape[0]

    def body(f_ref, b_ref, dc_ref, df_ref, db_ref):
        a, b, r, c, same = _scan_mats(nrow)
        dcv = dc_ref[...]
        within = _exact_dot(dcv, (a >= b).astype(BF16))
        tot = jnp.broadcast_to(within[:, 0:1], within.shape)
        after = _exact_dot_left((same & (c > r)).astype(BF16), tot)
        dls = within + after
        z = f_ref[...] + b_ref[...]
        dz = dls * _sigmoid(-z)
        df_ref[...] = dz
        rs = jnp.broadcast_to(jnp.sum(dz, axis=-1, keepdims=True), dz.shape)
        hr = lax.broadcasted_iota(jnp.int32, (8, nrow), 0)
        hc = lax.broadcasted_iota(jnp.int32, (8, nrow), 1)
        db_ref[...] = _exact_dot_left((hr == hc // (nrow // N_HEADS)).astype(BF16), rs)

    return pl.pallas_call(
        body, name="fox_gate_bwd",
        out_shape=[jax.ShapeDtypeStruct((nrow, 128), F32), jax.ShapeDtypeStruct((8, 128), F32)],
        compiler_params=_params(),
    )(ft, bcol, dc)


def _att_specs(s, qi, ki, vi):
    t = ATT_T
    q_spec = pl.BlockSpec((None, None, t, HEAD_DIM), lambda h, i: (qi, h, i, 0))
    k_spec = pl.BlockSpec((None, None, s, HEAD_DIM), lambda h, i: (ki, h, 0, 0))
    v_spec = pl.BlockSpec((None, None, s, HEAD_DIM), lambda h, i: (vi, h, 0, 0))
    row_spec = lambda w: pl.BlockSpec((None, t, w), lambda h, i: (h, i, 0))
    return q_spec, k_spec, v_spec, row_spec


def _causal(strict):
    row = lax.broadcasted_iota(jnp.int32, (ATT_T, ATT_T), 0)
    col = lax.broadcasted_iota(jnp.int32, (ATT_T, ATT_T), 1)
    return (col < row) if strict else (col <= row)


def _gate_row(cr_ref, kb, g):
    if g == 1:
        return cr_ref[kb]
    return jnp.concatenate([cr_ref[kb + n] for n in range(g)], axis=1)


def _fox_walk(i, carry, tile, alive):
    g = FOX_WIDE
    nmid = i // FOX_MID
    nwide = i // g
    carry = tile(i, 1, carry, True)
    carry = lax.fori_loop(0, i - nmid * FOX_MID, lambda n, c: tile(i - 1 - n, 1, c, False), carry)
    carry = lax.fori_loop(0, nmid - nwide * (g // FOX_MID), lambda n, c: tile(nwide * g, FOX_MID, c, False), carry)

    def cond(state):
        return jnp.logical_and(state[0] >= 0, state[1] > 0)

    def step(state):
        n = state[0]
        c = tile(n * g, g, state[2:], False)
        return (n - 1, alive(n * g, c)) + tuple(c)

    out = lax.while_loop(cond, step, (nwide - 1, alive(nwide * g, carry)) + tuple(carry))
    return out[2:]


def _fox_reach(qs, k_ref, kmax_ref, cc, i):
    s = k_ref.shape[0]
    rows = 4 * ATT_T

    @pl.when(i == 0)
    def _():
        def chunk(n, mx):
            kc = k_ref[pl.ds(pl.multiple_of(n * rows, rows), rows), :].astype(F32)
            return jnp.maximum(mx, jnp.max(jnp.sum(kc * kc, axis=-1, keepdims=True)))

        kmax_ref[0] = jnp.sqrt(lax.fori_loop(0, s // rows, chunk, jnp.float32(0.0)))

    qf = qs.astype(F32)
    return jnp.sqrt(jnp.sum(qf * qf, axis=-1, keepdims=True)) * kmax_ref[0] + cc


def fox_fwd(qkv, c_col, c_row):
    s = qkv.shape[2]
    t = ATT_T
    nq = s // t
    q_spec, k_spec, v_spec, row_spec = _att_specs(s, 3, 4, 5)

    def body(q_ref, k_ref, v_ref, cc_ref, cr_ref, o_ref, lse_ref, kmax_ref):
        i = pl.program_id(1)
        qs = q_ref[...] * 0.125
        cc = cc_ref[...]
        reach = _fox_reach(qs, k_ref, kmax_ref, cc, i)

        def alive(kb, carry):
            return (jnp.max(reach - cr_ref[kb][:, 0:1] - carry[0]) > FOX_DEAD).astype(jnp.int32)

        def tile(kb, g, carry, masked):
            m, l, acc = carry
            k0 = pl.multiple_of(kb * t, t)
            sc = _dot_nt(qs, k_ref[pl.ds(k0, g * t), :]) + (cc - _gate_row(cr_ref, kb, g))
            if masked:
                sc = jnp.where(_causal(False), sc, NEG)
            m_new = jnp.maximum(m, jnp.max(sc, axis=-1, keepdims=True))
            alpha = jnp.exp(m - m_new)
            p = jnp.exp(sc - m_new)
            l = alpha * l + jnp.sum(p, axis=-1, keepdims=True)
            p_hi, p_lo = _split2(p)
            v = v_ref[pl.ds(k0, g * t), :]
            acc = alpha * acc + (_dot(p_hi, v) + _dot(p_lo, v))
            return m_new, l, acc

        init = (jnp.full((t, 1), NEG, F32), jnp.zeros((t, 1), F32), jnp.zeros((t, HEAD_DIM), F32))
        m, l, acc = _fox_walk(i, init, tile, alive)
        o_ref[...] = acc / l
        lse_ref[...] = m + jnp.log(l)

    return pl.pallas_call(
        body, name="fox_fwd", grid=(N_HEADS, nq),
        in_specs=[q_spec, k_spec, v_spec, row_spec(1),
                  pl.BlockSpec((None, nq, 1, t), lambda h, i: (h, 0, 0, 0))],
        out_specs=[row_spec(HEAD_DIM), row_spec(1)],
        out_shape=[jax.ShapeDtypeStruct((N_HEADS, s, HEAD_DIM), F32),
                   jax.ShapeDtypeStruct((N_HEADS, s, 1), F32)],
        scratch_shapes=[pltpu.SMEM((1,), F32)],
        compiler_params=_params(("arbitrary", "arbitrary")),
    )(qkv, qkv, qkv, c_col, c_row)


def fox_bwd(qkv, c_col, c_row, do, o, lse):
    s = qkv.shape[2]
    t = ATT_T
    nq = s // t
    q_spec, k_spec, v_spec, row_spec = _att_specs(s, 3, 4, 5)
    any_spec = pl.BlockSpec(memory_space=pl.ANY)

    def body(q_ref, k_ref, v_ref, cc_ref, cr_ref, do_ref, o_ref, lse_ref,
             dq_ref, dk_hbm, dv_hbm, dc_ref, dk_acc, dv_acc, kmax_ref):
        h = pl.program_id(0)
        i = pl.program_id(1)

        @pl.when(i == 0)
        def _():
            dk_acc[...] = jnp.zeros_like(dk_acc)
            dv_acc[...] = jnp.zeros_like(dv_acc)
            dc_ref[...] = jnp.zeros_like(dc_ref)

        qs = q_ref[...] * 0.125
        dob = do_ref[...]
        delta = jnp.sum(o_ref[...] * dob.astype(F32), axis=-1, keepdims=True)
        lse = lse_ref[...]
        cc = cc_ref[...]
        margin = _fox_reach(qs, k_ref, kmax_ref, cc, i) - lse

        def alive(kb, carry):
            return (jnp.max(margin - cr_ref[kb][:, 0:1]) > FOX_DEAD).astype(jnp.int32)

        def tile(kb, g, carry, masked):
            dq, = carry
            k0 = pl.multiple_of(kb * t, t)
            k = k_ref[pl.ds(k0, g * t), :]
            sc = _dot_nt(qs, k) + (cc - _gate_row(cr_ref, kb, g))
            if masked:
                sc = jnp.where(_causal(False), sc, NEG)
            p = jnp.exp(sc - lse)
            ds = p * (_dot_nt(dob, v_ref[pl.ds(k0, g * t), :]) - delta)
            dsb = ds.astype(BF16)
            dk_acc[pl.ds(k0, g * t), :] += _dot_tn(dsb, qs)
            dv_acc[pl.ds(k0, g * t), :] += _dot_tn(p.astype(BF16), dob)
            dcs = -jnp.sum(ds, axis=0, keepdims=True)
            for n in range(g):
                dc_ref[kb + n] += dcs[:, n * t:(n + 1) * t]
            return (dq + _dot(dsb, k),)

        dq, = _fox_walk(i, (jnp.zeros((t, HEAD_DIM), F32),), tile, alive)
        dq_ref[...] = dq * 0.125

        @pl.when(i == nq - 1)
        def _():
            pltpu.sync_copy(dk_acc, dk_hbm.at[h])
            pltpu.sync_copy(dv_acc, dv_hbm.at[h])

    return pl.pallas_call(
        body, name="fox_bwd", grid=(N_HEADS, nq),
        in_specs=[q_spec, k_spec, v_spec, row_spec(1),
                  pl.BlockSpec((None, nq, 1, t), lambda h, i: (h, 0, 0, 0)),
                  row_spec(HEAD_DIM), row_spec(HEAD_DIM), row_spec(1)],
        out_specs=[row_spec(HEAD_DIM), any_spec, any_spec,
                   pl.BlockSpec((None, nq, 1, t), lambda h, i: (h, 0, 0, 0))],
        out_shape=[jax.ShapeDtypeStruct((N_HEADS, s, HEAD_DIM), F32),
                   jax.ShapeDtypeStruct((N_HEADS, s, HEAD_DIM), F32),
                   jax.ShapeDtypeStruct((N_HEADS, s, HEAD_DIM), F32),
                   jax.ShapeDtypeStruct((N_HEADS, nq, 1, t), F32)],
        scratch_shapes=[pltpu.VMEM((s, HEAD_DIM), F32), pltpu.VMEM((s, HEAD_DIM), F32), pltpu.SMEM((1,), F32)],
        compiler_params=_params(("arbitrary", "arbitrary")),
    )(qkv, qkv, qkv, c_col, c_row, do, o, lse)


def _sb_tile(qs, k, run, masked):
    z = _dot_nt(qs, k)
    sp = jnp.log(1.0 + jnp.exp(-jnp.abs(z)))
    ls = jnp.minimum(z, 0.0) - sp
    lm = -jnp.maximum(z, 0.0) - sp
    if masked:
        valid = _causal(True)
        lm = jnp.where(valid, lm, 0.0)
    row = lax.broadcasted_iota(jnp.int32, (ATT_T, ATT_T), 0)
    col = lax.broadcasted_iota(jnp.int32, (ATT_T, ATT_T), 1)
    later = (row > col).astype(BF16)
    hi, lo = _split2(lm)
    between = run + _dot(hi, later) + _dot(lo, later)
    a = jnp.exp(ls + between)
    if masked:
        a = jnp.where(valid, a, 0.0)
    return ls, lm, a


def _sb_walk(i, carry, tile):
    def alive_of(c):
        return (jnp.max(c[0]) > SB_DEAD).astype(jnp.int32)

    def cond(state):
        n, alive = state[0], state[1]
        return jnp.logical_and(n < i, alive > 0)

    def step(state):
        n = state[0]
        c = tile(i - 1 - n, state[2:], False)
        return (n + 1, alive_of(c)) + tuple(c)

    out = lax.while_loop(cond, step, (jnp.int32(0), alive_of(carry)) + tuple(carry))
    return out[2:]


def sb_fwd(qkv):
    s = qkv.shape[2]
    t = ATT_T
    nq = s // t
    q_spec, k_spec, v_spec, row_spec = _att_specs(s, 6, 7, 8)

    def body(q_ref, k_ref, v_ref, o_ref):
        i = pl.program_id(1)
        qs = q_ref[...] * 0.125

        def tile(kb, carry, masked):
            run, acc = carry
            k0 = pl.multiple_of(kb * t, t)
            _, lm, a = _sb_tile(qs, k_ref[pl.ds(k0, t), :], run, masked)
            acc = acc + _dot(a.astype(BF16), v_ref[pl.ds(k0, t), :])
            return run + jnp.sum(lm, axis=-1, keepdims=True), acc

        carry = tile(i, (jnp.zeros((t, 1), F32), jnp.zeros((t, HEAD_DIM), F32)), True)
        _, acc = _sb_walk(i, carry, tile)
        o_ref[...] = acc

    return pl.pallas_call(
        body, name="sb_fwd", grid=(N_HEADS, nq),
        in_specs=[q_spec, k_spec, v_spec],
        out_specs=row_spec(HEAD_DIM),
        out_shape=jax.ShapeDtypeStruct((N_HEADS, s, HEAD_DIM), F32),
        compiler_params=_params(("arbitrary", "arbitrary")),
    )(qkv, qkv, qkv)


def sb_bwd(qkv, do, o):
    s = qkv.shape[2]
    t = ATT_T
    nq = s // t
    q_spec, k_spec, v_spec, row_spec = _att_specs(s, 6, 7, 8)
    any_spec = pl.BlockSpec(memory_space=pl.ANY)

    def body(q_ref, k_ref, v_ref, do_ref, o_ref, dq_ref, dk_hbm, dv_hbm, dk_acc, dv_acc):
        h = pl.program_id(0)
        i = pl.program_id(1)

        @pl.when(i == 0)
        def _():
            dk_acc[...] = jnp.zeros_like(dk_acc)
            dv_acc[...] = jnp.zeros_like(dv_acc)

        qs = q_ref[...] * 0.125
        dob = do_ref[...]
        tot = jnp.sum(o_ref[...] * dob.astype(F32), axis=-1, keepdims=True)

        def tile(kb, carry, masked):
            run, run_g, dq = carry
            k0 = pl.multiple_of(kb * t, t)
            k = k_ref[pl.ds(k0, t), :]
            ls, lm, a = _sb_tile(qs, k, run, masked)
            ab = a.astype(BF16)
            g = ab.astype(F32) * _dot_nt(dob, v_ref[pl.ds(k0, t), :])
            row = lax.broadcasted_iota(jnp.int32, (t, t), 0)
            col = lax.broadcasted_iota(jnp.int32, (t, t), 1)
            from_here = (row >= col).astype(BF16)
            hi, lo = _split2(g)
            g_right = run_g + _dot(hi, from_here) + _dot(lo, from_here)
            g_left = tot - g_right
            dz = g - jnp.exp(ls) * (g + g_left)
            if masked:
                dz = jnp.where(_causal(True), dz, 0.0)
            dzb = dz.astype(BF16)
            dk_acc[pl.ds(k0, t), :] += _dot_tn(dzb, qs)
            dv_acc[pl.ds(k0, t), :] += _dot_tn(ab, dob)
            return (run + jnp.sum(lm, axis=-1, keepdims=True),
                    run_g + jnp.sum(g, axis=-1, keepdims=True),
                    dq + _dot(dzb, k))

        zero = jnp.zeros((t, 1), F32)
        carry = tile(i, (zero, zero, jnp.zeros((t, HEAD_DIM), F32)), True)
        _, _, dq = _sb_walk(i, carry, tile)
        dq_ref[...] = dq * 0.125

        @pl.when(i == nq - 1)
        def _():
            pltpu.sync_copy(dk_acc, dk_hbm.at[h])
            pltpu.sync_copy(dv_acc, dv_hbm.at[h])

    return pl.pallas_call(
        body, name="sb_bwd", grid=(N_HEADS, nq),
        in_specs=[q_spec, k_spec, v_spec, row_spec(HEAD_DIM), row_spec(HEAD_DIM)],
        out_specs=[row_spec(HEAD_DIM), any_spec, any_spec],
        out_shape=[jax.ShapeDtypeStruct((N_HEADS, s, HEAD_DIM), F32)] * 3,
        scratch_shapes=[pltpu.VMEM((s, HEAD_DIM), F32), pltpu.VMEM((s, HEAD_DIM), F32)],
        compiler_params=_params(("arbitrary", "arbitrary")),
    )(qkv, qkv, qkv, do, o)


def _branch_inputs(refs, br):
    ya_ref, yb_ref, yc_ref, yd_ref = refs
    if br == 1:
        return yb_ref[...]
    return _heads_to_lanes((ya_ref, None, yc_ref, yd_ref)[br])


def outproj_fwd(x, ya, yb, yc, yd, gates, bg, wout):
    s = x.shape[0]
    tm = min(ROW_T, s)

    def body(x_ref, ya_ref, yb_ref, yc_ref, yd_ref, gates_ref, bg_ref, w_ref, out_ref):
        pieces = []
        for br in range(4):
            cols = slice(br * D_BRANCH, (br + 1) * D_BRANCH)
            y = _branch_inputs((ya_ref, yb_ref, yc_ref, yd_ref), br)
            r = lax.rsqrt(jnp.mean(y * y, axis=-1, keepdims=True) + EPS)
            gt = gates_ref[:, cols]
            pieces.append((y * r * bg_ref[:, cols]) * (gt * _sigmoid(gt)))
        merged = jnp.concatenate(pieces, axis=1).astype(BF16)
        out_ref[...] = x_ref[...] + _dot(merged, w_ref[...])

    head_spec = pl.BlockSpec((N_HEADS, tm, HEAD_DIM), lambda i: (0, i, 0))
    return pl.pallas_call(
        body, name="outproj_fwd", grid=(s // tm,),
        in_specs=[pl.BlockSpec((tm, D_MODEL), lambda i: (i, 0)),
                  head_spec, pl.BlockSpec((tm, D_BRANCH), lambda i: (i, 0)), head_spec, head_spec,
                  pl.BlockSpec((tm, D_MODEL), lambda i: (i, 0)),
                  pl.BlockSpec((1, D_MODEL), lambda i: (0, 0)),
                  pl.BlockSpec((D_MODEL, D_MODEL), lambda i: (0, 0))],
        out_specs=pl.BlockSpec((tm, D_MODEL), lambda i: (i, 0)),
        out_shape=jax.ShapeDtypeStruct((s, D_MODEL), F32),
        compiler_params=_params(("arbitrary",)),
    )(x, ya, yb, yc, yd, gates, bg, wout)


def outproj_bwd(dout, ya, yb, yc, yd, gates, bg, wout):
    s = dout.shape[0]
    tm = min(ROW_T, s)

    def body(dout_ref, ya_ref, yb_ref, yc_ref, yd_ref, gates_ref, bg_ref, w_ref,
             dya_ref, dyb_ref, dyc_ref, dyd_ref, dgates_ref, dbg_ref, dw_ref):
        i = pl.program_id(0)

        @pl.when(i == 0)
        def _():
            dbg_ref[...] = jnp.zeros_like(dbg_ref)
            dw_ref[...] = jnp.zeros_like(dw_ref)

        doutb = dout_ref[...].astype(BF16)
        dmerged = _dot_nt(doutb, w_ref[...])
        pieces = []
        for br in range(4):
            cols = slice(br * D_BRANCH, (br + 1) * D_BRANCH)
            y = _branch_inputs((ya_ref, yb_ref, yc_ref, yd_ref), br)
            r = lax.rsqrt(jnp.mean(y * y, axis=-1, keepdims=True) + EPS)
            yn = y * r
            bgv = bg_ref[:, cols]
            gt = gates_ref[:, cols]
            sig = _sigmoid(gt)
            act = gt * sig
            n = yn * bgv
            pieces.append(n * act)
            dm = dmerged[:, cols]
            dn = dm * act
            dgates_ref[:, cols] = (dm * n * (sig * (1.0 + gt * (1.0 - sig)))).astype(BF16)
            dbg_ref[:, cols] += jnp.sum(dn * yn, axis=0, keepdims=True)
            u = dn * bgv
            dy = r * (u - yn * jnp.mean(yn * u, axis=-1, keepdims=True))
            if br == 1:
                dyb_ref[...] = dy
            else:
                dref = (dya_ref, None, dyc_ref, dyd_ref)[br]
                for hh in range(N_HEADS):
                    dref[hh] = dy[:, hh * HEAD_DIM:(hh + 1) * HEAD_DIM].astype(BF16)
        merged = jnp.concatenate(pieces, axis=1).astype(BF16)
        dw_ref[...] += _dot_tn(merged, doutb)

    head_spec = pl.BlockSpec((N_HEADS, tm, HEAD_DIM), lambda i: (0, i, 0))
    head_shape = jax.ShapeDtypeStruct((N_HEADS, s, HEAD_DIM), BF16)
    return pl.pallas_call(
        body, name="outproj_bwd", grid=(s // tm,),
        in_specs=[pl.BlockSpec((tm, D_MODEL), lambda i: (i, 0)),
                  head_spec, pl.BlockSpec((tm, D_BRANCH), lambda i: (i, 0)), head_spec, head_spec,
                  pl.BlockSpec((tm, D_MODEL), lambda i: (i, 0)),
                  pl.BlockSpec((1, D_MODEL), lambda i: (0, 0)),
                  pl.BlockSpec((D_MODEL, D_MODEL), lambda i: (0, 0))],
        out_specs=[head_spec, pl.BlockSpec((tm, D_BRANCH), lambda i: (i, 0)), head_spec, head_spec,
                   pl.BlockSpec((tm, D_MODEL), lambda i: (i, 0)),
                   pl.BlockSpec((1, D_MODEL), lambda i: (0, 0)),
                   pl.BlockSpec((D_MODEL, D_MODEL), lambda i: (0, 0))],
        out_shape=[head_shape, jax.ShapeDtypeStruct((s, D_BRANCH), F32), head_shape, head_shape,
                   jax.ShapeDtypeStruct((s, D_MODEL), BF16),
                   jax.ShapeDtypeStruct((1, D_MODEL), F32),
                   jax.ShapeDtypeStruct((D_MODEL, D_MODEL), F32)],
        compiler_params=_params(("arbitrary",)),
    )(dout, ya, yb, yc, yd, gates, bg, wout)


def final_loss(x, tgt, g):
    s = x.shape[0]
    tm = min(ROW_T, s)

    def body(x_ref, t_ref, g_ref, loss_ref, dx_ref, dg_ref):
        i = pl.program_id(0)

        @pl.when(i == 0)
        def _():
            loss_ref[...] = jnp.zeros_like(loss_ref)
            dg_ref[...] = jnp.zeros_like(dg_ref)

        xv = x_ref[...]
        gv = g_ref[...]
        r = lax.rsqrt(jnp.mean(xv * xv, axis=-1, keepdims=True) + EPS)
        xn = xv * r
        err = xn * gv - t_ref[...]
        loss_ref[...] += jnp.sum(err * err) * (0.5 / D_MODEL)
        dy = err * (1.0 / D_MODEL)
        u = dy * gv
        dx_ref[...] = r * (u - xn * jnp.mean(xn * u, axis=-1, keepdims=True))
        dg_ref[...] += jnp.sum(dy * xn, axis=0, keepdims=True)

    return pl.pallas_call(
        body, name="final_loss", grid=(s // tm,),
        in_specs=[pl.BlockSpec((tm, D_MODEL), lambda i: (i, 0)),
                  pl.BlockSpec((tm, D_MODEL), lambda i: (i, 0)),
                  pl.BlockSpec((1, D_MODEL), lambda i: (0, 0))],
        out_specs=[pl.BlockSpec((1, 128), lambda i: (0, 0)),
                   pl.BlockSpec((tm, D_MODEL), lambda i: (i, 0)),
                   pl.BlockSpec((1, D_MODEL), lambda i: (0, 0))],
        out_shape=[jax.ShapeDtypeStruct((1, 128), F32),
                   jax.ShapeDtypeStruct((s, D_MODEL), F32),
                   jax.ShapeDtypeStruct((1, D_MODEL), F32)],
        compiler_params=_params(("arbitrary",)),
    )(x, tgt, g)


def _rel_index():
    i = np.arange(A_TQ)[:, None]
    j = np.arange(A_BAND)[None, :]
    rel = np.clip(i - j + (A_BAND - A_TQ), -MAX_REL, MAX_REL) + MAX_REL
    dchunk = i // CHUNK + LOOKBACK - j // CHUNK
    valid = (dchunk >= 0) & (dchunk <= LOOKBACK)
    return jnp.asarray(np.where(valid, rel, -1).astype(np.int32))


def _layer_consts(p):
    tbias = relbias_tile(p["rel_bias"], _rel_index())
    return dict(
        norm_g=p["norm_g"].reshape(1, D_MODEL),
        v_gain=p["v_gain"].reshape(1, D_BRANCH),
        b_col=p["b_s"].reshape(N_HEADS, SG_CHUNK, 1),
        bg=p["branch_gain"].reshape(1, D_MODEL),
        tbias=tbias,
    )


def _gate_layout(fp, b_f, s):
    nb = s // 128
    ft = fp[:, :N_HEADS].T.reshape(N_HEADS * nb, 128)
    bcol = jnp.repeat(b_f, nb).reshape(N_HEADS * nb, 1)
    return ft, bcol


def layer_fwd(x, p):
    s = x.shape[0]
    c = _layer_consts(p)
    h, qkv, gates, uv, fp = inproj_fwd(x, c["norm_g"], p["wp"])
    keep = A_BAND - A_TQ
    kpad = jnp.pad(qkv[1], ((0, 0), (keep, 0), (0, 0)))
    vpad = jnp.pad(qkv[2], ((0, 0), (keep, 0), (0, 0)))
    ya, lse_a = mix_a_fwd(qkv, kpad, vpad, c["tbias"])
    yb = mix_b_fwd(uv, c["v_gain"], p["w_s"], c["b_col"])
    ft, bcol = _gate_layout(fp, p["b_f"], s)
    cum = fox_gate_fwd(ft, bcol).reshape(N_HEADS, s)
    c_col = cum.reshape(N_HEADS, s, 1)
    c_row = cum.reshape(N_HEADS, s // ATT_T, 1, ATT_T)
    yc, lse_c = fox_fwd(qkv, c_col, c_row)
    yd = sb_fwd(qkv)
    out = outproj_fwd(x, ya, yb, yc, yd, gates, c["bg"], p["wout"])
    saved = dict(consts=c, x=x, h=h, qkv=qkv, gates=gates, uv=uv, kpad=kpad, vpad=vpad, ft=ft, bcol=bcol,
                 c_col=c_col, c_row=c_row, ya=ya, lse_a=lse_a, yb=yb, yc=yc, lse_c=lse_c, yd=yd)
    return out, saved


def layer_bwd(dout, p, sv):
    s = dout.shape[0]
    c = sv["consts"]
    dya, dyb, dyc, dyd, dgates, dbg, dwout = outproj_bwd(
        dout, sv["ya"], sv["yb"], sv["yc"], sv["yd"], sv["gates"], c["bg"], p["wout"])
    keep = A_BAND - A_TQ
    dqa, dkpad, dvpad, dt = mix_a_bwd(sv["qkv"], sv["kpad"], sv["vpad"], c["tbias"], dya, sv["ya"], sv["lse_a"])
    dka, dva = dkpad[:, keep:], dvpad[:, keep:]
    drel = relbias_grad(dt, _rel_index())[:N_HEADS, :2 * MAX_REL + 1]
    duv, dws, dbs, dvgain = mix_b_bwd(sv["uv"], c["v_gain"], p["w_s"], c["b_col"], dyb)
    dqc, dkc, dvc, dc = fox_bwd(sv["qkv"], sv["c_col"], sv["c_row"], dyc, sv["yc"], sv["lse_c"])
    dft, dbf = fox_gate_bwd(sv["ft"], sv["bcol"], dc.reshape(N_HEADS * (s // 128), 128))
    dfp = jnp.pad(dft.reshape(N_HEADS, s).T, ((0, 0), (0, 128 - N_HEADS)))
    dqd, dkd, dvd = sb_bwd(sv["qkv"], dyd, sv["yd"])
    dp, dx, dnorm = inproj_bwd((dqa, dka, dva, dqc, dkc, dvc, dqd, dkd, dvd), dgates, duv, dfp,
                               p["wp"], sv["x"], c["norm_g"], dout)
    dwp = weight_grad(sv["h"], dp, "inproj_wgrad")
    grads = dict(norm_g=dnorm.reshape(D_MODEL), wp=dwp, b_f=dbf[:N_HEADS, 0], rel_bias=drel,
                 w_s=dws, b_s=dbs.reshape(N_HEADS, SG_CHUNK), v_gain=dvgain.reshape(D_BRANCH),
                 branch_gain=dbg.reshape(4, D_BRANCH), wout=dwout)
    return dx, grads


def local_step(x, tgt, layers, final_g):
    saved = []
    cur = x
    for p in layers:
        cur, sv = layer_fwd(cur, p)
        saved.append(sv)
    loss, dcur, dfinal = final_loss(cur, tgt, final_g.reshape(1, D_MODEL))
    grads = [None] * len(layers)
    for l in reversed(range(len(layers))):
        dcur, grads[l] = layer_bwd(dcur, layers[l], saved[l])
    return loss[0, 0], dcur, grads, dfinal.reshape(D_MODEL)


def gather_weights(wb, wf):
    def body(wb_ref, wf_ref, ob_ref, of_ref, send_sems, recv_sems, loc_sems):
        x, y, c = lax.axis_index("x"), lax.axis_index("y"), lax.axis_index("c")
        me = 2 * x + y
        chips = [(1 - x, y), (x, 1 - y), (1 - x, 1 - y)]
        pairs = [(wb_ref, ob_ref), (wf_ref, of_ref)]
        local = [pltpu.make_async_copy(src, dst.at[me], loc_sems.at[n]) for n, (src, dst) in enumerate(pairs)]
        for cp in local:
            cp.start()

        def copy(j, n, slot):
            src, dst = pairs[n]
            return pltpu.make_async_remote_copy(
                src_ref=src, dst_ref=dst.at[slot], send_sem=send_sems.at[2 * j + n], recv_sem=recv_sems.at[2 * j + n],
                device_id=(chips[j][0], chips[j][1], c), device_id_type=MESH)

        sends = [copy(j, n, me) for j in range(3) for n in range(2)]
        for cp in sends:
            cp.start()
        for j in range(3):
            for n in range(2):
                copy(j, n, 2 * chips[j][0] + chips[j][1]).wait_recv()
        for cp in sends:
            cp.wait_send()
        for cp in local:
            cp.wait()

    any_spec = pl.BlockSpec(memory_space=pl.ANY)
    return pl.pallas_call(
        body, name="gather_weights",
        in_specs=[any_spec, any_spec], out_specs=[any_spec, any_spec],
        out_shape=[jax.ShapeDtypeStruct((4,) + wb.shape, wb.dtype), jax.ShapeDtypeStruct((4,) + wf.shape, wf.dtype)],
        scratch_shapes=[pltpu.SemaphoreType.DMA((6,)), pltpu.SemaphoreType.DMA((6,)), pltpu.SemaphoreType.DMA((2,))],
    )(wb, wf)


def exchange_grads(big, small):
    def body(b_ref, s_ref, rb_ref, rs_ref, send_sems, recv_sems, loc_sems):
        x, y, c = lax.axis_index("x"), lax.axis_index("y"), lax.axis_index("c")
        me_chip = 2 * x + y
        me = 4 * x + 2 * y + c
        peers = [(x, y, 1 - c)]
        for px, py in [(1 - x, y), (x, 1 - y), (1 - x, 1 - y)]:
            peers += [(px, py, c), (px, py, 1 - c)]
        local = [pltpu.make_async_copy(b_ref.at[me_chip], rb_ref.at[me], loc_sems.at[0]),
                 pltpu.make_async_copy(s_ref, rs_ref.at[me], loc_sems.at[1])]
        for cp in local:
            cp.start()

        def copies(n, chip, slot):
            kw = dict(device_id=peers[n], device_id_type=MESH)
            return [pltpu.make_async_remote_copy(src_ref=b_ref.at[chip], dst_ref=rb_ref.at[slot],
                                                 send_sem=send_sems.at[2 * n], recv_sem=recv_sems.at[2 * n], **kw),
                    pltpu.make_async_remote_copy(src_ref=s_ref, dst_ref=rs_ref.at[slot],
                                                 send_sem=send_sems.at[2 * n + 1], recv_sem=recv_sems.at[2 * n + 1], **kw)]

        sends = [cp for n, (px, py, _) in enumerate(peers) for cp in copies(n, 2 * px + py, me)]
        for cp in sends:
            cp.start()
        for n, (px, py, pc) in enumerate(peers):
            for cp in copies(n, me_chip, 4 * px + 2 * py + pc):
                cp.wait_recv()
        for cp in sends:
            cp.wait_send()
        for cp in local:
            cp.wait()

    any_spec = pl.BlockSpec(memory_space=pl.ANY)
    return pl.pallas_call(
        body, name="exchange_grads",
        in_specs=[any_spec, any_spec], out_specs=[any_spec, any_spec],
        out_shape=[jax.ShapeDtypeStruct((8,) + big.shape[1:], big.dtype),
                   jax.ShapeDtypeStruct((8,) + small.shape, small.dtype)],
        scratch_shapes=[pltpu.SemaphoreType.DMA((14,)), pltpu.SemaphoreType.DMA((14,)), pltpu.SemaphoreType.DMA((2,))],
    )(big, small)


def adamw_reduce(parts, w, m, v, name):
    rows = w.shape[0]
    tr = PACK_ROW_TILE
    c1 = 1.0 - ADAM_B1 ** ADAM_STEP
    c2 = 1.0 - ADAM_B2 ** ADAM_STEP

    def body(p_ref, w_ref, m_ref, v_ref, g_ref, d_ref, nm_ref, nv_ref):
        g = p_ref[0].astype(F32)
        for n in range(1, 8):
            g = g + p_ref[n].astype(F32)
        g_ref[...] = g
        nm = ADAM_B1 * m_ref[...] + (1.0 - ADAM_B1) * g
        nv = ADAM_B2 * v_ref[...] + (1.0 - ADAM_B2) * (g * g)
        nm_ref[...] = nm
        nv_ref[...] = nv
        d_ref[...] = -ADAM_LR * ((nm / c1) / (jnp.sqrt(nv / c2) + ADAM_EPS) + ADAM_WD * w_ref[...])

    spec = pl.BlockSpec((tr, 128), lambda i: (i, 0))
    shape = jax.ShapeDtypeStruct((rows, 128), F32)
    return pl.pallas_call(
        body, name=name, grid=(rows // tr,),
        in_specs=[pl.BlockSpec((8, tr, 128), lambda i: (0, i, 0)), spec, spec, spec],
        out_specs=[spec] * 4, out_shape=[shape] * 4,
        compiler_params=_params(("arbitrary",)),
    )(parts, w, m, v)


SHARDED = ("w_in", "w_out", "branch_gain")
SMALL = ("norm_g", "b_f", "rel_bias", "w_s", "b_s", "v_gain", "final_g")
WEIGHTS = ("norm_g", "w_in", "b_f", "rel_bias", "w_s", "b_s", "v_gain", "branch_gain", "w_out", "final_g")
PACK_ROW_TILE = 512


def _rows_of(shape):
    return -(-int(np.prod(shape)) // 128)


def _pack(leaves):
    parts = []
    for a in leaves:
        flat = a.reshape(-1).astype(F32)
        parts.append(jnp.pad(flat, (0, _rows_of(a.shape) * 128 - flat.shape[0])))
    flat = jnp.concatenate(parts)
    rows = flat.shape[0] // 128
    total = -(-rows // PACK_ROW_TILE) * PACK_ROW_TILE
    return jnp.pad(flat, (0, (total - rows) * 128)).reshape(total, 128)


def _unpack(slab, shapes):
    out, row = [], 0
    for shp in shapes:
        n = int(np.prod(shp))
        r = _rows_of(shp)
        out.append(slab[row:row + r].reshape(-1)[:n].reshape(shp))
        row += r
    return out


def _pack_w_in(w):
    return jnp.concatenate([w[:, :2816], w[:, 2820:], w[:, 2816:2820],
                            jnp.zeros((w.shape[0], N_PACK - N_IN), w.dtype)], axis=1)


def _unpack_w_in(wp):
    return jnp.concatenate([wp[:, :2816], wp[:, F_COL:F_COL + N_HEADS], wp[:, 2816:F_COL]], axis=1)


def kernel(x, norm_g, w_in, b_f, rel_bias, w_s, b_s, v_gain, branch_gain, w_out, final_g, loss_target, m_norm_g, m_w_in, m_b_f, m_rel_bias, m_w_s, m_b_s, m_v_gain, m_branch_gain, m_w_out, m_final_g, v_norm_g, v_w_in, v_b_f, v_rel_bias, v_w_s, v_b_s, v_v_gain, v_branch_gain, v_w_out, v_final_g):
    depth = norm_g.shape[0]
    weights = dict(norm_g=norm_g, w_in=w_in, b_f=b_f, rel_bias=rel_bias, w_s=w_s, b_s=b_s, v_gain=v_gain,
                   branch_gain=branch_gain, w_out=w_out, final_g=final_g)
    mom1 = dict(norm_g=m_norm_g, w_in=m_w_in, b_f=m_b_f, rel_bias=m_rel_bias, w_s=m_w_s, b_s=m_b_s,
                v_gain=m_v_gain, branch_gain=m_branch_gain, w_out=m_w_out, final_g=m_final_g)
    mom2 = dict(norm_g=v_norm_g, w_in=v_w_in, b_f=v_b_f, rel_bias=v_rel_bias, w_s=v_w_s, b_s=v_b_s,
                v_gain=v_v_gain, branch_gain=v_branch_gain, w_out=v_w_out, final_g=v_final_g)

    n_in_rows = _rows_of(w_in.shape)
    n_out_rows = _rows_of(w_out.shape)
    wb = jnp.concatenate([w_in.astype(BF16).reshape(n_in_rows, 128), w_out.astype(BF16).reshape(n_out_rows, 128)])
    wf = jnp.pad(branch_gain.reshape(-1), (0, 8 * 128 - branch_gain.size)).reshape(8, 128)
    gb, gf = gather_weights(wb, wf)
    w_in_full = gb[:, :n_in_rows].reshape((4,) + w_in.shape)
    w_in_full = jnp.moveaxis(w_in_full, 0, 2).reshape(depth, D_MODEL, N_IN)
    w_out_full = gb[:, n_in_rows:].reshape((4,) + w_out.shape)
    w_out_full = jnp.moveaxis(w_out_full, 0, 1).reshape(depth, D_MODEL, D_MODEL)
    bg_full = gf.reshape(4, -1)[:, :branch_gain.size].reshape((4,) + branch_gain.shape)
    bg_full = jnp.moveaxis(bg_full, 0, 2).reshape(depth, 4, D_BRANCH)

    layers = [dict(norm_g=norm_g[l], wp=_pack_w_in(w_in_full[l]), b_f=b_f[l], rel_bias=rel_bias[l], w_s=w_s[l],
                   b_s=b_s[l], v_gain=v_gain[l], branch_gain=bg_full[l], wout=w_out_full[l]) for l in range(depth)]

    loss_part, grad_x, lgrads, dfinal = local_step(x[0], loss_target[0], layers, final_g)
    loss = lax.psum(loss_part, ("x", "y", "c"))

    stack = lambda k: jnp.stack([g[k] for g in lgrads])
    d_w_in = jnp.stack([_unpack_w_in(g["wp"]) for g in lgrads])
    d_w_out = stack("wout")
    d_bg = stack("branch_gain")
    small = dict(norm_g=stack("norm_g"), b_f=stack("b_f"), rel_bias=stack("rel_bias"), w_s=stack("w_s"),
                 b_s=stack("b_s"), v_gain=stack("v_gain"), final_g=dfinal)
    slabs = []
    for sidx in range(4):
        slabs.append(_pack([d_w_in[:, :, sidx * N_SHARD:(sidx + 1) * N_SHARD],
                            d_w_out[:, sidx * D_BRANCH:(sidx + 1) * D_BRANCH, :],
                            d_bg[:, :, sidx * HEAD_DIM:(sidx + 1) * HEAD_DIM]]).astype(BF16))
    big_parts, small_parts = exchange_grads(jnp.stack(slabs), _pack([small[k] for k in SMALL]))

    outs = {}
    for names, parts, name in ((SHARDED, big_parts, "adamw_big"), (SMALL, small_parts, "adamw_small")):
        pack_local = lambda d: _pack([d[k] for k in names])
        slabs = adamw_reduce(parts, pack_local(weights), pack_local(mom1), pack_local(mom2), name)
        shapes = [weights[k].shape for k in names]
        for tag, slab in zip(("grad", "delta", "new_m", "new_v"), slabs):
            for k, a in zip(names, _unpack(slab, shapes)):
                outs[tag, k] = a
    result = [loss, grad_x[None]]
    for tag in ("grad", "delta", "new_m", "new_v"):
        result += [outs[tag, k] for k in WEIGHTS]
    return tuple(result)
```

```python
import functools

import jax
import jax.numpy as jnp
import numpy as np
from jax import lax
from jax.experimental import pallas as pl
from jax.experimental.pallas import tpu as pltpu

F32 = jnp.float32
BF16 = jnp.bfloat16
MESH = pl.DeviceIdType.MESH

D_MODEL = 1024
D_BRANCH = 256
N_HEADS = 4
HEAD_DIM = 64
CHUNK = 64
LOOKBACK = 8
MAX_REL = 128
SG_CHUNK = 128
EPS = 1e-6
N_IN = 3844
N_PACK = 3968
F_COL = 3840
N_SHARD = 961
NEG = -1e30

A_TQ = 128
A_BAND = A_TQ + LOOKBACK * CHUNK
A_QB = 512
ATT_T = 256
FOX_MID = 4
FOX_WIDE = 8
FOX_DEAD2 = -160.0
LOG2E = 1.4426950408889634
SB_DEAD = -110.0
ROW_T = 512
VMEM_LIMIT = 56 * 1024 * 1024

ADAM_LR = 0.001
ADAM_B1 = 0.9
ADAM_B2 = 0.999
ADAM_EPS = 1e-08
ADAM_WD = 0.01
ADAM_STEP = 10

SEC_A_Q, SEC_A_K, SEC_A_V, SEC_A_G = 0, 256, 512, 768
SEC_B_U, SEC_B_V, SEC_B_G = 1024, 1280, 1536
SEC_C_Q, SEC_C_K, SEC_C_V, SEC_C_G = 1792, 2048, 2304, 2560
SEC_D_Q, SEC_D_K, SEC_D_V, SEC_D_G = 2816, 3072, 3328, 3584
QKV_SECS = (SEC_A_Q, SEC_A_K, SEC_A_V, SEC_C_Q, SEC_C_K, SEC_C_V, SEC_D_Q, SEC_D_K, SEC_D_V)
GATE_SECS = (SEC_A_G, SEC_B_G, SEC_C_G, SEC_D_G)


def _dot(a, b):
    return jnp.dot(a, b, preferred_element_type=F32)


def _dot_nt(a, b):
    return lax.dot_general(a, b, (((1,), (1,)), ((), ())), preferred_element_type=F32)


def _dot_tn(a, b):
    return lax.dot_general(a, b, (((0,), (0,)), ((), ())), preferred_element_type=F32)


def _split2(x):
    hi = x.astype(BF16)
    lo = (x - hi.astype(F32)).astype(BF16)
    return hi, lo


def _split3(x):
    hi = x.astype(BF16)
    r = x - hi.astype(F32)
    mid = r.astype(BF16)
    lo = (r - mid.astype(F32)).astype(BF16)
    return hi, mid, lo


def _sigmoid(x):
    return 1.0 / (1.0 + jnp.exp(-x))


def _params(sem=None, vmem=VMEM_LIMIT):
    return pltpu.CompilerParams(dimension_semantics=sem, vmem_limit_bytes=vmem)


def _heads_to_lanes(ref):
    return jnp.concatenate([ref[h] for h in range(N_HEADS)], axis=1)


def inproj_fwd(x, g, wp):
    s = x.shape[0]
    tm = min(ROW_T, s)

    def body(x_ref, g_ref, w_ref, h_ref, qkv_ref, gates_ref, uv_ref, f_ref):
        xv = x_ref[...]
        r = lax.rsqrt(jnp.mean(xv * xv, axis=-1, keepdims=True) + EPS)
        h = (xv * r * g_ref[...]).astype(BF16)
        h_ref[...] = h
        for n, off in enumerate(QKV_SECS):
            p = _dot(h, w_ref[:, off:off + D_BRANCH])
            for hh in range(N_HEADS):
                qkv_ref[n, hh] = p[:, hh * HEAD_DIM:(hh + 1) * HEAD_DIM].astype(BF16)
        for n, off in enumerate(GATE_SECS):
            gates_ref[:, n * D_BRANCH:(n + 1) * D_BRANCH] = _dot(h, w_ref[:, off:off + D_BRANCH])
        uv_ref[...] = _dot(h, w_ref[:, SEC_B_U:SEC_B_U + 2 * D_BRANCH])
        f_ref[...] = _dot(h, w_ref[:, F_COL:F_COL + 128])

    return pl.pallas_call(
        body, name="inproj_fwd", grid=(s // tm,),
        in_specs=[pl.BlockSpec((tm, D_MODEL), lambda i: (i, 0)),
                  pl.BlockSpec((1, D_MODEL), lambda i: (0, 0)),
                  pl.BlockSpec((D_MODEL, N_PACK), lambda i: (0, 0))],
        out_specs=[pl.BlockSpec((tm, D_MODEL), lambda i: (i, 0)),
                   pl.BlockSpec((9, N_HEADS, tm, HEAD_DIM), lambda i: (0, 0, i, 0)),
                   pl.BlockSpec((tm, D_MODEL), lambda i: (i, 0)),
                   pl.BlockSpec((tm, 2 * D_BRANCH), lambda i: (i, 0)),
                   pl.BlockSpec((tm, 128), lambda i: (i, 0))],
        out_shape=[jax.ShapeDtypeStruct((s, D_MODEL), BF16),
                   jax.ShapeDtypeStruct((9, N_HEADS, s, HEAD_DIM), BF16),
                   jax.ShapeDtypeStruct((s, D_MODEL), F32),
                   jax.ShapeDtypeStruct((s, 2 * D_BRANCH), F32),
                   jax.ShapeDtypeStruct((s, 128), F32)],
        compiler_params=_params(("arbitrary",)),
    )(x, g, wp)


def inproj_bwd(dqkv, dgates, duv, dfp, wp, x, g, dres):
    s = x.shape[0]
    tm = min(ROW_T, s)

    def body(*refs):
        dq_refs = refs[:9]
        dgates_ref, duv_ref, dfp_ref, w_ref, x_ref, g_ref, dres_ref, dp_ref, dx_ref, dg_ref = refs[9:]
        i = pl.program_id(0)
        a_q, a_k, a_v, c_q, c_k, c_v, d_q, d_k, d_v = [_heads_to_lanes(r).astype(BF16) for r in dq_refs]
        dgt = dgates_ref[...]
        duv_b = duv_ref[...].astype(BF16)
        dp = jnp.concatenate(
            [a_q, a_k, a_v, dgt[:, 0:256], duv_b, dgt[:, 256:512], c_q, c_k, c_v, dgt[:, 512:768],
             d_q, d_k, d_v, dgt[:, 768:1024], dfp_ref[...].astype(BF16)], axis=1)
        dp_ref[...] = dp
        dh = _dot_nt(dp, w_ref[...])
        xv = x_ref[...]
        r = lax.rsqrt(jnp.mean(xv * xv, axis=-1, keepdims=True) + EPS)
        xn = xv * r
        u = dh * g_ref[...]
        dx_ref[...] = dres_ref[...] + r * (u - xn * jnp.mean(xn * u, axis=-1, keepdims=True))

        @pl.when(i == 0)
        def _():
            dg_ref[...] = jnp.zeros_like(dg_ref)

        dg_ref[...] += jnp.sum(dh * xn, axis=0, keepdims=True)

    head_spec = pl.BlockSpec((N_HEADS, tm, HEAD_DIM), lambda i: (0, i, 0))
    return pl.pallas_call(
        body, name="inproj_bwd", grid=(s // tm,),
        in_specs=[head_spec] * 9 + [
            pl.BlockSpec((tm, D_MODEL), lambda i: (i, 0)),
            pl.BlockSpec((tm, 2 * D_BRANCH), lambda i: (i, 0)),
            pl.BlockSpec((tm, 128), lambda i: (i, 0)),
            pl.BlockSpec((D_MODEL, N_PACK), lambda i: (0, 0)),
            pl.BlockSpec((tm, D_MODEL), lambda i: (i, 0)),
            pl.BlockSpec((1, D_MODEL), lambda i: (0, 0)),
            pl.BlockSpec((tm, D_MODEL), lambda i: (i, 0))],
        out_specs=[pl.BlockSpec((tm, N_PACK), lambda i: (i, 0)),
                   pl.BlockSpec((tm, D_MODEL), lambda i: (i, 0)),
                   pl.BlockSpec((1, D_MODEL), lambda i: (0, 0))],
        out_shape=[jax.ShapeDtypeStruct((s, N_PACK), BF16),
                   jax.ShapeDtypeStruct((s, D_MODEL), F32),
                   jax.ShapeDtypeStruct((1, D_MODEL), F32)],
        compiler_params=_params(("arbitrary",)),
    )(*dqkv, dgates, duv, dfp, wp, x, g, dres)


def weight_grad(a, b, name):
    s, m = a.shape
    n = b.shape[1]
    tm = min(ROW_T, s)
    tmm = 256
    nsteps = s // tm

    def body(a_ref, b_ref, o_ref):
        k = pl.program_id(1)

        @pl.when(k == 0)
        def _():
            o_ref[...] = jnp.zeros_like(o_ref)

        o_ref[...] += _dot_tn(a_ref[...], b_ref[...])

    return pl.pallas_call(
        body, name=name, grid=(m // tmm, nsteps),
        in_specs=[pl.BlockSpec((tm, tmm), lambda j, k: (k, j)),
                  pl.BlockSpec((tm, n), lambda j, k: (k, 0))],
        out_specs=pl.BlockSpec((tmm, n), lambda j, k: (j, 0)),
        out_shape=jax.ShapeDtypeStruct((m, n), F32),
        compiler_params=_params(("arbitrary", "arbitrary")),
    )(a, b)


def _a_specs(s):
    nq = s // A_QB
    q_spec = pl.BlockSpec((None, None, A_QB, HEAD_DIM), lambda h, i: (0, h, jnp.minimum(i, nq - 1), 0))
    kv_specs = [pl.BlockSpec((None, A_QB, HEAD_DIM), lambda h, i, m=m: (h, jnp.minimum(i + m, nq), 0)) for m in range(2)]
    t_spec = pl.BlockSpec((None, A_TQ, A_BAND), lambda h, i: (h, 0, 0))
    return nq, q_spec, kv_specs, t_spec


def _a_scores(q_ref, k, t_ref, i, j):
    rows = slice(j * A_TQ, (j + 1) * A_TQ)
    qs = q_ref[rows, :] * 0.125
    kj = k[j * A_TQ:j * A_TQ + A_BAND, :]
    sc = _dot_nt(qs, kj) + t_ref[...]
    col = lax.broadcasted_iota(jnp.int32, (A_TQ, A_BAND), 1)
    sc = jnp.where(col >= (A_BAND - A_TQ) - i * A_QB - j * A_TQ, sc, NEG)
    return rows, qs, kj, sc


def mix_a_fwd(qkv, kpad, vpad, tbias):
    s = qkv.shape[2]
    nq, q_spec, kv_specs, t_spec = _a_specs(s)

    def body(q_ref, k0_ref, k1_ref, v0_ref, v1_ref, t_ref, o_ref, lse_ref):
        i = pl.program_id(1)
        k = jnp.concatenate([k0_ref[...], k1_ref[...]], axis=0)
        v = jnp.concatenate([v0_ref[...], v1_ref[...]], axis=0)
        for j in range(A_QB // A_TQ):
            rows, _, _, sc = _a_scores(q_ref, k, t_ref, i, j)
            m = jnp.max(sc, axis=-1, keepdims=True)
            p = jnp.exp(sc - m)
            l = jnp.sum(p, axis=-1, keepdims=True)
            o_ref[rows, :] = _dot(p.astype(BF16), v[j * A_TQ:j * A_TQ + A_BAND, :]) / l
            lse_ref[rows, :] = m + jnp.log(l)

    return pl.pallas_call(
        body, name="mix_a_fwd", grid=(N_HEADS, nq),
        in_specs=[q_spec] + kv_specs + kv_specs + [t_spec],
        out_specs=[pl.BlockSpec((None, A_QB, HEAD_DIM), lambda h, i: (h, i, 0)),
                   pl.BlockSpec((None, A_QB, 1), lambda h, i: (h, i, 0))],
        out_shape=[jax.ShapeDtypeStruct((N_HEADS, s, HEAD_DIM), F32),
                   jax.ShapeDtypeStruct((N_HEADS, s, 1), F32)],
        compiler_params=_params(("arbitrary", "arbitrary")),
    )(qkv, kpad, kpad, vpad, vpad, tbias)


def mix_a_bwd(qkv, kpad, vpad, tbias, do, o, lse):
    s = qkv.shape[2]
    nq, q_spec, kv_specs, t_spec = _a_specs(s)
    row_spec = lambda w: pl.BlockSpec((None, A_QB, w), lambda h, i: (h, jnp.minimum(i, nq - 1), 0))
    done_spec = pl.BlockSpec((None, A_QB, HEAD_DIM), lambda h, i: (h, jnp.maximum(i - 1, 0), 0))
    win = 2 * A_QB

    def body(q_ref, k0_ref, k1_ref, v0_ref, v1_ref, t_ref, do_ref, o_ref, lse_ref,
             dq_ref, dk_ref, dv_ref, dt_ref, dk_win, dv_win):
        i = pl.program_id(1)

        @pl.when(i == 0)
        def _():
            dk_win[...] = jnp.zeros_like(dk_win)
            dv_win[...] = jnp.zeros_like(dv_win)
            dt_ref[...] = jnp.zeros_like(dt_ref)

        @pl.when(i < nq)
        def _():
            k = jnp.concatenate([k0_ref[...], k1_ref[...]], axis=0)
            v = jnp.concatenate([v0_ref[...], v1_ref[...]], axis=0)
            dt = jnp.zeros((A_TQ, A_BAND), F32)
            for j in range(A_QB // A_TQ):
                rows, qs, kj, sc = _a_scores(q_ref, k, t_ref, i, j)
                keys = slice(j * A_TQ, j * A_TQ + A_BAND)
                dob = do_ref[rows, :]
                p = jnp.exp(sc - lse_ref[rows, :])
                delta = jnp.sum(o_ref[rows, :] * dob.astype(F32), axis=-1, keepdims=True)
                ds = p * (_dot_nt(dob, v[keys, :]) - delta)
                dsb = ds.astype(BF16)
                dq_ref[rows, :] = _dot(dsb, kj) * 0.125
                dk_win[keys, :] += _dot_tn(dsb, qs)
                dv_win[keys, :] += _dot_tn(p.astype(BF16), dob)
                dt = dt + ds
            dt_ref[...] += dt

        dk_ref[...] = dk_win[0:A_QB, :]
        dv_ref[...] = dv_win[0:A_QB, :]
        dk_rest = dk_win[A_QB:win, :]
        dv_rest = dv_win[A_QB:win, :]
        dk_win[0:A_QB, :] = dk_rest
        dv_win[0:A_QB, :] = dv_rest
        dk_win[A_QB:win, :] = jnp.zeros((A_QB, HEAD_DIM), F32)
        dv_win[A_QB:win, :] = jnp.zeros((A_QB, HEAD_DIM), F32)

    return pl.pallas_call(
        body, name="mix_a_bwd", grid=(N_HEADS, nq + 1),
        in_specs=[q_spec] + kv_specs + kv_specs + [t_spec, row_spec(HEAD_DIM), row_spec(HEAD_DIM), row_spec(1)],
        out_specs=[row_spec(HEAD_DIM), done_spec, done_spec, t_spec],
        out_shape=[jax.ShapeDtypeStruct((N_HEADS, s, HEAD_DIM), F32),
                   jax.ShapeDtypeStruct((N_HEADS, s, HEAD_DIM), F32),
                   jax.ShapeDtypeStruct((N_HEADS, s, HEAD_DIM), F32),
                   jax.ShapeDtypeStruct((N_HEADS, A_TQ, A_BAND), F32)],
        scratch_shapes=[pltpu.VMEM((win, HEAD_DIM), F32), pltpu.VMEM((win, HEAD_DIM), F32)],
        compiler_params=_params(("arbitrary", "arbitrary")),
    )(qkv, kpad, kpad, vpad, vpad, tbias, do, o, lse)


def relbias_tile(rel_bias, relmat):
    nrel = 2 * MAX_REL + 1

    def body(rb_ref, rel_ref, o_ref):
        rel = rel_ref[...]
        o_ref[...] = jnp.full(o_ref.shape, NEG, F32)

        def step(r, carry):
            hit = rel == r
            for h in range(N_HEADS):
                o_ref[h] = jnp.where(hit, rb_ref[h, r], o_ref[h])
            return carry

        lax.fori_loop(0, nrel, step, 0)

    return pl.pallas_call(
        body, name="relbias_tile",
        in_specs=[pl.BlockSpec(memory_space=pltpu.SMEM), pl.BlockSpec(memory_space=pltpu.VMEM)],
        out_specs=pl.BlockSpec(memory_space=pltpu.VMEM),
        out_shape=jax.ShapeDtypeStruct((N_HEADS, A_TQ, A_BAND), F32),
        compiler_params=_params(),
    )(rel_bias, relmat)


def relbias_grad(dt, relmat):
    nrel = 2 * MAX_REL + 1

    def body(dt_ref, rel_ref, o_ref):
        rel = rel_ref[...]
        lane = lax.broadcasted_iota(jnp.int32, (8, 384), 1)
        row = lax.broadcasted_iota(jnp.int32, (8, 384), 0)

        def step(r, acc):
            hit = rel == r
            for h in range(N_HEADS):
                val = jnp.sum(jnp.where(hit, dt_ref[h], 0.0))
                acc = jnp.where((lane == r) & (row == h), val, acc)
            return acc

        o_ref[...] = lax.fori_loop(0, nrel, step, jnp.zeros((8, 384), F32))

    return pl.pallas_call(
        body, name="relbias_grad",
        out_shape=jax.ShapeDtypeStruct((8, 384), F32),
        compiler_params=_params(),
    )(dt, relmat)


def _b_norm(v, gain):
    mu = jnp.mean(v, axis=-1, keepdims=True)
    xc = v - mu
    rstd = lax.rsqrt(jnp.mean(xc * xc, axis=-1, keepdims=True) + EPS)
    xhat = xc * rstd
    return xhat, rstd, xhat * gain


def _tril_mask():
    t = lax.broadcasted_iota(jnp.int32, (SG_CHUNK, SG_CHUNK), 0)
    u = lax.broadcasted_iota(jnp.int32, (SG_CHUNK, SG_CHUNK), 1)
    return u <= t


def mix_b_fwd(uv, gain, w_s, b_col):
    s = uv.shape[0]
    tm = min(ROW_T, s)

    def body(uv_ref, gain_ref, w_ref, b_ref, y_ref):
        tril = _tril_mask()
        ws = [jnp.where(tril, w_ref[g], 0.0).astype(BF16) for g in range(N_HEADS)]
        for c in range(tm // SG_CHUNK):
            rows = slice(c * SG_CHUNK, (c + 1) * SG_CHUNK)
            u = uv_ref[rows, 0:D_BRANCH]
            _, _, vn = _b_norm(uv_ref[rows, D_BRANCH:2 * D_BRANCH], gain_ref[...])
            vnb = vn.astype(BF16)
            outs = []
            for g in range(N_HEADS):
                cols = slice(g * HEAD_DIM, (g + 1) * HEAD_DIM)
                mixed = _dot(ws[g], vnb[:, cols]) + b_ref[g]
                outs.append(u[:, cols] * mixed)
            y_ref[rows, :] = jnp.concatenate(outs, axis=1)

    return pl.pallas_call(
        body, name="mix_b_fwd", grid=(s // tm,),
        in_specs=[pl.BlockSpec((tm, 2 * D_BRANCH), lambda i: (i, 0)),
                  pl.BlockSpec((1, D_BRANCH), lambda i: (0, 0)),
                  pl.BlockSpec((N_HEADS, SG_CHUNK, SG_CHUNK), lambda i: (0, 0, 0)),
                  pl.BlockSpec((N_HEADS, SG_CHUNK, 1), lambda i: (0, 0, 0))],
        out_specs=pl.BlockSpec((tm, D_BRANCH), lambda i: (i, 0)),
        out_shape=jax.ShapeDtypeStruct((s, D_BRANCH), F32),
        compiler_params=_params(("arbitrary",)),
    )(uv, gain, w_s, b_col)


def mix_b_bwd(uv, gain, w_s, b_col, dy):
    s = uv.shape[0]
    tm = min(ROW_T, s)

    def body(uv_ref, gain_ref, w_ref, b_ref, dy_ref, duv_ref, dw_ref, db_ref, dgain_ref):
        i = pl.program_id(0)

        @pl.when(i == 0)
        def _():
            dw_ref[...] = jnp.zeros_like(dw_ref)
            db_ref[...] = jnp.zeros_like(db_ref)
            dgain_ref[...] = jnp.zeros_like(dgain_ref)

        tril = _tril_mask()
        ws = [jnp.where(tril, w_ref[g], 0.0).astype(BF16) for g in range(N_HEADS)]
        gain_v = gain_ref[...]
        for c in range(tm // SG_CHUNK):
            rows = slice(c * SG_CHUNK, (c + 1) * SG_CHUNK)
            u = uv_ref[rows, 0:D_BRANCH]
            xhat, rstd, vn = _b_norm(uv_ref[rows, D_BRANCH:2 * D_BRANCH], gain_v)
            vnb = vn.astype(BF16)
            dyv = dy_ref[rows, :]
            dus, dvns = [], []
            for g in range(N_HEADS):
                cols = slice(g * HEAD_DIM, (g + 1) * HEAD_DIM)
                mixed = _dot(ws[g], vnb[:, cols]) + b_ref[g]
                dus.append(dyv[:, cols] * mixed)
                dmixed = dyv[:, cols] * u[:, cols]
                dmb = dmixed.astype(BF16)
                db_ref[g] += jnp.sum(dmixed, axis=-1, keepdims=True)
                dw_ref[g] += jnp.where(tril, _dot_nt(dmb, vnb[:, cols]), 0.0)
                dvns.append(_dot_tn(ws[g], dmb))
            dvn = jnp.concatenate(dvns, axis=1)
            dgain_ref[...] += jnp.sum(dvn * xhat, axis=0, keepdims=True)
            dxh = dvn * gain_v
            dv = rstd * (dxh - jnp.mean(dxh, axis=-1, keepdims=True)
                         - xhat * jnp.mean(dxh * xhat, axis=-1, keepdims=True))
            duv_ref[rows, :] = jnp.concatenate(dus + [dv], axis=1)

    return pl.pallas_call(
        body, name="mix_b_bwd", grid=(s // tm,),
        in_specs=[pl.BlockSpec((tm, 2 * D_BRANCH), lambda i: (i, 0)),
                  pl.BlockSpec((1, D_BRANCH), lambda i: (0, 0)),
                  pl.BlockSpec((N_HEADS, SG_CHUNK, SG_CHUNK), lambda i: (0, 0, 0)),
                  pl.BlockSpec((N_HEADS, SG_CHUNK, 1), lambda i: (0, 0, 0)),
                  pl.BlockSpec((tm, D_BRANCH), lambda i: (i, 0))],
        out_specs=[pl.BlockSpec((tm, 2 * D_BRANCH), lambda i: (i, 0)),
                   pl.BlockSpec((N_HEADS, SG_CHUNK, SG_CHUNK), lambda i: (0, 0, 0)),
                   pl.BlockSpec((N_HEADS, SG_CHUNK, 1), lambda i: (0, 0, 0)),
                   pl.BlockSpec((1, D_BRANCH), lambda i: (0, 0))],
        out_shape=[jax.ShapeDtypeStruct((s, 2 * D_BRANCH), F32),
                   jax.ShapeDtypeStruct((N_HEADS, SG_CHUNK, SG_CHUNK), F32),
                   jax.ShapeDtypeStruct((N_HEADS, SG_CHUNK, 1), F32),
                   jax.ShapeDtypeStruct((1, D_BRANCH), F32)],
        compiler_params=_params(("arbitrary",)),
    )(uv, gain, w_s, b_col, dy)


def _scan_mats(nrow):
    a = lax.broadcasted_iota(jnp.int32, (128, 128), 0)
    b = lax.broadcasted_iota(jnp.int32, (128, 128), 1)
    r = lax.broadcasted_iota(jnp.int32, (nrow, nrow), 0)
    c = lax.broadcasted_iota(jnp.int32, (nrow, nrow), 1)
    nb = nrow // N_HEADS
    same = (r // nb) == (c // nb)
    return a, b, r, c, same


def _exact_dot(x, m):
    hi, mid, lo = _split3(x)
    return _dot(hi, m) + _dot(mid, m) + _dot(lo, m)


def _exact_dot_left(m, x):
    hi, mid, lo = _split3(x)
    return _dot(m, hi) + _dot(m, mid) + _dot(m, lo)


def fox_gate_fwd(ft, bcol):
    nrow = ft.shape[0]

    def body(f_ref, b_ref, c_ref):
        z = f_ref[...] + b_ref[...]
        ls = jnp.minimum(z, 0.0) - jnp.log(1.0 + jnp.exp(-jnp.abs(z)))
        a, b, r, c, same = _scan_mats(nrow)
        within = _exact_dot(ls, (a <= b).astype(BF16))
        tot = jnp.broadcast_to(within[:, 127:128], within.shape)
        before = _exact_dot_left((same & (c < r)).astype(BF16), tot)
        c_ref[...] = within + before

    return pl.pallas_call(
        body, name="fox_gate_fwd",
        out_shape=jax.ShapeDtypeStruct((nrow, 128), F32),
        compiler_params=_params(),
    )(ft, bcol)


def fox_gate_bwd(ft, bcol, dc):
    nrow = ft.shape[0]

    def body(f_ref, b_ref, dc_ref, df_ref, db_ref):
        a, b, r, c, same = _scan_mats(nrow)
        dcv = dc_ref[...]
        within = _exact_dot(dcv, (a >= b).astype(BF16))
        tot = jnp.broadcast_to(within[:, 0:1], within.shape)
        after = _exact_dot_left((same & (c > r)).astype(BF16), tot)
        dls = within + after
        z = f_ref[...] + b_ref[...]
        dz = dls * _sigmoid(-z)
        df_ref[...] = dz
        rs = jnp.broadcast_to(jnp.sum(dz, axis=-1, keepdims=True), dz.shape)
        hr = lax.broadcasted_iota(jnp.int32, (8, nrow), 0)
        hc = lax.broadcasted_iota(jnp.int32, (8, nrow), 1)
        db_ref[...] = _exact_dot_left((hr == hc // (nrow // N_HEADS)).astype(BF16), rs)

    return pl.pallas_call(
        body, name="fox_gate_bwd",
        out_shape=[jax.ShapeDtypeStruct((nrow, 128), F32), jax.ShapeDtypeStruct((8, 128), F32)],
        compiler_params=_params(),
    )(ft, bcol, dc)


def _att_specs(s, qi, ki, vi):
    t = ATT_T
    q_spec = pl.BlockSpec((None, None, t, HEAD_DIM), lambda h, i: (qi, h, i, 0))
    k_spec = pl.BlockSpec((None, None, s, HEAD_DIM), lambda h, i: (ki, h, 0, 0))
    v_spec = pl.BlockSpec((None, None, s, HEAD_DIM), lambda h, i: (vi, h, 0, 0))
    row_spec = lambda w: pl.BlockSpec((None, t, w), lambda h, i: (h, i, 0))
    return q_spec, k_spec, v_spec, row_spec


def _causal(strict):
    row = lax.broadcasted_iota(jnp.int32, (ATT_T, ATT_T), 0)
    col = lax.broadcasted_iota(jnp.int32, (ATT_T, ATT_T), 1)
    return (col < row) if strict else (col <= row)


def _gate_row(cr_ref, kb, g):
    if g == 1:
        return cr_ref[kb]
    return jnp.concatenate([cr_ref[kb + n] for n in range(g)], axis=1)


def _fox_walk(i, carry, tile, alive):
    g = FOX_WIDE
    nmid = i // FOX_MID
    nwide = i // g
    carry = tile(i, 1, carry, True)
    carry = lax.fori_loop(0, i - nmid * FOX_MID, lambda n, c: tile(i - 1 - n, 1, c, False), carry)
    carry = lax.fori_loop(0, nmid - nwide * (g // FOX_MID), lambda n, c: tile(nwide * g, FOX_MID, c, False), carry)

    def cond(state):
        return jnp.logical_and(state[0] >= 0, state[1] > 0)

    def step(state):
        n = state[0]
        c = tile(n * g, g, state[2:], False)
        return (n - 1, alive(n * g, c)) + tuple(c)

    out = lax.while_loop(cond, step, (nwide - 1, alive(nwide * g, carry)) + tuple(carry))
    return out[2:]


def _fox_reach(qs, k_ref, kmax_ref, cc, i):
    s = k_ref.shape[0]
    rows = 4 * ATT_T

    @pl.when(i == 0)
    def _():
        def chunk(n, mx):
            kc = k_ref[pl.ds(pl.multiple_of(n * rows, rows), rows), :].astype(F32)
            return jnp.maximum(mx, jnp.max(jnp.sum(kc * kc, axis=-1, keepdims=True)))

        kmax_ref[0] = jnp.sqrt(lax.fori_loop(0, s // rows, chunk, jnp.float32(0.0)))

    qf = qs.astype(F32)
    return jnp.sqrt(jnp.sum(qf * qf, axis=-1, keepdims=True)) * kmax_ref[0] + cc


def _gate_col(cr_ref, i):
    row = lax.broadcasted_iota(jnp.int32, (ATT_T, ATT_T), 0)
    col = lax.broadcasted_iota(jnp.int32, (ATT_T, ATT_T), 1)
    return jnp.sum(jnp.where(row == col, cr_ref[i], 0.0), axis=-1, keepdims=True)


def _fox_scores(qs, k, cc, crow, masked):
    sc = (_dot_nt(qs, k) + (cc - crow)) * LOG2E
    if masked:
        sc = jnp.where(_causal(False), sc, NEG)
    return sc


def fox_fwd(qkv, c_row):
    s = qkv.shape[2]
    t = ATT_T
    nq = s // t
    q_spec, k_spec, v_spec, row_spec = _att_specs(s, 3, 4, 5)
    rows = 4 * t

    def body(q_ref, k_ref, v_ref, cr_ref, o_ref, ref_ref, rl_ref, v1_ref, kmax_ref):
        i = pl.program_id(1)

        @pl.when(i == 0)
        def _():
            def chunk(n, carry):
                r0 = pl.multiple_of(n * rows, rows)
                v1_ref[pl.ds(r0, rows), :] = jnp.concatenate(
                    [v_ref[pl.ds(r0, rows), :], jnp.ones((rows, HEAD_DIM), BF16)], axis=1)
                return carry

            lax.fori_loop(0, s // rows, chunk, 0)

        qs = q_ref[...] * 0.125
        cc = _gate_col(cr_ref, i)
        reach = _fox_reach(qs, k_ref, kmax_ref, cc, i) * LOG2E

        def alive(kb, carry):
            return (jnp.max(reach - cr_ref[kb][:, 0:1] * LOG2E - carry[0]) > FOX_DEAD2).astype(jnp.int32)

        def tile(kb, g, carry, masked):
            m, acc = carry
            k0 = pl.multiple_of(kb * t, t)
            sc = _fox_scores(qs, k_ref[pl.ds(k0, g * t), :], cc, _gate_row(cr_ref, kb, g), masked)
            m_new = jnp.maximum(m, jnp.ceil(jnp.max(sc, axis=-1, keepdims=True)))
            pb = jnp.exp2(sc - m_new).astype(BF16)
            acc = jnp.exp2(m - m_new) * acc + _dot(pb, v1_ref[pl.ds(k0, g * t), :])
            return m_new, acc

        init = (jnp.full((t, 1), NEG, F32), jnp.zeros((t, 2 * HEAD_DIM), F32))
        m, acc = _fox_walk(i, init, tile, alive)
        rl = 1.0 / acc[:, HEAD_DIM:HEAD_DIM + 1]
        o_ref[...] = acc[:, 0:HEAD_DIM] * rl
        ref_ref[...] = m
        rl_ref[...] = rl

    return pl.pallas_call(
        body, name="fox_fwd", grid=(N_HEADS, nq),
        in_specs=[q_spec, k_spec, v_spec, pl.BlockSpec((None, nq, 1, t), lambda h, i: (h, 0, 0, 0))],
        out_specs=[row_spec(HEAD_DIM), row_spec(1), row_spec(1)],
        out_shape=[jax.ShapeDtypeStruct((N_HEADS, s, HEAD_DIM), F32),
                   jax.ShapeDtypeStruct((N_HEADS, s, 1), F32),
                   jax.ShapeDtypeStruct((N_HEADS, s, 1), F32)],
        scratch_shapes=[pltpu.VMEM((s, 2 * HEAD_DIM), BF16), pltpu.SMEM((1,), F32)],
        compiler_params=_params(("arbitrary", "arbitrary")),
    )(qkv, qkv, qkv, c_row)


def fox_bwd(qkv, c_row, do, o, ref, rl):
    s = qkv.shape[2]
    t = ATT_T
    nq = s // t
    q_spec, k_spec, v_spec, row_spec = _att_specs(s, 3, 4, 5)
    any_spec = pl.BlockSpec(memory_space=pl.ANY)

    def body(q_ref, k_ref, v_ref, cr_ref, do_ref, o_ref, ref_ref, rl_ref,
             dq_ref, dk_hbm, dv_hbm, dc_ref, dk_acc, dv_acc, kmax_ref):
        h = pl.program_id(0)
        i = pl.program_id(1)

        @pl.when(i == 0)
        def _():
            dk_acc[...] = jnp.zeros_like(dk_acc)
            dv_acc[...] = jnp.zeros_like(dv_acc)
            dc_ref[...] = jnp.zeros_like(dc_ref)

        qs = q_ref[...] * 0.125
        ref = ref_ref[...]
        rl = rl_ref[...]
        dob = (do_ref[...].astype(F32) * rl).astype(BF16)
        delta = jnp.sum(o_ref[...] * dob.astype(F32), axis=-1, keepdims=True)
        cc = _gate_col(cr_ref, i)
        margin = _fox_reach(qs, k_ref, kmax_ref, cc, i) * LOG2E - ref

        def alive(kb, carry):
            return (jnp.max(margin - cr_ref[kb][:, 0:1] * LOG2E) > FOX_DEAD2).astype(jnp.int32)

        def tile(kb, g, carry, masked):
            dq, = carry
            k0 = pl.multiple_of(kb * t, t)
            k = k_ref[pl.ds(k0, g * t), :]
            sc = _fox_scores(qs, k, cc, _gate_row(cr_ref, kb, g), masked)
            wb = jnp.exp2(sc - ref).astype(BF16)
            ds = wb.astype(F32) * (_dot_nt(dob, v_ref[pl.ds(k0, g * t), :]) - delta)
            dsb = ds.astype(BF16)
            dk_acc[pl.ds(k0, g * t), :] += _dot_tn(dsb, qs)
            dv_acc[pl.ds(k0, g * t), :] += _dot_tn(wb, dob)
            dcs = -jnp.sum(ds, axis=0, keepdims=True)
            for n in range(g):
                dc_ref[kb + n] += dcs[:, n * t:(n + 1) * t]
            return (dq + _dot(dsb, k),)

        dq, = _fox_walk(i, (jnp.zeros((t, HEAD_DIM), F32),), tile, alive)
        dq_ref[...] = dq * 0.125

        @pl.when(i == nq - 1)
        def _():
            pltpu.sync_copy(dk_acc, dk_hbm.at[h])
            pltpu.sync_copy(dv_acc, dv_hbm.at[h])

    return pl.pallas_call(
        body, name="fox_bwd", grid=(N_HEADS, nq),
        in_specs=[q_spec, k_spec, v_spec,
                  pl.BlockSpec((None, nq, 1, t), lambda h, i: (h, 0, 0, 0)),
                  row_spec(HEAD_DIM), row_spec(HEAD_DIM), row_spec(1), row_spec(1)],
        out_specs=[row_spec(HEAD_DIM), any_spec, any_spec,
                   pl.BlockSpec((None, nq, 1, t), lambda h, i: (h, 0, 0, 0))],
        out_shape=[jax.ShapeDtypeStruct((N_HEADS, s, HEAD_DIM), F32),
                   jax.ShapeDtypeStruct((N_HEADS, s, HEAD_DIM), F32),
                   jax.ShapeDtypeStruct((N_HEADS, s, HEAD_DIM), F32),
                   jax.ShapeDtypeStruct((N_HEADS, nq, 1, t), F32)],
        scratch_shapes=[pltpu.VMEM((s, HEAD_DIM), F32), pltpu.VMEM((s, HEAD_DIM), F32), pltpu.SMEM((1,), F32)],
        compiler_params=_params(("arbitrary", "arbitrary")),
    )(qkv, qkv, qkv, c_row, do, o, ref, rl)


def _sb_tile(qs, k, run, masked):
    z = _dot_nt(qs, k)
    sp = jnp.log(1.0 + jnp.exp(-jnp.abs(z)))
    ls = jnp.minimum(z, 0.0) - sp
    lm = -jnp.maximum(z, 0.0) - sp
    if masked:
        valid = _causal(True)
        lm = jnp.where(valid, lm, 0.0)
    row = lax.broadcasted_iota(jnp.int32, (ATT_T, ATT_T), 0)
    col = lax.broadcasted_iota(jnp.int32, (ATT_T, ATT_T), 1)
    later = (row > col).astype(BF16)
    hi, lo = _split2(lm)
    between = run + _dot(hi, later) + _dot(lo, later)
    a = jnp.exp(ls + between)
    if masked:
        a = jnp.where(valid, a, 0.0)
    return ls, lm, a


def _sb_walk(i, carry, tile):
    def alive_of(c):
        return (jnp.max(c[0]) > SB_DEAD).astype(jnp.int32)

    def cond(state):
        n, alive = state[0], state[1]
        return jnp.logical_and(n < i, alive > 0)

    def step(state):
        n = state[0]
        c = tile(i - 1 - n, state[2:], False)
        return (n + 1, alive_of(c)) + tuple(c)

    out = lax.while_loop(cond, step, (jnp.int32(0), alive_of(carry)) + tuple(carry))
    return out[2:]


def sb_fwd(qkv):
    s = qkv.shape[2]
    t = ATT_T
    nq = s // t
    q_spec, k_spec, v_spec, row_spec = _att_specs(s, 6, 7, 8)

    def body(q_ref, k_ref, v_ref, o_ref):
        i = pl.program_id(1)
        qs = q_ref[...] * 0.125

        def tile(kb, carry, masked):
            run, acc = carry
            k0 = pl.multiple_of(kb * t, t)
            _, lm, a = _sb_tile(qs, k_ref[pl.ds(k0, t), :], run, masked)
            acc = acc + _dot(a.astype(BF16), v_ref[pl.ds(k0, t), :])
            return run + jnp.sum(lm, axis=-1, keepdims=True), acc

        carry = tile(i, (jnp.zeros((t, 1), F32), jnp.zeros((t, HEAD_DIM), F32)), True)
        _, acc = _sb_walk(i, carry, tile)
        o_ref[...] = acc

    return pl.pallas_call(
        body, name="sb_fwd", grid=(N_HEADS, nq),
        in_specs=[q_spec, k_spec, v_spec],
        out_specs=row_spec(HEAD_DIM),
        out_shape=jax.ShapeDtypeStruct((N_HEADS, s, HEAD_DIM), F32),
        compiler_params=_params(("arbitrary", "arbitrary")),
    )(qkv, qkv, qkv)


def sb_bwd(qkv, do, o):
    s = qkv.shape[2]
    t = ATT_T
    nq = s // t
    q_spec, k_spec, v_spec, row_spec = _att_specs(s, 6, 7, 8)
    any_spec = pl.BlockSpec(memory_space=pl.ANY)

    def body(q_ref, k_ref, v_ref, do_ref, o_ref, dq_ref, dk_hbm, dv_hbm, dk_acc, dv_acc):
        h = pl.program_id(0)
        i = pl.program_id(1)

        @pl.when(i == 0)
        def _():
            dk_acc[...] = jnp.zeros_like(dk_acc)
            dv_acc[...] = jnp.zeros_like(dv_acc)

        qs = q_ref[...] * 0.125
        dob = do_ref[...]
        tot = jnp.sum(o_ref[...] * dob.astype(F32), axis=-1, keepdims=True)

        def tile(kb, carry, masked):
            run, run_g, dq = carry
            k0 = pl.multiple_of(kb * t, t)
            k = k_ref[pl.ds(k0, t), :]
            ls, lm, a = _sb_tile(qs, k, run, masked)
            ab = a.astype(BF16)
            g = ab.astype(F32) * _dot_nt(dob, v_ref[pl.ds(k0, t), :])
            row = lax.broadcasted_iota(jnp.int32, (t, t), 0)
            col = lax.broadcasted_iota(jnp.int32, (t, t), 1)
            from_here = (row >= col).astype(BF16)
            hi, lo = _split2(g)
            g_right = run_g + _dot(hi, from_here) + _dot(lo, from_here)
            g_left = tot - g_right
            dz = g - jnp.exp(ls) * (g + g_left)
            if masked:
                dz = jnp.where(_causal(True), dz, 0.0)
            dzb = dz.astype(BF16)
            dk_acc[pl.ds(k0, t), :] += _dot_tn(dzb, qs)
            dv_acc[pl.ds(k0, t), :] += _dot_tn(ab, dob)
            return (run + jnp.sum(lm, axis=-1, keepdims=True),
                    run_g + jnp.sum(g, axis=-1, keepdims=True),
                    dq + _dot(dzb, k))

        zero = jnp.zeros((t, 1), F32)
        carry = tile(i, (zero, zero, jnp.zeros((t, HEAD_DIM), F32)), True)
        _, _, dq = _sb_walk(i, carry, tile)
        dq_ref[...] = dq * 0.125

        @pl.when(i == nq - 1)
        def _():
            pltpu.sync_copy(dk_acc, dk_hbm.at[h])
            pltpu.sync_copy(dv_acc, dv_hbm.at[h])

    return pl.pallas_call(
        body, name="sb_bwd", grid=(N_HEADS, nq),
        in_specs=[q_spec, k_spec, v_spec, row_spec(HEAD_DIM), row_spec(HEAD_DIM)],
        out_specs=[row_spec(HEAD_DIM), any_spec, any_spec],
        out_shape=[jax.ShapeDtypeStruct((N_HEADS, s, HEAD_DIM), F32)] * 3,
        scratch_shapes=[pltpu.VMEM((s, HEAD_DIM), F32), pltpu.VMEM((s, HEAD_DIM), F32)],
        compiler_params=_params(("arbitrary", "arbitrary")),
    )(qkv, qkv, qkv, do, o)


def _branch_inputs(refs, br):
    ya_ref, yb_ref, yc_ref, yd_ref = refs
    if br == 1:
        return yb_ref[...]
    return _heads_to_lanes((ya_ref, None, yc_ref, yd_ref)[br])


def outproj_fwd(x, ya, yb, yc, yd, gates, bg, wout):
    s = x.shape[0]
    tm = min(ROW_T, s)

    def body(x_ref, ya_ref, yb_ref, yc_ref, yd_ref, gates_ref, bg_ref, w_ref, out_ref):
        pieces = []
        for br in range(4):
            cols = slice(br * D_BRANCH, (br + 1) * D_BRANCH)
            y = _branch_inputs((ya_ref, yb_ref, yc_ref, yd_ref), br)
            r = lax.rsqrt(jnp.mean(y * y, axis=-1, keepdims=True) + EPS)
            gt = gates_ref[:, cols]
            pieces.append((y * r * bg_ref[:, cols]) * (gt * _sigmoid(gt)))
        merged = jnp.concatenate(pieces, axis=1).astype(BF16)
        out_ref[...] = x_ref[...] + _dot(merged, w_ref[...])

    head_spec = pl.BlockSpec((N_HEADS, tm, HEAD_DIM), lambda i: (0, i, 0))
    return pl.pallas_call(
        body, name="outproj_fwd", grid=(s // tm,),
        in_specs=[pl.BlockSpec((tm, D_MODEL), lambda i: (i, 0)),
                  head_spec, pl.BlockSpec((tm, D_BRANCH), lambda i: (i, 0)), head_spec, head_spec,
                  pl.BlockSpec((tm, D_MODEL), lambda i: (i, 0)),
                  pl.BlockSpec((1, D_MODEL), lambda i: (0, 0)),
                  pl.BlockSpec((D_MODEL, D_MODEL), lambda i: (0, 0))],
        out_specs=pl.BlockSpec((tm, D_MODEL), lambda i: (i, 0)),
        out_shape=jax.ShapeDtypeStruct((s, D_MODEL), F32),
        compiler_params=_params(("arbitrary",)),
    )(x, ya, yb, yc, yd, gates, bg, wout)


def outproj_bwd(dout, ya, yb, yc, yd, gates, bg, wout):
    s = dout.shape[0]
    tm = min(ROW_T, s)

    def body(dout_ref, ya_ref, yb_ref, yc_ref, yd_ref, gates_ref, bg_ref, w_ref,
             dya_ref, dyb_ref, dyc_ref, dyd_ref, dgates_ref, dbg_ref, dw_ref):
        i = pl.program_id(0)

        @pl.when(i == 0)
        def _():
            dbg_ref[...] = jnp.zeros_like(dbg_ref)
            dw_ref[...] = jnp.zeros_like(dw_ref)

        doutb = dout_ref[...].astype(BF16)
        dmerged = _dot_nt(doutb, w_ref[...])
        pieces = []
        for br in range(4):
            cols = slice(br * D_BRANCH, (br + 1) * D_BRANCH)
            y = _branch_inputs((ya_ref, yb_ref, yc_ref, yd_ref), br)
            r = lax.rsqrt(jnp.mean(y * y, axis=-1, keepdims=True) + EPS)
            yn = y * r
            bgv = bg_ref[:, cols]
            gt = gates_ref[:, cols]
            sig = _sigmoid(gt)
            act = gt * sig
            n = yn * bgv
            pieces.append(n * act)
            dm = dmerged[:, cols]
            dn = dm * act
            dgates_ref[:, cols] = (dm * n * (sig * (1.0 + gt * (1.0 - sig)))).astype(BF16)
            dbg_ref[:, cols] += jnp.sum(dn * yn, axis=0, keepdims=True)
            u = dn * bgv
            dy = r * (u - yn * jnp.mean(yn * u, axis=-1, keepdims=True))
            if br == 1:
                dyb_ref[...] = dy
            else:
                dref = (dya_ref, None, dyc_ref, dyd_ref)[br]
                for hh in range(N_HEADS):
                    dref[hh] = dy[:, hh * HEAD_DIM:(hh + 1) * HEAD_DIM].astype(BF16)
        merged = jnp.concatenate(pieces, axis=1).astype(BF16)
        dw_ref[...] += _dot_tn(merged, doutb)

    head_spec = pl.BlockSpec((N_HEADS, tm, HEAD_DIM), lambda i: (0, i, 0))
    head_shape = jax.ShapeDtypeStruct((N_HEADS, s, HEAD_DIM), BF16)
    return pl.pallas_call(
        body, name="outproj_bwd", grid=(s // tm,),
        in_specs=[pl.BlockSpec((tm, D_MODEL), lambda i: (i, 0)),
                  head_spec, pl.BlockSpec((tm, D_BRANCH), lambda i: (i, 0)), head_spec, head_spec,
                  pl.BlockSpec((tm, D_MODEL), lambda i: (i, 0)),
                  pl.BlockSpec((1, D_MODEL), lambda i: (0, 0)),
                  pl.BlockSpec((D_MODEL, D_MODEL), lambda i: (0, 0))],
        out_specs=[head_spec, pl.BlockSpec((tm, D_BRANCH), lambda i: (i, 0)), head_spec, head_spec,
                   pl.BlockSpec((tm, D_MODEL), lambda i: (i, 0)),
                   pl.BlockSpec((1, D_MODEL), lambda i: (0, 0)),
                   pl.BlockSpec((D_MODEL, D_MODEL), lambda i: (0, 0))],
        out_shape=[head_shape, jax.ShapeDtypeStruct((s, D_BRANCH), F32), head_shape, head_shape,
                   jax.ShapeDtypeStruct((s, D_MODEL), BF16),
                   jax.ShapeDtypeStruct((1, D_MODEL), F32),
                   jax.ShapeDtypeStruct((D_MODEL, D_MODEL), F32)],
        compiler_params=_params(("arbitrary",)),
    )(dout, ya, yb, yc, yd, gates, bg, wout)


def final_loss(x, tgt, g):
    s = x.shape[0]
    tm = min(ROW_T, s)

    def body(x_ref, t_ref, g_ref, loss_ref, dx_ref, dg_ref):
        i = pl.program_id(0)

        @pl.when(i == 0)
        def _():
            loss_ref[...] = jnp.zeros_like(loss_ref)
            dg_ref[...] = jnp.zeros_like(dg_ref)

        xv = x_ref[...]
        gv = g_ref[...]
        r = lax.rsqrt(jnp.mean(xv * xv, axis=-1, keepdims=True) + EPS)
        xn = xv * r
        err = xn * gv - t_ref[...]
        loss_ref[...] += jnp.sum(err * err) * (0.5 / D_MODEL)
        dy = err * (1.0 / D_MODEL)
        u = dy * gv
        dx_ref[...] = r * (u - xn * jnp.mean(xn * u, axis=-1, keepdims=True))
        dg_ref[...] += jnp.sum(dy * xn, axis=0, keepdims=True)

    return pl.pallas_call(
        body, name="final_loss", grid=(s // tm,),
        in_specs=[pl.BlockSpec((tm, D_MODEL), lambda i: (i, 0)),
                  pl.BlockSpec((tm, D_MODEL), lambda i: (i, 0)),
                  pl.BlockSpec((1, D_MODEL), lambda i: (0, 0))],
        out_specs=[pl.BlockSpec((1, 128), lambda i: (0, 0)),
                   pl.BlockSpec((tm, D_MODEL), lambda i: (i, 0)),
                   pl.BlockSpec((1, D_MODEL), lambda i: (0, 0))],
        out_shape=[jax.ShapeDtypeStruct((1, 128), F32),
                   jax.ShapeDtypeStruct((s, D_MODEL), F32),
                   jax.ShapeDtypeStruct((1, D_MODEL), F32)],
        compiler_params=_params(("arbitrary",)),
    )(x, tgt, g)


def _rel_index():
    i = np.arange(A_TQ)[:, None]
    j = np.arange(A_BAND)[None, :]
    rel = np.clip(i - j + (A_BAND - A_TQ), -MAX_REL, MAX_REL) + MAX_REL
    dchunk = i // CHUNK + LOOKBACK - j // CHUNK
    valid = (dchunk >= 0) & (dchunk <= LOOKBACK)
    return jnp.asarray(np.where(valid, rel, -1).astype(np.int32))


def _layer_consts(p):
    tbias = relbias_tile(p["rel_bias"], _rel_index())
    return dict(
        norm_g=p["norm_g"].reshape(1, D_MODEL),
        v_gain=p["v_gain"].reshape(1, D_BRANCH),
        b_col=p["b_s"].reshape(N_HEADS, SG_CHUNK, 1),
        bg=p["branch_gain"].reshape(1, D_MODEL),
        tbias=tbias,
    )


def _gate_layout(fp, b_f, s):
    nb = s // 128
    ft = fp[:, :N_HEADS].T.reshape(N_HEADS * nb, 128)
    bcol = jnp.repeat(b_f, nb).reshape(N_HEADS * nb, 1)
    return ft, bcol


def layer_fwd(x, p):
    s = x.shape[0]
    c = _layer_consts(p)
    h, qkv, gates, uv, fp = inproj_fwd(x, c["norm_g"], p["wp"])
    keep = A_BAND - A_TQ
    kpad = jnp.pad(qkv[1], ((0, 0), (keep, 0), (0, 0)))
    vpad = jnp.pad(qkv[2], ((0, 0), (keep, 0), (0, 0)))
    ya, lse_a = mix_a_fwd(qkv, kpad, vpad, c["tbias"])
    yb = mix_b_fwd(uv, c["v_gain"], p["w_s"], c["b_col"])
    ft, bcol = _gate_layout(fp, p["b_f"], s)
    c_row = fox_gate_fwd(ft, bcol).reshape(N_HEADS, s // ATT_T, 1, ATT_T)
    yc, ref_c, rl_c = fox_fwd(qkv, c_row)
    yd = sb_fwd(qkv)
    out = outproj_fwd(x, ya, yb, yc, yd, gates, c["bg"], p["wout"])
    saved = dict(consts=c, x=x, h=h, qkv=qkv, gates=gates, uv=uv, kpad=kpad, vpad=vpad, ft=ft, bcol=bcol,
                 c_row=c_row, ya=ya, lse_a=lse_a, yb=yb, yc=yc, ref_c=ref_c, rl_c=rl_c, yd=yd)
    return out, saved


def layer_bwd(dout, p, sv):
    s = dout.shape[0]
    c = sv["consts"]
    dya, dyb, dyc, dyd, dgates, dbg, dwout = outproj_bwd(
        dout, sv["ya"], sv["yb"], sv["yc"], sv["yd"], sv["gates"], c["bg"], p["wout"])
    keep = A_BAND - A_TQ
    dqa, dka, dva, dt = mix_a_bwd(sv["qkv"], sv["kpad"], sv["vpad"], c["tbias"], dya, sv["ya"], sv["lse_a"])
    drel = relbias_grad(dt, _rel_index())[:N_HEADS, :2 * MAX_REL + 1]
    duv, dws, dbs, dvgain = mix_b_bwd(sv["uv"], c["v_gain"], p["w_s"], c["b_col"], dyb)
    dqc, dkc, dvc, dc = fox_bwd(sv["qkv"], sv["c_row"], dyc, sv["yc"], sv["ref_c"], sv["rl_c"])
    dft, dbf = fox_gate_bwd(sv["ft"], sv["bcol"], dc.reshape(N_HEADS * (s // 128), 128))
    dfp = jnp.pad(dft.reshape(N_HEADS, s).T, ((0, 0), (0, 128 - N_HEADS)))
    dqd, dkd, dvd = sb_bwd(sv["qkv"], dyd, sv["yd"])
    dp, dx, dnorm = inproj_bwd((dqa, dka, dva, dqc, dkc, dvc, dqd, dkd, dvd), dgates, duv, dfp,
                               p["wp"], sv["x"], c["norm_g"], dout)
    dwp = weight_grad(sv["h"], dp, "inproj_wgrad")
    grads = dict(norm_g=dnorm.reshape(D_MODEL), wp=dwp, b_f=dbf[:N_HEADS, 0], rel_bias=drel,
                 w_s=dws, b_s=dbs.reshape(N_HEADS, SG_CHUNK), v_gain=dvgain.reshape(D_BRANCH),
                 branch_gain=dbg.reshape(4, D_BRANCH), wout=dwout)
    return dx, grads


def local_step(x, tgt, layers, final_g):
    saved = []
    cur = x
    for p in layers:
        cur, sv = layer_fwd(cur, p)
        saved.append(sv)
    loss, dcur, dfinal = final_loss(cur, tgt, final_g.reshape(1, D_MODEL))
    grads = [None] * len(layers)
    for l in reversed(range(len(layers))):
        dcur, grads[l] = layer_bwd(dcur, layers[l], saved[l])
    return loss[0, 0], dcur, grads, dfinal.reshape(D_MODEL)


def gather_weights(wb, wf):
    def body(wb_ref, wf_ref, ob_ref, of_ref, send_sems, recv_sems, loc_sems):
        x, y, c = lax.axis_index("x"), lax.axis_index("y"), lax.axis_index("c")
        me = 2 * x + y
        chips = [(1 - x, y), (x, 1 - y), (1 - x, 1 - y)]
        pairs = [(wb_ref, ob_ref), (wf_ref, of_ref)]
        local = [pltpu.make_async_copy(src, dst.at[me], loc_sems.at[n]) for n, (src, dst) in enumerate(pairs)]
        for cp in local:
            cp.start()

        def copy(j, n, slot):
            src, dst = pairs[n]
            return pltpu.make_async_remote_copy(
                src_ref=src, dst_ref=dst.at[slot], send_sem=send_sems.at[2 * j + n], recv_sem=recv_sems.at[2 * j + n],
                device_id=(chips[j][0], chips[j][1], c), device_id_type=MESH)

        sends = [copy(j, n, me) for j in range(3) for n in range(2)]
        for cp in sends:
            cp.start()
        for j in range(3):
            for n in range(2):
                copy(j, n, 2 * chips[j][0] + chips[j][1]).wait_recv()
        for cp in sends:
            cp.wait_send()
        for cp in local:
            cp.wait()

    any_spec = pl.BlockSpec(memory_space=pl.ANY)
    return pl.pallas_call(
        body, name="gather_weights",
        in_specs=[any_spec, any_spec], out_specs=[any_spec, any_spec],
        out_shape=[jax.ShapeDtypeStruct((4,) + wb.shape, wb.dtype), jax.ShapeDtypeStruct((4,) + wf.shape, wf.dtype)],
        scratch_shapes=[pltpu.SemaphoreType.DMA((6,)), pltpu.SemaphoreType.DMA((6,)), pltpu.SemaphoreType.DMA((2,))],
    )(wb, wf)


def exchange_grads(big, small):
    def body(b_ref, s_ref, rb_ref, rs_ref, send_sems, recv_sems, loc_sems):
        x, y, c = lax.axis_index("x"), lax.axis_index("y"), lax.axis_index("c")
        me_chip = 2 * x + y
        me = 4 * x + 2 * y + c
        peers = [(x, y, 1 - c)]
        for px, py in [(1 - x, y), (x, 1 - y), (1 - x, 1 - y)]:
            peers += [(px, py, c), (px, py, 1 - c)]
        local = [pltpu.make_async_copy(b_ref.at[me_chip], rb_ref.at[me], loc_sems.at[0]),
                 pltpu.make_async_copy(s_ref, rs_ref.at[me], loc_sems.at[1])]
        for cp in local:
            cp.start()

        def copies(n, chip, slot):
            kw = dict(device_id=peers[n], device_id_type=MESH)
            return [pltpu.make_async_remote_copy(src_ref=b_ref.at[chip], dst_ref=rb_ref.at[slot],
                                                 send_sem=send_sems.at[2 * n], recv_sem=recv_sems.at[2 * n], **kw),
                    pltpu.make_async_remote_copy(src_ref=s_ref, dst_ref=rs_ref.at[slot],
                                                 send_sem=send_sems.at[2 * n + 1], recv_sem=recv_sems.at[2 * n + 1], **kw)]

        sends = [cp for n, (px, py, _) in enumerate(peers) for cp in copies(n, 2 * px + py, me)]
        for cp in sends:
            cp.start()
        for n, (px, py, pc) in enumerate(peers):
            for cp in copies(n, me_chip, 4 * px + 2 * py + pc):
                cp.wait_recv()
        for cp in sends:
            cp.wait_send()
        for cp in local:
            cp.wait()

    any_spec = pl.BlockSpec(memory_space=pl.ANY)
    return pl.pallas_call(
        body, name="exchange_grads",
        in_specs=[any_spec, any_spec], out_specs=[any_spec, any_spec],
        out_shape=[jax.ShapeDtypeStruct((8,) + big.shape[1:], big.dtype),
                   jax.ShapeDtypeStruct((8,) + small.shape, small.dtype)],
        scratch_shapes=[pltpu.SemaphoreType.DMA((14,)), pltpu.SemaphoreType.DMA((14,)), pltpu.SemaphoreType.DMA((2,))],
    )(big, small)


def adamw_reduce(parts, w, m, v, name):
    rows = w.shape[0]
    tr = PACK_ROW_TILE
    c1 = 1.0 - ADAM_B1 ** ADAM_STEP
    c2 = 1.0 - ADAM_B2 ** ADAM_STEP

    def body(p_ref, w_ref, m_ref, v_ref, g_ref, d_ref, nm_ref, nv_ref):
        g = p_ref[0].astype(F32)
        for n in range(1, 8):
            g = g + p_ref[n].astype(F32)
        g_ref[...] = g
        nm = ADAM_B1 * m_ref[...] + (1.0 - ADAM_B1) * g
        nv = ADAM_B2 * v_ref[...] + (1.0 - ADAM_B2) * (g * g)
        nm_ref[...] = nm
        nv_ref[...] = nv
        d_ref[...] = -ADAM_LR * ((nm / c1) / (jnp.sqrt(nv / c2) + ADAM_EPS) + ADAM_WD * w_ref[...])

    spec = pl.BlockSpec((tr, 128), lambda i: (i, 0))
    shape = jax.ShapeDtypeStruct((rows, 128), F32)
    return pl.pallas_call(
        body, name=name, grid=(rows // tr,),
        in_specs=[pl.BlockSpec((8, tr, 128), lambda i: (0, i, 0)), spec, spec, spec],
        out_specs=[spec] * 4, out_shape=[shape] * 4,
        compiler_params=_params(("arbitrary",)),
    )(parts, w, m, v)


SHARDED = ("w_in", "w_out", "branch_gain")
SMALL = ("norm_g", "b_f", "rel_bias", "w_s", "b_s", "v_gain", "final_g")
WEIGHTS = ("norm_g", "w_in", "b_f", "rel_bias", "w_s", "b_s", "v_gain", "branch_gain", "w_out", "final_g")
PACK_ROW_TILE = 512


def _rows_of(shape):
    return -(-int(np.prod(shape)) // 128)


def _pack(leaves):
    parts = []
    for a in leaves:
        flat = a.reshape(-1).astype(F32)
        parts.append(jnp.pad(flat, (0, _rows_of(a.shape) * 128 - flat.shape[0])))
    flat = jnp.concatenate(parts)
    rows = flat.shape[0] // 128
    total = -(-rows // PACK_ROW_TILE) * PACK_ROW_TILE
    return jnp.pad(flat, (0, (total - rows) * 128)).reshape(total, 128)


def _unpack(slab, shapes):
    out, row = [], 0
    for shp in shapes:
        n = int(np.prod(shp))
        r = _rows_of(shp)
        out.append(slab[row:row + r].reshape(-1)[:n].reshape(shp))
        row += r
    return out


def _pack_w_in(w):
    return jnp.concatenate([w[:, :2816], w[:, 2820:], w[:, 2816:2820],
                            jnp.zeros((w.shape[0], N_PACK - N_IN), w.dtype)], axis=1)


def _unpack_w_in(wp):
    return jnp.concatenate([wp[:, :2816], wp[:, F_COL:F_COL + N_HEADS], wp[:, 2816:F_COL]], axis=1)


def kernel(x, norm_g, w_in, b_f, rel_bias, w_s, b_s, v_gain, branch_gain, w_out, final_g, loss_target, m_norm_g, m_w_in, m_b_f, m_rel_bias, m_w_s, m_b_s, m_v_gain, m_branch_gain, m_w_out, m_final_g, v_norm_g, v_w_in, v_b_f, v_rel_bias, v_w_s, v_b_s, v_v_gain, v_branch_gain, v_w_out, v_final_g):
    depth = norm_g.shape[0]
    weights = dict(norm_g=norm_g, w_in=w_in, b_f=b_f, rel_bias=rel_bias, w_s=w_s, b_s=b_s, v_gain=v_gain,
                   branch_gain=branch_gain, w_out=w_out, final_g=final_g)
    mom1 = dict(norm_g=m_norm_g, w_in=m_w_in, b_f=m_b_f, rel_bias=m_rel_bias, w_s=m_w_s, b_s=m_b_s,
                v_gain=m_v_gain, branch_gain=m_branch_gain, w_out=m_w_out, final_g=m_final_g)
    mom2 = dict(norm_g=v_norm_g, w_in=v_w_in, b_f=v_b_f, rel_bias=v_rel_bias, w_s=v_w_s, b_s=v_b_s,
                v_gain=v_v_gain, branch_gain=v_branch_gain, w_out=v_w_out, final_g=v_final_g)

    n_in_rows = _rows_of(w_in.shape)
    n_out_rows = _rows_of(w_out.shape)
    wb = jnp.concatenate([w_in.astype(BF16).reshape(n_in_rows, 128), w_out.astype(BF16).reshape(n_out_rows, 128)])
    wf = jnp.pad(branch_gain.reshape(-1), (0, 8 * 128 - branch_gain.size)).reshape(8, 128)
    gb, gf = gather_weights(wb, wf)
    w_in_full = gb[:, :n_in_rows].reshape((4,) + w_in.shape)
    w_in_full = jnp.moveaxis(w_in_full, 0, 2).reshape(depth, D_MODEL, N_IN)
    w_out_full = gb[:, n_in_rows:].reshape((4,) + w_out.shape)
    w_out_full = jnp.moveaxis(w_out_full, 0, 1).reshape(depth, D_MODEL, D_MODEL)
    bg_full = gf.reshape(4, -1)[:, :branch_gain.size].reshape((4,) + branch_gain.shape)
    bg_full = jnp.moveaxis(bg_full, 0, 2).reshape(depth, 4, D_BRANCH)

    layers = [dict(norm_g=norm_g[l], wp=_pack_w_in(w_in_full[l]), b_f=b_f[l], rel_bias=rel_bias[l], w_s=w_s[l],
                   b_s=b_s[l], v_gain=v_gain[l], branch_gain=bg_full[l], wout=w_out_full[l]) for l in range(depth)]

    loss_part, grad_x, lgrads, dfinal = local_step(x[0], loss_target[0], layers, final_g)
    loss = lax.psum(loss_part, ("x", "y", "c"))

    stack = lambda k: jnp.stack([g[k] for g in lgrads])
    d_w_in = jnp.stack([_unpack_w_in(g["wp"]) for g in lgrads])
    d_w_out = stack("wout")
    d_bg = stack("branch_gain")
    small = dict(norm_g=stack("norm_g"), b_f=stack("b_f"), rel_bias=stack("rel_bias"), w_s=stack("w_s"),
                 b_s=stack("b_s"), v_gain=stack("v_gain"), final_g=dfinal)
    slabs = []
    for sidx in range(4):
        slabs.append(_pack([d_w_in[:, :, sidx * N_SHARD:(sidx + 1) * N_SHARD],
                            d_w_out[:, sidx * D_BRANCH:(sidx + 1) * D_BRANCH, :],
                            d_bg[:, :, sidx * HEAD_DIM:(sidx + 1) * HEAD_DIM]]).astype(BF16))
    big_parts, small_parts = exchange_grads(jnp.stack(slabs), _pack([small[k] for k in SMALL]))

    outs = {}
    for names, parts, name in ((SHARDED, big_parts, "adamw_big"), (SMALL, small_parts, "adamw_small")):
        pack_local = lambda d: _pack([d[k] for k in names])
        slabs = adamw_reduce(parts, pack_local(weights), pack_local(mom1), pack_local(mom2), name)
        shapes = [weights[k].shape for k in names]
        for tag, slab in zip(("grad", "delta", "new_m", "new_v"), slabs):
            for k, a in zip(names, _unpack(slab, shapes)):
                outs[tag, k] = a
    result = [loss, grad_x[None]]
    for tag in ("grad", "delta", "new_m", "new_v"):
        result += [outs[tag, k] for k in WEIGHTS]
    return tuple(result)
```

```python
import functools

import jax
import jax.numpy as jnp
import numpy as np
from jax import lax
from jax.experimental import pallas as pl
from jax.experimental.pallas import tpu as pltpu

F32 = jnp.float32
BF16 = jnp.bfloat16
MESH = pl.DeviceIdType.MESH

D_MODEL = 1024
D_BRANCH = 256
N_HEADS = 4
HEAD_DIM = 64
CHUNK = 64
LOOKBACK = 8
MAX_REL = 128
SG_CHUNK = 128
EPS = 1e-6
N_IN = 3844
N_PACK = 3968
F_COL = 3840
N_SHARD = 961
NEG = -1e30

A_TQ = 128
A_BAND = A_TQ + LOOKBACK * CHUNK
A_QB = 512
ATT_T = 256
FOX_MID = 4
FOX_WIDE = 8
FOX_DEAD2 = -160.0
LOG2E = 1.4426950408889634
SB_DEAD = -110.0
ROW_T = 512
VMEM_LIMIT = 56 * 1024 * 1024

ADAM_LR = 0.001
ADAM_B1 = 0.9
ADAM_B2 = 0.999
ADAM_EPS = 1e-08
ADAM_WD = 0.01
ADAM_STEP = 10

SEC_A_Q, SEC_A_K, SEC_A_V, SEC_A_G = 0, 256, 512, 768
SEC_B_U, SEC_B_V, SEC_B_G = 1024, 1280, 1536
SEC_C_Q, SEC_C_K, SEC_C_V, SEC_C_G = 1792, 2048, 2304, 2560
SEC_D_Q, SEC_D_K, SEC_D_V, SEC_D_G = 2816, 3072, 3328, 3584
QKV_SECS = (SEC_A_Q, SEC_C_Q, SEC_C_K, SEC_C_V, SEC_D_Q, SEC_D_K, SEC_D_V)
GATE_SECS = (SEC_A_G, SEC_B_G, SEC_C_G, SEC_D_G)


def _dot(a, b):
    return jnp.dot(a, b, preferred_element_type=F32)


def _dot_nt(a, b):
    return lax.dot_general(a, b, (((1,), (1,)), ((), ())), preferred_element_type=F32)


def _dot_tn(a, b):
    return lax.dot_general(a, b, (((0,), (0,)), ((), ())), preferred_element_type=F32)


def _split2(x):
    hi = x.astype(BF16)
    lo = (x - hi.astype(F32)).astype(BF16)
    return hi, lo


def _split3(x):
    hi = x.astype(BF16)
    r = x - hi.astype(F32)
    mid = r.astype(BF16)
    lo = (r - mid.astype(F32)).astype(BF16)
    return hi, mid, lo


def _sigmoid(x):
    return 1.0 / (1.0 + jnp.exp(-x))


def _params(sem=None, vmem=VMEM_LIMIT):
    return pltpu.CompilerParams(dimension_semantics=sem, vmem_limit_bytes=vmem)


def _heads_to_lanes(ref):
    return jnp.concatenate([ref[h] for h in range(N_HEADS)], axis=1)


def inproj_fwd(x, g, wp):
    s = x.shape[0]
    tm = A_QB

    def body(x_ref, g_ref, w_ref, h_ref, qkv_ref, kva_ref, gates_ref, uv_ref, f_ref):
        xv = x_ref[...]
        r = lax.rsqrt(jnp.mean(xv * xv, axis=-1, keepdims=True) + EPS)
        h = (xv * r * g_ref[...]).astype(BF16)
        h_ref[...] = h
        for n, off in enumerate(QKV_SECS):
            p = _dot(h, w_ref[:, off:off + D_BRANCH])
            for hh in range(N_HEADS):
                qkv_ref[n, hh] = p[:, hh * HEAD_DIM:(hh + 1) * HEAD_DIM].astype(BF16)
        for n, off in enumerate((SEC_A_K, SEC_A_V)):
            p = _dot(h, w_ref[:, off:off + D_BRANCH])
            for hh in range(N_HEADS):
                kva_ref[n, hh] = p[:, hh * HEAD_DIM:(hh + 1) * HEAD_DIM].astype(BF16)
        for n, off in enumerate(GATE_SECS):
            gates_ref[:, n * D_BRANCH:(n + 1) * D_BRANCH] = _dot(h, w_ref[:, off:off + D_BRANCH])
        uv_ref[...] = _dot(h, w_ref[:, SEC_B_U:SEC_B_U + 2 * D_BRANCH])
        f_ref[...] = _dot(h, w_ref[:, F_COL:F_COL + 128])

    return pl.pallas_call(
        body, name="inproj_fwd", grid=(s // tm,),
        in_specs=[pl.BlockSpec((tm, D_MODEL), lambda i: (i, 0)),
                  pl.BlockSpec((1, D_MODEL), lambda i: (0, 0)),
                  pl.BlockSpec((D_MODEL, N_PACK), lambda i: (0, 0))],
        out_specs=[pl.BlockSpec((tm, D_MODEL), lambda i: (i, 0)),
                   pl.BlockSpec((len(QKV_SECS), N_HEADS, tm, HEAD_DIM), lambda i: (0, 0, i, 0)),
                   pl.BlockSpec((2, N_HEADS, tm, HEAD_DIM), lambda i: (0, 0, i + 1, 0)),
                   pl.BlockSpec((tm, D_MODEL), lambda i: (i, 0)),
                   pl.BlockSpec((tm, 2 * D_BRANCH), lambda i: (i, 0)),
                   pl.BlockSpec((tm, 128), lambda i: (i, 0))],
        out_shape=[jax.ShapeDtypeStruct((s, D_MODEL), BF16),
                   jax.ShapeDtypeStruct((len(QKV_SECS), N_HEADS, s, HEAD_DIM), BF16),
                   jax.ShapeDtypeStruct((2, N_HEADS, s + tm, HEAD_DIM), BF16),
                   jax.ShapeDtypeStruct((s, D_MODEL), F32),
                   jax.ShapeDtypeStruct((s, 2 * D_BRANCH), F32),
                   jax.ShapeDtypeStruct((s, 128), F32)],
        compiler_params=_params(("arbitrary",)),
    )(x, g, wp)


def inproj_bwd(dqkv, dgates, duv, dfp, wp, x, g, dres):
    s = x.shape[0]
    tm = min(ROW_T, s)

    def body(*refs):
        dq_refs = refs[:9]
        dgates_ref, duv_ref, dfp_ref, w_ref, x_ref, g_ref, dres_ref, dp_ref, dx_ref, dg_ref = refs[9:]
        i = pl.program_id(0)
        a_q, a_k, a_v, c_q, c_k, c_v, d_q, d_k, d_v = [_heads_to_lanes(r).astype(BF16) for r in dq_refs]
        dgt = dgates_ref[...]
        duv_b = duv_ref[...].astype(BF16)
        dp = jnp.concatenate(
            [a_q, a_k, a_v, dgt[:, 0:256], duv_b, dgt[:, 256:512], c_q, c_k, c_v, dgt[:, 512:768],
             d_q, d_k, d_v, dgt[:, 768:1024], dfp_ref[...].astype(BF16)], axis=1)
        dp_ref[...] = dp
        dh = _dot_nt(dp, w_ref[...])
        xv = x_ref[...]
        r = lax.rsqrt(jnp.mean(xv * xv, axis=-1, keepdims=True) + EPS)
        xn = xv * r
        u = dh * g_ref[...]
        dx_ref[...] = dres_ref[...] + r * (u - xn * jnp.mean(xn * u, axis=-1, keepdims=True))

        @pl.when(i == 0)
        def _():
            dg_ref[...] = jnp.zeros_like(dg_ref)

        dg_ref[...] += jnp.sum(dh * xn, axis=0, keepdims=True)

    head_spec = pl.BlockSpec((N_HEADS, tm, HEAD_DIM), lambda i: (0, i, 0))
    return pl.pallas_call(
        body, name="inproj_bwd", grid=(s // tm,),
        in_specs=[head_spec] * 9 + [
            pl.BlockSpec((tm, D_MODEL), lambda i: (i, 0)),
            pl.BlockSpec((tm, 2 * D_BRANCH), lambda i: (i, 0)),
            pl.BlockSpec((tm, 128), lambda i: (i, 0)),
            pl.BlockSpec((D_MODEL, N_PACK), lambda i: (0, 0)),
            pl.BlockSpec((tm, D_MODEL), lambda i: (i, 0)),
            pl.BlockSpec((1, D_MODEL), lambda i: (0, 0)),
            pl.BlockSpec((tm, D_MODEL), lambda i: (i, 0))],
        out_specs=[pl.BlockSpec((tm, N_PACK), lambda i: (i, 0)),
                   pl.BlockSpec((tm, D_MODEL), lambda i: (i, 0)),
                   pl.BlockSpec((1, D_MODEL), lambda i: (0, 0))],
        out_shape=[jax.ShapeDtypeStruct((s, N_PACK), BF16),
                   jax.ShapeDtypeStruct((s, D_MODEL), F32),
                   jax.ShapeDtypeStruct((1, D_MODEL), F32)],
        compiler_params=_params(("arbitrary",)),
    )(*dqkv, dgates, duv, dfp, wp, x, g, dres)


def weight_grad(a, b, name):
    s, m = a.shape
    n = b.shape[1]
    tm = min(ROW_T, s)
    tmm = 256
    nsteps = s // tm

    def body(a_ref, b_ref, o_ref):
        k = pl.program_id(1)

        @pl.when(k == 0)
        def _():
            o_ref[...] = jnp.zeros_like(o_ref)

        o_ref[...] += _dot_tn(a_ref[...], b_ref[...])

    return pl.pallas_call(
        body, name=name, grid=(m // tmm, nsteps),
        in_specs=[pl.BlockSpec((tm, tmm), lambda j, k: (k, j)),
                  pl.BlockSpec((tm, n), lambda j, k: (k, 0))],
        out_specs=pl.BlockSpec((tmm, n), lambda j, k: (j, 0)),
        out_shape=jax.ShapeDtypeStruct((m, n), F32),
        compiler_params=_params(("arbitrary", "arbitrary")),
    )(a, b)


def _a_specs(s):
    nq = s // A_QB
    q_spec = pl.BlockSpec((None, None, A_QB, HEAD_DIM), lambda h, i: (0, h, jnp.minimum(i, nq - 1), 0))
    kv_specs = [pl.BlockSpec((None, None, A_QB, HEAD_DIM), lambda h, i, n=n, m=m: (n, h, jnp.minimum(i + m, nq), 0))
                for n in range(2) for m in range(2)]
    t_spec = pl.BlockSpec((None, A_TQ, A_BAND), lambda h, i: (h, 0, 0))
    return nq, q_spec, kv_specs, t_spec


def _a_window(first_ref, second_ref, i):
    first = first_ref[...]
    return jnp.concatenate([jnp.where(i > 0, first, jnp.zeros_like(first)), second_ref[...]], axis=0)


def _a_scores(q_ref, k, t_ref, i, j):
    rows = slice(j * A_TQ, (j + 1) * A_TQ)
    qs = q_ref[rows, :] * 0.125
    kj = k[j * A_TQ:j * A_TQ + A_BAND, :]
    sc = _dot_nt(qs, kj) + t_ref[...]
    col = lax.broadcasted_iota(jnp.int32, (A_TQ, A_BAND), 1)
    sc = jnp.where(col >= (A_BAND - A_TQ) - i * A_QB - j * A_TQ, sc, NEG)
    return rows, qs, kj, sc


def mix_a_fwd(qkv, kva, tbias):
    s = qkv.shape[2]
    nq, q_spec, kv_specs, t_spec = _a_specs(s)

    def body(q_ref, k0_ref, k1_ref, v0_ref, v1_ref, t_ref, o_ref, lse_ref):
        i = pl.program_id(1)
        k = _a_window(k0_ref, k1_ref, i)
        v = _a_window(v0_ref, v1_ref, i)
        for j in range(A_QB // A_TQ):
            rows, _, _, sc = _a_scores(q_ref, k, t_ref, i, j)
            m = jnp.max(sc, axis=-1, keepdims=True)
            p = jnp.exp(sc - m)
            l = jnp.sum(p, axis=-1, keepdims=True)
            o_ref[rows, :] = _dot(p.astype(BF16), v[j * A_TQ:j * A_TQ + A_BAND, :]) / l
            lse_ref[rows, :] = m + jnp.log(l)

    return pl.pallas_call(
        body, name="mix_a_fwd", grid=(N_HEADS, nq),
        in_specs=[q_spec] + kv_specs + [t_spec],
        out_specs=[pl.BlockSpec((None, A_QB, HEAD_DIM), lambda h, i: (h, i, 0)),
                   pl.BlockSpec((None, A_QB, 1), lambda h, i: (h, i, 0))],
        out_shape=[jax.ShapeDtypeStruct((N_HEADS, s, HEAD_DIM), F32),
                   jax.ShapeDtypeStruct((N_HEADS, s, 1), F32)],
        compiler_params=_params(("arbitrary", "arbitrary")),
    )(qkv, kva, kva, kva, kva, tbias)


def mix_a_bwd(qkv, kva, tbias, do, o, lse):
    s = qkv.shape[2]
    nq, q_spec, kv_specs, t_spec = _a_specs(s)
    row_spec = lambda w: pl.BlockSpec((None, A_QB, w), lambda h, i: (h, jnp.minimum(i, nq - 1), 0))
    done_spec = pl.BlockSpec((None, A_QB, HEAD_DIM), lambda h, i: (h, jnp.maximum(i - 1, 0), 0))
    win = 2 * A_QB

    def body(q_ref, k0_ref, k1_ref, v0_ref, v1_ref, t_ref, do_ref, o_ref, lse_ref,
             dq_ref, dk_ref, dv_ref, dt_ref, dk_win, dv_win):
        i = pl.program_id(1)

        @pl.when(i == 0)
        def _():
            dk_win[...] = jnp.zeros_like(dk_win)
            dv_win[...] = jnp.zeros_like(dv_win)
            dt_ref[...] = jnp.zeros_like(dt_ref)

        @pl.when(i < nq)
        def _():
            k = _a_window(k0_ref, k1_ref, i)
            v = _a_window(v0_ref, v1_ref, i)
            dt = jnp.zeros((A_TQ, A_BAND), F32)
            for j in range(A_QB // A_TQ):
                rows, qs, kj, sc = _a_scores(q_ref, k, t_ref, i, j)
                keys = slice(j * A_TQ, j * A_TQ + A_BAND)
                dob = do_ref[rows, :]
                p = jnp.exp(sc - lse_ref[rows, :])
                delta = jnp.sum(o_ref[rows, :] * dob.astype(F32), axis=-1, keepdims=True)
                ds = p * (_dot_nt(dob, v[keys, :]) - delta)
                dsb = ds.astype(BF16)
                dq_ref[rows, :] = _dot(dsb, kj) * 0.125
                dk_win[keys, :] += _dot_tn(dsb, qs)
                dv_win[keys, :] += _dot_tn(p.astype(BF16), dob)
                dt = dt + ds
            dt_ref[...] += dt

        dk_ref[...] = dk_win[0:A_QB, :]
        dv_ref[...] = dv_win[0:A_QB, :]
        dk_rest = dk_win[A_QB:win, :]
        dv_rest = dv_win[A_QB:win, :]
        dk_win[0:A_QB, :] = dk_rest
        dv_win[0:A_QB, :] = dv_rest
        dk_win[A_QB:win, :] = jnp.zeros((A_QB, HEAD_DIM), F32)
        dv_win[A_QB:win, :] = jnp.zeros((A_QB, HEAD_DIM), F32)

    return pl.pallas_call(
        body, name="mix_a_bwd", grid=(N_HEADS, nq + 1),
        in_specs=[q_spec] + kv_specs + [t_spec, row_spec(HEAD_DIM), row_spec(HEAD_DIM), row_spec(1)],
        out_specs=[row_spec(HEAD_DIM), done_spec, done_spec, t_spec],
        out_shape=[jax.ShapeDtypeStruct((N_HEADS, s, HEAD_DIM), F32),
                   jax.ShapeDtypeStruct((N_HEADS, s, HEAD_DIM), F32),
                   jax.ShapeDtypeStruct((N_HEADS, s, HEAD_DIM), F32),
                   jax.ShapeDtypeStruct((N_HEADS, A_TQ, A_BAND), F32)],
        scratch_shapes=[pltpu.VMEM((win, HEAD_DIM), F32), pltpu.VMEM((win, HEAD_DIM), F32)],
        compiler_params=_params(("arbitrary", "arbitrary")),
    )(qkv, kva, kva, kva, kva, tbias, do, o, lse)


def relbias_tile(rel_bias, relmat):
    nrel = 2 * MAX_REL + 1

    def body(rb_ref, rel_ref, o_ref):
        rel = rel_ref[...]
        o_ref[...] = jnp.full(o_ref.shape, NEG, F32)

        def step(r, carry):
            hit = rel == r
            for h in range(N_HEADS):
                o_ref[h] = jnp.where(hit, rb_ref[h, r], o_ref[h])
            return carry

        lax.fori_loop(0, nrel, step, 0)

    return pl.pallas_call(
        body, name="relbias_tile",
        in_specs=[pl.BlockSpec(memory_space=pltpu.SMEM), pl.BlockSpec(memory_space=pltpu.VMEM)],
        out_specs=pl.BlockSpec(memory_space=pltpu.VMEM),
        out_shape=jax.ShapeDtypeStruct((N_HEADS, A_TQ, A_BAND), F32),
        compiler_params=_params(),
    )(rel_bias, relmat)


def relbias_grad(dt, relmat):
    nrel = 2 * MAX_REL + 1

    def body(dt_ref, rel_ref, o_ref):
        rel = rel_ref[...]
        lane = lax.broadcasted_iota(jnp.int32, (8, 384), 1)
        row = lax.broadcasted_iota(jnp.int32, (8, 384), 0)

        def step(r, acc):
            hit = rel == r
            for h in range(N_HEADS):
                val = jnp.sum(jnp.where(hit, dt_ref[h], 0.0))
                acc = jnp.where((lane == r) & (row == h), val, acc)
            return acc

        o_ref[...] = lax.fori_loop(0, nrel, step, jnp.zeros((8, 384), F32))

    return pl.pallas_call(
        body, name="relbias_grad",
        out_shape=jax.ShapeDtypeStruct((8, 384), F32),
        compiler_params=_params(),
    )(dt, relmat)


def _b_norm(v, gain):
    mu = jnp.mean(v, axis=-1, keepdims=True)
    xc = v - mu
    rstd = lax.rsqrt(jnp.mean(xc * xc, axis=-1, keepdims=True) + EPS)
    xhat = xc * rstd
    return xhat, rstd, xhat * gain


def _tril_mask():
    t = lax.broadcasted_iota(jnp.int32, (SG_CHUNK, SG_CHUNK), 0)
    u = lax.broadcasted_iota(jnp.int32, (SG_CHUNK, SG_CHUNK), 1)
    return u <= t


def mix_b_fwd(uv, gain, w_s, b_col):
    s = uv.shape[0]
    tm = min(ROW_T, s)

    def body(uv_ref, gain_ref, w_ref, b_ref, y_ref):
        tril = _tril_mask()
        ws = [jnp.where(tril, w_ref[g], 0.0).astype(BF16) for g in range(N_HEADS)]
        for c in range(tm // SG_CHUNK):
            rows = slice(c * SG_CHUNK, (c + 1) * SG_CHUNK)
            u = uv_ref[rows, 0:D_BRANCH]
            _, _, vn = _b_norm(uv_ref[rows, D_BRANCH:2 * D_BRANCH], gain_ref[...])
            vnb = vn.astype(BF16)
            outs = []
            for g in range(N_HEADS):
                cols = slice(g * HEAD_DIM, (g + 1) * HEAD_DIM)
                mixed = _dot(ws[g], vnb[:, cols]) + b_ref[g]
                outs.append(u[:, cols] * mixed)
            y_ref[rows, :] = jnp.concatenate(outs, axis=1)

    return pl.pallas_call(
        body, name="mix_b_fwd", grid=(s // tm,),
        in_specs=[pl.BlockSpec((tm, 2 * D_BRANCH), lambda i: (i, 0)),
                  pl.BlockSpec((1, D_BRANCH), lambda i: (0, 0)),
                  pl.BlockSpec((N_HEADS, SG_CHUNK, SG_CHUNK), lambda i: (0, 0, 0)),
                  pl.BlockSpec((N_HEADS, SG_CHUNK, 1), lambda i: (0, 0, 0))],
        out_specs=pl.BlockSpec((tm, D_BRANCH), lambda i: (i, 0)),
        out_shape=jax.ShapeDtypeStruct((s, D_BRANCH), F32),
        compiler_params=_params(("arbitrary",)),
    )(uv, gain, w_s, b_col)


def mix_b_bwd(uv, gain, w_s, b_col, dy):
    s = uv.shape[0]
    tm = min(ROW_T, s)

    def body(uv_ref, gain_ref, w_ref, b_ref, dy_ref, duv_ref, dw_ref, db_ref, dgain_ref):
        i = pl.program_id(0)

        @pl.when(i == 0)
        def _():
            dw_ref[...] = jnp.zeros_like(dw_ref)
            db_ref[...] = jnp.zeros_like(db_ref)
            dgain_ref[...] = jnp.zeros_like(dgain_ref)

        tril = _tril_mask()
        ws = [jnp.where(tril, w_ref[g], 0.0).astype(BF16) for g in range(N_HEADS)]
        gain_v = gain_ref[...]
        for c in range(tm // SG_CHUNK):
            rows = slice(c * SG_CHUNK, (c + 1) * SG_CHUNK)
            u = uv_ref[rows, 0:D_BRANCH]
            xhat, rstd, vn = _b_norm(uv_ref[rows, D_BRANCH:2 * D_BRANCH], gain_v)
            vnb = vn.astype(BF16)
            dyv = dy_ref[rows, :]
            dus, dvns = [], []
            for g in range(N_HEADS):
                cols = slice(g * HEAD_DIM, (g + 1) * HEAD_DIM)
                mixed = _dot(ws[g], vnb[:, cols]) + b_ref[g]
                dus.append(dyv[:, cols] * mixed)
                dmixed = dyv[:, cols] * u[:, cols]
                dmb = dmixed.astype(BF16)
                db_ref[g] += jnp.sum(dmixed, axis=-1, keepdims=True)
                dw_ref[g] += jnp.where(tril, _dot_nt(dmb, vnb[:, cols]), 0.0)
                dvns.append(_dot_tn(ws[g], dmb))
            dvn = jnp.concatenate(dvns, axis=1)
            dgain_ref[...] += jnp.sum(dvn * xhat, axis=0, keepdims=True)
            dxh = dvn * gain_v
            dv = rstd * (dxh - jnp.mean(dxh, axis=-1, keepdims=True)
                         - xhat * jnp.mean(dxh * xhat, axis=-1, keepdims=True))
            duv_ref[rows, :] = jnp.concatenate(dus + [dv], axis=1)

    return pl.pallas_call(
        body, name="mix_b_bwd", grid=(s // tm,),
        in_specs=[pl.BlockSpec((tm, 2 * D_BRANCH), lambda i: (i, 0)),
                  pl.BlockSpec((1, D_BRANCH), lambda i: (0, 0)),
                  pl.BlockSpec((N_HEADS, SG_CHUNK, SG_CHUNK), lambda i: (0, 0, 0)),
                  pl.BlockSpec((N_HEADS, SG_CHUNK, 1), lambda i: (0, 0, 0)),
                  pl.BlockSpec((tm, D_BRANCH), lambda i: (i, 0))],
        out_specs=[pl.BlockSpec((tm, 2 * D_BRANCH), lambda i: (i, 0)),
                   pl.BlockSpec((N_HEADS, SG_CHUNK, SG_CHUNK), lambda i: (0, 0, 0)),
                   pl.BlockSpec((N_HEADS, SG_CHUNK, 1), lambda i: (0, 0, 0)),
                   pl.BlockSpec((1, D_BRANCH), lambda i: (0, 0))],
        out_shape=[jax.ShapeDtypeStruct((s, 2 * D_BRANCH), F32),
                   jax.ShapeDtypeStruct((N_HEADS, SG_CHUNK, SG_CHUNK), F32),
                   jax.ShapeDtypeStruct((N_HEADS, SG_CHUNK, 1), F32),
                   jax.ShapeDtypeStruct((1, D_BRANCH), F32)],
        compiler_params=_params(("arbitrary",)),
    )(uv, gain, w_s, b_col, dy)


def _scan_mats(nrow):
    a = lax.broadcasted_iota(jnp.int32, (128, 128), 0)
    b = lax.broadcasted_iota(jnp.int32, (128, 128), 1)
    r = lax.broadcasted_iota(jnp.int32, (nrow, nrow), 0)
    c = lax.broadcasted_iota(jnp.int32, (nrow, nrow), 1)
    nb = nrow // N_HEADS
    same = (r // nb) == (c // nb)
    return a, b, r, c, same


def _exact_dot(x, m):
    hi, mid, lo = _split3(x)
    return _dot(hi, m) + _dot(mid, m) + _dot(lo, m)


def _exact_dot_left(m, x):
    hi, mid, lo = _split3(x)
    return _dot(m, hi) + _dot(m, mid) + _dot(m, lo)


def fox_gate_fwd(ft, bcol):
    nrow = ft.shape[0]

    def body(f_ref, b_ref, c_ref):
        z = f_ref[...] + b_ref[...]
        ls = jnp.minimum(z, 0.0) - jnp.log(1.0 + jnp.exp(-jnp.abs(z)))
        a, b, r, c, same = _scan_mats(nrow)
        within = _exact_dot(ls, (a <= b).astype(BF16))
        tot = jnp.broadcast_to(within[:, 127:128], within.shape)
        before = _exact_dot_left((same & (c < r)).astype(BF16), tot)
        c_ref[...] = within + before

    return pl.pallas_call(
        body, name="fox_gate_fwd",
        out_shape=jax.ShapeDtypeStruct((nrow, 128), F32),
        compiler_params=_params(),
    )(ft, bcol)


def fox_gate_bwd(ft, bcol, dc):
    nrow = ft.shape[0]

    def body(f_ref, b_ref, dc_ref, df_ref, db_ref):
        a, b, r, c, same = _scan_mats(nrow)
        dcv = dc_ref[...]
        within = _exact_dot(dcv, (a >= b).astype(BF16))
        tot = jnp.broadcast_to(within[:, 0:1], within.shape)
        after = _exact_dot_left((same & (c > r)).astype(BF16), tot)
        dls = within + after
        z = f_ref[...] + b_ref[...]
        dz = dls * _sigmoid(-z)
        df_ref[...] = dz
        rs = jnp.broadcast_to(jnp.sum(dz, axis=-1, keepdims=True), dz.shape)
        hr = lax.broadcasted_iota(jnp.int32, (8, nrow), 0)
        hc = lax.broadcasted_iota(jnp.int32, (8, nrow), 1)
        db_ref[...] = _exact_dot_left((hr == hc // (nrow // N_HEADS)).astype(BF16), rs)

    return pl.pallas_call(
        body, name="fox_gate_bwd",
        out_shape=[jax.ShapeDtypeStruct((nrow, 128), F32), jax.ShapeDtypeStruct((8, 128), F32)],
        compiler_params=_params(),
    )(ft, bcol, dc)


def _att_specs(s, qi, ki, vi):
    t = ATT_T
    q_spec = pl.BlockSpec((None, None, t, HEAD_DIM), lambda h, i: (qi, h, i, 0))
    k_spec = pl.BlockSpec((None, None, s, HEAD_DIM), lambda h, i: (ki, h, 0, 0))
    v_spec = pl.BlockSpec((None, None, s, HEAD_DIM), lambda h, i: (vi, h, 0, 0))
    row_spec = lambda w: pl.BlockSpec((None, t, w), lambda h, i: (h, i, 0))
    return q_spec, k_spec, v_spec, row_spec


def _causal(strict):
    row = lax.broadcasted_iota(jnp.int32, (ATT_T, ATT_T), 0)
    col = lax.broadcasted_iota(jnp.int32, (ATT_T, ATT_T), 1)
    return (col < row) if strict else (col <= row)


def _gate_row(cr_ref, kb, g):
    if g == 1:
        return cr_ref[kb]
    return jnp.concatenate([cr_ref[kb + n] for n in range(g)], axis=1)


def _fox_walk(i, carry, tile, alive):
    g = FOX_WIDE
    nmid = i // FOX_MID
    nwide = i // g
    carry = tile(i, 1, carry, True)
    carry = lax.fori_loop(0, i - nmid * FOX_MID, lambda n, c: tile(i - 1 - n, 1, c, False), carry)
    carry = lax.fori_loop(0, nmid - nwide * (g // FOX_MID), lambda n, c: tile(nwide * g, FOX_MID, c, False), carry)

    def cond(state):
        return jnp.logical_and(state[0] >= 0, state[1] > 0)

    def step(state):
        n = state[0]
        c = tile(n * g, g, state[2:], False)
        return (n - 1, alive(n * g, c)) + tuple(c)

    out = lax.while_loop(cond, step, (nwide - 1, alive(nwide * g, carry)) + tuple(carry))
    return out[2:]


def _fox_reach(qs, k_ref, kmax_ref, cc, i):
    s = k_ref.shape[0]
    rows = 4 * ATT_T

    @pl.when(i == 0)
    def _():
        def chunk(n, mx):
            kc = k_ref[pl.ds(pl.multiple_of(n * rows, rows), rows), :].astype(F32)
            return jnp.maximum(mx, jnp.max(jnp.sum(kc * kc, axis=-1, keepdims=True)))

        kmax_ref[0] = jnp.sqrt(lax.fori_loop(0, s // rows, chunk, jnp.float32(0.0)))

    qf = qs.astype(F32)
    return jnp.sqrt(jnp.sum(qf * qf, axis=-1, keepdims=True)) * kmax_ref[0] + cc


def _gate_col(cr_ref, i):
    row = lax.broadcasted_iota(jnp.int32, (ATT_T, ATT_T), 0)
    col = lax.broadcasted_iota(jnp.int32, (ATT_T, ATT_T), 1)
    return jnp.sum(jnp.where(row == col, cr_ref[i], 0.0), axis=-1, keepdims=True)


def _fox_scores(qs, k, cc, crow, masked):
    sc = (_dot_nt(qs, k) + (cc - crow)) * LOG2E
    if masked:
        sc = jnp.where(_causal(False), sc, NEG)
    return sc


def fox_fwd(qkv, c_row):
    s = qkv.shape[2]
    t = ATT_T
    nq = s // t
    q_spec, k_spec, v_spec, row_spec = _att_specs(s, 1, 2, 3)
    rows = 4 * t

    def body(q_ref, k_ref, v_ref, cr_ref, o_ref, ref_ref, rl_ref, v1_ref, kmax_ref):
        i = pl.program_id(1)

        @pl.when(i == 0)
        def _():
            def chunk(n, carry):
                r0 = pl.multiple_of(n * rows, rows)
                v1_ref[pl.ds(r0, rows), :] = jnp.concatenate(
                    [v_ref[pl.ds(r0, rows), :], jnp.ones((rows, HEAD_DIM), BF16)], axis=1)
                return carry

            lax.fori_loop(0, s // rows, chunk, 0)

        qs = q_ref[...] * 0.125
        cc = _gate_col(cr_ref, i)
        reach = _fox_reach(qs, k_ref, kmax_ref, cc, i) * LOG2E

        def alive(kb, carry):
            return (jnp.max(reach - cr_ref[kb][:, 0:1] * LOG2E - carry[0]) > FOX_DEAD2).astype(jnp.int32)

        def tile(kb, g, carry, masked):
            m, acc = carry
            k0 = pl.multiple_of(kb * t, t)
            sc = _fox_scores(qs, k_ref[pl.ds(k0, g * t), :], cc, _gate_row(cr_ref, kb, g), masked)
            m_new = jnp.maximum(m, jnp.ceil(jnp.max(sc, axis=-1, keepdims=True)))
            pb = jnp.exp2(sc - m_new).astype(BF16)
            acc = jnp.exp2(m - m_new) * acc + _dot(pb, v1_ref[pl.ds(k0, g * t), :])
            return m_new, acc

        init = (jnp.full((t, 1), NEG, F32), jnp.zeros((t, 2 * HEAD_DIM), F32))
        m, acc = _fox_walk(i, init, tile, alive)
        rl = 1.0 / acc[:, HEAD_DIM:HEAD_DIM + 1]
        o_ref[...] = acc[:, 0:HEAD_DIM] * rl
        ref_ref[...] = m
        rl_ref[...] = rl

    return pl.pallas_call(
        body, name="fox_fwd", grid=(N_HEADS, nq),
        in_specs=[q_spec, k_spec, v_spec, pl.BlockSpec((None, nq, 1, t), lambda h, i: (h, 0, 0, 0))],
        out_specs=[row_spec(HEAD_DIM), row_spec(1), row_spec(1)],
        out_shape=[jax.ShapeDtypeStruct((N_HEADS, s, HEAD_DIM), F32),
                   jax.ShapeDtypeStruct((N_HEADS, s, 1), F32),
                   jax.ShapeDtypeStruct((N_HEADS, s, 1), F32)],
        scratch_shapes=[pltpu.VMEM((s, 2 * HEAD_DIM), BF16), pltpu.SMEM((1,), F32)],
        compiler_params=_params(("arbitrary", "arbitrary")),
    )(qkv, qkv, qkv, c_row)


def fox_bwd(qkv, c_row, do, o, ref, rl):
    s = qkv.shape[2]
    t = ATT_T
    nq = s // t
    q_spec, k_spec, v_spec, row_spec = _att_specs(s, 1, 2, 3)
    any_spec = pl.BlockSpec(memory_space=pl.ANY)

    def body(q_ref, k_ref, v_ref, cr_ref, do_ref, o_ref, ref_ref, rl_ref,
             dq_ref, dk_hbm, dv_hbm, dc_ref, dk_acc, dv_acc, kmax_ref):
        h = pl.program_id(0)
        i = pl.program_id(1)

        @pl.when(i == 0)
        def _():
            dk_acc[...] = jnp.zeros_like(dk_acc)
            dv_acc[...] = jnp.zeros_like(dv_acc)
            dc_ref[...] = jnp.zeros_like(dc_ref)

        qs = q_ref[...] * 0.125
        ref = ref_ref[...]
        rl = rl_ref[...]
        dob = (do_ref[...].astype(F32) * rl).astype(BF16)
        delta = jnp.sum(o_ref[...] * dob.astype(F32), axis=-1, keepdims=True)
        cc = _gate_col(cr_ref, i)
        margin = _fox_reach(qs, k_ref, kmax_ref, cc, i) * LOG2E - ref

        def alive(kb, carry):
            return (jnp.max(margin - cr_ref[kb][:, 0:1] * LOG2E) > FOX_DEAD2).astype(jnp.int32)

        def tile(kb, g, carry, masked):
            dq, = carry
            k0 = pl.multiple_of(kb * t, t)
            k = k_ref[pl.ds(k0, g * t), :]
            sc = _fox_scores(qs, k, cc, _gate_row(cr_ref, kb, g), masked)
            wb = jnp.exp2(sc - ref).astype(BF16)
            ds = wb.astype(F32) * (_dot_nt(dob, v_ref[pl.ds(k0, g * t), :]) - delta)
            dsb = ds.astype(BF16)
            dk_acc[pl.ds(k0, g * t), :] += _dot_tn(dsb, qs)
            dv_acc[pl.ds(k0, g * t), :] += _dot_tn(wb, dob)
            dcs = -jnp.sum(ds, axis=0, keepdims=True)
            for n in range(g):
                dc_ref[kb + n] += dcs[:, n * t:(n + 1) * t]
            return (dq + _dot(dsb, k),)

        dq, = _fox_walk(i, (jnp.zeros((t, HEAD_DIM), F32),), tile, alive)
        dq_ref[...] = dq * 0.125

        @pl.when(i == nq - 1)
        def _():
            pltpu.sync_copy(dk_acc, dk_hbm.at[h])
            pltpu.sync_copy(dv_acc, dv_hbm.at[h])

    return pl.pallas_call(
        body, name="fox_bwd", grid=(N_HEADS, nq),
        in_specs=[q_spec, k_spec, v_spec,
                  pl.BlockSpec((None, nq, 1, t), lambda h, i: (h, 0, 0, 0)),
                  row_spec(HEAD_DIM), row_spec(HEAD_DIM), row_spec(1), row_spec(1)],
        out_specs=[row_spec(HEAD_DIM), any_spec, any_spec,
                   pl.BlockSpec((None, nq, 1, t), lambda h, i: (h, 0, 0, 0))],
        out_shape=[jax.ShapeDtypeStruct((N_HEADS, s, HEAD_DIM), F32),
                   jax.ShapeDtypeStruct((N_HEADS, s, HEAD_DIM), F32),
                   jax.ShapeDtypeStruct((N_HEADS, s, HEAD_DIM), F32),
                   jax.ShapeDtypeStruct((N_HEADS, nq, 1, t), F32)],
        scratch_shapes=[pltpu.VMEM((s, HEAD_DIM), F32), pltpu.VMEM((s, HEAD_DIM), F32), pltpu.SMEM((1,), F32)],
        compiler_params=_params(("arbitrary", "arbitrary")),
    )(qkv, qkv, qkv, c_row, do, o, ref, rl)


def _sb_tile(qs, k, run, masked):
    z = _dot_nt(qs, k)
    sp = jnp.log(1.0 + jnp.exp(-jnp.abs(z)))
    ls = jnp.minimum(z, 0.0) - sp
    lm = -jnp.maximum(z, 0.0) - sp
    if masked:
        valid = _causal(True)
        lm = jnp.where(valid, lm, 0.0)
    row = lax.broadcasted_iota(jnp.int32, (ATT_T, ATT_T), 0)
    col = lax.broadcasted_iota(jnp.int32, (ATT_T, ATT_T), 1)
    later = (row > col).astype(BF16)
    hi, lo = _split2(lm)
    between = run + _dot(hi, later) + _dot(lo, later)
    a = jnp.exp(ls + between)
    if masked:
        a = jnp.where(valid, a, 0.0)
    return ls, lm, a


def _sb_walk(i, carry, tile):
    def alive_of(c):
        return (jnp.max(c[0]) > SB_DEAD).astype(jnp.int32)

    def cond(state):
        n, alive = state[0], state[1]
        return jnp.logical_and(n < i, alive > 0)

    def step(state):
        n = state[0]
        c = tile(i - 1 - n, state[2:], False)
        return (n + 1, alive_of(c)) + tuple(c)

    out = lax.while_loop(cond, step, (jnp.int32(0), alive_of(carry)) + tuple(carry))
    return out[2:]


def _sb_specs(s):
    tq = 2 * ATT_T
    q_spec = pl.BlockSpec((None, None, tq, HEAD_DIM), lambda h, i: (4, h, i, 0))
    k_spec = pl.BlockSpec((None, None, s, HEAD_DIM), lambda h, i: (5, h, 0, 0))
    v_spec = pl.BlockSpec((None, None, s, HEAD_DIM), lambda h, i: (6, h, 0, 0))
    row_spec = pl.BlockSpec((None, tq, HEAD_DIM), lambda h, i: (h, i, 0))
    return tq, q_spec, k_spec, v_spec, row_spec


def _sb_block(i, tile, zero):
    t = ATT_T
    lo, hi = slice(0, t), slice(t, 2 * t)
    c_hi = tile(2 * i + 1, hi, zero, True)
    c_lo = tile(2 * i, lo, zero, True)
    c_hi = tile(2 * i, hi, c_hi, False)
    carry = tuple(jnp.concatenate([a, b], axis=0) for a, b in zip(c_lo, c_hi))
    return _sb_walk(2 * i, carry, lambda kb, c, masked: tile(kb, slice(0, 2 * t), c, masked))


def sb_fwd(qkv):
    s = qkv.shape[2]
    t = ATT_T
    tq, q_spec, k_spec, v_spec, row_spec = _sb_specs(s)

    def body(q_ref, k_ref, v_ref, o_ref):
        i = pl.program_id(1)
        qs = q_ref[...] * 0.125

        def tile(kb, rows, carry, masked):
            run, acc = carry
            k0 = pl.multiple_of(kb * t, t)
            _, lm, a = _sb_tile(qs[rows], k_ref[pl.ds(k0, t), :], run, masked)
            acc = acc + _dot(a.astype(BF16), v_ref[pl.ds(k0, t), :])
            return run + jnp.sum(lm, axis=-1, keepdims=True), acc

        _, acc = _sb_block(i, tile, (jnp.zeros((t, 1), F32), jnp.zeros((t, HEAD_DIM), F32)))
        o_ref[...] = acc

    return pl.pallas_call(
        body, name="sb_fwd", grid=(N_HEADS, s // tq),
        in_specs=[q_spec, k_spec, v_spec],
        out_specs=row_spec,
        out_shape=jax.ShapeDtypeStruct((N_HEADS, s, HEAD_DIM), F32),
        compiler_params=_params(("arbitrary", "arbitrary")),
    )(qkv, qkv, qkv)


def sb_bwd(qkv, do, o):
    s = qkv.shape[2]
    t = ATT_T
    tq, q_spec, k_spec, v_spec, row_spec = _sb_specs(s)
    nq = s // tq
    any_spec = pl.BlockSpec(memory_space=pl.ANY)

    def body(q_ref, k_ref, v_ref, do_ref, o_ref, dq_ref, dk_hbm, dv_hbm, dk_acc, dv_acc):
        h = pl.program_id(0)
        i = pl.program_id(1)

        @pl.when(i == 0)
        def _():
            dk_acc[...] = jnp.zeros_like(dk_acc)
            dv_acc[...] = jnp.zeros_like(dv_acc)

        qs_all = q_ref[...] * 0.125
        dob_all = do_ref[...]
        tot_all = jnp.sum(o_ref[...] * dob_all.astype(F32), axis=-1, keepdims=True)

        def tile(kb, rows, carry, masked):
            run, run_g, dq = carry
            qs, dob, tot = qs_all[rows], dob_all[rows], tot_all[rows]
            k0 = pl.multiple_of(kb * t, t)
            k = k_ref[pl.ds(k0, t), :]
            ls, lm, a = _sb_tile(qs, k, run, masked)
            ab = a.astype(BF16)
            g = ab.astype(F32) * _dot_nt(dob, v_ref[pl.ds(k0, t), :])
            row = lax.broadcasted_iota(jnp.int32, (t, t), 0)
            col = lax.broadcasted_iota(jnp.int32, (t, t), 1)
            from_here = (row >= col).astype(BF16)
            hi, lo = _split2(g)
            g_right = run_g + _dot(hi, from_here) + _dot(lo, from_here)
            g_left = tot - g_right
            dz = g - jnp.exp(ls) * (g + g_left)
            if masked:
                dz = jnp.where(_causal(True), dz, 0.0)
            dzb = dz.astype(BF16)
            dk_acc[pl.ds(k0, t), :] += _dot_tn(dzb, qs)
            dv_acc[pl.ds(k0, t), :] += _dot_tn(ab, dob)
            return (run + jnp.sum(lm, axis=-1, keepdims=True),
                    run_g + jnp.sum(g, axis=-1, keepdims=True),
                    dq + _dot(dzb, k))

        zero = jnp.zeros((t, 1), F32)
        _, _, dq = _sb_block(i, tile, (zero, zero, jnp.zeros((t, HEAD_DIM), F32)))
        dq_ref[...] = dq * 0.125

        @pl.when(i == nq - 1)
        def _():
            pltpu.sync_copy(dk_acc, dk_hbm.at[h])
            pltpu.sync_copy(dv_acc, dv_hbm.at[h])

    return pl.pallas_call(
        body, name="sb_bwd", grid=(N_HEADS, nq),
        in_specs=[q_spec, k_spec, v_spec, row_spec, row_spec],
        out_specs=[row_spec, any_spec, any_spec],
        out_shape=[jax.ShapeDtypeStruct((N_HEADS, s, HEAD_DIM), F32)] * 3,
        scratch_shapes=[pltpu.VMEM((s, HEAD_DIM), F32), pltpu.VMEM((s, HEAD_DIM), F32)],
        compiler_params=_params(("arbitrary", "arbitrary")),
    )(qkv, qkv, qkv, do, o)


def _branch_inputs(refs, br):
    ya_ref, yb_ref, yc_ref, yd_ref = refs
    if br == 1:
        return yb_ref[...]
    return _heads_to_lanes((ya_ref, None, yc_ref, yd_ref)[br])


def outproj_fwd(x, ya, yb, yc, yd, gates, bg, wout):
    s = x.shape[0]
    tm = min(ROW_T, s)

    def body(x_ref, ya_ref, yb_ref, yc_ref, yd_ref, gates_ref, bg_ref, w_ref, out_ref):
        pieces = []
        for br in range(4):
            cols = slice(br * D_BRANCH, (br + 1) * D_BRANCH)
            y = _branch_inputs((ya_ref, yb_ref, yc_ref, yd_ref), br)
            r = lax.rsqrt(jnp.mean(y * y, axis=-1, keepdims=True) + EPS)
            gt = gates_ref[:, cols]
            pieces.append((y * r * bg_ref[:, cols]) * (gt * _sigmoid(gt)))
        merged = jnp.concatenate(pieces, axis=1).astype(BF16)
        out_ref[...] = x_ref[...] + _dot(merged, w_ref[...])

    head_spec = pl.BlockSpec((N_HEADS, tm, HEAD_DIM), lambda i: (0, i, 0))
    return pl.pallas_call(
        body, name="outproj_fwd", grid=(s // tm,),
        in_specs=[pl.BlockSpec((tm, D_MODEL), lambda i: (i, 0)),
                  head_spec, pl.BlockSpec((tm, D_BRANCH), lambda i: (i, 0)), head_spec, head_spec,
                  pl.BlockSpec((tm, D_MODEL), lambda i: (i, 0)),
                  pl.BlockSpec((1, D_MODEL), lambda i: (0, 0)),
                  pl.BlockSpec((D_MODEL, D_MODEL), lambda i: (0, 0))],
        out_specs=pl.BlockSpec((tm, D_MODEL), lambda i: (i, 0)),
        out_shape=jax.ShapeDtypeStruct((s, D_MODEL), F32),
        compiler_params=_params(("arbitrary",)),
    )(x, ya, yb, yc, yd, gates, bg, wout)


def outproj_bwd(dout, ya, yb, yc, yd, gates, bg, wout):
    s = dout.shape[0]
    tm = min(ROW_T, s)

    def body(dout_ref, ya_ref, yb_ref, yc_ref, yd_ref, gates_ref, bg_ref, w_ref,
             dya_ref, dyb_ref, dyc_ref, dyd_ref, dgates_ref, dbg_ref, dw_ref):
        i = pl.program_id(0)

        @pl.when(i == 0)
        def _():
            dbg_ref[...] = jnp.zeros_like(dbg_ref)
            dw_ref[...] = jnp.zeros_like(dw_ref)

        doutb = dout_ref[...].astype(BF16)
        dmerged = _dot_nt(doutb, w_ref[...])
        pieces = []
        for br in range(4):
            cols = slice(br * D_BRANCH, (br + 1) * D_BRANCH)
            y = _branch_inputs((ya_ref, yb_ref, yc_ref, yd_ref), br)
            r = lax.rsqrt(jnp.mean(y * y, axis=-1, keepdims=True) + EPS)
            yn = y * r
            bgv = bg_ref[:, cols]
            gt = gates_ref[:, cols]
            sig = _sigmoid(gt)
            act = gt * sig
            n = yn * bgv
            pieces.append(n * act)
            dm = dmerged[:, cols]
            dn = dm * act
            dgates_ref[:, cols] = (dm * n * (sig * (1.0 + gt * (1.0 - sig)))).astype(BF16)
            dbg_ref[:, cols] += jnp.sum(dn * yn, axis=0, keepdims=True)
            u = dn * bgv
            dy = r * (u - yn * jnp.mean(yn * u, axis=-1, keepdims=True))
            if br == 1:
                dyb_ref[...] = dy
            else:
                dref = (dya_ref, None, dyc_ref, dyd_ref)[br]
                for hh in range(N_HEADS):
                    dref[hh] = dy[:, hh * HEAD_DIM:(hh + 1) * HEAD_DIM].astype(BF16)
        merged = jnp.concatenate(pieces, axis=1).astype(BF16)
        dw_ref[...] += _dot_tn(merged, doutb)

    head_spec = pl.BlockSpec((N_HEADS, tm, HEAD_DIM), lambda i: (0, i, 0))
    head_shape = jax.ShapeDtypeStruct((N_HEADS, s, HEAD_DIM), BF16)
    return pl.pallas_call(
        body, name="outproj_bwd", grid=(s // tm,),
        in_specs=[pl.BlockSpec((tm, D_MODEL), lambda i: (i, 0)),
                  head_spec, pl.BlockSpec((tm, D_BRANCH), lambda i: (i, 0)), head_spec, head_spec,
                  pl.BlockSpec((tm, D_MODEL), lambda i: (i, 0)),
                  pl.BlockSpec((1, D_MODEL), lambda i: (0, 0)),
                  pl.BlockSpec((D_MODEL, D_MODEL), lambda i: (0, 0))],
        out_specs=[head_spec, pl.BlockSpec((tm, D_BRANCH), lambda i: (i, 0)), head_spec, head_spec,
                   pl.BlockSpec((tm, D_MODEL), lambda i: (i, 0)),
                   pl.BlockSpec((1, D_MODEL), lambda i: (0, 0)),
                   pl.BlockSpec((D_MODEL, D_MODEL), lambda i: (0, 0))],
        out_shape=[head_shape, jax.ShapeDtypeStruct((s, D_BRANCH), F32), head_shape, head_shape,
                   jax.ShapeDtypeStruct((s, D_MODEL), BF16),
                   jax.ShapeDtypeStruct((1, D_MODEL), F32),
                   jax.ShapeDtypeStruct((D_MODEL, D_MODEL), F32)],
        compiler_params=_params(("arbitrary",)),
    )(dout, ya, yb, yc, yd, gates, bg, wout)


def final_loss(x, tgt, g):
    s = x.shape[0]
    tm = min(ROW_T, s)

    def body(x_ref, t_ref, g_ref, loss_ref, dx_ref, dg_ref):
        i = pl.program_id(0)

        @pl.when(i == 0)
        def _():
            loss_ref[...] = jnp.zeros_like(loss_ref)
            dg_ref[...] = jnp.zeros_like(dg_ref)

        xv = x_ref[...]
        gv = g_ref[...]
        r = lax.rsqrt(jnp.mean(xv * xv, axis=-1, keepdims=True) + EPS)
        xn = xv * r
        err = xn * gv - t_ref[...]
        loss_ref[...] += jnp.sum(err * err) * (0.5 / D_MODEL)
        dy = err * (1.0 / D_MODEL)
        u = dy * gv
        dx_ref[...] = r * (u - xn * jnp.mean(xn * u, axis=-1, keepdims=True))
        dg_ref[...] += jnp.sum(dy * xn, axis=0, keepdims=True)

    return pl.pallas_call(
        body, name="final_loss", grid=(s // tm,),
        in_specs=[pl.BlockSpec((tm, D_MODEL), lambda i: (i, 0)),
                  pl.BlockSpec((tm, D_MODEL), lambda i: (i, 0)),
                  pl.BlockSpec((1, D_MODEL), lambda i: (0, 0))],
        out_specs=[pl.BlockSpec((1, 128), lambda i: (0, 0)),
                   pl.BlockSpec((tm, D_MODEL), lambda i: (i, 0)),
                   pl.BlockSpec((1, D_MODEL), lambda i: (0, 0))],
        out_shape=[jax.ShapeDtypeStruct((1, 128), F32),
                   jax.ShapeDtypeStruct((s, D_MODEL), F32),
                   jax.ShapeDtypeStruct((1, D_MODEL), F32)],
        compiler_params=_params(("arbitrary",)),
    )(x, tgt, g)


def _rel_index():
    i = np.arange(A_TQ)[:, None]
    j = np.arange(A_BAND)[None, :]
    rel = np.clip(i - j + (A_BAND - A_TQ), -MAX_REL, MAX_REL) + MAX_REL
    dchunk = i // CHUNK + LOOKBACK - j // CHUNK
    valid = (dchunk >= 0) & (dchunk <= LOOKBACK)
    return jnp.asarray(np.where(valid, rel, -1).astype(np.int32))


def _layer_consts(p):
    tbias = relbias_tile(p["rel_bias"], _rel_index())
    return dict(
        norm_g=p["norm_g"].reshape(1, D_MODEL),
        v_gain=p["v_gain"].reshape(1, D_BRANCH),
        b_col=p["b_s"].reshape(N_HEADS, SG_CHUNK, 1),
        bg=p["branch_gain"].reshape(1, D_MODEL),
        tbias=tbias,
    )


def _gate_layout(fp, b_f, s):
    nb = s // 128
    ft = fp[:, :N_HEADS].T.reshape(N_HEADS * nb, 128)
    bcol = jnp.repeat(b_f, nb).reshape(N_HEADS * nb, 1)
    return ft, bcol


def layer_fwd(x, p):
    s = x.shape[0]
    c = _layer_consts(p)
    h, qkv, kva, gates, uv, fp = inproj_fwd(x, c["norm_g"], p["wp"])
    ya, lse_a = mix_a_fwd(qkv, kva, c["tbias"])
    yb = mix_b_fwd(uv, c["v_gain"], p["w_s"], c["b_col"])
    ft, bcol = _gate_layout(fp, p["b_f"], s)
    c_row = fox_gate_fwd(ft, bcol).reshape(N_HEADS, s // ATT_T, 1, ATT_T)
    yc, ref_c, rl_c = fox_fwd(qkv, c_row)
    yd = sb_fwd(qkv)
    out = outproj_fwd(x, ya, yb, yc, yd, gates, c["bg"], p["wout"])
    saved = dict(consts=c, x=x, h=h, qkv=qkv, gates=gates, uv=uv, kva=kva, ft=ft, bcol=bcol,
                 c_row=c_row, ya=ya, lse_a=lse_a, yb=yb, yc=yc, ref_c=ref_c, rl_c=rl_c, yd=yd)
    return out, saved


def layer_bwd(dout, p, sv):
    s = dout.shape[0]
    c = sv["consts"]
    dya, dyb, dyc, dyd, dgates, dbg, dwout = outproj_bwd(
        dout, sv["ya"], sv["yb"], sv["yc"], sv["yd"], sv["gates"], c["bg"], p["wout"])
    dqa, dka, dva, dt = mix_a_bwd(sv["qkv"], sv["kva"], c["tbias"], dya, sv["ya"], sv["lse_a"])
    drel = relbias_grad(dt, _rel_index())[:N_HEADS, :2 * MAX_REL + 1]
    duv, dws, dbs, dvgain = mix_b_bwd(sv["uv"], c["v_gain"], p["w_s"], c["b_col"], dyb)
    dqc, dkc, dvc, dc = fox_bwd(sv["qkv"], sv["c_row"], dyc, sv["yc"], sv["ref_c"], sv["rl_c"])
    dft, dbf = fox_gate_bwd(sv["ft"], sv["bcol"], dc.reshape(N_HEADS * (s // 128), 128))
    dfp = jnp.pad(dft.reshape(N_HEADS, s).T, ((0, 0), (0, 128 - N_HEADS)))
    dqd, dkd, dvd = sb_bwd(sv["qkv"], dyd, sv["yd"])
    dp, dx, dnorm = inproj_bwd((dqa, dka, dva, dqc, dkc, dvc, dqd, dkd, dvd), dgates, duv, dfp,
                               p["wp"], sv["x"], c["norm_g"], dout)
    dwp = weight_grad(sv["h"], dp, "inproj_wgrad")
    grads = dict(norm_g=dnorm.reshape(D_MODEL), wp=dwp, b_f=dbf[:N_HEADS, 0], rel_bias=drel,
                 w_s=dws, b_s=dbs.reshape(N_HEADS, SG_CHUNK), v_gain=dvgain.reshape(D_BRANCH),
                 branch_gain=dbg.reshape(4, D_BRANCH), wout=dwout)
    return dx, grads


def local_step(x, tgt, layers, final_g):
    saved = []
    cur = x
    for p in layers:
        cur, sv = layer_fwd(cur, p)
        saved.append(sv)
    loss, dcur, dfinal = final_loss(cur, tgt, final_g.reshape(1, D_MODEL))
    grads = [None] * len(layers)
    for l in reversed(range(len(layers))):
        dcur, grads[l] = layer_bwd(dcur, layers[l], saved[l])
    return loss[0, 0], dcur, grads, dfinal.reshape(D_MODEL)


def gather_weights(wb, wf):
    def body(wb_ref, wf_ref, ob_ref, of_ref, send_sems, recv_sems, loc_sems):
        x, y, c = lax.axis_index("x"), lax.axis_index("y"), lax.axis_index("c")
        me = 2 * x + y
        chips = [(1 - x, y), (x, 1 - y), (1 - x, 1 - y)]
        pairs = [(wb_ref, ob_ref), (wf_ref, of_ref)]
        local = [pltpu.make_async_copy(src, dst.at[me], loc_sems.at[n]) for n, (src, dst) in enumerate(pairs)]
        for cp in local:
            cp.start()

        def copy(j, n, slot):
            src, dst = pairs[n]
            return pltpu.make_async_remote_copy(
                src_ref=src, dst_ref=dst.at[slot], send_sem=send_sems.at[2 * j + n], recv_sem=recv_sems.at[2 * j + n],
                device_id=(chips[j][0], chips[j][1], c), device_id_type=MESH)

        sends = [copy(j, n, me) for j in range(3) for n in range(2)]
        for cp in sends:
            cp.start()
        for j in range(3):
            for n in range(2):
                copy(j, n, 2 * chips[j][0] + chips[j][1]).wait_recv()
        for cp in sends:
            cp.wait_send()
        for cp in local:
            cp.wait()

    any_spec = pl.BlockSpec(memory_space=pl.ANY)
    return pl.pallas_call(
        body, name="gather_weights",
        in_specs=[any_spec, any_spec], out_specs=[any_spec, any_spec],
        out_shape=[jax.ShapeDtypeStruct((4,) + wb.shape, wb.dtype), jax.ShapeDtypeStruct((4,) + wf.shape, wf.dtype)],
        scratch_shapes=[pltpu.SemaphoreType.DMA((6,)), pltpu.SemaphoreType.DMA((6,)), pltpu.SemaphoreType.DMA((2,))],
    )(wb, wf)


def exchange_grads(big, small):
    def body(b_ref, s_ref, rb_ref, rs_ref, send_sems, recv_sems, loc_sems):
        x, y, c = lax.axis_index("x"), lax.axis_index("y"), lax.axis_index("c")
        me_chip = 2 * x + y
        me = 4 * x + 2 * y + c
        peers = [(x, y, 1 - c)]
        for px, py in [(1 - x, y), (x, 1 - y), (1 - x, 1 - y)]:
            peers += [(px, py, c), (px, py, 1 - c)]
        local = [pltpu.make_async_copy(b_ref.at[me_chip], rb_ref.at[me], loc_sems.at[0]),
                 pltpu.make_async_copy(s_ref, rs_ref.at[me], loc_sems.at[1])]
        for cp in local:
            cp.start()

        def copies(n, chip, slot):
            kw = dict(device_id=peers[n], device_id_type=MESH)
            return [pltpu.make_async_remote_copy(src_ref=b_ref.at[chip], dst_ref=rb_ref.at[slot],
                                                 send_sem=send_sems.at[2 * n], recv_sem=recv_sems.at[2 * n], **kw),
                    pltpu.make_async_remote_copy(src_ref=s_ref, dst_ref=rs_ref.at[slot],
                                                 send_sem=send_sems.at[2 * n + 1], recv_sem=recv_sems.at[2 * n + 1], **kw)]

        sends = [cp for n, (px, py, _) in enumerate(peers) for cp in copies(n, 2 * px + py, me)]
        for cp in sends:
            cp.start()
        for n, (px, py, pc) in enumerate(peers):
            for cp in copies(n, me_chip, 4 * px + 2 * py + pc):
                cp.wait_recv()
        for cp in sends:
            cp.wait_send()
        for cp in local:
            cp.wait()

    any_spec = pl.BlockSpec(memory_space=pl.ANY)
    return pl.pallas_call(
        body, name="exchange_grads",
        in_specs=[any_spec, any_spec], out_specs=[any_spec, any_spec],
        out_shape=[jax.ShapeDtypeStruct((8,) + big.shape[1:], big.dtype),
                   jax.ShapeDtypeStruct((8,) + small.shape, small.dtype)],
        scratch_shapes=[pltpu.SemaphoreType.DMA((14,)), pltpu.SemaphoreType.DMA((14,)), pltpu.SemaphoreType.DMA((2,))],
    )(big, small)


def adamw_reduce(parts, w, m, v, name):
    rows = w.shape[0]
    tr = PACK_ROW_TILE
    c1 = 1.0 - ADAM_B1 ** ADAM_STEP
    c2 = 1.0 - ADAM_B2 ** ADAM_STEP

    def body(p_ref, w_ref, m_ref, v_ref, g_ref, d_ref, nm_ref, nv_ref):
        g = p_ref[0].astype(F32)
        for n in range(1, 8):
            g = g + p_ref[n].astype(F32)
        g_ref[...] = g
        nm = ADAM_B1 * m_ref[...] + (1.0 - ADAM_B1) * g
        nv = ADAM_B2 * v_ref[...] + (1.0 - ADAM_B2) * (g * g)
        nm_ref[...] = nm
        nv_ref[...] = nv
        d_ref[...] = -ADAM_LR * ((nm / c1) / (jnp.sqrt(nv / c2) + ADAM_EPS) + ADAM_WD * w_ref[...])

    spec = pl.BlockSpec((tr, 128), lambda i: (i, 0))
    shape = jax.ShapeDtypeStruct((rows, 128), F32)
    return pl.pallas_call(
        body, name=name, grid=(rows // tr,),
        in_specs=[pl.BlockSpec((8, tr, 128), lambda i: (0, i, 0)), spec, spec, spec],
        out_specs=[spec] * 4, out_shape=[shape] * 4,
        compiler_params=_params(("arbitrary",)),
    )(parts, w, m, v)


SHARDED = ("w_in", "w_out", "branch_gain")
SMALL = ("norm_g", "b_f", "rel_bias", "w_s", "b_s", "v_gain", "final_g")
WEIGHTS = ("norm_g", "w_in", "b_f", "rel_bias", "w_s", "b_s", "v_gain", "branch_gain", "w_out", "final_g")
PACK_ROW_TILE = 512


def _rows_of(shape):
    return -(-int(np.prod(shape)) // 128)


def _pack(leaves):
    parts = []
    for a in leaves:
        flat = a.reshape(-1).astype(F32)
        parts.append(jnp.pad(flat, (0, _rows_of(a.shape) * 128 - flat.shape[0])))
    flat = jnp.concatenate(parts)
    rows = flat.shape[0] // 128
    total = -(-rows // PACK_ROW_TILE) * PACK_ROW_TILE
    return jnp.pad(flat, (0, (total - rows) * 128)).reshape(total, 128)


def _unpack(slab, shapes):
    out, row = [], 0
    for shp in shapes:
        n = int(np.prod(shp))
        r = _rows_of(shp)
        out.append(slab[row:row + r].reshape(-1)[:n].reshape(shp))
        row += r
    return out


def _pack_w_in(w):
    return jnp.concatenate([w[:, :2816], w[:, 2820:], w[:, 2816:2820],
                            jnp.zeros((w.shape[0], N_PACK - N_IN), w.dtype)], axis=1)


def _unpack_w_in(wp):
    return jnp.concatenate([wp[:, :2816], wp[:, F_COL:F_COL + N_HEADS], wp[:, 2816:F_COL]], axis=1)


def kernel(x, norm_g, w_in, b_f, rel_bias, w_s, b_s, v_gain, branch_gain, w_out, final_g, loss_target, m_norm_g, m_w_in, m_b_f, m_rel_bias, m_w_s, m_b_s, m_v_gain, m_branch_gain, m_w_out, m_final_g, v_norm_g, v_w_in, v_b_f, v_rel_bias, v_w_s, v_b_s, v_v_gain, v_branch_gain, v_w_out, v_final_g):
    depth = norm_g.shape[0]
    weights = dict(norm_g=norm_g, w_in=w_in, b_f=b_f, rel_bias=rel_bias, w_s=w_s, b_s=b_s, v_gain=v_gain,
                   branch_gain=branch_gain, w_out=w_out, final_g=final_g)
    mom1 = dict(norm_g=m_norm_g, w_in=m_w_in, b_f=m_b_f, rel_bias=m_rel_bias, w_s=m_w_s, b_s=m_b_s,
                v_gain=m_v_gain, branch_gain=m_branch_gain, w_out=m_w_out, final_g=m_final_g)
    mom2 = dict(norm_g=v_norm_g, w_in=v_w_in, b_f=v_b_f, rel_bias=v_rel_bias, w_s=v_w_s, b_s=v_b_s,
                v_gain=v_v_gain, branch_gain=v_branch_gain, w_out=v_w_out, final_g=v_final_g)

    n_in_rows = _rows_of(w_in.shape)
    n_out_rows = _rows_of(w_out.shape)
    wb = jnp.concatenate([w_in.astype(BF16).reshape(n_in_rows, 128), w_out.astype(BF16).reshape(n_out_rows, 128)])
    wf = jnp.pad(branch_gain.reshape(-1), (0, 8 * 128 - branch_gain.size)).reshape(8, 128)
    gb, gf = gather_weights(wb, wf)
    w_in_full = gb[:, :n_in_rows].reshape((4,) + w_in.shape)
    w_in_full = jnp.moveaxis(w_in_full, 0, 2).reshape(depth, D_MODEL, N_IN)
    w_out_full = gb[:, n_in_rows:].reshape((4,) + w_out.shape)
    w_out_full = jnp.moveaxis(w_out_full, 0, 1).reshape(depth, D_MODEL, D_MODEL)
    bg_full = gf.reshape(4, -1)[:, :branch_gain.size].reshape((4,) + branch_gain.shape)
    bg_full = jnp.moveaxis(bg_full, 0, 2).reshape(depth, 4, D_BRANCH)

    layers = [dict(norm_g=norm_g[l], wp=_pack_w_in(w_in_full[l]), b_f=b_f[l], rel_bias=rel_bias[l], w_s=w_s[l],
                   b_s=b_s[l], v_gain=v_gain[l], branch_gain=bg_full[l], wout=w_out_full[l]) for l in range(depth)]

    loss_part, grad_x, lgrads, dfinal = local_step(x[0], loss_target[0], layers, final_g)
    loss = lax.psum(loss_part, ("x", "y", "c"))

    stack = lambda k: jnp.stack([g[k] for g in lgrads])
    d_w_in = jnp.stack([_unpack_w_in(g["wp"]) for g in lgrads])
    d_w_out = stack("wout")
    d_bg = stack("branch_gain")
    small = dict(norm_g=stack("norm_g"), b_f=stack("b_f"), rel_bias=stack("rel_bias"), w_s=stack("w_s"),
                 b_s=stack("b_s"), v_gain=stack("v_gain"), final_g=dfinal)
    slabs = []
    for sidx in range(4):
        slabs.append(_pack([d_w_in[:, :, sidx * N_SHARD:(sidx + 1) * N_SHARD],
                            d_w_out[:, sidx * D_BRANCH:(sidx + 1) * D_BRANCH, :],
                            d_bg[:, :, sidx * HEAD_DIM:(sidx + 1) * HEAD_DIM]]).astype(BF16))
    big_parts, small_parts = exchange_grads(jnp.stack(slabs), _pack([small[k] for k in SMALL]))

    outs = {}
    for names, parts, name in ((SHARDED, big_parts, "adamw_big"), (SMALL, small_parts, "adamw_small")):
        pack_local = lambda d: _pack([d[k] for k in names])
        slabs = adamw_reduce(parts, pack_local(weights), pack_local(mom1), pack_local(mom2), name)
        shapes = [weights[k].shape for k in names]
        for tag, slab in zip(("grad", "delta", "new_m", "new_v"), slabs):
            for k, a in zip(names, _unpack(slab, shapes)):
                outs[tag, k] = a
    result = [loss, grad_x[None]]
    for tag in ("grad", "delta", "new_m", "new_v"):
        result += [outs[tag, k] for k in WEIGHTS]
    return tuple(result)
```

```python
import functools

import jax
import jax.numpy as jnp
import numpy as np
from jax import lax
from jax.experimental import pallas as pl
from jax.experimental.pallas import tpu as pltpu

F32 = jnp.float32
BF16 = jnp.bfloat16
MESH = pl.DeviceIdType.MESH

D_MODEL = 1024
D_BRANCH = 256
N_HEADS = 4
HEAD_DIM = 64
CHUNK = 64
LOOKBACK = 8
MAX_REL = 128
SG_CHUNK = 128
EPS = 1e-6
N_IN = 3844
N_PACK = 3968
F_COL = 3840
N_SHARD = 961
NEG = -1e30

A_TQ = 128
A_BAND = A_TQ + LOOKBACK * CHUNK
REL_LO = MAX_REL - (CHUNK - 1)
REL_HI = 2 * MAX_REL + 1
A_QB = 512
ATT_T = 256
FOX_TQ = 512
FOX_WIDE = 4
FOX_DEAD2 = -160.0
LOG2E = 1.4426950408889634
SB_DEAD = -110.0
ROW_T = 512
VMEM_LIMIT = 56 * 1024 * 1024

ADAM_LR = 0.001
ADAM_B1 = 0.9
ADAM_B2 = 0.999
ADAM_EPS = 1e-08
ADAM_WD = 0.01
ADAM_STEP = 10

SEC_A_Q, SEC_A_K, SEC_A_V, SEC_A_G = 0, 256, 512, 768
SEC_B_U, SEC_B_V, SEC_B_G = 1024, 1280, 1536
SEC_C_Q, SEC_C_K, SEC_C_V, SEC_C_G = 1792, 2048, 2304, 2560
SEC_D_Q, SEC_D_K, SEC_D_V, SEC_D_G = 2816, 3072, 3328, 3584
QKV_SECS = (SEC_A_Q, SEC_C_Q, SEC_C_K, SEC_C_V, SEC_D_Q, SEC_D_K, SEC_D_V)
GATE_SECS = (SEC_A_G, SEC_B_G, SEC_C_G, SEC_D_G)


def _dot(a, b):
    return jnp.dot(a, b, preferred_element_type=F32)


def _dot_nt(a, b):
    return lax.dot_general(a, b, (((1,), (1,)), ((), ())), preferred_element_type=F32)


def _dot_tn(a, b):
    return lax.dot_general(a, b, (((0,), (0,)), ((), ())), preferred_element_type=F32)


def _split2(x):
    hi = x.astype(BF16)
    lo = (x - hi.astype(F32)).astype(BF16)
    return hi, lo


def _split3(x):
    hi = x.astype(BF16)
    r = x - hi.astype(F32)
    mid = r.astype(BF16)
    lo = (r - mid.astype(F32)).astype(BF16)
    return hi, mid, lo


def _sigmoid(x):
    return 1.0 / (1.0 + jnp.exp(-x))


def _params(sem=None, vmem=VMEM_LIMIT):
    return pltpu.CompilerParams(dimension_semantics=sem, vmem_limit_bytes=vmem)


def _heads_to_lanes(ref):
    return jnp.concatenate([ref[h] for h in range(N_HEADS)], axis=1)


def inproj_fwd(x, g, wp):
    s = x.shape[0]
    tm = A_QB

    def body(x_ref, g_ref, w_ref, h_ref, qkv_ref, kva_ref, gates_ref, uv_ref, f_ref):
        xv = x_ref[...]
        r = lax.rsqrt(jnp.mean(xv * xv, axis=-1, keepdims=True) + EPS)
        h = (xv * r * g_ref[...]).astype(BF16)
        h_ref[...] = h
        for n, off in enumerate(QKV_SECS):
            p = _dot(h, w_ref[:, off:off + D_BRANCH])
            for hh in range(N_HEADS):
                qkv_ref[n, hh] = p[:, hh * HEAD_DIM:(hh + 1) * HEAD_DIM].astype(BF16)
        for n, off in enumerate((SEC_A_K, SEC_A_V)):
            p = _dot(h, w_ref[:, off:off + D_BRANCH])
            for hh in range(N_HEADS):
                kva_ref[n, hh] = p[:, hh * HEAD_DIM:(hh + 1) * HEAD_DIM].astype(BF16)
        for n, off in enumerate(GATE_SECS):
            gates_ref[:, n * D_BRANCH:(n + 1) * D_BRANCH] = _dot(h, w_ref[:, off:off + D_BRANCH])
        uv_ref[...] = _dot(h, w_ref[:, SEC_B_U:SEC_B_U + 2 * D_BRANCH])
        f_ref[...] = _dot(h, w_ref[:, F_COL:F_COL + 128])

    return pl.pallas_call(
        body, name="inproj_fwd", grid=(s // tm,),
        in_specs=[pl.BlockSpec((tm, D_MODEL), lambda i: (i, 0)),
                  pl.BlockSpec((1, D_MODEL), lambda i: (0, 0)),
                  pl.BlockSpec((D_MODEL, N_PACK), lambda i: (0, 0))],
        out_specs=[pl.BlockSpec((tm, D_MODEL), lambda i: (i, 0)),
                   pl.BlockSpec((len(QKV_SECS), N_HEADS, tm, HEAD_DIM), lambda i: (0, 0, i, 0)),
                   pl.BlockSpec((2, N_HEADS, tm, HEAD_DIM), lambda i: (0, 0, i + 1, 0)),
                   pl.BlockSpec((tm, D_MODEL), lambda i: (i, 0)),
                   pl.BlockSpec((tm, 2 * D_BRANCH), lambda i: (i, 0)),
                   pl.BlockSpec((tm, 128), lambda i: (i, 0))],
        out_shape=[jax.ShapeDtypeStruct((s, D_MODEL), BF16),
                   jax.ShapeDtypeStruct((len(QKV_SECS), N_HEADS, s, HEAD_DIM), BF16),
                   jax.ShapeDtypeStruct((2, N_HEADS, s + tm, HEAD_DIM), BF16),
                   jax.ShapeDtypeStruct((s, D_MODEL), F32),
                   jax.ShapeDtypeStruct((s, 2 * D_BRANCH), F32),
                   jax.ShapeDtypeStruct((s, 128), F32)],
        compiler_params=_params(("arbitrary",)),
    )(x, g, wp)


def inproj_bwd(dqkv, dgates, duv, dfp, wp, x, g, dres):
    s = x.shape[0]
    tm = min(ROW_T, s)

    def body(*refs):
        dq_refs = refs[:9]
        dgates_ref, duv_ref, dfp_ref, w_ref, x_ref, g_ref, dres_ref, dp_ref, dx_ref, dg_ref = refs[9:]
        i = pl.program_id(0)
        a_q, a_k, a_v, c_q, c_k, c_v, d_q, d_k, d_v = [_heads_to_lanes(r).astype(BF16) for r in dq_refs]
        dgt = dgates_ref[...]
        duv_b = duv_ref[...].astype(BF16)
        dp = jnp.concatenate(
            [a_q, a_k, a_v, dgt[:, 0:256], duv_b, dgt[:, 256:512], c_q, c_k, c_v, dgt[:, 512:768],
             d_q, d_k, d_v, dgt[:, 768:1024], dfp_ref[...].astype(BF16)], axis=1)
        dp_ref[...] = dp
        dh = _dot_nt(dp, w_ref[...])
        xv = x_ref[...]
        r = lax.rsqrt(jnp.mean(xv * xv, axis=-1, keepdims=True) + EPS)
        xn = xv * r
        u = dh * g_ref[...]
        dx_ref[...] = dres_ref[...] + r * (u - xn * jnp.mean(xn * u, axis=-1, keepdims=True))

        @pl.when(i == 0)
        def _():
            dg_ref[...] = jnp.zeros_like(dg_ref)

        dg_ref[...] += jnp.sum(dh * xn, axis=0, keepdims=True)

    head_spec = pl.BlockSpec((N_HEADS, tm, HEAD_DIM), lambda i: (0, i, 0))
    return pl.pallas_call(
        body, name="inproj_bwd", grid=(s // tm,),
        in_specs=[head_spec] * 9 + [
            pl.BlockSpec((tm, D_MODEL), lambda i: (i, 0)),
            pl.BlockSpec((tm, 2 * D_BRANCH), lambda i: (i, 0)),
            pl.BlockSpec((tm, 128), lambda i: (i, 0)),
            pl.BlockSpec((D_MODEL, N_PACK), lambda i: (0, 0)),
            pl.BlockSpec((tm, D_MODEL), lambda i: (i, 0)),
            pl.BlockSpec((1, D_MODEL), lambda i: (0, 0)),
            pl.BlockSpec((tm, D_MODEL), lambda i: (i, 0))],
        out_specs=[pl.BlockSpec((tm, N_PACK), lambda i: (i, 0)),
                   pl.BlockSpec((tm, D_MODEL), lambda i: (i, 0)),
                   pl.BlockSpec((1, D_MODEL), lambda i: (0, 0))],
        out_shape=[jax.ShapeDtypeStruct((s, N_PACK), BF16),
                   jax.ShapeDtypeStruct((s, D_MODEL), F32),
                   jax.ShapeDtypeStruct((1, D_MODEL), F32)],
        compiler_params=_params(("arbitrary",)),
    )(*dqkv, dgates, duv, dfp, wp, x, g, dres)


def weight_grad(a, b, name):
    s, m = a.shape
    n = b.shape[1]
    tm = min(2 * ROW_T, s)
    tmm = 256
    nsteps = s // tm

    def body(a_ref, b_ref, o_ref):
        k = pl.program_id(1)

        @pl.when(k == 0)
        def _():
            o_ref[...] = jnp.zeros_like(o_ref)

        o_ref[...] += _dot_tn(a_ref[...], b_ref[...])

    return pl.pallas_call(
        body, name=name, grid=(m // tmm, nsteps),
        in_specs=[pl.BlockSpec((tm, tmm), lambda j, k: (k, j)),
                  pl.BlockSpec((tm, n), lambda j, k: (k, 0))],
        out_specs=pl.BlockSpec((tmm, n), lambda j, k: (j, 0)),
        out_shape=jax.ShapeDtypeStruct((m, n), F32),
        compiler_params=_params(("arbitrary", "arbitrary")),
    )(a, b)


def _a_specs(s):
    nq = s // A_QB
    q_spec = pl.BlockSpec((None, None, A_QB, HEAD_DIM), lambda h, i: (0, h, jnp.minimum(i, nq - 1), 0))
    kv_specs = [pl.BlockSpec((None, None, A_QB, HEAD_DIM), lambda h, i, n=n, m=m: (n, h, jnp.minimum(i + m, nq), 0))
                for n in range(2) for m in range(2)]
    t_spec = pl.BlockSpec((None, A_TQ, A_BAND), lambda h, i: (h, 0, 0))
    return nq, q_spec, kv_specs, t_spec


def _a_window(first_ref, second_ref, i):
    first = first_ref[...]
    return jnp.concatenate([jnp.where(i > 0, first, jnp.zeros_like(first)), second_ref[...]], axis=0)


def _a_scores(q_ref, k, t_ref, i, j):
    rows = slice(j * A_TQ, (j + 1) * A_TQ)
    qs = q_ref[rows, :] * 0.125
    kj = k[j * A_TQ:j * A_TQ + A_BAND, :]
    sc = _dot_nt(qs, kj) + t_ref[...]
    col = lax.broadcasted_iota(jnp.int32, (A_TQ, A_BAND), 1)
    sc = jnp.where(col >= (A_BAND - A_TQ) - i * A_QB - j * A_TQ, sc, NEG)
    return rows, qs, kj, sc


def mix_a_fwd(qkv, kva, tbias):
    s = qkv.shape[2]
    nq, q_spec, kv_specs, t_spec = _a_specs(s)

    def body(q_ref, k0_ref, k1_ref, v0_ref, v1_ref, t_ref, o_ref, lse_ref):
        i = pl.program_id(1)
        k = _a_window(k0_ref, k1_ref, i)
        v = _a_window(v0_ref, v1_ref, i)
        for j in range(A_QB // A_TQ):
            rows, _, _, sc = _a_scores(q_ref, k, t_ref, i, j)
            m = jnp.max(sc, axis=-1, keepdims=True)
            p = jnp.exp(sc - m)
            l = jnp.sum(p, axis=-1, keepdims=True)
            o_ref[rows, :] = _dot(p.astype(BF16), v[j * A_TQ:j * A_TQ + A_BAND, :]) / l
            lse_ref[rows, :] = m + jnp.log(l)

    return pl.pallas_call(
        body, name="mix_a_fwd", grid=(N_HEADS, nq),
        in_specs=[q_spec] + kv_specs + [t_spec],
        out_specs=[pl.BlockSpec((None, A_QB, HEAD_DIM), lambda h, i: (h, i, 0)),
                   pl.BlockSpec((None, A_QB, 1), lambda h, i: (h, i, 0))],
        out_shape=[jax.ShapeDtypeStruct((N_HEADS, s, HEAD_DIM), F32),
                   jax.ShapeDtypeStruct((N_HEADS, s, 1), F32)],
        compiler_params=_params(("arbitrary", "arbitrary")),
    )(qkv, kva, kva, kva, kva, tbias)


def mix_a_bwd(qkv, kva, tbias, do, o, lse):
    s = qkv.shape[2]
    nq, q_spec, kv_specs, t_spec = _a_specs(s)
    row_spec = lambda w: pl.BlockSpec((None, A_QB, w), lambda h, i: (h, jnp.minimum(i, nq - 1), 0))
    done_spec = pl.BlockSpec((None, A_QB, HEAD_DIM), lambda h, i: (h, jnp.maximum(i - 1, 0), 0))
    win = 2 * A_QB

    def body(q_ref, k0_ref, k1_ref, v0_ref, v1_ref, t_ref, do_ref, o_ref, lse_ref,
             dq_ref, dk_ref, dv_ref, dt_ref, dk_win, dv_win):
        i = pl.program_id(1)

        @pl.when(i == 0)
        def _():
            dk_win[...] = jnp.zeros_like(dk_win)
            dv_win[...] = jnp.zeros_like(dv_win)
            dt_ref[...] = jnp.zeros_like(dt_ref)

        @pl.when(i < nq)
        def _():
            k = _a_window(k0_ref, k1_ref, i)
            v = _a_window(v0_ref, v1_ref, i)
            dt = jnp.zeros((A_TQ, A_BAND), F32)
            for j in range(A_QB // A_TQ):
                rows, qs, kj, sc = _a_scores(q_ref, k, t_ref, i, j)
                keys = slice(j * A_TQ, j * A_TQ + A_BAND)
                dob = do_ref[rows, :]
                p = jnp.exp(sc - lse_ref[rows, :])
                delta = jnp.sum(o_ref[rows, :] * dob.astype(F32), axis=-1, keepdims=True)
                ds = p * (_dot_nt(dob, v[keys, :]) - delta)
                dsb = ds.astype(BF16)
                dq_ref[rows, :] = _dot(dsb, kj) * 0.125
                dk_win[keys, :] += _dot_tn(dsb, qs)
                dv_win[keys, :] += _dot_tn(p.astype(BF16), dob)
                dt = dt + ds
            dt_ref[...] += dt

        dk_ref[...] = dk_win[0:A_QB, :]
        dv_ref[...] = dv_win[0:A_QB, :]
        dk_rest = dk_win[A_QB:win, :]
        dv_rest = dv_win[A_QB:win, :]
        dk_win[0:A_QB, :] = dk_rest
        dv_win[0:A_QB, :] = dv_rest
        dk_win[A_QB:win, :] = jnp.zeros((A_QB, HEAD_DIM), F32)
        dv_win[A_QB:win, :] = jnp.zeros((A_QB, HEAD_DIM), F32)

    return pl.pallas_call(
        body, name="mix_a_bwd", grid=(N_HEADS, nq + 1),
        in_specs=[q_spec] + kv_specs + [t_spec, row_spec(HEAD_DIM), row_spec(HEAD_DIM), row_spec(1)],
        out_specs=[row_spec(HEAD_DIM), done_spec, done_spec, t_spec],
        out_shape=[jax.ShapeDtypeStruct((N_HEADS, s, HEAD_DIM), F32),
                   jax.ShapeDtypeStruct((N_HEADS, s, HEAD_DIM), F32),
                   jax.ShapeDtypeStruct((N_HEADS, s, HEAD_DIM), F32),
                   jax.ShapeDtypeStruct((N_HEADS, A_TQ, A_BAND), F32)],
        scratch_shapes=[pltpu.VMEM((win, HEAD_DIM), F32), pltpu.VMEM((win, HEAD_DIM), F32)],
        compiler_params=_params(("arbitrary", "arbitrary")),
    )(qkv, kva, kva, kva, kva, tbias, do, o, lse)


def relbias_tile(rel_bias, relmat):
    def body(rb_ref, rel_ref, o_ref):
        rel = rel_ref[...]
        o_ref[...] = jnp.full(o_ref.shape, NEG, F32)

        def step(r, carry):
            hit = rel == r
            for h in range(N_HEADS):
                o_ref[h] = jnp.where(hit, rb_ref[h, r], o_ref[h])
            return carry

        lax.fori_loop(REL_LO, REL_HI, step, 0)

    return pl.pallas_call(
        body, name="relbias_tile",
        in_specs=[pl.BlockSpec(memory_space=pltpu.SMEM), pl.BlockSpec(memory_space=pltpu.VMEM)],
        out_specs=pl.BlockSpec(memory_space=pltpu.VMEM),
        out_shape=jax.ShapeDtypeStruct((N_HEADS, A_TQ, A_BAND), F32),
        compiler_params=_params(),
    )(rel_bias, relmat)


def relbias_grad(dt, relmat):
    def body(dt_ref, rel_ref, o_ref):
        rel = rel_ref[...]
        lane = lax.broadcasted_iota(jnp.int32, (8, 384), 1)
        row = lax.broadcasted_iota(jnp.int32, (8, 384), 0)

        def step(r, acc):
            hit = rel == r
            for h in range(N_HEADS):
                val = jnp.sum(jnp.where(hit, dt_ref[h], 0.0))
                acc = jnp.where((lane == r) & (row == h), val, acc)
            return acc

        o_ref[...] = lax.fori_loop(REL_LO, REL_HI, step, jnp.zeros((8, 384), F32))

    return pl.pallas_call(
        body, name="relbias_grad",
        out_shape=jax.ShapeDtypeStruct((8, 384), F32),
        compiler_params=_params(),
    )(dt, relmat)


def _b_norm(v, gain):
    mu = jnp.mean(v, axis=-1, keepdims=True)
    xc = v - mu
    rstd = lax.rsqrt(jnp.mean(xc * xc, axis=-1, keepdims=True) + EPS)
    xhat = xc * rstd
    return xhat, rstd, xhat * gain


def _tril_mask():
    t = lax.broadcasted_iota(jnp.int32, (SG_CHUNK, SG_CHUNK), 0)
    u = lax.broadcasted_iota(jnp.int32, (SG_CHUNK, SG_CHUNK), 1)
    return u <= t


def mix_b_fwd(uv, gain, w_s, b_col):
    s = uv.shape[0]
    tm = min(ROW_T, s)

    def body(uv_ref, gain_ref, w_ref, b_ref, y_ref):
        tril = _tril_mask()
        ws = [jnp.where(tril, w_ref[g], 0.0).astype(BF16) for g in range(N_HEADS)]
        for c in range(tm // SG_CHUNK):
            rows = slice(c * SG_CHUNK, (c + 1) * SG_CHUNK)
            u = uv_ref[rows, 0:D_BRANCH]
            _, _, vn = _b_norm(uv_ref[rows, D_BRANCH:2 * D_BRANCH], gain_ref[...])
            vnb = vn.astype(BF16)
            outs = []
            for g in range(N_HEADS):
                cols = slice(g * HEAD_DIM, (g + 1) * HEAD_DIM)
                mixed = _dot(ws[g], vnb[:, cols]) + b_ref[g]
                outs.append(u[:, cols] * mixed)
            y_ref[rows, :] = jnp.concatenate(outs, axis=1)

    return pl.pallas_call(
        body, name="mix_b_fwd", grid=(s // tm,),
        in_specs=[pl.BlockSpec((tm, 2 * D_BRANCH), lambda i: (i, 0)),
                  pl.BlockSpec((1, D_BRANCH), lambda i: (0, 0)),
                  pl.BlockSpec((N_HEADS, SG_CHUNK, SG_CHUNK), lambda i: (0, 0, 0)),
                  pl.BlockSpec((N_HEADS, SG_CHUNK, 1), lambda i: (0, 0, 0))],
        out_specs=pl.BlockSpec((tm, D_BRANCH), lambda i: (i, 0)),
        out_shape=jax.ShapeDtypeStruct((s, D_BRANCH), F32),
        compiler_params=_params(("arbitrary",)),
    )(uv, gain, w_s, b_col)


def mix_b_bwd(uv, gain, w_s, b_col, dy):
    s = uv.shape[0]
    tm = min(ROW_T, s)

    def body(uv_ref, gain_ref, w_ref, b_ref, dy_ref, duv_ref, dw_ref, db_ref, dgain_ref):
        i = pl.program_id(0)

        @pl.when(i == 0)
        def _():
            dw_ref[...] = jnp.zeros_like(dw_ref)
            db_ref[...] = jnp.zeros_like(db_ref)
            dgain_ref[...] = jnp.zeros_like(dgain_ref)

        tril = _tril_mask()
        ws = [jnp.where(tril, w_ref[g], 0.0).astype(BF16) for g in range(N_HEADS)]
        gain_v = gain_ref[...]
        for c in range(tm // SG_CHUNK):
            rows = slice(c * SG_CHUNK, (c + 1) * SG_CHUNK)
            u = uv_ref[rows, 0:D_BRANCH]
            xhat, rstd, vn = _b_norm(uv_ref[rows, D_BRANCH:2 * D_BRANCH], gain_v)
            vnb = vn.astype(BF16)
            dyv = dy_ref[rows, :]
            dus, dvns = [], []
            for g in range(N_HEADS):
                cols = slice(g * HEAD_DIM, (g + 1) * HEAD_DIM)
                mixed = _dot(ws[g], vnb[:, cols]) + b_ref[g]
                dus.append(dyv[:, cols] * mixed)
                dmixed = dyv[:, cols] * u[:, cols]
                dmb = dmixed.astype(BF16)
                db_ref[g] += jnp.sum(dmixed, axis=-1, keepdims=True)
                dw_ref[g] += jnp.where(tril, _dot_nt(dmb, vnb[:, cols]), 0.0)
                dvns.append(_dot_tn(ws[g], dmb))
            dvn = jnp.concatenate(dvns, axis=1)
            dgain_ref[...] += jnp.sum(dvn * xhat, axis=0, keepdims=True)
            dxh = dvn * gain_v
            dv = rstd * (dxh - jnp.mean(dxh, axis=-1, keepdims=True)
                         - xhat * jnp.mean(dxh * xhat, axis=-1, keepdims=True))
            duv_ref[rows, :] = jnp.concatenate(dus + [dv], axis=1)

    return pl.pallas_call(
        body, name="mix_b_bwd", grid=(s // tm,),
        in_specs=[pl.BlockSpec((tm, 2 * D_BRANCH), lambda i: (i, 0)),
                  pl.BlockSpec((1, D_BRANCH), lambda i: (0, 0)),
                  pl.BlockSpec((N_HEADS, SG_CHUNK, SG_CHUNK), lambda i: (0, 0, 0)),
                  pl.BlockSpec((N_HEADS, SG_CHUNK, 1), lambda i: (0, 0, 0)),
                  pl.BlockSpec((tm, D_BRANCH), lambda i: (i, 0))],
        out_specs=[pl.BlockSpec((tm, 2 * D_BRANCH), lambda i: (i, 0)),
                   pl.BlockSpec((N_HEADS, SG_CHUNK, SG_CHUNK), lambda i: (0, 0, 0)),
                   pl.BlockSpec((N_HEADS, SG_CHUNK, 1), lambda i: (0, 0, 0)),
                   pl.BlockSpec((1, D_BRANCH), lambda i: (0, 0))],
        out_shape=[jax.ShapeDtypeStruct((s, 2 * D_BRANCH), F32),
                   jax.ShapeDtypeStruct((N_HEADS, SG_CHUNK, SG_CHUNK), F32),
                   jax.ShapeDtypeStruct((N_HEADS, SG_CHUNK, 1), F32),
                   jax.ShapeDtypeStruct((1, D_BRANCH), F32)],
        compiler_params=_params(("arbitrary",)),
    )(uv, gain, w_s, b_col, dy)


def _scan_mats(nrow):
    a = lax.broadcasted_iota(jnp.int32, (128, 128), 0)
    b = lax.broadcasted_iota(jnp.int32, (128, 128), 1)
    r = lax.broadcasted_iota(jnp.int32, (nrow, nrow), 0)
    c = lax.broadcasted_iota(jnp.int32, (nrow, nrow), 1)
    nb = nrow // N_HEADS
    same = (r // nb) == (c // nb)
    return a, b, r, c, same


def _exact_dot(x, m):
    hi, mid, lo = _split3(x)
    return _dot(hi, m) + _dot(mid, m) + _dot(lo, m)


def _exact_dot_left(m, x):
    hi, mid, lo = _split3(x)
    return _dot(m, hi) + _dot(m, mid) + _dot(m, lo)


def fox_gate_fwd(ft, bcol):
    nrow = ft.shape[0]

    def body(f_ref, b_ref, c_ref):
        z = f_ref[...] + b_ref[...]
        ls = jnp.minimum(z, 0.0) - jnp.log(1.0 + jnp.exp(-jnp.abs(z)))
        a, b, r, c, same = _scan_mats(nrow)
        within = _exact_dot(ls, (a <= b).astype(BF16))
        tot = jnp.broadcast_to(within[:, 127:128], within.shape)
        before = _exact_dot_left((same & (c < r)).astype(BF16), tot)
        c_ref[...] = within + before

    return pl.pallas_call(
        body, name="fox_gate_fwd",
        out_shape=jax.ShapeDtypeStruct((nrow, 128), F32),
        compiler_params=_params(),
    )(ft, bcol)


def fox_gate_bwd(ft, bcol, dc):
    nrow = ft.shape[0]

    def body(f_ref, b_ref, dc_ref, df_ref, db_ref):
        a, b, r, c, same = _scan_mats(nrow)
        dcv = dc_ref[...]
        within = _exact_dot(dcv, (a >= b).astype(BF16))
        tot = jnp.broadcast_to(within[:, 0:1], within.shape)
        after = _exact_dot_left((same & (c > r)).astype(BF16), tot)
        dls = within + after
        z = f_ref[...] + b_ref[...]
        dz = dls * _sigmoid(-z)
        df_ref[...] = dz
        rs = jnp.broadcast_to(jnp.sum(dz, axis=-1, keepdims=True), dz.shape)
        hr = lax.broadcasted_iota(jnp.int32, (8, nrow), 0)
        hc = lax.broadcasted_iota(jnp.int32, (8, nrow), 1)
        db_ref[...] = _exact_dot_left((hr == hc // (nrow // N_HEADS)).astype(BF16), rs)

    return pl.pallas_call(
        body, name="fox_gate_bwd",
        out_shape=[jax.ShapeDtypeStruct((nrow, 128), F32), jax.ShapeDtypeStruct((8, 128), F32)],
        compiler_params=_params(),
    )(ft, bcol, dc)


def _att_specs(s, qi, ki, vi):
    q_spec = pl.BlockSpec((None, None, FOX_TQ, HEAD_DIM), lambda h, i: (qi, h, i, 0))
    k_spec = pl.BlockSpec((None, None, s, HEAD_DIM), lambda h, i: (ki, h, 0, 0))
    v_spec = pl.BlockSpec((None, None, s, HEAD_DIM), lambda h, i: (vi, h, 0, 0))
    row_spec = lambda w: pl.BlockSpec((None, FOX_TQ, w), lambda h, i: (h, i, 0))
    gate_spec = pl.BlockSpec((None, s // ATT_T, 1, ATT_T), lambda h, i: (h, 0, 0, 0))
    return q_spec, k_spec, v_spec, row_spec, gate_spec


def _causal(strict, n=ATT_T):
    row = lax.broadcasted_iota(jnp.int32, (n, n), 0)
    col = lax.broadcasted_iota(jnp.int32, (n, n), 1)
    return (col < row) if strict else (col <= row)


def _gate_row(cr_ref, kb, g):
    if g == 1:
        return cr_ref[kb]
    return jnp.concatenate([cr_ref[kb + n] for n in range(g)], axis=1)


def _fox_walk(i, carry, tile, alive):
    g = FOX_WIDE
    own = FOX_TQ // ATT_T
    nwide = (own * i) // g
    carry = tile(own * i, own, carry, True)
    carry = lax.fori_loop(0, (own * i - nwide * g) // own, lambda n, c: tile(nwide * g, own, c, False), carry)

    def cond(state):
        return jnp.logical_and(state[0] >= 0, state[1] > 0)

    def step(state):
        n = state[0]
        c = tile(n * g, g, state[2:], False)
        return (n - 1, alive(n * g, c)) + tuple(c)

    out = lax.while_loop(cond, step, (nwide - 1, alive(nwide * g, carry)) + tuple(carry))
    return out[2:]


def _fox_reach(qs, k_ref, kmax_ref, cc, i):
    s = k_ref.shape[0]
    rows = 4 * ATT_T

    @pl.when(i == 0)
    def _():
        def chunk(n, mx):
            kc = k_ref[pl.ds(pl.multiple_of(n * rows, rows), rows), :].astype(F32)
            return jnp.maximum(mx, jnp.max(jnp.sum(kc * kc, axis=-1, keepdims=True)))

        kmax_ref[0] = jnp.sqrt(lax.fori_loop(0, s // rows, chunk, jnp.float32(0.0)))

    qf = qs.astype(F32)
    return jnp.sqrt(jnp.sum(qf * qf, axis=-1, keepdims=True)) * kmax_ref[0] + cc


def _gate_col(cr_ref, i):
    row = lax.broadcasted_iota(jnp.int32, (ATT_T, ATT_T), 0)
    col = lax.broadcasted_iota(jnp.int32, (ATT_T, ATT_T), 1)
    own = FOX_TQ // ATT_T
    return jnp.concatenate([jnp.sum(jnp.where(row == col, cr_ref[own * i + n], 0.0), axis=-1, keepdims=True)
                            for n in range(own)], axis=0)


def _fox_scores(qs, k, cc, crow, masked):
    sc = (_dot_nt(qs, k) + (cc - crow)) * LOG2E
    if masked:
        sc = jnp.where(_causal(False, FOX_TQ), sc, NEG)
    return sc


def fox_fwd(qkv, c_row):
    s = qkv.shape[2]
    t = ATT_T
    nq = s // FOX_TQ
    q_spec, k_spec, v_spec, row_spec, gate_spec = _att_specs(s, 1, 2, 3)
    rows = 4 * t

    def body(q_ref, k_ref, v_ref, cr_ref, o_ref, ref_ref, rl_ref, v1_ref, kmax_ref):
        i = pl.program_id(1)

        @pl.when(i == 0)
        def _():
            def chunk(n, carry):
                r0 = pl.multiple_of(n * rows, rows)
                v1_ref[pl.ds(r0, rows), :] = jnp.concatenate(
                    [v_ref[pl.ds(r0, rows), :], jnp.ones((rows, HEAD_DIM), BF16)], axis=1)
                return carry

            lax.fori_loop(0, s // rows, chunk, 0)

        qs = q_ref[...] * 0.125
        cc = _gate_col(cr_ref, i)
        reach = _fox_reach(qs, k_ref, kmax_ref, cc, i) * LOG2E

        def alive(kb, carry):
            return (jnp.max(reach - cr_ref[kb][:, 0:1] * LOG2E - carry[0]) > FOX_DEAD2).astype(jnp.int32)

        def tile(kb, g, carry, masked):
            m, acc = carry
            k0 = pl.multiple_of(kb * t, t)
            sc = _fox_scores(qs, k_ref[pl.ds(k0, g * t), :], cc, _gate_row(cr_ref, kb, g), masked)
            m_new = jnp.maximum(m, jnp.ceil(jnp.max(sc, axis=-1, keepdims=True)))
            pb = jnp.exp2(sc - m_new).astype(BF16)
            acc = jnp.exp2(m - m_new) * acc + _dot(pb, v1_ref[pl.ds(k0, g * t), :])
            return m_new, acc

        init = (jnp.full((FOX_TQ, 1), NEG, F32), jnp.zeros((FOX_TQ, 2 * HEAD_DIM), F32))
        m, acc = _fox_walk(i, init, tile, alive)
        rl = 1.0 / acc[:, HEAD_DIM:HEAD_DIM + 1]
        o_ref[...] = acc[:, 0:HEAD_DIM] * rl
        ref_ref[...] = m
        rl_ref[...] = rl

    return pl.pallas_call(
        body, name="fox_fwd", grid=(N_HEADS, nq),
        in_specs=[q_spec, k_spec, v_spec, gate_spec],
        out_specs=[row_spec(HEAD_DIM), row_spec(1), row_spec(1)],
        out_shape=[jax.ShapeDtypeStruct((N_HEADS, s, HEAD_DIM), F32),
                   jax.ShapeDtypeStruct((N_HEADS, s, 1), F32),
                   jax.ShapeDtypeStruct((N_HEADS, s, 1), F32)],
        scratch_shapes=[pltpu.VMEM((s, 2 * HEAD_DIM), BF16), pltpu.SMEM((1,), F32)],
        compiler_params=_params(("arbitrary", "arbitrary")),
    )(qkv, qkv, qkv, c_row)


def fox_bwd(qkv, c_row, do, o, ref, rl):
    s = qkv.shape[2]
    t = ATT_T
    nq = s // FOX_TQ
    q_spec, k_spec, v_spec, row_spec, gate_spec = _att_specs(s, 1, 2, 3)
    any_spec = pl.BlockSpec(memory_space=pl.ANY)

    def body(q_ref, k_ref, v_ref, cr_ref, do_ref, o_ref, ref_ref, rl_ref,
             dq_ref, dk_hbm, dv_hbm, dc_ref, dk_acc, dv_acc, kmax_ref):
        h = pl.program_id(0)
        i = pl.program_id(1)

        @pl.when(i == 0)
        def _():
            dk_acc[...] = jnp.zeros_like(dk_acc)
            dv_acc[...] = jnp.zeros_like(dv_acc)
            dc_ref[...] = jnp.zeros_like(dc_ref)

        qs = q_ref[...] * 0.125
        ref = ref_ref[...]
        rl = rl_ref[...]
        dob = (do_ref[...].astype(F32) * rl).astype(BF16)
        delta = jnp.sum(o_ref[...] * dob.astype(F32), axis=-1, keepdims=True)
        cc = _gate_col(cr_ref, i)
        margin = _fox_reach(qs, k_ref, kmax_ref, cc, i) * LOG2E - ref

        def alive(kb, carry):
            return (jnp.max(margin - cr_ref[kb][:, 0:1] * LOG2E) > FOX_DEAD2).astype(jnp.int32)

        def tile(kb, g, carry, masked):
            dq, = carry
            k0 = pl.multiple_of(kb * t, t)
            k = k_ref[pl.ds(k0, g * t), :]
            sc = _fox_scores(qs, k, cc, _gate_row(cr_ref, kb, g), masked)
            wb = jnp.exp2(sc - ref).astype(BF16)
            ds = wb.astype(F32) * (_dot_nt(dob, v_ref[pl.ds(k0, g * t), :]) - delta)
            dsb = ds.astype(BF16)
            dk_acc[pl.ds(k0, g * t), :] += _dot_tn(dsb, qs)
            dv_acc[pl.ds(k0, g * t), :] += _dot_tn(wb, dob)
            dcs = -jnp.sum(ds, axis=0, keepdims=True)
            for n in range(g):
                dc_ref[kb + n] += dcs[:, n * t:(n + 1) * t]
            return (dq + _dot(dsb, k),)

        dq, = _fox_walk(i, (jnp.zeros((FOX_TQ, HEAD_DIM), F32),), tile, alive)
        dq_ref[...] = dq * 0.125

        @pl.when(i == nq - 1)
        def _():
            pltpu.sync_copy(dk_acc, dk_hbm.at[h])
            pltpu.sync_copy(dv_acc, dv_hbm.at[h])

    return pl.pallas_call(
        body, name="fox_bwd", grid=(N_HEADS, nq),
        in_specs=[q_spec, k_spec, v_spec,
                  gate_spec,
                  row_spec(HEAD_DIM), row_spec(HEAD_DIM), row_spec(1), row_spec(1)],
        out_specs=[row_spec(HEAD_DIM), any_spec, any_spec,
                   gate_spec],
        out_shape=[jax.ShapeDtypeStruct((N_HEADS, s, HEAD_DIM), F32),
                   jax.ShapeDtypeStruct((N_HEADS, s, HEAD_DIM), F32),
                   jax.ShapeDtypeStruct((N_HEADS, s, HEAD_DIM), F32),
                   jax.ShapeDtypeStruct((N_HEADS, s // t, 1, t), F32)],
        scratch_shapes=[pltpu.VMEM((s, HEAD_DIM), F32), pltpu.VMEM((s, HEAD_DIM), F32), pltpu.SMEM((1,), F32)],
        compiler_params=_params(("arbitrary", "arbitrary")),
    )(qkv, qkv, qkv, c_row, do, o, ref, rl)


def _sb_tile(qs, k, run, masked):
    z = _dot_nt(qs, k)
    sp = jnp.log(1.0 + jnp.exp(-jnp.abs(z)))
    ls = jnp.minimum(z, 0.0) - sp
    lm = -jnp.maximum(z, 0.0) - sp
    if masked:
        valid = _causal(True)
        lm = jnp.where(valid, lm, 0.0)
    row = lax.broadcasted_iota(jnp.int32, (ATT_T, ATT_T), 0)
    col = lax.broadcasted_iota(jnp.int32, (ATT_T, ATT_T), 1)
    later = (row > col).astype(BF16)
    hi, lo = _split2(lm)
    between = run + _dot(hi, later) + _dot(lo, later)
    a = jnp.exp(ls + between)
    if masked:
        a = jnp.where(valid, a, 0.0)
    return ls, lm, a


def _sb_walk(i, carry, tile):
    def alive_of(c):
        return (jnp.max(c[0]) > SB_DEAD).astype(jnp.int32)

    def cond(state):
        n, alive = state[0], state[1]
        return jnp.logical_and(n < i, alive > 0)

    def step(state):
        n = state[0]
        c = tile(i - 1 - n, state[2:], False)
        return (n + 1, alive_of(c)) + tuple(c)

    out = lax.while_loop(cond, step, (jnp.int32(0), alive_of(carry)) + tuple(carry))
    return out[2:]


def _sb_specs(s):
    tq = 2 * ATT_T
    q_spec = pl.BlockSpec((None, None, tq, HEAD_DIM), lambda h, i: (4, h, i, 0))
    k_spec = pl.BlockSpec((None, None, s, HEAD_DIM), lambda h, i: (5, h, 0, 0))
    v_spec = pl.BlockSpec((None, None, s, HEAD_DIM), lambda h, i: (6, h, 0, 0))
    row_spec = pl.BlockSpec((None, tq, HEAD_DIM), lambda h, i: (h, i, 0))
    return tq, q_spec, k_spec, v_spec, row_spec


def _sb_block(i, tile, zero):
    t = ATT_T
    lo, hi = slice(0, t), slice(t, 2 * t)
    c_hi = tile(2 * i + 1, hi, zero, True)
    c_lo = tile(2 * i, lo, zero, True)
    c_hi = tile(2 * i, hi, c_hi, False)
    carry = tuple(jnp.concatenate([a, b], axis=0) for a, b in zip(c_lo, c_hi))
    return _sb_walk(2 * i, carry, lambda kb, c, masked: tile(kb, slice(0, 2 * t), c, masked))


def sb_fwd(qkv):
    s = qkv.shape[2]
    t = ATT_T
    tq, q_spec, k_spec, v_spec, row_spec = _sb_specs(s)

    def body(q_ref, k_ref, v_ref, o_ref):
        i = pl.program_id(1)
        qs = q_ref[...] * 0.125

        def tile(kb, rows, carry, masked):
            run, acc = carry
            k0 = pl.multiple_of(kb * t, t)
            _, lm, a = _sb_tile(qs[rows], k_ref[pl.ds(k0, t), :], run, masked)
            acc = acc + _dot(a.astype(BF16), v_ref[pl.ds(k0, t), :])
            return run + jnp.sum(lm, axis=-1, keepdims=True), acc

        _, acc = _sb_block(i, tile, (jnp.zeros((t, 1), F32), jnp.zeros((t, HEAD_DIM), F32)))
        o_ref[...] = acc

    return pl.pallas_call(
        body, name="sb_fwd", grid=(N_HEADS, s // tq),
        in_specs=[q_spec, k_spec, v_spec],
        out_specs=row_spec,
        out_shape=jax.ShapeDtypeStruct((N_HEADS, s, HEAD_DIM), F32),
        compiler_params=_params(("arbitrary", "arbitrary")),
    )(qkv, qkv, qkv)


def sb_bwd(qkv, do, o):
    s = qkv.shape[2]
    t = ATT_T
    tq, q_spec, k_spec, v_spec, row_spec = _sb_specs(s)
    nq = s // tq
    any_spec = pl.BlockSpec(memory_space=pl.ANY)

    def body(q_ref, k_ref, v_ref, do_ref, o_ref, dq_ref, dk_hbm, dv_hbm, dk_acc, dv_acc):
        h = pl.program_id(0)
        i = pl.program_id(1)

        @pl.when(i == 0)
        def _():
            dk_acc[...] = jnp.zeros_like(dk_acc)
            dv_acc[...] = jnp.zeros_like(dv_acc)

        qs_all = q_ref[...] * 0.125
        dob_all = do_ref[...]
        tot_all = jnp.sum(o_ref[...] * dob_all.astype(F32), axis=-1, keepdims=True)

        def tile(kb, rows, carry, masked):
            run, run_g, dq = carry
            qs, dob, tot = qs_all[rows], dob_all[rows], tot_all[rows]
            k0 = pl.multiple_of(kb * t, t)
            k = k_ref[pl.ds(k0, t), :]
            ls, lm, a = _sb_tile(qs, k, run, masked)
            ab = a.astype(BF16)
            g = ab.astype(F32) * _dot_nt(dob, v_ref[pl.ds(k0, t), :])
            row = lax.broadcasted_iota(jnp.int32, (t, t), 0)
            col = lax.broadcasted_iota(jnp.int32, (t, t), 1)
            from_here = (row >= col).astype(BF16)
            hi, lo = _split2(g)
            g_right = run_g + _dot(hi, from_here) + _dot(lo, from_here)
            g_left = tot - g_right
            dz = g - jnp.exp(ls) * (g + g_left)
            if masked:
                dz = jnp.where(_causal(True), dz, 0.0)
            dzb = dz.astype(BF16)
            dk_acc[pl.ds(k0, t), :] += _dot_tn(dzb, qs)
            dv_acc[pl.ds(k0, t), :] += _dot_tn(ab, dob)
            return (run + jnp.sum(lm, axis=-1, keepdims=True),
                    run_g + jnp.sum(g, axis=-1, keepdims=True),
                    dq + _dot(dzb, k))

        zero = jnp.zeros((t, 1), F32)
        _, _, dq = _sb_block(i, tile, (zero, zero, jnp.zeros((t, HEAD_DIM), F32)))
        dq_ref[...] = dq * 0.125

        @pl.when(i == nq - 1)
        def _():
            pltpu.sync_copy(dk_acc, dk_hbm.at[h])
            pltpu.sync_copy(dv_acc, dv_hbm.at[h])

    return pl.pallas_call(
        body, name="sb_bwd", grid=(N_HEADS, nq),
        in_specs=[q_spec, k_spec, v_spec, row_spec, row_spec],
        out_specs=[row_spec, any_spec, any_spec],
        out_shape=[jax.ShapeDtypeStruct((N_HEADS, s, HEAD_DIM), F32)] * 3,
        scratch_shapes=[pltpu.VMEM((s, HEAD_DIM), F32), pltpu.VMEM((s, HEAD_DIM), F32)],
        compiler_params=_params(("arbitrary", "arbitrary")),
    )(qkv, qkv, qkv, do, o)


def _branch_inputs(refs, br):
    ya_ref, yb_ref, yc_ref, yd_ref = refs
    if br == 1:
        return yb_ref[...]
    return _heads_to_lanes((ya_ref, None, yc_ref, yd_ref)[br])


def outproj_fwd(x, ya, yb, yc, yd, gates, bg, wout):
    s = x.shape[0]
    tm = min(ROW_T, s)

    def body(x_ref, ya_ref, yb_ref, yc_ref, yd_ref, gates_ref, bg_ref, w_ref, out_ref):
        pieces = []
        for br in range(4):
            cols = slice(br * D_BRANCH, (br + 1) * D_BRANCH)
            y = _branch_inputs((ya_ref, yb_ref, yc_ref, yd_ref), br)
            r = lax.rsqrt(jnp.mean(y * y, axis=-1, keepdims=True) + EPS)
            gt = gates_ref[:, cols]
            pieces.append((y * r * bg_ref[:, cols]) * (gt * _sigmoid(gt)))
        merged = jnp.concatenate(pieces, axis=1).astype(BF16)
        out_ref[...] = x_ref[...] + _dot(merged, w_ref[...])

    head_spec = pl.BlockSpec((N_HEADS, tm, HEAD_DIM), lambda i: (0, i, 0))
    return pl.pallas_call(
        body, name="outproj_fwd", grid=(s // tm,),
        in_specs=[pl.BlockSpec((tm, D_MODEL), lambda i: (i, 0)),
                  head_spec, pl.BlockSpec((tm, D_BRANCH), lambda i: (i, 0)), head_spec, head_spec,
                  pl.BlockSpec((tm, D_MODEL), lambda i: (i, 0)),
                  pl.BlockSpec((1, D_MODEL), lambda i: (0, 0)),
                  pl.BlockSpec((D_MODEL, D_MODEL), lambda i: (0, 0))],
        out_specs=pl.BlockSpec((tm, D_MODEL), lambda i: (i, 0)),
        out_shape=jax.ShapeDtypeStruct((s, D_MODEL), F32),
        compiler_params=_params(("arbitrary",)),
    )(x, ya, yb, yc, yd, gates, bg, wout)


def outproj_bwd(dout, ya, yb, yc, yd, gates, bg, wout):
    s = dout.shape[0]
    tm = min(ROW_T, s)

    def body(dout_ref, ya_ref, yb_ref, yc_ref, yd_ref, gates_ref, bg_ref, w_ref,
             dya_ref, dyb_ref, dyc_ref, dyd_ref, dgates_ref, dbg_ref, dw_ref):
        i = pl.program_id(0)

        @pl.when(i == 0)
        def _():
            dbg_ref[...] = jnp.zeros_like(dbg_ref)
            dw_ref[...] = jnp.zeros_like(dw_ref)

        doutb = dout_ref[...].astype(BF16)
        dmerged = _dot_nt(doutb, w_ref[...])
        pieces = []
        for br in range(4):
            cols = slice(br * D_BRANCH, (br + 1) * D_BRANCH)
            y = _branch_inputs((ya_ref, yb_ref, yc_ref, yd_ref), br)
            r = lax.rsqrt(jnp.mean(y * y, axis=-1, keepdims=True) + EPS)
            yn = y * r
            bgv = bg_ref[:, cols]
            gt = gates_ref[:, cols]
            sig = _sigmoid(gt)
            act = gt * sig
            n = yn * bgv
            pieces.append(n * act)
            dm = dmerged[:, cols]
            dn = dm * act
            dgates_ref[:, cols] = (dm * n * (sig * (1.0 + gt * (1.0 - sig)))).astype(BF16)
            dbg_ref[:, cols] += jnp.sum(dn * yn, axis=0, keepdims=True)
            u = dn * bgv
            dy = r * (u - yn * jnp.mean(yn * u, axis=-1, keepdims=True))
            if br == 1:
                dyb_ref[...] = dy
            else:
                dref = (dya_ref, None, dyc_ref, dyd_ref)[br]
                for hh in range(N_HEADS):
                    dref[hh] = dy[:, hh * HEAD_DIM:(hh + 1) * HEAD_DIM].astype(BF16)
        merged = jnp.concatenate(pieces, axis=1).astype(BF16)
        dw_ref[...] += _dot_tn(merged, doutb)

    head_spec = pl.BlockSpec((N_HEADS, tm, HEAD_DIM), lambda i: (0, i, 0))
    head_shape = jax.ShapeDtypeStruct((N_HEADS, s, HEAD_DIM), BF16)
    return pl.pallas_call(
        body, name="outproj_bwd", grid=(s // tm,),
        in_specs=[pl.BlockSpec((tm, D_MODEL), lambda i: (i, 0)),
                  head_spec, pl.BlockSpec((tm, D_BRANCH), lambda i: (i, 0)), head_spec, head_spec,
                  pl.BlockSpec((tm, D_MODEL), lambda i: (i, 0)),
                  pl.BlockSpec((1, D_MODEL), lambda i: (0, 0)),
                  pl.BlockSpec((D_MODEL, D_MODEL), lambda i: (0, 0))],
        out_specs=[head_spec, pl.BlockSpec((tm, D_BRANCH), lambda i: (i, 0)), head_spec, head_spec,
                   pl.BlockSpec((tm, D_MODEL), lambda i: (i, 0)),
                   pl.BlockSpec((1, D_MODEL), lambda i: (0, 0)),
                   pl.BlockSpec((D_MODEL, D_MODEL), lambda i: (0, 0))],
        out_shape=[head_shape, jax.ShapeDtypeStruct((s, D_BRANCH), F32), head_shape, head_shape,
                   jax.ShapeDtypeStruct((s, D_MODEL), BF16),
                   jax.ShapeDtypeStruct((1, D_MODEL), F32),
                   jax.ShapeDtypeStruct((D_MODEL, D_MODEL), F32)],
        compiler_params=_params(("arbitrary",)),
    )(dout, ya, yb, yc, yd, gates, bg, wout)


def final_loss(x, tgt, g):
    s = x.shape[0]
    tm = min(ROW_T, s)

    def body(x_ref, t_ref, g_ref, loss_ref, dx_ref, dg_ref):
        i = pl.program_id(0)

        @pl.when(i == 0)
        def _():
            loss_ref[...] = jnp.zeros_like(loss_ref)
            dg_ref[...] = jnp.zeros_like(dg_ref)

        xv = x_ref[...]
        gv = g_ref[...]
        r = lax.rsqrt(jnp.mean(xv * xv, axis=-1, keepdims=True) + EPS)
        xn = xv * r
        err = xn * gv - t_ref[...]
        loss_ref[...] += jnp.sum(err * err) * (0.5 / D_MODEL)
        dy = err * (1.0 / D_MODEL)
        u = dy * gv
        dx_ref[...] = r * (u - xn * jnp.mean(xn * u, axis=-1, keepdims=True))
        dg_ref[...] += jnp.sum(dy * xn, axis=0, keepdims=True)

    return pl.pallas_call(
        body, name="final_loss", grid=(s // tm,),
        in_specs=[pl.BlockSpec((tm, D_MODEL), lambda i: (i, 0)),
                  pl.BlockSpec((tm, D_MODEL), lambda i: (i, 0)),
                  pl.BlockSpec((1, D_MODEL), lambda i: (0, 0))],
        out_specs=[pl.BlockSpec((1, 128), lambda i: (0, 0)),
                   pl.BlockSpec((tm, D_MODEL), lambda i: (i, 0)),
                   pl.BlockSpec((1, D_MODEL), lambda i: (0, 0))],
        out_shape=[jax.ShapeDtypeStruct((1, 128), F32),
                   jax.ShapeDtypeStruct((s, D_MODEL), F32),
                   jax.ShapeDtypeStruct((1, D_MODEL), F32)],
        compiler_params=_params(("arbitrary",)),
    )(x, tgt, g)


def _rel_index():
    i = np.arange(A_TQ)[:, None]
    j = np.arange(A_BAND)[None, :]
    rel = np.clip(i - j + (A_BAND - A_TQ), -MAX_REL, MAX_REL) + MAX_REL
    dchunk = i // CHUNK + LOOKBACK - j // CHUNK
    valid = (dchunk >= 0) & (dchunk <= LOOKBACK)
    return jnp.asarray(np.where(valid, rel, -1).astype(np.int32))


def _layer_consts(p):
    tbias = relbias_tile(p["rel_bias"], _rel_index())
    return dict(
        norm_g=p["norm_g"].reshape(1, D_MODEL),
        v_gain=p["v_gain"].reshape(1, D_BRANCH),
        b_col=p["b_s"].reshape(N_HEADS, SG_CHUNK, 1),
        bg=p["branch_gain"].reshape(1, D_MODEL),
        tbias=tbias,
    )


def _gate_layout(fp, b_f, s):
    nb = s // 128
    ft = fp[:, :N_HEADS].T.reshape(N_HEADS * nb, 128)
    bcol = jnp.repeat(b_f, nb).reshape(N_HEADS * nb, 1)
    return ft, bcol


def layer_fwd(x, p):
    s = x.shape[0]
    c = _layer_consts(p)
    h, qkv, kva, gates, uv, fp = inproj_fwd(x, c["norm_g"], p["wp"])
    ya, lse_a = mix_a_fwd(qkv, kva, c["tbias"])
    yb = mix_b_fwd(uv, c["v_gain"], p["w_s"], c["b_col"])
    ft, bcol = _gate_layout(fp, p["b_f"], s)
    c_row = fox_gate_fwd(ft, bcol).reshape(N_HEADS, s // ATT_T, 1, ATT_T)
    yc, ref_c, rl_c = fox_fwd(qkv, c_row)
    yd = sb_fwd(qkv)
    out = outproj_fwd(x, ya, yb, yc, yd, gates, c["bg"], p["wout"])
    saved = dict(consts=c, x=x, h=h, qkv=qkv, gates=gates, uv=uv, kva=kva, ft=ft, bcol=bcol,
                 c_row=c_row, ya=ya, lse_a=lse_a, yb=yb, yc=yc, ref_c=ref_c, rl_c=rl_c, yd=yd)
    return out, saved


def layer_bwd(dout, p, sv):
    s = dout.shape[0]
    c = sv["consts"]
    dya, dyb, dyc, dyd, dgates, dbg, dwout = outproj_bwd(
        dout, sv["ya"], sv["yb"], sv["yc"], sv["yd"], sv["gates"], c["bg"], p["wout"])
    dqa, dka, dva, dt = mix_a_bwd(sv["qkv"], sv["kva"], c["tbias"], dya, sv["ya"], sv["lse_a"])
    drel = relbias_grad(dt, _rel_index())[:N_HEADS, :2 * MAX_REL + 1]
    duv, dws, dbs, dvgain = mix_b_bwd(sv["uv"], c["v_gain"], p["w_s"], c["b_col"], dyb)
    dqc, dkc, dvc, dc = fox_bwd(sv["qkv"], sv["c_row"], dyc, sv["yc"], sv["ref_c"], sv["rl_c"])
    dft, dbf = fox_gate_bwd(sv["ft"], sv["bcol"], dc.reshape(N_HEADS * (s // 128), 128))
    dfp = jnp.pad(dft.reshape(N_HEADS, s).T, ((0, 0), (0, 128 - N_HEADS)))
    dqd, dkd, dvd = sb_bwd(sv["qkv"], dyd, sv["yd"])
    dp, dx, dnorm = inproj_bwd((dqa, dka, dva, dqc, dkc, dvc, dqd, dkd, dvd), dgates, duv, dfp,
                               p["wp"], sv["x"], c["norm_g"], dout)
    dwp = weight_grad(sv["h"], dp, "inproj_wgrad")
    grads = dict(norm_g=dnorm.reshape(D_MODEL), wp=dwp, b_f=dbf[:N_HEADS, 0], rel_bias=drel,
                 w_s=dws, b_s=dbs.reshape(N_HEADS, SG_CHUNK), v_gain=dvgain.reshape(D_BRANCH),
                 branch_gain=dbg.reshape(4, D_BRANCH), wout=dwout)
    return dx, grads


def local_step(x, tgt, layers, final_g):
    saved = []
    cur = x
    for p in layers:
        cur, sv = layer_fwd(cur, p)
        saved.append(sv)
    loss, dcur, dfinal = final_loss(cur, tgt, final_g.reshape(1, D_MODEL))
    grads = [None] * len(layers)
    for l in reversed(range(len(layers))):
        dcur, grads[l] = layer_bwd(dcur, layers[l], saved[l])
    return loss[0, 0], dcur, grads, dfinal.reshape(D_MODEL)


def gather_weights(wb, wf):
    def body(wb_ref, wf_ref, ob_ref, of_ref, send_sems, recv_sems, loc_sems):
        x, y, c = lax.axis_index("x"), lax.axis_index("y"), lax.axis_index("c")
        me = 2 * x + y
        chips = [(1 - x, y), (x, 1 - y), (1 - x, 1 - y)]
        pairs = [(wb_ref, ob_ref), (wf_ref, of_ref)]
        local = [pltpu.make_async_copy(src, dst.at[me], loc_sems.at[n]) for n, (src, dst) in enumerate(pairs)]
        for cp in local:
            cp.start()

        def copy(j, n, slot):
            src, dst = pairs[n]
            return pltpu.make_async_remote_copy(
                src_ref=src, dst_ref=dst.at[slot], send_sem=send_sems.at[2 * j + n], recv_sem=recv_sems.at[2 * j + n],
                device_id=(chips[j][0], chips[j][1], c), device_id_type=MESH)

        sends = [copy(j, n, me) for j in range(3) for n in range(2)]
        for cp in sends:
            cp.start()
        for j in range(3):
            for n in range(2):
                copy(j, n, 2 * chips[j][0] + chips[j][1]).wait_recv()
        for cp in sends:
            cp.wait_send()
        for cp in local:
            cp.wait()

    any_spec = pl.BlockSpec(memory_space=pl.ANY)
    return pl.pallas_call(
        body, name="gather_weights",
        in_specs=[any_spec, any_spec], out_specs=[any_spec, any_spec],
        out_shape=[jax.ShapeDtypeStruct((4,) + wb.shape, wb.dtype), jax.ShapeDtypeStruct((4,) + wf.shape, wf.dtype)],
        scratch_shapes=[pltpu.SemaphoreType.DMA((6,)), pltpu.SemaphoreType.DMA((6,)), pltpu.SemaphoreType.DMA((2,))],
    )(wb, wf)


def exchange_grads(big, small):
    def body(b_ref, s_ref, rb_ref, rs_ref, send_sems, recv_sems, loc_sems):
        x, y, c = lax.axis_index("x"), lax.axis_index("y"), lax.axis_index("c")
        me_chip = 2 * x + y
        me = 4 * x + 2 * y + c
        peers = [(x, y, 1 - c)]
        for px, py in [(1 - x, y), (x, 1 - y), (1 - x, 1 - y)]:
            peers += [(px, py, c), (px, py, 1 - c)]
        local = [pltpu.make_async_copy(b_ref.at[me_chip], rb_ref.at[me], loc_sems.at[0]),
                 pltpu.make_async_copy(s_ref, rs_ref.at[me], loc_sems.at[1])]
        for cp in local:
            cp.start()

        def copies(n, chip, slot):
            kw = dict(device_id=peers[n], device_id_type=MESH)
            return [pltpu.make_async_remote_copy(src_ref=b_ref.at[chip], dst_ref=rb_ref.at[slot],
                                                 send_sem=send_sems.at[2 * n], recv_sem=recv_sems.at[2 * n], **kw),
                    pltpu.make_async_remote_copy(src_ref=s_ref, dst_ref=rs_ref.at[slot],
                                                 send_sem=send_sems.at[2 * n + 1], recv_sem=recv_sems.at[2 * n + 1], **kw)]

        sends = [cp for n, (px, py, _) in enumerate(peers) for cp in copies(n, 2 * px + py, me)]
        for cp in sends:
            cp.start()
        for n, (px, py, pc) in enumerate(peers):
            for cp in copies(n, me_chip, 4 * px + 2 * py + pc):
                cp.wait_recv()
        for cp in sends:
            cp.wait_send()
        for cp in local:
            cp.wait()

    any_spec = pl.BlockSpec(memory_space=pl.ANY)
    return pl.pallas_call(
        body, name="exchange_grads",
        in_specs=[any_spec, any_spec], out_specs=[any_spec, any_spec],
        out_shape=[jax.ShapeDtypeStruct((8,) + big.shape[1:], big.dtype),
                   jax.ShapeDtypeStruct((8,) + small.shape, small.dtype)],
        scratch_shapes=[pltpu.SemaphoreType.DMA((14,)), pltpu.SemaphoreType.DMA((14,)), pltpu.SemaphoreType.DMA((2,))],
    )(big, small)


def adamw_reduce(parts, w, m, v, name):
    rows = w.shape[0]
    tr = PACK_ROW_TILE
    c1 = 1.0 - ADAM_B1 ** ADAM_STEP
    c2 = 1.0 - ADAM_B2 ** ADAM_STEP

    def body(p_ref, w_ref, m_ref, v_ref, g_ref, d_ref, nm_ref, nv_ref):
        g = p_ref[0].astype(F32)
        for n in range(1, 8):
            g = g + p_ref[n].astype(F32)
        g_ref[...] = g
        nm = ADAM_B1 * m_ref[...] + (1.0 - ADAM_B1) * g
        nv = ADAM_B2 * v_ref[...] + (1.0 - ADAM_B2) * (g * g)
        nm_ref[...] = nm
        nv_ref[...] = nv
        d_ref[...] = -ADAM_LR * ((nm / c1) / (jnp.sqrt(nv / c2) + ADAM_EPS) + ADAM_WD * w_ref[...])

    spec = pl.BlockSpec((tr, 128), lambda i: (i, 0))
    shape = jax.ShapeDtypeStruct((rows, 128), F32)
    return pl.pallas_call(
        body, name=name, grid=(rows // tr,),
        in_specs=[pl.BlockSpec((8, tr, 128), lambda i: (0, i, 0)), spec, spec, spec],
        out_specs=[spec] * 4, out_shape=[shape] * 4,
        compiler_params=_params(("arbitrary",)),
    )(parts, w, m, v)


SHARDED = ("w_in", "w_out", "branch_gain")
SMALL = ("norm_g", "b_f", "rel_bias", "w_s", "b_s", "v_gain", "final_g")
WEIGHTS = ("norm_g", "w_in", "b_f", "rel_bias", "w_s", "b_s", "v_gain", "branch_gain", "w_out", "final_g")
PACK_ROW_TILE = 512


def _rows_of(shape):
    return -(-int(np.prod(shape)) // 128)


def _pack(leaves):
    parts = []
    for a in leaves:
        flat = a.reshape(-1).astype(F32)
        parts.append(jnp.pad(flat, (0, _rows_of(a.shape) * 128 - flat.shape[0])))
    flat = jnp.concatenate(parts)
    rows = flat.shape[0] // 128
    total = -(-rows // PACK_ROW_TILE) * PACK_ROW_TILE
    return jnp.pad(flat, (0, (total - rows) * 128)).reshape(total, 128)


def _unpack(slab, shapes):
    out, row = [], 0
    for shp in shapes:
        n = int(np.prod(shp))
        r = _rows_of(shp)
        out.append(slab[row:row + r].reshape(-1)[:n].reshape(shp))
        row += r
    return out


def _pack_w_in(w):
    return jnp.concatenate([w[:, :2816], w[:, 2820:], w[:, 2816:2820],
                            jnp.zeros((w.shape[0], N_PACK - N_IN), w.dtype)], axis=1)


def _unpack_w_in(wp):
    return jnp.concatenate([wp[:, :2816], wp[:, F_COL:F_COL + N_HEADS], wp[:, 2816:F_COL]], axis=1)


def kernel(x, norm_g, w_in, b_f, rel_bias, w_s, b_s, v_gain, branch_gain, w_out, final_g, loss_target, m_norm_g, m_w_in, m_b_f, m_rel_bias, m_w_s, m_b_s, m_v_gain, m_branch_gain, m_w_out, m_final_g, v_norm_g, v_w_in, v_b_f, v_rel_bias, v_w_s, v_b_s, v_v_gain, v_branch_gain, v_w_out, v_final_g):
    depth = norm_g.shape[0]
    weights = dict(norm_g=norm_g, w_in=w_in, b_f=b_f, rel_bias=rel_bias, w_s=w_s, b_s=b_s, v_gain=v_gain,
                   branch_gain=branch_gain, w_out=w_out, final_g=final_g)
    mom1 = dict(norm_g=m_norm_g, w_in=m_w_in, b_f=m_b_f, rel_bias=m_rel_bias, w_s=m_w_s, b_s=m_b_s,
                v_gain=m_v_gain, branch_gain=m_branch_gain, w_out=m_w_out, final_g=m_final_g)
    mom2 = dict(norm_g=v_norm_g, w_in=v_w_in, b_f=v_b_f, rel_bias=v_rel_bias, w_s=v_w_s, b_s=v_b_s,
                v_gain=v_v_gain, branch_gain=v_branch_gain, w_out=v_w_out, final_g=v_final_g)

    n_in_rows = _rows_of(w_in.shape)
    n_out_rows = _rows_of(w_out.shape)
    wb = jnp.concatenate([w_in.astype(BF16).reshape(n_in_rows, 128), w_out.astype(BF16).reshape(n_out_rows, 128)])
    wf = jnp.pad(branch_gain.reshape(-1), (0, 8 * 128 - branch_gain.size)).reshape(8, 128)
    gb, gf = gather_weights(wb, wf)
    w_in_full = gb[:, :n_in_rows].reshape((4,) + w_in.shape)
    w_in_full = jnp.moveaxis(w_in_full, 0, 2).reshape(depth, D_MODEL, N_IN)
    w_out_full = gb[:, n_in_rows:].reshape((4,) + w_out.shape)
    w_out_full = jnp.moveaxis(w_out_full, 0, 1).reshape(depth, D_MODEL, D_MODEL)
    bg_full = gf.reshape(4, -1)[:, :branch_gain.size].reshape((4,) + branch_gain.shape)
    bg_full = jnp.moveaxis(bg_full, 0, 2).reshape(depth, 4, D_BRANCH)

    layers = [dict(norm_g=norm_g[l], wp=_pack_w_in(w_in_full[l]), b_f=b_f[l], rel_bias=rel_bias[l], w_s=w_s[l],
                   b_s=b_s[l], v_gain=v_gain[l], branch_gain=bg_full[l], wout=w_out_full[l]) for l in range(depth)]

    loss_part, grad_x, lgrads, dfinal = local_step(x[0], loss_target[0], layers, final_g)
    loss = lax.psum(loss_part, ("x", "y", "c"))

    stack = lambda k: jnp.stack([g[k] for g in lgrads])
    d_w_in = jnp.stack([_unpack_w_in(g["wp"]) for g in lgrads])
    d_w_out = stack("wout")
    d_bg = stack("branch_gain")
    small = dict(norm_g=stack("norm_g"), b_f=stack("b_f"), rel_bias=stack("rel_bias"), w_s=stack("w_s"),
                 b_s=stack("b_s"), v_gain=stack("v_gain"), final_g=dfinal)
    slabs = []
    for sidx in range(4):
        slabs.append(_pack([d_w_in[:, :, sidx * N_SHARD:(sidx + 1) * N_SHARD],
                            d_w_out[:, sidx * D_BRANCH:(sidx + 1) * D_BRANCH, :],
                            d_bg[:, :, sidx * HEAD_DIM:(sidx + 1) * HEAD_DIM]]).astype(BF16))
    big_parts, small_parts = exchange_grads(jnp.stack(slabs), _pack([small[k] for k in SMALL]))

    outs = {}
    for names, parts, name in ((SHARDED, big_parts, "adamw_big"), (SMALL, small_parts, "adamw_small")):
        pack_local = lambda d: _pack([d[k] for k in names])
        slabs = adamw_reduce(parts, pack_local(weights), pack_local(mom1), pack_local(mom2), name)
        shapes = [weights[k].shape for k in names]
        for tag, slab in zip(("grad", "delta", "new_m", "new_v"), slabs):
            for k, a in zip(names, _unpack(slab, shapes)):
                outs[tag, k] = a
    result = [loss, grad_x[None]]
    for tag in ("grad", "delta", "new_m", "new_v"):
        result += [outs[tag, k] for k in WEIGHTS]
    return tuple(result)
```

```python
import functools

import jax
import jax.numpy as jnp
import numpy as np
from jax import lax
from jax.experimental import pallas as pl
from jax.experimental.pallas import tpu as pltpu

F32 = jnp.float32
BF16 = jnp.bfloat16
MESH = pl.DeviceIdType.MESH

D_MODEL = 1024
D_BRANCH = 256
N_HEADS = 4
HEAD_DIM = 64
CHUNK = 64
LOOKBACK = 8
MAX_REL = 128
SG_CHUNK = 128
EPS = 1e-6
N_IN = 3844
N_PACK = 3968
F_COL = 3840
N_SHARD = 961
NEG = -1e30

A_TQ = 128
A_BAND = A_TQ + LOOKBACK * CHUNK
REL_LO = MAX_REL - (CHUNK - 1)
REL_HI = 2 * MAX_REL + 1
A_PAD = LOOKBACK * CHUNK
A_QB = 1024
ATT_T = 256
FOX_TQ = 512
FOX_WIDE = 4
FOX_DEAD2 = -160.0
LOG2E = 1.4426950408889634
SB_DEAD = -110.0
ROW_T = 512
VMEM_LIMIT = 56 * 1024 * 1024

ADAM_LR = 0.001
ADAM_B1 = 0.9
ADAM_B2 = 0.999
ADAM_EPS = 1e-08
ADAM_WD = 0.01
ADAM_STEP = 10

SEC_A_Q, SEC_A_K, SEC_A_V, SEC_A_G = 0, 256, 512, 768
SEC_B_U, SEC_B_V, SEC_B_G = 1024, 1280, 1536
SEC_C_Q, SEC_C_K, SEC_C_V, SEC_C_G = 1792, 2048, 2304, 2560
SEC_D_Q, SEC_D_K, SEC_D_V, SEC_D_G = 2816, 3072, 3328, 3584
QKV_SECS = (SEC_A_Q, SEC_C_Q, SEC_C_K, SEC_C_V, SEC_D_Q, SEC_D_K, SEC_D_V)
GATE_SECS = (SEC_A_G, SEC_B_G, SEC_C_G, SEC_D_G)


def _dot(a, b):
    return jnp.dot(a, b, preferred_element_type=F32)


def _dot_nt(a, b):
    return lax.dot_general(a, b, (((1,), (1,)), ((), ())), preferred_element_type=F32)


def _dot_tn(a, b):
    return lax.dot_general(a, b, (((0,), (0,)), ((), ())), preferred_element_type=F32)


def _split2(x):
    hi = x.astype(BF16)
    lo = (x - hi.astype(F32)).astype(BF16)
    return hi, lo


def _split3(x):
    hi = x.astype(BF16)
    r = x - hi.astype(F32)
    mid = r.astype(BF16)
    lo = (r - mid.astype(F32)).astype(BF16)
    return hi, mid, lo


def _sigmoid(x):
    return 1.0 / (1.0 + jnp.exp(-x))


def _params(sem=None, vmem=VMEM_LIMIT):
    return pltpu.CompilerParams(dimension_semantics=sem, vmem_limit_bytes=vmem)


def _heads_to_lanes(ref):
    return jnp.concatenate([ref[h] for h in range(N_HEADS)], axis=1)


def inproj_fwd(x, g, wp):
    s = x.shape[0]
    tm = A_PAD

    def body(x_ref, g_ref, w_ref, h_ref, qkv_ref, kva_ref, gates_ref, uv_ref, f_ref):
        xv = x_ref[...]
        r = lax.rsqrt(jnp.mean(xv * xv, axis=-1, keepdims=True) + EPS)
        h = (xv * r * g_ref[...]).astype(BF16)
        h_ref[...] = h
        for n, off in enumerate(QKV_SECS):
            p = _dot(h, w_ref[:, off:off + D_BRANCH])
            for hh in range(N_HEADS):
                qkv_ref[n, hh] = p[:, hh * HEAD_DIM:(hh + 1) * HEAD_DIM].astype(BF16)
        for n, off in enumerate((SEC_A_K, SEC_A_V)):
            p = _dot(h, w_ref[:, off:off + D_BRANCH])
            for hh in range(N_HEADS):
                kva_ref[n, hh] = p[:, hh * HEAD_DIM:(hh + 1) * HEAD_DIM].astype(BF16)
        for n, off in enumerate(GATE_SECS):
            gates_ref[:, n * D_BRANCH:(n + 1) * D_BRANCH] = _dot(h, w_ref[:, off:off + D_BRANCH])
        uv_ref[...] = _dot(h, w_ref[:, SEC_B_U:SEC_B_U + 2 * D_BRANCH])
        f_ref[...] = _dot(h, w_ref[:, F_COL:F_COL + 128])

    return pl.pallas_call(
        body, name="inproj_fwd", grid=(s // tm,),
        in_specs=[pl.BlockSpec((tm, D_MODEL), lambda i: (i, 0)),
                  pl.BlockSpec((1, D_MODEL), lambda i: (0, 0)),
                  pl.BlockSpec((D_MODEL, N_PACK), lambda i: (0, 0))],
        out_specs=[pl.BlockSpec((tm, D_MODEL), lambda i: (i, 0)),
                   pl.BlockSpec((len(QKV_SECS), N_HEADS, tm, HEAD_DIM), lambda i: (0, 0, i, 0)),
                   pl.BlockSpec((2, N_HEADS, tm, HEAD_DIM), lambda i: (0, 0, i + 1, 0)),
                   pl.BlockSpec((tm, D_MODEL), lambda i: (i, 0)),
                   pl.BlockSpec((tm, 2 * D_BRANCH), lambda i: (i, 0)),
                   pl.BlockSpec((tm, 128), lambda i: (i, 0))],
        out_shape=[jax.ShapeDtypeStruct((s, D_MODEL), BF16),
                   jax.ShapeDtypeStruct((len(QKV_SECS), N_HEADS, s, HEAD_DIM), BF16),
                   jax.ShapeDtypeStruct((2, N_HEADS, s + tm, HEAD_DIM), BF16),
                   jax.ShapeDtypeStruct((s, D_MODEL), F32),
                   jax.ShapeDtypeStruct((s, 2 * D_BRANCH), F32),
                   jax.ShapeDtypeStruct((s, 128), F32)],
        compiler_params=_params(("arbitrary",)),
    )(x, g, wp)


def inproj_bwd(dqkv, dgates, duv, dfp, wp, x, g, dres):
    s = x.shape[0]
    tm = A_PAD

    def body(*refs):
        dq_refs = refs[:9]
        dgates_ref, duv_ref, dfp_ref, w_ref, x_ref, g_ref, dres_ref, dp_ref, dx_ref, dg_ref = refs[9:]
        i = pl.program_id(0)
        a_q, a_k, a_v, c_q, c_k, c_v, d_q, d_k, d_v = [_heads_to_lanes(r).astype(BF16) for r in dq_refs]
        dgt = dgates_ref[...]
        duv_b = duv_ref[...].astype(BF16)
        dp = jnp.concatenate(
            [a_q, a_k, a_v, dgt[:, 0:256], duv_b, dgt[:, 256:512], c_q, c_k, c_v, dgt[:, 512:768],
             d_q, d_k, d_v, dgt[:, 768:1024], dfp_ref[...].astype(BF16)], axis=1)
        dp_ref[...] = dp
        dh = _dot_nt(dp, w_ref[...])
        xv = x_ref[...]
        r = lax.rsqrt(jnp.mean(xv * xv, axis=-1, keepdims=True) + EPS)
        xn = xv * r
        u = dh * g_ref[...]
        dx_ref[...] = dres_ref[...] + r * (u - xn * jnp.mean(xn * u, axis=-1, keepdims=True))

        @pl.when(i == 0)
        def _():
            dg_ref[...] = jnp.zeros_like(dg_ref)

        dg_ref[...] += jnp.sum(dh * xn, axis=0, keepdims=True)

    head_spec = pl.BlockSpec((N_HEADS, tm, HEAD_DIM), lambda i: (0, i, 0))
    padded_spec = pl.BlockSpec((N_HEADS, tm, HEAD_DIM), lambda i: (0, i + 1, 0))
    return pl.pallas_call(
        body, name="inproj_bwd", grid=(s // tm,),
        in_specs=[head_spec, padded_spec, padded_spec] + [head_spec] * 6 + [
            pl.BlockSpec((tm, D_MODEL), lambda i: (i, 0)),
            pl.BlockSpec((tm, 2 * D_BRANCH), lambda i: (i, 0)),
            pl.BlockSpec((tm, 128), lambda i: (i, 0)),
            pl.BlockSpec((D_MODEL, N_PACK), lambda i: (0, 0)),
            pl.BlockSpec((tm, D_MODEL), lambda i: (i, 0)),
            pl.BlockSpec((1, D_MODEL), lambda i: (0, 0)),
            pl.BlockSpec((tm, D_MODEL), lambda i: (i, 0))],
        out_specs=[pl.BlockSpec((tm, N_PACK), lambda i: (i, 0)),
                   pl.BlockSpec((tm, D_MODEL), lambda i: (i, 0)),
                   pl.BlockSpec((1, D_MODEL), lambda i: (0, 0))],
        out_shape=[jax.ShapeDtypeStruct((s, N_PACK), BF16),
                   jax.ShapeDtypeStruct((s, D_MODEL), F32),
                   jax.ShapeDtypeStruct((1, D_MODEL), F32)],
        compiler_params=_params(("arbitrary",)),
    )(*dqkv, dgates, duv, dfp, wp, x, g, dres)


def weight_grad(a, b, name):
    s, m = a.shape
    n = b.shape[1]
    tm = min(2 * ROW_T, s)
    tmm = 256
    nsteps = s // tm

    def body(a_ref, b_ref, o_ref):
        k = pl.program_id(1)

        @pl.when(k == 0)
        def _():
            o_ref[...] = jnp.zeros_like(o_ref)

        o_ref[...] += _dot_tn(a_ref[...], b_ref[...])

    return pl.pallas_call(
        body, name=name, grid=(m // tmm, nsteps),
        in_specs=[pl.BlockSpec((tm, tmm), lambda j, k: (k, j)),
                  pl.BlockSpec((tm, n), lambda j, k: (k, 0))],
        out_specs=pl.BlockSpec((tmm, n), lambda j, k: (j, 0)),
        out_shape=jax.ShapeDtypeStruct((m, n), F32),
        compiler_params=_params(("arbitrary", "arbitrary")),
    )(a, b)


def _a_specs(s):
    nq = s // A_QB
    per = A_QB // A_PAD
    q_spec = pl.BlockSpec((None, None, A_QB, HEAD_DIM), lambda h, i: (0, h, jnp.minimum(i, nq - 1), 0))
    kv_specs = [pl.BlockSpec((None, None, A_PAD, HEAD_DIM),
                             lambda h, i, n=n, m=m: (n, h, jnp.minimum(per * i + m, per * nq), 0))
                for n in range(2) for m in range(per + 1)]
    t_spec = pl.BlockSpec((None, A_TQ, A_BAND), lambda h, i: (h, 0, 0))
    return nq, q_spec, kv_specs, t_spec


def _a_window(refs, i):
    first = refs[0][...]
    return jnp.concatenate([jnp.where(i > 0, first, jnp.zeros_like(first))] + [r[...] for r in refs[1:]], axis=0)


def _a_scores(q_ref, k, t_ref, i, j):
    rows = slice(j * A_TQ, (j + 1) * A_TQ)
    qs = q_ref[rows, :] * 0.125
    kj = k[j * A_TQ:j * A_TQ + A_BAND, :]
    sc = _dot_nt(qs, kj) + t_ref[...]
    col = lax.broadcasted_iota(jnp.int32, (A_TQ, A_BAND), 1)
    sc = jnp.where(col >= A_PAD - i * A_QB - j * A_TQ, sc, NEG)
    return rows, qs, kj, sc


def mix_a_fwd(qkv, kva, tbias):
    s = qkv.shape[2]
    nq, q_spec, kv_specs, t_spec = _a_specs(s)
    nwin = len(kv_specs) // 2

    def body(*refs):
        q_ref, t_ref, o_ref, lse_ref = refs[0], refs[1 + 2 * nwin], refs[2 + 2 * nwin], refs[3 + 2 * nwin]
        i = pl.program_id(1)
        k = _a_window(refs[1:1 + nwin], i)
        v = _a_window(refs[1 + nwin:1 + 2 * nwin], i)
        for j in range(A_QB // A_TQ):
            rows, _, _, sc = _a_scores(q_ref, k, t_ref, i, j)
            m = jnp.max(sc, axis=-1, keepdims=True)
            p = jnp.exp(sc - m)
            l = jnp.sum(p, axis=-1, keepdims=True)
            o_ref[rows, :] = _dot(p.astype(BF16), v[j * A_TQ:j * A_TQ + A_BAND, :]) / l
            lse_ref[rows, :] = m + jnp.log(l)

    return pl.pallas_call(
        body, name="mix_a_fwd", grid=(N_HEADS, nq),
        in_specs=[q_spec] + kv_specs + [t_spec],
        out_specs=[pl.BlockSpec((None, A_QB, HEAD_DIM), lambda h, i: (h, i, 0)),
                   pl.BlockSpec((None, A_QB, 1), lambda h, i: (h, i, 0))],
        out_shape=[jax.ShapeDtypeStruct((N_HEADS, s, HEAD_DIM), F32),
                   jax.ShapeDtypeStruct((N_HEADS, s, 1), F32)],
        compiler_params=_params(("arbitrary", "arbitrary")),
    )(qkv, *([kva] * (2 * nwin)), tbias)


def mix_a_bwd(qkv, kva, tbias, do, o, lse):
    s = qkv.shape[2]
    nq, q_spec, kv_specs, t_spec = _a_specs(s)
    nwin = len(kv_specs) // 2
    row_spec = lambda w: pl.BlockSpec((None, A_QB, w), lambda h, i: (h, jnp.minimum(i, nq - 1), 0))
    done_spec = pl.BlockSpec((None, A_QB, HEAD_DIM), lambda h, i: (h, i, 0))
    win = A_QB + A_PAD

    def body(*refs):
        q_ref = refs[0]
        t_ref, do_ref, o_ref, lse_ref, dq_ref, dk_ref, dv_ref, dt_ref, dk_win, dv_win = refs[1 + 2 * nwin:]
        i = pl.program_id(1)

        @pl.when(i == 0)
        def _():
            dk_win[...] = jnp.zeros_like(dk_win)
            dv_win[...] = jnp.zeros_like(dv_win)
            dt_ref[...] = jnp.zeros_like(dt_ref)

        @pl.when(i < nq)
        def _():
            k = _a_window(refs[1:1 + nwin], i)
            v = _a_window(refs[1 + nwin:1 + 2 * nwin], i)
            dt = jnp.zeros((A_TQ, A_BAND), F32)
            for j in range(A_QB // A_TQ):
                rows, qs, kj, sc = _a_scores(q_ref, k, t_ref, i, j)
                keys = slice(j * A_TQ, j * A_TQ + A_BAND)
                dob = do_ref[rows, :]
                p = jnp.exp(sc - lse_ref[rows, :])
                delta = jnp.sum(o_ref[rows, :] * dob.astype(F32), axis=-1, keepdims=True)
                ds = p * (_dot_nt(dob, v[keys, :]) - delta)
                dsb = ds.astype(BF16)
                dq_ref[rows, :] = _dot(dsb, kj) * 0.125
                dk_win[keys, :] += _dot_tn(dsb, qs)
                dv_win[keys, :] += _dot_tn(p.astype(BF16), dob)
                dt = dt + ds
            dt_ref[...] += dt

        dk_ref[...] = dk_win[0:A_QB, :]
        dv_ref[...] = dv_win[0:A_QB, :]
        dk_rest = dk_win[A_QB:win, :]
        dv_rest = dv_win[A_QB:win, :]
        dk_win[0:A_PAD, :] = dk_rest
        dv_win[0:A_PAD, :] = dv_rest
        dk_win[A_PAD:win, :] = jnp.zeros((A_QB, HEAD_DIM), F32)
        dv_win[A_PAD:win, :] = jnp.zeros((A_QB, HEAD_DIM), F32)

    return pl.pallas_call(
        body, name="mix_a_bwd", grid=(N_HEADS, nq + 1),
        in_specs=[q_spec] + kv_specs + [t_spec, row_spec(HEAD_DIM), row_spec(HEAD_DIM), row_spec(1)],
        out_specs=[row_spec(HEAD_DIM), done_spec, done_spec, t_spec],
        out_shape=[jax.ShapeDtypeStruct((N_HEADS, s, HEAD_DIM), F32),
                   jax.ShapeDtypeStruct((N_HEADS, s + A_QB, HEAD_DIM), F32),
                   jax.ShapeDtypeStruct((N_HEADS, s + A_QB, HEAD_DIM), F32),
                   jax.ShapeDtypeStruct((N_HEADS, A_TQ, A_BAND), F32)],
        scratch_shapes=[pltpu.VMEM((win, HEAD_DIM), F32), pltpu.VMEM((win, HEAD_DIM), F32)],
        compiler_params=_params(("arbitrary", "arbitrary")),
    )(qkv, *([kva] * (2 * nwin)), tbias, do, o, lse)


def relbias_tile(rel_bias, relmat):
    def body(rb_ref, rel_ref, o_ref):
        rel = rel_ref[...]
        o_ref[...] = jnp.full(o_ref.shape, NEG, F32)

        def step(r, carry):
            hit = rel == r
            for h in range(N_HEADS):
                o_ref[h] = jnp.where(hit, rb_ref[h, r], o_ref[h])
            return carry

        lax.fori_loop(REL_LO, REL_HI, step, 0)

    return pl.pallas_call(
        body, name="relbias_tile",
        in_specs=[pl.BlockSpec(memory_space=pltpu.SMEM), pl.BlockSpec(memory_space=pltpu.VMEM)],
        out_specs=pl.BlockSpec(memory_space=pltpu.VMEM),
        out_shape=jax.ShapeDtypeStruct((N_HEADS, A_TQ, A_BAND), F32),
        compiler_params=_params(),
    )(rel_bias, relmat)


def relbias_grad(dt, relmat):
    def body(dt_ref, rel_ref, o_ref):
        rel = rel_ref[...]
        lane = lax.broadcasted_iota(jnp.int32, (8, 384), 1)
        row = lax.broadcasted_iota(jnp.int32, (8, 384), 0)

        def step(r, acc):
            hit = rel == r
            for h in range(N_HEADS):
                val = jnp.sum(jnp.where(hit, dt_ref[h], 0.0))
                acc = jnp.where((lane == r) & (row == h), val, acc)
            return acc

        o_ref[...] = lax.fori_loop(REL_LO, REL_HI, step, jnp.zeros((8, 384), F32))

    return pl.pallas_call(
        body, name="relbias_grad",
        out_shape=jax.ShapeDtypeStruct((8, 384), F32),
        compiler_params=_params(),
    )(dt, relmat)


def _b_norm(v, gain):
    mu = jnp.mean(v, axis=-1, keepdims=True)
    xc = v - mu
    rstd = lax.rsqrt(jnp.mean(xc * xc, axis=-1, keepdims=True) + EPS)
    xhat = xc * rstd
    return xhat, rstd, xhat * gain


def _tril_mask():
    t = lax.broadcasted_iota(jnp.int32, (SG_CHUNK, SG_CHUNK), 0)
    u = lax.broadcasted_iota(jnp.int32, (SG_CHUNK, SG_CHUNK), 1)
    return u <= t


def mix_b_fwd(uv, gain, w_s, b_col):
    s = uv.shape[0]
    tm = min(ROW_T, s)

    def body(uv_ref, gain_ref, w_ref, b_ref, y_ref):
        tril = _tril_mask()
        ws = [jnp.where(tril, w_ref[g], 0.0).astype(BF16) for g in range(N_HEADS)]
        for c in range(tm // SG_CHUNK):
            rows = slice(c * SG_CHUNK, (c + 1) * SG_CHUNK)
            u = uv_ref[rows, 0:D_BRANCH]
            _, _, vn = _b_norm(uv_ref[rows, D_BRANCH:2 * D_BRANCH], gain_ref[...])
            vnb = vn.astype(BF16)
            outs = []
            for g in range(N_HEADS):
                cols = slice(g * HEAD_DIM, (g + 1) * HEAD_DIM)
                mixed = _dot(ws[g], vnb[:, cols]) + b_ref[g]
                outs.append(u[:, cols] * mixed)
            y_ref[rows, :] = jnp.concatenate(outs, axis=1)

    return pl.pallas_call(
        body, name="mix_b_fwd", grid=(s // tm,),
        in_specs=[pl.BlockSpec((tm, 2 * D_BRANCH), lambda i: (i, 0)),
                  pl.BlockSpec((1, D_BRANCH), lambda i: (0, 0)),
                  pl.BlockSpec((N_HEADS, SG_CHUNK, SG_CHUNK), lambda i: (0, 0, 0)),
                  pl.BlockSpec((N_HEADS, SG_CHUNK, 1), lambda i: (0, 0, 0))],
        out_specs=pl.BlockSpec((tm, D_BRANCH), lambda i: (i, 0)),
        out_shape=jax.ShapeDtypeStruct((s, D_BRANCH), F32),
        compiler_params=_params(("arbitrary",)),
    )(uv, gain, w_s, b_col)


def mix_b_bwd(uv, gain, w_s, b_col, dy):
    s = uv.shape[0]
    tm = min(ROW_T, s)

    def body(uv_ref, gain_ref, w_ref, b_ref, dy_ref, duv_ref, dw_ref, db_ref, dgain_ref):
        i = pl.program_id(0)

        @pl.when(i == 0)
        def _():
            dw_ref[...] = jnp.zeros_like(dw_ref)
            db_ref[...] = jnp.zeros_like(db_ref)
            dgain_ref[...] = jnp.zeros_like(dgain_ref)

        tril = _tril_mask()
        ws = [jnp.where(tril, w_ref[g], 0.0).astype(BF16) for g in range(N_HEADS)]
        gain_v = gain_ref[...]
        for c in range(tm // SG_CHUNK):
            rows = slice(c * SG_CHUNK, (c + 1) * SG_CHUNK)
            u = uv_ref[rows, 0:D_BRANCH]
            xhat, rstd, vn = _b_norm(uv_ref[rows, D_BRANCH:2 * D_BRANCH], gain_v)
            vnb = vn.astype(BF16)
            dyv = dy_ref[rows, :]
            dus, dvns = [], []
            for g in range(N_HEADS):
                cols = slice(g * HEAD_DIM, (g + 1) * HEAD_DIM)
                mixed = _dot(ws[g], vnb[:, cols]) + b_ref[g]
                dus.append(dyv[:, cols] * mixed)
                dmixed = dyv[:, cols] * u[:, cols]
                dmb = dmixed.astype(BF16)
                db_ref[g] += jnp.sum(dmixed, axis=-1, keepdims=True)
                dw_ref[g] += jnp.where(tril, _dot_nt(dmb, vnb[:, cols]), 0.0)
                dvns.append(_dot_tn(ws[g], dmb))
            dvn = jnp.concatenate(dvns, axis=1)
            dgain_ref[...] += jnp.sum(dvn * xhat, axis=0, keepdims=True)
            dxh = dvn * gain_v
            dv = rstd * (dxh - jnp.mean(dxh, axis=-1, keepdims=True)
                         - xhat * jnp.mean(dxh * xhat, axis=-1, keepdims=True))
            duv_ref[rows, :] = jnp.concatenate(dus + [dv], axis=1)

    return pl.pallas_call(
        body, name="mix_b_bwd", grid=(s // tm,),
        in_specs=[pl.BlockSpec((tm, 2 * D_BRANCH), lambda i: (i, 0)),
                  pl.BlockSpec((1, D_BRANCH), lambda i: (0, 0)),
                  pl.BlockSpec((N_HEADS, SG_CHUNK, SG_CHUNK), lambda i: (0, 0, 0)),
                  pl.BlockSpec((N_HEADS, SG_CHUNK, 1), lambda i: (0, 0, 0)),
                  pl.BlockSpec((tm, D_BRANCH), lambda i: (i, 0))],
        out_specs=[pl.BlockSpec((tm, 2 * D_BRANCH), lambda i: (i, 0)),
                   pl.BlockSpec((N_HEADS, SG_CHUNK, SG_CHUNK), lambda i: (0, 0, 0)),
                   pl.BlockSpec((N_HEADS, SG_CHUNK, 1), lambda i: (0, 0, 0)),
                   pl.BlockSpec((1, D_BRANCH), lambda i: (0, 0))],
        out_shape=[jax.ShapeDtypeStruct((s, 2 * D_BRANCH), F32),
                   jax.ShapeDtypeStruct((N_HEADS, SG_CHUNK, SG_CHUNK), F32),
                   jax.ShapeDtypeStruct((N_HEADS, SG_CHUNK, 1), F32),
                   jax.ShapeDtypeStruct((1, D_BRANCH), F32)],
        compiler_params=_params(("arbitrary",)),
    )(uv, gain, w_s, b_col, dy)


def _scan_mats(nrow):
    a = lax.broadcasted_iota(jnp.int32, (128, 128), 0)
    b = lax.broadcasted_iota(jnp.int32, (128, 128), 1)
    r = lax.broadcasted_iota(jnp.int32, (nrow, nrow), 0)
    c = lax.broadcasted_iota(jnp.int32, (nrow, nrow), 1)
    nb = nrow // N_HEADS
    same = (r // nb) == (c // nb)
    return a, b, r, c, same


def _exact_dot(x, m):
    hi, mid, lo = _split3(x)
    return _dot(hi, m) + _dot(mid, m) + _dot(lo, m)


def _exact_dot_left(m, x):
    hi, mid, lo = _split3(x)
    return _dot(m, hi) + _dot(m, mid) + _dot(m, lo)


def fox_gate_fwd(ft, bcol):
    nrow = ft.shape[0]

    def body(f_ref, b_ref, c_ref):
        z = f_ref[...] + b_ref[...]
        ls = jnp.minimum(z, 0.0) - jnp.log(1.0 + jnp.exp(-jnp.abs(z)))
        a, b, r, c, same = _scan_mats(nrow)
        within = _exact_dot(ls, (a <= b).astype(BF16))
        tot = jnp.broadcast_to(within[:, 127:128], within.shape)
        before = _exact_dot_left((same & (c < r)).astype(BF16), tot)
        c_ref[...] = within + before

    return pl.pallas_call(
        body, name="fox_gate_fwd",
        out_shape=jax.ShapeDtypeStruct((nrow, 128), F32),
        compiler_params=_params(),
    )(ft, bcol)


def fox_gate_bwd(ft, bcol, dc):
    nrow = ft.shape[0]

    def body(f_ref, b_ref, dc_ref, df_ref, db_ref):
        a, b, r, c, same = _scan_mats(nrow)
        dcv = dc_ref[...]
        within = _exact_dot(dcv, (a >= b).astype(BF16))
        tot = jnp.broadcast_to(within[:, 0:1], within.shape)
        after = _exact_dot_left((same & (c > r)).astype(BF16), tot)
        dls = within + after
        z = f_ref[...] + b_ref[...]
        dz = dls * _sigmoid(-z)
        df_ref[...] = dz
        rs = jnp.broadcast_to(jnp.sum(dz, axis=-1, keepdims=True), dz.shape)
        hr = lax.broadcasted_iota(jnp.int32, (8, nrow), 0)
        hc = lax.broadcasted_iota(jnp.int32, (8, nrow), 1)
        db_ref[...] = _exact_dot_left((hr == hc // (nrow // N_HEADS)).astype(BF16), rs)

    return pl.pallas_call(
        body, name="fox_gate_bwd",
        out_shape=[jax.ShapeDtypeStruct((nrow, 128), F32), jax.ShapeDtypeStruct((8, 128), F32)],
        compiler_params=_params(),
    )(ft, bcol, dc)


def _att_specs(s, qi, ki, vi):
    q_spec = pl.BlockSpec((None, None, FOX_TQ, HEAD_DIM), lambda h, i: (qi, h, i, 0))
    k_spec = pl.BlockSpec((None, None, s, HEAD_DIM), lambda h, i: (ki, h, 0, 0))
    v_spec = pl.BlockSpec((None, None, s, HEAD_DIM), lambda h, i: (vi, h, 0, 0))
    row_spec = lambda w: pl.BlockSpec((None, FOX_TQ, w), lambda h, i: (h, i, 0))
    gate_spec = pl.BlockSpec((None, s // ATT_T, 1, ATT_T), lambda h, i: (h, 0, 0, 0))
    return q_spec, k_spec, v_spec, row_spec, gate_spec


def _causal(strict, n=ATT_T):
    row = lax.broadcasted_iota(jnp.int32, (n, n), 0)
    col = lax.broadcasted_iota(jnp.int32, (n, n), 1)
    return (col < row) if strict else (col <= row)


def _gate_row(cr_ref, kb, g):
    if g == 1:
        return cr_ref[kb]
    return jnp.concatenate([cr_ref[kb + n] for n in range(g)], axis=1)


def _fox_walk(i, carry, tile, alive):
    g = FOX_WIDE
    own = FOX_TQ // ATT_T
    nwide = (own * i) // g
    carry = tile(own * i, own, carry, True)
    carry = lax.fori_loop(0, (own * i - nwide * g) // own, lambda n, c: tile(nwide * g, own, c, False), carry)

    def cond(state):
        return jnp.logical_and(state[0] >= 0, state[1] > 0)

    def step(state):
        n = state[0]
        c = tile(n * g, g, state[2:], False)
        return (n - 1, alive(n * g, c)) + tuple(c)

    out = lax.while_loop(cond, step, (nwide - 1, alive(nwide * g, carry)) + tuple(carry))
    return out[2:]


def _fox_reach(qs, k_ref, kmax_ref, cc, i):
    s = k_ref.shape[0]
    rows = 4 * ATT_T

    @pl.when(i == 0)
    def _():
        def chunk(n, mx):
            kc = k_ref[pl.ds(pl.multiple_of(n * rows, rows), rows), :].astype(F32)
            return jnp.maximum(mx, jnp.max(jnp.sum(kc * kc, axis=-1, keepdims=True)))

        kmax_ref[0] = jnp.sqrt(lax.fori_loop(0, s // rows, chunk, jnp.float32(0.0)))

    qf = qs.astype(F32)
    return jnp.sqrt(jnp.sum(qf * qf, axis=-1, keepdims=True)) * kmax_ref[0] + cc


def _gate_col(cr_ref, i):
    row = lax.broadcasted_iota(jnp.int32, (ATT_T, ATT_T), 0)
    col = lax.broadcasted_iota(jnp.int32, (ATT_T, ATT_T), 1)
    own = FOX_TQ // ATT_T
    return jnp.concatenate([jnp.sum(jnp.where(row == col, cr_ref[own * i + n], 0.0), axis=-1, keepdims=True)
                            for n in range(own)], axis=0)


def _fox_scores(qs, k, cc, crow, masked):
    sc = (_dot_nt(qs, k) + (cc - crow)) * LOG2E
    if masked:
        sc = jnp.where(_causal(False, FOX_TQ), sc, NEG)
    return sc


def fox_fwd(qkv, c_row):
    s = qkv.shape[2]
    t = ATT_T
    nq = s // FOX_TQ
    q_spec, k_spec, v_spec, row_spec, gate_spec = _att_specs(s, 1, 2, 3)
    rows = 4 * t

    def body(q_ref, k_ref, v_ref, cr_ref, o_ref, ref_ref, rl_ref, v1_ref, kmax_ref):
        i = pl.program_id(1)

        @pl.when(i == 0)
        def _():
            def chunk(n, carry):
                r0 = pl.multiple_of(n * rows, rows)
                v1_ref[pl.ds(r0, rows), :] = jnp.concatenate(
                    [v_ref[pl.ds(r0, rows), :], jnp.ones((rows, HEAD_DIM), BF16)], axis=1)
                return carry

            lax.fori_loop(0, s // rows, chunk, 0)

        qs = q_ref[...] * 0.125
        cc = _gate_col(cr_ref, i)
        reach = _fox_reach(qs, k_ref, kmax_ref, cc, i) * LOG2E

        def alive(kb, carry):
            return (jnp.max(reach - cr_ref[kb][:, 0:1] * LOG2E - carry[0]) > FOX_DEAD2).astype(jnp.int32)

        def tile(kb, g, carry, masked):
            m, acc = carry
            k0 = pl.multiple_of(kb * t, t)
            sc = _fox_scores(qs, k_ref[pl.ds(k0, g * t), :], cc, _gate_row(cr_ref, kb, g), masked)
            m_new = jnp.maximum(m, jnp.ceil(jnp.max(sc, axis=-1, keepdims=True)))
            pb = jnp.exp2(sc - m_new).astype(BF16)
            acc = jnp.exp2(m - m_new) * acc + _dot(pb, v1_ref[pl.ds(k0, g * t), :])
            return m_new, acc

        init = (jnp.full((FOX_TQ, 1), NEG, F32), jnp.zeros((FOX_TQ, 2 * HEAD_DIM), F32))
        m, acc = _fox_walk(i, init, tile, alive)
        rl = 1.0 / acc[:, HEAD_DIM:HEAD_DIM + 1]
        o_ref[...] = acc[:, 0:HEAD_DIM] * rl
        ref_ref[...] = m
        rl_ref[...] = rl

    return pl.pallas_call(
        body, name="fox_fwd", grid=(N_HEADS, nq),
        in_specs=[q_spec, k_spec, v_spec, gate_spec],
        out_specs=[row_spec(HEAD_DIM), row_spec(1), row_spec(1)],
        out_shape=[jax.ShapeDtypeStruct((N_HEADS, s, HEAD_DIM), F32),
                   jax.ShapeDtypeStruct((N_HEADS, s, 1), F32),
                   jax.ShapeDtypeStruct((N_HEADS, s, 1), F32)],
        scratch_shapes=[pltpu.VMEM((s, 2 * HEAD_DIM), BF16), pltpu.SMEM((1,), F32)],
        compiler_params=_params(("arbitrary", "arbitrary")),
    )(qkv, qkv, qkv, c_row)


def fox_bwd(qkv, c_row, do, o, ref, rl):
    s = qkv.shape[2]
    t = ATT_T
    nq = s // FOX_TQ
    q_spec, k_spec, v_spec, row_spec, gate_spec = _att_specs(s, 1, 2, 3)
    any_spec = pl.BlockSpec(memory_space=pl.ANY)

    def body(q_ref, k_ref, v_ref, cr_ref, do_ref, o_ref, ref_ref, rl_ref,
             dq_ref, dk_hbm, dv_hbm, dc_ref, dk_acc, dv_acc, kmax_ref):
        h = pl.program_id(0)
        i = pl.program_id(1)

        @pl.when(i == 0)
        def _():
            dk_acc[...] = jnp.zeros_like(dk_acc)
            dv_acc[...] = jnp.zeros_like(dv_acc)
            dc_ref[...] = jnp.zeros_like(dc_ref)

        qs = q_ref[...] * 0.125
        ref = ref_ref[...]
        rl = rl_ref[...]
        dob = (do_ref[...].astype(F32) * rl).astype(BF16)
        delta = jnp.sum(o_ref[...] * dob.astype(F32), axis=-1, keepdims=True)
        cc = _gate_col(cr_ref, i)
        margin = _fox_reach(qs, k_ref, kmax_ref, cc, i) * LOG2E - ref

        def alive(kb, carry):
            return (jnp.max(margin - cr_ref[kb][:, 0:1] * LOG2E) > FOX_DEAD2).astype(jnp.int32)

        def tile(kb, g, carry, masked):
            dq, = carry
            k0 = pl.multiple_of(kb * t, t)
            k = k_ref[pl.ds(k0, g * t), :]
            sc = _fox_scores(qs, k, cc, _gate_row(cr_ref, kb, g), masked)
            wb = jnp.exp2(sc - ref).astype(BF16)
            ds = wb.astype(F32) * (_dot_nt(dob, v_ref[pl.ds(k0, g * t), :]) - delta)
            dsb = ds.astype(BF16)
            dk_acc[pl.ds(k0, g * t), :] += _dot_tn(dsb, qs)
            dv_acc[pl.ds(k0, g * t), :] += _dot_tn(wb, dob)
            dcs = -jnp.sum(ds, axis=0, keepdims=True)
            for n in range(g):
                dc_ref[kb + n] += dcs[:, n * t:(n + 1) * t]
            return (dq + _dot(dsb, k),)

        dq, = _fox_walk(i, (jnp.zeros((FOX_TQ, HEAD_DIM), F32),), tile, alive)
        dq_ref[...] = dq * 0.125

        @pl.when(i == nq - 1)
        def _():
            pltpu.sync_copy(dk_acc, dk_hbm.at[h])
            pltpu.sync_copy(dv_acc, dv_hbm.at[h])

    return pl.pallas_call(
        body, name="fox_bwd", grid=(N_HEADS, nq),
        in_specs=[q_spec, k_spec, v_spec,
                  gate_spec,
                  row_spec(HEAD_DIM), row_spec(HEAD_DIM), row_spec(1), row_spec(1)],
        out_specs=[row_spec(HEAD_DIM), any_spec, any_spec,
                   gate_spec],
        out_shape=[jax.ShapeDtypeStruct((N_HEADS, s, HEAD_DIM), F32),
                   jax.ShapeDtypeStruct((N_HEADS, s, HEAD_DIM), F32),
                   jax.ShapeDtypeStruct((N_HEADS, s, HEAD_DIM), F32),
                   jax.ShapeDtypeStruct((N_HEADS, s // t, 1, t), F32)],
        scratch_shapes=[pltpu.VMEM((s, HEAD_DIM), F32), pltpu.VMEM((s, HEAD_DIM), F32), pltpu.SMEM((1,), F32)],
        compiler_params=_params(("arbitrary", "arbitrary")),
    )(qkv, qkv, qkv, c_row, do, o, ref, rl)


def _sb_valid(nrows, ahead):
    row = lax.broadcasted_iota(jnp.int32, (nrows, ATT_T), 0)
    col = lax.broadcasted_iota(jnp.int32, (nrows, ATT_T), 1)
    return col + ahead < row


def _sb_tile(qs, k, run, ahead):
    z = _dot_nt(qs, k)
    sp = jnp.log(1.0 + jnp.exp(-jnp.abs(z)))
    ls = jnp.minimum(z, 0.0) - sp
    lm = -jnp.maximum(z, 0.0) - sp
    masked = ahead is not None
    if masked:
        valid = _sb_valid(qs.shape[0], ahead)
        lm = jnp.where(valid, lm, 0.0)
    row = lax.broadcasted_iota(jnp.int32, (ATT_T, ATT_T), 0)
    col = lax.broadcasted_iota(jnp.int32, (ATT_T, ATT_T), 1)
    later = (row > col).astype(BF16)
    hi, lo = _split2(lm)
    between = run + _dot(hi, later) + _dot(lo, later)
    a = jnp.exp(ls + between)
    if masked:
        a = jnp.where(valid, a, 0.0)
    return ls, lm, a


def _sb_walk(i, carry, tile):
    def alive_of(c):
        return (jnp.max(c[0]) > SB_DEAD).astype(jnp.int32)

    def cond(state):
        n, alive = state[0], state[1]
        return jnp.logical_and(n < i, alive > 0)

    def step(state):
        n = state[0]
        c = tile(i - 1 - n, state[2:], False)
        return (n + 1, alive_of(c)) + tuple(c)

    out = lax.while_loop(cond, step, (jnp.int32(0), alive_of(carry)) + tuple(carry))
    return out[2:]


def _sb_specs(s):
    tq = 2 * ATT_T
    q_spec = pl.BlockSpec((None, None, tq, HEAD_DIM), lambda h, i: (4, h, i, 0))
    k_spec = pl.BlockSpec((None, None, s, HEAD_DIM), lambda h, i: (5, h, 0, 0))
    v_spec = pl.BlockSpec((None, None, s, HEAD_DIM), lambda h, i: (6, h, 0, 0))
    row_spec = pl.BlockSpec((None, tq, HEAD_DIM), lambda h, i: (h, i, 0))
    return tq, q_spec, k_spec, v_spec, row_spec


def _sb_block(i, tile, zero):
    t = ATT_T
    lo, hi, both = slice(0, t), slice(t, 2 * t), slice(0, 2 * t)
    c_hi = tile(2 * i + 1, hi, zero, 0)
    c_lo = tile(2 * i, lo, zero, 0)
    c_hi = tile(2 * i, hi, c_hi, None)
    carry = tuple(jnp.concatenate([a, b], axis=0) for a, b in zip(c_lo, c_hi))
    return _sb_walk(2 * i, carry, lambda kb, c, _: tile(kb, both, c, None))


def sb_fwd(qkv):
    s = qkv.shape[2]
    t = ATT_T
    tq, q_spec, k_spec, v_spec, row_spec = _sb_specs(s)

    def body(q_ref, k_ref, v_ref, o_ref):
        i = pl.program_id(1)
        qs = q_ref[...] * 0.125

        def tile(kb, rows, carry, ahead):
            run, acc = carry
            k0 = pl.multiple_of(kb * t, t)
            _, lm, a = _sb_tile(qs[rows], k_ref[pl.ds(k0, t), :], run, ahead)
            acc = acc + _dot(a.astype(BF16), v_ref[pl.ds(k0, t), :])
            return run + jnp.sum(lm, axis=-1, keepdims=True), acc

        _, acc = _sb_block(i, tile, (jnp.zeros((t, 1), F32), jnp.zeros((t, HEAD_DIM), F32)))
        o_ref[...] = acc

    return pl.pallas_call(
        body, name="sb_fwd", grid=(N_HEADS, s // tq),
        in_specs=[q_spec, k_spec, v_spec],
        out_specs=row_spec,
        out_shape=jax.ShapeDtypeStruct((N_HEADS, s, HEAD_DIM), F32),
        compiler_params=_params(("arbitrary", "arbitrary")),
    )(qkv, qkv, qkv)


def sb_bwd(qkv, do, o):
    s = qkv.shape[2]
    t = ATT_T
    tq, q_spec, k_spec, v_spec, row_spec = _sb_specs(s)
    nq = s // tq
    any_spec = pl.BlockSpec(memory_space=pl.ANY)

    def body(q_ref, k_ref, v_ref, do_ref, o_ref, dq_ref, dk_hbm, dv_hbm, dk_acc, dv_acc):
        h = pl.program_id(0)
        i = pl.program_id(1)

        @pl.when(i == 0)
        def _():
            dk_acc[...] = jnp.zeros_like(dk_acc)
            dv_acc[...] = jnp.zeros_like(dv_acc)

        qs_all = q_ref[...] * 0.125
        dob_all = do_ref[...]
        tot_all = jnp.sum(o_ref[...] * dob_all.astype(F32), axis=-1, keepdims=True)

        def tile(kb, rows, carry, ahead):
            run, run_g, dq = carry
            qs, dob, tot = qs_all[rows], dob_all[rows], tot_all[rows]
            k0 = pl.multiple_of(kb * t, t)
            k = k_ref[pl.ds(k0, t), :]
            ls, lm, a = _sb_tile(qs, k, run, ahead)
            ab = a.astype(BF16)
            g = ab.astype(F32) * _dot_nt(dob, v_ref[pl.ds(k0, t), :])
            row = lax.broadcasted_iota(jnp.int32, (t, t), 0)
            col = lax.broadcasted_iota(jnp.int32, (t, t), 1)
            from_here = (row >= col).astype(BF16)
            hi, lo = _split2(g)
            g_right = run_g + _dot(hi, from_here) + _dot(lo, from_here)
            g_left = tot - g_right
            dz = g - jnp.exp(ls) * (g + g_left)
            if ahead is not None:
                dz = jnp.where(_sb_valid(qs.shape[0], ahead), dz, 0.0)
            dzb = dz.astype(BF16)
            dk_acc[pl.ds(k0, t), :] += _dot_tn(dzb, qs)
            dv_acc[pl.ds(k0, t), :] += _dot_tn(ab, dob)
            return (run + jnp.sum(lm, axis=-1, keepdims=True),
                    run_g + jnp.sum(g, axis=-1, keepdims=True),
                    dq + _dot(dzb, k))

        zero = jnp.zeros((t, 1), F32)
        _, _, dq = _sb_block(i, tile, (zero, zero, jnp.zeros((t, HEAD_DIM), F32)))
        dq_ref[...] = dq * 0.125

        @pl.when(i == nq - 1)
        def _():
            pltpu.sync_copy(dk_acc, dk_hbm.at[h])
            pltpu.sync_copy(dv_acc, dv_hbm.at[h])

    return pl.pallas_call(
        body, name="sb_bwd", grid=(N_HEADS, nq),
        in_specs=[q_spec, k_spec, v_spec, row_spec, row_spec],
        out_specs=[row_spec, any_spec, any_spec],
        out_shape=[jax.ShapeDtypeStruct((N_HEADS, s, HEAD_DIM), F32)] * 3,
        scratch_shapes=[pltpu.VMEM((s, HEAD_DIM), F32), pltpu.VMEM((s, HEAD_DIM), F32)],
        compiler_params=_params(("arbitrary", "arbitrary")),
    )(qkv, qkv, qkv, do, o)


def _branch_inputs(refs, br):
    ya_ref, yb_ref, yc_ref, yd_ref = refs
    if br == 1:
        return yb_ref[...]
    return _heads_to_lanes((ya_ref, None, yc_ref, yd_ref)[br])


def outproj_fwd(x, ya, yb, yc, yd, gates, bg, wout):
    s = x.shape[0]
    tm = min(ROW_T, s)

    def body(x_ref, ya_ref, yb_ref, yc_ref, yd_ref, gates_ref, bg_ref, w_ref, out_ref):
        pieces = []
        for br in range(4):
            cols = slice(br * D_BRANCH, (br + 1) * D_BRANCH)
            y = _branch_inputs((ya_ref, yb_ref, yc_ref, yd_ref), br)
            r = lax.rsqrt(jnp.mean(y * y, axis=-1, keepdims=True) + EPS)
            gt = gates_ref[:, cols]
            pieces.append((y * r * bg_ref[:, cols]) * (gt * _sigmoid(gt)))
        merged = jnp.concatenate(pieces, axis=1).astype(BF16)
        out_ref[...] = x_ref[...] + _dot(merged, w_ref[...])

    head_spec = pl.BlockSpec((N_HEADS, tm, HEAD_DIM), lambda i: (0, i, 0))
    return pl.pallas_call(
        body, name="outproj_fwd", grid=(s // tm,),
        in_specs=[pl.BlockSpec((tm, D_MODEL), lambda i: (i, 0)),
                  head_spec, pl.BlockSpec((tm, D_BRANCH), lambda i: (i, 0)), head_spec, head_spec,
                  pl.BlockSpec((tm, D_MODEL), lambda i: (i, 0)),
                  pl.BlockSpec((1, D_MODEL), lambda i: (0, 0)),
                  pl.BlockSpec((D_MODEL, D_MODEL), lambda i: (0, 0))],
        out_specs=pl.BlockSpec((tm, D_MODEL), lambda i: (i, 0)),
        out_shape=jax.ShapeDtypeStruct((s, D_MODEL), F32),
        compiler_params=_params(("arbitrary",)),
    )(x, ya, yb, yc, yd, gates, bg, wout)


def outproj_bwd(dout, ya, yb, yc, yd, gates, bg, wout):
    s = dout.shape[0]
    tm = min(ROW_T, s)

    def body(dout_ref, ya_ref, yb_ref, yc_ref, yd_ref, gates_ref, bg_ref, w_ref,
             dya_ref, dyb_ref, dyc_ref, dyd_ref, dgates_ref, dbg_ref, dw_ref):
        i = pl.program_id(0)

        @pl.when(i == 0)
        def _():
            dbg_ref[...] = jnp.zeros_like(dbg_ref)
            dw_ref[...] = jnp.zeros_like(dw_ref)

        doutb = dout_ref[...].astype(BF16)
        dmerged = _dot_nt(doutb, w_ref[...])
        pieces = []
        for br in range(4):
            cols = slice(br * D_BRANCH, (br + 1) * D_BRANCH)
            y = _branch_inputs((ya_ref, yb_ref, yc_ref, yd_ref), br)
            r = lax.rsqrt(jnp.mean(y * y, axis=-1, keepdims=True) + EPS)
            yn = y * r
            bgv = bg_ref[:, cols]
            gt = gates_ref[:, cols]
            sig = _sigmoid(gt)
            act = gt * sig
            n = yn * bgv
            pieces.append(n * act)
            dm = dmerged[:, cols]
            dn = dm * act
            dgates_ref[:, cols] = (dm * n * (sig * (1.0 + gt * (1.0 - sig)))).astype(BF16)
            dbg_ref[:, cols] += jnp.sum(dn * yn, axis=0, keepdims=True)
            u = dn * bgv
            dy = r * (u - yn * jnp.mean(yn * u, axis=-1, keepdims=True))
            if br == 1:
                dyb_ref[...] = dy
            else:
                dref = (dya_ref, None, dyc_ref, dyd_ref)[br]
                for hh in range(N_HEADS):
                    dref[hh] = dy[:, hh * HEAD_DIM:(hh + 1) * HEAD_DIM].astype(BF16)
        merged = jnp.concatenate(pieces, axis=1).astype(BF16)
        dw_ref[...] += _dot_tn(merged, doutb)

    head_spec = pl.BlockSpec((N_HEADS, tm, HEAD_DIM), lambda i: (0, i, 0))
    head_shape = jax.ShapeDtypeStruct((N_HEADS, s, HEAD_DIM), BF16)
    return pl.pallas_call(
        body, name="outproj_bwd", grid=(s // tm,),
        in_specs=[pl.BlockSpec((tm, D_MODEL), lambda i: (i, 0)),
                  head_spec, pl.BlockSpec((tm, D_BRANCH), lambda i: (i, 0)), head_spec, head_spec,
                  pl.BlockSpec((tm, D_MODEL), lambda i: (i, 0)),
                  pl.BlockSpec((1, D_MODEL), lambda i: (0, 0)),
                  pl.BlockSpec((D_MODEL, D_MODEL), lambda i: (0, 0))],
        out_specs=[head_spec, pl.BlockSpec((tm, D_BRANCH), lambda i: (i, 0)), head_spec, head_spec,
                   pl.BlockSpec((tm, D_MODEL), lambda i: (i, 0)),
                   pl.BlockSpec((1, D_MODEL), lambda i: (0, 0)),
                   pl.BlockSpec((D_MODEL, D_MODEL), lambda i: (0, 0))],
        out_shape=[head_shape, jax.ShapeDtypeStruct((s, D_BRANCH), F32), head_shape, head_shape,
                   jax.ShapeDtypeStruct((s, D_MODEL), BF16),
                   jax.ShapeDtypeStruct((1, D_MODEL), F32),
                   jax.ShapeDtypeStruct((D_MODEL, D_MODEL), F32)],
        compiler_params=_params(("arbitrary",)),
    )(dout, ya, yb, yc, yd, gates, bg, wout)


def final_loss(x, tgt, g):
    s = x.shape[0]
    tm = min(ROW_T, s)

    def body(x_ref, t_ref, g_ref, loss_ref, dx_ref, dg_ref):
        i = pl.program_id(0)

        @pl.when(i == 0)
        def _():
            loss_ref[...] = jnp.zeros_like(loss_ref)
            dg_ref[...] = jnp.zeros_like(dg_ref)

        xv = x_ref[...]
        gv = g_ref[...]
        r = lax.rsqrt(jnp.mean(xv * xv, axis=-1, keepdims=True) + EPS)
        xn = xv * r
        err = xn * gv - t_ref[...]
        loss_ref[...] += jnp.sum(err * err) * (0.5 / D_MODEL)
        dy = err * (1.0 / D_MODEL)
        u = dy * gv
        dx_ref[...] = r * (u - xn * jnp.mean(xn * u, axis=-1, keepdims=True))
        dg_ref[...] += jnp.sum(dy * xn, axis=0, keepdims=True)

    return pl.pallas_call(
        body, name="final_loss", grid=(s // tm,),
        in_specs=[pl.BlockSpec((tm, D_MODEL), lambda i: (i, 0)),
                  pl.BlockSpec((tm, D_MODEL), lambda i: (i, 0)),
                  pl.BlockSpec((1, D_MODEL), lambda i: (0, 0))],
        out_specs=[pl.BlockSpec((1, 128), lambda i: (0, 0)),
                   pl.BlockSpec((tm, D_MODEL), lambda i: (i, 0)),
                   pl.BlockSpec((1, D_MODEL), lambda i: (0, 0))],
        out_shape=[jax.ShapeDtypeStruct((1, 128), F32),
                   jax.ShapeDtypeStruct((s, D_MODEL), F32),
                   jax.ShapeDtypeStruct((1, D_MODEL), F32)],
        compiler_params=_params(("arbitrary",)),
    )(x, tgt, g)


def _rel_index():
    i = np.arange(A_TQ)[:, None]
    j = np.arange(A_BAND)[None, :]
    rel = np.clip(i - j + (A_BAND - A_TQ), -MAX_REL, MAX_REL) + MAX_REL
    dchunk = i // CHUNK + LOOKBACK - j // CHUNK
    valid = (dchunk >= 0) & (dchunk <= LOOKBACK)
    return jnp.asarray(np.where(valid, rel, -1).astype(np.int32))


def _layer_consts(p):
    tbias = relbias_tile(p["rel_bias"], _rel_index())
    return dict(
        norm_g=p["norm_g"].reshape(1, D_MODEL),
        v_gain=p["v_gain"].reshape(1, D_BRANCH),
        b_col=p["b_s"].reshape(N_HEADS, SG_CHUNK, 1),
        bg=p["branch_gain"].reshape(1, D_MODEL),
        tbias=tbias,
    )


def _gate_layout(fp, b_f, s):
    nb = s // 128
    ft = fp[:, :N_HEADS].T.reshape(N_HEADS * nb, 128)
    bcol = jnp.repeat(b_f, nb).reshape(N_HEADS * nb, 1)
    return ft, bcol


def layer_fwd(x, p):
    s = x.shape[0]
    c = _layer_consts(p)
    h, qkv, kva, gates, uv, fp = inproj_fwd(x, c["norm_g"], p["wp"])
    ya, lse_a = mix_a_fwd(qkv, kva, c["tbias"])
    yb = mix_b_fwd(uv, c["v_gain"], p["w_s"], c["b_col"])
    ft, bcol = _gate_layout(fp, p["b_f"], s)
    c_row = fox_gate_fwd(ft, bcol).reshape(N_HEADS, s // ATT_T, 1, ATT_T)
    yc, ref_c, rl_c = fox_fwd(qkv, c_row)
    yd = sb_fwd(qkv)
    out = outproj_fwd(x, ya, yb, yc, yd, gates, c["bg"], p["wout"])
    saved = dict(consts=c, x=x, h=h, qkv=qkv, gates=gates, uv=uv, kva=kva, ft=ft, bcol=bcol,
                 c_row=c_row, ya=ya, lse_a=lse_a, yb=yb, yc=yc, ref_c=ref_c, rl_c=rl_c, yd=yd)
    return out, saved


def layer_bwd(dout, p, sv):
    s = dout.shape[0]
    c = sv["consts"]
    dya, dyb, dyc, dyd, dgates, dbg, dwout = outproj_bwd(
        dout, sv["ya"], sv["yb"], sv["yc"], sv["yd"], sv["gates"], c["bg"], p["wout"])
    dqa, dka, dva, dt = mix_a_bwd(sv["qkv"], sv["kva"], c["tbias"], dya, sv["ya"], sv["lse_a"])
    drel = relbias_grad(dt, _rel_index())[:N_HEADS, :2 * MAX_REL + 1]
    duv, dws, dbs, dvgain = mix_b_bwd(sv["uv"], c["v_gain"], p["w_s"], c["b_col"], dyb)
    dqc, dkc, dvc, dc = fox_bwd(sv["qkv"], sv["c_row"], dyc, sv["yc"], sv["ref_c"], sv["rl_c"])
    dft, dbf = fox_gate_bwd(sv["ft"], sv["bcol"], dc.reshape(N_HEADS * (s // 128), 128))
    dfp = jnp.pad(dft.reshape(N_HEADS, s).T, ((0, 0), (0, 128 - N_HEADS)))
    dqd, dkd, dvd = sb_bwd(sv["qkv"], dyd, sv["yd"])
    dp, dx, dnorm = inproj_bwd((dqa, dka, dva, dqc, dkc, dvc, dqd, dkd, dvd), dgates, duv, dfp,
                               p["wp"], sv["x"], c["norm_g"], dout)
    dwp = weight_grad(sv["h"], dp, "inproj_wgrad")
    grads = dict(norm_g=dnorm.reshape(D_MODEL), wp=dwp, b_f=dbf[:N_HEADS, 0], rel_bias=drel,
                 w_s=dws, b_s=dbs.reshape(N_HEADS, SG_CHUNK), v_gain=dvgain.reshape(D_BRANCH),
                 branch_gain=dbg.reshape(4, D_BRANCH), wout=dwout)
    return dx, grads


def local_step(x, tgt, layers, final_g):
    saved = []
    cur = x
    for p in layers:
        cur, sv = layer_fwd(cur, p)
        saved.append(sv)
    loss, dcur, dfinal = final_loss(cur, tgt, final_g.reshape(1, D_MODEL))
    grads = [None] * len(layers)
    for l in reversed(range(len(layers))):
        dcur, grads[l] = layer_bwd(dcur, layers[l], saved[l])
    return loss[0, 0], dcur, grads, dfinal.reshape(D_MODEL)


def gather_weights(wb, wf):
    def body(wb_ref, wf_ref, ob_ref, of_ref, send_sems, recv_sems, loc_sems):
        x, y, c = lax.axis_index("x"), lax.axis_index("y"), lax.axis_index("c")
        me = 2 * x + y
        chips = [(1 - x, y), (x, 1 - y), (1 - x, 1 - y)]
        pairs = [(wb_ref, ob_ref), (wf_ref, of_ref)]
        local = [pltpu.make_async_copy(src, dst.at[me], loc_sems.at[n]) for n, (src, dst) in enumerate(pairs)]
        for cp in local:
            cp.start()

        def copy(j, n, slot):
            src, dst = pairs[n]
            return pltpu.make_async_remote_copy(
                src_ref=src, dst_ref=dst.at[slot], send_sem=send_sems.at[2 * j + n], recv_sem=recv_sems.at[2 * j + n],
                device_id=(chips[j][0], chips[j][1], c), device_id_type=MESH)

        sends = [copy(j, n, me) for j in range(3) for n in range(2)]
        for cp in sends:
            cp.start()
        for j in range(3):
            for n in range(2):
                copy(j, n, 2 * chips[j][0] + chips[j][1]).wait_recv()
        for cp in sends:
            cp.wait_send()
        for cp in local:
            cp.wait()

    any_spec = pl.BlockSpec(memory_space=pl.ANY)
    return pl.pallas_call(
        body, name="gather_weights",
        in_specs=[any_spec, any_spec], out_specs=[any_spec, any_spec],
        out_shape=[jax.ShapeDtypeStruct((4,) + wb.shape, wb.dtype), jax.ShapeDtypeStruct((4,) + wf.shape, wf.dtype)],
        scratch_shapes=[pltpu.SemaphoreType.DMA((6,)), pltpu.SemaphoreType.DMA((6,)), pltpu.SemaphoreType.DMA((2,))],
    )(wb, wf)


def exchange_grads(big, small):
    def body(b_ref, s_ref, rb_ref, rs_ref, send_sems, recv_sems, loc_sems):
        x, y, c = lax.axis_index("x"), lax.axis_index("y"), lax.axis_index("c")
        me_chip = 2 * x + y
        me = 4 * x + 2 * y + c
        peers = [(x, y, 1 - c)]
        for px, py in [(1 - x, y), (x, 1 - y), (1 - x, 1 - y)]:
            peers += [(px, py, c), (px, py, 1 - c)]
        local = [pltpu.make_async_copy(b_ref.at[me_chip], rb_ref.at[me], loc_sems.at[0]),
                 pltpu.make_async_copy(s_ref, rs_ref.at[me], loc_sems.at[1])]
        for cp in local:
            cp.start()

        def copies(n, chip, slot):
            kw = dict(device_id=peers[n], device_id_type=MESH)
            return [pltpu.make_async_remote_copy(src_ref=b_ref.at[chip], dst_ref=rb_ref.at[slot],
                                                 send_sem=send_sems.at[2 * n], recv_sem=recv_sems.at[2 * n], **kw),
                    pltpu.make_async_remote_copy(src_ref=s_ref, dst_ref=rs_ref.at[slot],
                                                 send_sem=send_sems.at[2 * n + 1], recv_sem=recv_sems.at[2 * n + 1], **kw)]

        sends = [cp for n, (px, py, _) in enumerate(peers) for cp in copies(n, 2 * px + py, me)]
        for cp in sends:
            cp.start()
        for n, (px, py, pc) in enumerate(peers):
            for cp in copies(n, me_chip, 4 * px + 2 * py + pc):
                cp.wait_recv()
        for cp in sends:
            cp.wait_send()
        for cp in local:
            cp.wait()

    any_spec = pl.BlockSpec(memory_space=pl.ANY)
    return pl.pallas_call(
        body, name="exchange_grads",
        in_specs=[any_spec, any_spec], out_specs=[any_spec, any_spec],
        out_shape=[jax.ShapeDtypeStruct((8,) + big.shape[1:], big.dtype),
                   jax.ShapeDtypeStruct((8,) + small.shape, small.dtype)],
        scratch_shapes=[pltpu.SemaphoreType.DMA((14,)), pltpu.SemaphoreType.DMA((14,)), pltpu.SemaphoreType.DMA((2,))],
    )(big, small)


def adamw_reduce(parts, w, m, v, name):
    rows = w.shape[0]
    tr = PACK_ROW_TILE
    c1 = 1.0 - ADAM_B1 ** ADAM_STEP
    c2 = 1.0 - ADAM_B2 ** ADAM_STEP

    def body(p_ref, w_ref, m_ref, v_ref, g_ref, d_ref, nm_ref, nv_ref):
        g = p_ref[0].astype(F32)
        for n in range(1, 8):
            g = g + p_ref[n].astype(F32)
        g_ref[...] = g
        nm = ADAM_B1 * m_ref[...] + (1.0 - ADAM_B1) * g
        nv = ADAM_B2 * v_ref[...] + (1.0 - ADAM_B2) * (g * g)
        nm_ref[...] = nm
        nv_ref[...] = nv
        d_ref[...] = -ADAM_LR * ((nm / c1) / (jnp.sqrt(nv / c2) + ADAM_EPS) + ADAM_WD * w_ref[...])

    spec = pl.BlockSpec((tr, 128), lambda i: (i, 0))
    shape = jax.ShapeDtypeStruct((rows, 128), F32)
    return pl.pallas_call(
        body, name=name, grid=(rows // tr,),
        in_specs=[pl.BlockSpec((8, tr, 128), lambda i: (0, i, 0)), spec, spec, spec],
        out_specs=[spec] * 4, out_shape=[shape] * 4,
        compiler_params=_params(("arbitrary",)),
    )(parts, w, m, v)


SHARDED = ("w_in", "w_out", "branch_gain")
SMALL = ("norm_g", "b_f", "rel_bias", "w_s", "b_s", "v_gain", "final_g")
WEIGHTS = ("norm_g", "w_in", "b_f", "rel_bias", "w_s", "b_s", "v_gain", "branch_gain", "w_out", "final_g")
PACK_ROW_TILE = 512


def _rows_of(shape):
    return -(-int(np.prod(shape)) // 128)


def _pack(leaves):
    parts = []
    for a in leaves:
        flat = a.reshape(-1).astype(F32)
        parts.append(jnp.pad(flat, (0, _rows_of(a.shape) * 128 - flat.shape[0])))
    flat = jnp.concatenate(parts)
    rows = flat.shape[0] // 128
    total = -(-rows // PACK_ROW_TILE) * PACK_ROW_TILE
    return jnp.pad(flat, (0, (total - rows) * 128)).reshape(total, 128)


def _unpack(slab, shapes):
    out, row = [], 0
    for shp in shapes:
        n = int(np.prod(shp))
        r = _rows_of(shp)
        out.append(slab[row:row + r].reshape(-1)[:n].reshape(shp))
        row += r
    return out


def _pack_w_in(w):
    return jnp.concatenate([w[:, :2816], w[:, 2820:], w[:, 2816:2820],
                            jnp.zeros((w.shape[0], N_PACK - N_IN), w.dtype)], axis=1)


def _unpack_w_in(wp):
    return jnp.concatenate([wp[:, :2816], wp[:, F_COL:F_COL + N_HEADS], wp[:, 2816:F_COL]], axis=1)


def kernel(x, norm_g, w_in, b_f, rel_bias, w_s, b_s, v_gain, branch_gain, w_out, final_g, loss_target, m_norm_g, m_w_in, m_b_f, m_rel_bias, m_w_s, m_b_s, m_v_gain, m_branch_gain, m_w_out, m_final_g, v_norm_g, v_w_in, v_b_f, v_rel_bias, v_w_s, v_b_s, v_v_gain, v_branch_gain, v_w_out, v_final_g):
    depth = norm_g.shape[0]
    weights = dict(norm_g=norm_g, w_in=w_in, b_f=b_f, rel_bias=rel_bias, w_s=w_s, b_s=b_s, v_gain=v_gain,
                   branch_gain=branch_gain, w_out=w_out, final_g=final_g)
    mom1 = dict(norm_g=m_norm_g, w_in=m_w_in, b_f=m_b_f, rel_bias=m_rel_bias, w_s=m_w_s, b_s=m_b_s,
                v_gain=m_v_gain, branch_gain=m_branch_gain, w_out=m_w_out, final_g=m_final_g)
    mom2 = dict(norm_g=v_norm_g, w_in=v_w_in, b_f=v_b_f, rel_bias=v_rel_bias, w_s=v_w_s, b_s=v_b_s,
                v_gain=v_v_gain, branch_gain=v_branch_gain, w_out=v_w_out, final_g=v_final_g)

    n_in_rows = _rows_of(w_in.shape)
    n_out_rows = _rows_of(w_out.shape)
    wb = jnp.concatenate([w_in.astype(BF16).reshape(n_in_rows, 128), w_out.astype(BF16).reshape(n_out_rows, 128)])
    wf = jnp.pad(branch_gain.reshape(-1), (0, 8 * 128 - branch_gain.size)).reshape(8, 128)
    gb, gf = gather_weights(wb, wf)
    w_in_full = gb[:, :n_in_rows].reshape((4,) + w_in.shape)
    w_in_full = jnp.moveaxis(w_in_full, 0, 2).reshape(depth, D_MODEL, N_IN)
    w_out_full = gb[:, n_in_rows:].reshape((4,) + w_out.shape)
    w_out_full = jnp.moveaxis(w_out_full, 0, 1).reshape(depth, D_MODEL, D_MODEL)
    bg_full = gf.reshape(4, -1)[:, :branch_gain.size].reshape((4,) + branch_gain.shape)
    bg_full = jnp.moveaxis(bg_full, 0, 2).reshape(depth, 4, D_BRANCH)

    layers = [dict(norm_g=norm_g[l], wp=_pack_w_in(w_in_full[l]), b_f=b_f[l], rel_bias=rel_bias[l], w_s=w_s[l],
                   b_s=b_s[l], v_gain=v_gain[l], branch_gain=bg_full[l], wout=w_out_full[l]) for l in range(depth)]

    loss_part, grad_x, lgrads, dfinal = local_step(x[0], loss_target[0], layers, final_g)
    loss = lax.psum(loss_part, ("x", "y", "c"))

    stack = lambda k: jnp.stack([g[k] for g in lgrads])
    d_w_in = jnp.stack([_unpack_w_in(g["wp"]) for g in lgrads])
    d_w_out = stack("wout")
    d_bg = stack("branch_gain")
    small = dict(norm_g=stack("norm_g"), b_f=stack("b_f"), rel_bias=stack("rel_bias"), w_s=stack("w_s"),
                 b_s=stack("b_s"), v_gain=stack("v_gain"), final_g=dfinal)
    slabs = []
    for sidx in range(4):
        slabs.append(_pack([d_w_in[:, :, sidx * N_SHARD:(sidx + 1) * N_SHARD],
                            d_w_out[:, sidx * D_BRANCH:(sidx + 1) * D_BRANCH, :],
                            d_bg[:, :, sidx * HEAD_DIM:(sidx + 1) * HEAD_DIM]]).astype(BF16))
    big_parts, small_parts = exchange_grads(jnp.stack(slabs), _pack([small[k] for k in SMALL]))

    outs = {}
    for names, parts, name in ((SHARDED, big_parts, "adamw_big"), (SMALL, small_parts, "adamw_small")):
        pack_local = lambda d: _pack([d[k] for k in names])
        slabs = adamw_reduce(parts, pack_local(weights), pack_local(mom1), pack_local(mom2), name)
        shapes = [weights[k].shape for k in names]
        for tag, slab in zip(("grad", "delta", "new_m", "new_v"), slabs):
            for k, a in zip(names, _unpack(slab, shapes)):
                outs[tag, k] = a
    result = [loss, grad_x[None]]
    for tag in ("grad", "delta", "new_m", "new_v"):
        result += [outs[tag, k] for k in WEIGHTS]
    return tuple(result)
```

```python
import functools

import jax
import jax.numpy as jnp
import numpy as np
from jax import lax
from jax.experimental import pallas as pl
from jax.experimental.pallas import tpu as pltpu

F32 = jnp.float32
BF16 = jnp.bfloat16
MESH = pl.DeviceIdType.MESH

D_MODEL = 1024
D_BRANCH = 256
N_HEADS = 4
HEAD_DIM = 64
CHUNK = 64
LOOKBACK = 8
MAX_REL = 128
SG_CHUNK = 128
EPS = 1e-6
N_IN = 3844
N_PACK = 3968
F_COL = 3840
N_SHARD = 961
NEG = -1e30

A_TQ = 128
A_BAND = A_TQ + LOOKBACK * CHUNK
REL_LO = MAX_REL - (CHUNK - 1)
REL_HI = 2 * MAX_REL + 1
A_PAD = LOOKBACK * CHUNK
A_QB = 1024
ATT_T = 256
FOX_TQ = 512
FOX_WIDE = 4
FOX_DEAD2 = -160.0
LOG2E = 1.4426950408889634
SB_SUB = 128
SB_BACK = 256
SB_BAND = SB_SUB + SB_BACK
SB_DEAD = -110.0
ROW_T = 512
VMEM_LIMIT = 56 * 1024 * 1024

ADAM_LR = 0.001
ADAM_B1 = 0.9
ADAM_B2 = 0.999
ADAM_EPS = 1e-08
ADAM_WD = 0.01
ADAM_STEP = 10

SEC_A_Q, SEC_A_K, SEC_A_V, SEC_A_G = 0, 256, 512, 768
SEC_B_U, SEC_B_V, SEC_B_G = 1024, 1280, 1536
SEC_C_Q, SEC_C_K, SEC_C_V, SEC_C_G = 1792, 2048, 2304, 2560
SEC_D_Q, SEC_D_K, SEC_D_V, SEC_D_G = 2816, 3072, 3328, 3584
QKV_SECS = (SEC_A_Q, SEC_C_Q, SEC_C_K, SEC_C_V, SEC_D_Q, SEC_D_K, SEC_D_V)
GATE_SECS = (SEC_A_G, SEC_B_G, SEC_C_G, SEC_D_G)


def _dot(a, b):
    return jnp.dot(a, b, preferred_element_type=F32)


def _dot_nt(a, b):
    return lax.dot_general(a, b, (((1,), (1,)), ((), ())), preferred_element_type=F32)


def _dot_tn(a, b):
    return lax.dot_general(a, b, (((0,), (0,)), ((), ())), preferred_element_type=F32)


def _split2(x):
    hi = x.astype(BF16)
    lo = (x - hi.astype(F32)).astype(BF16)
    return hi, lo


def _split3(x):
    hi = x.astype(BF16)
    r = x - hi.astype(F32)
    mid = r.astype(BF16)
    lo = (r - mid.astype(F32)).astype(BF16)
    return hi, mid, lo


def _sigmoid(x):
    return 1.0 / (1.0 + jnp.exp(-x))


def _params(sem=None, vmem=VMEM_LIMIT):
    return pltpu.CompilerParams(dimension_semantics=sem, vmem_limit_bytes=vmem)


def _heads_to_lanes(ref):
    return jnp.concatenate([ref[h] for h in range(N_HEADS)], axis=1)


def inproj_fwd(x, g, wp):
    s = x.shape[0]
    tm = A_PAD

    def body(x_ref, g_ref, w_ref, h_ref, qkv_ref, kva_ref, gates_ref, uv_ref, f_ref):
        xv = x_ref[...]
        r = lax.rsqrt(jnp.mean(xv * xv, axis=-1, keepdims=True) + EPS)
        h = (xv * r * g_ref[...]).astype(BF16)
        h_ref[...] = h
        for n, off in enumerate(QKV_SECS):
            p = _dot(h, w_ref[:, off:off + D_BRANCH])
            for hh in range(N_HEADS):
                qkv_ref[n, hh] = p[:, hh * HEAD_DIM:(hh + 1) * HEAD_DIM].astype(BF16)
        for n, off in enumerate((SEC_A_K, SEC_A_V)):
            p = _dot(h, w_ref[:, off:off + D_BRANCH])
            for hh in range(N_HEADS):
                kva_ref[n, hh] = p[:, hh * HEAD_DIM:(hh + 1) * HEAD_DIM].astype(BF16)
        for n, off in enumerate(GATE_SECS):
            gates_ref[:, n * D_BRANCH:(n + 1) * D_BRANCH] = _dot(h, w_ref[:, off:off + D_BRANCH])
        uv_ref[...] = _dot(h, w_ref[:, SEC_B_U:SEC_B_U + 2 * D_BRANCH])
        f_ref[...] = _dot(h, w_ref[:, F_COL:F_COL + 128])

    return pl.pallas_call(
        body, name="inproj_fwd", grid=(s // tm,),
        in_specs=[pl.BlockSpec((tm, D_MODEL), lambda i: (i, 0)),
                  pl.BlockSpec((1, D_MODEL), lambda i: (0, 0)),
                  pl.BlockSpec((D_MODEL, N_PACK), lambda i: (0, 0))],
        out_specs=[pl.BlockSpec((tm, D_MODEL), lambda i: (i, 0)),
                   pl.BlockSpec((len(QKV_SECS), N_HEADS, tm, HEAD_DIM), lambda i: (0, 0, i, 0)),
                   pl.BlockSpec((2, N_HEADS, tm, HEAD_DIM), lambda i: (0, 0, i + 1, 0)),
                   pl.BlockSpec((tm, D_MODEL), lambda i: (i, 0)),
                   pl.BlockSpec((tm, 2 * D_BRANCH), lambda i: (i, 0)),
                   pl.BlockSpec((tm, 128), lambda i: (i, 0))],
        out_shape=[jax.ShapeDtypeStruct((s, D_MODEL), BF16),
                   jax.ShapeDtypeStruct((len(QKV_SECS), N_HEADS, s, HEAD_DIM), BF16),
                   jax.ShapeDtypeStruct((2, N_HEADS, s + tm, HEAD_DIM), BF16),
                   jax.ShapeDtypeStruct((s, D_MODEL), F32),
                   jax.ShapeDtypeStruct((s, 2 * D_BRANCH), F32),
                   jax.ShapeDtypeStruct((s, 128), F32)],
        compiler_params=_params(("arbitrary",)),
    )(x, g, wp)


def inproj_bwd(dqkv, dgates, duv, dfp, wp, x, g, dres):
    s = x.shape[0]
    tm = A_PAD

    def body(*refs):
        dq_refs = refs[:9]
        dgates_ref, duv_ref, dfp_ref, w_ref, x_ref, g_ref, dres_ref, dp_ref, dx_ref, dg_ref = refs[9:]
        i = pl.program_id(0)
        a_q, a_k, a_v, c_q, c_k, c_v, d_q, d_k, d_v = [_heads_to_lanes(r).astype(BF16) for r in dq_refs]
        dgt = dgates_ref[...]
        duv_b = duv_ref[...].astype(BF16)
        dp = jnp.concatenate(
            [a_q, a_k, a_v, dgt[:, 0:256], duv_b, dgt[:, 256:512], c_q, c_k, c_v, dgt[:, 512:768],
             d_q, d_k, d_v, dgt[:, 768:1024], dfp_ref[...].astype(BF16)], axis=1)
        dp_ref[...] = dp
        dh = _dot_nt(dp, w_ref[...])
        xv = x_ref[...]
        r = lax.rsqrt(jnp.mean(xv * xv, axis=-1, keepdims=True) + EPS)
        xn = xv * r
        u = dh * g_ref[...]
        dx_ref[...] = dres_ref[...] + r * (u - xn * jnp.mean(xn * u, axis=-1, keepdims=True))

        @pl.when(i == 0)
        def _():
            dg_ref[...] = jnp.zeros_like(dg_ref)

        dg_ref[...] += jnp.sum(dh * xn, axis=0, keepdims=True)

    head_spec = pl.BlockSpec((N_HEADS, tm, HEAD_DIM), lambda i: (0, i, 0))
    padded_spec = pl.BlockSpec((N_HEADS, tm, HEAD_DIM), lambda i: (0, i + 1, 0))
    return pl.pallas_call(
        body, name="inproj_bwd", grid=(s // tm,),
        in_specs=[head_spec, padded_spec, padded_spec] + [head_spec] * 6 + [
            pl.BlockSpec((tm, D_MODEL), lambda i: (i, 0)),
            pl.BlockSpec((tm, 2 * D_BRANCH), lambda i: (i, 0)),
            pl.BlockSpec((tm, 128), lambda i: (i, 0)),
            pl.BlockSpec((D_MODEL, N_PACK), lambda i: (0, 0)),
            pl.BlockSpec((tm, D_MODEL), lambda i: (i, 0)),
            pl.BlockSpec((1, D_MODEL), lambda i: (0, 0)),
            pl.BlockSpec((tm, D_MODEL), lambda i: (i, 0))],
        out_specs=[pl.BlockSpec((tm, N_PACK), lambda i: (i, 0)),
                   pl.BlockSpec((tm, D_MODEL), lambda i: (i, 0)),
                   pl.BlockSpec((1, D_MODEL), lambda i: (0, 0))],
        out_shape=[jax.ShapeDtypeStruct((s, N_PACK), BF16),
                   jax.ShapeDtypeStruct((s, D_MODEL), F32),
                   jax.ShapeDtypeStruct((1, D_MODEL), F32)],
        compiler_params=_params(("arbitrary",)),
    )(*dqkv, dgates, duv, dfp, wp, x, g, dres)


def weight_grad(a, b, name):
    s, m = a.shape
    n = b.shape[1]
    tm = min(2 * ROW_T, s)
    tmm = 256
    nsteps = s // tm

    def body(a_ref, b_ref, o_ref):
        k = pl.program_id(1)

        @pl.when(k == 0)
        def _():
            o_ref[...] = jnp.zeros_like(o_ref)

        o_ref[...] += _dot_tn(a_ref[...], b_ref[...])

    return pl.pallas_call(
        body, name=name, grid=(m // tmm, nsteps),
        in_specs=[pl.BlockSpec((tm, tmm), lambda j, k: (k, j)),
                  pl.BlockSpec((tm, n), lambda j, k: (k, 0))],
        out_specs=pl.BlockSpec((tmm, n), lambda j, k: (j, 0)),
        out_shape=jax.ShapeDtypeStruct((m, n), F32),
        compiler_params=_params(("arbitrary", "arbitrary")),
    )(a, b)


def _a_specs(s):
    nq = s // A_QB
    per = A_QB // A_PAD
    q_spec = pl.BlockSpec((None, None, A_QB, HEAD_DIM), lambda h, i: (0, h, jnp.minimum(i, nq - 1), 0))
    kv_specs = [pl.BlockSpec((None, None, A_PAD, HEAD_DIM),
                             lambda h, i, n=n, m=m: (n, h, jnp.minimum(per * i + m, per * nq), 0))
                for n in range(2) for m in range(per + 1)]
    t_spec = pl.BlockSpec((None, A_TQ, A_BAND), lambda h, i: (h, 0, 0))
    return nq, q_spec, kv_specs, t_spec


def _a_window(refs, i):
    first = refs[0][...]
    return jnp.concatenate([jnp.where(i > 0, first, jnp.zeros_like(first))] + [r[...] for r in refs[1:]], axis=0)


def _a_scores(q_ref, k, t_ref, i, j):
    rows = slice(j * A_TQ, (j + 1) * A_TQ)
    qs = q_ref[rows, :] * 0.125
    kj = k[j * A_TQ:j * A_TQ + A_BAND, :]
    sc = _dot_nt(qs, kj) + t_ref[...]
    col = lax.broadcasted_iota(jnp.int32, (A_TQ, A_BAND), 1)
    sc = jnp.where(col >= A_PAD - i * A_QB - j * A_TQ, sc, NEG)
    return rows, qs, kj, sc


def mix_a_fwd(qkv, kva, tbias):
    s = qkv.shape[2]
    nq, q_spec, kv_specs, t_spec = _a_specs(s)
    nwin = len(kv_specs) // 2

    def body(*refs):
        q_ref, t_ref, o_ref, lse_ref = refs[0], refs[1 + 2 * nwin], refs[2 + 2 * nwin], refs[3 + 2 * nwin]
        i = pl.program_id(1)
        k = _a_window(refs[1:1 + nwin], i)
        v = _a_window(refs[1 + nwin:1 + 2 * nwin], i)
        for j in range(A_QB // A_TQ):
            rows, _, _, sc = _a_scores(q_ref, k, t_ref, i, j)
            m = jnp.max(sc, axis=-1, keepdims=True)
            p = jnp.exp(sc - m)
            l = jnp.sum(p, axis=-1, keepdims=True)
            o_ref[rows, :] = _dot(p.astype(BF16), v[j * A_TQ:j * A_TQ + A_BAND, :]) / l
            lse_ref[rows, :] = m + jnp.log(l)

    return pl.pallas_call(
        body, name="mix_a_fwd", grid=(N_HEADS, nq),
        in_specs=[q_spec] + kv_specs + [t_spec],
        out_specs=[pl.BlockSpec((None, A_QB, HEAD_DIM), lambda h, i: (h, i, 0)),
                   pl.BlockSpec((None, A_QB, 1), lambda h, i: (h, i, 0))],
        out_shape=[jax.ShapeDtypeStruct((N_HEADS, s, HEAD_DIM), F32),
                   jax.ShapeDtypeStruct((N_HEADS, s, 1), F32)],
        compiler_params=_params(("arbitrary", "arbitrary")),
    )(qkv, *([kva] * (2 * nwin)), tbias)


def mix_a_bwd(qkv, kva, tbias, do, o, lse):
    s = qkv.shape[2]
    nq, q_spec, kv_specs, t_spec = _a_specs(s)
    nwin = len(kv_specs) // 2
    row_spec = lambda w: pl.BlockSpec((None, A_QB, w), lambda h, i: (h, jnp.minimum(i, nq - 1), 0))
    done_spec = pl.BlockSpec((None, A_QB, HEAD_DIM), lambda h, i: (h, i, 0))
    win = A_QB + A_PAD

    def body(*refs):
        q_ref = refs[0]
        t_ref, do_ref, o_ref, lse_ref, dq_ref, dk_ref, dv_ref, dt_ref, dk_win, dv_win = refs[1 + 2 * nwin:]
        i = pl.program_id(1)

        @pl.when(i == 0)
        def _():
            dk_win[...] = jnp.zeros_like(dk_win)
            dv_win[...] = jnp.zeros_like(dv_win)
            dt_ref[...] = jnp.zeros_like(dt_ref)

        @pl.when(i < nq)
        def _():
            k = _a_window(refs[1:1 + nwin], i)
            v = _a_window(refs[1 + nwin:1 + 2 * nwin], i)
            dt = jnp.zeros((A_TQ, A_BAND), F32)
            for j in range(A_QB // A_TQ):
                rows, qs, kj, sc = _a_scores(q_ref, k, t_ref, i, j)
                keys = slice(j * A_TQ, j * A_TQ + A_BAND)
                dob = do_ref[rows, :]
                p = jnp.exp(sc - lse_ref[rows, :])
                delta = jnp.sum(o_ref[rows, :] * dob.astype(F32), axis=-1, keepdims=True)
                ds = p * (_dot_nt(dob, v[keys, :]) - delta)
                dsb = ds.astype(BF16)
                dq_ref[rows, :] = _dot(dsb, kj) * 0.125
                dk_win[keys, :] += _dot_tn(dsb, qs)
                dv_win[keys, :] += _dot_tn(p.astype(BF16), dob)
                dt = dt + ds
            dt_ref[...] += dt

        dk_ref[...] = dk_win[0:A_QB, :]
        dv_ref[...] = dv_win[0:A_QB, :]
        dk_rest = dk_win[A_QB:win, :]
        dv_rest = dv_win[A_QB:win, :]
        dk_win[0:A_PAD, :] = dk_rest
        dv_win[0:A_PAD, :] = dv_rest
        dk_win[A_PAD:win, :] = jnp.zeros((A_QB, HEAD_DIM), F32)
        dv_win[A_PAD:win, :] = jnp.zeros((A_QB, HEAD_DIM), F32)

    return pl.pallas_call(
        body, name="mix_a_bwd", grid=(N_HEADS, nq + 1),
        in_specs=[q_spec] + kv_specs + [t_spec, row_spec(HEAD_DIM), row_spec(HEAD_DIM), row_spec(1)],
        out_specs=[row_spec(HEAD_DIM), done_spec, done_spec, t_spec],
        out_shape=[jax.ShapeDtypeStruct((N_HEADS, s, HEAD_DIM), F32),
                   jax.ShapeDtypeStruct((N_HEADS, s + A_QB, HEAD_DIM), F32),
                   jax.ShapeDtypeStruct((N_HEADS, s + A_QB, HEAD_DIM), F32),
                   jax.ShapeDtypeStruct((N_HEADS, A_TQ, A_BAND), F32)],
        scratch_shapes=[pltpu.VMEM((win, HEAD_DIM), F32), pltpu.VMEM((win, HEAD_DIM), F32)],
        compiler_params=_params(("arbitrary", "arbitrary")),
    )(qkv, *([kva] * (2 * nwin)), tbias, do, o, lse)


def relbias_tile(rel_bias, relmat):
    def body(rb_ref, rel_ref, o_ref):
        rel = rel_ref[...]
        o_ref[...] = jnp.full(o_ref.shape, NEG, F32)

        def step(r, carry):
            hit = rel == r
            for h in range(N_HEADS):
                o_ref[h] = jnp.where(hit, rb_ref[h, r], o_ref[h])
            return carry

        lax.fori_loop(REL_LO, REL_HI, step, 0)

    return pl.pallas_call(
        body, name="relbias_tile",
        in_specs=[pl.BlockSpec(memory_space=pltpu.SMEM), pl.BlockSpec(memory_space=pltpu.VMEM)],
        out_specs=pl.BlockSpec(memory_space=pltpu.VMEM),
        out_shape=jax.ShapeDtypeStruct((N_HEADS, A_TQ, A_BAND), F32),
        compiler_params=_params(),
    )(rel_bias, relmat)


def relbias_grad(dt, relmat):
    def body(dt_ref, rel_ref, o_ref):
        rel = rel_ref[...]
        lane = lax.broadcasted_iota(jnp.int32, (8, 384), 1)
        row = lax.broadcasted_iota(jnp.int32, (8, 384), 0)

        def step(r, acc):
            hit = rel == r
            for h in range(N_HEADS):
                val = jnp.sum(jnp.where(hit, dt_ref[h], 0.0))
                acc = jnp.where((lane == r) & (row == h), val, acc)
            return acc

        o_ref[...] = lax.fori_loop(REL_LO, REL_HI, step, jnp.zeros((8, 384), F32))

    return pl.pallas_call(
        body, name="relbias_grad",
        out_shape=jax.ShapeDtypeStruct((8, 384), F32),
        compiler_params=_params(),
    )(dt, relmat)


def _b_norm(v, gain):
    mu = jnp.mean(v, axis=-1, keepdims=True)
    xc = v - mu
    rstd = lax.rsqrt(jnp.mean(xc * xc, axis=-1, keepdims=True) + EPS)
    xhat = xc * rstd
    return xhat, rstd, xhat * gain


def _tril_mask():
    t = lax.broadcasted_iota(jnp.int32, (SG_CHUNK, SG_CHUNK), 0)
    u = lax.broadcasted_iota(jnp.int32, (SG_CHUNK, SG_CHUNK), 1)
    return u <= t


def mix_b_fwd(uv, gain, w_s, b_col):
    s = uv.shape[0]
    tm = min(ROW_T, s)

    def body(uv_ref, gain_ref, w_ref, b_ref, y_ref):
        tril = _tril_mask()
        ws = [jnp.where(tril, w_ref[g], 0.0).astype(BF16) for g in range(N_HEADS)]
        for c in range(tm // SG_CHUNK):
            rows = slice(c * SG_CHUNK, (c + 1) * SG_CHUNK)
            u = uv_ref[rows, 0:D_BRANCH]
            _, _, vn = _b_norm(uv_ref[rows, D_BRANCH:2 * D_BRANCH], gain_ref[...])
            vnb = vn.astype(BF16)
            outs = []
            for g in range(N_HEADS):
                cols = slice(g * HEAD_DIM, (g + 1) * HEAD_DIM)
                mixed = _dot(ws[g], vnb[:, cols]) + b_ref[g]
                outs.append(u[:, cols] * mixed)
            y_ref[rows, :] = jnp.concatenate(outs, axis=1)

    return pl.pallas_call(
        body, name="mix_b_fwd", grid=(s // tm,),
        in_specs=[pl.BlockSpec((tm, 2 * D_BRANCH), lambda i: (i, 0)),
                  pl.BlockSpec((1, D_BRANCH), lambda i: (0, 0)),
                  pl.BlockSpec((N_HEADS, SG_CHUNK, SG_CHUNK), lambda i: (0, 0, 0)),
                  pl.BlockSpec((N_HEADS, SG_CHUNK, 1), lambda i: (0, 0, 0))],
        out_specs=pl.BlockSpec((tm, D_BRANCH), lambda i: (i, 0)),
        out_shape=jax.ShapeDtypeStruct((s, D_BRANCH), F32),
        compiler_params=_params(("arbitrary",)),
    )(uv, gain, w_s, b_col)


def mix_b_bwd(uv, gain, w_s, b_col, dy):
    s = uv.shape[0]
    tm = min(ROW_T, s)

    def body(uv_ref, gain_ref, w_ref, b_ref, dy_ref, duv_ref, dw_ref, db_ref, dgain_ref):
        i = pl.program_id(0)

        @pl.when(i == 0)
        def _():
            dw_ref[...] = jnp.zeros_like(dw_ref)
            db_ref[...] = jnp.zeros_like(db_ref)
            dgain_ref[...] = jnp.zeros_like(dgain_ref)

        tril = _tril_mask()
        ws = [jnp.where(tril, w_ref[g], 0.0).astype(BF16) for g in range(N_HEADS)]
        gain_v = gain_ref[...]
        for c in range(tm // SG_CHUNK):
            rows = slice(c * SG_CHUNK, (c + 1) * SG_CHUNK)
            u = uv_ref[rows, 0:D_BRANCH]
            xhat, rstd, vn = _b_norm(uv_ref[rows, D_BRANCH:2 * D_BRANCH], gain_v)
            vnb = vn.astype(BF16)
            dyv = dy_ref[rows, :]
            dus, dvns = [], []
            for g in range(N_HEADS):
                cols = slice(g * HEAD_DIM, (g + 1) * HEAD_DIM)
                mixed = _dot(ws[g], vnb[:, cols]) + b_ref[g]
                dus.append(dyv[:, cols] * mixed)
                dmixed = dyv[:, cols] * u[:, cols]
                dmb = dmixed.astype(BF16)
                db_ref[g] += jnp.sum(dmixed, axis=-1, keepdims=True)
                dw_ref[g] += jnp.where(tril, _dot_nt(dmb, vnb[:, cols]), 0.0)
                dvns.append(_dot_tn(ws[g], dmb))
            dvn = jnp.concatenate(dvns, axis=1)
            dgain_ref[...] += jnp.sum(dvn * xhat, axis=0, keepdims=True)
            dxh = dvn * gain_v
            dv = rstd * (dxh - jnp.mean(dxh, axis=-1, keepdims=True)
                         - xhat * jnp.mean(dxh * xhat, axis=-1, keepdims=True))
            duv_ref[rows, :] = jnp.concatenate(dus + [dv], axis=1)

    return pl.pallas_call(
        body, name="mix_b_bwd", grid=(s // tm,),
        in_specs=[pl.BlockSpec((tm, 2 * D_BRANCH), lambda i: (i, 0)),
                  pl.BlockSpec((1, D_BRANCH), lambda i: (0, 0)),
                  pl.BlockSpec((N_HEADS, SG_CHUNK, SG_CHUNK), lambda i: (0, 0, 0)),
                  pl.BlockSpec((N_HEADS, SG_CHUNK, 1), lambda i: (0, 0, 0)),
                  pl.BlockSpec((tm, D_BRANCH), lambda i: (i, 0))],
        out_specs=[pl.BlockSpec((tm, 2 * D_BRANCH), lambda i: (i, 0)),
                   pl.BlockSpec((N_HEADS, SG_CHUNK, SG_CHUNK), lambda i: (0, 0, 0)),
                   pl.BlockSpec((N_HEADS, SG_CHUNK, 1), lambda i: (0, 0, 0)),
                   pl.BlockSpec((1, D_BRANCH), lambda i: (0, 0))],
        out_shape=[jax.ShapeDtypeStruct((s, 2 * D_BRANCH), F32),
                   jax.ShapeDtypeStruct((N_HEADS, SG_CHUNK, SG_CHUNK), F32),
                   jax.ShapeDtypeStruct((N_HEADS, SG_CHUNK, 1), F32),
                   jax.ShapeDtypeStruct((1, D_BRANCH), F32)],
        compiler_params=_params(("arbitrary",)),
    )(uv, gain, w_s, b_col, dy)


def _scan_mats(nrow):
    a = lax.broadcasted_iota(jnp.int32, (128, 128), 0)
    b = lax.broadcasted_iota(jnp.int32, (128, 128), 1)
    r = lax.broadcasted_iota(jnp.int32, (nrow, nrow), 0)
    c = lax.broadcasted_iota(jnp.int32, (nrow, nrow), 1)
    nb = nrow // N_HEADS
    same = (r // nb) == (c // nb)
    return a, b, r, c, same


def _exact_dot(x, m):
    hi, mid, lo = _split3(x)
    return _dot(hi, m) + _dot(mid, m) + _dot(lo, m)


def _exact_dot_left(m, x):
    hi, mid, lo = _split3(x)
    return _dot(m, hi) + _dot(m, mid) + _dot(m, lo)


def fox_gate_fwd(ft, bcol):
    nrow = ft.shape[0]

    def body(f_ref, b_ref, c_ref):
        z = f_ref[...] + b_ref[...]
        ls = jnp.minimum(z, 0.0) - jnp.log(1.0 + jnp.exp(-jnp.abs(z)))
        a, b, r, c, same = _scan_mats(nrow)
        within = _exact_dot(ls, (a <= b).astype(BF16))
        tot = jnp.broadcast_to(within[:, 127:128], within.shape)
        before = _exact_dot_left((same & (c < r)).astype(BF16), tot)
        c_ref[...] = within + before

    return pl.pallas_call(
        body, name="fox_gate_fwd",
        out_shape=jax.ShapeDtypeStruct((nrow, 128), F32),
        compiler_params=_params(),
    )(ft, bcol)


def fox_gate_bwd(ft, bcol, dc):
    nrow = ft.shape[0]

    def body(f_ref, b_ref, dc_ref, df_ref, db_ref):
        a, b, r, c, same = _scan_mats(nrow)
        dcv = dc_ref[...]
        within = _exact_dot(dcv, (a >= b).astype(BF16))
        tot = jnp.broadcast_to(within[:, 0:1], within.shape)
        after = _exact_dot_left((same & (c > r)).astype(BF16), tot)
        dls = within + after
        z = f_ref[...] + b_ref[...]
        dz = dls * _sigmoid(-z)
        df_ref[...] = dz
        rs = jnp.broadcast_to(jnp.sum(dz, axis=-1, keepdims=True), dz.shape)
        hr = lax.broadcasted_iota(jnp.int32, (8, nrow), 0)
        hc = lax.broadcasted_iota(jnp.int32, (8, nrow), 1)
        db_ref[...] = _exact_dot_left((hr == hc // (nrow // N_HEADS)).astype(BF16), rs)

    return pl.pallas_call(
        body, name="fox_gate_bwd",
        out_shape=[jax.ShapeDtypeStruct((nrow, 128), F32), jax.ShapeDtypeStruct((8, 128), F32)],
        compiler_params=_params(),
    )(ft, bcol, dc)


def _att_specs(s, qi, ki, vi):
    q_spec = pl.BlockSpec((None, None, FOX_TQ, HEAD_DIM), lambda h, i: (qi, h, i, 0))
    k_spec = pl.BlockSpec((None, None, s, HEAD_DIM), lambda h, i: (ki, h, 0, 0))
    v_spec = pl.BlockSpec((None, None, s, HEAD_DIM), lambda h, i: (vi, h, 0, 0))
    row_spec = lambda w: pl.BlockSpec((None, FOX_TQ, w), lambda h, i: (h, i, 0))
    gate_spec = pl.BlockSpec((None, s // ATT_T, 1, ATT_T), lambda h, i: (h, 0, 0, 0))
    return q_spec, k_spec, v_spec, row_spec, gate_spec


def _causal(strict, n=ATT_T):
    row = lax.broadcasted_iota(jnp.int32, (n, n), 0)
    col = lax.broadcasted_iota(jnp.int32, (n, n), 1)
    return (col < row) if strict else (col <= row)


def _gate_row(cr_ref, kb, g):
    if g == 1:
        return cr_ref[kb]
    return jnp.concatenate([cr_ref[kb + n] for n in range(g)], axis=1)


def _fox_walk(i, carry, tile, alive):
    g = FOX_WIDE
    own = FOX_TQ // ATT_T
    nwide = (own * i) // g
    carry = tile(own * i, own, carry, True)
    carry = lax.fori_loop(0, (own * i - nwide * g) // own, lambda n, c: tile(nwide * g, own, c, False), carry)

    def cond(state):
        return jnp.logical_and(state[0] >= 0, state[1] > 0)

    def step(state):
        n = state[0]
        c = tile(n * g, g, state[2:], False)
        return (n - 1, alive(n * g, c)) + tuple(c)

    out = lax.while_loop(cond, step, (nwide - 1, alive(nwide * g, carry)) + tuple(carry))
    return out[2:]


def _fox_reach(qs, k_ref, kmax_ref, cc, i):
    s = k_ref.shape[0]
    rows = 4 * ATT_T

    @pl.when(i == 0)
    def _():
        def chunk(n, mx):
            kc = k_ref[pl.ds(pl.multiple_of(n * rows, rows), rows), :].astype(F32)
            return jnp.maximum(mx, jnp.max(jnp.sum(kc * kc, axis=-1, keepdims=True)))

        kmax_ref[0] = jnp.sqrt(lax.fori_loop(0, s // rows, chunk, jnp.float32(0.0)))

    qf = qs.astype(F32)
    return jnp.sqrt(jnp.sum(qf * qf, axis=-1, keepdims=True)) * kmax_ref[0] + cc


def _gate_col(cr_ref, i):
    row = lax.broadcasted_iota(jnp.int32, (ATT_T, ATT_T), 0)
    col = lax.broadcasted_iota(jnp.int32, (ATT_T, ATT_T), 1)
    own = FOX_TQ // ATT_T
    return jnp.concatenate([jnp.sum(jnp.where(row == col, cr_ref[own * i + n], 0.0), axis=-1, keepdims=True)
                            for n in range(own)], axis=0)


def _fox_scores(qs, k, cc, crow, masked):
    sc = (_dot_nt(qs, k) + (cc - crow)) * LOG2E
    if masked:
        sc = jnp.where(_causal(False, FOX_TQ), sc, NEG)
    return sc


def fox_fwd(qkv, c_row):
    s = qkv.shape[2]
    t = ATT_T
    nq = s // FOX_TQ
    q_spec, k_spec, v_spec, row_spec, gate_spec = _att_specs(s, 1, 2, 3)
    rows = 4 * t

    def body(q_ref, k_ref, v_ref, cr_ref, o_ref, ref_ref, rl_ref, v1_ref, kmax_ref):
        i = pl.program_id(1)

        @pl.when(i == 0)
        def _():
            def chunk(n, carry):
                r0 = pl.multiple_of(n * rows, rows)
                v1_ref[pl.ds(r0, rows), :] = jnp.concatenate(
                    [v_ref[pl.ds(r0, rows), :], jnp.ones((rows, HEAD_DIM), BF16)], axis=1)
                return carry

            lax.fori_loop(0, s // rows, chunk, 0)

        qs = q_ref[...] * 0.125
        cc = _gate_col(cr_ref, i)
        reach = _fox_reach(qs, k_ref, kmax_ref, cc, i) * LOG2E

        def alive(kb, carry):
            return (jnp.max(reach - cr_ref[kb][:, 0:1] * LOG2E - carry[0]) > FOX_DEAD2).astype(jnp.int32)

        def tile(kb, g, carry, masked):
            m, acc = carry
            k0 = pl.multiple_of(kb * t, t)
            sc = _fox_scores(qs, k_ref[pl.ds(k0, g * t), :], cc, _gate_row(cr_ref, kb, g), masked)
            m_new = jnp.maximum(m, jnp.ceil(jnp.max(sc, axis=-1, keepdims=True)))
            pb = jnp.exp2(sc - m_new).astype(BF16)
            acc = jnp.exp2(m - m_new) * acc + _dot(pb, v1_ref[pl.ds(k0, g * t), :])
            return m_new, acc

        init = (jnp.full((FOX_TQ, 1), NEG, F32), jnp.zeros((FOX_TQ, 2 * HEAD_DIM), F32))
        m, acc = _fox_walk(i, init, tile, alive)
        rl = 1.0 / acc[:, HEAD_DIM:HEAD_DIM + 1]
        o_ref[...] = acc[:, 0:HEAD_DIM] * rl
        ref_ref[...] = m
        rl_ref[...] = rl

    return pl.pallas_call(
        body, name="fox_fwd", grid=(N_HEADS, nq),
        in_specs=[q_spec, k_spec, v_spec, gate_spec],
        out_specs=[row_spec(HEAD_DIM), row_spec(1), row_spec(1)],
        out_shape=[jax.ShapeDtypeStruct((N_HEADS, s, HEAD_DIM), F32),
                   jax.ShapeDtypeStruct((N_HEADS, s, 1), F32),
                   jax.ShapeDtypeStruct((N_HEADS, s, 1), F32)],
        scratch_shapes=[pltpu.VMEM((s, 2 * HEAD_DIM), BF16), pltpu.SMEM((1,), F32)],
        compiler_params=_params(("arbitrary", "arbitrary")),
    )(qkv, qkv, qkv, c_row)


def fox_bwd(qkv, c_row, do, o, ref, rl):
    s = qkv.shape[2]
    t = ATT_T
    nq = s // FOX_TQ
    q_spec, k_spec, v_spec, row_spec, gate_spec = _att_specs(s, 1, 2, 3)
    any_spec = pl.BlockSpec(memory_space=pl.ANY)

    def body(q_ref, k_ref, v_ref, cr_ref, do_ref, o_ref, ref_ref, rl_ref,
             dq_ref, dk_hbm, dv_hbm, dc_ref, dk_acc, dv_acc, kmax_ref):
        h = pl.program_id(0)
        i = pl.program_id(1)

        @pl.when(i == 0)
        def _():
            dk_acc[...] = jnp.zeros_like(dk_acc)
            dv_acc[...] = jnp.zeros_like(dv_acc)
            dc_ref[...] = jnp.zeros_like(dc_ref)

        qs = q_ref[...] * 0.125
        ref = ref_ref[...]
        rl = rl_ref[...]
        dob = (do_ref[...].astype(F32) * rl).astype(BF16)
        delta = jnp.sum(o_ref[...] * dob.astype(F32), axis=-1, keepdims=True)
        cc = _gate_col(cr_ref, i)
        margin = _fox_reach(qs, k_ref, kmax_ref, cc, i) * LOG2E - ref

        def alive(kb, carry):
            return (jnp.max(margin - cr_ref[kb][:, 0:1] * LOG2E) > FOX_DEAD2).astype(jnp.int32)

        def tile(kb, g, carry, masked):
            dq, = carry
            k0 = pl.multiple_of(kb * t, t)
            k = k_ref[pl.ds(k0, g * t), :]
            sc = _fox_scores(qs, k, cc, _gate_row(cr_ref, kb, g), masked)
            wb = jnp.exp2(sc - ref).astype(BF16)
            ds = wb.astype(F32) * (_dot_nt(dob, v_ref[pl.ds(k0, g * t), :]) - delta)
            dsb = ds.astype(BF16)
            dk_acc[pl.ds(k0, g * t), :] += _dot_tn(dsb, qs)
            dv_acc[pl.ds(k0, g * t), :] += _dot_tn(wb, dob)
            dcs = -jnp.sum(ds, axis=0, keepdims=True)
            for n in range(g):
                dc_ref[kb + n] += dcs[:, n * t:(n + 1) * t]
            return (dq + _dot(dsb, k),)

        dq, = _fox_walk(i, (jnp.zeros((FOX_TQ, HEAD_DIM), F32),), tile, alive)
        dq_ref[...] = dq * 0.125

        @pl.when(i == nq - 1)
        def _():
            pltpu.sync_copy(dk_acc, dk_hbm.at[h])
            pltpu.sync_copy(dv_acc, dv_hbm.at[h])

    return pl.pallas_call(
        body, name="fox_bwd", grid=(N_HEADS, nq),
        in_specs=[q_spec, k_spec, v_spec,
                  gate_spec,
                  row_spec(HEAD_DIM), row_spec(HEAD_DIM), row_spec(1), row_spec(1)],
        out_specs=[row_spec(HEAD_DIM), any_spec, any_spec,
                   gate_spec],
        out_shape=[jax.ShapeDtypeStruct((N_HEADS, s, HEAD_DIM), F32),
                   jax.ShapeDtypeStruct((N_HEADS, s, HEAD_DIM), F32),
                   jax.ShapeDtypeStruct((N_HEADS, s, HEAD_DIM), F32),
                   jax.ShapeDtypeStruct((N_HEADS, s // t, 1, t), F32)],
        scratch_shapes=[pltpu.VMEM((s, HEAD_DIM), F32), pltpu.VMEM((s, HEAD_DIM), F32), pltpu.SMEM((1,), F32)],
        compiler_params=_params(("arbitrary", "arbitrary")),
    )(qkv, qkv, qkv, c_row, do, o, ref, rl)


def _sb_valid(nrows, ahead):
    row = lax.broadcasted_iota(jnp.int32, (nrows, ATT_T), 0)
    col = lax.broadcasted_iota(jnp.int32, (nrows, ATT_T), 1)
    return col + ahead < row


def _sb_band_valid(nsub):
    row = lax.broadcasted_iota(jnp.int32, (nsub * SB_SUB, SB_BAND), 0)
    col = lax.broadcasted_iota(jnp.int32, (nsub * SB_SUB, SB_BAND), 1)
    return col < (row & (SB_SUB - 1)) + SB_BACK


def _sb_logits(qs, k):
    z = _dot_nt(qs, k)
    sp = jnp.log(1.0 + jnp.exp(-jnp.abs(z)))
    return jnp.minimum(z, 0.0) - sp, -jnp.maximum(z, 0.0) - sp


def _sb_weights(ls, lm, run, valid):
    if valid is not None:
        lm = jnp.where(valid, lm, 0.0)
    n = lm.shape[1]
    row = lax.broadcasted_iota(jnp.int32, (n, n), 0)
    col = lax.broadcasted_iota(jnp.int32, (n, n), 1)
    later = (row > col).astype(BF16)
    hi, lo = _split2(lm)
    between = _dot(hi, later) + _dot(lo, later)
    if run is not None:
        between = run + between
    a = jnp.exp(ls + between)
    if valid is not None:
        a = jnp.where(valid, a, 0.0)
    return lm, a


def _sb_band_start(i, j):
    return pl.multiple_of(i * 2 * ATT_T + j * SB_SUB - SB_BACK, SB_SUB)


def _sb_tile(qs, k, run, valid):
    ls, lm = _sb_logits(qs, k)
    lm, a = _sb_weights(ls, lm, run, valid)
    return ls, lm, a


def _sb_band(i, qs_all, k_ref):
    nsub = qs_all.shape[0] // SB_SUB
    valid = _sb_band_valid(nsub)
    starts = [_sb_band_start(i, j) for j in range(nsub)]
    kwins = [k_ref[pl.ds(k0, SB_BAND), :] for k0 in starts]
    parts = [_sb_logits(qs_all[j * SB_SUB:(j + 1) * SB_SUB], kwins[j]) for j in range(nsub)]
    ls = jnp.concatenate([p[0] for p in parts], axis=0)
    lm, a = _sb_weights(ls, jnp.concatenate([p[1] for p in parts], axis=0), None, valid)
    return starts, kwins, ls, lm, a, valid


def _sb_suffix(g, run_g):
    n = g.shape[1]
    row = lax.broadcasted_iota(jnp.int32, (n, n), 0)
    col = lax.broadcasted_iota(jnp.int32, (n, n), 1)
    from_here = (row >= col).astype(BF16)
    hi, lo = _split2(g)
    out = _dot(hi, from_here) + _dot(lo, from_here)
    return out if run_g is None else run_g + out


def _sb_walk(i, carry, tile):
    def alive_of(c):
        return (jnp.max(c[0]) > SB_DEAD).astype(jnp.int32)

    def cond(state):
        n, alive = state[0], state[1]
        return jnp.logical_and(n < i, alive > 0)

    def step(state):
        n = state[0]
        c = tile(i - 1 - n, state[2:], False)
        return (n + 1, alive_of(c)) + tuple(c)

    out = lax.while_loop(cond, step, (jnp.int32(0), alive_of(carry)) + tuple(carry))
    return out[2:]


def _sb_specs(s):
    tq = 2 * ATT_T
    q_spec = pl.BlockSpec((None, None, tq, HEAD_DIM), lambda h, i: (4, h, i, 0))
    k_spec = pl.BlockSpec((None, None, s, HEAD_DIM), lambda h, i: (5, h, 0, 0))
    v_spec = pl.BlockSpec((None, None, s, HEAD_DIM), lambda h, i: (6, h, 0, 0))
    row_spec = pl.BlockSpec((None, tq, HEAD_DIM), lambda h, i: (h, i, 0))
    band_spec = pl.BlockSpec((None, None, 1, 128), lambda h, i: (h, i, 0, 0))
    return tq, q_spec, k_spec, v_spec, row_spec, band_spec


def _sb_block(i, tile, zero):
    t = ATT_T
    lo, hi, both = slice(0, t), slice(t, 2 * t), slice(0, 2 * t)
    c_hi = tile(2 * i + 1, hi, zero, 0)
    c_lo = tile(2 * i, lo, zero, 0)
    c_hi = tile(2 * i, hi, c_hi, None)
    carry = tuple(jnp.concatenate([a, b], axis=0) for a, b in zip(c_lo, c_hi))
    return _sb_walk(2 * i, carry, lambda kb, c, _: tile(kb, both, c, None))


def sb_fwd(qkv):
    s = qkv.shape[2]
    t = ATT_T
    tq, q_spec, k_spec, v_spec, row_spec, band_spec = _sb_specs(s)

    def body(q_ref, k_ref, v_ref, o_ref, band_ref, done_ref):
        i = pl.program_id(1)
        qs = q_ref[...] * 0.125
        done_ref[0] = 0

        @pl.when(i > 0)
        def _():
            starts, _, _, lm, a, _ = _sb_band(i, qs, k_ref)
            ab = a.astype(BF16)
            for j, k0 in enumerate(starts):
                rows = slice(j * SB_SUB, (j + 1) * SB_SUB)
                o_ref[rows, :] = _dot(ab[rows], v_ref[pl.ds(k0, SB_BAND), :])
            worst = jnp.max(jnp.sum(lm, axis=-1, keepdims=True))
            done_ref[0] = (worst <= SB_DEAD).astype(jnp.int32)

        @pl.when(done_ref[0] == 0)
        def _():
            def tile(kb, rows, carry, ahead):
                run, acc = carry
                k0 = pl.multiple_of(kb * t, t)
                valid = None if ahead is None else _sb_valid(t, ahead)
                _, lm, a = _sb_tile(qs[rows], k_ref[pl.ds(k0, t), :], run, valid)
                acc = acc + _dot(a.astype(BF16), v_ref[pl.ds(k0, t), :])
                return run + jnp.sum(lm, axis=-1, keepdims=True), acc

            _, acc = _sb_block(i, tile, (jnp.zeros((t, 1), F32), jnp.zeros((t, HEAD_DIM), F32)))
            o_ref[...] = acc

        band_ref[...] = jnp.full(band_ref.shape, done_ref[0], jnp.int32).astype(F32)

    return pl.pallas_call(
        body, name="sb_fwd", grid=(N_HEADS, s // tq),
        in_specs=[q_spec, k_spec, v_spec],
        out_specs=[row_spec, band_spec],
        out_shape=[jax.ShapeDtypeStruct((N_HEADS, s, HEAD_DIM), F32),
                   jax.ShapeDtypeStruct((N_HEADS, s // tq, 1, 128), F32)],
        scratch_shapes=[pltpu.SMEM((1,), jnp.int32)],
        compiler_params=_params(("arbitrary", "arbitrary")),
    )(qkv, qkv, qkv)


def sb_bwd(qkv, do, o, band):
    s = qkv.shape[2]
    t = ATT_T
    tq, q_spec, k_spec, v_spec, row_spec, band_spec = _sb_specs(s)
    nq = s // tq
    any_spec = pl.BlockSpec(memory_space=pl.ANY)

    def body(q_ref, k_ref, v_ref, do_ref, o_ref, band_ref, dq_ref, dk_hbm, dv_hbm, dk_acc, dv_acc):
        h = pl.program_id(0)
        i = pl.program_id(1)

        @pl.when(i == 0)
        def _():
            dk_acc[...] = jnp.zeros_like(dk_acc)
            dv_acc[...] = jnp.zeros_like(dv_acc)

        qs_all = q_ref[...] * 0.125
        dob_all = do_ref[...]
        tot_all = jnp.sum(o_ref[...] * dob_all.astype(F32), axis=-1, keepdims=True)
        on_band = jnp.max(band_ref[...]) > 0.5

        def grads(qs, dob, tot, k, v, k0, run, run_g, valid):
            ls, lm, a = _sb_tile(qs, k, run, valid)
            ab = a.astype(BF16)
            g = ab.astype(F32) * _dot_nt(dob, v)
            g_left = tot - _sb_suffix(g, run_g)
            dz = g - jnp.exp(ls) * (g + g_left)
            if valid is not None:
                dz = jnp.where(valid, dz, 0.0)
            dzb = dz.astype(BF16)
            n = k.shape[0]
            dk_acc[pl.ds(k0, n), :] += _dot_tn(dzb, qs)
            dv_acc[pl.ds(k0, n), :] += _dot_tn(ab, dob)
            return dzb, lm, g

        @pl.when(on_band)
        def _():
            starts, kwins, ls, _, a, valid = _sb_band(i, qs_all, k_ref)
            ab = a.astype(BF16)
            subs = [slice(j * SB_SUB, (j + 1) * SB_SUB) for j in range(len(starts))]
            vwins = [v_ref[pl.ds(k0, SB_BAND), :] for k0 in starts]
            g = ab.astype(F32) * jnp.concatenate([_dot_nt(dob_all[r], v) for r, v in zip(subs, vwins)], axis=0)
            dz = jnp.where(valid, g - jnp.exp(ls) * (g + (tot_all - _sb_suffix(g, None))), 0.0)
            dzb = dz.astype(BF16)
            for r, k0, k in zip(subs, starts, kwins):
                dq_ref[r, :] = _dot(dzb[r], k) * 0.125
                dk_acc[pl.ds(k0, SB_BAND), :] += _dot_tn(dzb[r], qs_all[r])
                dv_acc[pl.ds(k0, SB_BAND), :] += _dot_tn(ab[r], dob_all[r])

        @pl.when(jnp.logical_not(on_band))
        def _():
            def tile(kb, rows, carry, ahead):
                run, run_g, dq = carry
                k0 = pl.multiple_of(kb * t, t)
                k = k_ref[pl.ds(k0, t), :]
                valid = None if ahead is None else _sb_valid(t, ahead)
                dzb, lm, g = grads(qs_all[rows], dob_all[rows], tot_all[rows], k, v_ref[pl.ds(k0, t), :], k0,
                                   run, run_g, valid)
                return (run + jnp.sum(lm, axis=-1, keepdims=True),
                        run_g + jnp.sum(g, axis=-1, keepdims=True),
                        dq + _dot(dzb, k))

            zero = jnp.zeros((t, 1), F32)
            _, _, dq = _sb_block(i, tile, (zero, zero, jnp.zeros((t, HEAD_DIM), F32)))
            dq_ref[...] = dq * 0.125

        @pl.when(i == nq - 1)
        def _():
            pltpu.sync_copy(dk_acc, dk_hbm.at[h])
            pltpu.sync_copy(dv_acc, dv_hbm.at[h])

    return pl.pallas_call(
        body, name="sb_bwd", grid=(N_HEADS, nq),
        in_specs=[q_spec, k_spec, v_spec, row_spec, row_spec, band_spec],
        out_specs=[row_spec, any_spec, any_spec],
        out_shape=[jax.ShapeDtypeStruct((N_HEADS, s, HEAD_DIM), F32)] * 3,
        scratch_shapes=[pltpu.VMEM((s, HEAD_DIM), F32), pltpu.VMEM((s, HEAD_DIM), F32)],
        compiler_params=_params(("arbitrary", "arbitrary")),
    )(qkv, qkv, qkv, do, o, band)


def _branch_inputs(refs, br):
    ya_ref, yb_ref, yc_ref, yd_ref = refs
    if br == 1:
        return yb_ref[...]
    return _heads_to_lanes((ya_ref, None, yc_ref, yd_ref)[br])


def outproj_fwd(x, ya, yb, yc, yd, gates, bg, wout):
    s = x.shape[0]
    tm = min(ROW_T, s)

    def body(x_ref, ya_ref, yb_ref, yc_ref, yd_ref, gates_ref, bg_ref, w_ref, out_ref):
        pieces = []
        for br in range(4):
            cols = slice(br * D_BRANCH, (br + 1) * D_BRANCH)
            y = _branch_inputs((ya_ref, yb_ref, yc_ref, yd_ref), br)
            r = lax.rsqrt(jnp.mean(y * y, axis=-1, keepdims=True) + EPS)
            gt = gates_ref[:, cols]
            pieces.append((y * r * bg_ref[:, cols]) * (gt * _sigmoid(gt)))
        merged = jnp.concatenate(pieces, axis=1).astype(BF16)
        out_ref[...] = x_ref[...] + _dot(merged, w_ref[...])

    head_spec = pl.BlockSpec((N_HEADS, tm, HEAD_DIM), lambda i: (0, i, 0))
    return pl.pallas_call(
        body, name="outproj_fwd", grid=(s // tm,),
        in_specs=[pl.BlockSpec((tm, D_MODEL), lambda i: (i, 0)),
                  head_spec, pl.BlockSpec((tm, D_BRANCH), lambda i: (i, 0)), head_spec, head_spec,
                  pl.BlockSpec((tm, D_MODEL), lambda i: (i, 0)),
                  pl.BlockSpec((1, D_MODEL), lambda i: (0, 0)),
                  pl.BlockSpec((D_MODEL, D_MODEL), lambda i: (0, 0))],
        out_specs=pl.BlockSpec((tm, D_MODEL), lambda i: (i, 0)),
        out_shape=jax.ShapeDtypeStruct((s, D_MODEL), F32),
        compiler_params=_params(("arbitrary",)),
    )(x, ya, yb, yc, yd, gates, bg, wout)


def outproj_bwd(dout, ya, yb, yc, yd, gates, bg, wout):
    s = dout.shape[0]
    tm = min(ROW_T, s)

    def body(dout_ref, ya_ref, yb_ref, yc_ref, yd_ref, gates_ref, bg_ref, w_ref,
             dya_ref, dyb_ref, dyc_ref, dyd_ref, dgates_ref, dbg_ref, dw_ref):
        i = pl.program_id(0)

        @pl.when(i == 0)
        def _():
            dbg_ref[...] = jnp.zeros_like(dbg_ref)
            dw_ref[...] = jnp.zeros_like(dw_ref)

        doutb = dout_ref[...].astype(BF16)
        dmerged = _dot_nt(doutb, w_ref[...])
        pieces = []
        for br in range(4):
            cols = slice(br * D_BRANCH, (br + 1) * D_BRANCH)
            y = _branch_inputs((ya_ref, yb_ref, yc_ref, yd_ref), br)
            r = lax.rsqrt(jnp.mean(y * y, axis=-1, keepdims=True) + EPS)
            yn = y * r
            bgv = bg_ref[:, cols]
            gt = gates_ref[:, cols]
            sig = _sigmoid(gt)
            act = gt * sig
            n = yn * bgv
            pieces.append(n * act)
            dm = dmerged[:, cols]
            dn = dm * act
            dgates_ref[:, cols] = (dm * n * (sig * (1.0 + gt * (1.0 - sig)))).astype(BF16)
            dbg_ref[:, cols] += jnp.sum(dn * yn, axis=0, keepdims=True)
            u = dn * bgv
            dy = r * (u - yn * jnp.mean(yn * u, axis=-1, keepdims=True))
            if br == 1:
                dyb_ref[...] = dy
            else:
                dref = (dya_ref, None, dyc_ref, dyd_ref)[br]
                for hh in range(N_HEADS):
                    dref[hh] = dy[:, hh * HEAD_DIM:(hh + 1) * HEAD_DIM].astype(BF16)
        merged = jnp.concatenate(pieces, axis=1).astype(BF16)
        dw_ref[...] += _dot_tn(merged, doutb)

    head_spec = pl.BlockSpec((N_HEADS, tm, HEAD_DIM), lambda i: (0, i, 0))
    head_shape = jax.ShapeDtypeStruct((N_HEADS, s, HEAD_DIM), BF16)
    return pl.pallas_call(
        body, name="outproj_bwd", grid=(s // tm,),
        in_specs=[pl.BlockSpec((tm, D_MODEL), lambda i: (i, 0)),
                  head_spec, pl.BlockSpec((tm, D_BRANCH), lambda i: (i, 0)), head_spec, head_spec,
                  pl.BlockSpec((tm, D_MODEL), lambda i: (i, 0)),
                  pl.BlockSpec((1, D_MODEL), lambda i: (0, 0)),
                  pl.BlockSpec((D_MODEL, D_MODEL), lambda i: (0, 0))],
        out_specs=[head_spec, pl.BlockSpec((tm, D_BRANCH), lambda i: (i, 0)), head_spec, head_spec,
                   pl.BlockSpec((tm, D_MODEL), lambda i: (i, 0)),
                   pl.BlockSpec((1, D_MODEL), lambda i: (0, 0)),
                   pl.BlockSpec((D_MODEL, D_MODEL), lambda i: (0, 0))],
        out_shape=[head_shape, jax.ShapeDtypeStruct((s, D_BRANCH), F32), head_shape, head_shape,
                   jax.ShapeDtypeStruct((s, D_MODEL), BF16),
                   jax.ShapeDtypeStruct((1, D_MODEL), F32),
                   jax.ShapeDtypeStruct((D_MODEL, D_MODEL), F32)],
        compiler_params=_params(("arbitrary",)),
    )(dout, ya, yb, yc, yd, gates, bg, wout)


def final_loss(x, tgt, g):
    s = x.shape[0]
    tm = min(ROW_T, s)

    def body(x_ref, t_ref, g_ref, loss_ref, dx_ref, dg_ref):
        i = pl.program_id(0)

        @pl.when(i == 0)
        def _():
            loss_ref[...] = jnp.zeros_like(loss_ref)
            dg_ref[...] = jnp.zeros_like(dg_ref)

        xv = x_ref[...]
        gv = g_ref[...]
        r = lax.rsqrt(jnp.mean(xv * xv, axis=-1, keepdims=True) + EPS)
        xn = xv * r
        err = xn * gv - t_ref[...]
        loss_ref[...] += jnp.sum(err * err) * (0.5 / D_MODEL)
        dy = err * (1.0 / D_MODEL)
        u = dy * gv
        dx_ref[...] = r * (u - xn * jnp.mean(xn * u, axis=-1, keepdims=True))
        dg_ref[...] += jnp.sum(dy * xn, axis=0, keepdims=True)

    return pl.pallas_call(
        body, name="final_loss", grid=(s // tm,),
        in_specs=[pl.BlockSpec((tm, D_MODEL), lambda i: (i, 0)),
                  pl.BlockSpec((tm, D_MODEL), lambda i: (i, 0)),
                  pl.BlockSpec((1, D_MODEL), lambda i: (0, 0))],
        out_specs=[pl.BlockSpec((1, 128), lambda i: (0, 0)),
                   pl.BlockSpec((tm, D_MODEL), lambda i: (i, 0)),
                   pl.BlockSpec((1, D_MODEL), lambda i: (0, 0))],
        out_shape=[jax.ShapeDtypeStruct((1, 128), F32),
                   jax.ShapeDtypeStruct((s, D_MODEL), F32),
                   jax.ShapeDtypeStruct((1, D_MODEL), F32)],
        compiler_params=_params(("arbitrary",)),
    )(x, tgt, g)


def _rel_index():
    i = np.arange(A_TQ)[:, None]
    j = np.arange(A_BAND)[None, :]
    rel = np.clip(i - j + (A_BAND - A_TQ), -MAX_REL, MAX_REL) + MAX_REL
    dchunk = i // CHUNK + LOOKBACK - j // CHUNK
    valid = (dchunk >= 0) & (dchunk <= LOOKBACK)
    return jnp.asarray(np.where(valid, rel, -1).astype(np.int32))


def _layer_consts(p):
    tbias = relbias_tile(p["rel_bias"], _rel_index())
    return dict(
        norm_g=p["norm_g"].reshape(1, D_MODEL),
        v_gain=p["v_gain"].reshape(1, D_BRANCH),
        b_col=p["b_s"].reshape(N_HEADS, SG_CHUNK, 1),
        bg=p["branch_gain"].reshape(1, D_MODEL),
        tbias=tbias,
    )


def _gate_layout(fp, b_f, s):
    nb = s // 128
    ft = fp[:, :N_HEADS].T.reshape(N_HEADS * nb, 128)
    bcol = jnp.repeat(b_f, nb).reshape(N_HEADS * nb, 1)
    return ft, bcol


def layer_fwd(x, p):
    s = x.shape[0]
    c = _layer_consts(p)
    h, qkv, kva, gates, uv, fp = inproj_fwd(x, c["norm_g"], p["wp"])
    ya, lse_a = mix_a_fwd(qkv, kva, c["tbias"])
    yb = mix_b_fwd(uv, c["v_gain"], p["w_s"], c["b_col"])
    ft, bcol = _gate_layout(fp, p["b_f"], s)
    c_row = fox_gate_fwd(ft, bcol).reshape(N_HEADS, s // ATT_T, 1, ATT_T)
    yc, ref_c, rl_c = fox_fwd(qkv, c_row)
    yd, band_d = sb_fwd(qkv)
    out = outproj_fwd(x, ya, yb, yc, yd, gates, c["bg"], p["wout"])
    saved = dict(consts=c, x=x, h=h, qkv=qkv, gates=gates, uv=uv, kva=kva, ft=ft, bcol=bcol,
                 c_row=c_row, ya=ya, lse_a=lse_a, yb=yb, yc=yc, ref_c=ref_c, rl_c=rl_c, yd=yd, band_d=band_d)
    return out, saved


def layer_bwd(dout, p, sv):
    s = dout.shape[0]
    c = sv["consts"]
    dya, dyb, dyc, dyd, dgates, dbg, dwout = outproj_bwd(
        dout, sv["ya"], sv["yb"], sv["yc"], sv["yd"], sv["gates"], c["bg"], p["wout"])
    dqa, dka, dva, dt = mix_a_bwd(sv["qkv"], sv["kva"], c["tbias"], dya, sv["ya"], sv["lse_a"])
    drel = relbias_grad(dt, _rel_index())[:N_HEADS, :2 * MAX_REL + 1]
    duv, dws, dbs, dvgain = mix_b_bwd(sv["uv"], c["v_gain"], p["w_s"], c["b_col"], dyb)
    dqc, dkc, dvc, dc = fox_bwd(sv["qkv"], sv["c_row"], dyc, sv["yc"], sv["ref_c"], sv["rl_c"])
    dft, dbf = fox_gate_bwd(sv["ft"], sv["bcol"], dc.reshape(N_HEADS * (s // 128), 128))
    dfp = jnp.pad(dft.reshape(N_HEADS, s).T, ((0, 0), (0, 128 - N_HEADS)))
    dqd, dkd, dvd = sb_bwd(sv["qkv"], dyd, sv["yd"], sv["band_d"])
    dp, dx, dnorm = inproj_bwd((dqa, dka, dva, dqc, dkc, dvc, dqd, dkd, dvd), dgates, duv, dfp,
                               p["wp"], sv["x"], c["norm_g"], dout)
    dwp = weight_grad(sv["h"], dp, "inproj_wgrad")
    grads = dict(norm_g=dnorm.reshape(D_MODEL), wp=dwp, b_f=dbf[:N_HEADS, 0], rel_bias=drel,
                 w_s=dws, b_s=dbs.reshape(N_HEADS, SG_CHUNK), v_gain=dvgain.reshape(D_BRANCH),
                 branch_gain=dbg.reshape(4, D_BRANCH), wout=dwout)
    return dx, grads


def local_step(x, tgt, layers, final_g):
    saved = []
    cur = x
    for p in layers:
        cur, sv = layer_fwd(cur, p)
        saved.append(sv)
    loss, dcur, dfinal = final_loss(cur, tgt, final_g.reshape(1, D_MODEL))
    grads = [None] * len(layers)
    for l in reversed(range(len(layers))):
        dcur, grads[l] = layer_bwd(dcur, layers[l], saved[l])
    return loss[0, 0], dcur, grads, dfinal.reshape(D_MODEL)


def gather_weights(wb, wf):
    def body(wb_ref, wf_ref, ob_ref, of_ref, send_sems, recv_sems, loc_sems):
        x, y, c = lax.axis_index("x"), lax.axis_index("y"), lax.axis_index("c")
        me = 2 * x + y
        chips = [(1 - x, y), (x, 1 - y), (1 - x, 1 - y)]
        pairs = [(wb_ref, ob_ref), (wf_ref, of_ref)]
        local = [pltpu.make_async_copy(src, dst.at[me], loc_sems.at[n]) for n, (src, dst) in enumerate(pairs)]
        for cp in local:
            cp.start()

        def copy(j, n, slot):
            src, dst = pairs[n]
            return pltpu.make_async_remote_copy(
                src_ref=src, dst_ref=dst.at[slot], send_sem=send_sems.at[2 * j + n], recv_sem=recv_sems.at[2 * j + n],
                device_id=(chips[j][0], chips[j][1], c), device_id_type=MESH)

        sends = [copy(j, n, me) for j in range(3) for n in range(2)]
        for cp in sends:
            cp.start()
        for j in range(3):
            for n in range(2):
                copy(j, n, 2 * chips[j][0] + chips[j][1]).wait_recv()
        for cp in sends:
            cp.wait_send()
        for cp in local:
            cp.wait()

    any_spec = pl.BlockSpec(memory_space=pl.ANY)
    return pl.pallas_call(
        body, name="gather_weights",
        in_specs=[any_spec, any_spec], out_specs=[any_spec, any_spec],
        out_shape=[jax.ShapeDtypeStruct((4,) + wb.shape, wb.dtype), jax.ShapeDtypeStruct((4,) + wf.shape, wf.dtype)],
        scratch_shapes=[pltpu.SemaphoreType.DMA((6,)), pltpu.SemaphoreType.DMA((6,)), pltpu.SemaphoreType.DMA((2,))],
    )(wb, wf)


def exchange_grads(big, small):
    def body(b_ref, s_ref, rb_ref, rs_ref, send_sems, recv_sems, loc_sems):
        x, y, c = lax.axis_index("x"), lax.axis_index("y"), lax.axis_index("c")
        me_chip = 2 * x + y
        me = 4 * x + 2 * y + c
        peers = [(x, y, 1 - c)]
        for px, py in [(1 - x, y), (x, 1 - y), (1 - x, 1 - y)]:
            peers += [(px, py, c), (px, py, 1 - c)]
        local = [pltpu.make_async_copy(b_ref.at[me_chip], rb_ref.at[me], loc_sems.at[0]),
                 pltpu.make_async_copy(s_ref, rs_ref.at[me], loc_sems.at[1])]
        for cp in local:
            cp.start()

        def copies(n, chip, slot):
            kw = dict(device_id=peers[n], device_id_type=MESH)
            return [pltpu.make_async_remote_copy(src_ref=b_ref.at[chip], dst_ref=rb_ref.at[slot],
                                                 send_sem=send_sems.at[2 * n], recv_sem=recv_sems.at[2 * n], **kw),
                    pltpu.make_async_remote_copy(src_ref=s_ref, dst_ref=rs_ref.at[slot],
                                                 send_sem=send_sems.at[2 * n + 1], recv_sem=recv_sems.at[2 * n + 1], **kw)]

        sends = [cp for n, (px, py, _) in enumerate(peers) for cp in copies(n, 2 * px + py, me)]
        for cp in sends:
            cp.start()
        for n, (px, py, pc) in enumerate(peers):
            for cp in copies(n, me_chip, 4 * px + 2 * py + pc):
                cp.wait_recv()
        for cp in sends:
            cp.wait_send()
        for cp in local:
            cp.wait()

    any_spec = pl.BlockSpec(memory_space=pl.ANY)
    return pl.pallas_call(
        body, name="exchange_grads",
        in_specs=[any_spec, any_spec], out_specs=[any_spec, any_spec],
        out_shape=[jax.ShapeDtypeStruct((8,) + big.shape[1:], big.dtype),
                   jax.ShapeDtypeStruct((8,) + small.shape, small.dtype)],
        scratch_shapes=[pltpu.SemaphoreType.DMA((14,)), pltpu.SemaphoreType.DMA((14,)), pltpu.SemaphoreType.DMA((2,))],
    )(big, small)


def adamw_reduce(parts, w, m, v, name):
    rows = w.shape[0]
    tr = PACK_ROW_TILE
    c1 = 1.0 - ADAM_B1 ** ADAM_STEP
    c2 = 1.0 - ADAM_B2 ** ADAM_STEP

    def body(p_ref, w_ref, m_ref, v_ref, g_ref, d_ref, nm_ref, nv_ref):
        g = p_ref[0].astype(F32)
        for n in range(1, 8):
            g = g + p_ref[n].astype(F32)
        g_ref[...] = g
        nm = ADAM_B1 * m_ref[...] + (1.0 - ADAM_B1) * g
        nv = ADAM_B2 * v_ref[...] + (1.0 - ADAM_B2) * (g * g)
        nm_ref[...] = nm
        nv_ref[...] = nv
        d_ref[...] = -ADAM_LR * ((nm / c1) / (jnp.sqrt(nv / c2) + ADAM_EPS) + ADAM_WD * w_ref[...])

    spec = pl.BlockSpec((tr, 128), lambda i: (i, 0))
    shape = jax.ShapeDtypeStruct((rows, 128), F32)
    return pl.pallas_call(
        body, name=name, grid=(rows // tr,),
        in_specs=[pl.BlockSpec((8, tr, 128), lambda i: (0, i, 0)), spec, spec, spec],
        out_specs=[spec] * 4, out_shape=[shape] * 4,
        compiler_params=_params(("arbitrary",)),
    )(parts, w, m, v)


SHARDED = ("w_in", "w_out", "branch_gain")
SMALL = ("norm_g", "b_f", "rel_bias", "w_s", "b_s", "v_gain", "final_g")
WEIGHTS = ("norm_g", "w_in", "b_f", "rel_bias", "w_s", "b_s", "v_gain", "branch_gain", "w_out", "final_g")
PACK_ROW_TILE = 512


def _rows_of(shape):
    return -(-int(np.prod(shape)) // 128)


def _pack(leaves):
    parts = []
    for a in leaves:
        flat = a.reshape(-1).astype(F32)
        parts.append(jnp.pad(flat, (0, _rows_of(a.shape) * 128 - flat.shape[0])))
    flat = jnp.concatenate(parts)
    rows = flat.shape[0] // 128
    total = -(-rows // PACK_ROW_TILE) * PACK_ROW_TILE
    return jnp.pad(flat, (0, (total - rows) * 128)).reshape(total, 128)


def _unpack(slab, shapes):
    out, row = [], 0
    for shp in shapes:
        n = int(np.prod(shp))
        r = _rows_of(shp)
        out.append(slab[row:row + r].reshape(-1)[:n].reshape(shp))
        row += r
    return out


def _pack_w_in(w):
    return jnp.concatenate([w[:, :2816], w[:, 2820:], w[:, 2816:2820],
                            jnp.zeros((w.shape[0], N_PACK - N_IN), w.dtype)], axis=1)


def _unpack_w_in(wp):
    return jnp.concatenate([wp[:, :2816], wp[:, F_COL:F_COL + N_HEADS], wp[:, 2816:F_COL]], axis=1)


def kernel(x, norm_g, w_in, b_f, rel_bias, w_s, b_s, v_gain, branch_gain, w_out, final_g, loss_target, m_norm_g, m_w_in, m_b_f, m_rel_bias, m_w_s, m_b_s, m_v_gain, m_branch_gain, m_w_out, m_final_g, v_norm_g, v_w_in, v_b_f, v_rel_bias, v_w_s, v_b_s, v_v_gain, v_branch_gain, v_w_out, v_final_g):
    depth = norm_g.shape[0]
    weights = dict(norm_g=norm_g, w_in=w_in, b_f=b_f, rel_bias=rel_bias, w_s=w_s, b_s=b_s, v_gain=v_gain,
                   branch_gain=branch_gain, w_out=w_out, final_g=final_g)
    mom1 = dict(norm_g=m_norm_g, w_in=m_w_in, b_f=m_b_f, rel_bias=m_rel_bias, w_s=m_w_s, b_s=m_b_s,
                v_gain=m_v_gain, branch_gain=m_branch_gain, w_out=m_w_out, final_g=m_final_g)
    mom2 = dict(norm_g=v_norm_g, w_in=v_w_in, b_f=v_b_f, rel_bias=v_rel_bias, w_s=v_w_s, b_s=v_b_s,
                v_gain=v_v_gain, branch_gain=v_branch_gain, w_out=v_w_out, final_g=v_final_g)

    n_in_rows = _rows_of(w_in.shape)
    n_out_rows = _rows_of(w_out.shape)
    wb = jnp.concatenate([w_in.astype(BF16).reshape(n_in_rows, 128), w_out.astype(BF16).reshape(n_out_rows, 128)])
    wf = jnp.pad(branch_gain.reshape(-1), (0, 8 * 128 - branch_gain.size)).reshape(8, 128)
    gb, gf = gather_weights(wb, wf)
    w_in_full = gb[:, :n_in_rows].reshape((4,) + w_in.shape)
    w_in_full = jnp.moveaxis(w_in_full, 0, 2).reshape(depth, D_MODEL, N_IN)
    w_out_full = gb[:, n_in_rows:].reshape((4,) + w_out.shape)
    w_out_full = jnp.moveaxis(w_out_full, 0, 1).reshape(depth, D_MODEL, D_MODEL)
    bg_full = gf.reshape(4, -1)[:, :branch_gain.size].reshape((4,) + branch_gain.shape)
    bg_full = jnp.moveaxis(bg_full, 0, 2).reshape(depth, 4, D_BRANCH)

    layers = [dict(norm_g=norm_g[l], wp=_pack_w_in(w_in_full[l]), b_f=b_f[l], rel_bias=rel_bias[l], w_s=w_s[l],
                   b_s=b_s[l], v_gain=v_gain[l], branch_gain=bg_full[l], wout=w_out_full[l]) for l in range(depth)]

    loss_part, grad_x, lgrads, dfinal = local_step(x[0], loss_target[0], layers, final_g)
    loss = lax.psum(loss_part, ("x", "y", "c"))

    stack = lambda k: jnp.stack([g[k] for g in lgrads])
    d_w_in = jnp.stack([_unpack_w_in(g["wp"]) for g in lgrads])
    d_w_out = stack("wout")
    d_bg = stack("branch_gain")
    small = dict(norm_g=stack("norm_g"), b_f=stack("b_f"), rel_bias=stack("rel_bias"), w_s=stack("w_s"),
                 b_s=stack("b_s"), v_gain=stack("v_gain"), final_g=dfinal)
    slabs = []
    for sidx in range(4):
        slabs.append(_pack([d_w_in[:, :, sidx * N_SHARD:(sidx + 1) * N_SHARD],
                            d_w_out[:, sidx * D_BRANCH:(sidx + 1) * D_BRANCH, :],
                            d_bg[:, :, sidx * HEAD_DIM:(sidx + 1) * HEAD_DIM]]).astype(BF16))
    big_parts, small_parts = exchange_grads(jnp.stack(slabs), _pack([small[k] for k in SMALL]))

    outs = {}
    for names, parts, name in ((SHARDED, big_parts, "adamw_big"), (SMALL, small_parts, "adamw_small")):
        pack_local = lambda d: _pack([d[k] for k in names])
        slabs = adamw_reduce(parts, pack_local(weights), pack_local(mom1), pack_local(mom2), name)
        shapes = [weights[k].shape for k in names]
        for tag, slab in zip(("grad", "delta", "new_m", "new_v"), slabs):
            for k, a in zip(names, _unpack(slab, shapes)):
                outs[tag, k] = a
    result = [loss, grad_x[None]]
    for tag in ("grad", "delta", "new_m", "new_v"):
        result += [outs[tag, k] for k in WEIGHTS]
    return tuple(result)
```

```python
import functools

import jax
import jax.numpy as jnp
import numpy as np
from jax import lax
from jax.experimental import pallas as pl
from jax.experimental.pallas import tpu as pltpu

F32 = jnp.float32
BF16 = jnp.bfloat16
MESH = pl.DeviceIdType.MESH

D_MODEL = 1024
D_BRANCH = 256
N_HEADS = 4
HEAD_DIM = 64
CHUNK = 64
LOOKBACK = 8
MAX_REL = 128
SG_CHUNK = 128
EPS = 1e-6
N_IN = 3844
N_PACK = 3968
F_COL = 3840
N_SHARD = 961
NEG = -1e30

A_TQ = 128
A_BAND = A_TQ + LOOKBACK * CHUNK
REL_LO = MAX_REL - (CHUNK - 1)
REL_HI = 2 * MAX_REL + 1
A_PAD = LOOKBACK * CHUNK
A_QB = 1024
ATT_T = 256
FOX_TQ = 512
FOX_WIDE = 4
FOX_DEAD2 = -160.0
LOG2E = 1.4426950408889634
SB_SUB = 128
SB_BACK = 256
SB_BAND = SB_SUB + SB_BACK
SB_DEAD = -110.0
ROW_T = 512
VMEM_LIMIT = 56 * 1024 * 1024

ADAM_LR = 0.001
ADAM_B1 = 0.9
ADAM_B2 = 0.999
ADAM_EPS = 1e-08
ADAM_WD = 0.01
ADAM_STEP = 10

SEC_A_Q, SEC_A_K, SEC_A_V, SEC_A_G = 0, 256, 512, 768
SEC_B_U, SEC_B_V, SEC_B_G = 1024, 1280, 1536
SEC_C_Q, SEC_C_K, SEC_C_V, SEC_C_G = 1792, 2048, 2304, 2560
SEC_D_Q, SEC_D_K, SEC_D_V, SEC_D_G = 2816, 3072, 3328, 3584
QKV_SECS = (SEC_A_Q, SEC_C_Q, SEC_C_K, SEC_C_V, SEC_D_Q, SEC_D_K, SEC_D_V)
GATE_SECS = (SEC_A_G, SEC_B_G, SEC_C_G, SEC_D_G)


def _dot(a, b):
    return jnp.dot(a, b, preferred_element_type=F32)


def _dot_nt(a, b):
    return lax.dot_general(a, b, (((1,), (1,)), ((), ())), preferred_element_type=F32)


def _dot_tn(a, b):
    return lax.dot_general(a, b, (((0,), (0,)), ((), ())), preferred_element_type=F32)


def _split2(x):
    hi = x.astype(BF16)
    lo = (x - hi.astype(F32)).astype(BF16)
    return hi, lo


def _split3(x):
    hi = x.astype(BF16)
    r = x - hi.astype(F32)
    mid = r.astype(BF16)
    lo = (r - mid.astype(F32)).astype(BF16)
    return hi, mid, lo


def _sigmoid(x):
    return 1.0 / (1.0 + jnp.exp(-x))


def _params(sem=None, vmem=VMEM_LIMIT):
    return pltpu.CompilerParams(dimension_semantics=sem, vmem_limit_bytes=vmem)


def _heads_to_lanes(ref):
    return jnp.concatenate([ref[h] for h in range(N_HEADS)], axis=1)


def inproj_fwd(x, g, wp):
    s = x.shape[0]
    tm = A_PAD

    def body(x_ref, g_ref, w_ref, h_ref, qkv_ref, kva_ref, gates_ref, uv_ref, f_ref):
        xv = x_ref[...]
        r = lax.rsqrt(jnp.mean(xv * xv, axis=-1, keepdims=True) + EPS)
        h = (xv * r * g_ref[...]).astype(BF16)
        h_ref[...] = h
        for n, off in enumerate(QKV_SECS):
            p = _dot(h, w_ref[:, off:off + D_BRANCH])
            for hh in range(N_HEADS):
                qkv_ref[n, hh] = p[:, hh * HEAD_DIM:(hh + 1) * HEAD_DIM].astype(BF16)
        for n, off in enumerate((SEC_A_K, SEC_A_V)):
            p = _dot(h, w_ref[:, off:off + D_BRANCH])
            for hh in range(N_HEADS):
                kva_ref[n, hh] = p[:, hh * HEAD_DIM:(hh + 1) * HEAD_DIM].astype(BF16)
        for n, off in enumerate(GATE_SECS):
            gates_ref[:, n * D_BRANCH:(n + 1) * D_BRANCH] = _dot(h, w_ref[:, off:off + D_BRANCH])
        uv_ref[...] = _dot(h, w_ref[:, SEC_B_U:SEC_B_U + 2 * D_BRANCH])
        f_ref[...] = _dot(h, w_ref[:, F_COL:F_COL + 128])

    return pl.pallas_call(
        body, name="inproj_fwd", grid=(s // tm,),
        in_specs=[pl.BlockSpec((tm, D_MODEL), lambda i: (i, 0)),
                  pl.BlockSpec((1, D_MODEL), lambda i: (0, 0)),
                  pl.BlockSpec((D_MODEL, N_PACK), lambda i: (0, 0))],
        out_specs=[pl.BlockSpec((tm, D_MODEL), lambda i: (i, 0)),
                   pl.BlockSpec((len(QKV_SECS), N_HEADS, tm, HEAD_DIM), lambda i: (0, 0, i, 0)),
                   pl.BlockSpec((2, N_HEADS, tm, HEAD_DIM), lambda i: (0, 0, i + 1, 0)),
                   pl.BlockSpec((tm, D_MODEL), lambda i: (i, 0)),
                   pl.BlockSpec((tm, 2 * D_BRANCH), lambda i: (i, 0)),
                   pl.BlockSpec((tm, 128), lambda i: (i, 0))],
        out_shape=[jax.ShapeDtypeStruct((s, D_MODEL), BF16),
                   jax.ShapeDtypeStruct((len(QKV_SECS), N_HEADS, s, HEAD_DIM), BF16),
                   jax.ShapeDtypeStruct((2, N_HEADS, s + tm, HEAD_DIM), BF16),
                   jax.ShapeDtypeStruct((s, D_MODEL), F32),
                   jax.ShapeDtypeStruct((s, 2 * D_BRANCH), F32),
                   jax.ShapeDtypeStruct((s, 128), F32)],
        compiler_params=_params(("arbitrary",)),
    )(x, g, wp)


def inproj_bwd(dqkv, dgates, duv, dfp, wp, x, g, dres):
    s = x.shape[0]
    tm = A_PAD

    def body(*refs):
        dq_refs = refs[:9]
        dgates_ref, duv_ref, dfp_ref, w_ref, x_ref, g_ref, dres_ref, dp_ref, dx_ref, dg_ref = refs[9:]
        i = pl.program_id(0)
        a_q, a_k, a_v, c_q, c_k, c_v, d_q, d_k, d_v = [_heads_to_lanes(r).astype(BF16) for r in dq_refs]
        dgt = dgates_ref[...]
        duv_b = duv_ref[...].astype(BF16)
        dp = jnp.concatenate(
            [a_q, a_k, a_v, dgt[:, 0:256], duv_b, dgt[:, 256:512], c_q, c_k, c_v, dgt[:, 512:768],
             d_q, d_k, d_v, dgt[:, 768:1024], dfp_ref[...].astype(BF16)], axis=1)
        dp_ref[...] = dp
        dh = _dot_nt(dp, w_ref[...])
        xv = x_ref[...]
        r = lax.rsqrt(jnp.mean(xv * xv, axis=-1, keepdims=True) + EPS)
        xn = xv * r
        u = dh * g_ref[...]
        dx_ref[...] = dres_ref[...] + r * (u - xn * jnp.mean(xn * u, axis=-1, keepdims=True))

        @pl.when(i == 0)
        def _():
            dg_ref[...] = jnp.zeros_like(dg_ref)

        dg_ref[...] += jnp.sum(dh * xn, axis=0, keepdims=True)

    head_spec = pl.BlockSpec((N_HEADS, tm, HEAD_DIM), lambda i: (0, i, 0))
    padded_spec = pl.BlockSpec((N_HEADS, tm, HEAD_DIM), lambda i: (0, i + 1, 0))
    return pl.pallas_call(
        body, name="inproj_bwd", grid=(s // tm,),
        in_specs=[head_spec, padded_spec, padded_spec] + [head_spec] * 6 + [
            pl.BlockSpec((tm, D_MODEL), lambda i: (i, 0)),
            pl.BlockSpec((tm, 2 * D_BRANCH), lambda i: (i, 0)),
            pl.BlockSpec((tm, 128), lambda i: (i, 0)),
            pl.BlockSpec((D_MODEL, N_PACK), lambda i: (0, 0)),
            pl.BlockSpec((tm, D_MODEL), lambda i: (i, 0)),
            pl.BlockSpec((1, D_MODEL), lambda i: (0, 0)),
            pl.BlockSpec((tm, D_MODEL), lambda i: (i, 0))],
        out_specs=[pl.BlockSpec((tm, N_PACK), lambda i: (i, 0)),
                   pl.BlockSpec((tm, D_MODEL), lambda i: (i, 0)),
                   pl.BlockSpec((1, D_MODEL), lambda i: (0, 0))],
        out_shape=[jax.ShapeDtypeStruct((s, N_PACK), BF16),
                   jax.ShapeDtypeStruct((s, D_MODEL), F32),
                   jax.ShapeDtypeStruct((1, D_MODEL), F32)],
        compiler_params=_params(("arbitrary",)),
    )(*dqkv, dgates, duv, dfp, wp, x, g, dres)


def inproj_wgrad(h, dp):
    s, m = h.shape
    tm = min(2 * ROW_T, s)
    tmm = 256
    nsteps = s // tm

    def body(a_ref, b_ref, o_ref, acc_ref):
        k = pl.program_id(1)

        @pl.when(k == 0)
        def _():
            acc_ref[...] = jnp.zeros_like(acc_ref)

        acc_ref[...] += _dot_tn(a_ref[...], b_ref[...])

        @pl.when(k == nsteps - 1)
        def _():
            acc = acc_ref[...]
            full = jnp.concatenate([acc[:, :SEC_D_Q], acc[:, F_COL:F_COL + N_HEADS], acc[:, SEC_D_Q:F_COL]], axis=1)
            for n in range(4):
                o_ref[n] = full[:, n * N_SHARD:(n + 1) * N_SHARD].astype(BF16)

    return pl.pallas_call(
        body, name="inproj_wgrad", grid=(m // tmm, nsteps),
        in_specs=[pl.BlockSpec((tm, tmm), lambda j, k: (k, j)),
                  pl.BlockSpec((tm, N_PACK), lambda j, k: (k, 0))],
        out_specs=pl.BlockSpec((4, tmm, N_SHARD), lambda j, k: (0, j, 0)),
        out_shape=jax.ShapeDtypeStruct((4, m, N_SHARD), BF16),
        scratch_shapes=[pltpu.VMEM((tmm, N_PACK), F32)],
        compiler_params=_params(("arbitrary", "arbitrary")),
    )(h, dp)


def _a_specs(s):
    nq = s // A_QB
    per = A_QB // A_PAD
    q_spec = pl.BlockSpec((None, None, A_QB, HEAD_DIM), lambda h, i: (0, h, jnp.minimum(i, nq - 1), 0))
    kv_specs = [pl.BlockSpec((None, None, A_PAD, HEAD_DIM),
                             lambda h, i, n=n, m=m: (n, h, jnp.minimum(per * i + m, per * nq), 0))
                for n in range(2) for m in range(per + 1)]
    t_spec = pl.BlockSpec((None, A_TQ, A_BAND), lambda h, i: (h, 0, 0))
    return nq, q_spec, kv_specs, t_spec


def _a_window(refs, i):
    first = refs[0][...]
    return jnp.concatenate([jnp.where(i > 0, first, jnp.zeros_like(first))] + [r[...] for r in refs[1:]], axis=0)


def _a_scores(q_ref, k, t_ref, i, j):
    rows = slice(j * A_TQ, (j + 1) * A_TQ)
    qs = q_ref[rows, :] * 0.125
    kj = k[j * A_TQ:j * A_TQ + A_BAND, :]
    sc = _dot_nt(qs, kj) + t_ref[...]
    col = lax.broadcasted_iota(jnp.int32, (A_TQ, A_BAND), 1)
    sc = jnp.where(col >= A_PAD - i * A_QB - j * A_TQ, sc, NEG)
    return rows, qs, kj, sc


def mix_a_fwd(qkv, kva, tbias):
    s = qkv.shape[2]
    nq, q_spec, kv_specs, t_spec = _a_specs(s)
    nwin = len(kv_specs) // 2

    def body(*refs):
        q_ref, t_ref, o_ref, lse_ref = refs[0], refs[1 + 2 * nwin], refs[2 + 2 * nwin], refs[3 + 2 * nwin]
        i = pl.program_id(1)
        k = _a_window(refs[1:1 + nwin], i)
        v = _a_window(refs[1 + nwin:1 + 2 * nwin], i)
        for j in range(A_QB // A_TQ):
            rows, _, _, sc = _a_scores(q_ref, k, t_ref, i, j)
            m = jnp.max(sc, axis=-1, keepdims=True)
            p = jnp.exp(sc - m)
            l = jnp.sum(p, axis=-1, keepdims=True)
            o_ref[rows, :] = _dot(p.astype(BF16), v[j * A_TQ:j * A_TQ + A_BAND, :]) / l
            lse_ref[rows, :] = m + jnp.log(l)

    return pl.pallas_call(
        body, name="mix_a_fwd", grid=(N_HEADS, nq),
        in_specs=[q_spec] + kv_specs + [t_spec],
        out_specs=[pl.BlockSpec((None, A_QB, HEAD_DIM), lambda h, i: (h, i, 0)),
                   pl.BlockSpec((None, A_QB, 1), lambda h, i: (h, i, 0))],
        out_shape=[jax.ShapeDtypeStruct((N_HEADS, s, HEAD_DIM), F32),
                   jax.ShapeDtypeStruct((N_HEADS, s, 1), F32)],
        compiler_params=_params(("arbitrary", "arbitrary")),
    )(qkv, *([kva] * (2 * nwin)), tbias)


def mix_a_bwd(qkv, kva, tbias, do, o, lse):
    s = qkv.shape[2]
    nq, q_spec, kv_specs, t_spec = _a_specs(s)
    nwin = len(kv_specs) // 2
    row_spec = lambda w: pl.BlockSpec((None, A_QB, w), lambda h, i: (h, jnp.minimum(i, nq - 1), 0))
    done_spec = pl.BlockSpec((None, A_QB, HEAD_DIM), lambda h, i: (h, i, 0))
    win = A_QB + A_PAD

    def body(*refs):
        q_ref = refs[0]
        t_ref, do_ref, o_ref, lse_ref, dq_ref, dk_ref, dv_ref, dt_ref, dk_win, dv_win = refs[1 + 2 * nwin:]
        i = pl.program_id(1)

        @pl.when(i == 0)
        def _():
            dk_win[...] = jnp.zeros_like(dk_win)
            dv_win[...] = jnp.zeros_like(dv_win)
            dt_ref[...] = jnp.zeros_like(dt_ref)

        @pl.when(i < nq)
        def _():
            k = _a_window(refs[1:1 + nwin], i)
            v = _a_window(refs[1 + nwin:1 + 2 * nwin], i)
            dt = jnp.zeros((A_TQ, A_BAND), F32)
            for j in range(A_QB // A_TQ):
                rows, qs, kj, sc = _a_scores(q_ref, k, t_ref, i, j)
                keys = slice(j * A_TQ, j * A_TQ + A_BAND)
                dob = do_ref[rows, :]
                p = jnp.exp(sc - lse_ref[rows, :])
                delta = jnp.sum(o_ref[rows, :] * dob.astype(F32), axis=-1, keepdims=True)
                ds = p * (_dot_nt(dob, v[keys, :]) - delta)
                dsb = ds.astype(BF16)
                dq_ref[rows, :] = _dot(dsb, kj) * 0.125
                dk_win[keys, :] += _dot_tn(dsb, qs)
                dv_win[keys, :] += _dot_tn(p.astype(BF16), dob)
                dt = dt + ds
            dt_ref[...] += dt

        dk_ref[...] = dk_win[0:A_QB, :]
        dv_ref[...] = dv_win[0:A_QB, :]
        dk_rest = dk_win[A_QB:win, :]
        dv_rest = dv_win[A_QB:win, :]
        dk_win[0:A_PAD, :] = dk_rest
        dv_win[0:A_PAD, :] = dv_rest
        dk_win[A_PAD:win, :] = jnp.zeros((A_QB, HEAD_DIM), F32)
        dv_win[A_PAD:win, :] = jnp.zeros((A_QB, HEAD_DIM), F32)

    return pl.pallas_call(
        body, name="mix_a_bwd", grid=(N_HEADS, nq + 1),
        in_specs=[q_spec] + kv_specs + [t_spec, row_spec(HEAD_DIM), row_spec(HEAD_DIM), row_spec(1)],
        out_specs=[row_spec(HEAD_DIM), done_spec, done_spec, t_spec],
        out_shape=[jax.ShapeDtypeStruct((N_HEADS, s, HEAD_DIM), F32),
                   jax.ShapeDtypeStruct((N_HEADS, s + A_QB, HEAD_DIM), F32),
                   jax.ShapeDtypeStruct((N_HEADS, s + A_QB, HEAD_DIM), F32),
                   jax.ShapeDtypeStruct((N_HEADS, A_TQ, A_BAND), F32)],
        scratch_shapes=[pltpu.VMEM((win, HEAD_DIM), F32), pltpu.VMEM((win, HEAD_DIM), F32)],
        compiler_params=_params(("arbitrary", "arbitrary")),
    )(qkv, *([kva] * (2 * nwin)), tbias, do, o, lse)


def relbias_tile(rel_bias, relmat):
    def body(rb_ref, rel_ref, o_ref):
        rel = rel_ref[...]
        o_ref[...] = jnp.full(o_ref.shape, NEG, F32)

        def step(r, carry):
            hit = rel == r
            for h in range(N_HEADS):
                o_ref[h] = jnp.where(hit, rb_ref[h, r], o_ref[h])
            return carry

        lax.fori_loop(REL_LO, REL_HI, step, 0)

    return pl.pallas_call(
        body, name="relbias_tile",
        in_specs=[pl.BlockSpec(memory_space=pltpu.SMEM), pl.BlockSpec(memory_space=pltpu.VMEM)],
        out_specs=pl.BlockSpec(memory_space=pltpu.VMEM),
        out_shape=jax.ShapeDtypeStruct((N_HEADS, A_TQ, A_BAND), F32),
        compiler_params=_params(),
    )(rel_bias, relmat)


def relbias_grad(dt, relmat):
    def body(dt_ref, rel_ref, o_ref):
        rel = rel_ref[...]
        lane = lax.broadcasted_iota(jnp.int32, (8, 384), 1)
        row = lax.broadcasted_iota(jnp.int32, (8, 384), 0)

        def step(r, acc):
            hit = rel == r
            for h in range(N_HEADS):
                val = jnp.sum(jnp.where(hit, dt_ref[h], 0.0))
                acc = jnp.where((lane == r) & (row == h), val, acc)
            return acc

        o_ref[...] = lax.fori_loop(REL_LO, REL_HI, step, jnp.zeros((8, 384), F32))

    return pl.pallas_call(
        body, name="relbias_grad",
        out_shape=jax.ShapeDtypeStruct((8, 384), F32),
        compiler_params=_params(),
    )(dt, relmat)


def _b_norm(v, gain):
    mu = jnp.mean(v, axis=-1, keepdims=True)
    xc = v - mu
    rstd = lax.rsqrt(jnp.mean(xc * xc, axis=-1, keepdims=True) + EPS)
    xhat = xc * rstd
    return xhat, rstd, xhat * gain


def _tril_mask():
    t = lax.broadcasted_iota(jnp.int32, (SG_CHUNK, SG_CHUNK), 0)
    u = lax.broadcasted_iota(jnp.int32, (SG_CHUNK, SG_CHUNK), 1)
    return u <= t


def mix_b_fwd(uv, gain, w_s, b_col):
    s = uv.shape[0]
    tm = min(ROW_T, s)

    def body(uv_ref, gain_ref, w_ref, b_ref, y_ref):
        tril = _tril_mask()
        ws = [jnp.where(tril, w_ref[g], 0.0).astype(BF16) for g in range(N_HEADS)]
        for c in range(tm // SG_CHUNK):
            rows = slice(c * SG_CHUNK, (c + 1) * SG_CHUNK)
            u = uv_ref[rows, 0:D_BRANCH]
            _, _, vn = _b_norm(uv_ref[rows, D_BRANCH:2 * D_BRANCH], gain_ref[...])
            vnb = vn.astype(BF16)
            outs = []
            for g in range(N_HEADS):
                cols = slice(g * HEAD_DIM, (g + 1) * HEAD_DIM)
                mixed = _dot(ws[g], vnb[:, cols]) + b_ref[g]
                outs.append(u[:, cols] * mixed)
            y_ref[rows, :] = jnp.concatenate(outs, axis=1)

    return pl.pallas_call(
        body, name="mix_b_fwd", grid=(s // tm,),
        in_specs=[pl.BlockSpec((tm, 2 * D_BRANCH), lambda i: (i, 0)),
                  pl.BlockSpec((1, D_BRANCH), lambda i: (0, 0)),
                  pl.BlockSpec((N_HEADS, SG_CHUNK, SG_CHUNK), lambda i: (0, 0, 0)),
                  pl.BlockSpec((N_HEADS, SG_CHUNK, 1), lambda i: (0, 0, 0))],
        out_specs=pl.BlockSpec((tm, D_BRANCH), lambda i: (i, 0)),
        out_shape=jax.ShapeDtypeStruct((s, D_BRANCH), F32),
        compiler_params=_params(("arbitrary",)),
    )(uv, gain, w_s, b_col)


def mix_b_bwd(uv, gain, w_s, b_col, dy):
    s = uv.shape[0]
    tm = min(ROW_T, s)

    def body(uv_ref, gain_ref, w_ref, b_ref, dy_ref, duv_ref, dw_ref, db_ref, dgain_ref):
        i = pl.program_id(0)

        @pl.when(i == 0)
        def _():
            dw_ref[...] = jnp.zeros_like(dw_ref)
            db_ref[...] = jnp.zeros_like(db_ref)
            dgain_ref[...] = jnp.zeros_like(dgain_ref)

        tril = _tril_mask()
        ws = [jnp.where(tril, w_ref[g], 0.0).astype(BF16) for g in range(N_HEADS)]
        gain_v = gain_ref[...]
        for c in range(tm // SG_CHUNK):
            rows = slice(c * SG_CHUNK, (c + 1) * SG_CHUNK)
            u = uv_ref[rows, 0:D_BRANCH]
            xhat, rstd, vn = _b_norm(uv_ref[rows, D_BRANCH:2 * D_BRANCH], gain_v)
            vnb = vn.astype(BF16)
            dyv = dy_ref[rows, :]
            dus, dvns = [], []
            for g in range(N_HEADS):
                cols = slice(g * HEAD_DIM, (g + 1) * HEAD_DIM)
                mixed = _dot(ws[g], vnb[:, cols]) + b_ref[g]
                dus.append(dyv[:, cols] * mixed)
                dmixed = dyv[:, cols] * u[:, cols]
                dmb = dmixed.astype(BF16)
                db_ref[g] += jnp.sum(dmixed, axis=-1, keepdims=True)
                dw_ref[g] += jnp.where(tril, _dot_nt(dmb, vnb[:, cols]), 0.0)
                dvns.append(_dot_tn(ws[g], dmb))
            dvn = jnp.concatenate(dvns, axis=1)
            dgain_ref[...] += jnp.sum(dvn * xhat, axis=0, keepdims=True)
            dxh = dvn * gain_v
            dv = rstd * (dxh - jnp.mean(dxh, axis=-1, keepdims=True)
                         - xhat * jnp.mean(dxh * xhat, axis=-1, keepdims=True))
            duv_ref[rows, :] = jnp.concatenate(dus + [dv], axis=1)

    return pl.pallas_call(
        body, name="mix_b_bwd", grid=(s // tm,),
        in_specs=[pl.BlockSpec((tm, 2 * D_BRANCH), lambda i: (i, 0)),
                  pl.BlockSpec((1, D_BRANCH), lambda i: (0, 0)),
                  pl.BlockSpec((N_HEADS, SG_CHUNK, SG_CHUNK), lambda i: (0, 0, 0)),
                  pl.BlockSpec((N_HEADS, SG_CHUNK, 1), lambda i: (0, 0, 0)),
                  pl.BlockSpec((tm, D_BRANCH), lambda i: (i, 0))],
        out_specs=[pl.BlockSpec((tm, 2 * D_BRANCH), lambda i: (i, 0)),
                   pl.BlockSpec((N_HEADS, SG_CHUNK, SG_CHUNK), lambda i: (0, 0, 0)),
                   pl.BlockSpec((N_HEADS, SG_CHUNK, 1), lambda i: (0, 0, 0)),
                   pl.BlockSpec((1, D_BRANCH), lambda i: (0, 0))],
        out_shape=[jax.ShapeDtypeStruct((s, 2 * D_BRANCH), F32),
                   jax.ShapeDtypeStruct((N_HEADS, SG_CHUNK, SG_CHUNK), F32),
                   jax.ShapeDtypeStruct((N_HEADS, SG_CHUNK, 1), F32),
                   jax.ShapeDtypeStruct((1, D_BRANCH), F32)],
        compiler_params=_params(("arbitrary",)),
    )(uv, gain, w_s, b_col, dy)


def _scan_mats(nrow):
    a = lax.broadcasted_iota(jnp.int32, (128, 128), 0)
    b = lax.broadcasted_iota(jnp.int32, (128, 128), 1)
    r = lax.broadcasted_iota(jnp.int32, (nrow, nrow), 0)
    c = lax.broadcasted_iota(jnp.int32, (nrow, nrow), 1)
    nb = nrow // N_HEADS
    same = (r // nb) == (c // nb)
    return a, b, r, c, same


def _exact_dot(x, m):
    hi, mid, lo = _split3(x)
    return _dot(hi, m) + _dot(mid, m) + _dot(lo, m)


def _exact_dot_left(m, x):
    hi, mid, lo = _split3(x)
    return _dot(m, hi) + _dot(m, mid) + _dot(m, lo)


def fox_gate_fwd(ft, bcol):
    nrow = ft.shape[0]

    def body(f_ref, b_ref, c_ref):
        z = f_ref[...] + b_ref[...]
        ls = jnp.minimum(z, 0.0) - jnp.log(1.0 + jnp.exp(-jnp.abs(z)))
        a, b, r, c, same = _scan_mats(nrow)
        within = _exact_dot(ls, (a <= b).astype(BF16))
        tot = jnp.broadcast_to(within[:, 127:128], within.shape)
        before = _exact_dot_left((same & (c < r)).astype(BF16), tot)
        c_ref[...] = within + before

    return pl.pallas_call(
        body, name="fox_gate_fwd",
        out_shape=jax.ShapeDtypeStruct((nrow, 128), F32),
        compiler_params=_params(),
    )(ft, bcol)


def fox_gate_bwd(ft, bcol, dc):
    nrow = ft.shape[0]

    def body(f_ref, b_ref, dc_ref, df_ref, db_ref):
        a, b, r, c, same = _scan_mats(nrow)
        dcv = dc_ref[...]
        within = _exact_dot(dcv, (a >= b).astype(BF16))
        tot = jnp.broadcast_to(within[:, 0:1], within.shape)
        after = _exact_dot_left((same & (c > r)).astype(BF16), tot)
        dls = within + after
        z = f_ref[...] + b_ref[...]
        dz = dls * _sigmoid(-z)
        df_ref[...] = dz
        rs = jnp.broadcast_to(jnp.sum(dz, axis=-1, keepdims=True), dz.shape)
        hr = lax.broadcasted_iota(jnp.int32, (8, nrow), 0)
        hc = lax.broadcasted_iota(jnp.int32, (8, nrow), 1)
        db_ref[...] = _exact_dot_left((hr == hc // (nrow // N_HEADS)).astype(BF16), rs)

    return pl.pallas_call(
        body, name="fox_gate_bwd",
        out_shape=[jax.ShapeDtypeStruct((nrow, 128), F32), jax.ShapeDtypeStruct((8, 128), F32)],
        compiler_params=_params(),
    )(ft, bcol, dc)


def _att_specs(s, qi, ki, vi):
    q_spec = pl.BlockSpec((None, None, FOX_TQ, HEAD_DIM), lambda h, i: (qi, h, i, 0))
    k_spec = pl.BlockSpec((None, None, s, HEAD_DIM), lambda h, i: (ki, h, 0, 0))
    v_spec = pl.BlockSpec((None, None, s, HEAD_DIM), lambda h, i: (vi, h, 0, 0))
    row_spec = lambda w: pl.BlockSpec((None, FOX_TQ, w), lambda h, i: (h, i, 0))
    gate_spec = pl.BlockSpec((None, s // ATT_T, 1, ATT_T), lambda h, i: (h, 0, 0, 0))
    return q_spec, k_spec, v_spec, row_spec, gate_spec


def _causal(strict, n=ATT_T):
    row = lax.broadcasted_iota(jnp.int32, (n, n), 0)
    col = lax.broadcasted_iota(jnp.int32, (n, n), 1)
    return (col < row) if strict else (col <= row)


def _gate_row(cr_ref, kb, g):
    if g == 1:
        return cr_ref[kb]
    return jnp.concatenate([cr_ref[kb + n] for n in range(g)], axis=1)


def _fox_walk(i, carry, tile, alive):
    g = FOX_WIDE
    own = FOX_TQ // ATT_T
    nwide = (own * i) // g
    carry = tile(own * i, own, carry, True)
    carry = lax.fori_loop(0, (own * i - nwide * g) // own, lambda n, c: tile(nwide * g, own, c, False), carry)

    def cond(state):
        return jnp.logical_and(state[0] >= 0, state[1] > 0)

    def step(state):
        n = state[0]
        c = tile(n * g, g, state[2:], False)
        return (n - 1, alive(n * g, c)) + tuple(c)

    out = lax.while_loop(cond, step, (nwide - 1, alive(nwide * g, carry)) + tuple(carry))
    return out[2:]


def _fox_reach(qs, k_ref, kmax_ref, cc, i):
    s = k_ref.shape[0]
    rows = 4 * ATT_T

    @pl.when(i == 0)
    def _():
        def chunk(n, mx):
            kc = k_ref[pl.ds(pl.multiple_of(n * rows, rows), rows), :].astype(F32)
            return jnp.maximum(mx, jnp.max(jnp.sum(kc * kc, axis=-1, keepdims=True)))

        kmax_ref[0] = jnp.sqrt(lax.fori_loop(0, s // rows, chunk, jnp.float32(0.0)))

    qf = qs.astype(F32)
    return jnp.sqrt(jnp.sum(qf * qf, axis=-1, keepdims=True)) * kmax_ref[0] + cc


def _gate_col(cr_ref, i):
    row = lax.broadcasted_iota(jnp.int32, (ATT_T, ATT_T), 0)
    col = lax.broadcasted_iota(jnp.int32, (ATT_T, ATT_T), 1)
    own = FOX_TQ // ATT_T
    return jnp.concatenate([jnp.sum(jnp.where(row == col, cr_ref[own * i + n], 0.0), axis=-1, keepdims=True)
                            for n in range(own)], axis=0)


def _fox_scores(qs, k, cc, crow, masked):
    sc = (_dot_nt(qs, k) + (cc - crow)) * LOG2E
    if masked:
        sc = jnp.where(_causal(False, FOX_TQ), sc, NEG)
    return sc


def fox_fwd(qkv, c_row):
    s = qkv.shape[2]
    t = ATT_T
    nq = s // FOX_TQ
    q_spec, k_spec, v_spec, row_spec, gate_spec = _att_specs(s, 1, 2, 3)
    rows = 4 * t

    def body(q_ref, k_ref, v_ref, cr_ref, o_ref, ref_ref, rl_ref, v1_ref, kmax_ref):
        i = pl.program_id(1)

        @pl.when(i == 0)
        def _():
            def chunk(n, carry):
                r0 = pl.multiple_of(n * rows, rows)
                v1_ref[pl.ds(r0, rows), :] = jnp.concatenate(
                    [v_ref[pl.ds(r0, rows), :], jnp.ones((rows, HEAD_DIM), BF16)], axis=1)
                return carry

            lax.fori_loop(0, s // rows, chunk, 0)

        qs = q_ref[...] * 0.125
        cc = _gate_col(cr_ref, i)
        reach = _fox_reach(qs, k_ref, kmax_ref, cc, i) * LOG2E

        def alive(kb, carry):
            return (jnp.max(reach - cr_ref[kb][:, 0:1] * LOG2E - carry[0]) > FOX_DEAD2).astype(jnp.int32)

        def tile(kb, g, carry, masked):
            m, acc = carry
            k0 = pl.multiple_of(kb * t, t)
            sc = _fox_scores(qs, k_ref[pl.ds(k0, g * t), :], cc, _gate_row(cr_ref, kb, g), masked)
            m_new = jnp.maximum(m, jnp.ceil(jnp.max(sc, axis=-1, keepdims=True)))
            pb = jnp.exp2(sc - m_new).astype(BF16)
            acc = jnp.exp2(m - m_new) * acc + _dot(pb, v1_ref[pl.ds(k0, g * t), :])
            return m_new, acc

        init = (jnp.full((FOX_TQ, 1), NEG, F32), jnp.zeros((FOX_TQ, 2 * HEAD_DIM), F32))
        m, acc = _fox_walk(i, init, tile, alive)
        rl = 1.0 / acc[:, HEAD_DIM:HEAD_DIM + 1]
        o_ref[...] = acc[:, 0:HEAD_DIM] * rl
        ref_ref[...] = m
        rl_ref[...] = rl

    return pl.pallas_call(
        body, name="fox_fwd", grid=(N_HEADS, nq),
        in_specs=[q_spec, k_spec, v_spec, gate_spec],
        out_specs=[row_spec(HEAD_DIM), row_spec(1), row_spec(1)],
        out_shape=[jax.ShapeDtypeStruct((N_HEADS, s, HEAD_DIM), F32),
                   jax.ShapeDtypeStruct((N_HEADS, s, 1), F32),
                   jax.ShapeDtypeStruct((N_HEADS, s, 1), F32)],
        scratch_shapes=[pltpu.VMEM((s, 2 * HEAD_DIM), BF16), pltpu.SMEM((1,), F32)],
        compiler_params=_params(("arbitrary", "arbitrary")),
    )(qkv, qkv, qkv, c_row)


def fox_bwd(qkv, c_row, do, o, ref, rl):
    s = qkv.shape[2]
    t = ATT_T
    nq = s // FOX_TQ
    q_spec, k_spec, v_spec, row_spec, gate_spec = _att_specs(s, 1, 2, 3)
    any_spec = pl.BlockSpec(memory_space=pl.ANY)

    def body(q_ref, k_ref, v_ref, cr_ref, do_ref, o_ref, ref_ref, rl_ref,
             dq_ref, dk_hbm, dv_hbm, dc_ref, dk_acc, dv_acc, kmax_ref):
        h = pl.program_id(0)
        i = pl.program_id(1)

        @pl.when(i == 0)
        def _():
            dk_acc[...] = jnp.zeros_like(dk_acc)
            dv_acc[...] = jnp.zeros_like(dv_acc)
            dc_ref[...] = jnp.zeros_like(dc_ref)

        qs = q_ref[...] * 0.125
        ref = ref_ref[...]
        rl = rl_ref[...]
        dob = (do_ref[...].astype(F32) * rl).astype(BF16)
        delta = jnp.sum(o_ref[...] * dob.astype(F32), axis=-1, keepdims=True)
        cc = _gate_col(cr_ref, i)
        margin = _fox_reach(qs, k_ref, kmax_ref, cc, i) * LOG2E - ref

        def alive(kb, carry):
            return (jnp.max(margin - cr_ref[kb][:, 0:1] * LOG2E) > FOX_DEAD2).astype(jnp.int32)

        def tile(kb, g, carry, masked):
            dq, = carry
            k0 = pl.multiple_of(kb * t, t)
            k = k_ref[pl.ds(k0, g * t), :]
            sc = _fox_scores(qs, k, cc, _gate_row(cr_ref, kb, g), masked)
            wb = jnp.exp2(sc - ref).astype(BF16)
            ds = wb.astype(F32) * (_dot_nt(dob, v_ref[pl.ds(k0, g * t), :]) - delta)
            dsb = ds.astype(BF16)
            dk_acc[pl.ds(k0, g * t), :] += _dot_tn(dsb, qs)
            dv_acc[pl.ds(k0, g * t), :] += _dot_tn(wb, dob)
            dcs = -jnp.sum(ds, axis=0, keepdims=True)
            for n in range(g):
                dc_ref[kb + n] += dcs[:, n * t:(n + 1) * t]
            return (dq + _dot(dsb, k),)

        dq, = _fox_walk(i, (jnp.zeros((FOX_TQ, HEAD_DIM), F32),), tile, alive)
        dq_ref[...] = dq * 0.125

        @pl.when(i == nq - 1)
        def _():
            pltpu.sync_copy(dk_acc, dk_hbm.at[h])
            pltpu.sync_copy(dv_acc, dv_hbm.at[h])

    return pl.pallas_call(
        body, name="fox_bwd", grid=(N_HEADS, nq),
        in_specs=[q_spec, k_spec, v_spec,
                  gate_spec,
                  row_spec(HEAD_DIM), row_spec(HEAD_DIM), row_spec(1), row_spec(1)],
        out_specs=[row_spec(HEAD_DIM), any_spec, any_spec,
                   gate_spec],
        out_shape=[jax.ShapeDtypeStruct((N_HEADS, s, HEAD_DIM), F32),
                   jax.ShapeDtypeStruct((N_HEADS, s, HEAD_DIM), F32),
                   jax.ShapeDtypeStruct((N_HEADS, s, HEAD_DIM), F32),
                   jax.ShapeDtypeStruct((N_HEADS, s // t, 1, t), F32)],
        scratch_shapes=[pltpu.VMEM((s, HEAD_DIM), F32), pltpu.VMEM((s, HEAD_DIM), F32), pltpu.SMEM((1,), F32)],
        compiler_params=_params(("arbitrary", "arbitrary")),
    )(qkv, qkv, qkv, c_row, do, o, ref, rl)


def _sb_valid(nrows, ahead):
    row = lax.broadcasted_iota(jnp.int32, (nrows, ATT_T), 0)
    col = lax.broadcasted_iota(jnp.int32, (nrows, ATT_T), 1)
    return col + ahead < row


def _sb_band_valid(nsub):
    row = lax.broadcasted_iota(jnp.int32, (nsub * SB_SUB, SB_BAND), 0)
    col = lax.broadcasted_iota(jnp.int32, (nsub * SB_SUB, SB_BAND), 1)
    return col < (row & (SB_SUB - 1)) + SB_BACK


def _sb_logits(qs, k):
    z = _dot_nt(qs, k)
    sp = jnp.log(1.0 + jnp.exp(-jnp.abs(z)))
    return jnp.minimum(z, 0.0) - sp, -jnp.maximum(z, 0.0) - sp


def _sb_weights(ls, lm, run, valid):
    if valid is not None:
        lm = jnp.where(valid, lm, 0.0)
    n = lm.shape[1]
    row = lax.broadcasted_iota(jnp.int32, (n, n), 0)
    col = lax.broadcasted_iota(jnp.int32, (n, n), 1)
    later = (row > col).astype(BF16)
    hi, lo = _split2(lm)
    between = _dot(hi, later) + _dot(lo, later)
    if run is not None:
        between = run + between
    a = jnp.exp(ls + between)
    if valid is not None:
        a = jnp.where(valid, a, 0.0)
    return lm, a


def _sb_band_start(i, j):
    return pl.multiple_of(i * 2 * ATT_T + j * SB_SUB - SB_BACK, SB_SUB)


def _sb_tile(qs, k, run, valid):
    ls, lm = _sb_logits(qs, k)
    lm, a = _sb_weights(ls, lm, run, valid)
    return ls, lm, a


def _sb_band(i, qs_all, k_ref):
    nsub = qs_all.shape[0] // SB_SUB
    valid = _sb_band_valid(nsub)
    starts = [_sb_band_start(i, j) for j in range(nsub)]
    kwins = [k_ref[pl.ds(k0, SB_BAND), :] for k0 in starts]
    parts = [_sb_logits(qs_all[j * SB_SUB:(j + 1) * SB_SUB], kwins[j]) for j in range(nsub)]
    ls = jnp.concatenate([p[0] for p in parts], axis=0)
    lm, a = _sb_weights(ls, jnp.concatenate([p[1] for p in parts], axis=0), None, valid)
    return starts, kwins, ls, lm, a, valid


def _sb_suffix(g, run_g):
    n = g.shape[1]
    row = lax.broadcasted_iota(jnp.int32, (n, n), 0)
    col = lax.broadcasted_iota(jnp.int32, (n, n), 1)
    from_here = (row >= col).astype(BF16)
    hi, lo = _split2(g)
    out = _dot(hi, from_here) + _dot(lo, from_here)
    return out if run_g is None else run_g + out


def _sb_walk(i, carry, tile):
    def alive_of(c):
        return (jnp.max(c[0]) > SB_DEAD).astype(jnp.int32)

    def cond(state):
        n, alive = state[0], state[1]
        return jnp.logical_and(n < i, alive > 0)

    def step(state):
        n = state[0]
        c = tile(i - 1 - n, state[2:], False)
        return (n + 1, alive_of(c)) + tuple(c)

    out = lax.while_loop(cond, step, (jnp.int32(0), alive_of(carry)) + tuple(carry))
    return out[2:]


def _sb_specs(s):
    tq = 2 * ATT_T
    q_spec = pl.BlockSpec((None, None, tq, HEAD_DIM), lambda h, i: (4, h, i, 0))
    k_spec = pl.BlockSpec((None, None, s, HEAD_DIM), lambda h, i: (5, h, 0, 0))
    v_spec = pl.BlockSpec((None, None, s, HEAD_DIM), lambda h, i: (6, h, 0, 0))
    row_spec = pl.BlockSpec((None, tq, HEAD_DIM), lambda h, i: (h, i, 0))
    band_spec = pl.BlockSpec((None, None, 1, 128), lambda h, i: (h, i, 0, 0))
    return tq, q_spec, k_spec, v_spec, row_spec, band_spec


def _sb_block(i, tile, zero):
    t = ATT_T
    lo, hi, both = slice(0, t), slice(t, 2 * t), slice(0, 2 * t)
    c_hi = tile(2 * i + 1, hi, zero, 0)
    c_lo = tile(2 * i, lo, zero, 0)
    c_hi = tile(2 * i, hi, c_hi, None)
    carry = tuple(jnp.concatenate([a, b], axis=0) for a, b in zip(c_lo, c_hi))
    return _sb_walk(2 * i, carry, lambda kb, c, _: tile(kb, both, c, None))


def sb_fwd(qkv):
    s = qkv.shape[2]
    t = ATT_T
    tq, q_spec, k_spec, v_spec, row_spec, band_spec = _sb_specs(s)

    def body(q_ref, k_ref, v_ref, o_ref, band_ref, done_ref):
        i = pl.program_id(1)
        qs = q_ref[...] * 0.125
        done_ref[0] = 0

        @pl.when(i > 0)
        def _():
            starts, _, _, lm, a, _ = _sb_band(i, qs, k_ref)
            ab = a.astype(BF16)
            for j, k0 in enumerate(starts):
                rows = slice(j * SB_SUB, (j + 1) * SB_SUB)
                o_ref[rows, :] = _dot(ab[rows], v_ref[pl.ds(k0, SB_BAND), :])
            worst = jnp.max(jnp.sum(lm, axis=-1, keepdims=True))
            done_ref[0] = (worst <= SB_DEAD).astype(jnp.int32)

        @pl.when(done_ref[0] == 0)
        def _():
            def tile(kb, rows, carry, ahead):
                run, acc = carry
                k0 = pl.multiple_of(kb * t, t)
                valid = None if ahead is None else _sb_valid(t, ahead)
                _, lm, a = _sb_tile(qs[rows], k_ref[pl.ds(k0, t), :], run, valid)
                acc = acc + _dot(a.astype(BF16), v_ref[pl.ds(k0, t), :])
                return run + jnp.sum(lm, axis=-1, keepdims=True), acc

            _, acc = _sb_block(i, tile, (jnp.zeros((t, 1), F32), jnp.zeros((t, HEAD_DIM), F32)))
            o_ref[...] = acc

        band_ref[...] = jnp.full(band_ref.shape, done_ref[0], jnp.int32).astype(F32)

    return pl.pallas_call(
        body, name="sb_fwd", grid=(N_HEADS, s // tq),
        in_specs=[q_spec, k_spec, v_spec],
        out_specs=[row_spec, band_spec],
        out_shape=[jax.ShapeDtypeStruct((N_HEADS, s, HEAD_DIM), F32),
                   jax.ShapeDtypeStruct((N_HEADS, s // tq, 1, 128), F32)],
        scratch_shapes=[pltpu.SMEM((1,), jnp.int32)],
        compiler_params=_params(("arbitrary", "arbitrary")),
    )(qkv, qkv, qkv)


def sb_bwd(qkv, do, o, band):
    s = qkv.shape[2]
    t = ATT_T
    tq, q_spec, k_spec, v_spec, row_spec, band_spec = _sb_specs(s)
    nq = s // tq
    any_spec = pl.BlockSpec(memory_space=pl.ANY)

    def body(q_ref, k_ref, v_ref, do_ref, o_ref, band_ref, dq_ref, dk_hbm, dv_hbm, dk_acc, dv_acc):
        h = pl.program_id(0)
        i = pl.program_id(1)

        @pl.when(i == 0)
        def _():
            dk_acc[...] = jnp.zeros_like(dk_acc)
            dv_acc[...] = jnp.zeros_like(dv_acc)

        qs_all = q_ref[...] * 0.125
        dob_all = do_ref[...]
        tot_all = jnp.sum(o_ref[...] * dob_all.astype(F32), axis=-1, keepdims=True)
        on_band = jnp.max(band_ref[...]) > 0.5

        def grads(qs, dob, tot, k, v, k0, run, run_g, valid):
            ls, lm, a = _sb_tile(qs, k, run, valid)
            ab = a.astype(BF16)
            g = ab.astype(F32) * _dot_nt(dob, v)
            g_left = tot - _sb_suffix(g, run_g)
            dz = g - jnp.exp(ls) * (g + g_left)
            if valid is not None:
                dz = jnp.where(valid, dz, 0.0)
            dzb = dz.astype(BF16)
            n = k.shape[0]
            dk_acc[pl.ds(k0, n), :] += _dot_tn(dzb, qs)
            dv_acc[pl.ds(k0, n), :] += _dot_tn(ab, dob)
            return dzb, lm, g

        @pl.when(on_band)
        def _():
            starts, kwins, ls, _, a, valid = _sb_band(i, qs_all, k_ref)
            ab = a.astype(BF16)
            subs = [slice(j * SB_SUB, (j + 1) * SB_SUB) for j in range(len(starts))]
            vwins = [v_ref[pl.ds(k0, SB_BAND), :] for k0 in starts]
            g = ab.astype(F32) * jnp.concatenate([_dot_nt(dob_all[r], v) for r, v in zip(subs, vwins)], axis=0)
            dz = jnp.where(valid, g - jnp.exp(ls) * (g + (tot_all - _sb_suffix(g, None))), 0.0)
            dzb = dz.astype(BF16)
            for r, k0, k in zip(subs, starts, kwins):
                dq_ref[r, :] = _dot(dzb[r], k) * 0.125
                dk_acc[pl.ds(k0, SB_BAND), :] += _dot_tn(dzb[r], qs_all[r])
                dv_acc[pl.ds(k0, SB_BAND), :] += _dot_tn(ab[r], dob_all[r])

        @pl.when(jnp.logical_not(on_band))
        def _():
            def tile(kb, rows, carry, ahead):
                run, run_g, dq = carry
                k0 = pl.multiple_of(kb * t, t)
                k = k_ref[pl.ds(k0, t), :]
                valid = None if ahead is None else _sb_valid(t, ahead)
                dzb, lm, g = grads(qs_all[rows], dob_all[rows], tot_all[rows], k, v_ref[pl.ds(k0, t), :], k0,
                                   run, run_g, valid)
                return (run + jnp.sum(lm, axis=-1, keepdims=True),
                        run_g + jnp.sum(g, axis=-1, keepdims=True),
                        dq + _dot(dzb, k))

            zero = jnp.zeros((t, 1), F32)
            _, _, dq = _sb_block(i, tile, (zero, zero, jnp.zeros((t, HEAD_DIM), F32)))
            dq_ref[...] = dq * 0.125

        @pl.when(i == nq - 1)
        def _():
            pltpu.sync_copy(dk_acc, dk_hbm.at[h])
            pltpu.sync_copy(dv_acc, dv_hbm.at[h])

    return pl.pallas_call(
        body, name="sb_bwd", grid=(N_HEADS, nq),
        in_specs=[q_spec, k_spec, v_spec, row_spec, row_spec, band_spec],
        out_specs=[row_spec, any_spec, any_spec],
        out_shape=[jax.ShapeDtypeStruct((N_HEADS, s, HEAD_DIM), F32)] * 3,
        scratch_shapes=[pltpu.VMEM((s, HEAD_DIM), F32), pltpu.VMEM((s, HEAD_DIM), F32)],
        compiler_params=_params(("arbitrary", "arbitrary")),
    )(qkv, qkv, qkv, do, o, band)


def _branch_inputs(refs, br):
    ya_ref, yb_ref, yc_ref, yd_ref = refs
    if br == 1:
        return yb_ref[...]
    return _heads_to_lanes((ya_ref, None, yc_ref, yd_ref)[br])


def outproj_fwd(x, ya, yb, yc, yd, gates, bg, wout):
    s = x.shape[0]
    tm = min(ROW_T, s)

    def body(x_ref, ya_ref, yb_ref, yc_ref, yd_ref, gates_ref, bg_ref, w_ref, out_ref):
        pieces = []
        for br in range(4):
            cols = slice(br * D_BRANCH, (br + 1) * D_BRANCH)
            y = _branch_inputs((ya_ref, yb_ref, yc_ref, yd_ref), br)
            r = lax.rsqrt(jnp.mean(y * y, axis=-1, keepdims=True) + EPS)
            gt = gates_ref[:, cols]
            pieces.append((y * r * bg_ref[:, cols]) * (gt * _sigmoid(gt)))
        merged = jnp.concatenate(pieces, axis=1).astype(BF16)
        out_ref[...] = x_ref[...] + _dot(merged, w_ref[...])

    head_spec = pl.BlockSpec((N_HEADS, tm, HEAD_DIM), lambda i: (0, i, 0))
    return pl.pallas_call(
        body, name="outproj_fwd", grid=(s // tm,),
        in_specs=[pl.BlockSpec((tm, D_MODEL), lambda i: (i, 0)),
                  head_spec, pl.BlockSpec((tm, D_BRANCH), lambda i: (i, 0)), head_spec, head_spec,
                  pl.BlockSpec((tm, D_MODEL), lambda i: (i, 0)),
                  pl.BlockSpec((1, D_MODEL), lambda i: (0, 0)),
                  pl.BlockSpec((D_MODEL, D_MODEL), lambda i: (0, 0))],
        out_specs=pl.BlockSpec((tm, D_MODEL), lambda i: (i, 0)),
        out_shape=jax.ShapeDtypeStruct((s, D_MODEL), F32),
        compiler_params=_params(("arbitrary",)),
    )(x, ya, yb, yc, yd, gates, bg, wout)


def outproj_bwd(dout, ya, yb, yc, yd, gates, bg, wout):
    s = dout.shape[0]
    tm = min(ROW_T, s)

    def body(dout_ref, ya_ref, yb_ref, yc_ref, yd_ref, gates_ref, bg_ref, w_ref,
             dya_ref, dyb_ref, dyc_ref, dyd_ref, dgates_ref, dbg_ref, dw_ref):
        i = pl.program_id(0)

        @pl.when(i == 0)
        def _():
            dbg_ref[...] = jnp.zeros_like(dbg_ref)
            dw_ref[...] = jnp.zeros_like(dw_ref)

        doutb = dout_ref[...].astype(BF16)
        dmerged = _dot_nt(doutb, w_ref[...])
        pieces = []
        for br in range(4):
            cols = slice(br * D_BRANCH, (br + 1) * D_BRANCH)
            y = _branch_inputs((ya_ref, yb_ref, yc_ref, yd_ref), br)
            r = lax.rsqrt(jnp.mean(y * y, axis=-1, keepdims=True) + EPS)
            yn = y * r
            bgv = bg_ref[:, cols]
            gt = gates_ref[:, cols]
            sig = _sigmoid(gt)
            act = gt * sig
            n = yn * bgv
            pieces.append(n * act)
            dm = dmerged[:, cols]
            dn = dm * act
            dgates_ref[:, cols] = (dm * n * (sig * (1.0 + gt * (1.0 - sig)))).astype(BF16)
            dbg_ref[:, cols] += jnp.sum(dn * yn, axis=0, keepdims=True)
            u = dn * bgv
            dy = r * (u - yn * jnp.mean(yn * u, axis=-1, keepdims=True))
            if br == 1:
                dyb_ref[...] = dy
            else:
                dref = (dya_ref, None, dyc_ref, dyd_ref)[br]
                for hh in range(N_HEADS):
                    dref[hh] = dy[:, hh * HEAD_DIM:(hh + 1) * HEAD_DIM].astype(BF16)
        merged = jnp.concatenate(pieces, axis=1).astype(BF16)
        dw_ref[...] += _dot_tn(merged, doutb)

    head_spec = pl.BlockSpec((N_HEADS, tm, HEAD_DIM), lambda i: (0, i, 0))
    head_shape = jax.ShapeDtypeStruct((N_HEADS, s, HEAD_DIM), BF16)
    return pl.pallas_call(
        body, name="outproj_bwd", grid=(s // tm,),
        in_specs=[pl.BlockSpec((tm, D_MODEL), lambda i: (i, 0)),
                  head_spec, pl.BlockSpec((tm, D_BRANCH), lambda i: (i, 0)), head_spec, head_spec,
                  pl.BlockSpec((tm, D_MODEL), lambda i: (i, 0)),
                  pl.BlockSpec((1, D_MODEL), lambda i: (0, 0)),
                  pl.BlockSpec((D_MODEL, D_MODEL), lambda i: (0, 0))],
        out_specs=[head_spec, pl.BlockSpec((tm, D_BRANCH), lambda i: (i, 0)), head_spec, head_spec,
                   pl.BlockSpec((tm, D_MODEL), lambda i: (i, 0)),
                   pl.BlockSpec((1, D_MODEL), lambda i: (0, 0)),
                   pl.BlockSpec((D_MODEL, D_MODEL), lambda i: (0, 0))],
        out_shape=[head_shape, jax.ShapeDtypeStruct((s, D_BRANCH), F32), head_shape, head_shape,
                   jax.ShapeDtypeStruct((s, D_MODEL), BF16),
                   jax.ShapeDtypeStruct((1, D_MODEL), F32),
                   jax.ShapeDtypeStruct((D_MODEL, D_MODEL), F32)],
        compiler_params=_params(("arbitrary",)),
    )(dout, ya, yb, yc, yd, gates, bg, wout)


def final_loss(x, tgt, g):
    s = x.shape[0]
    tm = min(ROW_T, s)

    def body(x_ref, t_ref, g_ref, loss_ref, dx_ref, dg_ref):
        i = pl.program_id(0)

        @pl.when(i == 0)
        def _():
            loss_ref[...] = jnp.zeros_like(loss_ref)
            dg_ref[...] = jnp.zeros_like(dg_ref)

        xv = x_ref[...]
        gv = g_ref[...]
        r = lax.rsqrt(jnp.mean(xv * xv, axis=-1, keepdims=True) + EPS)
        xn = xv * r
        err = xn * gv - t_ref[...]
        loss_ref[...] += jnp.sum(err * err) * (0.5 / D_MODEL)
        dy = err * (1.0 / D_MODEL)
        u = dy * gv
        dx_ref[...] = r * (u - xn * jnp.mean(xn * u, axis=-1, keepdims=True))
        dg_ref[...] += jnp.sum(dy * xn, axis=0, keepdims=True)

    return pl.pallas_call(
        body, name="final_loss", grid=(s // tm,),
        in_specs=[pl.BlockSpec((tm, D_MODEL), lambda i: (i, 0)),
                  pl.BlockSpec((tm, D_MODEL), lambda i: (i, 0)),
                  pl.BlockSpec((1, D_MODEL), lambda i: (0, 0))],
        out_specs=[pl.BlockSpec((1, 128), lambda i: (0, 0)),
                   pl.BlockSpec((tm, D_MODEL), lambda i: (i, 0)),
                   pl.BlockSpec((1, D_MODEL), lambda i: (0, 0))],
        out_shape=[jax.ShapeDtypeStruct((1, 128), F32),
                   jax.ShapeDtypeStruct((s, D_MODEL), F32),
                   jax.ShapeDtypeStruct((1, D_MODEL), F32)],
        compiler_params=_params(("arbitrary",)),
    )(x, tgt, g)


def _rel_index():
    i = np.arange(A_TQ)[:, None]
    j = np.arange(A_BAND)[None, :]
    rel = np.clip(i - j + (A_BAND - A_TQ), -MAX_REL, MAX_REL) + MAX_REL
    dchunk = i // CHUNK + LOOKBACK - j // CHUNK
    valid = (dchunk >= 0) & (dchunk <= LOOKBACK)
    return jnp.asarray(np.where(valid, rel, -1).astype(np.int32))


def _layer_consts(p):
    tbias = relbias_tile(p["rel_bias"], _rel_index())
    return dict(
        norm_g=p["norm_g"].reshape(1, D_MODEL),
        v_gain=p["v_gain"].reshape(1, D_BRANCH),
        b_col=p["b_s"].reshape(N_HEADS, SG_CHUNK, 1),
        bg=p["branch_gain"].reshape(1, D_MODEL),
        tbias=tbias,
    )


def _gate_layout(fp, b_f, s):
    nb = s // 128
    ft = fp[:, :N_HEADS].T.reshape(N_HEADS * nb, 128)
    bcol = jnp.repeat(b_f, nb).reshape(N_HEADS * nb, 1)
    return ft, bcol


def layer_fwd(x, p):
    s = x.shape[0]
    c = _layer_consts(p)
    h, qkv, kva, gates, uv, fp = inproj_fwd(x, c["norm_g"], p["wp"])
    ya, lse_a = mix_a_fwd(qkv, kva, c["tbias"])
    yb = mix_b_fwd(uv, c["v_gain"], p["w_s"], c["b_col"])
    ft, bcol = _gate_layout(fp, p["b_f"], s)
    c_row = fox_gate_fwd(ft, bcol).reshape(N_HEADS, s // ATT_T, 1, ATT_T)
    yc, ref_c, rl_c = fox_fwd(qkv, c_row)
    yd, band_d = sb_fwd(qkv)
    out = outproj_fwd(x, ya, yb, yc, yd, gates, c["bg"], p["wout"])
    saved = dict(consts=c, x=x, h=h, qkv=qkv, gates=gates, uv=uv, kva=kva, ft=ft, bcol=bcol,
                 c_row=c_row, ya=ya, lse_a=lse_a, yb=yb, yc=yc, ref_c=ref_c, rl_c=rl_c, yd=yd, band_d=band_d)
    return out, saved


def layer_bwd(dout, p, sv):
    s = dout.shape[0]
    c = sv["consts"]
    dya, dyb, dyc, dyd, dgates, dbg, dwout = outproj_bwd(
        dout, sv["ya"], sv["yb"], sv["yc"], sv["yd"], sv["gates"], c["bg"], p["wout"])
    dqa, dka, dva, dt = mix_a_bwd(sv["qkv"], sv["kva"], c["tbias"], dya, sv["ya"], sv["lse_a"])
    drel = relbias_grad(dt, _rel_index())[:N_HEADS, :2 * MAX_REL + 1]
    duv, dws, dbs, dvgain = mix_b_bwd(sv["uv"], c["v_gain"], p["w_s"], c["b_col"], dyb)
    dqc, dkc, dvc, dc = fox_bwd(sv["qkv"], sv["c_row"], dyc, sv["yc"], sv["ref_c"], sv["rl_c"])
    dft, dbf = fox_gate_bwd(sv["ft"], sv["bcol"], dc.reshape(N_HEADS * (s // 128), 128))
    dfp = jnp.pad(dft.reshape(N_HEADS, s).T, ((0, 0), (0, 128 - N_HEADS)))
    dqd, dkd, dvd = sb_bwd(sv["qkv"], dyd, sv["yd"], sv["band_d"])
    dp, dx, dnorm = inproj_bwd((dqa, dka, dva, dqc, dkc, dvc, dqd, dkd, dvd), dgates, duv, dfp,
                               p["wp"], sv["x"], c["norm_g"], dout)
    grads = dict(norm_g=dnorm.reshape(D_MODEL), w_in_shards=inproj_wgrad(sv["h"], dp), b_f=dbf[:N_HEADS, 0], rel_bias=drel,
                 w_s=dws, b_s=dbs.reshape(N_HEADS, SG_CHUNK), v_gain=dvgain.reshape(D_BRANCH),
                 branch_gain=dbg.reshape(4, D_BRANCH), wout=dwout)
    return dx, grads


def local_step(x, tgt, layers, final_g):
    saved = []
    cur = x
    for p in layers:
        cur, sv = layer_fwd(cur, p)
        saved.append(sv)
    loss, dcur, dfinal = final_loss(cur, tgt, final_g.reshape(1, D_MODEL))
    grads = [None] * len(layers)
    for l in reversed(range(len(layers))):
        dcur, grads[l] = layer_bwd(dcur, layers[l], saved[l])
    return loss[0, 0], dcur, grads, dfinal.reshape(D_MODEL)


def gather_weights(w_in, w_out, gains):
    depth = w_in.shape[0]

    def body(in_ref, out_ref, g_ref, oin_ref, oout_ref, og_ref, send_sems, recv_sems, loc_sems):
        x, y, c = lax.axis_index("x"), lax.axis_index("y"), lax.axis_index("c")
        me = 2 * x + y
        chips = [(1 - x, y), (x, 1 - y), (1 - x, 1 - y)]
        pairs = [(in_ref, lambda s: oin_ref.at[:, s]), (out_ref, lambda s: oout_ref.at[:, s]), (g_ref, lambda s: og_ref.at[s])]
        local = [pltpu.make_async_copy(src, dst(me), loc_sems.at[n]) for n, (src, dst) in enumerate(pairs)]
        for cp in local:
            cp.start()

        def copy(j, n, slot):
            src, dst = pairs[n]
            return pltpu.make_async_remote_copy(
                src_ref=src, dst_ref=dst(slot), send_sem=send_sems.at[3 * j + n], recv_sem=recv_sems.at[3 * j + n],
                device_id=(chips[j][0], chips[j][1], c), device_id_type=MESH)

        sends = [copy(j, n, me) for j in range(3) for n in range(3)]
        for cp in sends:
            cp.start()
        for j in range(3):
            for n in range(3):
                copy(j, n, 2 * chips[j][0] + chips[j][1]).wait_recv()
        for cp in sends:
            cp.wait_send()
        for cp in local:
            cp.wait()

    any_spec = pl.BlockSpec(memory_space=pl.ANY)
    return pl.pallas_call(
        body, name="gather_weights",
        in_specs=[any_spec] * 3, out_specs=[any_spec] * 3,
        out_shape=[jax.ShapeDtypeStruct((depth, 4) + w_in.shape[1:], w_in.dtype),
                   jax.ShapeDtypeStruct((depth, 4) + w_out.shape[1:], w_out.dtype),
                   jax.ShapeDtypeStruct((4,) + gains.shape, gains.dtype)],
        scratch_shapes=[pltpu.SemaphoreType.DMA((9,)), pltpu.SemaphoreType.DMA((9,)), pltpu.SemaphoreType.DMA((3,))],
    )(w_in, w_out, gains)


def pack_w_in(shards):
    depth = shards.shape[0]
    tr = 256

    def body(s_ref, o_ref):
        full = jnp.concatenate([s_ref[n] for n in range(4)], axis=1)
        o_ref[...] = jnp.concatenate([full[:, :SEC_D_Q], full[:, SEC_D_Q + N_HEADS:], full[:, SEC_D_Q:SEC_D_Q + N_HEADS],
                                      jnp.zeros((tr, N_PACK - N_IN), BF16)], axis=1)

    return pl.pallas_call(
        body, name="pack_w_in", grid=(depth, D_MODEL // tr),
        in_specs=[pl.BlockSpec((None, 4, tr, N_SHARD), lambda l, r: (l, 0, r, 0))],
        out_specs=pl.BlockSpec((None, tr, N_PACK), lambda l, r: (l, r, 0)),
        out_shape=jax.ShapeDtypeStruct((depth, D_MODEL, N_PACK), BF16),
        compiler_params=_params(("arbitrary", "arbitrary")),
    )(shards)


def exchange_grads(d_w_in, d_w_out, d_gain, small):
    depth = d_w_in.shape[0]

    def body(in_ref, out_ref, gain_ref, small_ref, rin_ref, rout_ref, rgain_ref, rsmall_ref,
             send_sems, recv_sems, loc_sems):
        x, y, c = lax.axis_index("x"), lax.axis_index("y"), lax.axis_index("c")
        me_chip = 2 * x + y
        me = 4 * x + 2 * y + c
        peers = [(x, y, 1 - c)]
        for px, py in [(1 - x, y), (x, 1 - y), (1 - x, 1 - y)]:
            peers += [(px, py, c), (px, py, 1 - c)]
        flows = [(lambda s: in_ref.at[:, s], lambda d: rin_ref.at[d]),
                 (lambda s: out_ref.at[:, s], lambda d: rout_ref.at[d]),
                 (lambda s: gain_ref.at[s], lambda d: rgain_ref.at[d]),
                 (lambda s: small_ref, lambda d: rsmall_ref.at[d])]
        nflow = len(flows)
        local = [pltpu.make_async_copy(src(me_chip), dst(me), loc_sems.at[f]) for f, (src, dst) in enumerate(flows)]
        for cp in local:
            cp.start()

        def copies(n, chip, slot):
            return [pltpu.make_async_remote_copy(src_ref=src(chip), dst_ref=dst(slot), send_sem=send_sems.at[nflow * n + f],
                                                 recv_sem=recv_sems.at[nflow * n + f], device_id=peers[n], device_id_type=MESH)
                    for f, (src, dst) in enumerate(flows)]

        sends = [cp for n, (px, py, _) in enumerate(peers) for cp in copies(n, 2 * px + py, me)]
        for cp in sends:
            cp.start()
        for n, (px, py, pc) in enumerate(peers):
            for cp in copies(n, me_chip, 4 * px + 2 * py + pc):
                cp.wait_recv()
        for cp in sends:
            cp.wait_send()
        for cp in local:
            cp.wait()

    any_spec = pl.BlockSpec(memory_space=pl.ANY)
    return pl.pallas_call(
        body, name="exchange_grads",
        in_specs=[any_spec] * 4, out_specs=[any_spec] * 4,
        out_shape=[jax.ShapeDtypeStruct((8, depth) + d_w_in.shape[2:], d_w_in.dtype),
                   jax.ShapeDtypeStruct((8, depth) + d_w_out.shape[2:], d_w_out.dtype),
                   jax.ShapeDtypeStruct((8,) + d_gain.shape[1:], d_gain.dtype),
                   jax.ShapeDtypeStruct((8,) + small.shape, small.dtype)],
        scratch_shapes=[pltpu.SemaphoreType.DMA((28,)), pltpu.SemaphoreType.DMA((28,)), pltpu.SemaphoreType.DMA((4,))],
    )(d_w_in, d_w_out, d_gain, small)


def adamw_reduce(parts, w, m, v, name, tr):
    rows, width = w.shape
    c1 = 1.0 - ADAM_B1 ** ADAM_STEP
    c2 = 1.0 - ADAM_B2 ** ADAM_STEP

    def body(p_ref, w_ref, m_ref, v_ref, g_ref, d_ref, nm_ref, nv_ref):
        g = p_ref[0].astype(F32)
        for n in range(1, 8):
            g = g + p_ref[n].astype(F32)
        g_ref[...] = g
        nm = ADAM_B1 * m_ref[...] + (1.0 - ADAM_B1) * g
        nv = ADAM_B2 * v_ref[...] + (1.0 - ADAM_B2) * (g * g)
        nm_ref[...] = nm
        nv_ref[...] = nv
        d_ref[...] = -ADAM_LR * ((nm / c1) / (jnp.sqrt(nv / c2) + ADAM_EPS) + ADAM_WD * w_ref[...])

    spec = pl.BlockSpec((tr, width), lambda i: (i, 0))
    shape = jax.ShapeDtypeStruct((rows, width), F32)
    return pl.pallas_call(
        body, name=name, grid=(rows // tr,),
        in_specs=[pl.BlockSpec((8, tr, width), lambda i: (0, i, 0)), spec, spec, spec],
        out_specs=[spec] * 4, out_shape=[shape] * 4,
        compiler_params=_params(("arbitrary",)),
    )(parts, w, m, v)


SMALL =("norm_g", "b_f", "rel_bias", "w_s", "b_s", "v_gain", "final_g")
WEIGHTS = ("norm_g", "w_in", "b_f", "rel_bias", "w_s", "b_s", "v_gain", "branch_gain", "w_out", "final_g")
PACK_ROW_TILE = 512


def _rows_of(shape):
    return -(-int(np.prod(shape)) // 128)


def _pack(leaves):
    parts = []
    for a in leaves:
        flat = a.reshape(-1).astype(F32)
        parts.append(jnp.pad(flat, (0, _rows_of(a.shape) * 128 - flat.shape[0])))
    flat = jnp.concatenate(parts)
    rows = flat.shape[0] // 128
    total = -(-rows // PACK_ROW_TILE) * PACK_ROW_TILE
    return jnp.pad(flat, (0, (total - rows) * 128)).reshape(total, 128)


def _unpack(slab, shapes):
    out, row = [], 0
    for shp in shapes:
        n = int(np.prod(shp))
        r = _rows_of(shp)
        out.append(slab[row:row + r].reshape(-1)[:n].reshape(shp))
        row += r
    return out


def kernel(x, norm_g, w_in, b_f, rel_bias, w_s, b_s, v_gain, branch_gain, w_out, final_g, loss_target, m_norm_g, m_w_in, m_b_f, m_rel_bias, m_w_s, m_b_s, m_v_gain, m_branch_gain, m_w_out, m_final_g, v_norm_g, v_w_in, v_b_f, v_rel_bias, v_w_s, v_b_s, v_v_gain, v_branch_gain, v_w_out, v_final_g):
    depth = norm_g.shape[0]
    weights = dict(norm_g=norm_g, w_in=w_in, b_f=b_f, rel_bias=rel_bias, w_s=w_s, b_s=b_s, v_gain=v_gain,
                   branch_gain=branch_gain, w_out=w_out, final_g=final_g)
    mom1 = dict(norm_g=m_norm_g, w_in=m_w_in, b_f=m_b_f, rel_bias=m_rel_bias, w_s=m_w_s, b_s=m_b_s,
                v_gain=m_v_gain, branch_gain=m_branch_gain, w_out=m_w_out, final_g=m_final_g)
    mom2 = dict(norm_g=v_norm_g, w_in=v_w_in, b_f=v_b_f, rel_bias=v_rel_bias, w_s=v_w_s, b_s=v_b_s,
                v_gain=v_v_gain, branch_gain=v_branch_gain, w_out=v_w_out, final_g=v_final_g)

    wf = jnp.pad(branch_gain.reshape(-1), (0, 8 * 128 - branch_gain.size)).reshape(8, 128)
    w_in_shards, w_out_shards, gf = gather_weights(w_in.astype(BF16), w_out.astype(BF16), wf)
    wp_full = pack_w_in(w_in_shards)
    w_out_full = w_out_shards.reshape(depth, D_MODEL, D_MODEL)
    bg_full = gf.reshape(4, -1)[:, :branch_gain.size].reshape((4,) + branch_gain.shape)
    bg_full = jnp.moveaxis(bg_full, 0, 2).reshape(depth, 4, D_BRANCH)

    layers = [dict(norm_g=norm_g[l], wp=wp_full[l], b_f=b_f[l], rel_bias=rel_bias[l], w_s=w_s[l],
                   b_s=b_s[l], v_gain=v_gain[l], branch_gain=bg_full[l], wout=w_out_full[l]) for l in range(depth)]

    loss_part, grad_x, lgrads, dfinal = local_step(x[0], loss_target[0], layers, final_g)
    loss = lax.psum(loss_part, ("x", "y", "c"))

    stack = lambda k: jnp.stack([g[k] for g in lgrads])
    d_w_in = stack("w_in_shards")
    d_w_out = stack("wout").astype(BF16).reshape(depth, 4, D_BRANCH, D_MODEL)
    d_gain = jnp.moveaxis(stack("branch_gain").reshape(depth, 4, 4, HEAD_DIM), 2, 0).reshape(4, -1)
    d_gain = jnp.pad(d_gain, ((0, 0), (0, 8 * 128 - d_gain.shape[1]))).reshape(4, 8, 128)
    small = dict(norm_g=stack("norm_g"), b_f=stack("b_f"), rel_bias=stack("rel_bias"), w_s=stack("w_s"),
                 b_s=stack("b_s"), v_gain=stack("v_gain"), final_g=dfinal)
    parts_in, parts_out, parts_gain, parts_small = exchange_grads(d_w_in, d_w_out, d_gain, _pack([small[k] for k in SMALL]))

    outs = {}
    tags = ("grad", "delta", "new_m", "new_v")
    for k, parts, tr in (("w_in", parts_in, 256), ("w_out", parts_out, 256)):
        rows = depth * weights[k].shape[1]
        flat = lambda a: a.reshape(rows, a.shape[-1])
        res = adamw_reduce(parts.reshape(8, rows, parts.shape[-1]), flat(weights[k]), flat(mom1[k]), flat(mom2[k]),
                           "adamw_" + k, tr)
        for tag, a in zip(tags, res):
            outs[tag, k] = a.reshape(weights[k].shape)
    gain8 = lambda a: jnp.pad(a.reshape(-1), (0, 8 * 128 - a.size)).reshape(8, 128)
    res = adamw_reduce(parts_gain, gain8(branch_gain), gain8(m_branch_gain), gain8(v_branch_gain), "adamw_gain", 8)
    for tag, a in zip(tags, res):
        outs[tag, "branch_gain"] = a.reshape(-1)[:branch_gain.size].reshape(branch_gain.shape)
    pack_small = lambda d: _pack([d[k] for k in SMALL])
    res = adamw_reduce(parts_small, pack_small(weights), pack_small(mom1), pack_small(mom2), "adamw_small", PACK_ROW_TILE)
    for tag, slab in zip(tags, res):
        for k, a in zip(SMALL, _unpack(slab, [weights[k].shape for k in SMALL])):
            outs[tag, k] = a
    result = [loss, grad_x[None]]
    for tag in ("grad", "delta", "new_m", "new_v"):
        result += [outs[tag, k] for k in WEIGHTS]
    return tuple(result)
```

```python
import functools

import jax
import jax.numpy as jnp
import numpy as np
from jax import lax
from jax.experimental import pallas as pl
from jax.experimental.pallas import tpu as pltpu

F32 = jnp.float32
BF16 = jnp.bfloat16
MESH = pl.DeviceIdType.MESH

D_MODEL = 1024
D_BRANCH = 256
N_HEADS = 4
HEAD_DIM = 64
CHUNK = 64
LOOKBACK = 8
MAX_REL = 128
SG_CHUNK = 128
EPS = 1e-6
N_IN = 3844
N_PACK = 3968
F_COL = 3840
N_SHARD = 961
NEG = -1e30

A_TQ = 128
A_BAND = A_TQ + LOOKBACK * CHUNK
REL_LO = MAX_REL - (CHUNK - 1)
REL_HI = 2 * MAX_REL + 1
A_PAD = LOOKBACK * CHUNK
A_QB = 1024
ATT_T = 256
FOX_TQ = 512
FOX_WIDE = 4
FOX_DEAD2 = -160.0
LOG2E = 1.4426950408889634
SB_SUB = 128
SB_BACK = 256
SB_BAND = SB_SUB + SB_BACK
SB_DEAD = -110.0
ROW_T = 512
VMEM_LIMIT = 56 * 1024 * 1024

ADAM_LR = 0.001
ADAM_B1 = 0.9
ADAM_B2 = 0.999
ADAM_EPS = 1e-08
ADAM_WD = 0.01
ADAM_STEP = 10

SEC_A_Q, SEC_A_K, SEC_A_V, SEC_A_G = 0, 256, 512, 768
SEC_B_U, SEC_B_V, SEC_B_G = 1024, 1280, 1536
SEC_C_Q, SEC_C_K, SEC_C_V, SEC_C_G = 1792, 2048, 2304, 2560
SEC_D_Q, SEC_D_K, SEC_D_V, SEC_D_G = 2816, 3072, 3328, 3584
QKV_SECS = (SEC_A_Q, SEC_C_Q, SEC_C_K, SEC_C_V, SEC_D_Q, SEC_D_K, SEC_D_V)
GATE_SECS = (SEC_A_G, SEC_B_G, SEC_C_G, SEC_D_G)


def _dot(a, b):
    return jnp.dot(a, b, preferred_element_type=F32)


def _dot_nt(a, b):
    return lax.dot_general(a, b, (((1,), (1,)), ((), ())), preferred_element_type=F32)


def _dot_tn(a, b):
    return lax.dot_general(a, b, (((0,), (0,)), ((), ())), preferred_element_type=F32)


def _split2(x):
    hi = x.astype(BF16)
    lo = (x - hi.astype(F32)).astype(BF16)
    return hi, lo


def _split3(x):
    hi = x.astype(BF16)
    r = x - hi.astype(F32)
    mid = r.astype(BF16)
    lo = (r - mid.astype(F32)).astype(BF16)
    return hi, mid, lo


def _sigmoid(x):
    return 1.0 / (1.0 + jnp.exp(-x))


def _params(sem=None, vmem=VMEM_LIMIT):
    return pltpu.CompilerParams(dimension_semantics=sem, vmem_limit_bytes=vmem)


def _heads_to_lanes(ref):
    return jnp.concatenate([ref[h] for h in range(N_HEADS)], axis=1)


def inproj_fwd(x, g, wp):
    s = x.shape[0]
    tm = A_PAD

    def body(x_ref, g_ref, w_ref, h_ref, qkv_ref, kva_ref, gates_ref, uv_ref, f_ref):
        xv = x_ref[...]
        r = lax.rsqrt(jnp.mean(xv * xv, axis=-1, keepdims=True) + EPS)
        h = (xv * r * g_ref[...]).astype(BF16)
        h_ref[...] = h
        for n, off in enumerate(QKV_SECS):
            p = _dot(h, w_ref[:, off:off + D_BRANCH])
            for hh in range(N_HEADS):
                qkv_ref[n, hh] = p[:, hh * HEAD_DIM:(hh + 1) * HEAD_DIM].astype(BF16)
        for n, off in enumerate((SEC_A_K, SEC_A_V)):
            p = _dot(h, w_ref[:, off:off + D_BRANCH])
            for hh in range(N_HEADS):
                kva_ref[n, hh] = p[:, hh * HEAD_DIM:(hh + 1) * HEAD_DIM].astype(BF16)
        for n, off in enumerate(GATE_SECS):
            gates_ref[:, n * D_BRANCH:(n + 1) * D_BRANCH] = _dot(h, w_ref[:, off:off + D_BRANCH])
        uv_ref[...] = _dot(h, w_ref[:, SEC_B_U:SEC_B_U + 2 * D_BRANCH])
        f_ref[...] = _dot(h, w_ref[:, F_COL:F_COL + 128])

    return pl.pallas_call(
        body, name="inproj_fwd", grid=(s // tm,),
        in_specs=[pl.BlockSpec((tm, D_MODEL), lambda i: (i, 0)),
                  pl.BlockSpec((1, D_MODEL), lambda i: (0, 0)),
                  pl.BlockSpec((D_MODEL, N_PACK), lambda i: (0, 0))],
        out_specs=[pl.BlockSpec((tm, D_MODEL), lambda i: (i, 0)),
                   pl.BlockSpec((len(QKV_SECS), N_HEADS, tm, HEAD_DIM), lambda i: (0, 0, i, 0)),
                   pl.BlockSpec((2, N_HEADS, tm, HEAD_DIM), lambda i: (0, 0, i + 1, 0)),
                   pl.BlockSpec((tm, D_MODEL), lambda i: (i, 0)),
                   pl.BlockSpec((tm, 2 * D_BRANCH), lambda i: (i, 0)),
                   pl.BlockSpec((tm, 128), lambda i: (i, 0))],
        out_shape=[jax.ShapeDtypeStruct((s, D_MODEL), BF16),
                   jax.ShapeDtypeStruct((len(QKV_SECS), N_HEADS, s, HEAD_DIM), BF16),
                   jax.ShapeDtypeStruct((2, N_HEADS, s + tm, HEAD_DIM), BF16),
                   jax.ShapeDtypeStruct((s, D_MODEL), F32),
                   jax.ShapeDtypeStruct((s, 2 * D_BRANCH), F32),
                   jax.ShapeDtypeStruct((s, 128), F32)],
        compiler_params=_params(("arbitrary",)),
    )(x, g, wp)


def inproj_bwd(dqkv, dgates, duv, dfp, wp, x, g, dres):
    s = x.shape[0]
    tm = A_PAD

    def body(*refs):
        dq_refs = refs[:9]
        dgates_ref, duv_ref, dfp_ref, w_ref, x_ref, g_ref, dres_ref, dp_ref, dx_ref, dg_ref = refs[9:]
        i = pl.program_id(0)
        a_q, a_k, a_v, c_q, c_k, c_v, d_q, d_k, d_v = [_heads_to_lanes(r).astype(BF16) for r in dq_refs]
        dgt = dgates_ref[...]
        duv_b = duv_ref[...].astype(BF16)
        dp = jnp.concatenate(
            [a_q, a_k, a_v, dgt[:, 0:256], duv_b, dgt[:, 256:512], c_q, c_k, c_v, dgt[:, 512:768],
             d_q, d_k, d_v, dgt[:, 768:1024], dfp_ref[...].astype(BF16)], axis=1)
        dp_ref[...] = dp
        dh = _dot_nt(dp, w_ref[...])
        xv = x_ref[...]
        r = lax.rsqrt(jnp.mean(xv * xv, axis=-1, keepdims=True) + EPS)
        xn = xv * r
        u = dh * g_ref[...]
        dx_ref[...] = dres_ref[...] + r * (u - xn * jnp.mean(xn * u, axis=-1, keepdims=True))

        @pl.when(i == 0)
        def _():
            dg_ref[...] = jnp.zeros_like(dg_ref)

        dg_ref[...] += jnp.sum(dh * xn, axis=0, keepdims=True)

    head_spec = pl.BlockSpec((N_HEADS, tm, HEAD_DIM), lambda i: (0, i, 0))
    padded_spec = pl.BlockSpec((N_HEADS, tm, HEAD_DIM), lambda i: (0, i + 1, 0))
    return pl.pallas_call(
        body, name="inproj_bwd", grid=(s // tm,),
        in_specs=[head_spec, padded_spec, padded_spec] + [head_spec] * 6 + [
            pl.BlockSpec((tm, D_MODEL), lambda i: (i, 0)),
            pl.BlockSpec((tm, 2 * D_BRANCH), lambda i: (i, 0)),
            pl.BlockSpec((tm, 128), lambda i: (i, 0)),
            pl.BlockSpec((D_MODEL, N_PACK), lambda i: (0, 0)),
            pl.BlockSpec((tm, D_MODEL), lambda i: (i, 0)),
            pl.BlockSpec((1, D_MODEL), lambda i: (0, 0)),
            pl.BlockSpec((tm, D_MODEL), lambda i: (i, 0))],
        out_specs=[pl.BlockSpec((tm, N_PACK), lambda i: (i, 0)),
                   pl.BlockSpec((tm, D_MODEL), lambda i: (i, 0)),
                   pl.BlockSpec((1, D_MODEL), lambda i: (0, 0))],
        out_shape=[jax.ShapeDtypeStruct((s, N_PACK), BF16),
                   jax.ShapeDtypeStruct((s, D_MODEL), F32),
                   jax.ShapeDtypeStruct((1, D_MODEL), F32)],
        compiler_params=_params(("arbitrary",)),
    )(*dqkv, dgates, duv, dfp, wp, x, g, dres)


def inproj_wgrad(h, dp):
    s, m = h.shape
    tm = min(2 * ROW_T, s)
    tmm = 256
    nsteps = s // tm

    def body(a_ref, b_ref, o_ref, acc_ref):
        k = pl.program_id(1)

        @pl.when(k == 0)
        def _():
            acc_ref[...] = jnp.zeros_like(acc_ref)

        acc_ref[...] += _dot_tn(a_ref[...], b_ref[...])

        @pl.when(k == nsteps - 1)
        def _():
            acc = acc_ref[...]
            full = jnp.concatenate([acc[:, :SEC_D_Q], acc[:, F_COL:F_COL + N_HEADS], acc[:, SEC_D_Q:F_COL]], axis=1)
            for n in range(4):
                o_ref[n] = full[:, n * N_SHARD:(n + 1) * N_SHARD].astype(BF16)

    return pl.pallas_call(
        body, name="inproj_wgrad", grid=(m // tmm, nsteps),
        in_specs=[pl.BlockSpec((tm, tmm), lambda j, k: (k, j)),
                  pl.BlockSpec((tm, N_PACK), lambda j, k: (k, 0))],
        out_specs=pl.BlockSpec((4, tmm, N_SHARD), lambda j, k: (0, j, 0)),
        out_shape=jax.ShapeDtypeStruct((4, m, N_SHARD), BF16),
        scratch_shapes=[pltpu.VMEM((tmm, N_PACK), F32)],
        compiler_params=_params(("arbitrary", "arbitrary")),
    )(h, dp)


def _a_specs(s):
    nq = s // A_QB
    per = A_QB // A_PAD
    q_spec = pl.BlockSpec((None, None, A_QB, HEAD_DIM), lambda h, i: (0, h, jnp.minimum(i, nq - 1), 0))
    kv_specs = [pl.BlockSpec((None, None, A_PAD, HEAD_DIM),
                             lambda h, i, n=n, m=m: (n, h, jnp.minimum(per * i + m, per * nq), 0))
                for n in range(2) for m in range(per + 1)]
    t_spec = pl.BlockSpec((None, A_TQ, A_BAND), lambda h, i: (h, 0, 0))
    return nq, q_spec, kv_specs, t_spec


def _a_window(refs, i):
    first = refs[0][...]
    return jnp.concatenate([jnp.where(i > 0, first, jnp.zeros_like(first))] + [r[...] for r in refs[1:]], axis=0)


def _a_scores(q_ref, k, t_ref, i, j):
    rows = slice(j * A_TQ, (j + 1) * A_TQ)
    qs = q_ref[rows, :] * 0.125
    kj = k[j * A_TQ:j * A_TQ + A_BAND, :]
    sc = _dot_nt(qs, kj) + t_ref[...]
    col = lax.broadcasted_iota(jnp.int32, (A_TQ, A_BAND), 1)
    sc = jnp.where(col >= A_PAD - i * A_QB - j * A_TQ, sc, NEG)
    return rows, qs, kj, sc


def mix_a_fwd(qkv, kva, tbias):
    s = qkv.shape[2]
    nq, q_spec, kv_specs, t_spec = _a_specs(s)
    nwin = len(kv_specs) // 2

    def body(*refs):
        q_ref, t_ref, o_ref, lse_ref = refs[0], refs[1 + 2 * nwin], refs[2 + 2 * nwin], refs[3 + 2 * nwin]
        i = pl.program_id(1)
        k = _a_window(refs[1:1 + nwin], i)
        v = _a_window(refs[1 + nwin:1 + 2 * nwin], i)
        for j in range(A_QB // A_TQ):
            rows, _, _, sc = _a_scores(q_ref, k, t_ref, i, j)
            m = jnp.max(sc, axis=-1, keepdims=True)
            p = jnp.exp(sc - m)
            l = jnp.sum(p, axis=-1, keepdims=True)
            o_ref[rows, :] = _dot(p.astype(BF16), v[j * A_TQ:j * A_TQ + A_BAND, :]) / l
            lse_ref[rows, :] = m + jnp.log(l)

    return pl.pallas_call(
        body, name="mix_a_fwd", grid=(N_HEADS, nq),
        in_specs=[q_spec] + kv_specs + [t_spec],
        out_specs=[pl.BlockSpec((None, A_QB, HEAD_DIM), lambda h, i: (h, i, 0)),
                   pl.BlockSpec((None, A_QB, 1), lambda h, i: (h, i, 0))],
        out_shape=[jax.ShapeDtypeStruct((N_HEADS, s, HEAD_DIM), F32),
                   jax.ShapeDtypeStruct((N_HEADS, s, 1), F32)],
        compiler_params=_params(("arbitrary", "arbitrary")),
    )(qkv, *([kva] * (2 * nwin)), tbias)


def mix_a_bwd(qkv, kva, tbias, do, o, lse):
    s = qkv.shape[2]
    nq, q_spec, kv_specs, t_spec = _a_specs(s)
    nwin = len(kv_specs) // 2
    row_spec = lambda w: pl.BlockSpec((None, A_QB, w), lambda h, i: (h, jnp.minimum(i, nq - 1), 0))
    done_spec = pl.BlockSpec((None, A_QB, HEAD_DIM), lambda h, i: (h, i, 0))
    win = A_QB + A_PAD

    def body(*refs):
        q_ref = refs[0]
        t_ref, do_ref, o_ref, lse_ref, dq_ref, dk_ref, dv_ref, dt_ref, dk_win, dv_win = refs[1 + 2 * nwin:]
        i = pl.program_id(1)

        @pl.when(i == 0)
        def _():
            dk_win[...] = jnp.zeros_like(dk_win)
            dv_win[...] = jnp.zeros_like(dv_win)
            dt_ref[...] = jnp.zeros_like(dt_ref)

        @pl.when(i < nq)
        def _():
            k = _a_window(refs[1:1 + nwin], i)
            v = _a_window(refs[1 + nwin:1 + 2 * nwin], i)
            dt = jnp.zeros((A_TQ, A_BAND), F32)
            for j in range(A_QB // A_TQ):
                rows, qs, kj, sc = _a_scores(q_ref, k, t_ref, i, j)
                keys = slice(j * A_TQ, j * A_TQ + A_BAND)
                dob = do_ref[rows, :]
                p = jnp.exp(sc - lse_ref[rows, :])
                delta = jnp.sum(o_ref[rows, :] * dob.astype(F32), axis=-1, keepdims=True)
                ds = p * (_dot_nt(dob, v[keys, :]) - delta)
                dsb = ds.astype(BF16)
                dq_ref[rows, :] = _dot(dsb, kj) * 0.125
                dk_win[keys, :] += _dot_tn(dsb, qs)
                dv_win[keys, :] += _dot_tn(p.astype(BF16), dob)
                dt = dt + ds
            dt_ref[...] += dt

        dk_ref[...] = dk_win[0:A_QB, :]
        dv_ref[...] = dv_win[0:A_QB, :]
        dk_rest = dk_win[A_QB:win, :]
        dv_rest = dv_win[A_QB:win, :]
        dk_win[0:A_PAD, :] = dk_rest
        dv_win[0:A_PAD, :] = dv_rest
        dk_win[A_PAD:win, :] = jnp.zeros((A_QB, HEAD_DIM), F32)
        dv_win[A_PAD:win, :] = jnp.zeros((A_QB, HEAD_DIM), F32)

    return pl.pallas_call(
        body, name="mix_a_bwd", grid=(N_HEADS, nq + 1),
        in_specs=[q_spec] + kv_specs + [t_spec, row_spec(HEAD_DIM), row_spec(HEAD_DIM), row_spec(1)],
        out_specs=[row_spec(HEAD_DIM), done_spec, done_spec, t_spec],
        out_shape=[jax.ShapeDtypeStruct((N_HEADS, s, HEAD_DIM), F32),
                   jax.ShapeDtypeStruct((N_HEADS, s + A_QB, HEAD_DIM), F32),
                   jax.ShapeDtypeStruct((N_HEADS, s + A_QB, HEAD_DIM), F32),
                   jax.ShapeDtypeStruct((N_HEADS, A_TQ, A_BAND), F32)],
        scratch_shapes=[pltpu.VMEM((win, HEAD_DIM), F32), pltpu.VMEM((win, HEAD_DIM), F32)],
        compiler_params=_params(("arbitrary", "arbitrary")),
    )(qkv, *([kva] * (2 * nwin)), tbias, do, o, lse)


def relbias_tile(rel_bias, relmat):
    def body(rb_ref, rel_ref, o_ref):
        rel = rel_ref[...]
        o_ref[...] = jnp.full(o_ref.shape, NEG, F32)

        def step(r, carry):
            hit = rel == r
            for h in range(N_HEADS):
                o_ref[h] = jnp.where(hit, rb_ref[h, r], o_ref[h])
            return carry

        lax.fori_loop(REL_LO, REL_HI, step, 0)

    return pl.pallas_call(
        body, name="relbias_tile",
        in_specs=[pl.BlockSpec(memory_space=pltpu.SMEM), pl.BlockSpec(memory_space=pltpu.VMEM)],
        out_specs=pl.BlockSpec(memory_space=pltpu.VMEM),
        out_shape=jax.ShapeDtypeStruct((N_HEADS, A_TQ, A_BAND), F32),
        compiler_params=_params(),
    )(rel_bias, relmat)


def relbias_grad(dt, relmat):
    def body(dt_ref, rel_ref, o_ref):
        rel = rel_ref[...]
        lane = lax.broadcasted_iota(jnp.int32, (8, 384), 1)
        row = lax.broadcasted_iota(jnp.int32, (8, 384), 0)

        def step(r, acc):
            hit = rel == r
            for h in range(N_HEADS):
                val = jnp.sum(jnp.where(hit, dt_ref[h], 0.0))
                acc = jnp.where((lane == r) & (row == h), val, acc)
            return acc

        o_ref[...] = lax.fori_loop(REL_LO, REL_HI, step, jnp.zeros((8, 384), F32))

    return pl.pallas_call(
        body, name="relbias_grad",
        out_shape=jax.ShapeDtypeStruct((8, 384), F32),
        compiler_params=_params(),
    )(dt, relmat)


def _b_norm(v, gain):
    mu = jnp.mean(v, axis=-1, keepdims=True)
    xc = v - mu
    rstd = lax.rsqrt(jnp.mean(xc * xc, axis=-1, keepdims=True) + EPS)
    xhat = xc * rstd
    return xhat, rstd, xhat * gain


def _tril_mask():
    t = lax.broadcasted_iota(jnp.int32, (SG_CHUNK, SG_CHUNK), 0)
    u = lax.broadcasted_iota(jnp.int32, (SG_CHUNK, SG_CHUNK), 1)
    return u <= t


def mix_b_fwd(uv, gain, w_s, b_col):
    s = uv.shape[0]
    tm = min(ROW_T, s)

    def body(uv_ref, gain_ref, w_ref, b_ref, y_ref):
        tril = _tril_mask()
        ws = [jnp.where(tril, w_ref[g], 0.0).astype(BF16) for g in range(N_HEADS)]
        for c in range(tm // SG_CHUNK):
            rows = slice(c * SG_CHUNK, (c + 1) * SG_CHUNK)
            u = uv_ref[rows, 0:D_BRANCH]
            _, _, vn = _b_norm(uv_ref[rows, D_BRANCH:2 * D_BRANCH], gain_ref[...])
            vnb = vn.astype(BF16)
            outs = []
            for g in range(N_HEADS):
                cols = slice(g * HEAD_DIM, (g + 1) * HEAD_DIM)
                mixed = _dot(ws[g], vnb[:, cols]) + b_ref[g]
                outs.append(u[:, cols] * mixed)
            y_ref[rows, :] = jnp.concatenate(outs, axis=1)

    return pl.pallas_call(
        body, name="mix_b_fwd", grid=(s // tm,),
        in_specs=[pl.BlockSpec((tm, 2 * D_BRANCH), lambda i: (i, 0)),
                  pl.BlockSpec((1, D_BRANCH), lambda i: (0, 0)),
                  pl.BlockSpec((N_HEADS, SG_CHUNK, SG_CHUNK), lambda i: (0, 0, 0)),
                  pl.BlockSpec((N_HEADS, SG_CHUNK, 1), lambda i: (0, 0, 0))],
        out_specs=pl.BlockSpec((tm, D_BRANCH), lambda i: (i, 0)),
        out_shape=jax.ShapeDtypeStruct((s, D_BRANCH), F32),
        compiler_params=_params(("arbitrary",)),
    )(uv, gain, w_s, b_col)


def mix_b_bwd(uv, gain, w_s, b_col, dy):
    s = uv.shape[0]
    tm = min(ROW_T, s)

    def body(uv_ref, gain_ref, w_ref, b_ref, dy_ref, duv_ref, dw_ref, db_ref, dgain_ref):
        i = pl.program_id(0)

        @pl.when(i == 0)
        def _():
            dw_ref[...] = jnp.zeros_like(dw_ref)
            db_ref[...] = jnp.zeros_like(db_ref)
            dgain_ref[...] = jnp.zeros_like(dgain_ref)

        tril = _tril_mask()
        ws = [jnp.where(tril, w_ref[g], 0.0).astype(BF16) for g in range(N_HEADS)]
        gain_v = gain_ref[...]
        for c in range(tm // SG_CHUNK):
            rows = slice(c * SG_CHUNK, (c + 1) * SG_CHUNK)
            u = uv_ref[rows, 0:D_BRANCH]
            xhat, rstd, vn = _b_norm(uv_ref[rows, D_BRANCH:2 * D_BRANCH], gain_v)
            vnb = vn.astype(BF16)
            dyv = dy_ref[rows, :]
            dus, dvns = [], []
            for g in range(N_HEADS):
                cols = slice(g * HEAD_DIM, (g + 1) * HEAD_DIM)
                mixed = _dot(ws[g], vnb[:, cols]) + b_ref[g]
                dus.append(dyv[:, cols] * mixed)
                dmixed = dyv[:, cols] * u[:, cols]
                dmb = dmixed.astype(BF16)
                db_ref[g] += jnp.sum(dmixed, axis=-1, keepdims=True)
                dw_ref[g] += jnp.where(tril, _dot_nt(dmb, vnb[:, cols]), 0.0)
                dvns.append(_dot_tn(ws[g], dmb))
            dvn = jnp.concatenate(dvns, axis=1)
            dgain_ref[...] += jnp.sum(dvn * xhat, axis=0, keepdims=True)
            dxh = dvn * gain_v
            dv = rstd * (dxh - jnp.mean(dxh, axis=-1, keepdims=True)
                         - xhat * jnp.mean(dxh * xhat, axis=-1, keepdims=True))
            duv_ref[rows, :] = jnp.concatenate(dus + [dv], axis=1)

    return pl.pallas_call(
        body, name="mix_b_bwd", grid=(s // tm,),
        in_specs=[pl.BlockSpec((tm, 2 * D_BRANCH), lambda i: (i, 0)),
                  pl.BlockSpec((1, D_BRANCH), lambda i: (0, 0)),
                  pl.BlockSpec((N_HEADS, SG_CHUNK, SG_CHUNK), lambda i: (0, 0, 0)),
                  pl.BlockSpec((N_HEADS, SG_CHUNK, 1), lambda i: (0, 0, 0)),
                  pl.BlockSpec((tm, D_BRANCH), lambda i: (i, 0))],
        out_specs=[pl.BlockSpec((tm, 2 * D_BRANCH), lambda i: (i, 0)),
                   pl.BlockSpec((N_HEADS, SG_CHUNK, SG_CHUNK), lambda i: (0, 0, 0)),
                   pl.BlockSpec((N_HEADS, SG_CHUNK, 1), lambda i: (0, 0, 0)),
                   pl.BlockSpec((1, D_BRANCH), lambda i: (0, 0))],
        out_shape=[jax.ShapeDtypeStruct((s, 2 * D_BRANCH), F32),
                   jax.ShapeDtypeStruct((N_HEADS, SG_CHUNK, SG_CHUNK), F32),
                   jax.ShapeDtypeStruct((N_HEADS, SG_CHUNK, 1), F32),
                   jax.ShapeDtypeStruct((1, D_BRANCH), F32)],
        compiler_params=_params(("arbitrary",)),
    )(uv, gain, w_s, b_col, dy)


def _scan_mats(nrow):
    a = lax.broadcasted_iota(jnp.int32, (128, 128), 0)
    b = lax.broadcasted_iota(jnp.int32, (128, 128), 1)
    r = lax.broadcasted_iota(jnp.int32, (nrow, nrow), 0)
    c = lax.broadcasted_iota(jnp.int32, (nrow, nrow), 1)
    nb = nrow // N_HEADS
    same = (r // nb) == (c // nb)
    return a, b, r, c, same


def _exact_dot(x, m):
    hi, mid, lo = _split3(x)
    return _dot(hi, m) + _dot(mid, m) + _dot(lo, m)


def _exact_dot_left(m, x):
    hi, mid, lo = _split3(x)
    return _dot(m, hi) + _dot(m, mid) + _dot(m, lo)


def fox_gate_fwd(ft, bcol):
    nrow = ft.shape[0]

    def body(f_ref, b_ref, c_ref):
        z = f_ref[...] + b_ref[...]
        ls = jnp.minimum(z, 0.0) - jnp.log(1.0 + jnp.exp(-jnp.abs(z)))
        a, b, r, c, same = _scan_mats(nrow)
        within = _exact_dot(ls, (a <= b).astype(BF16))
        tot = jnp.broadcast_to(within[:, 127:128], within.shape)
        before = _exact_dot_left((same & (c < r)).astype(BF16), tot)
        c_ref[...] = within + before

    return pl.pallas_call(
        body, name="fox_gate_fwd",
        out_shape=jax.ShapeDtypeStruct((nrow, 128), F32),
        compiler_params=_params(),
    )(ft, bcol)


def fox_gate_bwd(ft, bcol, dc):
    nrow = ft.shape[0]

    def body(f_ref, b_ref, dc_ref, df_ref, db_ref):
        a, b, r, c, same = _scan_mats(nrow)
        dcv = dc_ref[...]
        within = _exact_dot(dcv, (a >= b).astype(BF16))
        tot = jnp.broadcast_to(within[:, 0:1], within.shape)
        after = _exact_dot_left((same & (c > r)).astype(BF16), tot)
        dls = within + after
        z = f_ref[...] + b_ref[...]
        dz = dls * _sigmoid(-z)
        df_ref[...] = dz
        rs = jnp.broadcast_to(jnp.sum(dz, axis=-1, keepdims=True), dz.shape)
        hr = lax.broadcasted_iota(jnp.int32, (8, nrow), 0)
        hc = lax.broadcasted_iota(jnp.int32, (8, nrow), 1)
        db_ref[...] = _exact_dot_left((hr == hc // (nrow // N_HEADS)).astype(BF16), rs)

    return pl.pallas_call(
        body, name="fox_gate_bwd",
        out_shape=[jax.ShapeDtypeStruct((nrow, 128), F32), jax.ShapeDtypeStruct((8, 128), F32)],
        compiler_params=_params(),
    )(ft, bcol, dc)


def _att_specs(s, qi, ki, vi):
    q_spec = pl.BlockSpec((None, None, FOX_TQ, HEAD_DIM), lambda h, i: (qi, h, i, 0))
    k_spec = pl.BlockSpec((None, None, s, HEAD_DIM), lambda h, i: (ki, h, 0, 0))
    v_spec = pl.BlockSpec((None, None, s, HEAD_DIM), lambda h, i: (vi, h, 0, 0))
    row_spec = lambda w: pl.BlockSpec((None, FOX_TQ, w), lambda h, i: (h, i, 0))
    gate_spec = pl.BlockSpec((None, s // ATT_T, 1, ATT_T), lambda h, i: (h, 0, 0, 0))
    return q_spec, k_spec, v_spec, row_spec, gate_spec


def _causal(strict, n=ATT_T):
    row = lax.broadcasted_iota(jnp.int32, (n, n), 0)
    col = lax.broadcasted_iota(jnp.int32, (n, n), 1)
    return (col < row) if strict else (col <= row)


def _gate_row(cr_ref, kb, g):
    if g == 1:
        return cr_ref[kb]
    return jnp.concatenate([cr_ref[kb + n] for n in range(g)], axis=1)


def _fox_walk(i, carry, tile, alive):
    g = FOX_WIDE
    own = FOX_TQ // ATT_T
    nwide = (own * i) // g
    carry = tile(own * i, own, carry, True)
    carry = lax.fori_loop(0, (own * i - nwide * g) // own, lambda n, c: tile(nwide * g, own, c, False), carry)

    def cond(state):
        return jnp.logical_and(state[0] >= 0, state[1] > 0)

    def step(state):
        n = state[0]
        c = tile(n * g, g, state[2:], False)
        return (n - 1, alive(n * g, c)) + tuple(c)

    out = lax.while_loop(cond, step, (nwide - 1, alive(nwide * g, carry)) + tuple(carry))
    return out[2:]


def _fox_reach(qs, k_ref, kmax_ref, cc, i):
    s = k_ref.shape[0]
    rows = 4 * ATT_T

    @pl.when(i == 0)
    def _():
        def chunk(n, mx):
            kc = k_ref[pl.ds(pl.multiple_of(n * rows, rows), rows), :].astype(F32)
            return jnp.maximum(mx, jnp.max(jnp.sum(kc * kc, axis=-1, keepdims=True)))

        kmax_ref[0] = jnp.sqrt(lax.fori_loop(0, s // rows, chunk, jnp.float32(0.0)))

    qf = qs.astype(F32)
    return jnp.sqrt(jnp.sum(qf * qf, axis=-1, keepdims=True)) * kmax_ref[0] + cc


def _gate_col(cr_ref, i):
    row = lax.broadcasted_iota(jnp.int32, (ATT_T, ATT_T), 0)
    col = lax.broadcasted_iota(jnp.int32, (ATT_T, ATT_T), 1)
    own = FOX_TQ // ATT_T
    return jnp.concatenate([jnp.sum(jnp.where(row == col, cr_ref[own * i + n], 0.0), axis=-1, keepdims=True)
                            for n in range(own)], axis=0)


def _fox_scores(qs, k, cc, crow, masked):
    sc = (_dot_nt(qs, k) + (cc - crow)) * LOG2E
    if masked:
        sc = jnp.where(_causal(False, FOX_TQ), sc, NEG)
    return sc


def fox_fwd(qkv, c_row, ride=()):
    s = qkv.shape[2]
    t = ATT_T
    nq = s // FOX_TQ
    q_spec, k_spec, v_spec, row_spec, gate_spec = _att_specs(s, 1, 2, 3)
    rows = 4 * t
    nride = len(ride)

    def body(q_ref, k_ref, v_ref, cr_ref, *refs):
        ride_in, refs = refs[:nride], refs[nride:]
        o_ref, ref_ref, rl_ref = refs[:3]
        ride_out, refs = refs[3:3 + nride], refs[3 + nride:]
        v1_ref, kmax_ref = refs[:2]
        i = pl.program_id(1)
        if nride:
            h = pl.program_id(0)
            start, wait = _chip_gather([(src, lambda slot, dst=dst: dst.at[slot]) for src, dst in zip(ride_in, ride_out)],
                                       *refs[2:])
            pl.when(jnp.logical_and(h == 0, i == 0))(start)

        @pl.when(i == 0)
        def _():
            def chunk(n, carry):
                r0 = pl.multiple_of(n * rows, rows)
                v1_ref[pl.ds(r0, rows), :] = jnp.concatenate(
                    [v_ref[pl.ds(r0, rows), :], jnp.ones((rows, HEAD_DIM), BF16)], axis=1)
                return carry

            lax.fori_loop(0, s // rows, chunk, 0)

        qs = q_ref[...] * 0.125
        cc = _gate_col(cr_ref, i)
        reach = _fox_reach(qs, k_ref, kmax_ref, cc, i) * LOG2E

        def alive(kb, carry):
            return (jnp.max(reach - cr_ref[kb][:, 0:1] * LOG2E - carry[0]) > FOX_DEAD2).astype(jnp.int32)

        def tile(kb, g, carry, masked):
            m, acc = carry
            k0 = pl.multiple_of(kb * t, t)
            sc = _fox_scores(qs, k_ref[pl.ds(k0, g * t), :], cc, _gate_row(cr_ref, kb, g), masked)
            m_new = jnp.maximum(m, jnp.ceil(jnp.max(sc, axis=-1, keepdims=True)))
            pb = jnp.exp2(sc - m_new).astype(BF16)
            acc = jnp.exp2(m - m_new) * acc + _dot(pb, v1_ref[pl.ds(k0, g * t), :])
            return m_new, acc

        init = (jnp.full((FOX_TQ, 1), NEG, F32), jnp.zeros((FOX_TQ, 2 * HEAD_DIM), F32))
        m, acc = _fox_walk(i, init, tile, alive)
        rl = 1.0 / acc[:, HEAD_DIM:HEAD_DIM + 1]
        o_ref[...] = acc[:, 0:HEAD_DIM] * rl
        ref_ref[...] = m
        rl_ref[...] = rl
        if nride:
            pl.when(jnp.logical_and(h == N_HEADS - 1, i == nq - 1))(wait)

    any_spec = pl.BlockSpec(memory_space=pl.ANY)
    ride_sems = [pltpu.SemaphoreType.DMA((3 * nride,)), pltpu.SemaphoreType.DMA((3 * nride,)),
                 pltpu.SemaphoreType.DMA((nride,))] if nride else []
    return pl.pallas_call(
        body, name="fox_fwd_gather" if nride else "fox_fwd", grid=(N_HEADS, nq),
        in_specs=[q_spec, k_spec, v_spec, gate_spec] + [any_spec] * nride,
        out_specs=[row_spec(HEAD_DIM), row_spec(1), row_spec(1)] + [any_spec] * nride,
        out_shape=[jax.ShapeDtypeStruct((N_HEADS, s, HEAD_DIM), F32),
                   jax.ShapeDtypeStruct((N_HEADS, s, 1), F32),
                   jax.ShapeDtypeStruct((N_HEADS, s, 1), F32)]
        + [jax.ShapeDtypeStruct((4,) + a.shape, a.dtype) for a in ride],
        scratch_shapes=[pltpu.VMEM((s, 2 * HEAD_DIM), BF16), pltpu.SMEM((1,), F32)] + ride_sems,
        compiler_params=_params(("arbitrary", "arbitrary")),
    )(qkv, qkv, qkv, c_row, *ride)


def fox_bwd(qkv, c_row, do, o, ref, rl):
    s = qkv.shape[2]
    t = ATT_T
    nq = s // FOX_TQ
    q_spec, k_spec, v_spec, row_spec, gate_spec = _att_specs(s, 1, 2, 3)
    any_spec = pl.BlockSpec(memory_space=pl.ANY)

    def body(q_ref, k_ref, v_ref, cr_ref, do_ref, o_ref, ref_ref, rl_ref,
             dq_ref, dk_hbm, dv_hbm, dc_ref, dk_acc, dv_acc, kmax_ref):
        h = pl.program_id(0)
        i = pl.program_id(1)

        @pl.when(i == 0)
        def _():
            dk_acc[...] = jnp.zeros_like(dk_acc)
            dv_acc[...] = jnp.zeros_like(dv_acc)
            dc_ref[...] = jnp.zeros_like(dc_ref)

        qs = q_ref[...] * 0.125
        ref = ref_ref[...]
        rl = rl_ref[...]
        dob = (do_ref[...].astype(F32) * rl).astype(BF16)
        delta = jnp.sum(o_ref[...] * dob.astype(F32), axis=-1, keepdims=True)
        cc = _gate_col(cr_ref, i)
        margin = _fox_reach(qs, k_ref, kmax_ref, cc, i) * LOG2E - ref

        def alive(kb, carry):
            return (jnp.max(margin - cr_ref[kb][:, 0:1] * LOG2E) > FOX_DEAD2).astype(jnp.int32)

        def tile(kb, g, carry, masked):
            dq, = carry
            k0 = pl.multiple_of(kb * t, t)
            k = k_ref[pl.ds(k0, g * t), :]
            sc = _fox_scores(qs, k, cc, _gate_row(cr_ref, kb, g), masked)
            wb = jnp.exp2(sc - ref).astype(BF16)
            ds = wb.astype(F32) * (_dot_nt(dob, v_ref[pl.ds(k0, g * t), :]) - delta)
            dsb = ds.astype(BF16)
            dk_acc[pl.ds(k0, g * t), :] += _dot_tn(dsb, qs)
            dv_acc[pl.ds(k0, g * t), :] += _dot_tn(wb, dob)
            dcs = -jnp.sum(ds, axis=0, keepdims=True)
            for n in range(g):
                dc_ref[kb + n] += dcs[:, n * t:(n + 1) * t]
            return (dq + _dot(dsb, k),)

        dq, = _fox_walk(i, (jnp.zeros((FOX_TQ, HEAD_DIM), F32),), tile, alive)
        dq_ref[...] = dq * 0.125

        @pl.when(i == nq - 1)
        def _():
            pltpu.sync_copy(dk_acc, dk_hbm.at[h])
            pltpu.sync_copy(dv_acc, dv_hbm.at[h])

    return pl.pallas_call(
        body, name="fox_bwd", grid=(N_HEADS, nq),
        in_specs=[q_spec, k_spec, v_spec,
                  gate_spec,
                  row_spec(HEAD_DIM), row_spec(HEAD_DIM), row_spec(1), row_spec(1)],
        out_specs=[row_spec(HEAD_DIM), any_spec, any_spec,
                   gate_spec],
        out_shape=[jax.ShapeDtypeStruct((N_HEADS, s, HEAD_DIM), F32),
                   jax.ShapeDtypeStruct((N_HEADS, s, HEAD_DIM), F32),
                   jax.ShapeDtypeStruct((N_HEADS, s, HEAD_DIM), F32),
                   jax.ShapeDtypeStruct((N_HEADS, s // t, 1, t), F32)],
        scratch_shapes=[pltpu.VMEM((s, HEAD_DIM), F32), pltpu.VMEM((s, HEAD_DIM), F32), pltpu.SMEM((1,), F32)],
        compiler_params=_params(("arbitrary", "arbitrary")),
    )(qkv, qkv, qkv, c_row, do, o, ref, rl)


def _sb_valid(nrows, ahead):
    row = lax.broadcasted_iota(jnp.int32, (nrows, ATT_T), 0)
    col = lax.broadcasted_iota(jnp.int32, (nrows, ATT_T), 1)
    return col + ahead < row


def _sb_band_valid(nsub):
    row = lax.broadcasted_iota(jnp.int32, (nsub * SB_SUB, SB_BAND), 0)
    col = lax.broadcasted_iota(jnp.int32, (nsub * SB_SUB, SB_BAND), 1)
    return col < (row & (SB_SUB - 1)) + SB_BACK


def _sb_logits(qs, k):
    z = _dot_nt(qs, k)
    sp = jnp.log(1.0 + jnp.exp(-jnp.abs(z)))
    return jnp.minimum(z, 0.0) - sp, -jnp.maximum(z, 0.0) - sp


def _sb_weights(ls, lm, run, valid):
    if valid is not None:
        lm = jnp.where(valid, lm, 0.0)
    n = lm.shape[1]
    row = lax.broadcasted_iota(jnp.int32, (n, n), 0)
    col = lax.broadcasted_iota(jnp.int32, (n, n), 1)
    later = (row > col).astype(BF16)
    hi, lo = _split2(lm)
    between = _dot(hi, later) + _dot(lo, later)
    if run is not None:
        between = run + between
    a = jnp.exp(ls + between)
    if valid is not None:
        a = jnp.where(valid, a, 0.0)
    return lm, a


def _sb_band_start(i, j):
    return pl.multiple_of(i * 2 * ATT_T + j * SB_SUB - SB_BACK, SB_SUB)


def _sb_tile(qs, k, run, valid):
    ls, lm = _sb_logits(qs, k)
    lm, a = _sb_weights(ls, lm, run, valid)
    return ls, lm, a


def _sb_band(i, qs_all, k_ref):
    nsub = qs_all.shape[0] // SB_SUB
    valid = _sb_band_valid(nsub)
    starts = [_sb_band_start(i, j) for j in range(nsub)]
    kwins = [k_ref[pl.ds(k0, SB_BAND), :] for k0 in starts]
    parts = [_sb_logits(qs_all[j * SB_SUB:(j + 1) * SB_SUB], kwins[j]) for j in range(nsub)]
    ls = jnp.concatenate([p[0] for p in parts], axis=0)
    lm, a = _sb_weights(ls, jnp.concatenate([p[1] for p in parts], axis=0), None, valid)
    return starts, kwins, ls, lm, a, valid


def _sb_suffix(g, run_g):
    n = g.shape[1]
    row = lax.broadcasted_iota(jnp.int32, (n, n), 0)
    col = lax.broadcasted_iota(jnp.int32, (n, n), 1)
    from_here = (row >= col).astype(BF16)
    hi, lo = _split2(g)
    out = _dot(hi, from_here) + _dot(lo, from_here)
    return out if run_g is None else run_g + out


def _sb_walk(i, carry, tile):
    def alive_of(c):
        return (jnp.max(c[0]) > SB_DEAD).astype(jnp.int32)

    def cond(state):
        n, alive = state[0], state[1]
        return jnp.logical_and(n < i, alive > 0)

    def step(state):
        n = state[0]
        c = tile(i - 1 - n, state[2:], False)
        return (n + 1, alive_of(c)) + tuple(c)

    out = lax.while_loop(cond, step, (jnp.int32(0), alive_of(carry)) + tuple(carry))
    return out[2:]


def _sb_specs(s):
    tq = 2 * ATT_T
    q_spec = pl.BlockSpec((None, None, tq, HEAD_DIM), lambda h, i: (4, h, i, 0))
    k_spec = pl.BlockSpec((None, None, s, HEAD_DIM), lambda h, i: (5, h, 0, 0))
    v_spec = pl.BlockSpec((None, None, s, HEAD_DIM), lambda h, i: (6, h, 0, 0))
    row_spec = pl.BlockSpec((None, tq, HEAD_DIM), lambda h, i: (h, i, 0))
    band_spec = pl.BlockSpec((None, None, 1, 128), lambda h, i: (h, i, 0, 0))
    return tq, q_spec, k_spec, v_spec, row_spec, band_spec


def _sb_block(i, tile, zero):
    t = ATT_T
    lo, hi, both = slice(0, t), slice(t, 2 * t), slice(0, 2 * t)
    c_hi = tile(2 * i + 1, hi, zero, 0)
    c_lo = tile(2 * i, lo, zero, 0)
    c_hi = tile(2 * i, hi, c_hi, None)
    carry = tuple(jnp.concatenate([a, b], axis=0) for a, b in zip(c_lo, c_hi))
    return _sb_walk(2 * i, carry, lambda kb, c, _: tile(kb, both, c, None))


def sb_fwd(qkv):
    s = qkv.shape[2]
    t = ATT_T
    tq, q_spec, k_spec, v_spec, row_spec, band_spec = _sb_specs(s)

    def body(q_ref, k_ref, v_ref, o_ref, band_ref, done_ref):
        i = pl.program_id(1)
        qs = q_ref[...] * 0.125
        done_ref[0] = 0

        @pl.when(i > 0)
        def _():
            starts, _, _, lm, a, _ = _sb_band(i, qs, k_ref)
            ab = a.astype(BF16)
            for j, k0 in enumerate(starts):
                rows = slice(j * SB_SUB, (j + 1) * SB_SUB)
                o_ref[rows, :] = _dot(ab[rows], v_ref[pl.ds(k0, SB_BAND), :])
            worst = jnp.max(jnp.sum(lm, axis=-1, keepdims=True))
            done_ref[0] = (worst <= SB_DEAD).astype(jnp.int32)

        @pl.when(done_ref[0] == 0)
        def _():
            def tile(kb, rows, carry, ahead):
                run, acc = carry
                k0 = pl.multiple_of(kb * t, t)
                valid = None if ahead is None else _sb_valid(t, ahead)
                _, lm, a = _sb_tile(qs[rows], k_ref[pl.ds(k0, t), :], run, valid)
                acc = acc + _dot(a.astype(BF16), v_ref[pl.ds(k0, t), :])
                return run + jnp.sum(lm, axis=-1, keepdims=True), acc

            _, acc = _sb_block(i, tile, (jnp.zeros((t, 1), F32), jnp.zeros((t, HEAD_DIM), F32)))
            o_ref[...] = acc

        band_ref[...] = jnp.full(band_ref.shape, done_ref[0], jnp.int32).astype(F32)

    return pl.pallas_call(
        body, name="sb_fwd", grid=(N_HEADS, s // tq),
        in_specs=[q_spec, k_spec, v_spec],
        out_specs=[row_spec, band_spec],
        out_shape=[jax.ShapeDtypeStruct((N_HEADS, s, HEAD_DIM), F32),
                   jax.ShapeDtypeStruct((N_HEADS, s // tq, 1, 128), F32)],
        scratch_shapes=[pltpu.SMEM((1,), jnp.int32)],
        compiler_params=_params(("arbitrary", "arbitrary")),
    )(qkv, qkv, qkv)


def sb_bwd(qkv, do, o, band):
    s = qkv.shape[2]
    t = ATT_T
    tq, q_spec, k_spec, v_spec, row_spec, band_spec = _sb_specs(s)
    nq = s // tq
    any_spec = pl.BlockSpec(memory_space=pl.ANY)

    def body(q_ref, k_ref, v_ref, do_ref, o_ref, band_ref, dq_ref, dk_hbm, dv_hbm, dk_acc, dv_acc):
        h = pl.program_id(0)
        i = pl.program_id(1)

        @pl.when(i == 0)
        def _():
            dk_acc[...] = jnp.zeros_like(dk_acc)
            dv_acc[...] = jnp.zeros_like(dv_acc)

        qs_all = q_ref[...] * 0.125
        dob_all = do_ref[...]
        tot_all = jnp.sum(o_ref[...] * dob_all.astype(F32), axis=-1, keepdims=True)
        on_band = jnp.max(band_ref[...]) > 0.5

        def grads(qs, dob, tot, k, v, k0, run, run_g, valid):
            ls, lm, a = _sb_tile(qs, k, run, valid)
            ab = a.astype(BF16)
            g = ab.astype(F32) * _dot_nt(dob, v)
            g_left = tot - _sb_suffix(g, run_g)
            dz = g - jnp.exp(ls) * (g + g_left)
            if valid is not None:
                dz = jnp.where(valid, dz, 0.0)
            dzb = dz.astype(BF16)
            n = k.shape[0]
            dk_acc[pl.ds(k0, n), :] += _dot_tn(dzb, qs)
            dv_acc[pl.ds(k0, n), :] += _dot_tn(ab, dob)
            return dzb, lm, g

        @pl.when(on_band)
        def _():
            starts, kwins, ls, _, a, valid = _sb_band(i, qs_all, k_ref)
            ab = a.astype(BF16)
            subs = [slice(j * SB_SUB, (j + 1) * SB_SUB) for j in range(len(starts))]
            vwins = [v_ref[pl.ds(k0, SB_BAND), :] for k0 in starts]
            g = ab.astype(F32) * jnp.concatenate([_dot_nt(dob_all[r], v) for r, v in zip(subs, vwins)], axis=0)
            dz = jnp.where(valid, g - jnp.exp(ls) * (g + (tot_all - _sb_suffix(g, None))), 0.0)
            dzb = dz.astype(BF16)
            for r, k0, k in zip(subs, starts, kwins):
                dq_ref[r, :] = _dot(dzb[r], k) * 0.125
                dk_acc[pl.ds(k0, SB_BAND), :] += _dot_tn(dzb[r], qs_all[r])
                dv_acc[pl.ds(k0, SB_BAND), :] += _dot_tn(ab[r], dob_all[r])

        @pl.when(jnp.logical_not(on_band))
        def _():
            def tile(kb, rows, carry, ahead):
                run, run_g, dq = carry
                k0 = pl.multiple_of(kb * t, t)
                k = k_ref[pl.ds(k0, t), :]
                valid = None if ahead is None else _sb_valid(t, ahead)
                dzb, lm, g = grads(qs_all[rows], dob_all[rows], tot_all[rows], k, v_ref[pl.ds(k0, t), :], k0,
                                   run, run_g, valid)
                return (run + jnp.sum(lm, axis=-1, keepdims=True),
                        run_g + jnp.sum(g, axis=-1, keepdims=True),
                        dq + _dot(dzb, k))

            zero = jnp.zeros((t, 1), F32)
            _, _, dq = _sb_block(i, tile, (zero, zero, jnp.zeros((t, HEAD_DIM), F32)))
            dq_ref[...] = dq * 0.125

        @pl.when(i == nq - 1)
        def _():
            pltpu.sync_copy(dk_acc, dk_hbm.at[h])
            pltpu.sync_copy(dv_acc, dv_hbm.at[h])

    return pl.pallas_call(
        body, name="sb_bwd", grid=(N_HEADS, nq),
        in_specs=[q_spec, k_spec, v_spec, row_spec, row_spec, band_spec],
        out_specs=[row_spec, any_spec, any_spec],
        out_shape=[jax.ShapeDtypeStruct((N_HEADS, s, HEAD_DIM), F32)] * 3,
        scratch_shapes=[pltpu.VMEM((s, HEAD_DIM), F32), pltpu.VMEM((s, HEAD_DIM), F32)],
        compiler_params=_params(("arbitrary", "arbitrary")),
    )(qkv, qkv, qkv, do, o, band)


def _branch_inputs(refs, br):
    ya_ref, yb_ref, yc_ref, yd_ref = refs
    if br == 1:
        return yb_ref[...]
    return _heads_to_lanes((ya_ref, None, yc_ref, yd_ref)[br])


def outproj_fwd(x, ya, yb, yc, yd, gates, bg, wout):
    s = x.shape[0]
    tm = min(ROW_T, s)

    def body(x_ref, ya_ref, yb_ref, yc_ref, yd_ref, gates_ref, bg_ref, w_ref, out_ref):
        pieces = []
        for br in range(4):
            cols = slice(br * D_BRANCH, (br + 1) * D_BRANCH)
            y = _branch_inputs((ya_ref, yb_ref, yc_ref, yd_ref), br)
            r = lax.rsqrt(jnp.mean(y * y, axis=-1, keepdims=True) + EPS)
            gt = gates_ref[:, cols]
            pieces.append((y * r * bg_ref[:, cols]) * (gt * _sigmoid(gt)))
        merged = jnp.concatenate(pieces, axis=1).astype(BF16)
        out_ref[...] = x_ref[...] + _dot(merged, w_ref[...])

    head_spec = pl.BlockSpec((N_HEADS, tm, HEAD_DIM), lambda i: (0, i, 0))
    return pl.pallas_call(
        body, name="outproj_fwd", grid=(s // tm,),
        in_specs=[pl.BlockSpec((tm, D_MODEL), lambda i: (i, 0)),
                  head_spec, pl.BlockSpec((tm, D_BRANCH), lambda i: (i, 0)), head_spec, head_spec,
                  pl.BlockSpec((tm, D_MODEL), lambda i: (i, 0)),
                  pl.BlockSpec((1, D_MODEL), lambda i: (0, 0)),
                  pl.BlockSpec((D_MODEL, D_MODEL), lambda i: (0, 0))],
        out_specs=pl.BlockSpec((tm, D_MODEL), lambda i: (i, 0)),
        out_shape=jax.ShapeDtypeStruct((s, D_MODEL), F32),
        compiler_params=_params(("arbitrary",)),
    )(x, ya, yb, yc, yd, gates, bg, wout)


def outproj_bwd(dout, ya, yb, yc, yd, gates, bg, wout):
    s = dout.shape[0]
    tm = min(ROW_T, s)

    def body(dout_ref, ya_ref, yb_ref, yc_ref, yd_ref, gates_ref, bg_ref, w_ref,
             dya_ref, dyb_ref, dyc_ref, dyd_ref, dgates_ref, dbg_ref, dw_ref):
        i = pl.program_id(0)

        @pl.when(i == 0)
        def _():
            dbg_ref[...] = jnp.zeros_like(dbg_ref)
            dw_ref[...] = jnp.zeros_like(dw_ref)

        doutb = dout_ref[...].astype(BF16)
        dmerged = _dot_nt(doutb, w_ref[...])
        pieces = []
        for br in range(4):
            cols = slice(br * D_BRANCH, (br + 1) * D_BRANCH)
            y = _branch_inputs((ya_ref, yb_ref, yc_ref, yd_ref), br)
            r = lax.rsqrt(jnp.mean(y * y, axis=-1, keepdims=True) + EPS)
            yn = y * r
            bgv = bg_ref[:, cols]
            gt = gates_ref[:, cols]
            sig = _sigmoid(gt)
            act = gt * sig
            n = yn * bgv
            pieces.append(n * act)
            dm = dmerged[:, cols]
            dn = dm * act
            dgates_ref[:, cols] = (dm * n * (sig * (1.0 + gt * (1.0 - sig)))).astype(BF16)
            dbg_ref[:, cols] += jnp.sum(dn * yn, axis=0, keepdims=True)
            u = dn * bgv
            dy = r * (u - yn * jnp.mean(yn * u, axis=-1, keepdims=True))
            if br == 1:
                dyb_ref[...] = dy
            else:
                dref = (dya_ref, None, dyc_ref, dyd_ref)[br]
                for hh in range(N_HEADS):
                    dref[hh] = dy[:, hh * HEAD_DIM:(hh + 1) * HEAD_DIM].astype(BF16)
        merged = jnp.concatenate(pieces, axis=1).astype(BF16)
        dw_ref[...] += _dot_tn(merged, doutb)

    head_spec = pl.BlockSpec((N_HEADS, tm, HEAD_DIM), lambda i: (0, i, 0))
    head_shape = jax.ShapeDtypeStruct((N_HEADS, s, HEAD_DIM), BF16)
    return pl.pallas_call(
        body, name="outproj_bwd", grid=(s // tm,),
        in_specs=[pl.BlockSpec((tm, D_MODEL), lambda i: (i, 0)),
                  head_spec, pl.BlockSpec((tm, D_BRANCH), lambda i: (i, 0)), head_spec, head_spec,
                  pl.BlockSpec((tm, D_MODEL), lambda i: (i, 0)),
                  pl.BlockSpec((1, D_MODEL), lambda i: (0, 0)),
                  pl.BlockSpec((D_MODEL, D_MODEL), lambda i: (0, 0))],
        out_specs=[head_spec, pl.BlockSpec((tm, D_BRANCH), lambda i: (i, 0)), head_spec, head_spec,
                   pl.BlockSpec((tm, D_MODEL), lambda i: (i, 0)),
                   pl.BlockSpec((1, D_MODEL), lambda i: (0, 0)),
                   pl.BlockSpec((D_MODEL, D_MODEL), lambda i: (0, 0))],
        out_shape=[head_shape, jax.ShapeDtypeStruct((s, D_BRANCH), F32), head_shape, head_shape,
                   jax.ShapeDtypeStruct((s, D_MODEL), BF16),
                   jax.ShapeDtypeStruct((1, D_MODEL), F32),
                   jax.ShapeDtypeStruct((D_MODEL, D_MODEL), F32)],
        compiler_params=_params(("arbitrary",)),
    )(dout, ya, yb, yc, yd, gates, bg, wout)


def final_loss(x, tgt, g):
    s = x.shape[0]
    tm = min(ROW_T, s)

    def body(x_ref, t_ref, g_ref, loss_ref, dx_ref, dg_ref):
        i = pl.program_id(0)

        @pl.when(i == 0)
        def _():
            loss_ref[...] = jnp.zeros_like(loss_ref)
            dg_ref[...] = jnp.zeros_like(dg_ref)

        xv = x_ref[...]
        gv = g_ref[...]
        r = lax.rsqrt(jnp.mean(xv * xv, axis=-1, keepdims=True) + EPS)
        xn = xv * r
        err = xn * gv - t_ref[...]
        loss_ref[...] += jnp.sum(err * err) * (0.5 / D_MODEL)
        dy = err * (1.0 / D_MODEL)
        u = dy * gv
        dx_ref[...] = r * (u - xn * jnp.mean(xn * u, axis=-1, keepdims=True))
        dg_ref[...] += jnp.sum(dy * xn, axis=0, keepdims=True)

    return pl.pallas_call(
        body, name="final_loss", grid=(s // tm,),
        in_specs=[pl.BlockSpec((tm, D_MODEL), lambda i: (i, 0)),
                  pl.BlockSpec((tm, D_MODEL), lambda i: (i, 0)),
                  pl.BlockSpec((1, D_MODEL), lambda i: (0, 0))],
        out_specs=[pl.BlockSpec((1, 128), lambda i: (0, 0)),
                   pl.BlockSpec((tm, D_MODEL), lambda i: (i, 0)),
                   pl.BlockSpec((1, D_MODEL), lambda i: (0, 0))],
        out_shape=[jax.ShapeDtypeStruct((1, 128), F32),
                   jax.ShapeDtypeStruct((s, D_MODEL), F32),
                   jax.ShapeDtypeStruct((1, D_MODEL), F32)],
        compiler_params=_params(("arbitrary",)),
    )(x, tgt, g)


def _rel_index():
    i = np.arange(A_TQ)[:, None]
    j = np.arange(A_BAND)[None, :]
    rel = np.clip(i - j + (A_BAND - A_TQ), -MAX_REL, MAX_REL) + MAX_REL
    dchunk = i // CHUNK + LOOKBACK - j // CHUNK
    valid = (dchunk >= 0) & (dchunk <= LOOKBACK)
    return jnp.asarray(np.where(valid, rel, -1).astype(np.int32))


def _layer_consts(p):
    tbias = relbias_tile(p["rel_bias"], _rel_index())
    return dict(
        norm_g=p["norm_g"].reshape(1, D_MODEL),
        v_gain=p["v_gain"].reshape(1, D_BRANCH),
        b_col=p["b_s"].reshape(N_HEADS, SG_CHUNK, 1),
        bg=p["branch_gain"].reshape(1, D_MODEL),
        tbias=tbias,
    )


def _gate_layout(fp, b_f, s):
    nb = s // 128
    ft = fp[:, :N_HEADS].T.reshape(N_HEADS * nb, 128)
    bcol = jnp.repeat(b_f, nb).reshape(N_HEADS * nb, 1)
    return ft, bcol


def layer_fwd(x, p, ride=()):
    s = x.shape[0]
    c = _layer_consts(p)
    h, qkv, kva, gates, uv, fp = inproj_fwd(x, c["norm_g"], p["wp"])
    ya, lse_a = mix_a_fwd(qkv, kva, c["tbias"])
    yb = mix_b_fwd(uv, c["v_gain"], p["w_s"], c["b_col"])
    ft, bcol = _gate_layout(fp, p["b_f"], s)
    c_row = fox_gate_fwd(ft, bcol).reshape(N_HEADS, s // ATT_T, 1, ATT_T)
    yc, ref_c, rl_c, *rode = fox_fwd(qkv, c_row, ride)
    yd, band_d = sb_fwd(qkv)
    out = outproj_fwd(x, ya, yb, yc, yd, gates, c["bg"], p["wout"])
    saved = dict(consts=c, x=x, h=h, qkv=qkv, gates=gates, uv=uv, kva=kva, ft=ft, bcol=bcol,
                 c_row=c_row, ya=ya, lse_a=lse_a, yb=yb, yc=yc, ref_c=ref_c, rl_c=rl_c, yd=yd, band_d=band_d)
    return out, saved, rode


def layer_bwd(dout, p, sv):
    s = dout.shape[0]
    c = sv["consts"]
    dya, dyb, dyc, dyd, dgates, dbg, dwout = outproj_bwd(
        dout, sv["ya"], sv["yb"], sv["yc"], sv["yd"], sv["gates"], c["bg"], p["wout"])
    dqa, dka, dva, dt = mix_a_bwd(sv["qkv"], sv["kva"], c["tbias"], dya, sv["ya"], sv["lse_a"])
    drel = relbias_grad(dt, _rel_index())[:N_HEADS, :2 * MAX_REL + 1]
    duv, dws, dbs, dvgain = mix_b_bwd(sv["uv"], c["v_gain"], p["w_s"], c["b_col"], dyb)
    dqc, dkc, dvc, dc = fox_bwd(sv["qkv"], sv["c_row"], dyc, sv["yc"], sv["ref_c"], sv["rl_c"])
    dft, dbf = fox_gate_bwd(sv["ft"], sv["bcol"], dc.reshape(N_HEADS * (s // 128), 128))
    dfp = jnp.pad(dft.reshape(N_HEADS, s).T, ((0, 0), (0, 128 - N_HEADS)))
    dqd, dkd, dvd = sb_bwd(sv["qkv"], dyd, sv["yd"], sv["band_d"])
    dp, dx, dnorm = inproj_bwd((dqa, dka, dva, dqc, dkc, dvc, dqd, dkd, dvd), dgates, duv, dfp,
                               p["wp"], sv["x"], c["norm_g"], dout)
    grads = dict(norm_g=dnorm.reshape(D_MODEL), w_in_shards=inproj_wgrad(sv["h"], dp), b_f=dbf[:N_HEADS, 0], rel_bias=drel,
                 w_s=dws, b_s=dbs.reshape(N_HEADS, SG_CHUNK), v_gain=dvgain.reshape(D_BRANCH),
                 branch_gain=dbg.reshape(4, D_BRANCH), wout=dwout)
    return dx, grads


def local_step(x, tgt, layers, final_g, next_shards=None):
    layers = list(layers)
    saved = []
    cur = x
    for l, p in enumerate(layers):
        ride = next_shards[l] if next_shards is not None and l + 1 < len(layers) else ()
        cur, sv, rode = layer_fwd(cur, p, ride)
        saved.append(sv)
        if ride:
            layers[l + 1] = dict(layers[l + 1], wp=pack_w_in(rode[0][None])[0], wout=rode[1].reshape(D_MODEL, D_MODEL))
    loss, dcur, dfinal = final_loss(cur, tgt, final_g.reshape(1, D_MODEL))
    grads = [None] * len(layers)
    for l in reversed(range(len(layers))):
        dcur, grads[l] = layer_bwd(dcur, layers[l], saved[l])
    return loss[0, 0], dcur, grads, dfinal.reshape(D_MODEL)


def _chip_gather(pairs, send_sems, recv_sems, loc_sems):
    x, y, c = lax.axis_index("x"), lax.axis_index("y"), lax.axis_index("c")
    me = 2 * x + y
    chips = [(1 - x, y), (x, 1 - y), (1 - x, 1 - y)]
    npair = len(pairs)

    def local():
        return [pltpu.make_async_copy(src, dst(me), loc_sems.at[n]) for n, (src, dst) in enumerate(pairs)]

    def remote(j, n, slot):
        src, dst = pairs[n]
        return pltpu.make_async_remote_copy(
            src_ref=src, dst_ref=dst(slot), send_sem=send_sems.at[npair * j + n], recv_sem=recv_sems.at[npair * j + n],
            device_id=(chips[j][0], chips[j][1], c), device_id_type=MESH)

    def start():
        for cp in local():
            cp.start()
        for j in range(3):
            for n in range(npair):
                remote(j, n, me).start()

    def wait():
        for j in range(3):
            for n in range(npair):
                remote(j, n, 2 * chips[j][0] + chips[j][1]).wait_recv()
        for j in range(3):
            for n in range(npair):
                remote(j, n, me).wait_send()
        for cp in local():
            cp.wait()

    return start, wait


def gather_weights(w_in, w_out, gains):
    depth = w_in.shape[0]

    def body(in_ref, out_ref, g_ref, oin_ref, oout_ref, og_ref, send_sems, recv_sems, loc_sems):
        pairs = [(in_ref, lambda s: oin_ref.at[:, s]), (out_ref, lambda s: oout_ref.at[:, s]), (g_ref, lambda s: og_ref.at[s])]
        start, wait = _chip_gather(pairs, send_sems, recv_sems, loc_sems)
        start()
        wait()

    any_spec = pl.BlockSpec(memory_space=pl.ANY)
    return pl.pallas_call(
        body, name="gather_weights",
        in_specs=[any_spec] * 3, out_specs=[any_spec] * 3,
        out_shape=[jax.ShapeDtypeStruct((depth, 4) + w_in.shape[1:], w_in.dtype),
                   jax.ShapeDtypeStruct((depth, 4) + w_out.shape[1:], w_out.dtype),
                   jax.ShapeDtypeStruct((4,) + gains.shape, gains.dtype)],
        scratch_shapes=[pltpu.SemaphoreType.DMA((9,)), pltpu.SemaphoreType.DMA((9,)), pltpu.SemaphoreType.DMA((3,))],
    )(w_in, w_out, gains)


def pack_w_in(shards):
    depth = shards.shape[0]
    tr = 256

    def body(s_ref, o_ref):
        full = jnp.concatenate([s_ref[n] for n in range(4)], axis=1)
        o_ref[...] = jnp.concatenate([full[:, :SEC_D_Q], full[:, SEC_D_Q + N_HEADS:], full[:, SEC_D_Q:SEC_D_Q + N_HEADS],
                                      jnp.zeros((tr, N_PACK - N_IN), BF16)], axis=1)

    return pl.pallas_call(
        body, name="pack_w_in", grid=(depth, D_MODEL // tr),
        in_specs=[pl.BlockSpec((None, 4, tr, N_SHARD), lambda l, r: (l, 0, r, 0))],
        out_specs=pl.BlockSpec((None, tr, N_PACK), lambda l, r: (l, r, 0)),
        out_shape=jax.ShapeDtypeStruct((depth, D_MODEL, N_PACK), BF16),
        compiler_params=_params(("arbitrary", "arbitrary")),
    )(shards)


def exchange_grads(d_w_in, d_w_out, d_gain, small):
    depth = d_w_in.shape[0]

    def body(in_ref, out_ref, gain_ref, small_ref, rin_ref, rout_ref, rgain_ref, rsmall_ref,
             send_sems, recv_sems, loc_sems):
        x, y, c = lax.axis_index("x"), lax.axis_index("y"), lax.axis_index("c")
        me_chip = 2 * x + y
        me = 4 * x + 2 * y + c
        peers = [(x, y, 1 - c)]
        for px, py in [(1 - x, y), (x, 1 - y), (1 - x, 1 - y)]:
            peers += [(px, py, c), (px, py, 1 - c)]
        flows = [(lambda s: in_ref.at[:, s], lambda d: rin_ref.at[d]),
                 (lambda s: out_ref.at[:, s], lambda d: rout_ref.at[d]),
                 (lambda s: gain_ref.at[s], lambda d: rgain_ref.at[d]),
                 (lambda s: small_ref, lambda d: rsmall_ref.at[d])]
        nflow = len(flows)
        local = [pltpu.make_async_copy(src(me_chip), dst(me), loc_sems.at[f]) for f, (src, dst) in enumerate(flows)]
        for cp in local:
            cp.start()

        def copies(n, chip, slot):
            return [pltpu.make_async_remote_copy(src_ref=src(chip), dst_ref=dst(slot), send_sem=send_sems.at[nflow * n + f],
                                                 recv_sem=recv_sems.at[nflow * n + f], device_id=peers[n], device_id_type=MESH)
                    for f, (src, dst) in enumerate(flows)]

        sends = [cp for n, (px, py, _) in enumerate(peers) for cp in copies(n, 2 * px + py, me)]
        for cp in sends:
            cp.start()
        for n, (px, py, pc) in enumerate(peers):
            for cp in copies(n, me_chip, 4 * px + 2 * py + pc):
                cp.wait_recv()
        for cp in sends:
            cp.wait_send()
        for cp in local:
            cp.wait()

    any_spec = pl.BlockSpec(memory_space=pl.ANY)
    return pl.pallas_call(
        body, name="exchange_grads",
        in_specs=[any_spec] * 4, out_specs=[any_spec] * 4,
        out_shape=[jax.ShapeDtypeStruct((8, depth) + d_w_in.shape[2:], d_w_in.dtype),
                   jax.ShapeDtypeStruct((8, depth) + d_w_out.shape[2:], d_w_out.dtype),
                   jax.ShapeDtypeStruct((8,) + d_gain.shape[1:], d_gain.dtype),
                   jax.ShapeDtypeStruct((8,) + small.shape, small.dtype)],
        scratch_shapes=[pltpu.SemaphoreType.DMA((28,)), pltpu.SemaphoreType.DMA((28,)), pltpu.SemaphoreType.DMA((4,))],
    )(d_w_in, d_w_out, d_gain, small)


def adamw_reduce(parts, w, m, v, name, tr):
    rows, width = w.shape
    c1 = 1.0 - ADAM_B1 ** ADAM_STEP
    c2 = 1.0 - ADAM_B2 ** ADAM_STEP

    def body(p_ref, w_ref, m_ref, v_ref, g_ref, d_ref, nm_ref, nv_ref):
        g = p_ref[0].astype(F32)
        for n in range(1, 8):
            g = g + p_ref[n].astype(F32)
        g_ref[...] = g
        nm = ADAM_B1 * m_ref[...] + (1.0 - ADAM_B1) * g
        nv = ADAM_B2 * v_ref[...] + (1.0 - ADAM_B2) * (g * g)
        nm_ref[...] = nm
        nv_ref[...] = nv
        d_ref[...] = -ADAM_LR * ((nm / c1) / (jnp.sqrt(nv / c2) + ADAM_EPS) + ADAM_WD * w_ref[...])

    spec = pl.BlockSpec((tr, width), lambda i: (i, 0))
    shape = jax.ShapeDtypeStruct((rows, width), F32)
    return pl.pallas_call(
        body, name=name, grid=(rows // tr,),
        in_specs=[pl.BlockSpec((8, tr, width), lambda i: (0, i, 0)), spec, spec, spec],
        out_specs=[spec] * 4, out_shape=[shape] * 4,
        compiler_params=_params(("arbitrary",)),
    )(parts, w, m, v)


SMALL =("norm_g", "b_f", "rel_bias", "w_s", "b_s", "v_gain", "final_g")
WEIGHTS = ("norm_g", "w_in", "b_f", "rel_bias", "w_s", "b_s", "v_gain", "branch_gain", "w_out", "final_g")
PACK_ROW_TILE = 512


def _rows_of(shape):
    return -(-int(np.prod(shape)) // 128)


def _pack(leaves):
    parts = []
    for a in leaves:
        flat = a.reshape(-1).astype(F32)
        parts.append(jnp.pad(flat, (0, _rows_of(a.shape) * 128 - flat.shape[0])))
    flat = jnp.concatenate(parts)
    rows = flat.shape[0] // 128
    total = -(-rows // PACK_ROW_TILE) * PACK_ROW_TILE
    return jnp.pad(flat, (0, (total - rows) * 128)).reshape(total, 128)


def _unpack(slab, shapes):
    out, row = [], 0
    for shp in shapes:
        n = int(np.prod(shp))
        r = _rows_of(shp)
        out.append(slab[row:row + r].reshape(-1)[:n].reshape(shp))
        row += r
    return out


def kernel(x, norm_g, w_in, b_f, rel_bias, w_s, b_s, v_gain, branch_gain, w_out, final_g, loss_target, m_norm_g, m_w_in, m_b_f, m_rel_bias, m_w_s, m_b_s, m_v_gain, m_branch_gain, m_w_out, m_final_g, v_norm_g, v_w_in, v_b_f, v_rel_bias, v_w_s, v_b_s, v_v_gain, v_branch_gain, v_w_out, v_final_g):
    depth = norm_g.shape[0]
    weights = dict(norm_g=norm_g, w_in=w_in, b_f=b_f, rel_bias=rel_bias, w_s=w_s, b_s=b_s, v_gain=v_gain,
                   branch_gain=branch_gain, w_out=w_out, final_g=final_g)
    mom1 = dict(norm_g=m_norm_g, w_in=m_w_in, b_f=m_b_f, rel_bias=m_rel_bias, w_s=m_w_s, b_s=m_b_s,
                v_gain=m_v_gain, branch_gain=m_branch_gain, w_out=m_w_out, final_g=m_final_g)
    mom2 = dict(norm_g=v_norm_g, w_in=v_w_in, b_f=v_b_f, rel_bias=v_rel_bias, w_s=v_w_s, b_s=v_b_s,
                v_gain=v_v_gain, branch_gain=v_branch_gain, w_out=v_w_out, final_g=v_final_g)

    wf = jnp.pad(branch_gain.reshape(-1), (0, 8 * 128 - branch_gain.size)).reshape(8, 128)
    w_in_b, w_out_b = w_in.astype(BF16), w_out.astype(BF16)
    w_in_shards, w_out_shards, gf = gather_weights(w_in_b[:1], w_out_b[:1], wf)
    bg_full = gf.reshape(4, -1)[:, :branch_gain.size].reshape((4,) + branch_gain.shape)
    bg_full = jnp.moveaxis(bg_full, 0, 2).reshape(depth, 4, D_BRANCH)

    layers = [dict(norm_g=norm_g[l], b_f=b_f[l], rel_bias=rel_bias[l], w_s=w_s[l],
                   b_s=b_s[l], v_gain=v_gain[l], branch_gain=bg_full[l]) for l in range(depth)]
    layers[0].update(wp=pack_w_in(w_in_shards)[0], wout=w_out_shards.reshape(D_MODEL, D_MODEL))
    next_shards = [(w_in_b[l + 1], w_out_b[l + 1]) for l in range(depth - 1)]

    loss_part, grad_x, lgrads, dfinal = local_step(x[0], loss_target[0], layers, final_g, next_shards)
    loss = lax.psum(loss_part, ("x", "y", "c"))

    stack = lambda k: jnp.stack([g[k] for g in lgrads])
    d_w_in = stack("w_in_shards")
    d_w_out = stack("wout").astype(BF16).reshape(depth, 4, D_BRANCH, D_MODEL)
    d_gain = jnp.moveaxis(stack("branch_gain").reshape(depth, 4, 4, HEAD_DIM), 2, 0).reshape(4, -1)
    d_gain = jnp.pad(d_gain, ((0, 0), (0, 8 * 128 - d_gain.shape[1]))).reshape(4, 8, 128)
    small = dict(norm_g=stack("norm_g"), b_f=stack("b_f"), rel_bias=stack("rel_bias"), w_s=stack("w_s"),
                 b_s=stack("b_s"), v_gain=stack("v_gain"), final_g=dfinal)
    parts_in, parts_out, parts_gain, parts_small = exchange_grads(d_w_in, d_w_out, d_gain, _pack([small[k] for k in SMALL]))

    outs = {}
    tags = ("grad", "delta", "new_m", "new_v")
    for k, parts, tr in (("w_in", parts_in, 256), ("w_out", parts_out, 256)):
        rows = depth * weights[k].shape[1]
        flat = lambda a: a.reshape(rows, a.shape[-1])
        res = adamw_reduce(parts.reshape(8, rows, parts.shape[-1]), flat(weights[k]), flat(mom1[k]), flat(mom2[k]),
                           "adamw_" + k, tr)
        for tag, a in zip(tags, res):
            outs[tag, k] = a.reshape(weights[k].shape)
    gain8 = lambda a: jnp.pad(a.reshape(-1), (0, 8 * 128 - a.size)).reshape(8, 128)
    res = adamw_reduce(parts_gain, gain8(branch_gain), gain8(m_branch_gain), gain8(v_branch_gain), "adamw_gain", 8)
    for tag, a in zip(tags, res):
        outs[tag, "branch_gain"] = a.reshape(-1)[:branch_gain.size].reshape(branch_gain.shape)
    pack_small = lambda d: _pack([d[k] for k in SMALL])
    res = adamw_reduce(parts_small, pack_small(weights), pack_small(mom1), pack_small(mom2), "adamw_small", PACK_ROW_TILE)
    for tag, slab in zip(tags, res):
        for k, a in zip(SMALL, _unpack(slab, [weights[k].shape for k in SMALL])):
            outs[tag, k] = a
    result = [loss, grad_x[None]]
    for tag in ("grad", "delta", "new_m", "new_v"):
        result += [outs[tag, k] for k in WEIGHTS]
    return tuple(result)
```

```python
import functools

import jax
import jax.numpy as jnp
import numpy as np
from jax import lax
from jax.experimental import pallas as pl
from jax.experimental.pallas import tpu as pltpu

F32 = jnp.float32
BF16 = jnp.bfloat16
MESH = pl.DeviceIdType.MESH

D_MODEL = 1024
D_BRANCH = 256
N_HEADS = 4
HEAD_DIM = 64
CHUNK = 64
LOOKBACK = 8
MAX_REL = 128
SG_CHUNK = 128
EPS = 1e-6
N_IN = 3844
N_PACK = 3968
F_COL = 3840
N_SHARD = 961
NEG = -1e30

A_TQ = 128
A_BAND = A_TQ + LOOKBACK * CHUNK
REL_LO = MAX_REL - (CHUNK - 1)
REL_HI = 2 * MAX_REL + 1
A_PAD = LOOKBACK * CHUNK
A_QB = 1024
ATT_T = 256
FOX_TQ = 512
FOX_WIDE = 4
FOX_DEAD2 = -160.0
LOG2E = 1.4426950408889634
SB_SUB = 128
SB_BACK = 256
SB_BAND = SB_SUB + SB_BACK
SB_DEAD = -110.0
ROW_T = 512
VMEM_LIMIT = 56 * 1024 * 1024

ADAM_LR = 0.001
ADAM_B1 = 0.9
ADAM_B2 = 0.999
ADAM_EPS = 1e-08
ADAM_WD = 0.01
ADAM_STEP = 10

SEC_A_Q, SEC_A_K, SEC_A_V, SEC_A_G = 0, 256, 512, 768
SEC_B_U, SEC_B_V, SEC_B_G = 1024, 1280, 1536
SEC_C_Q, SEC_C_K, SEC_C_V, SEC_C_G = 1792, 2048, 2304, 2560
SEC_D_Q, SEC_D_K, SEC_D_V, SEC_D_G = 2816, 3072, 3328, 3584
QKV_SECS = (SEC_A_Q, SEC_C_Q, SEC_C_K, SEC_C_V, SEC_D_Q, SEC_D_K, SEC_D_V)
GATE_SECS = (SEC_A_G, SEC_B_G, SEC_C_G, SEC_D_G)


def _dot(a, b):
    return jnp.dot(a, b, preferred_element_type=F32)


def _dot_nt(a, b):
    return lax.dot_general(a, b, (((1,), (1,)), ((), ())), preferred_element_type=F32)


def _dot_tn(a, b):
    return lax.dot_general(a, b, (((0,), (0,)), ((), ())), preferred_element_type=F32)


def _split2(x):
    hi = x.astype(BF16)
    lo = (x - hi.astype(F32)).astype(BF16)
    return hi, lo


def _split3(x):
    hi = x.astype(BF16)
    r = x - hi.astype(F32)
    mid = r.astype(BF16)
    lo = (r - mid.astype(F32)).astype(BF16)
    return hi, mid, lo


def _sigmoid(x):
    return 1.0 / (1.0 + jnp.exp(-x))


def _params(sem=None, vmem=VMEM_LIMIT):
    return pltpu.CompilerParams(dimension_semantics=sem, vmem_limit_bytes=vmem)


def _heads_to_lanes(ref):
    return jnp.concatenate([ref[h] for h in range(N_HEADS)], axis=1)


def inproj_fwd(x, g, wp):
    s = x.shape[0]
    tm = A_PAD

    def body(x_ref, g_ref, w_ref, h_ref, qkv_ref, kva_ref, gates_ref, uv_ref, f_ref):
        xv = x_ref[...]
        r = lax.rsqrt(jnp.mean(xv * xv, axis=-1, keepdims=True) + EPS)
        h = (xv * r * g_ref[...]).astype(BF16)
        h_ref[...] = h
        for n, off in enumerate(QKV_SECS):
            p = _dot(h, w_ref[:, off:off + D_BRANCH])
            for hh in range(N_HEADS):
                qkv_ref[n, hh] = p[:, hh * HEAD_DIM:(hh + 1) * HEAD_DIM].astype(BF16)
        for n, off in enumerate((SEC_A_K, SEC_A_V)):
            p = _dot(h, w_ref[:, off:off + D_BRANCH])
            for hh in range(N_HEADS):
                kva_ref[n, hh] = p[:, hh * HEAD_DIM:(hh + 1) * HEAD_DIM].astype(BF16)
        for n, off in enumerate(GATE_SECS):
            gates_ref[:, n * D_BRANCH:(n + 1) * D_BRANCH] = _dot(h, w_ref[:, off:off + D_BRANCH])
        uv_ref[...] = _dot(h, w_ref[:, SEC_B_U:SEC_B_U + 2 * D_BRANCH])
        f_ref[...] = _dot(h, w_ref[:, F_COL:F_COL + 128])

    return pl.pallas_call(
        body, name="inproj_fwd", grid=(s // tm,),
        in_specs=[pl.BlockSpec((tm, D_MODEL), lambda i: (i, 0)),
                  pl.BlockSpec((1, D_MODEL), lambda i: (0, 0)),
                  pl.BlockSpec((D_MODEL, N_PACK), lambda i: (0, 0))],
        out_specs=[pl.BlockSpec((tm, D_MODEL), lambda i: (i, 0)),
                   pl.BlockSpec((len(QKV_SECS), N_HEADS, tm, HEAD_DIM), lambda i: (0, 0, i, 0)),
                   pl.BlockSpec((2, N_HEADS, tm, HEAD_DIM), lambda i: (0, 0, i + 1, 0)),
                   pl.BlockSpec((tm, D_MODEL), lambda i: (i, 0)),
                   pl.BlockSpec((tm, 2 * D_BRANCH), lambda i: (i, 0)),
                   pl.BlockSpec((tm, 128), lambda i: (i, 0))],
        out_shape=[jax.ShapeDtypeStruct((s, D_MODEL), BF16),
                   jax.ShapeDtypeStruct((len(QKV_SECS), N_HEADS, s, HEAD_DIM), BF16),
                   jax.ShapeDtypeStruct((2, N_HEADS, s + tm, HEAD_DIM), BF16),
                   jax.ShapeDtypeStruct((s, D_MODEL), F32),
                   jax.ShapeDtypeStruct((s, 2 * D_BRANCH), F32),
                   jax.ShapeDtypeStruct((s, 128), F32)],
        compiler_params=_params(("arbitrary",)),
    )(x, g, wp)


def inproj_bwd(dqkv, dgates, duv, dfp, wp, x, g, dres):
    s = x.shape[0]
    tm = A_PAD

    def body(*refs):
        dq_refs = refs[:9]
        dgates_ref, duv_ref, dfp_ref, w_ref, x_ref, g_ref, dres_ref, dp_ref, dx_ref, dg_ref = refs[9:]
        i = pl.program_id(0)
        a_q, a_k, a_v, c_q, c_k, c_v, d_q, d_k, d_v = [_heads_to_lanes(r).astype(BF16) for r in dq_refs]
        dgt = dgates_ref[...]
        duv_b = duv_ref[...].astype(BF16)
        dp = jnp.concatenate(
            [a_q, a_k, a_v, dgt[:, 0:256], duv_b, dgt[:, 256:512], c_q, c_k, c_v, dgt[:, 512:768],
             d_q, d_k, d_v, dgt[:, 768:1024], dfp_ref[...].astype(BF16)], axis=1)
        dp_ref[...] = dp
        dh = _dot_nt(dp, w_ref[...])
        xv = x_ref[...]
        r = lax.rsqrt(jnp.mean(xv * xv, axis=-1, keepdims=True) + EPS)
        xn = xv * r
        u = dh * g_ref[...]
        dx_ref[...] = dres_ref[...] + r * (u - xn * jnp.mean(xn * u, axis=-1, keepdims=True))

        @pl.when(i == 0)
        def _():
            dg_ref[...] = jnp.zeros_like(dg_ref)

        dg_ref[...] += jnp.sum(dh * xn, axis=0, keepdims=True)

    head_spec = pl.BlockSpec((N_HEADS, tm, HEAD_DIM), lambda i: (0, i, 0))
    padded_spec = pl.BlockSpec((N_HEADS, tm, HEAD_DIM), lambda i: (0, i + 1, 0))
    return pl.pallas_call(
        body, name="inproj_bwd", grid=(s // tm,),
        in_specs=[head_spec, padded_spec, padded_spec] + [head_spec] * 6 + [
            pl.BlockSpec((tm, D_MODEL), lambda i: (i, 0)),
            pl.BlockSpec((tm, 2 * D_BRANCH), lambda i: (i, 0)),
            pl.BlockSpec((tm, 128), lambda i: (i, 0)),
            pl.BlockSpec((D_MODEL, N_PACK), lambda i: (0, 0)),
            pl.BlockSpec((tm, D_MODEL), lambda i: (i, 0)),
            pl.BlockSpec((1, D_MODEL), lambda i: (0, 0)),
            pl.BlockSpec((tm, D_MODEL), lambda i: (i, 0))],
        out_specs=[pl.BlockSpec((tm, N_PACK), lambda i: (i, 0)),
                   pl.BlockSpec((tm, D_MODEL), lambda i: (i, 0)),
                   pl.BlockSpec((1, D_MODEL), lambda i: (0, 0))],
        out_shape=[jax.ShapeDtypeStruct((s, N_PACK), BF16),
                   jax.ShapeDtypeStruct((s, D_MODEL), F32),
                   jax.ShapeDtypeStruct((1, D_MODEL), F32)],
        compiler_params=_params(("arbitrary",)),
    )(*dqkv, dgates, duv, dfp, wp, x, g, dres)


def inproj_wgrad(h, dp):
    s, m = h.shape
    tm = min(2 * ROW_T, s)
    tmm = 256
    nsteps = s // tm

    def body(a_ref, b_ref, o_ref, acc_ref):
        k = pl.program_id(1)

        @pl.when(k == 0)
        def _():
            acc_ref[...] = jnp.zeros_like(acc_ref)

        acc_ref[...] += _dot_tn(a_ref[...], b_ref[...])

        @pl.when(k == nsteps - 1)
        def _():
            acc = acc_ref[...]
            full = jnp.concatenate([acc[:, :SEC_D_Q], acc[:, F_COL:F_COL + N_HEADS], acc[:, SEC_D_Q:F_COL]], axis=1)
            for n in range(4):
                o_ref[n] = full[:, n * N_SHARD:(n + 1) * N_SHARD].astype(BF16)

    return pl.pallas_call(
        body, name="inproj_wgrad", grid=(m // tmm, nsteps),
        in_specs=[pl.BlockSpec((tm, tmm), lambda j, k: (k, j)),
                  pl.BlockSpec((tm, N_PACK), lambda j, k: (k, 0))],
        out_specs=pl.BlockSpec((4, tmm, N_SHARD), lambda j, k: (0, j, 0)),
        out_shape=jax.ShapeDtypeStruct((4, m, N_SHARD), BF16),
        scratch_shapes=[pltpu.VMEM((tmm, N_PACK), F32)],
        compiler_params=_params(("arbitrary", "arbitrary")),
    )(h, dp)


def _a_specs(s):
    nq = s // A_QB
    per = A_QB // A_PAD
    q_spec = pl.BlockSpec((None, None, A_QB, HEAD_DIM), lambda h, i: (0, h, jnp.minimum(i, nq - 1), 0))
    kv_specs = [pl.BlockSpec((None, None, A_PAD, HEAD_DIM),
                             lambda h, i, n=n, m=m: (n, h, jnp.minimum(per * i + m, per * nq), 0))
                for n in range(2) for m in range(per + 1)]
    t_spec = pl.BlockSpec((None, A_TQ, A_BAND), lambda h, i: (h, 0, 0))
    return nq, q_spec, kv_specs, t_spec


def _a_window(refs, i):
    first = refs[0][...]
    return jnp.concatenate([jnp.where(i > 0, first, jnp.zeros_like(first))] + [r[...] for r in refs[1:]], axis=0)


def _a_scores(q_ref, k, t_ref, i, j):
    rows = slice(j * A_TQ, (j + 1) * A_TQ)
    qs = q_ref[rows, :] * 0.125
    kj = k[j * A_TQ:j * A_TQ + A_BAND, :]
    sc = _dot_nt(qs, kj) + t_ref[...]
    col = lax.broadcasted_iota(jnp.int32, (A_TQ, A_BAND), 1)
    sc = jnp.where(col >= A_PAD - i * A_QB - j * A_TQ, sc, NEG)
    return rows, qs, kj, sc


def mix_a_fwd(qkv, kva, tbias):
    s = qkv.shape[2]
    nq, q_spec, kv_specs, t_spec = _a_specs(s)
    nwin = len(kv_specs) // 2

    def body(*refs):
        q_ref, t_ref, o_ref, lse_ref = refs[0], refs[1 + 2 * nwin], refs[2 + 2 * nwin], refs[3 + 2 * nwin]
        i = pl.program_id(1)
        k = _a_window(refs[1:1 + nwin], i)
        v = _a_window(refs[1 + nwin:1 + 2 * nwin], i)
        for j in range(A_QB // A_TQ):
            rows, _, _, sc = _a_scores(q_ref, k, t_ref, i, j)
            m = jnp.max(sc, axis=-1, keepdims=True)
            p = jnp.exp(sc - m)
            l = jnp.sum(p, axis=-1, keepdims=True)
            o_ref[rows, :] = _dot(p.astype(BF16), v[j * A_TQ:j * A_TQ + A_BAND, :]) / l
            lse_ref[rows, :] = m + jnp.log(l)

    return pl.pallas_call(
        body, name="mix_a_fwd", grid=(N_HEADS, nq),
        in_specs=[q_spec] + kv_specs + [t_spec],
        out_specs=[pl.BlockSpec((None, A_QB, HEAD_DIM), lambda h, i: (h, i, 0)),
                   pl.BlockSpec((None, A_QB, 1), lambda h, i: (h, i, 0))],
        out_shape=[jax.ShapeDtypeStruct((N_HEADS, s, HEAD_DIM), F32),
                   jax.ShapeDtypeStruct((N_HEADS, s, 1), F32)],
        compiler_params=_params(("arbitrary", "arbitrary")),
    )(qkv, *([kva] * (2 * nwin)), tbias)


def mix_a_bwd(qkv, kva, tbias, do, o, lse):
    s = qkv.shape[2]
    nq, q_spec, kv_specs, t_spec = _a_specs(s)
    nwin = len(kv_specs) // 2
    row_spec = lambda w: pl.BlockSpec((None, A_QB, w), lambda h, i: (h, jnp.minimum(i, nq - 1), 0))
    done_spec = pl.BlockSpec((None, A_QB, HEAD_DIM), lambda h, i: (h, i, 0))
    win = A_QB + A_PAD

    def body(*refs):
        q_ref = refs[0]
        t_ref, do_ref, o_ref, lse_ref, dq_ref, dk_ref, dv_ref, dt_ref, dk_win, dv_win = refs[1 + 2 * nwin:]
        i = pl.program_id(1)

        @pl.when(i == 0)
        def _():
            dk_win[...] = jnp.zeros_like(dk_win)
            dv_win[...] = jnp.zeros_like(dv_win)
            dt_ref[...] = jnp.zeros_like(dt_ref)

        @pl.when(i < nq)
        def _():
            k = _a_window(refs[1:1 + nwin], i)
            v = _a_window(refs[1 + nwin:1 + 2 * nwin], i)
            dt = jnp.zeros((A_TQ, A_BAND), F32)
            for j in range(A_QB // A_TQ):
                rows, qs, kj, sc = _a_scores(q_ref, k, t_ref, i, j)
                keys = slice(j * A_TQ, j * A_TQ + A_BAND)
                dob = do_ref[rows, :]
                p = jnp.exp(sc - lse_ref[rows, :])
                delta = jnp.sum(o_ref[rows, :] * dob.astype(F32), axis=-1, keepdims=True)
                ds = p * (_dot_nt(dob, v[keys, :]) - delta)
                dsb = ds.astype(BF16)
                dq_ref[rows, :] = _dot(dsb, kj) * 0.125
                dk_win[keys, :] += _dot_tn(dsb, qs)
                dv_win[keys, :] += _dot_tn(p.astype(BF16), dob)
                dt = dt + ds
            dt_ref[...] += dt

        dk_ref[...] = dk_win[0:A_QB, :]
        dv_ref[...] = dv_win[0:A_QB, :]
        dk_rest = dk_win[A_QB:win, :]
        dv_rest = dv_win[A_QB:win, :]
        dk_win[0:A_PAD, :] = dk_rest
        dv_win[0:A_PAD, :] = dv_rest
        dk_win[A_PAD:win, :] = jnp.zeros((A_QB, HEAD_DIM), F32)
        dv_win[A_PAD:win, :] = jnp.zeros((A_QB, HEAD_DIM), F32)

    return pl.pallas_call(
        body, name="mix_a_bwd", grid=(N_HEADS, nq + 1),
        in_specs=[q_spec] + kv_specs + [t_spec, row_spec(HEAD_DIM), row_spec(HEAD_DIM), row_spec(1)],
        out_specs=[row_spec(HEAD_DIM), done_spec, done_spec, t_spec],
        out_shape=[jax.ShapeDtypeStruct((N_HEADS, s, HEAD_DIM), F32),
                   jax.ShapeDtypeStruct((N_HEADS, s + A_QB, HEAD_DIM), F32),
                   jax.ShapeDtypeStruct((N_HEADS, s + A_QB, HEAD_DIM), F32),
                   jax.ShapeDtypeStruct((N_HEADS, A_TQ, A_BAND), F32)],
        scratch_shapes=[pltpu.VMEM((win, HEAD_DIM), F32), pltpu.VMEM((win, HEAD_DIM), F32)],
        compiler_params=_params(("arbitrary", "arbitrary")),
    )(qkv, *([kva] * (2 * nwin)), tbias, do, o, lse)


def relbias_tile(rel_bias, relmat):
    def body(rb_ref, rel_ref, o_ref):
        rel = rel_ref[...]
        o_ref[...] = jnp.full(o_ref.shape, NEG, F32)

        def step(r, carry):
            hit = rel == r
            for h in range(N_HEADS):
                o_ref[h] = jnp.where(hit, rb_ref[h, r], o_ref[h])
            return carry

        lax.fori_loop(REL_LO, REL_HI, step, 0)

    return pl.pallas_call(
        body, name="relbias_tile",
        in_specs=[pl.BlockSpec(memory_space=pltpu.SMEM), pl.BlockSpec(memory_space=pltpu.VMEM)],
        out_specs=pl.BlockSpec(memory_space=pltpu.VMEM),
        out_shape=jax.ShapeDtypeStruct((N_HEADS, A_TQ, A_BAND), F32),
        compiler_params=_params(),
    )(rel_bias, relmat)


def relbias_grad(dt, relmat):
    def body(dt_ref, rel_ref, o_ref):
        rel = rel_ref[...]
        lane = lax.broadcasted_iota(jnp.int32, (8, 384), 1)
        row = lax.broadcasted_iota(jnp.int32, (8, 384), 0)

        def step(r, acc):
            hit = rel == r
            for h in range(N_HEADS):
                val = jnp.sum(jnp.where(hit, dt_ref[h], 0.0))
                acc = jnp.where((lane == r) & (row == h), val, acc)
            return acc

        o_ref[...] = lax.fori_loop(REL_LO, REL_HI, step, jnp.zeros((8, 384), F32))

    return pl.pallas_call(
        body, name="relbias_grad",
        out_shape=jax.ShapeDtypeStruct((8, 384), F32),
        compiler_params=_params(),
    )(dt, relmat)


def _b_norm(v, gain):
    mu = jnp.mean(v, axis=-1, keepdims=True)
    xc = v - mu
    rstd = lax.rsqrt(jnp.mean(xc * xc, axis=-1, keepdims=True) + EPS)
    xhat = xc * rstd
    return xhat, rstd, xhat * gain


def _tril_mask():
    t = lax.broadcasted_iota(jnp.int32, (SG_CHUNK, SG_CHUNK), 0)
    u = lax.broadcasted_iota(jnp.int32, (SG_CHUNK, SG_CHUNK), 1)
    return u <= t


def mix_b_fwd(uv, gain, w_s, b_col):
    s = uv.shape[0]
    tm = min(ROW_T, s)

    def body(uv_ref, gain_ref, w_ref, b_ref, y_ref):
        tril = _tril_mask()
        ws = [jnp.where(tril, w_ref[g], 0.0).astype(BF16) for g in range(N_HEADS)]
        for c in range(tm // SG_CHUNK):
            rows = slice(c * SG_CHUNK, (c + 1) * SG_CHUNK)
            u = uv_ref[rows, 0:D_BRANCH]
            _, _, vn = _b_norm(uv_ref[rows, D_BRANCH:2 * D_BRANCH], gain_ref[...])
            vnb = vn.astype(BF16)
            outs = []
            for g in range(N_HEADS):
                cols = slice(g * HEAD_DIM, (g + 1) * HEAD_DIM)
                mixed = _dot(ws[g], vnb[:, cols]) + b_ref[g]
                outs.append(u[:, cols] * mixed)
            y_ref[rows, :] = jnp.concatenate(outs, axis=1)

    return pl.pallas_call(
        body, name="mix_b_fwd", grid=(s // tm,),
        in_specs=[pl.BlockSpec((tm, 2 * D_BRANCH), lambda i: (i, 0)),
                  pl.BlockSpec((1, D_BRANCH), lambda i: (0, 0)),
                  pl.BlockSpec((N_HEADS, SG_CHUNK, SG_CHUNK), lambda i: (0, 0, 0)),
                  pl.BlockSpec((N_HEADS, SG_CHUNK, 1), lambda i: (0, 0, 0))],
        out_specs=pl.BlockSpec((tm, D_BRANCH), lambda i: (i, 0)),
        out_shape=jax.ShapeDtypeStruct((s, D_BRANCH), F32),
        compiler_params=_params(("arbitrary",)),
    )(uv, gain, w_s, b_col)


def mix_b_bwd(uv, gain, w_s, b_col, dy):
    s = uv.shape[0]
    tm = min(ROW_T, s)

    def body(uv_ref, gain_ref, w_ref, b_ref, dy_ref, duv_ref, dw_ref, db_ref, dgain_ref):
        i = pl.program_id(0)

        @pl.when(i == 0)
        def _():
            dw_ref[...] = jnp.zeros_like(dw_ref)
            db_ref[...] = jnp.zeros_like(db_ref)
            dgain_ref[...] = jnp.zeros_like(dgain_ref)

        tril = _tril_mask()
        ws = [jnp.where(tril, w_ref[g], 0.0).astype(BF16) for g in range(N_HEADS)]
        gain_v = gain_ref[...]
        for c in range(tm // SG_CHUNK):
            rows = slice(c * SG_CHUNK, (c + 1) * SG_CHUNK)
            u = uv_ref[rows, 0:D_BRANCH]
            xhat, rstd, vn = _b_norm(uv_ref[rows, D_BRANCH:2 * D_BRANCH], gain_v)
            vnb = vn.astype(BF16)
            dyv = dy_ref[rows, :]
            dus, dvns = [], []
            for g in range(N_HEADS):
                cols = slice(g * HEAD_DIM, (g + 1) * HEAD_DIM)
                mixed = _dot(ws[g], vnb[:, cols]) + b_ref[g]
                dus.append(dyv[:, cols] * mixed)
                dmixed = dyv[:, cols] * u[:, cols]
                dmb = dmixed.astype(BF16)
                db_ref[g] += jnp.sum(dmixed, axis=-1, keepdims=True)
                dw_ref[g] += jnp.where(tril, _dot_nt(dmb, vnb[:, cols]), 0.0)
                dvns.append(_dot_tn(ws[g], dmb))
            dvn = jnp.concatenate(dvns, axis=1)
            dgain_ref[...] += jnp.sum(dvn * xhat, axis=0, keepdims=True)
            dxh = dvn * gain_v
            dv = rstd * (dxh - jnp.mean(dxh, axis=-1, keepdims=True)
                         - xhat * jnp.mean(dxh * xhat, axis=-1, keepdims=True))
            duv_ref[rows, :] = jnp.concatenate(dus + [dv], axis=1)

    return pl.pallas_call(
        body, name="mix_b_bwd", grid=(s // tm,),
        in_specs=[pl.BlockSpec((tm, 2 * D_BRANCH), lambda i: (i, 0)),
                  pl.BlockSpec((1, D_BRANCH), lambda i: (0, 0)),
                  pl.BlockSpec((N_HEADS, SG_CHUNK, SG_CHUNK), lambda i: (0, 0, 0)),
                  pl.BlockSpec((N_HEADS, SG_CHUNK, 1), lambda i: (0, 0, 0)),
                  pl.BlockSpec((tm, D_BRANCH), lambda i: (i, 0))],
        out_specs=[pl.BlockSpec((tm, 2 * D_BRANCH), lambda i: (i, 0)),
                   pl.BlockSpec((N_HEADS, SG_CHUNK, SG_CHUNK), lambda i: (0, 0, 0)),
                   pl.BlockSpec((N_HEADS, SG_CHUNK, 1), lambda i: (0, 0, 0)),
                   pl.BlockSpec((1, D_BRANCH), lambda i: (0, 0))],
        out_shape=[jax.ShapeDtypeStruct((s, 2 * D_BRANCH), F32),
                   jax.ShapeDtypeStruct((N_HEADS, SG_CHUNK, SG_CHUNK), F32),
                   jax.ShapeDtypeStruct((N_HEADS, SG_CHUNK, 1), F32),
                   jax.ShapeDtypeStruct((1, D_BRANCH), F32)],
        compiler_params=_params(("arbitrary",)),
    )(uv, gain, w_s, b_col, dy)


def _scan_mats(nrow):
    a = lax.broadcasted_iota(jnp.int32, (128, 128), 0)
    b = lax.broadcasted_iota(jnp.int32, (128, 128), 1)
    r = lax.broadcasted_iota(jnp.int32, (nrow, nrow), 0)
    c = lax.broadcasted_iota(jnp.int32, (nrow, nrow), 1)
    nb = nrow // N_HEADS
    same = (r // nb) == (c // nb)
    return a, b, r, c, same


def _exact_dot(x, m):
    hi, mid, lo = _split3(x)
    return _dot(hi, m) + _dot(mid, m) + _dot(lo, m)


def _exact_dot_left(m, x):
    hi, mid, lo = _split3(x)
    return _dot(m, hi) + _dot(m, mid) + _dot(m, lo)


def fox_gate_fwd(ft, bcol):
    nrow = ft.shape[0]

    def body(f_ref, b_ref, c_ref):
        z = f_ref[...] + b_ref[...]
        ls = jnp.minimum(z, 0.0) - jnp.log(1.0 + jnp.exp(-jnp.abs(z)))
        a, b, r, c, same = _scan_mats(nrow)
        within = _exact_dot(ls, (a <= b).astype(BF16))
        tot = jnp.broadcast_to(within[:, 127:128], within.shape)
        before = _exact_dot_left((same & (c < r)).astype(BF16), tot)
        c_ref[...] = within + before

    return pl.pallas_call(
        body, name="fox_gate_fwd",
        out_shape=jax.ShapeDtypeStruct((nrow, 128), F32),
        compiler_params=_params(),
    )(ft, bcol)


def fox_gate_bwd(ft, bcol, dc):
    nrow = ft.shape[0]

    def body(f_ref, b_ref, dc_ref, df_ref, db_ref):
        a, b, r, c, same = _scan_mats(nrow)
        dcv = dc_ref[...]
        within = _exact_dot(dcv, (a >= b).astype(BF16))
        tot = jnp.broadcast_to(within[:, 0:1], within.shape)
        after = _exact_dot_left((same & (c > r)).astype(BF16), tot)
        dls = within + after
        z = f_ref[...] + b_ref[...]
        dz = dls * _sigmoid(-z)
        df_ref[...] = dz
        rs = jnp.broadcast_to(jnp.sum(dz, axis=-1, keepdims=True), dz.shape)
        hr = lax.broadcasted_iota(jnp.int32, (8, nrow), 0)
        hc = lax.broadcasted_iota(jnp.int32, (8, nrow), 1)
        db_ref[...] = _exact_dot_left((hr == hc // (nrow // N_HEADS)).astype(BF16), rs)

    return pl.pallas_call(
        body, name="fox_gate_bwd",
        out_shape=[jax.ShapeDtypeStruct((nrow, 128), F32), jax.ShapeDtypeStruct((8, 128), F32)],
        compiler_params=_params(),
    )(ft, bcol, dc)


def _att_specs(s, qi, ki, vi):
    q_spec = pl.BlockSpec((None, None, FOX_TQ, HEAD_DIM), lambda h, i: (qi, h, i, 0))
    k_spec = pl.BlockSpec((None, None, s, HEAD_DIM), lambda h, i: (ki, h, 0, 0))
    v_spec = pl.BlockSpec((None, None, s, HEAD_DIM), lambda h, i: (vi, h, 0, 0))
    row_spec = lambda w: pl.BlockSpec((None, FOX_TQ, w), lambda h, i: (h, i, 0))
    gate_spec = pl.BlockSpec((None, s // ATT_T, 1, ATT_T), lambda h, i: (h, 0, 0, 0))
    return q_spec, k_spec, v_spec, row_spec, gate_spec


def _causal(strict, n=ATT_T):
    row = lax.broadcasted_iota(jnp.int32, (n, n), 0)
    col = lax.broadcasted_iota(jnp.int32, (n, n), 1)
    return (col < row) if strict else (col <= row)


def _gate_row(cr_ref, kb, g):
    if g == 1:
        return cr_ref[kb]
    return jnp.concatenate([cr_ref[kb + n] for n in range(g)], axis=1)


def _fox_walk(i, carry, tile, alive):
    g = FOX_WIDE
    own = FOX_TQ // ATT_T
    nwide = (own * i) // g
    carry = tile(own * i, own, carry, True)
    carry = lax.fori_loop(0, (own * i - nwide * g) // own, lambda n, c: tile(nwide * g, own, c, False), carry)

    def cond(state):
        return jnp.logical_and(state[0] >= 0, state[1] > 0)

    def step(state):
        n = state[0]
        c = tile(n * g, g, state[2:], False)
        return (n - 1, alive(n * g, c)) + tuple(c)

    out = lax.while_loop(cond, step, (nwide - 1, alive(nwide * g, carry)) + tuple(carry))
    return out[2:]


def _fox_reach(qs, k_ref, kmax_ref, cc, i):
    s = k_ref.shape[0]
    rows = 4 * ATT_T

    @pl.when(i == 0)
    def _():
        def chunk(n, mx):
            kc = k_ref[pl.ds(pl.multiple_of(n * rows, rows), rows), :].astype(F32)
            return jnp.maximum(mx, jnp.max(jnp.sum(kc * kc, axis=-1, keepdims=True)))

        kmax_ref[0] = jnp.sqrt(lax.fori_loop(0, s // rows, chunk, jnp.float32(0.0)))

    qf = qs.astype(F32)
    return jnp.sqrt(jnp.sum(qf * qf, axis=-1, keepdims=True)) * kmax_ref[0] + cc


def _gate_col(cr_ref, i):
    row = lax.broadcasted_iota(jnp.int32, (ATT_T, ATT_T), 0)
    col = lax.broadcasted_iota(jnp.int32, (ATT_T, ATT_T), 1)
    own = FOX_TQ // ATT_T
    return jnp.concatenate([jnp.sum(jnp.where(row == col, cr_ref[own * i + n], 0.0), axis=-1, keepdims=True)
                            for n in range(own)], axis=0)


def _fox_scores(qs, k, cc, crow, masked):
    sc = (_dot_nt(qs, k) + (cc - crow)) * LOG2E
    if masked:
        sc = jnp.where(_causal(False, FOX_TQ), sc, NEG)
    return sc


def fox_fwd(qkv, c_row, ride=()):
    s = qkv.shape[2]
    t = ATT_T
    nq = s // FOX_TQ
    q_spec, k_spec, v_spec, row_spec, gate_spec = _att_specs(s, 1, 2, 3)
    rows = 4 * t
    nride = len(ride)

    def body(q_ref, k_ref, v_ref, cr_ref, *refs):
        ride_in, refs = refs[:nride], refs[nride:]
        o_ref, ref_ref, rl_ref = refs[:3]
        ride_out, refs = refs[3:3 + nride], refs[3 + nride:]
        v1_ref, kmax_ref = refs[:2]
        i = pl.program_id(1)
        if nride:
            h = pl.program_id(0)
            start, wait = _chip_gather([(src, lambda slot, dst=dst: dst.at[slot]) for src, dst in zip(ride_in, ride_out)],
                                       *refs[2:])
            pl.when(jnp.logical_and(h == 0, i == 0))(start)

        @pl.when(i == 0)
        def _():
            def chunk(n, carry):
                r0 = pl.multiple_of(n * rows, rows)
                v1_ref[pl.ds(r0, rows), :] = jnp.concatenate(
                    [v_ref[pl.ds(r0, rows), :], jnp.ones((rows, HEAD_DIM), BF16)], axis=1)
                return carry

            lax.fori_loop(0, s // rows, chunk, 0)

        qs = q_ref[...] * 0.125
        cc = _gate_col(cr_ref, i)
        reach = _fox_reach(qs, k_ref, kmax_ref, cc, i) * LOG2E

        def alive(kb, carry):
            return (jnp.max(reach - cr_ref[kb][:, 0:1] * LOG2E - carry[0]) > FOX_DEAD2).astype(jnp.int32)

        def tile(kb, g, carry, masked):
            m, acc = carry
            k0 = pl.multiple_of(kb * t, t)
            sc = _fox_scores(qs, k_ref[pl.ds(k0, g * t), :], cc, _gate_row(cr_ref, kb, g), masked)
            m_new = jnp.maximum(m, jnp.ceil(jnp.max(sc, axis=-1, keepdims=True)))
            pb = jnp.exp2(sc - m_new).astype(BF16)
            acc = jnp.exp2(m - m_new) * acc + _dot(pb, v1_ref[pl.ds(k0, g * t), :])
            return m_new, acc

        init = (jnp.full((FOX_TQ, 1), NEG, F32), jnp.zeros((FOX_TQ, 2 * HEAD_DIM), F32))
        m, acc = _fox_walk(i, init, tile, alive)
        rl = 1.0 / acc[:, HEAD_DIM:HEAD_DIM + 1]
        o_ref[...] = acc[:, 0:HEAD_DIM] * rl
        ref_ref[...] = m
        rl_ref[...] = rl
        if nride:
            pl.when(jnp.logical_and(h == N_HEADS - 1, i == nq - 1))(wait)

    any_spec = pl.BlockSpec(memory_space=pl.ANY)
    ride_sems = [pltpu.SemaphoreType.DMA((3 * nride,)), pltpu.SemaphoreType.DMA((3 * nride,)),
                 pltpu.SemaphoreType.DMA((nride,))] if nride else []
    return pl.pallas_call(
        body, name="fox_fwd_gather" if nride else "fox_fwd", grid=(N_HEADS, nq),
        in_specs=[q_spec, k_spec, v_spec, gate_spec] + [any_spec] * nride,
        out_specs=[row_spec(HEAD_DIM), row_spec(1), row_spec(1)] + [any_spec] * nride,
        out_shape=[jax.ShapeDtypeStruct((N_HEADS, s, HEAD_DIM), F32),
                   jax.ShapeDtypeStruct((N_HEADS, s, 1), F32),
                   jax.ShapeDtypeStruct((N_HEADS, s, 1), F32)]
        + [jax.ShapeDtypeStruct((4,) + a.shape, a.dtype) for a in ride],
        scratch_shapes=[pltpu.VMEM((s, 2 * HEAD_DIM), BF16), pltpu.SMEM((1,), F32)] + ride_sems,
        compiler_params=_params(("arbitrary", "arbitrary")),
    )(qkv, qkv, qkv, c_row, *ride)


def fox_bwd(qkv, c_row, do, o, ref, rl, ride=()):
    s = qkv.shape[2]
    t = ATT_T
    nq = s // FOX_TQ
    q_spec, k_spec, v_spec, row_spec, gate_spec = _att_specs(s, 1, 2, 3)
    any_spec = pl.BlockSpec(memory_space=pl.ANY)
    nride = len(ride)

    def body(q_ref, k_ref, v_ref, cr_ref, do_ref, o_ref, ref_ref, rl_ref, *refs):
        ride_in, refs = refs[:nride], refs[nride:]
        dq_ref, dk_hbm, dv_hbm, dc_ref = refs[:4]
        ride_out, refs = refs[4:4 + nride], refs[4 + nride:]
        dk_acc, dv_acc, kmax_ref = refs[:3]
        h = pl.program_id(0)
        i = pl.program_id(1)
        if nride:
            start, wait = _device_exchange(_exchange_flows(ride_in, ride_out), *refs[3:])
            pl.when(jnp.logical_and(h == 0, i == 0))(start)

        @pl.when(i == 0)
        def _():
            dk_acc[...] = jnp.zeros_like(dk_acc)
            dv_acc[...] = jnp.zeros_like(dv_acc)
            dc_ref[...] = jnp.zeros_like(dc_ref)

        qs = q_ref[...] * 0.125
        ref = ref_ref[...]
        rl = rl_ref[...]
        dob = (do_ref[...].astype(F32) * rl).astype(BF16)
        delta = jnp.sum(o_ref[...] * dob.astype(F32), axis=-1, keepdims=True)
        cc = _gate_col(cr_ref, i)
        margin = _fox_reach(qs, k_ref, kmax_ref, cc, i) * LOG2E - ref

        def alive(kb, carry):
            return (jnp.max(margin - cr_ref[kb][:, 0:1] * LOG2E) > FOX_DEAD2).astype(jnp.int32)

        def tile(kb, g, carry, masked):
            dq, = carry
            k0 = pl.multiple_of(kb * t, t)
            k = k_ref[pl.ds(k0, g * t), :]
            sc = _fox_scores(qs, k, cc, _gate_row(cr_ref, kb, g), masked)
            wb = jnp.exp2(sc - ref).astype(BF16)
            ds = wb.astype(F32) * (_dot_nt(dob, v_ref[pl.ds(k0, g * t), :]) - delta)
            dsb = ds.astype(BF16)
            dk_acc[pl.ds(k0, g * t), :] += _dot_tn(dsb, qs)
            dv_acc[pl.ds(k0, g * t), :] += _dot_tn(wb, dob)
            dcs = -jnp.sum(ds, axis=0, keepdims=True)
            for n in range(g):
                dc_ref[kb + n] += dcs[:, n * t:(n + 1) * t]
            return (dq + _dot(dsb, k),)

        dq, = _fox_walk(i, (jnp.zeros((FOX_TQ, HEAD_DIM), F32),), tile, alive)
        dq_ref[...] = dq * 0.125

        @pl.when(i == nq - 1)
        def _():
            pltpu.sync_copy(dk_acc, dk_hbm.at[h])
            pltpu.sync_copy(dv_acc, dv_hbm.at[h])

        if nride:
            pl.when(jnp.logical_and(h == N_HEADS - 1, i == nq - 1))(wait)

    return pl.pallas_call(
        body, name="fox_bwd_exchange" if nride else "fox_bwd", grid=(N_HEADS, nq),
        in_specs=[q_spec, k_spec, v_spec,
                  gate_spec,
                  row_spec(HEAD_DIM), row_spec(HEAD_DIM), row_spec(1), row_spec(1)] + [any_spec] * nride,
        out_specs=[row_spec(HEAD_DIM), any_spec, any_spec,
                   gate_spec] + [any_spec] * nride,
        out_shape=[jax.ShapeDtypeStruct((N_HEADS, s, HEAD_DIM), F32),
                   jax.ShapeDtypeStruct((N_HEADS, s, HEAD_DIM), F32),
                   jax.ShapeDtypeStruct((N_HEADS, s, HEAD_DIM), F32),
                   jax.ShapeDtypeStruct((N_HEADS, s // t, 1, t), F32)] + _exchange_shapes(ride),
        scratch_shapes=[pltpu.VMEM((s, HEAD_DIM), F32), pltpu.VMEM((s, HEAD_DIM), F32), pltpu.SMEM((1,), F32)]
        + (_exchange_sems(nride) if nride else []),
        compiler_params=_params(("arbitrary", "arbitrary")),
    )(qkv, qkv, qkv, c_row, do, o, ref, rl, *ride)


def _sb_valid(nrows, ahead):
    row = lax.broadcasted_iota(jnp.int32, (nrows, ATT_T), 0)
    col = lax.broadcasted_iota(jnp.int32, (nrows, ATT_T), 1)
    return col + ahead < row


def _sb_band_valid(nsub):
    row = lax.broadcasted_iota(jnp.int32, (nsub * SB_SUB, SB_BAND), 0)
    col = lax.broadcasted_iota(jnp.int32, (nsub * SB_SUB, SB_BAND), 1)
    return col < (row & (SB_SUB - 1)) + SB_BACK


def _sb_logits(qs, k):
    z = _dot_nt(qs, k)
    sp = jnp.log(1.0 + jnp.exp(-jnp.abs(z)))
    return jnp.minimum(z, 0.0) - sp, -jnp.maximum(z, 0.0) - sp


def _sb_weights(ls, lm, run, valid):
    if valid is not None:
        lm = jnp.where(valid, lm, 0.0)
    n = lm.shape[1]
    row = lax.broadcasted_iota(jnp.int32, (n, n), 0)
    col = lax.broadcasted_iota(jnp.int32, (n, n), 1)
    later = (row > col).astype(BF16)
    hi, lo = _split2(lm)
    between = _dot(hi, later) + _dot(lo, later)
    if run is not None:
        between = run + between
    a = jnp.exp(ls + between)
    if valid is not None:
        a = jnp.where(valid, a, 0.0)
    return lm, a


def _sb_band_start(i, j):
    return pl.multiple_of(i * 2 * ATT_T + j * SB_SUB - SB_BACK, SB_SUB)


def _sb_tile(qs, k, run, valid):
    ls, lm = _sb_logits(qs, k)
    lm, a = _sb_weights(ls, lm, run, valid)
    return ls, lm, a


def _sb_band(i, qs_all, k_ref):
    nsub = qs_all.shape[0] // SB_SUB
    valid = _sb_band_valid(nsub)
    starts = [_sb_band_start(i, j) for j in range(nsub)]
    kwins = [k_ref[pl.ds(k0, SB_BAND), :] for k0 in starts]
    parts = [_sb_logits(qs_all[j * SB_SUB:(j + 1) * SB_SUB], kwins[j]) for j in range(nsub)]
    ls = jnp.concatenate([p[0] for p in parts], axis=0)
    lm, a = _sb_weights(ls, jnp.concatenate([p[1] for p in parts], axis=0), None, valid)
    return starts, kwins, ls, lm, a, valid


def _sb_suffix(g, run_g):
    n = g.shape[1]
    row = lax.broadcasted_iota(jnp.int32, (n, n), 0)
    col = lax.broadcasted_iota(jnp.int32, (n, n), 1)
    from_here = (row >= col).astype(BF16)
    hi, lo = _split2(g)
    out = _dot(hi, from_here) + _dot(lo, from_here)
    return out if run_g is None else run_g + out


def _sb_walk(i, carry, tile):
    def alive_of(c):
        return (jnp.max(c[0]) > SB_DEAD).astype(jnp.int32)

    def cond(state):
        n, alive = state[0], state[1]
        return jnp.logical_and(n < i, alive > 0)

    def step(state):
        n = state[0]
        c = tile(i - 1 - n, state[2:], False)
        return (n + 1, alive_of(c)) + tuple(c)

    out = lax.while_loop(cond, step, (jnp.int32(0), alive_of(carry)) + tuple(carry))
    return out[2:]


def _sb_specs(s):
    tq = 2 * ATT_T
    q_spec = pl.BlockSpec((None, None, tq, HEAD_DIM), lambda h, i: (4, h, i, 0))
    k_spec = pl.BlockSpec((None, None, s, HEAD_DIM), lambda h, i: (5, h, 0, 0))
    v_spec = pl.BlockSpec((None, None, s, HEAD_DIM), lambda h, i: (6, h, 0, 0))
    row_spec = pl.BlockSpec((None, tq, HEAD_DIM), lambda h, i: (h, i, 0))
    band_spec = pl.BlockSpec((None, None, 1, 128), lambda h, i: (h, i, 0, 0))
    return tq, q_spec, k_spec, v_spec, row_spec, band_spec


def _sb_block(i, tile, zero):
    t = ATT_T
    lo, hi, both = slice(0, t), slice(t, 2 * t), slice(0, 2 * t)
    c_hi = tile(2 * i + 1, hi, zero, 0)
    c_lo = tile(2 * i, lo, zero, 0)
    c_hi = tile(2 * i, hi, c_hi, None)
    carry = tuple(jnp.concatenate([a, b], axis=0) for a, b in zip(c_lo, c_hi))
    return _sb_walk(2 * i, carry, lambda kb, c, _: tile(kb, both, c, None))


def sb_fwd(qkv):
    s = qkv.shape[2]
    t = ATT_T
    tq, q_spec, k_spec, v_spec, row_spec, band_spec = _sb_specs(s)

    def body(q_ref, k_ref, v_ref, o_ref, band_ref, done_ref):
        i = pl.program_id(1)
        qs = q_ref[...] * 0.125
        done_ref[0] = 0

        @pl.when(i > 0)
        def _():
            starts, _, _, lm, a, _ = _sb_band(i, qs, k_ref)
            ab = a.astype(BF16)
            for j, k0 in enumerate(starts):
                rows = slice(j * SB_SUB, (j + 1) * SB_SUB)
                o_ref[rows, :] = _dot(ab[rows], v_ref[pl.ds(k0, SB_BAND), :])
            worst = jnp.max(jnp.sum(lm, axis=-1, keepdims=True))
            done_ref[0] = (worst <= SB_DEAD).astype(jnp.int32)

        @pl.when(done_ref[0] == 0)
        def _():
            def tile(kb, rows, carry, ahead):
                run, acc = carry
                k0 = pl.multiple_of(kb * t, t)
                valid = None if ahead is None else _sb_valid(t, ahead)
                _, lm, a = _sb_tile(qs[rows], k_ref[pl.ds(k0, t), :], run, valid)
                acc = acc + _dot(a.astype(BF16), v_ref[pl.ds(k0, t), :])
                return run + jnp.sum(lm, axis=-1, keepdims=True), acc

            _, acc = _sb_block(i, tile, (jnp.zeros((t, 1), F32), jnp.zeros((t, HEAD_DIM), F32)))
            o_ref[...] = acc

        band_ref[...] = jnp.full(band_ref.shape, done_ref[0], jnp.int32).astype(F32)

    return pl.pallas_call(
        body, name="sb_fwd", grid=(N_HEADS, s // tq),
        in_specs=[q_spec, k_spec, v_spec],
        out_specs=[row_spec, band_spec],
        out_shape=[jax.ShapeDtypeStruct((N_HEADS, s, HEAD_DIM), F32),
                   jax.ShapeDtypeStruct((N_HEADS, s // tq, 1, 128), F32)],
        scratch_shapes=[pltpu.SMEM((1,), jnp.int32)],
        compiler_params=_params(("arbitrary", "arbitrary")),
    )(qkv, qkv, qkv)


def sb_bwd(qkv, do, o, band):
    s = qkv.shape[2]
    t = ATT_T
    tq, q_spec, k_spec, v_spec, row_spec, band_spec = _sb_specs(s)
    nq = s // tq
    any_spec = pl.BlockSpec(memory_space=pl.ANY)

    def body(q_ref, k_ref, v_ref, do_ref, o_ref, band_ref, dq_ref, dk_hbm, dv_hbm, dk_acc, dv_acc):
        h = pl.program_id(0)
        i = pl.program_id(1)

        @pl.when(i == 0)
        def _():
            dk_acc[...] = jnp.zeros_like(dk_acc)
            dv_acc[...] = jnp.zeros_like(dv_acc)

        qs_all = q_ref[...] * 0.125
        dob_all = do_ref[...]
        tot_all = jnp.sum(o_ref[...] * dob_all.astype(F32), axis=-1, keepdims=True)
        on_band = jnp.max(band_ref[...]) > 0.5

        def grads(qs, dob, tot, k, v, k0, run, run_g, valid):
            ls, lm, a = _sb_tile(qs, k, run, valid)
            ab = a.astype(BF16)
            g = ab.astype(F32) * _dot_nt(dob, v)
            g_left = tot - _sb_suffix(g, run_g)
            dz = g - jnp.exp(ls) * (g + g_left)
            if valid is not None:
                dz = jnp.where(valid, dz, 0.0)
            dzb = dz.astype(BF16)
            n = k.shape[0]
            dk_acc[pl.ds(k0, n), :] += _dot_tn(dzb, qs)
            dv_acc[pl.ds(k0, n), :] += _dot_tn(ab, dob)
            return dzb, lm, g

        @pl.when(on_band)
        def _():
            starts, kwins, ls, _, a, valid = _sb_band(i, qs_all, k_ref)
            ab = a.astype(BF16)
            subs = [slice(j * SB_SUB, (j + 1) * SB_SUB) for j in range(len(starts))]
            vwins = [v_ref[pl.ds(k0, SB_BAND), :] for k0 in starts]
            g = ab.astype(F32) * jnp.concatenate([_dot_nt(dob_all[r], v) for r, v in zip(subs, vwins)], axis=0)
            dz = jnp.where(valid, g - jnp.exp(ls) * (g + (tot_all - _sb_suffix(g, None))), 0.0)
            dzb = dz.astype(BF16)
            for r, k0, k in zip(subs, starts, kwins):
                dq_ref[r, :] = _dot(dzb[r], k) * 0.125
                dk_acc[pl.ds(k0, SB_BAND), :] += _dot_tn(dzb[r], qs_all[r])
                dv_acc[pl.ds(k0, SB_BAND), :] += _dot_tn(ab[r], dob_all[r])

        @pl.when(jnp.logical_not(on_band))
        def _():
            def tile(kb, rows, carry, ahead):
                run, run_g, dq = carry
                k0 = pl.multiple_of(kb * t, t)
                k = k_ref[pl.ds(k0, t), :]
                valid = None if ahead is None else _sb_valid(t, ahead)
                dzb, lm, g = grads(qs_all[rows], dob_all[rows], tot_all[rows], k, v_ref[pl.ds(k0, t), :], k0,
                                   run, run_g, valid)
                return (run + jnp.sum(lm, axis=-1, keepdims=True),
                        run_g + jnp.sum(g, axis=-1, keepdims=True),
                        dq + _dot(dzb, k))

            zero = jnp.zeros((t, 1), F32)
            _, _, dq = _sb_block(i, tile, (zero, zero, jnp.zeros((t, HEAD_DIM), F32)))
            dq_ref[...] = dq * 0.125

        @pl.when(i == nq - 1)
        def _():
            pltpu.sync_copy(dk_acc, dk_hbm.at[h])
            pltpu.sync_copy(dv_acc, dv_hbm.at[h])

    return pl.pallas_call(
        body, name="sb_bwd", grid=(N_HEADS, nq),
        in_specs=[q_spec, k_spec, v_spec, row_spec, row_spec, band_spec],
        out_specs=[row_spec, any_spec, any_spec],
        out_shape=[jax.ShapeDtypeStruct((N_HEADS, s, HEAD_DIM), F32)] * 3,
        scratch_shapes=[pltpu.VMEM((s, HEAD_DIM), F32), pltpu.VMEM((s, HEAD_DIM), F32)],
        compiler_params=_params(("arbitrary", "arbitrary")),
    )(qkv, qkv, qkv, do, o, band)


def _branch_inputs(refs, br):
    ya_ref, yb_ref, yc_ref, yd_ref = refs
    if br == 1:
        return yb_ref[...]
    return _heads_to_lanes((ya_ref, None, yc_ref, yd_ref)[br])


def outproj_fwd(x, ya, yb, yc, yd, gates, bg, wout):
    s = x.shape[0]
    tm = min(ROW_T, s)

    def body(x_ref, ya_ref, yb_ref, yc_ref, yd_ref, gates_ref, bg_ref, w_ref, out_ref):
        pieces = []
        for br in range(4):
            cols = slice(br * D_BRANCH, (br + 1) * D_BRANCH)
            y = _branch_inputs((ya_ref, yb_ref, yc_ref, yd_ref), br)
            r = lax.rsqrt(jnp.mean(y * y, axis=-1, keepdims=True) + EPS)
            gt = gates_ref[:, cols]
            pieces.append((y * r * bg_ref[:, cols]) * (gt * _sigmoid(gt)))
        merged = jnp.concatenate(pieces, axis=1).astype(BF16)
        out_ref[...] = x_ref[...] + _dot(merged, w_ref[...])

    head_spec = pl.BlockSpec((N_HEADS, tm, HEAD_DIM), lambda i: (0, i, 0))
    return pl.pallas_call(
        body, name="outproj_fwd", grid=(s // tm,),
        in_specs=[pl.BlockSpec((tm, D_MODEL), lambda i: (i, 0)),
                  head_spec, pl.BlockSpec((tm, D_BRANCH), lambda i: (i, 0)), head_spec, head_spec,
                  pl.BlockSpec((tm, D_MODEL), lambda i: (i, 0)),
                  pl.BlockSpec((1, D_MODEL), lambda i: (0, 0)),
                  pl.BlockSpec((D_MODEL, D_MODEL), lambda i: (0, 0))],
        out_specs=pl.BlockSpec((tm, D_MODEL), lambda i: (i, 0)),
        out_shape=jax.ShapeDtypeStruct((s, D_MODEL), F32),
        compiler_params=_params(("arbitrary",)),
    )(x, ya, yb, yc, yd, gates, bg, wout)


def outproj_bwd(dout, ya, yb, yc, yd, gates, bg, wout):
    s = dout.shape[0]
    tm = min(ROW_T, s)

    def body(dout_ref, ya_ref, yb_ref, yc_ref, yd_ref, gates_ref, bg_ref, w_ref,
             dya_ref, dyb_ref, dyc_ref, dyd_ref, dgates_ref, dbg_ref, dw_ref):
        i = pl.program_id(0)

        @pl.when(i == 0)
        def _():
            dbg_ref[...] = jnp.zeros_like(dbg_ref)
            dw_ref[...] = jnp.zeros_like(dw_ref)

        doutb = dout_ref[...].astype(BF16)
        dmerged = _dot_nt(doutb, w_ref[...])
        pieces = []
        for br in range(4):
            cols = slice(br * D_BRANCH, (br + 1) * D_BRANCH)
            y = _branch_inputs((ya_ref, yb_ref, yc_ref, yd_ref), br)
            r = lax.rsqrt(jnp.mean(y * y, axis=-1, keepdims=True) + EPS)
            yn = y * r
            bgv = bg_ref[:, cols]
            gt = gates_ref[:, cols]
            sig = _sigmoid(gt)
            act = gt * sig
            n = yn * bgv
            pieces.append(n * act)
            dm = dmerged[:, cols]
            dn = dm * act
            dgates_ref[:, cols] = (dm * n * (sig * (1.0 + gt * (1.0 - sig)))).astype(BF16)
            dbg_ref[:, cols] += jnp.sum(dn * yn, axis=0, keepdims=True)
            u = dn * bgv
            dy = r * (u - yn * jnp.mean(yn * u, axis=-1, keepdims=True))
            if br == 1:
                dyb_ref[...] = dy
            else:
                dref = (dya_ref, None, dyc_ref, dyd_ref)[br]
                for hh in range(N_HEADS):
                    dref[hh] = dy[:, hh * HEAD_DIM:(hh + 1) * HEAD_DIM].astype(BF16)
        merged = jnp.concatenate(pieces, axis=1).astype(BF16)
        dw_ref[...] += _dot_tn(merged, doutb)

    head_spec = pl.BlockSpec((N_HEADS, tm, HEAD_DIM), lambda i: (0, i, 0))
    head_shape = jax.ShapeDtypeStruct((N_HEADS, s, HEAD_DIM), BF16)
    return pl.pallas_call(
        body, name="outproj_bwd", grid=(s // tm,),
        in_specs=[pl.BlockSpec((tm, D_MODEL), lambda i: (i, 0)),
                  head_spec, pl.BlockSpec((tm, D_BRANCH), lambda i: (i, 0)), head_spec, head_spec,
                  pl.BlockSpec((tm, D_MODEL), lambda i: (i, 0)),
                  pl.BlockSpec((1, D_MODEL), lambda i: (0, 0)),
                  pl.BlockSpec((D_MODEL, D_MODEL), lambda i: (0, 0))],
        out_specs=[head_spec, pl.BlockSpec((tm, D_BRANCH), lambda i: (i, 0)), head_spec, head_spec,
                   pl.BlockSpec((tm, D_MODEL), lambda i: (i, 0)),
                   pl.BlockSpec((1, D_MODEL), lambda i: (0, 0)),
                   pl.BlockSpec((D_MODEL, D_MODEL), lambda i: (0, 0))],
        out_shape=[head_shape, jax.ShapeDtypeStruct((s, D_BRANCH), F32), head_shape, head_shape,
                   jax.ShapeDtypeStruct((s, D_MODEL), BF16),
                   jax.ShapeDtypeStruct((1, D_MODEL), F32),
                   jax.ShapeDtypeStruct((D_MODEL, D_MODEL), F32)],
        compiler_params=_params(("arbitrary",)),
    )(dout, ya, yb, yc, yd, gates, bg, wout)


def final_loss(x, tgt, g):
    s = x.shape[0]
    tm = min(ROW_T, s)

    def body(x_ref, t_ref, g_ref, loss_ref, dx_ref, dg_ref):
        i = pl.program_id(0)

        @pl.when(i == 0)
        def _():
            loss_ref[...] = jnp.zeros_like(loss_ref)
            dg_ref[...] = jnp.zeros_like(dg_ref)

        xv = x_ref[...]
        gv = g_ref[...]
        r = lax.rsqrt(jnp.mean(xv * xv, axis=-1, keepdims=True) + EPS)
        xn = xv * r
        err = xn * gv - t_ref[...]
        loss_ref[...] += jnp.sum(err * err) * (0.5 / D_MODEL)
        dy = err * (1.0 / D_MODEL)
        u = dy * gv
        dx_ref[...] = r * (u - xn * jnp.mean(xn * u, axis=-1, keepdims=True))
        dg_ref[...] += jnp.sum(dy * xn, axis=0, keepdims=True)

    return pl.pallas_call(
        body, name="final_loss", grid=(s // tm,),
        in_specs=[pl.BlockSpec((tm, D_MODEL), lambda i: (i, 0)),
                  pl.BlockSpec((tm, D_MODEL), lambda i: (i, 0)),
                  pl.BlockSpec((1, D_MODEL), lambda i: (0, 0))],
        out_specs=[pl.BlockSpec((1, 128), lambda i: (0, 0)),
                   pl.BlockSpec((tm, D_MODEL), lambda i: (i, 0)),
                   pl.BlockSpec((1, D_MODEL), lambda i: (0, 0))],
        out_shape=[jax.ShapeDtypeStruct((1, 128), F32),
                   jax.ShapeDtypeStruct((s, D_MODEL), F32),
                   jax.ShapeDtypeStruct((1, D_MODEL), F32)],
        compiler_params=_params(("arbitrary",)),
    )(x, tgt, g)


def _rel_index():
    i = np.arange(A_TQ)[:, None]
    j = np.arange(A_BAND)[None, :]
    rel = np.clip(i - j + (A_BAND - A_TQ), -MAX_REL, MAX_REL) + MAX_REL
    dchunk = i // CHUNK + LOOKBACK - j // CHUNK
    valid = (dchunk >= 0) & (dchunk <= LOOKBACK)
    return jnp.asarray(np.where(valid, rel, -1).astype(np.int32))


def _layer_consts(p):
    tbias = relbias_tile(p["rel_bias"], _rel_index())
    return dict(
        norm_g=p["norm_g"].reshape(1, D_MODEL),
        v_gain=p["v_gain"].reshape(1, D_BRANCH),
        b_col=p["b_s"].reshape(N_HEADS, SG_CHUNK, 1),
        bg=p["branch_gain"].reshape(1, D_MODEL),
        tbias=tbias,
    )


def _gate_layout(fp, b_f, s):
    nb = s // 128
    ft = fp[:, :N_HEADS].T.reshape(N_HEADS * nb, 128)
    bcol = jnp.repeat(b_f, nb).reshape(N_HEADS * nb, 1)
    return ft, bcol


def layer_fwd(x, p, ride=()):
    s = x.shape[0]
    c = _layer_consts(p)
    h, qkv, kva, gates, uv, fp = inproj_fwd(x, c["norm_g"], p["wp"])
    ya, lse_a = mix_a_fwd(qkv, kva, c["tbias"])
    yb = mix_b_fwd(uv, c["v_gain"], p["w_s"], c["b_col"])
    ft, bcol = _gate_layout(fp, p["b_f"], s)
    c_row = fox_gate_fwd(ft, bcol).reshape(N_HEADS, s // ATT_T, 1, ATT_T)
    yc, ref_c, rl_c, *rode = fox_fwd(qkv, c_row, ride)
    yd, band_d = sb_fwd(qkv)
    out = outproj_fwd(x, ya, yb, yc, yd, gates, c["bg"], p["wout"])
    saved = dict(consts=c, x=x, h=h, qkv=qkv, gates=gates, uv=uv, kva=kva, ft=ft, bcol=bcol,
                 c_row=c_row, ya=ya, lse_a=lse_a, yb=yb, yc=yc, ref_c=ref_c, rl_c=rl_c, yd=yd, band_d=band_d)
    return out, saved, rode


def _sharded_partials(grads):
    return [grads["w_in_shards"], grads["wout"].astype(BF16).reshape(4, D_BRANCH, D_MODEL)]


def layer_bwd(dout, p, sv, ride=()):
    s = dout.shape[0]
    c = sv["consts"]
    dya, dyb, dyc, dyd, dgates, dbg, dwout = outproj_bwd(
        dout, sv["ya"], sv["yb"], sv["yc"], sv["yd"], sv["gates"], c["bg"], p["wout"])
    dqa, dka, dva, dt = mix_a_bwd(sv["qkv"], sv["kva"], c["tbias"], dya, sv["ya"], sv["lse_a"])
    drel = relbias_grad(dt, _rel_index())[:N_HEADS, :2 * MAX_REL + 1]
    duv, dws, dbs, dvgain = mix_b_bwd(sv["uv"], c["v_gain"], p["w_s"], c["b_col"], dyb)
    dqc, dkc, dvc, dc, *rode = fox_bwd(sv["qkv"], sv["c_row"], dyc, sv["yc"], sv["ref_c"], sv["rl_c"], ride)
    dft, dbf = fox_gate_bwd(sv["ft"], sv["bcol"], dc.reshape(N_HEADS * (s // 128), 128))
    dfp = jnp.pad(dft.reshape(N_HEADS, s).T, ((0, 0), (0, 128 - N_HEADS)))
    dqd, dkd, dvd = sb_bwd(sv["qkv"], dyd, sv["yd"], sv["band_d"])
    dp, dx, dnorm = inproj_bwd((dqa, dka, dva, dqc, dkc, dvc, dqd, dkd, dvd), dgates, duv, dfp,
                               p["wp"], sv["x"], c["norm_g"], dout)
    grads = dict(norm_g=dnorm.reshape(D_MODEL), w_in_shards=inproj_wgrad(sv["h"], dp), b_f=dbf[:N_HEADS, 0], rel_bias=drel,
                 w_s=dws, b_s=dbs.reshape(N_HEADS, SG_CHUNK), v_gain=dvgain.reshape(D_BRANCH),
                 branch_gain=dbg.reshape(4, D_BRANCH), wout=dwout)
    return dx, grads, rode


def local_step(x, tgt, layers, final_g, next_shards=None):
    layers = list(layers)
    saved = []
    cur = x
    for l, p in enumerate(layers):
        ride = next_shards[l] if next_shards is not None and l + 1 < len(layers) else ()
        cur, sv, rode = layer_fwd(cur, p, ride)
        saved.append(sv)
        if ride:
            layers[l + 1] = dict(layers[l + 1], wp=pack_w_in(rode[0][None])[0], wout=rode[1].reshape(D_MODEL, D_MODEL))
    loss, dcur, dfinal = final_loss(cur, tgt, final_g.reshape(1, D_MODEL))
    grads = [None] * len(layers)
    for l in reversed(range(len(layers))):
        ride = _sharded_partials(grads[l + 1]) if next_shards is not None and l + 1 < len(layers) else ()
        dcur, grads[l], rode = layer_bwd(dcur, layers[l], saved[l], ride)
        if ride:
            grads[l + 1]["exchanged"] = rode
    return loss[0, 0], dcur, grads, dfinal.reshape(D_MODEL)


def _chip_gather(pairs, send_sems, recv_sems, loc_sems):
    x, y, c = lax.axis_index("x"), lax.axis_index("y"), lax.axis_index("c")
    me = 2 * x + y
    chips = [(1 - x, y), (x, 1 - y), (1 - x, 1 - y)]
    npair = len(pairs)

    def local():
        return [pltpu.make_async_copy(src, dst(me), loc_sems.at[n]) for n, (src, dst) in enumerate(pairs)]

    def remote(j, n, slot):
        src, dst = pairs[n]
        return pltpu.make_async_remote_copy(
            src_ref=src, dst_ref=dst(slot), send_sem=send_sems.at[npair * j + n], recv_sem=recv_sems.at[npair * j + n],
            device_id=(chips[j][0], chips[j][1], c), device_id_type=MESH)

    def start():
        for cp in local():
            cp.start()
        for j in range(3):
            for n in range(npair):
                remote(j, n, me).start()

    def wait():
        for j in range(3):
            for n in range(npair):
                remote(j, n, 2 * chips[j][0] + chips[j][1]).wait_recv()
        for j in range(3):
            for n in range(npair):
                remote(j, n, me).wait_send()
        for cp in local():
            cp.wait()

    return start, wait


def gather_weights(w_in, w_out, gains):
    depth = w_in.shape[0]

    def body(in_ref, out_ref, g_ref, oin_ref, oout_ref, og_ref, send_sems, recv_sems, loc_sems):
        pairs = [(in_ref, lambda s: oin_ref.at[:, s]), (out_ref, lambda s: oout_ref.at[:, s]), (g_ref, lambda s: og_ref.at[s])]
        start, wait = _chip_gather(pairs, send_sems, recv_sems, loc_sems)
        start()
        wait()

    any_spec = pl.BlockSpec(memory_space=pl.ANY)
    return pl.pallas_call(
        body, name="gather_weights",
        in_specs=[any_spec] * 3, out_specs=[any_spec] * 3,
        out_shape=[jax.ShapeDtypeStruct((depth, 4) + w_in.shape[1:], w_in.dtype),
                   jax.ShapeDtypeStruct((depth, 4) + w_out.shape[1:], w_out.dtype),
                   jax.ShapeDtypeStruct((4,) + gains.shape, gains.dtype)],
        scratch_shapes=[pltpu.SemaphoreType.DMA((9,)), pltpu.SemaphoreType.DMA((9,)), pltpu.SemaphoreType.DMA((3,))],
    )(w_in, w_out, gains)


def pack_w_in(shards):
    depth = shards.shape[0]
    tr = 256

    def body(s_ref, o_ref):
        full = jnp.concatenate([s_ref[n] for n in range(4)], axis=1)
        o_ref[...] = jnp.concatenate([full[:, :SEC_D_Q], full[:, SEC_D_Q + N_HEADS:], full[:, SEC_D_Q:SEC_D_Q + N_HEADS],
                                      jnp.zeros((tr, N_PACK - N_IN), BF16)], axis=1)

    return pl.pallas_call(
        body, name="pack_w_in", grid=(depth, D_MODEL // tr),
        in_specs=[pl.BlockSpec((None, 4, tr, N_SHARD), lambda l, r: (l, 0, r, 0))],
        out_specs=pl.BlockSpec((None, tr, N_PACK), lambda l, r: (l, r, 0)),
        out_shape=jax.ShapeDtypeStruct((depth, D_MODEL, N_PACK), BF16),
        compiler_params=_params(("arbitrary", "arbitrary")),
    )(shards)


def _device_exchange(flows, send_sems, recv_sems, loc_sems):
    x, y, c = lax.axis_index("x"), lax.axis_index("y"), lax.axis_index("c")
    me_chip = 2 * x + y
    me = 4 * x + 2 * y + c
    peers = [(x, y, 1 - c)]
    for px, py in [(1 - x, y), (x, 1 - y), (1 - x, 1 - y)]:
        peers += [(px, py, c), (px, py, 1 - c)]
    nflow = len(flows)

    def local():
        return [pltpu.make_async_copy(src(me_chip), dst(me), loc_sems.at[f]) for f, (src, dst) in enumerate(flows)]

    def copies(n, chip, slot):
        return [pltpu.make_async_remote_copy(src_ref=src(chip), dst_ref=dst(slot), send_sem=send_sems.at[nflow * n + f],
                                             recv_sem=recv_sems.at[nflow * n + f], device_id=peers[n], device_id_type=MESH)
                for f, (src, dst) in enumerate(flows)]

    def start():
        for cp in local():
            cp.start()
        for n, (px, py, _) in enumerate(peers):
            for cp in copies(n, 2 * px + py, me):
                cp.start()

    def wait():
        for n, (px, py, pc) in enumerate(peers):
            for cp in copies(n, me_chip, 4 * px + 2 * py + pc):
                cp.wait_recv()
        for n, (px, py, _) in enumerate(peers):
            for cp in copies(n, 2 * px + py, me):
                cp.wait_send()
        for cp in local():
            cp.wait()

    return start, wait


def _exchange_flows(srcs, dsts):
    return [((lambda s, src=src: src.at[s]) if src.shape[0] == 4 else (lambda s, src=src: src),
             lambda d, dst=dst: dst.at[d]) for src, dst in zip(srcs, dsts)]


def _exchange_shapes(arrays):
    return [jax.ShapeDtypeStruct((8,) + (a.shape[1:] if a.shape[0] == 4 else a.shape), a.dtype) for a in arrays]


def _exchange_sems(n):
    return [pltpu.SemaphoreType.DMA((7 * n,)), pltpu.SemaphoreType.DMA((7 * n,)), pltpu.SemaphoreType.DMA((n,))]


def exchange_grads(*arrays):
    n = len(arrays)

    def body(*refs):
        start, wait = _device_exchange(_exchange_flows(refs[:n], refs[n:2 * n]), *refs[2 * n:])
        start()
        wait()

    any_spec = pl.BlockSpec(memory_space=pl.ANY)
    return pl.pallas_call(
        body, name="exchange_grads",
        in_specs=[any_spec] * n, out_specs=[any_spec] * n, out_shape=_exchange_shapes(arrays),
        scratch_shapes=_exchange_sems(n),
    )(*arrays)


def adamw_reduce(parts, w, m, v, name, tr):
    rows, width = w.shape
    per = rows // len(parts) // tr
    c1 = 1.0 - ADAM_B1 ** ADAM_STEP
    c2 = 1.0 - ADAM_B2 ** ADAM_STEP

    def body(*refs):
        p_refs = refs[:len(parts)]
        w_ref, m_ref, v_ref, g_ref, d_ref, nm_ref, nv_ref = refs[len(parts):]
        i = pl.program_id(0)
        p = p_refs[0][...]
        for n in range(1, len(parts)):
            p = jnp.where(i >= n * per, p_refs[n][...], p)
        g = p[0].astype(F32)
        for n in range(1, 8):
            g = g + p[n].astype(F32)
        g_ref[...] = g
        nm = ADAM_B1 * m_ref[...] + (1.0 - ADAM_B1) * g
        nv = ADAM_B2 * v_ref[...] + (1.0 - ADAM_B2) * (g * g)
        nm_ref[...] = nm
        nv_ref[...] = nv
        d_ref[...] = -ADAM_LR * ((nm / c1) / (jnp.sqrt(nv / c2) + ADAM_EPS) + ADAM_WD * w_ref[...])

    spec = pl.BlockSpec((tr, width), lambda i: (i, 0))
    shape = jax.ShapeDtypeStruct((rows, width), F32)
    return pl.pallas_call(
        body, name=name, grid=(rows // tr,),
        in_specs=[pl.BlockSpec((8, tr, width), lambda i, n=n: (0, jnp.clip(i - n * per, 0, per - 1), 0))
                  for n in range(len(parts))] + [spec, spec, spec],
        out_specs=[spec] * 4, out_shape=[shape] * 4,
        compiler_params=_params(("arbitrary",)),
    )(*parts, w, m, v)


SMALL =("norm_g", "b_f", "rel_bias", "w_s", "b_s", "v_gain", "final_g")
WEIGHTS = ("norm_g", "w_in", "b_f", "rel_bias", "w_s", "b_s", "v_gain", "branch_gain", "w_out", "final_g")
PACK_ROW_TILE = 512


def _rows_of(shape):
    return -(-int(np.prod(shape)) // 128)


def _pack(leaves):
    parts = []
    for a in leaves:
        flat = a.reshape(-1).astype(F32)
        parts.append(jnp.pad(flat, (0, _rows_of(a.shape) * 128 - flat.shape[0])))
    flat = jnp.concatenate(parts)
    rows = flat.shape[0] // 128
    total = -(-rows // PACK_ROW_TILE) * PACK_ROW_TILE
    return jnp.pad(flat, (0, (total - rows) * 128)).reshape(total, 128)


def _unpack(slab, shapes):
    out, row = [], 0
    for shp in shapes:
        n = int(np.prod(shp))
        r = _rows_of(shp)
        out.append(slab[row:row + r].reshape(-1)[:n].reshape(shp))
        row += r
    return out


def kernel(x, norm_g, w_in, b_f, rel_bias, w_s, b_s, v_gain, branch_gain, w_out, final_g, loss_target, m_norm_g, m_w_in, m_b_f, m_rel_bias, m_w_s, m_b_s, m_v_gain, m_branch_gain, m_w_out, m_final_g, v_norm_g, v_w_in, v_b_f, v_rel_bias, v_w_s, v_b_s, v_v_gain, v_branch_gain, v_w_out, v_final_g):
    depth = norm_g.shape[0]
    weights = dict(norm_g=norm_g, w_in=w_in, b_f=b_f, rel_bias=rel_bias, w_s=w_s, b_s=b_s, v_gain=v_gain,
                   branch_gain=branch_gain, w_out=w_out, final_g=final_g)
    mom1 = dict(norm_g=m_norm_g, w_in=m_w_in, b_f=m_b_f, rel_bias=m_rel_bias, w_s=m_w_s, b_s=m_b_s,
                v_gain=m_v_gain, branch_gain=m_branch_gain, w_out=m_w_out, final_g=m_final_g)
    mom2 = dict(norm_g=v_norm_g, w_in=v_w_in, b_f=v_b_f, rel_bias=v_rel_bias, w_s=v_w_s, b_s=v_b_s,
                v_gain=v_v_gain, branch_gain=v_branch_gain, w_out=v_w_out, final_g=v_final_g)

    wf = jnp.pad(branch_gain.reshape(-1), (0, 8 * 128 - branch_gain.size)).reshape(8, 128)
    w_in_b, w_out_b = w_in.astype(BF16), w_out.astype(BF16)
    w_in_shards, w_out_shards, gf = gather_weights(w_in_b[:1], w_out_b[:1], wf)
    bg_full = gf.reshape(4, -1)[:, :branch_gain.size].reshape((4,) + branch_gain.shape)
    bg_full = jnp.moveaxis(bg_full, 0, 2).reshape(depth, 4, D_BRANCH)

    layers = [dict(norm_g=norm_g[l], b_f=b_f[l], rel_bias=rel_bias[l], w_s=w_s[l],
                   b_s=b_s[l], v_gain=v_gain[l], branch_gain=bg_full[l]) for l in range(depth)]
    layers[0].update(wp=pack_w_in(w_in_shards)[0], wout=w_out_shards.reshape(D_MODEL, D_MODEL))
    next_shards = [(w_in_b[l + 1], w_out_b[l + 1]) for l in range(depth - 1)]

    loss_part, grad_x, lgrads, dfinal = local_step(x[0], loss_target[0], layers, final_g, next_shards)
    loss = lax.psum(loss_part, ("x", "y", "c"))

    stack = lambda k: jnp.stack([g[k] for g in lgrads])
    d_gain = jnp.moveaxis(stack("branch_gain").reshape(depth, 4, 4, HEAD_DIM), 2, 0).reshape(4, -1)
    d_gain = jnp.pad(d_gain, ((0, 0), (0, 8 * 128 - d_gain.shape[1]))).reshape(4, 8, 128)
    small = dict(norm_g=stack("norm_g"), b_f=stack("b_f"), rel_bias=stack("rel_bias"), w_s=stack("w_s"),
                 b_s=stack("b_s"), v_gain=stack("v_gain"), final_g=dfinal)
    parts_in, parts_out, parts_gain, parts_small = exchange_grads(*_sharded_partials(lgrads[0]), d_gain,
                                                                  _pack([small[k] for k in SMALL]))
    parts = dict(w_in=[parts_in] + [g["exchanged"][0] for g in lgrads[1:]],
                 w_out=[parts_out] + [g["exchanged"][1] for g in lgrads[1:]])

    outs = {}
    tags = ("grad", "delta", "new_m", "new_v")
    for k in ("w_in", "w_out"):
        rows = depth * weights[k].shape[1]
        flat = lambda a: a.reshape(rows, a.shape[-1])
        res = adamw_reduce(parts[k], flat(weights[k]), flat(mom1[k]), flat(mom2[k]), "adamw_" + k, 256)
        for tag, a in zip(tags, res):
            outs[tag, k] = a.reshape(weights[k].shape)
    gain8 = lambda a: jnp.pad(a.reshape(-1), (0, 8 * 128 - a.size)).reshape(8, 128)
    res = adamw_reduce([parts_gain], gain8(branch_gain), gain8(m_branch_gain), gain8(v_branch_gain), "adamw_gain", 8)
    for tag, a in zip(tags, res):
        outs[tag, "branch_gain"] = a.reshape(-1)[:branch_gain.size].reshape(branch_gain.shape)
    pack_small = lambda d: _pack([d[k] for k in SMALL])
    res = adamw_reduce([parts_small], pack_small(weights), pack_small(mom1), pack_small(mom2), "adamw_small", PACK_ROW_TILE)
    for tag, slab in zip(tags, res):
        for k, a in zip(SMALL, _unpack(slab, [weights[k].shape for k in SMALL])):
            outs[tag, k] = a
    result = [loss, grad_x[None]]
    for tag in ("grad", "delta", "new_m", "new_v"):
        result += [outs[tag, k] for k in WEIGHTS]
    return tuple(result)
```

```python
import functools

import jax
import jax.numpy as jnp
import numpy as np
from jax import lax
from jax.experimental import pallas as pl
from jax.experimental.pallas import tpu as pltpu

F32 = jnp.float32
BF16 = jnp.bfloat16
MESH = pl.DeviceIdType.MESH

D_MODEL = 1024
D_BRANCH = 256
N_HEADS = 4
HEAD_DIM = 64
CHUNK = 64
LOOKBACK = 8
MAX_REL = 128
SG_CHUNK = 128
EPS = 1e-6
N_IN = 3844
N_PACK = 3968
F_COL = 3840
N_SHARD = 961
NEG = -1e30

A_TQ = 128
A_BAND = A_TQ + LOOKBACK * CHUNK
REL_LO = MAX_REL - (CHUNK - 1)
REL_HI = 2 * MAX_REL + 1
A_PAD = LOOKBACK * CHUNK
A_QB = 1024
ATT_T = 256
FOX_TQ = 512
FOX_WIDE = 4
FOX_DEAD2 = -160.0
LOG2E = 1.4426950408889634
SB_SUB = 128
SB_BACK = 256
SB_BAND = SB_SUB + SB_BACK
SB_DEAD = -110.0
ROW_T = 512
VMEM_LIMIT = 56 * 1024 * 1024

ADAM_LR = 0.001
ADAM_B1 = 0.9
ADAM_B2 = 0.999
ADAM_EPS = 1e-08
ADAM_WD = 0.01
ADAM_STEP = 10

SEC_A_Q, SEC_A_K, SEC_A_V, SEC_A_G = 0, 256, 512, 768
SEC_B_U, SEC_B_V, SEC_B_G = 1024, 1280, 1536
SEC_C_Q, SEC_C_K, SEC_C_V, SEC_C_G = 1792, 2048, 2304, 2560
SEC_D_Q, SEC_D_K, SEC_D_V, SEC_D_G = 2816, 3072, 3328, 3584
QKV_SECS = (SEC_A_Q, SEC_C_Q, SEC_C_K, SEC_C_V, SEC_D_Q, SEC_D_K, SEC_D_V)
GATE_SECS = (SEC_A_G, SEC_B_G, SEC_C_G, SEC_D_G)


def _dot(a, b):
    return jnp.dot(a, b, preferred_element_type=F32)


def _dot_nt(a, b):
    return lax.dot_general(a, b, (((1,), (1,)), ((), ())), preferred_element_type=F32)


def _dot_tn(a, b):
    return lax.dot_general(a, b, (((0,), (0,)), ((), ())), preferred_element_type=F32)


def _split2(x):
    hi = x.astype(BF16)
    lo = (x - hi.astype(F32)).astype(BF16)
    return hi, lo


def _split3(x):
    hi = x.astype(BF16)
    r = x - hi.astype(F32)
    mid = r.astype(BF16)
    lo = (r - mid.astype(F32)).astype(BF16)
    return hi, mid, lo


def _sigmoid(x):
    return 1.0 / (1.0 + jnp.exp(-x))


def _params(sem=None, vmem=VMEM_LIMIT):
    return pltpu.CompilerParams(dimension_semantics=sem, vmem_limit_bytes=vmem)


def _heads_to_lanes(ref):
    return jnp.concatenate([ref[h] for h in range(N_HEADS)], axis=1)


def inproj_fwd(x, g, wp):
    s = x.shape[0]
    tm = A_PAD

    def body(x_ref, g_ref, w_ref, h_ref, qkv_ref, kva_ref, gates_ref, uv_ref, f_ref):
        xv = x_ref[...]
        r = lax.rsqrt(jnp.mean(xv * xv, axis=-1, keepdims=True) + EPS)
        h = (xv * r * g_ref[...]).astype(BF16)
        h_ref[...] = h
        for n, off in enumerate(QKV_SECS):
            p = _dot(h, w_ref[:, off:off + D_BRANCH])
            for hh in range(N_HEADS):
                qkv_ref[n, hh] = p[:, hh * HEAD_DIM:(hh + 1) * HEAD_DIM].astype(BF16)
        for n, off in enumerate((SEC_A_K, SEC_A_V)):
            p = _dot(h, w_ref[:, off:off + D_BRANCH])
            for hh in range(N_HEADS):
                kva_ref[n, hh] = p[:, hh * HEAD_DIM:(hh + 1) * HEAD_DIM].astype(BF16)
        for n, off in enumerate(GATE_SECS):
            gates_ref[:, n * D_BRANCH:(n + 1) * D_BRANCH] = _dot(h, w_ref[:, off:off + D_BRANCH])
        uv_ref[...] = _dot(h, w_ref[:, SEC_B_U:SEC_B_U + 2 * D_BRANCH])
        f_ref[...] = _dot(h, w_ref[:, F_COL:F_COL + 128])

    return pl.pallas_call(
        body, name="inproj_fwd", grid=(s // tm,),
        in_specs=[pl.BlockSpec((tm, D_MODEL), lambda i: (i, 0)),
                  pl.BlockSpec((1, D_MODEL), lambda i: (0, 0)),
                  pl.BlockSpec((D_MODEL, N_PACK), lambda i: (0, 0))],
        out_specs=[pl.BlockSpec((tm, D_MODEL), lambda i: (i, 0)),
                   pl.BlockSpec((len(QKV_SECS), N_HEADS, tm, HEAD_DIM), lambda i: (0, 0, i, 0)),
                   pl.BlockSpec((2, N_HEADS, tm, HEAD_DIM), lambda i: (0, 0, i + 1, 0)),
                   pl.BlockSpec((tm, D_MODEL), lambda i: (i, 0)),
                   pl.BlockSpec((tm, 2 * D_BRANCH), lambda i: (i, 0)),
                   pl.BlockSpec((tm, 128), lambda i: (i, 0))],
        out_shape=[jax.ShapeDtypeStruct((s, D_MODEL), BF16),
                   jax.ShapeDtypeStruct((len(QKV_SECS), N_HEADS, s, HEAD_DIM), BF16),
                   jax.ShapeDtypeStruct((2, N_HEADS, s + tm, HEAD_DIM), BF16),
                   jax.ShapeDtypeStruct((s, D_MODEL), F32),
                   jax.ShapeDtypeStruct((s, 2 * D_BRANCH), F32),
                   jax.ShapeDtypeStruct((s, 128), F32)],
        compiler_params=_params(("arbitrary",)),
    )(x, g, wp)


def inproj_bwd(dqkv, dgates, duv, dfp, wp, x, g, dres):
    s = x.shape[0]
    tm = A_PAD

    def body(*refs):
        dq_refs = refs[:9]
        dgates_ref, duv_ref, dfp_ref, w_ref, x_ref, g_ref, dres_ref, dp_ref, dx_ref, dg_ref = refs[9:]
        i = pl.program_id(0)
        a_q, a_k, a_v, c_q, c_k, c_v, d_q, d_k, d_v = [_heads_to_lanes(r).astype(BF16) for r in dq_refs]
        dgt = dgates_ref[...]
        duv_b = duv_ref[...].astype(BF16)
        dp = jnp.concatenate(
            [a_q, a_k, a_v, dgt[:, 0:256], duv_b, dgt[:, 256:512], c_q, c_k, c_v, dgt[:, 512:768],
             d_q, d_k, d_v, dgt[:, 768:1024], dfp_ref[...].astype(BF16)], axis=1)
        dp_ref[...] = dp
        dh = _dot_nt(dp, w_ref[...])
        xv = x_ref[...]
        r = lax.rsqrt(jnp.mean(xv * xv, axis=-1, keepdims=True) + EPS)
        xn = xv * r
        u = dh * g_ref[...]
        dx_ref[...] = dres_ref[...] + r * (u - xn * jnp.mean(xn * u, axis=-1, keepdims=True))

        @pl.when(i == 0)
        def _():
            dg_ref[...] = jnp.zeros_like(dg_ref)

        dg_ref[...] += jnp.sum(dh * xn, axis=0, keepdims=True)

    head_spec = pl.BlockSpec((N_HEADS, tm, HEAD_DIM), lambda i: (0, i, 0))
    padded_spec = pl.BlockSpec((N_HEADS, tm, HEAD_DIM), lambda i: (0, i + 1, 0))
    return pl.pallas_call(
        body, name="inproj_bwd", grid=(s // tm,),
        in_specs=[head_spec, padded_spec, padded_spec] + [head_spec] * 6 + [
            pl.BlockSpec((tm, D_MODEL), lambda i: (i, 0)),
            pl.BlockSpec((tm, 2 * D_BRANCH), lambda i: (i, 0)),
            pl.BlockSpec((tm, 128), lambda i: (i, 0)),
            pl.BlockSpec((D_MODEL, N_PACK), lambda i: (0, 0)),
            pl.BlockSpec((tm, D_MODEL), lambda i: (i, 0)),
            pl.BlockSpec((1, D_MODEL), lambda i: (0, 0)),
            pl.BlockSpec((tm, D_MODEL), lambda i: (i, 0))],
        out_specs=[pl.BlockSpec((tm, N_PACK), lambda i: (i, 0)),
                   pl.BlockSpec((tm, D_MODEL), lambda i: (i, 0)),
                   pl.BlockSpec((1, D_MODEL), lambda i: (0, 0))],
        out_shape=[jax.ShapeDtypeStruct((s, N_PACK), BF16),
                   jax.ShapeDtypeStruct((s, D_MODEL), F32),
                   jax.ShapeDtypeStruct((1, D_MODEL), F32)],
        compiler_params=_params(("arbitrary",)),
    )(*dqkv, dgates, duv, dfp, wp, x, g, dres)


def inproj_wgrad(h, dp):
    s, m = h.shape
    tm = min(2 * ROW_T, s)
    tmm = 256
    nsteps = s // tm

    def body(a_ref, b_ref, o_ref, acc_ref):
        k = pl.program_id(1)

        @pl.when(k == 0)
        def _():
            acc_ref[...] = jnp.zeros_like(acc_ref)

        acc_ref[...] += _dot_tn(a_ref[...], b_ref[...])

        @pl.when(k == nsteps - 1)
        def _():
            acc = acc_ref[...]
            full = jnp.concatenate([acc[:, :SEC_D_Q], acc[:, F_COL:F_COL + N_HEADS], acc[:, SEC_D_Q:F_COL]], axis=1)
            for n in range(4):
                o_ref[n] = full[:, n * N_SHARD:(n + 1) * N_SHARD].astype(BF16)

    return pl.pallas_call(
        body, name="inproj_wgrad", grid=(m // tmm, nsteps),
        in_specs=[pl.BlockSpec((tm, tmm), lambda j, k: (k, j)),
                  pl.BlockSpec((tm, N_PACK), lambda j, k: (k, 0))],
        out_specs=pl.BlockSpec((4, tmm, N_SHARD), lambda j, k: (0, j, 0)),
        out_shape=jax.ShapeDtypeStruct((4, m, N_SHARD), BF16),
        scratch_shapes=[pltpu.VMEM((tmm, N_PACK), F32)],
        compiler_params=_params(("arbitrary", "arbitrary")),
    )(h, dp)


def _a_specs(s):
    nq = s // A_QB
    per = A_QB // A_PAD
    q_spec = pl.BlockSpec((None, None, A_QB, HEAD_DIM), lambda h, i: (0, h, jnp.minimum(i, nq - 1), 0))
    kv_specs = [pl.BlockSpec((None, None, A_PAD, HEAD_DIM),
                             lambda h, i, n=n, m=m: (n, h, jnp.minimum(per * i + m, per * nq), 0))
                for n in range(2) for m in range(per + 1)]
    t_spec = pl.BlockSpec((None, A_TQ, A_BAND), lambda h, i: (h, 0, 0))
    return nq, q_spec, kv_specs, t_spec


def _a_window(refs, i):
    first = refs[0][...]
    return jnp.concatenate([jnp.where(i > 0, first, jnp.zeros_like(first))] + [r[...] for r in refs[1:]], axis=0)


def _a_scores(q_ref, k, t_ref, i, j):
    rows = slice(j * A_TQ, (j + 1) * A_TQ)
    qs = q_ref[rows, :] * 0.125
    kj = k[j * A_TQ:j * A_TQ + A_BAND, :]
    sc = _dot_nt(qs, kj) + t_ref[...]
    col = lax.broadcasted_iota(jnp.int32, (A_TQ, A_BAND), 1)
    sc = jnp.where(col >= A_PAD - i * A_QB - j * A_TQ, sc, NEG)
    return rows, qs, kj, sc


def mix_a_fwd(qkv, kva, tbias):
    s = qkv.shape[2]
    nq, q_spec, kv_specs, t_spec = _a_specs(s)
    nwin = len(kv_specs) // 2

    def body(*refs):
        q_ref, t_ref, o_ref, lse_ref = refs[0], refs[1 + 2 * nwin], refs[2 + 2 * nwin], refs[3 + 2 * nwin]
        i = pl.program_id(1)
        k = _a_window(refs[1:1 + nwin], i)
        v = _a_window(refs[1 + nwin:1 + 2 * nwin], i)
        for j in range(A_QB // A_TQ):
            rows, _, _, sc = _a_scores(q_ref, k, t_ref, i, j)
            m = jnp.max(sc, axis=-1, keepdims=True)
            p = jnp.exp(sc - m)
            l = jnp.sum(p, axis=-1, keepdims=True)
            o_ref[rows, :] = _dot(p.astype(BF16), v[j * A_TQ:j * A_TQ + A_BAND, :]) / l
            lse_ref[rows, :] = m + jnp.log(l)

    return pl.pallas_call(
        body, name="mix_a_fwd", grid=(N_HEADS, nq),
        in_specs=[q_spec] + kv_specs + [t_spec],
        out_specs=[pl.BlockSpec((None, A_QB, HEAD_DIM), lambda h, i: (h, i, 0)),
                   pl.BlockSpec((None, A_QB, 1), lambda h, i: (h, i, 0))],
        out_shape=[jax.ShapeDtypeStruct((N_HEADS, s, HEAD_DIM), F32),
                   jax.ShapeDtypeStruct((N_HEADS, s, 1), F32)],
        compiler_params=_params(("arbitrary", "arbitrary")),
    )(qkv, *([kva] * (2 * nwin)), tbias)


def mix_a_bwd(qkv, kva, tbias, do, o, lse):
    s = qkv.shape[2]
    nq, q_spec, kv_specs, t_spec = _a_specs(s)
    nwin = len(kv_specs) // 2
    row_spec = lambda w: pl.BlockSpec((None, A_QB, w), lambda h, i: (h, jnp.minimum(i, nq - 1), 0))
    done_spec = pl.BlockSpec((None, A_QB, HEAD_DIM), lambda h, i: (h, i, 0))
    win = A_QB + A_PAD

    def body(*refs):
        q_ref = refs[0]
        t_ref, do_ref, o_ref, lse_ref, dq_ref, dk_ref, dv_ref, dt_ref, dk_win, dv_win = refs[1 + 2 * nwin:]
        i = pl.program_id(1)

        @pl.when(i == 0)
        def _():
            dk_win[...] = jnp.zeros_like(dk_win)
            dv_win[...] = jnp.zeros_like(dv_win)
            dt_ref[...] = jnp.zeros_like(dt_ref)

        @pl.when(i < nq)
        def _():
            k = _a_window(refs[1:1 + nwin], i)
            v = _a_window(refs[1 + nwin:1 + 2 * nwin], i)
            dt = jnp.zeros((A_TQ, A_BAND), F32)
            for j in range(A_QB // A_TQ):
                rows, qs, kj, sc = _a_scores(q_ref, k, t_ref, i, j)
                keys = slice(j * A_TQ, j * A_TQ + A_BAND)
                dob = do_ref[rows, :]
                p = jnp.exp(sc - lse_ref[rows, :])
                delta = jnp.sum(o_ref[rows, :] * dob.astype(F32), axis=-1, keepdims=True)
                ds = p * (_dot_nt(dob, v[keys, :]) - delta)
                dsb = ds.astype(BF16)
                dq_ref[rows, :] = _dot(dsb, kj) * 0.125
                dk_win[keys, :] += _dot_tn(dsb, qs)
                dv_win[keys, :] += _dot_tn(p.astype(BF16), dob)
                dt = dt + ds
            dt_ref[...] += dt

        dk_ref[...] = dk_win[0:A_QB, :]
        dv_ref[...] = dv_win[0:A_QB, :]
        dk_rest = dk_win[A_QB:win, :]
        dv_rest = dv_win[A_QB:win, :]
        dk_win[0:A_PAD, :] = dk_rest
        dv_win[0:A_PAD, :] = dv_rest
        dk_win[A_PAD:win, :] = jnp.zeros((A_QB, HEAD_DIM), F32)
        dv_win[A_PAD:win, :] = jnp.zeros((A_QB, HEAD_DIM), F32)

    return pl.pallas_call(
        body, name="mix_a_bwd", grid=(N_HEADS, nq + 1),
        in_specs=[q_spec] + kv_specs + [t_spec, row_spec(HEAD_DIM), row_spec(HEAD_DIM), row_spec(1)],
        out_specs=[row_spec(HEAD_DIM), done_spec, done_spec, t_spec],
        out_shape=[jax.ShapeDtypeStruct((N_HEADS, s, HEAD_DIM), F32),
                   jax.ShapeDtypeStruct((N_HEADS, s + A_QB, HEAD_DIM), F32),
                   jax.ShapeDtypeStruct((N_HEADS, s + A_QB, HEAD_DIM), F32),
                   jax.ShapeDtypeStruct((N_HEADS, A_TQ, A_BAND), F32)],
        scratch_shapes=[pltpu.VMEM((win, HEAD_DIM), F32), pltpu.VMEM((win, HEAD_DIM), F32)],
        compiler_params=_params(("arbitrary", "arbitrary")),
    )(qkv, *([kva] * (2 * nwin)), tbias, do, o, lse)


def relbias_tile(rel_bias, relmat):
    def body(rb_ref, rel_ref, o_ref):
        rel = rel_ref[...]
        o_ref[...] = jnp.full(o_ref.shape, NEG, F32)

        def step(r, carry):
            hit = rel == r
            for h in range(N_HEADS):
                o_ref[h] = jnp.where(hit, rb_ref[h, r], o_ref[h])
            return carry

        lax.fori_loop(REL_LO, REL_HI, step, 0)

    return pl.pallas_call(
        body, name="relbias_tile",
        in_specs=[pl.BlockSpec(memory_space=pltpu.SMEM), pl.BlockSpec(memory_space=pltpu.VMEM)],
        out_specs=pl.BlockSpec(memory_space=pltpu.VMEM),
        out_shape=jax.ShapeDtypeStruct((N_HEADS, A_TQ, A_BAND), F32),
        compiler_params=_params(),
    )(rel_bias, relmat)


def relbias_grad(dt, relmat):
    def body(dt_ref, rel_ref, o_ref):
        rel = rel_ref[...]
        lane = lax.broadcasted_iota(jnp.int32, (8, 384), 1)
        row = lax.broadcasted_iota(jnp.int32, (8, 384), 0)

        def step(r, acc):
            hit = rel == r
            for h in range(N_HEADS):
                val = jnp.sum(jnp.where(hit, dt_ref[h], 0.0))
                acc = jnp.where((lane == r) & (row == h), val, acc)
            return acc

        o_ref[...] = lax.fori_loop(REL_LO, REL_HI, step, jnp.zeros((8, 384), F32))

    return pl.pallas_call(
        body, name="relbias_grad",
        out_shape=jax.ShapeDtypeStruct((8, 384), F32),
        compiler_params=_params(),
    )(dt, relmat)


def _b_norm(v, gain):
    mu = jnp.mean(v, axis=-1, keepdims=True)
    xc = v - mu
    rstd = lax.rsqrt(jnp.mean(xc * xc, axis=-1, keepdims=True) + EPS)
    xhat = xc * rstd
    return xhat, rstd, xhat * gain


def _tril_mask():
    t = lax.broadcasted_iota(jnp.int32, (SG_CHUNK, SG_CHUNK), 0)
    u = lax.broadcasted_iota(jnp.int32, (SG_CHUNK, SG_CHUNK), 1)
    return u <= t


def mix_b_fwd(uv, gain, w_s, b_col):
    s = uv.shape[0]
    tm = min(ROW_T, s)

    def body(uv_ref, gain_ref, w_ref, b_ref, y_ref):
        tril = _tril_mask()
        ws = [jnp.where(tril, w_ref[g], 0.0).astype(BF16) for g in range(N_HEADS)]
        for c in range(tm // SG_CHUNK):
            rows = slice(c * SG_CHUNK, (c + 1) * SG_CHUNK)
            u = uv_ref[rows, 0:D_BRANCH]
            _, _, vn = _b_norm(uv_ref[rows, D_BRANCH:2 * D_BRANCH], gain_ref[...])
            vnb = vn.astype(BF16)
            outs = []
            for g in range(N_HEADS):
                cols = slice(g * HEAD_DIM, (g + 1) * HEAD_DIM)
                mixed = _dot(ws[g], vnb[:, cols]) + b_ref[g]
                outs.append(u[:, cols] * mixed)
            y_ref[rows, :] = jnp.concatenate(outs, axis=1)

    return pl.pallas_call(
        body, name="mix_b_fwd", grid=(s // tm,),
        in_specs=[pl.BlockSpec((tm, 2 * D_BRANCH), lambda i: (i, 0)),
                  pl.BlockSpec((1, D_BRANCH), lambda i: (0, 0)),
                  pl.BlockSpec((N_HEADS, SG_CHUNK, SG_CHUNK), lambda i: (0, 0, 0)),
                  pl.BlockSpec((N_HEADS, SG_CHUNK, 1), lambda i: (0, 0, 0))],
        out_specs=pl.BlockSpec((tm, D_BRANCH), lambda i: (i, 0)),
        out_shape=jax.ShapeDtypeStruct((s, D_BRANCH), F32),
        compiler_params=_params(("arbitrary",)),
    )(uv, gain, w_s, b_col)


def mix_b_bwd(uv, gain, w_s, b_col, dy):
    s = uv.shape[0]
    tm = min(ROW_T, s)

    def body(uv_ref, gain_ref, w_ref, b_ref, dy_ref, duv_ref, dw_ref, db_ref, dgain_ref):
        i = pl.program_id(0)

        @pl.when(i == 0)
        def _():
            dw_ref[...] = jnp.zeros_like(dw_ref)
            db_ref[...] = jnp.zeros_like(db_ref)
            dgain_ref[...] = jnp.zeros_like(dgain_ref)

        tril = _tril_mask()
        ws = [jnp.where(tril, w_ref[g], 0.0).astype(BF16) for g in range(N_HEADS)]
        gain_v = gain_ref[...]
        for c in range(tm // SG_CHUNK):
            rows = slice(c * SG_CHUNK, (c + 1) * SG_CHUNK)
            u = uv_ref[rows, 0:D_BRANCH]
            xhat, rstd, vn = _b_norm(uv_ref[rows, D_BRANCH:2 * D_BRANCH], gain_v)
            vnb = vn.astype(BF16)
            dyv = dy_ref[rows, :]
            dus, dvns = [], []
            for g in range(N_HEADS):
                cols = slice(g * HEAD_DIM, (g + 1) * HEAD_DIM)
                mixed = _dot(ws[g], vnb[:, cols]) + b_ref[g]
                dus.append(dyv[:, cols] * mixed)
                dmixed = dyv[:, cols] * u[:, cols]
                dmb = dmixed.astype(BF16)
                db_ref[g] += jnp.sum(dmixed, axis=-1, keepdims=True)
                dw_ref[g] += jnp.where(tril, _dot_nt(dmb, vnb[:, cols]), 0.0)
                dvns.append(_dot_tn(ws[g], dmb))
            dvn = jnp.concatenate(dvns, axis=1)
            dgain_ref[...] += jnp.sum(dvn * xhat, axis=0, keepdims=True)
            dxh = dvn * gain_v
            dv = rstd * (dxh - jnp.mean(dxh, axis=-1, keepdims=True)
                         - xhat * jnp.mean(dxh * xhat, axis=-1, keepdims=True))
            duv_ref[rows, :] = jnp.concatenate(dus + [dv], axis=1)

    return pl.pallas_call(
        body, name="mix_b_bwd", grid=(s // tm,),
        in_specs=[pl.BlockSpec((tm, 2 * D_BRANCH), lambda i: (i, 0)),
                  pl.BlockSpec((1, D_BRANCH), lambda i: (0, 0)),
                  pl.BlockSpec((N_HEADS, SG_CHUNK, SG_CHUNK), lambda i: (0, 0, 0)),
                  pl.BlockSpec((N_HEADS, SG_CHUNK, 1), lambda i: (0, 0, 0)),
                  pl.BlockSpec((tm, D_BRANCH), lambda i: (i, 0))],
        out_specs=[pl.BlockSpec((tm, 2 * D_BRANCH), lambda i: (i, 0)),
                   pl.BlockSpec((N_HEADS, SG_CHUNK, SG_CHUNK), lambda i: (0, 0, 0)),
                   pl.BlockSpec((N_HEADS, SG_CHUNK, 1), lambda i: (0, 0, 0)),
                   pl.BlockSpec((1, D_BRANCH), lambda i: (0, 0))],
        out_shape=[jax.ShapeDtypeStruct((s, 2 * D_BRANCH), F32),
                   jax.ShapeDtypeStruct((N_HEADS, SG_CHUNK, SG_CHUNK), F32),
                   jax.ShapeDtypeStruct((N_HEADS, SG_CHUNK, 1), F32),
                   jax.ShapeDtypeStruct((1, D_BRANCH), F32)],
        compiler_params=_params(("arbitrary",)),
    )(uv, gain, w_s, b_col, dy)


def _scan_mats(nrow):
    a = lax.broadcasted_iota(jnp.int32, (128, 128), 0)
    b = lax.broadcasted_iota(jnp.int32, (128, 128), 1)
    r = lax.broadcasted_iota(jnp.int32, (nrow, nrow), 0)
    c = lax.broadcasted_iota(jnp.int32, (nrow, nrow), 1)
    nb = nrow // N_HEADS
    same = (r // nb) == (c // nb)
    return a, b, r, c, same


def _exact_dot(x, m):
    hi, mid, lo = _split3(x)
    return _dot(hi, m) + _dot(mid, m) + _dot(lo, m)


def _exact_dot_left(m, x):
    hi, mid, lo = _split3(x)
    return _dot(m, hi) + _dot(m, mid) + _dot(m, lo)


def fox_gate_fwd(ft, bcol):
    nrow = ft.shape[0]

    def body(f_ref, b_ref, c_ref):
        z = f_ref[...] + b_ref[...]
        ls = jnp.minimum(z, 0.0) - jnp.log(1.0 + jnp.exp(-jnp.abs(z)))
        a, b, r, c, same = _scan_mats(nrow)
        within = _exact_dot(ls, (a <= b).astype(BF16))
        tot = jnp.broadcast_to(within[:, 127:128], within.shape)
        before = _exact_dot_left((same & (c < r)).astype(BF16), tot)
        c_ref[...] = within + before

    return pl.pallas_call(
        body, name="fox_gate_fwd",
        out_shape=jax.ShapeDtypeStruct((nrow, 128), F32),
        compiler_params=_params(),
    )(ft, bcol)


def fox_gate_bwd(ft, bcol, dc):
    nrow = ft.shape[0]

    def body(f_ref, b_ref, dc_ref, df_ref, db_ref):
        a, b, r, c, same = _scan_mats(nrow)
        dcv = dc_ref[...]
        within = _exact_dot(dcv, (a >= b).astype(BF16))
        tot = jnp.broadcast_to(within[:, 0:1], within.shape)
        after = _exact_dot_left((same & (c > r)).astype(BF16), tot)
        dls = within + after
        z = f_ref[...] + b_ref[...]
        dz = dls * _sigmoid(-z)
        df_ref[...] = dz
        rs = jnp.broadcast_to(jnp.sum(dz, axis=-1, keepdims=True), dz.shape)
        hr = lax.broadcasted_iota(jnp.int32, (8, nrow), 0)
        hc = lax.broadcasted_iota(jnp.int32, (8, nrow), 1)
        db_ref[...] = _exact_dot_left((hr == hc // (nrow // N_HEADS)).astype(BF16), rs)

    return pl.pallas_call(
        body, name="fox_gate_bwd",
        out_shape=[jax.ShapeDtypeStruct((nrow, 128), F32), jax.ShapeDtypeStruct((8, 128), F32)],
        compiler_params=_params(),
    )(ft, bcol, dc)


def _att_specs(s, qi, ki, vi):
    q_spec = pl.BlockSpec((None, None, FOX_TQ, HEAD_DIM), lambda h, i: (qi, h, i, 0))
    k_spec = pl.BlockSpec((None, None, s, HEAD_DIM), lambda h, i: (ki, h, 0, 0))
    v_spec = pl.BlockSpec((None, None, s, HEAD_DIM), lambda h, i: (vi, h, 0, 0))
    row_spec = lambda w: pl.BlockSpec((None, FOX_TQ, w), lambda h, i: (h, i, 0))
    gate_spec = pl.BlockSpec((None, s // ATT_T, 1, ATT_T), lambda h, i: (h, 0, 0, 0))
    return q_spec, k_spec, v_spec, row_spec, gate_spec


def _causal(strict, n=ATT_T):
    row = lax.broadcasted_iota(jnp.int32, (n, n), 0)
    col = lax.broadcasted_iota(jnp.int32, (n, n), 1)
    return (col < row) if strict else (col <= row)


def _gate_row(cr_ref, kb, g):
    if g == 1:
        return cr_ref[kb]
    return jnp.concatenate([cr_ref[kb + n] for n in range(g)], axis=1)


def _fox_walk(i, carry, tile, alive):
    g = FOX_WIDE
    own = FOX_TQ // ATT_T
    nwide = (own * i) // g
    carry = tile(own * i, own, carry, True)
    carry = lax.fori_loop(0, (own * i - nwide * g) // own, lambda n, c: tile(nwide * g, own, c, False), carry)

    def cond(state):
        return jnp.logical_and(state[0] >= 0, state[1] > 0)

    def step(state):
        n = state[0]
        c = tile(n * g, g, state[2:], False)
        return (n - 1, alive(n * g, c)) + tuple(c)

    out = lax.while_loop(cond, step, (nwide - 1, alive(nwide * g, carry)) + tuple(carry))
    return out[2:]


def _fox_reach(qs, k_ref, kmax_ref, cc, i):
    s = k_ref.shape[0]
    rows = 4 * ATT_T

    @pl.when(i == 0)
    def _():
        def chunk(n, mx):
            kc = k_ref[pl.ds(pl.multiple_of(n * rows, rows), rows), :].astype(F32)
            return jnp.maximum(mx, jnp.max(jnp.sum(kc * kc, axis=-1, keepdims=True)))

        kmax_ref[0] = jnp.sqrt(lax.fori_loop(0, s // rows, chunk, jnp.float32(0.0)))

    qf = qs.astype(F32)
    return jnp.sqrt(jnp.sum(qf * qf, axis=-1, keepdims=True)) * kmax_ref[0] + cc


def _gate_col(cr_ref, i):
    row = lax.broadcasted_iota(jnp.int32, (ATT_T, ATT_T), 0)
    col = lax.broadcasted_iota(jnp.int32, (ATT_T, ATT_T), 1)
    own = FOX_TQ // ATT_T
    return jnp.concatenate([jnp.sum(jnp.where(row == col, cr_ref[own * i + n], 0.0), axis=-1, keepdims=True)
                            for n in range(own)], axis=0)


def _fox_scores(qs, k, cc, crow, masked):
    sc = (_dot_nt(qs, k) + (cc - crow)) * LOG2E
    if masked:
        sc = jnp.where(_causal(False, FOX_TQ), sc, NEG)
    return sc


def fox_fwd(qkv, c_row, ride=()):
    s = qkv.shape[2]
    t = ATT_T
    nq = s // FOX_TQ
    q_spec, k_spec, v_spec, row_spec, gate_spec = _att_specs(s, 1, 2, 3)
    rows = 4 * t
    nride = len(ride)

    def body(q_ref, k_ref, v_ref, cr_ref, *refs):
        ride_in, refs = refs[:nride], refs[nride:]
        o_ref, ref_ref, rl_ref = refs[:3]
        ride_out, refs = refs[3:3 + nride], refs[3 + nride:]
        v1_ref, kmax_ref = refs[:2]
        i = pl.program_id(1)
        if nride:
            h = pl.program_id(0)
            start, wait = _chip_gather([(src, lambda slot, dst=dst: dst.at[slot]) for src, dst in zip(ride_in, ride_out)],
                                       *refs[2:])
            pl.when(jnp.logical_and(h == 0, i == 0))(start)

        @pl.when(i == 0)
        def _():
            def chunk(n, carry):
                r0 = pl.multiple_of(n * rows, rows)
                v1_ref[pl.ds(r0, rows), :] = jnp.concatenate(
                    [v_ref[pl.ds(r0, rows), :], jnp.ones((rows, HEAD_DIM), BF16)], axis=1)
                return carry

            lax.fori_loop(0, s // rows, chunk, 0)

        qs = q_ref[...] * 0.125
        cc = _gate_col(cr_ref, i)
        reach = _fox_reach(qs, k_ref, kmax_ref, cc, i) * LOG2E

        def alive(kb, carry):
            return (jnp.max(reach - cr_ref[kb][:, 0:1] * LOG2E - carry[0]) > FOX_DEAD2).astype(jnp.int32)

        def tile(kb, g, carry, masked):
            m, acc = carry
            k0 = pl.multiple_of(kb * t, t)
            sc = _fox_scores(qs, k_ref[pl.ds(k0, g * t), :], cc, _gate_row(cr_ref, kb, g), masked)
            m_new = jnp.maximum(m, jnp.ceil(jnp.max(sc, axis=-1, keepdims=True)))
            pb = jnp.exp2(sc - m_new).astype(BF16)
            acc = jnp.exp2(m - m_new) * acc + _dot(pb, v1_ref[pl.ds(k0, g * t), :])
            return m_new, acc

        init = (jnp.full((FOX_TQ, 1), NEG, F32), jnp.zeros((FOX_TQ, 2 * HEAD_DIM), F32))
        m, acc = _fox_walk(i, init, tile, alive)
        rl = 1.0 / acc[:, HEAD_DIM:HEAD_DIM + 1]
        o_ref[...] = acc[:, 0:HEAD_DIM] * rl
        ref_ref[...] = m
        rl_ref[...] = rl
        if nride:
            pl.when(jnp.logical_and(h == N_HEADS - 1, i == nq - 1))(wait)

    any_spec = pl.BlockSpec(memory_space=pl.ANY)
    ride_sems = [pltpu.SemaphoreType.DMA((3 * nride,)), pltpu.SemaphoreType.DMA((3 * nride,)),
                 pltpu.SemaphoreType.DMA((nride,))] if nride else []
    return pl.pallas_call(
        body, name="fox_fwd_gather" if nride else "fox_fwd", grid=(N_HEADS, nq),
        in_specs=[q_spec, k_spec, v_spec, gate_spec] + [any_spec] * nride,
        out_specs=[row_spec(HEAD_DIM), row_spec(1), row_spec(1)] + [any_spec] * nride,
        out_shape=[jax.ShapeDtypeStruct((N_HEADS, s, HEAD_DIM), F32),
                   jax.ShapeDtypeStruct((N_HEADS, s, 1), F32),
                   jax.ShapeDtypeStruct((N_HEADS, s, 1), F32)]
        + [jax.ShapeDtypeStruct((4,) + a.shape, a.dtype) for a in ride],
        scratch_shapes=[pltpu.VMEM((s, 2 * HEAD_DIM), BF16), pltpu.SMEM((1,), F32)] + ride_sems,
        compiler_params=_params(("arbitrary", "arbitrary")),
    )(qkv, qkv, qkv, c_row, *ride)


def fox_bwd(qkv, c_row, do, o, ref, rl, ride=()):
    s = qkv.shape[2]
    t = ATT_T
    nq = s // FOX_TQ
    q_spec, k_spec, v_spec, row_spec, gate_spec = _att_specs(s, 1, 2, 3)
    any_spec = pl.BlockSpec(memory_space=pl.ANY)
    nride = len(ride)

    def body(q_ref, k_ref, v_ref, cr_ref, do_ref, o_ref, ref_ref, rl_ref, *refs):
        ride_in, refs = refs[:nride], refs[nride:]
        dq_ref, dk_hbm, dv_hbm, dc_ref = refs[:4]
        ride_out, refs = refs[4:4 + nride], refs[4 + nride:]
        dk_acc, dv_acc, kmax_ref = refs[:3]
        h = pl.program_id(0)
        i = pl.program_id(1)
        if nride:
            start, wait = _device_exchange(_exchange_flows(ride_in, ride_out), *refs[3:])
            pl.when(jnp.logical_and(h == 0, i == 0))(start)

        @pl.when(i == 0)
        def _():
            dk_acc[...] = jnp.zeros_like(dk_acc)
            dv_acc[...] = jnp.zeros_like(dv_acc)
            dc_ref[...] = jnp.zeros_like(dc_ref)

        qs = q_ref[...] * 0.125
        ref = ref_ref[...]
        rl = rl_ref[...]
        dob = (do_ref[...].astype(F32) * rl).astype(BF16)
        delta = jnp.sum(o_ref[...] * dob.astype(F32), axis=-1, keepdims=True)
        cc = _gate_col(cr_ref, i)
        margin = _fox_reach(qs, k_ref, kmax_ref, cc, i) * LOG2E - ref

        def alive(kb, carry):
            return (jnp.max(margin - cr_ref[kb][:, 0:1] * LOG2E) > FOX_DEAD2).astype(jnp.int32)

        def tile(kb, g, carry, masked):
            dq, = carry
            k0 = pl.multiple_of(kb * t, t)
            k = k_ref[pl.ds(k0, g * t), :]
            sc = _fox_scores(qs, k, cc, _gate_row(cr_ref, kb, g), masked)
            wb = jnp.exp2(sc - ref).astype(BF16)
            ds = wb.astype(F32) * (_dot_nt(dob, v_ref[pl.ds(k0, g * t), :]) - delta)
            dsb = ds.astype(BF16)
            dk_acc[pl.ds(k0, g * t), :] += _dot_tn(dsb, qs)
            dv_acc[pl.ds(k0, g * t), :] += _dot_tn(wb, dob)
            dcs = -jnp.sum(ds, axis=0, keepdims=True)
            for n in range(g):
                dc_ref[kb + n] += dcs[:, n * t:(n + 1) * t]
            return (dq + _dot(dsb, k),)

        dq, = _fox_walk(i, (jnp.zeros((FOX_TQ, HEAD_DIM), F32),), tile, alive)
        dq_ref[...] = dq * 0.125

        @pl.when(i == nq - 1)
        def _():
            pltpu.sync_copy(dk_acc, dk_hbm.at[h])
            pltpu.sync_copy(dv_acc, dv_hbm.at[h])

        if nride:
            pl.when(jnp.logical_and(h == N_HEADS - 1, i == nq - 1))(wait)

    return pl.pallas_call(
        body, name="fox_bwd_exchange" if nride else "fox_bwd", grid=(N_HEADS, nq),
        in_specs=[q_spec, k_spec, v_spec,
                  gate_spec,
                  row_spec(HEAD_DIM), row_spec(HEAD_DIM), row_spec(1), row_spec(1)] + [any_spec] * nride,
        out_specs=[row_spec(HEAD_DIM), any_spec, any_spec,
                   gate_spec] + [any_spec] * nride,
        out_shape=[jax.ShapeDtypeStruct((N_HEADS, s, HEAD_DIM), F32),
                   jax.ShapeDtypeStruct((N_HEADS, s, HEAD_DIM), F32),
                   jax.ShapeDtypeStruct((N_HEADS, s, HEAD_DIM), F32),
                   jax.ShapeDtypeStruct((N_HEADS, s // t, 1, t), F32)] + _exchange_shapes(ride),
        scratch_shapes=[pltpu.VMEM((s, HEAD_DIM), F32), pltpu.VMEM((s, HEAD_DIM), F32), pltpu.SMEM((1,), F32)]
        + (_exchange_sems(nride) if nride else []),
        compiler_params=_params(("arbitrary", "arbitrary")),
    )(qkv, qkv, qkv, c_row, do, o, ref, rl, *ride)


def _sb_valid(nrows, ahead):
    row = lax.broadcasted_iota(jnp.int32, (nrows, ATT_T), 0)
    col = lax.broadcasted_iota(jnp.int32, (nrows, ATT_T), 1)
    return col + ahead < row


def _sb_band_valid(nsub):
    row = lax.broadcasted_iota(jnp.int32, (nsub * SB_SUB, SB_BAND), 0)
    col = lax.broadcasted_iota(jnp.int32, (nsub * SB_SUB, SB_BAND), 1)
    return col < (row & (SB_SUB - 1)) + SB_BACK


def _sb_logits(qs, k):
    z = _dot_nt(qs, k)
    sp = jnp.log(1.0 + jnp.exp(-jnp.abs(z)))
    return jnp.minimum(z, 0.0) - sp, -jnp.maximum(z, 0.0) - sp


def _sb_weights(ls, lm, run, valid):
    if valid is not None:
        lm = jnp.where(valid, lm, 0.0)
    n = lm.shape[1]
    row = lax.broadcasted_iota(jnp.int32, (n, n), 0)
    col = lax.broadcasted_iota(jnp.int32, (n, n), 1)
    later = (row > col).astype(BF16)
    hi, lo = _split2(lm)
    between = _dot(hi, later) + _dot(lo, later)
    if run is not None:
        between = run + between
    a = jnp.exp(ls + between)
    if valid is not None:
        a = jnp.where(valid, a, 0.0)
    return lm, a


def _sb_band_start(i, j):
    return pl.multiple_of(i * 2 * ATT_T + j * SB_SUB - SB_BACK, SB_SUB)


def _sb_tile(qs, k, run, valid):
    ls, lm = _sb_logits(qs, k)
    lm, a = _sb_weights(ls, lm, run, valid)
    return ls, lm, a


def _sb_band(i, qs_all, k_ref):
    nsub = qs_all.shape[0] // SB_SUB
    valid = _sb_band_valid(nsub)
    starts = [_sb_band_start(i, j) for j in range(nsub)]
    kwins = [k_ref[pl.ds(k0, SB_BAND), :] for k0 in starts]
    parts = [_sb_logits(qs_all[j * SB_SUB:(j + 1) * SB_SUB], kwins[j]) for j in range(nsub)]
    ls = jnp.concatenate([p[0] for p in parts], axis=0)
    lm, a = _sb_weights(ls, jnp.concatenate([p[1] for p in parts], axis=0), None, valid)
    return starts, kwins, ls, lm, a, valid


def _sb_suffix(g, run_g):
    n = g.shape[1]
    row = lax.broadcasted_iota(jnp.int32, (n, n), 0)
    col = lax.broadcasted_iota(jnp.int32, (n, n), 1)
    from_here = (row >= col).astype(BF16)
    hi, lo = _split2(g)
    out = _dot(hi, from_here) + _dot(lo, from_here)
    return out if run_g is None else run_g + out


def _sb_walk(i, carry, tile):
    def alive_of(c):
        return (jnp.max(c[0]) > SB_DEAD).astype(jnp.int32)

    def cond(state):
        n, alive = state[0], state[1]
        return jnp.logical_and(n < i, alive > 0)

    def step(state):
        n = state[0]
        c = tile(i - 1 - n, state[2:], False)
        return (n + 1, alive_of(c)) + tuple(c)

    out = lax.while_loop(cond, step, (jnp.int32(0), alive_of(carry)) + tuple(carry))
    return out[2:]


def _sb_specs(s):
    tq = 2 * ATT_T
    q_spec = pl.BlockSpec((None, None, tq, HEAD_DIM), lambda h, i: (4, h, i, 0))
    k_spec = pl.BlockSpec((None, None, s, HEAD_DIM), lambda h, i: (5, h, 0, 0))
    v_spec = pl.BlockSpec((None, None, s, HEAD_DIM), lambda h, i: (6, h, 0, 0))
    row_spec = pl.BlockSpec((None, tq, HEAD_DIM), lambda h, i: (h, i, 0))
    band_spec = pl.BlockSpec((None, None, 1, 128), lambda h, i: (h, i, 0, 0))
    return tq, q_spec, k_spec, v_spec, row_spec, band_spec


def _sb_block(i, tile, zero):
    t = ATT_T
    lo, hi, both = slice(0, t), slice(t, 2 * t), slice(0, 2 * t)
    c_hi = tile(2 * i + 1, hi, zero, 0)
    c_lo = tile(2 * i, lo, zero, 0)
    c_hi = tile(2 * i, hi, c_hi, None)
    carry = tuple(jnp.concatenate([a, b], axis=0) for a, b in zip(c_lo, c_hi))
    return _sb_walk(2 * i, carry, lambda kb, c, _: tile(kb, both, c, None))


def sb_fwd(qkv):
    s = qkv.shape[2]
    t = ATT_T
    tq, q_spec, k_spec, v_spec, row_spec, band_spec = _sb_specs(s)

    def body(q_ref, k_ref, v_ref, o_ref, band_ref, done_ref):
        i = pl.program_id(1)
        qs = q_ref[...] * 0.125
        done_ref[0] = 0

        @pl.when(i > 0)
        def _():
            starts, _, _, lm, a, _ = _sb_band(i, qs, k_ref)
            ab = a.astype(BF16)
            for j, k0 in enumerate(starts):
                rows = slice(j * SB_SUB, (j + 1) * SB_SUB)
                o_ref[rows, :] = _dot(ab[rows], v_ref[pl.ds(k0, SB_BAND), :])
            worst = jnp.max(jnp.sum(lm, axis=-1, keepdims=True))
            done_ref[0] = (worst <= SB_DEAD).astype(jnp.int32)

        @pl.when(done_ref[0] == 0)
        def _():
            def tile(kb, rows, carry, ahead):
                run, acc = carry
                k0 = pl.multiple_of(kb * t, t)
                valid = None if ahead is None else _sb_valid(t, ahead)
                _, lm, a = _sb_tile(qs[rows], k_ref[pl.ds(k0, t), :], run, valid)
                acc = acc + _dot(a.astype(BF16), v_ref[pl.ds(k0, t), :])
                return run + jnp.sum(lm, axis=-1, keepdims=True), acc

            _, acc = _sb_block(i, tile, (jnp.zeros((t, 1), F32), jnp.zeros((t, HEAD_DIM), F32)))
            o_ref[...] = acc

        band_ref[...] = jnp.full(band_ref.shape, done_ref[0], jnp.int32).astype(F32)

    return pl.pallas_call(
        body, name="sb_fwd", grid=(N_HEADS, s // tq),
        in_specs=[q_spec, k_spec, v_spec],
        out_specs=[row_spec, band_spec],
        out_shape=[jax.ShapeDtypeStruct((N_HEADS, s, HEAD_DIM), F32),
                   jax.ShapeDtypeStruct((N_HEADS, s // tq, 1, 128), F32)],
        scratch_shapes=[pltpu.SMEM((1,), jnp.int32)],
        compiler_params=_params(("arbitrary", "arbitrary")),
    )(qkv, qkv, qkv)


def sb_bwd(qkv, do, o, band):
    s = qkv.shape[2]
    t = ATT_T
    tq, q_spec, k_spec, v_spec, row_spec, band_spec = _sb_specs(s)
    nq = s // tq
    any_spec = pl.BlockSpec(memory_space=pl.ANY)

    def body(q_ref, k_ref, v_ref, do_ref, o_ref, band_ref, dq_ref, dk_hbm, dv_hbm, dk_acc, dv_acc):
        h = pl.program_id(0)
        i = pl.program_id(1)

        @pl.when(i == 0)
        def _():
            dk_acc[...] = jnp.zeros_like(dk_acc)
            dv_acc[...] = jnp.zeros_like(dv_acc)

        qs_all = q_ref[...] * 0.125
        dob_all = do_ref[...]
        tot_all = jnp.sum(o_ref[...] * dob_all.astype(F32), axis=-1, keepdims=True)
        on_band = jnp.max(band_ref[...]) > 0.5

        def grads(qs, dob, tot, k, v, k0, run, run_g, valid):
            ls, lm, a = _sb_tile(qs, k, run, valid)
            ab = a.astype(BF16)
            g = ab.astype(F32) * _dot_nt(dob, v)
            g_left = tot - _sb_suffix(g, run_g)
            dz = g - jnp.exp(ls) * (g + g_left)
            if valid is not None:
                dz = jnp.where(valid, dz, 0.0)
            dzb = dz.astype(BF16)
            n = k.shape[0]
            dk_acc[pl.ds(k0, n), :] += _dot_tn(dzb, qs)
            dv_acc[pl.ds(k0, n), :] += _dot_tn(ab, dob)
            return dzb, lm, g

        @pl.when(on_band)
        def _():
            starts, kwins, ls, _, a, valid = _sb_band(i, qs_all, k_ref)
            ab = a.astype(BF16)
            subs = [slice(j * SB_SUB, (j + 1) * SB_SUB) for j in range(len(starts))]
            vwins = [v_ref[pl.ds(k0, SB_BAND), :] for k0 in starts]
            g = ab.astype(F32) * jnp.concatenate([_dot_nt(dob_all[r], v) for r, v in zip(subs, vwins)], axis=0)
            dz = jnp.where(valid, g - jnp.exp(ls) * (g + (tot_all - _sb_suffix(g, None))), 0.0)
            dzb = dz.astype(BF16)
            for r, k0, k in zip(subs, starts, kwins):
                dq_ref[r, :] = _dot(dzb[r], k) * 0.125
                dk_acc[pl.ds(k0, SB_BAND), :] += _dot_tn(dzb[r], qs_all[r])
                dv_acc[pl.ds(k0, SB_BAND), :] += _dot_tn(ab[r], dob_all[r])

        @pl.when(jnp.logical_not(on_band))
        def _():
            def tile(kb, rows, carry, ahead):
                run, run_g, dq = carry
                k0 = pl.multiple_of(kb * t, t)
                k = k_ref[pl.ds(k0, t), :]
                valid = None if ahead is None else _sb_valid(t, ahead)
                dzb, lm, g = grads(qs_all[rows], dob_all[rows], tot_all[rows], k, v_ref[pl.ds(k0, t), :], k0,
                                   run, run_g, valid)
                return (run + jnp.sum(lm, axis=-1, keepdims=True),
                        run_g + jnp.sum(g, axis=-1, keepdims=True),
                        dq + _dot(dzb, k))

            zero = jnp.zeros((t, 1), F32)
            _, _, dq = _sb_block(i, tile, (zero, zero, jnp.zeros((t, HEAD_DIM), F32)))
            dq_ref[...] = dq * 0.125

        @pl.when(i == nq - 1)
        def _():
            pltpu.sync_copy(dk_acc, dk_hbm.at[h])
            pltpu.sync_copy(dv_acc, dv_hbm.at[h])

    return pl.pallas_call(
        body, name="sb_bwd", grid=(N_HEADS, nq),
        in_specs=[q_spec, k_spec, v_spec, row_spec, row_spec, band_spec],
        out_specs=[row_spec, any_spec, any_spec],
        out_shape=[jax.ShapeDtypeStruct((N_HEADS, s, HEAD_DIM), F32)] * 3,
        scratch_shapes=[pltpu.VMEM((s, HEAD_DIM), F32), pltpu.VMEM((s, HEAD_DIM), F32)],
        compiler_params=_params(("arbitrary", "arbitrary")),
    )(qkv, qkv, qkv, do, o, band)


def _branch_inputs(refs, br):
    ya_ref, yb_ref, yc_ref, yd_ref = refs
    if br == 1:
        return yb_ref[...]
    return _heads_to_lanes((ya_ref, None, yc_ref, yd_ref)[br])


def outproj_fwd(x, ya, yb, yc, yd, gates, bg, wout):
    s = x.shape[0]
    tm = min(ROW_T, s)

    def body(x_ref, ya_ref, yb_ref, yc_ref, yd_ref, gates_ref, bg_ref, w_ref, out_ref):
        pieces = []
        for br in range(4):
            cols = slice(br * D_BRANCH, (br + 1) * D_BRANCH)
            y = _branch_inputs((ya_ref, yb_ref, yc_ref, yd_ref), br)
            r = lax.rsqrt(jnp.mean(y * y, axis=-1, keepdims=True) + EPS)
            gt = gates_ref[:, cols]
            pieces.append((y * r * bg_ref[:, cols]) * (gt * _sigmoid(gt)))
        merged = jnp.concatenate(pieces, axis=1).astype(BF16)
        out_ref[...] = x_ref[...] + _dot(merged, w_ref[...])

    head_spec = pl.BlockSpec((N_HEADS, tm, HEAD_DIM), lambda i: (0, i, 0))
    return pl.pallas_call(
        body, name="outproj_fwd", grid=(s // tm,),
        in_specs=[pl.BlockSpec((tm, D_MODEL), lambda i: (i, 0)),
                  head_spec, pl.BlockSpec((tm, D_BRANCH), lambda i: (i, 0)), head_spec, head_spec,
                  pl.BlockSpec((tm, D_MODEL), lambda i: (i, 0)),
                  pl.BlockSpec((1, D_MODEL), lambda i: (0, 0)),
                  pl.BlockSpec((D_MODEL, D_MODEL), lambda i: (0, 0))],
        out_specs=pl.BlockSpec((tm, D_MODEL), lambda i: (i, 0)),
        out_shape=jax.ShapeDtypeStruct((s, D_MODEL), F32),
        compiler_params=_params(("arbitrary",)),
    )(x, ya, yb, yc, yd, gates, bg, wout)


def outproj_bwd(dout, ya, yb, yc, yd, gates, bg, wout):
    s = dout.shape[0]
    tm = min(ROW_T, s)

    def body(dout_ref, ya_ref, yb_ref, yc_ref, yd_ref, gates_ref, bg_ref, w_ref,
             dya_ref, dyb_ref, dyc_ref, dyd_ref, dgates_ref, dbg_ref, dw_ref):
        i = pl.program_id(0)

        @pl.when(i == 0)
        def _():
            dbg_ref[...] = jnp.zeros_like(dbg_ref)
            dw_ref[...] = jnp.zeros_like(dw_ref)

        doutb = dout_ref[...].astype(BF16)
        dmerged = _dot_nt(doutb, w_ref[...])
        pieces = []
        for br in range(4):
            cols = slice(br * D_BRANCH, (br + 1) * D_BRANCH)
            y = _branch_inputs((ya_ref, yb_ref, yc_ref, yd_ref), br)
            r = lax.rsqrt(jnp.mean(y * y, axis=-1, keepdims=True) + EPS)
            yn = y * r
            bgv = bg_ref[:, cols]
            gt = gates_ref[:, cols]
            sig = _sigmoid(gt)
            act = gt * sig
            n = yn * bgv
            pieces.append(n * act)
            dm = dmerged[:, cols]
            dn = dm * act
            dgates_ref[:, cols] = (dm * n * (sig * (1.0 + gt * (1.0 - sig)))).astype(BF16)
            dbg_ref[:, cols] += jnp.sum(dn * yn, axis=0, keepdims=True)
            u = dn * bgv
            dy = r * (u - yn * jnp.mean(yn * u, axis=-1, keepdims=True))
            if br == 1:
                dyb_ref[...] = dy
            else:
                dref = (dya_ref, None, dyc_ref, dyd_ref)[br]
                for hh in range(N_HEADS):
                    dref[hh] = dy[:, hh * HEAD_DIM:(hh + 1) * HEAD_DIM].astype(BF16)
        merged = jnp.concatenate(pieces, axis=1).astype(BF16)
        dw_ref[...] += _dot_tn(merged, doutb)

    head_spec = pl.BlockSpec((N_HEADS, tm, HEAD_DIM), lambda i: (0, i, 0))
    head_shape = jax.ShapeDtypeStruct((N_HEADS, s, HEAD_DIM), BF16)
    return pl.pallas_call(
        body, name="outproj_bwd", grid=(s // tm,),
        in_specs=[pl.BlockSpec((tm, D_MODEL), lambda i: (i, 0)),
                  head_spec, pl.BlockSpec((tm, D_BRANCH), lambda i: (i, 0)), head_spec, head_spec,
                  pl.BlockSpec((tm, D_MODEL), lambda i: (i, 0)),
                  pl.BlockSpec((1, D_MODEL), lambda i: (0, 0)),
                  pl.BlockSpec((D_MODEL, D_MODEL), lambda i: (0, 0))],
        out_specs=[head_spec, pl.BlockSpec((tm, D_BRANCH), lambda i: (i, 0)), head_spec, head_spec,
                   pl.BlockSpec((tm, D_MODEL), lambda i: (i, 0)),
                   pl.BlockSpec((1, D_MODEL), lambda i: (0, 0)),
                   pl.BlockSpec((D_MODEL, D_MODEL), lambda i: (0, 0))],
        out_shape=[head_shape, jax.ShapeDtypeStruct((s, D_BRANCH), F32), head_shape, head_shape,
                   jax.ShapeDtypeStruct((s, D_MODEL), BF16),
                   jax.ShapeDtypeStruct((1, D_MODEL), F32),
                   jax.ShapeDtypeStruct((D_MODEL, D_MODEL), F32)],
        compiler_params=_params(("arbitrary",)),
    )(dout, ya, yb, yc, yd, gates, bg, wout)


def final_loss(x, tgt, g):
    s = x.shape[0]
    tm = min(ROW_T, s)

    def body(x_ref, t_ref, g_ref, loss_ref, dx_ref, dg_ref):
        i = pl.program_id(0)

        @pl.when(i == 0)
        def _():
            loss_ref[...] = jnp.zeros_like(loss_ref)
            dg_ref[...] = jnp.zeros_like(dg_ref)

        xv = x_ref[...]
        gv = g_ref[...]
        r = lax.rsqrt(jnp.mean(xv * xv, axis=-1, keepdims=True) + EPS)
        xn = xv * r
        err = xn * gv - t_ref[...]
        loss_ref[...] += jnp.sum(err * err) * (0.5 / D_MODEL)
        dy = err * (1.0 / D_MODEL)
        u = dy * gv
        dx_ref[...] = r * (u - xn * jnp.mean(xn * u, axis=-1, keepdims=True))
        dg_ref[...] += jnp.sum(dy * xn, axis=0, keepdims=True)

    return pl.pallas_call(
        body, name="final_loss", grid=(s // tm,),
        in_specs=[pl.BlockSpec((tm, D_MODEL), lambda i: (i, 0)),
                  pl.BlockSpec((tm, D_MODEL), lambda i: (i, 0)),
                  pl.BlockSpec((1, D_MODEL), lambda i: (0, 0))],
        out_specs=[pl.BlockSpec((1, 128), lambda i: (0, 0)),
                   pl.BlockSpec((tm, D_MODEL), lambda i: (i, 0)),
                   pl.BlockSpec((1, D_MODEL), lambda i: (0, 0))],
        out_shape=[jax.ShapeDtypeStruct((1, 128), F32),
                   jax.ShapeDtypeStruct((s, D_MODEL), F32),
                   jax.ShapeDtypeStruct((1, D_MODEL), F32)],
        compiler_params=_params(("arbitrary",)),
    )(x, tgt, g)


def _rel_index():
    i = np.arange(A_TQ)[:, None]
    j = np.arange(A_BAND)[None, :]
    rel = np.clip(i - j + (A_BAND - A_TQ), -MAX_REL, MAX_REL) + MAX_REL
    dchunk = i // CHUNK + LOOKBACK - j // CHUNK
    valid = (dchunk >= 0) & (dchunk <= LOOKBACK)
    return jnp.asarray(np.where(valid, rel, -1).astype(np.int32))


def _layer_consts(p):
    tbias = relbias_tile(p["rel_bias"], _rel_index())
    return dict(
        norm_g=p["norm_g"].reshape(1, D_MODEL),
        v_gain=p["v_gain"].reshape(1, D_BRANCH),
        b_col=p["b_s"].reshape(N_HEADS, SG_CHUNK, 1),
        bg=p["branch_gain"].reshape(1, D_MODEL),
        tbias=tbias,
    )


def _gate_layout(fp, b_f, s):
    nb = s // 128
    ft = fp[:, :N_HEADS].T.reshape(N_HEADS * nb, 128)
    bcol = jnp.repeat(b_f, nb).reshape(N_HEADS * nb, 1)
    return ft, bcol


def layer_fwd(x, p, ride=()):
    s = x.shape[0]
    c = _layer_consts(p)
    h, qkv, kva, gates, uv, fp = inproj_fwd(x, c["norm_g"], p["wp"])
    ya, lse_a = mix_a_fwd(qkv, kva, c["tbias"])
    yb = mix_b_fwd(uv, c["v_gain"], p["w_s"], c["b_col"])
    ft, bcol = _gate_layout(fp, p["b_f"], s)
    c_row = fox_gate_fwd(ft, bcol).reshape(N_HEADS, s // ATT_T, 1, ATT_T)
    yc, ref_c, rl_c, *rode = fox_fwd(qkv, c_row, ride)
    yd, band_d = sb_fwd(qkv)
    out = outproj_fwd(x, ya, yb, yc, yd, gates, c["bg"], p["wout"])
    saved = dict(consts=c, x=x, h=h, qkv=qkv, gates=gates, uv=uv, kva=kva, ft=ft, bcol=bcol,
                 c_row=c_row, ya=ya, lse_a=lse_a, yb=yb, yc=yc, ref_c=ref_c, rl_c=rl_c, yd=yd, band_d=band_d)
    return out, saved, rode


def layer_bwd(dout, p, sv, exchange=False, upper_w_in=None):
    s = dout.shape[0]
    c = sv["consts"]
    dya, dyb, dyc, dyd, dgates, dbg, dwout = outproj_bwd(
        dout, sv["ya"], sv["yb"], sv["yc"], sv["yd"], sv["gates"], c["bg"], p["wout"])
    dqa, dka, dva, dt = mix_a_bwd(sv["qkv"], sv["kva"], c["tbias"], dya, sv["ya"], sv["lse_a"])
    drel = relbias_grad(dt, _rel_index())[:N_HEADS, :2 * MAX_REL + 1]
    duv, dws, dbs, dvgain = mix_b_bwd(sv["uv"], c["v_gain"], p["w_s"], c["b_col"], dyb)
    ride = [dwout.astype(BF16).reshape(4, D_BRANCH, D_MODEL)] if exchange else []
    if upper_w_in is not None:
        ride.append(upper_w_in)
    dqc, dkc, dvc, dc, *rode = fox_bwd(sv["qkv"], sv["c_row"], dyc, sv["yc"], sv["ref_c"], sv["rl_c"], ride)
    dft, dbf = fox_gate_bwd(sv["ft"], sv["bcol"], dc.reshape(N_HEADS * (s // 128), 128))
    dfp = jnp.pad(dft.reshape(N_HEADS, s).T, ((0, 0), (0, 128 - N_HEADS)))
    dqd, dkd, dvd = sb_bwd(sv["qkv"], dyd, sv["yd"], sv["band_d"])
    dp, dx, dnorm = inproj_bwd((dqa, dka, dva, dqc, dkc, dvc, dqd, dkd, dvd), dgates, duv, dfp,
                               p["wp"], sv["x"], c["norm_g"], dout)
    grads = dict(norm_g=dnorm.reshape(D_MODEL), w_in_shards=inproj_wgrad(sv["h"], dp), b_f=dbf[:N_HEADS, 0], rel_bias=drel,
                 w_s=dws, b_s=dbs.reshape(N_HEADS, SG_CHUNK), v_gain=dvgain.reshape(D_BRANCH),
                 branch_gain=dbg.reshape(4, D_BRANCH), wout=dwout)
    if exchange:
        grads["w_out_parts"] = rode[0]
    return dx, grads, (rode[1] if upper_w_in is not None else None)


def local_step(x, tgt, layers, final_g, next_shards=None):
    layers = list(layers)
    saved = []
    cur = x
    for l, p in enumerate(layers):
        ride = next_shards[l] if next_shards is not None and l + 1 < len(layers) else ()
        cur, sv, rode = layer_fwd(cur, p, ride)
        saved.append(sv)
        if ride:
            layers[l + 1] = dict(layers[l + 1], wp=pack_w_in(rode[0][None])[0], wout=rode[1].reshape(D_MODEL, D_MODEL))
    loss, dcur, dfinal = final_loss(cur, tgt, final_g.reshape(1, D_MODEL))
    grads = [None] * len(layers)
    for l in reversed(range(len(layers))):
        exchange = next_shards is not None
        upper = grads[l + 1]["w_in_shards"] if exchange and l + 1 < len(layers) else None
        dcur, grads[l], got = layer_bwd(dcur, layers[l], saved[l], exchange, upper)
        if upper is not None:
            grads[l + 1]["w_in_parts"] = got
    return loss[0, 0], dcur, grads, dfinal.reshape(D_MODEL)


def _chip_gather(pairs, send_sems, recv_sems, loc_sems):
    x, y, c = lax.axis_index("x"), lax.axis_index("y"), lax.axis_index("c")
    me = 2 * x + y
    chips = [(1 - x, y), (x, 1 - y), (1 - x, 1 - y)]
    npair = len(pairs)

    def local():
        return [pltpu.make_async_copy(src, dst(me), loc_sems.at[n]) for n, (src, dst) in enumerate(pairs)]

    def remote(j, n, slot):
        src, dst = pairs[n]
        return pltpu.make_async_remote_copy(
            src_ref=src, dst_ref=dst(slot), send_sem=send_sems.at[npair * j + n], recv_sem=recv_sems.at[npair * j + n],
            device_id=(chips[j][0], chips[j][1], c), device_id_type=MESH)

    def start():
        for cp in local():
            cp.start()
        for j in range(3):
            for n in range(npair):
                remote(j, n, me).start()

    def wait():
        for j in range(3):
            for n in range(npair):
                remote(j, n, 2 * chips[j][0] + chips[j][1]).wait_recv()
        for j in range(3):
            for n in range(npair):
                remote(j, n, me).wait_send()
        for cp in local():
            cp.wait()

    return start, wait


def gather_weights(w_in, w_out, gains):
    depth = w_in.shape[0]

    def body(in_ref, out_ref, g_ref, oin_ref, oout_ref, og_ref, send_sems, recv_sems, loc_sems):
        pairs = [(in_ref, lambda s: oin_ref.at[:, s]), (out_ref, lambda s: oout_ref.at[:, s]), (g_ref, lambda s: og_ref.at[s])]
        start, wait = _chip_gather(pairs, send_sems, recv_sems, loc_sems)
        start()
        wait()

    any_spec = pl.BlockSpec(memory_space=pl.ANY)
    return pl.pallas_call(
        body, name="gather_weights",
        in_specs=[any_spec] * 3, out_specs=[any_spec] * 3,
        out_shape=[jax.ShapeDtypeStruct((depth, 4) + w_in.shape[1:], w_in.dtype),
                   jax.ShapeDtypeStruct((depth, 4) + w_out.shape[1:], w_out.dtype),
                   jax.ShapeDtypeStruct((4,) + gains.shape, gains.dtype)],
        scratch_shapes=[pltpu.SemaphoreType.DMA((9,)), pltpu.SemaphoreType.DMA((9,)), pltpu.SemaphoreType.DMA((3,))],
    )(w_in, w_out, gains)


def pack_w_in(shards):
    depth = shards.shape[0]
    tr = 256

    def body(s_ref, o_ref):
        full = jnp.concatenate([s_ref[n] for n in range(4)], axis=1)
        o_ref[...] = jnp.concatenate([full[:, :SEC_D_Q], full[:, SEC_D_Q + N_HEADS:], full[:, SEC_D_Q:SEC_D_Q + N_HEADS],
                                      jnp.zeros((tr, N_PACK - N_IN), BF16)], axis=1)

    return pl.pallas_call(
        body, name="pack_w_in", grid=(depth, D_MODEL // tr),
        in_specs=[pl.BlockSpec((None, 4, tr, N_SHARD), lambda l, r: (l, 0, r, 0))],
        out_specs=pl.BlockSpec((None, tr, N_PACK), lambda l, r: (l, r, 0)),
        out_shape=jax.ShapeDtypeStruct((depth, D_MODEL, N_PACK), BF16),
        compiler_params=_params(("arbitrary", "arbitrary")),
    )(shards)


def _device_exchange(flows, send_sems, recv_sems, loc_sems):
    x, y, c = lax.axis_index("x"), lax.axis_index("y"), lax.axis_index("c")
    me_chip = 2 * x + y
    me = 4 * x + 2 * y + c
    peers = [(x, y, 1 - c)]
    for px, py in [(1 - x, y), (x, 1 - y), (1 - x, 1 - y)]:
        peers += [(px, py, c), (px, py, 1 - c)]
    nflow = len(flows)

    def local():
        return [pltpu.make_async_copy(src(me_chip), dst(me), loc_sems.at[f]) for f, (src, dst) in enumerate(flows)]

    def copies(n, chip, slot):
        return [pltpu.make_async_remote_copy(src_ref=src(chip), dst_ref=dst(slot), send_sem=send_sems.at[nflow * n + f],
                                             recv_sem=recv_sems.at[nflow * n + f], device_id=peers[n], device_id_type=MESH)
                for f, (src, dst) in enumerate(flows)]

    def start():
        for cp in local():
            cp.start()
        for n, (px, py, _) in enumerate(peers):
            for cp in copies(n, 2 * px + py, me):
                cp.start()

    def wait():
        for n, (px, py, pc) in enumerate(peers):
            for cp in copies(n, me_chip, 4 * px + 2 * py + pc):
                cp.wait_recv()
        for n, (px, py, _) in enumerate(peers):
            for cp in copies(n, 2 * px + py, me):
                cp.wait_send()
        for cp in local():
            cp.wait()

    return start, wait


def _exchange_flows(srcs, dsts):
    return [((lambda s, src=src: src.at[s]) if src.shape[0] == 4 else (lambda s, src=src: src),
             lambda d, dst=dst: dst.at[d]) for src, dst in zip(srcs, dsts)]


def _exchange_shapes(arrays):
    return [jax.ShapeDtypeStruct((8,) + (a.shape[1:] if a.shape[0] == 4 else a.shape), a.dtype) for a in arrays]


def _exchange_sems(n):
    return [pltpu.SemaphoreType.DMA((7 * n,)), pltpu.SemaphoreType.DMA((7 * n,)), pltpu.SemaphoreType.DMA((n,))]


def exchange_grads(*arrays):
    n = len(arrays)

    def body(*refs):
        start, wait = _device_exchange(_exchange_flows(refs[:n], refs[n:2 * n]), *refs[2 * n:])
        start()
        wait()

    any_spec = pl.BlockSpec(memory_space=pl.ANY)
    return pl.pallas_call(
        body, name="exchange_grads",
        in_specs=[any_spec] * n, out_specs=[any_spec] * n, out_shape=_exchange_shapes(arrays),
        scratch_shapes=_exchange_sems(n),
    )(*arrays)


def adamw_reduce(parts, w, m, v, name, tr):
    rows, width = w.shape
    per = rows // len(parts) // tr
    c1 = 1.0 - ADAM_B1 ** ADAM_STEP
    c2 = 1.0 - ADAM_B2 ** ADAM_STEP

    def body(*refs):
        p_refs = refs[:len(parts)]
        w_ref, m_ref, v_ref, g_ref, d_ref, nm_ref, nv_ref = refs[len(parts):]
        i = pl.program_id(0)
        p = p_refs[0][...]
        for n in range(1, len(parts)):
            p = jnp.where(i >= n * per, p_refs[n][...], p)
        g = p[0].astype(F32)
        for n in range(1, 8):
            g = g + p[n].astype(F32)
        g_ref[...] = g
        nm = ADAM_B1 * m_ref[...] + (1.0 - ADAM_B1) * g
        nv = ADAM_B2 * v_ref[...] + (1.0 - ADAM_B2) * (g * g)
        nm_ref[...] = nm
        nv_ref[...] = nv
        d_ref[...] = -ADAM_LR * ((nm / c1) / (jnp.sqrt(nv / c2) + ADAM_EPS) + ADAM_WD * w_ref[...])

    spec = pl.BlockSpec((tr, width), lambda i: (i, 0))
    shape = jax.ShapeDtypeStruct((rows, width), F32)
    return pl.pallas_call(
        body, name=name, grid=(rows // tr,),
        in_specs=[pl.BlockSpec((8, tr, width), lambda i, n=n: (0, jnp.clip(i - n * per, 0, per - 1), 0))
                  for n in range(len(parts))] + [spec, spec, spec],
        out_specs=[spec] * 4, out_shape=[shape] * 4,
        compiler_params=_params(("arbitrary",)),
    )(*parts, w, m, v)


SMALL =("norm_g", "b_f", "rel_bias", "w_s", "b_s", "v_gain", "final_g")
WEIGHTS = ("norm_g", "w_in", "b_f", "rel_bias", "w_s", "b_s", "v_gain", "branch_gain", "w_out", "final_g")
PACK_ROW_TILE = 512


def _rows_of(shape):
    return -(-int(np.prod(shape)) // 128)


def _pack(leaves):
    parts = []
    for a in leaves:
        flat = a.reshape(-1).astype(F32)
        parts.append(jnp.pad(flat, (0, _rows_of(a.shape) * 128 - flat.shape[0])))
    flat = jnp.concatenate(parts)
    rows = flat.shape[0] // 128
    total = -(-rows // PACK_ROW_TILE) * PACK_ROW_TILE
    return jnp.pad(flat, (0, (total - rows) * 128)).reshape(total, 128)


def _unpack(slab, shapes):
    out, row = [], 0
    for shp in shapes:
        n = int(np.prod(shp))
        r = _rows_of(shp)
        out.append(slab[row:row + r].reshape(-1)[:n].reshape(shp))
        row += r
    return out


def kernel(x, norm_g, w_in, b_f, rel_bias, w_s, b_s, v_gain, branch_gain, w_out, final_g, loss_target, m_norm_g, m_w_in, m_b_f, m_rel_bias, m_w_s, m_b_s, m_v_gain, m_branch_gain, m_w_out, m_final_g, v_norm_g, v_w_in, v_b_f, v_rel_bias, v_w_s, v_b_s, v_v_gain, v_branch_gain, v_w_out, v_final_g):
    depth = norm_g.shape[0]
    weights = dict(norm_g=norm_g, w_in=w_in, b_f=b_f, rel_bias=rel_bias, w_s=w_s, b_s=b_s, v_gain=v_gain,
                   branch_gain=branch_gain, w_out=w_out, final_g=final_g)
    mom1 = dict(norm_g=m_norm_g, w_in=m_w_in, b_f=m_b_f, rel_bias=m_rel_bias, w_s=m_w_s, b_s=m_b_s,
                v_gain=m_v_gain, branch_gain=m_branch_gain, w_out=m_w_out, final_g=m_final_g)
    mom2 = dict(norm_g=v_norm_g, w_in=v_w_in, b_f=v_b_f, rel_bias=v_rel_bias, w_s=v_w_s, b_s=v_b_s,
                v_gain=v_v_gain, branch_gain=v_branch_gain, w_out=v_w_out, final_g=v_final_g)

    wf = jnp.pad(branch_gain.reshape(-1), (0, 8 * 128 - branch_gain.size)).reshape(8, 128)
    w_in_b, w_out_b = w_in.astype(BF16), w_out.astype(BF16)
    w_in_shards, w_out_shards, gf = gather_weights(w_in_b[:1], w_out_b[:1], wf)
    bg_full = gf.reshape(4, -1)[:, :branch_gain.size].reshape((4,) + branch_gain.shape)
    bg_full = jnp.moveaxis(bg_full, 0, 2).reshape(depth, 4, D_BRANCH)

    layers = [dict(norm_g=norm_g[l], b_f=b_f[l], rel_bias=rel_bias[l], w_s=w_s[l],
                   b_s=b_s[l], v_gain=v_gain[l], branch_gain=bg_full[l]) for l in range(depth)]
    layers[0].update(wp=pack_w_in(w_in_shards)[0], wout=w_out_shards.reshape(D_MODEL, D_MODEL))
    next_shards = [(w_in_b[l + 1], w_out_b[l + 1]) for l in range(depth - 1)]

    loss_part, grad_x, lgrads, dfinal = local_step(x[0], loss_target[0], layers, final_g, next_shards)
    loss = lax.psum(loss_part, ("x", "y", "c"))

    stack = lambda k: jnp.stack([g[k] for g in lgrads])
    d_gain = jnp.moveaxis(stack("branch_gain").reshape(depth, 4, 4, HEAD_DIM), 2, 0).reshape(4, -1)
    d_gain = jnp.pad(d_gain, ((0, 0), (0, 8 * 128 - d_gain.shape[1]))).reshape(4, 8, 128)
    small = dict(norm_g=stack("norm_g"), b_f=stack("b_f"), rel_bias=stack("rel_bias"), w_s=stack("w_s"),
                 b_s=stack("b_s"), v_gain=stack("v_gain"), final_g=dfinal)
    parts_in, parts_gain, parts_small = exchange_grads(lgrads[0]["w_in_shards"], d_gain, _pack([small[k] for k in SMALL]))
    parts = dict(w_in=[parts_in] + [g["w_in_parts"] for g in lgrads[1:]], w_out=[g["w_out_parts"] for g in lgrads])

    outs = {}
    tags = ("grad", "delta", "new_m", "new_v")
    for k in ("w_in", "w_out"):
        rows = depth * weights[k].shape[1]
        flat = lambda a: a.reshape(rows, a.shape[-1])
        res = adamw_reduce(parts[k], flat(weights[k]), flat(mom1[k]), flat(mom2[k]), "adamw_" + k, 256)
        for tag, a in zip(tags, res):
            outs[tag, k] = a.reshape(weights[k].shape)
    gain8 = lambda a: jnp.pad(a.reshape(-1), (0, 8 * 128 - a.size)).reshape(8, 128)
    res = adamw_reduce([parts_gain], gain8(branch_gain), gain8(m_branch_gain), gain8(v_branch_gain), "adamw_gain", 8)
    for tag, a in zip(tags, res):
        outs[tag, "branch_gain"] = a.reshape(-1)[:branch_gain.size].reshape(branch_gain.shape)
    pack_small = lambda d: _pack([d[k] for k in SMALL])
    res = adamw_reduce([parts_small], pack_small(weights), pack_small(mom1), pack_small(mom2), "adamw_small", PACK_ROW_TILE)
    for tag, slab in zip(tags, res):
        for k, a in zip(SMALL, _unpack(slab, [weights[k].shape for k in SMALL])):
            outs[tag, k] = a
    result = [loss, grad_x[None]]
    for tag in ("grad", "delta", "new_m", "new_v"):
        result += [outs[tag, k] for k in WEIGHTS]
    return tuple(result)
```

```python
import functools

import jax
import jax.numpy as jnp
import numpy as np
from jax import lax
from jax.experimental import pallas as pl
from jax.experimental.pallas import tpu as pltpu

F32 = jnp.float32
BF16 = jnp.bfloat16
MESH = pl.DeviceIdType.MESH

D_MODEL = 1024
D_BRANCH = 256
N_HEADS = 4
HEAD_DIM = 64
CHUNK = 64
LOOKBACK = 8
MAX_REL = 128
SG_CHUNK = 128
EPS = 1e-6
N_IN = 3844
N_PACK = 3968
F_COL = 3840
N_SHARD = 961
NEG = -1e30

A_TQ = 128
A_BAND = A_TQ + LOOKBACK * CHUNK
REL_LO = MAX_REL - (CHUNK - 1)
REL_HI = 2 * MAX_REL + 1
A_PAD = LOOKBACK * CHUNK
A_QB = 1024
ATT_T = 256
FOX_TQ = 512
FOX_WIDE = 4
FOX_DEAD2 = -160.0
LOG2E = 1.4426950408889634
SB_TQ = 1024
SB_SUB = 128
SB_BACK = 256
SB_BAND = SB_SUB + SB_BACK
SB_DEAD = -110.0
ROW_T = 512
VMEM_LIMIT = 56 * 1024 * 1024

ADAM_LR = 0.001
ADAM_B1 = 0.9
ADAM_B2 = 0.999
ADAM_EPS = 1e-08
ADAM_WD = 0.01
ADAM_STEP = 10

SEC_A_Q, SEC_A_K, SEC_A_V, SEC_A_G = 0, 256, 512, 768
SEC_B_U, SEC_B_V, SEC_B_G = 1024, 1280, 1536
SEC_C_Q, SEC_C_K, SEC_C_V, SEC_C_G = 1792, 2048, 2304, 2560
SEC_D_Q, SEC_D_K, SEC_D_V, SEC_D_G = 2816, 3072, 3328, 3584
QKV_SECS = (SEC_A_Q, SEC_C_Q, SEC_C_K, SEC_C_V, SEC_D_Q, SEC_D_K, SEC_D_V)
GATE_SECS = (SEC_A_G, SEC_B_G, SEC_C_G, SEC_D_G)


def _dot(a, b):
    return jnp.dot(a, b, preferred_element_type=F32)


def _dot_nt(a, b):
    return lax.dot_general(a, b, (((1,), (1,)), ((), ())), preferred_element_type=F32)


def _dot_tn(a, b):
    return lax.dot_general(a, b, (((0,), (0,)), ((), ())), preferred_element_type=F32)


def _split2(x):
    hi = x.astype(BF16)
    lo = (x - hi.astype(F32)).astype(BF16)
    return hi, lo


def _split3(x):
    hi = x.astype(BF16)
    r = x - hi.astype(F32)
    mid = r.astype(BF16)
    lo = (r - mid.astype(F32)).astype(BF16)
    return hi, mid, lo


def _sigmoid(x):
    return 1.0 / (1.0 + jnp.exp(-x))


def _params(sem=None, vmem=VMEM_LIMIT):
    return pltpu.CompilerParams(dimension_semantics=sem, vmem_limit_bytes=vmem)


def _heads_to_lanes(ref):
    return jnp.concatenate([ref[h] for h in range(N_HEADS)], axis=1)


def inproj_fwd(x, g, wp):
    s = x.shape[0]
    tm = A_PAD

    def body(x_ref, g_ref, w_ref, h_ref, qkv_ref, kva_ref, gates_ref, uv_ref, f_ref):
        xv = x_ref[...]
        r = lax.rsqrt(jnp.mean(xv * xv, axis=-1, keepdims=True) + EPS)
        h = (xv * r * g_ref[...]).astype(BF16)
        h_ref[...] = h
        for n, off in enumerate(QKV_SECS):
            p = _dot(h, w_ref[:, off:off + D_BRANCH])
            for hh in range(N_HEADS):
                qkv_ref[n, hh] = p[:, hh * HEAD_DIM:(hh + 1) * HEAD_DIM].astype(BF16)
        for n, off in enumerate((SEC_A_K, SEC_A_V)):
            p = _dot(h, w_ref[:, off:off + D_BRANCH])
            for hh in range(N_HEADS):
                kva_ref[n, hh] = p[:, hh * HEAD_DIM:(hh + 1) * HEAD_DIM].astype(BF16)
        for n, off in enumerate(GATE_SECS):
            gates_ref[:, n * D_BRANCH:(n + 1) * D_BRANCH] = _dot(h, w_ref[:, off:off + D_BRANCH])
        uv_ref[...] = _dot(h, w_ref[:, SEC_B_U:SEC_B_U + 2 * D_BRANCH])
        f_ref[...] = _dot(h, w_ref[:, F_COL:F_COL + 128])

    return pl.pallas_call(
        body, name="inproj_fwd", grid=(s // tm,),
        in_specs=[pl.BlockSpec((tm, D_MODEL), lambda i: (i, 0)),
                  pl.BlockSpec((1, D_MODEL), lambda i: (0, 0)),
                  pl.BlockSpec((D_MODEL, N_PACK), lambda i: (0, 0))],
        out_specs=[pl.BlockSpec((tm, D_MODEL), lambda i: (i, 0)),
                   pl.BlockSpec((len(QKV_SECS), N_HEADS, tm, HEAD_DIM), lambda i: (0, 0, i, 0)),
                   pl.BlockSpec((2, N_HEADS, tm, HEAD_DIM), lambda i: (0, 0, i + 1, 0)),
                   pl.BlockSpec((tm, D_MODEL), lambda i: (i, 0)),
                   pl.BlockSpec((tm, 2 * D_BRANCH), lambda i: (i, 0)),
                   pl.BlockSpec((tm, 128), lambda i: (i, 0))],
        out_shape=[jax.ShapeDtypeStruct((s, D_MODEL), BF16),
                   jax.ShapeDtypeStruct((len(QKV_SECS), N_HEADS, s, HEAD_DIM), BF16),
                   jax.ShapeDtypeStruct((2, N_HEADS, s + tm, HEAD_DIM), BF16),
                   jax.ShapeDtypeStruct((s, D_MODEL), F32),
                   jax.ShapeDtypeStruct((s, 2 * D_BRANCH), F32),
                   jax.ShapeDtypeStruct((s, 128), F32)],
        compiler_params=_params(("arbitrary",)),
    )(x, g, wp)


def inproj_bwd(dqkv, dgates, duv, dfp, wp, x, g, dres):
    s = x.shape[0]
    tm = A_PAD

    def body(*refs):
        dq_refs = refs[:9]
        dgates_ref, duv_ref, dfp_ref, w_ref, x_ref, g_ref, dres_ref, dp_ref, dx_ref, dg_ref = refs[9:]
        i = pl.program_id(0)
        a_q, a_k, a_v, c_q, c_k, c_v, d_q, d_k, d_v = [_heads_to_lanes(r).astype(BF16) for r in dq_refs]
        dgt = dgates_ref[...]
        duv_b = duv_ref[...].astype(BF16)
        dp = jnp.concatenate(
            [a_q, a_k, a_v, dgt[:, 0:256], duv_b, dgt[:, 256:512], c_q, c_k, c_v, dgt[:, 512:768],
             d_q, d_k, d_v, dgt[:, 768:1024], dfp_ref[...].astype(BF16)], axis=1)
        dp_ref[...] = dp
        dh = _dot_nt(dp, w_ref[...])
        xv = x_ref[...]
        r = lax.rsqrt(jnp.mean(xv * xv, axis=-1, keepdims=True) + EPS)
        xn = xv * r
        u = dh * g_ref[...]
        dx_ref[...] = dres_ref[...] + r * (u - xn * jnp.mean(xn * u, axis=-1, keepdims=True))

        @pl.when(i == 0)
        def _():
            dg_ref[...] = jnp.zeros_like(dg_ref)

        dg_ref[...] += jnp.sum(dh * xn, axis=0, keepdims=True)

    head_spec = pl.BlockSpec((N_HEADS, tm, HEAD_DIM), lambda i: (0, i, 0))
    padded_spec = pl.BlockSpec((N_HEADS, tm, HEAD_DIM), lambda i: (0, i + 1, 0))
    return pl.pallas_call(
        body, name="inproj_bwd", grid=(s // tm,),
        in_specs=[head_spec, padded_spec, padded_spec] + [head_spec] * 6 + [
            pl.BlockSpec((tm, D_MODEL), lambda i: (i, 0)),
            pl.BlockSpec((tm, 2 * D_BRANCH), lambda i: (i, 0)),
            pl.BlockSpec((tm, 128), lambda i: (i, 0)),
            pl.BlockSpec((D_MODEL, N_PACK), lambda i: (0, 0)),
            pl.BlockSpec((tm, D_MODEL), lambda i: (i, 0)),
            pl.BlockSpec((1, D_MODEL), lambda i: (0, 0)),
            pl.BlockSpec((tm, D_MODEL), lambda i: (i, 0))],
        out_specs=[pl.BlockSpec((tm, N_PACK), lambda i: (i, 0)),
                   pl.BlockSpec((tm, D_MODEL), lambda i: (i, 0)),
                   pl.BlockSpec((1, D_MODEL), lambda i: (0, 0))],
        out_shape=[jax.ShapeDtypeStruct((s, N_PACK), BF16),
                   jax.ShapeDtypeStruct((s, D_MODEL), F32),
                   jax.ShapeDtypeStruct((1, D_MODEL), F32)],
        compiler_params=_params(("arbitrary",)),
    )(*dqkv, dgates, duv, dfp, wp, x, g, dres)


def inproj_wgrad(h, dp):
    s, m = h.shape
    tm = min(2 * ROW_T, s)
    tmm = 256
    nsteps = s // tm

    def body(a_ref, b_ref, o_ref, acc_ref):
        k = pl.program_id(1)

        @pl.when(k == 0)
        def _():
            acc_ref[...] = jnp.zeros_like(acc_ref)

        acc_ref[...] += _dot_tn(a_ref[...], b_ref[...])

        @pl.when(k == nsteps - 1)
        def _():
            acc = acc_ref[...]
            full = jnp.concatenate([acc[:, :SEC_D_Q], acc[:, F_COL:F_COL + N_HEADS], acc[:, SEC_D_Q:F_COL]], axis=1)
            for n in range(4):
                o_ref[n] = full[:, n * N_SHARD:(n + 1) * N_SHARD].astype(BF16)

    return pl.pallas_call(
        body, name="inproj_wgrad", grid=(m // tmm, nsteps),
        in_specs=[pl.BlockSpec((tm, tmm), lambda j, k: (k, j)),
                  pl.BlockSpec((tm, N_PACK), lambda j, k: (k, 0))],
        out_specs=pl.BlockSpec((4, tmm, N_SHARD), lambda j, k: (0, j, 0)),
        out_shape=jax.ShapeDtypeStruct((4, m, N_SHARD), BF16),
        scratch_shapes=[pltpu.VMEM((tmm, N_PACK), F32)],
        compiler_params=_params(("arbitrary", "arbitrary")),
    )(h, dp)


def _a_specs(s):
    nq = s // A_QB
    per = A_QB // A_PAD
    q_spec = pl.BlockSpec((None, None, A_QB, HEAD_DIM), lambda h, i: (0, h, jnp.minimum(i, nq - 1), 0))
    kv_specs = [pl.BlockSpec((None, None, A_PAD, HEAD_DIM),
                             lambda h, i, n=n, m=m: (n, h, jnp.minimum(per * i + m, per * nq), 0))
                for n in range(2) for m in range(per + 1)]
    t_spec = pl.BlockSpec((None, A_TQ, A_BAND), lambda h, i: (h, 0, 0))
    return nq, q_spec, kv_specs, t_spec


def _a_window(refs, i):
    first = refs[0][...]
    return jnp.concatenate([jnp.where(i > 0, first, jnp.zeros_like(first))] + [r[...] for r in refs[1:]], axis=0)


def _a_scores(q_ref, k, t_ref, i, j):
    rows = slice(j * A_TQ, (j + 1) * A_TQ)
    qs = q_ref[rows, :] * 0.125
    kj = k[j * A_TQ:j * A_TQ + A_BAND, :]
    sc = _dot_nt(qs, kj) + t_ref[...]
    col = lax.broadcasted_iota(jnp.int32, (A_TQ, A_BAND), 1)
    sc = jnp.where(col >= A_PAD - i * A_QB - j * A_TQ, sc, NEG)
    return rows, qs, kj, sc


def mix_a_fwd(qkv, kva, tbias):
    s = qkv.shape[2]
    nq, q_spec, kv_specs, t_spec = _a_specs(s)
    nwin = len(kv_specs) // 2

    def body(*refs):
        q_ref, t_ref, o_ref, lse_ref = refs[0], refs[1 + 2 * nwin], refs[2 + 2 * nwin], refs[3 + 2 * nwin]
        i = pl.program_id(1)
        k = _a_window(refs[1:1 + nwin], i)
        v = _a_window(refs[1 + nwin:1 + 2 * nwin], i)
        for j in range(A_QB // A_TQ):
            rows, _, _, sc = _a_scores(q_ref, k, t_ref, i, j)
            m = jnp.max(sc, axis=-1, keepdims=True)
            p = jnp.exp(sc - m)
            l = jnp.sum(p, axis=-1, keepdims=True)
            o_ref[rows, :] = _dot(p.astype(BF16), v[j * A_TQ:j * A_TQ + A_BAND, :]) / l
            lse_ref[rows, :] = m + jnp.log(l)

    return pl.pallas_call(
        body, name="mix_a_fwd", grid=(N_HEADS, nq),
        in_specs=[q_spec] + kv_specs + [t_spec],
        out_specs=[pl.BlockSpec((None, A_QB, HEAD_DIM), lambda h, i: (h, i, 0)),
                   pl.BlockSpec((None, A_QB, 1), lambda h, i: (h, i, 0))],
        out_shape=[jax.ShapeDtypeStruct((N_HEADS, s, HEAD_DIM), F32),
                   jax.ShapeDtypeStruct((N_HEADS, s, 1), F32)],
        compiler_params=_params(("arbitrary", "arbitrary")),
    )(qkv, *([kva] * (2 * nwin)), tbias)


def mix_a_bwd(qkv, kva, tbias, do, o, lse):
    s = qkv.shape[2]
    nq, q_spec, kv_specs, t_spec = _a_specs(s)
    nwin = len(kv_specs) // 2
    row_spec = lambda w: pl.BlockSpec((None, A_QB, w), lambda h, i: (h, jnp.minimum(i, nq - 1), 0))
    done_spec = pl.BlockSpec((None, A_QB, HEAD_DIM), lambda h, i: (h, i, 0))
    win = A_QB + A_PAD

    def body(*refs):
        q_ref = refs[0]
        t_ref, do_ref, o_ref, lse_ref, dq_ref, dk_ref, dv_ref, dt_ref, dk_win, dv_win = refs[1 + 2 * nwin:]
        i = pl.program_id(1)

        @pl.when(i == 0)
        def _():
            dk_win[...] = jnp.zeros_like(dk_win)
            dv_win[...] = jnp.zeros_like(dv_win)
            dt_ref[...] = jnp.zeros_like(dt_ref)

        @pl.when(i < nq)
        def _():
            k = _a_window(refs[1:1 + nwin], i)
            v = _a_window(refs[1 + nwin:1 + 2 * nwin], i)
            dt = jnp.zeros((A_TQ, A_BAND), F32)
            for j in range(A_QB // A_TQ):
                rows, qs, kj, sc = _a_scores(q_ref, k, t_ref, i, j)
                keys = slice(j * A_TQ, j * A_TQ + A_BAND)
                dob = do_ref[rows, :]
                p = jnp.exp(sc - lse_ref[rows, :])
                delta = jnp.sum(o_ref[rows, :] * dob.astype(F32), axis=-1, keepdims=True)
                ds = p * (_dot_nt(dob, v[keys, :]) - delta)
                dsb = ds.astype(BF16)
                dq_ref[rows, :] = _dot(dsb, kj) * 0.125
                dk_win[keys, :] += _dot_tn(dsb, qs)
                dv_win[keys, :] += _dot_tn(p.astype(BF16), dob)
                dt = dt + ds
            dt_ref[...] += dt

        dk_ref[...] = dk_win[0:A_QB, :]
        dv_ref[...] = dv_win[0:A_QB, :]
        dk_rest = dk_win[A_QB:win, :]
        dv_rest = dv_win[A_QB:win, :]
        dk_win[0:A_PAD, :] = dk_rest
        dv_win[0:A_PAD, :] = dv_rest
        dk_win[A_PAD:win, :] = jnp.zeros((A_QB, HEAD_DIM), F32)
        dv_win[A_PAD:win, :] = jnp.zeros((A_QB, HEAD_DIM), F32)

    return pl.pallas_call(
        body, name="mix_a_bwd", grid=(N_HEADS, nq + 1),
        in_specs=[q_spec] + kv_specs + [t_spec, row_spec(HEAD_DIM), row_spec(HEAD_DIM), row_spec(1)],
        out_specs=[row_spec(HEAD_DIM), done_spec, done_spec, t_spec],
        out_shape=[jax.ShapeDtypeStruct((N_HEADS, s, HEAD_DIM), F32),
                   jax.ShapeDtypeStruct((N_HEADS, s + A_QB, HEAD_DIM), F32),
                   jax.ShapeDtypeStruct((N_HEADS, s + A_QB, HEAD_DIM), F32),
                   jax.ShapeDtypeStruct((N_HEADS, A_TQ, A_BAND), F32)],
        scratch_shapes=[pltpu.VMEM((win, HEAD_DIM), F32), pltpu.VMEM((win, HEAD_DIM), F32)],
        compiler_params=_params(("arbitrary", "arbitrary")),
    )(qkv, *([kva] * (2 * nwin)), tbias, do, o, lse)


def relbias_tile(rel_bias, relmat):
    def body(rb_ref, rel_ref, o_ref):
        rel = rel_ref[...]
        o_ref[...] = jnp.full(o_ref.shape, NEG, F32)

        def step(r, carry):
            hit = rel == r
            for h in range(N_HEADS):
                o_ref[h] = jnp.where(hit, rb_ref[h, r], o_ref[h])
            return carry

        lax.fori_loop(REL_LO, REL_HI, step, 0)

    return pl.pallas_call(
        body, name="relbias_tile",
        in_specs=[pl.BlockSpec(memory_space=pltpu.SMEM), pl.BlockSpec(memory_space=pltpu.VMEM)],
        out_specs=pl.BlockSpec(memory_space=pltpu.VMEM),
        out_shape=jax.ShapeDtypeStruct((N_HEADS, A_TQ, A_BAND), F32),
        compiler_params=_params(),
    )(rel_bias, relmat)


def relbias_grad(dt, relmat):
    def body(dt_ref, rel_ref, o_ref):
        rel = rel_ref[...]
        lane = lax.broadcasted_iota(jnp.int32, (8, 384), 1)
        row = lax.broadcasted_iota(jnp.int32, (8, 384), 0)

        def step(r, acc):
            hit = rel == r
            for h in range(N_HEADS):
                val = jnp.sum(jnp.where(hit, dt_ref[h], 0.0))
                acc = jnp.where((lane == r) & (row == h), val, acc)
            return acc

        o_ref[...] = lax.fori_loop(REL_LO, REL_HI, step, jnp.zeros((8, 384), F32))

    return pl.pallas_call(
        body, name="relbias_grad",
        out_shape=jax.ShapeDtypeStruct((8, 384), F32),
        compiler_params=_params(),
    )(dt, relmat)


def _b_norm(v, gain):
    mu = jnp.mean(v, axis=-1, keepdims=True)
    xc = v - mu
    rstd = lax.rsqrt(jnp.mean(xc * xc, axis=-1, keepdims=True) + EPS)
    xhat = xc * rstd
    return xhat, rstd, xhat * gain


def _tril_mask():
    t = lax.broadcasted_iota(jnp.int32, (SG_CHUNK, SG_CHUNK), 0)
    u = lax.broadcasted_iota(jnp.int32, (SG_CHUNK, SG_CHUNK), 1)
    return u <= t


def mix_b_fwd(uv, gain, w_s, b_col):
    s = uv.shape[0]
    tm = min(ROW_T, s)

    def body(uv_ref, gain_ref, w_ref, b_ref, y_ref):
        tril = _tril_mask()
        ws = [jnp.where(tril, w_ref[g], 0.0).astype(BF16) for g in range(N_HEADS)]
        for c in range(tm // SG_CHUNK):
            rows = slice(c * SG_CHUNK, (c + 1) * SG_CHUNK)
            u = uv_ref[rows, 0:D_BRANCH]
            _, _, vn = _b_norm(uv_ref[rows, D_BRANCH:2 * D_BRANCH], gain_ref[...])
            vnb = vn.astype(BF16)
            outs = []
            for g in range(N_HEADS):
                cols = slice(g * HEAD_DIM, (g + 1) * HEAD_DIM)
                mixed = _dot(ws[g], vnb[:, cols]) + b_ref[g]
                outs.append(u[:, cols] * mixed)
            y_ref[rows, :] = jnp.concatenate(outs, axis=1)

    return pl.pallas_call(
        body, name="mix_b_fwd", grid=(s // tm,),
        in_specs=[pl.BlockSpec((tm, 2 * D_BRANCH), lambda i: (i, 0)),
                  pl.BlockSpec((1, D_BRANCH), lambda i: (0, 0)),
                  pl.BlockSpec((N_HEADS, SG_CHUNK, SG_CHUNK), lambda i: (0, 0, 0)),
                  pl.BlockSpec((N_HEADS, SG_CHUNK, 1), lambda i: (0, 0, 0))],
        out_specs=pl.BlockSpec((tm, D_BRANCH), lambda i: (i, 0)),
        out_shape=jax.ShapeDtypeStruct((s, D_BRANCH), F32),
        compiler_params=_params(("arbitrary",)),
    )(uv, gain, w_s, b_col)


def mix_b_bwd(uv, gain, w_s, b_col, dy):
    s = uv.shape[0]
    tm = min(ROW_T, s)

    def body(uv_ref, gain_ref, w_ref, b_ref, dy_ref, duv_ref, dw_ref, db_ref, dgain_ref):
        i = pl.program_id(0)

        @pl.when(i == 0)
        def _():
            dw_ref[...] = jnp.zeros_like(dw_ref)
            db_ref[...] = jnp.zeros_like(db_ref)
            dgain_ref[...] = jnp.zeros_like(dgain_ref)

        tril = _tril_mask()
        ws = [jnp.where(tril, w_ref[g], 0.0).astype(BF16) for g in range(N_HEADS)]
        gain_v = gain_ref[...]
        for c in range(tm // SG_CHUNK):
            rows = slice(c * SG_CHUNK, (c + 1) * SG_CHUNK)
            u = uv_ref[rows, 0:D_BRANCH]
            xhat, rstd, vn = _b_norm(uv_ref[rows, D_BRANCH:2 * D_BRANCH], gain_v)
            vnb = vn.astype(BF16)
            dyv = dy_ref[rows, :]
            dus, dvns = [], []
            for g in range(N_HEADS):
                cols = slice(g * HEAD_DIM, (g + 1) * HEAD_DIM)
                mixed = _dot(ws[g], vnb[:, cols]) + b_ref[g]
                dus.append(dyv[:, cols] * mixed)
                dmixed = dyv[:, cols] * u[:, cols]
                dmb = dmixed.astype(BF16)
                db_ref[g] += jnp.sum(dmixed, axis=-1, keepdims=True)
                dw_ref[g] += jnp.where(tril, _dot_nt(dmb, vnb[:, cols]), 0.0)
                dvns.append(_dot_tn(ws[g], dmb))
            dvn = jnp.concatenate(dvns, axis=1)
            dgain_ref[...] += jnp.sum(dvn * xhat, axis=0, keepdims=True)
            dxh = dvn * gain_v
            dv = rstd * (dxh - jnp.mean(dxh, axis=-1, keepdims=True)
                         - xhat * jnp.mean(dxh * xhat, axis=-1, keepdims=True))
            duv_ref[rows, :] = jnp.concatenate(dus + [dv], axis=1)

    return pl.pallas_call(
        body, name="mix_b_bwd", grid=(s // tm,),
        in_specs=[pl.BlockSpec((tm, 2 * D_BRANCH), lambda i: (i, 0)),
                  pl.BlockSpec((1, D_BRANCH), lambda i: (0, 0)),
                  pl.BlockSpec((N_HEADS, SG_CHUNK, SG_CHUNK), lambda i: (0, 0, 0)),
                  pl.BlockSpec((N_HEADS, SG_CHUNK, 1), lambda i: (0, 0, 0)),
                  pl.BlockSpec((tm, D_BRANCH), lambda i: (i, 0))],
        out_specs=[pl.BlockSpec((tm, 2 * D_BRANCH), lambda i: (i, 0)),
                   pl.BlockSpec((N_HEADS, SG_CHUNK, SG_CHUNK), lambda i: (0, 0, 0)),
                   pl.BlockSpec((N_HEADS, SG_CHUNK, 1), lambda i: (0, 0, 0)),
                   pl.BlockSpec((1, D_BRANCH), lambda i: (0, 0))],
        out_shape=[jax.ShapeDtypeStruct((s, 2 * D_BRANCH), F32),
                   jax.ShapeDtypeStruct((N_HEADS, SG_CHUNK, SG_CHUNK), F32),
                   jax.ShapeDtypeStruct((N_HEADS, SG_CHUNK, 1), F32),
                   jax.ShapeDtypeStruct((1, D_BRANCH), F32)],
        compiler_params=_params(("arbitrary",)),
    )(uv, gain, w_s, b_col, dy)


def _scan_mats(nrow):
    a = lax.broadcasted_iota(jnp.int32, (128, 128), 0)
    b = lax.broadcasted_iota(jnp.int32, (128, 128), 1)
    r = lax.broadcasted_iota(jnp.int32, (nrow, nrow), 0)
    c = lax.broadcasted_iota(jnp.int32, (nrow, nrow), 1)
    nb = nrow // N_HEADS
    same = (r // nb) == (c // nb)
    return a, b, r, c, same


def _exact_dot(x, m):
    hi, mid, lo = _split3(x)
    return _dot(hi, m) + _dot(mid, m) + _dot(lo, m)


def _exact_dot_left(m, x):
    hi, mid, lo = _split3(x)
    return _dot(m, hi) + _dot(m, mid) + _dot(m, lo)


def fox_gate_fwd(ft, bcol):
    nrow = ft.shape[0]

    def body(f_ref, b_ref, c_ref):
        z = f_ref[...] + b_ref[...]
        ls = jnp.minimum(z, 0.0) - jnp.log(1.0 + jnp.exp(-jnp.abs(z)))
        a, b, r, c, same = _scan_mats(nrow)
        within = _exact_dot(ls, (a <= b).astype(BF16))
        tot = jnp.broadcast_to(within[:, 127:128], within.shape)
        before = _exact_dot_left((same & (c < r)).astype(BF16), tot)
        c_ref[...] = within + before

    return pl.pallas_call(
        body, name="fox_gate_fwd",
        out_shape=jax.ShapeDtypeStruct((nrow, 128), F32),
        compiler_params=_params(),
    )(ft, bcol)


def fox_gate_bwd(ft, bcol, dc):
    nrow = ft.shape[0]

    def body(f_ref, b_ref, dc_ref, df_ref, db_ref):
        a, b, r, c, same = _scan_mats(nrow)
        dcv = dc_ref[...]
        within = _exact_dot(dcv, (a >= b).astype(BF16))
        tot = jnp.broadcast_to(within[:, 0:1], within.shape)
        after = _exact_dot_left((same & (c > r)).astype(BF16), tot)
        dls = within + after
        z = f_ref[...] + b_ref[...]
        dz = dls * _sigmoid(-z)
        df_ref[...] = dz
        rs = jnp.broadcast_to(jnp.sum(dz, axis=-1, keepdims=True), dz.shape)
        hr = lax.broadcasted_iota(jnp.int32, (8, nrow), 0)
        hc = lax.broadcasted_iota(jnp.int32, (8, nrow), 1)
        db_ref[...] = _exact_dot_left((hr == hc // (nrow // N_HEADS)).astype(BF16), rs)

    return pl.pallas_call(
        body, name="fox_gate_bwd",
        out_shape=[jax.ShapeDtypeStruct((nrow, 128), F32), jax.ShapeDtypeStruct((8, 128), F32)],
        compiler_params=_params(),
    )(ft, bcol, dc)


def _att_specs(s, qi, ki, vi):
    q_spec = pl.BlockSpec((None, None, FOX_TQ, HEAD_DIM), lambda h, i: (qi, h, i, 0))
    k_spec = pl.BlockSpec((None, None, s, HEAD_DIM), lambda h, i: (ki, h, 0, 0))
    v_spec = pl.BlockSpec((None, None, s, HEAD_DIM), lambda h, i: (vi, h, 0, 0))
    row_spec = lambda w: pl.BlockSpec((None, FOX_TQ, w), lambda h, i: (h, i, 0))
    gate_spec = pl.BlockSpec((None, s // ATT_T, 1, ATT_T), lambda h, i: (h, 0, 0, 0))
    return q_spec, k_spec, v_spec, row_spec, gate_spec


def _causal(strict, n=ATT_T):
    row = lax.broadcasted_iota(jnp.int32, (n, n), 0)
    col = lax.broadcasted_iota(jnp.int32, (n, n), 1)
    return (col < row) if strict else (col <= row)


def _gate_row(cr_ref, kb, g):
    if g == 1:
        return cr_ref[kb]
    return jnp.concatenate([cr_ref[kb + n] for n in range(g)], axis=1)


def _fox_walk(i, carry, tile, alive):
    g = FOX_WIDE
    own = FOX_TQ // ATT_T
    nwide = (own * i) // g
    carry = tile(own * i, own, carry, True)
    carry = lax.fori_loop(0, (own * i - nwide * g) // own, lambda n, c: tile(nwide * g, own, c, False), carry)

    def cond(state):
        return jnp.logical_and(state[0] >= 0, state[1] > 0)

    def step(state):
        n = state[0]
        c = tile(n * g, g, state[2:], False)
        return (n - 1, alive(n * g, c)) + tuple(c)

    out = lax.while_loop(cond, step, (nwide - 1, alive(nwide * g, carry)) + tuple(carry))
    return out[2:]


def _fox_reach(qs, k_ref, kmax_ref, cc, i):
    s = k_ref.shape[0]
    rows = 4 * ATT_T

    @pl.when(i == 0)
    def _():
        def chunk(n, mx):
            kc = k_ref[pl.ds(pl.multiple_of(n * rows, rows), rows), :].astype(F32)
            return jnp.maximum(mx, jnp.max(jnp.sum(kc * kc, axis=-1, keepdims=True)))

        kmax_ref[0] = jnp.sqrt(lax.fori_loop(0, s // rows, chunk, jnp.float32(0.0)))

    qf = qs.astype(F32)
    return jnp.sqrt(jnp.sum(qf * qf, axis=-1, keepdims=True)) * kmax_ref[0] + cc


def _gate_col(cr_ref, i):
    row = lax.broadcasted_iota(jnp.int32, (ATT_T, ATT_T), 0)
    col = lax.broadcasted_iota(jnp.int32, (ATT_T, ATT_T), 1)
    own = FOX_TQ // ATT_T
    return jnp.concatenate([jnp.sum(jnp.where(row == col, cr_ref[own * i + n], 0.0), axis=-1, keepdims=True)
                            for n in range(own)], axis=0)


def _fox_scores(qs, k, cc, crow, masked):
    sc = (_dot_nt(qs, k) + (cc - crow)) * LOG2E
    if masked:
        sc = jnp.where(_causal(False, FOX_TQ), sc, NEG)
    return sc


def fox_fwd(qkv, c_row, ride=()):
    s = qkv.shape[2]
    t = ATT_T
    nq = s // FOX_TQ
    q_spec, k_spec, v_spec, row_spec, gate_spec = _att_specs(s, 1, 2, 3)
    rows = 4 * t
    nride = len(ride)

    def body(q_ref, k_ref, v_ref, cr_ref, *refs):
        ride_in, refs = refs[:nride], refs[nride:]
        o_ref, ref_ref, rl_ref = refs[:3]
        ride_out, refs = refs[3:3 + nride], refs[3 + nride:]
        v1_ref, kmax_ref = refs[:2]
        i = pl.program_id(1)
        if nride:
            h = pl.program_id(0)
            start, wait = _chip_gather([(src, lambda slot, dst=dst: dst.at[slot]) for src, dst in zip(ride_in, ride_out)],
                                       *refs[2:])
            pl.when(jnp.logical_and(h == 0, i == 0))(start)

        @pl.when(i == 0)
        def _():
            def chunk(n, carry):
                r0 = pl.multiple_of(n * rows, rows)
                v1_ref[pl.ds(r0, rows), :] = jnp.concatenate(
                    [v_ref[pl.ds(r0, rows), :], jnp.ones((rows, HEAD_DIM), BF16)], axis=1)
                return carry

            lax.fori_loop(0, s // rows, chunk, 0)

        qs = q_ref[...] * 0.125
        cc = _gate_col(cr_ref, i)
        reach = _fox_reach(qs, k_ref, kmax_ref, cc, i) * LOG2E

        def alive(kb, carry):
            return (jnp.max(reach - cr_ref[kb][:, 0:1] * LOG2E - carry[0]) > FOX_DEAD2).astype(jnp.int32)

        def tile(kb, g, carry, masked):
            m, acc = carry
            k0 = pl.multiple_of(kb * t, t)
            sc = _fox_scores(qs, k_ref[pl.ds(k0, g * t), :], cc, _gate_row(cr_ref, kb, g), masked)
            m_new = jnp.maximum(m, jnp.ceil(jnp.max(sc, axis=-1, keepdims=True)))
            pb = jnp.exp2(sc - m_new).astype(BF16)
            acc = jnp.exp2(m - m_new) * acc + _dot(pb, v1_ref[pl.ds(k0, g * t), :])
            return m_new, acc

        init = (jnp.full((FOX_TQ, 1), NEG, F32), jnp.zeros((FOX_TQ, 2 * HEAD_DIM), F32))
        m, acc = _fox_walk(i, init, tile, alive)
        rl = 1.0 / acc[:, HEAD_DIM:HEAD_DIM + 1]
        o_ref[...] = acc[:, 0:HEAD_DIM] * rl
        ref_ref[...] = m
        rl_ref[...] = rl
        if nride:
            pl.when(jnp.logical_and(h == N_HEADS - 1, i == nq - 1))(wait)

    any_spec = pl.BlockSpec(memory_space=pl.ANY)
    ride_sems = [pltpu.SemaphoreType.DMA((3 * nride,)), pltpu.SemaphoreType.DMA((3 * nride,)),
                 pltpu.SemaphoreType.DMA((nride,))] if nride else []
    return pl.pallas_call(
        body, name="fox_fwd_gather" if nride else "fox_fwd", grid=(N_HEADS, nq),
        in_specs=[q_spec, k_spec, v_spec, gate_spec] + [any_spec] * nride,
        out_specs=[row_spec(HEAD_DIM), row_spec(1), row_spec(1)] + [any_spec] * nride,
        out_shape=[jax.ShapeDtypeStruct((N_HEADS, s, HEAD_DIM), F32),
                   jax.ShapeDtypeStruct((N_HEADS, s, 1), F32),
                   jax.ShapeDtypeStruct((N_HEADS, s, 1), F32)]
        + [jax.ShapeDtypeStruct((4,) + a.shape, a.dtype) for a in ride],
        scratch_shapes=[pltpu.VMEM((s, 2 * HEAD_DIM), BF16), pltpu.SMEM((1,), F32)] + ride_sems,
        compiler_params=_params(("arbitrary", "arbitrary")),
    )(qkv, qkv, qkv, c_row, *ride)


def fox_bwd(qkv, c_row, do, o, ref, rl, ride=()):
    s = qkv.shape[2]
    t = ATT_T
    nq = s // FOX_TQ
    q_spec, k_spec, v_spec, row_spec, gate_spec = _att_specs(s, 1, 2, 3)
    any_spec = pl.BlockSpec(memory_space=pl.ANY)
    nride = len(ride)

    def body(q_ref, k_ref, v_ref, cr_ref, do_ref, o_ref, ref_ref, rl_ref, *refs):
        ride_in, refs = refs[:nride], refs[nride:]
        dq_ref, dk_hbm, dv_hbm, dc_ref = refs[:4]
        ride_out, refs = refs[4:4 + nride], refs[4 + nride:]
        dk_acc, dv_acc, kmax_ref = refs[:3]
        h = pl.program_id(0)
        i = pl.program_id(1)
        if nride:
            start, wait = _device_exchange(_exchange_flows(ride_in, ride_out), *refs[3:])
            pl.when(jnp.logical_and(h == 0, i == 0))(start)

        @pl.when(i == 0)
        def _():
            dk_acc[...] = jnp.zeros_like(dk_acc)
            dv_acc[...] = jnp.zeros_like(dv_acc)
            dc_ref[...] = jnp.zeros_like(dc_ref)

        qs = q_ref[...] * 0.125
        ref = ref_ref[...]
        rl = rl_ref[...]
        dob = (do_ref[...].astype(F32) * rl).astype(BF16)
        delta = jnp.sum(o_ref[...] * dob.astype(F32), axis=-1, keepdims=True)
        cc = _gate_col(cr_ref, i)
        margin = _fox_reach(qs, k_ref, kmax_ref, cc, i) * LOG2E - ref

        def alive(kb, carry):
            return (jnp.max(margin - cr_ref[kb][:, 0:1] * LOG2E) > FOX_DEAD2).astype(jnp.int32)

        def tile(kb, g, carry, masked):
            dq, = carry
            k0 = pl.multiple_of(kb * t, t)
            k = k_ref[pl.ds(k0, g * t), :]
            sc = _fox_scores(qs, k, cc, _gate_row(cr_ref, kb, g), masked)
            wb = jnp.exp2(sc - ref).astype(BF16)
            ds = wb.astype(F32) * (_dot_nt(dob, v_ref[pl.ds(k0, g * t), :]) - delta)
            dsb = ds.astype(BF16)
            dk_acc[pl.ds(k0, g * t), :] += _dot_tn(dsb, qs)
            dv_acc[pl.ds(k0, g * t), :] += _dot_tn(wb, dob)
            dcs = -jnp.sum(ds, axis=0, keepdims=True)
            for n in range(g):
                dc_ref[kb + n] += dcs[:, n * t:(n + 1) * t]
            return (dq + _dot(dsb, k),)

        dq, = _fox_walk(i, (jnp.zeros((FOX_TQ, HEAD_DIM), F32),), tile, alive)
        dq_ref[...] = dq * 0.125

        @pl.when(i == nq - 1)
        def _():
            pltpu.sync_copy(dk_acc, dk_hbm.at[h])
            pltpu.sync_copy(dv_acc, dv_hbm.at[h])

        if nride:
            pl.when(jnp.logical_and(h == N_HEADS - 1, i == nq - 1))(wait)

    return pl.pallas_call(
        body, name="fox_bwd_exchange" if nride else "fox_bwd", grid=(N_HEADS, nq),
        in_specs=[q_spec, k_spec, v_spec,
                  gate_spec,
                  row_spec(HEAD_DIM), row_spec(HEAD_DIM), row_spec(1), row_spec(1)] + [any_spec] * nride,
        out_specs=[row_spec(HEAD_DIM), any_spec, any_spec,
                   gate_spec] + [any_spec] * nride,
        out_shape=[jax.ShapeDtypeStruct((N_HEADS, s, HEAD_DIM), F32),
                   jax.ShapeDtypeStruct((N_HEADS, s, HEAD_DIM), F32),
                   jax.ShapeDtypeStruct((N_HEADS, s, HEAD_DIM), F32),
                   jax.ShapeDtypeStruct((N_HEADS, s // t, 1, t), F32)] + _exchange_shapes(ride),
        scratch_shapes=[pltpu.VMEM((s, HEAD_DIM), F32), pltpu.VMEM((s, HEAD_DIM), F32), pltpu.SMEM((1,), F32)]
        + (_exchange_sems(nride) if nride else []),
        compiler_params=_params(("arbitrary", "arbitrary")),
    )(qkv, qkv, qkv, c_row, do, o, ref, rl, *ride)


def _sb_valid(nrows, ahead):
    row = lax.broadcasted_iota(jnp.int32, (nrows, ATT_T), 0)
    col = lax.broadcasted_iota(jnp.int32, (nrows, ATT_T), 1)
    return col + ahead < row


def _sb_band_valid(nsub):
    row = lax.broadcasted_iota(jnp.int32, (nsub * SB_SUB, SB_BAND), 0)
    col = lax.broadcasted_iota(jnp.int32, (nsub * SB_SUB, SB_BAND), 1)
    return col < (row & (SB_SUB - 1)) + SB_BACK


def _sb_logits(qs, k):
    z = _dot_nt(qs, k)
    sp = jnp.log(1.0 + jnp.exp(-jnp.abs(z)))
    return jnp.minimum(z, 0.0) - sp, -jnp.maximum(z, 0.0) - sp


def _sb_weights(ls, lm, run, valid):
    if valid is not None:
        lm = jnp.where(valid, lm, 0.0)
    n = lm.shape[1]
    row = lax.broadcasted_iota(jnp.int32, (n, n), 0)
    col = lax.broadcasted_iota(jnp.int32, (n, n), 1)
    later = (row > col).astype(BF16)
    hi, lo = _split2(lm)
    between = _dot(hi, later) + _dot(lo, later)
    if run is not None:
        between = run + between
    a = jnp.exp(ls + between)
    if valid is not None:
        a = jnp.where(valid, a, 0.0)
    return lm, a


def _sb_band_start(i, j):
    return pl.multiple_of(i * SB_TQ + j * SB_SUB - SB_BACK, SB_SUB)


def _sb_tile(qs, k, run, valid):
    ls, lm = _sb_logits(qs, k)
    lm, a = _sb_weights(ls, lm, run, valid)
    return ls, lm, a


def _sb_band(i, qs_all, k_ref):
    nsub = qs_all.shape[0] // SB_SUB
    valid = _sb_band_valid(nsub)
    starts = [_sb_band_start(i, j) for j in range(nsub)]
    kwins = [k_ref[pl.ds(k0, SB_BAND), :] for k0 in starts]
    parts = [_sb_logits(qs_all[j * SB_SUB:(j + 1) * SB_SUB], kwins[j]) for j in range(nsub)]
    ls = jnp.concatenate([p[0] for p in parts], axis=0)
    lm, a = _sb_weights(ls, jnp.concatenate([p[1] for p in parts], axis=0), None, valid)
    return starts, kwins, ls, lm, a, valid


def _sb_suffix(g, run_g):
    n = g.shape[1]
    row = lax.broadcasted_iota(jnp.int32, (n, n), 0)
    col = lax.broadcasted_iota(jnp.int32, (n, n), 1)
    from_here = (row >= col).astype(BF16)
    hi, lo = _split2(g)
    out = _dot(hi, from_here) + _dot(lo, from_here)
    return out if run_g is None else run_g + out


def _sb_walk(i, carry, tile):
    def alive_of(c):
        return (jnp.max(c[0]) > SB_DEAD).astype(jnp.int32)

    def cond(state):
        n, alive = state[0], state[1]
        return jnp.logical_and(n < i, alive > 0)

    def step(state):
        n = state[0]
        c = tile(i - 1 - n, state[2:], False)
        return (n + 1, alive_of(c)) + tuple(c)

    out = lax.while_loop(cond, step, (jnp.int32(0), alive_of(carry)) + tuple(carry))
    return out[2:]


def _sb_specs(s):
    tq = SB_TQ
    q_spec = pl.BlockSpec((None, None, tq, HEAD_DIM), lambda h, i: (4, h, i, 0))
    k_spec = pl.BlockSpec((None, None, s, HEAD_DIM), lambda h, i: (5, h, 0, 0))
    v_spec = pl.BlockSpec((None, None, s, HEAD_DIM), lambda h, i: (6, h, 0, 0))
    row_spec = pl.BlockSpec((None, tq, HEAD_DIM), lambda h, i: (h, i, 0))
    band_spec = pl.BlockSpec((None, None, 1, 128), lambda h, i: (h, i, 0, 0))
    return tq, q_spec, k_spec, v_spec, row_spec, band_spec


def _sb_block(b, row0, tile, zero):
    t = ATT_T
    lo, hi, both = slice(row0, row0 + t), slice(row0 + t, row0 + 2 * t), slice(row0, row0 + 2 * t)
    c_hi = tile(2 * b + 1, hi, zero, 0)
    c_lo = tile(2 * b, lo, zero, 0)
    c_hi = tile(2 * b, hi, c_hi, None)
    carry = tuple(jnp.concatenate([x, y], axis=0) for x, y in zip(c_lo, c_hi))
    return _sb_walk(2 * b, carry, lambda kb, c, _: tile(kb, both, c, None))


def sb_fwd(qkv):
    s = qkv.shape[2]
    t = ATT_T
    tq, q_spec, k_spec, v_spec, row_spec, band_spec = _sb_specs(s)

    def body(q_ref, k_ref, v_ref, o_ref, band_ref, done_ref):
        i = pl.program_id(1)
        qs = q_ref[...] * 0.125
        done_ref[0] = 0

        @pl.when(i > 0)
        def _():
            starts, _, _, lm, a, _ = _sb_band(i, qs, k_ref)
            ab = a.astype(BF16)
            for j, k0 in enumerate(starts):
                rows = slice(j * SB_SUB, (j + 1) * SB_SUB)
                o_ref[rows, :] = _dot(ab[rows], v_ref[pl.ds(k0, SB_BAND), :])
            worst = jnp.max(jnp.sum(lm, axis=-1, keepdims=True))
            done_ref[0] = (worst <= SB_DEAD).astype(jnp.int32)

        @pl.when(done_ref[0] == 0)
        def _():
            def tile(kb, rows, carry, ahead):
                run, acc = carry
                k0 = pl.multiple_of(kb * t, t)
                valid = None if ahead is None else _sb_valid(t, ahead)
                _, lm, a = _sb_tile(qs[rows], k_ref[pl.ds(k0, t), :], run, valid)
                acc = acc + _dot(a.astype(BF16), v_ref[pl.ds(k0, t), :])
                return run + jnp.sum(lm, axis=-1, keepdims=True), acc

            for n in range(tq // (2 * t)):
                _, acc = _sb_block(i * (tq // (2 * t)) + n, n * 2 * t, tile,
                                   (jnp.zeros((t, 1), F32), jnp.zeros((t, HEAD_DIM), F32)))
                o_ref[n * 2 * t:(n + 1) * 2 * t, :] = acc

        band_ref[...] = jnp.full(band_ref.shape, done_ref[0], jnp.int32).astype(F32)

    return pl.pallas_call(
        body, name="sb_fwd", grid=(N_HEADS, s // tq),
        in_specs=[q_spec, k_spec, v_spec],
        out_specs=[row_spec, band_spec],
        out_shape=[jax.ShapeDtypeStruct((N_HEADS, s, HEAD_DIM), F32),
                   jax.ShapeDtypeStruct((N_HEADS, s // tq, 1, 128), F32)],
        scratch_shapes=[pltpu.SMEM((1,), jnp.int32)],
        compiler_params=_params(("arbitrary", "arbitrary")),
    )(qkv, qkv, qkv)


def sb_bwd(qkv, do, o, band):
    s = qkv.shape[2]
    t = ATT_T
    tq, q_spec, k_spec, v_spec, row_spec, band_spec = _sb_specs(s)
    nq = s // tq
    any_spec = pl.BlockSpec(memory_space=pl.ANY)

    def body(q_ref, k_ref, v_ref, do_ref, o_ref, band_ref, dq_ref, dk_hbm, dv_hbm, dk_acc, dv_acc):
        h = pl.program_id(0)
        i = pl.program_id(1)

        @pl.when(i == 0)
        def _():
            dk_acc[...] = jnp.zeros_like(dk_acc)
            dv_acc[...] = jnp.zeros_like(dv_acc)

        qs_all = q_ref[...] * 0.125
        dob_all = do_ref[...]
        tot_all = jnp.sum(o_ref[...] * dob_all.astype(F32), axis=-1, keepdims=True)
        on_band = jnp.max(band_ref[...]) > 0.5

        def grads(qs, dob, tot, k, v, k0, run, run_g, valid):
            ls, lm, a = _sb_tile(qs, k, run, valid)
            ab = a.astype(BF16)
            g = ab.astype(F32) * _dot_nt(dob, v)
            g_left = tot - _sb_suffix(g, run_g)
            dz = g - jnp.exp(ls) * (g + g_left)
            if valid is not None:
                dz = jnp.where(valid, dz, 0.0)
            dzb = dz.astype(BF16)
            n = k.shape[0]
            dk_acc[pl.ds(k0, n), :] += _dot_tn(dzb, qs)
            dv_acc[pl.ds(k0, n), :] += _dot_tn(ab, dob)
            return dzb, lm, g

        @pl.when(on_band)
        def _():
            starts, kwins, ls, _, a, valid = _sb_band(i, qs_all, k_ref)
            ab = a.astype(BF16)
            subs = [slice(j * SB_SUB, (j + 1) * SB_SUB) for j in range(len(starts))]
            vwins = [v_ref[pl.ds(k0, SB_BAND), :] for k0 in starts]
            g = ab.astype(F32) * jnp.concatenate([_dot_nt(dob_all[r], v) for r, v in zip(subs, vwins)], axis=0)
            dz = jnp.where(valid, g - jnp.exp(ls) * (g + (tot_all - _sb_suffix(g, None))), 0.0)
            dzb = dz.astype(BF16)
            for r, k0, k in zip(subs, starts, kwins):
                dq_ref[r, :] = _dot(dzb[r], k) * 0.125
                dk_acc[pl.ds(k0, SB_BAND), :] += _dot_tn(dzb[r], qs_all[r])
                dv_acc[pl.ds(k0, SB_BAND), :] += _dot_tn(ab[r], dob_all[r])

        @pl.when(jnp.logical_not(on_band))
        def _():
            def tile(kb, rows, carry, ahead):
                run, run_g, dq = carry
                k0 = pl.multiple_of(kb * t, t)
                k = k_ref[pl.ds(k0, t), :]
                valid = None if ahead is None else _sb_valid(t, ahead)
                dzb, lm, g = grads(qs_all[rows], dob_all[rows], tot_all[rows], k, v_ref[pl.ds(k0, t), :], k0,
                                   run, run_g, valid)
                return (run + jnp.sum(lm, axis=-1, keepdims=True),
                        run_g + jnp.sum(g, axis=-1, keepdims=True),
                        dq + _dot(dzb, k))

            zero = jnp.zeros((t, 1), F32)
            for n in range(tq // (2 * t)):
                _, _, dq = _sb_block(i * (tq // (2 * t)) + n, n * 2 * t, tile, (zero, zero, jnp.zeros((t, HEAD_DIM), F32)))
                dq_ref[n * 2 * t:(n + 1) * 2 * t, :] = dq * 0.125

        @pl.when(i == nq - 1)
        def _():
            pltpu.sync_copy(dk_acc, dk_hbm.at[h])
            pltpu.sync_copy(dv_acc, dv_hbm.at[h])

    return pl.pallas_call(
        body, name="sb_bwd", grid=(N_HEADS, nq),
        in_specs=[q_spec, k_spec, v_spec, row_spec, row_spec, band_spec],
        out_specs=[row_spec, any_spec, any_spec],
        out_shape=[jax.ShapeDtypeStruct((N_HEADS, s, HEAD_DIM), F32)] * 3,
        scratch_shapes=[pltpu.VMEM((s, HEAD_DIM), F32), pltpu.VMEM((s, HEAD_DIM), F32)],
        compiler_params=_params(("arbitrary", "arbitrary")),
    )(qkv, qkv, qkv, do, o, band)


def _branch_inputs(refs, br):
    ya_ref, yb_ref, yc_ref, yd_ref = refs
    if br == 1:
        return yb_ref[...]
    return _heads_to_lanes((ya_ref, None, yc_ref, yd_ref)[br])


def outproj_fwd(x, ya, yb, yc, yd, gates, bg, wout):
    s = x.shape[0]
    tm = min(ROW_T, s)

    def body(x_ref, ya_ref, yb_ref, yc_ref, yd_ref, gates_ref, bg_ref, w_ref, out_ref):
        pieces = []
        for br in range(4):
            cols = slice(br * D_BRANCH, (br + 1) * D_BRANCH)
            y = _branch_inputs((ya_ref, yb_ref, yc_ref, yd_ref), br)
            r = lax.rsqrt(jnp.mean(y * y, axis=-1, keepdims=True) + EPS)
            gt = gates_ref[:, cols]
            pieces.append((y * r * bg_ref[:, cols]) * (gt * _sigmoid(gt)))
        merged = jnp.concatenate(pieces, axis=1).astype(BF16)
        out_ref[...] = x_ref[...] + _dot(merged, w_ref[...])

    head_spec = pl.BlockSpec((N_HEADS, tm, HEAD_DIM), lambda i: (0, i, 0))
    return pl.pallas_call(
        body, name="outproj_fwd", grid=(s // tm,),
        in_specs=[pl.BlockSpec((tm, D_MODEL), lambda i: (i, 0)),
                  head_spec, pl.BlockSpec((tm, D_BRANCH), lambda i: (i, 0)), head_spec, head_spec,
                  pl.BlockSpec((tm, D_MODEL), lambda i: (i, 0)),
                  pl.BlockSpec((1, D_MODEL), lambda i: (0, 0)),
                  pl.BlockSpec((D_MODEL, D_MODEL), lambda i: (0, 0))],
        out_specs=pl.BlockSpec((tm, D_MODEL), lambda i: (i, 0)),
        out_shape=jax.ShapeDtypeStruct((s, D_MODEL), F32),
        compiler_params=_params(("arbitrary",)),
    )(x, ya, yb, yc, yd, gates, bg, wout)


def outproj_bwd(dout, ya, yb, yc, yd, gates, bg, wout):
    s = dout.shape[0]
    tm = min(ROW_T, s)

    def body(dout_ref, ya_ref, yb_ref, yc_ref, yd_ref, gates_ref, bg_ref, w_ref,
             dya_ref, dyb_ref, dyc_ref, dyd_ref, dgates_ref, dbg_ref, dw_ref):
        i = pl.program_id(0)

        @pl.when(i == 0)
        def _():
            dbg_ref[...] = jnp.zeros_like(dbg_ref)
            dw_ref[...] = jnp.zeros_like(dw_ref)

        doutb = dout_ref[...].astype(BF16)
        dmerged = _dot_nt(doutb, w_ref[...])
        pieces = []
        for br in range(4):
            cols = slice(br * D_BRANCH, (br + 1) * D_BRANCH)
            y = _branch_inputs((ya_ref, yb_ref, yc_ref, yd_ref), br)
            r = lax.rsqrt(jnp.mean(y * y, axis=-1, keepdims=True) + EPS)
            yn = y * r
            bgv = bg_ref[:, cols]
            gt = gates_ref[:, cols]
            sig = _sigmoid(gt)
            act = gt * sig
            n = yn * bgv
            pieces.append(n * act)
            dm = dmerged[:, cols]
            dn = dm * act
            dgates_ref[:, cols] = (dm * n * (sig * (1.0 + gt * (1.0 - sig)))).astype(BF16)
            dbg_ref[:, cols] += jnp.sum(dn * yn, axis=0, keepdims=True)
            u = dn * bgv
            dy = r * (u - yn * jnp.mean(yn * u, axis=-1, keepdims=True))
            if br == 1:
                dyb_ref[...] = dy
            else:
                dref = (dya_ref, None, dyc_ref, dyd_ref)[br]
                for hh in range(N_HEADS):
                    dref[hh] = dy[:, hh * HEAD_DIM:(hh + 1) * HEAD_DIM].astype(BF16)
        merged = jnp.concatenate(pieces, axis=1).astype(BF16)
        dw_ref[...] += _dot_tn(merged, doutb)

    head_spec = pl.BlockSpec((N_HEADS, tm, HEAD_DIM), lambda i: (0, i, 0))
    head_shape = jax.ShapeDtypeStruct((N_HEADS, s, HEAD_DIM), BF16)
    return pl.pallas_call(
        body, name="outproj_bwd", grid=(s // tm,),
        in_specs=[pl.BlockSpec((tm, D_MODEL), lambda i: (i, 0)),
                  head_spec, pl.BlockSpec((tm, D_BRANCH), lambda i: (i, 0)), head_spec, head_spec,
                  pl.BlockSpec((tm, D_MODEL), lambda i: (i, 0)),
                  pl.BlockSpec((1, D_MODEL), lambda i: (0, 0)),
                  pl.BlockSpec((D_MODEL, D_MODEL), lambda i: (0, 0))],
        out_specs=[head_spec, pl.BlockSpec((tm, D_BRANCH), lambda i: (i, 0)), head_spec, head_spec,
                   pl.BlockSpec((tm, D_MODEL), lambda i: (i, 0)),
                   pl.BlockSpec((1, D_MODEL), lambda i: (0, 0)),
                   pl.BlockSpec((D_MODEL, D_MODEL), lambda i: (0, 0))],
        out_shape=[head_shape, jax.ShapeDtypeStruct((s, D_BRANCH), F32), head_shape, head_shape,
                   jax.ShapeDtypeStruct((s, D_MODEL), BF16),
                   jax.ShapeDtypeStruct((1, D_MODEL), F32),
                   jax.ShapeDtypeStruct((D_MODEL, D_MODEL), F32)],
        compiler_params=_params(("arbitrary",)),
    )(dout, ya, yb, yc, yd, gates, bg, wout)


def final_loss(x, tgt, g):
    s = x.shape[0]
    tm = min(ROW_T, s)

    def body(x_ref, t_ref, g_ref, loss_ref, dx_ref, dg_ref):
        i = pl.program_id(0)

        @pl.when(i == 0)
        def _():
            loss_ref[...] = jnp.zeros_like(loss_ref)
            dg_ref[...] = jnp.zeros_like(dg_ref)

        xv = x_ref[...]
        gv = g_ref[...]
        r = lax.rsqrt(jnp.mean(xv * xv, axis=-1, keepdims=True) + EPS)
        xn = xv * r
        err = xn * gv - t_ref[...]
        loss_ref[...] += jnp.sum(err * err) * (0.5 / D_MODEL)
        dy = err * (1.0 / D_MODEL)
        u = dy * gv
        dx_ref[...] = r * (u - xn * jnp.mean(xn * u, axis=-1, keepdims=True))
        dg_ref[...] += jnp.sum(dy * xn, axis=0, keepdims=True)

    return pl.pallas_call(
        body, name="final_loss", grid=(s // tm,),
        in_specs=[pl.BlockSpec((tm, D_MODEL), lambda i: (i, 0)),
                  pl.BlockSpec((tm, D_MODEL), lambda i: (i, 0)),
                  pl.BlockSpec((1, D_MODEL), lambda i: (0, 0))],
        out_specs=[pl.BlockSpec((1, 128), lambda i: (0, 0)),
                   pl.BlockSpec((tm, D_MODEL), lambda i: (i, 0)),
                   pl.BlockSpec((1, D_MODEL), lambda i: (0, 0))],
        out_shape=[jax.ShapeDtypeStruct((1, 128), F32),
                   jax.ShapeDtypeStruct((s, D_MODEL), F32),
                   jax.ShapeDtypeStruct((1, D_MODEL), F32)],
        compiler_params=_params(("arbitrary",)),
    )(x, tgt, g)


def _rel_index():
    i = np.arange(A_TQ)[:, None]
    j = np.arange(A_BAND)[None, :]
    rel = np.clip(i - j + (A_BAND - A_TQ), -MAX_REL, MAX_REL) + MAX_REL
    dchunk = i // CHUNK + LOOKBACK - j // CHUNK
    valid = (dchunk >= 0) & (dchunk <= LOOKBACK)
    return jnp.asarray(np.where(valid, rel, -1).astype(np.int32))


def _layer_consts(p):
    tbias = relbias_tile(p["rel_bias"], _rel_index())
    return dict(
        norm_g=p["norm_g"].reshape(1, D_MODEL),
        v_gain=p["v_gain"].reshape(1, D_BRANCH),
        b_col=p["b_s"].reshape(N_HEADS, SG_CHUNK, 1),
        bg=p["branch_gain"].reshape(1, D_MODEL),
        tbias=tbias,
    )


def _gate_layout(fp, b_f, s):
    nb = s // 128
    ft = fp[:, :N_HEADS].T.reshape(N_HEADS * nb, 128)
    bcol = jnp.repeat(b_f, nb).reshape(N_HEADS * nb, 1)
    return ft, bcol


def layer_fwd(x, p, ride=()):
    s = x.shape[0]
    c = _layer_consts(p)
    h, qkv, kva, gates, uv, fp = inproj_fwd(x, c["norm_g"], p["wp"])
    ya, lse_a = mix_a_fwd(qkv, kva, c["tbias"])
    yb = mix_b_fwd(uv, c["v_gain"], p["w_s"], c["b_col"])
    ft, bcol = _gate_layout(fp, p["b_f"], s)
    c_row = fox_gate_fwd(ft, bcol).reshape(N_HEADS, s // ATT_T, 1, ATT_T)
    yc, ref_c, rl_c, *rode = fox_fwd(qkv, c_row, ride)
    yd, band_d = sb_fwd(qkv)
    out = outproj_fwd(x, ya, yb, yc, yd, gates, c["bg"], p["wout"])
    saved = dict(consts=c, x=x, h=h, qkv=qkv, gates=gates, uv=uv, kva=kva, ft=ft, bcol=bcol,
                 c_row=c_row, ya=ya, lse_a=lse_a, yb=yb, yc=yc, ref_c=ref_c, rl_c=rl_c, yd=yd, band_d=band_d)
    return out, saved, rode


def layer_bwd(dout, p, sv, exchange=False, upper_w_in=None):
    s = dout.shape[0]
    c = sv["consts"]
    dya, dyb, dyc, dyd, dgates, dbg, dwout = outproj_bwd(
        dout, sv["ya"], sv["yb"], sv["yc"], sv["yd"], sv["gates"], c["bg"], p["wout"])
    dqa, dka, dva, dt = mix_a_bwd(sv["qkv"], sv["kva"], c["tbias"], dya, sv["ya"], sv["lse_a"])
    drel = relbias_grad(dt, _rel_index())[:N_HEADS, :2 * MAX_REL + 1]
    duv, dws, dbs, dvgain = mix_b_bwd(sv["uv"], c["v_gain"], p["w_s"], c["b_col"], dyb)
    ride = [dwout.astype(BF16).reshape(4, D_BRANCH, D_MODEL)] if exchange else []
    if upper_w_in is not None:
        ride.append(upper_w_in)
    dqc, dkc, dvc, dc, *rode = fox_bwd(sv["qkv"], sv["c_row"], dyc, sv["yc"], sv["ref_c"], sv["rl_c"], ride)
    dft, dbf = fox_gate_bwd(sv["ft"], sv["bcol"], dc.reshape(N_HEADS * (s // 128), 128))
    dfp = jnp.pad(dft.reshape(N_HEADS, s).T, ((0, 0), (0, 128 - N_HEADS)))
    dqd, dkd, dvd = sb_bwd(sv["qkv"], dyd, sv["yd"], sv["band_d"])
    dp, dx, dnorm = inproj_bwd((dqa, dka, dva, dqc, dkc, dvc, dqd, dkd, dvd), dgates, duv, dfp,
                               p["wp"], sv["x"], c["norm_g"], dout)
    grads = dict(norm_g=dnorm.reshape(D_MODEL), w_in_shards=inproj_wgrad(sv["h"], dp), b_f=dbf[:N_HEADS, 0], rel_bias=drel,
                 w_s=dws, b_s=dbs.reshape(N_HEADS, SG_CHUNK), v_gain=dvgain.reshape(D_BRANCH),
                 branch_gain=dbg.reshape(4, D_BRANCH), wout=dwout)
    if exchange:
        grads["w_out_parts"] = rode[0]
    return dx, grads, (rode[1] if upper_w_in is not None else None)


def local_step(x, tgt, layers, final_g, next_shards=None):
    layers = list(layers)
    saved = []
    cur = x
    for l, p in enumerate(layers):
        ride = next_shards[l] if next_shards is not None and l + 1 < len(layers) else ()
        cur, sv, rode = layer_fwd(cur, p, ride)
        saved.append(sv)
        if ride:
            layers[l + 1] = dict(layers[l + 1], wp=pack_w_in(rode[0][None])[0], wout=rode[1].reshape(D_MODEL, D_MODEL))
    loss, dcur, dfinal = final_loss(cur, tgt, final_g.reshape(1, D_MODEL))
    grads = [None] * len(layers)
    for l in reversed(range(len(layers))):
        exchange = next_shards is not None
        upper = grads[l + 1]["w_in_shards"] if exchange and l + 1 < len(layers) else None
        dcur, grads[l], got = layer_bwd(dcur, layers[l], saved[l], exchange, upper)
        if upper is not None:
            grads[l + 1]["w_in_parts"] = got
    return loss[0, 0], dcur, grads, dfinal.reshape(D_MODEL)


def _chip_gather(pairs, send_sems, recv_sems, loc_sems):
    x, y, c = lax.axis_index("x"), lax.axis_index("y"), lax.axis_index("c")
    me = 2 * x + y
    chips = [(1 - x, y), (x, 1 - y), (1 - x, 1 - y)]
    npair = len(pairs)

    def local():
        return [pltpu.make_async_copy(src, dst(me), loc_sems.at[n]) for n, (src, dst) in enumerate(pairs)]

    def remote(j, n, slot):
        src, dst = pairs[n]
        return pltpu.make_async_remote_copy(
            src_ref=src, dst_ref=dst(slot), send_sem=send_sems.at[npair * j + n], recv_sem=recv_sems.at[npair * j + n],
            device_id=(chips[j][0], chips[j][1], c), device_id_type=MESH)

    def start():
        for cp in local():
            cp.start()
        for j in range(3):
            for n in range(npair):
                remote(j, n, me).start()

    def wait():
        for j in range(3):
            for n in range(npair):
                remote(j, n, 2 * chips[j][0] + chips[j][1]).wait_recv()
        for j in range(3):
            for n in range(npair):
                remote(j, n, me).wait_send()
        for cp in local():
            cp.wait()

    return start, wait


def gather_weights(w_in, w_out, gains):
    depth = w_in.shape[0]

    def body(in_ref, out_ref, g_ref, oin_ref, oout_ref, og_ref, send_sems, recv_sems, loc_sems):
        pairs = [(in_ref, lambda s: oin_ref.at[:, s]), (out_ref, lambda s: oout_ref.at[:, s]), (g_ref, lambda s: og_ref.at[s])]
        start, wait = _chip_gather(pairs, send_sems, recv_sems, loc_sems)
        start()
        wait()

    any_spec = pl.BlockSpec(memory_space=pl.ANY)
    return pl.pallas_call(
        body, name="gather_weights",
        in_specs=[any_spec] * 3, out_specs=[any_spec] * 3,
        out_shape=[jax.ShapeDtypeStruct((depth, 4) + w_in.shape[1:], w_in.dtype),
                   jax.ShapeDtypeStruct((depth, 4) + w_out.shape[1:], w_out.dtype),
                   jax.ShapeDtypeStruct((4,) + gains.shape, gains.dtype)],
        scratch_shapes=[pltpu.SemaphoreType.DMA((9,)), pltpu.SemaphoreType.DMA((9,)), pltpu.SemaphoreType.DMA((3,))],
    )(w_in, w_out, gains)


def pack_w_in(shards):
    depth = shards.shape[0]
    tr = 256

    def body(s_ref, o_ref):
        full = jnp.concatenate([s_ref[n] for n in range(4)], axis=1)
        o_ref[...] = jnp.concatenate([full[:, :SEC_D_Q], full[:, SEC_D_Q + N_HEADS:], full[:, SEC_D_Q:SEC_D_Q + N_HEADS],
                                      jnp.zeros((tr, N_PACK - N_IN), BF16)], axis=1)

    return pl.pallas_call(
        body, name="pack_w_in", grid=(depth, D_MODEL // tr),
        in_specs=[pl.BlockSpec((None, 4, tr, N_SHARD), lambda l, r: (l, 0, r, 0))],
        out_specs=pl.BlockSpec((None, tr, N_PACK), lambda l, r: (l, r, 0)),
        out_shape=jax.ShapeDtypeStruct((depth, D_MODEL, N_PACK), BF16),
        compiler_params=_params(("arbitrary", "arbitrary")),
    )(shards)


def _device_exchange(flows, send_sems, recv_sems, loc_sems):
    x, y, c = lax.axis_index("x"), lax.axis_index("y"), lax.axis_index("c")
    me_chip = 2 * x + y
    me = 4 * x + 2 * y + c
    peers = [(x, y, 1 - c)]
    for px, py in [(1 - x, y), (x, 1 - y), (1 - x, 1 - y)]:
        peers += [(px, py, c), (px, py, 1 - c)]
    nflow = len(flows)

    def local():
        return [pltpu.make_async_copy(src(me_chip), dst(me), loc_sems.at[f]) for f, (src, dst) in enumerate(flows)]

    def copies(n, chip, slot):
        return [pltpu.make_async_remote_copy(src_ref=src(chip), dst_ref=dst(slot), send_sem=send_sems.at[nflow * n + f],
                                             recv_sem=recv_sems.at[nflow * n + f], device_id=peers[n], device_id_type=MESH)
                for f, (src, dst) in enumerate(flows)]

    def start():
        for cp in local():
            cp.start()
        for n, (px, py, _) in enumerate(peers):
            for cp in copies(n, 2 * px + py, me):
                cp.start()

    def wait():
        for n, (px, py, pc) in enumerate(peers):
            for cp in copies(n, me_chip, 4 * px + 2 * py + pc):
                cp.wait_recv()
        for n, (px, py, _) in enumerate(peers):
            for cp in copies(n, 2 * px + py, me):
                cp.wait_send()
        for cp in local():
            cp.wait()

    return start, wait


def _exchange_flows(srcs, dsts):
    return [((lambda s, src=src: src.at[s]) if src.shape[0] == 4 else (lambda s, src=src: src),
             lambda d, dst=dst: dst.at[d]) for src, dst in zip(srcs, dsts)]


def _exchange_shapes(arrays):
    return [jax.ShapeDtypeStruct((8,) + (a.shape[1:] if a.shape[0] == 4 else a.shape), a.dtype) for a in arrays]


def _exchange_sems(n):
    return [pltpu.SemaphoreType.DMA((7 * n,)), pltpu.SemaphoreType.DMA((7 * n,)), pltpu.SemaphoreType.DMA((n,))]


def exchange_grads(*arrays):
    n = len(arrays)

    def body(*refs):
        start, wait = _device_exchange(_exchange_flows(refs[:n], refs[n:2 * n]), *refs[2 * n:])
        start()
        wait()

    any_spec = pl.BlockSpec(memory_space=pl.ANY)
    return pl.pallas_call(
        body, name="exchange_grads",
        in_specs=[any_spec] * n, out_specs=[any_spec] * n, out_shape=_exchange_shapes(arrays),
        scratch_shapes=_exchange_sems(n),
    )(*arrays)


def adamw_reduce(parts, w, m, v, name, tr):
    rows, width = w.shape
    per = rows // len(parts) // tr
    c1 = 1.0 - ADAM_B1 ** ADAM_STEP
    c2 = 1.0 - ADAM_B2 ** ADAM_STEP

    def body(*refs):
        p_refs = refs[:len(parts)]
        w_ref, m_ref, v_ref, g_ref, d_ref, nm_ref, nv_ref = refs[len(parts):]
        i = pl.program_id(0)
        p = p_refs[0][...]
        for n in range(1, len(parts)):
            p = jnp.where(i >= n * per, p_refs[n][...], p)
        g = p[0].astype(F32)
        for n in range(1, 8):
            g = g + p[n].astype(F32)
        g_ref[...] = g
        nm = ADAM_B1 * m_ref[...] + (1.0 - ADAM_B1) * g
        nv = ADAM_B2 * v_ref[...] + (1.0 - ADAM_B2) * (g * g)
        nm_ref[...] = nm
        nv_ref[...] = nv
        d_ref[...] = -ADAM_LR * ((nm / c1) / (jnp.sqrt(nv / c2) + ADAM_EPS) + ADAM_WD * w_ref[...])

    spec = pl.BlockSpec((tr, width), lambda i: (i, 0))
    shape = jax.ShapeDtypeStruct((rows, width), F32)
    return pl.pallas_call(
        body, name=name, grid=(rows // tr,),
        in_specs=[pl.BlockSpec((8, tr, width), lambda i, n=n: (0, jnp.clip(i - n * per, 0, per - 1), 0))
                  for n in range(len(parts))] + [spec, spec, spec],
        out_specs=[spec] * 4, out_shape=[shape] * 4,
        compiler_params=_params(("arbitrary",)),
    )(*parts, w, m, v)


SMALL =("norm_g", "b_f", "rel_bias", "w_s", "b_s", "v_gain", "final_g")
WEIGHTS = ("norm_g", "w_in", "b_f", "rel_bias", "w_s", "b_s", "v_gain", "branch_gain", "w_out", "final_g")
PACK_ROW_TILE = 512


def _rows_of(shape):
    return -(-int(np.prod(shape)) // 128)


def _pack(leaves):
    parts = []
    for a in leaves:
        flat = a.reshape(-1).astype(F32)
        parts.append(jnp.pad(flat, (0, _rows_of(a.shape) * 128 - flat.shape[0])))
    flat = jnp.concatenate(parts)
    rows = flat.shape[0] // 128
    total = -(-rows // PACK_ROW_TILE) * PACK_ROW_TILE
    return jnp.pad(flat, (0, (total - rows) * 128)).reshape(total, 128)


def _unpack(slab, shapes):
    out, row = [], 0
    for shp in shapes:
        n = int(np.prod(shp))
        r = _rows_of(shp)
        out.append(slab[row:row + r].reshape(-1)[:n].reshape(shp))
        row += r
    return out


def kernel(x, norm_g, w_in, b_f, rel_bias, w_s, b_s, v_gain, branch_gain, w_out, final_g, loss_target, m_norm_g, m_w_in, m_b_f, m_rel_bias, m_w_s, m_b_s, m_v_gain, m_branch_gain, m_w_out, m_final_g, v_norm_g, v_w_in, v_b_f, v_rel_bias, v_w_s, v_b_s, v_v_gain, v_branch_gain, v_w_out, v_final_g):
    depth = norm_g.shape[0]
    weights = dict(norm_g=norm_g, w_in=w_in, b_f=b_f, rel_bias=rel_bias, w_s=w_s, b_s=b_s, v_gain=v_gain,
                   branch_gain=branch_gain, w_out=w_out, final_g=final_g)
    mom1 = dict(norm_g=m_norm_g, w_in=m_w_in, b_f=m_b_f, rel_bias=m_rel_bias, w_s=m_w_s, b_s=m_b_s,
                v_gain=m_v_gain, branch_gain=m_branch_gain, w_out=m_w_out, final_g=m_final_g)
    mom2 = dict(norm_g=v_norm_g, w_in=v_w_in, b_f=v_b_f, rel_bias=v_rel_bias, w_s=v_w_s, b_s=v_b_s,
                v_gain=v_v_gain, branch_gain=v_branch_gain, w_out=v_w_out, final_g=v_final_g)

    wf = jnp.pad(branch_gain.reshape(-1), (0, 8 * 128 - branch_gain.size)).reshape(8, 128)
    w_in_b, w_out_b = w_in.astype(BF16), w_out.astype(BF16)
    w_in_shards, w_out_shards, gf = gather_weights(w_in_b[:1], w_out_b[:1], wf)
    bg_full = gf.reshape(4, -1)[:, :branch_gain.size].reshape((4,) + branch_gain.shape)
    bg_full = jnp.moveaxis(bg_full, 0, 2).reshape(depth, 4, D_BRANCH)

    layers = [dict(norm_g=norm_g[l], b_f=b_f[l], rel_bias=rel_bias[l], w_s=w_s[l],
                   b_s=b_s[l], v_gain=v_gain[l], branch_gain=bg_full[l]) for l in range(depth)]
    layers[0].update(wp=pack_w_in(w_in_shards)[0], wout=w_out_shards.reshape(D_MODEL, D_MODEL))
    next_shards = [(w_in_b[l + 1], w_out_b[l + 1]) for l in range(depth - 1)]

    loss_part, grad_x, lgrads, dfinal = local_step(x[0], loss_target[0], layers, final_g, next_shards)
    loss = lax.psum(loss_part, ("x", "y", "c"))

    stack = lambda k: jnp.stack([g[k] for g in lgrads])
    d_gain = jnp.moveaxis(stack("branch_gain").reshape(depth, 4, 4, HEAD_DIM), 2, 0).reshape(4, -1)
    d_gain = jnp.pad(d_gain, ((0, 0), (0, 8 * 128 - d_gain.shape[1]))).reshape(4, 8, 128)
    small = dict(norm_g=stack("norm_g"), b_f=stack("b_f"), rel_bias=stack("rel_bias"), w_s=stack("w_s"),
                 b_s=stack("b_s"), v_gain=stack("v_gain"), final_g=dfinal)
    parts_in, parts_gain, parts_small = exchange_grads(lgrads[0]["w_in_shards"], d_gain, _pack([small[k] for k in SMALL]))
    parts = dict(w_in=[parts_in] + [g["w_in_parts"] for g in lgrads[1:]], w_out=[g["w_out_parts"] for g in lgrads])

    outs = {}
    tags = ("grad", "delta", "new_m", "new_v")
    for k in ("w_in", "w_out"):
        rows = depth * weights[k].shape[1]
        flat = lambda a: a.reshape(rows, a.shape[-1])
        res = adamw_reduce(parts[k], flat(weights[k]), flat(mom1[k]), flat(mom2[k]), "adamw_" + k, 256)
        for tag, a in zip(tags, res):
            outs[tag, k] = a.reshape(weights[k].shape)
    gain8 = lambda a: jnp.pad(a.reshape(-1), (0, 8 * 128 - a.size)).reshape(8, 128)
    res = adamw_reduce([parts_gain], gain8(branch_gain), gain8(m_branch_gain), gain8(v_branch_gain), "adamw_gain", 8)
    for tag, a in zip(tags, res):
        outs[tag, "branch_gain"] = a.reshape(-1)[:branch_gain.size].reshape(branch_gain.shape)
    pack_small = lambda d: _pack([d[k] for k in SMALL])
    res = adamw_reduce([parts_small], pack_small(weights), pack_small(mom1), pack_small(mom2), "adamw_small", PACK_ROW_TILE)
    for tag, slab in zip(tags, res):
        for k, a in zip(SMALL, _unpack(slab, [weights[k].shape for k in SMALL])):
            outs[tag, k] = a
    result = [loss, grad_x[None]]
    for tag in ("grad", "delta", "new_m", "new_v"):
        result += [outs[tag, k] for k in WEIGHTS]
    return tuple(result)
```

```python
import functools

import jax
import jax.numpy as jnp
import numpy as np
from jax import lax
from jax.experimental import pallas as pl
from jax.experimental.pallas import tpu as pltpu

F32 = jnp.float32
BF16 = jnp.bfloat16
MESH = pl.DeviceIdType.MESH

D_MODEL = 1024
D_BRANCH = 256
N_HEADS = 4
HEAD_DIM = 64
CHUNK = 64
LOOKBACK = 8
MAX_REL = 128
SG_CHUNK = 128
EPS = 1e-6
N_IN = 3844
N_PACK = 3968
F_COL = 3840
N_SHARD = 961
NEG = -1e30

A_TQ = 128
A_BAND = A_TQ + LOOKBACK * CHUNK
REL_LO = MAX_REL - (CHUNK - 1)
REL_HI = 2 * MAX_REL + 1
A_PAD = LOOKBACK * CHUNK
A_QB = 1024
ATT_T = 256
FOX_TQ = 512
FOX_WIDE = 4
FOX_DEAD2 = -136.0
LOG2E = 1.4426950408889634
SB_TQ = 1024
SB_SUB = 128
SB_BACK = 256
SB_BAND = SB_SUB + SB_BACK
SB_DEAD = -110.0
ROW_T = 512
VMEM_LIMIT = 56 * 1024 * 1024

ADAM_LR = 0.001
ADAM_B1 = 0.9
ADAM_B2 = 0.999
ADAM_EPS = 1e-08
ADAM_WD = 0.01
ADAM_STEP = 10

SEC_A_Q, SEC_A_K, SEC_A_V, SEC_A_G = 0, 256, 512, 768
SEC_B_U, SEC_B_V, SEC_B_G = 1024, 1280, 1536
SEC_C_Q, SEC_C_K, SEC_C_V, SEC_C_G = 1792, 2048, 2304, 2560
SEC_D_Q, SEC_D_K, SEC_D_V, SEC_D_G = 2816, 3072, 3328, 3584
QKV_SECS = (SEC_A_Q, SEC_C_Q, SEC_C_K, SEC_C_V, SEC_D_Q, SEC_D_K, SEC_D_V)
GATE_SECS = (SEC_A_G, SEC_B_G, SEC_C_G, SEC_D_G)


def _dot(a, b):
    return jnp.dot(a, b, preferred_element_type=F32)


def _dot_nt(a, b):
    return lax.dot_general(a, b, (((1,), (1,)), ((), ())), preferred_element_type=F32)


def _dot_tn(a, b):
    return lax.dot_general(a, b, (((0,), (0,)), ((), ())), preferred_element_type=F32)


def _split2(x):
    hi = x.astype(BF16)
    lo = (x - hi.astype(F32)).astype(BF16)
    return hi, lo


def _split3(x):
    hi = x.astype(BF16)
    r = x - hi.astype(F32)
    mid = r.astype(BF16)
    lo = (r - mid.astype(F32)).astype(BF16)
    return hi, mid, lo


def _sigmoid(x):
    return 1.0 / (1.0 + jnp.exp(-x))


def _params(sem=None, vmem=VMEM_LIMIT):
    return pltpu.CompilerParams(dimension_semantics=sem, vmem_limit_bytes=vmem)


def _heads_to_lanes(ref):
    return jnp.concatenate([ref[h] for h in range(N_HEADS)], axis=1)


def inproj_fwd(x, g, wp):
    s = x.shape[0]
    tm = A_PAD

    def body(x_ref, g_ref, w_ref, h_ref, qkv_ref, kva_ref, gates_ref, uv_ref, f_ref):
        xv = x_ref[...]
        r = lax.rsqrt(jnp.mean(xv * xv, axis=-1, keepdims=True) + EPS)
        h = (xv * r * g_ref[...]).astype(BF16)
        h_ref[...] = h
        for n, off in enumerate(QKV_SECS):
            p = _dot(h, w_ref[:, off:off + D_BRANCH])
            for hh in range(N_HEADS):
                qkv_ref[n, hh] = p[:, hh * HEAD_DIM:(hh + 1) * HEAD_DIM].astype(BF16)
        for n, off in enumerate((SEC_A_K, SEC_A_V)):
            p = _dot(h, w_ref[:, off:off + D_BRANCH])
            for hh in range(N_HEADS):
                kva_ref[n, hh] = p[:, hh * HEAD_DIM:(hh + 1) * HEAD_DIM].astype(BF16)
        for n, off in enumerate(GATE_SECS):
            gates_ref[:, n * D_BRANCH:(n + 1) * D_BRANCH] = _dot(h, w_ref[:, off:off + D_BRANCH])
        uv_ref[...] = _dot(h, w_ref[:, SEC_B_U:SEC_B_U + 2 * D_BRANCH])
        f_ref[...] = _dot(h, w_ref[:, F_COL:F_COL + 128])

    return pl.pallas_call(
        body, name="inproj_fwd", grid=(s // tm,),
        in_specs=[pl.BlockSpec((tm, D_MODEL), lambda i: (i, 0)),
                  pl.BlockSpec((1, D_MODEL), lambda i: (0, 0)),
                  pl.BlockSpec((D_MODEL, N_PACK), lambda i: (0, 0))],
        out_specs=[pl.BlockSpec((tm, D_MODEL), lambda i: (i, 0)),
                   pl.BlockSpec((len(QKV_SECS), N_HEADS, tm, HEAD_DIM), lambda i: (0, 0, i, 0)),
                   pl.BlockSpec((2, N_HEADS, tm, HEAD_DIM), lambda i: (0, 0, i + 1, 0)),
                   pl.BlockSpec((tm, D_MODEL), lambda i: (i, 0)),
                   pl.BlockSpec((tm, 2 * D_BRANCH), lambda i: (i, 0)),
                   pl.BlockSpec((tm, 128), lambda i: (i, 0))],
        out_shape=[jax.ShapeDtypeStruct((s, D_MODEL), BF16),
                   jax.ShapeDtypeStruct((len(QKV_SECS), N_HEADS, s, HEAD_DIM), BF16),
                   jax.ShapeDtypeStruct((2, N_HEADS, s + tm, HEAD_DIM), BF16),
                   jax.ShapeDtypeStruct((s, D_MODEL), F32),
                   jax.ShapeDtypeStruct((s, 2 * D_BRANCH), F32),
                   jax.ShapeDtypeStruct((s, 128), F32)],
        compiler_params=_params(("arbitrary",)),
    )(x, g, wp)


def inproj_bwd(dqkv, dgates, duv, dfp, wp, x, g, dres):
    s = x.shape[0]
    tm = A_PAD

    def body(*refs):
        dq_refs = refs[:9]
        dgates_ref, duv_ref, dfp_ref, w_ref, x_ref, g_ref, dres_ref, dp_ref, dx_ref, dg_ref = refs[9:]
        i = pl.program_id(0)
        a_q, a_k, a_v, c_q, c_k, c_v, d_q, d_k, d_v = [_heads_to_lanes(r).astype(BF16) for r in dq_refs]
        dgt = dgates_ref[...]
        duv_b = duv_ref[...].astype(BF16)
        dp = jnp.concatenate(
            [a_q, a_k, a_v, dgt[:, 0:256], duv_b, dgt[:, 256:512], c_q, c_k, c_v, dgt[:, 512:768],
             d_q, d_k, d_v, dgt[:, 768:1024], dfp_ref[...].astype(BF16)], axis=1)
        dp_ref[...] = dp
        dh = _dot_nt(dp, w_ref[...])
        xv = x_ref[...]
        r = lax.rsqrt(jnp.mean(xv * xv, axis=-1, keepdims=True) + EPS)
        xn = xv * r
        u = dh * g_ref[...]
        dx_ref[...] = dres_ref[...] + r * (u - xn * jnp.mean(xn * u, axis=-1, keepdims=True))

        @pl.when(i == 0)
        def _():
            dg_ref[...] = jnp.zeros_like(dg_ref)

        dg_ref[...] += jnp.sum(dh * xn, axis=0, keepdims=True)

    head_spec = pl.BlockSpec((N_HEADS, tm, HEAD_DIM), lambda i: (0, i, 0))
    padded_spec = pl.BlockSpec((N_HEADS, tm, HEAD_DIM), lambda i: (0, i + 1, 0))
    return pl.pallas_call(
        body, name="inproj_bwd", grid=(s // tm,),
        in_specs=[head_spec, padded_spec, padded_spec] + [head_spec] * 6 + [
            pl.BlockSpec((tm, D_MODEL), lambda i: (i, 0)),
            pl.BlockSpec((tm, 2 * D_BRANCH), lambda i: (i, 0)),
            pl.BlockSpec((tm, 128), lambda i: (i, 0)),
            pl.BlockSpec((D_MODEL, N_PACK), lambda i: (0, 0)),
            pl.BlockSpec((tm, D_MODEL), lambda i: (i, 0)),
            pl.BlockSpec((1, D_MODEL), lambda i: (0, 0)),
            pl.BlockSpec((tm, D_MODEL), lambda i: (i, 0))],
        out_specs=[pl.BlockSpec((tm, N_PACK), lambda i: (i, 0)),
                   pl.BlockSpec((tm, D_MODEL), lambda i: (i, 0)),
                   pl.BlockSpec((1, D_MODEL), lambda i: (0, 0))],
        out_shape=[jax.ShapeDtypeStruct((s, N_PACK), BF16),
                   jax.ShapeDtypeStruct((s, D_MODEL), F32),
                   jax.ShapeDtypeStruct((1, D_MODEL), F32)],
        compiler_params=_params(("arbitrary",)),
    )(*dqkv, dgates, duv, dfp, wp, x, g, dres)


def inproj_wgrad(h, dp):
    s, m = h.shape
    tm = min(2 * ROW_T, s)
    tmm = 256
    nsteps = s // tm

    def body(a_ref, b_ref, o_ref, acc_ref):
        k = pl.program_id(1)

        @pl.when(k == 0)
        def _():
            acc_ref[...] = jnp.zeros_like(acc_ref)

        acc_ref[...] += _dot_tn(a_ref[...], b_ref[...])

        @pl.when(k == nsteps - 1)
        def _():
            acc = acc_ref[...]
            full = jnp.concatenate([acc[:, :SEC_D_Q], acc[:, F_COL:F_COL + N_HEADS], acc[:, SEC_D_Q:F_COL]], axis=1)
            for n in range(4):
                o_ref[n] = full[:, n * N_SHARD:(n + 1) * N_SHARD].astype(BF16)

    return pl.pallas_call(
        body, name="inproj_wgrad", grid=(m // tmm, nsteps),
        in_specs=[pl.BlockSpec((tm, tmm), lambda j, k: (k, j)),
                  pl.BlockSpec((tm, N_PACK), lambda j, k: (k, 0))],
        out_specs=pl.BlockSpec((4, tmm, N_SHARD), lambda j, k: (0, j, 0)),
        out_shape=jax.ShapeDtypeStruct((4, m, N_SHARD), BF16),
        scratch_shapes=[pltpu.VMEM((tmm, N_PACK), F32)],
        compiler_params=_params(("arbitrary", "arbitrary")),
    )(h, dp)


def _a_specs(s):
    nq = s // A_QB
    per = A_QB // A_PAD
    q_spec = pl.BlockSpec((None, None, A_QB, HEAD_DIM), lambda h, i: (0, h, jnp.minimum(i, nq - 1), 0))
    kv_specs = [pl.BlockSpec((None, None, A_PAD, HEAD_DIM),
                             lambda h, i, n=n, m=m: (n, h, jnp.minimum(per * i + m, per * nq), 0))
                for n in range(2) for m in range(per + 1)]
    t_spec = pl.BlockSpec((None, A_TQ, A_BAND), lambda h, i: (h, 0, 0))
    return nq, q_spec, kv_specs, t_spec


def _a_window(refs, i):
    first = refs[0][...]
    return jnp.concatenate([jnp.where(i > 0, first, jnp.zeros_like(first))] + [r[...] for r in refs[1:]], axis=0)


def _a_scores(q_ref, k, t_ref, i, j):
    rows = slice(j * A_TQ, (j + 1) * A_TQ)
    qs = q_ref[rows, :] * 0.125
    kj = k[j * A_TQ:j * A_TQ + A_BAND, :]
    sc = _dot_nt(qs, kj) + t_ref[...]
    col = lax.broadcasted_iota(jnp.int32, (A_TQ, A_BAND), 1)
    sc = jnp.where(col >= A_PAD - i * A_QB - j * A_TQ, sc, NEG)
    return rows, qs, kj, sc


def mix_a_fwd(qkv, kva, tbias):
    s = qkv.shape[2]
    nq, q_spec, kv_specs, t_spec = _a_specs(s)
    nwin = len(kv_specs) // 2

    def body(*refs):
        q_ref, t_ref, o_ref, lse_ref = refs[0], refs[1 + 2 * nwin], refs[2 + 2 * nwin], refs[3 + 2 * nwin]
        i = pl.program_id(1)
        k = _a_window(refs[1:1 + nwin], i)
        v = _a_window(refs[1 + nwin:1 + 2 * nwin], i)
        for j in range(A_QB // A_TQ):
            rows, _, _, sc = _a_scores(q_ref, k, t_ref, i, j)
            m = jnp.max(sc, axis=-1, keepdims=True)
            p = jnp.exp(sc - m)
            l = jnp.sum(p, axis=-1, keepdims=True)
            o_ref[rows, :] = _dot(p.astype(BF16), v[j * A_TQ:j * A_TQ + A_BAND, :]) / l
            lse_ref[rows, :] = m + jnp.log(l)

    return pl.pallas_call(
        body, name="mix_a_fwd", grid=(N_HEADS, nq),
        in_specs=[q_spec] + kv_specs + [t_spec],
        out_specs=[pl.BlockSpec((None, A_QB, HEAD_DIM), lambda h, i: (h, i, 0)),
                   pl.BlockSpec((None, A_QB, 1), lambda h, i: (h, i, 0))],
        out_shape=[jax.ShapeDtypeStruct((N_HEADS, s, HEAD_DIM), F32),
                   jax.ShapeDtypeStruct((N_HEADS, s, 1), F32)],
        compiler_params=_params(("arbitrary", "arbitrary")),
    )(qkv, *([kva] * (2 * nwin)), tbias)


def mix_a_bwd(qkv, kva, tbias, do, o, lse):
    s = qkv.shape[2]
    nq, q_spec, kv_specs, t_spec = _a_specs(s)
    nwin = len(kv_specs) // 2
    row_spec = lambda w: pl.BlockSpec((None, A_QB, w), lambda h, i: (h, jnp.minimum(i, nq - 1), 0))
    done_spec = pl.BlockSpec((None, A_QB, HEAD_DIM), lambda h, i: (h, i, 0))
    win = A_QB + A_PAD

    def body(*refs):
        q_ref = refs[0]
        t_ref, do_ref, o_ref, lse_ref, dq_ref, dk_ref, dv_ref, dt_ref, dk_win, dv_win = refs[1 + 2 * nwin:]
        i = pl.program_id(1)

        @pl.when(i == 0)
        def _():
            dk_win[...] = jnp.zeros_like(dk_win)
            dv_win[...] = jnp.zeros_like(dv_win)
            dt_ref[...] = jnp.zeros_like(dt_ref)

        @pl.when(i < nq)
        def _():
            k = _a_window(refs[1:1 + nwin], i)
            v = _a_window(refs[1 + nwin:1 + 2 * nwin], i)
            dt = jnp.zeros((A_TQ, A_BAND), F32)
            for j in range(A_QB // A_TQ):
                rows, qs, kj, sc = _a_scores(q_ref, k, t_ref, i, j)
                keys = slice(j * A_TQ, j * A_TQ + A_BAND)
                dob = do_ref[rows, :]
                p = jnp.exp(sc - lse_ref[rows, :])
                delta = jnp.sum(o_ref[rows, :] * dob.astype(F32), axis=-1, keepdims=True)
                ds = p * (_dot_nt(dob, v[keys, :]) - delta)
                dsb = ds.astype(BF16)
                dq_ref[rows, :] = _dot(dsb, kj) * 0.125
                dk_win[keys, :] += _dot_tn(dsb, qs)
                dv_win[keys, :] += _dot_tn(p.astype(BF16), dob)
                dt = dt + ds
            dt_ref[...] += dt

        dk_ref[...] = dk_win[0:A_QB, :]
        dv_ref[...] = dv_win[0:A_QB, :]
        dk_rest = dk_win[A_QB:win, :]
        dv_rest = dv_win[A_QB:win, :]
        dk_win[0:A_PAD, :] = dk_rest
        dv_win[0:A_PAD, :] = dv_rest
        dk_win[A_PAD:win, :] = jnp.zeros((A_QB, HEAD_DIM), F32)
        dv_win[A_PAD:win, :] = jnp.zeros((A_QB, HEAD_DIM), F32)

    return pl.pallas_call(
        body, name="mix_a_bwd", grid=(N_HEADS, nq + 1),
        in_specs=[q_spec] + kv_specs + [t_spec, row_spec(HEAD_DIM), row_spec(HEAD_DIM), row_spec(1)],
        out_specs=[row_spec(HEAD_DIM), done_spec, done_spec, t_spec],
        out_shape=[jax.ShapeDtypeStruct((N_HEADS, s, HEAD_DIM), F32),
                   jax.ShapeDtypeStruct((N_HEADS, s + A_QB, HEAD_DIM), F32),
                   jax.ShapeDtypeStruct((N_HEADS, s + A_QB, HEAD_DIM), F32),
                   jax.ShapeDtypeStruct((N_HEADS, A_TQ, A_BAND), F32)],
        scratch_shapes=[pltpu.VMEM((win, HEAD_DIM), F32), pltpu.VMEM((win, HEAD_DIM), F32)],
        compiler_params=_params(("arbitrary", "arbitrary")),
    )(qkv, *([kva] * (2 * nwin)), tbias, do, o, lse)


def relbias_tile(rel_bias, relmat):
    def body(rb_ref, rel_ref, o_ref):
        rel = rel_ref[...]
        o_ref[...] = jnp.full(o_ref.shape, NEG, F32)

        def step(r, carry):
            hit = rel == r
            for h in range(N_HEADS):
                o_ref[h] = jnp.where(hit, rb_ref[h, r], o_ref[h])
            return carry

        lax.fori_loop(REL_LO, REL_HI, step, 0)

    return pl.pallas_call(
        body, name="relbias_tile",
        in_specs=[pl.BlockSpec(memory_space=pltpu.SMEM), pl.BlockSpec(memory_space=pltpu.VMEM)],
        out_specs=pl.BlockSpec(memory_space=pltpu.VMEM),
        out_shape=jax.ShapeDtypeStruct((N_HEADS, A_TQ, A_BAND), F32),
        compiler_params=_params(),
    )(rel_bias, relmat)


def relbias_grad(dt, relmat):
    def body(dt_ref, rel_ref, o_ref):
        rel = rel_ref[...]
        lane = lax.broadcasted_iota(jnp.int32, (8, 384), 1)
        row = lax.broadcasted_iota(jnp.int32, (8, 384), 0)

        def step(r, acc):
            hit = rel == r
            for h in range(N_HEADS):
                val = jnp.sum(jnp.where(hit, dt_ref[h], 0.0))
                acc = jnp.where((lane == r) & (row == h), val, acc)
            return acc

        o_ref[...] = lax.fori_loop(REL_LO, REL_HI, step, jnp.zeros((8, 384), F32))

    return pl.pallas_call(
        body, name="relbias_grad",
        out_shape=jax.ShapeDtypeStruct((8, 384), F32),
        compiler_params=_params(),
    )(dt, relmat)


def _b_norm(v, gain):
    mu = jnp.mean(v, axis=-1, keepdims=True)
    xc = v - mu
    rstd = lax.rsqrt(jnp.mean(xc * xc, axis=-1, keepdims=True) + EPS)
    xhat = xc * rstd
    return xhat, rstd, xhat * gain


def _tril_mask():
    t = lax.broadcasted_iota(jnp.int32, (SG_CHUNK, SG_CHUNK), 0)
    u = lax.broadcasted_iota(jnp.int32, (SG_CHUNK, SG_CHUNK), 1)
    return u <= t


def mix_b_fwd(uv, gain, w_s, b_col):
    s = uv.shape[0]
    tm = min(ROW_T, s)

    def body(uv_ref, gain_ref, w_ref, b_ref, y_ref):
        tril = _tril_mask()
        ws = [jnp.where(tril, w_ref[g], 0.0).astype(BF16) for g in range(N_HEADS)]
        for c in range(tm // SG_CHUNK):
            rows = slice(c * SG_CHUNK, (c + 1) * SG_CHUNK)
            u = uv_ref[rows, 0:D_BRANCH]
            _, _, vn = _b_norm(uv_ref[rows, D_BRANCH:2 * D_BRANCH], gain_ref[...])
            vnb = vn.astype(BF16)
            outs = []
            for g in range(N_HEADS):
                cols = slice(g * HEAD_DIM, (g + 1) * HEAD_DIM)
                mixed = _dot(ws[g], vnb[:, cols]) + b_ref[g]
                outs.append(u[:, cols] * mixed)
            y_ref[rows, :] = jnp.concatenate(outs, axis=1)

    return pl.pallas_call(
        body, name="mix_b_fwd", grid=(s // tm,),
        in_specs=[pl.BlockSpec((tm, 2 * D_BRANCH), lambda i: (i, 0)),
                  pl.BlockSpec((1, D_BRANCH), lambda i: (0, 0)),
                  pl.BlockSpec((N_HEADS, SG_CHUNK, SG_CHUNK), lambda i: (0, 0, 0)),
                  pl.BlockSpec((N_HEADS, SG_CHUNK, 1), lambda i: (0, 0, 0))],
        out_specs=pl.BlockSpec((tm, D_BRANCH), lambda i: (i, 0)),
        out_shape=jax.ShapeDtypeStruct((s, D_BRANCH), F32),
        compiler_params=_params(("arbitrary",)),
    )(uv, gain, w_s, b_col)


def mix_b_bwd(uv, gain, w_s, b_col, dy):
    s = uv.shape[0]
    tm = min(ROW_T, s)

    def body(uv_ref, gain_ref, w_ref, b_ref, dy_ref, duv_ref, dw_ref, db_ref, dgain_ref):
        i = pl.program_id(0)

        @pl.when(i == 0)
        def _():
            dw_ref[...] = jnp.zeros_like(dw_ref)
            db_ref[...] = jnp.zeros_like(db_ref)
            dgain_ref[...] = jnp.zeros_like(dgain_ref)

        tril = _tril_mask()
        ws = [jnp.where(tril, w_ref[g], 0.0).astype(BF16) for g in range(N_HEADS)]
        gain_v = gain_ref[...]
        for c in range(tm // SG_CHUNK):
            rows = slice(c * SG_CHUNK, (c + 1) * SG_CHUNK)
            u = uv_ref[rows, 0:D_BRANCH]
            xhat, rstd, vn = _b_norm(uv_ref[rows, D_BRANCH:2 * D_BRANCH], gain_v)
            vnb = vn.astype(BF16)
            dyv = dy_ref[rows, :]
            dus, dvns = [], []
            for g in range(N_HEADS):
                cols = slice(g * HEAD_DIM, (g + 1) * HEAD_DIM)
                mixed = _dot(ws[g], vnb[:, cols]) + b_ref[g]
                dus.append(dyv[:, cols] * mixed)
                dmixed = dyv[:, cols] * u[:, cols]
                dmb = dmixed.astype(BF16)
                db_ref[g] += jnp.sum(dmixed, axis=-1, keepdims=True)
                dw_ref[g] += jnp.where(tril, _dot_nt(dmb, vnb[:, cols]), 0.0)
                dvns.append(_dot_tn(ws[g], dmb))
            dvn = jnp.concatenate(dvns, axis=1)
            dgain_ref[...] += jnp.sum(dvn * xhat, axis=0, keepdims=True)
            dxh = dvn * gain_v
            dv = rstd * (dxh - jnp.mean(dxh, axis=-1, keepdims=True)
                         - xhat * jnp.mean(dxh * xhat, axis=-1, keepdims=True))
            duv_ref[rows, :] = jnp.concatenate(dus + [dv], axis=1)

    return pl.pallas_call(
        body, name="mix_b_bwd", grid=(s // tm,),
        in_specs=[pl.BlockSpec((tm, 2 * D_BRANCH), lambda i: (i, 0)),
                  pl.BlockSpec((1, D_BRANCH), lambda i: (0, 0)),
                  pl.BlockSpec((N_HEADS, SG_CHUNK, SG_CHUNK), lambda i: (0, 0, 0)),
                  pl.BlockSpec((N_HEADS, SG_CHUNK, 1), lambda i: (0, 0, 0)),
                  pl.BlockSpec((tm, D_BRANCH), lambda i: (i, 0))],
        out_specs=[pl.BlockSpec((tm, 2 * D_BRANCH), lambda i: (i, 0)),
                   pl.BlockSpec((N_HEADS, SG_CHUNK, SG_CHUNK), lambda i: (0, 0, 0)),
                   pl.BlockSpec((N_HEADS, SG_CHUNK, 1), lambda i: (0, 0, 0)),
                   pl.BlockSpec((1, D_BRANCH), lambda i: (0, 0))],
        out_shape=[jax.ShapeDtypeStruct((s, 2 * D_BRANCH), F32),
                   jax.ShapeDtypeStruct((N_HEADS, SG_CHUNK, SG_CHUNK), F32),
                   jax.ShapeDtypeStruct((N_HEADS, SG_CHUNK, 1), F32),
                   jax.ShapeDtypeStruct((1, D_BRANCH), F32)],
        compiler_params=_params(("arbitrary",)),
    )(uv, gain, w_s, b_col, dy)


def _scan_mats(nrow):
    a = lax.broadcasted_iota(jnp.int32, (128, 128), 0)
    b = lax.broadcasted_iota(jnp.int32, (128, 128), 1)
    r = lax.broadcasted_iota(jnp.int32, (nrow, nrow), 0)
    c = lax.broadcasted_iota(jnp.int32, (nrow, nrow), 1)
    nb = nrow // N_HEADS
    same = (r // nb) == (c // nb)
    return a, b, r, c, same


def _exact_dot(x, m):
    hi, mid, lo = _split3(x)
    return _dot(hi, m) + _dot(mid, m) + _dot(lo, m)


def _exact_dot_left(m, x):
    hi, mid, lo = _split3(x)
    return _dot(m, hi) + _dot(m, mid) + _dot(m, lo)


def fox_gate_fwd(ft, bcol):
    nrow = ft.shape[0]

    def body(f_ref, b_ref, c_ref):
        z = f_ref[...] + b_ref[...]
        ls = jnp.minimum(z, 0.0) - jnp.log(1.0 + jnp.exp(-jnp.abs(z)))
        a, b, r, c, same = _scan_mats(nrow)
        within = _exact_dot(ls, (a <= b).astype(BF16))
        tot = jnp.broadcast_to(within[:, 127:128], within.shape)
        before = _exact_dot_left((same & (c < r)).astype(BF16), tot)
        c_ref[...] = within + before

    return pl.pallas_call(
        body, name="fox_gate_fwd",
        out_shape=jax.ShapeDtypeStruct((nrow, 128), F32),
        compiler_params=_params(),
    )(ft, bcol)


def fox_gate_bwd(ft, bcol, dc):
    nrow = ft.shape[0]

    def body(f_ref, b_ref, dc_ref, df_ref, db_ref):
        a, b, r, c, same = _scan_mats(nrow)
        dcv = dc_ref[...]
        within = _exact_dot(dcv, (a >= b).astype(BF16))
        tot = jnp.broadcast_to(within[:, 0:1], within.shape)
        after = _exact_dot_left((same & (c > r)).astype(BF16), tot)
        dls = within + after
        z = f_ref[...] + b_ref[...]
        dz = dls * _sigmoid(-z)
        df_ref[...] = dz
        rs = jnp.broadcast_to(jnp.sum(dz, axis=-1, keepdims=True), dz.shape)
        hr = lax.broadcasted_iota(jnp.int32, (8, nrow), 0)
        hc = lax.broadcasted_iota(jnp.int32, (8, nrow), 1)
        db_ref[...] = _exact_dot_left((hr == hc // (nrow // N_HEADS)).astype(BF16), rs)

    return pl.pallas_call(
        body, name="fox_gate_bwd",
        out_shape=[jax.ShapeDtypeStruct((nrow, 128), F32), jax.ShapeDtypeStruct((8, 128), F32)],
        compiler_params=_params(),
    )(ft, bcol, dc)


def _att_specs(s, qi, ki, vi):
    q_spec = pl.BlockSpec((None, None, FOX_TQ, HEAD_DIM), lambda h, i: (qi, h, i, 0))
    k_spec = pl.BlockSpec((None, None, s, HEAD_DIM), lambda h, i: (ki, h, 0, 0))
    v_spec = pl.BlockSpec((None, None, s, HEAD_DIM), lambda h, i: (vi, h, 0, 0))
    row_spec = lambda w: pl.BlockSpec((None, FOX_TQ, w), lambda h, i: (h, i, 0))
    gate_spec = pl.BlockSpec((None, s // ATT_T, 1, ATT_T), lambda h, i: (h, 0, 0, 0))
    return q_spec, k_spec, v_spec, row_spec, gate_spec


def _causal(strict, n=ATT_T):
    row = lax.broadcasted_iota(jnp.int32, (n, n), 0)
    col = lax.broadcasted_iota(jnp.int32, (n, n), 1)
    return (col < row) if strict else (col <= row)


def _gate_row(cr_ref, kb, g):
    if g == 1:
        return cr_ref[kb]
    return jnp.concatenate([cr_ref[kb + n] for n in range(g)], axis=1)


def _fox_walk(i, carry, tile, alive):
    g = FOX_WIDE
    own = FOX_TQ // ATT_T
    nwide = (own * i) // g
    carry = tile(own * i, own, carry, True)
    carry = lax.fori_loop(0, (own * i - nwide * g) // own, lambda n, c: tile(nwide * g, own, c, False), carry)

    def cond(state):
        return jnp.logical_and(state[0] >= 0, state[1] > 0)

    def step(state):
        n = state[0]
        c = tile(n * g, g, state[2:], False)
        return (n - 1, alive(n * g, c)) + tuple(c)

    out = lax.while_loop(cond, step, (nwide - 1, alive(nwide * g, carry)) + tuple(carry))
    return out[2:]


def _fox_reach(qs, k_ref, kmax_ref, cc, i):
    s = k_ref.shape[0]
    rows = 4 * ATT_T

    @pl.when(i == 0)
    def _():
        def chunk(n, mx):
            kc = k_ref[pl.ds(pl.multiple_of(n * rows, rows), rows), :].astype(F32)
            return jnp.maximum(mx, jnp.max(jnp.sum(kc * kc, axis=-1, keepdims=True)))

        kmax_ref[0] = jnp.sqrt(lax.fori_loop(0, s // rows, chunk, jnp.float32(0.0)))

    qf = qs.astype(F32)
    return jnp.sqrt(jnp.sum(qf * qf, axis=-1, keepdims=True)) * kmax_ref[0] + cc


def _gate_col(cr_ref, i):
    row = lax.broadcasted_iota(jnp.int32, (ATT_T, ATT_T), 0)
    col = lax.broadcasted_iota(jnp.int32, (ATT_T, ATT_T), 1)
    own = FOX_TQ // ATT_T
    return jnp.concatenate([jnp.sum(jnp.where(row == col, cr_ref[own * i + n], 0.0), axis=-1, keepdims=True)
                            for n in range(own)], axis=0)


def _fox_scores(qs, k, cc, crow, masked):
    sc = (_dot_nt(qs, k) + (cc - crow)) * LOG2E
    if masked:
        sc = jnp.where(_causal(False, FOX_TQ), sc, NEG)
    return sc


def fox_fwd(qkv, c_row, ride=()):
    s = qkv.shape[2]
    t = ATT_T
    nq = s // FOX_TQ
    q_spec, k_spec, v_spec, row_spec, gate_spec = _att_specs(s, 1, 2, 3)
    rows = 4 * t
    nride = len(ride)

    def body(q_ref, k_ref, v_ref, cr_ref, *refs):
        ride_in, refs = refs[:nride], refs[nride:]
        o_ref, ref_ref, rl_ref = refs[:3]
        ride_out, refs = refs[3:3 + nride], refs[3 + nride:]
        v1_ref, kmax_ref = refs[:2]
        i = pl.program_id(1)
        if nride:
            h = pl.program_id(0)
            start, wait = _chip_gather([(src, lambda slot, dst=dst: dst.at[slot]) for src, dst in zip(ride_in, ride_out)],
                                       *refs[2:])
            pl.when(jnp.logical_and(h == 0, i == 0))(start)

        @pl.when(i == 0)
        def _():
            def chunk(n, carry):
                r0 = pl.multiple_of(n * rows, rows)
                v1_ref[pl.ds(r0, rows), :] = jnp.concatenate(
                    [v_ref[pl.ds(r0, rows), :], jnp.ones((rows, HEAD_DIM), BF16)], axis=1)
                return carry

            lax.fori_loop(0, s // rows, chunk, 0)

        qs = q_ref[...] * 0.125
        cc = _gate_col(cr_ref, i)
        reach = _fox_reach(qs, k_ref, kmax_ref, cc, i) * LOG2E

        def alive(kb, carry):
            return (jnp.max(reach - cr_ref[kb][:, 0:1] * LOG2E - carry[0]) > FOX_DEAD2).astype(jnp.int32)

        def tile(kb, g, carry, masked):
            m, acc = carry
            k0 = pl.multiple_of(kb * t, t)
            sc = _fox_scores(qs, k_ref[pl.ds(k0, g * t), :], cc, _gate_row(cr_ref, kb, g), masked)
            m_new = jnp.maximum(m, jnp.ceil(jnp.max(sc, axis=-1, keepdims=True)))
            pb = jnp.exp2(sc - m_new).astype(BF16)
            acc = jnp.exp2(m - m_new) * acc + _dot(pb, v1_ref[pl.ds(k0, g * t), :])
            return m_new, acc

        init = (jnp.full((FOX_TQ, 1), NEG, F32), jnp.zeros((FOX_TQ, 2 * HEAD_DIM), F32))
        m, acc = _fox_walk(i, init, tile, alive)
        rl = 1.0 / acc[:, HEAD_DIM:HEAD_DIM + 1]
        o_ref[...] = acc[:, 0:HEAD_DIM] * rl
        ref_ref[...] = m
        rl_ref[...] = rl
        if nride:
            pl.when(jnp.logical_and(h == N_HEADS - 1, i == nq - 1))(wait)

    any_spec = pl.BlockSpec(memory_space=pl.ANY)
    ride_sems = [pltpu.SemaphoreType.DMA((3 * nride,)), pltpu.SemaphoreType.DMA((3 * nride,)),
                 pltpu.SemaphoreType.DMA((nride,))] if nride else []
    return pl.pallas_call(
        body, name="fox_fwd_gather" if nride else "fox_fwd", grid=(N_HEADS, nq),
        in_specs=[q_spec, k_spec, v_spec, gate_spec] + [any_spec] * nride,
        out_specs=[row_spec(HEAD_DIM), row_spec(1), row_spec(1)] + [any_spec] * nride,
        out_shape=[jax.ShapeDtypeStruct((N_HEADS, s, HEAD_DIM), F32),
                   jax.ShapeDtypeStruct((N_HEADS, s, 1), F32),
                   jax.ShapeDtypeStruct((N_HEADS, s, 1), F32)]
        + [jax.ShapeDtypeStruct((4,) + a.shape, a.dtype) for a in ride],
        scratch_shapes=[pltpu.VMEM((s, 2 * HEAD_DIM), BF16), pltpu.SMEM((1,), F32)] + ride_sems,
        compiler_params=_params(("arbitrary", "arbitrary")),
    )(qkv, qkv, qkv, c_row, *ride)


def fox_bwd(qkv, c_row, do, o, ref, rl, ride=()):
    s = qkv.shape[2]
    t = ATT_T
    nq = s // FOX_TQ
    q_spec, k_spec, v_spec, row_spec, gate_spec = _att_specs(s, 1, 2, 3)
    any_spec = pl.BlockSpec(memory_space=pl.ANY)
    nride = len(ride)

    def body(q_ref, k_ref, v_ref, cr_ref, do_ref, o_ref, ref_ref, rl_ref, *refs):
        ride_in, refs = refs[:nride], refs[nride:]
        dq_ref, dk_hbm, dv_hbm, dc_ref = refs[:4]
        ride_out, refs = refs[4:4 + nride], refs[4 + nride:]
        dk_acc, dv_acc, kmax_ref = refs[:3]
        h = pl.program_id(0)
        i = pl.program_id(1)
        if nride:
            start, wait = _device_exchange(_exchange_flows(ride_in, ride_out), *refs[3:])
            pl.when(jnp.logical_and(h == 0, i == 0))(start)

        @pl.when(i == 0)
        def _():
            dk_acc[...] = jnp.zeros_like(dk_acc)
            dv_acc[...] = jnp.zeros_like(dv_acc)
            dc_ref[...] = jnp.zeros_like(dc_ref)

        qs = q_ref[...] * 0.125
        ref = ref_ref[...]
        rl = rl_ref[...]
        dob = (do_ref[...].astype(F32) * rl).astype(BF16)
        delta = jnp.sum(o_ref[...] * dob.astype(F32), axis=-1, keepdims=True)
        cc = _gate_col(cr_ref, i)
        margin = _fox_reach(qs, k_ref, kmax_ref, cc, i) * LOG2E - ref

        def alive(kb, carry):
            return (jnp.max(margin - cr_ref[kb][:, 0:1] * LOG2E) > FOX_DEAD2).astype(jnp.int32)

        def tile(kb, g, carry, masked):
            dq, = carry
            k0 = pl.multiple_of(kb * t, t)
            k = k_ref[pl.ds(k0, g * t), :]
            sc = _fox_scores(qs, k, cc, _gate_row(cr_ref, kb, g), masked)
            wb = jnp.exp2(sc - ref).astype(BF16)
            ds = wb.astype(F32) * (_dot_nt(dob, v_ref[pl.ds(k0, g * t), :]) - delta)
            dsb = ds.astype(BF16)
            dk_acc[pl.ds(k0, g * t), :] += _dot_tn(dsb, qs)
            dv_acc[pl.ds(k0, g * t), :] += _dot_tn(wb, dob)
            dcs = -jnp.sum(ds, axis=0, keepdims=True)
            for n in range(g):
                dc_ref[kb + n] += dcs[:, n * t:(n + 1) * t]
            return (dq + _dot(dsb, k),)

        dq, = _fox_walk(i, (jnp.zeros((FOX_TQ, HEAD_DIM), F32),), tile, alive)
        dq_ref[...] = dq * 0.125

        @pl.when(i == nq - 1)
        def _():
            pltpu.sync_copy(dk_acc, dk_hbm.at[h])
            pltpu.sync_copy(dv_acc, dv_hbm.at[h])

        if nride:
            pl.when(jnp.logical_and(h == N_HEADS - 1, i == nq - 1))(wait)

    return pl.pallas_call(
        body, name="fox_bwd_exchange" if nride else "fox_bwd", grid=(N_HEADS, nq),
        in_specs=[q_spec, k_spec, v_spec,
                  gate_spec,
                  row_spec(HEAD_DIM), row_spec(HEAD_DIM), row_spec(1), row_spec(1)] + [any_spec] * nride,
        out_specs=[row_spec(HEAD_DIM), any_spec, any_spec,
                   gate_spec] + [any_spec] * nride,
        out_shape=[jax.ShapeDtypeStruct((N_HEADS, s, HEAD_DIM), F32),
                   jax.ShapeDtypeStruct((N_HEADS, s, HEAD_DIM), F32),
                   jax.ShapeDtypeStruct((N_HEADS, s, HEAD_DIM), F32),
                   jax.ShapeDtypeStruct((N_HEADS, s // t, 1, t), F32)] + _exchange_shapes(ride),
        scratch_shapes=[pltpu.VMEM((s, HEAD_DIM), F32), pltpu.VMEM((s, HEAD_DIM), F32), pltpu.SMEM((1,), F32)]
        + (_exchange_sems(nride) if nride else []),
        compiler_params=_params(("arbitrary", "arbitrary")),
    )(qkv, qkv, qkv, c_row, do, o, ref, rl, *ride)


def _sb_valid(nrows, ahead):
    row = lax.broadcasted_iota(jnp.int32, (nrows, ATT_T), 0)
    col = lax.broadcasted_iota(jnp.int32, (nrows, ATT_T), 1)
    return col + ahead < row


def _sb_band_valid(nsub):
    row = lax.broadcasted_iota(jnp.int32, (nsub * SB_SUB, SB_BAND), 0)
    col = lax.broadcasted_iota(jnp.int32, (nsub * SB_SUB, SB_BAND), 1)
    return col < (row & (SB_SUB - 1)) + SB_BACK


def _sb_logits(qs, k):
    z = _dot_nt(qs, k)
    sp = jnp.log(1.0 + jnp.exp(-jnp.abs(z)))
    return jnp.minimum(z, 0.0) - sp, -jnp.maximum(z, 0.0) - sp


def _sb_weights(ls, lm, run, valid):
    if valid is not None:
        lm = jnp.where(valid, lm, 0.0)
    n = lm.shape[1]
    row = lax.broadcasted_iota(jnp.int32, (n, n), 0)
    col = lax.broadcasted_iota(jnp.int32, (n, n), 1)
    later = (row > col).astype(BF16)
    hi, lo = _split2(lm)
    between = _dot(hi, later) + _dot(lo, later)
    if run is not None:
        between = run + between
    a = jnp.exp(ls + between)
    if valid is not None:
        a = jnp.where(valid, a, 0.0)
    return lm, a


def _sb_band_start(i, j):
    return pl.multiple_of(i * SB_TQ + j * SB_SUB - SB_BACK, SB_SUB)


def _sb_tile(qs, k, run, valid):
    ls, lm = _sb_logits(qs, k)
    lm, a = _sb_weights(ls, lm, run, valid)
    return ls, lm, a


def _sb_band(i, qs_all, k_ref):
    nsub = qs_all.shape[0] // SB_SUB
    valid = _sb_band_valid(nsub)
    starts = [_sb_band_start(i, j) for j in range(nsub)]
    kwins = [k_ref[pl.ds(k0, SB_BAND), :] for k0 in starts]
    parts = [_sb_logits(qs_all[j * SB_SUB:(j + 1) * SB_SUB], kwins[j]) for j in range(nsub)]
    ls = jnp.concatenate([p[0] for p in parts], axis=0)
    lm, a = _sb_weights(ls, jnp.concatenate([p[1] for p in parts], axis=0), None, valid)
    return starts, kwins, ls, lm, a, valid


def _sb_suffix(g, run_g):
    n = g.shape[1]
    row = lax.broadcasted_iota(jnp.int32, (n, n), 0)
    col = lax.broadcasted_iota(jnp.int32, (n, n), 1)
    from_here = (row >= col).astype(BF16)
    hi, lo = _split2(g)
    out = _dot(hi, from_here) + _dot(lo, from_here)
    return out if run_g is None else run_g + out


def _sb_walk(i, carry, tile):
    def alive_of(c):
        return (jnp.max(c[0]) > SB_DEAD).astype(jnp.int32)

    def cond(state):
        n, alive = state[0], state[1]
        return jnp.logical_and(n < i, alive > 0)

    def step(state):
        n = state[0]
        c = tile(i - 1 - n, state[2:], False)
        return (n + 1, alive_of(c)) + tuple(c)

    out = lax.while_loop(cond, step, (jnp.int32(0), alive_of(carry)) + tuple(carry))
    return out[2:]


def _sb_specs(s):
    tq = SB_TQ
    q_spec = pl.BlockSpec((None, None, tq, HEAD_DIM), lambda h, i: (4, h, i, 0))
    k_spec = pl.BlockSpec((None, None, s, HEAD_DIM), lambda h, i: (5, h, 0, 0))
    v_spec = pl.BlockSpec((None, None, s, HEAD_DIM), lambda h, i: (6, h, 0, 0))
    row_spec = pl.BlockSpec((None, tq, HEAD_DIM), lambda h, i: (h, i, 0))
    band_spec = pl.BlockSpec((None, None, 1, 128), lambda h, i: (h, i, 0, 0))
    return tq, q_spec, k_spec, v_spec, row_spec, band_spec


def _sb_block(b, row0, tile, zero):
    t = ATT_T
    lo, hi, both = slice(row0, row0 + t), slice(row0 + t, row0 + 2 * t), slice(row0, row0 + 2 * t)
    c_hi = tile(2 * b + 1, hi, zero, 0)
    c_lo = tile(2 * b, lo, zero, 0)
    c_hi = tile(2 * b, hi, c_hi, None)
    carry = tuple(jnp.concatenate([x, y], axis=0) for x, y in zip(c_lo, c_hi))
    return _sb_walk(2 * b, carry, lambda kb, c, _: tile(kb, both, c, None))


def sb_fwd(qkv):
    s = qkv.shape[2]
    t = ATT_T
    tq, q_spec, k_spec, v_spec, row_spec, band_spec = _sb_specs(s)

    def body(q_ref, k_ref, v_ref, o_ref, band_ref, done_ref):
        i = pl.program_id(1)
        qs = q_ref[...] * 0.125
        done_ref[0] = 0

        @pl.when(i > 0)
        def _():
            starts, _, _, lm, a, _ = _sb_band(i, qs, k_ref)
            ab = a.astype(BF16)
            for j, k0 in enumerate(starts):
                rows = slice(j * SB_SUB, (j + 1) * SB_SUB)
                o_ref[rows, :] = _dot(ab[rows], v_ref[pl.ds(k0, SB_BAND), :])
            worst = jnp.max(jnp.sum(lm, axis=-1, keepdims=True))
            done_ref[0] = (worst <= SB_DEAD).astype(jnp.int32)

        @pl.when(done_ref[0] == 0)
        def _():
            def tile(kb, rows, carry, ahead):
                run, acc = carry
                k0 = pl.multiple_of(kb * t, t)
                valid = None if ahead is None else _sb_valid(t, ahead)
                _, lm, a = _sb_tile(qs[rows], k_ref[pl.ds(k0, t), :], run, valid)
                acc = acc + _dot(a.astype(BF16), v_ref[pl.ds(k0, t), :])
                return run + jnp.sum(lm, axis=-1, keepdims=True), acc

            for n in range(tq // (2 * t)):
                _, acc = _sb_block(i * (tq // (2 * t)) + n, n * 2 * t, tile,
                                   (jnp.zeros((t, 1), F32), jnp.zeros((t, HEAD_DIM), F32)))
                o_ref[n * 2 * t:(n + 1) * 2 * t, :] = acc

        band_ref[...] = jnp.full(band_ref.shape, done_ref[0], jnp.int32).astype(F32)

    return pl.pallas_call(
        body, name="sb_fwd", grid=(N_HEADS, s // tq),
        in_specs=[q_spec, k_spec, v_spec],
        out_specs=[row_spec, band_spec],
        out_shape=[jax.ShapeDtypeStruct((N_HEADS, s, HEAD_DIM), F32),
                   jax.ShapeDtypeStruct((N_HEADS, s // tq, 1, 128), F32)],
        scratch_shapes=[pltpu.SMEM((1,), jnp.int32)],
        compiler_params=_params(("arbitrary", "arbitrary")),
    )(qkv, qkv, qkv)


def sb_bwd(qkv, do, o, band):
    s = qkv.shape[2]
    t = ATT_T
    tq, q_spec, k_spec, v_spec, row_spec, band_spec = _sb_specs(s)
    nq = s // tq
    any_spec = pl.BlockSpec(memory_space=pl.ANY)

    def body(q_ref, k_ref, v_ref, do_ref, o_ref, band_ref, dq_ref, dk_hbm, dv_hbm, dk_acc, dv_acc):
        h = pl.program_id(0)
        i = pl.program_id(1)

        @pl.when(i == 0)
        def _():
            dk_acc[...] = jnp.zeros_like(dk_acc)
            dv_acc[...] = jnp.zeros_like(dv_acc)

        qs_all = q_ref[...] * 0.125
        dob_all = do_ref[...]
        tot_all = jnp.sum(o_ref[...] * dob_all.astype(F32), axis=-1, keepdims=True)
        on_band = jnp.max(band_ref[...]) > 0.5

        def grads(qs, dob, tot, k, v, k0, run, run_g, valid):
            ls, lm, a = _sb_tile(qs, k, run, valid)
            ab = a.astype(BF16)
            g = ab.astype(F32) * _dot_nt(dob, v)
            g_left = tot - _sb_suffix(g, run_g)
            dz = g - jnp.exp(ls) * (g + g_left)
            if valid is not None:
                dz = jnp.where(valid, dz, 0.0)
            dzb = dz.astype(BF16)
            n = k.shape[0]
            dk_acc[pl.ds(k0, n), :] += _dot_tn(dzb, qs)
            dv_acc[pl.ds(k0, n), :] += _dot_tn(ab, dob)
            return dzb, lm, g

        @pl.when(on_band)
        def _():
            starts, kwins, ls, _, a, valid = _sb_band(i, qs_all, k_ref)
            ab = a.astype(BF16)
            subs = [slice(j * SB_SUB, (j + 1) * SB_SUB) for j in range(len(starts))]
            vwins = [v_ref[pl.ds(k0, SB_BAND), :] for k0 in starts]
            g = ab.astype(F32) * jnp.concatenate([_dot_nt(dob_all[r], v) for r, v in zip(subs, vwins)], axis=0)
            dz = jnp.where(valid, g - jnp.exp(ls) * (g + (tot_all - _sb_suffix(g, None))), 0.0)
            dzb = dz.astype(BF16)
            for r, k0, k in zip(subs, starts, kwins):
                dq_ref[r, :] = _dot(dzb[r], k) * 0.125
                dk_acc[pl.ds(k0, SB_BAND), :] += _dot_tn(dzb[r], qs_all[r])
                dv_acc[pl.ds(k0, SB_BAND), :] += _dot_tn(ab[r], dob_all[r])

        @pl.when(jnp.logical_not(on_band))
        def _():
            def tile(kb, rows, carry, ahead):
                run, run_g, dq = carry
                k0 = pl.multiple_of(kb * t, t)
                k = k_ref[pl.ds(k0, t), :]
                valid = None if ahead is None else _sb_valid(t, ahead)
                dzb, lm, g = grads(qs_all[rows], dob_all[rows], tot_all[rows], k, v_ref[pl.ds(k0, t), :], k0,
                                   run, run_g, valid)
                return (run + jnp.sum(lm, axis=-1, keepdims=True),
                        run_g + jnp.sum(g, axis=-1, keepdims=True),
                        dq + _dot(dzb, k))

            zero = jnp.zeros((t, 1), F32)
            for n in range(tq // (2 * t)):
                _, _, dq = _sb_block(i * (tq // (2 * t)) + n, n * 2 * t, tile, (zero, zero, jnp.zeros((t, HEAD_DIM), F32)))
                dq_ref[n * 2 * t:(n + 1) * 2 * t, :] = dq * 0.125

        @pl.when(i == nq - 1)
        def _():
            pltpu.sync_copy(dk_acc, dk_hbm.at[h])
            pltpu.sync_copy(dv_acc, dv_hbm.at[h])

    return pl.pallas_call(
        body, name="sb_bwd", grid=(N_HEADS, nq),
        in_specs=[q_spec, k_spec, v_spec, row_spec, row_spec, band_spec],
        out_specs=[row_spec, any_spec, any_spec],
        out_shape=[jax.ShapeDtypeStruct((N_HEADS, s, HEAD_DIM), F32)] * 3,
        scratch_shapes=[pltpu.VMEM((s, HEAD_DIM), F32), pltpu.VMEM((s, HEAD_DIM), F32)],
        compiler_params=_params(("arbitrary", "arbitrary")),
    )(qkv, qkv, qkv, do, o, band)


def _branch_inputs(refs, br):
    ya_ref, yb_ref, yc_ref, yd_ref = refs
    if br == 1:
        return yb_ref[...]
    return _heads_to_lanes((ya_ref, None, yc_ref, yd_ref)[br])


def outproj_fwd(x, ya, yb, yc, yd, gates, bg, wout):
    s = x.shape[0]
    tm = min(ROW_T, s)

    def body(x_ref, ya_ref, yb_ref, yc_ref, yd_ref, gates_ref, bg_ref, w_ref, out_ref):
        pieces = []
        for br in range(4):
            cols = slice(br * D_BRANCH, (br + 1) * D_BRANCH)
            y = _branch_inputs((ya_ref, yb_ref, yc_ref, yd_ref), br)
            r = lax.rsqrt(jnp.mean(y * y, axis=-1, keepdims=True) + EPS)
            gt = gates_ref[:, cols]
            pieces.append((y * r * bg_ref[:, cols]) * (gt * _sigmoid(gt)))
        merged = jnp.concatenate(pieces, axis=1).astype(BF16)
        out_ref[...] = x_ref[...] + _dot(merged, w_ref[...])

    head_spec = pl.BlockSpec((N_HEADS, tm, HEAD_DIM), lambda i: (0, i, 0))
    return pl.pallas_call(
        body, name="outproj_fwd", grid=(s // tm,),
        in_specs=[pl.BlockSpec((tm, D_MODEL), lambda i: (i, 0)),
                  head_spec, pl.BlockSpec((tm, D_BRANCH), lambda i: (i, 0)), head_spec, head_spec,
                  pl.BlockSpec((tm, D_MODEL), lambda i: (i, 0)),
                  pl.BlockSpec((1, D_MODEL), lambda i: (0, 0)),
                  pl.BlockSpec((D_MODEL, D_MODEL), lambda i: (0, 0))],
        out_specs=pl.BlockSpec((tm, D_MODEL), lambda i: (i, 0)),
        out_shape=jax.ShapeDtypeStruct((s, D_MODEL), F32),
        compiler_params=_params(("arbitrary",)),
    )(x, ya, yb, yc, yd, gates, bg, wout)


def outproj_bwd(dout, ya, yb, yc, yd, gates, bg, wout):
    s = dout.shape[0]
    tm = min(ROW_T, s)

    def body(dout_ref, ya_ref, yb_ref, yc_ref, yd_ref, gates_ref, bg_ref, w_ref,
             dya_ref, dyb_ref, dyc_ref, dyd_ref, dgates_ref, dbg_ref, dw_ref):
        i = pl.program_id(0)

        @pl.when(i == 0)
        def _():
            dbg_ref[...] = jnp.zeros_like(dbg_ref)
            dw_ref[...] = jnp.zeros_like(dw_ref)

        doutb = dout_ref[...].astype(BF16)
        dmerged = _dot_nt(doutb, w_ref[...])
        pieces = []
        for br in range(4):
            cols = slice(br * D_BRANCH, (br + 1) * D_BRANCH)
            y = _branch_inputs((ya_ref, yb_ref, yc_ref, yd_ref), br)
            r = lax.rsqrt(jnp.mean(y * y, axis=-1, keepdims=True) + EPS)
            yn = y * r
            bgv = bg_ref[:, cols]
            gt = gates_ref[:, cols]
            sig = _sigmoid(gt)
            act = gt * sig
            n = yn * bgv
            pieces.append(n * act)
            dm = dmerged[:, cols]
            dn = dm * act
            dgates_ref[:, cols] = (dm * n * (sig * (1.0 + gt * (1.0 - sig)))).astype(BF16)
            dbg_ref[:, cols] += jnp.sum(dn * yn, axis=0, keepdims=True)
            u = dn * bgv
            dy = r * (u - yn * jnp.mean(yn * u, axis=-1, keepdims=True))
            if br == 1:
                dyb_ref[...] = dy
            else:
                dref = (dya_ref, None, dyc_ref, dyd_ref)[br]
                for hh in range(N_HEADS):
                    dref[hh] = dy[:, hh * HEAD_DIM:(hh + 1) * HEAD_DIM].astype(BF16)
        merged = jnp.concatenate(pieces, axis=1).astype(BF16)
        dw_ref[...] += _dot_tn(merged, doutb)

    head_spec = pl.BlockSpec((N_HEADS, tm, HEAD_DIM), lambda i: (0, i, 0))
    head_shape = jax.ShapeDtypeStruct((N_HEADS, s, HEAD_DIM), BF16)
    return pl.pallas_call(
        body, name="outproj_bwd", grid=(s // tm,),
        in_specs=[pl.BlockSpec((tm, D_MODEL), lambda i: (i, 0)),
                  head_spec, pl.BlockSpec((tm, D_BRANCH), lambda i: (i, 0)), head_spec, head_spec,
                  pl.BlockSpec((tm, D_MODEL), lambda i: (i, 0)),
                  pl.BlockSpec((1, D_MODEL), lambda i: (0, 0)),
                  pl.BlockSpec((D_MODEL, D_MODEL), lambda i: (0, 0))],
        out_specs=[head_spec, pl.BlockSpec((tm, D_BRANCH), lambda i: (i, 0)), head_spec, head_spec,
                   pl.BlockSpec((tm, D_MODEL), lambda i: (i, 0)),
                   pl.BlockSpec((1, D_MODEL), lambda i: (0, 0)),
                   pl.BlockSpec((D_MODEL, D_MODEL), lambda i: (0, 0))],
        out_shape=[head_shape, jax.ShapeDtypeStruct((s, D_BRANCH), F32), head_shape, head_shape,
                   jax.ShapeDtypeStruct((s, D_MODEL), BF16),
                   jax.ShapeDtypeStruct((1, D_MODEL), F32),
                   jax.ShapeDtypeStruct((D_MODEL, D_MODEL), F32)],
        compiler_params=_params(("arbitrary",)),
    )(dout, ya, yb, yc, yd, gates, bg, wout)


def final_loss(x, tgt, g):
    s = x.shape[0]
    tm = min(ROW_T, s)

    def body(x_ref, t_ref, g_ref, loss_ref, dx_ref, dg_ref):
        i = pl.program_id(0)

        @pl.when(i == 0)
        def _():
            loss_ref[...] = jnp.zeros_like(loss_ref)
            dg_ref[...] = jnp.zeros_like(dg_ref)

        xv = x_ref[...]
        gv = g_ref[...]
        r = lax.rsqrt(jnp.mean(xv * xv, axis=-1, keepdims=True) + EPS)
        xn = xv * r
        err = xn * gv - t_ref[...]
        loss_ref[...] += jnp.sum(err * err) * (0.5 / D_MODEL)
        dy = err * (1.0 / D_MODEL)
        u = dy * gv
        dx_ref[...] = r * (u - xn * jnp.mean(xn * u, axis=-1, keepdims=True))
        dg_ref[...] += jnp.sum(dy * xn, axis=0, keepdims=True)

    return pl.pallas_call(
        body, name="final_loss", grid=(s // tm,),
        in_specs=[pl.BlockSpec((tm, D_MODEL), lambda i: (i, 0)),
                  pl.BlockSpec((tm, D_MODEL), lambda i: (i, 0)),
                  pl.BlockSpec((1, D_MODEL), lambda i: (0, 0))],
        out_specs=[pl.BlockSpec((1, 128), lambda i: (0, 0)),
                   pl.BlockSpec((tm, D_MODEL), lambda i: (i, 0)),
                   pl.BlockSpec((1, D_MODEL), lambda i: (0, 0))],
        out_shape=[jax.ShapeDtypeStruct((1, 128), F32),
                   jax.ShapeDtypeStruct((s, D_MODEL), F32),
                   jax.ShapeDtypeStruct((1, D_MODEL), F32)],
        compiler_params=_params(("arbitrary",)),
    )(x, tgt, g)


def _rel_index():
    i = np.arange(A_TQ)[:, None]
    j = np.arange(A_BAND)[None, :]
    rel = np.clip(i - j + (A_BAND - A_TQ), -MAX_REL, MAX_REL) + MAX_REL
    dchunk = i // CHUNK + LOOKBACK - j // CHUNK
    valid = (dchunk >= 0) & (dchunk <= LOOKBACK)
    return jnp.asarray(np.where(valid, rel, -1).astype(np.int32))


def _layer_consts(p):
    tbias = relbias_tile(p["rel_bias"], _rel_index())
    return dict(
        norm_g=p["norm_g"].reshape(1, D_MODEL),
        v_gain=p["v_gain"].reshape(1, D_BRANCH),
        b_col=p["b_s"].reshape(N_HEADS, SG_CHUNK, 1),
        bg=p["branch_gain"].reshape(1, D_MODEL),
        tbias=tbias,
    )


def _gate_layout(fp, b_f, s):
    nb = s // 128
    ft = fp[:, :N_HEADS].T.reshape(N_HEADS * nb, 128)
    bcol = jnp.repeat(b_f, nb).reshape(N_HEADS * nb, 1)
    return ft, bcol


def layer_fwd(x, p, ride=()):
    s = x.shape[0]
    c = _layer_consts(p)
    h, qkv, kva, gates, uv, fp = inproj_fwd(x, c["norm_g"], p["wp"])
    ya, lse_a = mix_a_fwd(qkv, kva, c["tbias"])
    yb = mix_b_fwd(uv, c["v_gain"], p["w_s"], c["b_col"])
    ft, bcol = _gate_layout(fp, p["b_f"], s)
    c_row = fox_gate_fwd(ft, bcol).reshape(N_HEADS, s // ATT_T, 1, ATT_T)
    yc, ref_c, rl_c, *rode = fox_fwd(qkv, c_row, ride)
    yd, band_d = sb_fwd(qkv)
    out = outproj_fwd(x, ya, yb, yc, yd, gates, c["bg"], p["wout"])
    saved = dict(consts=c, x=x, h=h, qkv=qkv, gates=gates, uv=uv, kva=kva, ft=ft, bcol=bcol,
                 c_row=c_row, ya=ya, lse_a=lse_a, yb=yb, yc=yc, ref_c=ref_c, rl_c=rl_c, yd=yd, band_d=band_d)
    return out, saved, rode


def layer_bwd(dout, p, sv, exchange=False, upper_w_in=None):
    s = dout.shape[0]
    c = sv["consts"]
    dya, dyb, dyc, dyd, dgates, dbg, dwout = outproj_bwd(
        dout, sv["ya"], sv["yb"], sv["yc"], sv["yd"], sv["gates"], c["bg"], p["wout"])
    dqa, dka, dva, dt = mix_a_bwd(sv["qkv"], sv["kva"], c["tbias"], dya, sv["ya"], sv["lse_a"])
    drel = relbias_grad(dt, _rel_index())[:N_HEADS, :2 * MAX_REL + 1]
    duv, dws, dbs, dvgain = mix_b_bwd(sv["uv"], c["v_gain"], p["w_s"], c["b_col"], dyb)
    ride = [dwout.astype(BF16).reshape(4, D_BRANCH, D_MODEL)] if exchange else []
    if upper_w_in is not None:
        ride.append(upper_w_in)
    dqc, dkc, dvc, dc, *rode = fox_bwd(sv["qkv"], sv["c_row"], dyc, sv["yc"], sv["ref_c"], sv["rl_c"], ride)
    dft, dbf = fox_gate_bwd(sv["ft"], sv["bcol"], dc.reshape(N_HEADS * (s // 128), 128))
    dfp = jnp.pad(dft.reshape(N_HEADS, s).T, ((0, 0), (0, 128 - N_HEADS)))
    dqd, dkd, dvd = sb_bwd(sv["qkv"], dyd, sv["yd"], sv["band_d"])
    dp, dx, dnorm = inproj_bwd((dqa, dka, dva, dqc, dkc, dvc, dqd, dkd, dvd), dgates, duv, dfp,
                               p["wp"], sv["x"], c["norm_g"], dout)
    grads = dict(norm_g=dnorm.reshape(D_MODEL), w_in_shards=inproj_wgrad(sv["h"], dp), b_f=dbf[:N_HEADS, 0], rel_bias=drel,
                 w_s=dws, b_s=dbs.reshape(N_HEADS, SG_CHUNK), v_gain=dvgain.reshape(D_BRANCH),
                 branch_gain=dbg.reshape(4, D_BRANCH), wout=dwout)
    if exchange:
        grads["w_out_parts"] = rode[0]
    return dx, grads, (rode[1] if upper_w_in is not None else None)


def local_step(x, tgt, layers, final_g, next_shards=None):
    layers = list(layers)
    saved = []
    cur = x
    for l, p in enumerate(layers):
        ride = next_shards[l] if next_shards is not None and l + 1 < len(layers) else ()
        cur, sv, rode = layer_fwd(cur, p, ride)
        saved.append(sv)
        if ride:
            layers[l + 1] = dict(layers[l + 1], wp=pack_w_in(rode[0][None])[0], wout=rode[1].reshape(D_MODEL, D_MODEL))
    loss, dcur, dfinal = final_loss(cur, tgt, final_g.reshape(1, D_MODEL))
    grads = [None] * len(layers)
    for l in reversed(range(len(layers))):
        exchange = next_shards is not None
        upper = grads[l + 1]["w_in_shards"] if exchange and l + 1 < len(layers) else None
        dcur, grads[l], got = layer_bwd(dcur, layers[l], saved[l], exchange, upper)
        if upper is not None:
            grads[l + 1]["w_in_parts"] = got
    return loss[0, 0], dcur, grads, dfinal.reshape(D_MODEL)


def _chip_gather(pairs, send_sems, recv_sems, loc_sems):
    x, y, c = lax.axis_index("x"), lax.axis_index("y"), lax.axis_index("c")
    me = 2 * x + y
    chips = [(1 - x, y), (x, 1 - y), (1 - x, 1 - y)]
    npair = len(pairs)

    def local():
        return [pltpu.make_async_copy(src, dst(me), loc_sems.at[n]) for n, (src, dst) in enumerate(pairs)]

    def remote(j, n, slot):
        src, dst = pairs[n]
        return pltpu.make_async_remote_copy(
            src_ref=src, dst_ref=dst(slot), send_sem=send_sems.at[npair * j + n], recv_sem=recv_sems.at[npair * j + n],
            device_id=(chips[j][0], chips[j][1], c), device_id_type=MESH)

    def start():
        for cp in local():
            cp.start()
        for j in range(3):
            for n in range(npair):
                remote(j, n, me).start()

    def wait():
        for j in range(3):
            for n in range(npair):
                remote(j, n, 2 * chips[j][0] + chips[j][1]).wait_recv()
        for j in range(3):
            for n in range(npair):
                remote(j, n, me).wait_send()
        for cp in local():
            cp.wait()

    return start, wait


def gather_weights(w_in, w_out, gains):
    depth = w_in.shape[0]

    def body(in_ref, out_ref, g_ref, oin_ref, oout_ref, og_ref, send_sems, recv_sems, loc_sems):
        pairs = [(in_ref, lambda s: oin_ref.at[:, s]), (out_ref, lambda s: oout_ref.at[:, s]), (g_ref, lambda s: og_ref.at[s])]
        start, wait = _chip_gather(pairs, send_sems, recv_sems, loc_sems)
        start()
        wait()

    any_spec = pl.BlockSpec(memory_space=pl.ANY)
    return pl.pallas_call(
        body, name="gather_weights",
        in_specs=[any_spec] * 3, out_specs=[any_spec] * 3,
        out_shape=[jax.ShapeDtypeStruct((depth, 4) + w_in.shape[1:], w_in.dtype),
                   jax.ShapeDtypeStruct((depth, 4) + w_out.shape[1:], w_out.dtype),
                   jax.ShapeDtypeStruct((4,) + gains.shape, gains.dtype)],
        scratch_shapes=[pltpu.SemaphoreType.DMA((9,)), pltpu.SemaphoreType.DMA((9,)), pltpu.SemaphoreType.DMA((3,))],
    )(w_in, w_out, gains)


def pack_w_in(shards):
    depth = shards.shape[0]
    tr = 256

    def body(s_ref, o_ref):
        full = jnp.concatenate([s_ref[n] for n in range(4)], axis=1)
        o_ref[...] = jnp.concatenate([full[:, :SEC_D_Q], full[:, SEC_D_Q + N_HEADS:], full[:, SEC_D_Q:SEC_D_Q + N_HEADS],
                                      jnp.zeros((tr, N_PACK - N_IN), BF16)], axis=1)

    return pl.pallas_call(
        body, name="pack_w_in", grid=(depth, D_MODEL // tr),
        in_specs=[pl.BlockSpec((None, 4, tr, N_SHARD), lambda l, r: (l, 0, r, 0))],
        out_specs=pl.BlockSpec((None, tr, N_PACK), lambda l, r: (l, r, 0)),
        out_shape=jax.ShapeDtypeStruct((depth, D_MODEL, N_PACK), BF16),
        compiler_params=_params(("arbitrary", "arbitrary")),
    )(shards)


def _device_exchange(flows, send_sems, recv_sems, loc_sems):
    x, y, c = lax.axis_index("x"), lax.axis_index("y"), lax.axis_index("c")
    me_chip = 2 * x + y
    me = 4 * x + 2 * y + c
    peers = [(x, y, 1 - c)]
    for px, py in [(1 - x, y), (x, 1 - y), (1 - x, 1 - y)]:
        peers += [(px, py, c), (px, py, 1 - c)]
    nflow = len(flows)

    def local():
        return [pltpu.make_async_copy(src(me_chip), dst(me), loc_sems.at[f]) for f, (src, dst) in enumerate(flows)]

    def copies(n, chip, slot):
        return [pltpu.make_async_remote_copy(src_ref=src(chip), dst_ref=dst(slot), send_sem=send_sems.at[nflow * n + f],
                                             recv_sem=recv_sems.at[nflow * n + f], device_id=peers[n], device_id_type=MESH)
                for f, (src, dst) in enumerate(flows)]

    def start():
        for cp in local():
            cp.start()
        for n, (px, py, _) in enumerate(peers):
            for cp in copies(n, 2 * px + py, me):
                cp.start()

    def wait():
        for n, (px, py, pc) in enumerate(peers):
            for cp in copies(n, me_chip, 4 * px + 2 * py + pc):
                cp.wait_recv()
        for n, (px, py, _) in enumerate(peers):
            for cp in copies(n, 2 * px + py, me):
                cp.wait_send()
        for cp in local():
            cp.wait()

    return start, wait


def _exchange_flows(srcs, dsts):
    return [((lambda s, src=src: src.at[s]) if src.shape[0] == 4 else (lambda s, src=src: src),
             lambda d, dst=dst: dst.at[d]) for src, dst in zip(srcs, dsts)]


def _exchange_shapes(arrays):
    return [jax.ShapeDtypeStruct((8,) + (a.shape[1:] if a.shape[0] == 4 else a.shape), a.dtype) for a in arrays]


def _exchange_sems(n):
    return [pltpu.SemaphoreType.DMA((7 * n,)), pltpu.SemaphoreType.DMA((7 * n,)), pltpu.SemaphoreType.DMA((n,))]


def exchange_grads(*arrays):
    n = len(arrays)

    def body(*refs):
        start, wait = _device_exchange(_exchange_flows(refs[:n], refs[n:2 * n]), *refs[2 * n:])
        start()
        wait()

    any_spec = pl.BlockSpec(memory_space=pl.ANY)
    return pl.pallas_call(
        body, name="exchange_grads",
        in_specs=[any_spec] * n, out_specs=[any_spec] * n, out_shape=_exchange_shapes(arrays),
        scratch_shapes=_exchange_sems(n),
    )(*arrays)


def adamw_reduce(parts, w, m, v, name, tr):
    rows, width = w.shape
    per = rows // len(parts) // tr
    c1 = 1.0 - ADAM_B1 ** ADAM_STEP
    c2 = 1.0 - ADAM_B2 ** ADAM_STEP

    def body(*refs):
        p_refs = refs[:len(parts)]
        w_ref, m_ref, v_ref, g_ref, d_ref, nm_ref, nv_ref = refs[len(parts):]
        i = pl.program_id(0)
        p = p_refs[0][...]
        for n in range(1, len(parts)):
            p = jnp.where(i >= n * per, p_refs[n][...], p)
        g = p[0].astype(F32)
        for n in range(1, 8):
            g = g + p[n].astype(F32)
        g_ref[...] = g
        nm = ADAM_B1 * m_ref[...] + (1.0 - ADAM_B1) * g
        nv = ADAM_B2 * v_ref[...] + (1.0 - ADAM_B2) * (g * g)
        nm_ref[...] = nm
        nv_ref[...] = nv
        d_ref[...] = -ADAM_LR * ((nm / c1) / (jnp.sqrt(nv / c2) + ADAM_EPS) + ADAM_WD * w_ref[...])

    spec = pl.BlockSpec((tr, width), lambda i: (i, 0))
    shape = jax.ShapeDtypeStruct((rows, width), F32)
    return pl.pallas_call(
        body, name=name, grid=(rows // tr,),
        in_specs=[pl.BlockSpec((8, tr, width), lambda i, n=n: (0, jnp.clip(i - n * per, 0, per - 1), 0))
                  for n in range(len(parts))] + [spec, spec, spec],
        out_specs=[spec] * 4, out_shape=[shape] * 4,
        compiler_params=_params(("arbitrary",)),
    )(*parts, w, m, v)


SMALL =("norm_g", "b_f", "rel_bias", "w_s", "b_s", "v_gain", "final_g")
WEIGHTS = ("norm_g", "w_in", "b_f", "rel_bias", "w_s", "b_s", "v_gain", "branch_gain", "w_out", "final_g")
PACK_ROW_TILE = 512


def _rows_of(shape):
    return -(-int(np.prod(shape)) // 128)


def _pack(leaves):
    parts = []
    for a in leaves:
        flat = a.reshape(-1).astype(F32)
        parts.append(jnp.pad(flat, (0, _rows_of(a.shape) * 128 - flat.shape[0])))
    flat = jnp.concatenate(parts)
    rows = flat.shape[0] // 128
    total = -(-rows // PACK_ROW_TILE) * PACK_ROW_TILE
    return jnp.pad(flat, (0, (total - rows) * 128)).reshape(total, 128)


def _unpack(slab, shapes):
    out, row = [], 0
    for shp in shapes:
        n = int(np.prod(shp))
        r = _rows_of(shp)
        out.append(slab[row:row + r].reshape(-1)[:n].reshape(shp))
        row += r
    return out


def kernel(x, norm_g, w_in, b_f, rel_bias, w_s, b_s, v_gain, branch_gain, w_out, final_g, loss_target, m_norm_g, m_w_in, m_b_f, m_rel_bias, m_w_s, m_b_s, m_v_gain, m_branch_gain, m_w_out, m_final_g, v_norm_g, v_w_in, v_b_f, v_rel_bias, v_w_s, v_b_s, v_v_gain, v_branch_gain, v_w_out, v_final_g):
    depth = norm_g.shape[0]
    weights = dict(norm_g=norm_g, w_in=w_in, b_f=b_f, rel_bias=rel_bias, w_s=w_s, b_s=b_s, v_gain=v_gain,
                   branch_gain=branch_gain, w_out=w_out, final_g=final_g)
    mom1 = dict(norm_g=m_norm_g, w_in=m_w_in, b_f=m_b_f, rel_bias=m_rel_bias, w_s=m_w_s, b_s=m_b_s,
                v_gain=m_v_gain, branch_gain=m_branch_gain, w_out=m_w_out, final_g=m_final_g)
    mom2 = dict(norm_g=v_norm_g, w_in=v_w_in, b_f=v_b_f, rel_bias=v_rel_bias, w_s=v_w_s, b_s=v_b_s,
                v_gain=v_v_gain, branch_gain=v_branch_gain, w_out=v_w_out, final_g=v_final_g)

    wf = jnp.pad(branch_gain.reshape(-1), (0, 8 * 128 - branch_gain.size)).reshape(8, 128)
    w_in_b, w_out_b = w_in.astype(BF16), w_out.astype(BF16)
    w_in_shards, w_out_shards, gf = gather_weights(w_in_b[:1], w_out_b[:1], wf)
    bg_full = gf.reshape(4, -1)[:, :branch_gain.size].reshape((4,) + branch_gain.shape)
    bg_full = jnp.moveaxis(bg_full, 0, 2).reshape(depth, 4, D_BRANCH)

    layers = [dict(norm_g=norm_g[l], b_f=b_f[l], rel_bias=rel_bias[l], w_s=w_s[l],
                   b_s=b_s[l], v_gain=v_gain[l], branch_gain=bg_full[l]) for l in range(depth)]
    layers[0].update(wp=pack_w_in(w_in_shards)[0], wout=w_out_shards.reshape(D_MODEL, D_MODEL))
    next_shards = [(w_in_b[l + 1], w_out_b[l + 1]) for l in range(depth - 1)]

    loss_part, grad_x, lgrads, dfinal = local_step(x[0], loss_target[0], layers, final_g, next_shards)
    loss = lax.psum(loss_part, ("x", "y", "c"))

    stack = lambda k: jnp.stack([g[k] for g in lgrads])
    d_gain = jnp.moveaxis(stack("branch_gain").reshape(depth, 4, 4, HEAD_DIM), 2, 0).reshape(4, -1)
    d_gain = jnp.pad(d_gain, ((0, 0), (0, 8 * 128 - d_gain.shape[1]))).reshape(4, 8, 128)
    small = dict(norm_g=stack("norm_g"), b_f=stack("b_f"), rel_bias=stack("rel_bias"), w_s=stack("w_s"),
                 b_s=stack("b_s"), v_gain=stack("v_gain"), final_g=dfinal)
    parts_in, parts_gain, parts_small = exchange_grads(lgrads[0]["w_in_shards"], d_gain, _pack([small[k] for k in SMALL]))
    parts = dict(w_in=[parts_in] + [g["w_in_parts"] for g in lgrads[1:]], w_out=[g["w_out_parts"] for g in lgrads])

    outs = {}
    tags = ("grad", "delta", "new_m", "new_v")
    for k in ("w_in", "w_out"):
        rows = depth * weights[k].shape[1]
        flat = lambda a: a.reshape(rows, a.shape[-1])
        res = adamw_reduce(parts[k], flat(weights[k]), flat(mom1[k]), flat(mom2[k]), "adamw_" + k, 256)
        for tag, a in zip(tags, res):
            outs[tag, k] = a.reshape(weights[k].shape)
    gain8 = lambda a: jnp.pad(a.reshape(-1), (0, 8 * 128 - a.size)).reshape(8, 128)
    res = adamw_reduce([parts_gain], gain8(branch_gain), gain8(m_branch_gain), gain8(v_branch_gain), "adamw_gain", 8)
    for tag, a in zip(tags, res):
        outs[tag, "branch_gain"] = a.reshape(-1)[:branch_gain.size].reshape(branch_gain.shape)
    pack_small = lambda d: _pack([d[k] for k in SMALL])
    res = adamw_reduce([parts_small], pack_small(weights), pack_small(mom1), pack_small(mom2), "adamw_small", PACK_ROW_TILE)
    for tag, slab in zip(tags, res):
        for k, a in zip(SMALL, _unpack(slab, [weights[k].shape for k in SMALL])):
            outs[tag, k] = a
    result = [loss, grad_x[None]]
    for tag in ("grad", "delta", "new_m", "new_v"):
        result += [outs[tag, k] for k in WEIGHTS]
    return tuple(result)
```

```python
import jax
import jax.numpy as jnp
import numpy as np
from jax import lax
from jax.experimental import pallas as pl
from jax.experimental.pallas import tpu as pltpu

F32 = jnp.float32
BF16 = jnp.bfloat16
MESH = pl.DeviceIdType.MESH

D_MODEL = 1024
D_BRANCH = 256
N_HEADS = 4
HEAD_DIM = 64
CHUNK = 64
LOOKBACK = 8
MAX_REL = 128
SG_CHUNK = 128
EPS = 1e-6
N_IN = 3844
N_PACK = 3968
F_COL = 3840
N_SHARD = 961
NEG = -1e30

A_TQ = 128
A_BAND = A_TQ + LOOKBACK * CHUNK
REL_LO = MAX_REL - (CHUNK - 1)
REL_HI = 2 * MAX_REL + 1
A_PAD = LOOKBACK * CHUNK
A_QB = 1024
ATT_T = 256
FOX_TQ = 512
FOX_WIDE = 4
FOX_DEAD2 = -136.0
LOG2E = 1.4426950408889634
SB_TQ = 1024
SB_SUB = 128
SB_BACK = 256
SB_BAND = SB_SUB + SB_BACK
SB_DEAD = -110.0
ROW_T = 512
VMEM_LIMIT = 56 * 1024 * 1024

ADAM_LR = 0.001
ADAM_B1 = 0.9
ADAM_B2 = 0.999
ADAM_EPS = 1e-08
ADAM_WD = 0.01
ADAM_STEP = 10

SEC_A_Q, SEC_A_K, SEC_A_V, SEC_A_G = 0, 256, 512, 768
SEC_B_U, SEC_B_V, SEC_B_G = 1024, 1280, 1536
SEC_C_Q, SEC_C_K, SEC_C_V, SEC_C_G = 1792, 2048, 2304, 2560
SEC_D_Q, SEC_D_K, SEC_D_V, SEC_D_G = 2816, 3072, 3328, 3584
QKV_SECS = (SEC_A_Q, SEC_C_Q, SEC_C_K, SEC_C_V, SEC_D_Q, SEC_D_K, SEC_D_V)
GATE_SECS = (SEC_A_G, SEC_B_G, SEC_C_G, SEC_D_G)


def _dot(a, b):
    return jnp.dot(a, b, preferred_element_type=F32)


def _dot_nt(a, b):
    return lax.dot_general(a, b, (((1,), (1,)), ((), ())), preferred_element_type=F32)


def _dot_tn(a, b):
    return lax.dot_general(a, b, (((0,), (0,)), ((), ())), preferred_element_type=F32)


def _split2(x):
    hi = x.astype(BF16)
    lo = (x - hi.astype(F32)).astype(BF16)
    return hi, lo


def _split3(x):
    hi = x.astype(BF16)
    r = x - hi.astype(F32)
    mid = r.astype(BF16)
    lo = (r - mid.astype(F32)).astype(BF16)
    return hi, mid, lo


def _sigmoid(x):
    return 1.0 / (1.0 + jnp.exp(-x))


def _params(sem=None, vmem=VMEM_LIMIT):
    return pltpu.CompilerParams(dimension_semantics=sem, vmem_limit_bytes=vmem)


def _heads_to_lanes(ref):
    return jnp.concatenate([ref[h] for h in range(N_HEADS)], axis=1)


def inproj_fwd(x, g, wp):
    s = x.shape[0]
    tm = A_PAD

    def body(x_ref, g_ref, w_ref, h_ref, qkv_ref, kva_ref, gates_ref, uv_ref, f_ref):
        xv = x_ref[...]
        r = lax.rsqrt(jnp.mean(xv * xv, axis=-1, keepdims=True) + EPS)
        h = (xv * r * g_ref[...]).astype(BF16)
        h_ref[...] = h
        for n, off in enumerate(QKV_SECS):
            p = _dot(h, w_ref[:, off:off + D_BRANCH])
            for hh in range(N_HEADS):
                qkv_ref[n, hh] = p[:, hh * HEAD_DIM:(hh + 1) * HEAD_DIM].astype(BF16)
        for n, off in enumerate((SEC_A_K, SEC_A_V)):
            p = _dot(h, w_ref[:, off:off + D_BRANCH])
            for hh in range(N_HEADS):
                kva_ref[n, hh] = p[:, hh * HEAD_DIM:(hh + 1) * HEAD_DIM].astype(BF16)
        for n, off in enumerate(GATE_SECS):
            gates_ref[:, n * D_BRANCH:(n + 1) * D_BRANCH] = _dot(h, w_ref[:, off:off + D_BRANCH])
        uv_ref[...] = _dot(h, w_ref[:, SEC_B_U:SEC_B_U + 2 * D_BRANCH])
        f_ref[...] = _dot(h, w_ref[:, F_COL:F_COL + 128])

    return pl.pallas_call(
        body, name="inproj_fwd", grid=(s // tm,),
        in_specs=[pl.BlockSpec((tm, D_MODEL), lambda i: (i, 0)),
                  pl.BlockSpec((1, D_MODEL), lambda i: (0, 0)),
                  pl.BlockSpec((D_MODEL, N_PACK), lambda i: (0, 0))],
        out_specs=[pl.BlockSpec((tm, D_MODEL), lambda i: (i, 0)),
                   pl.BlockSpec((len(QKV_SECS), N_HEADS, tm, HEAD_DIM), lambda i: (0, 0, i, 0)),
                   pl.BlockSpec((2, N_HEADS, tm, HEAD_DIM), lambda i: (0, 0, i + 1, 0)),
                   pl.BlockSpec((tm, D_MODEL), lambda i: (i, 0)),
                   pl.BlockSpec((tm, 2 * D_BRANCH), lambda i: (i, 0)),
                   pl.BlockSpec((tm, 128), lambda i: (i, 0))],
        out_shape=[jax.ShapeDtypeStruct((s, D_MODEL), BF16),
                   jax.ShapeDtypeStruct((len(QKV_SECS), N_HEADS, s, HEAD_DIM), BF16),
                   jax.ShapeDtypeStruct((2, N_HEADS, s + tm, HEAD_DIM), BF16),
                   jax.ShapeDtypeStruct((s, D_MODEL), F32),
                   jax.ShapeDtypeStruct((s, 2 * D_BRANCH), F32),
                   jax.ShapeDtypeStruct((s, 128), F32)],
        compiler_params=_params(("arbitrary",)),
    )(x, g, wp)


def inproj_bwd(dqkv, dgates, duv, dfp, wp, x, g, dres):
    s = x.shape[0]
    tm = A_PAD

    def body(*refs):
        dq_refs = refs[:9]
        dgates_ref, duv_ref, dfp_ref, w_ref, x_ref, g_ref, dres_ref, dp_ref, dx_ref, dg_ref = refs[9:]
        i = pl.program_id(0)
        a_q, a_k, a_v, c_q, c_k, c_v, d_q, d_k, d_v = [_heads_to_lanes(r).astype(BF16) for r in dq_refs]
        dgt = dgates_ref[...]
        duv_b = duv_ref[...].astype(BF16)
        dp = jnp.concatenate(
            [a_q, a_k, a_v, dgt[:, 0:256], duv_b, dgt[:, 256:512], c_q, c_k, c_v, dgt[:, 512:768],
             d_q, d_k, d_v, dgt[:, 768:1024], dfp_ref[...].astype(BF16)], axis=1)
        dp_ref[...] = dp
        dh = _dot_nt(dp, w_ref[...])
        xv = x_ref[...]
        r = lax.rsqrt(jnp.mean(xv * xv, axis=-1, keepdims=True) + EPS)
        xn = xv * r
        u = dh * g_ref[...]
        dx_ref[...] = dres_ref[...] + r * (u - xn * jnp.mean(xn * u, axis=-1, keepdims=True))

        @pl.when(i == 0)
        def _():
            dg_ref[...] = jnp.zeros_like(dg_ref)

        dg_ref[...] += jnp.sum(dh * xn, axis=0, keepdims=True)

    head_spec = pl.BlockSpec((N_HEADS, tm, HEAD_DIM), lambda i: (0, i, 0))
    padded_spec = pl.BlockSpec((N_HEADS, tm, HEAD_DIM), lambda i: (0, i + 1, 0))
    return pl.pallas_call(
        body, name="inproj_bwd", grid=(s // tm,),
        in_specs=[head_spec, padded_spec, padded_spec] + [head_spec] * 6 + [
            pl.BlockSpec((tm, D_MODEL), lambda i: (i, 0)),
            pl.BlockSpec((tm, 2 * D_BRANCH), lambda i: (i, 0)),
            pl.BlockSpec((tm, 128), lambda i: (i, 0)),
            pl.BlockSpec((D_MODEL, N_PACK), lambda i: (0, 0)),
            pl.BlockSpec((tm, D_MODEL), lambda i: (i, 0)),
            pl.BlockSpec((1, D_MODEL), lambda i: (0, 0)),
            pl.BlockSpec((tm, D_MODEL), lambda i: (i, 0))],
        out_specs=[pl.BlockSpec((tm, N_PACK), lambda i: (i, 0)),
                   pl.BlockSpec((tm, D_MODEL), lambda i: (i, 0)),
                   pl.BlockSpec((1, D_MODEL), lambda i: (0, 0))],
        out_shape=[jax.ShapeDtypeStruct((s, N_PACK), BF16),
                   jax.ShapeDtypeStruct((s, D_MODEL), F32),
                   jax.ShapeDtypeStruct((1, D_MODEL), F32)],
        compiler_params=_params(("arbitrary",)),
    )(*dqkv, dgates, duv, dfp, wp, x, g, dres)


def inproj_wgrad(h, dp):
    s, m = h.shape
    tm = min(2 * ROW_T, s)
    tmm = 256
    nsteps = s // tm

    def body(a_ref, b_ref, o_ref, acc_ref):
        k = pl.program_id(1)

        @pl.when(k == 0)
        def _():
            acc_ref[...] = jnp.zeros_like(acc_ref)

        acc_ref[...] += _dot_tn(a_ref[...], b_ref[...])

        @pl.when(k == nsteps - 1)
        def _():
            acc = acc_ref[...]
            full = jnp.concatenate([acc[:, :SEC_D_Q], acc[:, F_COL:F_COL + N_HEADS], acc[:, SEC_D_Q:F_COL]], axis=1)
            for n in range(4):
                o_ref[n] = full[:, n * N_SHARD:(n + 1) * N_SHARD].astype(BF16)

    return pl.pallas_call(
        body, name="inproj_wgrad", grid=(m // tmm, nsteps),
        in_specs=[pl.BlockSpec((tm, tmm), lambda j, k: (k, j)),
                  pl.BlockSpec((tm, N_PACK), lambda j, k: (k, 0))],
        out_specs=pl.BlockSpec((4, tmm, N_SHARD), lambda j, k: (0, j, 0)),
        out_shape=jax.ShapeDtypeStruct((4, m, N_SHARD), BF16),
        scratch_shapes=[pltpu.VMEM((tmm, N_PACK), F32)],
        compiler_params=_params(("arbitrary", "arbitrary")),
    )(h, dp)


def _a_specs(s):
    nq = s // A_QB
    per = A_QB // A_PAD
    q_spec = pl.BlockSpec((None, None, A_QB, HEAD_DIM), lambda h, i: (0, h, jnp.minimum(i, nq - 1), 0))
    kv_specs = [pl.BlockSpec((None, None, A_PAD, HEAD_DIM),
                             lambda h, i, n=n, m=m: (n, h, jnp.minimum(per * i + m, per * nq), 0))
                for n in range(2) for m in range(per + 1)]
    t_spec = pl.BlockSpec((None, A_TQ, A_BAND), lambda h, i: (h, 0, 0))
    return nq, q_spec, kv_specs, t_spec


def _a_window(refs, i):
    first = refs[0][...]
    return jnp.concatenate([jnp.where(i > 0, first, jnp.zeros_like(first))] + [r[...] for r in refs[1:]], axis=0)


def _a_scores(q_ref, k, t_ref, i, j):
    rows = slice(j * A_TQ, (j + 1) * A_TQ)
    qs = q_ref[rows, :] * 0.125
    kj = k[j * A_TQ:j * A_TQ + A_BAND, :]
    sc = _dot_nt(qs, kj) + t_ref[...]
    col = lax.broadcasted_iota(jnp.int32, (A_TQ, A_BAND), 1)
    sc = jnp.where(col >= A_PAD - i * A_QB - j * A_TQ, sc, NEG)
    return rows, qs, kj, sc


def mix_a_fwd(qkv, kva, tbias):
    s = qkv.shape[2]
    nq, q_spec, kv_specs, t_spec = _a_specs(s)
    nwin = len(kv_specs) // 2

    def body(*refs):
        q_ref, t_ref, o_ref, lse_ref = refs[0], refs[1 + 2 * nwin], refs[2 + 2 * nwin], refs[3 + 2 * nwin]
        i = pl.program_id(1)
        k = _a_window(refs[1:1 + nwin], i)
        v = _a_window(refs[1 + nwin:1 + 2 * nwin], i)
        for j in range(A_QB // A_TQ):
            rows, _, _, sc = _a_scores(q_ref, k, t_ref, i, j)
            m = jnp.max(sc, axis=-1, keepdims=True)
            p = jnp.exp(sc - m)
            l = jnp.sum(p, axis=-1, keepdims=True)
            o_ref[rows, :] = _dot(p.astype(BF16), v[j * A_TQ:j * A_TQ + A_BAND, :]) / l
            lse_ref[rows, :] = m + jnp.log(l)

    return pl.pallas_call(
        body, name="mix_a_fwd", grid=(N_HEADS, nq),
        in_specs=[q_spec] + kv_specs + [t_spec],
        out_specs=[pl.BlockSpec((None, A_QB, HEAD_DIM), lambda h, i: (h, i, 0)),
                   pl.BlockSpec((None, A_QB, 1), lambda h, i: (h, i, 0))],
        out_shape=[jax.ShapeDtypeStruct((N_HEADS, s, HEAD_DIM), F32),
                   jax.ShapeDtypeStruct((N_HEADS, s, 1), F32)],
        compiler_params=_params(("arbitrary", "arbitrary")),
    )(qkv, *([kva] * (2 * nwin)), tbias)


def mix_a_bwd(qkv, kva, tbias, do, o, lse):
    s = qkv.shape[2]
    nq, q_spec, kv_specs, t_spec = _a_specs(s)
    nwin = len(kv_specs) // 2
    row_spec = lambda w: pl.BlockSpec((None, A_QB, w), lambda h, i: (h, jnp.minimum(i, nq - 1), 0))
    done_spec = pl.BlockSpec((None, A_QB, HEAD_DIM), lambda h, i: (h, i, 0))
    win = A_QB + A_PAD

    def body(*refs):
        q_ref = refs[0]
        t_ref, do_ref, o_ref, lse_ref, dq_ref, dk_ref, dv_ref, dt_ref, dk_win, dv_win = refs[1 + 2 * nwin:]
        i = pl.program_id(1)

        @pl.when(i == 0)
        def _():
            dk_win[...] = jnp.zeros_like(dk_win)
            dv_win[...] = jnp.zeros_like(dv_win)
            dt_ref[...] = jnp.zeros_like(dt_ref)

        @pl.when(i < nq)
        def _():
            k = _a_window(refs[1:1 + nwin], i)
            v = _a_window(refs[1 + nwin:1 + 2 * nwin], i)
            dt = jnp.zeros((A_TQ, A_BAND), F32)
            for j in range(A_QB // A_TQ):
                rows, qs, kj, sc = _a_scores(q_ref, k, t_ref, i, j)
                keys = slice(j * A_TQ, j * A_TQ + A_BAND)
                dob = do_ref[rows, :]
                p = jnp.exp(sc - lse_ref[rows, :])
                delta = jnp.sum(o_ref[rows, :] * dob.astype(F32), axis=-1, keepdims=True)
                ds = p * (_dot_nt(dob, v[keys, :]) - delta)
                dsb = ds.astype(BF16)
                dq_ref[rows, :] = _dot(dsb, kj) * 0.125
                dk_win[keys, :] += _dot_tn(dsb, qs)
                dv_win[keys, :] += _dot_tn(p.astype(BF16), dob)
                dt = dt + ds
            dt_ref[...] += dt

        dk_ref[...] = dk_win[0:A_QB, :]
        dv_ref[...] = dv_win[0:A_QB, :]
        dk_rest = dk_win[A_QB:win, :]
        dv_rest = dv_win[A_QB:win, :]
        dk_win[0:A_PAD, :] = dk_rest
        dv_win[0:A_PAD, :] = dv_rest
        dk_win[A_PAD:win, :] = jnp.zeros((A_QB, HEAD_DIM), F32)
        dv_win[A_PAD:win, :] = jnp.zeros((A_QB, HEAD_DIM), F32)

    return pl.pallas_call(
        body, name="mix_a_bwd", grid=(N_HEADS, nq + 1),
        in_specs=[q_spec] + kv_specs + [t_spec, row_spec(HEAD_DIM), row_spec(HEAD_DIM), row_spec(1)],
        out_specs=[row_spec(HEAD_DIM), done_spec, done_spec, t_spec],
        out_shape=[jax.ShapeDtypeStruct((N_HEADS, s, HEAD_DIM), F32),
                   jax.ShapeDtypeStruct((N_HEADS, s + A_QB, HEAD_DIM), F32),
                   jax.ShapeDtypeStruct((N_HEADS, s + A_QB, HEAD_DIM), F32),
                   jax.ShapeDtypeStruct((N_HEADS, A_TQ, A_BAND), F32)],
        scratch_shapes=[pltpu.VMEM((win, HEAD_DIM), F32), pltpu.VMEM((win, HEAD_DIM), F32)],
        compiler_params=_params(("arbitrary", "arbitrary")),
    )(qkv, *([kva] * (2 * nwin)), tbias, do, o, lse)


def relbias_tile(rel_bias, relmat):
    def body(rb_ref, rel_ref, o_ref):
        rel = rel_ref[...]
        o_ref[...] = jnp.full(o_ref.shape, NEG, F32)

        def step(r, carry):
            hit = rel == r
            for h in range(N_HEADS):
                o_ref[h] = jnp.where(hit, rb_ref[h, r], o_ref[h])
            return carry

        lax.fori_loop(REL_LO, REL_HI, step, 0)

    return pl.pallas_call(
        body, name="relbias_tile",
        in_specs=[pl.BlockSpec(memory_space=pltpu.SMEM), pl.BlockSpec(memory_space=pltpu.VMEM)],
        out_specs=pl.BlockSpec(memory_space=pltpu.VMEM),
        out_shape=jax.ShapeDtypeStruct((N_HEADS, A_TQ, A_BAND), F32),
        compiler_params=_params(),
    )(rel_bias, relmat)


def relbias_grad(dt, relmat):
    def body(dt_ref, rel_ref, o_ref):
        rel = rel_ref[...]
        lane = lax.broadcasted_iota(jnp.int32, (8, 384), 1)
        row = lax.broadcasted_iota(jnp.int32, (8, 384), 0)

        def step(r, acc):
            hit = rel == r
            for h in range(N_HEADS):
                val = jnp.sum(jnp.where(hit, dt_ref[h], 0.0))
                acc = jnp.where((lane == r) & (row == h), val, acc)
            return acc

        o_ref[...] = lax.fori_loop(REL_LO, REL_HI, step, jnp.zeros((8, 384), F32))

    return pl.pallas_call(
        body, name="relbias_grad",
        out_shape=jax.ShapeDtypeStruct((8, 384), F32),
        compiler_params=_params(),
    )(dt, relmat)


def _b_norm(v, gain):
    mu = jnp.mean(v, axis=-1, keepdims=True)
    xc = v - mu
    rstd = lax.rsqrt(jnp.mean(xc * xc, axis=-1, keepdims=True) + EPS)
    xhat = xc * rstd
    return xhat, rstd, xhat * gain


def _tril_mask():
    t = lax.broadcasted_iota(jnp.int32, (SG_CHUNK, SG_CHUNK), 0)
    u = lax.broadcasted_iota(jnp.int32, (SG_CHUNK, SG_CHUNK), 1)
    return u <= t


def mix_b_fwd(uv, gain, w_s, b_col):
    s = uv.shape[0]
    tm = min(ROW_T, s)

    def body(uv_ref, gain_ref, w_ref, b_ref, y_ref):
        tril = _tril_mask()
        ws = [jnp.where(tril, w_ref[g], 0.0).astype(BF16) for g in range(N_HEADS)]
        for c in range(tm // SG_CHUNK):
            rows = slice(c * SG_CHUNK, (c + 1) * SG_CHUNK)
            u = uv_ref[rows, 0:D_BRANCH]
            _, _, vn = _b_norm(uv_ref[rows, D_BRANCH:2 * D_BRANCH], gain_ref[...])
            vnb = vn.astype(BF16)
            outs = []
            for g in range(N_HEADS):
                cols = slice(g * HEAD_DIM, (g + 1) * HEAD_DIM)
                mixed = _dot(ws[g], vnb[:, cols]) + b_ref[g]
                outs.append(u[:, cols] * mixed)
            y_ref[rows, :] = jnp.concatenate(outs, axis=1)

    return pl.pallas_call(
        body, name="mix_b_fwd", grid=(s // tm,),
        in_specs=[pl.BlockSpec((tm, 2 * D_BRANCH), lambda i: (i, 0)),
                  pl.BlockSpec((1, D_BRANCH), lambda i: (0, 0)),
                  pl.BlockSpec((N_HEADS, SG_CHUNK, SG_CHUNK), lambda i: (0, 0, 0)),
                  pl.BlockSpec((N_HEADS, SG_CHUNK, 1), lambda i: (0, 0, 0))],
        out_specs=pl.BlockSpec((tm, D_BRANCH), lambda i: (i, 0)),
        out_shape=jax.ShapeDtypeStruct((s, D_BRANCH), F32),
        compiler_params=_params(("arbitrary",)),
    )(uv, gain, w_s, b_col)


def mix_b_bwd(uv, gain, w_s, b_col, dy):
    s = uv.shape[0]
    tm = min(ROW_T, s)

    def body(uv_ref, gain_ref, w_ref, b_ref, dy_ref, duv_ref, dw_ref, db_ref, dgain_ref):
        i = pl.program_id(0)

        @pl.when(i == 0)
        def _():
            dw_ref[...] = jnp.zeros_like(dw_ref)
            db_ref[...] = jnp.zeros_like(db_ref)
            dgain_ref[...] = jnp.zeros_like(dgain_ref)

        tril = _tril_mask()
        ws = [jnp.where(tril, w_ref[g], 0.0).astype(BF16) for g in range(N_HEADS)]
        gain_v = gain_ref[...]
        for c in range(tm // SG_CHUNK):
            rows = slice(c * SG_CHUNK, (c + 1) * SG_CHUNK)
            u = uv_ref[rows, 0:D_BRANCH]
            xhat, rstd, vn = _b_norm(uv_ref[rows, D_BRANCH:2 * D_BRANCH], gain_v)
            vnb = vn.astype(BF16)
            dyv = dy_ref[rows, :]
            dus, dvns = [], []
            for g in range(N_HEADS):
                cols = slice(g * HEAD_DIM, (g + 1) * HEAD_DIM)
                mixed = _dot(ws[g], vnb[:, cols]) + b_ref[g]
                dus.append(dyv[:, cols] * mixed)
                dmixed = dyv[:, cols] * u[:, cols]
                dmb = dmixed.astype(BF16)
                db_ref[g] += jnp.sum(dmixed, axis=-1, keepdims=True)
                dw_ref[g] += jnp.where(tril, _dot_nt(dmb, vnb[:, cols]), 0.0)
                dvns.append(_dot_tn(ws[g], dmb))
            dvn = jnp.concatenate(dvns, axis=1)
            dgain_ref[...] += jnp.sum(dvn * xhat, axis=0, keepdims=True)
            dxh = dvn * gain_v
            dv = rstd * (dxh - jnp.mean(dxh, axis=-1, keepdims=True)
                         - xhat * jnp.mean(dxh * xhat, axis=-1, keepdims=True))
            duv_ref[rows, :] = jnp.concatenate(dus + [dv], axis=1)

    return pl.pallas_call(
        body, name="mix_b_bwd", grid=(s // tm,),
        in_specs=[pl.BlockSpec((tm, 2 * D_BRANCH), lambda i: (i, 0)),
                  pl.BlockSpec((1, D_BRANCH), lambda i: (0, 0)),
                  pl.BlockSpec((N_HEADS, SG_CHUNK, SG_CHUNK), lambda i: (0, 0, 0)),
                  pl.BlockSpec((N_HEADS, SG_CHUNK, 1), lambda i: (0, 0, 0)),
                  pl.BlockSpec((tm, D_BRANCH), lambda i: (i, 0))],
        out_specs=[pl.BlockSpec((tm, 2 * D_BRANCH), lambda i: (i, 0)),
                   pl.BlockSpec((N_HEADS, SG_CHUNK, SG_CHUNK), lambda i: (0, 0, 0)),
                   pl.BlockSpec((N_HEADS, SG_CHUNK, 1), lambda i: (0, 0, 0)),
                   pl.BlockSpec((1, D_BRANCH), lambda i: (0, 0))],
        out_shape=[jax.ShapeDtypeStruct((s, 2 * D_BRANCH), F32),
                   jax.ShapeDtypeStruct((N_HEADS, SG_CHUNK, SG_CHUNK), F32),
                   jax.ShapeDtypeStruct((N_HEADS, SG_CHUNK, 1), F32),
                   jax.ShapeDtypeStruct((1, D_BRANCH), F32)],
        compiler_params=_params(("arbitrary",)),
    )(uv, gain, w_s, b_col, dy)


def _scan_mats(nrow):
    a = lax.broadcasted_iota(jnp.int32, (128, 128), 0)
    b = lax.broadcasted_iota(jnp.int32, (128, 128), 1)
    r = lax.broadcasted_iota(jnp.int32, (nrow, nrow), 0)
    c = lax.broadcasted_iota(jnp.int32, (nrow, nrow), 1)
    nb = nrow // N_HEADS
    same = (r // nb) == (c // nb)
    return a, b, r, c, same


def _exact_dot(x, m):
    hi, mid, lo = _split3(x)
    return _dot(hi, m) + _dot(mid, m) + _dot(lo, m)


def _exact_dot_left(m, x):
    hi, mid, lo = _split3(x)
    return _dot(m, hi) + _dot(m, mid) + _dot(m, lo)


def fox_gate_fwd(ft, bcol):
    nrow = ft.shape[0]

    def body(f_ref, b_ref, c_ref):
        z = f_ref[...] + b_ref[...]
        ls = jnp.minimum(z, 0.0) - jnp.log(1.0 + jnp.exp(-jnp.abs(z)))
        a, b, r, c, same = _scan_mats(nrow)
        within = _exact_dot(ls, (a <= b).astype(BF16))
        tot = jnp.broadcast_to(within[:, 127:128], within.shape)
        before = _exact_dot_left((same & (c < r)).astype(BF16), tot)
        c_ref[...] = within + before

    return pl.pallas_call(
        body, name="fox_gate_fwd",
        out_shape=jax.ShapeDtypeStruct((nrow, 128), F32),
        compiler_params=_params(),
    )(ft, bcol)


def fox_gate_bwd(ft, bcol, dc):
    nrow = ft.shape[0]

    def body(f_ref, b_ref, dc_ref, df_ref, db_ref):
        a, b, r, c, same = _scan_mats(nrow)
        dcv = dc_ref[...]
        within = _exact_dot(dcv, (a >= b).astype(BF16))
        tot = jnp.broadcast_to(within[:, 0:1], within.shape)
        after = _exact_dot_left((same & (c > r)).astype(BF16), tot)
        dls = within + after
        z = f_ref[...] + b_ref[...]
        dz = dls * _sigmoid(-z)
        df_ref[...] = dz
        rs = jnp.broadcast_to(jnp.sum(dz, axis=-1, keepdims=True), dz.shape)
        hr = lax.broadcasted_iota(jnp.int32, (8, nrow), 0)
        hc = lax.broadcasted_iota(jnp.int32, (8, nrow), 1)
        db_ref[...] = _exact_dot_left((hr == hc // (nrow // N_HEADS)).astype(BF16), rs)

    return pl.pallas_call(
        body, name="fox_gate_bwd",
        out_shape=[jax.ShapeDtypeStruct((nrow, 128), F32), jax.ShapeDtypeStruct((8, 128), F32)],
        compiler_params=_params(),
    )(ft, bcol, dc)


def _att_specs(s, qi, ki, vi):
    q_spec = pl.BlockSpec((None, None, FOX_TQ, HEAD_DIM), lambda h, i: (qi, h, i, 0))
    k_spec = pl.BlockSpec((None, None, s, HEAD_DIM), lambda h, i: (ki, h, 0, 0))
    v_spec = pl.BlockSpec((None, None, s, HEAD_DIM), lambda h, i: (vi, h, 0, 0))
    row_spec = lambda w: pl.BlockSpec((None, FOX_TQ, w), lambda h, i: (h, i, 0))
    gate_spec = pl.BlockSpec((None, s // ATT_T, 1, ATT_T), lambda h, i: (h, 0, 0, 0))
    return q_spec, k_spec, v_spec, row_spec, gate_spec


def _causal(n):
    row = lax.broadcasted_iota(jnp.int32, (n, n), 0)
    col = lax.broadcasted_iota(jnp.int32, (n, n), 1)
    return col <= row


def _gate_row(cr_ref, kb, g):
    if g == 1:
        return cr_ref[kb]
    return jnp.concatenate([cr_ref[kb + n] for n in range(g)], axis=1)


def _fox_walk(i, carry, tile, alive):
    g = FOX_WIDE
    own = FOX_TQ // ATT_T
    nwide = (own * i) // g
    carry = tile(own * i, own, carry, True)
    carry = lax.fori_loop(0, (own * i - nwide * g) // own, lambda n, c: tile(nwide * g, own, c, False), carry)

    def cond(state):
        return jnp.logical_and(state[0] >= 0, state[1] > 0)

    def step(state):
        n = state[0]
        c = tile(n * g, g, state[2:], False)
        return (n - 1, alive(n * g, c)) + tuple(c)

    out = lax.while_loop(cond, step, (nwide - 1, alive(nwide * g, carry)) + tuple(carry))
    return out[2:]


def _fox_reach(qs, k_ref, kmax_ref, cc, i):
    s = k_ref.shape[0]
    rows = 4 * ATT_T

    @pl.when(i == 0)
    def _():
        def chunk(n, mx):
            kc = k_ref[pl.ds(pl.multiple_of(n * rows, rows), rows), :].astype(F32)
            return jnp.maximum(mx, jnp.max(jnp.sum(kc * kc, axis=-1, keepdims=True)))

        kmax_ref[0] = jnp.sqrt(lax.fori_loop(0, s // rows, chunk, jnp.float32(0.0)))

    qf = qs.astype(F32)
    return jnp.sqrt(jnp.sum(qf * qf, axis=-1, keepdims=True)) * kmax_ref[0] + cc


def _gate_col(cr_ref, i):
    row = lax.broadcasted_iota(jnp.int32, (ATT_T, ATT_T), 0)
    col = lax.broadcasted_iota(jnp.int32, (ATT_T, ATT_T), 1)
    own = FOX_TQ // ATT_T
    return jnp.concatenate([jnp.sum(jnp.where(row == col, cr_ref[own * i + n], 0.0), axis=-1, keepdims=True)
                            for n in range(own)], axis=0)


def _fox_scores(qs, k, cc, crow, masked):
    sc = (_dot_nt(qs, k) + (cc - crow)) * LOG2E
    if masked:
        sc = jnp.where(_causal(FOX_TQ), sc, NEG)
    return sc


def fox_fwd(qkv, c_row, ride=()):
    s = qkv.shape[2]
    t = ATT_T
    nq = s // FOX_TQ
    q_spec, k_spec, v_spec, row_spec, gate_spec = _att_specs(s, 1, 2, 3)
    rows = 4 * t
    nride = len(ride)

    def body(q_ref, k_ref, v_ref, cr_ref, *refs):
        ride_in, refs = refs[:nride], refs[nride:]
        o_ref, ref_ref, rl_ref = refs[:3]
        ride_out, refs = refs[3:3 + nride], refs[3 + nride:]
        v1_ref, kmax_ref = refs[:2]
        i = pl.program_id(1)
        if nride:
            h = pl.program_id(0)
            start, wait = _chip_gather([(src, lambda slot, dst=dst: dst.at[slot]) for src, dst in zip(ride_in, ride_out)],
                                       *refs[2:])
            pl.when(jnp.logical_and(h == 0, i == 0))(start)

        @pl.when(i == 0)
        def _():
            def chunk(n, carry):
                r0 = pl.multiple_of(n * rows, rows)
                v1_ref[pl.ds(r0, rows), :] = jnp.concatenate(
                    [v_ref[pl.ds(r0, rows), :], jnp.ones((rows, HEAD_DIM), BF16)], axis=1)
                return carry

            lax.fori_loop(0, s // rows, chunk, 0)

        qs = q_ref[...] * 0.125
        cc = _gate_col(cr_ref, i)
        reach = _fox_reach(qs, k_ref, kmax_ref, cc, i) * LOG2E

        def alive(kb, carry):
            return (jnp.max(reach - cr_ref[kb][:, 0:1] * LOG2E - carry[0]) > FOX_DEAD2).astype(jnp.int32)

        def tile(kb, g, carry, masked):
            m, acc = carry
            k0 = pl.multiple_of(kb * t, t)
            sc = _fox_scores(qs, k_ref[pl.ds(k0, g * t), :], cc, _gate_row(cr_ref, kb, g), masked)
            m_new = jnp.maximum(m, jnp.ceil(jnp.max(sc, axis=-1, keepdims=True)))
            pb = jnp.exp2(sc - m_new).astype(BF16)
            acc = jnp.exp2(m - m_new) * acc + _dot(pb, v1_ref[pl.ds(k0, g * t), :])
            return m_new, acc

        init = (jnp.full((FOX_TQ, 1), NEG, F32), jnp.zeros((FOX_TQ, 2 * HEAD_DIM), F32))
        m, acc = _fox_walk(i, init, tile, alive)
        rl = 1.0 / acc[:, HEAD_DIM:HEAD_DIM + 1]
        o_ref[...] = acc[:, 0:HEAD_DIM] * rl
        ref_ref[...] = m
        rl_ref[...] = rl
        if nride:
            pl.when(jnp.logical_and(h == N_HEADS - 1, i == nq - 1))(wait)

    any_spec = pl.BlockSpec(memory_space=pl.ANY)
    ride_sems = [pltpu.SemaphoreType.DMA((3 * nride,)), pltpu.SemaphoreType.DMA((3 * nride,)),
                 pltpu.SemaphoreType.DMA((nride,))] if nride else []
    return pl.pallas_call(
        body, name="fox_fwd_gather" if nride else "fox_fwd", grid=(N_HEADS, nq),
        in_specs=[q_spec, k_spec, v_spec, gate_spec] + [any_spec] * nride,
        out_specs=[row_spec(HEAD_DIM), row_spec(1), row_spec(1)] + [any_spec] * nride,
        out_shape=[jax.ShapeDtypeStruct((N_HEADS, s, HEAD_DIM), F32),
                   jax.ShapeDtypeStruct((N_HEADS, s, 1), F32),
                   jax.ShapeDtypeStruct((N_HEADS, s, 1), F32)]
        + [jax.ShapeDtypeStruct((4,) + a.shape, a.dtype) for a in ride],
        scratch_shapes=[pltpu.VMEM((s, 2 * HEAD_DIM), BF16), pltpu.SMEM((1,), F32)] + ride_sems,
        compiler_params=_params(("arbitrary", "arbitrary")),
    )(qkv, qkv, qkv, c_row, *ride)


def fox_bwd(qkv, c_row, do, o, ref, rl, ride=()):
    s = qkv.shape[2]
    t = ATT_T
    nq = s // FOX_TQ
    q_spec, k_spec, v_spec, row_spec, gate_spec = _att_specs(s, 1, 2, 3)
    any_spec = pl.BlockSpec(memory_space=pl.ANY)
    nride = len(ride)

    def body(q_ref, k_ref, v_ref, cr_ref, do_ref, o_ref, ref_ref, rl_ref, *refs):
        ride_in, refs = refs[:nride], refs[nride:]
        dq_ref, dk_hbm, dv_hbm, dc_ref = refs[:4]
        ride_out, refs = refs[4:4 + nride], refs[4 + nride:]
        dk_acc, dv_acc, kmax_ref = refs[:3]
        h = pl.program_id(0)
        i = pl.program_id(1)
        if nride:
            start, wait = _device_exchange(_exchange_flows(ride_in, ride_out), *refs[3:])
            pl.when(jnp.logical_and(h == 0, i == 0))(start)

        @pl.when(i == 0)
        def _():
            dk_acc[...] = jnp.zeros_like(dk_acc)
            dv_acc[...] = jnp.zeros_like(dv_acc)
            dc_ref[...] = jnp.zeros_like(dc_ref)

        qs = q_ref[...] * 0.125
        ref = ref_ref[...]
        rl = rl_ref[...]
        dob = (do_ref[...].astype(F32) * rl).astype(BF16)
        delta = jnp.sum(o_ref[...] * dob.astype(F32), axis=-1, keepdims=True)
        cc = _gate_col(cr_ref, i)
        margin = _fox_reach(qs, k_ref, kmax_ref, cc, i) * LOG2E - ref

        def alive(kb, carry):
            return (jnp.max(margin - cr_ref[kb][:, 0:1] * LOG2E) > FOX_DEAD2).astype(jnp.int32)

        def tile(kb, g, carry, masked):
            dq, = carry
            k0 = pl.multiple_of(kb * t, t)
            k = k_ref[pl.ds(k0, g * t), :]
            sc = _fox_scores(qs, k, cc, _gate_row(cr_ref, kb, g), masked)
            wb = jnp.exp2(sc - ref).astype(BF16)
            ds = wb.astype(F32) * (_dot_nt(dob, v_ref[pl.ds(k0, g * t), :]) - delta)
            dsb = ds.astype(BF16)
            dk_acc[pl.ds(k0, g * t), :] += _dot_tn(dsb, qs)
            dv_acc[pl.ds(k0, g * t), :] += _dot_tn(wb, dob)
            dcs = -jnp.sum(ds, axis=0, keepdims=True)
            for n in range(g):
                dc_ref[kb + n] += dcs[:, n * t:(n + 1) * t]
            return (dq + _dot(dsb, k),)

        dq, = _fox_walk(i, (jnp.zeros((FOX_TQ, HEAD_DIM), F32),), tile, alive)
        dq_ref[...] = dq * 0.125

        @pl.when(i == nq - 1)
        def _():
            pltpu.sync_copy(dk_acc, dk_hbm.at[h])
            pltpu.sync_copy(dv_acc, dv_hbm.at[h])

        if nride:
            pl.when(jnp.logical_and(h == N_HEADS - 1, i == nq - 1))(wait)

    return pl.pallas_call(
        body, name="fox_bwd_exchange" if nride else "fox_bwd", grid=(N_HEADS, nq),
        in_specs=[q_spec, k_spec, v_spec,
                  gate_spec,
                  row_spec(HEAD_DIM), row_spec(HEAD_DIM), row_spec(1), row_spec(1)] + [any_spec] * nride,
        out_specs=[row_spec(HEAD_DIM), any_spec, any_spec,
                   gate_spec] + [any_spec] * nride,
        out_shape=[jax.ShapeDtypeStruct((N_HEADS, s, HEAD_DIM), F32),
                   jax.ShapeDtypeStruct((N_HEADS, s, HEAD_DIM), F32),
                   jax.ShapeDtypeStruct((N_HEADS, s, HEAD_DIM), F32),
                   jax.ShapeDtypeStruct((N_HEADS, s // t, 1, t), F32)] + _exchange_shapes(ride),
        scratch_shapes=[pltpu.VMEM((s, HEAD_DIM), F32), pltpu.VMEM((s, HEAD_DIM), F32), pltpu.SMEM((1,), F32)]
        + (_exchange_sems(nride) if nride else []),
        compiler_params=_params(("arbitrary", "arbitrary")),
    )(qkv, qkv, qkv, c_row, do, o, ref, rl, *ride)


def _sb_valid(nrows, ahead):
    row = lax.broadcasted_iota(jnp.int32, (nrows, ATT_T), 0)
    col = lax.broadcasted_iota(jnp.int32, (nrows, ATT_T), 1)
    return col + ahead < row


def _sb_band_valid(nsub, i):
    shape = (nsub * SB_SUB, SB_BAND)
    row = lax.broadcasted_iota(jnp.int32, shape, 0)
    col = lax.broadcasted_iota(jnp.int32, shape, 1)
    first = i * SB_TQ + (row - (row & (SB_SUB - 1)))
    valid = col < (row & (SB_SUB - 1)) + jnp.minimum(first, SB_BACK)
    return valid, first[:, 0:1] > SB_BACK


def _sb_logits(qs, k):
    z = _dot_nt(qs, k)
    sp = jnp.log(1.0 + jnp.exp(-jnp.abs(z)))
    return jnp.minimum(z, 0.0) - sp, -jnp.maximum(z, 0.0) - sp


def _sb_weights(ls, lm, run, valid):
    if valid is not None:
        lm = jnp.where(valid, lm, 0.0)
    n = lm.shape[1]
    row = lax.broadcasted_iota(jnp.int32, (n, n), 0)
    col = lax.broadcasted_iota(jnp.int32, (n, n), 1)
    later = (row > col).astype(BF16)
    hi, lo = _split2(lm)
    between = _dot(hi, later) + _dot(lo, later)
    if run is not None:
        between = run + between
    a = jnp.exp(ls + between)
    if valid is not None:
        a = jnp.where(valid, a, 0.0)
    return lm, a


def _sb_band_start(i, j):
    return pl.multiple_of(jnp.maximum(i * SB_TQ + j * SB_SUB - SB_BACK, 0), SB_SUB)


def _sb_tile(qs, k, run, valid):
    ls, lm = _sb_logits(qs, k)
    lm, a = _sb_weights(ls, lm, run, valid)
    return ls, lm, a


def _sb_band(i, qs_all, k_ref):
    nsub = qs_all.shape[0] // SB_SUB
    valid, open_left = _sb_band_valid(nsub, i)
    starts = [_sb_band_start(i, j) for j in range(nsub)]
    kwins = [k_ref[pl.ds(k0, SB_BAND), :] for k0 in starts]
    parts = [_sb_logits(qs_all[j * SB_SUB:(j + 1) * SB_SUB], kwins[j]) for j in range(nsub)]
    ls = jnp.concatenate([p[0] for p in parts], axis=0)
    lm, a = _sb_weights(ls, jnp.concatenate([p[1] for p in parts], axis=0), None, valid)
    return starts, kwins, ls, lm, a, valid, open_left


def _sb_suffix(g, run_g):
    n = g.shape[1]
    row = lax.broadcasted_iota(jnp.int32, (n, n), 0)
    col = lax.broadcasted_iota(jnp.int32, (n, n), 1)
    from_here = (row >= col).astype(BF16)
    hi, lo = _split2(g)
    out = _dot(hi, from_here) + _dot(lo, from_here)
    return out if run_g is None else run_g + out


def _sb_walk(i, carry, tile):
    def alive_of(c):
        return (jnp.max(c[0]) > SB_DEAD).astype(jnp.int32)

    def cond(state):
        n, alive = state[0], state[1]
        return jnp.logical_and(n < i, alive > 0)

    def step(state):
        n = state[0]
        c = tile(i - 1 - n, state[2:], False)
        return (n + 1, alive_of(c)) + tuple(c)

    out = lax.while_loop(cond, step, (jnp.int32(0), alive_of(carry)) + tuple(carry))
    return out[2:]


def _sb_specs(s):
    tq = SB_TQ
    q_spec = pl.BlockSpec((None, None, tq, HEAD_DIM), lambda h, i: (4, h, i, 0))
    k_spec = pl.BlockSpec((None, None, s, HEAD_DIM), lambda h, i: (5, h, 0, 0))
    v_spec = pl.BlockSpec((None, None, s, HEAD_DIM), lambda h, i: (6, h, 0, 0))
    row_spec = pl.BlockSpec((None, tq, HEAD_DIM), lambda h, i: (h, i, 0))
    band_spec = pl.BlockSpec((None, None, 1, 128), lambda h, i: (h, i, 0, 0))
    return tq, q_spec, k_spec, v_spec, row_spec, band_spec


def _sb_block(b, row0, tile, zero):
    t = ATT_T
    lo, hi, both = slice(row0, row0 + t), slice(row0 + t, row0 + 2 * t), slice(row0, row0 + 2 * t)
    c_hi = tile(2 * b + 1, hi, zero, 0)
    c_lo = tile(2 * b, lo, zero, 0)
    c_hi = tile(2 * b, hi, c_hi, None)
    carry = tuple(jnp.concatenate([x, y], axis=0) for x, y in zip(c_lo, c_hi))
    return _sb_walk(2 * b, carry, lambda kb, c, _: tile(kb, both, c, None))


def sb_fwd(qkv):
    s = qkv.shape[2]
    t = ATT_T
    tq, q_spec, k_spec, v_spec, row_spec, band_spec = _sb_specs(s)

    def body(q_ref, k_ref, v_ref, o_ref, band_ref, done_ref):
        i = pl.program_id(1)
        qs = q_ref[...] * 0.125
        starts, _, _, lm, a, _, open_left = _sb_band(i, qs, k_ref)
        ab = a.astype(BF16)
        for j, k0 in enumerate(starts):
            rows = slice(j * SB_SUB, (j + 1) * SB_SUB)
            o_ref[rows, :] = _dot(ab[rows], v_ref[pl.ds(k0, SB_BAND), :])
        worst = jnp.max(jnp.where(open_left, jnp.sum(lm, axis=-1, keepdims=True), NEG))
        done_ref[0] = (worst <= SB_DEAD).astype(jnp.int32)

        @pl.when(done_ref[0] == 0)
        def _():
            def tile(kb, rows, carry, ahead):
                run, acc = carry
                k0 = pl.multiple_of(kb * t, t)
                valid = None if ahead is None else _sb_valid(t, ahead)
                _, lm, a = _sb_tile(qs[rows], k_ref[pl.ds(k0, t), :], run, valid)
                acc = acc + _dot(a.astype(BF16), v_ref[pl.ds(k0, t), :])
                return run + jnp.sum(lm, axis=-1, keepdims=True), acc

            for n in range(tq // (2 * t)):
                _, acc = _sb_block(i * (tq // (2 * t)) + n, n * 2 * t, tile,
                                   (jnp.zeros((t, 1), F32), jnp.zeros((t, HEAD_DIM), F32)))
                o_ref[n * 2 * t:(n + 1) * 2 * t, :] = acc

        band_ref[...] = jnp.full(band_ref.shape, done_ref[0], jnp.int32).astype(F32)

    return pl.pallas_call(
        body, name="sb_fwd", grid=(N_HEADS, s // tq),
        in_specs=[q_spec, k_spec, v_spec],
        out_specs=[row_spec, band_spec],
        out_shape=[jax.ShapeDtypeStruct((N_HEADS, s, HEAD_DIM), F32),
                   jax.ShapeDtypeStruct((N_HEADS, s // tq, 1, 128), F32)],
        scratch_shapes=[pltpu.SMEM((1,), jnp.int32)],
        compiler_params=_params(("arbitrary", "arbitrary")),
    )(qkv, qkv, qkv)


def sb_bwd(qkv, do, o, band):
    s = qkv.shape[2]
    t = ATT_T
    tq, q_spec, k_spec, v_spec, row_spec, band_spec = _sb_specs(s)
    nq = s // tq
    any_spec = pl.BlockSpec(memory_space=pl.ANY)

    def body(q_ref, k_ref, v_ref, do_ref, o_ref, band_ref, dq_ref, dk_hbm, dv_hbm, dk_acc, dv_acc):
        h = pl.program_id(0)
        i = pl.program_id(1)

        @pl.when(i == 0)
        def _():
            dk_acc[...] = jnp.zeros_like(dk_acc)
            dv_acc[...] = jnp.zeros_like(dv_acc)

        qs_all = q_ref[...] * 0.125
        dob_all = do_ref[...]
        tot_all = jnp.sum(o_ref[...] * dob_all.astype(F32), axis=-1, keepdims=True)
        on_band = jnp.max(band_ref[...]) > 0.5

        def grads(qs, dob, tot, k, v, k0, run, run_g, valid):
            ls, lm, a = _sb_tile(qs, k, run, valid)
            ab = a.astype(BF16)
            g = ab.astype(F32) * _dot_nt(dob, v)
            g_left = tot - _sb_suffix(g, run_g)
            dz = g - jnp.exp(ls) * (g + g_left)
            if valid is not None:
                dz = jnp.where(valid, dz, 0.0)
            dzb = dz.astype(BF16)
            n = k.shape[0]
            dk_acc[pl.ds(k0, n), :] += _dot_tn(dzb, qs)
            dv_acc[pl.ds(k0, n), :] += _dot_tn(ab, dob)
            return dzb, lm, g

        @pl.when(on_band)
        def _():
            starts, kwins, ls, _, a, valid, _ = _sb_band(i, qs_all, k_ref)
            ab = a.astype(BF16)
            subs = [slice(j * SB_SUB, (j + 1) * SB_SUB) for j in range(len(starts))]
            vwins = [v_ref[pl.ds(k0, SB_BAND), :] for k0 in starts]
            g = ab.astype(F32) * jnp.concatenate([_dot_nt(dob_all[r], v) for r, v in zip(subs, vwins)], axis=0)
            dz = jnp.where(valid, g - jnp.exp(ls) * (g + (tot_all - _sb_suffix(g, None))), 0.0)
            dzb = dz.astype(BF16)
            for r, k0, k in zip(subs, starts, kwins):
                dq_ref[r, :] = _dot(dzb[r], k) * 0.125
                dk_acc[pl.ds(k0, SB_BAND), :] += _dot_tn(dzb[r], qs_all[r])
                dv_acc[pl.ds(k0, SB_BAND), :] += _dot_tn(ab[r], dob_all[r])

        @pl.when(jnp.logical_not(on_band))
        def _():
            def tile(kb, rows, carry, ahead):
                run, run_g, dq = carry
                k0 = pl.multiple_of(kb * t, t)
                k = k_ref[pl.ds(k0, t), :]
                valid = None if ahead is None else _sb_valid(t, ahead)
                dzb, lm, g = grads(qs_all[rows], dob_all[rows], tot_all[rows], k, v_ref[pl.ds(k0, t), :], k0,
                                   run, run_g, valid)
                return (run + jnp.sum(lm, axis=-1, keepdims=True),
                        run_g + jnp.sum(g, axis=-1, keepdims=True),
                        dq + _dot(dzb, k))

            zero = jnp.zeros((t, 1), F32)
            for n in range(tq // (2 * t)):
                _, _, dq = _sb_block(i * (tq // (2 * t)) + n, n * 2 * t, tile, (zero, zero, jnp.zeros((t, HEAD_DIM), F32)))
                dq_ref[n * 2 * t:(n + 1) * 2 * t, :] = dq * 0.125

        @pl.when(i == nq - 1)
        def _():
            pltpu.sync_copy(dk_acc, dk_hbm.at[h])
            pltpu.sync_copy(dv_acc, dv_hbm.at[h])

    return pl.pallas_call(
        body, name="sb_bwd", grid=(N_HEADS, nq),
        in_specs=[q_spec, k_spec, v_spec, row_spec, row_spec, band_spec],
        out_specs=[row_spec, any_spec, any_spec],
        out_shape=[jax.ShapeDtypeStruct((N_HEADS, s, HEAD_DIM), F32)] * 3,
        scratch_shapes=[pltpu.VMEM((s, HEAD_DIM), F32), pltpu.VMEM((s, HEAD_DIM), F32)],
        compiler_params=_params(("arbitrary", "arbitrary")),
    )(qkv, qkv, qkv, do, o, band)


def _branch_inputs(refs, br):
    ya_ref, yb_ref, yc_ref, yd_ref = refs
    if br == 1:
        return yb_ref[...]
    return _heads_to_lanes((ya_ref, None, yc_ref, yd_ref)[br])


def outproj_fwd(x, ya, yb, yc, yd, gates, bg, wout):
    s = x.shape[0]
    tm = min(ROW_T, s)

    def body(x_ref, ya_ref, yb_ref, yc_ref, yd_ref, gates_ref, bg_ref, w_ref, out_ref):
        pieces = []
        for br in range(4):
            cols = slice(br * D_BRANCH, (br + 1) * D_BRANCH)
            y = _branch_inputs((ya_ref, yb_ref, yc_ref, yd_ref), br)
            r = lax.rsqrt(jnp.mean(y * y, axis=-1, keepdims=True) + EPS)
            gt = gates_ref[:, cols]
            pieces.append((y * r * bg_ref[:, cols]) * (gt * _sigmoid(gt)))
        merged = jnp.concatenate(pieces, axis=1).astype(BF16)
        out_ref[...] = x_ref[...] + _dot(merged, w_ref[...])

    head_spec = pl.BlockSpec((N_HEADS, tm, HEAD_DIM), lambda i: (0, i, 0))
    return pl.pallas_call(
        body, name="outproj_fwd", grid=(s // tm,),
        in_specs=[pl.BlockSpec((tm, D_MODEL), lambda i: (i, 0)),
                  head_spec, pl.BlockSpec((tm, D_BRANCH), lambda i: (i, 0)), head_spec, head_spec,
                  pl.BlockSpec((tm, D_MODEL), lambda i: (i, 0)),
                  pl.BlockSpec((1, D_MODEL), lambda i: (0, 0)),
                  pl.BlockSpec((D_MODEL, D_MODEL), lambda i: (0, 0))],
        out_specs=pl.BlockSpec((tm, D_MODEL), lambda i: (i, 0)),
        out_shape=jax.ShapeDtypeStruct((s, D_MODEL), F32),
        compiler_params=_params(("arbitrary",)),
    )(x, ya, yb, yc, yd, gates, bg, wout)


def outproj_bwd(dout, ya, yb, yc, yd, gates, bg, wout):
    s = dout.shape[0]
    tm = min(ROW_T, s)

    def body(dout_ref, ya_ref, yb_ref, yc_ref, yd_ref, gates_ref, bg_ref, w_ref,
             dya_ref, dyb_ref, dyc_ref, dyd_ref, dgates_ref, dbg_ref, dw_ref):
        i = pl.program_id(0)

        @pl.when(i == 0)
        def _():
            dbg_ref[...] = jnp.zeros_like(dbg_ref)
            dw_ref[...] = jnp.zeros_like(dw_ref)

        doutb = dout_ref[...].astype(BF16)
        dmerged = _dot_nt(doutb, w_ref[...])
        pieces = []
        for br in range(4):
            cols = slice(br * D_BRANCH, (br + 1) * D_BRANCH)
            y = _branch_inputs((ya_ref, yb_ref, yc_ref, yd_ref), br)
            r = lax.rsqrt(jnp.mean(y * y, axis=-1, keepdims=True) + EPS)
            yn = y * r
            bgv = bg_ref[:, cols]
            gt = gates_ref[:, cols]
            sig = _sigmoid(gt)
            act = gt * sig
            n = yn * bgv
            pieces.append(n * act)
            dm = dmerged[:, cols]
            dn = dm * act
            dgates_ref[:, cols] = (dm * n * (sig * (1.0 + gt * (1.0 - sig)))).astype(BF16)
            dbg_ref[:, cols] += jnp.sum(dn * yn, axis=0, keepdims=True)
            u = dn * bgv
            dy = r * (u - yn * jnp.mean(yn * u, axis=-1, keepdims=True))
            if br == 1:
                dyb_ref[...] = dy
            else:
                dref = (dya_ref, None, dyc_ref, dyd_ref)[br]
                for hh in range(N_HEADS):
                    dref[hh] = dy[:, hh * HEAD_DIM:(hh + 1) * HEAD_DIM].astype(BF16)
        merged = jnp.concatenate(pieces, axis=1).astype(BF16)
        dw_ref[...] += _dot_tn(merged, doutb)

    head_spec = pl.BlockSpec((N_HEADS, tm, HEAD_DIM), lambda i: (0, i, 0))
    head_shape = jax.ShapeDtypeStruct((N_HEADS, s, HEAD_DIM), BF16)
    return pl.pallas_call(
        body, name="outproj_bwd", grid=(s // tm,),
        in_specs=[pl.BlockSpec((tm, D_MODEL), lambda i: (i, 0)),
                  head_spec, pl.BlockSpec((tm, D_BRANCH), lambda i: (i, 0)), head_spec, head_spec,
                  pl.BlockSpec((tm, D_MODEL), lambda i: (i, 0)),
                  pl.BlockSpec((1, D_MODEL), lambda i: (0, 0)),
                  pl.BlockSpec((D_MODEL, D_MODEL), lambda i: (0, 0))],
        out_specs=[head_spec, pl.BlockSpec((tm, D_BRANCH), lambda i: (i, 0)), head_spec, head_spec,
                   pl.BlockSpec((tm, D_MODEL), lambda i: (i, 0)),
                   pl.BlockSpec((1, D_MODEL), lambda i: (0, 0)),
                   pl.BlockSpec((D_MODEL, D_MODEL), lambda i: (0, 0))],
        out_shape=[head_shape, jax.ShapeDtypeStruct((s, D_BRANCH), F32), head_shape, head_shape,
                   jax.ShapeDtypeStruct((s, D_MODEL), BF16),
                   jax.ShapeDtypeStruct((1, D_MODEL), F32),
                   jax.ShapeDtypeStruct((D_MODEL, D_MODEL), F32)],
        compiler_params=_params(("arbitrary",)),
    )(dout, ya, yb, yc, yd, gates, bg, wout)


def final_loss(x, tgt, g):
    s = x.shape[0]
    tm = min(ROW_T, s)

    def body(x_ref, t_ref, g_ref, loss_ref, dx_ref, dg_ref):
        i = pl.program_id(0)

        @pl.when(i == 0)
        def _():
            loss_ref[...] = jnp.zeros_like(loss_ref)
            dg_ref[...] = jnp.zeros_like(dg_ref)

        xv = x_ref[...]
        gv = g_ref[...]
        r = lax.rsqrt(jnp.mean(xv * xv, axis=-1, keepdims=True) + EPS)
        xn = xv * r
        err = xn * gv - t_ref[...]
        loss_ref[...] += jnp.sum(err * err) * (0.5 / D_MODEL)
        dy = err * (1.0 / D_MODEL)
        u = dy * gv
        dx_ref[...] = r * (u - xn * jnp.mean(xn * u, axis=-1, keepdims=True))
        dg_ref[...] += jnp.sum(dy * xn, axis=0, keepdims=True)

    return pl.pallas_call(
        body, name="final_loss", grid=(s // tm,),
        in_specs=[pl.BlockSpec((tm, D_MODEL), lambda i: (i, 0)),
                  pl.BlockSpec((tm, D_MODEL), lambda i: (i, 0)),
                  pl.BlockSpec((1, D_MODEL), lambda i: (0, 0))],
        out_specs=[pl.BlockSpec((1, 128), lambda i: (0, 0)),
                   pl.BlockSpec((tm, D_MODEL), lambda i: (i, 0)),
                   pl.BlockSpec((1, D_MODEL), lambda i: (0, 0))],
        out_shape=[jax.ShapeDtypeStruct((1, 128), F32),
                   jax.ShapeDtypeStruct((s, D_MODEL), F32),
                   jax.ShapeDtypeStruct((1, D_MODEL), F32)],
        compiler_params=_params(("arbitrary",)),
    )(x, tgt, g)


def _rel_index():
    i = np.arange(A_TQ)[:, None]
    j = np.arange(A_BAND)[None, :]
    rel = np.clip(i - j + (A_BAND - A_TQ), -MAX_REL, MAX_REL) + MAX_REL
    dchunk = i // CHUNK + LOOKBACK - j // CHUNK
    valid = (dchunk >= 0) & (dchunk <= LOOKBACK)
    return jnp.asarray(np.where(valid, rel, -1).astype(np.int32))


def _layer_consts(p):
    tbias = relbias_tile(p["rel_bias"], _rel_index())
    return dict(
        norm_g=p["norm_g"].reshape(1, D_MODEL),
        v_gain=p["v_gain"].reshape(1, D_BRANCH),
        b_col=p["b_s"].reshape(N_HEADS, SG_CHUNK, 1),
        bg=p["branch_gain"].reshape(1, D_MODEL),
        tbias=tbias,
    )


def _gate_layout(fp, b_f, s):
    nb = s // 128
    ft = fp[:, :N_HEADS].T.reshape(N_HEADS * nb, 128)
    bcol = jnp.repeat(b_f, nb).reshape(N_HEADS * nb, 1)
    return ft, bcol


def layer_fwd(x, p, ride=()):
    s = x.shape[0]
    c = _layer_consts(p)
    h, qkv, kva, gates, uv, fp = inproj_fwd(x, c["norm_g"], p["wp"])
    ya, lse_a = mix_a_fwd(qkv, kva, c["tbias"])
    yb = mix_b_fwd(uv, c["v_gain"], p["w_s"], c["b_col"])
    ft, bcol = _gate_layout(fp, p["b_f"], s)
    c_row = fox_gate_fwd(ft, bcol).reshape(N_HEADS, s // ATT_T, 1, ATT_T)
    yc, ref_c, rl_c, *rode = fox_fwd(qkv, c_row, ride)
    yd, band_d = sb_fwd(qkv)
    out = outproj_fwd(x, ya, yb, yc, yd, gates, c["bg"], p["wout"])
    saved = dict(consts=c, x=x, h=h, qkv=qkv, gates=gates, uv=uv, kva=kva, ft=ft, bcol=bcol,
                 c_row=c_row, ya=ya, lse_a=lse_a, yb=yb, yc=yc, ref_c=ref_c, rl_c=rl_c, yd=yd, band_d=band_d)
    return out, saved, rode


def layer_bwd(dout, p, sv, exchange=False, upper_w_in=None):
    s = dout.shape[0]
    c = sv["consts"]
    dya, dyb, dyc, dyd, dgates, dbg, dwout = outproj_bwd(
        dout, sv["ya"], sv["yb"], sv["yc"], sv["yd"], sv["gates"], c["bg"], p["wout"])
    dqa, dka, dva, dt = mix_a_bwd(sv["qkv"], sv["kva"], c["tbias"], dya, sv["ya"], sv["lse_a"])
    drel = relbias_grad(dt, _rel_index())[:N_HEADS, :2 * MAX_REL + 1]
    duv, dws, dbs, dvgain = mix_b_bwd(sv["uv"], c["v_gain"], p["w_s"], c["b_col"], dyb)
    ride = [dwout.astype(BF16).reshape(4, D_BRANCH, D_MODEL)] if exchange else []
    if upper_w_in is not None:
        ride.append(upper_w_in)
    dqc, dkc, dvc, dc, *rode = fox_bwd(sv["qkv"], sv["c_row"], dyc, sv["yc"], sv["ref_c"], sv["rl_c"], ride)
    dft, dbf = fox_gate_bwd(sv["ft"], sv["bcol"], dc.reshape(N_HEADS * (s // 128), 128))
    dfp = jnp.pad(dft.reshape(N_HEADS, s).T, ((0, 0), (0, 128 - N_HEADS)))
    dqd, dkd, dvd = sb_bwd(sv["qkv"], dyd, sv["yd"], sv["band_d"])
    dp, dx, dnorm = inproj_bwd((dqa, dka, dva, dqc, dkc, dvc, dqd, dkd, dvd), dgates, duv, dfp,
                               p["wp"], sv["x"], c["norm_g"], dout)
    grads = dict(norm_g=dnorm.reshape(D_MODEL), w_in_shards=inproj_wgrad(sv["h"], dp), b_f=dbf[:N_HEADS, 0], rel_bias=drel,
                 w_s=dws, b_s=dbs.reshape(N_HEADS, SG_CHUNK), v_gain=dvgain.reshape(D_BRANCH),
                 branch_gain=dbg.reshape(4, D_BRANCH), wout=dwout)
    if exchange:
        grads["w_out_parts"] = rode[0]
    return dx, grads, (rode[1] if upper_w_in is not None else None)


def local_step(x, tgt, layers, final_g, next_shards=None):
    layers = list(layers)
    saved = []
    cur = x
    for l, p in enumerate(layers):
        ride = next_shards[l] if next_shards is not None and l + 1 < len(layers) else ()
        cur, sv, rode = layer_fwd(cur, p, ride)
        saved.append(sv)
        if ride:
            layers[l + 1] = dict(layers[l + 1], wp=pack_w_in(rode[0][None])[0], wout=rode[1].reshape(D_MODEL, D_MODEL))
    loss, dcur, dfinal = final_loss(cur, tgt, final_g.reshape(1, D_MODEL))
    grads = [None] * len(layers)
    for l in reversed(range(len(layers))):
        exchange = next_shards is not None
        upper = grads[l + 1]["w_in_shards"] if exchange and l + 1 < len(layers) else None
        dcur, grads[l], got = layer_bwd(dcur, layers[l], saved[l], exchange, upper)
        if upper is not None:
            grads[l + 1]["w_in_parts"] = got
    return loss[0, 0], dcur, grads, dfinal.reshape(D_MODEL)


def _chip_gather(pairs, send_sems, recv_sems, loc_sems):
    x, y, c = lax.axis_index("x"), lax.axis_index("y"), lax.axis_index("c")
    me = 2 * x + y
    chips = [(1 - x, y), (x, 1 - y), (1 - x, 1 - y)]
    npair = len(pairs)

    def local():
        return [pltpu.make_async_copy(src, dst(me), loc_sems.at[n]) for n, (src, dst) in enumerate(pairs)]

    def remote(j, n, slot):
        src, dst = pairs[n]
        return pltpu.make_async_remote_copy(
            src_ref=src, dst_ref=dst(slot), send_sem=send_sems.at[npair * j + n], recv_sem=recv_sems.at[npair * j + n],
            device_id=(chips[j][0], chips[j][1], c), device_id_type=MESH)

    def start():
        for cp in local():
            cp.start()
        for j in range(3):
            for n in range(npair):
                remote(j, n, me).start()

    def wait():
        for j in range(3):
            for n in range(npair):
                remote(j, n, 2 * chips[j][0] + chips[j][1]).wait_recv()
        for j in range(3):
            for n in range(npair):
                remote(j, n, me).wait_send()
        for cp in local():
            cp.wait()

    return start, wait


def gather_weights(w_in, w_out, gains):
    depth = w_in.shape[0]

    def body(in_ref, out_ref, g_ref, oin_ref, oout_ref, og_ref, send_sems, recv_sems, loc_sems):
        pairs = [(in_ref, lambda s: oin_ref.at[:, s]), (out_ref, lambda s: oout_ref.at[:, s]), (g_ref, lambda s: og_ref.at[s])]
        start, wait = _chip_gather(pairs, send_sems, recv_sems, loc_sems)
        start()
        wait()

    any_spec = pl.BlockSpec(memory_space=pl.ANY)
    return pl.pallas_call(
        body, name="gather_weights",
        in_specs=[any_spec] * 3, out_specs=[any_spec] * 3,
        out_shape=[jax.ShapeDtypeStruct((depth, 4) + w_in.shape[1:], w_in.dtype),
                   jax.ShapeDtypeStruct((depth, 4) + w_out.shape[1:], w_out.dtype),
                   jax.ShapeDtypeStruct((4,) + gains.shape, gains.dtype)],
        scratch_shapes=[pltpu.SemaphoreType.DMA((9,)), pltpu.SemaphoreType.DMA((9,)), pltpu.SemaphoreType.DMA((3,))],
    )(w_in, w_out, gains)


def pack_w_in(shards):
    depth = shards.shape[0]
    tr = 256

    def body(s_ref, o_ref):
        full = jnp.concatenate([s_ref[n] for n in range(4)], axis=1)
        o_ref[...] = jnp.concatenate([full[:, :SEC_D_Q], full[:, SEC_D_Q + N_HEADS:], full[:, SEC_D_Q:SEC_D_Q + N_HEADS],
                                      jnp.zeros((tr, N_PACK - N_IN), BF16)], axis=1)

    return pl.pallas_call(
        body, name="pack_w_in", grid=(depth, D_MODEL // tr),
        in_specs=[pl.BlockSpec((None, 4, tr, N_SHARD), lambda l, r: (l, 0, r, 0))],
        out_specs=pl.BlockSpec((None, tr, N_PACK), lambda l, r: (l, r, 0)),
        out_shape=jax.ShapeDtypeStruct((depth, D_MODEL, N_PACK), BF16),
        compiler_params=_params(("arbitrary", "arbitrary")),
    )(shards)


def _device_exchange(flows, send_sems, recv_sems, loc_sems):
    x, y, c = lax.axis_index("x"), lax.axis_index("y"), lax.axis_index("c")
    me_chip = 2 * x + y
    me = 4 * x + 2 * y + c
    peers = [(x, y, 1 - c)]
    for px, py in [(1 - x, y), (x, 1 - y), (1 - x, 1 - y)]:
        peers += [(px, py, c), (px, py, 1 - c)]
    nflow = len(flows)

    def local():
        return [pltpu.make_async_copy(src(me_chip), dst(me), loc_sems.at[f]) for f, (src, dst) in enumerate(flows)]

    def copies(n, chip, slot):
        return [pltpu.make_async_remote_copy(src_ref=src(chip), dst_ref=dst(slot), send_sem=send_sems.at[nflow * n + f],
                                             recv_sem=recv_sems.at[nflow * n + f], device_id=peers[n], device_id_type=MESH)
                for f, (src, dst) in enumerate(flows)]

    def start():
        for cp in local():
            cp.start()
        for n, (px, py, _) in enumerate(peers):
            for cp in copies(n, 2 * px + py, me):
                cp.start()

    def wait():
        for n, (px, py, pc) in enumerate(peers):
            for cp in copies(n, me_chip, 4 * px + 2 * py + pc):
                cp.wait_recv()
        for n, (px, py, _) in enumerate(peers):
            for cp in copies(n, 2 * px + py, me):
                cp.wait_send()
        for cp in local():
            cp.wait()

    return start, wait


def _exchange_flows(srcs, dsts):
    return [((lambda s, src=src: src.at[s]) if src.shape[0] == 4 else (lambda s, src=src: src),
             lambda d, dst=dst: dst.at[d]) for src, dst in zip(srcs, dsts)]


def _exchange_shapes(arrays):
    return [jax.ShapeDtypeStruct((8,) + (a.shape[1:] if a.shape[0] == 4 else a.shape), a.dtype) for a in arrays]


def _exchange_sems(n):
    return [pltpu.SemaphoreType.DMA((7 * n,)), pltpu.SemaphoreType.DMA((7 * n,)), pltpu.SemaphoreType.DMA((n,))]


def exchange_grads(*arrays):
    n = len(arrays)

    def body(*refs):
        start, wait = _device_exchange(_exchange_flows(refs[:n], refs[n:2 * n]), *refs[2 * n:])
        start()
        wait()

    any_spec = pl.BlockSpec(memory_space=pl.ANY)
    return pl.pallas_call(
        body, name="exchange_grads",
        in_specs=[any_spec] * n, out_specs=[any_spec] * n, out_shape=_exchange_shapes(arrays),
        scratch_shapes=_exchange_sems(n),
    )(*arrays)


def adamw_reduce(parts, w, m, v, name, tr):
    rows, width = w.shape
    per = rows // len(parts) // tr
    c1 = 1.0 - ADAM_B1 ** ADAM_STEP
    c2 = 1.0 - ADAM_B2 ** ADAM_STEP

    def body(*refs):
        p_refs = refs[:len(parts)]
        w_ref, m_ref, v_ref, g_ref, d_ref, nm_ref, nv_ref = refs[len(parts):]
        i = pl.program_id(0)
        p = p_refs[0][...]
        for n in range(1, len(parts)):
            p = jnp.where(i >= n * per, p_refs[n][...], p)
        g = p[0].astype(F32)
        for n in range(1, 8):
            g = g + p[n].astype(F32)
        g_ref[...] = g
        nm = ADAM_B1 * m_ref[...] + (1.0 - ADAM_B1) * g
        nv = ADAM_B2 * v_ref[...] + (1.0 - ADAM_B2) * (g * g)
        nm_ref[...] = nm
        nv_ref[...] = nv
        d_ref[...] = -ADAM_LR * ((nm / c1) / (jnp.sqrt(nv / c2) + ADAM_EPS) + ADAM_WD * w_ref[...])

    spec = pl.BlockSpec((tr, width), lambda i: (i, 0))
    shape = jax.ShapeDtypeStruct((rows, width), F32)
    return pl.pallas_call(
        body, name=name, grid=(rows // tr,),
        in_specs=[pl.BlockSpec((8, tr, width), lambda i, n=n: (0, jnp.clip(i - n * per, 0, per - 1), 0))
                  for n in range(len(parts))] + [spec, spec, spec],
        out_specs=[spec] * 4, out_shape=[shape] * 4,
        compiler_params=_params(("arbitrary",)),
    )(*parts, w, m, v)


SMALL =("norm_g", "b_f", "rel_bias", "w_s", "b_s", "v_gain", "final_g")
WEIGHTS = ("norm_g", "w_in", "b_f", "rel_bias", "w_s", "b_s", "v_gain", "branch_gain", "w_out", "final_g")
PACK_ROW_TILE = 512


def _rows_of(shape):
    return -(-int(np.prod(shape)) // 128)


def _pack(leaves):
    parts = []
    for a in leaves:
        flat = a.reshape(-1).astype(F32)
        parts.append(jnp.pad(flat, (0, _rows_of(a.shape) * 128 - flat.shape[0])))
    flat = jnp.concatenate(parts)
    rows = flat.shape[0] // 128
    total = -(-rows // PACK_ROW_TILE) * PACK_ROW_TILE
    return jnp.pad(flat, (0, (total - rows) * 128)).reshape(total, 128)


def _unpack(slab, shapes):
    out, row = [], 0
    for shp in shapes:
        n = int(np.prod(shp))
        r = _rows_of(shp)
        out.append(slab[row:row + r].reshape(-1)[:n].reshape(shp))
        row += r
    return out


def kernel(x, norm_g, w_in, b_f, rel_bias, w_s, b_s, v_gain, branch_gain, w_out, final_g, loss_target, m_norm_g, m_w_in, m_b_f, m_rel_bias, m_w_s, m_b_s, m_v_gain, m_branch_gain, m_w_out, m_final_g, v_norm_g, v_w_in, v_b_f, v_rel_bias, v_w_s, v_b_s, v_v_gain, v_branch_gain, v_w_out, v_final_g):
    depth = norm_g.shape[0]
    weights = dict(norm_g=norm_g, w_in=w_in, b_f=b_f, rel_bias=rel_bias, w_s=w_s, b_s=b_s, v_gain=v_gain,
                   branch_gain=branch_gain, w_out=w_out, final_g=final_g)
    mom1 = dict(norm_g=m_norm_g, w_in=m_w_in, b_f=m_b_f, rel_bias=m_rel_bias, w_s=m_w_s, b_s=m_b_s,
                v_gain=m_v_gain, branch_gain=m_branch_gain, w_out=m_w_out, final_g=m_final_g)
    mom2 = dict(norm_g=v_norm_g, w_in=v_w_in, b_f=v_b_f, rel_bias=v_rel_bias, w_s=v_w_s, b_s=v_b_s,
                v_gain=v_v_gain, branch_gain=v_branch_gain, w_out=v_w_out, final_g=v_final_g)

    wf = jnp.pad(branch_gain.reshape(-1), (0, 8 * 128 - branch_gain.size)).reshape(8, 128)
    w_in_b, w_out_b = w_in.astype(BF16), w_out.astype(BF16)
    w_in_shards, w_out_shards, gf = gather_weights(w_in_b[:1], w_out_b[:1], wf)
    bg_full = gf.reshape(4, -1)[:, :branch_gain.size].reshape((4,) + branch_gain.shape)
    bg_full = jnp.moveaxis(bg_full, 0, 2).reshape(depth, 4, D_BRANCH)

    layers = [dict(norm_g=norm_g[l], b_f=b_f[l], rel_bias=rel_bias[l], w_s=w_s[l],
                   b_s=b_s[l], v_gain=v_gain[l], branch_gain=bg_full[l]) for l in range(depth)]
    layers[0].update(wp=pack_w_in(w_in_shards)[0], wout=w_out_shards.reshape(D_MODEL, D_MODEL))
    next_shards = [(w_in_b[l + 1], w_out_b[l + 1]) for l in range(depth - 1)]

    loss_part, grad_x, lgrads, dfinal = local_step(x[0], loss_target[0], layers, final_g, next_shards)
    loss = lax.psum(loss_part, ("x", "y", "c"))

    stack = lambda k: jnp.stack([g[k] for g in lgrads])
    d_gain = jnp.moveaxis(stack("branch_gain").reshape(depth, 4, 4, HEAD_DIM), 2, 0).reshape(4, -1)
    d_gain = jnp.pad(d_gain, ((0, 0), (0, 8 * 128 - d_gain.shape[1]))).reshape(4, 8, 128)
    small = dict(norm_g=stack("norm_g"), b_f=stack("b_f"), rel_bias=stack("rel_bias"), w_s=stack("w_s"),
                 b_s=stack("b_s"), v_gain=stack("v_gain"), final_g=dfinal)
    parts_in, parts_gain, parts_small = exchange_grads(lgrads[0]["w_in_shards"], d_gain, _pack([small[k] for k in SMALL]))
    parts = dict(w_in=[parts_in] + [g["w_in_parts"] for g in lgrads[1:]], w_out=[g["w_out_parts"] for g in lgrads])

    outs = {}
    tags = ("grad", "delta", "new_m", "new_v")
    for k in ("w_in", "w_out"):
        rows = depth * weights[k].shape[1]
        flat = lambda a: a.reshape(rows, a.shape[-1])
        res = adamw_reduce(parts[k], flat(weights[k]), flat(mom1[k]), flat(mom2[k]), "adamw_" + k, 256)
        for tag, a in zip(tags, res):
            outs[tag, k] = a.reshape(weights[k].shape)
    gain8 = lambda a: jnp.pad(a.reshape(-1), (0, 8 * 128 - a.size)).reshape(8, 128)
    res = adamw_reduce([parts_gain], gain8(branch_gain), gain8(m_branch_gain), gain8(v_branch_gain), "adamw_gain", 8)
    for tag, a in zip(tags, res):
        outs[tag, "branch_gain"] = a.reshape(-1)[:branch_gain.size].reshape(branch_gain.shape)
    pack_small = lambda d: _pack([d[k] for k in SMALL])
    res = adamw_reduce([parts_small], pack_small(weights), pack_small(mom1), pack_small(mom2), "adamw_small", PACK_ROW_TILE)
    for tag, slab in zip(tags, res):
        for k, a in zip(SMALL, _unpack(slab, [weights[k].shape for k in SMALL])):
            outs[tag, k] = a
    result = [loss, grad_x[None]]
    for tag in ("grad", "delta", "new_m", "new_v"):
        result += [outs[tag, k] for k in WEIGHTS]
    return tuple(result)
```

```python
import jax
import jax.numpy as jnp
import numpy as np
from jax import lax
from jax.experimental import pallas as pl
from jax.experimental.pallas import tpu as pltpu

F32 = jnp.float32
BF16 = jnp.bfloat16
MESH = pl.DeviceIdType.MESH

D_MODEL = 1024
D_BRANCH = 256
N_HEADS = 4
HEAD_DIM = 64
CHUNK = 64
LOOKBACK = 8
MAX_REL = 128
SG_CHUNK = 128
EPS = 1e-6
N_IN = 3844
N_PACK = 3968
F_COL = 3840
N_SHARD = 961
NEG = -1e30

A_TQ = 128
A_BAND = A_TQ + LOOKBACK * CHUNK
REL_LO = MAX_REL - (CHUNK - 1)
REL_HI = 2 * MAX_REL + 1
A_PAD = LOOKBACK * CHUNK
A_QB = 1024
ATT_T = 256
FOX_TQ = 512
FOX_WIDE = 4
FOX_DEAD2 = -136.0
LOG2E = 1.4426950408889634
SB_TQ = 1024
SB_SUB = 128
SB_BACK = 256
SB_BAND = SB_SUB + SB_BACK
SB_DEAD = -110.0
ROW_T = 512
VMEM_LIMIT = 56 * 1024 * 1024

ADAM_LR = 0.001
ADAM_B1 = 0.9
ADAM_B2 = 0.999
ADAM_EPS = 1e-08
ADAM_WD = 0.01
ADAM_STEP = 10

SEC_A_Q, SEC_A_K, SEC_A_V, SEC_A_G = 0, 256, 512, 768
SEC_B_U, SEC_B_V, SEC_B_G = 1024, 1280, 1536
SEC_C_Q, SEC_C_K, SEC_C_V, SEC_C_G = 1792, 2048, 2304, 2560
SEC_D_Q, SEC_D_K, SEC_D_V, SEC_D_G = 2816, 3072, 3328, 3584
QKV_SECS = (SEC_A_Q, SEC_C_Q, SEC_C_K, SEC_C_V, SEC_D_Q, SEC_D_K, SEC_D_V)
GATE_SECS = (SEC_A_G, SEC_B_G, SEC_C_G, SEC_D_G)


def _dot(a, b):
    return jnp.dot(a, b, preferred_element_type=F32)


def _dot_nt(a, b):
    return lax.dot_general(a, b, (((1,), (1,)), ((), ())), preferred_element_type=F32)


def _dot_tn(a, b):
    return lax.dot_general(a, b, (((0,), (0,)), ((), ())), preferred_element_type=F32)


def _split2(x):
    hi = x.astype(BF16)
    lo = (x - hi.astype(F32)).astype(BF16)
    return hi, lo


def _split3(x):
    hi = x.astype(BF16)
    r = x - hi.astype(F32)
    mid = r.astype(BF16)
    lo = (r - mid.astype(F32)).astype(BF16)
    return hi, mid, lo


def _sigmoid(x):
    return 1.0 / (1.0 + jnp.exp(-x))


def _params(sem=None, vmem=VMEM_LIMIT):
    return pltpu.CompilerParams(dimension_semantics=sem, vmem_limit_bytes=vmem)


def _heads_to_lanes(ref):
    return jnp.concatenate([ref[h] for h in range(N_HEADS)], axis=1)


def inproj_fwd(x, g, wp):
    s = x.shape[0]
    tm = A_PAD

    def body(x_ref, g_ref, w_ref, h_ref, qkv_ref, kva_ref, gates_ref, uv_ref, f_ref):
        xv = x_ref[...]
        r = lax.rsqrt(jnp.mean(xv * xv, axis=-1, keepdims=True) + EPS)
        h = (xv * r * g_ref[...]).astype(BF16)
        h_ref[...] = h
        for n, off in enumerate(QKV_SECS):
            p = _dot(h, w_ref[:, off:off + D_BRANCH])
            for hh in range(N_HEADS):
                qkv_ref[n, hh] = p[:, hh * HEAD_DIM:(hh + 1) * HEAD_DIM].astype(BF16)
        for n, off in enumerate((SEC_A_K, SEC_A_V)):
            p = _dot(h, w_ref[:, off:off + D_BRANCH])
            for hh in range(N_HEADS):
                kva_ref[n, hh] = p[:, hh * HEAD_DIM:(hh + 1) * HEAD_DIM].astype(BF16)
        for n, off in enumerate(GATE_SECS):
            gates_ref[:, n * D_BRANCH:(n + 1) * D_BRANCH] = _dot(h, w_ref[:, off:off + D_BRANCH])
        uv_ref[...] = _dot(h, w_ref[:, SEC_B_U:SEC_B_U + 2 * D_BRANCH])
        f_ref[...] = _dot(h, w_ref[:, F_COL:F_COL + 128])

    return pl.pallas_call(
        body, name="inproj_fwd", grid=(s // tm,),
        in_specs=[pl.BlockSpec((tm, D_MODEL), lambda i: (i, 0)),
                  pl.BlockSpec((1, D_MODEL), lambda i: (0, 0)),
                  pl.BlockSpec((D_MODEL, N_PACK), lambda i: (0, 0))],
        out_specs=[pl.BlockSpec((tm, D_MODEL), lambda i: (i, 0)),
                   pl.BlockSpec((len(QKV_SECS), N_HEADS, tm, HEAD_DIM), lambda i: (0, 0, i, 0)),
                   pl.BlockSpec((2, N_HEADS, tm, HEAD_DIM), lambda i: (0, 0, i + 1, 0)),
                   pl.BlockSpec((tm, D_MODEL), lambda i: (i, 0)),
                   pl.BlockSpec((tm, 2 * D_BRANCH), lambda i: (i, 0)),
                   pl.BlockSpec((tm, 128), lambda i: (i, 0))],
        out_shape=[jax.ShapeDtypeStruct((s, D_MODEL), BF16),
                   jax.ShapeDtypeStruct((len(QKV_SECS), N_HEADS, s, HEAD_DIM), BF16),
                   jax.ShapeDtypeStruct((2, N_HEADS, s + tm, HEAD_DIM), BF16),
                   jax.ShapeDtypeStruct((s, D_MODEL), F32),
                   jax.ShapeDtypeStruct((s, 2 * D_BRANCH), F32),
                   jax.ShapeDtypeStruct((s, 128), F32)],
        compiler_params=_params(("arbitrary",)),
    )(x, g, wp)


def inproj_bwd(dqkv, dgates, duv, dfp, wp, x, g, dres):
    s = x.shape[0]
    tm = A_PAD

    def body(*refs):
        dq_refs = refs[:9]
        dgates_ref, duv_ref, dfp_ref, w_ref, x_ref, g_ref, dres_ref, dp_ref, dx_ref, dg_ref = refs[9:]
        i = pl.program_id(0)
        a_q, a_k, a_v, c_q, c_k, c_v, d_q, d_k, d_v = [_heads_to_lanes(r).astype(BF16) for r in dq_refs]
        dgt = dgates_ref[...]
        duv_b = duv_ref[...].astype(BF16)
        dp = jnp.concatenate(
            [a_q, a_k, a_v, dgt[:, 0:256], duv_b, dgt[:, 256:512], c_q, c_k, c_v, dgt[:, 512:768],
             d_q, d_k, d_v, dgt[:, 768:1024], dfp_ref[...].astype(BF16)], axis=1)
        dp_ref[...] = dp
        dh = _dot_nt(dp, w_ref[...])
        xv = x_ref[...]
        r = lax.rsqrt(jnp.mean(xv * xv, axis=-1, keepdims=True) + EPS)
        xn = xv * r
        u = dh * g_ref[...]
        dx_ref[...] = dres_ref[...] + r * (u - xn * jnp.mean(xn * u, axis=-1, keepdims=True))

        @pl.when(i == 0)
        def _():
            dg_ref[...] = jnp.zeros_like(dg_ref)

        dg_ref[...] += jnp.sum(dh * xn, axis=0, keepdims=True)

    head_spec = pl.BlockSpec((N_HEADS, tm, HEAD_DIM), lambda i: (0, i, 0))
    padded_spec = pl.BlockSpec((N_HEADS, tm, HEAD_DIM), lambda i: (0, i + 1, 0))
    return pl.pallas_call(
        body, name="inproj_bwd", grid=(s // tm,),
        in_specs=[head_spec, padded_spec, padded_spec] + [head_spec] * 6 + [
            pl.BlockSpec((tm, D_MODEL), lambda i: (i, 0)),
            pl.BlockSpec((tm, 2 * D_BRANCH), lambda i: (i, 0)),
            pl.BlockSpec((tm, 128), lambda i: (i, 0)),
            pl.BlockSpec((D_MODEL, N_PACK), lambda i: (0, 0)),
            pl.BlockSpec((tm, D_MODEL), lambda i: (i, 0)),
            pl.BlockSpec((1, D_MODEL), lambda i: (0, 0)),
            pl.BlockSpec((tm, D_MODEL), lambda i: (i, 0))],
        out_specs=[pl.BlockSpec((tm, N_PACK), lambda i: (i, 0)),
                   pl.BlockSpec((tm, D_MODEL), lambda i: (i, 0)),
                   pl.BlockSpec((1, D_MODEL), lambda i: (0, 0))],
        out_shape=[jax.ShapeDtypeStruct((s, N_PACK), BF16),
                   jax.ShapeDtypeStruct((s, D_MODEL), F32),
                   jax.ShapeDtypeStruct((1, D_MODEL), F32)],
        compiler_params=_params(("arbitrary",)),
    )(*dqkv, dgates, duv, dfp, wp, x, g, dres)


def inproj_wgrad(h, dp):
    s, m = h.shape
    tm = min(2 * ROW_T, s)
    tmm = 256
    nsteps = s // tm

    def body(a_ref, b_ref, o_ref, acc_ref):
        k = pl.program_id(1)

        @pl.when(k == 0)
        def _():
            acc_ref[...] = jnp.zeros_like(acc_ref)

        acc_ref[...] += _dot_tn(a_ref[...], b_ref[...])

        @pl.when(k == nsteps - 1)
        def _():
            acc = acc_ref[...]
            full = jnp.concatenate([acc[:, :SEC_D_Q], acc[:, F_COL:F_COL + N_HEADS], acc[:, SEC_D_Q:F_COL]], axis=1)
            for n in range(4):
                o_ref[n] = full[:, n * N_SHARD:(n + 1) * N_SHARD].astype(BF16)

    return pl.pallas_call(
        body, name="inproj_wgrad", grid=(m // tmm, nsteps),
        in_specs=[pl.BlockSpec((tm, tmm), lambda j, k: (k, j)),
                  pl.BlockSpec((tm, N_PACK), lambda j, k: (k, 0))],
        out_specs=pl.BlockSpec((4, tmm, N_SHARD), lambda j, k: (0, j, 0)),
        out_shape=jax.ShapeDtypeStruct((4, m, N_SHARD), BF16),
        scratch_shapes=[pltpu.VMEM((tmm, N_PACK), F32)],
        compiler_params=_params(("arbitrary", "arbitrary")),
    )(h, dp)


def _a_specs(s):
    nq = s // A_QB
    per = A_QB // A_PAD
    q_spec = pl.BlockSpec((None, None, A_QB, HEAD_DIM), lambda h, i: (0, h, jnp.minimum(i, nq - 1), 0))
    kv_specs = [pl.BlockSpec((None, None, A_PAD, HEAD_DIM),
                             lambda h, i, n=n, m=m: (n, h, jnp.minimum(per * i + m, per * nq), 0))
                for n in range(2) for m in range(per + 1)]
    t_spec = pl.BlockSpec((None, A_TQ, A_BAND), lambda h, i: (h, 0, 0))
    return nq, q_spec, kv_specs, t_spec


def _a_window(refs, i):
    first = refs[0][...]
    return jnp.concatenate([jnp.where(i > 0, first, jnp.zeros_like(first))] + [r[...] for r in refs[1:]], axis=0)


def _a_scores(q_ref, k, t_ref, i, j):
    rows = slice(j * A_TQ, (j + 1) * A_TQ)
    qs = q_ref[rows, :] * 0.125
    kj = k[j * A_TQ:j * A_TQ + A_BAND, :]
    sc = _dot_nt(qs, kj) + t_ref[...]
    col = lax.broadcasted_iota(jnp.int32, (A_TQ, A_BAND), 1)
    sc = jnp.where(col >= A_PAD - i * A_QB - j * A_TQ, sc, NEG)
    return rows, qs, kj, sc


def mix_a_fwd(qkv, kva, tbias):
    s = qkv.shape[2]
    nq, q_spec, kv_specs, t_spec = _a_specs(s)
    nwin = len(kv_specs) // 2

    def body(*refs):
        q_ref, t_ref, o_ref, lse_ref = refs[0], refs[1 + 2 * nwin], refs[2 + 2 * nwin], refs[3 + 2 * nwin]
        i = pl.program_id(1)
        k = _a_window(refs[1:1 + nwin], i)
        v = _a_window(refs[1 + nwin:1 + 2 * nwin], i)
        for j in range(A_QB // A_TQ):
            rows, _, _, sc = _a_scores(q_ref, k, t_ref, i, j)
            m = jnp.max(sc, axis=-1, keepdims=True)
            p = jnp.exp(sc - m)
            l = jnp.sum(p, axis=-1, keepdims=True)
            o_ref[rows, :] = _dot(p.astype(BF16), v[j * A_TQ:j * A_TQ + A_BAND, :]) / l
            lse_ref[rows, :] = m + jnp.log(l)

    return pl.pallas_call(
        body, name="mix_a_fwd", grid=(N_HEADS, nq),
        in_specs=[q_spec] + kv_specs + [t_spec],
        out_specs=[pl.BlockSpec((None, A_QB, HEAD_DIM), lambda h, i: (h, i, 0)),
                   pl.BlockSpec((None, A_QB, 1), lambda h, i: (h, i, 0))],
        out_shape=[jax.ShapeDtypeStruct((N_HEADS, s, HEAD_DIM), F32),
                   jax.ShapeDtypeStruct((N_HEADS, s, 1), F32)],
        compiler_params=_params(("arbitrary", "arbitrary")),
    )(qkv, *([kva] * (2 * nwin)), tbias)


def mix_a_bwd(qkv, kva, tbias, do, o, lse):
    s = qkv.shape[2]
    nq, q_spec, kv_specs, t_spec = _a_specs(s)
    nwin = len(kv_specs) // 2
    row_spec = lambda w: pl.BlockSpec((None, A_QB, w), lambda h, i: (h, jnp.minimum(i, nq - 1), 0))
    done_spec = pl.BlockSpec((None, A_QB, HEAD_DIM), lambda h, i: (h, i, 0))
    win = A_QB + A_PAD

    def body(*refs):
        q_ref = refs[0]
        t_ref, do_ref, o_ref, lse_ref, dq_ref, dk_ref, dv_ref, dt_ref, dk_win, dv_win = refs[1 + 2 * nwin:]
        i = pl.program_id(1)

        @pl.when(i == 0)
        def _():
            dk_win[...] = jnp.zeros_like(dk_win)
            dv_win[...] = jnp.zeros_like(dv_win)
            dt_ref[...] = jnp.zeros_like(dt_ref)

        @pl.when(i < nq)
        def _():
            k = _a_window(refs[1:1 + nwin], i)
            v = _a_window(refs[1 + nwin:1 + 2 * nwin], i)
            dt = jnp.zeros((A_TQ, A_BAND), F32)
            for j in range(A_QB // A_TQ):
                rows, qs, kj, sc = _a_scores(q_ref, k, t_ref, i, j)
                keys = slice(j * A_TQ, j * A_TQ + A_BAND)
                dob = do_ref[rows, :]
                p = jnp.exp(sc - lse_ref[rows, :])
                delta = jnp.sum(o_ref[rows, :] * dob.astype(F32), axis=-1, keepdims=True)
                ds = p * (_dot_nt(dob, v[keys, :]) - delta)
                dsb = ds.astype(BF16)
                dq_ref[rows, :] = _dot(dsb, kj) * 0.125
                dk_win[keys, :] += _dot_tn(dsb, qs)
                dv_win[keys, :] += _dot_tn(p.astype(BF16), dob)
                dt = dt + ds
            dt_ref[...] += dt

        dk_ref[...] = dk_win[0:A_QB, :]
        dv_ref[...] = dv_win[0:A_QB, :]
        dk_rest = dk_win[A_QB:win, :]
        dv_rest = dv_win[A_QB:win, :]
        dk_win[0:A_PAD, :] = dk_rest
        dv_win[0:A_PAD, :] = dv_rest
        dk_win[A_PAD:win, :] = jnp.zeros((A_QB, HEAD_DIM), F32)
        dv_win[A_PAD:win, :] = jnp.zeros((A_QB, HEAD_DIM), F32)

    return pl.pallas_call(
        body, name="mix_a_bwd", grid=(N_HEADS, nq + 1),
        in_specs=[q_spec] + kv_specs + [t_spec, row_spec(HEAD_DIM), row_spec(HEAD_DIM), row_spec(1)],
        out_specs=[row_spec(HEAD_DIM), done_spec, done_spec, t_spec],
        out_shape=[jax.ShapeDtypeStruct((N_HEADS, s, HEAD_DIM), F32),
                   jax.ShapeDtypeStruct((N_HEADS, s + A_QB, HEAD_DIM), F32),
                   jax.ShapeDtypeStruct((N_HEADS, s + A_QB, HEAD_DIM), F32),
                   jax.ShapeDtypeStruct((N_HEADS, A_TQ, A_BAND), F32)],
        scratch_shapes=[pltpu.VMEM((win, HEAD_DIM), F32), pltpu.VMEM((win, HEAD_DIM), F32)],
        compiler_params=_params(("arbitrary", "arbitrary")),
    )(qkv, *([kva] * (2 * nwin)), tbias, do, o, lse)


def relbias_tile(rel_bias, relmat):
    def body(rb_ref, rel_ref, o_ref):
        rel = rel_ref[...]
        o_ref[...] = jnp.full(o_ref.shape, NEG, F32)

        def step(r, carry):
            hit = rel == r
            for h in range(N_HEADS):
                o_ref[h] = jnp.where(hit, rb_ref[h, r], o_ref[h])
            return carry

        lax.fori_loop(REL_LO, REL_HI, step, 0)

    return pl.pallas_call(
        body, name="relbias_tile",
        in_specs=[pl.BlockSpec(memory_space=pltpu.SMEM), pl.BlockSpec(memory_space=pltpu.VMEM)],
        out_specs=pl.BlockSpec(memory_space=pltpu.VMEM),
        out_shape=jax.ShapeDtypeStruct((N_HEADS, A_TQ, A_BAND), F32),
        compiler_params=_params(),
    )(rel_bias, relmat)


def relbias_grad(dt, relmat):
    def body(dt_ref, rel_ref, o_ref):
        rel = rel_ref[...]
        lane = lax.broadcasted_iota(jnp.int32, (8, 384), 1)
        row = lax.broadcasted_iota(jnp.int32, (8, 384), 0)

        def step(r, acc):
            hit = rel == r
            for h in range(N_HEADS):
                val = jnp.sum(jnp.where(hit, dt_ref[h], 0.0))
                acc = jnp.where((lane == r) & (row == h), val, acc)
            return acc

        o_ref[...] = lax.fori_loop(REL_LO, REL_HI, step, jnp.zeros((8, 384), F32))

    return pl.pallas_call(
        body, name="relbias_grad",
        out_shape=jax.ShapeDtypeStruct((8, 384), F32),
        compiler_params=_params(),
    )(dt, relmat)


def _b_norm(v, gain):
    mu = jnp.mean(v, axis=-1, keepdims=True)
    xc = v - mu
    rstd = lax.rsqrt(jnp.mean(xc * xc, axis=-1, keepdims=True) + EPS)
    xhat = xc * rstd
    return xhat, rstd, xhat * gain


def _tril_mask():
    t = lax.broadcasted_iota(jnp.int32, (SG_CHUNK, SG_CHUNK), 0)
    u = lax.broadcasted_iota(jnp.int32, (SG_CHUNK, SG_CHUNK), 1)
    return u <= t


def mix_b_fwd(uv, gain, w_s, b_col):
    s = uv.shape[0]
    tm = min(ROW_T, s)

    def body(uv_ref, gain_ref, w_ref, b_ref, y_ref):
        tril = _tril_mask()
        ws = [jnp.where(tril, w_ref[g], 0.0).astype(BF16) for g in range(N_HEADS)]
        for c in range(tm // SG_CHUNK):
            rows = slice(c * SG_CHUNK, (c + 1) * SG_CHUNK)
            u = uv_ref[rows, 0:D_BRANCH]
            _, _, vn = _b_norm(uv_ref[rows, D_BRANCH:2 * D_BRANCH], gain_ref[...])
            vnb = vn.astype(BF16)
            outs = []
            for g in range(N_HEADS):
                cols = slice(g * HEAD_DIM, (g + 1) * HEAD_DIM)
                mixed = _dot(ws[g], vnb[:, cols]) + b_ref[g]
                outs.append(u[:, cols] * mixed)
            y_ref[rows, :] = jnp.concatenate(outs, axis=1)

    return pl.pallas_call(
        body, name="mix_b_fwd", grid=(s // tm,),
        in_specs=[pl.BlockSpec((tm, 2 * D_BRANCH), lambda i: (i, 0)),
                  pl.BlockSpec((1, D_BRANCH), lambda i: (0, 0)),
                  pl.BlockSpec((N_HEADS, SG_CHUNK, SG_CHUNK), lambda i: (0, 0, 0)),
                  pl.BlockSpec((N_HEADS, SG_CHUNK, 1), lambda i: (0, 0, 0))],
        out_specs=pl.BlockSpec((tm, D_BRANCH), lambda i: (i, 0)),
        out_shape=jax.ShapeDtypeStruct((s, D_BRANCH), F32),
        compiler_params=_params(("arbitrary",)),
    )(uv, gain, w_s, b_col)


def mix_b_bwd(uv, gain, w_s, b_col, dy):
    s = uv.shape[0]
    tm = min(ROW_T, s)

    def body(uv_ref, gain_ref, w_ref, b_ref, dy_ref, duv_ref, dw_ref, db_ref, dgain_ref):
        i = pl.program_id(0)

        @pl.when(i == 0)
        def _():
            dw_ref[...] = jnp.zeros_like(dw_ref)
            db_ref[...] = jnp.zeros_like(db_ref)
            dgain_ref[...] = jnp.zeros_like(dgain_ref)

        tril = _tril_mask()
        ws = [jnp.where(tril, w_ref[g], 0.0).astype(BF16) for g in range(N_HEADS)]
        gain_v = gain_ref[...]
        for c in range(tm // SG_CHUNK):
            rows = slice(c * SG_CHUNK, (c + 1) * SG_CHUNK)
            u = uv_ref[rows, 0:D_BRANCH]
            xhat, rstd, vn = _b_norm(uv_ref[rows, D_BRANCH:2 * D_BRANCH], gain_v)
            vnb = vn.astype(BF16)
            dyv = dy_ref[rows, :]
            dus, dvns = [], []
            for g in range(N_HEADS):
                cols = slice(g * HEAD_DIM, (g + 1) * HEAD_DIM)
                mixed = _dot(ws[g], vnb[:, cols]) + b_ref[g]
                dus.append(dyv[:, cols] * mixed)
                dmixed = dyv[:, cols] * u[:, cols]
                dmb = dmixed.astype(BF16)
                db_ref[g] += jnp.sum(dmixed, axis=-1, keepdims=True)
                dw_ref[g] += jnp.where(tril, _dot_nt(dmb, vnb[:, cols]), 0.0)
                dvns.append(_dot_tn(ws[g], dmb))
            dvn = jnp.concatenate(dvns, axis=1)
            dgain_ref[...] += jnp.sum(dvn * xhat, axis=0, keepdims=True)
            dxh = dvn * gain_v
            dv = rstd * (dxh - jnp.mean(dxh, axis=-1, keepdims=True)
                         - xhat * jnp.mean(dxh * xhat, axis=-1, keepdims=True))
            duv_ref[rows, :] = jnp.concatenate(dus + [dv], axis=1)

    return pl.pallas_call(
        body, name="mix_b_bwd", grid=(s // tm,),
        in_specs=[pl.BlockSpec((tm, 2 * D_BRANCH), lambda i: (i, 0)),
                  pl.BlockSpec((1, D_BRANCH), lambda i: (0, 0)),
                  pl.BlockSpec((N_HEADS, SG_CHUNK, SG_CHUNK), lambda i: (0, 0, 0)),
                  pl.BlockSpec((N_HEADS, SG_CHUNK, 1), lambda i: (0, 0, 0)),
                  pl.BlockSpec((tm, D_BRANCH), lambda i: (i, 0))],
        out_specs=[pl.BlockSpec((tm, 2 * D_BRANCH), lambda i: (i, 0)),
                   pl.BlockSpec((N_HEADS, SG_CHUNK, SG_CHUNK), lambda i: (0, 0, 0)),
                   pl.BlockSpec((N_HEADS, SG_CHUNK, 1), lambda i: (0, 0, 0)),
                   pl.BlockSpec((1, D_BRANCH), lambda i: (0, 0))],
        out_shape=[jax.ShapeDtypeStruct((s, 2 * D_BRANCH), F32),
                   jax.ShapeDtypeStruct((N_HEADS, SG_CHUNK, SG_CHUNK), F32),
                   jax.ShapeDtypeStruct((N_HEADS, SG_CHUNK, 1), F32),
                   jax.ShapeDtypeStruct((1, D_BRANCH), F32)],
        compiler_params=_params(("arbitrary",)),
    )(uv, gain, w_s, b_col, dy)


def _scan_mats(nrow):
    a = lax.broadcasted_iota(jnp.int32, (128, 128), 0)
    b = lax.broadcasted_iota(jnp.int32, (128, 128), 1)
    r = lax.broadcasted_iota(jnp.int32, (nrow, nrow), 0)
    c = lax.broadcasted_iota(jnp.int32, (nrow, nrow), 1)
    nb = nrow // N_HEADS
    same = (r // nb) == (c // nb)
    return a, b, r, c, same


def _exact_dot(x, m):
    hi, mid, lo = _split3(x)
    return _dot(hi, m) + _dot(mid, m) + _dot(lo, m)


def _exact_dot_left(m, x):
    hi, mid, lo = _split3(x)
    return _dot(m, hi) + _dot(m, mid) + _dot(m, lo)


def fox_gate_fwd(ft, bcol):
    nrow = ft.shape[0]

    def body(f_ref, b_ref, c_ref):
        z = f_ref[...] + b_ref[...]
        ls = jnp.minimum(z, 0.0) - jnp.log(1.0 + jnp.exp(-jnp.abs(z)))
        a, b, r, c, same = _scan_mats(nrow)
        within = _exact_dot(ls, (a <= b).astype(BF16))
        tot = jnp.broadcast_to(within[:, 127:128], within.shape)
        before = _exact_dot_left((same & (c < r)).astype(BF16), tot)
        c_ref[...] = within + before

    return pl.pallas_call(
        body, name="fox_gate_fwd",
        out_shape=jax.ShapeDtypeStruct((nrow, 128), F32),
        compiler_params=_params(),
    )(ft, bcol)


def fox_gate_bwd(ft, bcol, dc):
    nrow = ft.shape[0]

    def body(f_ref, b_ref, dc_ref, df_ref, db_ref):
        a, b, r, c, same = _scan_mats(nrow)
        dcv = dc_ref[...]
        within = _exact_dot(dcv, (a >= b).astype(BF16))
        tot = jnp.broadcast_to(within[:, 0:1], within.shape)
        after = _exact_dot_left((same & (c > r)).astype(BF16), tot)
        dls = within + after
        z = f_ref[...] + b_ref[...]
        dz = dls * _sigmoid(-z)
        df_ref[...] = dz
        rs = jnp.broadcast_to(jnp.sum(dz, axis=-1, keepdims=True), dz.shape)
        hr = lax.broadcasted_iota(jnp.int32, (8, nrow), 0)
        hc = lax.broadcasted_iota(jnp.int32, (8, nrow), 1)
        db_ref[...] = _exact_dot_left((hr == hc // (nrow // N_HEADS)).astype(BF16), rs)

    return pl.pallas_call(
        body, name="fox_gate_bwd",
        out_shape=[jax.ShapeDtypeStruct((nrow, 128), F32), jax.ShapeDtypeStruct((8, 128), F32)],
        compiler_params=_params(),
    )(ft, bcol, dc)


def _att_specs(s, qi, ki, vi):
    q_spec = pl.BlockSpec((None, None, FOX_TQ, HEAD_DIM), lambda h, i: (qi, h, i, 0))
    k_spec = pl.BlockSpec((None, None, s, HEAD_DIM), lambda h, i: (ki, h, 0, 0))
    v_spec = pl.BlockSpec((None, None, s, HEAD_DIM), lambda h, i: (vi, h, 0, 0))
    row_spec = lambda w: pl.BlockSpec((None, FOX_TQ, w), lambda h, i: (h, i, 0))
    gate_spec = pl.BlockSpec((None, s // ATT_T, 1, ATT_T), lambda h, i: (h, 0, 0, 0))
    return q_spec, k_spec, v_spec, row_spec, gate_spec


def _causal(n):
    row = lax.broadcasted_iota(jnp.int32, (n, n), 0)
    col = lax.broadcasted_iota(jnp.int32, (n, n), 1)
    return col <= row


def _gate_row(cr_ref, kb, g):
    if g == 1:
        return cr_ref[kb]
    return jnp.concatenate([cr_ref[kb + n] for n in range(g)], axis=1)


def _fox_walk(i, carry, tile, alive):
    g = FOX_WIDE
    own = FOX_TQ // ATT_T
    nwide = (own * i) // g
    carry = tile(own * i, own, carry, True)
    carry = lax.fori_loop(0, (own * i - nwide * g) // own, lambda n, c: tile(nwide * g, own, c, False), carry)

    def cond(state):
        return jnp.logical_and(state[0] >= 0, state[1] > 0)

    def step(state):
        n = state[0]
        kb = n * g
        whole = alive(kb + g // 2, state[2:])
        c = lax.cond(whole > 0,
                     lambda c: tuple(tile(kb, g, c, False)),
                     lambda c: tuple(tile(kb + g // 2, g // 2, c, False)), tuple(state[2:]))
        return (n - 1, jnp.where(whole > 0, alive(kb, c), 0)) + tuple(c)

    out = lax.while_loop(cond, step, (nwide - 1, alive(nwide * g, carry)) + tuple(carry))
    return out[2:]


def _fox_reach(qs, k_ref, kmax_ref, cc, i):
    s = k_ref.shape[0]
    rows = 4 * ATT_T

    @pl.when(i == 0)
    def _():
        def chunk(n, mx):
            kc = k_ref[pl.ds(pl.multiple_of(n * rows, rows), rows), :].astype(F32)
            return jnp.maximum(mx, jnp.max(jnp.sum(kc * kc, axis=-1, keepdims=True)))

        kmax_ref[0] = jnp.sqrt(lax.fori_loop(0, s // rows, chunk, jnp.float32(0.0)))

    qf = qs.astype(F32)
    return jnp.sqrt(jnp.sum(qf * qf, axis=-1, keepdims=True)) * kmax_ref[0] + cc


def _gate_col(cr_ref, i):
    row = lax.broadcasted_iota(jnp.int32, (ATT_T, ATT_T), 0)
    col = lax.broadcasted_iota(jnp.int32, (ATT_T, ATT_T), 1)
    own = FOX_TQ // ATT_T
    return jnp.concatenate([jnp.sum(jnp.where(row == col, cr_ref[own * i + n], 0.0), axis=-1, keepdims=True)
                            for n in range(own)], axis=0)


def _fox_scores(qs, k, cc, crow, masked):
    sc = (_dot_nt(qs, k) + (cc - crow)) * LOG2E
    if masked:
        sc = jnp.where(_causal(FOX_TQ), sc, NEG)
    return sc


def fox_fwd(qkv, c_row, ride=()):
    s = qkv.shape[2]
    t = ATT_T
    nq = s // FOX_TQ
    q_spec, k_spec, v_spec, row_spec, gate_spec = _att_specs(s, 1, 2, 3)
    rows = 4 * t
    nride = len(ride)

    def body(q_ref, k_ref, v_ref, cr_ref, *refs):
        ride_in, refs = refs[:nride], refs[nride:]
        o_ref, ref_ref, rl_ref = refs[:3]
        ride_out, refs = refs[3:3 + nride], refs[3 + nride:]
        v1_ref, kmax_ref = refs[:2]
        i = pl.program_id(1)
        if nride:
            h = pl.program_id(0)
            start, wait = _chip_gather([(src, lambda slot, dst=dst: dst.at[slot]) for src, dst in zip(ride_in, ride_out)],
                                       *refs[2:])
            pl.when(jnp.logical_and(h == 0, i == 0))(start)

        @pl.when(i == 0)
        def _():
            def chunk(n, carry):
                r0 = pl.multiple_of(n * rows, rows)
                v1_ref[pl.ds(r0, rows), :] = jnp.concatenate(
                    [v_ref[pl.ds(r0, rows), :], jnp.ones((rows, HEAD_DIM), BF16)], axis=1)
                return carry

            lax.fori_loop(0, s // rows, chunk, 0)

        qs = q_ref[...] * 0.125
        cc = _gate_col(cr_ref, i)
        reach = _fox_reach(qs, k_ref, kmax_ref, cc, i) * LOG2E

        def alive(kb, carry):
            return (jnp.max(reach - cr_ref[kb][:, 0:1] * LOG2E - carry[0]) > FOX_DEAD2).astype(jnp.int32)

        def tile(kb, g, carry, masked):
            m, acc = carry
            k0 = pl.multiple_of(kb * t, t)
            sc = _fox_scores(qs, k_ref[pl.ds(k0, g * t), :], cc, _gate_row(cr_ref, kb, g), masked)
            m_new = jnp.maximum(m, jnp.ceil(jnp.max(sc, axis=-1, keepdims=True)))
            pb = jnp.exp2(sc - m_new).astype(BF16)
            acc = jnp.exp2(m - m_new) * acc + _dot(pb, v1_ref[pl.ds(k0, g * t), :])
            return m_new, acc

        init = (jnp.full((FOX_TQ, 1), NEG, F32), jnp.zeros((FOX_TQ, 2 * HEAD_DIM), F32))
        m, acc = _fox_walk(i, init, tile, alive)
        rl = 1.0 / acc[:, HEAD_DIM:HEAD_DIM + 1]
        o_ref[...] = acc[:, 0:HEAD_DIM] * rl
        ref_ref[...] = m
        rl_ref[...] = rl
        if nride:
            pl.when(jnp.logical_and(h == N_HEADS - 1, i == nq - 1))(wait)

    any_spec = pl.BlockSpec(memory_space=pl.ANY)
    ride_sems = [pltpu.SemaphoreType.DMA((3 * nride,)), pltpu.SemaphoreType.DMA((3 * nride,)),
                 pltpu.SemaphoreType.DMA((nride,))] if nride else []
    return pl.pallas_call(
        body, name="fox_fwd_gather" if nride else "fox_fwd", grid=(N_HEADS, nq),
        in_specs=[q_spec, k_spec, v_spec, gate_spec] + [any_spec] * nride,
        out_specs=[row_spec(HEAD_DIM), row_spec(1), row_spec(1)] + [any_spec] * nride,
        out_shape=[jax.ShapeDtypeStruct((N_HEADS, s, HEAD_DIM), F32),
                   jax.ShapeDtypeStruct((N_HEADS, s, 1), F32),
                   jax.ShapeDtypeStruct((N_HEADS, s, 1), F32)]
        + [jax.ShapeDtypeStruct((4,) + a.shape, a.dtype) for a in ride],
        scratch_shapes=[pltpu.VMEM((s, 2 * HEAD_DIM), BF16), pltpu.SMEM((1,), F32)] + ride_sems,
        compiler_params=_params(("arbitrary", "arbitrary")),
    )(qkv, qkv, qkv, c_row, *ride)


def fox_bwd(qkv, c_row, do, o, ref, rl, ride=()):
    s = qkv.shape[2]
    t = ATT_T
    nq = s // FOX_TQ
    q_spec, k_spec, v_spec, row_spec, gate_spec = _att_specs(s, 1, 2, 3)
    any_spec = pl.BlockSpec(memory_space=pl.ANY)
    nride = len(ride)

    def body(q_ref, k_ref, v_ref, cr_ref, do_ref, o_ref, ref_ref, rl_ref, *refs):
        ride_in, refs = refs[:nride], refs[nride:]
        dq_ref, dk_hbm, dv_hbm, dc_ref = refs[:4]
        ride_out, refs = refs[4:4 + nride], refs[4 + nride:]
        dk_acc, dv_acc, kmax_ref = refs[:3]
        h = pl.program_id(0)
        i = pl.program_id(1)
        if nride:
            start, wait = _device_exchange(_exchange_flows(ride_in, ride_out), *refs[3:])
            pl.when(jnp.logical_and(h == 0, i == 0))(start)

        @pl.when(i == 0)
        def _():
            dk_acc[...] = jnp.zeros_like(dk_acc)
            dv_acc[...] = jnp.zeros_like(dv_acc)
            dc_ref[...] = jnp.zeros_like(dc_ref)

        qs = q_ref[...] * 0.125
        ref = ref_ref[...]
        rl = rl_ref[...]
        dob = (do_ref[...].astype(F32) * rl).astype(BF16)
        delta = jnp.sum(o_ref[...] * dob.astype(F32), axis=-1, keepdims=True)
        cc = _gate_col(cr_ref, i)
        margin = _fox_reach(qs, k_ref, kmax_ref, cc, i) * LOG2E - ref

        def alive(kb, carry):
            return (jnp.max(margin - cr_ref[kb][:, 0:1] * LOG2E) > FOX_DEAD2).astype(jnp.int32)

        def tile(kb, g, carry, masked):
            dq, = carry
            k0 = pl.multiple_of(kb * t, t)
            k = k_ref[pl.ds(k0, g * t), :]
            sc = _fox_scores(qs, k, cc, _gate_row(cr_ref, kb, g), masked)
            wb = jnp.exp2(sc - ref).astype(BF16)
            ds = wb.astype(F32) * (_dot_nt(dob, v_ref[pl.ds(k0, g * t), :]) - delta)
            dsb = ds.astype(BF16)
            dk_acc[pl.ds(k0, g * t), :] += _dot_tn(dsb, qs)
            dv_acc[pl.ds(k0, g * t), :] += _dot_tn(wb, dob)
            dcs = -jnp.sum(ds, axis=0, keepdims=True)
            for n in range(g):
                dc_ref[kb + n] += dcs[:, n * t:(n + 1) * t]
            return (dq + _dot(dsb, k),)

        dq, = _fox_walk(i, (jnp.zeros((FOX_TQ, HEAD_DIM), F32),), tile, alive)
        dq_ref[...] = dq * 0.125

        @pl.when(i == nq - 1)
        def _():
            pltpu.sync_copy(dk_acc, dk_hbm.at[h])
            pltpu.sync_copy(dv_acc, dv_hbm.at[h])

        if nride:
            pl.when(jnp.logical_and(h == N_HEADS - 1, i == nq - 1))(wait)

    return pl.pallas_call(
        body, name="fox_bwd_exchange" if nride else "fox_bwd", grid=(N_HEADS, nq),
        in_specs=[q_spec, k_spec, v_spec,
                  gate_spec,
                  row_spec(HEAD_DIM), row_spec(HEAD_DIM), row_spec(1), row_spec(1)] + [any_spec] * nride,
        out_specs=[row_spec(HEAD_DIM), any_spec, any_spec,
                   gate_spec] + [any_spec] * nride,
        out_shape=[jax.ShapeDtypeStruct((N_HEADS, s, HEAD_DIM), F32),
                   jax.ShapeDtypeStruct((N_HEADS, s, HEAD_DIM), F32),
                   jax.ShapeDtypeStruct((N_HEADS, s, HEAD_DIM), F32),
                   jax.ShapeDtypeStruct((N_HEADS, s // t, 1, t), F32)] + _exchange_shapes(ride),
        scratch_shapes=[pltpu.VMEM((s, HEAD_DIM), F32), pltpu.VMEM((s, HEAD_DIM), F32), pltpu.SMEM((1,), F32)]
        + (_exchange_sems(nride) if nride else []),
        compiler_params=_params(("arbitrary", "arbitrary")),
    )(qkv, qkv, qkv, c_row, do, o, ref, rl, *ride)


def _sb_valid(nrows, ahead):
    row = lax.broadcasted_iota(jnp.int32, (nrows, ATT_T), 0)
    col = lax.broadcasted_iota(jnp.int32, (nrows, ATT_T), 1)
    return col + ahead < row


def _sb_band_valid(nsub, i):
    shape = (nsub * SB_SUB, SB_BAND)
    row = lax.broadcasted_iota(jnp.int32, shape, 0)
    col = lax.broadcasted_iota(jnp.int32, shape, 1)
    first = i * SB_TQ + (row - (row & (SB_SUB - 1)))
    valid = col < (row & (SB_SUB - 1)) + jnp.minimum(first, SB_BACK)
    return valid, first[:, 0:1] > SB_BACK


def _sb_logits(qs, k):
    z = _dot_nt(qs, k)
    sp = jnp.log(1.0 + jnp.exp(-jnp.abs(z)))
    return jnp.minimum(z, 0.0) - sp, -jnp.maximum(z, 0.0) - sp


def _sb_weights(ls, lm, run, valid):
    if valid is not None:
        lm = jnp.where(valid, lm, 0.0)
    n = lm.shape[1]
    row = lax.broadcasted_iota(jnp.int32, (n, n), 0)
    col = lax.broadcasted_iota(jnp.int32, (n, n), 1)
    later = (row > col).astype(BF16)
    hi, lo = _split2(lm)
    between = _dot(hi, later) + _dot(lo, later)
    if run is not None:
        between = run + between
    a = jnp.exp(ls + between)
    if valid is not None:
        a = jnp.where(valid, a, 0.0)
    return lm, a


def _sb_band_start(i, j):
    return pl.multiple_of(jnp.maximum(i * SB_TQ + j * SB_SUB - SB_BACK, 0), SB_SUB)


def _sb_tile(qs, k, run, valid):
    ls, lm = _sb_logits(qs, k)
    lm, a = _sb_weights(ls, lm, run, valid)
    return ls, lm, a


def _sb_band(i, qs_all, k_ref):
    nsub = qs_all.shape[0] // SB_SUB
    valid, open_left = _sb_band_valid(nsub, i)
    starts = [_sb_band_start(i, j) for j in range(nsub)]
    kwins = [k_ref[pl.ds(k0, SB_BAND), :] for k0 in starts]
    parts = [_sb_logits(qs_all[j * SB_SUB:(j + 1) * SB_SUB], kwins[j]) for j in range(nsub)]
    ls = jnp.concatenate([p[0] for p in parts], axis=0)
    lm, a = _sb_weights(ls, jnp.concatenate([p[1] for p in parts], axis=0), None, valid)
    return starts, kwins, ls, lm, a, valid, open_left


def _sb_suffix(g, run_g):
    n = g.shape[1]
    row = lax.broadcasted_iota(jnp.int32, (n, n), 0)
    col = lax.broadcasted_iota(jnp.int32, (n, n), 1)
    from_here = (row >= col).astype(BF16)
    hi, lo = _split2(g)
    out = _dot(hi, from_here) + _dot(lo, from_here)
    return out if run_g is None else run_g + out


def _sb_walk(i, carry, tile):
    def alive_of(c):
        return (jnp.max(c[0]) > SB_DEAD).astype(jnp.int32)

    def cond(state):
        n, alive = state[0], state[1]
        return jnp.logical_and(n < i, alive > 0)

    def step(state):
        n = state[0]
        c = tile(i - 1 - n, state[2:], False)
        return (n + 1, alive_of(c)) + tuple(c)

    out = lax.while_loop(cond, step, (jnp.int32(0), alive_of(carry)) + tuple(carry))
    return out[2:]


def _sb_specs(s):
    tq = SB_TQ
    q_spec = pl.BlockSpec((None, None, tq, HEAD_DIM), lambda h, i: (4, h, i, 0))
    k_spec = pl.BlockSpec((None, None, s, HEAD_DIM), lambda h, i: (5, h, 0, 0))
    v_spec = pl.BlockSpec((None, None, s, HEAD_DIM), lambda h, i: (6, h, 0, 0))
    row_spec = pl.BlockSpec((None, tq, HEAD_DIM), lambda h, i: (h, i, 0))
    band_spec = pl.BlockSpec((None, None, 1, 128), lambda h, i: (h, i, 0, 0))
    return tq, q_spec, k_spec, v_spec, row_spec, band_spec


def _sb_block(b, row0, tile, zero):
    t = ATT_T
    lo, hi, both = slice(row0, row0 + t), slice(row0 + t, row0 + 2 * t), slice(row0, row0 + 2 * t)
    c_hi = tile(2 * b + 1, hi, zero, 0)
    c_lo = tile(2 * b, lo, zero, 0)
    c_hi = tile(2 * b, hi, c_hi, None)
    carry = tuple(jnp.concatenate([x, y], axis=0) for x, y in zip(c_lo, c_hi))
    return _sb_walk(2 * b, carry, lambda kb, c, _: tile(kb, both, c, None))


def sb_fwd(qkv):
    s = qkv.shape[2]
    t = ATT_T
    tq, q_spec, k_spec, v_spec, row_spec, band_spec = _sb_specs(s)

    def body(q_ref, k_ref, v_ref, o_ref, band_ref, done_ref):
        i = pl.program_id(1)
        qs = q_ref[...] * 0.125
        starts, _, _, lm, a, _, open_left = _sb_band(i, qs, k_ref)
        ab = a.astype(BF16)
        for j, k0 in enumerate(starts):
            rows = slice(j * SB_SUB, (j + 1) * SB_SUB)
            o_ref[rows, :] = _dot(ab[rows], v_ref[pl.ds(k0, SB_BAND), :])
        worst = jnp.max(jnp.where(open_left, jnp.sum(lm, axis=-1, keepdims=True), NEG))
        done_ref[0] = (worst <= SB_DEAD).astype(jnp.int32)

        @pl.when(done_ref[0] == 0)
        def _():
            def tile(kb, rows, carry, ahead):
                run, acc = carry
                k0 = pl.multiple_of(kb * t, t)
                valid = None if ahead is None else _sb_valid(t, ahead)
                _, lm, a = _sb_tile(qs[rows], k_ref[pl.ds(k0, t), :], run, valid)
                acc = acc + _dot(a.astype(BF16), v_ref[pl.ds(k0, t), :])
                return run + jnp.sum(lm, axis=-1, keepdims=True), acc

            for n in range(tq // (2 * t)):
                _, acc = _sb_block(i * (tq // (2 * t)) + n, n * 2 * t, tile,
                                   (jnp.zeros((t, 1), F32), jnp.zeros((t, HEAD_DIM), F32)))
                o_ref[n * 2 * t:(n + 1) * 2 * t, :] = acc

        band_ref[...] = jnp.full(band_ref.shape, done_ref[0], jnp.int32).astype(F32)

    return pl.pallas_call(
        body, name="sb_fwd", grid=(N_HEADS, s // tq),
        in_specs=[q_spec, k_spec, v_spec],
        out_specs=[row_spec, band_spec],
        out_shape=[jax.ShapeDtypeStruct((N_HEADS, s, HEAD_DIM), F32),
                   jax.ShapeDtypeStruct((N_HEADS, s // tq, 1, 128), F32)],
        scratch_shapes=[pltpu.SMEM((1,), jnp.int32)],
        compiler_params=_params(("arbitrary", "arbitrary")),
    )(qkv, qkv, qkv)


def sb_bwd(qkv, do, o, band):
    s = qkv.shape[2]
    t = ATT_T
    tq, q_spec, k_spec, v_spec, row_spec, band_spec = _sb_specs(s)
    nq = s // tq
    any_spec = pl.BlockSpec(memory_space=pl.ANY)

    def body(q_ref, k_ref, v_ref, do_ref, o_ref, band_ref, dq_ref, dk_hbm, dv_hbm, dk_acc, dv_acc):
        h = pl.program_id(0)
        i = pl.program_id(1)

        @pl.when(i == 0)
        def _():
            dk_acc[...] = jnp.zeros_like(dk_acc)
            dv_acc[...] = jnp.zeros_like(dv_acc)

        qs_all = q_ref[...] * 0.125
        dob_all = do_ref[...]
        tot_all = jnp.sum(o_ref[...] * dob_all.astype(F32), axis=-1, keepdims=True)
        on_band = jnp.max(band_ref[...]) > 0.5

        def grads(qs, dob, tot, k, v, k0, run, run_g, valid):
            ls, lm, a = _sb_tile(qs, k, run, valid)
            ab = a.astype(BF16)
            g = ab.astype(F32) * _dot_nt(dob, v)
            g_left = tot - _sb_suffix(g, run_g)
            dz = g - jnp.exp(ls) * (g + g_left)
            if valid is not None:
                dz = jnp.where(valid, dz, 0.0)
            dzb = dz.astype(BF16)
            n = k.shape[0]
            dk_acc[pl.ds(k0, n), :] += _dot_tn(dzb, qs)
            dv_acc[pl.ds(k0, n), :] += _dot_tn(ab, dob)
            return dzb, lm, g

        @pl.when(on_band)
        def _():
            starts, kwins, ls, _, a, valid, _ = _sb_band(i, qs_all, k_ref)
            ab = a.astype(BF16)
            subs = [slice(j * SB_SUB, (j + 1) * SB_SUB) for j in range(len(starts))]
            vwins = [v_ref[pl.ds(k0, SB_BAND), :] for k0 in starts]
            g = ab.astype(F32) * jnp.concatenate([_dot_nt(dob_all[r], v) for r, v in zip(subs, vwins)], axis=0)
            dz = jnp.where(valid, g - jnp.exp(ls) * (g + (tot_all - _sb_suffix(g, None))), 0.0)
            dzb = dz.astype(BF16)
            for r, k0, k in zip(subs, starts, kwins):
                dq_ref[r, :] = _dot(dzb[r], k) * 0.125
                dk_acc[pl.ds(k0, SB_BAND), :] += _dot_tn(dzb[r], qs_all[r])
                dv_acc[pl.ds(k0, SB_BAND), :] += _dot_tn(ab[r], dob_all[r])

        @pl.when(jnp.logical_not(on_band))
        def _():
            def tile(kb, rows, carry, ahead):
                run, run_g, dq = carry
                k0 = pl.multiple_of(kb * t, t)
                k = k_ref[pl.ds(k0, t), :]
                valid = None if ahead is None else _sb_valid(t, ahead)
                dzb, lm, g = grads(qs_all[rows], dob_all[rows], tot_all[rows], k, v_ref[pl.ds(k0, t), :], k0,
                                   run, run_g, valid)
                return (run + jnp.sum(lm, axis=-1, keepdims=True),
                        run_g + jnp.sum(g, axis=-1, keepdims=True),
                        dq + _dot(dzb, k))

            zero = jnp.zeros((t, 1), F32)
            for n in range(tq // (2 * t)):
                _, _, dq = _sb_block(i * (tq // (2 * t)) + n, n * 2 * t, tile, (zero, zero, jnp.zeros((t, HEAD_DIM), F32)))
                dq_ref[n * 2 * t:(n + 1) * 2 * t, :] = dq * 0.125

        @pl.when(i == nq - 1)
        def _():
            pltpu.sync_copy(dk_acc, dk_hbm.at[h])
            pltpu.sync_copy(dv_acc, dv_hbm.at[h])

    return pl.pallas_call(
        body, name="sb_bwd", grid=(N_HEADS, nq),
        in_specs=[q_spec, k_spec, v_spec, row_spec, row_spec, band_spec],
        out_specs=[row_spec, any_spec, any_spec],
        out_shape=[jax.ShapeDtypeStruct((N_HEADS, s, HEAD_DIM), F32)] * 3,
        scratch_shapes=[pltpu.VMEM((s, HEAD_DIM), F32), pltpu.VMEM((s, HEAD_DIM), F32)],
        compiler_params=_params(("arbitrary", "arbitrary")),
    )(qkv, qkv, qkv, do, o, band)


def _branch_inputs(refs, br):
    ya_ref, yb_ref, yc_ref, yd_ref = refs
    if br == 1:
        return yb_ref[...]
    return _heads_to_lanes((ya_ref, None, yc_ref, yd_ref)[br])


def outproj_fwd(x, ya, yb, yc, yd, gates, bg, wout):
    s = x.shape[0]
    tm = min(ROW_T, s)

    def body(x_ref, ya_ref, yb_ref, yc_ref, yd_ref, gates_ref, bg_ref, w_ref, out_ref):
        pieces = []
        for br in range(4):
            cols = slice(br * D_BRANCH, (br + 1) * D_BRANCH)
            y = _branch_inputs((ya_ref, yb_ref, yc_ref, yd_ref), br)
            r = lax.rsqrt(jnp.mean(y * y, axis=-1, keepdims=True) + EPS)
            gt = gates_ref[:, cols]
            pieces.append((y * r * bg_ref[:, cols]) * (gt * _sigmoid(gt)))
        merged = jnp.concatenate(pieces, axis=1).astype(BF16)
        out_ref[...] = x_ref[...] + _dot(merged, w_ref[...])

    head_spec = pl.BlockSpec((N_HEADS, tm, HEAD_DIM), lambda i: (0, i, 0))
    return pl.pallas_call(
        body, name="outproj_fwd", grid=(s // tm,),
        in_specs=[pl.BlockSpec((tm, D_MODEL), lambda i: (i, 0)),
                  head_spec, pl.BlockSpec((tm, D_BRANCH), lambda i: (i, 0)), head_spec, head_spec,
                  pl.BlockSpec((tm, D_MODEL), lambda i: (i, 0)),
                  pl.BlockSpec((1, D_MODEL), lambda i: (0, 0)),
                  pl.BlockSpec((D_MODEL, D_MODEL), lambda i: (0, 0))],
        out_specs=pl.BlockSpec((tm, D_MODEL), lambda i: (i, 0)),
        out_shape=jax.ShapeDtypeStruct((s, D_MODEL), F32),
        compiler_params=_params(("arbitrary",)),
    )(x, ya, yb, yc, yd, gates, bg, wout)


def outproj_bwd(dout, ya, yb, yc, yd, gates, bg, wout):
    s = dout.shape[0]
    tm = min(ROW_T, s)

    def body(dout_ref, ya_ref, yb_ref, yc_ref, yd_ref, gates_ref, bg_ref, w_ref,
             dya_ref, dyb_ref, dyc_ref, dyd_ref, dgates_ref, dbg_ref, dw_ref):
        i = pl.program_id(0)

        @pl.when(i == 0)
        def _():
            dbg_ref[...] = jnp.zeros_like(dbg_ref)
            dw_ref[...] = jnp.zeros_like(dw_ref)

        doutb = dout_ref[...].astype(BF16)
        dmerged = _dot_nt(doutb, w_ref[...])
        pieces = []
        for br in range(4):
            cols = slice(br * D_BRANCH, (br + 1) * D_BRANCH)
            y = _branch_inputs((ya_ref, yb_ref, yc_ref, yd_ref), br)
            r = lax.rsqrt(jnp.mean(y * y, axis=-1, keepdims=True) + EPS)
            yn = y * r
            bgv = bg_ref[:, cols]
            gt = gates_ref[:, cols]
            sig = _sigmoid(gt)
            act = gt * sig
            n = yn * bgv
            pieces.append(n * act)
            dm = dmerged[:, cols]
            dn = dm * act
            dgates_ref[:, cols] = (dm * n * (sig * (1.0 + gt * (1.0 - sig)))).astype(BF16)
            dbg_ref[:, cols] += jnp.sum(dn * yn, axis=0, keepdims=True)
            u = dn * bgv
            dy = r * (u - yn * jnp.mean(yn * u, axis=-1, keepdims=True))
            if br == 1:
                dyb_ref[...] = dy
            else:
                dref = (dya_ref, None, dyc_ref, dyd_ref)[br]
                for hh in range(N_HEADS):
                    dref[hh] = dy[:, hh * HEAD_DIM:(hh + 1) * HEAD_DIM].astype(BF16)
        merged = jnp.concatenate(pieces, axis=1).astype(BF16)
        dw_ref[...] += _dot_tn(merged, doutb)

    head_spec = pl.BlockSpec((N_HEADS, tm, HEAD_DIM), lambda i: (0, i, 0))
    head_shape = jax.ShapeDtypeStruct((N_HEADS, s, HEAD_DIM), BF16)
    return pl.pallas_call(
        body, name="outproj_bwd", grid=(s // tm,),
        in_specs=[pl.BlockSpec((tm, D_MODEL), lambda i: (i, 0)),
                  head_spec, pl.BlockSpec((tm, D_BRANCH), lambda i: (i, 0)), head_spec, head_spec,
                  pl.BlockSpec((tm, D_MODEL), lambda i: (i, 0)),
                  pl.BlockSpec((1, D_MODEL), lambda i: (0, 0)),
                  pl.BlockSpec((D_MODEL, D_MODEL), lambda i: (0, 0))],
        out_specs=[head_spec, pl.BlockSpec((tm, D_BRANCH), lambda i: (i, 0)), head_spec, head_spec,
                   pl.BlockSpec((tm, D_MODEL), lambda i: (i, 0)),
                   pl.BlockSpec((1, D_MODEL), lambda i: (0, 0)),
                   pl.BlockSpec((D_MODEL, D_MODEL), lambda i: (0, 0))],
        out_shape=[head_shape, jax.ShapeDtypeStruct((s, D_BRANCH), F32), head_shape, head_shape,
                   jax.ShapeDtypeStruct((s, D_MODEL), BF16),
                   jax.ShapeDtypeStruct((1, D_MODEL), F32),
                   jax.ShapeDtypeStruct((D_MODEL, D_MODEL), F32)],
        compiler_params=_params(("arbitrary",)),
    )(dout, ya, yb, yc, yd, gates, bg, wout)


def final_loss(x, tgt, g):
    s = x.shape[0]
    tm = min(ROW_T, s)

    def body(x_ref, t_ref, g_ref, loss_ref, dx_ref, dg_ref):
        i = pl.program_id(0)

        @pl.when(i == 0)
        def _():
            loss_ref[...] = jnp.zeros_like(loss_ref)
            dg_ref[...] = jnp.zeros_like(dg_ref)

        xv = x_ref[...]
        gv = g_ref[...]
        r = lax.rsqrt(jnp.mean(xv * xv, axis=-1, keepdims=True) + EPS)
        xn = xv * r
        err = xn * gv - t_ref[...]
        loss_ref[...] += jnp.sum(err * err) * (0.5 / D_MODEL)
        dy = err * (1.0 / D_MODEL)
        u = dy * gv
        dx_ref[...] = r * (u - xn * jnp.mean(xn * u, axis=-1, keepdims=True))
        dg_ref[...] += jnp.sum(dy * xn, axis=0, keepdims=True)

    return pl.pallas_call(
        body, name="final_loss", grid=(s // tm,),
        in_specs=[pl.BlockSpec((tm, D_MODEL), lambda i: (i, 0)),
                  pl.BlockSpec((tm, D_MODEL), lambda i: (i, 0)),
                  pl.BlockSpec((1, D_MODEL), lambda i: (0, 0))],
        out_specs=[pl.BlockSpec((1, 128), lambda i: (0, 0)),
                   pl.BlockSpec((tm, D_MODEL), lambda i: (i, 0)),
                   pl.BlockSpec((1, D_MODEL), lambda i: (0, 0))],
        out_shape=[jax.ShapeDtypeStruct((1, 128), F32),
                   jax.ShapeDtypeStruct((s, D_MODEL), F32),
                   jax.ShapeDtypeStruct((1, D_MODEL), F32)],
        compiler_params=_params(("arbitrary",)),
    )(x, tgt, g)


def _rel_index():
    i = np.arange(A_TQ)[:, None]
    j = np.arange(A_BAND)[None, :]
    rel = np.clip(i - j + (A_BAND - A_TQ), -MAX_REL, MAX_REL) + MAX_REL
    dchunk = i // CHUNK + LOOKBACK - j // CHUNK
    valid = (dchunk >= 0) & (dchunk <= LOOKBACK)
    return jnp.asarray(np.where(valid, rel, -1).astype(np.int32))


def _layer_consts(p):
    tbias = relbias_tile(p["rel_bias"], _rel_index())
    return dict(
        norm_g=p["norm_g"].reshape(1, D_MODEL),
        v_gain=p["v_gain"].reshape(1, D_BRANCH),
        b_col=p["b_s"].reshape(N_HEADS, SG_CHUNK, 1),
        bg=p["branch_gain"].reshape(1, D_MODEL),
        tbias=tbias,
    )


def _gate_layout(fp, b_f, s):
    nb = s // 128
    ft = fp[:, :N_HEADS].T.reshape(N_HEADS * nb, 128)
    bcol = jnp.repeat(b_f, nb).reshape(N_HEADS * nb, 1)
    return ft, bcol


def layer_fwd(x, p, ride=()):
    s = x.shape[0]
    c = _layer_consts(p)
    h, qkv, kva, gates, uv, fp = inproj_fwd(x, c["norm_g"], p["wp"])
    ya, lse_a = mix_a_fwd(qkv, kva, c["tbias"])
    yb = mix_b_fwd(uv, c["v_gain"], p["w_s"], c["b_col"])
    ft, bcol = _gate_layout(fp, p["b_f"], s)
    c_row = fox_gate_fwd(ft, bcol).reshape(N_HEADS, s // ATT_T, 1, ATT_T)
    yc, ref_c, rl_c, *rode = fox_fwd(qkv, c_row, ride)
    yd, band_d = sb_fwd(qkv)
    out = outproj_fwd(x, ya, yb, yc, yd, gates, c["bg"], p["wout"])
    saved = dict(consts=c, x=x, h=h, qkv=qkv, gates=gates, uv=uv, kva=kva, ft=ft, bcol=bcol,
                 c_row=c_row, ya=ya, lse_a=lse_a, yb=yb, yc=yc, ref_c=ref_c, rl_c=rl_c, yd=yd, band_d=band_d)
    return out, saved, rode


def layer_bwd(dout, p, sv, exchange=False, upper_w_in=None):
    s = dout.shape[0]
    c = sv["consts"]
    dya, dyb, dyc, dyd, dgates, dbg, dwout = outproj_bwd(
        dout, sv["ya"], sv["yb"], sv["yc"], sv["yd"], sv["gates"], c["bg"], p["wout"])
    dqa, dka, dva, dt = mix_a_bwd(sv["qkv"], sv["kva"], c["tbias"], dya, sv["ya"], sv["lse_a"])
    drel = relbias_grad(dt, _rel_index())[:N_HEADS, :2 * MAX_REL + 1]
    duv, dws, dbs, dvgain = mix_b_bwd(sv["uv"], c["v_gain"], p["w_s"], c["b_col"], dyb)
    ride = [dwout.astype(BF16).reshape(4, D_BRANCH, D_MODEL)] if exchange else []
    if upper_w_in is not None:
        ride.append(upper_w_in)
    dqc, dkc, dvc, dc, *rode = fox_bwd(sv["qkv"], sv["c_row"], dyc, sv["yc"], sv["ref_c"], sv["rl_c"], ride)
    dft, dbf = fox_gate_bwd(sv["ft"], sv["bcol"], dc.reshape(N_HEADS * (s // 128), 128))
    dfp = jnp.pad(dft.reshape(N_HEADS, s).T, ((0, 0), (0, 128 - N_HEADS)))
    dqd, dkd, dvd = sb_bwd(sv["qkv"], dyd, sv["yd"], sv["band_d"])
    dp, dx, dnorm = inproj_bwd((dqa, dka, dva, dqc, dkc, dvc, dqd, dkd, dvd), dgates, duv, dfp,
                               p["wp"], sv["x"], c["norm_g"], dout)
    grads = dict(norm_g=dnorm.reshape(D_MODEL), w_in_shards=inproj_wgrad(sv["h"], dp), b_f=dbf[:N_HEADS, 0], rel_bias=drel,
                 w_s=dws, b_s=dbs.reshape(N_HEADS, SG_CHUNK), v_gain=dvgain.reshape(D_BRANCH),
                 branch_gain=dbg.reshape(4, D_BRANCH), wout=dwout)
    if exchange:
        grads["w_out_parts"] = rode[0]
    return dx, grads, (rode[1] if upper_w_in is not None else None)


def local_step(x, tgt, layers, final_g, next_shards=None):
    layers = list(layers)
    saved = []
    cur = x
    for l, p in enumerate(layers):
        ride = next_shards[l] if next_shards is not None and l + 1 < len(layers) else ()
        cur, sv, rode = layer_fwd(cur, p, ride)
        saved.append(sv)
        if ride:
            layers[l + 1] = dict(layers[l + 1], wp=pack_w_in(rode[0][None])[0], wout=rode[1].reshape(D_MODEL, D_MODEL))
    loss, dcur, dfinal = final_loss(cur, tgt, final_g.reshape(1, D_MODEL))
    grads = [None] * len(layers)
    for l in reversed(range(len(layers))):
        exchange = next_shards is not None
        upper = grads[l + 1]["w_in_shards"] if exchange and l + 1 < len(layers) else None
        dcur, grads[l], got = layer_bwd(dcur, layers[l], saved[l], exchange, upper)
        if upper is not None:
            grads[l + 1]["w_in_parts"] = got
    return loss[0, 0], dcur, grads, dfinal.reshape(D_MODEL)


def _chip_gather(pairs, send_sems, recv_sems, loc_sems):
    x, y, c = lax.axis_index("x"), lax.axis_index("y"), lax.axis_index("c")
    me = 2 * x + y
    chips = [(1 - x, y), (x, 1 - y), (1 - x, 1 - y)]
    npair = len(pairs)

    def local():
        return [pltpu.make_async_copy(src, dst(me), loc_sems.at[n]) for n, (src, dst) in enumerate(pairs)]

    def remote(j, n, slot):
        src, dst = pairs[n]
        return pltpu.make_async_remote_copy(
            src_ref=src, dst_ref=dst(slot), send_sem=send_sems.at[npair * j + n], recv_sem=recv_sems.at[npair * j + n],
            device_id=(chips[j][0], chips[j][1], c), device_id_type=MESH)

    def start():
        for cp in local():
            cp.start()
        for j in range(3):
            for n in range(npair):
                remote(j, n, me).start()

    def wait():
        for j in range(3):
            for n in range(npair):
                remote(j, n, 2 * chips[j][0] + chips[j][1]).wait_recv()
        for j in range(3):
            for n in range(npair):
                remote(j, n, me).wait_send()
        for cp in local():
            cp.wait()

    return start, wait


def gather_weights(w_in, w_out, gains):
    depth = w_in.shape[0]

    def body(in_ref, out_ref, g_ref, oin_ref, oout_ref, og_ref, send_sems, recv_sems, loc_sems):
        pairs = [(in_ref, lambda s: oin_ref.at[:, s]), (out_ref, lambda s: oout_ref.at[:, s]), (g_ref, lambda s: og_ref.at[s])]
        start, wait = _chip_gather(pairs, send_sems, recv_sems, loc_sems)
        start()
        wait()

    any_spec = pl.BlockSpec(memory_space=pl.ANY)
    return pl.pallas_call(
        body, name="gather_weights",
        in_specs=[any_spec] * 3, out_specs=[any_spec] * 3,
        out_shape=[jax.ShapeDtypeStruct((depth, 4) + w_in.shape[1:], w_in.dtype),
                   jax.ShapeDtypeStruct((depth, 4) + w_out.shape[1:], w_out.dtype),
                   jax.ShapeDtypeStruct((4,) + gains.shape, gains.dtype)],
        scratch_shapes=[pltpu.SemaphoreType.DMA((9,)), pltpu.SemaphoreType.DMA((9,)), pltpu.SemaphoreType.DMA((3,))],
    )(w_in, w_out, gains)


def pack_w_in(shards):
    depth = shards.shape[0]
    tr = 256

    def body(s_ref, o_ref):
        full = jnp.concatenate([s_ref[n] for n in range(4)], axis=1)
        o_ref[...] = jnp.concatenate([full[:, :SEC_D_Q], full[:, SEC_D_Q + N_HEADS:], full[:, SEC_D_Q:SEC_D_Q + N_HEADS],
                                      jnp.zeros((tr, N_PACK - N_IN), BF16)], axis=1)

    return pl.pallas_call(
        body, name="pack_w_in", grid=(depth, D_MODEL // tr),
        in_specs=[pl.BlockSpec((None, 4, tr, N_SHARD), lambda l, r: (l, 0, r, 0))],
        out_specs=pl.BlockSpec((None, tr, N_PACK), lambda l, r: (l, r, 0)),
        out_shape=jax.ShapeDtypeStruct((depth, D_MODEL, N_PACK), BF16),
        compiler_params=_params(("arbitrary", "arbitrary")),
    )(shards)


def _device_exchange(flows, send_sems, recv_sems, loc_sems):
    x, y, c = lax.axis_index("x"), lax.axis_index("y"), lax.axis_index("c")
    me_chip = 2 * x + y
    me = 4 * x + 2 * y + c
    peers = [(x, y, 1 - c)]
    for px, py in [(1 - x, y), (x, 1 - y), (1 - x, 1 - y)]:
        peers += [(px, py, c), (px, py, 1 - c)]
    nflow = len(flows)

    def local():
        return [pltpu.make_async_copy(src(me_chip), dst(me), loc_sems.at[f]) for f, (src, dst) in enumerate(flows)]

    def copies(n, chip, slot):
        return [pltpu.make_async_remote_copy(src_ref=src(chip), dst_ref=dst(slot), send_sem=send_sems.at[nflow * n + f],
                                             recv_sem=recv_sems.at[nflow * n + f], device_id=peers[n], device_id_type=MESH)
                for f, (src, dst) in enumerate(flows)]

    def start():
        for cp in local():
            cp.start()
        for n, (px, py, _) in enumerate(peers):
            for cp in copies(n, 2 * px + py, me):
                cp.start()

    def wait():
        for n, (px, py, pc) in enumerate(peers):
            for cp in copies(n, me_chip, 4 * px + 2 * py + pc):
                cp.wait_recv()
        for n, (px, py, _) in enumerate(peers):
            for cp in copies(n, 2 * px + py, me):
                cp.wait_send()
        for cp in local():
            cp.wait()

    return start, wait


def _exchange_flows(srcs, dsts):
    return [((lambda s, src=src: src.at[s]) if src.shape[0] == 4 else (lambda s, src=src: src),
             lambda d, dst=dst: dst.at[d]) for src, dst in zip(srcs, dsts)]


def _exchange_shapes(arrays):
    return [jax.ShapeDtypeStruct((8,) + (a.shape[1:] if a.shape[0] == 4 else a.shape), a.dtype) for a in arrays]


def _exchange_sems(n):
    return [pltpu.SemaphoreType.DMA((7 * n,)), pltpu.SemaphoreType.DMA((7 * n,)), pltpu.SemaphoreType.DMA((n,))]


def exchange_grads(*arrays):
    n = len(arrays)

    def body(*refs):
        start, wait = _device_exchange(_exchange_flows(refs[:n], refs[n:2 * n]), *refs[2 * n:])
        start()
        wait()

    any_spec = pl.BlockSpec(memory_space=pl.ANY)
    return pl.pallas_call(
        body, name="exchange_grads",
        in_specs=[any_spec] * n, out_specs=[any_spec] * n, out_shape=_exchange_shapes(arrays),
        scratch_shapes=_exchange_sems(n),
    )(*arrays)


def adamw_reduce(parts, w, m, v, name, tr):
    rows, width = w.shape
    per = rows // len(parts) // tr
    c1 = 1.0 - ADAM_B1 ** ADAM_STEP
    c2 = 1.0 - ADAM_B2 ** ADAM_STEP

    def body(*refs):
        p_refs = refs[:len(parts)]
        w_ref, m_ref, v_ref, g_ref, d_ref, nm_ref, nv_ref = refs[len(parts):]
        i = pl.program_id(0)
        p = p_refs[0][...]
        for n in range(1, len(parts)):
            p = jnp.where(i >= n * per, p_refs[n][...], p)
        g = p[0].astype(F32)
        for n in range(1, 8):
            g = g + p[n].astype(F32)
        g_ref[...] = g
        nm = ADAM_B1 * m_ref[...] + (1.0 - ADAM_B1) * g
        nv = ADAM_B2 * v_ref[...] + (1.0 - ADAM_B2) * (g * g)
        nm_ref[...] = nm
        nv_ref[...] = nv
        d_ref[...] = -ADAM_LR * ((nm / c1) / (jnp.sqrt(nv / c2) + ADAM_EPS) + ADAM_WD * w_ref[...])

    spec = pl.BlockSpec((tr, width), lambda i: (i, 0))
    shape = jax.ShapeDtypeStruct((rows, width), F32)
    return pl.pallas_call(
        body, name=name, grid=(rows // tr,),
        in_specs=[pl.BlockSpec((8, tr, width), lambda i, n=n: (0, jnp.clip(i - n * per, 0, per - 1), 0))
                  for n in range(len(parts))] + [spec, spec, spec],
        out_specs=[spec] * 4, out_shape=[shape] * 4,
        compiler_params=_params(("arbitrary",)),
    )(*parts, w, m, v)


SMALL =("norm_g", "b_f", "rel_bias", "w_s", "b_s", "v_gain", "final_g")
WEIGHTS = ("norm_g", "w_in", "b_f", "rel_bias", "w_s", "b_s", "v_gain", "branch_gain", "w_out", "final_g")
PACK_ROW_TILE = 512


def _rows_of(shape):
    return -(-int(np.prod(shape)) // 128)


def _pack(leaves):
    parts = []
    for a in leaves:
        flat = a.reshape(-1).astype(F32)
        parts.append(jnp.pad(flat, (0, _rows_of(a.shape) * 128 - flat.shape[0])))
    flat = jnp.concatenate(parts)
    rows = flat.shape[0] // 128
    total = -(-rows // PACK_ROW_TILE) * PACK_ROW_TILE
    return jnp.pad(flat, (0, (total - rows) * 128)).reshape(total, 128)


def _unpack(slab, shapes):
    out, row = [], 0
    for shp in shapes:
        n = int(np.prod(shp))
        r = _rows_of(shp)
        out.append(slab[row:row + r].reshape(-1)[:n].reshape(shp))
        row += r
    return out


def kernel(x, norm_g, w_in, b_f, rel_bias, w_s, b_s, v_gain, branch_gain, w_out, final_g, loss_target, m_norm_g, m_w_in, m_b_f, m_rel_bias, m_w_s, m_b_s, m_v_gain, m_branch_gain, m_w_out, m_final_g, v_norm_g, v_w_in, v_b_f, v_rel_bias, v_w_s, v_b_s, v_v_gain, v_branch_gain, v_w_out, v_final_g):
    depth = norm_g.shape[0]
    weights = dict(norm_g=norm_g, w_in=w_in, b_f=b_f, rel_bias=rel_bias, w_s=w_s, b_s=b_s, v_gain=v_gain,
                   branch_gain=branch_gain, w_out=w_out, final_g=final_g)
    mom1 = dict(norm_g=m_norm_g, w_in=m_w_in, b_f=m_b_f, rel_bias=m_rel_bias, w_s=m_w_s, b_s=m_b_s,
                v_gain=m_v_gain, branch_gain=m_branch_gain, w_out=m_w_out, final_g=m_final_g)
    mom2 = dict(norm_g=v_norm_g, w_in=v_w_in, b_f=v_b_f, rel_bias=v_rel_bias, w_s=v_w_s, b_s=v_b_s,
                v_gain=v_v_gain, branch_gain=v_branch_gain, w_out=v_w_out, final_g=v_final_g)

    wf = jnp.pad(branch_gain.reshape(-1), (0, 8 * 128 - branch_gain.size)).reshape(8, 128)
    w_in_b, w_out_b = w_in.astype(BF16), w_out.astype(BF16)
    w_in_shards, w_out_shards, gf = gather_weights(w_in_b[:1], w_out_b[:1], wf)
    bg_full = gf.reshape(4, -1)[:, :branch_gain.size].reshape((4,) + branch_gain.shape)
    bg_full = jnp.moveaxis(bg_full, 0, 2).reshape(depth, 4, D_BRANCH)

    layers = [dict(norm_g=norm_g[l], b_f=b_f[l], rel_bias=rel_bias[l], w_s=w_s[l],
                   b_s=b_s[l], v_gain=v_gain[l], branch_gain=bg_full[l]) for l in range(depth)]
    layers[0].update(wp=pack_w_in(w_in_shards)[0], wout=w_out_shards.reshape(D_MODEL, D_MODEL))
    next_shards = [(w_in_b[l + 1], w_out_b[l + 1]) for l in range(depth - 1)]

    loss_part, grad_x, lgrads, dfinal = local_step(x[0], loss_target[0], layers, final_g, next_shards)
    loss = lax.psum(loss_part, ("x", "y", "c"))

    stack = lambda k: jnp.stack([g[k] for g in lgrads])
    d_gain = jnp.moveaxis(stack("branch_gain").reshape(depth, 4, 4, HEAD_DIM), 2, 0).reshape(4, -1)
    d_gain = jnp.pad(d_gain, ((0, 0), (0, 8 * 128 - d_gain.shape[1]))).reshape(4, 8, 128)
    small = dict(norm_g=stack("norm_g"), b_f=stack("b_f"), rel_bias=stack("rel_bias"), w_s=stack("w_s"),
                 b_s=stack("b_s"), v_gain=stack("v_gain"), final_g=dfinal)
    parts_in, parts_gain, parts_small = exchange_grads(lgrads[0]["w_in_shards"], d_gain, _pack([small[k] for k in SMALL]))
    parts = dict(w_in=[parts_in] + [g["w_in_parts"] for g in lgrads[1:]], w_out=[g["w_out_parts"] for g in lgrads])

    outs = {}
    tags = ("grad", "delta", "new_m", "new_v")
    for k in ("w_in", "w_out"):
        rows = depth * weights[k].shape[1]
        flat = lambda a: a.reshape(rows, a.shape[-1])
        res = adamw_reduce(parts[k], flat(weights[k]), flat(mom1[k]), flat(mom2[k]), "adamw_" + k, 256)
        for tag, a in zip(tags, res):
            outs[tag, k] = a.reshape(weights[k].shape)
    gain8 = lambda a: jnp.pad(a.reshape(-1), (0, 8 * 128 - a.size)).reshape(8, 128)
    res = adamw_reduce([parts_gain], gain8(branch_gain), gain8(m_branch_gain), gain8(v_branch_gain), "adamw_gain", 8)
    for tag, a in zip(tags, res):
        outs[tag, "branch_gain"] = a.reshape(-1)[:branch_gain.size].reshape(branch_gain.shape)
    pack_small = lambda d: _pack([d[k] for k in SMALL])
    res = adamw_reduce([parts_small], pack_small(weights), pack_small(mom1), pack_small(mom2), "adamw_small", PACK_ROW_TILE)
    for tag, slab in zip(tags, res):
        for k, a in zip(SMALL, _unpack(slab, [weights[k].shape for k in SMALL])):
            outs[tag, k] = a
    result = [loss, grad_x[None]]
    for tag in ("grad", "delta", "new_m", "new_v"):
        result += [outs[tag, k] for k in WEIGHTS]
    return tuple(result)
```

```python
import jax
import jax.numpy as jnp
import numpy as np
from jax import lax
from jax.experimental import pallas as pl
from jax.experimental.pallas import tpu as pltpu

F32 = jnp.float32
BF16 = jnp.bfloat16
MESH = pl.DeviceIdType.MESH

D_MODEL = 1024
D_BRANCH = 256
N_HEADS = 4
HEAD_DIM = 64
CHUNK = 64
LOOKBACK = 8
MAX_REL = 128
SG_CHUNK = 128
EPS = 1e-6
N_IN = 3844
N_PACK = 3968
F_COL = 3840
N_SHARD = 961
NEG = -1e30

A_TQ = 128
A_BAND = A_TQ + LOOKBACK * CHUNK
REL_LO = MAX_REL - (CHUNK - 1)
REL_HI = 2 * MAX_REL + 1
A_PAD = LOOKBACK * CHUNK
A_QB = 1024
ATT_T = 256
FOX_TQ = 512
FOX_WIDE = 4
FOX_DEAD2 = -136.0
LOG2E = 1.4426950408889634
SB_TQ = 1024
SB_SUB = 128
SB_BACK = 256
SB_BAND = SB_SUB + SB_BACK
SB_DEAD = -110.0
ROW_T = 512
VMEM_LIMIT = 56 * 1024 * 1024

ADAM_LR = 0.001
ADAM_B1 = 0.9
ADAM_B2 = 0.999
ADAM_EPS = 1e-08
ADAM_WD = 0.01
ADAM_STEP = 10

SEC_A_Q, SEC_A_K, SEC_A_V, SEC_A_G = 0, 256, 512, 768
SEC_B_U, SEC_B_V, SEC_B_G = 1024, 1280, 1536
SEC_C_Q, SEC_C_K, SEC_C_V, SEC_C_G = 1792, 2048, 2304, 2560
SEC_D_Q, SEC_D_K, SEC_D_V, SEC_D_G = 2816, 3072, 3328, 3584
QKV_SECS = (SEC_A_Q, SEC_C_Q, SEC_C_K, SEC_C_V, SEC_D_Q, SEC_D_K, SEC_D_V)
GATE_SECS = (SEC_A_G, SEC_B_G, SEC_C_G, SEC_D_G)


def _dot(a, b):
    return jnp.dot(a, b, preferred_element_type=F32)


def _dot_nt(a, b):
    return lax.dot_general(a, b, (((1,), (1,)), ((), ())), preferred_element_type=F32)


def _dot_tn(a, b):
    return lax.dot_general(a, b, (((0,), (0,)), ((), ())), preferred_element_type=F32)


def _split2(x):
    hi = x.astype(BF16)
    lo = (x - hi.astype(F32)).astype(BF16)
    return hi, lo


def _split3(x):
    hi = x.astype(BF16)
    r = x - hi.astype(F32)
    mid = r.astype(BF16)
    lo = (r - mid.astype(F32)).astype(BF16)
    return hi, mid, lo


def _sigmoid(x):
    return 1.0 / (1.0 + jnp.exp(-x))


def _params(sem=None, vmem=VMEM_LIMIT):
    return pltpu.CompilerParams(dimension_semantics=sem, vmem_limit_bytes=vmem)


def _heads_to_lanes(ref):
    return jnp.concatenate([ref[h] for h in range(N_HEADS)], axis=1)


def inproj_fwd(x, g, wp):
    s = x.shape[0]
    tm = A_PAD

    def body(x_ref, g_ref, w_ref, h_ref, qkv_ref, kva_ref, gates_ref, uv_ref, f_ref):
        xv = x_ref[...]
        r = lax.rsqrt(jnp.mean(xv * xv, axis=-1, keepdims=True) + EPS)
        h = (xv * r * g_ref[...]).astype(BF16)
        h_ref[...] = h
        for n, off in enumerate(QKV_SECS):
            p = _dot(h, w_ref[:, off:off + D_BRANCH])
            for hh in range(N_HEADS):
                qkv_ref[n, hh] = p[:, hh * HEAD_DIM:(hh + 1) * HEAD_DIM].astype(BF16)
        for n, off in enumerate((SEC_A_K, SEC_A_V)):
            p = _dot(h, w_ref[:, off:off + D_BRANCH])
            for hh in range(N_HEADS):
                kva_ref[n, hh] = p[:, hh * HEAD_DIM:(hh + 1) * HEAD_DIM].astype(BF16)
        for n, off in enumerate(GATE_SECS):
            gates_ref[:, n * D_BRANCH:(n + 1) * D_BRANCH] = _dot(h, w_ref[:, off:off + D_BRANCH])
        uv_ref[...] = _dot(h, w_ref[:, SEC_B_U:SEC_B_U + 2 * D_BRANCH])
        f_ref[...] = _dot(h, w_ref[:, F_COL:F_COL + 128])

    return pl.pallas_call(
        body, name="inproj_fwd", grid=(s // tm,),
        in_specs=[pl.BlockSpec((tm, D_MODEL), lambda i: (i, 0)),
                  pl.BlockSpec((1, D_MODEL), lambda i: (0, 0)),
                  pl.BlockSpec((D_MODEL, N_PACK), lambda i: (0, 0))],
        out_specs=[pl.BlockSpec((tm, D_MODEL), lambda i: (i, 0)),
                   pl.BlockSpec((len(QKV_SECS), N_HEADS, tm, HEAD_DIM), lambda i: (0, 0, i, 0)),
                   pl.BlockSpec((2, N_HEADS, tm, HEAD_DIM), lambda i: (0, 0, i + 1, 0)),
                   pl.BlockSpec((tm, D_MODEL), lambda i: (i, 0)),
                   pl.BlockSpec((tm, 2 * D_BRANCH), lambda i: (i, 0)),
                   pl.BlockSpec((tm, 128), lambda i: (i, 0))],
        out_shape=[jax.ShapeDtypeStruct((s, D_MODEL), BF16),
                   jax.ShapeDtypeStruct((len(QKV_SECS), N_HEADS, s, HEAD_DIM), BF16),
                   jax.ShapeDtypeStruct((2, N_HEADS, s + tm, HEAD_DIM), BF16),
                   jax.ShapeDtypeStruct((s, D_MODEL), F32),
                   jax.ShapeDtypeStruct((s, 2 * D_BRANCH), F32),
                   jax.ShapeDtypeStruct((s, 128), F32)],
        compiler_params=_params(("arbitrary",)),
    )(x, g, wp)


def inproj_bwd(dqkv, dgates, duv, dfp, wp, x, g, dres):
    s = x.shape[0]
    tm = A_PAD

    def body(*refs):
        dq_refs = refs[:9]
        dgates_ref, duv_ref, dfp_ref, w_ref, x_ref, g_ref, dres_ref, dp_ref, dx_ref, dg_ref = refs[9:]
        i = pl.program_id(0)
        a_q, a_k, a_v, c_q, c_k, c_v, d_q, d_k, d_v = [_heads_to_lanes(r).astype(BF16) for r in dq_refs]
        dgt = dgates_ref[...]
        duv_b = duv_ref[...].astype(BF16)
        dp = jnp.concatenate(
            [a_q, a_k, a_v, dgt[:, 0:256], duv_b, dgt[:, 256:512], c_q, c_k, c_v, dgt[:, 512:768],
             d_q, d_k, d_v, dgt[:, 768:1024], dfp_ref[...].astype(BF16)], axis=1)
        dp_ref[...] = dp
        dh = _dot_nt(dp, w_ref[...])
        xv = x_ref[...]
        r = lax.rsqrt(jnp.mean(xv * xv, axis=-1, keepdims=True) + EPS)
        xn = xv * r
        u = dh * g_ref[...]
        dx_ref[...] = dres_ref[...] + r * (u - xn * jnp.mean(xn * u, axis=-1, keepdims=True))

        @pl.when(i == 0)
        def _():
            dg_ref[...] = jnp.zeros_like(dg_ref)

        dg_ref[...] += jnp.sum(dh * xn, axis=0, keepdims=True)

    head_spec = pl.BlockSpec((N_HEADS, tm, HEAD_DIM), lambda i: (0, i, 0))
    padded_spec = pl.BlockSpec((N_HEADS, tm, HEAD_DIM), lambda i: (0, i + 1, 0))
    return pl.pallas_call(
        body, name="inproj_bwd", grid=(s // tm,),
        in_specs=[head_spec, padded_spec, padded_spec] + [head_spec] * 6 + [
            pl.BlockSpec((tm, D_MODEL), lambda i: (i, 0)),
            pl.BlockSpec((tm, 2 * D_BRANCH), lambda i: (i, 0)),
            pl.BlockSpec((tm, 128), lambda i: (i, 0)),
            pl.BlockSpec((D_MODEL, N_PACK), lambda i: (0, 0)),
            pl.BlockSpec((tm, D_MODEL), lambda i: (i, 0)),
            pl.BlockSpec((1, D_MODEL), lambda i: (0, 0)),
            pl.BlockSpec((tm, D_MODEL), lambda i: (i, 0))],
        out_specs=[pl.BlockSpec((tm, N_PACK), lambda i: (i, 0)),
                   pl.BlockSpec((tm, D_MODEL), lambda i: (i, 0)),
                   pl.BlockSpec((1, D_MODEL), lambda i: (0, 0))],
        out_shape=[jax.ShapeDtypeStruct((s, N_PACK), BF16),
                   jax.ShapeDtypeStruct((s, D_MODEL), F32),
                   jax.ShapeDtypeStruct((1, D_MODEL), F32)],
        compiler_params=_params(("arbitrary",)),
    )(*dqkv, dgates, duv, dfp, wp, x, g, dres)


def inproj_wgrad(h, dp):
    s, m = h.shape
    tm = min(2 * ROW_T, s)
    tmm = 256
    nsteps = s // tm

    def body(a_ref, b_ref, o_ref, acc_ref):
        k = pl.program_id(1)

        @pl.when(k == 0)
        def _():
            acc_ref[...] = jnp.zeros_like(acc_ref)

        acc_ref[...] += _dot_tn(a_ref[...], b_ref[...])

        @pl.when(k == nsteps - 1)
        def _():
            acc = acc_ref[...]
            full = jnp.concatenate([acc[:, :SEC_D_Q], acc[:, F_COL:F_COL + N_HEADS], acc[:, SEC_D_Q:F_COL]], axis=1)
            for n in range(4):
                o_ref[n] = full[:, n * N_SHARD:(n + 1) * N_SHARD].astype(BF16)

    return pl.pallas_call(
        body, name="inproj_wgrad", grid=(m // tmm, nsteps),
        in_specs=[pl.BlockSpec((tm, tmm), lambda j, k: (k, j)),
                  pl.BlockSpec((tm, N_PACK), lambda j, k: (k, 0))],
        out_specs=pl.BlockSpec((4, tmm, N_SHARD), lambda j, k: (0, j, 0)),
        out_shape=jax.ShapeDtypeStruct((4, m, N_SHARD), BF16),
        scratch_shapes=[pltpu.VMEM((tmm, N_PACK), F32)],
        compiler_params=_params(("arbitrary", "arbitrary")),
    )(h, dp)


def _a_specs(s):
    nq = s // A_QB
    per = A_QB // A_PAD
    q_spec = pl.BlockSpec((None, None, A_QB, HEAD_DIM), lambda h, i: (0, h, jnp.minimum(i, nq - 1), 0))
    kv_specs = [pl.BlockSpec((None, None, A_PAD, HEAD_DIM),
                             lambda h, i, n=n, m=m: (n, h, jnp.minimum(per * i + m, per * nq), 0))
                for n in range(2) for m in range(per + 1)]
    t_spec = pl.BlockSpec((None, A_TQ, A_BAND), lambda h, i: (h, 0, 0))
    return nq, q_spec, kv_specs, t_spec


def _a_window(refs, i):
    first = refs[0][...]
    return jnp.concatenate([jnp.where(i > 0, first, jnp.zeros_like(first))] + [r[...] for r in refs[1:]], axis=0)


def _a_scores(q_ref, k, t_ref, i, j):
    rows = slice(j * A_TQ, (j + 1) * A_TQ)
    qs = q_ref[rows, :] * 0.125
    kj = k[j * A_TQ:j * A_TQ + A_BAND, :]
    sc = _dot_nt(qs, kj) + t_ref[...]
    col = lax.broadcasted_iota(jnp.int32, (A_TQ, A_BAND), 1)
    sc = jnp.where(col >= A_PAD - i * A_QB - j * A_TQ, sc, NEG)
    return rows, qs, kj, sc


def mix_a_fwd(qkv, kva, tbias):
    s = qkv.shape[2]
    nq, q_spec, kv_specs, t_spec = _a_specs(s)
    nwin = len(kv_specs) // 2

    def body(*refs):
        q_ref, t_ref, o_ref, lse_ref = refs[0], refs[1 + 2 * nwin], refs[2 + 2 * nwin], refs[3 + 2 * nwin]
        i = pl.program_id(1)
        k = _a_window(refs[1:1 + nwin], i)
        v = _a_window(refs[1 + nwin:1 + 2 * nwin], i)
        for j in range(A_QB // A_TQ):
            rows, _, _, sc = _a_scores(q_ref, k, t_ref, i, j)
            m = jnp.max(sc, axis=-1, keepdims=True)
            p = jnp.exp(sc - m)
            l = jnp.sum(p, axis=-1, keepdims=True)
            o_ref[rows, :] = _dot(p.astype(BF16), v[j * A_TQ:j * A_TQ + A_BAND, :]) / l
            lse_ref[rows, :] = m + jnp.log(l)

    return pl.pallas_call(
        body, name="mix_a_fwd", grid=(N_HEADS, nq),
        in_specs=[q_spec] + kv_specs + [t_spec],
        out_specs=[pl.BlockSpec((None, A_QB, HEAD_DIM), lambda h, i: (h, i, 0)),
                   pl.BlockSpec((None, A_QB, 1), lambda h, i: (h, i, 0))],
        out_shape=[jax.ShapeDtypeStruct((N_HEADS, s, HEAD_DIM), F32),
                   jax.ShapeDtypeStruct((N_HEADS, s, 1), F32)],
        compiler_params=_params(("arbitrary", "arbitrary")),
    )(qkv, *([kva] * (2 * nwin)), tbias)


def mix_a_bwd(qkv, kva, tbias, do, o, lse):
    s = qkv.shape[2]
    nq, q_spec, kv_specs, t_spec = _a_specs(s)
    nwin = len(kv_specs) // 2
    row_spec = lambda w: pl.BlockSpec((None, A_QB, w), lambda h, i: (h, jnp.minimum(i, nq - 1), 0))
    done_spec = pl.BlockSpec((None, A_QB, HEAD_DIM), lambda h, i: (h, i, 0))
    win = A_QB + A_PAD

    def body(*refs):
        q_ref = refs[0]
        t_ref, do_ref, o_ref, lse_ref, dq_ref, dk_ref, dv_ref, dt_ref, dk_win, dv_win = refs[1 + 2 * nwin:]
        i = pl.program_id(1)

        @pl.when(i == 0)
        def _():
            dk_win[...] = jnp.zeros_like(dk_win)
            dv_win[...] = jnp.zeros_like(dv_win)
            dt_ref[...] = jnp.zeros_like(dt_ref)

        @pl.when(i < nq)
        def _():
            k = _a_window(refs[1:1 + nwin], i)
            v = _a_window(refs[1 + nwin:1 + 2 * nwin], i)
            dt = jnp.zeros((A_TQ, A_BAND), F32)
            for j in range(A_QB // A_TQ):
                rows, qs, kj, sc = _a_scores(q_ref, k, t_ref, i, j)
                keys = slice(j * A_TQ, j * A_TQ + A_BAND)
                dob = do_ref[rows, :]
                p = jnp.exp(sc - lse_ref[rows, :])
                delta = jnp.sum(o_ref[rows, :] * dob.astype(F32), axis=-1, keepdims=True)
                ds = p * (_dot_nt(dob, v[keys, :]) - delta)
                dsb = ds.astype(BF16)
                dq_ref[rows, :] = _dot(dsb, kj) * 0.125
                dk_win[keys, :] += _dot_tn(dsb, qs)
                dv_win[keys, :] += _dot_tn(p.astype(BF16), dob)
                dt = dt + ds
            dt_ref[...] += dt

        dk_ref[...] = dk_win[0:A_QB, :]
        dv_ref[...] = dv_win[0:A_QB, :]
        dk_rest = dk_win[A_QB:win, :]
        dv_rest = dv_win[A_QB:win, :]
        dk_win[0:A_PAD, :] = dk_rest
        dv_win[0:A_PAD, :] = dv_rest
        dk_win[A_PAD:win, :] = jnp.zeros((A_QB, HEAD_DIM), F32)
        dv_win[A_PAD:win, :] = jnp.zeros((A_QB, HEAD_DIM), F32)

    return pl.pallas_call(
        body, name="mix_a_bwd", grid=(N_HEADS, nq + 1),
        in_specs=[q_spec] + kv_specs + [t_spec, row_spec(HEAD_DIM), row_spec(HEAD_DIM), row_spec(1)],
        out_specs=[row_spec(HEAD_DIM), done_spec, done_spec, t_spec],
        out_shape=[jax.ShapeDtypeStruct((N_HEADS, s, HEAD_DIM), F32),
                   jax.ShapeDtypeStruct((N_HEADS, s + A_QB, HEAD_DIM), F32),
                   jax.ShapeDtypeStruct((N_HEADS, s + A_QB, HEAD_DIM), F32),
                   jax.ShapeDtypeStruct((N_HEADS, A_TQ, A_BAND), F32)],
        scratch_shapes=[pltpu.VMEM((win, HEAD_DIM), F32), pltpu.VMEM((win, HEAD_DIM), F32)],
        compiler_params=_params(("arbitrary", "arbitrary")),
    )(qkv, *([kva] * (2 * nwin)), tbias, do, o, lse)


def relbias_tile(rel_bias, relmat):
    def body(rb_ref, rel_ref, o_ref):
        rel = rel_ref[...]
        o_ref[...] = jnp.full(o_ref.shape, NEG, F32)

        def step(r, carry):
            hit = rel == r
            for h in range(N_HEADS):
                o_ref[h] = jnp.where(hit, rb_ref[h, r], o_ref[h])
            return carry

        lax.fori_loop(REL_LO, REL_HI, step, 0)

    return pl.pallas_call(
        body, name="relbias_tile",
        in_specs=[pl.BlockSpec(memory_space=pltpu.SMEM), pl.BlockSpec(memory_space=pltpu.VMEM)],
        out_specs=pl.BlockSpec(memory_space=pltpu.VMEM),
        out_shape=jax.ShapeDtypeStruct((N_HEADS, A_TQ, A_BAND), F32),
        compiler_params=_params(),
    )(rel_bias, relmat)


def relbias_grad(dt, relmat):
    def body(dt_ref, rel_ref, o_ref):
        rel = rel_ref[...]
        lane = lax.broadcasted_iota(jnp.int32, (8, 384), 1)
        row = lax.broadcasted_iota(jnp.int32, (8, 384), 0)

        def step(r, acc):
            hit = rel == r
            for h in range(N_HEADS):
                val = jnp.sum(jnp.where(hit, dt_ref[h], 0.0))
                acc = jnp.where((lane == r) & (row == h), val, acc)
            return acc

        o_ref[...] = lax.fori_loop(REL_LO, REL_HI, step, jnp.zeros((8, 384), F32))

    return pl.pallas_call(
        body, name="relbias_grad",
        out_shape=jax.ShapeDtypeStruct((8, 384), F32),
        compiler_params=_params(),
    )(dt, relmat)


def _b_norm(v, gain):
    mu = jnp.mean(v, axis=-1, keepdims=True)
    xc = v - mu
    rstd = lax.rsqrt(jnp.mean(xc * xc, axis=-1, keepdims=True) + EPS)
    xhat = xc * rstd
    return xhat, rstd, xhat * gain


def _tril_mask():
    t = lax.broadcasted_iota(jnp.int32, (SG_CHUNK, SG_CHUNK), 0)
    u = lax.broadcasted_iota(jnp.int32, (SG_CHUNK, SG_CHUNK), 1)
    return u <= t


def mix_b_fwd(uv, gain, w_s, b_col):
    s = uv.shape[0]
    tm = min(ROW_T, s)

    def body(uv_ref, gain_ref, w_ref, b_ref, y_ref):
        tril = _tril_mask()
        ws = [jnp.where(tril, w_ref[g], 0.0).astype(BF16) for g in range(N_HEADS)]
        for c in range(tm // SG_CHUNK):
            rows = slice(c * SG_CHUNK, (c + 1) * SG_CHUNK)
            u = uv_ref[rows, 0:D_BRANCH]
            _, _, vn = _b_norm(uv_ref[rows, D_BRANCH:2 * D_BRANCH], gain_ref[...])
            vnb = vn.astype(BF16)
            outs = []
            for g in range(N_HEADS):
                cols = slice(g * HEAD_DIM, (g + 1) * HEAD_DIM)
                mixed = _dot(ws[g], vnb[:, cols]) + b_ref[g]
                outs.append(u[:, cols] * mixed)
            y_ref[rows, :] = jnp.concatenate(outs, axis=1)

    return pl.pallas_call(
        body, name="mix_b_fwd", grid=(s // tm,),
        in_specs=[pl.BlockSpec((tm, 2 * D_BRANCH), lambda i: (i, 0)),
                  pl.BlockSpec((1, D_BRANCH), lambda i: (0, 0)),
                  pl.BlockSpec((N_HEADS, SG_CHUNK, SG_CHUNK), lambda i: (0, 0, 0)),
                  pl.BlockSpec((N_HEADS, SG_CHUNK, 1), lambda i: (0, 0, 0))],
        out_specs=pl.BlockSpec((tm, D_BRANCH), lambda i: (i, 0)),
        out_shape=jax.ShapeDtypeStruct((s, D_BRANCH), F32),
        compiler_params=_params(("arbitrary",)),
    )(uv, gain, w_s, b_col)


def mix_b_bwd(uv, gain, w_s, b_col, dy):
    s = uv.shape[0]
    tm = min(ROW_T, s)

    def body(uv_ref, gain_ref, w_ref, b_ref, dy_ref, duv_ref, dw_ref, db_ref, dgain_ref):
        i = pl.program_id(0)

        @pl.when(i == 0)
        def _():
            dw_ref[...] = jnp.zeros_like(dw_ref)
            db_ref[...] = jnp.zeros_like(db_ref)
            dgain_ref[...] = jnp.zeros_like(dgain_ref)

        tril = _tril_mask()
        ws = [jnp.where(tril, w_ref[g], 0.0).astype(BF16) for g in range(N_HEADS)]
        gain_v = gain_ref[...]
        for c in range(tm // SG_CHUNK):
            rows = slice(c * SG_CHUNK, (c + 1) * SG_CHUNK)
            u = uv_ref[rows, 0:D_BRANCH]
            xhat, rstd, vn = _b_norm(uv_ref[rows, D_BRANCH:2 * D_BRANCH], gain_v)
            vnb = vn.astype(BF16)
            dyv = dy_ref[rows, :]
            dus, dvns = [], []
            for g in range(N_HEADS):
                cols = slice(g * HEAD_DIM, (g + 1) * HEAD_DIM)
                mixed = _dot(ws[g], vnb[:, cols]) + b_ref[g]
                dus.append(dyv[:, cols] * mixed)
                dmixed = dyv[:, cols] * u[:, cols]
                dmb = dmixed.astype(BF16)
                db_ref[g] += jnp.sum(dmixed, axis=-1, keepdims=True)
                dw_ref[g] += jnp.where(tril, _dot_nt(dmb, vnb[:, cols]), 0.0)
                dvns.append(_dot_tn(ws[g], dmb))
            dvn = jnp.concatenate(dvns, axis=1)
            dgain_ref[...] += jnp.sum(dvn * xhat, axis=0, keepdims=True)
            dxh = dvn * gain_v
            dv = rstd * (dxh - jnp.mean(dxh, axis=-1, keepdims=True)
                         - xhat * jnp.mean(dxh * xhat, axis=-1, keepdims=True))
            duv_ref[rows, :] = jnp.concatenate(dus + [dv], axis=1)

    return pl.pallas_call(
        body, name="mix_b_bwd", grid=(s // tm,),
        in_specs=[pl.BlockSpec((tm, 2 * D_BRANCH), lambda i: (i, 0)),
                  pl.BlockSpec((1, D_BRANCH), lambda i: (0, 0)),
                  pl.BlockSpec((N_HEADS, SG_CHUNK, SG_CHUNK), lambda i: (0, 0, 0)),
                  pl.BlockSpec((N_HEADS, SG_CHUNK, 1), lambda i: (0, 0, 0)),
                  pl.BlockSpec((tm, D_BRANCH), lambda i: (i, 0))],
        out_specs=[pl.BlockSpec((tm, 2 * D_BRANCH), lambda i: (i, 0)),
                   pl.BlockSpec((N_HEADS, SG_CHUNK, SG_CHUNK), lambda i: (0, 0, 0)),
                   pl.BlockSpec((N_HEADS, SG_CHUNK, 1), lambda i: (0, 0, 0)),
                   pl.BlockSpec((1, D_BRANCH), lambda i: (0, 0))],
        out_shape=[jax.ShapeDtypeStruct((s, 2 * D_BRANCH), F32),
                   jax.ShapeDtypeStruct((N_HEADS, SG_CHUNK, SG_CHUNK), F32),
                   jax.ShapeDtypeStruct((N_HEADS, SG_CHUNK, 1), F32),
                   jax.ShapeDtypeStruct((1, D_BRANCH), F32)],
        compiler_params=_params(("arbitrary",)),
    )(uv, gain, w_s, b_col, dy)


def _scan_mats(nrow):
    a = lax.broadcasted_iota(jnp.int32, (128, 128), 0)
    b = lax.broadcasted_iota(jnp.int32, (128, 128), 1)
    r = lax.broadcasted_iota(jnp.int32, (nrow, nrow), 0)
    c = lax.broadcasted_iota(jnp.int32, (nrow, nrow), 1)
    nb = nrow // N_HEADS
    same = (r // nb) == (c // nb)
    return a, b, r, c, same


def _exact_dot(x, m):
    hi, mid, lo = _split3(x)
    return _dot(hi, m) + _dot(mid, m) + _dot(lo, m)


def _exact_dot_left(m, x):
    hi, mid, lo = _split3(x)
    return _dot(m, hi) + _dot(m, mid) + _dot(m, lo)


def fox_gate_fwd(ft, bcol):
    nrow = ft.shape[0]

    def body(f_ref, b_ref, c_ref):
        z = f_ref[...] + b_ref[...]
        ls = jnp.minimum(z, 0.0) - jnp.log(1.0 + jnp.exp(-jnp.abs(z)))
        a, b, r, c, same = _scan_mats(nrow)
        within = _exact_dot(ls, (a <= b).astype(BF16))
        tot = jnp.broadcast_to(within[:, 127:128], within.shape)
        before = _exact_dot_left((same & (c < r)).astype(BF16), tot)
        c_ref[...] = within + before

    return pl.pallas_call(
        body, name="fox_gate_fwd",
        out_shape=jax.ShapeDtypeStruct((nrow, 128), F32),
        compiler_params=_params(),
    )(ft, bcol)


def fox_gate_bwd(ft, bcol, dc):
    nrow = ft.shape[0]

    def body(f_ref, b_ref, dc_ref, df_ref, db_ref):
        a, b, r, c, same = _scan_mats(nrow)
        dcv = dc_ref[...]
        within = _exact_dot(dcv, (a >= b).astype(BF16))
        tot = jnp.broadcast_to(within[:, 0:1], within.shape)
        after = _exact_dot_left((same & (c > r)).astype(BF16), tot)
        dls = within + after
        z = f_ref[...] + b_ref[...]
        dz = dls * _sigmoid(-z)
        df_ref[...] = dz
        rs = jnp.broadcast_to(jnp.sum(dz, axis=-1, keepdims=True), dz.shape)
        hr = lax.broadcasted_iota(jnp.int32, (8, nrow), 0)
        hc = lax.broadcasted_iota(jnp.int32, (8, nrow), 1)
        db_ref[...] = _exact_dot_left((hr == hc // (nrow // N_HEADS)).astype(BF16), rs)

    return pl.pallas_call(
        body, name="fox_gate_bwd",
        out_shape=[jax.ShapeDtypeStruct((nrow, 128), F32), jax.ShapeDtypeStruct((8, 128), F32)],
        compiler_params=_params(),
    )(ft, bcol, dc)


def _att_specs(s, qi, ki, vi):
    q_spec = pl.BlockSpec((None, None, FOX_TQ, HEAD_DIM), lambda h, i: (qi, h, i, 0))
    k_spec = pl.BlockSpec((None, None, s, HEAD_DIM), lambda h, i: (ki, h, 0, 0))
    v_spec = pl.BlockSpec((None, None, s, HEAD_DIM), lambda h, i: (vi, h, 0, 0))
    row_spec = lambda w: pl.BlockSpec((None, FOX_TQ, w), lambda h, i: (h, i, 0))
    gate_spec = pl.BlockSpec((None, s // ATT_T, 1, ATT_T), lambda h, i: (h, 0, 0, 0))
    return q_spec, k_spec, v_spec, row_spec, gate_spec


def _causal(n):
    row = lax.broadcasted_iota(jnp.int32, (n, n), 0)
    col = lax.broadcasted_iota(jnp.int32, (n, n), 1)
    return col <= row


def _gate_row(cr_ref, kb, g):
    if g == 1:
        return cr_ref[kb]
    return jnp.concatenate([cr_ref[kb + n] for n in range(g)], axis=1)


def _fox_walk(i, carry, tile, alive):
    g = FOX_WIDE
    own = FOX_TQ // ATT_T
    nwide = (own * i) // g
    carry = tile(own * i, own, carry, True)
    carry = lax.fori_loop(0, (own * i - nwide * g) // own, lambda n, c: tile(nwide * g, own, c, False), carry)

    def cond(state):
        return jnp.logical_and(state[0] >= 0, state[1] > 0)

    def step(state):
        n = state[0]
        c = tile(n * g, g, state[2:], False)
        return (n - 1, alive(n * g, c)) + tuple(c)

    out = lax.while_loop(cond, step, (nwide - 1, alive(nwide * g, carry)) + tuple(carry))
    return out[2:]


def _fox_reach(qs, k_ref, kmax_ref, cc, i):
    s = k_ref.shape[0]
    rows = 4 * ATT_T

    @pl.when(i == 0)
    def _():
        def chunk(n, mx):
            kc = k_ref[pl.ds(pl.multiple_of(n * rows, rows), rows), :].astype(F32)
            return jnp.maximum(mx, jnp.max(jnp.sum(kc * kc, axis=-1, keepdims=True)))

        kmax_ref[0] = jnp.sqrt(lax.fori_loop(0, s // rows, chunk, jnp.float32(0.0)))

    qf = qs.astype(F32)
    return jnp.sqrt(jnp.sum(qf * qf, axis=-1, keepdims=True)) * kmax_ref[0] + cc


def _gate_col(cr_ref, i):
    row = lax.broadcasted_iota(jnp.int32, (ATT_T, ATT_T), 0)
    col = lax.broadcasted_iota(jnp.int32, (ATT_T, ATT_T), 1)
    own = FOX_TQ // ATT_T
    return jnp.concatenate([jnp.sum(jnp.where(row == col, cr_ref[own * i + n], 0.0), axis=-1, keepdims=True)
                            for n in range(own)], axis=0)


def _fox_scores(qs, k, cc, crow, masked):
    sc = (_dot_nt(qs, k) + (cc - crow)) * LOG2E
    if masked:
        sc = jnp.where(_causal(FOX_TQ), sc, NEG)
    return sc


def fox_fwd(qkv, c_row, ride=()):
    s = qkv.shape[2]
    t = ATT_T
    nq = s // FOX_TQ
    q_spec, k_spec, v_spec, row_spec, gate_spec = _att_specs(s, 1, 2, 3)
    rows = 4 * t
    nride = len(ride)

    def body(q_ref, k_ref, v_ref, cr_ref, *refs):
        ride_in, refs = refs[:nride], refs[nride:]
        o_ref, ref_ref, rl_ref = refs[:3]
        ride_out, refs = refs[3:3 + nride], refs[3 + nride:]
        v1_ref, kmax_ref = refs[:2]
        i = pl.program_id(1)
        if nride:
            h = pl.program_id(0)
            start, wait = _chip_gather([(src, lambda slot, dst=dst: dst.at[slot]) for src, dst in zip(ride_in, ride_out)],
                                       *refs[2:])
            pl.when(jnp.logical_and(h == 0, i == 0))(start)

        @pl.when(i == 0)
        def _():
            def chunk(n, carry):
                r0 = pl.multiple_of(n * rows, rows)
                v1_ref[pl.ds(r0, rows), :] = jnp.concatenate(
                    [v_ref[pl.ds(r0, rows), :], jnp.ones((rows, HEAD_DIM), BF16)], axis=1)
                return carry

            lax.fori_loop(0, s // rows, chunk, 0)

        qs = q_ref[...] * 0.125
        cc = _gate_col(cr_ref, i)
        reach = _fox_reach(qs, k_ref, kmax_ref, cc, i) * LOG2E

        def alive(kb, carry):
            return (jnp.max(reach - cr_ref[kb][:, 0:1] * LOG2E - carry[0]) > FOX_DEAD2).astype(jnp.int32)

        def tile(kb, g, carry, masked):
            m, acc = carry
            k0 = pl.multiple_of(kb * t, t)
            sc = _fox_scores(qs, k_ref[pl.ds(k0, g * t), :], cc, _gate_row(cr_ref, kb, g), masked)
            m_new = jnp.maximum(m, jnp.ceil(jnp.max(sc, axis=-1, keepdims=True)))
            pb = jnp.exp2(sc - m_new).astype(BF16)
            acc = jnp.exp2(m - m_new) * acc + _dot(pb, v1_ref[pl.ds(k0, g * t), :])
            return m_new, acc

        init = (jnp.full((FOX_TQ, 1), NEG, F32), jnp.zeros((FOX_TQ, 2 * HEAD_DIM), F32))
        m, acc = _fox_walk(i, init, tile, alive)
        rl = 1.0 / acc[:, HEAD_DIM:HEAD_DIM + 1]
        o_ref[...] = acc[:, 0:HEAD_DIM] * rl
        ref_ref[...] = m
        rl_ref[...] = rl
        if nride:
            pl.when(jnp.logical_and(h == N_HEADS - 1, i == nq - 1))(wait)

    any_spec = pl.BlockSpec(memory_space=pl.ANY)
    ride_sems = [pltpu.SemaphoreType.DMA((3 * nride,)), pltpu.SemaphoreType.DMA((3 * nride,)),
                 pltpu.SemaphoreType.DMA((nride,))] if nride else []
    return pl.pallas_call(
        body, name="fox_fwd_gather" if nride else "fox_fwd", grid=(N_HEADS, nq),
        in_specs=[q_spec, k_spec, v_spec, gate_spec] + [any_spec] * nride,
        out_specs=[row_spec(HEAD_DIM), row_spec(1), row_spec(1)] + [any_spec] * nride,
        out_shape=[jax.ShapeDtypeStruct((N_HEADS, s, HEAD_DIM), F32),
                   jax.ShapeDtypeStruct((N_HEADS, s, 1), F32),
                   jax.ShapeDtypeStruct((N_HEADS, s, 1), F32)]
        + [jax.ShapeDtypeStruct((4,) + a.shape, a.dtype) for a in ride],
        scratch_shapes=[pltpu.VMEM((s, 2 * HEAD_DIM), BF16), pltpu.SMEM((1,), F32)] + ride_sems,
        compiler_params=_params(("arbitrary", "arbitrary")),
    )(qkv, qkv, qkv, c_row, *ride)


def fox_bwd(qkv, c_row, do, o, ref, rl, ride=()):
    s = qkv.shape[2]
    t = ATT_T
    nq = s // FOX_TQ
    q_spec, k_spec, v_spec, row_spec, gate_spec = _att_specs(s, 1, 2, 3)
    any_spec = pl.BlockSpec(memory_space=pl.ANY)
    nride = len(ride)

    def body(q_ref, k_ref, v_ref, cr_ref, do_ref, o_ref, ref_ref, rl_ref, *refs):
        ride_in, refs = refs[:nride], refs[nride:]
        dq_ref, dk_hbm, dv_hbm, dc_ref = refs[:4]
        ride_out, refs = refs[4:4 + nride], refs[4 + nride:]
        dk_acc, dv_acc, kmax_ref = refs[:3]
        h = pl.program_id(0)
        i = pl.program_id(1)
        if nride:
            start, wait = _device_exchange(_exchange_flows(ride_in, ride_out), *refs[3:])
            pl.when(jnp.logical_and(h == 0, i == 0))(start)

        @pl.when(i == 0)
        def _():
            dk_acc[...] = jnp.zeros_like(dk_acc)
            dv_acc[...] = jnp.zeros_like(dv_acc)
            dc_ref[...] = jnp.zeros_like(dc_ref)

        qs = q_ref[...] * 0.125
        ref = ref_ref[...]
        rl = rl_ref[...]
        dob = (do_ref[...].astype(F32) * rl).astype(BF16)
        delta = jnp.sum(o_ref[...] * dob.astype(F32), axis=-1, keepdims=True)
        cc = _gate_col(cr_ref, i)
        margin = _fox_reach(qs, k_ref, kmax_ref, cc, i) * LOG2E - ref

        def alive(kb, carry):
            return (jnp.max(margin - cr_ref[kb][:, 0:1] * LOG2E) > FOX_DEAD2).astype(jnp.int32)

        def tile(kb, g, carry, masked):
            dq, = carry
            k0 = pl.multiple_of(kb * t, t)
            k = k_ref[pl.ds(k0, g * t), :]
            sc = _fox_scores(qs, k, cc, _gate_row(cr_ref, kb, g), masked)
            wb = jnp.exp2(sc - ref).astype(BF16)
            ds = wb.astype(F32) * (_dot_nt(dob, v_ref[pl.ds(k0, g * t), :]) - delta)
            dsb = ds.astype(BF16)
            dk_acc[pl.ds(k0, g * t), :] += _dot_tn(dsb, qs)
            dv_acc[pl.ds(k0, g * t), :] += _dot_tn(wb, dob)
            dcs = -jnp.sum(ds, axis=0, keepdims=True)
            for n in range(g):
                dc_ref[kb + n] += dcs[:, n * t:(n + 1) * t]
            return (dq + _dot(dsb, k),)

        dq, = _fox_walk(i, (jnp.zeros((FOX_TQ, HEAD_DIM), F32),), tile, alive)
        dq_ref[...] = dq * 0.125

        @pl.when(i == nq - 1)
        def _():
            pltpu.sync_copy(dk_acc, dk_hbm.at[h])
            pltpu.sync_copy(dv_acc, dv_hbm.at[h])

        if nride:
            pl.when(jnp.logical_and(h == N_HEADS - 1, i == nq - 1))(wait)

    return pl.pallas_call(
        body, name="fox_bwd_exchange" if nride else "fox_bwd", grid=(N_HEADS, nq),
        in_specs=[q_spec, k_spec, v_spec,
                  gate_spec,
                  row_spec(HEAD_DIM), row_spec(HEAD_DIM), row_spec(1), row_spec(1)] + [any_spec] * nride,
        out_specs=[row_spec(HEAD_DIM), any_spec, any_spec,
                   gate_spec] + [any_spec] * nride,
        out_shape=[jax.ShapeDtypeStruct((N_HEADS, s, HEAD_DIM), F32),
                   jax.ShapeDtypeStruct((N_HEADS, s, HEAD_DIM), F32),
                   jax.ShapeDtypeStruct((N_HEADS, s, HEAD_DIM), F32),
                   jax.ShapeDtypeStruct((N_HEADS, s // t, 1, t), F32)] + _exchange_shapes(ride),
        scratch_shapes=[pltpu.VMEM((s, HEAD_DIM), F32), pltpu.VMEM((s, HEAD_DIM), F32), pltpu.SMEM((1,), F32)]
        + (_exchange_sems(nride) if nride else []),
        compiler_params=_params(("arbitrary", "arbitrary")),
    )(qkv, qkv, qkv, c_row, do, o, ref, rl, *ride)


def _sb_valid(nrows, ahead):
    row = lax.broadcasted_iota(jnp.int32, (nrows, ATT_T), 0)
    col = lax.broadcasted_iota(jnp.int32, (nrows, ATT_T), 1)
    return col + ahead < row


def _sb_band_valid(nsub, i):
    shape = (nsub * SB_SUB, SB_BAND)
    row = lax.broadcasted_iota(jnp.int32, shape, 0)
    col = lax.broadcasted_iota(jnp.int32, shape, 1)
    first = i * SB_TQ + (row - (row & (SB_SUB - 1)))
    valid = col < (row & (SB_SUB - 1)) + jnp.minimum(first, SB_BACK)
    return valid, first[:, 0:1] > SB_BACK


def _sb_logits(qs, k):
    z = _dot_nt(qs, k)
    sp = jnp.log(1.0 + jnp.exp(-jnp.abs(z)))
    return jnp.minimum(z, 0.0) - sp, -jnp.maximum(z, 0.0) - sp


def _sb_weights(ls, lm, run, valid):
    if valid is not None:
        lm = jnp.where(valid, lm, 0.0)
    n = lm.shape[1]
    row = lax.broadcasted_iota(jnp.int32, (n, n), 0)
    col = lax.broadcasted_iota(jnp.int32, (n, n), 1)
    later = (row > col).astype(BF16)
    hi, lo = _split2(lm)
    between = _dot(hi, later) + _dot(lo, later)
    if run is not None:
        between = run + between
    a = jnp.exp(ls + between)
    if valid is not None:
        a = jnp.where(valid, a, 0.0)
    return lm, a


def _sb_band_start(i, j):
    return pl.multiple_of(jnp.maximum(i * SB_TQ + j * SB_SUB - SB_BACK, 0), SB_SUB)


def _sb_tile(qs, k, run, valid):
    ls, lm = _sb_logits(qs, k)
    lm, a = _sb_weights(ls, lm, run, valid)
    return ls, lm, a


def _sb_band(i, qs_all, k_ref):
    nsub = qs_all.shape[0] // SB_SUB
    valid, open_left = _sb_band_valid(nsub, i)
    starts = [_sb_band_start(i, j) for j in range(nsub)]
    kwins = [k_ref[pl.ds(k0, SB_BAND), :] for k0 in starts]
    parts = [_sb_logits(qs_all[j * SB_SUB:(j + 1) * SB_SUB], kwins[j]) for j in range(nsub)]
    ls = jnp.concatenate([p[0] for p in parts], axis=0)
    lm, a = _sb_weights(ls, jnp.concatenate([p[1] for p in parts], axis=0), None, valid)
    return starts, kwins, ls, lm, a, valid, open_left


def _sb_suffix(g, run_g):
    n = g.shape[1]
    row = lax.broadcasted_iota(jnp.int32, (n, n), 0)
    col = lax.broadcasted_iota(jnp.int32, (n, n), 1)
    from_here = (row >= col).astype(BF16)
    hi, lo = _split2(g)
    out = _dot(hi, from_here) + _dot(lo, from_here)
    return out if run_g is None else run_g + out


def _sb_walk(i, carry, tile):
    def alive_of(c):
        return (jnp.max(c[0]) > SB_DEAD).astype(jnp.int32)

    def cond(state):
        n, alive = state[0], state[1]
        return jnp.logical_and(n < i, alive > 0)

    def step(state):
        n = state[0]
        c = tile(i - 1 - n, state[2:], False)
        return (n + 1, alive_of(c)) + tuple(c)

    out = lax.while_loop(cond, step, (jnp.int32(0), alive_of(carry)) + tuple(carry))
    return out[2:]


def _sb_specs(s):
    tq = SB_TQ
    q_spec = pl.BlockSpec((None, None, tq, HEAD_DIM), lambda h, i: (4, h, i, 0))
    k_spec = pl.BlockSpec((None, None, s, HEAD_DIM), lambda h, i: (5, h, 0, 0))
    v_spec = pl.BlockSpec((None, None, s, HEAD_DIM), lambda h, i: (6, h, 0, 0))
    row_spec = pl.BlockSpec((None, tq, HEAD_DIM), lambda h, i: (h, i, 0))
    band_spec = pl.BlockSpec((None, None, 1, 128), lambda h, i: (h, i, 0, 0))
    return tq, q_spec, k_spec, v_spec, row_spec, band_spec


def _sb_block(b, row0, tile, zero):
    t = ATT_T
    lo, hi, both = slice(row0, row0 + t), slice(row0 + t, row0 + 2 * t), slice(row0, row0 + 2 * t)
    c_hi = tile(2 * b + 1, hi, zero, 0)
    c_lo = tile(2 * b, lo, zero, 0)
    c_hi = tile(2 * b, hi, c_hi, None)
    carry = tuple(jnp.concatenate([x, y], axis=0) for x, y in zip(c_lo, c_hi))
    return _sb_walk(2 * b, carry, lambda kb, c, _: tile(kb, both, c, None))


def sb_fwd(qkv):
    s = qkv.shape[2]
    t = ATT_T
    tq, q_spec, k_spec, v_spec, row_spec, band_spec = _sb_specs(s)

    def body(q_ref, k_ref, v_ref, o_ref, band_ref, done_ref):
        i = pl.program_id(1)
        qs = q_ref[...] * 0.125
        starts, _, _, lm, a, _, open_left = _sb_band(i, qs, k_ref)
        ab = a.astype(BF16)
        for j, k0 in enumerate(starts):
            rows = slice(j * SB_SUB, (j + 1) * SB_SUB)
            o_ref[rows, :] = _dot(ab[rows], v_ref[pl.ds(k0, SB_BAND), :])
        worst = jnp.max(jnp.where(open_left, jnp.sum(lm, axis=-1, keepdims=True), NEG))
        done_ref[0] = (worst <= SB_DEAD).astype(jnp.int32)

        @pl.when(done_ref[0] == 0)
        def _():
            def tile(kb, rows, carry, ahead):
                run, acc = carry
                k0 = pl.multiple_of(kb * t, t)
                valid = None if ahead is None else _sb_valid(t, ahead)
                _, lm, a = _sb_tile(qs[rows], k_ref[pl.ds(k0, t), :], run, valid)
                acc = acc + _dot(a.astype(BF16), v_ref[pl.ds(k0, t), :])
                return run + jnp.sum(lm, axis=-1, keepdims=True), acc

            for n in range(tq // (2 * t)):
                _, acc = _sb_block(i * (tq // (2 * t)) + n, n * 2 * t, tile,
                                   (jnp.zeros((t, 1), F32), jnp.zeros((t, HEAD_DIM), F32)))
                o_ref[n * 2 * t:(n + 1) * 2 * t, :] = acc

        band_ref[...] = jnp.full(band_ref.shape, done_ref[0], jnp.int32).astype(F32)

    return pl.pallas_call(
        body, name="sb_fwd", grid=(N_HEADS, s // tq),
        in_specs=[q_spec, k_spec, v_spec],
        out_specs=[row_spec, band_spec],
        out_shape=[jax.ShapeDtypeStruct((N_HEADS, s, HEAD_DIM), F32),
                   jax.ShapeDtypeStruct((N_HEADS, s // tq, 1, 128), F32)],
        scratch_shapes=[pltpu.SMEM((1,), jnp.int32)],
        compiler_params=_params(("arbitrary", "arbitrary")),
    )(qkv, qkv, qkv)


def sb_bwd(qkv, do, o, band, ride=()):
    s = qkv.shape[2]
    t = ATT_T
    tq, q_spec, k_spec, v_spec, row_spec, band_spec = _sb_specs(s)
    nq = s // tq
    any_spec = pl.BlockSpec(memory_space=pl.ANY)
    nride = len(ride)

    def body(q_ref, k_ref, v_ref, do_ref, o_ref, band_ref, *refs):
        ride_in, refs = refs[:nride], refs[nride:]
        dq_ref, dk_hbm, dv_hbm = refs[:3]
        ride_out, refs = refs[3:3 + nride], refs[3 + nride:]
        dk_acc, dv_acc = refs[:2]
        h = pl.program_id(0)
        i = pl.program_id(1)
        if nride:
            start, wait = _device_exchange(_exchange_flows(ride_in, ride_out), *refs[2:])
            pl.when(jnp.logical_and(h == 0, i == 0))(start)

        @pl.when(i == 0)
        def _():
            dk_acc[...] = jnp.zeros_like(dk_acc)
            dv_acc[...] = jnp.zeros_like(dv_acc)

        qs_all = q_ref[...] * 0.125
        dob_all = do_ref[...]
        tot_all = jnp.sum(o_ref[...] * dob_all.astype(F32), axis=-1, keepdims=True)
        on_band = jnp.max(band_ref[...]) > 0.5

        def grads(qs, dob, tot, k, v, k0, run, run_g, valid):
            ls, lm, a = _sb_tile(qs, k, run, valid)
            ab = a.astype(BF16)
            g = ab.astype(F32) * _dot_nt(dob, v)
            g_left = tot - _sb_suffix(g, run_g)
            dz = g - jnp.exp(ls) * (g + g_left)
            if valid is not None:
                dz = jnp.where(valid, dz, 0.0)
            dzb = dz.astype(BF16)
            n = k.shape[0]
            dk_acc[pl.ds(k0, n), :] += _dot_tn(dzb, qs)
            dv_acc[pl.ds(k0, n), :] += _dot_tn(ab, dob)
            return dzb, lm, g

        @pl.when(on_band)
        def _():
            starts, kwins, ls, _, a, valid, _ = _sb_band(i, qs_all, k_ref)
            ab = a.astype(BF16)
            subs = [slice(j * SB_SUB, (j + 1) * SB_SUB) for j in range(len(starts))]
            vwins = [v_ref[pl.ds(k0, SB_BAND), :] for k0 in starts]
            g = ab.astype(F32) * jnp.concatenate([_dot_nt(dob_all[r], v) for r, v in zip(subs, vwins)], axis=0)
            dz = jnp.where(valid, g - jnp.exp(ls) * (g + (tot_all - _sb_suffix(g, None))), 0.0)
            dzb = dz.astype(BF16)
            for r, k0, k in zip(subs, starts, kwins):
                dq_ref[r, :] = _dot(dzb[r], k) * 0.125
                dk_acc[pl.ds(k0, SB_BAND), :] += _dot_tn(dzb[r], qs_all[r])
                dv_acc[pl.ds(k0, SB_BAND), :] += _dot_tn(ab[r], dob_all[r])

        @pl.when(jnp.logical_not(on_band))
        def _():
            def tile(kb, rows, carry, ahead):
                run, run_g, dq = carry
                k0 = pl.multiple_of(kb * t, t)
                k = k_ref[pl.ds(k0, t), :]
                valid = None if ahead is None else _sb_valid(t, ahead)
                dzb, lm, g = grads(qs_all[rows], dob_all[rows], tot_all[rows], k, v_ref[pl.ds(k0, t), :], k0,
                                   run, run_g, valid)
                return (run + jnp.sum(lm, axis=-1, keepdims=True),
                        run_g + jnp.sum(g, axis=-1, keepdims=True),
                        dq + _dot(dzb, k))

            zero = jnp.zeros((t, 1), F32)
            for n in range(tq // (2 * t)):
                _, _, dq = _sb_block(i * (tq // (2 * t)) + n, n * 2 * t, tile, (zero, zero, jnp.zeros((t, HEAD_DIM), F32)))
                dq_ref[n * 2 * t:(n + 1) * 2 * t, :] = dq * 0.125

        @pl.when(i == nq - 1)
        def _():
            pltpu.sync_copy(dk_acc, dk_hbm.at[h])
            pltpu.sync_copy(dv_acc, dv_hbm.at[h])

        if nride:
            pl.when(jnp.logical_and(h == N_HEADS - 1, i == nq - 1))(wait)

    return pl.pallas_call(
        body, name="sb_bwd_exchange" if nride else "sb_bwd", grid=(N_HEADS, nq),
        in_specs=[q_spec, k_spec, v_spec, row_spec, row_spec, band_spec] + [any_spec] * nride,
        out_specs=[row_spec, any_spec, any_spec] + [any_spec] * nride,
        out_shape=[jax.ShapeDtypeStruct((N_HEADS, s, HEAD_DIM), F32)] * 3 + _exchange_shapes(ride),
        scratch_shapes=[pltpu.VMEM((s, HEAD_DIM), F32), pltpu.VMEM((s, HEAD_DIM), F32)]
        + (_exchange_sems(nride) if nride else []),
        compiler_params=_params(("arbitrary", "arbitrary")),
    )(qkv, qkv, qkv, do, o, band, *ride)


def _branch_inputs(refs, br):
    ya_ref, yb_ref, yc_ref, yd_ref = refs
    if br == 1:
        return yb_ref[...]
    return _heads_to_lanes((ya_ref, None, yc_ref, yd_ref)[br])


def outproj_fwd(x, ya, yb, yc, yd, gates, bg, wout):
    s = x.shape[0]
    tm = min(ROW_T, s)

    def body(x_ref, ya_ref, yb_ref, yc_ref, yd_ref, gates_ref, bg_ref, w_ref, out_ref):
        pieces = []
        for br in range(4):
            cols = slice(br * D_BRANCH, (br + 1) * D_BRANCH)
            y = _branch_inputs((ya_ref, yb_ref, yc_ref, yd_ref), br)
            r = lax.rsqrt(jnp.mean(y * y, axis=-1, keepdims=True) + EPS)
            gt = gates_ref[:, cols]
            pieces.append((y * r * bg_ref[:, cols]) * (gt * _sigmoid(gt)))
        merged = jnp.concatenate(pieces, axis=1).astype(BF16)
        out_ref[...] = x_ref[...] + _dot(merged, w_ref[...])

    head_spec = pl.BlockSpec((N_HEADS, tm, HEAD_DIM), lambda i: (0, i, 0))
    return pl.pallas_call(
        body, name="outproj_fwd", grid=(s // tm,),
        in_specs=[pl.BlockSpec((tm, D_MODEL), lambda i: (i, 0)),
                  head_spec, pl.BlockSpec((tm, D_BRANCH), lambda i: (i, 0)), head_spec, head_spec,
                  pl.BlockSpec((tm, D_MODEL), lambda i: (i, 0)),
                  pl.BlockSpec((1, D_MODEL), lambda i: (0, 0)),
                  pl.BlockSpec((D_MODEL, D_MODEL), lambda i: (0, 0))],
        out_specs=pl.BlockSpec((tm, D_MODEL), lambda i: (i, 0)),
        out_shape=jax.ShapeDtypeStruct((s, D_MODEL), F32),
        compiler_params=_params(("arbitrary",)),
    )(x, ya, yb, yc, yd, gates, bg, wout)


def outproj_bwd(dout, ya, yb, yc, yd, gates, bg, wout):
    s = dout.shape[0]
    tm = min(ROW_T, s)

    def body(dout_ref, ya_ref, yb_ref, yc_ref, yd_ref, gates_ref, bg_ref, w_ref,
             dya_ref, dyb_ref, dyc_ref, dyd_ref, dgates_ref, dbg_ref, dw_ref):
        i = pl.program_id(0)

        @pl.when(i == 0)
        def _():
            dbg_ref[...] = jnp.zeros_like(dbg_ref)
            dw_ref[...] = jnp.zeros_like(dw_ref)

        doutb = dout_ref[...].astype(BF16)
        dmerged = _dot_nt(doutb, w_ref[...])
        pieces = []
        for br in range(4):
            cols = slice(br * D_BRANCH, (br + 1) * D_BRANCH)
            y = _branch_inputs((ya_ref, yb_ref, yc_ref, yd_ref), br)
            r = lax.rsqrt(jnp.mean(y * y, axis=-1, keepdims=True) + EPS)
            yn = y * r
            bgv = bg_ref[:, cols]
            gt = gates_ref[:, cols]
            sig = _sigmoid(gt)
            act = gt * sig
            n = yn * bgv
            pieces.append(n * act)
            dm = dmerged[:, cols]
            dn = dm * act
            dgates_ref[:, cols] = (dm * n * (sig * (1.0 + gt * (1.0 - sig)))).astype(BF16)
            dbg_ref[:, cols] += jnp.sum(dn * yn, axis=0, keepdims=True)
            u = dn * bgv
            dy = r * (u - yn * jnp.mean(yn * u, axis=-1, keepdims=True))
            if br == 1:
                dyb_ref[...] = dy
            else:
                dref = (dya_ref, None, dyc_ref, dyd_ref)[br]
                for hh in range(N_HEADS):
                    dref[hh] = dy[:, hh * HEAD_DIM:(hh + 1) * HEAD_DIM].astype(BF16)
        merged = jnp.concatenate(pieces, axis=1).astype(BF16)
        dw_ref[...] += _dot_tn(merged, doutb)

    head_spec = pl.BlockSpec((N_HEADS, tm, HEAD_DIM), lambda i: (0, i, 0))
    head_shape = jax.ShapeDtypeStruct((N_HEADS, s, HEAD_DIM), BF16)
    return pl.pallas_call(
        body, name="outproj_bwd", grid=(s // tm,),
        in_specs=[pl.BlockSpec((tm, D_MODEL), lambda i: (i, 0)),
                  head_spec, pl.BlockSpec((tm, D_BRANCH), lambda i: (i, 0)), head_spec, head_spec,
                  pl.BlockSpec((tm, D_MODEL), lambda i: (i, 0)),
                  pl.BlockSpec((1, D_MODEL), lambda i: (0, 0)),
                  pl.BlockSpec((D_MODEL, D_MODEL), lambda i: (0, 0))],
        out_specs=[head_spec, pl.BlockSpec((tm, D_BRANCH), lambda i: (i, 0)), head_spec, head_spec,
                   pl.BlockSpec((tm, D_MODEL), lambda i: (i, 0)),
                   pl.BlockSpec((1, D_MODEL), lambda i: (0, 0)),
                   pl.BlockSpec((D_MODEL, D_MODEL), lambda i: (0, 0))],
        out_shape=[head_shape, jax.ShapeDtypeStruct((s, D_BRANCH), F32), head_shape, head_shape,
                   jax.ShapeDtypeStruct((s, D_MODEL), BF16),
                   jax.ShapeDtypeStruct((1, D_MODEL), F32),
                   jax.ShapeDtypeStruct((D_MODEL, D_MODEL), F32)],
        compiler_params=_params(("arbitrary",)),
    )(dout, ya, yb, yc, yd, gates, bg, wout)


def final_loss(x, tgt, g):
    s = x.shape[0]
    tm = min(ROW_T, s)

    def body(x_ref, t_ref, g_ref, loss_ref, dx_ref, dg_ref):
        i = pl.program_id(0)

        @pl.when(i == 0)
        def _():
            loss_ref[...] = jnp.zeros_like(loss_ref)
            dg_ref[...] = jnp.zeros_like(dg_ref)

        xv = x_ref[...]
        gv = g_ref[...]
        r = lax.rsqrt(jnp.mean(xv * xv, axis=-1, keepdims=True) + EPS)
        xn = xv * r
        err = xn * gv - t_ref[...]
        loss_ref[...] += jnp.sum(err * err) * (0.5 / D_MODEL)
        dy = err * (1.0 / D_MODEL)
        u = dy * gv
        dx_ref[...] = r * (u - xn * jnp.mean(xn * u, axis=-1, keepdims=True))
        dg_ref[...] += jnp.sum(dy * xn, axis=0, keepdims=True)

    return pl.pallas_call(
        body, name="final_loss", grid=(s // tm,),
        in_specs=[pl.BlockSpec((tm, D_MODEL), lambda i: (i, 0)),
                  pl.BlockSpec((tm, D_MODEL), lambda i: (i, 0)),
                  pl.BlockSpec((1, D_MODEL), lambda i: (0, 0))],
        out_specs=[pl.BlockSpec((1, 128), lambda i: (0, 0)),
                   pl.BlockSpec((tm, D_MODEL), lambda i: (i, 0)),
                   pl.BlockSpec((1, D_MODEL), lambda i: (0, 0))],
        out_shape=[jax.ShapeDtypeStruct((1, 128), F32),
                   jax.ShapeDtypeStruct((s, D_MODEL), F32),
                   jax.ShapeDtypeStruct((1, D_MODEL), F32)],
        compiler_params=_params(("arbitrary",)),
    )(x, tgt, g)


def _rel_index():
    i = np.arange(A_TQ)[:, None]
    j = np.arange(A_BAND)[None, :]
    rel = np.clip(i - j + (A_BAND - A_TQ), -MAX_REL, MAX_REL) + MAX_REL
    dchunk = i // CHUNK + LOOKBACK - j // CHUNK
    valid = (dchunk >= 0) & (dchunk <= LOOKBACK)
    return jnp.asarray(np.where(valid, rel, -1).astype(np.int32))


def _layer_consts(p):
    tbias = relbias_tile(p["rel_bias"], _rel_index())
    return dict(
        norm_g=p["norm_g"].reshape(1, D_MODEL),
        v_gain=p["v_gain"].reshape(1, D_BRANCH),
        b_col=p["b_s"].reshape(N_HEADS, SG_CHUNK, 1),
        bg=p["branch_gain"].reshape(1, D_MODEL),
        tbias=tbias,
    )


def _gate_layout(fp, b_f, s):
    nb = s // 128
    ft = fp[:, :N_HEADS].T.reshape(N_HEADS * nb, 128)
    bcol = jnp.repeat(b_f, nb).reshape(N_HEADS * nb, 1)
    return ft, bcol


def layer_fwd(x, p, ride=()):
    s = x.shape[0]
    c = _layer_consts(p)
    h, qkv, kva, gates, uv, fp = inproj_fwd(x, c["norm_g"], p["wp"])
    ya, lse_a = mix_a_fwd(qkv, kva, c["tbias"])
    yb = mix_b_fwd(uv, c["v_gain"], p["w_s"], c["b_col"])
    ft, bcol = _gate_layout(fp, p["b_f"], s)
    c_row = fox_gate_fwd(ft, bcol).reshape(N_HEADS, s // ATT_T, 1, ATT_T)
    yc, ref_c, rl_c, *rode = fox_fwd(qkv, c_row, ride)
    yd, band_d = sb_fwd(qkv)
    out = outproj_fwd(x, ya, yb, yc, yd, gates, c["bg"], p["wout"])
    saved = dict(consts=c, x=x, h=h, qkv=qkv, gates=gates, uv=uv, kva=kva, ft=ft, bcol=bcol,
                 c_row=c_row, ya=ya, lse_a=lse_a, yb=yb, yc=yc, ref_c=ref_c, rl_c=rl_c, yd=yd, band_d=band_d)
    return out, saved, rode


def layer_bwd(dout, p, sv, exchange=False, upper_w_in=None, small_ride=None):
    s = dout.shape[0]
    c = sv["consts"]
    dya, dyb, dyc, dyd, dgates, dbg, dwout = outproj_bwd(
        dout, sv["ya"], sv["yb"], sv["yc"], sv["yd"], sv["gates"], c["bg"], p["wout"])
    dqa, dka, dva, dt = mix_a_bwd(sv["qkv"], sv["kva"], c["tbias"], dya, sv["ya"], sv["lse_a"])
    drel = relbias_grad(dt, _rel_index())[:N_HEADS, :2 * MAX_REL + 1]
    duv, dws, dbs, dvgain = mix_b_bwd(sv["uv"], c["v_gain"], p["w_s"], c["b_col"], dyb)
    ride = [dwout.astype(BF16).reshape(4, D_BRANCH, D_MODEL)] if exchange else []
    if upper_w_in is not None:
        ride.append(upper_w_in)
    dqc, dkc, dvc, dc, *rode = fox_bwd(sv["qkv"], sv["c_row"], dyc, sv["yc"], sv["ref_c"], sv["rl_c"], ride)
    dft, dbf = fox_gate_bwd(sv["ft"], sv["bcol"], dc.reshape(N_HEADS * (s // 128), 128))
    dfp = jnp.pad(dft.reshape(N_HEADS, s).T, ((0, 0), (0, 128 - N_HEADS)))
    grads = dict(b_f=dbf[:N_HEADS, 0], rel_bias=drel, w_s=dws, b_s=dbs.reshape(N_HEADS, SG_CHUNK),
                 v_gain=dvgain.reshape(D_BRANCH), branch_gain=dbg.reshape(4, D_BRANCH), wout=dwout)
    dqd, dkd, dvd, *small_parts = sb_bwd(sv["qkv"], dyd, sv["yd"], sv["band_d"], small_ride(grads) if small_ride else ())
    dp, dx, dnorm = inproj_bwd((dqa, dka, dva, dqc, dkc, dvc, dqd, dkd, dvd), dgates, duv, dfp,
                               p["wp"], sv["x"], c["norm_g"], dout)
    grads.update(norm_g=dnorm.reshape(D_MODEL), w_in_shards=inproj_wgrad(sv["h"], dp))
    if exchange:
        grads["w_out_parts"] = rode[0]
    if small_ride:
        grads["small_parts"] = small_parts[0]
    return dx, grads, (rode[1] if upper_w_in is not None else None)


def local_step(x, tgt, layers, final_g, next_shards=None):
    layers = list(layers)
    saved = []
    cur = x
    for l, p in enumerate(layers):
        ride = next_shards[l] if next_shards is not None and l + 1 < len(layers) else ()
        cur, sv, rode = layer_fwd(cur, p, ride)
        saved.append(sv)
        if ride:
            layers[l + 1] = dict(layers[l + 1], wp=pack_w_in(rode[0][None])[0], wout=rode[1].reshape(D_MODEL, D_MODEL))
    loss, dcur, dfinal = final_loss(cur, tgt, final_g.reshape(1, D_MODEL))
    grads = [None] * len(layers)
    for l in reversed(range(len(layers))):
        exchange = next_shards is not None
        upper = grads[l + 1]["w_in_shards"] if exchange and l + 1 < len(layers) else None
        small_ride = None
        if exchange and l == 0:
            def small_ride(g0, above=tuple(grads[1:])):
                stacked = {k: jnp.stack([g[k] for g in (g0,) + above]) for k in SMALL_EARLY if k != "final_g"}
                return [_pack([stacked.get(k, dfinal.reshape(D_MODEL)) for k in SMALL_EARLY])]
        dcur, grads[l], got = layer_bwd(dcur, layers[l], saved[l], exchange, upper, small_ride)
        if upper is not None:
            grads[l + 1]["w_in_parts"] = got
    return loss[0, 0], dcur, grads, dfinal.reshape(D_MODEL)


def _chip_gather(pairs, send_sems, recv_sems, loc_sems):
    x, y, c = lax.axis_index("x"), lax.axis_index("y"), lax.axis_index("c")
    me = 2 * x + y
    chips = [(1 - x, y), (x, 1 - y), (1 - x, 1 - y)]
    npair = len(pairs)

    def local():
        return [pltpu.make_async_copy(src, dst(me), loc_sems.at[n]) for n, (src, dst) in enumerate(pairs)]

    def remote(j, n, slot):
        src, dst = pairs[n]
        return pltpu.make_async_remote_copy(
            src_ref=src, dst_ref=dst(slot), send_sem=send_sems.at[npair * j + n], recv_sem=recv_sems.at[npair * j + n],
            device_id=(chips[j][0], chips[j][1], c), device_id_type=MESH)

    def start():
        for cp in local():
            cp.start()
        for j in range(3):
            for n in range(npair):
                remote(j, n, me).start()

    def wait():
        for j in range(3):
            for n in range(npair):
                remote(j, n, 2 * chips[j][0] + chips[j][1]).wait_recv()
        for j in range(3):
            for n in range(npair):
                remote(j, n, me).wait_send()
        for cp in local():
            cp.wait()

    return start, wait


def gather_weights(w_in, w_out, gains):
    depth = w_in.shape[0]

    def body(in_ref, out_ref, g_ref, oin_ref, oout_ref, og_ref, send_sems, recv_sems, loc_sems):
        pairs = [(in_ref, lambda s: oin_ref.at[:, s]), (out_ref, lambda s: oout_ref.at[:, s]), (g_ref, lambda s: og_ref.at[s])]
        start, wait = _chip_gather(pairs, send_sems, recv_sems, loc_sems)
        start()
        wait()

    any_spec = pl.BlockSpec(memory_space=pl.ANY)
    return pl.pallas_call(
        body, name="gather_weights",
        in_specs=[any_spec] * 3, out_specs=[any_spec] * 3,
        out_shape=[jax.ShapeDtypeStruct((depth, 4) + w_in.shape[1:], w_in.dtype),
                   jax.ShapeDtypeStruct((depth, 4) + w_out.shape[1:], w_out.dtype),
                   jax.ShapeDtypeStruct((4,) + gains.shape, gains.dtype)],
        scratch_shapes=[pltpu.SemaphoreType.DMA((9,)), pltpu.SemaphoreType.DMA((9,)), pltpu.SemaphoreType.DMA((3,))],
    )(w_in, w_out, gains)


def pack_w_in(shards):
    depth = shards.shape[0]
    tr = 256

    def body(s_ref, o_ref):
        full = jnp.concatenate([s_ref[n] for n in range(4)], axis=1)
        o_ref[...] = jnp.concatenate([full[:, :SEC_D_Q], full[:, SEC_D_Q + N_HEADS:], full[:, SEC_D_Q:SEC_D_Q + N_HEADS],
                                      jnp.zeros((tr, N_PACK - N_IN), BF16)], axis=1)

    return pl.pallas_call(
        body, name="pack_w_in", grid=(depth, D_MODEL // tr),
        in_specs=[pl.BlockSpec((None, 4, tr, N_SHARD), lambda l, r: (l, 0, r, 0))],
        out_specs=pl.BlockSpec((None, tr, N_PACK), lambda l, r: (l, r, 0)),
        out_shape=jax.ShapeDtypeStruct((depth, D_MODEL, N_PACK), BF16),
        compiler_params=_params(("arbitrary", "arbitrary")),
    )(shards)


def _device_exchange(flows, send_sems, recv_sems, loc_sems):
    x, y, c = lax.axis_index("x"), lax.axis_index("y"), lax.axis_index("c")
    me_chip = 2 * x + y
    me = 4 * x + 2 * y + c
    peers = [(x, y, 1 - c)]
    for px, py in [(1 - x, y), (x, 1 - y), (1 - x, 1 - y)]:
        peers += [(px, py, c), (px, py, 1 - c)]
    nflow = len(flows)

    def local():
        return [pltpu.make_async_copy(src(me_chip), dst(me), loc_sems.at[f]) for f, (src, dst) in enumerate(flows)]

    def copies(n, chip, slot):
        return [pltpu.make_async_remote_copy(src_ref=src(chip), dst_ref=dst(slot), send_sem=send_sems.at[nflow * n + f],
                                             recv_sem=recv_sems.at[nflow * n + f], device_id=peers[n], device_id_type=MESH)
                for f, (src, dst) in enumerate(flows)]

    def start():
        for cp in local():
            cp.start()
        for n, (px, py, _) in enumerate(peers):
            for cp in copies(n, 2 * px + py, me):
                cp.start()

    def wait():
        for n, (px, py, pc) in enumerate(peers):
            for cp in copies(n, me_chip, 4 * px + 2 * py + pc):
                cp.wait_recv()
        for n, (px, py, _) in enumerate(peers):
            for cp in copies(n, 2 * px + py, me):
                cp.wait_send()
        for cp in local():
            cp.wait()

    return start, wait


def _exchange_flows(srcs, dsts):
    return [((lambda s, src=src: src.at[s]) if src.shape[0] == 4 else (lambda s, src=src: src),
             lambda d, dst=dst: dst.at[d]) for src, dst in zip(srcs, dsts)]


def _exchange_shapes(arrays):
    return [jax.ShapeDtypeStruct((8,) + (a.shape[1:] if a.shape[0] == 4 else a.shape), a.dtype) for a in arrays]


def _exchange_sems(n):
    return [pltpu.SemaphoreType.DMA((7 * n,)), pltpu.SemaphoreType.DMA((7 * n,)), pltpu.SemaphoreType.DMA((n,))]


def exchange_grads(*arrays):
    n = len(arrays)

    def body(*refs):
        start, wait = _device_exchange(_exchange_flows(refs[:n], refs[n:2 * n]), *refs[2 * n:])
        start()
        wait()

    any_spec = pl.BlockSpec(memory_space=pl.ANY)
    return pl.pallas_call(
        body, name="exchange_grads",
        in_specs=[any_spec] * n, out_specs=[any_spec] * n, out_shape=_exchange_shapes(arrays),
        scratch_shapes=_exchange_sems(n),
    )(*arrays)


def adamw_reduce(parts, w, m, v, name, tr):
    rows, width = w.shape
    per = rows // len(parts) // tr
    c1 = 1.0 - ADAM_B1 ** ADAM_STEP
    c2 = 1.0 - ADAM_B2 ** ADAM_STEP

    def body(*refs):
        p_refs = refs[:len(parts)]
        w_ref, m_ref, v_ref, g_ref, d_ref, nm_ref, nv_ref = refs[len(parts):]
        i = pl.program_id(0)
        p = p_refs[0][...]
        for n in range(1, len(parts)):
            p = jnp.where(i >= n * per, p_refs[n][...], p)
        g = p[0].astype(F32)
        for n in range(1, 8):
            g = g + p[n].astype(F32)
        g_ref[...] = g
        nm = ADAM_B1 * m_ref[...] + (1.0 - ADAM_B1) * g
        nv = ADAM_B2 * v_ref[...] + (1.0 - ADAM_B2) * (g * g)
        nm_ref[...] = nm
        nv_ref[...] = nv
        d_ref[...] = -ADAM_LR * ((nm / c1) / (jnp.sqrt(nv / c2) + ADAM_EPS) + ADAM_WD * w_ref[...])

    spec = pl.BlockSpec((tr, width), lambda i: (i, 0))
    shape = jax.ShapeDtypeStruct((rows, width), F32)
    return pl.pallas_call(
        body, name=name, grid=(rows // tr,),
        in_specs=[pl.BlockSpec((8, tr, width), lambda i, n=n: (0, jnp.clip(i - n * per, 0, per - 1), 0))
                  for n in range(len(parts))] + [spec, spec, spec],
        out_specs=[spec] * 4, out_shape=[shape] * 4,
        compiler_params=_params(("arbitrary",)),
    )(*parts, w, m, v)


SMALL_EARLY = ("b_f", "rel_bias", "w_s", "b_s", "v_gain", "final_g")
WEIGHTS = ("norm_g", "w_in", "b_f", "rel_bias", "w_s", "b_s", "v_gain", "branch_gain", "w_out", "final_g")
PACK_ROW_TILE = 512


def _rows_of(shape):
    return -(-int(np.prod(shape)) // 128)


def _pack(leaves, tile=PACK_ROW_TILE):
    parts = []
    for a in leaves:
        flat = a.reshape(-1).astype(F32)
        parts.append(jnp.pad(flat, (0, _rows_of(a.shape) * 128 - flat.shape[0])))
    flat = jnp.concatenate(parts)
    rows = flat.shape[0] // 128
    total = -(-rows // tile) * tile
    return jnp.pad(flat, (0, (total - rows) * 128)).reshape(total, 128)


def _unpack(slab, shapes):
    out, row = [], 0
    for shp in shapes:
        n = int(np.prod(shp))
        r = _rows_of(shp)
        out.append(slab[row:row + r].reshape(-1)[:n].reshape(shp))
        row += r
    return out


def kernel(x, norm_g, w_in, b_f, rel_bias, w_s, b_s, v_gain, branch_gain, w_out, final_g, loss_target, m_norm_g, m_w_in, m_b_f, m_rel_bias, m_w_s, m_b_s, m_v_gain, m_branch_gain, m_w_out, m_final_g, v_norm_g, v_w_in, v_b_f, v_rel_bias, v_w_s, v_b_s, v_v_gain, v_branch_gain, v_w_out, v_final_g):
    depth = norm_g.shape[0]
    weights = dict(norm_g=norm_g, w_in=w_in, b_f=b_f, rel_bias=rel_bias, w_s=w_s, b_s=b_s, v_gain=v_gain,
                   branch_gain=branch_gain, w_out=w_out, final_g=final_g)
    mom1 = dict(norm_g=m_norm_g, w_in=m_w_in, b_f=m_b_f, rel_bias=m_rel_bias, w_s=m_w_s, b_s=m_b_s,
                v_gain=m_v_gain, branch_gain=m_branch_gain, w_out=m_w_out, final_g=m_final_g)
    mom2 = dict(norm_g=v_norm_g, w_in=v_w_in, b_f=v_b_f, rel_bias=v_rel_bias, w_s=v_w_s, b_s=v_b_s,
                v_gain=v_v_gain, branch_gain=v_branch_gain, w_out=v_w_out, final_g=v_final_g)

    wf = jnp.pad(branch_gain.reshape(-1), (0, 8 * 128 - branch_gain.size)).reshape(8, 128)
    w_in_b, w_out_b = w_in.astype(BF16), w_out.astype(BF16)
    w_in_shards, w_out_shards, gf = gather_weights(w_in_b[:1], w_out_b[:1], wf)
    bg_full = gf.reshape(4, -1)[:, :branch_gain.size].reshape((4,) + branch_gain.shape)
    bg_full = jnp.moveaxis(bg_full, 0, 2).reshape(depth, 4, D_BRANCH)

    layers = [dict(norm_g=norm_g[l], b_f=b_f[l], rel_bias=rel_bias[l], w_s=w_s[l],
                   b_s=b_s[l], v_gain=v_gain[l], branch_gain=bg_full[l]) for l in range(depth)]
    layers[0].update(wp=pack_w_in(w_in_shards)[0], wout=w_out_shards.reshape(D_MODEL, D_MODEL))
    next_shards = [(w_in_b[l + 1], w_out_b[l + 1]) for l in range(depth - 1)]

    loss_part, grad_x, lgrads, dfinal = local_step(x[0], loss_target[0], layers, final_g, next_shards)
    loss = lax.psum(loss_part, ("x", "y", "c"))

    stack = lambda k: jnp.stack([g[k] for g in lgrads])
    d_gain = jnp.moveaxis(stack("branch_gain").reshape(depth, 4, 4, HEAD_DIM), 2, 0).reshape(4, -1)
    d_gain = jnp.pad(d_gain, ((0, 0), (0, 8 * 128 - d_gain.shape[1]))).reshape(4, 8, 128)
    parts_in, parts_gain, parts_norm = exchange_grads(lgrads[0]["w_in_shards"], d_gain, _pack([stack("norm_g")], 16))
    parts = dict(w_in=[parts_in] + [g["w_in_parts"] for g in lgrads[1:]], w_out=[g["w_out_parts"] for g in lgrads])

    outs = {}
    tags = ("grad", "delta", "new_m", "new_v")
    for k in ("w_in", "w_out"):
        rows = depth * weights[k].shape[1]
        flat = lambda a: a.reshape(rows, a.shape[-1])
        res = adamw_reduce(parts[k], flat(weights[k]), flat(mom1[k]), flat(mom2[k]), "adamw_" + k, 256)
        for tag, a in zip(tags, res):
            outs[tag, k] = a.reshape(weights[k].shape)
    gain8 = lambda a: jnp.pad(a.reshape(-1), (0, 8 * 128 - a.size)).reshape(8, 128)
    res = adamw_reduce([parts_gain], gain8(branch_gain), gain8(m_branch_gain), gain8(v_branch_gain), "adamw_gain", 8)
    for tag, a in zip(tags, res):
        outs[tag, "branch_gain"] = a.reshape(-1)[:branch_gain.size].reshape(branch_gain.shape)
    for names, parts_small, tile in ((SMALL_EARLY, lgrads[0]["small_parts"], PACK_ROW_TILE), (("norm_g",), parts_norm, 16)):
        pack_small = lambda d: _pack([d[k] for k in names], tile)
        res = adamw_reduce([parts_small], pack_small(weights), pack_small(mom1), pack_small(mom2),
                           "adamw_" + names[0], tile)
        for tag, slab in zip(tags, res):
            for k, a in zip(names, _unpack(slab, [weights[k].shape for k in names])):
                outs[tag, k] = a
    result = [loss, grad_x[None]]
    for tag in ("grad", "delta", "new_m", "new_v"):
        result += [outs[tag, k] for k in WEIGHTS]
    return tuple(result)
```

```python
import jax
import jax.numpy as jnp
import numpy as np
from jax import lax
from jax.experimental import pallas as pl
from jax.experimental.pallas import tpu as pltpu

F32 = jnp.float32
BF16 = jnp.bfloat16
MESH = pl.DeviceIdType.MESH

D_MODEL = 1024
D_BRANCH = 256
N_HEADS = 4
HEAD_DIM = 64
CHUNK = 64
LOOKBACK = 8
MAX_REL = 128
SG_CHUNK = 128
EPS = 1e-6
N_IN = 3844
N_PACK = 3968
F_COL = 3840
N_SHARD = 961
NEG = -1e30

A_TQ = 128
A_BAND = A_TQ + LOOKBACK * CHUNK
REL_LO = MAX_REL - (CHUNK - 1)
REL_HI = 2 * MAX_REL + 1
A_PAD = LOOKBACK * CHUNK
A_QB = 1024
ATT_T = 256
FOX_TQ = 512
FOX_WIDE = 4
FOX_DEAD2 = -136.0
LOG2E = 1.4426950408889634
SB_TQ = 1024
SB_SUB = 128
SB_BACK = 256
SB_BAND = SB_SUB + SB_BACK
SB_DEAD = -110.0
ROW_T = 512
VMEM_LIMIT = 56 * 1024 * 1024

ADAM_LR = 0.001
ADAM_B1 = 0.9
ADAM_B2 = 0.999
ADAM_EPS = 1e-08
ADAM_WD = 0.01
ADAM_STEP = 10

SEC_A_Q, SEC_A_K, SEC_A_V, SEC_A_G = 0, 256, 512, 768
SEC_B_U, SEC_B_V, SEC_B_G = 1024, 1280, 1536
SEC_C_Q, SEC_C_K, SEC_C_V, SEC_C_G = 1792, 2048, 2304, 2560
SEC_D_Q, SEC_D_K, SEC_D_V, SEC_D_G = 2816, 3072, 3328, 3584
QKV_SECS = (SEC_A_Q, SEC_C_Q, SEC_C_K, SEC_C_V, SEC_D_Q, SEC_D_K, SEC_D_V)
GATE_SECS = (SEC_A_G, SEC_B_G, SEC_C_G, SEC_D_G)


def _dot(a, b):
    return jnp.dot(a, b, preferred_element_type=F32)


def _dot_nt(a, b):
    return lax.dot_general(a, b, (((1,), (1,)), ((), ())), preferred_element_type=F32)


def _dot_tn(a, b):
    return lax.dot_general(a, b, (((0,), (0,)), ((), ())), preferred_element_type=F32)


def _split2(x):
    hi = x.astype(BF16)
    lo = (x - hi.astype(F32)).astype(BF16)
    return hi, lo


def _split3(x):
    hi = x.astype(BF16)
    r = x - hi.astype(F32)
    mid = r.astype(BF16)
    lo = (r - mid.astype(F32)).astype(BF16)
    return hi, mid, lo


def _sigmoid(x):
    return 1.0 / (1.0 + jnp.exp(-x))


def _params(sem=None, vmem=VMEM_LIMIT):
    return pltpu.CompilerParams(dimension_semantics=sem, vmem_limit_bytes=vmem)


def _heads_to_lanes(ref):
    return jnp.concatenate([ref[h] for h in range(N_HEADS)], axis=1)


def inproj_fwd(x, g, wp):
    s = x.shape[0]
    tm = A_PAD

    def body(x_ref, g_ref, w_ref, h_ref, qkv_ref, kva_ref, gates_ref, uv_ref, f_ref):
        xv = x_ref[...]
        r = lax.rsqrt(jnp.mean(xv * xv, axis=-1, keepdims=True) + EPS)
        h = (xv * r * g_ref[...]).astype(BF16)
        h_ref[...] = h
        for n, off in enumerate(QKV_SECS):
            p = _dot(h, w_ref[:, off:off + D_BRANCH])
            for hh in range(N_HEADS):
                qkv_ref[n, hh] = p[:, hh * HEAD_DIM:(hh + 1) * HEAD_DIM].astype(BF16)
        for n, off in enumerate((SEC_A_K, SEC_A_V)):
            p = _dot(h, w_ref[:, off:off + D_BRANCH])
            for hh in range(N_HEADS):
                kva_ref[n, hh] = p[:, hh * HEAD_DIM:(hh + 1) * HEAD_DIM].astype(BF16)
        for n, off in enumerate(GATE_SECS):
            gates_ref[:, n * D_BRANCH:(n + 1) * D_BRANCH] = _dot(h, w_ref[:, off:off + D_BRANCH])
        uv_ref[...] = _dot(h, w_ref[:, SEC_B_U:SEC_B_U + 2 * D_BRANCH])
        f_ref[...] = _dot(h, w_ref[:, F_COL:F_COL + 128])

    return pl.pallas_call(
        body, name="inproj_fwd", grid=(s // tm,),
        in_specs=[pl.BlockSpec((tm, D_MODEL), lambda i: (i, 0)),
                  pl.BlockSpec((1, D_MODEL), lambda i: (0, 0)),
                  pl.BlockSpec((D_MODEL, N_PACK), lambda i: (0, 0))],
        out_specs=[pl.BlockSpec((tm, D_MODEL), lambda i: (i, 0)),
                   pl.BlockSpec((len(QKV_SECS), N_HEADS, tm, HEAD_DIM), lambda i: (0, 0, i, 0)),
                   pl.BlockSpec((2, N_HEADS, tm, HEAD_DIM), lambda i: (0, 0, i + 1, 0)),
                   pl.BlockSpec((tm, D_MODEL), lambda i: (i, 0)),
                   pl.BlockSpec((tm, 2 * D_BRANCH), lambda i: (i, 0)),
                   pl.BlockSpec((tm, 128), lambda i: (i, 0))],
        out_shape=[jax.ShapeDtypeStruct((s, D_MODEL), BF16),
                   jax.ShapeDtypeStruct((len(QKV_SECS), N_HEADS, s, HEAD_DIM), BF16),
                   jax.ShapeDtypeStruct((2, N_HEADS, s + tm, HEAD_DIM), BF16),
                   jax.ShapeDtypeStruct((s, D_MODEL), F32),
                   jax.ShapeDtypeStruct((s, 2 * D_BRANCH), F32),
                   jax.ShapeDtypeStruct((s, 128), F32)],
        compiler_params=_params(("arbitrary",)),
    )(x, g, wp)


def inproj_bwd(dqkv, dgates, duv, dfp, wp, x, g, dres):
    s = x.shape[0]
    tm = A_PAD

    def body(*refs):
        dq_refs = refs[:9]
        dgates_ref, duv_ref, dfp_ref, w_ref, x_ref, g_ref, dres_ref, dp_ref, dx_ref, dg_ref = refs[9:]
        i = pl.program_id(0)
        a_q, a_k, a_v, c_q, c_k, c_v, d_q, d_k, d_v = [_heads_to_lanes(r).astype(BF16) for r in dq_refs]
        dgt = dgates_ref[...]
        duv_b = duv_ref[...].astype(BF16)
        dp = jnp.concatenate(
            [a_q, a_k, a_v, dgt[:, 0:256], duv_b, dgt[:, 256:512], c_q, c_k, c_v, dgt[:, 512:768],
             d_q, d_k, d_v, dgt[:, 768:1024], dfp_ref[...].astype(BF16)], axis=1)
        dp_ref[...] = dp
        dh = _dot_nt(dp, w_ref[...])
        xv = x_ref[...]
        r = lax.rsqrt(jnp.mean(xv * xv, axis=-1, keepdims=True) + EPS)
        xn = xv * r
        u = dh * g_ref[...]
        dx_ref[...] = dres_ref[...] + r * (u - xn * jnp.mean(xn * u, axis=-1, keepdims=True))

        @pl.when(i == 0)
        def _():
            dg_ref[...] = jnp.zeros_like(dg_ref)

        dg_ref[...] += jnp.sum(dh * xn, axis=0, keepdims=True)

    head_spec = pl.BlockSpec((N_HEADS, tm, HEAD_DIM), lambda i: (0, i, 0))
    padded_spec = pl.BlockSpec((N_HEADS, tm, HEAD_DIM), lambda i: (0, i + 1, 0))
    return pl.pallas_call(
        body, name="inproj_bwd", grid=(s // tm,),
        in_specs=[head_spec, padded_spec, padded_spec] + [head_spec] * 6 + [
            pl.BlockSpec((tm, D_MODEL), lambda i: (i, 0)),
            pl.BlockSpec((tm, 2 * D_BRANCH), lambda i: (i, 0)),
            pl.BlockSpec((tm, 128), lambda i: (i, 0)),
            pl.BlockSpec((D_MODEL, N_PACK), lambda i: (0, 0)),
            pl.BlockSpec((tm, D_MODEL), lambda i: (i, 0)),
            pl.BlockSpec((1, D_MODEL), lambda i: (0, 0)),
            pl.BlockSpec((tm, D_MODEL), lambda i: (i, 0))],
        out_specs=[pl.BlockSpec((tm, N_PACK), lambda i: (i, 0)),
                   pl.BlockSpec((tm, D_MODEL), lambda i: (i, 0)),
                   pl.BlockSpec((1, D_MODEL), lambda i: (0, 0))],
        out_shape=[jax.ShapeDtypeStruct((s, N_PACK), BF16),
                   jax.ShapeDtypeStruct((s, D_MODEL), F32),
                   jax.ShapeDtypeStruct((1, D_MODEL), F32)],
        compiler_params=_params(("arbitrary",)),
    )(*dqkv, dgates, duv, dfp, wp, x, g, dres)


def inproj_wgrad(h, dp, half=None, ride=()):
    s, m = h.shape
    tm = min(2 * ROW_T, s)
    tmm = 256
    nsteps = s // tm
    ntile = m // tmm if half is None else m // tmm // 2
    first = 0 if half is None else half * ntile
    nride = len(ride)

    def body(a_ref, b_ref, *refs):
        ride_in, o_ref = refs[:nride], refs[nride]
        ride_out, acc_ref = refs[nride + 1:2 * nride + 1], refs[2 * nride + 1]
        k = pl.program_id(1)
        if nride:
            j = pl.program_id(0)
            start, wait = _device_exchange(_exchange_flows(ride_in, ride_out), *refs[2 * nride + 2:])
            pl.when(jnp.logical_and(j == 0, k == 0))(start)

        @pl.when(k == 0)
        def _():
            acc_ref[...] = jnp.zeros_like(acc_ref)

        acc_ref[...] += _dot_tn(a_ref[...], b_ref[...])

        @pl.when(k == nsteps - 1)
        def _():
            acc = acc_ref[...]
            full = jnp.concatenate([acc[:, :SEC_D_Q], acc[:, F_COL:F_COL + N_HEADS], acc[:, SEC_D_Q:F_COL]], axis=1)
            for n in range(4):
                o_ref[n] = full[:, n * N_SHARD:(n + 1) * N_SHARD].astype(BF16)

        if nride:
            pl.when(jnp.logical_and(j == ntile - 1, k == nsteps - 1))(wait)

    any_spec = pl.BlockSpec(memory_space=pl.ANY)
    return pl.pallas_call(
        body, name="inproj_wgrad_exchange" if nride else "inproj_wgrad", grid=(ntile, nsteps),
        in_specs=[pl.BlockSpec((tm, tmm), lambda j, k: (k, first + j)),
                  pl.BlockSpec((tm, N_PACK), lambda j, k: (k, 0))] + [any_spec] * nride,
        out_specs=[pl.BlockSpec((4, tmm, N_SHARD), lambda j, k: (0, j, 0))] + [any_spec] * nride,
        out_shape=[jax.ShapeDtypeStruct((4, ntile * tmm, N_SHARD), BF16)] + _exchange_shapes(ride),
        scratch_shapes=[pltpu.VMEM((tmm, N_PACK), F32)] + (_exchange_sems(nride) if nride else []),
        compiler_params=_params(("arbitrary", "arbitrary")),
    )(h, dp, *ride)


def _a_specs(s):
    nq = s // A_QB
    per = A_QB // A_PAD
    q_spec = pl.BlockSpec((None, None, A_QB, HEAD_DIM), lambda h, i: (0, h, jnp.minimum(i, nq - 1), 0))
    kv_specs = [pl.BlockSpec((None, None, A_PAD, HEAD_DIM),
                             lambda h, i, n=n, m=m: (n, h, jnp.minimum(per * i + m, per * nq), 0))
                for n in range(2) for m in range(per + 1)]
    t_spec = pl.BlockSpec((None, A_TQ, A_BAND), lambda h, i: (h, 0, 0))
    return nq, q_spec, kv_specs, t_spec


def _a_window(refs, i):
    first = refs[0][...]
    return jnp.concatenate([jnp.where(i > 0, first, jnp.zeros_like(first))] + [r[...] for r in refs[1:]], axis=0)


def _a_scores(q_ref, k, t_ref, i, j):
    rows = slice(j * A_TQ, (j + 1) * A_TQ)
    qs = q_ref[rows, :] * 0.125
    kj = k[j * A_TQ:j * A_TQ + A_BAND, :]
    sc = _dot_nt(qs, kj) + t_ref[...]
    col = lax.broadcasted_iota(jnp.int32, (A_TQ, A_BAND), 1)
    sc = jnp.where(col >= A_PAD - i * A_QB - j * A_TQ, sc, NEG)
    return rows, qs, kj, sc


def mix_a_fwd(qkv, kva, tbias):
    s = qkv.shape[2]
    nq, q_spec, kv_specs, t_spec = _a_specs(s)
    nwin = len(kv_specs) // 2

    def body(*refs):
        q_ref, t_ref, o_ref, lse_ref = refs[0], refs[1 + 2 * nwin], refs[2 + 2 * nwin], refs[3 + 2 * nwin]
        i = pl.program_id(1)
        k = _a_window(refs[1:1 + nwin], i)
        v = _a_window(refs[1 + nwin:1 + 2 * nwin], i)
        for j in range(A_QB // A_TQ):
            rows, _, _, sc = _a_scores(q_ref, k, t_ref, i, j)
            m = jnp.max(sc, axis=-1, keepdims=True)
            p = jnp.exp(sc - m)
            l = jnp.sum(p, axis=-1, keepdims=True)
            o_ref[rows, :] = _dot(p.astype(BF16), v[j * A_TQ:j * A_TQ + A_BAND, :]) / l
            lse_ref[rows, :] = m + jnp.log(l)

    return pl.pallas_call(
        body, name="mix_a_fwd", grid=(N_HEADS, nq),
        in_specs=[q_spec] + kv_specs + [t_spec],
        out_specs=[pl.BlockSpec((None, A_QB, HEAD_DIM), lambda h, i: (h, i, 0)),
                   pl.BlockSpec((None, A_QB, 1), lambda h, i: (h, i, 0))],
        out_shape=[jax.ShapeDtypeStruct((N_HEADS, s, HEAD_DIM), F32),
                   jax.ShapeDtypeStruct((N_HEADS, s, 1), F32)],
        compiler_params=_params(("arbitrary", "arbitrary")),
    )(qkv, *([kva] * (2 * nwin)), tbias)


def mix_a_bwd(qkv, kva, tbias, do, o, lse):
    s = qkv.shape[2]
    nq, q_spec, kv_specs, t_spec = _a_specs(s)
    nwin = len(kv_specs) // 2
    row_spec = lambda w: pl.BlockSpec((None, A_QB, w), lambda h, i: (h, jnp.minimum(i, nq - 1), 0))
    done_spec = pl.BlockSpec((None, A_QB, HEAD_DIM), lambda h, i: (h, i, 0))
    win = A_QB + A_PAD

    def body(*refs):
        q_ref = refs[0]
        t_ref, do_ref, o_ref, lse_ref, dq_ref, dk_ref, dv_ref, dt_ref, dk_win, dv_win = refs[1 + 2 * nwin:]
        i = pl.program_id(1)

        @pl.when(i == 0)
        def _():
            dk_win[...] = jnp.zeros_like(dk_win)
            dv_win[...] = jnp.zeros_like(dv_win)
            dt_ref[...] = jnp.zeros_like(dt_ref)

        @pl.when(i < nq)
        def _():
            k = _a_window(refs[1:1 + nwin], i)
            v = _a_window(refs[1 + nwin:1 + 2 * nwin], i)
            dt = jnp.zeros((A_TQ, A_BAND), F32)
            for j in range(A_QB // A_TQ):
                rows, qs, kj, sc = _a_scores(q_ref, k, t_ref, i, j)
                keys = slice(j * A_TQ, j * A_TQ + A_BAND)
                dob = do_ref[rows, :]
                p = jnp.exp(sc - lse_ref[rows, :])
                delta = jnp.sum(o_ref[rows, :] * dob.astype(F32), axis=-1, keepdims=True)
                ds = p * (_dot_nt(dob, v[keys, :]) - delta)
                dsb = ds.astype(BF16)
                dq_ref[rows, :] = _dot(dsb, kj) * 0.125
                dk_win[keys, :] += _dot_tn(dsb, qs)
                dv_win[keys, :] += _dot_tn(p.astype(BF16), dob)
                dt = dt + ds
            dt_ref[...] += dt

        dk_ref[...] = dk_win[0:A_QB, :]
        dv_ref[...] = dv_win[0:A_QB, :]
        dk_rest = dk_win[A_QB:win, :]
        dv_rest = dv_win[A_QB:win, :]
        dk_win[0:A_PAD, :] = dk_rest
        dv_win[0:A_PAD, :] = dv_rest
        dk_win[A_PAD:win, :] = jnp.zeros((A_QB, HEAD_DIM), F32)
        dv_win[A_PAD:win, :] = jnp.zeros((A_QB, HEAD_DIM), F32)

    return pl.pallas_call(
        body, name="mix_a_bwd", grid=(N_HEADS, nq + 1),
        in_specs=[q_spec] + kv_specs + [t_spec, row_spec(HEAD_DIM), row_spec(HEAD_DIM), row_spec(1)],
        out_specs=[row_spec(HEAD_DIM), done_spec, done_spec, t_spec],
        out_shape=[jax.ShapeDtypeStruct((N_HEADS, s, HEAD_DIM), F32),
                   jax.ShapeDtypeStruct((N_HEADS, s + A_QB, HEAD_DIM), F32),
                   jax.ShapeDtypeStruct((N_HEADS, s + A_QB, HEAD_DIM), F32),
                   jax.ShapeDtypeStruct((N_HEADS, A_TQ, A_BAND), F32)],
        scratch_shapes=[pltpu.VMEM((win, HEAD_DIM), F32), pltpu.VMEM((win, HEAD_DIM), F32)],
        compiler_params=_params(("arbitrary", "arbitrary")),
    )(qkv, *([kva] * (2 * nwin)), tbias, do, o, lse)


def relbias_tile(rel_bias, relmat):
    def body(rb_ref, rel_ref, o_ref):
        rel = rel_ref[...]
        o_ref[...] = jnp.full(o_ref.shape, NEG, F32)

        def step(r, carry):
            hit = rel == r
            for h in range(N_HEADS):
                o_ref[h] = jnp.where(hit, rb_ref[h, r], o_ref[h])
            return carry

        lax.fori_loop(REL_LO, REL_HI, step, 0)

    return pl.pallas_call(
        body, name="relbias_tile",
        in_specs=[pl.BlockSpec(memory_space=pltpu.SMEM), pl.BlockSpec(memory_space=pltpu.VMEM)],
        out_specs=pl.BlockSpec(memory_space=pltpu.VMEM),
        out_shape=jax.ShapeDtypeStruct((N_HEADS, A_TQ, A_BAND), F32),
        compiler_params=_params(),
    )(rel_bias, relmat)


def relbias_grad(dt, relmat):
    def body(dt_ref, rel_ref, o_ref):
        rel = rel_ref[...]
        lane = lax.broadcasted_iota(jnp.int32, (8, 384), 1)
        row = lax.broadcasted_iota(jnp.int32, (8, 384), 0)

        def step(r, acc):
            hit = rel == r
            for h in range(N_HEADS):
                val = jnp.sum(jnp.where(hit, dt_ref[h], 0.0))
                acc = jnp.where((lane == r) & (row == h), val, acc)
            return acc

        o_ref[...] = lax.fori_loop(REL_LO, REL_HI, step, jnp.zeros((8, 384), F32))

    return pl.pallas_call(
        body, name="relbias_grad",
        out_shape=jax.ShapeDtypeStruct((8, 384), F32),
        compiler_params=_params(),
    )(dt, relmat)


def _b_norm(v, gain):
    mu = jnp.mean(v, axis=-1, keepdims=True)
    xc = v - mu
    rstd = lax.rsqrt(jnp.mean(xc * xc, axis=-1, keepdims=True) + EPS)
    xhat = xc * rstd
    return xhat, rstd, xhat * gain


def _tril_mask():
    t = lax.broadcasted_iota(jnp.int32, (SG_CHUNK, SG_CHUNK), 0)
    u = lax.broadcasted_iota(jnp.int32, (SG_CHUNK, SG_CHUNK), 1)
    return u <= t


def mix_b_fwd(uv, gain, w_s, b_col):
    s = uv.shape[0]
    tm = min(ROW_T, s)

    def body(uv_ref, gain_ref, w_ref, b_ref, y_ref):
        tril = _tril_mask()
        ws = [jnp.where(tril, w_ref[g], 0.0).astype(BF16) for g in range(N_HEADS)]
        for c in range(tm // SG_CHUNK):
            rows = slice(c * SG_CHUNK, (c + 1) * SG_CHUNK)
            u = uv_ref[rows, 0:D_BRANCH]
            _, _, vn = _b_norm(uv_ref[rows, D_BRANCH:2 * D_BRANCH], gain_ref[...])
            vnb = vn.astype(BF16)
            outs = []
            for g in range(N_HEADS):
                cols = slice(g * HEAD_DIM, (g + 1) * HEAD_DIM)
                mixed = _dot(ws[g], vnb[:, cols]) + b_ref[g]
                outs.append(u[:, cols] * mixed)
            y_ref[rows, :] = jnp.concatenate(outs, axis=1)

    return pl.pallas_call(
        body, name="mix_b_fwd", grid=(s // tm,),
        in_specs=[pl.BlockSpec((tm, 2 * D_BRANCH), lambda i: (i, 0)),
                  pl.BlockSpec((1, D_BRANCH), lambda i: (0, 0)),
                  pl.BlockSpec((N_HEADS, SG_CHUNK, SG_CHUNK), lambda i: (0, 0, 0)),
                  pl.BlockSpec((N_HEADS, SG_CHUNK, 1), lambda i: (0, 0, 0))],
        out_specs=pl.BlockSpec((tm, D_BRANCH), lambda i: (i, 0)),
        out_shape=jax.ShapeDtypeStruct((s, D_BRANCH), F32),
        compiler_params=_params(("arbitrary",)),
    )(uv, gain, w_s, b_col)


def mix_b_bwd(uv, gain, w_s, b_col, dy):
    s = uv.shape[0]
    tm = min(ROW_T, s)

    def body(uv_ref, gain_ref, w_ref, b_ref, dy_ref, duv_ref, dw_ref, db_ref, dgain_ref):
        i = pl.program_id(0)

        @pl.when(i == 0)
        def _():
            dw_ref[...] = jnp.zeros_like(dw_ref)
            db_ref[...] = jnp.zeros_like(db_ref)
            dgain_ref[...] = jnp.zeros_like(dgain_ref)

        tril = _tril_mask()
        ws = [jnp.where(tril, w_ref[g], 0.0).astype(BF16) for g in range(N_HEADS)]
        gain_v = gain_ref[...]
        for c in range(tm // SG_CHUNK):
            rows = slice(c * SG_CHUNK, (c + 1) * SG_CHUNK)
            u = uv_ref[rows, 0:D_BRANCH]
            xhat, rstd, vn = _b_norm(uv_ref[rows, D_BRANCH:2 * D_BRANCH], gain_v)
            vnb = vn.astype(BF16)
            dyv = dy_ref[rows, :]
            dus, dvns = [], []
            for g in range(N_HEADS):
                cols = slice(g * HEAD_DIM, (g + 1) * HEAD_DIM)
                mixed = _dot(ws[g], vnb[:, cols]) + b_ref[g]
                dus.append(dyv[:, cols] * mixed)
                dmixed = dyv[:, cols] * u[:, cols]
                dmb = dmixed.astype(BF16)
                db_ref[g] += jnp.sum(dmixed, axis=-1, keepdims=True)
                dw_ref[g] += jnp.where(tril, _dot_nt(dmb, vnb[:, cols]), 0.0)
                dvns.append(_dot_tn(ws[g], dmb))
            dvn = jnp.concatenate(dvns, axis=1)
            dgain_ref[...] += jnp.sum(dvn * xhat, axis=0, keepdims=True)
            dxh = dvn * gain_v
            dv = rstd * (dxh - jnp.mean(dxh, axis=-1, keepdims=True)
                         - xhat * jnp.mean(dxh * xhat, axis=-1, keepdims=True))
            duv_ref[rows, :] = jnp.concatenate(dus + [dv], axis=1)

    return pl.pallas_call(
        body, name="mix_b_bwd", grid=(s // tm,),
        in_specs=[pl.BlockSpec((tm, 2 * D_BRANCH), lambda i: (i, 0)),
                  pl.BlockSpec((1, D_BRANCH), lambda i: (0, 0)),
                  pl.BlockSpec((N_HEADS, SG_CHUNK, SG_CHUNK), lambda i: (0, 0, 0)),
                  pl.BlockSpec((N_HEADS, SG_CHUNK, 1), lambda i: (0, 0, 0)),
                  pl.BlockSpec((tm, D_BRANCH), lambda i: (i, 0))],
        out_specs=[pl.BlockSpec((tm, 2 * D_BRANCH), lambda i: (i, 0)),
                   pl.BlockSpec((N_HEADS, SG_CHUNK, SG_CHUNK), lambda i: (0, 0, 0)),
                   pl.BlockSpec((N_HEADS, SG_CHUNK, 1), lambda i: (0, 0, 0)),
                   pl.BlockSpec((1, D_BRANCH), lambda i: (0, 0))],
        out_shape=[jax.ShapeDtypeStruct((s, 2 * D_BRANCH), F32),
                   jax.ShapeDtypeStruct((N_HEADS, SG_CHUNK, SG_CHUNK), F32),
                   jax.ShapeDtypeStruct((N_HEADS, SG_CHUNK, 1), F32),
                   jax.ShapeDtypeStruct((1, D_BRANCH), F32)],
        compiler_params=_params(("arbitrary",)),
    )(uv, gain, w_s, b_col, dy)


def _scan_mats(nrow):
    a = lax.broadcasted_iota(jnp.int32, (128, 128), 0)
    b = lax.broadcasted_iota(jnp.int32, (128, 128), 1)
    r = lax.broadcasted_iota(jnp.int32, (nrow, nrow), 0)
    c = lax.broadcasted_iota(jnp.int32, (nrow, nrow), 1)
    nb = nrow // N_HEADS
    same = (r // nb) == (c // nb)
    return a, b, r, c, same


def _exact_dot(x, m):
    hi, mid, lo = _split3(x)
    return _dot(hi, m) + _dot(mid, m) + _dot(lo, m)


def _exact_dot_left(m, x):
    hi, mid, lo = _split3(x)
    return _dot(m, hi) + _dot(m, mid) + _dot(m, lo)


def fox_gate_fwd(ft, bcol):
    nrow = ft.shape[0]

    def body(f_ref, b_ref, c_ref):
        z = f_ref[...] + b_ref[...]
        ls = jnp.minimum(z, 0.0) - jnp.log(1.0 + jnp.exp(-jnp.abs(z)))
        a, b, r, c, same = _scan_mats(nrow)
        within = _exact_dot(ls, (a <= b).astype(BF16))
        tot = jnp.broadcast_to(within[:, 127:128], within.shape)
        before = _exact_dot_left((same & (c < r)).astype(BF16), tot)
        c_ref[...] = within + before

    return pl.pallas_call(
        body, name="fox_gate_fwd",
        out_shape=jax.ShapeDtypeStruct((nrow, 128), F32),
        compiler_params=_params(),
    )(ft, bcol)


def fox_gate_bwd(ft, bcol, dc):
    nrow = ft.shape[0]

    def body(f_ref, b_ref, dc_ref, df_ref, db_ref):
        a, b, r, c, same = _scan_mats(nrow)
        dcv = dc_ref[...]
        within = _exact_dot(dcv, (a >= b).astype(BF16))
        tot = jnp.broadcast_to(within[:, 0:1], within.shape)
        after = _exact_dot_left((same & (c > r)).astype(BF16), tot)
        dls = within + after
        z = f_ref[...] + b_ref[...]
        dz = dls * _sigmoid(-z)
        df_ref[...] = dz
        rs = jnp.broadcast_to(jnp.sum(dz, axis=-1, keepdims=True), dz.shape)
        hr = lax.broadcasted_iota(jnp.int32, (8, nrow), 0)
        hc = lax.broadcasted_iota(jnp.int32, (8, nrow), 1)
        db_ref[...] = _exact_dot_left((hr == hc // (nrow // N_HEADS)).astype(BF16), rs)

    return pl.pallas_call(
        body, name="fox_gate_bwd",
        out_shape=[jax.ShapeDtypeStruct((nrow, 128), F32), jax.ShapeDtypeStruct((8, 128), F32)],
        compiler_params=_params(),
    )(ft, bcol, dc)


def _att_specs(s, qi, ki, vi):
    q_spec = pl.BlockSpec((None, None, FOX_TQ, HEAD_DIM), lambda h, i: (qi, h, i, 0))
    k_spec = pl.BlockSpec((None, None, s, HEAD_DIM), lambda h, i: (ki, h, 0, 0))
    v_spec = pl.BlockSpec((None, None, s, HEAD_DIM), lambda h, i: (vi, h, 0, 0))
    row_spec = lambda w: pl.BlockSpec((None, FOX_TQ, w), lambda h, i: (h, i, 0))
    gate_spec = pl.BlockSpec((None, s // ATT_T, 1, ATT_T), lambda h, i: (h, 0, 0, 0))
    return q_spec, k_spec, v_spec, row_spec, gate_spec


def _causal(n):
    row = lax.broadcasted_iota(jnp.int32, (n, n), 0)
    col = lax.broadcasted_iota(jnp.int32, (n, n), 1)
    return col <= row


def _gate_row(cr_ref, kb, g):
    if g == 1:
        return cr_ref[kb]
    return jnp.concatenate([cr_ref[kb + n] for n in range(g)], axis=1)


def _fox_walk(i, carry, tile, alive):
    g = FOX_WIDE
    own = FOX_TQ // ATT_T
    nwide = (own * i) // g
    carry = tile(own * i, own, carry, True)
    carry = lax.fori_loop(0, (own * i - nwide * g) // own, lambda n, c: tile(nwide * g, own, c, False), carry)

    def cond(state):
        return jnp.logical_and(state[0] >= 0, state[1] > 0)

    def step(state):
        n = state[0]
        c = tile(n * g, g, state[2:], False)
        return (n - 1, alive(n * g, c)) + tuple(c)

    out = lax.while_loop(cond, step, (nwide - 1, alive(nwide * g, carry)) + tuple(carry))
    return out[2:]


def _fox_reach(qs, k_ref, kmax_ref, cc, i):
    s = k_ref.shape[0]
    rows = 4 * ATT_T

    @pl.when(i == 0)
    def _():
        def chunk(n, mx):
            kc = k_ref[pl.ds(pl.multiple_of(n * rows, rows), rows), :].astype(F32)
            return jnp.maximum(mx, jnp.max(jnp.sum(kc * kc, axis=-1, keepdims=True)))

        kmax_ref[0] = jnp.sqrt(lax.fori_loop(0, s // rows, chunk, jnp.float32(0.0)))

    qf = qs.astype(F32)
    return jnp.sqrt(jnp.sum(qf * qf, axis=-1, keepdims=True)) * kmax_ref[0] + cc


def _gate_col(cr_ref, i):
    row = lax.broadcasted_iota(jnp.int32, (ATT_T, ATT_T), 0)
    col = lax.broadcasted_iota(jnp.int32, (ATT_T, ATT_T), 1)
    own = FOX_TQ // ATT_T
    return jnp.concatenate([jnp.sum(jnp.where(row == col, cr_ref[own * i + n], 0.0), axis=-1, keepdims=True)
                            for n in range(own)], axis=0)


def _fox_scores(qs, k, cc, crow, masked):
    sc = (_dot_nt(qs, k) + (cc - crow)) * LOG2E
    if masked:
        sc = jnp.where(_causal(FOX_TQ), sc, NEG)
    return sc


def fox_fwd(qkv, c_row, ride=()):
    s = qkv.shape[2]
    t = ATT_T
    nq = s // FOX_TQ
    q_spec, k_spec, v_spec, row_spec, gate_spec = _att_specs(s, 1, 2, 3)
    rows = 4 * t
    nride = len(ride)

    def body(q_ref, k_ref, v_ref, cr_ref, *refs):
        ride_in, refs = refs[:nride], refs[nride:]
        o_ref, ref_ref, rl_ref = refs[:3]
        ride_out, refs = refs[3:3 + nride], refs[3 + nride:]
        v1_ref, kmax_ref = refs[:2]
        i = pl.program_id(1)
        if nride:
            h = pl.program_id(0)
            start, wait = _chip_gather([(src, lambda slot, dst=dst: dst.at[slot]) for src, dst in zip(ride_in, ride_out)],
                                       *refs[2:])
            pl.when(jnp.logical_and(h == 0, i == 0))(start)

        @pl.when(i == 0)
        def _():
            def chunk(n, carry):
                r0 = pl.multiple_of(n * rows, rows)
                v1_ref[pl.ds(r0, rows), :] = jnp.concatenate(
                    [v_ref[pl.ds(r0, rows), :], jnp.ones((rows, HEAD_DIM), BF16)], axis=1)
                return carry

            lax.fori_loop(0, s // rows, chunk, 0)

        qs = q_ref[...] * 0.125
        cc = _gate_col(cr_ref, i)
        reach = _fox_reach(qs, k_ref, kmax_ref, cc, i) * LOG2E

        def alive(kb, carry):
            return (jnp.max(reach - cr_ref[kb][:, 0:1] * LOG2E - carry[0]) > FOX_DEAD2).astype(jnp.int32)

        def tile(kb, g, carry, masked):
            m, acc = carry
            k0 = pl.multiple_of(kb * t, t)
            sc = _fox_scores(qs, k_ref[pl.ds(k0, g * t), :], cc, _gate_row(cr_ref, kb, g), masked)
            m_new = jnp.maximum(m, jnp.ceil(jnp.max(sc, axis=-1, keepdims=True)))
            pb = jnp.exp2(sc - m_new).astype(BF16)
            acc = jnp.exp2(m - m_new) * acc + _dot(pb, v1_ref[pl.ds(k0, g * t), :])
            return m_new, acc

        init = (jnp.full((FOX_TQ, 1), NEG, F32), jnp.zeros((FOX_TQ, 2 * HEAD_DIM), F32))
        m, acc = _fox_walk(i, init, tile, alive)
        rl = 1.0 / acc[:, HEAD_DIM:HEAD_DIM + 1]
        o_ref[...] = acc[:, 0:HEAD_DIM] * rl
        ref_ref[...] = m
        rl_ref[...] = rl
        if nride:
            pl.when(jnp.logical_and(h == N_HEADS - 1, i == nq - 1))(wait)

    any_spec = pl.BlockSpec(memory_space=pl.ANY)
    ride_sems = [pltpu.SemaphoreType.DMA((3 * nride,)), pltpu.SemaphoreType.DMA((3 * nride,)),
                 pltpu.SemaphoreType.DMA((nride,))] if nride else []
    return pl.pallas_call(
        body, name="fox_fwd_gather" if nride else "fox_fwd", grid=(N_HEADS, nq),
        in_specs=[q_spec, k_spec, v_spec, gate_spec] + [any_spec] * nride,
        out_specs=[row_spec(HEAD_DIM), row_spec(1), row_spec(1)] + [any_spec] * nride,
        out_shape=[jax.ShapeDtypeStruct((N_HEADS, s, HEAD_DIM), F32),
                   jax.ShapeDtypeStruct((N_HEADS, s, 1), F32),
                   jax.ShapeDtypeStruct((N_HEADS, s, 1), F32)]
        + [jax.ShapeDtypeStruct((4,) + a.shape, a.dtype) for a in ride],
        scratch_shapes=[pltpu.VMEM((s, 2 * HEAD_DIM), BF16), pltpu.SMEM((1,), F32)] + ride_sems,
        compiler_params=_params(("arbitrary", "arbitrary")),
    )(qkv, qkv, qkv, c_row, *ride)


def fox_bwd(qkv, c_row, do, o, ref, rl, ride=()):
    s = qkv.shape[2]
    t = ATT_T
    nq = s // FOX_TQ
    q_spec, k_spec, v_spec, row_spec, gate_spec = _att_specs(s, 1, 2, 3)
    any_spec = pl.BlockSpec(memory_space=pl.ANY)
    nride = len(ride)

    def body(q_ref, k_ref, v_ref, cr_ref, do_ref, o_ref, ref_ref, rl_ref, *refs):
        ride_in, refs = refs[:nride], refs[nride:]
        dq_ref, dk_hbm, dv_hbm, dc_ref = refs[:4]
        ride_out, refs = refs[4:4 + nride], refs[4 + nride:]
        dk_acc, dv_acc, kmax_ref = refs[:3]
        h = pl.program_id(0)
        i = pl.program_id(1)
        if nride:
            start, wait = _device_exchange(_exchange_flows(ride_in, ride_out), *refs[3:])
            pl.when(jnp.logical_and(h == 0, i == 0))(start)

        @pl.when(i == 0)
        def _():
            dk_acc[...] = jnp.zeros_like(dk_acc)
            dv_acc[...] = jnp.zeros_like(dv_acc)
            dc_ref[...] = jnp.zeros_like(dc_ref)

        qs = q_ref[...] * 0.125
        ref = ref_ref[...]
        rl = rl_ref[...]
        dob = (do_ref[...].astype(F32) * rl).astype(BF16)
        delta = jnp.sum(o_ref[...] * dob.astype(F32), axis=-1, keepdims=True)
        cc = _gate_col(cr_ref, i)
        margin = _fox_reach(qs, k_ref, kmax_ref, cc, i) * LOG2E - ref

        def alive(kb, carry):
            return (jnp.max(margin - cr_ref[kb][:, 0:1] * LOG2E) > FOX_DEAD2).astype(jnp.int32)

        def tile(kb, g, carry, masked):
            dq, = carry
            k0 = pl.multiple_of(kb * t, t)
            k = k_ref[pl.ds(k0, g * t), :]
            sc = _fox_scores(qs, k, cc, _gate_row(cr_ref, kb, g), masked)
            wb = jnp.exp2(sc - ref).astype(BF16)
            ds = wb.astype(F32) * (_dot_nt(dob, v_ref[pl.ds(k0, g * t), :]) - delta)
            dsb = ds.astype(BF16)
            dk_acc[pl.ds(k0, g * t), :] += _dot_tn(dsb, qs)
            dv_acc[pl.ds(k0, g * t), :] += _dot_tn(wb, dob)
            dcs = -jnp.sum(ds, axis=0, keepdims=True)
            for n in range(g):
                dc_ref[kb + n] += dcs[:, n * t:(n + 1) * t]
            return (dq + _dot(dsb, k),)

        dq, = _fox_walk(i, (jnp.zeros((FOX_TQ, HEAD_DIM), F32),), tile, alive)
        dq_ref[...] = dq * 0.125

        @pl.when(i == nq - 1)
        def _():
            pltpu.sync_copy(dk_acc, dk_hbm.at[h])
            pltpu.sync_copy(dv_acc, dv_hbm.at[h])

        if nride:
            pl.when(jnp.logical_and(h == N_HEADS - 1, i == nq - 1))(wait)

    return pl.pallas_call(
        body, name="fox_bwd_exchange" if nride else "fox_bwd", grid=(N_HEADS, nq),
        in_specs=[q_spec, k_spec, v_spec,
                  gate_spec,
                  row_spec(HEAD_DIM), row_spec(HEAD_DIM), row_spec(1), row_spec(1)] + [any_spec] * nride,
        out_specs=[row_spec(HEAD_DIM), any_spec, any_spec,
                   gate_spec] + [any_spec] * nride,
        out_shape=[jax.ShapeDtypeStruct((N_HEADS, s, HEAD_DIM), F32),
                   jax.ShapeDtypeStruct((N_HEADS, s, HEAD_DIM), F32),
                   jax.ShapeDtypeStruct((N_HEADS, s, HEAD_DIM), F32),
                   jax.ShapeDtypeStruct((N_HEADS, s // t, 1, t), F32)] + _exchange_shapes(ride),
        scratch_shapes=[pltpu.VMEM((s, HEAD_DIM), F32), pltpu.VMEM((s, HEAD_DIM), F32), pltpu.SMEM((1,), F32)]
        + (_exchange_sems(nride) if nride else []),
        compiler_params=_params(("arbitrary", "arbitrary")),
    )(qkv, qkv, qkv, c_row, do, o, ref, rl, *ride)


def _sb_valid(nrows, ahead):
    row = lax.broadcasted_iota(jnp.int32, (nrows, ATT_T), 0)
    col = lax.broadcasted_iota(jnp.int32, (nrows, ATT_T), 1)
    return col + ahead < row


def _sb_band_valid(nsub, i):
    shape = (nsub * SB_SUB, SB_BAND)
    row = lax.broadcasted_iota(jnp.int32, shape, 0)
    col = lax.broadcasted_iota(jnp.int32, shape, 1)
    first = i * SB_TQ + (row - (row & (SB_SUB - 1)))
    valid = col < (row & (SB_SUB - 1)) + jnp.minimum(first, SB_BACK)
    return valid, first[:, 0:1] > SB_BACK


def _sb_logits(qs, k):
    z = _dot_nt(qs, k)
    sp = jnp.log(1.0 + jnp.exp(-jnp.abs(z)))
    return jnp.minimum(z, 0.0) - sp, -jnp.maximum(z, 0.0) - sp


def _sb_weights(ls, lm, run, valid):
    if valid is not None:
        lm = jnp.where(valid, lm, 0.0)
    n = lm.shape[1]
    row = lax.broadcasted_iota(jnp.int32, (n, n), 0)
    col = lax.broadcasted_iota(jnp.int32, (n, n), 1)
    later = (row > col).astype(BF16)
    hi, lo = _split2(lm)
    between = _dot(hi, later) + _dot(lo, later)
    if run is not None:
        between = run + between
    a = jnp.exp(ls + between)
    if valid is not None:
        a = jnp.where(valid, a, 0.0)
    return lm, a


def _sb_band_start(i, j):
    return pl.multiple_of(jnp.maximum(i * SB_TQ + j * SB_SUB - SB_BACK, 0), SB_SUB)


def _sb_tile(qs, k, run, valid):
    ls, lm = _sb_logits(qs, k)
    lm, a = _sb_weights(ls, lm, run, valid)
    return ls, lm, a


def _sb_band(i, qs_all, k_ref):
    nsub = qs_all.shape[0] // SB_SUB
    valid, open_left = _sb_band_valid(nsub, i)
    starts = [_sb_band_start(i, j) for j in range(nsub)]
    kwins = [k_ref[pl.ds(k0, SB_BAND), :] for k0 in starts]
    parts = [_sb_logits(qs_all[j * SB_SUB:(j + 1) * SB_SUB], kwins[j]) for j in range(nsub)]
    ls = jnp.concatenate([p[0] for p in parts], axis=0)
    lm, a = _sb_weights(ls, jnp.concatenate([p[1] for p in parts], axis=0), None, valid)
    return starts, kwins, ls, lm, a, valid, open_left


def _sb_suffix(g, run_g):
    n = g.shape[1]
    row = lax.broadcasted_iota(jnp.int32, (n, n), 0)
    col = lax.broadcasted_iota(jnp.int32, (n, n), 1)
    from_here = (row >= col).astype(BF16)
    hi, lo = _split2(g)
    out = _dot(hi, from_here) + _dot(lo, from_here)
    return out if run_g is None else run_g + out


def _sb_walk(i, carry, tile):
    def alive_of(c):
        return (jnp.max(c[0]) > SB_DEAD).astype(jnp.int32)

    def cond(state):
        n, alive = state[0], state[1]
        return jnp.logical_and(n < i, alive > 0)

    def step(state):
        n = state[0]
        c = tile(i - 1 - n, state[2:], False)
        return (n + 1, alive_of(c)) + tuple(c)

    out = lax.while_loop(cond, step, (jnp.int32(0), alive_of(carry)) + tuple(carry))
    return out[2:]


def _sb_specs(s):
    tq = SB_TQ
    q_spec = pl.BlockSpec((None, None, tq, HEAD_DIM), lambda h, i: (4, h, i, 0))
    k_spec = pl.BlockSpec((None, None, s, HEAD_DIM), lambda h, i: (5, h, 0, 0))
    v_spec = pl.BlockSpec((None, None, s, HEAD_DIM), lambda h, i: (6, h, 0, 0))
    row_spec = pl.BlockSpec((None, tq, HEAD_DIM), lambda h, i: (h, i, 0))
    band_spec = pl.BlockSpec((None, None, 1, 128), lambda h, i: (h, i, 0, 0))
    return tq, q_spec, k_spec, v_spec, row_spec, band_spec


def _sb_block(b, row0, tile, zero):
    t = ATT_T
    lo, hi, both = slice(row0, row0 + t), slice(row0 + t, row0 + 2 * t), slice(row0, row0 + 2 * t)
    c_hi = tile(2 * b + 1, hi, zero, 0)
    c_lo = tile(2 * b, lo, zero, 0)
    c_hi = tile(2 * b, hi, c_hi, None)
    carry = tuple(jnp.concatenate([x, y], axis=0) for x, y in zip(c_lo, c_hi))
    return _sb_walk(2 * b, carry, lambda kb, c, _: tile(kb, both, c, None))


def sb_fwd(qkv):
    s = qkv.shape[2]
    t = ATT_T
    tq, q_spec, k_spec, v_spec, row_spec, band_spec = _sb_specs(s)

    def body(q_ref, k_ref, v_ref, o_ref, band_ref, done_ref):
        i = pl.program_id(1)
        qs = q_ref[...] * 0.125
        starts, _, _, lm, a, _, open_left = _sb_band(i, qs, k_ref)
        ab = a.astype(BF16)
        for j, k0 in enumerate(starts):
            rows = slice(j * SB_SUB, (j + 1) * SB_SUB)
            o_ref[rows, :] = _dot(ab[rows], v_ref[pl.ds(k0, SB_BAND), :])
        worst = jnp.max(jnp.where(open_left, jnp.sum(lm, axis=-1, keepdims=True), NEG))
        done_ref[0] = (worst <= SB_DEAD).astype(jnp.int32)

        @pl.when(done_ref[0] == 0)
        def _():
            def tile(kb, rows, carry, ahead):
                run, acc = carry
                k0 = pl.multiple_of(kb * t, t)
                valid = None if ahead is None else _sb_valid(t, ahead)
                _, lm, a = _sb_tile(qs[rows], k_ref[pl.ds(k0, t), :], run, valid)
                acc = acc + _dot(a.astype(BF16), v_ref[pl.ds(k0, t), :])
                return run + jnp.sum(lm, axis=-1, keepdims=True), acc

            for n in range(tq // (2 * t)):
                _, acc = _sb_block(i * (tq // (2 * t)) + n, n * 2 * t, tile,
                                   (jnp.zeros((t, 1), F32), jnp.zeros((t, HEAD_DIM), F32)))
                o_ref[n * 2 * t:(n + 1) * 2 * t, :] = acc

        band_ref[...] = jnp.full(band_ref.shape, done_ref[0], jnp.int32).astype(F32)

    return pl.pallas_call(
        body, name="sb_fwd", grid=(N_HEADS, s // tq),
        in_specs=[q_spec, k_spec, v_spec],
        out_specs=[row_spec, band_spec],
        out_shape=[jax.ShapeDtypeStruct((N_HEADS, s, HEAD_DIM), F32),
                   jax.ShapeDtypeStruct((N_HEADS, s // tq, 1, 128), F32)],
        scratch_shapes=[pltpu.SMEM((1,), jnp.int32)],
        compiler_params=_params(("arbitrary", "arbitrary")),
    )(qkv, qkv, qkv)


def sb_bwd(qkv, do, o, band, ride=()):
    s = qkv.shape[2]
    t = ATT_T
    tq, q_spec, k_spec, v_spec, row_spec, band_spec = _sb_specs(s)
    nq = s // tq
    any_spec = pl.BlockSpec(memory_space=pl.ANY)
    nride = len(ride)

    def body(q_ref, k_ref, v_ref, do_ref, o_ref, band_ref, *refs):
        ride_in, refs = refs[:nride], refs[nride:]
        dq_ref, dk_hbm, dv_hbm = refs[:3]
        ride_out, refs = refs[3:3 + nride], refs[3 + nride:]
        dk_acc, dv_acc = refs[:2]
        h = pl.program_id(0)
        i = pl.program_id(1)
        if nride:
            start, wait = _device_exchange(_exchange_flows(ride_in, ride_out), *refs[2:])
            pl.when(jnp.logical_and(h == 0, i == 0))(start)

        @pl.when(i == 0)
        def _():
            dk_acc[...] = jnp.zeros_like(dk_acc)
            dv_acc[...] = jnp.zeros_like(dv_acc)

        qs_all = q_ref[...] * 0.125
        dob_all = do_ref[...]
        tot_all = jnp.sum(o_ref[...] * dob_all.astype(F32), axis=-1, keepdims=True)
        on_band = jnp.max(band_ref[...]) > 0.5

        def grads(qs, dob, tot, k, v, k0, run, run_g, valid):
            ls, lm, a = _sb_tile(qs, k, run, valid)
            ab = a.astype(BF16)
            g = ab.astype(F32) * _dot_nt(dob, v)
            g_left = tot - _sb_suffix(g, run_g)
            dz = g - jnp.exp(ls) * (g + g_left)
            if valid is not None:
                dz = jnp.where(valid, dz, 0.0)
            dzb = dz.astype(BF16)
            n = k.shape[0]
            dk_acc[pl.ds(k0, n), :] += _dot_tn(dzb, qs)
            dv_acc[pl.ds(k0, n), :] += _dot_tn(ab, dob)
            return dzb, lm, g

        @pl.when(on_band)
        def _():
            starts, kwins, ls, _, a, valid, _ = _sb_band(i, qs_all, k_ref)
            ab = a.astype(BF16)
            subs = [slice(j * SB_SUB, (j + 1) * SB_SUB) for j in range(len(starts))]
            vwins = [v_ref[pl.ds(k0, SB_BAND), :] for k0 in starts]
            g = ab.astype(F32) * jnp.concatenate([_dot_nt(dob_all[r], v) for r, v in zip(subs, vwins)], axis=0)
            dz = jnp.where(valid, g - jnp.exp(ls) * (g + (tot_all - _sb_suffix(g, None))), 0.0)
            dzb = dz.astype(BF16)
            for r, k0, k in zip(subs, starts, kwins):
                dq_ref[r, :] = _dot(dzb[r], k) * 0.125
                dk_acc[pl.ds(k0, SB_BAND), :] += _dot_tn(dzb[r], qs_all[r])
                dv_acc[pl.ds(k0, SB_BAND), :] += _dot_tn(ab[r], dob_all[r])

        @pl.when(jnp.logical_not(on_band))
        def _():
            def tile(kb, rows, carry, ahead):
                run, run_g, dq = carry
                k0 = pl.multiple_of(kb * t, t)
                k = k_ref[pl.ds(k0, t), :]
                valid = None if ahead is None else _sb_valid(t, ahead)
                dzb, lm, g = grads(qs_all[rows], dob_all[rows], tot_all[rows], k, v_ref[pl.ds(k0, t), :], k0,
                                   run, run_g, valid)
                return (run + jnp.sum(lm, axis=-1, keepdims=True),
                        run_g + jnp.sum(g, axis=-1, keepdims=True),
                        dq + _dot(dzb, k))

            zero = jnp.zeros((t, 1), F32)
            for n in range(tq // (2 * t)):
                _, _, dq = _sb_block(i * (tq // (2 * t)) + n, n * 2 * t, tile, (zero, zero, jnp.zeros((t, HEAD_DIM), F32)))
                dq_ref[n * 2 * t:(n + 1) * 2 * t, :] = dq * 0.125

        @pl.when(i == nq - 1)
        def _():
            pltpu.sync_copy(dk_acc, dk_hbm.at[h])
            pltpu.sync_copy(dv_acc, dv_hbm.at[h])

        if nride:
            pl.when(jnp.logical_and(h == N_HEADS - 1, i == nq - 1))(wait)

    return pl.pallas_call(
        body, name="sb_bwd_exchange" if nride else "sb_bwd", grid=(N_HEADS, nq),
        in_specs=[q_spec, k_spec, v_spec, row_spec, row_spec, band_spec] + [any_spec] * nride,
        out_specs=[row_spec, any_spec, any_spec] + [any_spec] * nride,
        out_shape=[jax.ShapeDtypeStruct((N_HEADS, s, HEAD_DIM), F32)] * 3 + _exchange_shapes(ride),
        scratch_shapes=[pltpu.VMEM((s, HEAD_DIM), F32), pltpu.VMEM((s, HEAD_DIM), F32)]
        + (_exchange_sems(nride) if nride else []),
        compiler_params=_params(("arbitrary", "arbitrary")),
    )(qkv, qkv, qkv, do, o, band, *ride)


def _branch_inputs(refs, br):
    ya_ref, yb_ref, yc_ref, yd_ref = refs
    if br == 1:
        return yb_ref[...]
    return _heads_to_lanes((ya_ref, None, yc_ref, yd_ref)[br])


def outproj_fwd(x, ya, yb, yc, yd, gates, bg, wout):
    s = x.shape[0]
    tm = min(ROW_T, s)

    def body(x_ref, ya_ref, yb_ref, yc_ref, yd_ref, gates_ref, bg_ref, w_ref, out_ref):
        pieces = []
        for br in range(4):
            cols = slice(br * D_BRANCH, (br + 1) * D_BRANCH)
            y = _branch_inputs((ya_ref, yb_ref, yc_ref, yd_ref), br)
            r = lax.rsqrt(jnp.mean(y * y, axis=-1, keepdims=True) + EPS)
            gt = gates_ref[:, cols]
            pieces.append((y * r * bg_ref[:, cols]) * (gt * _sigmoid(gt)))
        merged = jnp.concatenate(pieces, axis=1).astype(BF16)
        out_ref[...] = x_ref[...] + _dot(merged, w_ref[...])

    head_spec = pl.BlockSpec((N_HEADS, tm, HEAD_DIM), lambda i: (0, i, 0))
    return pl.pallas_call(
        body, name="outproj_fwd", grid=(s // tm,),
        in_specs=[pl.BlockSpec((tm, D_MODEL), lambda i: (i, 0)),
                  head_spec, pl.BlockSpec((tm, D_BRANCH), lambda i: (i, 0)), head_spec, head_spec,
                  pl.BlockSpec((tm, D_MODEL), lambda i: (i, 0)),
                  pl.BlockSpec((1, D_MODEL), lambda i: (0, 0)),
                  pl.BlockSpec((D_MODEL, D_MODEL), lambda i: (0, 0))],
        out_specs=pl.BlockSpec((tm, D_MODEL), lambda i: (i, 0)),
        out_shape=jax.ShapeDtypeStruct((s, D_MODEL), F32),
        compiler_params=_params(("arbitrary",)),
    )(x, ya, yb, yc, yd, gates, bg, wout)


def outproj_bwd(dout, ya, yb, yc, yd, gates, bg, wout):
    s = dout.shape[0]
    tm = min(ROW_T, s)

    def body(dout_ref, ya_ref, yb_ref, yc_ref, yd_ref, gates_ref, bg_ref, w_ref,
             dya_ref, dyb_ref, dyc_ref, dyd_ref, dgates_ref, dbg_ref, dw_ref):
        i = pl.program_id(0)

        @pl.when(i == 0)
        def _():
            dbg_ref[...] = jnp.zeros_like(dbg_ref)
            dw_ref[...] = jnp.zeros_like(dw_ref)

        doutb = dout_ref[...].astype(BF16)
        dmerged = _dot_nt(doutb, w_ref[...])
        pieces = []
        for br in range(4):
            cols = slice(br * D_BRANCH, (br + 1) * D_BRANCH)
            y = _branch_inputs((ya_ref, yb_ref, yc_ref, yd_ref), br)
            r = lax.rsqrt(jnp.mean(y * y, axis=-1, keepdims=True) + EPS)
            yn = y * r
            bgv = bg_ref[:, cols]
            gt = gates_ref[:, cols]
            sig = _sigmoid(gt)
            act = gt * sig
            n = yn * bgv
            pieces.append(n * act)
            dm = dmerged[:, cols]
            dn = dm * act
            dgates_ref[:, cols] = (dm * n * (sig * (1.0 + gt * (1.0 - sig)))).astype(BF16)
            dbg_ref[:, cols] += jnp.sum(dn * yn, axis=0, keepdims=True)
            u = dn * bgv
            dy = r * (u - yn * jnp.mean(yn * u, axis=-1, keepdims=True))
            if br == 1:
                dyb_ref[...] = dy
            else:
                dref = (dya_ref, None, dyc_ref, dyd_ref)[br]
                for hh in range(N_HEADS):
                    dref[hh] = dy[:, hh * HEAD_DIM:(hh + 1) * HEAD_DIM].astype(BF16)
        merged = jnp.concatenate(pieces, axis=1).astype(BF16)
        dw_ref[...] += _dot_tn(merged, doutb)

    head_spec = pl.BlockSpec((N_HEADS, tm, HEAD_DIM), lambda i: (0, i, 0))
    head_shape = jax.ShapeDtypeStruct((N_HEADS, s, HEAD_DIM), BF16)
    return pl.pallas_call(
        body, name="outproj_bwd", grid=(s // tm,),
        in_specs=[pl.BlockSpec((tm, D_MODEL), lambda i: (i, 0)),
                  head_spec, pl.BlockSpec((tm, D_BRANCH), lambda i: (i, 0)), head_spec, head_spec,
                  pl.BlockSpec((tm, D_MODEL), lambda i: (i, 0)),
                  pl.BlockSpec((1, D_MODEL), lambda i: (0, 0)),
                  pl.BlockSpec((D_MODEL, D_MODEL), lambda i: (0, 0))],
        out_specs=[head_spec, pl.BlockSpec((tm, D_BRANCH), lambda i: (i, 0)), head_spec, head_spec,
                   pl.BlockSpec((tm, D_MODEL), lambda i: (i, 0)),
                   pl.BlockSpec((1, D_MODEL), lambda i: (0, 0)),
                   pl.BlockSpec((D_MODEL, D_MODEL), lambda i: (0, 0))],
        out_shape=[head_shape, jax.ShapeDtypeStruct((s, D_BRANCH), F32), head_shape, head_shape,
                   jax.ShapeDtypeStruct((s, D_MODEL), BF16),
                   jax.ShapeDtypeStruct((1, D_MODEL), F32),
                   jax.ShapeDtypeStruct((D_MODEL, D_MODEL), F32)],
        compiler_params=_params(("arbitrary",)),
    )(dout, ya, yb, yc, yd, gates, bg, wout)


def final_loss(x, tgt, g):
    s = x.shape[0]
    tm = min(ROW_T, s)

    def body(x_ref, t_ref, g_ref, loss_ref, dx_ref, dg_ref):
        i = pl.program_id(0)

        @pl.when(i == 0)
        def _():
            loss_ref[...] = jnp.zeros_like(loss_ref)
            dg_ref[...] = jnp.zeros_like(dg_ref)

        xv = x_ref[...]
        gv = g_ref[...]
        r = lax.rsqrt(jnp.mean(xv * xv, axis=-1, keepdims=True) + EPS)
        xn = xv * r
        err = xn * gv - t_ref[...]
        loss_ref[...] += jnp.sum(err * err) * (0.5 / D_MODEL)
        dy = err * (1.0 / D_MODEL)
        u = dy * gv
        dx_ref[...] = r * (u - xn * jnp.mean(xn * u, axis=-1, keepdims=True))
        dg_ref[...] += jnp.sum(dy * xn, axis=0, keepdims=True)

    return pl.pallas_call(
        body, name="final_loss", grid=(s // tm,),
        in_specs=[pl.BlockSpec((tm, D_MODEL), lambda i: (i, 0)),
                  pl.BlockSpec((tm, D_MODEL), lambda i: (i, 0)),
                  pl.BlockSpec((1, D_MODEL), lambda i: (0, 0))],
        out_specs=[pl.BlockSpec((1, 128), lambda i: (0, 0)),
                   pl.BlockSpec((tm, D_MODEL), lambda i: (i, 0)),
                   pl.BlockSpec((1, D_MODEL), lambda i: (0, 0))],
        out_shape=[jax.ShapeDtypeStruct((1, 128), F32),
                   jax.ShapeDtypeStruct((s, D_MODEL), F32),
                   jax.ShapeDtypeStruct((1, D_MODEL), F32)],
        compiler_params=_params(("arbitrary",)),
    )(x, tgt, g)


def _rel_index():
    i = np.arange(A_TQ)[:, None]
    j = np.arange(A_BAND)[None, :]
    rel = np.clip(i - j + (A_BAND - A_TQ), -MAX_REL, MAX_REL) + MAX_REL
    dchunk = i // CHUNK + LOOKBACK - j // CHUNK
    valid = (dchunk >= 0) & (dchunk <= LOOKBACK)
    return jnp.asarray(np.where(valid, rel, -1).astype(np.int32))


def _layer_consts(p):
    tbias = relbias_tile(p["rel_bias"], _rel_index())
    return dict(
        norm_g=p["norm_g"].reshape(1, D_MODEL),
        v_gain=p["v_gain"].reshape(1, D_BRANCH),
        b_col=p["b_s"].reshape(N_HEADS, SG_CHUNK, 1),
        bg=p["branch_gain"].reshape(1, D_MODEL),
        tbias=tbias,
    )


def _gate_layout(fp, b_f, s):
    nb = s // 128
    ft = fp[:, :N_HEADS].T.reshape(N_HEADS * nb, 128)
    bcol = jnp.repeat(b_f, nb).reshape(N_HEADS * nb, 1)
    return ft, bcol


def layer_fwd(x, p, ride=()):
    s = x.shape[0]
    c = _layer_consts(p)
    h, qkv, kva, gates, uv, fp = inproj_fwd(x, c["norm_g"], p["wp"])
    ya, lse_a = mix_a_fwd(qkv, kva, c["tbias"])
    yb = mix_b_fwd(uv, c["v_gain"], p["w_s"], c["b_col"])
    ft, bcol = _gate_layout(fp, p["b_f"], s)
    c_row = fox_gate_fwd(ft, bcol).reshape(N_HEADS, s // ATT_T, 1, ATT_T)
    yc, ref_c, rl_c, *rode = fox_fwd(qkv, c_row, ride)
    yd, band_d = sb_fwd(qkv)
    out = outproj_fwd(x, ya, yb, yc, yd, gates, c["bg"], p["wout"])
    saved = dict(consts=c, x=x, h=h, qkv=qkv, gates=gates, uv=uv, kva=kva, ft=ft, bcol=bcol,
                 c_row=c_row, ya=ya, lse_a=lse_a, yb=yb, yc=yc, ref_c=ref_c, rl_c=rl_c, yd=yd, band_d=band_d)
    return out, saved, rode


def layer_bwd(dout, p, sv, exchange=False, upper_w_in=None, small_ride=None):
    s = dout.shape[0]
    c = sv["consts"]
    dya, dyb, dyc, dyd, dgates, dbg, dwout = outproj_bwd(
        dout, sv["ya"], sv["yb"], sv["yc"], sv["yd"], sv["gates"], c["bg"], p["wout"])
    dqa, dka, dva, dt = mix_a_bwd(sv["qkv"], sv["kva"], c["tbias"], dya, sv["ya"], sv["lse_a"])
    drel = relbias_grad(dt, _rel_index())[:N_HEADS, :2 * MAX_REL + 1]
    duv, dws, dbs, dvgain = mix_b_bwd(sv["uv"], c["v_gain"], p["w_s"], c["b_col"], dyb)
    ride = [dwout.astype(BF16).reshape(4, D_BRANCH, D_MODEL)] if exchange else []
    if upper_w_in is not None:
        ride.append(upper_w_in)
    dqc, dkc, dvc, dc, *rode = fox_bwd(sv["qkv"], sv["c_row"], dyc, sv["yc"], sv["ref_c"], sv["rl_c"], ride)
    dft, dbf = fox_gate_bwd(sv["ft"], sv["bcol"], dc.reshape(N_HEADS * (s // 128), 128))
    dfp = jnp.pad(dft.reshape(N_HEADS, s).T, ((0, 0), (0, 128 - N_HEADS)))
    grads = dict(b_f=dbf[:N_HEADS, 0], rel_bias=drel, w_s=dws, b_s=dbs.reshape(N_HEADS, SG_CHUNK),
                 v_gain=dvgain.reshape(D_BRANCH), branch_gain=dbg.reshape(4, D_BRANCH), wout=dwout)
    dqd, dkd, dvd, *small_parts = sb_bwd(sv["qkv"], dyd, sv["yd"], sv["band_d"], small_ride(grads) if small_ride else ())
    dp, dx, dnorm = inproj_bwd((dqa, dka, dva, dqc, dkc, dvc, dqd, dkd, dvd), dgates, duv, dfp,
                               p["wp"], sv["x"], c["norm_g"], dout)
    grads["norm_g"] = dnorm.reshape(D_MODEL)
    if small_ride:
        top, = inproj_wgrad(sv["h"], dp, 0)
        grads["w_in_shards"], grads["w_in_top_parts"] = inproj_wgrad(sv["h"], dp, 1, [top])
        grads["small_parts"] = small_parts[0]
    else:
        grads["w_in_shards"], = inproj_wgrad(sv["h"], dp)
    if exchange:
        grads["w_out_parts"] = rode[0]
    return dx, grads, (rode[1] if upper_w_in is not None else None)


def local_step(x, tgt, layers, final_g, next_shards=None):
    layers = list(layers)
    saved = []
    cur = x
    for l, p in enumerate(layers):
        ride = next_shards[l] if next_shards is not None and l + 1 < len(layers) else ()
        cur, sv, rode = layer_fwd(cur, p, ride)
        saved.append(sv)
        if ride:
            layers[l + 1] = dict(layers[l + 1], wp=pack_w_in(rode[0][None])[0], wout=rode[1].reshape(D_MODEL, D_MODEL))
    loss, dcur, dfinal = final_loss(cur, tgt, final_g.reshape(1, D_MODEL))
    grads = [None] * len(layers)
    for l in reversed(range(len(layers))):
        exchange = next_shards is not None
        upper = grads[l + 1]["w_in_shards"] if exchange and l + 1 < len(layers) else None
        small_ride = None
        if exchange and l == 0:
            def small_ride(g0, above=tuple(grads[1:])):
                stacked = {k: jnp.stack([g[k] for g in (g0,) + above]) for k in SMALL_EARLY if k != "final_g"}
                return [_pack([stacked.get(k, dfinal.reshape(D_MODEL)) for k in SMALL_EARLY])]
        dcur, grads[l], got = layer_bwd(dcur, layers[l], saved[l], exchange, upper, small_ride)
        if upper is not None:
            grads[l + 1]["w_in_parts"] = got
    return loss[0, 0], dcur, grads, dfinal.reshape(D_MODEL)


def _chip_gather(pairs, send_sems, recv_sems, loc_sems):
    x, y, c = lax.axis_index("x"), lax.axis_index("y"), lax.axis_index("c")
    me = 2 * x + y
    chips = [(1 - x, y), (x, 1 - y), (1 - x, 1 - y)]
    npair = len(pairs)

    def local():
        return [pltpu.make_async_copy(src, dst(me), loc_sems.at[n]) for n, (src, dst) in enumerate(pairs)]

    def remote(j, n, slot):
        src, dst = pairs[n]
        return pltpu.make_async_remote_copy(
            src_ref=src, dst_ref=dst(slot), send_sem=send_sems.at[npair * j + n], recv_sem=recv_sems.at[npair * j + n],
            device_id=(chips[j][0], chips[j][1], c), device_id_type=MESH)

    def start():
        for cp in local():
            cp.start()
        for j in range(3):
            for n in range(npair):
                remote(j, n, me).start()

    def wait():
        for j in range(3):
            for n in range(npair):
                remote(j, n, 2 * chips[j][0] + chips[j][1]).wait_recv()
        for j in range(3):
            for n in range(npair):
                remote(j, n, me).wait_send()
        for cp in local():
            cp.wait()

    return start, wait


def gather_weights(w_in, w_out, gains):
    depth = w_in.shape[0]

    def body(in_ref, out_ref, g_ref, oin_ref, oout_ref, og_ref, send_sems, recv_sems, loc_sems):
        pairs = [(in_ref, lambda s: oin_ref.at[:, s]), (out_ref, lambda s: oout_ref.at[:, s]), (g_ref, lambda s: og_ref.at[s])]
        start, wait = _chip_gather(pairs, send_sems, recv_sems, loc_sems)
        start()
        wait()

    any_spec = pl.BlockSpec(memory_space=pl.ANY)
    return pl.pallas_call(
        body, name="gather_weights",
        in_specs=[any_spec] * 3, out_specs=[any_spec] * 3,
        out_shape=[jax.ShapeDtypeStruct((depth, 4) + w_in.shape[1:], w_in.dtype),
                   jax.ShapeDtypeStruct((depth, 4) + w_out.shape[1:], w_out.dtype),
                   jax.ShapeDtypeStruct((4,) + gains.shape, gains.dtype)],
        scratch_shapes=[pltpu.SemaphoreType.DMA((9,)), pltpu.SemaphoreType.DMA((9,)), pltpu.SemaphoreType.DMA((3,))],
    )(w_in, w_out, gains)


def pack_w_in(shards):
    depth = shards.shape[0]
    tr = 256

    def body(s_ref, o_ref):
        full = jnp.concatenate([s_ref[n] for n in range(4)], axis=1)
        o_ref[...] = jnp.concatenate([full[:, :SEC_D_Q], full[:, SEC_D_Q + N_HEADS:], full[:, SEC_D_Q:SEC_D_Q + N_HEADS],
                                      jnp.zeros((tr, N_PACK - N_IN), BF16)], axis=1)

    return pl.pallas_call(
        body, name="pack_w_in", grid=(depth, D_MODEL // tr),
        in_specs=[pl.BlockSpec((None, 4, tr, N_SHARD), lambda l, r: (l, 0, r, 0))],
        out_specs=pl.BlockSpec((None, tr, N_PACK), lambda l, r: (l, r, 0)),
        out_shape=jax.ShapeDtypeStruct((depth, D_MODEL, N_PACK), BF16),
        compiler_params=_params(("arbitrary", "arbitrary")),
    )(shards)


def _device_exchange(flows, send_sems, recv_sems, loc_sems):
    x, y, c = lax.axis_index("x"), lax.axis_index("y"), lax.axis_index("c")
    me_chip = 2 * x + y
    me = 4 * x + 2 * y + c
    peers = [(x, y, 1 - c)]
    for px, py in [(1 - x, y), (x, 1 - y), (1 - x, 1 - y)]:
        peers += [(px, py, c), (px, py, 1 - c)]
    nflow = len(flows)

    def local():
        return [pltpu.make_async_copy(src(me_chip), dst(me), loc_sems.at[f]) for f, (src, dst) in enumerate(flows)]

    def copies(n, chip, slot):
        return [pltpu.make_async_remote_copy(src_ref=src(chip), dst_ref=dst(slot), send_sem=send_sems.at[nflow * n + f],
                                             recv_sem=recv_sems.at[nflow * n + f], device_id=peers[n], device_id_type=MESH)
                for f, (src, dst) in enumerate(flows)]

    def start():
        for cp in local():
            cp.start()
        for n, (px, py, _) in enumerate(peers):
            for cp in copies(n, 2 * px + py, me):
                cp.start()

    def wait():
        for n, (px, py, pc) in enumerate(peers):
            for cp in copies(n, me_chip, 4 * px + 2 * py + pc):
                cp.wait_recv()
        for n, (px, py, _) in enumerate(peers):
            for cp in copies(n, 2 * px + py, me):
                cp.wait_send()
        for cp in local():
            cp.wait()

    return start, wait


def _exchange_flows(srcs, dsts):
    return [((lambda s, src=src: src.at[s]) if src.shape[0] == 4 else (lambda s, src=src: src),
             lambda d, dst=dst: dst.at[d]) for src, dst in zip(srcs, dsts)]


def _exchange_shapes(arrays):
    return [jax.ShapeDtypeStruct((8,) + (a.shape[1:] if a.shape[0] == 4 else a.shape), a.dtype) for a in arrays]


def _exchange_sems(n):
    return [pltpu.SemaphoreType.DMA((7 * n,)), pltpu.SemaphoreType.DMA((7 * n,)), pltpu.SemaphoreType.DMA((n,))]


def exchange_grads(*arrays):
    n = len(arrays)

    def body(*refs):
        start, wait = _device_exchange(_exchange_flows(refs[:n], refs[n:2 * n]), *refs[2 * n:])
        start()
        wait()

    any_spec = pl.BlockSpec(memory_space=pl.ANY)
    return pl.pallas_call(
        body, name="exchange_grads",
        in_specs=[any_spec] * n, out_specs=[any_spec] * n, out_shape=_exchange_shapes(arrays),
        scratch_shapes=_exchange_sems(n),
    )(*arrays)


def adamw_reduce(parts, w, m, v, name, tr):
    rows, width = w.shape
    steps = [p.shape[1] // tr for p in parts]
    offs = [sum(steps[:n]) for n in range(len(parts))]
    c1 = 1.0 - ADAM_B1 ** ADAM_STEP
    c2 = 1.0 - ADAM_B2 ** ADAM_STEP

    def body(*refs):
        p_refs = refs[:len(parts)]
        w_ref, m_ref, v_ref, g_ref, d_ref, nm_ref, nv_ref = refs[len(parts):]
        i = pl.program_id(0)
        p = p_refs[0][...]
        for n in range(1, len(parts)):
            p = jnp.where(i >= offs[n], p_refs[n][...], p)
        g = p[0].astype(F32)
        for n in range(1, 8):
            g = g + p[n].astype(F32)
        g_ref[...] = g
        nm = ADAM_B1 * m_ref[...] + (1.0 - ADAM_B1) * g
        nv = ADAM_B2 * v_ref[...] + (1.0 - ADAM_B2) * (g * g)
        nm_ref[...] = nm
        nv_ref[...] = nv
        d_ref[...] = -ADAM_LR * ((nm / c1) / (jnp.sqrt(nv / c2) + ADAM_EPS) + ADAM_WD * w_ref[...])

    spec = pl.BlockSpec((tr, width), lambda i: (i, 0))
    shape = jax.ShapeDtypeStruct((rows, width), F32)
    return pl.pallas_call(
        body, name=name, grid=(rows // tr,),
        in_specs=[pl.BlockSpec((8, tr, width), lambda i, n=n: (0, jnp.clip(i - offs[n], 0, steps[n] - 1), 0))
                  for n in range(len(parts))] + [spec, spec, spec],
        out_specs=[spec] * 4, out_shape=[shape] * 4,
        compiler_params=_params(("arbitrary",)),
    )(*parts, w, m, v)


SMALL_EARLY = ("b_f", "rel_bias", "w_s", "b_s", "v_gain", "final_g")
WEIGHTS = ("norm_g", "w_in", "b_f", "rel_bias", "w_s", "b_s", "v_gain", "branch_gain", "w_out", "final_g")
PACK_ROW_TILE = 512


def _rows_of(shape):
    return -(-int(np.prod(shape)) // 128)


def _pack(leaves, tile=PACK_ROW_TILE):
    parts = []
    for a in leaves:
        flat = a.reshape(-1).astype(F32)
        parts.append(jnp.pad(flat, (0, _rows_of(a.shape) * 128 - flat.shape[0])))
    flat = jnp.concatenate(parts)
    rows = flat.shape[0] // 128
    total = -(-rows // tile) * tile
    return jnp.pad(flat, (0, (total - rows) * 128)).reshape(total, 128)


def _unpack(slab, shapes):
    out, row = [], 0
    for shp in shapes:
        n = int(np.prod(shp))
        r = _rows_of(shp)
        out.append(slab[row:row + r].reshape(-1)[:n].reshape(shp))
        row += r
    return out


def kernel(x, norm_g, w_in, b_f, rel_bias, w_s, b_s, v_gain, branch_gain, w_out, final_g, loss_target, m_norm_g, m_w_in, m_b_f, m_rel_bias, m_w_s, m_b_s, m_v_gain, m_branch_gain, m_w_out, m_final_g, v_norm_g, v_w_in, v_b_f, v_rel_bias, v_w_s, v_b_s, v_v_gain, v_branch_gain, v_w_out, v_final_g):
    depth = norm_g.shape[0]
    weights = dict(norm_g=norm_g, w_in=w_in, b_f=b_f, rel_bias=rel_bias, w_s=w_s, b_s=b_s, v_gain=v_gain,
                   branch_gain=branch_gain, w_out=w_out, final_g=final_g)
    mom1 = dict(norm_g=m_norm_g, w_in=m_w_in, b_f=m_b_f, rel_bias=m_rel_bias, w_s=m_w_s, b_s=m_b_s,
                v_gain=m_v_gain, branch_gain=m_branch_gain, w_out=m_w_out, final_g=m_final_g)
    mom2 = dict(norm_g=v_norm_g, w_in=v_w_in, b_f=v_b_f, rel_bias=v_rel_bias, w_s=v_w_s, b_s=v_b_s,
                v_gain=v_v_gain, branch_gain=v_branch_gain, w_out=v_w_out, final_g=v_final_g)

    wf = jnp.pad(branch_gain.reshape(-1), (0, 8 * 128 - branch_gain.size)).reshape(8, 128)
    w_in_b, w_out_b = w_in.astype(BF16), w_out.astype(BF16)
    w_in_shards, w_out_shards, gf = gather_weights(w_in_b[:1], w_out_b[:1], wf)
    bg_full = gf.reshape(4, -1)[:, :branch_gain.size].reshape((4,) + branch_gain.shape)
    bg_full = jnp.moveaxis(bg_full, 0, 2).reshape(depth, 4, D_BRANCH)

    layers = [dict(norm_g=norm_g[l], b_f=b_f[l], rel_bias=rel_bias[l], w_s=w_s[l],
                   b_s=b_s[l], v_gain=v_gain[l], branch_gain=bg_full[l]) for l in range(depth)]
    layers[0].update(wp=pack_w_in(w_in_shards)[0], wout=w_out_shards.reshape(D_MODEL, D_MODEL))
    next_shards = [(w_in_b[l + 1], w_out_b[l + 1]) for l in range(depth - 1)]

    loss_part, grad_x, lgrads, dfinal = local_step(x[0], loss_target[0], layers, final_g, next_shards)
    loss = lax.psum(loss_part, ("x", "y", "c"))

    stack = lambda k: jnp.stack([g[k] for g in lgrads])
    d_gain = jnp.moveaxis(stack("branch_gain").reshape(depth, 4, 4, HEAD_DIM), 2, 0).reshape(4, -1)
    d_gain = jnp.pad(d_gain, ((0, 0), (0, 8 * 128 - d_gain.shape[1]))).reshape(4, 8, 128)
    parts_in, parts_gain, parts_norm = exchange_grads(lgrads[0]["w_in_shards"], d_gain, _pack([stack("norm_g")], 16))
    parts = dict(w_in=[lgrads[0]["w_in_top_parts"], parts_in] + [g["w_in_parts"] for g in lgrads[1:]],
                 w_out=[g["w_out_parts"] for g in lgrads])

    outs = {}
    tags = ("grad", "delta", "new_m", "new_v")
    for k in ("w_in", "w_out"):
        rows = depth * weights[k].shape[1]
        flat = lambda a: a.reshape(rows, a.shape[-1])
        res = adamw_reduce(parts[k], flat(weights[k]), flat(mom1[k]), flat(mom2[k]), "adamw_" + k, 256)
        for tag, a in zip(tags, res):
            outs[tag, k] = a.reshape(weights[k].shape)
    gain8 = lambda a: jnp.pad(a.reshape(-1), (0, 8 * 128 - a.size)).reshape(8, 128)
    res = adamw_reduce([parts_gain], gain8(branch_gain), gain8(m_branch_gain), gain8(v_branch_gain), "adamw_gain", 8)
    for tag, a in zip(tags, res):
        outs[tag, "branch_gain"] = a.reshape(-1)[:branch_gain.size].reshape(branch_gain.shape)
    for names, parts_small, tile in ((SMALL_EARLY, lgrads[0]["small_parts"], PACK_ROW_TILE), (("norm_g",), parts_norm, 16)):
        pack_small = lambda d: _pack([d[k] for k in names], tile)
        res = adamw_reduce([parts_small], pack_small(weights), pack_small(mom1), pack_small(mom2),
                           "adamw_" + names[0], tile)
        for tag, slab in zip(tags, res):
            for k, a in zip(names, _unpack(slab, [weights[k].shape for k in names])):
                outs[tag, k] = a
    result = [loss, grad_x[None]]
    for tag in ("grad", "delta", "new_m", "new_v"):
        result += [outs[tag, k] for k in WEIGHTS]
    return tuple(result)
```

```python
import jax
import jax.numpy as jnp
import numpy as np
from jax import lax
from jax.experimental import pallas as pl
from jax.experimental.pallas import tpu as pltpu

F32 = jnp.float32
BF16 = jnp.bfloat16
MESH = pl.DeviceIdType.MESH

D_MODEL = 1024
D_BRANCH = 256
N_HEADS = 4
HEAD_DIM = 64
CHUNK = 64
LOOKBACK = 8
MAX_REL = 128
SG_CHUNK = 128
EPS = 1e-6
N_IN = 3844
N_PACK = 3968
F_COL = 3840
N_SHARD = 961
NEG = -1e30

A_TQ = 128
A_BAND = A_TQ + LOOKBACK * CHUNK
REL_LO = MAX_REL - (CHUNK - 1)
REL_HI = 2 * MAX_REL + 1
A_PAD = LOOKBACK * CHUNK
A_QB = 1024
ATT_T = 256
FOX_TQ = 512
FOX_WIDE = 4
FOX_DEAD2 = -136.0
LOG2E = 1.4426950408889634
SB_TQ = 1024
SB_SUB = 128
SB_BACK = 256
SB_BAND = SB_SUB + SB_BACK
SB_DEAD = -110.0
ROW_T = 512
VMEM_LIMIT = 56 * 1024 * 1024

ADAM_LR = 0.001
ADAM_B1 = 0.9
ADAM_B2 = 0.999
ADAM_EPS = 1e-08
ADAM_WD = 0.01
ADAM_STEP = 10

SEC_A_Q, SEC_A_K, SEC_A_V, SEC_A_G = 0, 256, 512, 768
SEC_B_U, SEC_B_V, SEC_B_G = 1024, 1280, 1536
SEC_C_Q, SEC_C_K, SEC_C_V, SEC_C_G = 1792, 2048, 2304, 2560
SEC_D_Q, SEC_D_K, SEC_D_V, SEC_D_G = 2816, 3072, 3328, 3584
QKV_SECS = (SEC_A_Q, SEC_C_Q, SEC_C_K, SEC_C_V, SEC_D_Q, SEC_D_K, SEC_D_V)
GATE_SECS = (SEC_A_G, SEC_B_G, SEC_C_G, SEC_D_G)


def _dot(a, b):
    return jnp.dot(a, b, preferred_element_type=F32)


def _dot_nt(a, b):
    return lax.dot_general(a, b, (((1,), (1,)), ((), ())), preferred_element_type=F32)


def _dot_tn(a, b):
    return lax.dot_general(a, b, (((0,), (0,)), ((), ())), preferred_element_type=F32)


def _split2(x):
    hi = x.astype(BF16)
    lo = (x - hi.astype(F32)).astype(BF16)
    return hi, lo


def _split3(x):
    hi = x.astype(BF16)
    r = x - hi.astype(F32)
    mid = r.astype(BF16)
    lo = (r - mid.astype(F32)).astype(BF16)
    return hi, mid, lo


def _sigmoid(x):
    return 1.0 / (1.0 + jnp.exp(-x))


def _params(sem=None, vmem=VMEM_LIMIT):
    return pltpu.CompilerParams(dimension_semantics=sem, vmem_limit_bytes=vmem)


def _heads_to_lanes(ref):
    return jnp.concatenate([ref[h] for h in range(N_HEADS)], axis=1)


def inproj_fwd(x, g, wp):
    s = x.shape[0]
    tm = A_PAD

    def body(x_ref, g_ref, w_ref, h_ref, qkv_ref, kva_ref, gates_ref, uv_ref, f_ref):
        xv = x_ref[...]
        r = lax.rsqrt(jnp.mean(xv * xv, axis=-1, keepdims=True) + EPS)
        h = (xv * r * g_ref[...]).astype(BF16)
        h_ref[...] = h
        for n, off in enumerate(QKV_SECS):
            p = _dot(h, w_ref[:, off:off + D_BRANCH])
            for hh in range(N_HEADS):
                qkv_ref[n, hh] = p[:, hh * HEAD_DIM:(hh + 1) * HEAD_DIM].astype(BF16)
        for n, off in enumerate((SEC_A_K, SEC_A_V)):
            p = _dot(h, w_ref[:, off:off + D_BRANCH])
            for hh in range(N_HEADS):
                kva_ref[n, hh] = p[:, hh * HEAD_DIM:(hh + 1) * HEAD_DIM].astype(BF16)
        for n, off in enumerate(GATE_SECS):
            gates_ref[:, n * D_BRANCH:(n + 1) * D_BRANCH] = _dot(h, w_ref[:, off:off + D_BRANCH])
        uv_ref[...] = _dot(h, w_ref[:, SEC_B_U:SEC_B_U + 2 * D_BRANCH])
        f_ref[...] = _dot(h, w_ref[:, F_COL:F_COL + 128])

    return pl.pallas_call(
        body, name="inproj_fwd", grid=(s // tm,),
        in_specs=[pl.BlockSpec((tm, D_MODEL), lambda i: (i, 0)),
                  pl.BlockSpec((1, D_MODEL), lambda i: (0, 0)),
                  pl.BlockSpec((D_MODEL, N_PACK), lambda i: (0, 0))],
        out_specs=[pl.BlockSpec((tm, D_MODEL), lambda i: (i, 0)),
                   pl.BlockSpec((len(QKV_SECS), N_HEADS, tm, HEAD_DIM), lambda i: (0, 0, i, 0)),
                   pl.BlockSpec((2, N_HEADS, tm, HEAD_DIM), lambda i: (0, 0, i + 1, 0)),
                   pl.BlockSpec((tm, D_MODEL), lambda i: (i, 0)),
                   pl.BlockSpec((tm, 2 * D_BRANCH), lambda i: (i, 0)),
                   pl.BlockSpec((tm, 128), lambda i: (i, 0))],
        out_shape=[jax.ShapeDtypeStruct((s, D_MODEL), BF16),
                   jax.ShapeDtypeStruct((len(QKV_SECS), N_HEADS, s, HEAD_DIM), BF16),
                   jax.ShapeDtypeStruct((2, N_HEADS, s + tm, HEAD_DIM), BF16),
                   jax.ShapeDtypeStruct((s, D_MODEL), F32),
                   jax.ShapeDtypeStruct((s, 2 * D_BRANCH), F32),
                   jax.ShapeDtypeStruct((s, 128), F32)],
        compiler_params=_params(("arbitrary",)),
    )(x, g, wp)


def inproj_bwd(dqkv, dgates, duv, dfp, wp, x, g, dres):
    s = x.shape[0]
    tm = A_PAD

    def body(*refs):
        dq_refs = refs[:9]
        dgates_ref, duv_ref, dfp_ref, w_ref, x_ref, g_ref, dres_ref, dp_ref, dx_ref, dg_ref = refs[9:]
        i = pl.program_id(0)
        a_q, a_k, a_v, c_q, c_k, c_v, d_q, d_k, d_v = [_heads_to_lanes(r).astype(BF16) for r in dq_refs]
        dgt = dgates_ref[...]
        duv_b = duv_ref[...].astype(BF16)
        dp = jnp.concatenate(
            [a_q, a_k, a_v, dgt[:, 0:256], duv_b, dgt[:, 256:512], c_q, c_k, c_v, dgt[:, 512:768],
             d_q, d_k, d_v, dgt[:, 768:1024], dfp_ref[...].astype(BF16)], axis=1)
        dp_ref[...] = dp
        dh = _dot_nt(dp, w_ref[...])
        xv = x_ref[...]
        r = lax.rsqrt(jnp.mean(xv * xv, axis=-1, keepdims=True) + EPS)
        xn = xv * r
        u = dh * g_ref[...]
        dx_ref[...] = dres_ref[...] + r * (u - xn * jnp.mean(xn * u, axis=-1, keepdims=True))

        @pl.when(i == 0)
        def _():
            dg_ref[...] = jnp.zeros_like(dg_ref)

        dg_ref[...] += jnp.sum(dh * xn, axis=0, keepdims=True)

    head_spec = pl.BlockSpec((N_HEADS, tm, HEAD_DIM), lambda i: (0, i, 0))
    padded_spec = pl.BlockSpec((N_HEADS, tm, HEAD_DIM), lambda i: (0, i + 1, 0))
    return pl.pallas_call(
        body, name="inproj_bwd", grid=(s // tm,),
        in_specs=[head_spec, padded_spec, padded_spec] + [head_spec] * 6 + [
            pl.BlockSpec((tm, D_MODEL), lambda i: (i, 0)),
            pl.BlockSpec((tm, 2 * D_BRANCH), lambda i: (i, 0)),
            pl.BlockSpec((tm, 128), lambda i: (i, 0)),
            pl.BlockSpec((D_MODEL, N_PACK), lambda i: (0, 0)),
            pl.BlockSpec((tm, D_MODEL), lambda i: (i, 0)),
            pl.BlockSpec((1, D_MODEL), lambda i: (0, 0)),
            pl.BlockSpec((tm, D_MODEL), lambda i: (i, 0))],
        out_specs=[pl.BlockSpec((tm, N_PACK), lambda i: (i, 0)),
                   pl.BlockSpec((tm, D_MODEL), lambda i: (i, 0)),
                   pl.BlockSpec((1, D_MODEL), lambda i: (0, 0))],
        out_shape=[jax.ShapeDtypeStruct((s, N_PACK), BF16),
                   jax.ShapeDtypeStruct((s, D_MODEL), F32),
                   jax.ShapeDtypeStruct((1, D_MODEL), F32)],
        compiler_params=_params(("arbitrary",)),
    )(*dqkv, dgates, duv, dfp, wp, x, g, dres)


def inproj_wgrad(h, dp, half=None, ride=()):
    s, m = h.shape
    tm = min(4 * ROW_T, s)
    tmm = 256
    nsteps = s // tm
    ntile = m // tmm if half is None else m // tmm // 2
    first = 0 if half is None else half * ntile
    nride = len(ride)

    def body(a_ref, b_ref, *refs):
        ride_in, o_ref = refs[:nride], refs[nride]
        ride_out, acc_ref = refs[nride + 1:2 * nride + 1], refs[2 * nride + 1]
        k = pl.program_id(1)
        if nride:
            j = pl.program_id(0)
            start, wait = _device_exchange(_exchange_flows(ride_in, ride_out), *refs[2 * nride + 2:])
            pl.when(jnp.logical_and(j == 0, k == 0))(start)

        @pl.when(k == 0)
        def _():
            acc_ref[...] = jnp.zeros_like(acc_ref)

        acc_ref[...] += _dot_tn(a_ref[...], b_ref[...])

        @pl.when(k == nsteps - 1)
        def _():
            acc = acc_ref[...]
            full = jnp.concatenate([acc[:, :SEC_D_Q], acc[:, F_COL:F_COL + N_HEADS], acc[:, SEC_D_Q:F_COL]], axis=1)
            for n in range(4):
                o_ref[n] = full[:, n * N_SHARD:(n + 1) * N_SHARD].astype(BF16)

        if nride:
            pl.when(jnp.logical_and(j == ntile - 1, k == nsteps - 1))(wait)

    any_spec = pl.BlockSpec(memory_space=pl.ANY)
    return pl.pallas_call(
        body, name="inproj_wgrad_exchange" if nride else "inproj_wgrad", grid=(ntile, nsteps),
        in_specs=[pl.BlockSpec((tm, tmm), lambda j, k: (k, first + j)),
                  pl.BlockSpec((tm, N_PACK), lambda j, k: (k, 0))] + [any_spec] * nride,
        out_specs=[pl.BlockSpec((4, tmm, N_SHARD), lambda j, k: (0, j, 0))] + [any_spec] * nride,
        out_shape=[jax.ShapeDtypeStruct((4, ntile * tmm, N_SHARD), BF16)] + _exchange_shapes(ride),
        scratch_shapes=[pltpu.VMEM((tmm, N_PACK), F32)] + (_exchange_sems(nride) if nride else []),
        compiler_params=_params(("arbitrary", "arbitrary")),
    )(h, dp, *ride)


def _a_specs(s):
    nq = s // A_QB
    per = A_QB // A_PAD
    q_spec = pl.BlockSpec((None, None, A_QB, HEAD_DIM), lambda h, i: (0, h, jnp.minimum(i, nq - 1), 0))
    kv_specs = [pl.BlockSpec((None, None, A_PAD, HEAD_DIM),
                             lambda h, i, n=n, m=m: (n, h, jnp.minimum(per * i + m, per * nq), 0))
                for n in range(2) for m in range(per + 1)]
    t_spec = pl.BlockSpec((None, A_TQ, A_BAND), lambda h, i: (h, 0, 0))
    return nq, q_spec, kv_specs, t_spec


def _a_window(refs, i):
    first = refs[0][...]
    return jnp.concatenate([jnp.where(i > 0, first, jnp.zeros_like(first))] + [r[...] for r in refs[1:]], axis=0)


def _a_scores(q_ref, k, t_ref, i, j):
    rows = slice(j * A_TQ, (j + 1) * A_TQ)
    qs = q_ref[rows, :] * 0.125
    kj = k[j * A_TQ:j * A_TQ + A_BAND, :]
    sc = _dot_nt(qs, kj) + t_ref[...]
    col = lax.broadcasted_iota(jnp.int32, (A_TQ, A_BAND), 1)
    sc = jnp.where(col >= A_PAD - i * A_QB - j * A_TQ, sc, NEG)
    return rows, qs, kj, sc


def mix_a_fwd(qkv, kva, tbias):
    s = qkv.shape[2]
    nq, q_spec, kv_specs, t_spec = _a_specs(s)
    nwin = len(kv_specs) // 2

    def body(*refs):
        q_ref, t_ref, o_ref, lse_ref = refs[0], refs[1 + 2 * nwin], refs[2 + 2 * nwin], refs[3 + 2 * nwin]
        i = pl.program_id(1)
        k = _a_window(refs[1:1 + nwin], i)
        v = _a_window(refs[1 + nwin:1 + 2 * nwin], i)
        for j in range(A_QB // A_TQ):
            rows, _, _, sc = _a_scores(q_ref, k, t_ref, i, j)
            m = jnp.max(sc, axis=-1, keepdims=True)
            p = jnp.exp(sc - m)
            l = jnp.sum(p, axis=-1, keepdims=True)
            o_ref[rows, :] = _dot(p.astype(BF16), v[j * A_TQ:j * A_TQ + A_BAND, :]) / l
            lse_ref[rows, :] = m + jnp.log(l)

    return pl.pallas_call(
        body, name="mix_a_fwd", grid=(N_HEADS, nq),
        in_specs=[q_spec] + kv_specs + [t_spec],
        out_specs=[pl.BlockSpec((None, A_QB, HEAD_DIM), lambda h, i: (h, i, 0)),
                   pl.BlockSpec((None, A_QB, 1), lambda h, i: (h, i, 0))],
        out_shape=[jax.ShapeDtypeStruct((N_HEADS, s, HEAD_DIM), F32),
                   jax.ShapeDtypeStruct((N_HEADS, s, 1), F32)],
        compiler_params=_params(("arbitrary", "arbitrary")),
    )(qkv, *([kva] * (2 * nwin)), tbias)


def mix_a_bwd(qkv, kva, tbias, do, o, lse):
    s = qkv.shape[2]
    nq, q_spec, kv_specs, t_spec = _a_specs(s)
    nwin = len(kv_specs) // 2
    row_spec = lambda w: pl.BlockSpec((None, A_QB, w), lambda h, i: (h, jnp.minimum(i, nq - 1), 0))
    done_spec = pl.BlockSpec((None, A_QB, HEAD_DIM), lambda h, i: (h, i, 0))
    win = A_QB + A_PAD

    def body(*refs):
        q_ref = refs[0]
        t_ref, do_ref, o_ref, lse_ref, dq_ref, dk_ref, dv_ref, dt_ref, dk_win, dv_win = refs[1 + 2 * nwin:]
        i = pl.program_id(1)

        @pl.when(i == 0)
        def _():
            dk_win[...] = jnp.zeros_like(dk_win)
            dv_win[...] = jnp.zeros_like(dv_win)
            dt_ref[...] = jnp.zeros_like(dt_ref)

        @pl.when(i < nq)
        def _():
            k = _a_window(refs[1:1 + nwin], i)
            v = _a_window(refs[1 + nwin:1 + 2 * nwin], i)
            dt = jnp.zeros((A_TQ, A_BAND), F32)
            for j in range(A_QB // A_TQ):
                rows, qs, kj, sc = _a_scores(q_ref, k, t_ref, i, j)
                keys = slice(j * A_TQ, j * A_TQ + A_BAND)
                dob = do_ref[rows, :]
                p = jnp.exp(sc - lse_ref[rows, :])
                delta = jnp.sum(o_ref[rows, :] * dob.astype(F32), axis=-1, keepdims=True)
                ds = p * (_dot_nt(dob, v[keys, :]) - delta)
                dsb = ds.astype(BF16)
                dq_ref[rows, :] = _dot(dsb, kj) * 0.125
                dk_win[keys, :] += _dot_tn(dsb, qs)
                dv_win[keys, :] += _dot_tn(p.astype(BF16), dob)
                dt = dt + ds
            dt_ref[...] += dt

        dk_ref[...] = dk_win[0:A_QB, :]
        dv_ref[...] = dv_win[0:A_QB, :]
        dk_rest = dk_win[A_QB:win, :]
        dv_rest = dv_win[A_QB:win, :]
        dk_win[0:A_PAD, :] = dk_rest
        dv_win[0:A_PAD, :] = dv_rest
        dk_win[A_PAD:win, :] = jnp.zeros((A_QB, HEAD_DIM), F32)
        dv_win[A_PAD:win, :] = jnp.zeros((A_QB, HEAD_DIM), F32)

    return pl.pallas_call(
        body, name="mix_a_bwd", grid=(N_HEADS, nq + 1),
        in_specs=[q_spec] + kv_specs + [t_spec, row_spec(HEAD_DIM), row_spec(HEAD_DIM), row_spec(1)],
        out_specs=[row_spec(HEAD_DIM), done_spec, done_spec, t_spec],
        out_shape=[jax.ShapeDtypeStruct((N_HEADS, s, HEAD_DIM), F32),
                   jax.ShapeDtypeStruct((N_HEADS, s + A_QB, HEAD_DIM), F32),
                   jax.ShapeDtypeStruct((N_HEADS, s + A_QB, HEAD_DIM), F32),
                   jax.ShapeDtypeStruct((N_HEADS, A_TQ, A_BAND), F32)],
        scratch_shapes=[pltpu.VMEM((win, HEAD_DIM), F32), pltpu.VMEM((win, HEAD_DIM), F32)],
        compiler_params=_params(("arbitrary", "arbitrary")),
    )(qkv, *([kva] * (2 * nwin)), tbias, do, o, lse)


def relbias_tile(rel_bias, relmat):
    def body(rb_ref, rel_ref, o_ref):
        rel = rel_ref[...]
        o_ref[...] = jnp.full(o_ref.shape, NEG, F32)

        def step(r, carry):
            hit = rel == r
            for h in range(N_HEADS):
                o_ref[h] = jnp.where(hit, rb_ref[h, r], o_ref[h])
            return carry

        lax.fori_loop(REL_LO, REL_HI, step, 0)

    return pl.pallas_call(
        body, name="relbias_tile",
        in_specs=[pl.BlockSpec(memory_space=pltpu.SMEM), pl.BlockSpec(memory_space=pltpu.VMEM)],
        out_specs=pl.BlockSpec(memory_space=pltpu.VMEM),
        out_shape=jax.ShapeDtypeStruct((N_HEADS, A_TQ, A_BAND), F32),
        compiler_params=_params(),
    )(rel_bias, relmat)


def relbias_grad(dt, relmat):
    def body(dt_ref, rel_ref, o_ref):
        rel = rel_ref[...]
        lane = lax.broadcasted_iota(jnp.int32, (8, 384), 1)
        row = lax.broadcasted_iota(jnp.int32, (8, 384), 0)

        def step(r, acc):
            hit = rel == r
            for h in range(N_HEADS):
                val = jnp.sum(jnp.where(hit, dt_ref[h], 0.0))
                acc = jnp.where((lane == r) & (row == h), val, acc)
            return acc

        o_ref[...] = lax.fori_loop(REL_LO, REL_HI, step, jnp.zeros((8, 384), F32))

    return pl.pallas_call(
        body, name="relbias_grad",
        out_shape=jax.ShapeDtypeStruct((8, 384), F32),
        compiler_params=_params(),
    )(dt, relmat)


def _b_norm(v, gain):
    mu = jnp.mean(v, axis=-1, keepdims=True)
    xc = v - mu
    rstd = lax.rsqrt(jnp.mean(xc * xc, axis=-1, keepdims=True) + EPS)
    xhat = xc * rstd
    return xhat, rstd, xhat * gain


def _tril_mask():
    t = lax.broadcasted_iota(jnp.int32, (SG_CHUNK, SG_CHUNK), 0)
    u = lax.broadcasted_iota(jnp.int32, (SG_CHUNK, SG_CHUNK), 1)
    return u <= t


def mix_b_fwd(uv, gain, w_s, b_col):
    s = uv.shape[0]
    tm = min(ROW_T, s)

    def body(uv_ref, gain_ref, w_ref, b_ref, y_ref):
        tril = _tril_mask()
        ws = [jnp.where(tril, w_ref[g], 0.0).astype(BF16) for g in range(N_HEADS)]
        for c in range(tm // SG_CHUNK):
            rows = slice(c * SG_CHUNK, (c + 1) * SG_CHUNK)
            u = uv_ref[rows, 0:D_BRANCH]
            _, _, vn = _b_norm(uv_ref[rows, D_BRANCH:2 * D_BRANCH], gain_ref[...])
            vnb = vn.astype(BF16)
            outs = []
            for g in range(N_HEADS):
                cols = slice(g * HEAD_DIM, (g + 1) * HEAD_DIM)
                mixed = _dot(ws[g], vnb[:, cols]) + b_ref[g]
                outs.append(u[:, cols] * mixed)
            y_ref[rows, :] = jnp.concatenate(outs, axis=1)

    return pl.pallas_call(
        body, name="mix_b_fwd", grid=(s // tm,),
        in_specs=[pl.BlockSpec((tm, 2 * D_BRANCH), lambda i: (i, 0)),
                  pl.BlockSpec((1, D_BRANCH), lambda i: (0, 0)),
                  pl.BlockSpec((N_HEADS, SG_CHUNK, SG_CHUNK), lambda i: (0, 0, 0)),
                  pl.BlockSpec((N_HEADS, SG_CHUNK, 1), lambda i: (0, 0, 0))],
        out_specs=pl.BlockSpec((tm, D_BRANCH), lambda i: (i, 0)),
        out_shape=jax.ShapeDtypeStruct((s, D_BRANCH), F32),
        compiler_params=_params(("arbitrary",)),
    )(uv, gain, w_s, b_col)


def mix_b_bwd(uv, gain, w_s, b_col, dy):
    s = uv.shape[0]
    tm = min(ROW_T, s)

    def body(uv_ref, gain_ref, w_ref, b_ref, dy_ref, duv_ref, dw_ref, db_ref, dgain_ref):
        i = pl.program_id(0)

        @pl.when(i == 0)
        def _():
            dw_ref[...] = jnp.zeros_like(dw_ref)
            db_ref[...] = jnp.zeros_like(db_ref)
            dgain_ref[...] = jnp.zeros_like(dgain_ref)

        tril = _tril_mask()
        ws = [jnp.where(tril, w_ref[g], 0.0).astype(BF16) for g in range(N_HEADS)]
        gain_v = gain_ref[...]
        for c in range(tm // SG_CHUNK):
            rows = slice(c * SG_CHUNK, (c + 1) * SG_CHUNK)
            u = uv_ref[rows, 0:D_BRANCH]
            xhat, rstd, vn = _b_norm(uv_ref[rows, D_BRANCH:2 * D_BRANCH], gain_v)
            vnb = vn.astype(BF16)
            dyv = dy_ref[rows, :]
            dus, dvns = [], []
            for g in range(N_HEADS):
                cols = slice(g * HEAD_DIM, (g + 1) * HEAD_DIM)
                mixed = _dot(ws[g], vnb[:, cols]) + b_ref[g]
                dus.append(dyv[:, cols] * mixed)
                dmixed = dyv[:, cols] * u[:, cols]
                dmb = dmixed.astype(BF16)
                db_ref[g] += jnp.sum(dmixed, axis=-1, keepdims=True)
                dw_ref[g] += jnp.where(tril, _dot_nt(dmb, vnb[:, cols]), 0.0)
                dvns.append(_dot_tn(ws[g], dmb))
            dvn = jnp.concatenate(dvns, axis=1)
            dgain_ref[...] += jnp.sum(dvn * xhat, axis=0, keepdims=True)
            dxh = dvn * gain_v
            dv = rstd * (dxh - jnp.mean(dxh, axis=-1, keepdims=True)
                         - xhat * jnp.mean(dxh * xhat, axis=-1, keepdims=True))
            duv_ref[rows, :] = jnp.concatenate(dus + [dv], axis=1)

    return pl.pallas_call(
        body, name="mix_b_bwd", grid=(s // tm,),
        in_specs=[pl.BlockSpec((tm, 2 * D_BRANCH), lambda i: (i, 0)),
                  pl.BlockSpec((1, D_BRANCH), lambda i: (0, 0)),
                  pl.BlockSpec((N_HEADS, SG_CHUNK, SG_CHUNK), lambda i: (0, 0, 0)),
                  pl.BlockSpec((N_HEADS, SG_CHUNK, 1), lambda i: (0, 0, 0)),
                  pl.BlockSpec((tm, D_BRANCH), lambda i: (i, 0))],
        out_specs=[pl.BlockSpec((tm, 2 * D_BRANCH), lambda i: (i, 0)),
                   pl.BlockSpec((N_HEADS, SG_CHUNK, SG_CHUNK), lambda i: (0, 0, 0)),
                   pl.BlockSpec((N_HEADS, SG_CHUNK, 1), lambda i: (0, 0, 0)),
                   pl.BlockSpec((1, D_BRANCH), lambda i: (0, 0))],
        out_shape=[jax.ShapeDtypeStruct((s, 2 * D_BRANCH), F32),
                   jax.ShapeDtypeStruct((N_HEADS, SG_CHUNK, SG_CHUNK), F32),
                   jax.ShapeDtypeStruct((N_HEADS, SG_CHUNK, 1), F32),
                   jax.ShapeDtypeStruct((1, D_BRANCH), F32)],
        compiler_params=_params(("arbitrary",)),
    )(uv, gain, w_s, b_col, dy)


def _scan_mats(nrow):
    a = lax.broadcasted_iota(jnp.int32, (128, 128), 0)
    b = lax.broadcasted_iota(jnp.int32, (128, 128), 1)
    r = lax.broadcasted_iota(jnp.int32, (nrow, nrow), 0)
    c = lax.broadcasted_iota(jnp.int32, (nrow, nrow), 1)
    nb = nrow // N_HEADS
    same = (r // nb) == (c // nb)
    return a, b, r, c, same


def _exact_dot(x, m):
    hi, mid, lo = _split3(x)
    return _dot(hi, m) + _dot(mid, m) + _dot(lo, m)


def _exact_dot_left(m, x):
    hi, mid, lo = _split3(x)
    return _dot(m, hi) + _dot(m, mid) + _dot(m, lo)


def fox_gate_fwd(ft, bcol):
    nrow = ft.shape[0]

    def body(f_ref, b_ref, c_ref):
        z = f_ref[...] + b_ref[...]
        ls = jnp.minimum(z, 0.0) - jnp.log(1.0 + jnp.exp(-jnp.abs(z)))
        a, b, r, c, same = _scan_mats(nrow)
        within = _exact_dot(ls, (a <= b).astype(BF16))
        tot = jnp.broadcast_to(within[:, 127:128], within.shape)
        before = _exact_dot_left((same & (c < r)).astype(BF16), tot)
        c_ref[...] = within + before

    return pl.pallas_call(
        body, name="fox_gate_fwd",
        out_shape=jax.ShapeDtypeStruct((nrow, 128), F32),
        compiler_params=_params(),
    )(ft, bcol)


def fox_gate_bwd(ft, bcol, dc):
    nrow = ft.shape[0]

    def body(f_ref, b_ref, dc_ref, df_ref, db_ref):
        a, b, r, c, same = _scan_mats(nrow)
        dcv = dc_ref[...]
        within = _exact_dot(dcv, (a >= b).astype(BF16))
        tot = jnp.broadcast_to(within[:, 0:1], within.shape)
        after = _exact_dot_left((same & (c > r)).astype(BF16), tot)
        dls = within + after
        z = f_ref[...] + b_ref[...]
        dz = dls * _sigmoid(-z)
        df_ref[...] = dz
        rs = jnp.broadcast_to(jnp.sum(dz, axis=-1, keepdims=True), dz.shape)
        hr = lax.broadcasted_iota(jnp.int32, (8, nrow), 0)
        hc = lax.broadcasted_iota(jnp.int32, (8, nrow), 1)
        db_ref[...] = _exact_dot_left((hr == hc // (nrow // N_HEADS)).astype(BF16), rs)

    return pl.pallas_call(
        body, name="fox_gate_bwd",
        out_shape=[jax.ShapeDtypeStruct((nrow, 128), F32), jax.ShapeDtypeStruct((8, 128), F32)],
        compiler_params=_params(),
    )(ft, bcol, dc)


def _att_specs(s, qi, ki, vi):
    q_spec = pl.BlockSpec((None, None, FOX_TQ, HEAD_DIM), lambda h, i: (qi, h, i, 0))
    k_spec = pl.BlockSpec((None, None, s, HEAD_DIM), lambda h, i: (ki, h, 0, 0))
    v_spec = pl.BlockSpec((None, None, s, HEAD_DIM), lambda h, i: (vi, h, 0, 0))
    row_spec = lambda w: pl.BlockSpec((None, FOX_TQ, w), lambda h, i: (h, i, 0))
    gate_spec = pl.BlockSpec((None, s // ATT_T, 1, ATT_T), lambda h, i: (h, 0, 0, 0))
    return q_spec, k_spec, v_spec, row_spec, gate_spec


def _causal(n):
    row = lax.broadcasted_iota(jnp.int32, (n, n), 0)
    col = lax.broadcasted_iota(jnp.int32, (n, n), 1)
    return col <= row


def _gate_row(cr_ref, kb, g):
    if g == 1:
        return cr_ref[kb]
    return jnp.concatenate([cr_ref[kb + n] for n in range(g)], axis=1)


def _fox_walk(i, carry, tile, alive):
    g = FOX_WIDE
    own = FOX_TQ // ATT_T
    nwide = (own * i) // g
    carry = tile(own * i, own, carry, True)
    carry = lax.fori_loop(0, (own * i - nwide * g) // own, lambda n, c: tile(nwide * g, own, c, False), carry)

    def cond(state):
        return jnp.logical_and(state[0] >= 0, state[1] > 0)

    def step(state):
        n = state[0]
        c = tile(n * g, g, state[2:], False)
        return (n - 1, alive(n * g, c)) + tuple(c)

    out = lax.while_loop(cond, step, (nwide - 1, alive(nwide * g, carry)) + tuple(carry))
    return out[2:]


def _fox_reach(qs, k_ref, kmax_ref, cc, i):
    s = k_ref.shape[0]
    rows = 4 * ATT_T

    @pl.when(i == 0)
    def _():
        def chunk(n, mx):
            kc = k_ref[pl.ds(pl.multiple_of(n * rows, rows), rows), :].astype(F32)
            return jnp.maximum(mx, jnp.max(jnp.sum(kc * kc, axis=-1, keepdims=True)))

        kmax_ref[0] = jnp.sqrt(lax.fori_loop(0, s // rows, chunk, jnp.float32(0.0)))

    qf = qs.astype(F32)
    return jnp.sqrt(jnp.sum(qf * qf, axis=-1, keepdims=True)) * kmax_ref[0] + cc


def _gate_col(cr_ref, i):
    row = lax.broadcasted_iota(jnp.int32, (ATT_T, ATT_T), 0)
    col = lax.broadcasted_iota(jnp.int32, (ATT_T, ATT_T), 1)
    own = FOX_TQ // ATT_T
    return jnp.concatenate([jnp.sum(jnp.where(row == col, cr_ref[own * i + n], 0.0), axis=-1, keepdims=True)
                            for n in range(own)], axis=0)


def _fox_scores(qs, k, cc, crow, masked):
    sc = (_dot_nt(qs, k) + (cc - crow)) * LOG2E
    if masked:
        sc = jnp.where(_causal(FOX_TQ), sc, NEG)
    return sc


def fox_fwd(qkv, c_row, ride=()):
    s = qkv.shape[2]
    t = ATT_T
    nq = s // FOX_TQ
    q_spec, k_spec, v_spec, row_spec, gate_spec = _att_specs(s, 1, 2, 3)
    rows = 4 * t
    nride = len(ride)

    def body(q_ref, k_ref, v_ref, cr_ref, *refs):
        ride_in, refs = refs[:nride], refs[nride:]
        o_ref, ref_ref, rl_ref = refs[:3]
        ride_out, refs = refs[3:3 + nride], refs[3 + nride:]
        v1_ref, kmax_ref = refs[:2]
        i = pl.program_id(1)
        if nride:
            h = pl.program_id(0)
            start, wait = _chip_gather([(src, lambda slot, dst=dst: dst.at[slot]) for src, dst in zip(ride_in, ride_out)],
                                       *refs[2:])
            pl.when(jnp.logical_and(h == 0, i == 0))(start)

        @pl.when(i == 0)
        def _():
            def chunk(n, carry):
                r0 = pl.multiple_of(n * rows, rows)
                v1_ref[pl.ds(r0, rows), :] = jnp.concatenate(
                    [v_ref[pl.ds(r0, rows), :], jnp.ones((rows, HEAD_DIM), BF16)], axis=1)
                return carry

            lax.fori_loop(0, s // rows, chunk, 0)

        qs = q_ref[...] * 0.125
        cc = _gate_col(cr_ref, i)
        reach = _fox_reach(qs, k_ref, kmax_ref, cc, i) * LOG2E

        def alive(kb, carry):
            return (jnp.max(reach - cr_ref[kb][:, 0:1] * LOG2E - carry[0]) > FOX_DEAD2).astype(jnp.int32)

        def tile(kb, g, carry, masked):
            m, acc = carry
            k0 = pl.multiple_of(kb * t, t)
            sc = _fox_scores(qs, k_ref[pl.ds(k0, g * t), :], cc, _gate_row(cr_ref, kb, g), masked)
            m_new = jnp.maximum(m, jnp.ceil(jnp.max(sc, axis=-1, keepdims=True)))
            pb = jnp.exp2(sc - m_new).astype(BF16)
            acc = jnp.exp2(m - m_new) * acc + _dot(pb, v1_ref[pl.ds(k0, g * t), :])
            return m_new, acc

        init = (jnp.full((FOX_TQ, 1), NEG, F32), jnp.zeros((FOX_TQ, 2 * HEAD_DIM), F32))
        m, acc = _fox_walk(i, init, tile, alive)
        rl = 1.0 / acc[:, HEAD_DIM:HEAD_DIM + 1]
        o_ref[...] = acc[:, 0:HEAD_DIM] * rl
        ref_ref[...] = m
        rl_ref[...] = rl
        if nride:
            pl.when(jnp.logical_and(h == N_HEADS - 1, i == nq - 1))(wait)

    any_spec = pl.BlockSpec(memory_space=pl.ANY)
    ride_sems = [pltpu.SemaphoreType.DMA((3 * nride,)), pltpu.SemaphoreType.DMA((3 * nride,)),
                 pltpu.SemaphoreType.DMA((nride,))] if nride else []
    return pl.pallas_call(
        body, name="fox_fwd_gather" if nride else "fox_fwd", grid=(N_HEADS, nq),
        in_specs=[q_spec, k_spec, v_spec, gate_spec] + [any_spec] * nride,
        out_specs=[row_spec(HEAD_DIM), row_spec(1), row_spec(1)] + [any_spec] * nride,
        out_shape=[jax.ShapeDtypeStruct((N_HEADS, s, HEAD_DIM), F32),
                   jax.ShapeDtypeStruct((N_HEADS, s, 1), F32),
                   jax.ShapeDtypeStruct((N_HEADS, s, 1), F32)]
        + [jax.ShapeDtypeStruct((4,) + a.shape, a.dtype) for a in ride],
        scratch_shapes=[pltpu.VMEM((s, 2 * HEAD_DIM), BF16), pltpu.SMEM((1,), F32)] + ride_sems,
        compiler_params=_params(("arbitrary", "arbitrary")),
    )(qkv, qkv, qkv, c_row, *ride)


def fox_bwd(qkv, c_row, do, o, ref, rl, ride=()):
    s = qkv.shape[2]
    t = ATT_T
    nq = s // FOX_TQ
    q_spec, k_spec, v_spec, row_spec, gate_spec = _att_specs(s, 1, 2, 3)
    any_spec = pl.BlockSpec(memory_space=pl.ANY)
    nride = len(ride)

    def body(q_ref, k_ref, v_ref, cr_ref, do_ref, o_ref, ref_ref, rl_ref, *refs):
        ride_in, refs = refs[:nride], refs[nride:]
        dq_ref, dk_hbm, dv_hbm, dc_ref = refs[:4]
        ride_out, refs = refs[4:4 + nride], refs[4 + nride:]
        dk_acc, dv_acc, kmax_ref = refs[:3]
        h = pl.program_id(0)
        i = pl.program_id(1)
        if nride:
            start, wait = _device_exchange(_exchange_flows(ride_in, ride_out), *refs[3:])
            pl.when(jnp.logical_and(h == 0, i == 0))(start)

        @pl.when(i == 0)
        def _():
            dk_acc[...] = jnp.zeros_like(dk_acc)
            dv_acc[...] = jnp.zeros_like(dv_acc)
            dc_ref[...] = jnp.zeros_like(dc_ref)

        qs = q_ref[...] * 0.125
        ref = ref_ref[...]
        rl = rl_ref[...]
        dob = (do_ref[...].astype(F32) * rl).astype(BF16)
        delta = jnp.sum(o_ref[...] * dob.astype(F32), axis=-1, keepdims=True)
        cc = _gate_col(cr_ref, i)
        margin = _fox_reach(qs, k_ref, kmax_ref, cc, i) * LOG2E - ref

        def alive(kb, carry):
            return (jnp.max(margin - cr_ref[kb][:, 0:1] * LOG2E) > FOX_DEAD2).astype(jnp.int32)

        def tile(kb, g, carry, masked):
            dq, = carry
            k0 = pl.multiple_of(kb * t, t)
            k = k_ref[pl.ds(k0, g * t), :]
            sc = _fox_scores(qs, k, cc, _gate_row(cr_ref, kb, g), masked)
            wb = jnp.exp2(sc - ref).astype(BF16)
            ds = wb.astype(F32) * (_dot_nt(dob, v_ref[pl.ds(k0, g * t), :]) - delta)
            dsb = ds.astype(BF16)
            dk_acc[pl.ds(k0, g * t), :] += _dot_tn(dsb, qs)
            dv_acc[pl.ds(k0, g * t), :] += _dot_tn(wb, dob)
            dcs = -jnp.sum(ds, axis=0, keepdims=True)
            for n in range(g):
                dc_ref[kb + n] += dcs[:, n * t:(n + 1) * t]
            return (dq + _dot(dsb, k),)

        dq, = _fox_walk(i, (jnp.zeros((FOX_TQ, HEAD_DIM), F32),), tile, alive)
        dq_ref[...] = dq * 0.125

        @pl.when(i == nq - 1)
        def _():
            pltpu.sync_copy(dk_acc, dk_hbm.at[h])
            pltpu.sync_copy(dv_acc, dv_hbm.at[h])

        if nride:
            pl.when(jnp.logical_and(h == N_HEADS - 1, i == nq - 1))(wait)

    return pl.pallas_call(
        body, name="fox_bwd_exchange" if nride else "fox_bwd", grid=(N_HEADS, nq),
        in_specs=[q_spec, k_spec, v_spec,
                  gate_spec,
                  row_spec(HEAD_DIM), row_spec(HEAD_DIM), row_spec(1), row_spec(1)] + [any_spec] * nride,
        out_specs=[row_spec(HEAD_DIM), any_spec, any_spec,
                   gate_spec] + [any_spec] * nride,
        out_shape=[jax.ShapeDtypeStruct((N_HEADS, s, HEAD_DIM), F32),
                   jax.ShapeDtypeStruct((N_HEADS, s, HEAD_DIM), F32),
                   jax.ShapeDtypeStruct((N_HEADS, s, HEAD_DIM), F32),
                   jax.ShapeDtypeStruct((N_HEADS, s // t, 1, t), F32)] + _exchange_shapes(ride),
        scratch_shapes=[pltpu.VMEM((s, HEAD_DIM), F32), pltpu.VMEM((s, HEAD_DIM), F32), pltpu.SMEM((1,), F32)]
        + (_exchange_sems(nride) if nride else []),
        compiler_params=_params(("arbitrary", "arbitrary")),
    )(qkv, qkv, qkv, c_row, do, o, ref, rl, *ride)


def _sb_valid(nrows, ahead):
    row = lax.broadcasted_iota(jnp.int32, (nrows, ATT_T), 0)
    col = lax.broadcasted_iota(jnp.int32, (nrows, ATT_T), 1)
    return col + ahead < row


def _sb_band_valid(nsub, i):
    shape = (nsub * SB_SUB, SB_BAND)
    row = lax.broadcasted_iota(jnp.int32, shape, 0)
    col = lax.broadcasted_iota(jnp.int32, shape, 1)
    first = i * SB_TQ + (row - (row & (SB_SUB - 1)))
    valid = col < (row & (SB_SUB - 1)) + jnp.minimum(first, SB_BACK)
    return valid, first[:, 0:1] > SB_BACK


def _sb_logits(qs, k):
    z = _dot_nt(qs, k)
    sp = jnp.log(1.0 + jnp.exp(-jnp.abs(z)))
    return jnp.minimum(z, 0.0) - sp, -jnp.maximum(z, 0.0) - sp


def _sb_weights(ls, lm, run, valid):
    if valid is not None:
        lm = jnp.where(valid, lm, 0.0)
    n = lm.shape[1]
    row = lax.broadcasted_iota(jnp.int32, (n, n), 0)
    col = lax.broadcasted_iota(jnp.int32, (n, n), 1)
    later = (row > col).astype(BF16)
    hi, lo = _split2(lm)
    between = _dot(hi, later) + _dot(lo, later)
    if run is not None:
        between = run + between
    a = jnp.exp(ls + between)
    if valid is not None:
        a = jnp.where(valid, a, 0.0)
    return lm, a


def _sb_band_start(i, j):
    return pl.multiple_of(jnp.maximum(i * SB_TQ + j * SB_SUB - SB_BACK, 0), SB_SUB)


def _sb_tile(qs, k, run, valid):
    ls, lm = _sb_logits(qs, k)
    lm, a = _sb_weights(ls, lm, run, valid)
    return ls, lm, a


def _sb_band(i, qs_all, k_ref):
    nsub = qs_all.shape[0] // SB_SUB
    valid, open_left = _sb_band_valid(nsub, i)
    starts = [_sb_band_start(i, j) for j in range(nsub)]
    kwins = [k_ref[pl.ds(k0, SB_BAND), :] for k0 in starts]
    parts = [_sb_logits(qs_all[j * SB_SUB:(j + 1) * SB_SUB], kwins[j]) for j in range(nsub)]
    ls = jnp.concatenate([p[0] for p in parts], axis=0)
    lm, a = _sb_weights(ls, jnp.concatenate([p[1] for p in parts], axis=0), None, valid)
    return starts, kwins, ls, lm, a, valid, open_left


def _sb_suffix(g, run_g):
    n = g.shape[1]
    row = lax.broadcasted_iota(jnp.int32, (n, n), 0)
    col = lax.broadcasted_iota(jnp.int32, (n, n), 1)
    from_here = (row >= col).astype(BF16)
    hi, lo = _split2(g)
    out = _dot(hi, from_here) + _dot(lo, from_here)
    return out if run_g is None else run_g + out


def _sb_walk(i, carry, tile):
    def alive_of(c):
        return (jnp.max(c[0]) > SB_DEAD).astype(jnp.int32)

    def cond(state):
        n, alive = state[0], state[1]
        return jnp.logical_and(n < i, alive > 0)

    def step(state):
        n = state[0]
        c = tile(i - 1 - n, state[2:], False)
        return (n + 1, alive_of(c)) + tuple(c)

    out = lax.while_loop(cond, step, (jnp.int32(0), alive_of(carry)) + tuple(carry))
    return out[2:]


def _sb_specs(s):
    tq = SB_TQ
    q_spec = pl.BlockSpec((None, None, tq, HEAD_DIM), lambda h, i: (4, h, i, 0))
    k_spec = pl.BlockSpec((None, None, s, HEAD_DIM), lambda h, i: (5, h, 0, 0))
    v_spec = pl.BlockSpec((None, None, s, HEAD_DIM), lambda h, i: (6, h, 0, 0))
    row_spec = pl.BlockSpec((None, tq, HEAD_DIM), lambda h, i: (h, i, 0))
    band_spec = pl.BlockSpec((None, None, 1, 128), lambda h, i: (h, i, 0, 0))
    return tq, q_spec, k_spec, v_spec, row_spec, band_spec


def _sb_block(b, row0, tile, zero):
    t = ATT_T
    lo, hi, both = slice(row0, row0 + t), slice(row0 + t, row0 + 2 * t), slice(row0, row0 + 2 * t)
    c_hi = tile(2 * b + 1, hi, zero, 0)
    c_lo = tile(2 * b, lo, zero, 0)
    c_hi = tile(2 * b, hi, c_hi, None)
    carry = tuple(jnp.concatenate([x, y], axis=0) for x, y in zip(c_lo, c_hi))
    return _sb_walk(2 * b, carry, lambda kb, c, _: tile(kb, both, c, None))


def sb_fwd(qkv):
    s = qkv.shape[2]
    t = ATT_T
    tq, q_spec, k_spec, v_spec, row_spec, band_spec = _sb_specs(s)

    def body(q_ref, k_ref, v_ref, o_ref, band_ref, done_ref):
        i = pl.program_id(1)
        qs = q_ref[...] * 0.125
        starts, _, _, lm, a, _, open_left = _sb_band(i, qs, k_ref)
        ab = a.astype(BF16)
        for j, k0 in enumerate(starts):
            rows = slice(j * SB_SUB, (j + 1) * SB_SUB)
            o_ref[rows, :] = _dot(ab[rows], v_ref[pl.ds(k0, SB_BAND), :])
        worst = jnp.max(jnp.where(open_left, jnp.sum(lm, axis=-1, keepdims=True), NEG))
        done_ref[0] = (worst <= SB_DEAD).astype(jnp.int32)

        @pl.when(done_ref[0] == 0)
        def _():
            def tile(kb, rows, carry, ahead):
                run, acc = carry
                k0 = pl.multiple_of(kb * t, t)
                valid = None if ahead is None else _sb_valid(t, ahead)
                _, lm, a = _sb_tile(qs[rows], k_ref[pl.ds(k0, t), :], run, valid)
                acc = acc + _dot(a.astype(BF16), v_ref[pl.ds(k0, t), :])
                return run + jnp.sum(lm, axis=-1, keepdims=True), acc

            for n in range(tq // (2 * t)):
                _, acc = _sb_block(i * (tq // (2 * t)) + n, n * 2 * t, tile,
                                   (jnp.zeros((t, 1), F32), jnp.zeros((t, HEAD_DIM), F32)))
                o_ref[n * 2 * t:(n + 1) * 2 * t, :] = acc

        band_ref[...] = jnp.full(band_ref.shape, done_ref[0], jnp.int32).astype(F32)

    return pl.pallas_call(
        body, name="sb_fwd", grid=(N_HEADS, s // tq),
        in_specs=[q_spec, k_spec, v_spec],
        out_specs=[row_spec, band_spec],
        out_shape=[jax.ShapeDtypeStruct((N_HEADS, s, HEAD_DIM), F32),
                   jax.ShapeDtypeStruct((N_HEADS, s // tq, 1, 128), F32)],
        scratch_shapes=[pltpu.SMEM((1,), jnp.int32)],
        compiler_params=_params(("arbitrary", "arbitrary")),
    )(qkv, qkv, qkv)


def sb_bwd(qkv, do, o, band, ride=()):
    s = qkv.shape[2]
    t = ATT_T
    tq, q_spec, k_spec, v_spec, row_spec, band_spec = _sb_specs(s)
    nq = s // tq
    any_spec = pl.BlockSpec(memory_space=pl.ANY)
    nride = len(ride)

    def body(q_ref, k_ref, v_ref, do_ref, o_ref, band_ref, *refs):
        ride_in, refs = refs[:nride], refs[nride:]
        dq_ref, dk_hbm, dv_hbm = refs[:3]
        ride_out, refs = refs[3:3 + nride], refs[3 + nride:]
        dk_acc, dv_acc = refs[:2]
        h = pl.program_id(0)
        i = pl.program_id(1)
        if nride:
            start, wait = _device_exchange(_exchange_flows(ride_in, ride_out), *refs[2:])
            pl.when(jnp.logical_and(h == 0, i == 0))(start)

        @pl.when(i == 0)
        def _():
            dk_acc[...] = jnp.zeros_like(dk_acc)
            dv_acc[...] = jnp.zeros_like(dv_acc)

        qs_all = q_ref[...] * 0.125
        dob_all = do_ref[...]
        tot_all = jnp.sum(o_ref[...] * dob_all.astype(F32), axis=-1, keepdims=True)
        on_band = jnp.max(band_ref[...]) > 0.5

        def grads(qs, dob, tot, k, v, k0, run, run_g, valid):
            ls, lm, a = _sb_tile(qs, k, run, valid)
            ab = a.astype(BF16)
            g = ab.astype(F32) * _dot_nt(dob, v)
            g_left = tot - _sb_suffix(g, run_g)
            dz = g - jnp.exp(ls) * (g + g_left)
            if valid is not None:
                dz = jnp.where(valid, dz, 0.0)
            dzb = dz.astype(BF16)
            n = k.shape[0]
            dk_acc[pl.ds(k0, n), :] += _dot_tn(dzb, qs)
            dv_acc[pl.ds(k0, n), :] += _dot_tn(ab, dob)
            return dzb, lm, g

        @pl.when(on_band)
        def _():
            starts, kwins, ls, _, a, valid, _ = _sb_band(i, qs_all, k_ref)
            ab = a.astype(BF16)
            subs = [slice(j * SB_SUB, (j + 1) * SB_SUB) for j in range(len(starts))]
            vwins = [v_ref[pl.ds(k0, SB_BAND), :] for k0 in starts]
            g = ab.astype(F32) * jnp.concatenate([_dot_nt(dob_all[r], v) for r, v in zip(subs, vwins)], axis=0)
            dz = jnp.where(valid, g - jnp.exp(ls) * (g + (tot_all - _sb_suffix(g, None))), 0.0)
            dzb = dz.astype(BF16)
            for r, k0, k in zip(subs, starts, kwins):
                dq_ref[r, :] = _dot(dzb[r], k) * 0.125
                dk_acc[pl.ds(k0, SB_BAND), :] += _dot_tn(dzb[r], qs_all[r])
                dv_acc[pl.ds(k0, SB_BAND), :] += _dot_tn(ab[r], dob_all[r])

        @pl.when(jnp.logical_not(on_band))
        def _():
            def tile(kb, rows, carry, ahead):
                run, run_g, dq = carry
                k0 = pl.multiple_of(kb * t, t)
                k = k_ref[pl.ds(k0, t), :]
                valid = None if ahead is None else _sb_valid(t, ahead)
                dzb, lm, g = grads(qs_all[rows], dob_all[rows], tot_all[rows], k, v_ref[pl.ds(k0, t), :], k0,
                                   run, run_g, valid)
                return (run + jnp.sum(lm, axis=-1, keepdims=True),
                        run_g + jnp.sum(g, axis=-1, keepdims=True),
                        dq + _dot(dzb, k))

            zero = jnp.zeros((t, 1), F32)
            for n in range(tq // (2 * t)):
                _, _, dq = _sb_block(i * (tq // (2 * t)) + n, n * 2 * t, tile, (zero, zero, jnp.zeros((t, HEAD_DIM), F32)))
                dq_ref[n * 2 * t:(n + 1) * 2 * t, :] = dq * 0.125

        @pl.when(i == nq - 1)
        def _():
            pltpu.sync_copy(dk_acc, dk_hbm.at[h])
            pltpu.sync_copy(dv_acc, dv_hbm.at[h])

        if nride:
            pl.when(jnp.logical_and(h == N_HEADS - 1, i == nq - 1))(wait)

    return pl.pallas_call(
        body, name="sb_bwd_exchange" if nride else "sb_bwd", grid=(N_HEADS, nq),
        in_specs=[q_spec, k_spec, v_spec, row_spec, row_spec, band_spec] + [any_spec] * nride,
        out_specs=[row_spec, any_spec, any_spec] + [any_spec] * nride,
        out_shape=[jax.ShapeDtypeStruct((N_HEADS, s, HEAD_DIM), F32)] * 3 + _exchange_shapes(ride),
        scratch_shapes=[pltpu.VMEM((s, HEAD_DIM), F32), pltpu.VMEM((s, HEAD_DIM), F32)]
        + (_exchange_sems(nride) if nride else []),
        compiler_params=_params(("arbitrary", "arbitrary")),
    )(qkv, qkv, qkv, do, o, band, *ride)


def _branch_inputs(refs, br):
    ya_ref, yb_ref, yc_ref, yd_ref = refs
    if br == 1:
        return yb_ref[...]
    return _heads_to_lanes((ya_ref, None, yc_ref, yd_ref)[br])


def outproj_fwd(x, ya, yb, yc, yd, gates, bg, wout):
    s = x.shape[0]
    tm = min(ROW_T, s)

    def body(x_ref, ya_ref, yb_ref, yc_ref, yd_ref, gates_ref, bg_ref, w_ref, out_ref):
        pieces = []
        for br in range(4):
            cols = slice(br * D_BRANCH, (br + 1) * D_BRANCH)
            y = _branch_inputs((ya_ref, yb_ref, yc_ref, yd_ref), br)
            r = lax.rsqrt(jnp.mean(y * y, axis=-1, keepdims=True) + EPS)
            gt = gates_ref[:, cols]
            pieces.append((y * r * bg_ref[:, cols]) * (gt * _sigmoid(gt)))
        merged = jnp.concatenate(pieces, axis=1).astype(BF16)
        out_ref[...] = x_ref[...] + _dot(merged, w_ref[...])

    head_spec = pl.BlockSpec((N_HEADS, tm, HEAD_DIM), lambda i: (0, i, 0))
    return pl.pallas_call(
        body, name="outproj_fwd", grid=(s // tm,),
        in_specs=[pl.BlockSpec((tm, D_MODEL), lambda i: (i, 0)),
                  head_spec, pl.BlockSpec((tm, D_BRANCH), lambda i: (i, 0)), head_spec, head_spec,
                  pl.BlockSpec((tm, D_MODEL), lambda i: (i, 0)),
                  pl.BlockSpec((1, D_MODEL), lambda i: (0, 0)),
                  pl.BlockSpec((D_MODEL, D_MODEL), lambda i: (0, 0))],
        out_specs=pl.BlockSpec((tm, D_MODEL), lambda i: (i, 0)),
        out_shape=jax.ShapeDtypeStruct((s, D_MODEL), F32),
        compiler_params=_params(("arbitrary",)),
    )(x, ya, yb, yc, yd, gates, bg, wout)


def outproj_bwd(dout, ya, yb, yc, yd, gates, bg, wout):
    s = dout.shape[0]
    tm = min(ROW_T, s)

    def body(dout_ref, ya_ref, yb_ref, yc_ref, yd_ref, gates_ref, bg_ref, w_ref,
             dya_ref, dyb_ref, dyc_ref, dyd_ref, dgates_ref, dbg_ref, dw_ref):
        i = pl.program_id(0)

        @pl.when(i == 0)
        def _():
            dbg_ref[...] = jnp.zeros_like(dbg_ref)
            dw_ref[...] = jnp.zeros_like(dw_ref)

        doutb = dout_ref[...].astype(BF16)
        dmerged = _dot_nt(doutb, w_ref[...])
        pieces = []
        for br in range(4):
            cols = slice(br * D_BRANCH, (br + 1) * D_BRANCH)
            y = _branch_inputs((ya_ref, yb_ref, yc_ref, yd_ref), br)
            r = lax.rsqrt(jnp.mean(y * y, axis=-1, keepdims=True) + EPS)
            yn = y * r
            bgv = bg_ref[:, cols]
            gt = gates_ref[:, cols]
            sig = _sigmoid(gt)
            act = gt * sig
            n = yn * bgv
            pieces.append(n * act)
            dm = dmerged[:, cols]
            dn = dm * act
            dgates_ref[:, cols] = (dm * n * (sig * (1.0 + gt * (1.0 - sig)))).astype(BF16)
            dbg_ref[:, cols] += jnp.sum(dn * yn, axis=0, keepdims=True)
            u = dn * bgv
            dy = r * (u - yn * jnp.mean(yn * u, axis=-1, keepdims=True))
            if br == 1:
                dyb_ref[...] = dy
            else:
                dref = (dya_ref, None, dyc_ref, dyd_ref)[br]
                for hh in range(N_HEADS):
                    dref[hh] = dy[:, hh * HEAD_DIM:(hh + 1) * HEAD_DIM].astype(BF16)
        merged = jnp.concatenate(pieces, axis=1).astype(BF16)
        dw_ref[...] += _dot_tn(merged, doutb)

    head_spec = pl.BlockSpec((N_HEADS, tm, HEAD_DIM), lambda i: (0, i, 0))
    head_shape = jax.ShapeDtypeStruct((N_HEADS, s, HEAD_DIM), BF16)
    return pl.pallas_call(
        body, name="outproj_bwd", grid=(s // tm,),
        in_specs=[pl.BlockSpec((tm, D_MODEL), lambda i: (i, 0)),
                  head_spec, pl.BlockSpec((tm, D_BRANCH), lambda i: (i, 0)), head_spec, head_spec,
                  pl.BlockSpec((tm, D_MODEL), lambda i: (i, 0)),
                  pl.BlockSpec((1, D_MODEL), lambda i: (0, 0)),
                  pl.BlockSpec((D_MODEL, D_MODEL), lambda i: (0, 0))],
        out_specs=[head_spec, pl.BlockSpec((tm, D_BRANCH), lambda i: (i, 0)), head_spec, head_spec,
                   pl.BlockSpec((tm, D_MODEL), lambda i: (i, 0)),
                   pl.BlockSpec((1, D_MODEL), lambda i: (0, 0)),
                   pl.BlockSpec((D_MODEL, D_MODEL), lambda i: (0, 0))],
        out_shape=[head_shape, jax.ShapeDtypeStruct((s, D_BRANCH), F32), head_shape, head_shape,
                   jax.ShapeDtypeStruct((s, D_MODEL), BF16),
                   jax.ShapeDtypeStruct((1, D_MODEL), F32),
                   jax.ShapeDtypeStruct((D_MODEL, D_MODEL), F32)],
        compiler_params=_params(("arbitrary",)),
    )(dout, ya, yb, yc, yd, gates, bg, wout)


def final_loss(x, tgt, g):
    s = x.shape[0]
    tm = min(ROW_T, s)

    def body(x_ref, t_ref, g_ref, loss_ref, dx_ref, dg_ref):
        i = pl.program_id(0)

        @pl.when(i == 0)
        def _():
            loss_ref[...] = jnp.zeros_like(loss_ref)
            dg_ref[...] = jnp.zeros_like(dg_ref)

        xv = x_ref[...]
        gv = g_ref[...]
        r = lax.rsqrt(jnp.mean(xv * xv, axis=-1, keepdims=True) + EPS)
        xn = xv * r
        err = xn * gv - t_ref[...]
        loss_ref[...] += jnp.sum(err * err) * (0.5 / D_MODEL)
        dy = err * (1.0 / D_MODEL)
        u = dy * gv
        dx_ref[...] = r * (u - xn * jnp.mean(xn * u, axis=-1, keepdims=True))
        dg_ref[...] += jnp.sum(dy * xn, axis=0, keepdims=True)

    return pl.pallas_call(
        body, name="final_loss", grid=(s // tm,),
        in_specs=[pl.BlockSpec((tm, D_MODEL), lambda i: (i, 0)),
                  pl.BlockSpec((tm, D_MODEL), lambda i: (i, 0)),
                  pl.BlockSpec((1, D_MODEL), lambda i: (0, 0))],
        out_specs=[pl.BlockSpec((1, 128), lambda i: (0, 0)),
                   pl.BlockSpec((tm, D_MODEL), lambda i: (i, 0)),
                   pl.BlockSpec((1, D_MODEL), lambda i: (0, 0))],
        out_shape=[jax.ShapeDtypeStruct((1, 128), F32),
                   jax.ShapeDtypeStruct((s, D_MODEL), F32),
                   jax.ShapeDtypeStruct((1, D_MODEL), F32)],
        compiler_params=_params(("arbitrary",)),
    )(x, tgt, g)


def _rel_index():
    i = np.arange(A_TQ)[:, None]
    j = np.arange(A_BAND)[None, :]
    rel = np.clip(i - j + (A_BAND - A_TQ), -MAX_REL, MAX_REL) + MAX_REL
    dchunk = i // CHUNK + LOOKBACK - j // CHUNK
    valid = (dchunk >= 0) & (dchunk <= LOOKBACK)
    return jnp.asarray(np.where(valid, rel, -1).astype(np.int32))


def _layer_consts(p):
    tbias = relbias_tile(p["rel_bias"], _rel_index())
    return dict(
        norm_g=p["norm_g"].reshape(1, D_MODEL),
        v_gain=p["v_gain"].reshape(1, D_BRANCH),
        b_col=p["b_s"].reshape(N_HEADS, SG_CHUNK, 1),
        bg=p["branch_gain"].reshape(1, D_MODEL),
        tbias=tbias,
    )


def _gate_layout(fp, b_f, s):
    nb = s // 128
    ft = fp[:, :N_HEADS].T.reshape(N_HEADS * nb, 128)
    bcol = jnp.repeat(b_f, nb).reshape(N_HEADS * nb, 1)
    return ft, bcol


def layer_fwd(x, p, ride=()):
    s = x.shape[0]
    c = _layer_consts(p)
    h, qkv, kva, gates, uv, fp = inproj_fwd(x, c["norm_g"], p["wp"])
    ya, lse_a = mix_a_fwd(qkv, kva, c["tbias"])
    yb = mix_b_fwd(uv, c["v_gain"], p["w_s"], c["b_col"])
    ft, bcol = _gate_layout(fp, p["b_f"], s)
    c_row = fox_gate_fwd(ft, bcol).reshape(N_HEADS, s // ATT_T, 1, ATT_T)
    yc, ref_c, rl_c, *rode = fox_fwd(qkv, c_row, ride)
    yd, band_d = sb_fwd(qkv)
    out = outproj_fwd(x, ya, yb, yc, yd, gates, c["bg"], p["wout"])
    saved = dict(consts=c, x=x, h=h, qkv=qkv, gates=gates, uv=uv, kva=kva, ft=ft, bcol=bcol,
                 c_row=c_row, ya=ya, lse_a=lse_a, yb=yb, yc=yc, ref_c=ref_c, rl_c=rl_c, yd=yd, band_d=band_d)
    return out, saved, rode


def layer_bwd(dout, p, sv, exchange=False, upper_w_in=None, small_ride=None):
    s = dout.shape[0]
    c = sv["consts"]
    dya, dyb, dyc, dyd, dgates, dbg, dwout = outproj_bwd(
        dout, sv["ya"], sv["yb"], sv["yc"], sv["yd"], sv["gates"], c["bg"], p["wout"])
    dqa, dka, dva, dt = mix_a_bwd(sv["qkv"], sv["kva"], c["tbias"], dya, sv["ya"], sv["lse_a"])
    drel = relbias_grad(dt, _rel_index())[:N_HEADS, :2 * MAX_REL + 1]
    duv, dws, dbs, dvgain = mix_b_bwd(sv["uv"], c["v_gain"], p["w_s"], c["b_col"], dyb)
    ride = [dwout.astype(BF16).reshape(4, D_BRANCH, D_MODEL)] if exchange else []
    if upper_w_in is not None:
        ride.append(upper_w_in)
    dqc, dkc, dvc, dc, *rode = fox_bwd(sv["qkv"], sv["c_row"], dyc, sv["yc"], sv["ref_c"], sv["rl_c"], ride)
    dft, dbf = fox_gate_bwd(sv["ft"], sv["bcol"], dc.reshape(N_HEADS * (s // 128), 128))
    dfp = jnp.pad(dft.reshape(N_HEADS, s).T, ((0, 0), (0, 128 - N_HEADS)))
    grads = dict(b_f=dbf[:N_HEADS, 0], rel_bias=drel, w_s=dws, b_s=dbs.reshape(N_HEADS, SG_CHUNK),
                 v_gain=dvgain.reshape(D_BRANCH), branch_gain=dbg.reshape(4, D_BRANCH), wout=dwout)
    dqd, dkd, dvd, *small_parts = sb_bwd(sv["qkv"], dyd, sv["yd"], sv["band_d"], small_ride(grads) if small_ride else ())
    dp, dx, dnorm = inproj_bwd((dqa, dka, dva, dqc, dkc, dvc, dqd, dkd, dvd), dgates, duv, dfp,
                               p["wp"], sv["x"], c["norm_g"], dout)
    grads["norm_g"] = dnorm.reshape(D_MODEL)
    if small_ride:
        top, = inproj_wgrad(sv["h"], dp, 0)
        grads["w_in_shards"], grads["w_in_top_parts"] = inproj_wgrad(sv["h"], dp, 1, [top])
        grads["small_parts"] = small_parts[0]
    else:
        grads["w_in_shards"], = inproj_wgrad(sv["h"], dp)
    if exchange:
        grads["w_out_parts"] = rode[0]
    return dx, grads, (rode[1] if upper_w_in is not None else None)


def local_step(x, tgt, layers, final_g, next_shards=None):
    layers = list(layers)
    saved = []
    cur = x
    for l, p in enumerate(layers):
        ride = next_shards[l] if next_shards is not None and l + 1 < len(layers) else ()
        cur, sv, rode = layer_fwd(cur, p, ride)
        saved.append(sv)
        if ride:
            layers[l + 1] = dict(layers[l + 1], wp=pack_w_in(rode[0][None])[0], wout=rode[1].reshape(D_MODEL, D_MODEL))
    loss, dcur, dfinal = final_loss(cur, tgt, final_g.reshape(1, D_MODEL))
    grads = [None] * len(layers)
    for l in reversed(range(len(layers))):
        exchange = next_shards is not None
        upper = grads[l + 1]["w_in_shards"] if exchange and l + 1 < len(layers) else None
        small_ride = None
        if exchange and l == 0:
            def small_ride(g0, above=tuple(grads[1:])):
                stacked = {k: jnp.stack([g[k] for g in (g0,) + above]) for k in SMALL_EARLY if k != "final_g"}
                return [_pack([stacked.get(k, dfinal.reshape(D_MODEL)) for k in SMALL_EARLY])]
        dcur, grads[l], got = layer_bwd(dcur, layers[l], saved[l], exchange, upper, small_ride)
        if upper is not None:
            grads[l + 1]["w_in_parts"] = got
    return loss[0, 0], dcur, grads, dfinal.reshape(D_MODEL)


def _chip_gather(pairs, send_sems, recv_sems, loc_sems):
    x, y, c = lax.axis_index("x"), lax.axis_index("y"), lax.axis_index("c")
    me = 2 * x + y
    chips = [(1 - x, y), (x, 1 - y), (1 - x, 1 - y)]
    npair = len(pairs)

    def local():
        return [pltpu.make_async_copy(src, dst(me), loc_sems.at[n]) for n, (src, dst) in enumerate(pairs)]

    def remote(j, n, slot):
        src, dst = pairs[n]
        return pltpu.make_async_remote_copy(
            src_ref=src, dst_ref=dst(slot), send_sem=send_sems.at[npair * j + n], recv_sem=recv_sems.at[npair * j + n],
            device_id=(chips[j][0], chips[j][1], c), device_id_type=MESH)

    def start():
        for cp in local():
            cp.start()
        for j in range(3):
            for n in range(npair):
                remote(j, n, me).start()

    def wait():
        for j in range(3):
            for n in range(npair):
                remote(j, n, 2 * chips[j][0] + chips[j][1]).wait_recv()
        for j in range(3):
            for n in range(npair):
                remote(j, n, me).wait_send()
        for cp in local():
            cp.wait()

    return start, wait


def gather_weights(w_in, w_out, gains):
    depth = w_in.shape[0]

    def body(in_ref, out_ref, g_ref, oin_ref, oout_ref, og_ref, send_sems, recv_sems, loc_sems):
        pairs = [(in_ref, lambda s: oin_ref.at[:, s]), (out_ref, lambda s: oout_ref.at[:, s]), (g_ref, lambda s: og_ref.at[s])]
        start, wait = _chip_gather(pairs, send_sems, recv_sems, loc_sems)
        start()
        wait()

    any_spec = pl.BlockSpec(memory_space=pl.ANY)
    return pl.pallas_call(
        body, name="gather_weights",
        in_specs=[any_spec] * 3, out_specs=[any_spec] * 3,
        out_shape=[jax.ShapeDtypeStruct((depth, 4) + w_in.shape[1:], w_in.dtype),
                   jax.ShapeDtypeStruct((depth, 4) + w_out.shape[1:], w_out.dtype),
                   jax.ShapeDtypeStruct((4,) + gains.shape, gains.dtype)],
        scratch_shapes=[pltpu.SemaphoreType.DMA((9,)), pltpu.SemaphoreType.DMA((9,)), pltpu.SemaphoreType.DMA((3,))],
    )(w_in, w_out, gains)


def pack_w_in(shards):
    depth = shards.shape[0]
    tr = 256

    def body(s_ref, o_ref):
        full = jnp.concatenate([s_ref[n] for n in range(4)], axis=1)
        o_ref[...] = jnp.concatenate([full[:, :SEC_D_Q], full[:, SEC_D_Q + N_HEADS:], full[:, SEC_D_Q:SEC_D_Q + N_HEADS],
                                      jnp.zeros((tr, N_PACK - N_IN), BF16)], axis=1)

    return pl.pallas_call(
        body, name="pack_w_in", grid=(depth, D_MODEL // tr),
        in_specs=[pl.BlockSpec((None, 4, tr, N_SHARD), lambda l, r: (l, 0, r, 0))],
        out_specs=pl.BlockSpec((None, tr, N_PACK), lambda l, r: (l, r, 0)),
        out_shape=jax.ShapeDtypeStruct((depth, D_MODEL, N_PACK), BF16),
        compiler_params=_params(("arbitrary", "arbitrary")),
    )(shards)


def _device_exchange(flows, send_sems, recv_sems, loc_sems):
    x, y, c = lax.axis_index("x"), lax.axis_index("y"), lax.axis_index("c")
    me_chip = 2 * x + y
    me = 4 * x + 2 * y + c
    peers = [(x, y, 1 - c)]
    for px, py in [(1 - x, y), (x, 1 - y), (1 - x, 1 - y)]:
        peers += [(px, py, c), (px, py, 1 - c)]
    nflow = len(flows)

    def local():
        return [pltpu.make_async_copy(src(me_chip), dst(me), loc_sems.at[f]) for f, (src, dst) in enumerate(flows)]

    def copies(n, chip, slot):
        return [pltpu.make_async_remote_copy(src_ref=src(chip), dst_ref=dst(slot), send_sem=send_sems.at[nflow * n + f],
                                             recv_sem=recv_sems.at[nflow * n + f], device_id=peers[n], device_id_type=MESH)
                for f, (src, dst) in enumerate(flows)]

    def start():
        for cp in local():
            cp.start()
        for n, (px, py, _) in enumerate(peers):
            for cp in copies(n, 2 * px + py, me):
                cp.start()

    def wait():
        for n, (px, py, pc) in enumerate(peers):
            for cp in copies(n, me_chip, 4 * px + 2 * py + pc):
                cp.wait_recv()
        for n, (px, py, _) in enumerate(peers):
            for cp in copies(n, 2 * px + py, me):
                cp.wait_send()
        for cp in local():
            cp.wait()

    return start, wait


def _exchange_flows(srcs, dsts):
    return [((lambda s, src=src: src.at[s]) if src.shape[0] == 4 else (lambda s, src=src: src),
             lambda d, dst=dst: dst.at[d]) for src, dst in zip(srcs, dsts)]


def _exchange_shapes(arrays):
    return [jax.ShapeDtypeStruct((8,) + (a.shape[1:] if a.shape[0] == 4 else a.shape), a.dtype) for a in arrays]


def _exchange_sems(n):
    return [pltpu.SemaphoreType.DMA((7 * n,)), pltpu.SemaphoreType.DMA((7 * n,)), pltpu.SemaphoreType.DMA((n,))]


def exchange_grads(*arrays):
    n = len(arrays)

    def body(*refs):
        start, wait = _device_exchange(_exchange_flows(refs[:n], refs[n:2 * n]), *refs[2 * n:])
        start()
        wait()

    any_spec = pl.BlockSpec(memory_space=pl.ANY)
    return pl.pallas_call(
        body, name="exchange_grads",
        in_specs=[any_spec] * n, out_specs=[any_spec] * n, out_shape=_exchange_shapes(arrays),
        scratch_shapes=_exchange_sems(n),
    )(*arrays)


def adamw_reduce(parts, w, m, v, name, tr):
    rows, width = w.shape
    steps = [p.shape[1] // tr for p in parts]
    offs = [sum(steps[:n]) for n in range(len(parts))]
    c1 = 1.0 - ADAM_B1 ** ADAM_STEP
    c2 = 1.0 - ADAM_B2 ** ADAM_STEP

    def body(*refs):
        p_refs = refs[:len(parts)]
        w_ref, m_ref, v_ref, g_ref, d_ref, nm_ref, nv_ref = refs[len(parts):]
        i = pl.program_id(0)
        p = p_refs[0][...]
        for n in range(1, len(parts)):
            p = jnp.where(i >= offs[n], p_refs[n][...], p)
        g = p[0].astype(F32)
        for n in range(1, 8):
            g = g + p[n].astype(F32)
        g_ref[...] = g
        nm = ADAM_B1 * m_ref[...] + (1.0 - ADAM_B1) * g
        nv = ADAM_B2 * v_ref[...] + (1.0 - ADAM_B2) * (g * g)
        nm_ref[...] = nm
        nv_ref[...] = nv
        d_ref[...] = -ADAM_LR * ((nm / c1) / (jnp.sqrt(nv / c2) + ADAM_EPS) + ADAM_WD * w_ref[...])

    spec = pl.BlockSpec((tr, width), lambda i: (i, 0))
    shape = jax.ShapeDtypeStruct((rows, width), F32)
    return pl.pallas_call(
        body, name=name, grid=(rows // tr,),
        in_specs=[pl.BlockSpec((8, tr, width), lambda i, n=n: (0, jnp.clip(i - offs[n], 0, steps[n] - 1), 0))
                  for n in range(len(parts))] + [spec, spec, spec],
        out_specs=[spec] * 4, out_shape=[shape] * 4,
        compiler_params=_params(("arbitrary",)),
    )(*parts, w, m, v)


SMALL_EARLY = ("b_f", "rel_bias", "w_s", "b_s", "v_gain", "final_g")
WEIGHTS = ("norm_g", "w_in", "b_f", "rel_bias", "w_s", "b_s", "v_gain", "branch_gain", "w_out", "final_g")
PACK_ROW_TILE = 512


def _rows_of(shape):
    return -(-int(np.prod(shape)) // 128)


def _pack(leaves, tile=PACK_ROW_TILE):
    parts = []
    for a in leaves:
        flat = a.reshape(-1).astype(F32)
        parts.append(jnp.pad(flat, (0, _rows_of(a.shape) * 128 - flat.shape[0])))
    flat = jnp.concatenate(parts)
    rows = flat.shape[0] // 128
    total = -(-rows // tile) * tile
    return jnp.pad(flat, (0, (total - rows) * 128)).reshape(total, 128)


def _unpack(slab, shapes):
    out, row = [], 0
    for shp in shapes:
        n = int(np.prod(shp))
        r = _rows_of(shp)
        out.append(slab[row:row + r].reshape(-1)[:n].reshape(shp))
        row += r
    return out


def kernel(x, norm_g, w_in, b_f, rel_bias, w_s, b_s, v_gain, branch_gain, w_out, final_g, loss_target, m_norm_g, m_w_in, m_b_f, m_rel_bias, m_w_s, m_b_s, m_v_gain, m_branch_gain, m_w_out, m_final_g, v_norm_g, v_w_in, v_b_f, v_rel_bias, v_w_s, v_b_s, v_v_gain, v_branch_gain, v_w_out, v_final_g):
    depth = norm_g.shape[0]
    weights = dict(norm_g=norm_g, w_in=w_in, b_f=b_f, rel_bias=rel_bias, w_s=w_s, b_s=b_s, v_gain=v_gain,
                   branch_gain=branch_gain, w_out=w_out, final_g=final_g)
    mom1 = dict(norm_g=m_norm_g, w_in=m_w_in, b_f=m_b_f, rel_bias=m_rel_bias, w_s=m_w_s, b_s=m_b_s,
                v_gain=m_v_gain, branch_gain=m_branch_gain, w_out=m_w_out, final_g=m_final_g)
    mom2 = dict(norm_g=v_norm_g, w_in=v_w_in, b_f=v_b_f, rel_bias=v_rel_bias, w_s=v_w_s, b_s=v_b_s,
                v_gain=v_v_gain, branch_gain=v_branch_gain, w_out=v_w_out, final_g=v_final_g)

    wf = jnp.pad(branch_gain.reshape(-1), (0, 8 * 128 - branch_gain.size)).reshape(8, 128)
    w_in_b, w_out_b = w_in.astype(BF16), w_out.astype(BF16)
    w_in_shards, w_out_shards, gf = gather_weights(w_in_b[:1], w_out_b[:1], wf)
    bg_full = gf.reshape(4, -1)[:, :branch_gain.size].reshape((4,) + branch_gain.shape)
    bg_full = jnp.moveaxis(bg_full, 0, 2).reshape(depth, 4, D_BRANCH)

    layers = [dict(norm_g=norm_g[l], b_f=b_f[l], rel_bias=rel_bias[l], w_s=w_s[l],
                   b_s=b_s[l], v_gain=v_gain[l], branch_gain=bg_full[l]) for l in range(depth)]
    layers[0].update(wp=pack_w_in(w_in_shards)[0], wout=w_out_shards.reshape(D_MODEL, D_MODEL))
    next_shards = [(w_in_b[l + 1], w_out_b[l + 1]) for l in range(depth - 1)]

    loss_part, grad_x, lgrads, dfinal = local_step(x[0], loss_target[0], layers, final_g, next_shards)
    loss = lax.psum(loss_part, ("x", "y", "c"))

    stack = lambda k: jnp.stack([g[k] for g in lgrads])
    d_gain = jnp.moveaxis(stack("branch_gain").reshape(depth, 4, 4, HEAD_DIM), 2, 0).reshape(4, -1)
    d_gain = jnp.pad(d_gain, ((0, 0), (0, 8 * 128 - d_gain.shape[1]))).reshape(4, 8, 128)
    parts_in, parts_gain, parts_norm = exchange_grads(lgrads[0]["w_in_shards"], d_gain, _pack([stack("norm_g")], 16))
    parts = dict(w_in=[lgrads[0]["w_in_top_parts"], parts_in] + [g["w_in_parts"] for g in lgrads[1:]],
                 w_out=[g["w_out_parts"] for g in lgrads])

    outs = {}
    tags = ("grad", "delta", "new_m", "new_v")
    for k in ("w_in", "w_out"):
        rows = depth * weights[k].shape[1]
        flat = lambda a: a.reshape(rows, a.shape[-1])
        res = adamw_reduce(parts[k], flat(weights[k]), flat(mom1[k]), flat(mom2[k]), "adamw_" + k, 256)
        for tag, a in zip(tags, res):
            outs[tag, k] = a.reshape(weights[k].shape)
    gain8 = lambda a: jnp.pad(a.reshape(-1), (0, 8 * 128 - a.size)).reshape(8, 128)
    res = adamw_reduce([parts_gain], gain8(branch_gain), gain8(m_branch_gain), gain8(v_branch_gain), "adamw_gain", 8)
    for tag, a in zip(tags, res):
        outs[tag, "branch_gain"] = a.reshape(-1)[:branch_gain.size].reshape(branch_gain.shape)
    for names, parts_small, tile in ((SMALL_EARLY, lgrads[0]["small_parts"], PACK_ROW_TILE), (("norm_g",), parts_norm, 16)):
        pack_small = lambda d: _pack([d[k] for k in names], tile)
        res = adamw_reduce([parts_small], pack_small(weights), pack_small(mom1), pack_small(mom2),
                           "adamw_" + names[0], tile)
        for tag, slab in zip(tags, res):
            for k, a in zip(names, _unpack(slab, [weights[k].shape for k in names])):
                outs[tag, k] = a
    result = [loss, grad_x[None]]
    for tag in ("grad", "delta", "new_m", "new_v"):
        result += [outs[tag, k] for k in WEIGHTS]
    return tuple(result)
```

```python
import jax
import jax.numpy as jnp
import numpy as np
from jax import lax
from jax.experimental import pallas as pl
from jax.experimental.pallas import tpu as pltpu

F32 = jnp.float32
BF16 = jnp.bfloat16
MESH = pl.DeviceIdType.MESH

D_MODEL = 1024
D_BRANCH = 256
N_HEADS = 4
HEAD_DIM = 64
CHUNK = 64
LOOKBACK = 8
MAX_REL = 128
SG_CHUNK = 128
EPS = 1e-6
N_IN = 3844
N_PACK = 3968
F_COL = 3840
N_SHARD = 961
NEG = -1e30

A_TQ = 128
A_BAND = A_TQ + LOOKBACK * CHUNK
REL_LO = MAX_REL - (CHUNK - 1)
REL_HI = 2 * MAX_REL + 1
A_PAD = LOOKBACK * CHUNK
A_QB = 2048
ATT_T = 256
FOX_TQ = 512
FOX_WIDE = 4
FOX_DEAD2 = -136.0
LOG2E = 1.4426950408889634
SB_TQ = 1024
SB_SUB = 128
SB_BACK = 256
SB_BAND = SB_SUB + SB_BACK
SB_DEAD = -110.0
ROW_T = 512
VMEM_LIMIT = 56 * 1024 * 1024

ADAM_LR = 0.001
ADAM_B1 = 0.9
ADAM_B2 = 0.999
ADAM_EPS = 1e-08
ADAM_WD = 0.01
ADAM_STEP = 10

SEC_A_Q, SEC_A_K, SEC_A_V, SEC_A_G = 0, 256, 512, 768
SEC_B_U, SEC_B_V, SEC_B_G = 1024, 1280, 1536
SEC_C_Q, SEC_C_K, SEC_C_V, SEC_C_G = 1792, 2048, 2304, 2560
SEC_D_Q, SEC_D_K, SEC_D_V, SEC_D_G = 2816, 3072, 3328, 3584
QKV_SECS = (SEC_A_Q, SEC_C_Q, SEC_C_K, SEC_C_V, SEC_D_Q, SEC_D_K, SEC_D_V)
GATE_SECS = (SEC_A_G, SEC_B_G, SEC_C_G, SEC_D_G)


def _dot(a, b):
    return jnp.dot(a, b, preferred_element_type=F32)


def _dot_nt(a, b):
    return lax.dot_general(a, b, (((1,), (1,)), ((), ())), preferred_element_type=F32)


def _dot_tn(a, b):
    return lax.dot_general(a, b, (((0,), (0,)), ((), ())), preferred_element_type=F32)


def _split2(x):
    hi = x.astype(BF16)
    lo = (x - hi.astype(F32)).astype(BF16)
    return hi, lo


def _split3(x):
    hi = x.astype(BF16)
    r = x - hi.astype(F32)
    mid = r.astype(BF16)
    lo = (r - mid.astype(F32)).astype(BF16)
    return hi, mid, lo


def _sigmoid(x):
    return 1.0 / (1.0 + jnp.exp(-x))


def _params(sem=None, vmem=VMEM_LIMIT):
    return pltpu.CompilerParams(dimension_semantics=sem, vmem_limit_bytes=vmem)


def _heads_to_lanes(ref):
    return jnp.concatenate([ref[h] for h in range(N_HEADS)], axis=1)


def inproj_fwd(x, g, wp):
    s = x.shape[0]
    tm = A_PAD

    def body(x_ref, g_ref, w_ref, h_ref, qkv_ref, kva_ref, gates_ref, uv_ref, f_ref):
        xv = x_ref[...]
        r = lax.rsqrt(jnp.mean(xv * xv, axis=-1, keepdims=True) + EPS)
        h = (xv * r * g_ref[...]).astype(BF16)
        h_ref[...] = h
        for n, off in enumerate(QKV_SECS):
            p = _dot(h, w_ref[:, off:off + D_BRANCH])
            for hh in range(N_HEADS):
                qkv_ref[n, hh] = p[:, hh * HEAD_DIM:(hh + 1) * HEAD_DIM].astype(BF16)
        for n, off in enumerate((SEC_A_K, SEC_A_V)):
            p = _dot(h, w_ref[:, off:off + D_BRANCH])
            for hh in range(N_HEADS):
                kva_ref[n, hh] = p[:, hh * HEAD_DIM:(hh + 1) * HEAD_DIM].astype(BF16)
        for n, off in enumerate(GATE_SECS):
            gates_ref[:, n * D_BRANCH:(n + 1) * D_BRANCH] = _dot(h, w_ref[:, off:off + D_BRANCH])
        uv_ref[...] = _dot(h, w_ref[:, SEC_B_U:SEC_B_U + 2 * D_BRANCH])
        f_ref[...] = _dot(h, w_ref[:, F_COL:F_COL + 128])

    return pl.pallas_call(
        body, name="inproj_fwd", grid=(s // tm,),
        in_specs=[pl.BlockSpec((tm, D_MODEL), lambda i: (i, 0)),
                  pl.BlockSpec((1, D_MODEL), lambda i: (0, 0)),
                  pl.BlockSpec((D_MODEL, N_PACK), lambda i: (0, 0))],
        out_specs=[pl.BlockSpec((tm, D_MODEL), lambda i: (i, 0)),
                   pl.BlockSpec((len(QKV_SECS), N_HEADS, tm, HEAD_DIM), lambda i: (0, 0, i, 0)),
                   pl.BlockSpec((2, N_HEADS, tm, HEAD_DIM), lambda i: (0, 0, i + 1, 0)),
                   pl.BlockSpec((tm, D_MODEL), lambda i: (i, 0)),
                   pl.BlockSpec((tm, 2 * D_BRANCH), lambda i: (i, 0)),
                   pl.BlockSpec((tm, 128), lambda i: (i, 0))],
        out_shape=[jax.ShapeDtypeStruct((s, D_MODEL), BF16),
                   jax.ShapeDtypeStruct((len(QKV_SECS), N_HEADS, s, HEAD_DIM), BF16),
                   jax.ShapeDtypeStruct((2, N_HEADS, s + tm, HEAD_DIM), BF16),
                   jax.ShapeDtypeStruct((s, D_MODEL), F32),
                   jax.ShapeDtypeStruct((s, 2 * D_BRANCH), F32),
                   jax.ShapeDtypeStruct((s, 128), F32)],
        compiler_params=_params(("arbitrary",)),
    )(x, g, wp)


def inproj_bwd(dqkv, dgates, duv, dfp, wp, x, g, dres):
    s = x.shape[0]
    tm = A_PAD

    def body(*refs):
        dq_refs = refs[:9]
        dgates_ref, duv_ref, dfp_ref, w_ref, x_ref, g_ref, dres_ref, dp_ref, dx_ref, dg_ref = refs[9:]
        i = pl.program_id(0)
        a_q, a_k, a_v, c_q, c_k, c_v, d_q, d_k, d_v = [_heads_to_lanes(r).astype(BF16) for r in dq_refs]
        dgt = dgates_ref[...]
        duv_b = duv_ref[...].astype(BF16)
        dp = jnp.concatenate(
            [a_q, a_k, a_v, dgt[:, 0:256], duv_b, dgt[:, 256:512], c_q, c_k, c_v, dgt[:, 512:768],
             d_q, d_k, d_v, dgt[:, 768:1024], dfp_ref[...].astype(BF16)], axis=1)
        dp_ref[...] = dp
        dh = _dot_nt(dp, w_ref[...])
        xv = x_ref[...]
        r = lax.rsqrt(jnp.mean(xv * xv, axis=-1, keepdims=True) + EPS)
        xn = xv * r
        u = dh * g_ref[...]
        dx_ref[...] = dres_ref[...] + r * (u - xn * jnp.mean(xn * u, axis=-1, keepdims=True))

        @pl.when(i == 0)
        def _():
            dg_ref[...] = jnp.zeros_like(dg_ref)

        dg_ref[...] += jnp.sum(dh * xn, axis=0, keepdims=True)

    head_spec = pl.BlockSpec((N_HEADS, tm, HEAD_DIM), lambda i: (0, i, 0))
    padded_spec = pl.BlockSpec((N_HEADS, tm, HEAD_DIM), lambda i: (0, i + 1, 0))
    return pl.pallas_call(
        body, name="inproj_bwd", grid=(s // tm,),
        in_specs=[head_spec, padded_spec, padded_spec] + [head_spec] * 6 + [
            pl.BlockSpec((tm, D_MODEL), lambda i: (i, 0)),
            pl.BlockSpec((tm, 2 * D_BRANCH), lambda i: (i, 0)),
            pl.BlockSpec((tm, 128), lambda i: (i, 0)),
            pl.BlockSpec((D_MODEL, N_PACK), lambda i: (0, 0)),
            pl.BlockSpec((tm, D_MODEL), lambda i: (i, 0)),
            pl.BlockSpec((1, D_MODEL), lambda i: (0, 0)),
            pl.BlockSpec((tm, D_MODEL), lambda i: (i, 0))],
        out_specs=[pl.BlockSpec((tm, N_PACK), lambda i: (i, 0)),
                   pl.BlockSpec((tm, D_MODEL), lambda i: (i, 0)),
                   pl.BlockSpec((1, D_MODEL), lambda i: (0, 0))],
        out_shape=[jax.ShapeDtypeStruct((s, N_PACK), BF16),
                   jax.ShapeDtypeStruct((s, D_MODEL), F32),
                   jax.ShapeDtypeStruct((1, D_MODEL), F32)],
        compiler_params=_params(("arbitrary",)),
    )(*dqkv, dgates, duv, dfp, wp, x, g, dres)


def inproj_wgrad(h, dp, half=None, ride=()):
    s, m = h.shape
    tm = min(4 * ROW_T, s)
    tmm = 256
    nsteps = s // tm
    ntile = m // tmm if half is None else m // tmm // 2
    first = 0 if half is None else half * ntile
    nride = len(ride)

    def body(a_ref, b_ref, *refs):
        ride_in, o_ref = refs[:nride], refs[nride]
        ride_out, acc_ref = refs[nride + 1:2 * nride + 1], refs[2 * nride + 1]
        k = pl.program_id(1)
        if nride:
            j = pl.program_id(0)
            start, wait = _device_exchange(_exchange_flows(ride_in, ride_out), *refs[2 * nride + 2:])
            pl.when(jnp.logical_and(j == 0, k == 0))(start)

        @pl.when(k == 0)
        def _():
            acc_ref[...] = jnp.zeros_like(acc_ref)

        acc_ref[...] += _dot_tn(a_ref[...], b_ref[...])

        @pl.when(k == nsteps - 1)
        def _():
            acc = acc_ref[...]
            full = jnp.concatenate([acc[:, :SEC_D_Q], acc[:, F_COL:F_COL + N_HEADS], acc[:, SEC_D_Q:F_COL]], axis=1)
            for n in range(4):
                o_ref[n] = full[:, n * N_SHARD:(n + 1) * N_SHARD].astype(BF16)

        if nride:
            pl.when(jnp.logical_and(j == ntile - 1, k == nsteps - 1))(wait)

    any_spec = pl.BlockSpec(memory_space=pl.ANY)
    return pl.pallas_call(
        body, name="inproj_wgrad_exchange" if nride else "inproj_wgrad", grid=(ntile, nsteps),
        in_specs=[pl.BlockSpec((tm, tmm), lambda j, k: (k, first + j)),
                  pl.BlockSpec((tm, N_PACK), lambda j, k: (k, 0))] + [any_spec] * nride,
        out_specs=[pl.BlockSpec((4, tmm, N_SHARD), lambda j, k: (0, j, 0))] + [any_spec] * nride,
        out_shape=[jax.ShapeDtypeStruct((4, ntile * tmm, N_SHARD), BF16)] + _exchange_shapes(ride),
        scratch_shapes=[pltpu.VMEM((tmm, N_PACK), F32)] + (_exchange_sems(nride) if nride else []),
        compiler_params=_params(("arbitrary", "arbitrary")),
    )(h, dp, *ride)


def _a_specs(s):
    nq = s // A_QB
    per = A_QB // A_PAD
    q_spec = pl.BlockSpec((None, None, A_QB, HEAD_DIM), lambda h, i: (0, h, jnp.minimum(i, nq - 1), 0))
    kv_specs = [pl.BlockSpec((None, None, A_PAD, HEAD_DIM),
                             lambda h, i, n=n, m=m: (n, h, jnp.minimum(per * i + m, per * nq), 0))
                for n in range(2) for m in range(per + 1)]
    t_spec = pl.BlockSpec((None, A_TQ, A_BAND), lambda h, i: (h, 0, 0))
    return nq, q_spec, kv_specs, t_spec


def _a_window(refs, i):
    first = refs[0][...]
    return jnp.concatenate([jnp.where(i > 0, first, jnp.zeros_like(first))] + [r[...] for r in refs[1:]], axis=0)


def _a_scores(q_ref, k, t_ref, i, j):
    rows = slice(j * A_TQ, (j + 1) * A_TQ)
    qs = q_ref[rows, :] * 0.125
    kj = k[j * A_TQ:j * A_TQ + A_BAND, :]
    sc = _dot_nt(qs, kj) + t_ref[...]
    col = lax.broadcasted_iota(jnp.int32, (A_TQ, A_BAND), 1)
    sc = jnp.where(col >= A_PAD - i * A_QB - j * A_TQ, sc, NEG)
    return rows, qs, kj, sc


def mix_a_fwd(qkv, kva, tbias):
    s = qkv.shape[2]
    nq, q_spec, kv_specs, t_spec = _a_specs(s)
    nwin = len(kv_specs) // 2

    def body(*refs):
        q_ref, t_ref, o_ref, lse_ref = refs[0], refs[1 + 2 * nwin], refs[2 + 2 * nwin], refs[3 + 2 * nwin]
        i = pl.program_id(1)
        k = _a_window(refs[1:1 + nwin], i)
        v = _a_window(refs[1 + nwin:1 + 2 * nwin], i)
        for j in range(A_QB // A_TQ):
            rows, _, _, sc = _a_scores(q_ref, k, t_ref, i, j)
            m = jnp.max(sc, axis=-1, keepdims=True)
            p = jnp.exp(sc - m)
            l = jnp.sum(p, axis=-1, keepdims=True)
            o_ref[rows, :] = _dot(p.astype(BF16), v[j * A_TQ:j * A_TQ + A_BAND, :]) / l
            lse_ref[rows, :] = m + jnp.log(l)

    return pl.pallas_call(
        body, name="mix_a_fwd", grid=(N_HEADS, nq),
        in_specs=[q_spec] + kv_specs + [t_spec],
        out_specs=[pl.BlockSpec((None, A_QB, HEAD_DIM), lambda h, i: (h, i, 0)),
                   pl.BlockSpec((None, A_QB, 1), lambda h, i: (h, i, 0))],
        out_shape=[jax.ShapeDtypeStruct((N_HEADS, s, HEAD_DIM), F32),
                   jax.ShapeDtypeStruct((N_HEADS, s, 1), F32)],
        compiler_params=_params(("arbitrary", "arbitrary")),
    )(qkv, *([kva] * (2 * nwin)), tbias)


def mix_a_bwd(qkv, kva, tbias, do, o, lse):
    s = qkv.shape[2]
    nq, q_spec, kv_specs, t_spec = _a_specs(s)
    nwin = len(kv_specs) // 2
    row_spec = lambda w: pl.BlockSpec((None, A_QB, w), lambda h, i: (h, jnp.minimum(i, nq - 1), 0))
    done_spec = pl.BlockSpec((None, A_QB, HEAD_DIM), lambda h, i: (h, i, 0))
    win = A_QB + A_PAD

    def body(*refs):
        q_ref = refs[0]
        t_ref, do_ref, o_ref, lse_ref, dq_ref, dk_ref, dv_ref, dt_ref, dk_win, dv_win = refs[1 + 2 * nwin:]
        i = pl.program_id(1)

        @pl.when(i == 0)
        def _():
            dk_win[...] = jnp.zeros_like(dk_win)
            dv_win[...] = jnp.zeros_like(dv_win)
            dt_ref[...] = jnp.zeros_like(dt_ref)

        @pl.when(i < nq)
        def _():
            k = _a_window(refs[1:1 + nwin], i)
            v = _a_window(refs[1 + nwin:1 + 2 * nwin], i)
            dt = jnp.zeros((A_TQ, A_BAND), F32)
            for j in range(A_QB // A_TQ):
                rows, qs, kj, sc = _a_scores(q_ref, k, t_ref, i, j)
                keys = slice(j * A_TQ, j * A_TQ + A_BAND)
                dob = do_ref[rows, :]
                p = jnp.exp(sc - lse_ref[rows, :])
                delta = jnp.sum(o_ref[rows, :] * dob.astype(F32), axis=-1, keepdims=True)
                ds = p * (_dot_nt(dob, v[keys, :]) - delta)
                dsb = ds.astype(BF16)
                dq_ref[rows, :] = _dot(dsb, kj) * 0.125
                dk_win[keys, :] += _dot_tn(dsb, qs)
                dv_win[keys, :] += _dot_tn(p.astype(BF16), dob)
                dt = dt + ds
            dt_ref[...] += dt

        dk_ref[...] = dk_win[0:A_QB, :]
        dv_ref[...] = dv_win[0:A_QB, :]
        dk_rest = dk_win[A_QB:win, :]
        dv_rest = dv_win[A_QB:win, :]
        dk_win[0:A_PAD, :] = dk_rest
        dv_win[0:A_PAD, :] = dv_rest
        dk_win[A_PAD:win, :] = jnp.zeros((A_QB, HEAD_DIM), F32)
        dv_win[A_PAD:win, :] = jnp.zeros((A_QB, HEAD_DIM), F32)

    return pl.pallas_call(
        body, name="mix_a_bwd", grid=(N_HEADS, nq + 1),
        in_specs=[q_spec] + kv_specs + [t_spec, row_spec(HEAD_DIM), row_spec(HEAD_DIM), row_spec(1)],
        out_specs=[row_spec(HEAD_DIM), done_spec, done_spec, t_spec],
        out_shape=[jax.ShapeDtypeStruct((N_HEADS, s, HEAD_DIM), F32),
                   jax.ShapeDtypeStruct((N_HEADS, s + A_QB, HEAD_DIM), F32),
                   jax.ShapeDtypeStruct((N_HEADS, s + A_QB, HEAD_DIM), F32),
                   jax.ShapeDtypeStruct((N_HEADS, A_TQ, A_BAND), F32)],
        scratch_shapes=[pltpu.VMEM((win, HEAD_DIM), F32), pltpu.VMEM((win, HEAD_DIM), F32)],
        compiler_params=_params(("arbitrary", "arbitrary")),
    )(qkv, *([kva] * (2 * nwin)), tbias, do, o, lse)


def relbias_tile(rel_bias, relmat):
    def body(rb_ref, rel_ref, o_ref):
        rel = rel_ref[...]
        o_ref[...] = jnp.full(o_ref.shape, NEG, F32)

        def step(r, carry):
            hit = rel == r
            for h in range(N_HEADS):
                o_ref[h] = jnp.where(hit, rb_ref[h, r], o_ref[h])
            return carry

        lax.fori_loop(REL_LO, REL_HI, step, 0)

    return pl.pallas_call(
        body, name="relbias_tile",
        in_specs=[pl.BlockSpec(memory_space=pltpu.SMEM), pl.BlockSpec(memory_space=pltpu.VMEM)],
        out_specs=pl.BlockSpec(memory_space=pltpu.VMEM),
        out_shape=jax.ShapeDtypeStruct((N_HEADS, A_TQ, A_BAND), F32),
        compiler_params=_params(),
    )(rel_bias, relmat)


def relbias_grad(dt, relmat):
    def body(dt_ref, rel_ref, o_ref):
        rel = rel_ref[...]
        lane = lax.broadcasted_iota(jnp.int32, (8, 384), 1)
        row = lax.broadcasted_iota(jnp.int32, (8, 384), 0)

        def step(r, acc):
            hit = rel == r
            for h in range(N_HEADS):
                val = jnp.sum(jnp.where(hit, dt_ref[h], 0.0))
                acc = jnp.where((lane == r) & (row == h), val, acc)
            return acc

        o_ref[...] = lax.fori_loop(REL_LO, REL_HI, step, jnp.zeros((8, 384), F32))

    return pl.pallas_call(
        body, name="relbias_grad",
        out_shape=jax.ShapeDtypeStruct((8, 384), F32),
        compiler_params=_params(),
    )(dt, relmat)


def _b_norm(v, gain):
    mu = jnp.mean(v, axis=-1, keepdims=True)
    xc = v - mu
    rstd = lax.rsqrt(jnp.mean(xc * xc, axis=-1, keepdims=True) + EPS)
    xhat = xc * rstd
    return xhat, rstd, xhat * gain


def _tril_mask():
    t = lax.broadcasted_iota(jnp.int32, (SG_CHUNK, SG_CHUNK), 0)
    u = lax.broadcasted_iota(jnp.int32, (SG_CHUNK, SG_CHUNK), 1)
    return u <= t


def mix_b_fwd(uv, gain, w_s, b_col):
    s = uv.shape[0]
    tm = min(ROW_T, s)

    def body(uv_ref, gain_ref, w_ref, b_ref, y_ref):
        tril = _tril_mask()
        ws = [jnp.where(tril, w_ref[g], 0.0).astype(BF16) for g in range(N_HEADS)]
        for c in range(tm // SG_CHUNK):
            rows = slice(c * SG_CHUNK, (c + 1) * SG_CHUNK)
            u = uv_ref[rows, 0:D_BRANCH]
            _, _, vn = _b_norm(uv_ref[rows, D_BRANCH:2 * D_BRANCH], gain_ref[...])
            vnb = vn.astype(BF16)
            outs = []
            for g in range(N_HEADS):
                cols = slice(g * HEAD_DIM, (g + 1) * HEAD_DIM)
                mixed = _dot(ws[g], vnb[:, cols]) + b_ref[g]
                outs.append(u[:, cols] * mixed)
            y_ref[rows, :] = jnp.concatenate(outs, axis=1)

    return pl.pallas_call(
        body, name="mix_b_fwd", grid=(s // tm,),
        in_specs=[pl.BlockSpec((tm, 2 * D_BRANCH), lambda i: (i, 0)),
                  pl.BlockSpec((1, D_BRANCH), lambda i: (0, 0)),
                  pl.BlockSpec((N_HEADS, SG_CHUNK, SG_CHUNK), lambda i: (0, 0, 0)),
                  pl.BlockSpec((N_HEADS, SG_CHUNK, 1), lambda i: (0, 0, 0))],
        out_specs=pl.BlockSpec((tm, D_BRANCH), lambda i: (i, 0)),
        out_shape=jax.ShapeDtypeStruct((s, D_BRANCH), F32),
        compiler_params=_params(("arbitrary",)),
    )(uv, gain, w_s, b_col)


def mix_b_bwd(uv, gain, w_s, b_col, dy):
    s = uv.shape[0]
    tm = min(ROW_T, s)

    def body(uv_ref, gain_ref, w_ref, b_ref, dy_ref, duv_ref, dw_ref, db_ref, dgain_ref):
        i = pl.program_id(0)

        @pl.when(i == 0)
        def _():
            dw_ref[...] = jnp.zeros_like(dw_ref)
            db_ref[...] = jnp.zeros_like(db_ref)
            dgain_ref[...] = jnp.zeros_like(dgain_ref)

        tril = _tril_mask()
        ws = [jnp.where(tril, w_ref[g], 0.0).astype(BF16) for g in range(N_HEADS)]
        gain_v = gain_ref[...]
        for c in range(tm // SG_CHUNK):
            rows = slice(c * SG_CHUNK, (c + 1) * SG_CHUNK)
            u = uv_ref[rows, 0:D_BRANCH]
            xhat, rstd, vn = _b_norm(uv_ref[rows, D_BRANCH:2 * D_BRANCH], gain_v)
            vnb = vn.astype(BF16)
            dyv = dy_ref[rows, :]
            dus, dvns = [], []
            for g in range(N_HEADS):
                cols = slice(g * HEAD_DIM, (g + 1) * HEAD_DIM)
                mixed = _dot(ws[g], vnb[:, cols]) + b_ref[g]
                dus.append(dyv[:, cols] * mixed)
                dmixed = dyv[:, cols] * u[:, cols]
                dmb = dmixed.astype(BF16)
                db_ref[g] += jnp.sum(dmixed, axis=-1, keepdims=True)
                dw_ref[g] += jnp.where(tril, _dot_nt(dmb, vnb[:, cols]), 0.0)
                dvns.append(_dot_tn(ws[g], dmb))
            dvn = jnp.concatenate(dvns, axis=1)
            dgain_ref[...] += jnp.sum(dvn * xhat, axis=0, keepdims=True)
            dxh = dvn * gain_v
            dv = rstd * (dxh - jnp.mean(dxh, axis=-1, keepdims=True)
                         - xhat * jnp.mean(dxh * xhat, axis=-1, keepdims=True))
            duv_ref[rows, :] = jnp.concatenate(dus + [dv], axis=1)

    return pl.pallas_call(
        body, name="mix_b_bwd", grid=(s // tm,),
        in_specs=[pl.BlockSpec((tm, 2 * D_BRANCH), lambda i: (i, 0)),
                  pl.BlockSpec((1, D_BRANCH), lambda i: (0, 0)),
                  pl.BlockSpec((N_HEADS, SG_CHUNK, SG_CHUNK), lambda i: (0, 0, 0)),
                  pl.BlockSpec((N_HEADS, SG_CHUNK, 1), lambda i: (0, 0, 0)),
                  pl.BlockSpec((tm, D_BRANCH), lambda i: (i, 0))],
        out_specs=[pl.BlockSpec((tm, 2 * D_BRANCH), lambda i: (i, 0)),
                   pl.BlockSpec((N_HEADS, SG_CHUNK, SG_CHUNK), lambda i: (0, 0, 0)),
                   pl.BlockSpec((N_HEADS, SG_CHUNK, 1), lambda i: (0, 0, 0)),
                   pl.BlockSpec((1, D_BRANCH), lambda i: (0, 0))],
        out_shape=[jax.ShapeDtypeStruct((s, 2 * D_BRANCH), F32),
                   jax.ShapeDtypeStruct((N_HEADS, SG_CHUNK, SG_CHUNK), F32),
                   jax.ShapeDtypeStruct((N_HEADS, SG_CHUNK, 1), F32),
                   jax.ShapeDtypeStruct((1, D_BRANCH), F32)],
        compiler_params=_params(("arbitrary",)),
    )(uv, gain, w_s, b_col, dy)


def _scan_mats(nrow):
    a = lax.broadcasted_iota(jnp.int32, (128, 128), 0)
    b = lax.broadcasted_iota(jnp.int32, (128, 128), 1)
    r = lax.broadcasted_iota(jnp.int32, (nrow, nrow), 0)
    c = lax.broadcasted_iota(jnp.int32, (nrow, nrow), 1)
    nb = nrow // N_HEADS
    same = (r // nb) == (c // nb)
    return a, b, r, c, same


def _exact_dot(x, m):
    hi, mid, lo = _split3(x)
    return _dot(hi, m) + _dot(mid, m) + _dot(lo, m)


def _exact_dot_left(m, x):
    hi, mid, lo = _split3(x)
    return _dot(m, hi) + _dot(m, mid) + _dot(m, lo)


def fox_gate_fwd(ft, bcol):
    nrow = ft.shape[0]

    def body(f_ref, b_ref, c_ref):
        z = f_ref[...] + b_ref[...]
        ls = jnp.minimum(z, 0.0) - jnp.log(1.0 + jnp.exp(-jnp.abs(z)))
        a, b, r, c, same = _scan_mats(nrow)
        within = _exact_dot(ls, (a <= b).astype(BF16))
        tot = jnp.broadcast_to(within[:, 127:128], within.shape)
        before = _exact_dot_left((same & (c < r)).astype(BF16), tot)
        c_ref[...] = within + before

    return pl.pallas_call(
        body, name="fox_gate_fwd",
        out_shape=jax.ShapeDtypeStruct((nrow, 128), F32),
        compiler_params=_params(),
    )(ft, bcol)


def fox_gate_bwd(ft, bcol, dc):
    nrow = ft.shape[0]

    def body(f_ref, b_ref, dc_ref, df_ref, db_ref):
        a, b, r, c, same = _scan_mats(nrow)
        dcv = dc_ref[...]
        within = _exact_dot(dcv, (a >= b).astype(BF16))
        tot = jnp.broadcast_to(within[:, 0:1], within.shape)
        after = _exact_dot_left((same & (c > r)).astype(BF16), tot)
        dls = within + after
        z = f_ref[...] + b_ref[...]
        dz = dls * _sigmoid(-z)
        df_ref[...] = dz
        rs = jnp.broadcast_to(jnp.sum(dz, axis=-1, keepdims=True), dz.shape)
        hr = lax.broadcasted_iota(jnp.int32, (8, nrow), 0)
        hc = lax.broadcasted_iota(jnp.int32, (8, nrow), 1)
        db_ref[...] = _exact_dot_left((hr == hc // (nrow // N_HEADS)).astype(BF16), rs)

    return pl.pallas_call(
        body, name="fox_gate_bwd",
        out_shape=[jax.ShapeDtypeStruct((nrow, 128), F32), jax.ShapeDtypeStruct((8, 128), F32)],
        compiler_params=_params(),
    )(ft, bcol, dc)


def _att_specs(s, qi, ki, vi):
    q_spec = pl.BlockSpec((None, None, FOX_TQ, HEAD_DIM), lambda h, i: (qi, h, i, 0))
    k_spec = pl.BlockSpec((None, None, s, HEAD_DIM), lambda h, i: (ki, h, 0, 0))
    v_spec = pl.BlockSpec((None, None, s, HEAD_DIM), lambda h, i: (vi, h, 0, 0))
    row_spec = lambda w: pl.BlockSpec((None, FOX_TQ, w), lambda h, i: (h, i, 0))
    gate_spec = pl.BlockSpec((None, s // ATT_T, 1, ATT_T), lambda h, i: (h, 0, 0, 0))
    return q_spec, k_spec, v_spec, row_spec, gate_spec


def _causal(n):
    row = lax.broadcasted_iota(jnp.int32, (n, n), 0)
    col = lax.broadcasted_iota(jnp.int32, (n, n), 1)
    return col <= row


def _gate_row(cr_ref, kb, g):
    if g == 1:
        return cr_ref[kb]
    return jnp.concatenate([cr_ref[kb + n] for n in range(g)], axis=1)


def _fox_walk(i, carry, tile, alive):
    g = FOX_WIDE
    own = FOX_TQ // ATT_T
    nwide = (own * i) // g
    carry = tile(own * i, own, carry, True)
    carry = lax.fori_loop(0, (own * i - nwide * g) // own, lambda n, c: tile(nwide * g, own, c, False), carry)

    def cond(state):
        return jnp.logical_and(state[0] >= 0, state[1] > 0)

    def step(state):
        n = state[0]
        c = tile(n * g, g, state[2:], False)
        return (n - 1, alive(n * g, c)) + tuple(c)

    out = lax.while_loop(cond, step, (nwide - 1, alive(nwide * g, carry)) + tuple(carry))
    return out[2:]


def _fox_reach(qs, k_ref, kmax_ref, cc, i):
    s = k_ref.shape[0]
    rows = 4 * ATT_T

    @pl.when(i == 0)
    def _():
        def chunk(n, mx):
            kc = k_ref[pl.ds(pl.multiple_of(n * rows, rows), rows), :].astype(F32)
            return jnp.maximum(mx, jnp.max(jnp.sum(kc * kc, axis=-1, keepdims=True)))

        kmax_ref[0] = jnp.sqrt(lax.fori_loop(0, s // rows, chunk, jnp.float32(0.0)))

    qf = qs.astype(F32)
    return jnp.sqrt(jnp.sum(qf * qf, axis=-1, keepdims=True)) * kmax_ref[0] + cc


def _gate_col(cr_ref, i):
    row = lax.broadcasted_iota(jnp.int32, (ATT_T, ATT_T), 0)
    col = lax.broadcasted_iota(jnp.int32, (ATT_T, ATT_T), 1)
    own = FOX_TQ // ATT_T
    return jnp.concatenate([jnp.sum(jnp.where(row == col, cr_ref[own * i + n], 0.0), axis=-1, keepdims=True)
                            for n in range(own)], axis=0)


def _fox_scores(qs, k, cc, crow, masked):
    sc = (_dot_nt(qs, k) + (cc - crow)) * LOG2E
    if masked:
        sc = jnp.where(_causal(FOX_TQ), sc, NEG)
    return sc


def fox_fwd(qkv, c_row, ride=()):
    s = qkv.shape[2]
    t = ATT_T
    nq = s // FOX_TQ
    q_spec, k_spec, v_spec, row_spec, gate_spec = _att_specs(s, 1, 2, 3)
    rows = 4 * t
    nride = len(ride)

    def body(q_ref, k_ref, v_ref, cr_ref, *refs):
        ride_in, refs = refs[:nride], refs[nride:]
        o_ref, ref_ref, rl_ref = refs[:3]
        ride_out, refs = refs[3:3 + nride], refs[3 + nride:]
        v1_ref, kmax_ref = refs[:2]
        i = pl.program_id(1)
        if nride:
            h = pl.program_id(0)
            start, wait = _chip_gather([(src, lambda slot, dst=dst: dst.at[slot]) for src, dst in zip(ride_in, ride_out)],
                                       *refs[2:])
            pl.when(jnp.logical_and(h == 0, i == 0))(start)

        @pl.when(i == 0)
        def _():
            def chunk(n, carry):
                r0 = pl.multiple_of(n * rows, rows)
                v1_ref[pl.ds(r0, rows), :] = jnp.concatenate(
                    [v_ref[pl.ds(r0, rows), :], jnp.ones((rows, HEAD_DIM), BF16)], axis=1)
                return carry

            lax.fori_loop(0, s // rows, chunk, 0)

        qs = q_ref[...] * 0.125
        cc = _gate_col(cr_ref, i)
        reach = _fox_reach(qs, k_ref, kmax_ref, cc, i) * LOG2E

        def alive(kb, carry):
            return (jnp.max(reach - cr_ref[kb][:, 0:1] * LOG2E - carry[0]) > FOX_DEAD2).astype(jnp.int32)

        def tile(kb, g, carry, masked):
            m, acc = carry
            k0 = pl.multiple_of(kb * t, t)
            sc = _fox_scores(qs, k_ref[pl.ds(k0, g * t), :], cc, _gate_row(cr_ref, kb, g), masked)
            m_new = jnp.maximum(m, jnp.ceil(jnp.max(sc, axis=-1, keepdims=True)))
            pb = jnp.exp2(sc - m_new).astype(BF16)
            acc = jnp.exp2(m - m_new) * acc + _dot(pb, v1_ref[pl.ds(k0, g * t), :])
            return m_new, acc

        init = (jnp.full((FOX_TQ, 1), NEG, F32), jnp.zeros((FOX_TQ, 2 * HEAD_DIM), F32))
        m, acc = _fox_walk(i, init, tile, alive)
        rl = 1.0 / acc[:, HEAD_DIM:HEAD_DIM + 1]
        o_ref[...] = acc[:, 0:HEAD_DIM] * rl
        ref_ref[...] = m
        rl_ref[...] = rl
        if nride:
            pl.when(jnp.logical_and(h == N_HEADS - 1, i == nq - 1))(wait)

    any_spec = pl.BlockSpec(memory_space=pl.ANY)
    ride_sems = [pltpu.SemaphoreType.DMA((3 * nride,)), pltpu.SemaphoreType.DMA((3 * nride,)),
                 pltpu.SemaphoreType.DMA((nride,))] if nride else []
    return pl.pallas_call(
        body, name="fox_fwd_gather" if nride else "fox_fwd", grid=(N_HEADS, nq),
        in_specs=[q_spec, k_spec, v_spec, gate_spec] + [any_spec] * nride,
        out_specs=[row_spec(HEAD_DIM), row_spec(1), row_spec(1)] + [any_spec] * nride,
        out_shape=[jax.ShapeDtypeStruct((N_HEADS, s, HEAD_DIM), F32),
                   jax.ShapeDtypeStruct((N_HEADS, s, 1), F32),
                   jax.ShapeDtypeStruct((N_HEADS, s, 1), F32)]
        + [jax.ShapeDtypeStruct((4,) + a.shape, a.dtype) for a in ride],
        scratch_shapes=[pltpu.VMEM((s, 2 * HEAD_DIM), BF16), pltpu.SMEM((1,), F32)] + ride_sems,
        compiler_params=_params(("arbitrary", "arbitrary")),
    )(qkv, qkv, qkv, c_row, *ride)


def fox_bwd(qkv, c_row, do, o, ref, rl, ride=()):
    s = qkv.shape[2]
    t = ATT_T
    nq = s // FOX_TQ
    q_spec, k_spec, v_spec, row_spec, gate_spec = _att_specs(s, 1, 2, 3)
    any_spec = pl.BlockSpec(memory_space=pl.ANY)
    nride = len(ride)

    def body(q_ref, k_ref, v_ref, cr_ref, do_ref, o_ref, ref_ref, rl_ref, *refs):
        ride_in, refs = refs[:nride], refs[nride:]
        dq_ref, dk_hbm, dv_hbm, dc_ref = refs[:4]
        ride_out, refs = refs[4:4 + nride], refs[4 + nride:]
        dk_acc, dv_acc, kmax_ref = refs[:3]
        h = pl.program_id(0)
        i = pl.program_id(1)
        if nride:
            start, wait = _device_exchange(_exchange_flows(ride_in, ride_out), *refs[3:])
            pl.when(jnp.logical_and(h == 0, i == 0))(start)

        @pl.when(i == 0)
        def _():
            dk_acc[...] = jnp.zeros_like(dk_acc)
            dv_acc[...] = jnp.zeros_like(dv_acc)
            dc_ref[...] = jnp.zeros_like(dc_ref)

        qs = q_ref[...] * 0.125
        ref = ref_ref[...]
        rl = rl_ref[...]
        dob = (do_ref[...].astype(F32) * rl).astype(BF16)
        delta = jnp.sum(o_ref[...] * dob.astype(F32), axis=-1, keepdims=True)
        cc = _gate_col(cr_ref, i)
        margin = _fox_reach(qs, k_ref, kmax_ref, cc, i) * LOG2E - ref

        def alive(kb, carry):
            return (jnp.max(margin - cr_ref[kb][:, 0:1] * LOG2E) > FOX_DEAD2).astype(jnp.int32)

        def tile(kb, g, carry, masked):
            dq, = carry
            k0 = pl.multiple_of(kb * t, t)
            k = k_ref[pl.ds(k0, g * t), :]
            sc = _fox_scores(qs, k, cc, _gate_row(cr_ref, kb, g), masked)
            wb = jnp.exp2(sc - ref).astype(BF16)
            ds = wb.astype(F32) * (_dot_nt(dob, v_ref[pl.ds(k0, g * t), :]) - delta)
            dsb = ds.astype(BF16)
            dk_acc[pl.ds(k0, g * t), :] += _dot_tn(dsb, qs)
            dv_acc[pl.ds(k0, g * t), :] += _dot_tn(wb, dob)
            dcs = -jnp.sum(ds, axis=0, keepdims=True)
            for n in range(g):
                dc_ref[kb + n] += dcs[:, n * t:(n + 1) * t]
            return (dq + _dot(dsb, k),)

        dq, = _fox_walk(i, (jnp.zeros((FOX_TQ, HEAD_DIM), F32),), tile, alive)
        dq_ref[...] = dq * 0.125

        @pl.when(i == nq - 1)
        def _():
            pltpu.sync_copy(dk_acc, dk_hbm.at[h])
            pltpu.sync_copy(dv_acc, dv_hbm.at[h])

        if nride:
            pl.when(jnp.logical_and(h == N_HEADS - 1, i == nq - 1))(wait)

    return pl.pallas_call(
        body, name="fox_bwd_exchange" if nride else "fox_bwd", grid=(N_HEADS, nq),
        in_specs=[q_spec, k_spec, v_spec,
                  gate_spec,
                  row_spec(HEAD_DIM), row_spec(HEAD_DIM), row_spec(1), row_spec(1)] + [any_spec] * nride,
        out_specs=[row_spec(HEAD_DIM), any_spec, any_spec,
                   gate_spec] + [any_spec] * nride,
        out_shape=[jax.ShapeDtypeStruct((N_HEADS, s, HEAD_DIM), F32),
                   jax.ShapeDtypeStruct((N_HEADS, s, HEAD_DIM), F32),
                   jax.ShapeDtypeStruct((N_HEADS, s, HEAD_DIM), F32),
                   jax.ShapeDtypeStruct((N_HEADS, s // t, 1, t), F32)] + _exchange_shapes(ride),
        scratch_shapes=[pltpu.VMEM((s, HEAD_DIM), F32), pltpu.VMEM((s, HEAD_DIM), F32), pltpu.SMEM((1,), F32)]
        + (_exchange_sems(nride) if nride else []),
        compiler_params=_params(("arbitrary", "arbitrary")),
    )(qkv, qkv, qkv, c_row, do, o, ref, rl, *ride)


def _sb_valid(nrows, ahead):
    row = lax.broadcasted_iota(jnp.int32, (nrows, ATT_T), 0)
    col = lax.broadcasted_iota(jnp.int32, (nrows, ATT_T), 1)
    return col + ahead < row


def _sb_band_valid(nsub, i):
    shape = (nsub * SB_SUB, SB_BAND)
    row = lax.broadcasted_iota(jnp.int32, shape, 0)
    col = lax.broadcasted_iota(jnp.int32, shape, 1)
    first = i * SB_TQ + (row - (row & (SB_SUB - 1)))
    valid = col < (row & (SB_SUB - 1)) + jnp.minimum(first, SB_BACK)
    return valid, first[:, 0:1] > SB_BACK


def _sb_logits(qs, k):
    z = _dot_nt(qs, k)
    sp = jnp.log(1.0 + jnp.exp(-jnp.abs(z)))
    return jnp.minimum(z, 0.0) - sp, -jnp.maximum(z, 0.0) - sp


def _sb_weights(ls, lm, run, valid):
    if valid is not None:
        lm = jnp.where(valid, lm, 0.0)
    n = lm.shape[1]
    row = lax.broadcasted_iota(jnp.int32, (n, n), 0)
    col = lax.broadcasted_iota(jnp.int32, (n, n), 1)
    later = (row > col).astype(BF16)
    hi, lo = _split2(lm)
    between = _dot(hi, later) + _dot(lo, later)
    if run is not None:
        between = run + between
    a = jnp.exp(ls + between)
    if valid is not None:
        a = jnp.where(valid, a, 0.0)
    return lm, a


def _sb_band_start(i, j):
    return pl.multiple_of(jnp.maximum(i * SB_TQ + j * SB_SUB - SB_BACK, 0), SB_SUB)


def _sb_tile(qs, k, run, valid):
    ls, lm = _sb_logits(qs, k)
    lm, a = _sb_weights(ls, lm, run, valid)
    return ls, lm, a


def _sb_band(i, qs_all, k_ref):
    nsub = qs_all.shape[0] // SB_SUB
    valid, open_left = _sb_band_valid(nsub, i)
    starts = [_sb_band_start(i, j) for j in range(nsub)]
    kwins = [k_ref[pl.ds(k0, SB_BAND), :] for k0 in starts]
    parts = [_sb_logits(qs_all[j * SB_SUB:(j + 1) * SB_SUB], kwins[j]) for j in range(nsub)]
    ls = jnp.concatenate([p[0] for p in parts], axis=0)
    lm, a = _sb_weights(ls, jnp.concatenate([p[1] for p in parts], axis=0), None, valid)
    return starts, kwins, ls, lm, a, valid, open_left


def _sb_suffix(g, run_g):
    n = g.shape[1]
    row = lax.broadcasted_iota(jnp.int32, (n, n), 0)
    col = lax.broadcasted_iota(jnp.int32, (n, n), 1)
    from_here = (row >= col).astype(BF16)
    hi, lo = _split2(g)
    out = _dot(hi, from_here) + _dot(lo, from_here)
    return out if run_g is None else run_g + out


def _sb_walk(i, carry, tile):
    def alive_of(c):
        return (jnp.max(c[0]) > SB_DEAD).astype(jnp.int32)

    def cond(state):
        n, alive = state[0], state[1]
        return jnp.logical_and(n < i, alive > 0)

    def step(state):
        n = state[0]
        c = tile(i - 1 - n, state[2:], False)
        return (n + 1, alive_of(c)) + tuple(c)

    out = lax.while_loop(cond, step, (jnp.int32(0), alive_of(carry)) + tuple(carry))
    return out[2:]


def _sb_specs(s):
    tq = SB_TQ
    q_spec = pl.BlockSpec((None, None, tq, HEAD_DIM), lambda h, i: (4, h, i, 0))
    k_spec = pl.BlockSpec((None, None, s, HEAD_DIM), lambda h, i: (5, h, 0, 0))
    v_spec = pl.BlockSpec((None, None, s, HEAD_DIM), lambda h, i: (6, h, 0, 0))
    row_spec = pl.BlockSpec((None, tq, HEAD_DIM), lambda h, i: (h, i, 0))
    band_spec = pl.BlockSpec((None, None, 1, 128), lambda h, i: (h, i, 0, 0))
    return tq, q_spec, k_spec, v_spec, row_spec, band_spec


def _sb_block(b, row0, tile, zero):
    t = ATT_T
    lo, hi, both = slice(row0, row0 + t), slice(row0 + t, row0 + 2 * t), slice(row0, row0 + 2 * t)
    c_hi = tile(2 * b + 1, hi, zero, 0)
    c_lo = tile(2 * b, lo, zero, 0)
    c_hi = tile(2 * b, hi, c_hi, None)
    carry = tuple(jnp.concatenate([x, y], axis=0) for x, y in zip(c_lo, c_hi))
    return _sb_walk(2 * b, carry, lambda kb, c, _: tile(kb, both, c, None))


def sb_fwd(qkv):
    s = qkv.shape[2]
    t = ATT_T
    tq, q_spec, k_spec, v_spec, row_spec, band_spec = _sb_specs(s)

    def body(q_ref, k_ref, v_ref, o_ref, band_ref, done_ref):
        i = pl.program_id(1)
        qs = q_ref[...] * 0.125
        starts, _, _, lm, a, _, open_left = _sb_band(i, qs, k_ref)
        ab = a.astype(BF16)
        for j, k0 in enumerate(starts):
            rows = slice(j * SB_SUB, (j + 1) * SB_SUB)
            o_ref[rows, :] = _dot(ab[rows], v_ref[pl.ds(k0, SB_BAND), :])
        worst = jnp.max(jnp.where(open_left, jnp.sum(lm, axis=-1, keepdims=True), NEG))
        done_ref[0] = (worst <= SB_DEAD).astype(jnp.int32)

        @pl.when(done_ref[0] == 0)
        def _():
            def tile(kb, rows, carry, ahead):
                run, acc = carry
                k0 = pl.multiple_of(kb * t, t)
                valid = None if ahead is None else _sb_valid(t, ahead)
                _, lm, a = _sb_tile(qs[rows], k_ref[pl.ds(k0, t), :], run, valid)
                acc = acc + _dot(a.astype(BF16), v_ref[pl.ds(k0, t), :])
                return run + jnp.sum(lm, axis=-1, keepdims=True), acc

            for n in range(tq // (2 * t)):
                _, acc = _sb_block(i * (tq // (2 * t)) + n, n * 2 * t, tile,
                                   (jnp.zeros((t, 1), F32), jnp.zeros((t, HEAD_DIM), F32)))
                o_ref[n * 2 * t:(n + 1) * 2 * t, :] = acc

        band_ref[...] = jnp.full(band_ref.shape, done_ref[0], jnp.int32).astype(F32)

    return pl.pallas_call(
        body, name="sb_fwd", grid=(N_HEADS, s // tq),
        in_specs=[q_spec, k_spec, v_spec],
        out_specs=[row_spec, band_spec],
        out_shape=[jax.ShapeDtypeStruct((N_HEADS, s, HEAD_DIM), F32),
                   jax.ShapeDtypeStruct((N_HEADS, s // tq, 1, 128), F32)],
        scratch_shapes=[pltpu.SMEM((1,), jnp.int32)],
        compiler_params=_params(("arbitrary", "arbitrary")),
    )(qkv, qkv, qkv)


def sb_bwd(qkv, do, o, band, ride=()):
    s = qkv.shape[2]
    t = ATT_T
    tq, q_spec, k_spec, v_spec, row_spec, band_spec = _sb_specs(s)
    nq = s // tq
    any_spec = pl.BlockSpec(memory_space=pl.ANY)
    nride = len(ride)

    def body(q_ref, k_ref, v_ref, do_ref, o_ref, band_ref, *refs):
        ride_in, refs = refs[:nride], refs[nride:]
        dq_ref, dk_hbm, dv_hbm = refs[:3]
        ride_out, refs = refs[3:3 + nride], refs[3 + nride:]
        dk_acc, dv_acc = refs[:2]
        h = pl.program_id(0)
        i = pl.program_id(1)
        if nride:
            start, wait = _device_exchange(_exchange_flows(ride_in, ride_out), *refs[2:])
            pl.when(jnp.logical_and(h == 0, i == 0))(start)

        @pl.when(i == 0)
        def _():
            dk_acc[...] = jnp.zeros_like(dk_acc)
            dv_acc[...] = jnp.zeros_like(dv_acc)

        qs_all = q_ref[...] * 0.125
        dob_all = do_ref[...]
        tot_all = jnp.sum(o_ref[...] * dob_all.astype(F32), axis=-1, keepdims=True)
        on_band = jnp.max(band_ref[...]) > 0.5

        def grads(qs, dob, tot, k, v, k0, run, run_g, valid):
            ls, lm, a = _sb_tile(qs, k, run, valid)
            ab = a.astype(BF16)
            g = ab.astype(F32) * _dot_nt(dob, v)
            g_left = tot - _sb_suffix(g, run_g)
            dz = g - jnp.exp(ls) * (g + g_left)
            if valid is not None:
                dz = jnp.where(valid, dz, 0.0)
            dzb = dz.astype(BF16)
            n = k.shape[0]
            dk_acc[pl.ds(k0, n), :] += _dot_tn(dzb, qs)
            dv_acc[pl.ds(k0, n), :] += _dot_tn(ab, dob)
            return dzb, lm, g

        @pl.when(on_band)
        def _():
            starts, kwins, ls, _, a, valid, _ = _sb_band(i, qs_all, k_ref)
            ab = a.astype(BF16)
            subs = [slice(j * SB_SUB, (j + 1) * SB_SUB) for j in range(len(starts))]
            vwins = [v_ref[pl.ds(k0, SB_BAND), :] for k0 in starts]
            g = ab.astype(F32) * jnp.concatenate([_dot_nt(dob_all[r], v) for r, v in zip(subs, vwins)], axis=0)
            dz = jnp.where(valid, g - jnp.exp(ls) * (g + (tot_all - _sb_suffix(g, None))), 0.0)
            dzb = dz.astype(BF16)
            for r, k0, k in zip(subs, starts, kwins):
                dq_ref[r, :] = _dot(dzb[r], k) * 0.125
                dk_acc[pl.ds(k0, SB_BAND), :] += _dot_tn(dzb[r], qs_all[r])
                dv_acc[pl.ds(k0, SB_BAND), :] += _dot_tn(ab[r], dob_all[r])

        @pl.when(jnp.logical_not(on_band))
        def _():
            def tile(kb, rows, carry, ahead):
                run, run_g, dq = carry
                k0 = pl.multiple_of(kb * t, t)
                k = k_ref[pl.ds(k0, t), :]
                valid = None if ahead is None else _sb_valid(t, ahead)
                dzb, lm, g = grads(qs_all[rows], dob_all[rows], tot_all[rows], k, v_ref[pl.ds(k0, t), :], k0,
                                   run, run_g, valid)
                return (run + jnp.sum(lm, axis=-1, keepdims=True),
                        run_g + jnp.sum(g, axis=-1, keepdims=True),
                        dq + _dot(dzb, k))

            zero = jnp.zeros((t, 1), F32)
            for n in range(tq // (2 * t)):
                _, _, dq = _sb_block(i * (tq // (2 * t)) + n, n * 2 * t, tile, (zero, zero, jnp.zeros((t, HEAD_DIM), F32)))
                dq_ref[n * 2 * t:(n + 1) * 2 * t, :] = dq * 0.125

        @pl.when(i == nq - 1)
        def _():
            pltpu.sync_copy(dk_acc, dk_hbm.at[h])
            pltpu.sync_copy(dv_acc, dv_hbm.at[h])

        if nride:
            pl.when(jnp.logical_and(h == N_HEADS - 1, i == nq - 1))(wait)

    return pl.pallas_call(
        body, name="sb_bwd_exchange" if nride else "sb_bwd", grid=(N_HEADS, nq),
        in_specs=[q_spec, k_spec, v_spec, row_spec, row_spec, band_spec] + [any_spec] * nride,
        out_specs=[row_spec, any_spec, any_spec] + [any_spec] * nride,
        out_shape=[jax.ShapeDtypeStruct((N_HEADS, s, HEAD_DIM), F32)] * 3 + _exchange_shapes(ride),
        scratch_shapes=[pltpu.VMEM((s, HEAD_DIM), F32), pltpu.VMEM((s, HEAD_DIM), F32)]
        + (_exchange_sems(nride) if nride else []),
        compiler_params=_params(("arbitrary", "arbitrary")),
    )(qkv, qkv, qkv, do, o, band, *ride)


def _branch_inputs(refs, br):
    ya_ref, yb_ref, yc_ref, yd_ref = refs
    if br == 1:
        return yb_ref[...]
    return _heads_to_lanes((ya_ref, None, yc_ref, yd_ref)[br])


def outproj_fwd(x, ya, yb, yc, yd, gates, bg, wout):
    s = x.shape[0]
    tm = min(ROW_T, s)

    def body(x_ref, ya_ref, yb_ref, yc_ref, yd_ref, gates_ref, bg_ref, w_ref, out_ref):
        pieces = []
        for br in range(4):
            cols = slice(br * D_BRANCH, (br + 1) * D_BRANCH)
            y = _branch_inputs((ya_ref, yb_ref, yc_ref, yd_ref), br)
            r = lax.rsqrt(jnp.mean(y * y, axis=-1, keepdims=True) + EPS)
            gt = gates_ref[:, cols]
            pieces.append((y * r * bg_ref[:, cols]) * (gt * _sigmoid(gt)))
        merged = jnp.concatenate(pieces, axis=1).astype(BF16)
        out_ref[...] = x_ref[...] + _dot(merged, w_ref[...])

    head_spec = pl.BlockSpec((N_HEADS, tm, HEAD_DIM), lambda i: (0, i, 0))
    return pl.pallas_call(
        body, name="outproj_fwd", grid=(s // tm,),
        in_specs=[pl.BlockSpec((tm, D_MODEL), lambda i: (i, 0)),
                  head_spec, pl.BlockSpec((tm, D_BRANCH), lambda i: (i, 0)), head_spec, head_spec,
                  pl.BlockSpec((tm, D_MODEL), lambda i: (i, 0)),
                  pl.BlockSpec((1, D_MODEL), lambda i: (0, 0)),
                  pl.BlockSpec((D_MODEL, D_MODEL), lambda i: (0, 0))],
        out_specs=pl.BlockSpec((tm, D_MODEL), lambda i: (i, 0)),
        out_shape=jax.ShapeDtypeStruct((s, D_MODEL), F32),
        compiler_params=_params(("arbitrary",)),
    )(x, ya, yb, yc, yd, gates, bg, wout)


def outproj_bwd(dout, ya, yb, yc, yd, gates, bg, wout):
    s = dout.shape[0]
    tm = min(ROW_T, s)

    def body(dout_ref, ya_ref, yb_ref, yc_ref, yd_ref, gates_ref, bg_ref, w_ref,
             dya_ref, dyb_ref, dyc_ref, dyd_ref, dgates_ref, dbg_ref, dw_ref):
        i = pl.program_id(0)

        @pl.when(i == 0)
        def _():
            dbg_ref[...] = jnp.zeros_like(dbg_ref)
            dw_ref[...] = jnp.zeros_like(dw_ref)

        doutb = dout_ref[...].astype(BF16)
        dmerged = _dot_nt(doutb, w_ref[...])
        pieces = []
        for br in range(4):
            cols = slice(br * D_BRANCH, (br + 1) * D_BRANCH)
            y = _branch_inputs((ya_ref, yb_ref, yc_ref, yd_ref), br)
            r = lax.rsqrt(jnp.mean(y * y, axis=-1, keepdims=True) + EPS)
            yn = y * r
            bgv = bg_ref[:, cols]
            gt = gates_ref[:, cols]
            sig = _sigmoid(gt)
            act = gt * sig
            n = yn * bgv
            pieces.append(n * act)
            dm = dmerged[:, cols]
            dn = dm * act
            dgates_ref[:, cols] = (dm * n * (sig * (1.0 + gt * (1.0 - sig)))).astype(BF16)
            dbg_ref[:, cols] += jnp.sum(dn * yn, axis=0, keepdims=True)
            u = dn * bgv
            dy = r * (u - yn * jnp.mean(yn * u, axis=-1, keepdims=True))
            if br == 1:
                dyb_ref[...] = dy
            else:
                dref = (dya_ref, None, dyc_ref, dyd_ref)[br]
                for hh in range(N_HEADS):
                    dref[hh] = dy[:, hh * HEAD_DIM:(hh + 1) * HEAD_DIM].astype(BF16)
        merged = jnp.concatenate(pieces, axis=1).astype(BF16)
        dw_ref[...] += _dot_tn(merged, doutb)

    head_spec = pl.BlockSpec((N_HEADS, tm, HEAD_DIM), lambda i: (0, i, 0))
    head_shape = jax.ShapeDtypeStruct((N_HEADS, s, HEAD_DIM), BF16)
    return pl.pallas_call(
        body, name="outproj_bwd", grid=(s // tm,),
        in_specs=[pl.BlockSpec((tm, D_MODEL), lambda i: (i, 0)),
                  head_spec, pl.BlockSpec((tm, D_BRANCH), lambda i: (i, 0)), head_spec, head_spec,
                  pl.BlockSpec((tm, D_MODEL), lambda i: (i, 0)),
                  pl.BlockSpec((1, D_MODEL), lambda i: (0, 0)),
                  pl.BlockSpec((D_MODEL, D_MODEL), lambda i: (0, 0))],
        out_specs=[head_spec, pl.BlockSpec((tm, D_BRANCH), lambda i: (i, 0)), head_spec, head_spec,
                   pl.BlockSpec((tm, D_MODEL), lambda i: (i, 0)),
                   pl.BlockSpec((1, D_MODEL), lambda i: (0, 0)),
                   pl.BlockSpec((D_MODEL, D_MODEL), lambda i: (0, 0))],
        out_shape=[head_shape, jax.ShapeDtypeStruct((s, D_BRANCH), F32), head_shape, head_shape,
                   jax.ShapeDtypeStruct((s, D_MODEL), BF16),
                   jax.ShapeDtypeStruct((1, D_MODEL), F32),
                   jax.ShapeDtypeStruct((D_MODEL, D_MODEL), F32)],
        compiler_params=_params(("arbitrary",)),
    )(dout, ya, yb, yc, yd, gates, bg, wout)


def final_loss(x, tgt, g):
    s = x.shape[0]
    tm = min(ROW_T, s)

    def body(x_ref, t_ref, g_ref, loss_ref, dx_ref, dg_ref):
        i = pl.program_id(0)

        @pl.when(i == 0)
        def _():
            loss_ref[...] = jnp.zeros_like(loss_ref)
            dg_ref[...] = jnp.zeros_like(dg_ref)

        xv = x_ref[...]
        gv = g_ref[...]
        r = lax.rsqrt(jnp.mean(xv * xv, axis=-1, keepdims=True) + EPS)
        xn = xv * r
        err = xn * gv - t_ref[...]
        loss_ref[...] += jnp.sum(err * err) * (0.5 / D_MODEL)
        dy = err * (1.0 / D_MODEL)
        u = dy * gv
        dx_ref[...] = r * (u - xn * jnp.mean(xn * u, axis=-1, keepdims=True))
        dg_ref[...] += jnp.sum(dy * xn, axis=0, keepdims=True)

    return pl.pallas_call(
        body, name="final_loss", grid=(s // tm,),
        in_specs=[pl.BlockSpec((tm, D_MODEL), lambda i: (i, 0)),
                  pl.BlockSpec((tm, D_MODEL), lambda i: (i, 0)),
                  pl.BlockSpec((1, D_MODEL), lambda i: (0, 0))],
        out_specs=[pl.BlockSpec((1, 128), lambda i: (0, 0)),
                   pl.BlockSpec((tm, D_MODEL), lambda i: (i, 0)),
                   pl.BlockSpec((1, D_MODEL), lambda i: (0, 0))],
        out_shape=[jax.ShapeDtypeStruct((1, 128), F32),
                   jax.ShapeDtypeStruct((s, D_MODEL), F32),
                   jax.ShapeDtypeStruct((1, D_MODEL), F32)],
        compiler_params=_params(("arbitrary",)),
    )(x, tgt, g)


def _rel_index():
    i = np.arange(A_TQ)[:, None]
    j = np.arange(A_BAND)[None, :]
    rel = np.clip(i - j + (A_BAND - A_TQ), -MAX_REL, MAX_REL) + MAX_REL
    dchunk = i // CHUNK + LOOKBACK - j // CHUNK
    valid = (dchunk >= 0) & (dchunk <= LOOKBACK)
    return jnp.asarray(np.where(valid, rel, -1).astype(np.int32))


def _layer_consts(p):
    tbias = relbias_tile(p["rel_bias"], _rel_index())
    return dict(
        norm_g=p["norm_g"].reshape(1, D_MODEL),
        v_gain=p["v_gain"].reshape(1, D_BRANCH),
        b_col=p["b_s"].reshape(N_HEADS, SG_CHUNK, 1),
        bg=p["branch_gain"].reshape(1, D_MODEL),
        tbias=tbias,
    )


def _gate_layout(fp, b_f, s):
    nb = s // 128
    ft = fp[:, :N_HEADS].T.reshape(N_HEADS * nb, 128)
    bcol = jnp.repeat(b_f, nb).reshape(N_HEADS * nb, 1)
    return ft, bcol


def layer_fwd(x, p, ride=()):
    s = x.shape[0]
    c = _layer_consts(p)
    h, qkv, kva, gates, uv, fp = inproj_fwd(x, c["norm_g"], p["wp"])
    ya, lse_a = mix_a_fwd(qkv, kva, c["tbias"])
    yb = mix_b_fwd(uv, c["v_gain"], p["w_s"], c["b_col"])
    ft, bcol = _gate_layout(fp, p["b_f"], s)
    c_row = fox_gate_fwd(ft, bcol).reshape(N_HEADS, s // ATT_T, 1, ATT_T)
    yc, ref_c, rl_c, *rode = fox_fwd(qkv, c_row, ride)
    yd, band_d = sb_fwd(qkv)
    out = outproj_fwd(x, ya, yb, yc, yd, gates, c["bg"], p["wout"])
    saved = dict(consts=c, x=x, h=h, qkv=qkv, gates=gates, uv=uv, kva=kva, ft=ft, bcol=bcol,
                 c_row=c_row, ya=ya, lse_a=lse_a, yb=yb, yc=yc, ref_c=ref_c, rl_c=rl_c, yd=yd, band_d=band_d)
    return out, saved, rode


def layer_bwd(dout, p, sv, exchange=False, upper_w_in=None, small_ride=None):
    s = dout.shape[0]
    c = sv["consts"]
    dya, dyb, dyc, dyd, dgates, dbg, dwout = outproj_bwd(
        dout, sv["ya"], sv["yb"], sv["yc"], sv["yd"], sv["gates"], c["bg"], p["wout"])
    dqa, dka, dva, dt = mix_a_bwd(sv["qkv"], sv["kva"], c["tbias"], dya, sv["ya"], sv["lse_a"])
    drel = relbias_grad(dt, _rel_index())[:N_HEADS, :2 * MAX_REL + 1]
    duv, dws, dbs, dvgain = mix_b_bwd(sv["uv"], c["v_gain"], p["w_s"], c["b_col"], dyb)
    ride = [dwout.astype(BF16).reshape(4, D_BRANCH, D_MODEL)] if exchange else []
    if upper_w_in is not None:
        ride.append(upper_w_in)
    dqc, dkc, dvc, dc, *rode = fox_bwd(sv["qkv"], sv["c_row"], dyc, sv["yc"], sv["ref_c"], sv["rl_c"], ride)
    dft, dbf = fox_gate_bwd(sv["ft"], sv["bcol"], dc.reshape(N_HEADS * (s // 128), 128))
    dfp = jnp.pad(dft.reshape(N_HEADS, s).T, ((0, 0), (0, 128 - N_HEADS)))
    grads = dict(b_f=dbf[:N_HEADS, 0], rel_bias=drel, w_s=dws, b_s=dbs.reshape(N_HEADS, SG_CHUNK),
                 v_gain=dvgain.reshape(D_BRANCH), branch_gain=dbg.reshape(4, D_BRANCH), wout=dwout)
    dqd, dkd, dvd, *small_parts = sb_bwd(sv["qkv"], dyd, sv["yd"], sv["band_d"], small_ride(grads) if small_ride else ())
    dp, dx, dnorm = inproj_bwd((dqa, dka, dva, dqc, dkc, dvc, dqd, dkd, dvd), dgates, duv, dfp,
                               p["wp"], sv["x"], c["norm_g"], dout)
    grads["norm_g"] = dnorm.reshape(D_MODEL)
    if small_ride:
        top, = inproj_wgrad(sv["h"], dp, 0)
        grads["w_in_shards"], grads["w_in_top_parts"] = inproj_wgrad(sv["h"], dp, 1, [top])
        grads["small_parts"] = small_parts[0]
    else:
        grads["w_in_shards"], = inproj_wgrad(sv["h"], dp)
    if exchange:
        grads["w_out_parts"] = rode[0]
    return dx, grads, (rode[1] if upper_w_in is not None else None)


def local_step(x, tgt, layers, final_g, next_shards=None):
    layers = list(layers)
    saved = []
    cur = x
    for l, p in enumerate(layers):
        ride = next_shards[l] if next_shards is not None and l + 1 < len(layers) else ()
        cur, sv, rode = layer_fwd(cur, p, ride)
        saved.append(sv)
        if ride:
            layers[l + 1] = dict(layers[l + 1], wp=pack_w_in(rode[0][None])[0], wout=rode[1].reshape(D_MODEL, D_MODEL))
    loss, dcur, dfinal = final_loss(cur, tgt, final_g.reshape(1, D_MODEL))
    grads = [None] * len(layers)
    for l in reversed(range(len(layers))):
        exchange = next_shards is not None
        upper = grads[l + 1]["w_in_shards"] if exchange and l + 1 < len(layers) else None
        small_ride = None
        if exchange and l == 0:
            def small_ride(g0, above=tuple(grads[1:])):
                stacked = {k: jnp.stack([g[k] for g in (g0,) + above]) for k in SMALL_EARLY if k != "final_g"}
                return [_pack([stacked.get(k, dfinal.reshape(D_MODEL)) for k in SMALL_EARLY])]
        dcur, grads[l], got = layer_bwd(dcur, layers[l], saved[l], exchange, upper, small_ride)
        if upper is not None:
            grads[l + 1]["w_in_parts"] = got
    return loss[0, 0], dcur, grads, dfinal.reshape(D_MODEL)


def _chip_gather(pairs, send_sems, recv_sems, loc_sems):
    x, y, c = lax.axis_index("x"), lax.axis_index("y"), lax.axis_index("c")
    me = 2 * x + y
    chips = [(1 - x, y), (x, 1 - y), (1 - x, 1 - y)]
    npair = len(pairs)

    def local():
        return [pltpu.make_async_copy(src, dst(me), loc_sems.at[n]) for n, (src, dst) in enumerate(pairs)]

    def remote(j, n, slot):
        src, dst = pairs[n]
        return pltpu.make_async_remote_copy(
            src_ref=src, dst_ref=dst(slot), send_sem=send_sems.at[npair * j + n], recv_sem=recv_sems.at[npair * j + n],
            device_id=(chips[j][0], chips[j][1], c), device_id_type=MESH)

    def start():
        for cp in local():
            cp.start()
        for j in range(3):
            for n in range(npair):
                remote(j, n, me).start()

    def wait():
        for j in range(3):
            for n in range(npair):
                remote(j, n, 2 * chips[j][0] + chips[j][1]).wait_recv()
        for j in range(3):
            for n in range(npair):
                remote(j, n, me).wait_send()
        for cp in local():
            cp.wait()

    return start, wait


def gather_weights(w_in, w_out, gains):
    depth = w_in.shape[0]

    def body(in_ref, out_ref, g_ref, oin_ref, oout_ref, og_ref, send_sems, recv_sems, loc_sems):
        pairs = [(in_ref, lambda s: oin_ref.at[:, s]), (out_ref, lambda s: oout_ref.at[:, s]), (g_ref, lambda s: og_ref.at[s])]
        start, wait = _chip_gather(pairs, send_sems, recv_sems, loc_sems)
        start()
        wait()

    any_spec = pl.BlockSpec(memory_space=pl.ANY)
    return pl.pallas_call(
        body, name="gather_weights",
        in_specs=[any_spec] * 3, out_specs=[any_spec] * 3,
        out_shape=[jax.ShapeDtypeStruct((depth, 4) + w_in.shape[1:], w_in.dtype),
                   jax.ShapeDtypeStruct((depth, 4) + w_out.shape[1:], w_out.dtype),
                   jax.ShapeDtypeStruct((4,) + gains.shape, gains.dtype)],
        scratch_shapes=[pltpu.SemaphoreType.DMA((9,)), pltpu.SemaphoreType.DMA((9,)), pltpu.SemaphoreType.DMA((3,))],
    )(w_in, w_out, gains)


def pack_w_in(shards):
    depth = shards.shape[0]
    tr = 256

    def body(s_ref, o_ref):
        full = jnp.concatenate([s_ref[n] for n in range(4)], axis=1)
        o_ref[...] = jnp.concatenate([full[:, :SEC_D_Q], full[:, SEC_D_Q + N_HEADS:], full[:, SEC_D_Q:SEC_D_Q + N_HEADS],
                                      jnp.zeros((tr, N_PACK - N_IN), BF16)], axis=1)

    return pl.pallas_call(
        body, name="pack_w_in", grid=(depth, D_MODEL // tr),
        in_specs=[pl.BlockSpec((None, 4, tr, N_SHARD), lambda l, r: (l, 0, r, 0))],
        out_specs=pl.BlockSpec((None, tr, N_PACK), lambda l, r: (l, r, 0)),
        out_shape=jax.ShapeDtypeStruct((depth, D_MODEL, N_PACK), BF16),
        compiler_params=_params(("arbitrary", "arbitrary")),
    )(shards)


def _device_exchange(flows, send_sems, recv_sems, loc_sems):
    x, y, c = lax.axis_index("x"), lax.axis_index("y"), lax.axis_index("c")
    me_chip = 2 * x + y
    me = 4 * x + 2 * y + c
    peers = [(x, y, 1 - c)]
    for px, py in [(1 - x, y), (x, 1 - y), (1 - x, 1 - y)]:
        peers += [(px, py, c), (px, py, 1 - c)]
    nflow = len(flows)

    def local():
        return [pltpu.make_async_copy(src(me_chip), dst(me), loc_sems.at[f]) for f, (src, dst) in enumerate(flows)]

    def copies(n, chip, slot):
        return [pltpu.make_async_remote_copy(src_ref=src(chip), dst_ref=dst(slot), send_sem=send_sems.at[nflow * n + f],
                                             recv_sem=recv_sems.at[nflow * n + f], device_id=peers[n], device_id_type=MESH)
                for f, (src, dst) in enumerate(flows)]

    def start():
        for cp in local():
            cp.start()
        for n, (px, py, _) in enumerate(peers):
            for cp in copies(n, 2 * px + py, me):
                cp.start()

    def wait():
        for n, (px, py, pc) in enumerate(peers):
            for cp in copies(n, me_chip, 4 * px + 2 * py + pc):
                cp.wait_recv()
        for n, (px, py, _) in enumerate(peers):
            for cp in copies(n, 2 * px + py, me):
                cp.wait_send()
        for cp in local():
            cp.wait()

    return start, wait


def _exchange_flows(srcs, dsts):
    return [((lambda s, src=src: src.at[s]) if src.shape[0] == 4 else (lambda s, src=src: src),
             lambda d, dst=dst: dst.at[d]) for src, dst in zip(srcs, dsts)]


def _exchange_shapes(arrays):
    return [jax.ShapeDtypeStruct((8,) + (a.shape[1:] if a.shape[0] == 4 else a.shape), a.dtype) for a in arrays]


def _exchange_sems(n):
    return [pltpu.SemaphoreType.DMA((7 * n,)), pltpu.SemaphoreType.DMA((7 * n,)), pltpu.SemaphoreType.DMA((n,))]


def exchange_grads(*arrays):
    n = len(arrays)

    def body(*refs):
        start, wait = _device_exchange(_exchange_flows(refs[:n], refs[n:2 * n]), *refs[2 * n:])
        start()
        wait()

    any_spec = pl.BlockSpec(memory_space=pl.ANY)
    return pl.pallas_call(
        body, name="exchange_grads",
        in_specs=[any_spec] * n, out_specs=[any_spec] * n, out_shape=_exchange_shapes(arrays),
        scratch_shapes=_exchange_sems(n),
    )(*arrays)


def adamw_reduce(parts, w, m, v, name, tr):
    rows, width = w.shape
    steps = [p.shape[1] // tr for p in parts]
    offs = [sum(steps[:n]) for n in range(len(parts))]
    c1 = 1.0 - ADAM_B1 ** ADAM_STEP
    c2 = 1.0 - ADAM_B2 ** ADAM_STEP

    def body(*refs):
        p_refs = refs[:len(parts)]
        w_ref, m_ref, v_ref, g_ref, d_ref, nm_ref, nv_ref = refs[len(parts):]
        i = pl.program_id(0)
        p = p_refs[0][...]
        for n in range(1, len(parts)):
            p = jnp.where(i >= offs[n], p_refs[n][...], p)
        g = p[0].astype(F32)
        for n in range(1, 8):
            g = g + p[n].astype(F32)
        g_ref[...] = g
        nm = ADAM_B1 * m_ref[...] + (1.0 - ADAM_B1) * g
        nv = ADAM_B2 * v_ref[...] + (1.0 - ADAM_B2) * (g * g)
        nm_ref[...] = nm
        nv_ref[...] = nv
        d_ref[...] = -ADAM_LR * ((nm / c1) / (jnp.sqrt(nv / c2) + ADAM_EPS) + ADAM_WD * w_ref[...])

    spec = pl.BlockSpec((tr, width), lambda i: (i, 0))
    shape = jax.ShapeDtypeStruct((rows, width), F32)
    return pl.pallas_call(
        body, name=name, grid=(rows // tr,),
        in_specs=[pl.BlockSpec((8, tr, width), lambda i, n=n: (0, jnp.clip(i - offs[n], 0, steps[n] - 1), 0))
                  for n in range(len(parts))] + [spec, spec, spec],
        out_specs=[spec] * 4, out_shape=[shape] * 4,
        compiler_params=_params(("arbitrary",)),
    )(*parts, w, m, v)


SMALL_EARLY = ("b_f", "rel_bias", "w_s", "b_s", "v_gain", "final_g")
WEIGHTS = ("norm_g", "w_in", "b_f", "rel_bias", "w_s", "b_s", "v_gain", "branch_gain", "w_out", "final_g")
PACK_ROW_TILE = 512


def _rows_of(shape):
    return -(-int(np.prod(shape)) // 128)


def _pack(leaves, tile=PACK_ROW_TILE):
    parts = []
    for a in leaves:
        flat = a.reshape(-1).astype(F32)
        parts.append(jnp.pad(flat, (0, _rows_of(a.shape) * 128 - flat.shape[0])))
    flat = jnp.concatenate(parts)
    rows = flat.shape[0] // 128
    total = -(-rows // tile) * tile
    return jnp.pad(flat, (0, (total - rows) * 128)).reshape(total, 128)


def _unpack(slab, shapes):
    out, row = [], 0
    for shp in shapes:
        n = int(np.prod(shp))
        r = _rows_of(shp)
        out.append(slab[row:row + r].reshape(-1)[:n].reshape(shp))
        row += r
    return out


def kernel(x, norm_g, w_in, b_f, rel_bias, w_s, b_s, v_gain, branch_gain, w_out, final_g, loss_target, m_norm_g, m_w_in, m_b_f, m_rel_bias, m_w_s, m_b_s, m_v_gain, m_branch_gain, m_w_out, m_final_g, v_norm_g, v_w_in, v_b_f, v_rel_bias, v_w_s, v_b_s, v_v_gain, v_branch_gain, v_w_out, v_final_g):
    depth = norm_g.shape[0]
    weights = dict(norm_g=norm_g, w_in=w_in, b_f=b_f, rel_bias=rel_bias, w_s=w_s, b_s=b_s, v_gain=v_gain,
                   branch_gain=branch_gain, w_out=w_out, final_g=final_g)
    mom1 = dict(norm_g=m_norm_g, w_in=m_w_in, b_f=m_b_f, rel_bias=m_rel_bias, w_s=m_w_s, b_s=m_b_s,
                v_gain=m_v_gain, branch_gain=m_branch_gain, w_out=m_w_out, final_g=m_final_g)
    mom2 = dict(norm_g=v_norm_g, w_in=v_w_in, b_f=v_b_f, rel_bias=v_rel_bias, w_s=v_w_s, b_s=v_b_s,
                v_gain=v_v_gain, branch_gain=v_branch_gain, w_out=v_w_out, final_g=v_final_g)

    wf = jnp.pad(branch_gain.reshape(-1), (0, 8 * 128 - branch_gain.size)).reshape(8, 128)
    w_in_b, w_out_b = w_in.astype(BF16), w_out.astype(BF16)
    w_in_shards, w_out_shards, gf = gather_weights(w_in_b[:1], w_out_b[:1], wf)
    bg_full = gf.reshape(4, -1)[:, :branch_gain.size].reshape((4,) + branch_gain.shape)
    bg_full = jnp.moveaxis(bg_full, 0, 2).reshape(depth, 4, D_BRANCH)

    layers = [dict(norm_g=norm_g[l], b_f=b_f[l], rel_bias=rel_bias[l], w_s=w_s[l],
                   b_s=b_s[l], v_gain=v_gain[l], branch_gain=bg_full[l]) for l in range(depth)]
    layers[0].update(wp=pack_w_in(w_in_shards)[0], wout=w_out_shards.reshape(D_MODEL, D_MODEL))
    next_shards = [(w_in_b[l + 1], w_out_b[l + 1]) for l in range(depth - 1)]

    loss_part, grad_x, lgrads, dfinal = local_step(x[0], loss_target[0], layers, final_g, next_shards)
    loss = lax.psum(loss_part, ("x", "y", "c"))

    stack = lambda k: jnp.stack([g[k] for g in lgrads])
    d_gain = jnp.moveaxis(stack("branch_gain").reshape(depth, 4, 4, HEAD_DIM), 2, 0).reshape(4, -1)
    d_gain = jnp.pad(d_gain, ((0, 0), (0, 8 * 128 - d_gain.shape[1]))).reshape(4, 8, 128)
    parts_in, parts_gain, parts_norm = exchange_grads(lgrads[0]["w_in_shards"], d_gain, _pack([stack("norm_g")], 16))
    parts = dict(w_in=[lgrads[0]["w_in_top_parts"], parts_in] + [g["w_in_parts"] for g in lgrads[1:]],
                 w_out=[g["w_out_parts"] for g in lgrads])

    outs = {}
    tags = ("grad", "delta", "new_m", "new_v")
    for k in ("w_in", "w_out"):
        rows = depth * weights[k].shape[1]
        flat = lambda a: a.reshape(rows, a.shape[-1])
        res = adamw_reduce(parts[k], flat(weights[k]), flat(mom1[k]), flat(mom2[k]), "adamw_" + k, 256)
        for tag, a in zip(tags, res):
            outs[tag, k] = a.reshape(weights[k].shape)
    gain8 = lambda a: jnp.pad(a.reshape(-1), (0, 8 * 128 - a.size)).reshape(8, 128)
    res = adamw_reduce([parts_gain], gain8(branch_gain), gain8(m_branch_gain), gain8(v_branch_gain), "adamw_gain", 8)
    for tag, a in zip(tags, res):
        outs[tag, "branch_gain"] = a.reshape(-1)[:branch_gain.size].reshape(branch_gain.shape)
    for names, parts_small, tile in ((SMALL_EARLY, lgrads[0]["small_parts"], PACK_ROW_TILE), (("norm_g",), parts_norm, 16)):
        pack_small = lambda d: _pack([d[k] for k in names], tile)
        res = adamw_reduce([parts_small], pack_small(weights), pack_small(mom1), pack_small(mom2),
                           "adamw_" + names[0], tile)
        for tag, slab in zip(tags, res):
            for k, a in zip(names, _unpack(slab, [weights[k].shape for k in names])):
                outs[tag, k] = a
    result = [loss, grad_x[None]]
    for tag in ("grad", "delta", "new_m", "new_v"):
        result += [outs[tag, k] for k in WEIGHTS]
    return tuple(result)
```

```python
import jax
import jax.numpy as jnp
import numpy as np
from jax import lax
from jax.experimental import pallas as pl
from jax.experimental.pallas import tpu as pltpu

F32 = jnp.float32
BF16 = jnp.bfloat16
MESH = pl.DeviceIdType.MESH

D_MODEL = 1024
D_BRANCH = 256
N_HEADS = 4
HEAD_DIM = 64
CHUNK = 64
LOOKBACK = 8
MAX_REL = 128
SG_CHUNK = 128
EPS = 1e-6
N_IN = 3844
N_PACK = 3968
F_COL = 3840
N_SHARD = 961
NEG = -1e30

A_TQ = 128
A_BAND = A_TQ + LOOKBACK * CHUNK
REL_LO = MAX_REL - (CHUNK - 1)
REL_HI = 2 * MAX_REL + 1
A_PAD = LOOKBACK * CHUNK
A_QB = 2048
ATT_T = 256
FOX_TQ = 512
FOX_WIDE = 4
FOX_DEAD2 = -136.0
LOG2E = 1.4426950408889634
SB_TQ = 1024
SB_SUB = 128
SB_BACK = 256
SB_BAND = SB_SUB + SB_BACK
SB_DEAD = -110.0
ROW_T = 512
VMEM_LIMIT = 56 * 1024 * 1024

ADAM_LR = 0.001
ADAM_B1 = 0.9
ADAM_B2 = 0.999
ADAM_EPS = 1e-08
ADAM_WD = 0.01
ADAM_STEP = 10

SEC_A_Q, SEC_A_K, SEC_A_V, SEC_A_G = 0, 256, 512, 768
SEC_B_U, SEC_B_V, SEC_B_G = 1024, 1280, 1536
SEC_C_Q, SEC_C_K, SEC_C_V, SEC_C_G = 1792, 2048, 2304, 2560
SEC_D_Q, SEC_D_K, SEC_D_V, SEC_D_G = 2816, 3072, 3328, 3584
QKV_SECS = (SEC_A_Q, SEC_C_Q, SEC_C_K, SEC_C_V, SEC_D_Q, SEC_D_K, SEC_D_V)
GATE_SECS = (SEC_A_G, SEC_B_G, SEC_C_G, SEC_D_G)


def _dot(a, b):
    return jnp.dot(a, b, preferred_element_type=F32)


def _dot_nt(a, b):
    return lax.dot_general(a, b, (((1,), (1,)), ((), ())), preferred_element_type=F32)


def _dot_tn(a, b):
    return lax.dot_general(a, b, (((0,), (0,)), ((), ())), preferred_element_type=F32)


def _split2(x):
    hi = x.astype(BF16)
    lo = (x - hi.astype(F32)).astype(BF16)
    return hi, lo


def _split3(x):
    hi = x.astype(BF16)
    r = x - hi.astype(F32)
    mid = r.astype(BF16)
    lo = (r - mid.astype(F32)).astype(BF16)
    return hi, mid, lo


def _sigmoid(x):
    return 1.0 / (1.0 + jnp.exp(-x))


def _params(sem=None, vmem=VMEM_LIMIT):
    return pltpu.CompilerParams(dimension_semantics=sem, vmem_limit_bytes=vmem)


def _heads_to_lanes(ref):
    return jnp.concatenate([ref[h] for h in range(N_HEADS)], axis=1)


def inproj_fwd(x, g, wp):
    s = x.shape[0]
    tm = A_PAD

    def body(x_ref, g_ref, w_ref, h_ref, qkv_ref, kva_ref, gates_ref, uv_ref, f_ref):
        xv = x_ref[...]
        r = lax.rsqrt(jnp.mean(xv * xv, axis=-1, keepdims=True) + EPS)
        h = (xv * r * g_ref[...]).astype(BF16)
        h_ref[...] = h
        for n, off in enumerate(QKV_SECS):
            p = _dot(h, w_ref[:, off:off + D_BRANCH])
            for hh in range(N_HEADS):
                qkv_ref[n, hh] = p[:, hh * HEAD_DIM:(hh + 1) * HEAD_DIM].astype(BF16)
        for n, off in enumerate((SEC_A_K, SEC_A_V)):
            p = _dot(h, w_ref[:, off:off + D_BRANCH])
            for hh in range(N_HEADS):
                kva_ref[n, hh] = p[:, hh * HEAD_DIM:(hh + 1) * HEAD_DIM].astype(BF16)
        for n, off in enumerate(GATE_SECS):
            gates_ref[:, n * D_BRANCH:(n + 1) * D_BRANCH] = _dot(h, w_ref[:, off:off + D_BRANCH])
        uv_ref[...] = _dot(h, w_ref[:, SEC_B_U:SEC_B_U + 2 * D_BRANCH])
        f_ref[...] = _dot(h, w_ref[:, F_COL:F_COL + 128])

    return pl.pallas_call(
        body, name="inproj_fwd", grid=(s // tm,),
        in_specs=[pl.BlockSpec((tm, D_MODEL), lambda i: (i, 0)),
                  pl.BlockSpec((1, D_MODEL), lambda i: (0, 0)),
                  pl.BlockSpec((D_MODEL, N_PACK), lambda i: (0, 0))],
        out_specs=[pl.BlockSpec((tm, D_MODEL), lambda i: (i, 0)),
                   pl.BlockSpec((len(QKV_SECS), N_HEADS, tm, HEAD_DIM), lambda i: (0, 0, i, 0)),
                   pl.BlockSpec((2, N_HEADS, tm, HEAD_DIM), lambda i: (0, 0, i + 1, 0)),
                   pl.BlockSpec((tm, D_MODEL), lambda i: (i, 0)),
                   pl.BlockSpec((tm, 2 * D_BRANCH), lambda i: (i, 0)),
                   pl.BlockSpec((tm, 128), lambda i: (i, 0))],
        out_shape=[jax.ShapeDtypeStruct((s, D_MODEL), BF16),
                   jax.ShapeDtypeStruct((len(QKV_SECS), N_HEADS, s, HEAD_DIM), BF16),
                   jax.ShapeDtypeStruct((2, N_HEADS, s + tm, HEAD_DIM), BF16),
                   jax.ShapeDtypeStruct((s, D_MODEL), F32),
                   jax.ShapeDtypeStruct((s, 2 * D_BRANCH), F32),
                   jax.ShapeDtypeStruct((s, 128), F32)],
        compiler_params=_params(("arbitrary",)),
    )(x, g, wp)


def inproj_bwd(dqkv, dgates, duv, dfp, wp, x, g, dres):
    s = x.shape[0]
    tm = A_PAD

    def body(*refs):
        dq_refs = refs[:9]
        dgates_ref, duv_ref, dfp_ref, w_ref, x_ref, g_ref, dres_ref, dp_ref, dx_ref, dg_ref = refs[9:]
        i = pl.program_id(0)
        a_q, a_k, a_v, c_q, c_k, c_v, d_q, d_k, d_v = [_heads_to_lanes(r).astype(BF16) for r in dq_refs]
        dgt = dgates_ref[...]
        duv_b = duv_ref[...].astype(BF16)
        dp = jnp.concatenate(
            [a_q, a_k, a_v, dgt[:, 0:256], duv_b, dgt[:, 256:512], c_q, c_k, c_v, dgt[:, 512:768],
             d_q, d_k, d_v, dgt[:, 768:1024], dfp_ref[...].astype(BF16)], axis=1)
        dp_ref[...] = dp
        dh = _dot_nt(dp, w_ref[...])
        xv = x_ref[...]
        r = lax.rsqrt(jnp.mean(xv * xv, axis=-1, keepdims=True) + EPS)
        xn = xv * r
        u = dh * g_ref[...]
        dx_ref[...] = dres_ref[...] + r * (u - xn * jnp.mean(xn * u, axis=-1, keepdims=True))

        @pl.when(i == 0)
        def _():
            dg_ref[...] = jnp.zeros_like(dg_ref)

        dg_ref[...] += jnp.sum(dh * xn, axis=0, keepdims=True)

    head_spec = pl.BlockSpec((N_HEADS, tm, HEAD_DIM), lambda i: (0, i, 0))
    padded_spec = pl.BlockSpec((N_HEADS, tm, HEAD_DIM), lambda i: (0, i + 1, 0))
    return pl.pallas_call(
        body, name="inproj_bwd", grid=(s // tm,),
        in_specs=[head_spec, padded_spec, padded_spec] + [head_spec] * 6 + [
            pl.BlockSpec((tm, D_MODEL), lambda i: (i, 0)),
            pl.BlockSpec((tm, 2 * D_BRANCH), lambda i: (i, 0)),
            pl.BlockSpec((tm, 128), lambda i: (i, 0)),
            pl.BlockSpec((D_MODEL, N_PACK), lambda i: (0, 0)),
            pl.BlockSpec((tm, D_MODEL), lambda i: (i, 0)),
            pl.BlockSpec((1, D_MODEL), lambda i: (0, 0)),
            pl.BlockSpec((tm, D_MODEL), lambda i: (i, 0))],
        out_specs=[pl.BlockSpec((tm, N_PACK), lambda i: (i, 0)),
                   pl.BlockSpec((tm, D_MODEL), lambda i: (i, 0)),
                   pl.BlockSpec((1, D_MODEL), lambda i: (0, 0))],
        out_shape=[jax.ShapeDtypeStruct((s, N_PACK), BF16),
                   jax.ShapeDtypeStruct((s, D_MODEL), F32),
                   jax.ShapeDtypeStruct((1, D_MODEL), F32)],
        compiler_params=_params(("arbitrary",)),
    )(*dqkv, dgates, duv, dfp, wp, x, g, dres)


def inproj_wgrad(h, dp, half=None, ride=()):
    s, m = h.shape
    tm = min(4 * ROW_T, s)
    tmm = 256
    nsteps = s // tm
    ntile = m // tmm if half is None else m // tmm // 2
    first = 0 if half is None else half * ntile
    nride = len(ride)

    def body(a_ref, b_ref, *refs):
        ride_in, o_ref = refs[:nride], refs[nride]
        ride_out, acc_ref = refs[nride + 1:2 * nride + 1], refs[2 * nride + 1]
        k = pl.program_id(1)
        if nride:
            j = pl.program_id(0)
            start, wait = _device_exchange(_exchange_flows(ride_in, ride_out), *refs[2 * nride + 2:])
            pl.when(jnp.logical_and(j == 0, k == 0))(start)

        @pl.when(k == 0)
        def _():
            acc_ref[...] = jnp.zeros_like(acc_ref)

        acc_ref[...] += _dot_tn(a_ref[...], b_ref[...])

        @pl.when(k == nsteps - 1)
        def _():
            acc = acc_ref[...]
            full = jnp.concatenate([acc[:, :SEC_D_Q], acc[:, F_COL:F_COL + N_HEADS], acc[:, SEC_D_Q:F_COL]], axis=1)
            for n in range(4):
                o_ref[n] = full[:, n * N_SHARD:(n + 1) * N_SHARD].astype(BF16)

        if nride:
            pl.when(jnp.logical_and(j == ntile - 1, k == nsteps - 1))(wait)

    any_spec = pl.BlockSpec(memory_space=pl.ANY)
    return pl.pallas_call(
        body, name="inproj_wgrad_exchange" if nride else "inproj_wgrad", grid=(ntile, nsteps),
        in_specs=[pl.BlockSpec((tm, tmm), lambda j, k: (k, first + j)),
                  pl.BlockSpec((tm, N_PACK), lambda j, k: (k, 0))] + [any_spec] * nride,
        out_specs=[pl.BlockSpec((4, tmm, N_SHARD), lambda j, k: (0, j, 0))] + [any_spec] * nride,
        out_shape=[jax.ShapeDtypeStruct((4, ntile * tmm, N_SHARD), BF16)] + _exchange_shapes(ride),
        scratch_shapes=[pltpu.VMEM((tmm, N_PACK), F32)] + (_exchange_sems(nride) if nride else []),
        compiler_params=_params(("arbitrary", "arbitrary")),
    )(h, dp, *ride)


def _a_specs(s):
    nq = s // A_QB
    per = A_QB // A_PAD
    q_spec = pl.BlockSpec((None, None, A_QB, HEAD_DIM), lambda h, i: (0, h, jnp.minimum(i, nq - 1), 0))
    kv_specs = [pl.BlockSpec((None, None, A_PAD, HEAD_DIM),
                             lambda h, i, n=n, m=m: (n, h, jnp.minimum(per * i + m, per * nq), 0))
                for n in range(2) for m in range(per + 1)]
    t_spec = pl.BlockSpec((None, A_TQ, A_BAND), lambda h, i: (h, 0, 0))
    return nq, q_spec, kv_specs, t_spec


def _a_window(refs, i):
    first = refs[0][...]
    return jnp.concatenate([jnp.where(i > 0, first, jnp.zeros_like(first))] + [r[...] for r in refs[1:]], axis=0)


def _a_scores(q_ref, k, t_ref, i, j):
    rows = slice(j * A_TQ, (j + 1) * A_TQ)
    qs = q_ref[rows, :] * 0.125
    kj = k[j * A_TQ:j * A_TQ + A_BAND, :]
    sc = _dot_nt(qs, kj) + t_ref[...]
    col = lax.broadcasted_iota(jnp.int32, (A_TQ, A_BAND), 1)
    sc = jnp.where(col >= A_PAD - i * A_QB - j * A_TQ, sc, NEG)
    return rows, qs, kj, sc


def mix_a_fwd(qkv, kva, tbias):
    s = qkv.shape[2]
    nq, q_spec, kv_specs, t_spec = _a_specs(s)
    nwin = len(kv_specs) // 2

    def body(*refs):
        q_ref, t_ref, o_ref, lse_ref = refs[0], refs[1 + 2 * nwin], refs[2 + 2 * nwin], refs[3 + 2 * nwin]
        i = pl.program_id(1)
        k = _a_window(refs[1:1 + nwin], i)
        v = _a_window(refs[1 + nwin:1 + 2 * nwin], i)
        for j in range(A_QB // A_TQ):
            rows, _, _, sc = _a_scores(q_ref, k, t_ref, i, j)
            m = jnp.max(sc, axis=-1, keepdims=True)
            p = jnp.exp(sc - m)
            l = jnp.sum(p, axis=-1, keepdims=True)
            o_ref[rows, :] = _dot(p.astype(BF16), v[j * A_TQ:j * A_TQ + A_BAND, :]) / l
            lse_ref[rows, :] = m + jnp.log(l)

    return pl.pallas_call(
        body, name="mix_a_fwd", grid=(N_HEADS, nq),
        in_specs=[q_spec] + kv_specs + [t_spec],
        out_specs=[pl.BlockSpec((None, A_QB, HEAD_DIM), lambda h, i: (h, i, 0)),
                   pl.BlockSpec((None, A_QB, 1), lambda h, i: (h, i, 0))],
        out_shape=[jax.ShapeDtypeStruct((N_HEADS, s, HEAD_DIM), F32),
                   jax.ShapeDtypeStruct((N_HEADS, s, 1), F32)],
        compiler_params=_params(("arbitrary", "arbitrary")),
    )(qkv, *([kva] * (2 * nwin)), tbias)


def mix_a_bwd(qkv, kva, tbias, do, o, lse):
    s = qkv.shape[2]
    nq, q_spec, kv_specs, t_spec = _a_specs(s)
    nwin = len(kv_specs) // 2
    row_spec = lambda w: pl.BlockSpec((None, A_QB, w), lambda h, i: (h, jnp.minimum(i, nq - 1), 0))
    done_spec = pl.BlockSpec((None, A_QB, HEAD_DIM), lambda h, i: (h, i, 0))
    win = A_QB + A_PAD

    def body(*refs):
        q_ref = refs[0]
        t_ref, do_ref, o_ref, lse_ref, dq_ref, dk_ref, dv_ref, dt_ref, dk_win, dv_win = refs[1 + 2 * nwin:]
        i = pl.program_id(1)

        @pl.when(i == 0)
        def _():
            dk_win[...] = jnp.zeros_like(dk_win)
            dv_win[...] = jnp.zeros_like(dv_win)
            dt_ref[...] = jnp.zeros_like(dt_ref)

        @pl.when(i < nq)
        def _():
            k = _a_window(refs[1:1 + nwin], i)
            v = _a_window(refs[1 + nwin:1 + 2 * nwin], i)
            dt = jnp.zeros((A_TQ, A_BAND), F32)
            for j in range(A_QB // A_TQ):
                rows, qs, kj, sc = _a_scores(q_ref, k, t_ref, i, j)
                keys = slice(j * A_TQ, j * A_TQ + A_BAND)
                dob = do_ref[rows, :]
                p = jnp.exp(sc - lse_ref[rows, :])
                delta = jnp.sum(o_ref[rows, :] * dob.astype(F32), axis=-1, keepdims=True)
                ds = p * (_dot_nt(dob, v[keys, :]) - delta)
                dsb = ds.astype(BF16)
                dq_ref[rows, :] = _dot(dsb, kj) * 0.125
                dk_win[keys, :] += _dot_tn(dsb, qs)
                dv_win[keys, :] += _dot_tn(p.astype(BF16), dob)
                dt = dt + ds
            dt_ref[...] += dt

        dk_ref[...] = dk_win[0:A_QB, :]
        dv_ref[...] = dv_win[0:A_QB, :]
        dk_rest = dk_win[A_QB:win, :]
        dv_rest = dv_win[A_QB:win, :]
        dk_win[0:A_PAD, :] = dk_rest
        dv_win[0:A_PAD, :] = dv_rest
        dk_win[A_PAD:win, :] = jnp.zeros((A_QB, HEAD_DIM), F32)
        dv_win[A_PAD:win, :] = jnp.zeros((A_QB, HEAD_DIM), F32)

    return pl.pallas_call(
        body, name="mix_a_bwd", grid=(N_HEADS, nq + 1),
        in_specs=[q_spec] + kv_specs + [t_spec, row_spec(HEAD_DIM), row_spec(HEAD_DIM), row_spec(1)],
        out_specs=[row_spec(HEAD_DIM), done_spec, done_spec, t_spec],
        out_shape=[jax.ShapeDtypeStruct((N_HEADS, s, HEAD_DIM), F32),
                   jax.ShapeDtypeStruct((N_HEADS, s + A_QB, HEAD_DIM), F32),
                   jax.ShapeDtypeStruct((N_HEADS, s + A_QB, HEAD_DIM), F32),
                   jax.ShapeDtypeStruct((N_HEADS, A_TQ, A_BAND), F32)],
        scratch_shapes=[pltpu.VMEM((win, HEAD_DIM), F32), pltpu.VMEM((win, HEAD_DIM), F32)],
        compiler_params=_params(("arbitrary", "arbitrary")),
    )(qkv, *([kva] * (2 * nwin)), tbias, do, o, lse)


def relbias_tile(rel_bias, relmat):
    def body(rb_ref, rel_ref, o_ref):
        rel = rel_ref[...]
        o_ref[...] = jnp.full(o_ref.shape, NEG, F32)

        def step(r, carry):
            hit = rel == r
            for h in range(N_HEADS):
                o_ref[h] = jnp.where(hit, rb_ref[h, r], o_ref[h])
            return carry

        lax.fori_loop(REL_LO, REL_HI, step, 0)

    return pl.pallas_call(
        body, name="relbias_tile",
        in_specs=[pl.BlockSpec(memory_space=pltpu.SMEM), pl.BlockSpec(memory_space=pltpu.VMEM)],
        out_specs=pl.BlockSpec(memory_space=pltpu.VMEM),
        out_shape=jax.ShapeDtypeStruct((N_HEADS, A_TQ, A_BAND), F32),
        compiler_params=_params(),
    )(rel_bias, relmat)


def relbias_grad(dt, relmat):
    def body(dt_ref, rel_ref, o_ref):
        rel = rel_ref[...]
        lane = lax.broadcasted_iota(jnp.int32, (8, 384), 1)
        row = lax.broadcasted_iota(jnp.int32, (8, 384), 0)

        def step(r, acc):
            hit = rel == r
            for h in range(N_HEADS):
                val = jnp.sum(jnp.where(hit, dt_ref[h], 0.0))
                acc = jnp.where((lane == r) & (row == h), val, acc)
            return acc

        o_ref[...] = lax.fori_loop(REL_LO, REL_HI, step, jnp.zeros((8, 384), F32))

    return pl.pallas_call(
        body, name="relbias_grad",
        out_shape=jax.ShapeDtypeStruct((8, 384), F32),
        compiler_params=_params(),
    )(dt, relmat)


def _b_norm(v, gain):
    mu = jnp.mean(v, axis=-1, keepdims=True)
    xc = v - mu
    rstd = lax.rsqrt(jnp.mean(xc * xc, axis=-1, keepdims=True) + EPS)
    xhat = xc * rstd
    return xhat, rstd, xhat * gain


def _tril_mask():
    t = lax.broadcasted_iota(jnp.int32, (SG_CHUNK, SG_CHUNK), 0)
    u = lax.broadcasted_iota(jnp.int32, (SG_CHUNK, SG_CHUNK), 1)
    return u <= t


def mix_b_fwd(uv, gain, w_s, b_col):
    s = uv.shape[0]
    tm = min(ROW_T, s)

    def body(uv_ref, gain_ref, w_ref, b_ref, y_ref):
        tril = _tril_mask()
        ws = [jnp.where(tril, w_ref[g], 0.0).astype(BF16) for g in range(N_HEADS)]
        for c in range(tm // SG_CHUNK):
            rows = slice(c * SG_CHUNK, (c + 1) * SG_CHUNK)
            u = uv_ref[rows, 0:D_BRANCH]
            _, _, vn = _b_norm(uv_ref[rows, D_BRANCH:2 * D_BRANCH], gain_ref[...])
            vnb = vn.astype(BF16)
            outs = []
            for g in range(N_HEADS):
                cols = slice(g * HEAD_DIM, (g + 1) * HEAD_DIM)
                mixed = _dot(ws[g], vnb[:, cols]) + b_ref[g]
                outs.append(u[:, cols] * mixed)
            y_ref[rows, :] = jnp.concatenate(outs, axis=1)

    return pl.pallas_call(
        body, name="mix_b_fwd", grid=(s // tm,),
        in_specs=[pl.BlockSpec((tm, 2 * D_BRANCH), lambda i: (i, 0)),
                  pl.BlockSpec((1, D_BRANCH), lambda i: (0, 0)),
                  pl.BlockSpec((N_HEADS, SG_CHUNK, SG_CHUNK), lambda i: (0, 0, 0)),
                  pl.BlockSpec((N_HEADS, SG_CHUNK, 1), lambda i: (0, 0, 0))],
        out_specs=pl.BlockSpec((tm, D_BRANCH), lambda i: (i, 0)),
        out_shape=jax.ShapeDtypeStruct((s, D_BRANCH), F32),
        compiler_params=_params(("arbitrary",)),
    )(uv, gain, w_s, b_col)


def mix_b_bwd(uv, gain, w_s, b_col, dy):
    s = uv.shape[0]
    tm = min(ROW_T, s)

    def body(uv_ref, gain_ref, w_ref, b_ref, dy_ref, duv_ref, dw_ref, db_ref, dgain_ref):
        i = pl.program_id(0)

        @pl.when(i == 0)
        def _():
            dw_ref[...] = jnp.zeros_like(dw_ref)
            db_ref[...] = jnp.zeros_like(db_ref)
            dgain_ref[...] = jnp.zeros_like(dgain_ref)

        tril = _tril_mask()
        ws = [jnp.where(tril, w_ref[g], 0.0).astype(BF16) for g in range(N_HEADS)]
        gain_v = gain_ref[...]
        for c in range(tm // SG_CHUNK):
            rows = slice(c * SG_CHUNK, (c + 1) * SG_CHUNK)
            u = uv_ref[rows, 0:D_BRANCH]
            xhat, rstd, vn = _b_norm(uv_ref[rows, D_BRANCH:2 * D_BRANCH], gain_v)
            vnb = vn.astype(BF16)
            dyv = dy_ref[rows, :]
            dus, dvns = [], []
            for g in range(N_HEADS):
                cols = slice(g * HEAD_DIM, (g + 1) * HEAD_DIM)
                mixed = _dot(ws[g], vnb[:, cols]) + b_ref[g]
                dus.append(dyv[:, cols] * mixed)
                dmixed = dyv[:, cols] * u[:, cols]
                dmb = dmixed.astype(BF16)
                db_ref[g] += jnp.sum(dmixed, axis=-1, keepdims=True)
                dw_ref[g] += jnp.where(tril, _dot_nt(dmb, vnb[:, cols]), 0.0)
                dvns.append(_dot_tn(ws[g], dmb))
            dvn = jnp.concatenate(dvns, axis=1)
            dgain_ref[...] += jnp.sum(dvn * xhat, axis=0, keepdims=True)
            dxh = dvn * gain_v
            dv = rstd * (dxh - jnp.mean(dxh, axis=-1, keepdims=True)
                         - xhat * jnp.mean(dxh * xhat, axis=-1, keepdims=True))
            duv_ref[rows, :] = jnp.concatenate(dus + [dv], axis=1)

    return pl.pallas_call(
        body, name="mix_b_bwd", grid=(s // tm,),
        in_specs=[pl.BlockSpec((tm, 2 * D_BRANCH), lambda i: (i, 0)),
                  pl.BlockSpec((1, D_BRANCH), lambda i: (0, 0)),
                  pl.BlockSpec((N_HEADS, SG_CHUNK, SG_CHUNK), lambda i: (0, 0, 0)),
                  pl.BlockSpec((N_HEADS, SG_CHUNK, 1), lambda i: (0, 0, 0)),
                  pl.BlockSpec((tm, D_BRANCH), lambda i: (i, 0))],
        out_specs=[pl.BlockSpec((tm, 2 * D_BRANCH), lambda i: (i, 0)),
                   pl.BlockSpec((N_HEADS, SG_CHUNK, SG_CHUNK), lambda i: (0, 0, 0)),
                   pl.BlockSpec((N_HEADS, SG_CHUNK, 1), lambda i: (0, 0, 0)),
                   pl.BlockSpec((1, D_BRANCH), lambda i: (0, 0))],
        out_shape=[jax.ShapeDtypeStruct((s, 2 * D_BRANCH), F32),
                   jax.ShapeDtypeStruct((N_HEADS, SG_CHUNK, SG_CHUNK), F32),
                   jax.ShapeDtypeStruct((N_HEADS, SG_CHUNK, 1), F32),
                   jax.ShapeDtypeStruct((1, D_BRANCH), F32)],
        compiler_params=_params(("arbitrary",)),
    )(uv, gain, w_s, b_col, dy)


def _scan_mats(nrow):
    a = lax.broadcasted_iota(jnp.int32, (128, 128), 0)
    b = lax.broadcasted_iota(jnp.int32, (128, 128), 1)
    r = lax.broadcasted_iota(jnp.int32, (nrow, nrow), 0)
    c = lax.broadcasted_iota(jnp.int32, (nrow, nrow), 1)
    nb = nrow // N_HEADS
    same = (r // nb) == (c // nb)
    return a, b, r, c, same


def _exact_dot(x, m):
    hi, mid, lo = _split3(x)
    return _dot(hi, m) + _dot(mid, m) + _dot(lo, m)


def _exact_dot_left(m, x):
    hi, mid, lo = _split3(x)
    return _dot(m, hi) + _dot(m, mid) + _dot(m, lo)


def fox_gate_fwd(ft, bcol):
    nrow = ft.shape[0]

    def body(f_ref, b_ref, c_ref):
        z = f_ref[...] + b_ref[...]
        ls = jnp.minimum(z, 0.0) - jnp.log(1.0 + jnp.exp(-jnp.abs(z)))
        a, b, r, c, same = _scan_mats(nrow)
        within = _exact_dot(ls, (a <= b).astype(BF16))
        tot = jnp.broadcast_to(within[:, 127:128], within.shape)
        before = _exact_dot_left((same & (c < r)).astype(BF16), tot)
        c_ref[...] = within + before

    return pl.pallas_call(
        body, name="fox_gate_fwd",
        out_shape=jax.ShapeDtypeStruct((nrow, 128), F32),
        compiler_params=_params(),
    )(ft, bcol)


def fox_gate_bwd(ft, bcol, dc):
    nrow = ft.shape[0]

    def body(f_ref, b_ref, dc_ref, df_ref, db_ref):
        a, b, r, c, same = _scan_mats(nrow)
        dcv = dc_ref[...]
        within = _exact_dot(dcv, (a >= b).astype(BF16))
        tot = jnp.broadcast_to(within[:, 0:1], within.shape)
        after = _exact_dot_left((same & (c > r)).astype(BF16), tot)
        dls = within + after
        z = f_ref[...] + b_ref[...]
        dz = dls * _sigmoid(-z)
        df_ref[...] = dz
        rs = jnp.broadcast_to(jnp.sum(dz, axis=-1, keepdims=True), dz.shape)
        hr = lax.broadcasted_iota(jnp.int32, (8, nrow), 0)
        hc = lax.broadcasted_iota(jnp.int32, (8, nrow), 1)
        db_ref[...] = _exact_dot_left((hr == hc // (nrow // N_HEADS)).astype(BF16), rs)

    return pl.pallas_call(
        body, name="fox_gate_bwd",
        out_shape=[jax.ShapeDtypeStruct((nrow, 128), F32), jax.ShapeDtypeStruct((8, 128), F32)],
        compiler_params=_params(),
    )(ft, bcol, dc)


def _att_specs(s, qi, ki, vi):
    q_spec = pl.BlockSpec((None, None, FOX_TQ, HEAD_DIM), lambda h, i: (qi, h, i, 0))
    k_spec = pl.BlockSpec((None, None, s, HEAD_DIM), lambda h, i: (ki, h, 0, 0))
    v_spec = pl.BlockSpec((None, None, s, HEAD_DIM), lambda h, i: (vi, h, 0, 0))
    row_spec = lambda w: pl.BlockSpec((None, FOX_TQ, w), lambda h, i: (h, i, 0))
    gate_spec = pl.BlockSpec((None, s // ATT_T, 1, ATT_T), lambda h, i: (h, 0, 0, 0))
    return q_spec, k_spec, v_spec, row_spec, gate_spec


def _causal(n):
    row = lax.broadcasted_iota(jnp.int32, (n, n), 0)
    col = lax.broadcasted_iota(jnp.int32, (n, n), 1)
    return col <= row


def _gate_row(cr_ref, kb, g):
    if g == 1:
        return cr_ref[kb]
    return jnp.concatenate([cr_ref[kb + n] for n in range(g)], axis=1)


def _fox_walk(i, carry, tile, alive):
    g = FOX_WIDE
    own = FOX_TQ // ATT_T
    nwide = (own * i) // g
    carry = tile(own * i, own, carry, True)
    carry = lax.fori_loop(0, (own * i - nwide * g) // own, lambda n, c: tile(nwide * g, own, c, False), carry)

    def cond(state):
        return jnp.logical_and(state[0] >= 0, state[1] > 0)

    def step(state):
        n = state[0]
        c = tile(n * g, g, state[2:], False)
        return (n - 1, alive(n * g, c)) + tuple(c)

    out = lax.while_loop(cond, step, (nwide - 1, alive(nwide * g, carry)) + tuple(carry))
    return out[2:]


def _fox_reach(qs, k_ref, kmax_ref, cc, i):
    s = k_ref.shape[0]
    rows = 4 * ATT_T

    @pl.when(i == 0)
    def _():
        def chunk(n, mx):
            kc = k_ref[pl.ds(pl.multiple_of(n * rows, rows), rows), :].astype(F32)
            return jnp.maximum(mx, jnp.max(jnp.sum(kc * kc, axis=-1, keepdims=True)))

        kmax_ref[0] = jnp.sqrt(lax.fori_loop(0, s // rows, chunk, jnp.float32(0.0)))

    qf = qs.astype(F32)
    return jnp.sqrt(jnp.sum(qf * qf, axis=-1, keepdims=True)) * kmax_ref[0] + cc


def _gate_col(cr_ref, i):
    row = lax.broadcasted_iota(jnp.int32, (ATT_T, ATT_T), 0)
    col = lax.broadcasted_iota(jnp.int32, (ATT_T, ATT_T), 1)
    own = FOX_TQ // ATT_T
    return jnp.concatenate([jnp.sum(jnp.where(row == col, cr_ref[own * i + n], 0.0), axis=-1, keepdims=True)
                            for n in range(own)], axis=0)


def _fox_scores(qs, k, cc, crow, masked):
    sc = (_dot_nt(qs, k) + (cc - crow)) * LOG2E
    if masked:
        sc = jnp.where(_causal(FOX_TQ), sc, NEG)
    return sc


def fox_fwd(qkv, c_row, ride=()):
    s = qkv.shape[2]
    t = ATT_T
    nq = s // FOX_TQ
    q_spec, k_spec, v_spec, row_spec, gate_spec = _att_specs(s, 1, 2, 3)
    rows = 4 * t
    nride = len(ride)

    def body(q_ref, k_ref, v_ref, cr_ref, *refs):
        ride_in, refs = refs[:nride], refs[nride:]
        o_ref, ref_ref, rl_ref = refs[:3]
        ride_out, refs = refs[3:3 + nride], refs[3 + nride:]
        v1_ref, kmax_ref = refs[:2]
        i = pl.program_id(1)
        if nride:
            h = pl.program_id(0)
            start, wait = _chip_gather([(src, lambda slot, dst=dst: dst.at[slot]) for src, dst in zip(ride_in, ride_out)],
                                       *refs[2:])
            pl.when(jnp.logical_and(h == 0, i == 0))(start)

        @pl.when(i == 0)
        def _():
            def chunk(n, carry):
                r0 = pl.multiple_of(n * rows, rows)
                v1_ref[pl.ds(r0, rows), :] = jnp.concatenate(
                    [v_ref[pl.ds(r0, rows), :], jnp.ones((rows, HEAD_DIM), BF16)], axis=1)
                return carry

            lax.fori_loop(0, s // rows, chunk, 0)

        qs = q_ref[...] * 0.125
        cc = _gate_col(cr_ref, i)
        reach = _fox_reach(qs, k_ref, kmax_ref, cc, i) * LOG2E

        def alive(kb, carry):
            return (jnp.max(reach - cr_ref[kb][:, 0:1] * LOG2E - carry[0]) > FOX_DEAD2).astype(jnp.int32)

        def tile(kb, g, carry, masked):
            m, acc = carry
            k0 = pl.multiple_of(kb * t, t)
            sc = _fox_scores(qs, k_ref[pl.ds(k0, g * t), :], cc, _gate_row(cr_ref, kb, g), masked)
            m_new = jnp.maximum(m, jnp.ceil(jnp.max(sc, axis=-1, keepdims=True)))
            pb = jnp.exp2(sc - m_new).astype(BF16)
            acc = jnp.exp2(m - m_new) * acc + _dot(pb, v1_ref[pl.ds(k0, g * t), :])
            return m_new, acc

        init = (jnp.full((FOX_TQ, 1), NEG, F32), jnp.zeros((FOX_TQ, 2 * HEAD_DIM), F32))
        m, acc = _fox_walk(i, init, tile, alive)
        rl = 1.0 / acc[:, HEAD_DIM:HEAD_DIM + 1]
        o_ref[...] = acc[:, 0:HEAD_DIM] * rl
        ref_ref[...] = m
        rl_ref[...] = rl
        if nride:
            pl.when(jnp.logical_and(h == N_HEADS - 1, i == nq - 1))(wait)

    any_spec = pl.BlockSpec(memory_space=pl.ANY)
    ride_sems = [pltpu.SemaphoreType.DMA((3 * nride,)), pltpu.SemaphoreType.DMA((3 * nride,)),
                 pltpu.SemaphoreType.DMA((nride,))] if nride else []
    return pl.pallas_call(
        body, name="fox_fwd_gather" if nride else "fox_fwd", grid=(N_HEADS, nq),
        in_specs=[q_spec, k_spec, v_spec, gate_spec] + [any_spec] * nride,
        out_specs=[row_spec(HEAD_DIM), row_spec(1), row_spec(1)] + [any_spec] * nride,
        out_shape=[jax.ShapeDtypeStruct((N_HEADS, s, HEAD_DIM), F32),
                   jax.ShapeDtypeStruct((N_HEADS, s, 1), F32),
                   jax.ShapeDtypeStruct((N_HEADS, s, 1), F32)]
        + [jax.ShapeDtypeStruct((4,) + a.shape, a.dtype) for a in ride],
        scratch_shapes=[pltpu.VMEM((s, 2 * HEAD_DIM), BF16), pltpu.SMEM((1,), F32)] + ride_sems,
        compiler_params=_params(("arbitrary", "arbitrary")),
    )(qkv, qkv, qkv, c_row, *ride)


def fox_bwd(qkv, c_row, do, o, ref, rl, ride=()):
    s = qkv.shape[2]
    t = ATT_T
    nq = s // FOX_TQ
    q_spec, k_spec, v_spec, row_spec, gate_spec = _att_specs(s, 1, 2, 3)
    any_spec = pl.BlockSpec(memory_space=pl.ANY)
    nride = len(ride)

    def body(q_ref, k_ref, v_ref, cr_ref, do_ref, o_ref, ref_ref, rl_ref, *refs):
        ride_in, refs = refs[:nride], refs[nride:]
        dq_ref, dk_hbm, dv_hbm, dc_ref = refs[:4]
        ride_out, refs = refs[4:4 + nride], refs[4 + nride:]
        dk_acc, dv_acc, kmax_ref = refs[:3]
        h = pl.program_id(0)
        i = pl.program_id(1)
        if nride:
            start, wait = _device_exchange(_exchange_flows(ride_in, ride_out), *refs[3:])
            pl.when(jnp.logical_and(h == 0, i == 0))(start)

        @pl.when(i == 0)
        def _():
            dk_acc[...] = jnp.zeros_like(dk_acc)
            dv_acc[...] = jnp.zeros_like(dv_acc)
            dc_ref[...] = jnp.zeros_like(dc_ref)

        qs = q_ref[...] * 0.125
        ref = ref_ref[...]
        rl = rl_ref[...]
        dob = (do_ref[...].astype(F32) * rl).astype(BF16)
        delta = jnp.sum(o_ref[...] * dob.astype(F32), axis=-1, keepdims=True)
        cc = _gate_col(cr_ref, i)
        margin = _fox_reach(qs, k_ref, kmax_ref, cc, i) * LOG2E - ref

        def alive(kb, carry):
            return (jnp.max(margin - cr_ref[kb][:, 0:1] * LOG2E) > FOX_DEAD2).astype(jnp.int32)

        def tile(kb, g, carry, masked):
            dq, = carry
            k0 = pl.multiple_of(kb * t, t)
            k = k_ref[pl.ds(k0, g * t), :]
            sc = _fox_scores(qs, k, cc, _gate_row(cr_ref, kb, g), masked)
            wb = jnp.exp2(sc - ref).astype(BF16)
            ds = wb.astype(F32) * (_dot_nt(dob, v_ref[pl.ds(k0, g * t), :]) - delta)
            dsb = ds.astype(BF16)
            dk_acc[pl.ds(k0, g * t), :] += _dot_tn(dsb, qs)
            dv_acc[pl.ds(k0, g * t), :] += _dot_tn(wb, dob)
            dcs = -jnp.sum(ds, axis=0, keepdims=True)
            for n in range(g):
                dc_ref[kb + n] += dcs[:, n * t:(n + 1) * t]
            return (dq + _dot(dsb, k),)

        dq, = _fox_walk(i, (jnp.zeros((FOX_TQ, HEAD_DIM), F32),), tile, alive)
        dq_ref[...] = dq * 0.125

        @pl.when(i == nq - 1)
        def _():
            pltpu.sync_copy(dk_acc, dk_hbm.at[h])
            pltpu.sync_copy(dv_acc, dv_hbm.at[h])

        if nride:
            pl.when(jnp.logical_and(h == N_HEADS - 1, i == nq - 1))(wait)

    return pl.pallas_call(
        body, name="fox_bwd_exchange" if nride else "fox_bwd", grid=(N_HEADS, nq),
        in_specs=[q_spec, k_spec, v_spec,
                  gate_spec,
                  row_spec(HEAD_DIM), row_spec(HEAD_DIM), row_spec(1), row_spec(1)] + [any_spec] * nride,
        out_specs=[row_spec(HEAD_DIM), any_spec, any_spec,
                   gate_spec] + [any_spec] * nride,
        out_shape=[jax.ShapeDtypeStruct((N_HEADS, s, HEAD_DIM), F32),
                   jax.ShapeDtypeStruct((N_HEADS, s, HEAD_DIM), F32),
                   jax.ShapeDtypeStruct((N_HEADS, s, HEAD_DIM), F32),
                   jax.ShapeDtypeStruct((N_HEADS, s // t, 1, t), F32)] + _exchange_shapes(ride),
        scratch_shapes=[pltpu.VMEM((s, HEAD_DIM), F32), pltpu.VMEM((s, HEAD_DIM), F32), pltpu.SMEM((1,), F32)]
        + (_exchange_sems(nride) if nride else []),
        compiler_params=_params(("arbitrary", "arbitrary")),
    )(qkv, qkv, qkv, c_row, do, o, ref, rl, *ride)


def _sb_valid(nrows, ahead):
    row = lax.broadcasted_iota(jnp.int32, (nrows, ATT_T), 0)
    col = lax.broadcasted_iota(jnp.int32, (nrows, ATT_T), 1)
    return col + ahead < row


def _sb_band_valid(nsub, i):
    shape = (nsub * SB_SUB, SB_BAND)
    row = lax.broadcasted_iota(jnp.int32, shape, 0)
    col = lax.broadcasted_iota(jnp.int32, shape, 1)
    first = i * SB_TQ + (row - (row & (SB_SUB - 1)))
    valid = col < (row & (SB_SUB - 1)) + jnp.minimum(first, SB_BACK)
    return valid, first[:, 0:1] > SB_BACK


def _sb_logits(qs, k):
    z = _dot_nt(qs, k)
    sp = jnp.log(1.0 + jnp.exp(-jnp.abs(z)))
    return jnp.minimum(z, 0.0) - sp, -jnp.maximum(z, 0.0) - sp


def _sb_weights(ls, lm, run, valid):
    if valid is not None:
        lm = jnp.where(valid, lm, 0.0)
    n = lm.shape[1]
    row = lax.broadcasted_iota(jnp.int32, (n, n), 0)
    col = lax.broadcasted_iota(jnp.int32, (n, n), 1)
    later = (row > col).astype(BF16)
    hi, lo = _split2(lm)
    between = _dot(hi, later) + _dot(lo, later)
    if run is not None:
        between = run + between
    a = jnp.exp(ls + between)
    if valid is not None:
        a = jnp.where(valid, a, 0.0)
    return lm, a


def _sb_band_start(i, j):
    return pl.multiple_of(jnp.maximum(i * SB_TQ + j * SB_SUB - SB_BACK, 0), SB_SUB)


def _sb_tile(qs, k, run, valid):
    ls, lm = _sb_logits(qs, k)
    lm, a = _sb_weights(ls, lm, run, valid)
    return ls, lm, a


def _sb_band(i, qs_all, k_ref):
    nsub = qs_all.shape[0] // SB_SUB
    valid, open_left = _sb_band_valid(nsub, i)
    starts = [_sb_band_start(i, j) for j in range(nsub)]
    kwins = [k_ref[pl.ds(k0, SB_BAND), :] for k0 in starts]
    parts = [_sb_logits(qs_all[j * SB_SUB:(j + 1) * SB_SUB], kwins[j]) for j in range(nsub)]
    ls = jnp.concatenate([p[0] for p in parts], axis=0)
    lm, a = _sb_weights(ls, jnp.concatenate([p[1] for p in parts], axis=0), None, valid)
    return starts, kwins, ls, lm, a, valid, open_left


def _sb_suffix(g, run_g):
    n = g.shape[1]
    row = lax.broadcasted_iota(jnp.int32, (n, n), 0)
    col = lax.broadcasted_iota(jnp.int32, (n, n), 1)
    from_here = (row >= col).astype(BF16)
    hi, lo = _split2(g)
    out = _dot(hi, from_here) + _dot(lo, from_here)
    return out if run_g is None else run_g + out


def _sb_walk(i, carry, tile):
    def alive_of(c):
        return (jnp.max(c[0]) > SB_DEAD).astype(jnp.int32)

    def cond(state):
        n, alive = state[0], state[1]
        return jnp.logical_and(n < i, alive > 0)

    def step(state):
        n = state[0]
        c = tile(i - 1 - n, state[2:], False)
        return (n + 1, alive_of(c)) + tuple(c)

    out = lax.while_loop(cond, step, (jnp.int32(0), alive_of(carry)) + tuple(carry))
    return out[2:]


def _sb_specs(s):
    tq = SB_TQ
    q_spec = pl.BlockSpec((None, None, tq, HEAD_DIM), lambda h, i: (4, h, i, 0))
    k_spec = pl.BlockSpec((None, None, s, HEAD_DIM), lambda h, i: (5, h, 0, 0))
    v_spec = pl.BlockSpec((None, None, s, HEAD_DIM), lambda h, i: (6, h, 0, 0))
    row_spec = pl.BlockSpec((None, tq, HEAD_DIM), lambda h, i: (h, i, 0))
    band_spec = pl.BlockSpec((None, None, 1, 128), lambda h, i: (h, i, 0, 0))
    return tq, q_spec, k_spec, v_spec, row_spec, band_spec


def _sb_block(b, row0, tile, zero):
    t = ATT_T
    lo, hi, both = slice(row0, row0 + t), slice(row0 + t, row0 + 2 * t), slice(row0, row0 + 2 * t)
    c_hi = tile(2 * b + 1, hi, zero, 0)
    c_lo = tile(2 * b, lo, zero, 0)
    c_hi = tile(2 * b, hi, c_hi, None)
    carry = tuple(jnp.concatenate([x, y], axis=0) for x, y in zip(c_lo, c_hi))
    return _sb_walk(2 * b, carry, lambda kb, c, _: tile(kb, both, c, None))


def sb_fwd(qkv):
    s = qkv.shape[2]
    t = ATT_T
    tq, q_spec, k_spec, v_spec, row_spec, band_spec = _sb_specs(s)

    def body(q_ref, k_ref, v_ref, o_ref, band_ref, done_ref):
        i = pl.program_id(1)
        qs = q_ref[...] * 0.125
        starts, _, _, lm, a, _, open_left = _sb_band(i, qs, k_ref)
        ab = a.astype(BF16)
        for j, k0 in enumerate(starts):
            rows = slice(j * SB_SUB, (j + 1) * SB_SUB)
            o_ref[rows, :] = _dot(ab[rows], v_ref[pl.ds(k0, SB_BAND), :])
        worst = jnp.max(jnp.where(open_left, jnp.sum(lm, axis=-1, keepdims=True), NEG))
        done_ref[0] = (worst <= SB_DEAD).astype(jnp.int32)

        @pl.when(done_ref[0] == 0)
        def _():
            def tile(kb, rows, carry, ahead):
                run, acc = carry
                k0 = pl.multiple_of(kb * t, t)
                valid = None if ahead is None else _sb_valid(t, ahead)
                _, lm, a = _sb_tile(qs[rows], k_ref[pl.ds(k0, t), :], run, valid)
                acc = acc + _dot(a.astype(BF16), v_ref[pl.ds(k0, t), :])
                return run + jnp.sum(lm, axis=-1, keepdims=True), acc

            for n in range(tq // (2 * t)):
                _, acc = _sb_block(i * (tq // (2 * t)) + n, n * 2 * t, tile,
                                   (jnp.zeros((t, 1), F32), jnp.zeros((t, HEAD_DIM), F32)))
                o_ref[n * 2 * t:(n + 1) * 2 * t, :] = acc

        band_ref[...] = jnp.full(band_ref.shape, done_ref[0], jnp.int32).astype(F32)

    return pl.pallas_call(
        body, name="sb_fwd", grid=(N_HEADS, s // tq),
        in_specs=[q_spec, k_spec, v_spec],
        out_specs=[row_spec, band_spec],
        out_shape=[jax.ShapeDtypeStruct((N_HEADS, s, HEAD_DIM), F32),
                   jax.ShapeDtypeStruct((N_HEADS, s // tq, 1, 128), F32)],
        scratch_shapes=[pltpu.SMEM((1,), jnp.int32)],
        compiler_params=_params(("arbitrary", "arbitrary")),
    )(qkv, qkv, qkv)


def sb_bwd(qkv, do, o, band, ride=()):
    s = qkv.shape[2]
    t = ATT_T
    tq, q_spec, k_spec, v_spec, row_spec, band_spec = _sb_specs(s)
    nq = s // tq
    any_spec = pl.BlockSpec(memory_space=pl.ANY)
    nride = len(ride)

    def body(q_ref, k_ref, v_ref, do_ref, o_ref, band_ref, *refs):
        ride_in, refs = refs[:nride], refs[nride:]
        dq_ref, dk_hbm, dv_hbm = refs[:3]
        ride_out, refs = refs[3:3 + nride], refs[3 + nride:]
        dk_acc, dv_acc = refs[:2]
        h = pl.program_id(0)
        i = pl.program_id(1)
        if nride:
            start, wait = _device_exchange(_exchange_flows(ride_in, ride_out), *refs[2:])
            pl.when(jnp.logical_and(h == 0, i == 0))(start)

        @pl.when(i == 0)
        def _():
            dk_acc[...] = jnp.zeros_like(dk_acc)
            dv_acc[...] = jnp.zeros_like(dv_acc)

        qs_all = q_ref[...] * 0.125
        dob_all = do_ref[...]
        tot_all = jnp.sum(o_ref[...] * dob_all.astype(F32), axis=-1, keepdims=True)
        on_band = jnp.max(band_ref[...]) > 0.5

        def grads(qs, dob, tot, k, v, k0, run, run_g, valid):
            ls, lm, a = _sb_tile(qs, k, run, valid)
            ab = a.astype(BF16)
            g = ab.astype(F32) * _dot_nt(dob, v)
            g_left = tot - _sb_suffix(g, run_g)
            dz = g - jnp.exp(ls) * (g + g_left)
            if valid is not None:
                dz = jnp.where(valid, dz, 0.0)
            dzb = dz.astype(BF16)
            n = k.shape[0]
            dk_acc[pl.ds(k0, n), :] += _dot_tn(dzb, qs)
            dv_acc[pl.ds(k0, n), :] += _dot_tn(ab, dob)
            return dzb, lm, g

        @pl.when(on_band)
        def _():
            starts, kwins, ls, _, a, valid, _ = _sb_band(i, qs_all, k_ref)
            ab = a.astype(BF16)
            subs = [slice(j * SB_SUB, (j + 1) * SB_SUB) for j in range(len(starts))]
            vwins = [v_ref[pl.ds(k0, SB_BAND), :] for k0 in starts]
            g = ab.astype(F32) * jnp.concatenate([_dot_nt(dob_all[r], v) for r, v in zip(subs, vwins)], axis=0)
            dz = jnp.where(valid, g - jnp.exp(ls) * (g + (tot_all - _sb_suffix(g, None))), 0.0)
            dzb = dz.astype(BF16)
            for r, k0, k in zip(subs, starts, kwins):
                dq_ref[r, :] = _dot(dzb[r], k) * 0.125
                dk_acc[pl.ds(k0, SB_BAND), :] += _dot_tn(dzb[r], qs_all[r])
                dv_acc[pl.ds(k0, SB_BAND), :] += _dot_tn(ab[r], dob_all[r])

        @pl.when(jnp.logical_not(on_band))
        def _():
            def tile(kb, rows, carry, ahead):
                run, run_g, dq = carry
                k0 = pl.multiple_of(kb * t, t)
                k = k_ref[pl.ds(k0, t), :]
                valid = None if ahead is None else _sb_valid(t, ahead)
                dzb, lm, g = grads(qs_all[rows], dob_all[rows], tot_all[rows], k, v_ref[pl.ds(k0, t), :], k0,
                                   run, run_g, valid)
                return (run + jnp.sum(lm, axis=-1, keepdims=True),
                        run_g + jnp.sum(g, axis=-1, keepdims=True),
                        dq + _dot(dzb, k))

            zero = jnp.zeros((t, 1), F32)
            for n in range(tq // (2 * t)):
                _, _, dq = _sb_block(i * (tq // (2 * t)) + n, n * 2 * t, tile, (zero, zero, jnp.zeros((t, HEAD_DIM), F32)))
                dq_ref[n * 2 * t:(n + 1) * 2 * t, :] = dq * 0.125

        @pl.when(i == nq - 1)
        def _():
            pltpu.sync_copy(dk_acc, dk_hbm.at[h])
            pltpu.sync_copy(dv_acc, dv_hbm.at[h])

        if nride:
            pl.when(jnp.logical_and(h == N_HEADS - 1, i == nq - 1))(wait)

    return pl.pallas_call(
        body, name="sb_bwd_exchange" if nride else "sb_bwd", grid=(N_HEADS, nq),
        in_specs=[q_spec, k_spec, v_spec, row_spec, row_spec, band_spec] + [any_spec] * nride,
        out_specs=[row_spec, any_spec, any_spec] + [any_spec] * nride,
        out_shape=[jax.ShapeDtypeStruct((N_HEADS, s, HEAD_DIM), F32)] * 3 + _exchange_shapes(ride),
        scratch_shapes=[pltpu.VMEM((s, HEAD_DIM), F32), pltpu.VMEM((s, HEAD_DIM), F32)]
        + (_exchange_sems(nride) if nride else []),
        compiler_params=_params(("arbitrary", "arbitrary")),
    )(qkv, qkv, qkv, do, o, band, *ride)


def _branch_inputs(refs, br):
    ya_ref, yb_ref, yc_ref, yd_ref = refs
    if br == 1:
        return yb_ref[...]
    return _heads_to_lanes((ya_ref, None, yc_ref, yd_ref)[br])


def outproj_fwd(x, ya, yb, yc, yd, gates, bg, wout):
    s = x.shape[0]
    tm = min(ROW_T, s)

    def body(x_ref, ya_ref, yb_ref, yc_ref, yd_ref, gates_ref, bg_ref, w_ref, out_ref):
        pieces = []
        for br in range(4):
            cols = slice(br * D_BRANCH, (br + 1) * D_BRANCH)
            y = _branch_inputs((ya_ref, yb_ref, yc_ref, yd_ref), br)
            r = lax.rsqrt(jnp.mean(y * y, axis=-1, keepdims=True) + EPS)
            gt = gates_ref[:, cols]
            pieces.append((y * r * bg_ref[:, cols]) * (gt * _sigmoid(gt)))
        merged = jnp.concatenate(pieces, axis=1).astype(BF16)
        out_ref[...] = x_ref[...] + _dot(merged, w_ref[...])

    head_spec = pl.BlockSpec((N_HEADS, tm, HEAD_DIM), lambda i: (0, i, 0))
    return pl.pallas_call(
        body, name="outproj_fwd", grid=(s // tm,),
        in_specs=[pl.BlockSpec((tm, D_MODEL), lambda i: (i, 0)),
                  head_spec, pl.BlockSpec((tm, D_BRANCH), lambda i: (i, 0)), head_spec, head_spec,
                  pl.BlockSpec((tm, D_MODEL), lambda i: (i, 0)),
                  pl.BlockSpec((1, D_MODEL), lambda i: (0, 0)),
                  pl.BlockSpec((D_MODEL, D_MODEL), lambda i: (0, 0))],
        out_specs=pl.BlockSpec((tm, D_MODEL), lambda i: (i, 0)),
        out_shape=jax.ShapeDtypeStruct((s, D_MODEL), F32),
        compiler_params=_params(("arbitrary",)),
    )(x, ya, yb, yc, yd, gates, bg, wout)


def outproj_bwd(dout, ya, yb, yc, yd, gates, bg, wout):
    s = dout.shape[0]
    tm = min(ROW_T, s)

    def body(dout_ref, ya_ref, yb_ref, yc_ref, yd_ref, gates_ref, bg_ref, w_ref,
             dya_ref, dyb_ref, dyc_ref, dyd_ref, dgates_ref, dbg_ref, dw_ref):
        i = pl.program_id(0)

        @pl.when(i == 0)
        def _():
            dbg_ref[...] = jnp.zeros_like(dbg_ref)
            dw_ref[...] = jnp.zeros_like(dw_ref)

        doutb = dout_ref[...].astype(BF16)
        dmerged = _dot_nt(doutb, w_ref[...])
        pieces = []
        for br in range(4):
            cols = slice(br * D_BRANCH, (br + 1) * D_BRANCH)
            y = _branch_inputs((ya_ref, yb_ref, yc_ref, yd_ref), br)
            r = lax.rsqrt(jnp.mean(y * y, axis=-1, keepdims=True) + EPS)
            yn = y * r
            bgv = bg_ref[:, cols]
            gt = gates_ref[:, cols]
            sig = _sigmoid(gt)
            act = gt * sig
            n = yn * bgv
            pieces.append(n * act)
            dm = dmerged[:, cols]
            dn = dm * act
            dgates_ref[:, cols] = (dm * n * (sig * (1.0 + gt * (1.0 - sig)))).astype(BF16)
            dbg_ref[:, cols] += jnp.sum(dn * yn, axis=0, keepdims=True)
            u = dn * bgv
            dy = r * (u - yn * jnp.mean(yn * u, axis=-1, keepdims=True))
            if br == 1:
                dyb_ref[...] = dy
            else:
                dref = (dya_ref, None, dyc_ref, dyd_ref)[br]
                for hh in range(N_HEADS):
                    dref[hh] = dy[:, hh * HEAD_DIM:(hh + 1) * HEAD_DIM].astype(BF16)
        merged = jnp.concatenate(pieces, axis=1).astype(BF16)
        dw_ref[...] += _dot_tn(merged, doutb)

    head_spec = pl.BlockSpec((N_HEADS, tm, HEAD_DIM), lambda i: (0, i, 0))
    head_shape = jax.ShapeDtypeStruct((N_HEADS, s, HEAD_DIM), BF16)
    return pl.pallas_call(
        body, name="outproj_bwd", grid=(s // tm,),
        in_specs=[pl.BlockSpec((tm, D_MODEL), lambda i: (i, 0)),
                  head_spec, pl.BlockSpec((tm, D_BRANCH), lambda i: (i, 0)), head_spec, head_spec,
                  pl.BlockSpec((tm, D_MODEL), lambda i: (i, 0)),
                  pl.BlockSpec((1, D_MODEL), lambda i: (0, 0)),
                  pl.BlockSpec((D_MODEL, D_MODEL), lambda i: (0, 0))],
        out_specs=[head_spec, pl.BlockSpec((tm, D_BRANCH), lambda i: (i, 0)), head_spec, head_spec,
                   pl.BlockSpec((tm, D_MODEL), lambda i: (i, 0)),
                   pl.BlockSpec((1, D_MODEL), lambda i: (0, 0)),
                   pl.BlockSpec((D_MODEL, D_MODEL), lambda i: (0, 0))],
        out_shape=[head_shape, jax.ShapeDtypeStruct((s, D_BRANCH), F32), head_shape, head_shape,
                   jax.ShapeDtypeStruct((s, D_MODEL), BF16),
                   jax.ShapeDtypeStruct((1, D_MODEL), F32),
                   jax.ShapeDtypeStruct((D_MODEL, D_MODEL), F32)],
        compiler_params=_params(("arbitrary",)),
    )(dout, ya, yb, yc, yd, gates, bg, wout)


def final_loss(x, tgt, g):
    s = x.shape[0]
    tm = min(ROW_T, s)

    def body(x_ref, t_ref, g_ref, loss_ref, dx_ref, dg_ref):
        i = pl.program_id(0)

        @pl.when(i == 0)
        def _():
            loss_ref[...] = jnp.zeros_like(loss_ref)
            dg_ref[...] = jnp.zeros_like(dg_ref)

        xv = x_ref[...]
        gv = g_ref[...]
        r = lax.rsqrt(jnp.mean(xv * xv, axis=-1, keepdims=True) + EPS)
        xn = xv * r
        err = xn * gv - t_ref[...]
        loss_ref[...] += jnp.sum(err * err) * (0.5 / D_MODEL)
        dy = err * (1.0 / D_MODEL)
        u = dy * gv
        dx_ref[...] = r * (u - xn * jnp.mean(xn * u, axis=-1, keepdims=True))
        dg_ref[...] += jnp.sum(dy * xn, axis=0, keepdims=True)

    return pl.pallas_call(
        body, name="final_loss", grid=(s // tm,),
        in_specs=[pl.BlockSpec((tm, D_MODEL), lambda i: (i, 0)),
                  pl.BlockSpec((tm, D_MODEL), lambda i: (i, 0)),
                  pl.BlockSpec((1, D_MODEL), lambda i: (0, 0))],
        out_specs=[pl.BlockSpec((1, 128), lambda i: (0, 0)),
                   pl.BlockSpec((tm, D_MODEL), lambda i: (i, 0)),
                   pl.BlockSpec((1, D_MODEL), lambda i: (0, 0))],
        out_shape=[jax.ShapeDtypeStruct((1, 128), F32),
                   jax.ShapeDtypeStruct((s, D_MODEL), F32),
                   jax.ShapeDtypeStruct((1, D_MODEL), F32)],
        compiler_params=_params(("arbitrary",)),
    )(x, tgt, g)


def _rel_index():
    i = np.arange(A_TQ)[:, None]
    j = np.arange(A_BAND)[None, :]
    rel = np.clip(i - j + (A_BAND - A_TQ), -MAX_REL, MAX_REL) + MAX_REL
    dchunk = i // CHUNK + LOOKBACK - j // CHUNK
    valid = (dchunk >= 0) & (dchunk <= LOOKBACK)
    return jnp.asarray(np.where(valid, rel, -1).astype(np.int32))


def _layer_consts(p):
    tbias = relbias_tile(p["rel_bias"], _rel_index())
    return dict(
        norm_g=p["norm_g"].reshape(1, D_MODEL),
        v_gain=p["v_gain"].reshape(1, D_BRANCH),
        b_col=p["b_s"].reshape(N_HEADS, SG_CHUNK, 1),
        bg=p["branch_gain"].reshape(1, D_MODEL),
        tbias=tbias,
    )


def _gate_layout(fp, b_f, s):
    nb = s // 128
    ft = fp[:, :N_HEADS].T.reshape(N_HEADS * nb, 128)
    bcol = jnp.repeat(b_f, nb).reshape(N_HEADS * nb, 1)
    return ft, bcol


def layer_fwd(x, p, ride=()):
    s = x.shape[0]
    c = _layer_consts(p)
    h, qkv, kva, gates, uv, fp = inproj_fwd(x, c["norm_g"], p["wp"])
    ya, lse_a = mix_a_fwd(qkv, kva, c["tbias"])
    yb = mix_b_fwd(uv, c["v_gain"], p["w_s"], c["b_col"])
    ft, bcol = _gate_layout(fp, p["b_f"], s)
    c_row = fox_gate_fwd(ft, bcol).reshape(N_HEADS, s // ATT_T, 1, ATT_T)
    yc, ref_c, rl_c, *rode = fox_fwd(qkv, c_row, ride)
    yd, band_d = sb_fwd(qkv)
    wout = p["wout"] if "wout" in p else rode[0].reshape(D_MODEL, D_MODEL)
    out = outproj_fwd(x, ya, yb, yc, yd, gates, c["bg"], wout)
    saved = dict(wout=wout, consts=c, x=x, h=h, qkv=qkv, gates=gates, uv=uv, kva=kva, ft=ft, bcol=bcol,
                 c_row=c_row, ya=ya, lse_a=lse_a, yb=yb, yc=yc, ref_c=ref_c, rl_c=rl_c, yd=yd, band_d=band_d)
    return out, saved, rode


def layer_bwd(dout, p, sv, exchange=False, upper_w_in=None, small_ride=None):
    s = dout.shape[0]
    c = sv["consts"]
    dya, dyb, dyc, dyd, dgates, dbg, dwout = outproj_bwd(
        dout, sv["ya"], sv["yb"], sv["yc"], sv["yd"], sv["gates"], c["bg"], sv["wout"])
    dqa, dka, dva, dt = mix_a_bwd(sv["qkv"], sv["kva"], c["tbias"], dya, sv["ya"], sv["lse_a"])
    drel = relbias_grad(dt, _rel_index())[:N_HEADS, :2 * MAX_REL + 1]
    duv, dws, dbs, dvgain = mix_b_bwd(sv["uv"], c["v_gain"], p["w_s"], c["b_col"], dyb)
    ride = [dwout.astype(BF16).reshape(4, D_BRANCH, D_MODEL)] if exchange else []
    if upper_w_in is not None:
        ride.append(upper_w_in)
    dqc, dkc, dvc, dc, *rode = fox_bwd(sv["qkv"], sv["c_row"], dyc, sv["yc"], sv["ref_c"], sv["rl_c"], ride)
    dft, dbf = fox_gate_bwd(sv["ft"], sv["bcol"], dc.reshape(N_HEADS * (s // 128), 128))
    dfp = jnp.pad(dft.reshape(N_HEADS, s).T, ((0, 0), (0, 128 - N_HEADS)))
    grads = dict(b_f=dbf[:N_HEADS, 0], rel_bias=drel, w_s=dws, b_s=dbs.reshape(N_HEADS, SG_CHUNK),
                 v_gain=dvgain.reshape(D_BRANCH), branch_gain=dbg.reshape(4, D_BRANCH), wout=dwout)
    dqd, dkd, dvd, *small_parts = sb_bwd(sv["qkv"], dyd, sv["yd"], sv["band_d"], small_ride(grads) if small_ride else ())
    dp, dx, dnorm = inproj_bwd((dqa, dka, dva, dqc, dkc, dvc, dqd, dkd, dvd), dgates, duv, dfp,
                               p["wp"], sv["x"], c["norm_g"], dout)
    grads["norm_g"] = dnorm.reshape(D_MODEL)
    if small_ride:
        top, = inproj_wgrad(sv["h"], dp, 0)
        grads["w_in_shards"], grads["w_in_top_parts"] = inproj_wgrad(sv["h"], dp, 1, [top])
        grads["small_parts"] = small_parts[0]
    else:
        grads["w_in_shards"], = inproj_wgrad(sv["h"], dp)
    if exchange:
        grads["w_out_parts"] = rode[0]
    return dx, grads, (rode[1] if upper_w_in is not None else None)


def local_step(x, tgt, layers, final_g, next_shards=None):
    layers = list(layers)
    saved = []
    cur = x
    for l, p in enumerate(layers):
        ride = next_shards[l] if next_shards is not None else ()
        cur, sv, rode = layer_fwd(cur, p, ride)
        saved.append(sv)
        if len(rode) > 1:
            layers[l + 1] = dict(layers[l + 1], wp=pack_w_in(rode[1][None])[0])
    loss, dcur, dfinal = final_loss(cur, tgt, final_g.reshape(1, D_MODEL))
    grads = [None] * len(layers)
    for l in reversed(range(len(layers))):
        exchange = next_shards is not None
        upper = grads[l + 1]["w_in_shards"] if exchange and l + 1 < len(layers) else None
        small_ride = None
        if exchange and l == 0:
            def small_ride(g0, above=tuple(grads[1:])):
                stacked = {k: jnp.stack([g[k] for g in (g0,) + above]) for k in SMALL_EARLY if k != "final_g"}
                return [_pack([stacked.get(k, dfinal.reshape(D_MODEL)) for k in SMALL_EARLY])]
        dcur, grads[l], got = layer_bwd(dcur, layers[l], saved[l], exchange, upper, small_ride)
        if upper is not None:
            grads[l + 1]["w_in_parts"] = got
    return loss[0, 0], dcur, grads, dfinal.reshape(D_MODEL)


def _chip_gather(pairs, send_sems, recv_sems, loc_sems):
    x, y, c = lax.axis_index("x"), lax.axis_index("y"), lax.axis_index("c")
    me = 2 * x + y
    chips = [(1 - x, y), (x, 1 - y), (1 - x, 1 - y)]
    npair = len(pairs)

    def local():
        return [pltpu.make_async_copy(src, dst(me), loc_sems.at[n]) for n, (src, dst) in enumerate(pairs)]

    def remote(j, n, slot):
        src, dst = pairs[n]
        return pltpu.make_async_remote_copy(
            src_ref=src, dst_ref=dst(slot), send_sem=send_sems.at[npair * j + n], recv_sem=recv_sems.at[npair * j + n],
            device_id=(chips[j][0], chips[j][1], c), device_id_type=MESH)

    def start():
        for cp in local():
            cp.start()
        for j in range(3):
            for n in range(npair):
                remote(j, n, me).start()

    def wait():
        for j in range(3):
            for n in range(npair):
                remote(j, n, 2 * chips[j][0] + chips[j][1]).wait_recv()
        for j in range(3):
            for n in range(npair):
                remote(j, n, me).wait_send()
        for cp in local():
            cp.wait()

    return start, wait


def gather_weights(w_in, gains):
    depth = w_in.shape[0]

    def body(in_ref, g_ref, oin_ref, og_ref, send_sems, recv_sems, loc_sems):
        pairs = [(in_ref, lambda s: oin_ref.at[:, s]), (g_ref, lambda s: og_ref.at[s])]
        start, wait = _chip_gather(pairs, send_sems, recv_sems, loc_sems)
        start()
        wait()

    any_spec = pl.BlockSpec(memory_space=pl.ANY)
    return pl.pallas_call(
        body, name="gather_weights",
        in_specs=[any_spec] * 2, out_specs=[any_spec] * 2,
        out_shape=[jax.ShapeDtypeStruct((depth, 4) + w_in.shape[1:], w_in.dtype),
                   jax.ShapeDtypeStruct((4,) + gains.shape, gains.dtype)],
        scratch_shapes=[pltpu.SemaphoreType.DMA((6,)), pltpu.SemaphoreType.DMA((6,)), pltpu.SemaphoreType.DMA((2,))],
    )(w_in, gains)


def pack_w_in(shards):
    depth = shards.shape[0]
    tr = 256

    def body(s_ref, o_ref):
        full = jnp.concatenate([s_ref[n] for n in range(4)], axis=1)
        o_ref[...] = jnp.concatenate([full[:, :SEC_D_Q], full[:, SEC_D_Q + N_HEADS:], full[:, SEC_D_Q:SEC_D_Q + N_HEADS],
                                      jnp.zeros((tr, N_PACK - N_IN), BF16)], axis=1)

    return pl.pallas_call(
        body, name="pack_w_in", grid=(depth, D_MODEL // tr),
        in_specs=[pl.BlockSpec((None, 4, tr, N_SHARD), lambda l, r: (l, 0, r, 0))],
        out_specs=pl.BlockSpec((None, tr, N_PACK), lambda l, r: (l, r, 0)),
        out_shape=jax.ShapeDtypeStruct((depth, D_MODEL, N_PACK), BF16),
        compiler_params=_params(("arbitrary", "arbitrary")),
    )(shards)


def _device_exchange(flows, send_sems, recv_sems, loc_sems):
    x, y, c = lax.axis_index("x"), lax.axis_index("y"), lax.axis_index("c")
    me_chip = 2 * x + y
    me = 4 * x + 2 * y + c
    peers = [(x, y, 1 - c)]
    for px, py in [(1 - x, y), (x, 1 - y), (1 - x, 1 - y)]:
        peers += [(px, py, c), (px, py, 1 - c)]
    nflow = len(flows)

    def local():
        return [pltpu.make_async_copy(src(me_chip), dst(me), loc_sems.at[f]) for f, (src, dst) in enumerate(flows)]

    def copies(n, chip, slot):
        return [pltpu.make_async_remote_copy(src_ref=src(chip), dst_ref=dst(slot), send_sem=send_sems.at[nflow * n + f],
                                             recv_sem=recv_sems.at[nflow * n + f], device_id=peers[n], device_id_type=MESH)
                for f, (src, dst) in enumerate(flows)]

    def start():
        for cp in local():
            cp.start()
        for n, (px, py, _) in enumerate(peers):
            for cp in copies(n, 2 * px + py, me):
                cp.start()

    def wait():
        for n, (px, py, pc) in enumerate(peers):
            for cp in copies(n, me_chip, 4 * px + 2 * py + pc):
                cp.wait_recv()
        for n, (px, py, _) in enumerate(peers):
            for cp in copies(n, 2 * px + py, me):
                cp.wait_send()
        for cp in local():
            cp.wait()

    return start, wait


def _exchange_flows(srcs, dsts):
    return [((lambda s, src=src: src.at[s]) if src.shape[0] == 4 else (lambda s, src=src: src),
             lambda d, dst=dst: dst.at[d]) for src, dst in zip(srcs, dsts)]


def _exchange_shapes(arrays):
    return [jax.ShapeDtypeStruct((8,) + (a.shape[1:] if a.shape[0] == 4 else a.shape), a.dtype) for a in arrays]


def _exchange_sems(n):
    return [pltpu.SemaphoreType.DMA((7 * n,)), pltpu.SemaphoreType.DMA((7 * n,)), pltpu.SemaphoreType.DMA((n,))]


def exchange_grads(*arrays):
    n = len(arrays)

    def body(*refs):
        start, wait = _device_exchange(_exchange_flows(refs[:n], refs[n:2 * n]), *refs[2 * n:])
        start()
        wait()

    any_spec = pl.BlockSpec(memory_space=pl.ANY)
    return pl.pallas_call(
        body, name="exchange_grads",
        in_specs=[any_spec] * n, out_specs=[any_spec] * n, out_shape=_exchange_shapes(arrays),
        scratch_shapes=_exchange_sems(n),
    )(*arrays)


def adamw_reduce(parts, w, m, v, name, tr):
    rows, width = w.shape
    steps = [p.shape[1] // tr for p in parts]
    offs = [sum(steps[:n]) for n in range(len(parts))]
    c1 = 1.0 - ADAM_B1 ** ADAM_STEP
    c2 = 1.0 - ADAM_B2 ** ADAM_STEP

    def body(*refs):
        p_refs = refs[:len(parts)]
        w_ref, m_ref, v_ref, g_ref, d_ref, nm_ref, nv_ref = refs[len(parts):]
        i = pl.program_id(0)
        p = p_refs[0][...]
        for n in range(1, len(parts)):
            p = jnp.where(i >= offs[n], p_refs[n][...], p)
        g = p[0].astype(F32)
        for n in range(1, 8):
            g = g + p[n].astype(F32)
        g_ref[...] = g
        nm = ADAM_B1 * m_ref[...] + (1.0 - ADAM_B1) * g
        nv = ADAM_B2 * v_ref[...] + (1.0 - ADAM_B2) * (g * g)
        nm_ref[...] = nm
        nv_ref[...] = nv
        d_ref[...] = -ADAM_LR * ((nm / c1) / (jnp.sqrt(nv / c2) + ADAM_EPS) + ADAM_WD * w_ref[...])

    spec = pl.BlockSpec((tr, width), lambda i: (i, 0))
    shape = jax.ShapeDtypeStruct((rows, width), F32)
    return pl.pallas_call(
        body, name=name, grid=(rows // tr,),
        in_specs=[pl.BlockSpec((8, tr, width), lambda i, n=n: (0, jnp.clip(i - offs[n], 0, steps[n] - 1), 0))
                  for n in range(len(parts))] + [spec, spec, spec],
        out_specs=[spec] * 4, out_shape=[shape] * 4,
        compiler_params=_params(("arbitrary",)),
    )(*parts, w, m, v)


SMALL_EARLY = ("b_f", "rel_bias", "w_s", "b_s", "v_gain", "final_g")
WEIGHTS = ("norm_g", "w_in", "b_f", "rel_bias", "w_s", "b_s", "v_gain", "branch_gain", "w_out", "final_g")
PACK_ROW_TILE = 512


def _rows_of(shape):
    return -(-int(np.prod(shape)) // 128)


def _pack(leaves, tile=PACK_ROW_TILE):
    parts = []
    for a in leaves:
        flat = a.reshape(-1).astype(F32)
        parts.append(jnp.pad(flat, (0, _rows_of(a.shape) * 128 - flat.shape[0])))
    flat = jnp.concatenate(parts)
    rows = flat.shape[0] // 128
    total = -(-rows // tile) * tile
    return jnp.pad(flat, (0, (total - rows) * 128)).reshape(total, 128)


def _unpack(slab, shapes):
    out, row = [], 0
    for shp in shapes:
        n = int(np.prod(shp))
        r = _rows_of(shp)
        out.append(slab[row:row + r].reshape(-1)[:n].reshape(shp))
        row += r
    return out


def kernel(x, norm_g, w_in, b_f, rel_bias, w_s, b_s, v_gain, branch_gain, w_out, final_g, loss_target, m_norm_g, m_w_in, m_b_f, m_rel_bias, m_w_s, m_b_s, m_v_gain, m_branch_gain, m_w_out, m_final_g, v_norm_g, v_w_in, v_b_f, v_rel_bias, v_w_s, v_b_s, v_v_gain, v_branch_gain, v_w_out, v_final_g):
    depth = norm_g.shape[0]
    weights = dict(norm_g=norm_g, w_in=w_in, b_f=b_f, rel_bias=rel_bias, w_s=w_s, b_s=b_s, v_gain=v_gain,
                   branch_gain=branch_gain, w_out=w_out, final_g=final_g)
    mom1 = dict(norm_g=m_norm_g, w_in=m_w_in, b_f=m_b_f, rel_bias=m_rel_bias, w_s=m_w_s, b_s=m_b_s,
                v_gain=m_v_gain, branch_gain=m_branch_gain, w_out=m_w_out, final_g=m_final_g)
    mom2 = dict(norm_g=v_norm_g, w_in=v_w_in, b_f=v_b_f, rel_bias=v_rel_bias, w_s=v_w_s, b_s=v_b_s,
                v_gain=v_v_gain, branch_gain=v_branch_gain, w_out=v_w_out, final_g=v_final_g)

    wf = jnp.pad(branch_gain.reshape(-1), (0, 8 * 128 - branch_gain.size)).reshape(8, 128)
    w_in_b, w_out_b = w_in.astype(BF16), w_out.astype(BF16)
    w_in_shards, gf = gather_weights(w_in_b[:1], wf)
    bg_full = gf.reshape(4, -1)[:, :branch_gain.size].reshape((4,) + branch_gain.shape)
    bg_full = jnp.moveaxis(bg_full, 0, 2).reshape(depth, 4, D_BRANCH)

    layers = [dict(norm_g=norm_g[l], b_f=b_f[l], rel_bias=rel_bias[l], w_s=w_s[l],
                   b_s=b_s[l], v_gain=v_gain[l], branch_gain=bg_full[l]) for l in range(depth)]
    layers[0].update(wp=pack_w_in(w_in_shards)[0])
    next_shards = [(w_out_b[l],) + ((w_in_b[l + 1],) if l + 1 < depth else ()) for l in range(depth)]

    loss_part, grad_x, lgrads, dfinal = local_step(x[0], loss_target[0], layers, final_g, next_shards)
    loss = lax.psum(loss_part, ("x", "y", "c"))

    stack = lambda k: jnp.stack([g[k] for g in lgrads])
    d_gain = jnp.moveaxis(stack("branch_gain").reshape(depth, 4, 4, HEAD_DIM), 2, 0).reshape(4, -1)
    d_gain = jnp.pad(d_gain, ((0, 0), (0, 8 * 128 - d_gain.shape[1]))).reshape(4, 8, 128)
    parts_in, parts_gain, parts_norm = exchange_grads(lgrads[0]["w_in_shards"], d_gain, _pack([stack("norm_g")], 16))
    parts = dict(w_in=[lgrads[0]["w_in_top_parts"], parts_in] + [g["w_in_parts"] for g in lgrads[1:]],
                 w_out=[g["w_out_parts"] for g in lgrads])

    outs = {}
    tags = ("grad", "delta", "new_m", "new_v")
    for k in ("w_in", "w_out"):
        rows = depth * weights[k].shape[1]
        flat = lambda a: a.reshape(rows, a.shape[-1])
        res = adamw_reduce(parts[k], flat(weights[k]), flat(mom1[k]), flat(mom2[k]), "adamw_" + k, 256)
        for tag, a in zip(tags, res):
            outs[tag, k] = a.reshape(weights[k].shape)
    gain8 = lambda a: jnp.pad(a.reshape(-1), (0, 8 * 128 - a.size)).reshape(8, 128)
    res = adamw_reduce([parts_gain], gain8(branch_gain), gain8(m_branch_gain), gain8(v_branch_gain), "adamw_gain", 8)
    for tag, a in zip(tags, res):
        outs[tag, "branch_gain"] = a.reshape(-1)[:branch_gain.size].reshape(branch_gain.shape)
    for names, parts_small, tile in ((SMALL_EARLY, lgrads[0]["small_parts"], PACK_ROW_TILE), (("norm_g",), parts_norm, 16)):
        pack_small = lambda d: _pack([d[k] for k in names], tile)
        res = adamw_reduce([parts_small], pack_small(weights), pack_small(mom1), pack_small(mom2),
                           "adamw_" + names[0], tile)
        for tag, slab in zip(tags, res):
            for k, a in zip(names, _unpack(slab, [weights[k].shape for k in names])):
                outs[tag, k] = a
    result = [loss, grad_x[None]]
    for tag in ("grad", "delta", "new_m", "new_v"):
        result += [outs[tag, k] for k in WEIGHTS]
    return tuple(result)
```
